```python
import math
import jax, jax.numpy as jnp
from jax import lax
import numpy as np

D_MODEL = 1024
BATCH = 8
SEQ = 2048
DEPTH = 1

HEAD_DIM = 64
N_Q_HEADS = 8
N_KV_HEADS = 2
GROUP = N_Q_HEADS // N_KV_HEADS
WINDOW = 128
BLOCK = 128
ATTN_WIDTH = N_Q_HEADS * HEAD_DIM
KV_WIDTH = N_KV_HEADS * HEAD_DIM
CONV_CHANNELS = 512
CONV_GROUPS = 8
CONV_WIDTH = 31
N_BRANCHES = 2
Q_OFF = 0
K_OFF = Q_OFF + ATTN_WIDTH
V_OFF = K_OFF + KV_WIDTH
GLU_OFF = V_OFF + KV_WIDTH
GATE_OFF = GLU_OFF + 2 * CONV_CHANNELS
IN_WIDTH = GATE_OFF + N_BRANCHES * D_MODEL
D_FF = int(math.ceil(8 * D_MODEL / 3 / 256)) * 256
EPS = 1e-5
NEG = -1e30

kernel_name = "hybrid_swa_sink_conformer_conv_gated"


def rmsnorm(x, g):
    xf = x.astype(jnp.float32)
    y = xf * lax.rsqrt(jnp.mean(xf * xf, axis=-1, keepdims=True) + EPS)
    return (y * g.astype(jnp.float32)).astype(x.dtype)


def layernorm(x, g, b):
    xf = x.astype(jnp.float32)
    mu = jnp.mean(xf, axis=-1, keepdims=True)
    xc = xf - mu
    var = jnp.mean(xc * xc, axis=-1, keepdims=True)
    y = xc * lax.rsqrt(var + EPS) * g.astype(jnp.float32) + b.astype(jnp.float32)
    return y.astype(x.dtype)


def sliding_window_attention(q, k, v, sinks):
    B, S = q.shape[0], q.shape[1]
    nb = S // BLOCK
    qb = q.reshape(B, nb, BLOCK, N_KV_HEADS, GROUP, HEAD_DIM)

    def band(t):
        padded = jnp.pad(t, ((0, 0), (BLOCK, 0), (0, 0), (0, 0)))
        prev = padded[:, :S].reshape(B, nb, BLOCK, N_KV_HEADS, HEAD_DIM)
        cur = t.reshape(B, nb, BLOCK, N_KV_HEADS, HEAD_DIM)
        return jnp.concatenate([prev, cur], axis=2)

    kb = band(k)
    vb = band(v)
    scale = HEAD_DIM ** -0.5
    s = jnp.einsum('bnqhgd,bnkhd->bnhgqk', qb, kb).astype(jnp.float32) * scale
    qi = jnp.arange(BLOCK)[:, None]
    kj = jnp.arange(2 * BLOCK)[None, :]
    diff = qi + BLOCK - kj
    kpos = jnp.arange(nb)[:, None, None] * BLOCK - BLOCK + kj[None]
    valid = (diff >= 0)[None] & (diff < WINDOW)[None] & (kpos >= 0)
    s = jnp.where(valid[None, :, None, None], s, NEG)
    sink_col = jnp.broadcast_to(
        sinks.astype(jnp.float32).reshape(1, 1, N_KV_HEADS, GROUP, 1, 1),
        s.shape[:-1] + (1,))
    p = jax.nn.softmax(jnp.concatenate([s, sink_col], axis=-1), axis=-1)[..., :-1]
    o = jnp.einsum('bnhgqk,bnkhd->bnqhgd', p.astype(v.dtype), vb)
    return o.reshape(B, S, ATTN_WIDTH)


def conformer_conv(u, conv_w, conv_b, ln_g, ln_b):
    a, b = jnp.split(u, 2, axis=-1)
    z = a * jax.nn.sigmoid(b)
    z = lax.conv_general_dilated(
        z, conv_w[:, None, :].astype(z.dtype),
        window_strides=(1,), padding=[(CONV_WIDTH - 1, 0)],
        dimension_numbers=('NWC', 'WIO', 'NWC'),
        feature_group_count=CONV_CHANNELS) + conv_b
    z = layernorm(z, ln_g, ln_b)
    return jax.nn.silu(z)


def _fwd_setup_inputs(seed: int = 0) -> dict:
    key = jax.random.key(seed)
    ks = jax.random.split(key, 20)
    L, D, C = DEPTH, D_MODEL, CONV_CHANNELS
    nrm = lambda k, shape, fan_in: jax.random.normal(k, shape, jnp.float32) * fan_in ** -0.5
    gain = lambda k, shape: 1.0 + 0.01 * jax.random.normal(k, shape, jnp.float32)
    small = lambda k, shape: 0.01 * jax.random.normal(k, shape, jnp.float32)
    return {
        "x": jax.random.normal(ks[0], (BATCH, SEQ, D), jnp.float32),
        "g_mix_norm": gain(ks[1], (L, D)),
        "w_in": nrm(ks[2], (L, D, IN_WIDTH), D),
        "b_in": small(ks[3], (L, IN_WIDTH)),
        "sinks": 0.5 * jax.random.normal(ks[4], (L, N_Q_HEADS), jnp.float32),
        "conv_w": nrm(ks[5], (L, CONV_WIDTH, C), CONV_WIDTH),
        "conv_b": small(ks[6], (L, C)),
        "ln_g": gain(ks[7], (L, C)),
        "ln_b": small(ks[8], (L, C)),
        "w_attn_proj": nrm(ks[9], (L, ATTN_WIDTH, D), ATTN_WIDTH),
        "w_conv_proj": nrm(ks[10], (L, C, D), C),
        "b_conv_proj": small(ks[11], (L, D)),
        "w_out": nrm(ks[12], (L, D, D), D),
        "g_ffn_norm": gain(ks[13], (L, D)),
        "w_ffn_in": nrm(ks[14], (L, D, 2 * D_FF), D),
        "w_ffn_down": nrm(ks[15], (L, D_FF, D), D_FF),
        "g_final": gain(ks[16], (D,)),
    }


def _fwd_reference(x, g_mix_norm, w_in, b_in, sinks, conv_w, conv_b, ln_g, ln_b,
              w_attn_proj, w_conv_proj, b_conv_proj, w_out, g_ffn_norm,
              w_ffn_in, w_ffn_down, g_final):
    B, S, D = x.shape
    for l in range(DEPTH):
        h = rmsnorm(x, g_mix_norm[l])
        proj = h @ w_in[l] + b_in[l]
        q = proj[..., Q_OFF:K_OFF].reshape(B, S, N_Q_HEADS, HEAD_DIM)
        k = proj[..., K_OFF:V_OFF].reshape(B, S, N_KV_HEADS, HEAD_DIM)
        v = proj[..., V_OFF:GLU_OFF].reshape(B, S, N_KV_HEADS, HEAD_DIM)
        glu_in = proj[..., GLU_OFF:GATE_OFF]
        gates = jax.nn.sigmoid(proj[..., GATE_OFF:].reshape(B, S, N_BRANCHES, D))

        y_attn = sliding_window_attention(q, k, v, sinks[l]) @ w_attn_proj[l]
        y_conv = conformer_conv(glu_in, conv_w[l], conv_b[l], ln_g[l], ln_b[l]) @ w_conv_proj[l] + b_conv_proj[l]
        merged = gates[:, :, 0] * y_attn + gates[:, :, 1] * y_conv
        x = x + merged @ w_out[l]

        h2 = rmsnorm(x, g_ffn_norm[l])
        gu = h2 @ w_ffn_in[l]
        gate, up = gu[..., :D_FF], gu[..., D_FF:]
        x = x + (jax.nn.silu(gate) * up) @ w_ffn_down[l]
    return rmsnorm(x, g_final)


import jax as _jax
import jax.numpy as _jnp

TWIN_FORMAT = 'train_step'
FWD_PARAMS = ['x', 'g_mix_norm', 'w_in', 'b_in', 'sinks', 'conv_w', 'conv_b', 'ln_g', 'ln_b', 'w_attn_proj', 'w_conv_proj', 'b_conv_proj', 'w_out', 'g_ffn_norm', 'w_ffn_in', 'w_ffn_down', 'g_final']
TWIN_WEIGHTS = ['g_mix_norm', 'w_in', 'b_in', 'sinks', 'conv_w', 'conv_b', 'ln_g', 'ln_b', 'w_attn_proj', 'w_conv_proj', 'b_conv_proj', 'w_out', 'g_ffn_norm', 'w_ffn_in', 'w_ffn_down', 'g_final']
TWIN_DIFF_INPUT = 'x'
TWIN_INPUTS = ['x', 'g_mix_norm', 'w_in', 'b_in', 'sinks', 'conv_w', 'conv_b', 'ln_g', 'ln_b', 'w_attn_proj', 'w_conv_proj', 'b_conv_proj', 'w_out', 'g_ffn_norm', 'w_ffn_in', 'w_ffn_down', 'g_final', 'loss_target', 'm_g_mix_norm', 'm_w_in', 'm_b_in', 'm_sinks', 'm_conv_w', 'm_conv_b', 'm_ln_g', 'm_ln_b', 'm_w_attn_proj', 'm_w_conv_proj', 'm_b_conv_proj', 'm_w_out', 'm_g_ffn_norm', 'm_w_ffn_in', 'm_w_ffn_down', 'm_g_final', 'v_g_mix_norm', 'v_w_in', 'v_b_in', 'v_sinks', 'v_conv_w', 'v_conv_b', 'v_ln_g', 'v_ln_b', 'v_w_attn_proj', 'v_w_conv_proj', 'v_b_conv_proj', 'v_w_out', 'v_g_ffn_norm', 'v_w_ffn_in', 'v_w_ffn_down', 'v_g_final']
TWIN_OUTPUTS = ['loss', 'grad_x', 'grad_g_mix_norm', 'grad_w_in', 'grad_b_in', 'grad_sinks', 'grad_conv_w', 'grad_conv_b', 'grad_ln_g', 'grad_ln_b', 'grad_w_attn_proj', 'grad_w_conv_proj', 'grad_b_conv_proj', 'grad_w_out', 'grad_g_ffn_norm', 'grad_w_ffn_in', 'grad_w_ffn_down', 'grad_g_final', 'delta_g_mix_norm', 'delta_w_in', 'delta_b_in', 'delta_sinks', 'delta_conv_w', 'delta_conv_b', 'delta_ln_g', 'delta_ln_b', 'delta_w_attn_proj', 'delta_w_conv_proj', 'delta_b_conv_proj', 'delta_w_out', 'delta_g_ffn_norm', 'delta_w_ffn_in', 'delta_w_ffn_down', 'delta_g_final', 'new_m_g_mix_norm', 'new_m_w_in', 'new_m_b_in', 'new_m_sinks', 'new_m_conv_w', 'new_m_conv_b', 'new_m_ln_g', 'new_m_ln_b', 'new_m_w_attn_proj', 'new_m_w_conv_proj', 'new_m_b_conv_proj', 'new_m_w_out', 'new_m_g_ffn_norm', 'new_m_w_ffn_in', 'new_m_w_ffn_down', 'new_m_g_final', 'new_v_g_mix_norm', 'new_v_w_in', 'new_v_b_in', 'new_v_sinks', 'new_v_conv_w', 'new_v_conv_b', 'new_v_ln_g', 'new_v_ln_b', 'new_v_w_attn_proj', 'new_v_w_conv_proj', 'new_v_b_conv_proj', 'new_v_w_out', 'new_v_g_ffn_norm', 'new_v_w_ffn_in', 'new_v_w_ffn_down', 'new_v_g_final']
TWIN_LEAF_KINDS = {'loss': 'loss', 'grad_x': 'grad_x', 'grad_g_mix_norm': 'grad_w', 'grad_w_in': 'grad_w', 'grad_b_in': 'grad_w', 'grad_sinks': 'grad_w', 'grad_conv_w': 'grad_w', 'grad_conv_b': 'grad_w', 'grad_ln_g': 'grad_w', 'grad_ln_b': 'grad_w', 'grad_w_attn_proj': 'grad_w', 'grad_w_conv_proj': 'grad_w', 'grad_b_conv_proj': 'grad_w', 'grad_w_out': 'grad_w', 'grad_g_ffn_norm': 'grad_w', 'grad_w_ffn_in': 'grad_w', 'grad_w_ffn_down': 'grad_w', 'grad_g_final': 'grad_w', 'delta_g_mix_norm': 'delta_w', 'delta_w_in': 'delta_w', 'delta_b_in': 'delta_w', 'delta_sinks': 'delta_w', 'delta_conv_w': 'delta_w', 'delta_conv_b': 'delta_w', 'delta_ln_g': 'delta_w', 'delta_ln_b': 'delta_w', 'delta_w_attn_proj': 'delta_w', 'delta_w_conv_proj': 'delta_w', 'delta_b_conv_proj': 'delta_w', 'delta_w_out': 'delta_w', 'delta_g_ffn_norm': 'delta_w', 'delta_w_ffn_in': 'delta_w', 'delta_w_ffn_down': 'delta_w', 'delta_g_final': 'delta_w', 'new_m_g_mix_norm': 'new_m', 'new_m_w_in': 'new_m', 'new_m_b_in': 'new_m', 'new_m_sinks': 'new_m', 'new_m_conv_w': 'new_m', 'new_m_conv_b': 'new_m', 'new_m_ln_g': 'new_m', 'new_m_ln_b': 'new_m', 'new_m_w_attn_proj': 'new_m', 'new_m_w_conv_proj': 'new_m', 'new_m_b_conv_proj': 'new_m', 'new_m_w_out': 'new_m', 'new_m_g_ffn_norm': 'new_m', 'new_m_w_ffn_in': 'new_m', 'new_m_w_ffn_down': 'new_m', 'new_m_g_final': 'new_m', 'new_v_g_mix_norm': 'new_v', 'new_v_w_in': 'new_v', 'new_v_b_in': 'new_v', 'new_v_sinks': 'new_v', 'new_v_conv_w': 'new_v', 'new_v_conv_b': 'new_v', 'new_v_ln_g': 'new_v', 'new_v_ln_b': 'new_v', 'new_v_w_attn_proj': 'new_v', 'new_v_w_conv_proj': 'new_v', 'new_v_b_conv_proj': 'new_v', 'new_v_w_out': 'new_v', 'new_v_g_ffn_norm': 'new_v', 'new_v_w_ffn_in': 'new_v', 'new_v_w_ffn_down': 'new_v', 'new_v_g_final': 'new_v'}


def _forward(args):
    return _fwd_reference(*[args[k] for k in FWD_PARAMS])


def _output_shape():
    out = _jax.eval_shape(lambda: _forward(_fwd_setup_inputs(0)))
    return out.shape, out.dtype

N_MICROBATCH = 1
ADAM_LR = 0.001
ADAM_B1 = 0.9
ADAM_B2 = 0.999
ADAM_EPS = 1e-08
ADAM_WD = 0.01
ADAM_STEP = 10
PER_EXAMPLE_BATCH_AXIS = {'x': 0, 'loss_target': 0}
SHARED_INPUTS = []
_WEIGHT_DTYPES = {'g_mix_norm': _jnp.float32, 'w_in': _jnp.float32, 'b_in': _jnp.float32, 'sinks': _jnp.float32, 'conv_w': _jnp.float32, 'conv_b': _jnp.float32, 'ln_g': _jnp.float32, 'ln_b': _jnp.float32, 'w_attn_proj': _jnp.float32, 'w_conv_proj': _jnp.float32, 'b_conv_proj': _jnp.float32, 'w_out': _jnp.float32, 'g_ffn_norm': _jnp.float32, 'w_ffn_in': _jnp.float32, 'w_ffn_down': _jnp.float32, 'g_final': _jnp.float32}
MOMENT_SCALE = {'g_mix_norm': 5.606357e-02, 'w_in': 2.909792e-02, 'b_in': 4.706168e-02, 'sinks': 2.131187e-02, 'conv_w': 6.321364e-02, 'conv_b': 1.326396e-01, 'ln_g': 7.522839e-02, 'ln_b': 6.322226e-02, 'w_attn_proj': 1.456594e-02, 'w_conv_proj': 4.313489e-02, 'b_conv_proj': 7.075303e-02, 'w_out': 4.559078e-02, 'g_ffn_norm': 8.957119e-02, 'w_ffn_in': 3.775048e-02, 'w_ffn_down': 6.155858e-02, 'g_final': 1.598420e+01}


def _to_microbatches(a, axis):
    t = _jnp.moveaxis(a, axis, 0)
    t = t.reshape((N_MICROBATCH, t.shape[0] // N_MICROBATCH) + t.shape[1:])
    return _jnp.moveaxis(t, 1, axis + 1)


def setup_inputs(seed: int = 0) -> dict:
    inp = _fwd_setup_inputs(seed)
    key = _jax.random.fold_in(_jax.random.key(seed), 7919)
    shape, _ = _output_shape()
    out = dict(inp)
    out["loss_target"] = _jax.random.normal(_jax.random.fold_in(key, 0), shape, _jnp.float32)
    for i, name in enumerate(TWIN_WEIGHTS):
        w = inp[name].astype(_jnp.float32)
        if MOMENT_SCALE is None:
            s = _jnp.sqrt(_jnp.mean(_jnp.square(w)) + 1e-30)
        else:
            s = MOMENT_SCALE[name]
        km, kv = _jax.random.split(_jax.random.fold_in(key, i + 1))
        out[name] = w
        out["m_" + name] = s * _jax.random.normal(km, w.shape, _jnp.float32)
        out["v_" + name] = (s * s) * _jax.random.uniform(kv, w.shape, _jnp.float32, 0.5, 1.5)
    if N_MICROBATCH > 1:
        for name, axis in PER_EXAMPLE_BATCH_AXIS.items():
            out[name] = _to_microbatches(out[name], axis)
    return {'x': out['x'], 'g_mix_norm': out['g_mix_norm'], 'w_in': out['w_in'], 'b_in': out['b_in'], 'sinks': out['sinks'], 'conv_w': out['conv_w'], 'conv_b': out['conv_b'], 'ln_g': out['ln_g'], 'ln_b': out['ln_b'], 'w_attn_proj': out['w_attn_proj'], 'w_conv_proj': out['w_conv_proj'], 'b_conv_proj': out['b_conv_proj'], 'w_out': out['w_out'], 'g_ffn_norm': out['g_ffn_norm'], 'w_ffn_in': out['w_ffn_in'], 'w_ffn_down': out['w_ffn_down'], 'g_final': out['g_final'], 'loss_target': out['loss_target'], 'm_g_mix_norm': out['m_g_mix_norm'], 'm_w_in': out['m_w_in'], 'm_b_in': out['m_b_in'], 'm_sinks': out['m_sinks'], 'm_conv_w': out['m_conv_w'], 'm_conv_b': out['m_conv_b'], 'm_ln_g': out['m_ln_g'], 'm_ln_b': out['m_ln_b'], 'm_w_attn_proj': out['m_w_attn_proj'], 'm_w_conv_proj': out['m_w_conv_proj'], 'm_b_conv_proj': out['m_b_conv_proj'], 'm_w_out': out['m_w_out'], 'm_g_ffn_norm': out['m_g_ffn_norm'], 'm_w_ffn_in': out['m_w_ffn_in'], 'm_w_ffn_down': out['m_w_ffn_down'], 'm_g_final': out['m_g_final'], 'v_g_mix_norm': out['v_g_mix_norm'], 'v_w_in': out['v_w_in'], 'v_b_in': out['v_b_in'], 'v_sinks': out['v_sinks'], 'v_conv_w': out['v_conv_w'], 'v_conv_b': out['v_conv_b'], 'v_ln_g': out['v_ln_g'], 'v_ln_b': out['v_ln_b'], 'v_w_attn_proj': out['v_w_attn_proj'], 'v_w_conv_proj': out['v_w_conv_proj'], 'v_b_conv_proj': out['v_b_conv_proj'], 'v_w_out': out['v_w_out'], 'v_g_ffn_norm': out['v_g_ffn_norm'], 'v_w_ffn_in': out['v_w_ffn_in'], 'v_w_ffn_down': out['v_w_ffn_down'], 'v_g_final': out['v_g_final']}


def _loss(weights, diff, rest, loss_target):
    with _jax.named_scope("forward"):
        args = {**rest, TWIN_DIFF_INPUT: diff, **{k: w.astype(_WEIGHT_DTYPES[k]) for k, w in weights.items()}}
        y = _forward(args)
    with _jax.named_scope("loss_head"):
        err = _jnp.square(y.astype(_jnp.float32) - loss_target)
        return 0.5 * _jnp.sum(_jnp.mean(err, axis=-1)) if err.ndim else 0.5 * err


def _adamw(w, g, m, v):
    m = ADAM_B1 * m + (1.0 - ADAM_B1) * g
    v = ADAM_B2 * v + (1.0 - ADAM_B2) * _jnp.square(g)
    m_hat = m / (1.0 - ADAM_B1 ** ADAM_STEP)
    v_hat = v / (1.0 - ADAM_B2 ** ADAM_STEP)
    delta = -ADAM_LR * (m_hat / (_jnp.sqrt(v_hat) + ADAM_EPS) + ADAM_WD * w)
    return delta, m, v


def reference(x, g_mix_norm, w_in, b_in, sinks, conv_w, conv_b, ln_g, ln_b, w_attn_proj, w_conv_proj, b_conv_proj, w_out, g_ffn_norm, w_ffn_in, w_ffn_down, g_final, loss_target, m_g_mix_norm, m_w_in, m_b_in, m_sinks, m_conv_w, m_conv_b, m_ln_g, m_ln_b, m_w_attn_proj, m_w_conv_proj, m_b_conv_proj, m_w_out, m_g_ffn_norm, m_w_ffn_in, m_w_ffn_down, m_g_final, v_g_mix_norm, v_w_in, v_b_in, v_sinks, v_conv_w, v_conv_b, v_ln_g, v_ln_b, v_w_attn_proj, v_w_conv_proj, v_b_conv_proj, v_w_out, v_g_ffn_norm, v_w_ffn_in, v_w_ffn_down, v_g_final):
    given = dict(x=x, g_mix_norm=g_mix_norm, w_in=w_in, b_in=b_in, sinks=sinks, conv_w=conv_w, conv_b=conv_b, ln_g=ln_g, ln_b=ln_b, w_attn_proj=w_attn_proj, w_conv_proj=w_conv_proj, b_conv_proj=b_conv_proj, w_out=w_out, g_ffn_norm=g_ffn_norm, w_ffn_in=w_ffn_in, w_ffn_down=w_ffn_down, g_final=g_final, loss_target=loss_target, m_g_mix_norm=m_g_mix_norm, m_w_in=m_w_in, m_b_in=m_b_in, m_sinks=m_sinks, m_conv_w=m_conv_w, m_conv_b=m_conv_b, m_ln_g=m_ln_g, m_ln_b=m_ln_b, m_w_attn_proj=m_w_attn_proj, m_w_conv_proj=m_w_conv_proj, m_b_conv_proj=m_b_conv_proj, m_w_out=m_w_out, m_g_ffn_norm=m_g_ffn_norm, m_w_ffn_in=m_w_ffn_in, m_w_ffn_down=m_w_ffn_down, m_g_final=m_g_final, v_g_mix_norm=v_g_mix_norm, v_w_in=v_w_in, v_b_in=v_b_in, v_sinks=v_sinks, v_conv_w=v_conv_w, v_conv_b=v_conv_b, v_ln_g=v_ln_g, v_ln_b=v_ln_b, v_w_attn_proj=v_w_attn_proj, v_w_conv_proj=v_w_conv_proj, v_b_conv_proj=v_b_conv_proj, v_w_out=v_w_out, v_g_ffn_norm=v_g_ffn_norm, v_w_ffn_in=v_w_ffn_in, v_w_ffn_down=v_w_ffn_down, v_g_final=v_g_final)
    weights = {n: given[n] for n in TWIN_WEIGHTS}
    shared = {n: given[n] for n in SHARED_INPUTS}
    per_example = {n: given[n] for n in ['x']}
    grad_fn = _jax.value_and_grad(_loss, argnums=(0, 1))

    def one_microbatch(ex, loss_target):
        ex = dict(ex)
        diff = ex.pop(TWIN_DIFF_INPUT)
        return grad_fn(weights, diff, {**shared, **ex}, loss_target)

    if N_MICROBATCH == 1:
        loss, (grad_w, grad_x) = one_microbatch(per_example, given["loss_target"])
    else:
        def body(carry, xs):
            loss_sum, grad_sum = carry
            l_k, (gw_k, gx_k) = one_microbatch(xs[0], xs[1])
            with _jax.named_scope("update"):
                return (loss_sum + l_k, _jax.tree.map(_jnp.add, grad_sum, gw_k)), gx_k

        init = (_jnp.zeros((), _jnp.float32), _jax.tree.map(_jnp.zeros_like, weights))
        (loss, grad_w), grad_x = _jax.lax.scan(body, init, (per_example, given["loss_target"]))
    with _jax.named_scope("update"):
        delta_w, new_m, new_v = {}, {}, {}
        for n in TWIN_WEIGHTS:
            delta_w[n], new_m[n], new_v[n] = _adamw(weights[n], grad_w[n], given["m_" + n], given["v_" + n])
    return (loss, grad_x, *[grad_w[n] for n in TWIN_WEIGHTS], *[delta_w[n] for n in TWIN_WEIGHTS],
            *[new_m[n] for n in TWIN_WEIGHTS], *[new_v[n] for n in TWIN_WEIGHTS])
```

```python
import functools

import jax
import jax.numpy as jnp
from jax import lax
from jax.experimental import pallas as pl
from jax.experimental.pallas import tpu as pltpu

F32 = jnp.float32
BF = jnp.bfloat16

SEQ = 2048
D_MODEL = 1024
HEAD_DIM = 64
N_Q_HEADS = 8
N_KV_HEADS = 2
GROUP = N_Q_HEADS // N_KV_HEADS
BLOCK = 128
ATTN_WIDTH = 512
KV_WIDTH = 128
CONV_CHANNELS = 512
CONV_WIDTH = 31
CONV_PAD = 32
GLU_OFF = 768
GATE_OFF = 1792
IN_WIDTH = 3840
D_FF = 2816
EPS = 1e-5
NEG = -1e30
N_DEV = 8

ADAM_LR = 0.001
ADAM_B1 = 0.9
ADAM_B2 = 0.999
ADAM_EPS = 1e-08
ADAM_WD = 0.01
ADAM_STEP = 10

VMEM_LIMIT_BYTES = 56 * 1024 * 1024
MESH = pl.DeviceIdType.MESH

_DIMS = {"NN": (((1,), (0,)), ((), ())), "NT": (((1,), (1,)), ((), ())), "TN": (((0,), (0,)), ((), ()))}


def _params(sem):
    return pltpu.CompilerParams(dimension_semantics=sem, vmem_limit_bytes=VMEM_LIMIT_BYTES)


def _matmul(name, a_list, b, mode, *, m, n, tm, tn, tk, epilogue, extra=(), outs, b_off=(0, 0), alias=None):
    seg_nk = [a.shape[0] // tk for a in a_list] if mode == "TN" else [a.shape[1] // tk for a in a_list]
    nk = sum(seg_nk)
    M, N = m, n
    starts = [sum(seg_nk[:s]) for s in range(len(seg_nk))]
    n_a, n_extra, n_out = len(a_list), len(extra), len(outs)

    a_specs = []
    for st, ns in zip(starts, seg_nk):
        if mode == "TN":
            a_specs.append(pl.BlockSpec((tk, tm), lambda j, i, k: (k, i)))
        else:
            a_specs.append(pl.BlockSpec((tm, tk), functools.partial(
                lambda j, i, k, st, ns: (i, jnp.clip(k - st, 0, ns - 1)), st=st, ns=ns)))
    if mode == "NT":
        b_spec = pl.BlockSpec((tn, tk), lambda j, i, k: (b_off[0] + j, b_off[1] + k))
    else:
        b_spec = pl.BlockSpec((tk, tn), lambda j, i, k: (b_off[0] + k, b_off[1] + j))
    n_alias = 0 if alias is None else 1

    def body(*refs):
        a_refs = refs[:n_a]
        b_ref = refs[n_a]
        ex = refs[n_a + 1 + n_alias:n_a + 1 + n_alias + n_extra]
        out_refs = refs[n_a + 1 + n_alias + n_extra:n_a + 1 + n_alias + n_extra + n_out]
        k = pl.program_id(2)
        ids = (pl.program_id(0), pl.program_id(1))

        def dot(a_ref):
            return lax.dot_general(a_ref[...].astype(BF), b_ref[...].astype(BF), _DIMS[mode],
                                   preferred_element_type=F32)

        if nk == 1:
            epilogue(dot(a_refs[0]), ex, out_refs, ids)
            return
        acc = refs[-1]

        @pl.when(k == 0)
        def _():
            acc[...] = jnp.zeros_like(acc)

        for a_ref, st, ns in zip(a_refs, starts, seg_nk):
            if n_a == 1:
                acc[...] += dot(a_ref)
            else:
                @pl.when((k >= st) & (k < st + ns))
                def _(a_ref=a_ref):
                    acc[...] += dot(a_ref)

        @pl.when(k == nk - 1)
        def _():
            epilogue(acc[...], ex, out_refs, ids)

    in_specs = [*a_specs, b_spec]
    args = [*a_list, b]
    io_alias = {}
    if alias is not None:
        in_specs.append(pl.BlockSpec(memory_space=pl.ANY))
        args.append(alias[0])
        io_alias = {n_a + 1: alias[1]}
    in_specs += [s for _, s in extra]
    args += [x for x, _ in extra]
    return pl.pallas_call(
        body, name=name, grid=(N // tn, M // tm, nk), in_specs=in_specs,
        out_specs=[s for _, s in outs], out_shape=[o for o, _ in outs],
        scratch_shapes=[] if nk == 1 else [pltpu.VMEM((tm, tn), F32)],
        input_output_aliases=io_alias,
        compiler_params=_params(("arbitrary", "arbitrary", "arbitrary")),
    )(*args)


def _tile(tm, tn):
    return pl.BlockSpec((tm, tn), lambda j, i, k: (i, j))


def _row(tn):
    return pl.BlockSpec((1, tn), lambda j, i, k: (0, j))


def _store(dtype):
    def ep(acc, ex, outs, ids):
        outs[0][...] = acc.astype(dtype)
    return ep


def _sds(shape, dtype):
    return jax.ShapeDtypeStruct(shape, dtype)


def _rms_fwd(name, x, g):
    T, D = x.shape
    tm = 256

    def body(x_ref, g_ref, h_ref, r_ref):
        xv = x_ref[...]
        r = lax.rsqrt(jnp.mean(xv * xv, axis=-1, keepdims=True) + EPS)
        h_ref[...] = (xv * r * g_ref[...]).astype(BF)
        r_ref[...] = r

    return pl.pallas_call(
        body, name=name, grid=(T // tm,),
        in_specs=[pl.BlockSpec((tm, D), lambda i: (i, 0)), pl.BlockSpec((1, D), lambda i: (0, 0))],
        out_specs=[pl.BlockSpec((tm, D), lambda i: (i, 0)), pl.BlockSpec((tm, 1), lambda i: (i, 0))],
        out_shape=[_sds((T, D), BF), _sds((T, 1), F32)],
        compiler_params=_params(("arbitrary",)),
    )(x, g)


def _rms_bwd(dh, xv, r, g):
    xh = xv * r
    dxh = dh * g
    dx = r * (dxh - xh * jnp.mean(dxh * xh, axis=-1, keepdims=True))
    return dx, jnp.sum(dh * xh, axis=0, keepdims=True)


def _accumulate_rows(ref, val, first):
    @pl.when(first)
    def _():
        ref[...] = val

    @pl.when(jnp.logical_not(first))
    def _():
        ref[...] += val


def _final(x3, g_final, target):
    T, D = x3.shape
    tm = 256

    def body(x_ref, g_ref, t_ref, dx_ref, dg_ref, loss_ref):
        i = pl.program_id(0)
        xv = x_ref[...]
        g = g_ref[...]
        r = lax.rsqrt(jnp.mean(xv * xv, axis=-1, keepdims=True) + EPS)
        err = xv * r * g - t_ref[...]
        dy = err * (1.0 / D)
        dx, dg = _rms_bwd(dy, xv, r, g)
        dx_ref[...] = dx
        part = 0.5 * jnp.sum(jnp.mean(err * err, axis=-1, keepdims=True), axis=0, keepdims=True)
        _accumulate_rows(dg_ref, dg, i == 0)
        _accumulate_rows(loss_ref, part, i == 0)

    return pl.pallas_call(
        body, name="final_loss", grid=(T // tm,),
        in_specs=[pl.BlockSpec((tm, D), lambda i: (i, 0)), pl.BlockSpec((1, D), lambda i: (0, 0)),
                  pl.BlockSpec((tm, D), lambda i: (i, 0))],
        out_specs=[pl.BlockSpec((tm, D), lambda i: (i, 0)), pl.BlockSpec((1, D), lambda i: (0, 0)),
                   pl.BlockSpec((1, 1), lambda i: (0, 0))],
        out_shape=[_sds((T, D), F32), _sds((1, D), F32), _sds((1, 1), F32)],
        compiler_params=_params(("arbitrary",)),
    )(x3, g_final, target)


def _colsum(name, a):
    T, N = a.shape
    tm = 512

    def body(a_ref, o_ref):
        _accumulate_rows(o_ref, jnp.sum(a_ref[...].astype(F32), axis=0, keepdims=True), pl.program_id(0) == 0)

    return pl.pallas_call(
        body, name=name, grid=(T // tm,),
        in_specs=[pl.BlockSpec((tm, N), lambda i: (i, 0))],
        out_specs=pl.BlockSpec((1, N), lambda i: (0, 0)),
        out_shape=_sds((1, N), F32),
        compiler_params=_params(("arbitrary",)),
    )(a)


def _lane_half(shape, h):
    lane = lax.broadcasted_iota(jnp.int32, shape, 1)
    return (lane >= HEAD_DIM * h) & (lane < HEAD_DIM * (h + 1))


def _to_half(v, w, h):
    if w != h:
        v = pltpu.roll(v, HEAD_DIM, 1)
    return jnp.where(_lane_half(v.shape, h), v, 0.0)


def _attn_block(qkv_ref, sinks_ref, n, h):
    r0 = pl.multiple_of(n * BLOCK, BLOCK)
    p0 = pl.multiple_of(jnp.maximum(n - 1, 0) * BLOCK, BLOCK)
    rows = pl.ds(r0, BLOCK)
    prev = pl.ds(p0, BLOCK)
    k2 = jnp.concatenate([qkv_ref[prev, ATTN_WIDTH:ATTN_WIDTH + KV_WIDTH],
                          qkv_ref[rows, ATTN_WIDTH:ATTN_WIDTH + KV_WIDTH]], axis=0).astype(BF)
    v2 = jnp.concatenate([qkv_ref[prev, ATTN_WIDTH + KV_WIDTH:ATTN_WIDTH + 2 * KV_WIDTH],
                          qkv_ref[rows, ATTN_WIDTH + KV_WIDTH:ATTN_WIDTH + 2 * KV_WIDTH]], axis=0).astype(BF)
    qs = []
    for g in range(GROUP):
        hq = GROUP * h + g
        blk = qkv_ref[rows, (hq // 2) * 128:(hq // 2 + 1) * 128]
        qs.append(_to_half(blk, hq % 2, h))
    q4 = jnp.concatenate(qs, axis=0).astype(BF)
    s = lax.dot_general(q4, k2, _DIMS["NT"], preferred_element_type=F32) * (HEAD_DIM ** -0.5)
    shape = s.shape
    row = lax.broadcasted_iota(jnp.int32, shape, 0)
    qi = row & (BLOCK - 1)
    kj = lax.broadcasted_iota(jnp.int32, shape, 1)
    diff = qi + BLOCK - kj
    valid = (diff >= 0) & (diff < BLOCK) & ((kj >= BLOCK) | (n > 0))
    s = jnp.where(valid, s, NEG)
    row1 = lax.broadcasted_iota(jnp.int32, (shape[0], 1), 0)
    sink = jnp.zeros((shape[0], 1), F32)
    for g in range(GROUP):
        sink = jnp.where((row1 >= g * BLOCK) & (row1 < (g + 1) * BLOCK), sinks_ref[0, GROUP * h + g], sink)
    m = jnp.maximum(jnp.max(s, axis=-1, keepdims=True), sink)
    e = jnp.exp(s - m)
    es = jnp.exp(sink - m)
    inv = 1.0 / (jnp.sum(e, axis=-1, keepdims=True) + es)
    return e * inv, es * inv, q4, k2, v2, rows, prev


def _attn_fwd(proj, sinks):
    T = proj.shape[0]

    def body(qkv_ref, sinks_ref, o_ref):
        def blk(n, carry):
            outs = [None] * (N_Q_HEADS // 2)
            for h in range(N_KV_HEADS):
                p, _, _, _, v2, rows, _ = _attn_block(qkv_ref, sinks_ref, n, h)
                o = lax.dot_general(p.astype(BF), v2, _DIMS["NN"], preferred_element_type=F32)
                for g in range(GROUP):
                    hq = GROUP * h + g
                    piece = jnp.where(_lane_half((BLOCK, 128), h), o[g * BLOCK:(g + 1) * BLOCK], 0.0)
                    if hq % 2 != h:
                        piece = pltpu.roll(piece, HEAD_DIM, 1)
                    outs[hq // 2] = piece if outs[hq // 2] is None else outs[hq // 2] + piece
            for pb in range(N_Q_HEADS // 2):
                o_ref[rows, pb * 128:(pb + 1) * 128] = outs[pb].astype(BF)
            return carry

        lax.fori_loop(0, T // BLOCK, blk, 0)

    return pl.pallas_call(
        body, name="attn_fwd", grid=(1,),
        in_specs=[pl.BlockSpec((T, GLU_OFF), lambda i: (0, 0)), pl.BlockSpec(memory_space=pltpu.SMEM)],
        out_specs=pl.BlockSpec((T, ATTN_WIDTH), lambda i: (0, 0)),
        out_shape=_sds((T, ATTN_WIDTH), BF),
        compiler_params=_params(("arbitrary",)),
    )(proj, sinks)


def _attn_bwd(proj, d_o, sinks):
    T = proj.shape[0]

    def body(qkv_ref, do_ref, sinks_ref, dqkv_ref, dsink_ref, dk_acc, dv_acc):
        dsink_ref[...] = jnp.zeros_like(dsink_ref)
        dk_acc[...] = jnp.zeros_like(dk_acc)
        dv_acc[...] = jnp.zeros_like(dv_acc)

        def blk(n, carry):
            dqs = [None] * (N_Q_HEADS // 2)
            for h in range(N_KV_HEADS):
                p, psink, q4, k2, v2, rows, prev = _attn_block(qkv_ref, sinks_ref, n, h)
                dos = []
                for g in range(GROUP):
                    hq = GROUP * h + g
                    dos.append(_to_half(do_ref[rows, (hq // 2) * 128:(hq // 2 + 1) * 128].astype(F32), hq % 2, h))
                do4 = jnp.concatenate(dos, axis=0).astype(BF)
                dp = lax.dot_general(do4, v2, _DIMS["NT"], preferred_element_type=F32)
                delta = jnp.sum(p * dp, axis=-1, keepdims=True)
                ds = (p * (dp - delta) * (HEAD_DIM ** -0.5)).astype(BF)
                dsk = psink * delta
                for g in range(GROUP):
                    hq = GROUP * h + g
                    tot = -jnp.sum(dsk[g * BLOCK:(g + 1) * BLOCK], axis=0, keepdims=True)
                    lane = lax.broadcasted_iota(jnp.int32, (1, 128), 1)
                    dsink_ref[...] += jnp.where(lane == hq, tot, 0.0)
                dq = lax.dot_general(ds, k2, _DIMS["NN"], preferred_element_type=F32)
                dk = lax.dot_general(ds, q4, _DIMS["TN"], preferred_element_type=F32)
                dv = lax.dot_general(p.astype(BF), do4, _DIMS["TN"], preferred_element_type=F32)
                dk_acc[prev, :] += dk[:BLOCK]
                dk_acc[rows, :] += dk[BLOCK:]
                dv_acc[prev, :] += dv[:BLOCK]
                dv_acc[rows, :] += dv[BLOCK:]
                for g in range(GROUP):
                    hq = GROUP * h + g
                    piece = jnp.where(_lane_half((BLOCK, 128), h), dq[g * BLOCK:(g + 1) * BLOCK], 0.0)
                    if hq % 2 != h:
                        piece = pltpu.roll(piece, HEAD_DIM, 1)
                    dqs[hq // 2] = piece if dqs[hq // 2] is None else dqs[hq // 2] + piece
            for pb in range(N_Q_HEADS // 2):
                dqkv_ref[rows, pb * 128:(pb + 1) * 128] = dqs[pb].astype(BF)
            return carry

        lax.fori_loop(0, T // BLOCK, blk, 0)
        dqkv_ref[:, ATTN_WIDTH:ATTN_WIDTH + KV_WIDTH] = dk_acc[...].astype(BF)
        dqkv_ref[:, ATTN_WIDTH + KV_WIDTH:] = dv_acc[...].astype(BF)

    return pl.pallas_call(
        body, name="attn_bwd", grid=(1,),
        in_specs=[pl.BlockSpec((T, GLU_OFF), lambda i: (0, 0)), pl.BlockSpec((T, ATTN_WIDTH), lambda i: (0, 0)),
                  pl.BlockSpec(memory_space=pltpu.SMEM)],
        out_specs=[pl.BlockSpec((T, GLU_OFF), lambda i: (0, 0)), pl.BlockSpec((1, 128), lambda i: (0, 0))],
        out_shape=[_sds((T, GLU_OFF), BF), _sds((1, 128), F32)],
        scratch_shapes=[pltpu.VMEM((T, KV_WIDTH), F32), pltpu.VMEM((T, KV_WIDTH), F32)],
        compiler_params=_params(("arbitrary",)),
    )(proj, d_o, sinks)


CHUNK = 256
_GLU_SPECS = [pl.BlockSpec((SEQ, 256), functools.partial(lambda i, c: (0, c), c=GLU_OFF // 256 + c)) for c in range(4)]


def _glu_to_pad(a0, a1, b0, b1, zpad):
    C = CONV_CHANNELS
    zpad[0:CONV_PAD, :] = jnp.zeros((CONV_PAD, C), F32)
    zpad[CONV_PAD:, 0:256] = a0[...] * jax.nn.sigmoid(b0[...])
    zpad[CONV_PAD:, 256:C] = a1[...] * jax.nn.sigmoid(b1[...])


def _conv_chunk(zpad, w_ref, cb_ref, ci):
    base = ci * CHUNK + CONV_PAD - (CONV_WIDTH - 1)
    u = jnp.broadcast_to(cb_ref[...], (CHUNK, CONV_CHANNELS))
    for k in range(CONV_WIDTH):
        u = u + w_ref[k:k + 1, :] * zpad[base + k:base + k + CHUNK, :]
    return u


def _ln_parts(u):
    mu = jnp.mean(u, axis=-1, keepdims=True)
    xc = u - mu
    rstd = lax.rsqrt(jnp.mean(xc * xc, axis=-1, keepdims=True) + EPS)
    return xc * rstd, rstd


def _conv_fwd(proj, conv_w, conv_b, ln_g, ln_b):
    T, C = proj.shape[0], CONV_CHANNELS
    vec = pl.BlockSpec((1, C), lambda i: (0, 0))

    def body(a0, a1, b0, b1, w_ref, cb_ref, g_ref, be_ref, c_ref, zpad):
        _glu_to_pad(a0, a1, b0, b1, zpad)
        for ci in range(T // CHUNK):
            xh, _ = _ln_parts(_conv_chunk(zpad, w_ref, cb_ref, ci))
            ln = xh * g_ref[...] + be_ref[...]
            c_ref[ci * CHUNK:(ci + 1) * CHUNK, :] = (ln * jax.nn.sigmoid(ln)).astype(BF)

    return pl.pallas_call(
        body, name="conv_fwd", grid=(1,),
        in_specs=[*_GLU_SPECS, pl.BlockSpec((CONV_PAD, C), lambda i: (0, 0)), vec, vec, vec],
        out_specs=pl.BlockSpec((T, C), lambda i: (0, 0)),
        out_shape=_sds((T, C), BF),
        scratch_shapes=[pltpu.VMEM((T + CONV_PAD, C), F32)],
        compiler_params=_params(("arbitrary",)),
    )(proj, proj, proj, proj, conv_w, conv_b, ln_g, ln_b)


def _conv_bwd(proj, d_c, conv_w, conv_b, ln_g, ln_b):
    T, C = proj.shape[0], CONV_CHANNELS
    vec = pl.BlockSpec((1, C), lambda i: (0, 0))
    wspec = pl.BlockSpec((CONV_PAD, C), lambda i: (0, 0))

    def body(a0, a1, b0, b1, dc_ref, w_ref, cb_ref, g_ref, be_ref, dglu_ref, dw_ref, dcb_ref, dg_ref, dbe_ref,
             zpad, dupad):
        _glu_to_pad(a0, a1, b0, b1, zpad)
        dupad[T:, :] = jnp.zeros((CONV_PAD, C), F32)
        dw_ref[...] = jnp.zeros_like(dw_ref)
        dcb_ref[...] = jnp.zeros_like(dcb_ref)
        dg_ref[...] = jnp.zeros_like(dg_ref)
        dbe_ref[...] = jnp.zeros_like(dbe_ref)
        for ci in range(T // CHUNK):
            rows = slice(ci * CHUNK, (ci + 1) * CHUNK)
            xh, rstd = _ln_parts(_conv_chunk(zpad, w_ref, cb_ref, ci))
            ln = xh * g_ref[...] + be_ref[...]
            sg = jax.nn.sigmoid(ln)
            dln = dc_ref[rows, :].astype(F32) * (sg * (1.0 + ln * (1.0 - sg)))
            dg_ref[...] += jnp.sum(dln * xh, axis=0, keepdims=True)
            dbe_ref[...] += jnp.sum(dln, axis=0, keepdims=True)
            dxh = dln * g_ref[...]
            du = rstd * (dxh - jnp.mean(dxh, axis=-1, keepdims=True)
                         - xh * jnp.mean(dxh * xh, axis=-1, keepdims=True))
            dupad[rows, :] = du
            dcb_ref[...] += jnp.sum(du, axis=0, keepdims=True)
            base = ci * CHUNK + CONV_PAD - (CONV_WIDTH - 1)
            for k in range(CONV_WIDTH):
                dw_ref[k:k + 1, :] += jnp.sum(du * zpad[base + k:base + k + CHUNK, :], axis=0, keepdims=True)
        for ci in range(T // CHUNK):
            rows = slice(ci * CHUNK, (ci + 1) * CHUNK)
            dz = jnp.zeros((CHUNK, C), F32)
            for k in range(CONV_WIDTH):
                off = ci * CHUNK + (CONV_WIDTH - 1) - k
                dz = dz + w_ref[k:k + 1, :] * dupad[off:off + CHUNK, :]
            for half, (a, b) in enumerate(((a0, b0), (a1, b1))):
                sb = jax.nn.sigmoid(b[rows, :])
                dzh = dz[:, half * 256:(half + 1) * 256]
                dglu_ref[rows, half * 256:(half + 1) * 256] = (dzh * sb).astype(BF)
                dglu_ref[rows, C + half * 256:C + (half + 1) * 256] = (dzh * a[rows, :] * sb * (1.0 - sb)).astype(BF)

    return pl.pallas_call(
        body, name="conv_bwd", grid=(1,),
        in_specs=[*_GLU_SPECS, pl.BlockSpec((T, C), lambda i: (0, 0)), wspec, vec, vec, vec],
        out_specs=[pl.BlockSpec((T, 2 * C), lambda i: (0, 0)), wspec, vec, vec, vec],
        out_shape=[_sds((T, 2 * C), BF), _sds((CONV_PAD, C), F32), _sds((1, C), F32), _sds((1, C), F32),
                   _sds((1, C), F32)],
        scratch_shapes=[pltpu.VMEM((T + CONV_PAD, C), F32), pltpu.VMEM((T + CONV_PAD, C), F32)],
        compiler_params=_params(("arbitrary",)),
    )(proj, proj, proj, proj, d_c, conv_w, conv_b, ln_g, ln_b)


_GATE_BLK = GATE_OFF // 256


def _merge(ya, yc, proj):
    T, D = ya.shape
    tm, tn = 512, 256
    nj = D // tn

    def body(ya_ref, yc_ref, g0_ref, g1_ref, o_ref):
        o_ref[...] = (jax.nn.sigmoid(g0_ref[...]) * ya_ref[...]
                      + jax.nn.sigmoid(g1_ref[...]) * yc_ref[...]).astype(BF)

    t = pl.BlockSpec((tm, tn), lambda i, j: (i, j))
    return pl.pallas_call(
        body, name="gated_merge", grid=(T // tm, nj),
        in_specs=[t, t, pl.BlockSpec((tm, tn), lambda i, j: (i, _GATE_BLK + j)),
                  pl.BlockSpec((tm, tn), lambda i, j: (i, _GATE_BLK + nj + j))],
        out_specs=t, out_shape=_sds((T, D), BF),
        compiler_params=_params(("arbitrary", "arbitrary")),
    )(ya, yc, proj, proj)


def _merge_bwd(dm, ya, yc, proj):
    T, D = ya.shape
    tm, tn = 512, 256
    nj = D // tn

    def body(dm_ref, ya_ref, yc_ref, g0_ref, g1_ref, dya_ref, dyc_ref, dg0_ref, dg1_ref):
        dmv = dm_ref[...]
        s0 = jax.nn.sigmoid(g0_ref[...])
        s1 = jax.nn.sigmoid(g1_ref[...])
        dya_ref[...] = (dmv * s0).astype(BF)
        dyc_ref[...] = (dmv * s1).astype(BF)
        dg0_ref[...] = (dmv * ya_ref[...] * s0 * (1.0 - s0)).astype(BF)
        dg1_ref[...] = (dmv * yc_ref[...] * s1 * (1.0 - s1)).astype(BF)

    t = pl.BlockSpec((tm, tn), lambda i, j: (i, j))
    return pl.pallas_call(
        body, name="gated_merge_bwd", grid=(T // tm, nj),
        in_specs=[t, t, t, pl.BlockSpec((tm, tn), lambda i, j: (i, _GATE_BLK + j)),
                  pl.BlockSpec((tm, tn), lambda i, j: (i, _GATE_BLK + nj + j))],
        out_specs=[t, t, t, t], out_shape=[_sds((T, D), BF)] * 4,
        compiler_params=_params(("arbitrary", "arbitrary")),
    )(dm, ya, yc, proj, proj)


def _swiglu(gu):
    T = gu.shape[0]
    tm, tn = 512, 256
    nj = D_FF // tn

    def body(g_ref, u_ref, o_ref):
        g = g_ref[...]
        o_ref[...] = (g * jax.nn.sigmoid(g) * u_ref[...]).astype(BF)

    return pl.pallas_call(
        body, name="swiglu", grid=(T // tm, nj),
        in_specs=[pl.BlockSpec((tm, tn), lambda i, j: (i, j)), pl.BlockSpec((tm, tn), lambda i, j: (i, nj + j))],
        out_specs=pl.BlockSpec((tm, tn), lambda i, j: (i, j)), out_shape=_sds((T, D_FF), BF),
        compiler_params=_params(("arbitrary", "arbitrary")),
    )(gu, gu)


def _local_step(x, target, small, wi_t, wap_t, wcp_t, w_out, wf_t, w_down, conv_w):
    T, D = x.shape
    tm = 256

    h, r1 = _rms_fwd("rms_mix", x, small["g_mix_norm"])

    def ep_add(acc, ex, outs, ids):
        outs[0][...] = acc + ex[0][...]

    tn_in = IN_WIDTH // 3
    proj, = _matmul("proj_in", [h], wi_t, "NT", m=T, n=IN_WIDTH, tm=tm, tn=tn_in, tk=D, epilogue=ep_add,
                    extra=[(small["b_in"], _row(tn_in))], outs=[(_sds((T, IN_WIDTH), F32), _tile(tm, tn_in))])
    o = _attn_fwd(proj, small["sinks"])
    c = _conv_fwd(proj, conv_w, small["conv_b"], small["ln_g"], small["ln_b"])
    ya, = _matmul("attn_proj", [o], wap_t, "NT", m=T, n=D, tm=tm, tn=D, tk=ATTN_WIDTH, epilogue=_store(F32),
                  outs=[(_sds((T, D), F32), _tile(tm, D))])
    yc, = _matmul("conv_proj", [c], wcp_t, "NT", m=T, n=D, tm=tm, tn=D, tk=CONV_CHANNELS, epilogue=ep_add,
                  extra=[(small["b_conv_proj"], _row(D))], outs=[(_sds((T, D), F32), _tile(tm, D))])
    merged = _merge(ya, yc, proj)
    x2, = _matmul("out_proj", [merged], w_out, "NN", m=T, n=D, tm=tm, tn=D, tk=D, epilogue=ep_add,
                  extra=[(x, _tile(tm, D))], outs=[(_sds((T, D), F32), _tile(tm, D))])
    h2, r2 = _rms_fwd("rms_ffn", x2, small["g_ffn_norm"])
    gu, = _matmul("ffn_in", [h2], wf_t, "NT", m=T, n=2 * D_FF, tm=tm, tn=D_FF, tk=D, epilogue=_store(F32),
                  outs=[(_sds((T, 2 * D_FF), F32), _tile(tm, D_FF))])
    act = _swiglu(gu)
    x3, = _matmul("ffn_down", [act], w_down, "NN", m=T, n=D, tm=tm, tn=D, tk=D_FF, epilogue=ep_add,
                  extra=[(x2, _tile(tm, D))], outs=[(_sds((T, D), F32), _tile(tm, D))])
    dx3, dg_final, loss = _final(x3, small["g_final"], target)

    tn_ff = D_FF // 2

    def ep_swiglu_bwd(acc, ex, outs, ids):
        g, u = ex[0][...], ex[1][...]
        sg = jax.nn.sigmoid(g)
        outs[0][...] = (acc * u * sg * (1.0 + g * (1.0 - sg))).astype(BF)
        outs[1][...] = (acc * g * sg).astype(BF)

    dgate, dup = _matmul(
        "ffn_down_bwd", [dx3], w_down, "NT", m=T, n=D_FF, tm=tm, tn=tn_ff, tk=D, epilogue=ep_swiglu_bwd,
        extra=[(gu, _tile(tm, tn_ff)), (gu, pl.BlockSpec((tm, tn_ff), lambda j, i, k: (i, 2 + j)))],
        outs=[(_sds((T, D_FF), BF), _tile(tm, tn_ff)), (_sds((T, D_FF), BF), _tile(tm, tn_ff))])

    def dw(name, a, b, rows, cols, blk_off=0, alias=None, total_rows=None):
        total_rows = rows if total_rows is None else total_rows
        out, = _matmul(name, [a], b, "TN", m=rows, n=cols, tm=256, tn=cols, tk=T, epilogue=_store(BF),
                       outs=[(_sds((total_rows, cols), BF), pl.BlockSpec((256, cols), lambda j, i, k: (blk_off + i, j)))],
                       alias=None if alias is None else (alias, 0))
        return out

    gw_down = dw("ffn_down_dw", act, dx3, D_FF, D)

    def ep_rms_bwd(acc, ex, outs, ids):
        dx, dg = _rms_bwd(acc, ex[0][...], ex[1][...], ex[2][...])
        outs[0][...] = ex[3][...] + dx
        _accumulate_rows(outs[1], dg, ids[1] == 0)

    def rms_bwd_io(tm_, xin, r, g, dres):
        return dict(
            extra=[(xin, _tile(tm_, D)), (r, pl.BlockSpec((tm_, 1), lambda j, i, k: (i, 0))), (g, _row(D)),
                   (dres, _tile(tm_, D))],
            outs=[(_sds((T, D), F32), _tile(tm_, D)), (_sds((1, D), F32), _row(D))])

    dx2, dg_ffn = _matmul("ffn_in_bwd", [dgate, dup], wf_t, "NN", m=T, n=D, tm=tm, tn=D, tk=tn_ff,
                          epilogue=ep_rms_bwd, **rms_bwd_io(tm, x2, r2, small["g_ffn_norm"], dx3))
    gwf_t = dw("ffn_in_dw_gate", dgate, h2, D_FF, D, total_rows=2 * D_FF)
    gwf_t = dw("ffn_in_dw_up", dup, h2, D_FF, D, blk_off=D_FF // 256, alias=gwf_t, total_rows=2 * D_FF)

    dm, = _matmul("out_proj_bwd", [dx2], w_out, "NT", m=T, n=D, tm=tm, tn=D, tk=D, epilogue=_store(F32),
                  outs=[(_sds((T, D), F32), _tile(tm, D))])
    gw_out = dw("out_proj_dw", merged, dx2, D, D)
    dya, dyc, dg0, dg1 = _merge_bwd(dm, ya, yc, proj)
    d_o, = _matmul("attn_proj_bwd", [dya], wap_t, "NN", m=T, n=ATTN_WIDTH, tm=tm, tn=ATTN_WIDTH, tk=D,
                   epilogue=_store(BF), outs=[(_sds((T, ATTN_WIDTH), BF), _tile(tm, ATTN_WIDTH))])
    d_c, = _matmul("conv_proj_bwd", [dyc], wcp_t, "NN", m=T, n=CONV_CHANNELS, tm=tm, tn=CONV_CHANNELS, tk=D,
                   epilogue=_store(BF), outs=[(_sds((T, CONV_CHANNELS), BF), _tile(tm, CONV_CHANNELS))])
    gwap_t = dw("attn_proj_dw", dya, o, D, ATTN_WIDTH)
    gwcp_t = dw("conv_proj_dw", dyc, c, D, CONV_CHANNELS)
    db_cp = _colsum("conv_proj_db", dyc)
    dglu, dcw, dcb, dlng, dlnb = _conv_bwd(proj, d_c, conv_w, small["conv_b"], small["ln_g"], small["ln_b"])
    dqkv, dsinks = _attn_bwd(proj, d_o, small["sinks"])

    segs = [dqkv, dglu, dg0, dg1]
    gwi_t, off = None, 0
    for s, seg in enumerate(segs):
        gwi_t = dw(f"proj_in_dw{s}", seg, h, seg.shape[1], D, blk_off=off, alias=gwi_t, total_rows=IN_WIDTH)
        off += seg.shape[1] // 256
    db_in = [_colsum(f"proj_in_db{s}", seg) for s, seg in enumerate(segs)]
    dx, dg_mix = _matmul("proj_in_bwd", segs, wi_t, "NN", m=T, n=D, tm=512, tn=D, tk=256, epilogue=ep_rms_bwd,
                         **rms_bwd_io(512, x, r1, small["g_mix_norm"], dx2))

    big = dict(w_in=gwi_t, w_attn_proj=gwap_t, w_conv_proj=gwcp_t, w_out=gw_out, w_ffn_in=gwf_t, w_ffn_down=gw_down)
    parts = dict(g_mix_norm=dg_mix, b_in=db_in, sinks=dsinks, conv_w=dcw, conv_b=dcb, ln_g=dlng, ln_b=dlnb,
                 b_conv_proj=db_cp, g_ffn_norm=dg_ffn, g_final=dg_final, loss=loss)
    return dx, big, parts


ANY = pl.BlockSpec(memory_space=pl.ANY)


def _place():
    x, y, c = lax.axis_index("x"), lax.axis_index("y"), lax.axis_index("c")
    return x, y, c, [(1 - x, y), (x, 1 - y), (1 - x, 1 - y)]


def _gather_blocks(x_refs, out_refs, rows_per, send_sems, recv_sems, local_sems):
    x, y, c, chips = _place()
    me, sibling = (x, y, c), (x, y, 1 - c)

    def rows(a, px, py, pc):
        return out_refs[a].at[pl.ds((4 * px + 2 * py + pc) * rows_per[a], rows_per[a])]

    def copy(a, k, block, to, src=None):
        return pltpu.make_async_remote_copy(
            src_ref=rows(a, *block) if src is None else src, dst_ref=rows(a, *block),
            send_sem=send_sems.at[7 * a + k], recv_sem=recv_sems.at[7 * a + k], device_id=to, device_id_type=MESH)

    n = len(x_refs)
    local, sent = [], []
    for a in range(n):
        mine = pltpu.make_async_copy(x_refs[a], rows(a, *me), local_sems.at[a])
        mine.start()
        local.append(mine)
        first = [copy(a, 0, me, sibling, src=x_refs[a])]
        first += [copy(a, 1 + j, me, (*chip, c), src=x_refs[a]) for j, chip in enumerate(chips)]
        for cp in first:
            cp.start()
        sent += first
    for a in range(n):
        for j, chip in enumerate(chips):
            copy(a, 1 + j, (*chip, c), me).wait_recv()
            passed = copy(a, 4 + j, (*chip, c), sibling)
            passed.start()
            sent.append(passed)
    for a in range(n):
        copy(a, 0, sibling, me).wait_recv()
        for j, chip in enumerate(chips):
            copy(a, 4 + j, (*chip, 1 - c), me).wait_recv()
    for cp in sent:
        cp.wait_send()
    for cp in local:
        cp.wait()


def _all_gather(shards):
    n = len(shards)
    rows_per = [s.shape[0] for s in shards]

    def body(*refs):
        _gather_blocks(refs[:n], refs[n:2 * n], rows_per, *refs[2 * n:])

    return pl.pallas_call(
        body, name="weights_all_gather",
        in_specs=[ANY] * n, out_specs=[ANY] * n,
        out_shape=[_sds((N_DEV * s.shape[0],) + s.shape[1:], s.dtype) for s in shards],
        scratch_shapes=[pltpu.SemaphoreType.DMA((7 * n,)), pltpu.SemaphoreType.DMA((7 * n,)),
                        pltpu.SemaphoreType.DMA((n,))],
    )(*shards)


def _swap_halves(grads):
    n = len(grads)

    def body(*refs):
        g_refs, out_refs, send_sems, recv_sems = refs[:n], refs[n:2 * n], refs[2 * n], refs[2 * n + 1]
        x, y, c, _ = _place()
        copies = []
        for a in range(n):
            for p in range(4):
                cp = pltpu.make_async_remote_copy(
                    src_ref=g_refs[a].at[2 * p + 1 - c], dst_ref=out_refs[a].at[p],
                    send_sem=send_sems.at[4 * a + p], recv_sem=recv_sems.at[4 * a + p],
                    device_id=(x, y, 1 - c), device_id_type=MESH)
                cp.start()
                copies.append(cp)
        for cp in copies:
            cp.wait()

    return pl.pallas_call(
        body, name="grad_swap_halves",
        in_specs=[ANY] * n, out_specs=[ANY] * n,
        out_shape=[_sds((4,) + g.shape[1:], g.dtype) for g in grads],
        scratch_shapes=[pltpu.SemaphoreType.DMA((4 * n,)), pltpu.SemaphoreType.DMA((4 * n,))],
    )(*grads)


def _chip_sum(name, g, got, c):
    _, rows, cols = g.shape

    def body(c_ref, g_ref, got_ref, o_ref):
        o_ref[...] = (g_ref[...].astype(F32) + got_ref[...].astype(F32)).astype(BF)

    return pl.pallas_call(
        body, name=name,
        grid_spec=pltpu.PrefetchScalarGridSpec(
            num_scalar_prefetch=1, grid=(4,),
            in_specs=[pl.BlockSpec((1, rows, cols), lambda p, c_ref: (2 * p + c_ref[0], 0, 0)),
                      pl.BlockSpec((1, rows, cols), lambda p, c_ref: (p, 0, 0))],
            out_specs=pl.BlockSpec((1, rows, cols), lambda p, c_ref: (p, 0, 0))),
        out_shape=_sds((4, rows, cols), BF),
        compiler_params=_params(("arbitrary",)),
    )(c, g, got)


def _send_chip_sums(sums):
    n = len(sums)

    def body(*refs):
        s_refs, out_refs, send_sems, recv_sems = refs[:n], refs[n:2 * n], refs[2 * n], refs[2 * n + 1]
        x, y, c, chips = _place()
        copies = []
        for a in range(n):
            for k, (px, py) in enumerate(chips):
                cp = pltpu.make_async_remote_copy(
                    src_ref=s_refs[a].at[2 * px + py], dst_ref=out_refs[a].at[k],
                    send_sem=send_sems.at[3 * a + k], recv_sem=recv_sems.at[3 * a + k],
                    device_id=(px, py, c), device_id_type=MESH)
                cp.start()
                copies.append(cp)
        for cp in copies:
            cp.wait()

    return pl.pallas_call(
        body, name="grad_send_chip_sums",
        in_specs=[ANY] * n, out_specs=[ANY] * n,
        out_shape=[_sds((3,) + s.shape[1:], s.dtype) for s in sums],
        scratch_shapes=[pltpu.SemaphoreType.DMA((3 * n,)), pltpu.SemaphoreType.DMA((3 * n,))],
    )(*sums)


def _grad_total(name, g, got, got3, ids):
    _, rows, cols = g.shape

    def body(ids_ref, g_ref, got_ref, got3_ref, o_ref):
        tot = g_ref[0].astype(F32) + got_ref[0].astype(F32)
        for k in range(3):
            tot = tot + got3_ref[k].astype(F32)
        o_ref[...] = tot

    return pl.pallas_call(
        body, name=name,
        grid_spec=pltpu.PrefetchScalarGridSpec(
            num_scalar_prefetch=1, grid=(1,),
            in_specs=[pl.BlockSpec((1, rows, cols), lambda i, ids_ref: (ids_ref[0], 0, 0)),
                      pl.BlockSpec((1, rows, cols), lambda i, ids_ref: (ids_ref[1], 0, 0)),
                      pl.BlockSpec((3, rows, cols), lambda i, ids_ref: (0, 0, 0))],
            out_specs=pl.BlockSpec((rows, cols), lambda i, ids_ref: (0, 0))),
        out_shape=_sds((rows, cols), F32),
        compiler_params=_params(("arbitrary",)),
    )(ids, g, got, got3)


def _adam_math(w, g, m, v):
    m = ADAM_B1 * m + (1.0 - ADAM_B1) * g
    v = ADAM_B2 * v + (1.0 - ADAM_B2) * (g * g)
    m_hat = m / (1.0 - ADAM_B1 ** ADAM_STEP)
    v_hat = v / (1.0 - ADAM_B2 ** ADAM_STEP)
    delta = -ADAM_LR * (m_hat / (jnp.sqrt(v_hat) + ADAM_EPS) + ADAM_WD * w)
    return delta, m, v


def _adamw(name, w, g, m, v):
    rows, cols = w.shape
    tr = 256 if rows % 256 == 0 else rows

    def body(w_ref, g_ref, m_ref, v_ref, d_ref, nm_ref, nv_ref):
        d_ref[...], nm_ref[...], nv_ref[...] = _adam_math(w_ref[...], g_ref[...], m_ref[...], v_ref[...])

    t = pl.BlockSpec((tr, cols), lambda i: (i, 0))
    return pl.pallas_call(
        body, name=name, grid=(rows // tr,), in_specs=[t] * 4, out_specs=[t] * 3,
        out_shape=[_sds((rows, cols), F32)] * 3, compiler_params=_params(("arbitrary",)),
    )(w, g, m, v)


SMALL_NAMES = ["g_mix_norm", "b_in", "sinks", "conv_b", "ln_g", "ln_b", "b_conv_proj", "g_ffn_norm", "g_final"]
_PACK_ROWS = 32


def _small_step(parts, small_w, small_m, small_v):
    C = CONV_CHANNELS
    names = SMALL_NAMES
    widths = [small_w[k].shape[1] for k in names]
    part_list = [parts["g_mix_norm"], *parts["b_in"], parts["sinks"], parts["conv_b"], parts["ln_g"], parts["ln_b"],
                 parts["b_conv_proj"], parts["g_ffn_norm"], parts["g_final"], parts["loss"], parts["conv_w"]]
    n_part = len(part_list)
    n_small = len(names)

    def body(*refs):
        (p_mix, p_b0, p_b1, p_b2, p_b3, p_sink, p_cb, p_lg, p_lb, p_bcp, p_ffn, p_fin, p_loss, p_cw) = refs[:n_part]
        w_refs = refs[n_part:n_part + n_small]
        m_refs = refs[n_part + n_small:n_part + 2 * n_small]
        v_refs = refs[n_part + 2 * n_small:n_part + 3 * n_small]
        o = n_part + 3 * n_small
        loss_ref, cw_ref = refs[o], refs[o + 1]
        out_refs = refs[o + 2:o + 2 + 4 * n_small]
        pack, gathered, send_sems, recv_sems, local_sems = refs[o + 2 + 4 * n_small:]

        pack[...] = jnp.zeros_like(pack)
        pack[0:1, :] = p_mix[...]
        pack[1:2, 0:GLU_OFF] = p_b0[...]
        pack[2:3, :] = p_b1[...]
        pack[3:4, :] = p_b2[...]
        pack[4:5, :] = p_b3[...]
        pack[5:6, 0:128] = p_sink[...]
        pack[6:7, 0:C] = p_cb[...]
        pack[6:7, C:2 * C] = p_lg[...]
        pack[7:8, 0:C] = p_lb[...]
        pack[8:9, :] = p_bcp[...]
        pack[9:10, :] = p_ffn[...]
        pack[10:11, :] = p_fin[...]
        pack[11:12, 0:128] = jnp.broadcast_to(p_loss[...], (1, 128))
        pack[12:28, 0:C] = p_cw[0:16, :]
        pack[12:28, C:2 * C] = p_cw[16:32, :]

        _gather_blocks([pack], [gathered], [_PACK_ROWS], send_sems, recv_sems, local_sems)
        tot = gathered[0:_PACK_ROWS, :]
        for d in range(1, N_DEV):
            tot = tot + gathered[d * _PACK_ROWS:(d + 1) * _PACK_ROWS, :]

        loss_ref[...] = tot[11:12, 0:1]
        cw_ref[0:16, :] = tot[12:28, 0:C]
        cw_ref[16:32, :] = tot[12:28, C:2 * C]
        grads = dict(
            g_mix_norm=tot[0:1, :],
            b_in=jnp.concatenate([tot[1:2, 0:GLU_OFF], tot[2:3, :], tot[3:4, :], tot[4:5, :]], axis=1),
            sinks=tot[5:6, 0:N_Q_HEADS], conv_b=tot[6:7, 0:C], ln_g=tot[6:7, C:2 * C], ln_b=tot[7:8, 0:C],
            b_conv_proj=tot[8:9, :], g_ffn_norm=tot[9:10, :], g_final=tot[10:11, :])
        for s, k in enumerate(names):
            g = grads[k]
            d, nm, nv = _adam_math(w_refs[s][...], g, m_refs[s][...], v_refs[s][...])
            out_refs[4 * s][...] = g
            out_refs[4 * s + 1][...] = d
            out_refs[4 * s + 2][...] = nm
            out_refs[4 * s + 3][...] = nv

    vm = pl.BlockSpec(memory_space=pltpu.VMEM)
    args = [*part_list, *[small_w[k] for k in names], *[small_m[k] for k in names], *[small_v[k] for k in names]]
    out_shape = [_sds((1, 1), F32), _sds((CONV_PAD, C), F32)]
    for wd in widths:
        out_shape += [_sds((1, wd), F32)] * 4
    res = pl.pallas_call(
        body, name="small_all_reduce_adamw",
        in_specs=[vm] * len(args), out_specs=[vm] * len(out_shape), out_shape=out_shape,
        scratch_shapes=[pltpu.VMEM((_PACK_ROWS, D_MODEL), F32), pltpu.VMEM((N_DEV * _PACK_ROWS, D_MODEL), F32),
                        pltpu.SemaphoreType.DMA((7,)), pltpu.SemaphoreType.DMA((7,)), pltpu.SemaphoreType.DMA((1,))],
        compiler_params=pltpu.CompilerParams(vmem_limit_bytes=VMEM_LIMIT_BYTES),
    )(*args)
    return res[0], res[1], {k: res[2 + 4 * s:6 + 4 * s] for s, k in enumerate(names)}


BIG = dict(w_in=True, w_attn_proj=True, w_conv_proj=True, w_out=False, w_ffn_in=True, w_ffn_down=False)
WEIGHT_NAMES = ["g_mix_norm", "w_in", "b_in", "sinks", "conv_w", "conv_b", "ln_g", "ln_b", "w_attn_proj",
                "w_conv_proj", "b_conv_proj", "w_out", "g_ffn_norm", "w_ffn_in", "w_ffn_down", "g_final"]


def kernel(x, g_mix_norm, w_in, b_in, sinks, conv_w, conv_b, ln_g, ln_b, w_attn_proj, w_conv_proj, b_conv_proj, w_out, g_ffn_norm, w_ffn_in, w_ffn_down, g_final, loss_target, m_g_mix_norm, m_w_in, m_b_in, m_sinks, m_conv_w, m_conv_b, m_ln_g, m_ln_b, m_w_attn_proj, m_w_conv_proj, m_b_conv_proj, m_w_out, m_g_ffn_norm, m_w_ffn_in, m_w_ffn_down, m_g_final, v_g_mix_norm, v_w_in, v_b_in, v_sinks, v_conv_w, v_conv_b, v_ln_g, v_ln_b, v_w_attn_proj, v_w_conv_proj, v_b_conv_proj, v_w_out, v_g_ffn_norm, v_w_ffn_in, v_w_ffn_down, v_g_final):
    w = dict(g_mix_norm=g_mix_norm, w_in=w_in, b_in=b_in, sinks=sinks, conv_w=conv_w, conv_b=conv_b, ln_g=ln_g,
             ln_b=ln_b, w_attn_proj=w_attn_proj, w_conv_proj=w_conv_proj, b_conv_proj=b_conv_proj, w_out=w_out,
             g_ffn_norm=g_ffn_norm, w_ffn_in=w_ffn_in, w_ffn_down=w_ffn_down, g_final=g_final)
    m = dict(g_mix_norm=m_g_mix_norm, w_in=m_w_in, b_in=m_b_in, sinks=m_sinks, conv_w=m_conv_w, conv_b=m_conv_b,
             ln_g=m_ln_g, ln_b=m_ln_b, w_attn_proj=m_w_attn_proj, w_conv_proj=m_w_conv_proj,
             b_conv_proj=m_b_conv_proj, w_out=m_w_out, g_ffn_norm=m_g_ffn_norm, w_ffn_in=m_w_ffn_in,
             w_ffn_down=m_w_ffn_down, g_final=m_g_final)
    v = dict(g_mix_norm=v_g_mix_norm, w_in=v_w_in, b_in=v_b_in, sinks=v_sinks, conv_w=v_conv_w, conv_b=v_conv_b,
             ln_g=v_ln_g, ln_b=v_ln_b, w_attn_proj=v_w_attn_proj, w_conv_proj=v_w_conv_proj,
             b_conv_proj=v_b_conv_proj, w_out=v_w_out, g_ffn_norm=v_g_ffn_norm, w_ffn_in=v_w_ffn_in,
             w_ffn_down=v_w_ffn_down, g_final=v_g_final)
    ax, ay, ac = lax.axis_index("x"), lax.axis_index("y"), lax.axis_index("c")
    me = 4 * ax + 2 * ay + ac
    chip = 2 * ax + ay

    shards = [(w[k][0].T if tr else w[k][0]).astype(BF) for k, tr in BIG.items()]
    cw_shard = jnp.pad(conv_w[0].T, ((0, 0), (0, 1))).reshape(16, 128)
    *full, cw_full = _all_gather([*shards, cw_shard])
    conv_full = cw_full.reshape(CONV_CHANNELS, CONV_PAD).T
    wi_t, wap_t, wcp_t, wout_f, wf_t, wd_f = full

    as_row = lambda a: a.reshape(1, -1)
    small_w = {k: as_row(w[k]) for k in SMALL_NAMES}
    small_m = {k: as_row(m[k]) for k in SMALL_NAMES}
    small_v = {k: as_row(v[k]) for k in SMALL_NAMES}
    dx, big, parts = _local_step(x[0], loss_target[0], small_w, wi_t, wap_t, wcp_t, wout_f, wf_t, wd_f, conv_full)

    names = list(BIG)
    slots = [big[k].reshape(N_DEV, big[k].shape[0] // N_DEV, big[k].shape[1]) for k in names]
    got = _swap_halves(slots)
    c1 = ac.reshape(1).astype(jnp.int32)
    sums = [_chip_sum(f"chip_sum_{k}", g, r, c1) for k, g, r in zip(names, slots, got)]
    got3 = _send_chip_sums(sums)
    ids = jnp.stack([me, chip]).astype(jnp.int32)
    grads, delta, new_m, new_v = {}, {}, {}, {}
    for k, g, r, r3 in zip(names, slots, got, got3):
        tot = _grad_total(f"grad_total_{k}", g, r, r3, ids)
        tot = tot.T if BIG[k] else tot
        d, nm, nv = _adamw(f"adamw_{k}", w[k][0], tot, m[k][0], v[k][0])
        grads[k], delta[k], new_m[k], new_v[k] = tot[None], d[None], nm[None], nv[None]

    loss, cw_grad, small_out = _small_step(parts, small_w, small_m, small_v)
    for k in SMALL_NAMES:
        g, d, nm, nv = (a.reshape(w[k].shape) for a in small_out[k])
        grads[k], delta[k], new_m[k], new_v[k] = g, d, nm, nv
    cw_mine = lax.dynamic_slice(cw_grad, (0, me * 64), (CONV_WIDTH, 64))
    d, nm, nv = _adamw("adamw_conv_w", conv_w[0], cw_mine, m_conv_w[0], v_conv_w[0])
    grads["conv_w"], delta["conv_w"], new_m["conv_w"], new_v["conv_w"] = cw_mine[None], d[None], nm[None], nv[None]

    return (loss.reshape(()), dx[None], *[grads[k] for k in WEIGHT_NAMES], *[delta[k] for k in WEIGHT_NAMES],
            *[new_m[k] for k in WEIGHT_NAMES], *[new_v[k] for k in WEIGHT_NAMES])
```

```python
import functools

import jax
import jax.numpy as jnp
from jax import lax
from jax.experimental import pallas as pl
from jax.experimental.pallas import tpu as pltpu

F32 = jnp.float32
BF = jnp.bfloat16

SEQ = 2048
D_MODEL = 1024
HEAD_DIM = 64
N_Q_HEADS = 8
N_KV_HEADS = 2
GROUP = N_Q_HEADS // N_KV_HEADS
BLOCK = 128
ATTN_WIDTH = 512
KV_WIDTH = 128
CONV_CHANNELS = 512
CONV_WIDTH = 31
CONV_PAD = 32
GLU_OFF = 768
GATE_OFF = 1792
IN_WIDTH = 3840
D_FF = 2816
EPS = 1e-5
NEG = -1e30
N_DEV = 8

ADAM_LR = 0.001
ADAM_B1 = 0.9
ADAM_B2 = 0.999
ADAM_EPS = 1e-08
ADAM_WD = 0.01
ADAM_STEP = 10

VMEM_LIMIT_BYTES = 56 * 1024 * 1024
MESH = pl.DeviceIdType.MESH

_DIMS = {"NN": (((1,), (0,)), ((), ())), "NT": (((1,), (1,)), ((), ())), "TN": (((0,), (0,)), ((), ()))}


def _params(sem):
    return pltpu.CompilerParams(dimension_semantics=sem, vmem_limit_bytes=VMEM_LIMIT_BYTES)


def _matmul(name, a_list, b, mode, *, m, n, tm, tn, tk=None, epilogue, extra=(), outs, b_off=(0, 0), alias=None,
            scratch=()):
    seg_k = [a.shape[0] if mode == "TN" else a.shape[1] for a in a_list]
    whole = tk is None
    seg_nk = [1] * len(a_list) if whole else [ks // tk for ks in seg_k]
    nk = 1 if whole else sum(seg_nk)
    starts = [sum(seg_nk[:s]) for s in range(len(seg_nk))]
    k_starts = [sum(seg_k[:s]) for s in range(len(seg_k))]
    k_tot = sum(seg_k)
    n_a, n_extra, n_out = len(a_list), len(extra), len(outs)

    a_specs = []
    for st, ns, ks in zip(starts, seg_nk, seg_k):
        if mode == "TN":
            a_specs.append(pl.BlockSpec((ks if whole else tk, tm), lambda j, i, k: (k, i)))
        elif whole:
            a_specs.append(pl.BlockSpec((tm, ks), lambda j, i, k: (i, 0)))
        else:
            a_specs.append(pl.BlockSpec((tm, tk), functools.partial(
                lambda j, i, k, st, ns: (i, jnp.clip(k - st, 0, ns - 1)), st=st, ns=ns)))
    bk = k_tot if whole else tk
    if mode == "NT":
        b_spec = pl.BlockSpec((tn, bk), lambda j, i, k: (b_off[0] + j, b_off[1] + k))
    else:
        b_spec = pl.BlockSpec((bk, tn), lambda j, i, k: (b_off[0] + k, b_off[1] + j))
    n_alias = 0 if alias is None else 1

    def body(*refs):
        a_refs = refs[:n_a]
        b_ref = refs[n_a]
        ex = refs[n_a + 1 + n_alias:n_a + 1 + n_alias + n_extra]
        out_refs = refs[n_a + 1 + n_alias + n_extra:n_a + 1 + n_alias + n_extra + n_out]
        k = pl.program_id(2)
        ids = (pl.program_id(0), pl.program_id(1))
        scr = refs[len(refs) - len(scratch):]

        def dot(a_ref, bv):
            return lax.dot_general(a_ref[...].astype(BF), bv.astype(BF), _DIMS[mode], preferred_element_type=F32)

        if whole:
            tot = None
            for a_ref, k0, ks in zip(a_refs, k_starts, seg_k):
                if n_a == 1:
                    bv = b_ref[...]
                else:
                    bv = b_ref[:, k0:k0 + ks] if mode == "NT" else b_ref[k0:k0 + ks, :]
                part = dot(a_ref, bv)
                tot = part if tot is None else tot + part
            epilogue(tot, ex, out_refs, ids, scr)
            return
        acc = refs[len(refs) - len(scratch) - 1]

        @pl.when(k == 0)
        def _():
            acc[...] = jnp.zeros_like(acc)

        for a_ref, st, ns in zip(a_refs, starts, seg_nk):
            if n_a == 1:
                acc[...] += dot(a_ref, b_ref[...])
            else:
                @pl.when((k >= st) & (k < st + ns))
                def _(a_ref=a_ref):
                    acc[...] += dot(a_ref, b_ref[...])

        @pl.when(k == nk - 1)
        def _():
            epilogue(acc[...], ex, out_refs, ids, scr)

    in_specs = [*a_specs, b_spec]
    args = [*a_list, b]
    io_alias = {}
    if alias is not None:
        in_specs.append(pl.BlockSpec(memory_space=pl.ANY))
        args.append(alias[0])
        io_alias = {n_a + 1: alias[1]}
    in_specs += [s for _, s in extra]
    args += [x for x, _ in extra]
    return pl.pallas_call(
        body, name=name, grid=(n // tn, m // tm, nk), in_specs=in_specs,
        out_specs=[s for _, s in outs], out_shape=[o for o, _ in outs],
        scratch_shapes=[*([] if whole else [pltpu.VMEM((tm, tn), F32)]), *scratch],
        input_output_aliases=io_alias,
        compiler_params=_params(("arbitrary", "arbitrary", "arbitrary")),
    )(*args)


def _tile(tm, tn):
    return pl.BlockSpec((tm, tn), lambda j, i, k: (i, j))


def _row(tn):
    return pl.BlockSpec((1, tn), lambda j, i, k: (0, j))


def _store(dtype):
    def ep(acc, ex, outs, ids, scr):
        outs[0][...] = acc.astype(dtype)
    return ep


def _sds(shape, dtype):
    return jax.ShapeDtypeStruct(shape, dtype)


def _rms_fwd(name, x, g):
    T, D = x.shape
    tm = 256

    def body(x_ref, g_ref, h_ref, r_ref):
        xv = x_ref[...]
        r = lax.rsqrt(jnp.mean(xv * xv, axis=-1, keepdims=True) + EPS)
        h_ref[...] = (xv * r * g_ref[...]).astype(BF)
        r_ref[...] = r

    return pl.pallas_call(
        body, name=name, grid=(T // tm,),
        in_specs=[pl.BlockSpec((tm, D), lambda i: (i, 0)), pl.BlockSpec((1, D), lambda i: (0, 0))],
        out_specs=[pl.BlockSpec((tm, D), lambda i: (i, 0)), pl.BlockSpec((tm, 1), lambda i: (i, 0))],
        out_shape=[_sds((T, D), BF), _sds((T, 1), F32)],
        compiler_params=_params(("arbitrary",)),
    )(x, g)


def _rms_bwd(dh, xv, r, g):
    xh = xv * r
    dxh = dh * g
    dx = r * (dxh - xh * jnp.mean(dxh * xh, axis=-1, keepdims=True))
    return dx, jnp.sum(dh * xh, axis=0, keepdims=True)


def _accumulate_rows(ref, val, first):
    @pl.when(first)
    def _():
        ref[...] = val

    @pl.when(jnp.logical_not(first))
    def _():
        ref[...] += val


def _final(x3, g_final, target):
    T, D = x3.shape
    tm = 256

    def body(x_ref, g_ref, t_ref, dx_ref, dxb_ref, dg_ref, loss_ref):
        i = pl.program_id(0)
        xv = x_ref[...]
        g = g_ref[...]
        r = lax.rsqrt(jnp.mean(xv * xv, axis=-1, keepdims=True) + EPS)
        err = xv * r * g - t_ref[...]
        dy = err * (1.0 / D)
        dx, dg = _rms_bwd(dy, xv, r, g)
        dx_ref[...] = dx
        dxb_ref[...] = dx.astype(BF)
        part = 0.5 * jnp.sum(jnp.mean(err * err, axis=-1, keepdims=True), axis=0, keepdims=True)
        _accumulate_rows(dg_ref, dg, i == 0)
        _accumulate_rows(loss_ref, part, i == 0)

    return pl.pallas_call(
        body, name="final_loss", grid=(T // tm,),
        in_specs=[pl.BlockSpec((tm, D), lambda i: (i, 0)), pl.BlockSpec((1, D), lambda i: (0, 0)),
                  pl.BlockSpec((tm, D), lambda i: (i, 0))],
        out_specs=[pl.BlockSpec((tm, D), lambda i: (i, 0)), pl.BlockSpec((tm, D), lambda i: (i, 0)),
                   pl.BlockSpec((1, D), lambda i: (0, 0)), pl.BlockSpec((1, 1), lambda i: (0, 0))],
        out_shape=[_sds((T, D), F32), _sds((T, D), BF), _sds((1, D), F32), _sds((1, 1), F32)],
        compiler_params=_params(("arbitrary",)),
    )(x3, g_final, target)


def _lane_half(shape, h):
    lane = lax.broadcasted_iota(jnp.int32, shape, 1)
    return (lane >= HEAD_DIM * h) & (lane < HEAD_DIM * (h + 1))


def _to_half(v, w, h):
    if w != h:
        v = pltpu.roll(v, HEAD_DIM, 1)
    return jnp.where(_lane_half(v.shape, h), v, 0.0)


def _attn_block(qkv_ref, sinks_ref, n, h):
    r0 = pl.multiple_of(n * BLOCK, BLOCK)
    p0 = pl.multiple_of(jnp.maximum(n - 1, 0) * BLOCK, BLOCK)
    rows = pl.ds(r0, BLOCK)
    prev = pl.ds(p0, BLOCK)
    k2 = jnp.concatenate([qkv_ref[prev, ATTN_WIDTH:ATTN_WIDTH + KV_WIDTH],
                          qkv_ref[rows, ATTN_WIDTH:ATTN_WIDTH + KV_WIDTH]], axis=0).astype(BF)
    v2 = jnp.concatenate([qkv_ref[prev, ATTN_WIDTH + KV_WIDTH:ATTN_WIDTH + 2 * KV_WIDTH],
                          qkv_ref[rows, ATTN_WIDTH + KV_WIDTH:ATTN_WIDTH + 2 * KV_WIDTH]], axis=0).astype(BF)
    qs = []
    for g in range(GROUP):
        hq = GROUP * h + g
        blk = qkv_ref[rows, (hq // 2) * 128:(hq // 2 + 1) * 128]
        qs.append(_to_half(blk, hq % 2, h))
    q4 = jnp.concatenate(qs, axis=0).astype(BF)
    s = lax.dot_general(q4, k2, _DIMS["NT"], preferred_element_type=F32) * (HEAD_DIM ** -0.5)
    shape = s.shape
    row = lax.broadcasted_iota(jnp.int32, shape, 0)
    qi = row & (BLOCK - 1)
    kj = lax.broadcasted_iota(jnp.int32, shape, 1)
    diff = qi + BLOCK - kj
    valid = (diff >= 0) & (diff < BLOCK) & ((kj >= BLOCK) | (n > 0))
    s = jnp.where(valid, s, NEG)
    row1 = lax.broadcasted_iota(jnp.int32, (shape[0], 1), 0)
    sink = jnp.zeros((shape[0], 1), F32)
    for g in range(GROUP):
        sink = jnp.where((row1 >= g * BLOCK) & (row1 < (g + 1) * BLOCK), sinks_ref[0, GROUP * h + g], sink)
    m = jnp.maximum(jnp.max(s, axis=-1, keepdims=True), sink)
    e = jnp.exp(s - m)
    es = jnp.exp(sink - m)
    inv = 1.0 / (jnp.sum(e, axis=-1, keepdims=True) + es)
    return e * inv, es * inv, q4, k2, v2, rows, prev


def _attn_fwd(proj, sinks):
    T = proj.shape[0]

    def body(qkv_ref, sinks_ref, o_ref):
        def blk(n, carry):
            outs = [None] * (N_Q_HEADS // 2)
            for h in range(N_KV_HEADS):
                p, _, _, _, v2, rows, _ = _attn_block(qkv_ref, sinks_ref, n, h)
                o = lax.dot_general(p.astype(BF), v2, _DIMS["NN"], preferred_element_type=F32)
                for g in range(GROUP):
                    hq = GROUP * h + g
                    piece = jnp.where(_lane_half((BLOCK, 128), h), o[g * BLOCK:(g + 1) * BLOCK], 0.0)
                    if hq % 2 != h:
                        piece = pltpu.roll(piece, HEAD_DIM, 1)
                    outs[hq // 2] = piece if outs[hq // 2] is None else outs[hq // 2] + piece
            for pb in range(N_Q_HEADS // 2):
                o_ref[rows, pb * 128:(pb + 1) * 128] = outs[pb].astype(BF)
            return carry

        lax.fori_loop(0, T // BLOCK, blk, 0)

    return pl.pallas_call(
        body, name="attn_fwd", grid=(1,),
        in_specs=[pl.BlockSpec((T, GLU_OFF), lambda i: (0, 0)), pl.BlockSpec(memory_space=pltpu.SMEM)],
        out_specs=pl.BlockSpec((T, ATTN_WIDTH), lambda i: (0, 0)),
        out_shape=_sds((T, ATTN_WIDTH), BF),
        compiler_params=_params(("arbitrary",)),
    )(proj, sinks)


def _attn_bwd(proj, d_o, sinks):
    T = proj.shape[0]

    def body(qkv_ref, do_ref, sinks_ref, dqkv_ref, dsink_ref, dk_acc, dv_acc):
        dsink_ref[...] = jnp.zeros_like(dsink_ref)
        dk_acc[...] = jnp.zeros_like(dk_acc)
        dv_acc[...] = jnp.zeros_like(dv_acc)

        def blk(n, carry):
            dqs = [None] * (N_Q_HEADS // 2)
            for h in range(N_KV_HEADS):
                p, psink, q4, k2, v2, rows, prev = _attn_block(qkv_ref, sinks_ref, n, h)
                dos = []
                for g in range(GROUP):
                    hq = GROUP * h + g
                    dos.append(_to_half(do_ref[rows, (hq // 2) * 128:(hq // 2 + 1) * 128].astype(F32), hq % 2, h))
                do4 = jnp.concatenate(dos, axis=0).astype(BF)
                dp = lax.dot_general(do4, v2, _DIMS["NT"], preferred_element_type=F32)
                delta = jnp.sum(p * dp, axis=-1, keepdims=True)
                ds = (p * (dp - delta) * (HEAD_DIM ** -0.5)).astype(BF)
                dsk = psink * delta
                for g in range(GROUP):
                    hq = GROUP * h + g
                    tot = -jnp.sum(dsk[g * BLOCK:(g + 1) * BLOCK], axis=0, keepdims=True)
                    lane = lax.broadcasted_iota(jnp.int32, (1, 128), 1)
                    dsink_ref[...] += jnp.where(lane == hq, tot, 0.0)
                dq = lax.dot_general(ds, k2, _DIMS["NN"], preferred_element_type=F32)
                dk = lax.dot_general(ds, q4, _DIMS["TN"], preferred_element_type=F32)
                dv = lax.dot_general(p.astype(BF), do4, _DIMS["TN"], preferred_element_type=F32)
                dk_acc[prev, :] += dk[:BLOCK]
                dk_acc[rows, :] += dk[BLOCK:]
                dv_acc[prev, :] += dv[:BLOCK]
                dv_acc[rows, :] += dv[BLOCK:]
                for g in range(GROUP):
                    hq = GROUP * h + g
                    piece = jnp.where(_lane_half((BLOCK, 128), h), dq[g * BLOCK:(g + 1) * BLOCK], 0.0)
                    if hq % 2 != h:
                        piece = pltpu.roll(piece, HEAD_DIM, 1)
                    dqs[hq // 2] = piece if dqs[hq // 2] is None else dqs[hq // 2] + piece
            for pb in range(N_Q_HEADS // 2):
                dqkv_ref[rows, pb * 128:(pb + 1) * 128] = dqs[pb].astype(BF)
            return carry

        lax.fori_loop(0, T // BLOCK, blk, 0)
        dqkv_ref[:, ATTN_WIDTH:ATTN_WIDTH + KV_WIDTH] = dk_acc[...].astype(BF)
        dqkv_ref[:, ATTN_WIDTH + KV_WIDTH:] = dv_acc[...].astype(BF)

    return pl.pallas_call(
        body, name="attn_bwd", grid=(1,),
        in_specs=[pl.BlockSpec((T, GLU_OFF), lambda i: (0, 0)), pl.BlockSpec((T, ATTN_WIDTH), lambda i: (0, 0)),
                  pl.BlockSpec(memory_space=pltpu.SMEM)],
        out_specs=[pl.BlockSpec((T, GLU_OFF), lambda i: (0, 0)), pl.BlockSpec((1, 128), lambda i: (0, 0))],
        out_shape=[_sds((T, GLU_OFF), BF), _sds((1, 128), F32)],
        scratch_shapes=[pltpu.VMEM((T, KV_WIDTH), F32), pltpu.VMEM((T, KV_WIDTH), F32)],
        compiler_params=_params(("arbitrary",)),
    )(proj, d_o, sinks)


CHUNK = 256
_GLU_SPECS = [pl.BlockSpec((SEQ, 256), functools.partial(lambda i, c: (0, c), c=GLU_OFF // 256 + c)) for c in range(4)]


def _glu_to_pad(a0, a1, b0, b1, zpad):
    C = CONV_CHANNELS
    zpad[0:CONV_PAD, :] = jnp.zeros((CONV_PAD, C), F32)
    zpad[CONV_PAD:, 0:256] = a0[...] * jax.nn.sigmoid(b0[...])
    zpad[CONV_PAD:, 256:C] = a1[...] * jax.nn.sigmoid(b1[...])


def _conv_chunk(zpad, w_ref, cb_ref, ci):
    base = ci * CHUNK + CONV_PAD - (CONV_WIDTH - 1)
    u = jnp.broadcast_to(cb_ref[...], (CHUNK, CONV_CHANNELS))
    for k in range(CONV_WIDTH):
        u = u + w_ref[k:k + 1, :] * zpad[base + k:base + k + CHUNK, :]
    return u


def _ln_parts(u):
    mu = jnp.mean(u, axis=-1, keepdims=True)
    xc = u - mu
    rstd = lax.rsqrt(jnp.mean(xc * xc, axis=-1, keepdims=True) + EPS)
    return xc * rstd, rstd


def _conv_fwd(proj, conv_w, conv_b, ln_g, ln_b):
    T, C = proj.shape[0], CONV_CHANNELS
    vec = pl.BlockSpec((1, C), lambda i: (0, 0))

    def body(a0, a1, b0, b1, w_ref, cb_ref, g_ref, be_ref, c_ref, zpad):
        _glu_to_pad(a0, a1, b0, b1, zpad)
        for ci in range(T // CHUNK):
            xh, _ = _ln_parts(_conv_chunk(zpad, w_ref, cb_ref, ci))
            ln = xh * g_ref[...] + be_ref[...]
            c_ref[ci * CHUNK:(ci + 1) * CHUNK, :] = (ln * jax.nn.sigmoid(ln)).astype(BF)

    return pl.pallas_call(
        body, name="conv_fwd", grid=(1,),
        in_specs=[*_GLU_SPECS, pl.BlockSpec((CONV_PAD, C), lambda i: (0, 0)), vec, vec, vec],
        out_specs=pl.BlockSpec((T, C), lambda i: (0, 0)),
        out_shape=_sds((T, C), BF),
        scratch_shapes=[pltpu.VMEM((T + CONV_PAD, C), F32)],
        compiler_params=_params(("arbitrary",)),
    )(proj, proj, proj, proj, conv_w, conv_b, ln_g, ln_b)


def _conv_bwd(proj, d_c, conv_w, conv_b, ln_g, ln_b):
    T, C = proj.shape[0], CONV_CHANNELS
    vec = pl.BlockSpec((1, C), lambda i: (0, 0))
    wspec = pl.BlockSpec((CONV_PAD, C), lambda i: (0, 0))

    def body(a0, a1, b0, b1, dc_ref, w_ref, cb_ref, g_ref, be_ref, dglu_ref, dw_ref, dcb_ref, dg_ref, dbe_ref,
             zpad, dupad):
        _glu_to_pad(a0, a1, b0, b1, zpad)
        dupad[T:, :] = jnp.zeros((CONV_PAD, C), F32)
        dw_ref[...] = jnp.zeros_like(dw_ref)
        dcb_ref[...] = jnp.zeros_like(dcb_ref)
        dg_ref[...] = jnp.zeros_like(dg_ref)
        dbe_ref[...] = jnp.zeros_like(dbe_ref)
        for ci in range(T // CHUNK):
            rows = slice(ci * CHUNK, (ci + 1) * CHUNK)
            xh, rstd = _ln_parts(_conv_chunk(zpad, w_ref, cb_ref, ci))
            ln = xh * g_ref[...] + be_ref[...]
            sg = jax.nn.sigmoid(ln)
            dln = dc_ref[rows, :].astype(F32) * (sg * (1.0 + ln * (1.0 - sg)))
            dg_ref[...] += jnp.sum(dln * xh, axis=0, keepdims=True)
            dbe_ref[...] += jnp.sum(dln, axis=0, keepdims=True)
            dxh = dln * g_ref[...]
            du = rstd * (dxh - jnp.mean(dxh, axis=-1, keepdims=True)
                         - xh * jnp.mean(dxh * xh, axis=-1, keepdims=True))
            dupad[rows, :] = du
            dcb_ref[...] += jnp.sum(du, axis=0, keepdims=True)
            base = ci * CHUNK + CONV_PAD - (CONV_WIDTH - 1)
            for k in range(CONV_WIDTH):
                dw_ref[k:k + 1, :] += jnp.sum(du * zpad[base + k:base + k + CHUNK, :], axis=0, keepdims=True)
        for ci in range(T // CHUNK):
            rows = slice(ci * CHUNK, (ci + 1) * CHUNK)
            dz = jnp.zeros((CHUNK, C), F32)
            for k in range(CONV_WIDTH):
                off = ci * CHUNK + (CONV_WIDTH - 1) - k
                dz = dz + w_ref[k:k + 1, :] * dupad[off:off + CHUNK, :]
            for half, (a, b) in enumerate(((a0, b0), (a1, b1))):
                sb = jax.nn.sigmoid(b[rows, :])
                dzh = dz[:, half * 256:(half + 1) * 256]
                dglu_ref[rows, half * 256:(half + 1) * 256] = (dzh * sb).astype(BF)
                dglu_ref[rows, C + half * 256:C + (half + 1) * 256] = (dzh * a[rows, :] * sb * (1.0 - sb)).astype(BF)

    return pl.pallas_call(
        body, name="conv_bwd", grid=(1,),
        in_specs=[*_GLU_SPECS, pl.BlockSpec((T, C), lambda i: (0, 0)), wspec, vec, vec, vec],
        out_specs=[pl.BlockSpec((T, 2 * C), lambda i: (0, 0)), wspec, vec, vec, vec],
        out_shape=[_sds((T, 2 * C), BF), _sds((CONV_PAD, C), F32), _sds((1, C), F32), _sds((1, C), F32),
                   _sds((1, C), F32)],
        scratch_shapes=[pltpu.VMEM((T + CONV_PAD, C), F32), pltpu.VMEM((T + CONV_PAD, C), F32)],
        compiler_params=_params(("arbitrary",)),
    )(proj, proj, proj, proj, d_c, conv_w, conv_b, ln_g, ln_b)


_GATE_BLK = GATE_OFF // 256


def _ffn_in_swiglu(h2, wf_t):
    T, D = h2.shape
    tm, tn = 512, D_FF // 2

    def body(a_ref, bg_ref, bu_ref, act_ref, g_ref, u_ref):
        a = a_ref[...]
        g = lax.dot_general(a, bg_ref[...], _DIMS["NT"], preferred_element_type=F32)
        u = lax.dot_general(a, bu_ref[...], _DIMS["NT"], preferred_element_type=F32)
        act_ref[...] = (g * jax.nn.sigmoid(g) * u).astype(BF)
        g_ref[...] = g.astype(BF)
        u_ref[...] = u.astype(BF)

    t = pl.BlockSpec((tm, tn), lambda j, i: (i, j))
    return pl.pallas_call(
        body, name="ffn_in_swiglu", grid=(D_FF // tn, T // tm),
        in_specs=[pl.BlockSpec((tm, D), lambda j, i: (i, 0)), pl.BlockSpec((tn, D), lambda j, i: (j, 0)),
                  pl.BlockSpec((tn, D), lambda j, i: (D_FF // tn + j, 0))],
        out_specs=[t, t, t], out_shape=[_sds((T, D_FF), BF)] * 3,
        compiler_params=_params(("arbitrary", "arbitrary")),
    )(h2, wf_t, wf_t)


def _local_step(x, target, small, wi_t, wap_t, wcp_t, w_out, wf_t, w_down, conv_w):
    T, D = x.shape
    tm = 1024

    h, r1 = _rms_fwd("rms_mix", x, small["g_mix_norm"])

    def ep_add(acc, ex, outs, ids, scr):
        outs[0][...] = acc + ex[0][...]

    tn_in = IN_WIDTH // 3
    proj, = _matmul("proj_in", [h], wi_t, "NT", m=T, n=IN_WIDTH, tm=tm, tn=tn_in, epilogue=ep_add,
                    extra=[(small["b_in"], _row(tn_in))], outs=[(_sds((T, IN_WIDTH), F32), _tile(tm, tn_in))])
    o = _attn_fwd(proj, small["sinks"])
    c = _conv_fwd(proj, conv_w, small["conv_b"], small["ln_g"], small["ln_b"])
    ya, = _matmul("attn_proj", [o], wap_t, "NT", m=T, n=D, tm=tm, tn=D, epilogue=_store(F32),
                  outs=[(_sds((T, D), F32), _tile(tm, D))])

    tg = 256
    gate_specs = [pl.BlockSpec((tm, tg), lambda j, i, k: (i, _GATE_BLK + j)),
                  pl.BlockSpec((tm, tg), lambda j, i, k: (i, _GATE_BLK + D // tg + j))]

    def ep_merge(acc, ex, outs, ids, scr):
        yc = acc + ex[0][...]
        outs[0][...] = yc
        outs[1][...] = (jax.nn.sigmoid(ex[2][...]) * ex[1][...] + jax.nn.sigmoid(ex[3][...]) * yc).astype(BF)

    yc, merged = _matmul(
        "conv_proj_merge", [c], wcp_t, "NT", m=T, n=D, tm=tm, tn=tg, epilogue=ep_merge,
        extra=[(small["b_conv_proj"], _row(tg)), (ya, _tile(tm, tg)), (proj, gate_specs[0]), (proj, gate_specs[1])],
        outs=[(_sds((T, D), F32), _tile(tm, tg)), (_sds((T, D), BF), _tile(tm, tg))])
    x2, = _matmul("out_proj", [merged], w_out, "NN", m=T, n=D, tm=tm, tn=D, epilogue=ep_add,
                  extra=[(x, _tile(tm, D))], outs=[(_sds((T, D), F32), _tile(tm, D))])
    h2, r2 = _rms_fwd("rms_ffn", x2, small["g_ffn_norm"])
    act, gate, up = _ffn_in_swiglu(h2, wf_t)
    x3, = _matmul("ffn_down", [act], w_down, "NN", m=T, n=D, tm=512, tn=D, epilogue=ep_add,
                  extra=[(x2, _tile(512, D))], outs=[(_sds((T, D), F32), _tile(512, D))])
    dx3, dx3_b, dg_final, loss = _final(x3, small["g_final"], target)

    tn_ff = D_FF // 2

    def ep_swiglu_bwd(acc, ex, outs, ids, scr):
        g, u = ex[0][...].astype(F32), ex[1][...].astype(F32)
        sg = jax.nn.sigmoid(g)
        outs[0][...] = (acc * u * sg * (1.0 + g * (1.0 - sg))).astype(BF)
        outs[1][...] = (acc * g * sg).astype(BF)

    dgate, dup = _matmul(
        "ffn_down_bwd", [dx3_b], w_down, "NT", m=T, n=D_FF, tm=512, tn=tn_ff, epilogue=ep_swiglu_bwd,
        extra=[(gate, _tile(512, tn_ff)), (up, _tile(512, tn_ff))],
        outs=[(_sds((T, D_FF), BF), _tile(512, tn_ff)), (_sds((T, D_FF), BF), _tile(512, tn_ff))])

    def dw(name, a, b, rows, cols, row_off=0, alias=None, total_rows=None, colsum=False):
        total_rows = rows if total_rows is None else total_rows
        tmw = rows if rows <= 1024 else D_FF // 2
        by_dma = row_off % tmw != 0

        def ep(acc, ex, outs, ids, scr):
            if by_dma:
                scr[0][...] = acc.astype(BF)
                pltpu.sync_copy(scr[0], outs[0].at[pl.ds(pl.multiple_of(row_off + ids[1] * tmw, 256), tmw)])
            else:
                outs[0][...] = acc.astype(BF)
            if colsum:
                outs[1][...] = jnp.sum(ex[0][...].astype(F32), axis=0, keepdims=True)

        blk = row_off // tmw
        spec = pl.BlockSpec(memory_space=pl.ANY) if by_dma else pl.BlockSpec((tmw, cols), lambda j, i, k: (blk + i, j))
        outs = [(_sds((total_rows, cols), BF), spec)]
        extra = []
        if colsum:
            extra = [(a, pl.BlockSpec((T, tmw), lambda j, i, k: (0, i)))]
            outs.append((_sds((1, rows), F32), pl.BlockSpec((1, tmw), lambda j, i, k: (0, i))))
        res = _matmul(name, [a], b, "TN", m=rows, n=cols, tm=tmw, tn=cols, epilogue=ep, extra=extra, outs=outs,
                      alias=None if alias is None else (alias, 0),
                      scratch=[pltpu.VMEM((tmw, cols), BF)] if by_dma else [])
        return res if colsum else res[0]

    gw_down = dw("ffn_down_dw", act, dx3_b, D_FF, D)

    def ep_rms_bwd(acc, ex, outs, ids, scr):
        dx, dg = _rms_bwd(acc, ex[0][...], ex[1][...], ex[2][...])
        dx = ex[3][...] + dx
        outs[0][...] = dx
        outs[1][...] = dx.astype(BF)
        _accumulate_rows(outs[2], dg, ids[1] == 0)

    def rms_bwd_io(tm_, xin, r, g, dres):
        return dict(
            extra=[(xin, _tile(tm_, D)), (r, pl.BlockSpec((tm_, 1), lambda j, i, k: (i, 0))), (g, _row(D)),
                   (dres, _tile(tm_, D))],
            outs=[(_sds((T, D), F32), _tile(tm_, D)), (_sds((T, D), BF), _tile(tm_, D)), (_sds((1, D), F32), _row(D))])

    dx2, dx2_b, dg_ffn = _matmul("ffn_in_bwd", [dgate, dup], wf_t, "NN", m=T, n=D, tm=512, tn=D, tk=D_FF,
                                 epilogue=ep_rms_bwd, **rms_bwd_io(512, x2, r2, small["g_ffn_norm"], dx3))
    gwf_t = dw("ffn_in_dw_gate", dgate, h2, D_FF, D, total_rows=2 * D_FF)
    gwf_t = dw("ffn_in_dw_up", dup, h2, D_FF, D, row_off=D_FF, alias=gwf_t, total_rows=2 * D_FF)

    def ep_merge_bwd(acc, ex, outs, ids, scr):
        s0 = jax.nn.sigmoid(ex[2][...])
        s1 = jax.nn.sigmoid(ex[3][...])
        outs[0][...] = (acc * s0).astype(BF)
        outs[1][...] = (acc * s1).astype(BF)
        outs[2][...] = (acc * ex[0][...] * s0 * (1.0 - s0)).astype(BF)
        outs[3][...] = (acc * ex[1][...] * s1 * (1.0 - s1)).astype(BF)

    dya, dyc, dg0, dg1 = _matmul(
        "out_proj_bwd_merge", [dx2_b], w_out, "NT", m=T, n=D, tm=tm, tn=tg, epilogue=ep_merge_bwd,
        extra=[(ya, _tile(tm, tg)), (yc, _tile(tm, tg)), (proj, gate_specs[0]), (proj, gate_specs[1])],
        outs=[(_sds((T, D), BF), _tile(tm, tg))] * 4)
    gw_out = dw("out_proj_dw", merged, dx2_b, D, D)
    d_o, = _matmul("attn_proj_bwd", [dya], wap_t, "NN", m=T, n=ATTN_WIDTH, tm=tm, tn=ATTN_WIDTH,
                   epilogue=_store(BF), outs=[(_sds((T, ATTN_WIDTH), BF), _tile(tm, ATTN_WIDTH))])
    d_c, = _matmul("conv_proj_bwd", [dyc], wcp_t, "NN", m=T, n=CONV_CHANNELS, tm=tm, tn=CONV_CHANNELS,
                   epilogue=_store(BF), outs=[(_sds((T, CONV_CHANNELS), BF), _tile(tm, CONV_CHANNELS))])
    gwap_t = dw("attn_proj_dw", dya, o, D, ATTN_WIDTH)
    gwcp_t, db_cp = dw("conv_proj_dw", dyc, c, D, CONV_CHANNELS, colsum=True)
    dglu, dcw, dcb, dlng, dlnb = _conv_bwd(proj, d_c, conv_w, small["conv_b"], small["ln_g"], small["ln_b"])
    dqkv, dsinks = _attn_bwd(proj, d_o, small["sinks"])

    segs = [dqkv, dglu, dg0, dg1]
    gwi_t, off, db_in = None, 0, []
    for s, seg in enumerate(segs):
        gwi_t, db = dw(f"proj_in_dw{s}", seg, h, seg.shape[1], D, row_off=off, alias=gwi_t, total_rows=IN_WIDTH,
                       colsum=True)
        db_in.append(db)
        off += seg.shape[1]
    dx, _, dg_mix = _matmul("proj_in_bwd", segs, wi_t, "NN", m=T, n=D, tm=512, tn=D, epilogue=ep_rms_bwd,
                            **rms_bwd_io(512, x, r1, small["g_mix_norm"], dx2))

    big = dict(w_in=gwi_t, w_attn_proj=gwap_t, w_conv_proj=gwcp_t, w_out=gw_out, w_ffn_in=gwf_t, w_ffn_down=gw_down)
    parts = dict(g_mix_norm=dg_mix, b_in=db_in, sinks=dsinks, conv_w=dcw, conv_b=dcb, ln_g=dlng, ln_b=dlnb,
                 b_conv_proj=db_cp, g_ffn_norm=dg_ffn, g_final=dg_final, loss=loss)
    return dx, big, parts


ANY = pl.BlockSpec(memory_space=pl.ANY)


def _place():
    x, y, c = lax.axis_index("x"), lax.axis_index("y"), lax.axis_index("c")
    return x, y, c, [(1 - x, y), (x, 1 - y), (1 - x, 1 - y)]


def _gather_blocks(x_refs, out_refs, rows_per, send_sems, recv_sems, local_sems):
    x, y, c, chips = _place()
    me, sibling = (x, y, c), (x, y, 1 - c)

    def rows(a, px, py, pc):
        return out_refs[a].at[pl.ds((4 * px + 2 * py + pc) * rows_per[a], rows_per[a])]

    def copy(a, k, block, to, src=None):
        return pltpu.make_async_remote_copy(
            src_ref=rows(a, *block) if src is None else src, dst_ref=rows(a, *block),
            send_sem=send_sems.at[7 * a + k], recv_sem=recv_sems.at[7 * a + k], device_id=to, device_id_type=MESH)

    n = len(x_refs)
    local, sent = [], []
    for a in range(n):
        mine = pltpu.make_async_copy(x_refs[a], rows(a, *me), local_sems.at[a])
        mine.start()
        local.append(mine)
        first = [copy(a, 0, me, sibling, src=x_refs[a])]
        first += [copy(a, 1 + j, me, (*chip, c), src=x_refs[a]) for j, chip in enumerate(chips)]
        for cp in first:
            cp.start()
        sent += first
    for a in range(n):
        for j, chip in enumerate(chips):
            copy(a, 1 + j, (*chip, c), me).wait_recv()
            passed = copy(a, 4 + j, (*chip, c), sibling)
            passed.start()
            sent.append(passed)
    for a in range(n):
        copy(a, 0, sibling, me).wait_recv()
        for j, chip in enumerate(chips):
            copy(a, 4 + j, (*chip, 1 - c), me).wait_recv()
    for cp in sent:
        cp.wait_send()
    for cp in local:
        cp.wait()


def _all_gather(shards):
    n = len(shards)
    rows_per = [s.shape[0] for s in shards]

    def body(*refs):
        _gather_blocks(refs[:n], refs[n:2 * n], rows_per, *refs[2 * n:])

    return pl.pallas_call(
        body, name="weights_all_gather",
        in_specs=[ANY] * n, out_specs=[ANY] * n,
        out_shape=[_sds((N_DEV * s.shape[0],) + s.shape[1:], s.dtype) for s in shards],
        scratch_shapes=[pltpu.SemaphoreType.DMA((7 * n,)), pltpu.SemaphoreType.DMA((7 * n,)),
                        pltpu.SemaphoreType.DMA((n,))],
    )(*shards)


def _swap_halves(grads):
    n = len(grads)

    def body(*refs):
        g_refs, out_refs, send_sems, recv_sems = refs[:n], refs[n:2 * n], refs[2 * n], refs[2 * n + 1]
        x, y, c, _ = _place()
        copies = []
        for a in range(n):
            for p in range(4):
                cp = pltpu.make_async_remote_copy(
                    src_ref=g_refs[a].at[2 * p + 1 - c], dst_ref=out_refs[a].at[p],
                    send_sem=send_sems.at[4 * a + p], recv_sem=recv_sems.at[4 * a + p],
                    device_id=(x, y, 1 - c), device_id_type=MESH)
                cp.start()
                copies.append(cp)
        for cp in copies:
            cp.wait()

    return pl.pallas_call(
        body, name="grad_swap_halves",
        in_specs=[ANY] * n, out_specs=[ANY] * n,
        out_shape=[_sds((4,) + g.shape[1:], g.dtype) for g in grads],
        scratch_shapes=[pltpu.SemaphoreType.DMA((4 * n,)), pltpu.SemaphoreType.DMA((4 * n,))],
    )(*grads)


def _chip_sum(name, g, got, c):
    _, rows, cols = g.shape

    def body(c_ref, g_ref, got_ref, o_ref):
        o_ref[...] = (g_ref[...].astype(F32) + got_ref[...].astype(F32)).astype(BF)

    return pl.pallas_call(
        body, name=name,
        grid_spec=pltpu.PrefetchScalarGridSpec(
            num_scalar_prefetch=1, grid=(4,),
            in_specs=[pl.BlockSpec((1, rows, cols), lambda p, c_ref: (2 * p + c_ref[0], 0, 0)),
                      pl.BlockSpec((1, rows, cols), lambda p, c_ref: (p, 0, 0))],
            out_specs=pl.BlockSpec((1, rows, cols), lambda p, c_ref: (p, 0, 0))),
        out_shape=_sds((4, rows, cols), BF),
        compiler_params=_params(("arbitrary",)),
    )(c, g, got)


def _send_chip_sums(sums):
    n = len(sums)

    def body(*refs):
        s_refs, out_refs, send_sems, recv_sems = refs[:n], refs[n:2 * n], refs[2 * n], refs[2 * n + 1]
        x, y, c, chips = _place()
        copies = []
        for a in range(n):
            for k, (px, py) in enumerate(chips):
                cp = pltpu.make_async_remote_copy(
                    src_ref=s_refs[a].at[2 * px + py], dst_ref=out_refs[a].at[k],
                    send_sem=send_sems.at[3 * a + k], recv_sem=recv_sems.at[3 * a + k],
                    device_id=(px, py, c), device_id_type=MESH)
                cp.start()
                copies.append(cp)
        for cp in copies:
            cp.wait()

    return pl.pallas_call(
        body, name="grad_send_chip_sums",
        in_specs=[ANY] * n, out_specs=[ANY] * n,
        out_shape=[_sds((3,) + s.shape[1:], s.dtype) for s in sums],
        scratch_shapes=[pltpu.SemaphoreType.DMA((3 * n,)), pltpu.SemaphoreType.DMA((3 * n,))],
    )(*sums)


def _grad_total(name, g, got, got3, ids):
    _, rows, cols = g.shape

    def body(ids_ref, g_ref, got_ref, got3_ref, o_ref):
        tot = g_ref[0].astype(F32) + got_ref[0].astype(F32)
        for k in range(3):
            tot = tot + got3_ref[k].astype(F32)
        o_ref[...] = tot

    return pl.pallas_call(
        body, name=name,
        grid_spec=pltpu.PrefetchScalarGridSpec(
            num_scalar_prefetch=1, grid=(1,),
            in_specs=[pl.BlockSpec((1, rows, cols), lambda i, ids_ref: (ids_ref[0], 0, 0)),
                      pl.BlockSpec((1, rows, cols), lambda i, ids_ref: (ids_ref[1], 0, 0)),
                      pl.BlockSpec((3, rows, cols), lambda i, ids_ref: (0, 0, 0))],
            out_specs=pl.BlockSpec((rows, cols), lambda i, ids_ref: (0, 0))),
        out_shape=_sds((rows, cols), F32),
        compiler_params=_params(("arbitrary",)),
    )(ids, g, got, got3)


def _adam_math(w, g, m, v):
    m = ADAM_B1 * m + (1.0 - ADAM_B1) * g
    v = ADAM_B2 * v + (1.0 - ADAM_B2) * (g * g)
    m_hat = m / (1.0 - ADAM_B1 ** ADAM_STEP)
    v_hat = v / (1.0 - ADAM_B2 ** ADAM_STEP)
    delta = -ADAM_LR * (m_hat / (jnp.sqrt(v_hat) + ADAM_EPS) + ADAM_WD * w)
    return delta, m, v


def _adamw(name, w, g, m, v):
    rows, cols = w.shape
    tr = 256 if rows % 256 == 0 else rows

    def body(w_ref, g_ref, m_ref, v_ref, d_ref, nm_ref, nv_ref):
        d_ref[...], nm_ref[...], nv_ref[...] = _adam_math(w_ref[...], g_ref[...], m_ref[...], v_ref[...])

    t = pl.BlockSpec((tr, cols), lambda i: (i, 0))
    return pl.pallas_call(
        body, name=name, grid=(rows // tr,), in_specs=[t] * 4, out_specs=[t] * 3,
        out_shape=[_sds((rows, cols), F32)] * 3, compiler_params=_params(("arbitrary",)),
    )(w, g, m, v)


SMALL_NAMES = ["g_mix_norm", "b_in", "sinks", "conv_b", "ln_g", "ln_b", "b_conv_proj", "g_ffn_norm", "g_final"]
_PACK_ROWS = 32


def _small_all_reduce(parts):
    C = CONV_CHANNELS
    part_list = [parts["g_mix_norm"], *parts["b_in"], parts["sinks"], parts["conv_b"], parts["ln_g"], parts["ln_b"],
                 parts["b_conv_proj"], parts["g_ffn_norm"], parts["g_final"], parts["loss"], parts["conv_w"]]
    n_part = len(part_list)

    def body(*refs):
        (p_mix, p_b0, p_b1, p_b2, p_b3, p_sink, p_cb, p_lg, p_lb, p_bcp, p_ffn, p_fin, p_loss, p_cw) = refs[:n_part]
        tot_ref, pack, gathered, send_sems, recv_sems, local_sems = refs[n_part:]
        pack[...] = jnp.zeros_like(pack)
        pack[0:1, :] = p_mix[...]
        pack[1:2, 0:GLU_OFF] = p_b0[...]
        pack[2:3, :] = p_b1[...]
        pack[3:4, :] = p_b2[...]
        pack[4:5, :] = p_b3[...]
        pack[5:6, 0:128] = p_sink[...]
        pack[6:7, 0:C] = p_cb[...]
        pack[6:7, C:2 * C] = p_lg[...]
        pack[7:8, 0:C] = p_lb[...]
        pack[8:9, :] = p_bcp[...]
        pack[9:10, :] = p_ffn[...]
        pack[10:11, :] = p_fin[...]
        pack[11:12, 0:128] = jnp.broadcast_to(p_loss[...], (1, 128))
        pack[12:28, 0:C] = p_cw[0:16, :]
        pack[12:28, C:2 * C] = p_cw[16:32, :]
        _gather_blocks([pack], [gathered], [_PACK_ROWS], send_sems, recv_sems, local_sems)
        tot = gathered[0:_PACK_ROWS, :]
        for d in range(1, N_DEV):
            tot = tot + gathered[d * _PACK_ROWS:(d + 1) * _PACK_ROWS, :]
        tot_ref[...] = tot

    vm = pl.BlockSpec(memory_space=pltpu.VMEM)
    return pl.pallas_call(
        body, name="small_all_reduce",
        in_specs=[vm] * n_part, out_specs=vm, out_shape=_sds((_PACK_ROWS, D_MODEL), F32),
        scratch_shapes=[pltpu.VMEM((_PACK_ROWS, D_MODEL), F32), pltpu.VMEM((N_DEV * _PACK_ROWS, D_MODEL), F32),
                        pltpu.SemaphoreType.DMA((7,)), pltpu.SemaphoreType.DMA((7,)), pltpu.SemaphoreType.DMA((1,))],
        compiler_params=pltpu.CompilerParams(vmem_limit_bytes=VMEM_LIMIT_BYTES),
    )(*part_list)


def _small_adamw(tot, small_w, small_m, small_v):
    C = CONV_CHANNELS
    names = SMALL_NAMES
    widths = [small_w[k].shape[1] for k in names]
    n_small = len(names)

    def body(*refs):
        tot_ref = refs[0]
        w_refs = refs[1:1 + n_small]
        m_refs = refs[1 + n_small:1 + 2 * n_small]
        v_refs = refs[1 + 2 * n_small:1 + 3 * n_small]
        o = 1 + 3 * n_small
        loss_ref, cw_ref = refs[o], refs[o + 1]
        out_refs = refs[o + 2:o + 2 + 4 * n_small]
        tot = tot_ref[...]
        loss_ref[...] = tot[11:12, 0:1]
        cw_ref[0:16, :] = tot[12:28, 0:C]
        cw_ref[16:32, :] = tot[12:28, C:2 * C]
        grads = dict(
            g_mix_norm=tot[0:1, :],
            b_in=jnp.concatenate([tot[1:2, 0:GLU_OFF], tot[2:3, :], tot[3:4, :], tot[4:5, :]], axis=1),
            sinks=tot[5:6, 0:N_Q_HEADS], conv_b=tot[6:7, 0:C], ln_g=tot[6:7, C:2 * C], ln_b=tot[7:8, 0:C],
            b_conv_proj=tot[8:9, :], g_ffn_norm=tot[9:10, :], g_final=tot[10:11, :])
        for s, k in enumerate(names):
            g = grads[k]
            d, nm, nv = _adam_math(w_refs[s][...], g, m_refs[s][...], v_refs[s][...])
            out_refs[4 * s][...] = g
            out_refs[4 * s + 1][...] = d
            out_refs[4 * s + 2][...] = nm
            out_refs[4 * s + 3][...] = nv

    vm = pl.BlockSpec(memory_space=pltpu.VMEM)
    args = [tot, *[small_w[k] for k in names], *[small_m[k] for k in names], *[small_v[k] for k in names]]
    out_shape = [_sds((1, 1), F32), _sds((CONV_PAD, C), F32)]
    for wd in widths:
        out_shape += [_sds((1, wd), F32)] * 4
    res = pl.pallas_call(
        body, name="small_adamw",
        in_specs=[vm] * len(args), out_specs=[vm] * len(out_shape), out_shape=out_shape,
        compiler_params=pltpu.CompilerParams(vmem_limit_bytes=VMEM_LIMIT_BYTES),
    )(*args)
    return res[0], res[1], {k: res[2 + 4 * s:6 + 4 * s] for s, k in enumerate(names)}


BIG = dict(w_in=True, w_attn_proj=True, w_conv_proj=True, w_out=False, w_ffn_in=True, w_ffn_down=False)
WEIGHT_NAMES = ["g_mix_norm", "w_in", "b_in", "sinks", "conv_w", "conv_b", "ln_g", "ln_b", "w_attn_proj",
                "w_conv_proj", "b_conv_proj", "w_out", "g_ffn_norm", "w_ffn_in", "w_ffn_down", "g_final"]


def kernel(x, g_mix_norm, w_in, b_in, sinks, conv_w, conv_b, ln_g, ln_b, w_attn_proj, w_conv_proj, b_conv_proj, w_out, g_ffn_norm, w_ffn_in, w_ffn_down, g_final, loss_target, m_g_mix_norm, m_w_in, m_b_in, m_sinks, m_conv_w, m_conv_b, m_ln_g, m_ln_b, m_w_attn_proj, m_w_conv_proj, m_b_conv_proj, m_w_out, m_g_ffn_norm, m_w_ffn_in, m_w_ffn_down, m_g_final, v_g_mix_norm, v_w_in, v_b_in, v_sinks, v_conv_w, v_conv_b, v_ln_g, v_ln_b, v_w_attn_proj, v_w_conv_proj, v_b_conv_proj, v_w_out, v_g_ffn_norm, v_w_ffn_in, v_w_ffn_down, v_g_final):
    w = dict(g_mix_norm=g_mix_norm, w_in=w_in, b_in=b_in, sinks=sinks, conv_w=conv_w, conv_b=conv_b, ln_g=ln_g,
             ln_b=ln_b, w_attn_proj=w_attn_proj, w_conv_proj=w_conv_proj, b_conv_proj=b_conv_proj, w_out=w_out,
             g_ffn_norm=g_ffn_norm, w_ffn_in=w_ffn_in, w_ffn_down=w_ffn_down, g_final=g_final)
    m = dict(g_mix_norm=m_g_mix_norm, w_in=m_w_in, b_in=m_b_in, sinks=m_sinks, conv_w=m_conv_w, conv_b=m_conv_b,
             ln_g=m_ln_g, ln_b=m_ln_b, w_attn_proj=m_w_attn_proj, w_conv_proj=m_w_conv_proj,
             b_conv_proj=m_b_conv_proj, w_out=m_w_out, g_ffn_norm=m_g_ffn_norm, w_ffn_in=m_w_ffn_in,
             w_ffn_down=m_w_ffn_down, g_final=m_g_final)
    v = dict(g_mix_norm=v_g_mix_norm, w_in=v_w_in, b_in=v_b_in, sinks=v_sinks, conv_w=v_conv_w, conv_b=v_conv_b,
             ln_g=v_ln_g, ln_b=v_ln_b, w_attn_proj=v_w_attn_proj, w_conv_proj=v_w_conv_proj,
             b_conv_proj=v_b_conv_proj, w_out=v_w_out, g_ffn_norm=v_g_ffn_norm, w_ffn_in=v_w_ffn_in,
             w_ffn_down=v_w_ffn_down, g_final=v_g_final)
    ax, ay, ac = lax.axis_index("x"), lax.axis_index("y"), lax.axis_index("c")
    me = 4 * ax + 2 * ay + ac
    chip = 2 * ax + ay

    shards = [(w[k][0].T if tr else w[k][0]).astype(BF) for k, tr in BIG.items()]
    cw_shard = jnp.pad(conv_w[0].T, ((0, 0), (0, 1))).reshape(16, 128)
    *full, cw_full = _all_gather([*shards, cw_shard])
    conv_full = cw_full.reshape(CONV_CHANNELS, CONV_PAD).T
    wi_t, wap_t, wcp_t, wout_f, wf_t, wd_f = full

    as_row = lambda a: a.reshape(1, -1)
    small_w = {k: as_row(w[k]) for k in SMALL_NAMES}
    small_m = {k: as_row(m[k]) for k in SMALL_NAMES}
    small_v = {k: as_row(v[k]) for k in SMALL_NAMES}
    dx, big, parts = _local_step(x[0], loss_target[0], small_w, wi_t, wap_t, wcp_t, wout_f, wf_t, wd_f, conv_full)

    names = list(BIG)
    slots = [big[k].reshape(N_DEV, big[k].shape[0] // N_DEV, big[k].shape[1]) for k in names]
    got = _swap_halves(slots)
    c1 = ac.reshape(1).astype(jnp.int32)
    sums = [_chip_sum(f"chip_sum_{k}", g, r, c1) for k, g, r in zip(names, slots, got)]
    got3 = _send_chip_sums(sums)
    ids = jnp.stack([me, chip]).astype(jnp.int32)
    grads, delta, new_m, new_v = {}, {}, {}, {}
    for k, g, r, r3 in zip(names, slots, got, got3):
        tot = _grad_total(f"grad_total_{k}", g, r, r3, ids)
        tot = tot.T if BIG[k] else tot
        d, nm, nv = _adamw(f"adamw_{k}", w[k][0], tot, m[k][0], v[k][0])
        grads[k], delta[k], new_m[k], new_v[k] = tot[None], d[None], nm[None], nv[None]

    loss, cw_grad, small_out = _small_adamw(_small_all_reduce(parts), small_w, small_m, small_v)
    for k in SMALL_NAMES:
        g, d, nm, nv = (a.reshape(w[k].shape) for a in small_out[k])
        grads[k], delta[k], new_m[k], new_v[k] = g, d, nm, nv
    cw_mine = lax.dynamic_slice(cw_grad, (0, me * 64), (CONV_WIDTH, 64))
    d, nm, nv = _adamw("adamw_conv_w", conv_w[0], cw_mine, m_conv_w[0], v_conv_w[0])
    grads["conv_w"], delta["conv_w"], new_m["conv_w"], new_v["conv_w"] = cw_mine[None], d[None], nm[None], nv[None]

    return (loss.reshape(()), dx[None], *[grads[k] for k in WEIGHT_NAMES], *[delta[k] for k in WEIGHT_NAMES],
            *[new_m[k] for k in WEIGHT_NAMES], *[new_v[k] for k in WEIGHT_NAMES])
```

```python
import functools

import jax
import jax.numpy as jnp
from jax import lax
from jax.experimental import pallas as pl
from jax.experimental.pallas import tpu as pltpu

F32 = jnp.float32
BF = jnp.bfloat16

SEQ = 2048
D_MODEL = 1024
HEAD_DIM = 64
N_Q_HEADS = 8
N_KV_HEADS = 2
GROUP = N_Q_HEADS // N_KV_HEADS
BLOCK = 128
ATTN_WIDTH = 512
KV_WIDTH = 128
CONV_CHANNELS = 512
CONV_WIDTH = 31
CONV_PAD = 32
GLU_OFF = 768
GATE_OFF = 1792
IN_WIDTH = 3840
D_FF = 2816
EPS = 1e-5
NEG = -1e30
N_DEV = 8

ADAM_LR = 0.001
ADAM_B1 = 0.9
ADAM_B2 = 0.999
ADAM_EPS = 1e-08
ADAM_WD = 0.01
ADAM_STEP = 10

VMEM_LIMIT_BYTES = 56 * 1024 * 1024
MESH = pl.DeviceIdType.MESH
ANY = pl.BlockSpec(memory_space=pl.ANY)

_DIMS = {"NN": (((1,), (0,)), ((), ())), "NT": (((1,), (1,)), ((), ())), "TN": (((0,), (0,)), ((), ()))}


def _params(sem):
    return pltpu.CompilerParams(dimension_semantics=sem, vmem_limit_bytes=VMEM_LIMIT_BYTES)


class _Carry:
    def __init__(self, arrays, out_shapes, sems, start, finish):
        self.arrays, self.out_shapes, self.sems, self.start, self.finish = arrays, out_shapes, sems, start, finish


def _carry_io(carry):
    if carry is None:
        return [], [], []
    return list(carry.arrays), list(carry.out_shapes), list(carry.sems)


def _matmul(name, a_list, b, mode, *, m, n, tm, tn, tk=None, epilogue, extra=(), outs, b_off=(0, 0), alias=None,
            scratch=(), carry=None):
    seg_k = [a.shape[0] if mode == "TN" else a.shape[1] for a in a_list]
    whole = tk is None
    seg_nk = [1] * len(a_list) if whole else [ks // tk for ks in seg_k]
    nk = 1 if whole else sum(seg_nk)
    starts = [sum(seg_nk[:s]) for s in range(len(seg_nk))]
    k_starts = [sum(seg_k[:s]) for s in range(len(seg_k))]
    k_tot = sum(seg_k)
    n_a, n_extra, n_out = len(a_list), len(extra), len(outs)

    a_specs = []
    for st, ns, ks in zip(starts, seg_nk, seg_k):
        if mode == "TN":
            a_specs.append(pl.BlockSpec((ks if whole else tk, tm), lambda j, i, k: (k, i)))
        elif whole:
            a_specs.append(pl.BlockSpec((tm, ks), lambda j, i, k: (i, 0)))
        else:
            a_specs.append(pl.BlockSpec((tm, tk), functools.partial(
                lambda j, i, k, st, ns: (i, jnp.clip(k - st, 0, ns - 1)), st=st, ns=ns)))
    bk = k_tot if whole else tk
    if mode == "NT":
        b_spec = pl.BlockSpec((tn, bk), lambda j, i, k: (b_off[0] + j, b_off[1] + k))
    else:
        b_spec = pl.BlockSpec((bk, tn), lambda j, i, k: (b_off[0] + k, b_off[1] + j))
    n_alias = 0 if alias is None else 1
    c_in, c_out, c_sems = _carry_io(carry)
    n_acc = 0 if whole else 1
    nj, ni = n // tn, m // tm

    def body(*refs):
        pos = [n_a, 1, n_alias, n_extra, len(c_in), n_out, len(c_out), n_acc, len(scratch), len(c_sems)]
        cuts = [sum(pos[:q]) for q in range(len(pos) + 1)]
        a_refs, (b_ref,), _, ex, ci_refs, out_refs, co_refs, acc_refs, scr, cs_refs = (
            refs[cuts[q]:cuts[q + 1]] for q in range(len(pos)))
        j, i, k = pl.program_id(0), pl.program_id(1), pl.program_id(2)
        ids = (j, i)
        if carry is not None:
            @pl.when((j == 0) & (i == 0) & (k == 0))
            def _():
                carry.start(ci_refs, co_refs, cs_refs)

        def dot(a_ref, bv):
            return lax.dot_general(a_ref[...].astype(BF), bv.astype(BF), _DIMS[mode], preferred_element_type=F32)

        if whole:
            tot = None
            for a_ref, k0, ks in zip(a_refs, k_starts, seg_k):
                if n_a == 1:
                    bv = b_ref[...]
                else:
                    bv = b_ref[:, k0:k0 + ks] if mode == "NT" else b_ref[k0:k0 + ks, :]
                part = dot(a_ref, bv)
                tot = part if tot is None else tot + part
            epilogue(tot, ex, out_refs, ids, scr)
        else:
            acc, = acc_refs

            @pl.when(k == 0)
            def _():
                acc[...] = jnp.zeros_like(acc)

            for a_ref, st, ns in zip(a_refs, starts, seg_nk):
                if n_a == 1:
                    acc[...] += dot(a_ref, b_ref[...])
                else:
                    @pl.when((k >= st) & (k < st + ns))
                    def _(a_ref=a_ref):
                        acc[...] += dot(a_ref, b_ref[...])

            @pl.when(k == nk - 1)
            def _():
                epilogue(acc[...], ex, out_refs, ids, scr)

        if carry is not None:
            @pl.when((j == nj - 1) & (i == ni - 1) & (k == nk - 1))
            def _():
                carry.finish(ci_refs, co_refs, cs_refs)

    in_specs = [*a_specs, b_spec]
    args = [*a_list, b]
    io_alias = {}
    if alias is not None:
        in_specs.append(pl.BlockSpec(memory_space=pl.ANY))
        args.append(alias[0])
        io_alias = {n_a + 1: alias[1]}
    in_specs += [s for _, s in extra] + [pl.BlockSpec(memory_space=pl.ANY)] * len(c_in)
    args += [x for x, _ in extra] + c_in
    res = pl.pallas_call(
        body, name=name, grid=(nj, ni, nk), in_specs=in_specs,
        out_specs=[s for _, s in outs] + [pl.BlockSpec(memory_space=pl.ANY)] * len(c_out),
        out_shape=[o for o, _ in outs] + c_out,
        scratch_shapes=[*([] if whole else [pltpu.VMEM((tm, tn), F32)]), *scratch, *c_sems],
        input_output_aliases=io_alias,
        compiler_params=_params(("arbitrary", "arbitrary", "arbitrary")),
    )(*args)
    return res if carry is None else (res[:n_out], res[n_out:])


def _tile(tm, tn):
    return pl.BlockSpec((tm, tn), lambda j, i, k: (i, j))


def _row(tn):
    return pl.BlockSpec((1, tn), lambda j, i, k: (0, j))


def _store(dtype):
    def ep(acc, ex, outs, ids, scr):
        outs[0][...] = acc.astype(dtype)
    return ep


def _sds(shape, dtype):
    return jax.ShapeDtypeStruct(shape, dtype)


def _rms_fwd(name, x, g):
    T, D = x.shape
    tm = 256

    def body(x_ref, g_ref, h_ref, r_ref):
        xv = x_ref[...]
        r = lax.rsqrt(jnp.mean(xv * xv, axis=-1, keepdims=True) + EPS)
        h_ref[...] = (xv * r * g_ref[...]).astype(BF)
        r_ref[...] = r

    return pl.pallas_call(
        body, name=name, grid=(T // tm,),
        in_specs=[pl.BlockSpec((tm, D), lambda i: (i, 0)), pl.BlockSpec((1, D), lambda i: (0, 0))],
        out_specs=[pl.BlockSpec((tm, D), lambda i: (i, 0)), pl.BlockSpec((tm, 1), lambda i: (i, 0))],
        out_shape=[_sds((T, D), BF), _sds((T, 1), F32)],
        compiler_params=_params(("arbitrary",)),
    )(x, g)


def _rms_bwd(dh, xv, r, g):
    xh = xv * r
    dxh = dh * g
    dx = r * (dxh - xh * jnp.mean(dxh * xh, axis=-1, keepdims=True))
    return dx, jnp.sum(dh * xh, axis=0, keepdims=True)


def _accumulate_rows(ref, val, first):
    @pl.when(first)
    def _():
        ref[...] = val

    @pl.when(jnp.logical_not(first))
    def _():
        ref[...] += val


def _final(x3, g_final, target):
    T, D = x3.shape
    tm = 256

    def body(x_ref, g_ref, t_ref, dx_ref, dxb_ref, dg_ref, loss_ref):
        i = pl.program_id(0)
        xv = x_ref[...]
        g = g_ref[...]
        r = lax.rsqrt(jnp.mean(xv * xv, axis=-1, keepdims=True) + EPS)
        err = xv * r * g - t_ref[...]
        dy = err * (1.0 / D)
        dx, dg = _rms_bwd(dy, xv, r, g)
        dx_ref[...] = dx
        dxb_ref[...] = dx.astype(BF)
        part = 0.5 * jnp.sum(jnp.mean(err * err, axis=-1, keepdims=True), axis=0, keepdims=True)
        _accumulate_rows(dg_ref, dg, i == 0)
        _accumulate_rows(loss_ref, part, i == 0)

    return pl.pallas_call(
        body, name="final_loss", grid=(T // tm,),
        in_specs=[pl.BlockSpec((tm, D), lambda i: (i, 0)), pl.BlockSpec((1, D), lambda i: (0, 0)),
                  pl.BlockSpec((tm, D), lambda i: (i, 0))],
        out_specs=[pl.BlockSpec((tm, D), lambda i: (i, 0)), pl.BlockSpec((tm, D), lambda i: (i, 0)),
                   pl.BlockSpec((1, D), lambda i: (0, 0)), pl.BlockSpec((1, 1), lambda i: (0, 0))],
        out_shape=[_sds((T, D), F32), _sds((T, D), BF), _sds((1, D), F32), _sds((1, 1), F32)],
        compiler_params=_params(("arbitrary",)),
    )(x3, g_final, target)


def _lane_half(shape, h):
    lane = lax.broadcasted_iota(jnp.int32, shape, 1)
    return (lane >= HEAD_DIM * h) & (lane < HEAD_DIM * (h + 1))


def _to_half(v, w, h):
    if w != h:
        v = pltpu.roll(v, HEAD_DIM, 1)
    return jnp.where(_lane_half(v.shape, h), v, 0.0)


def _attn_block(qkv_ref, sinks_ref, n, h):
    r0 = pl.multiple_of(n * BLOCK, BLOCK)
    p0 = pl.multiple_of(jnp.maximum(n - 1, 0) * BLOCK, BLOCK)
    rows = pl.ds(r0, BLOCK)
    prev = pl.ds(p0, BLOCK)
    k2 = jnp.concatenate([qkv_ref[prev, ATTN_WIDTH:ATTN_WIDTH + KV_WIDTH],
                          qkv_ref[rows, ATTN_WIDTH:ATTN_WIDTH + KV_WIDTH]], axis=0).astype(BF)
    v2 = jnp.concatenate([qkv_ref[prev, ATTN_WIDTH + KV_WIDTH:ATTN_WIDTH + 2 * KV_WIDTH],
                          qkv_ref[rows, ATTN_WIDTH + KV_WIDTH:ATTN_WIDTH + 2 * KV_WIDTH]], axis=0).astype(BF)
    qs = []
    for g in range(GROUP):
        hq = GROUP * h + g
        blk = qkv_ref[rows, (hq // 2) * 128:(hq // 2 + 1) * 128]
        qs.append(_to_half(blk, hq % 2, h))
    q4 = jnp.concatenate(qs, axis=0).astype(BF)
    s = lax.dot_general(q4, k2, _DIMS["NT"], preferred_element_type=F32) * (HEAD_DIM ** -0.5)
    shape = s.shape
    row = lax.broadcasted_iota(jnp.int32, shape, 0)
    qi = row & (BLOCK - 1)
    kj = lax.broadcasted_iota(jnp.int32, shape, 1)
    diff = qi + BLOCK - kj
    valid = (diff >= 0) & (diff < BLOCK) & ((kj >= BLOCK) | (n > 0))
    s = jnp.where(valid, s, NEG)
    row1 = lax.broadcasted_iota(jnp.int32, (shape[0], 1), 0)
    sink = jnp.zeros((shape[0], 1), F32)
    for g in range(GROUP):
        sink = jnp.where((row1 >= g * BLOCK) & (row1 < (g + 1) * BLOCK), sinks_ref[0, GROUP * h + g], sink)
    m = jnp.maximum(jnp.max(s, axis=-1, keepdims=True), sink)
    e = jnp.exp(s - m)
    es = jnp.exp(sink - m)
    inv = 1.0 / (jnp.sum(e, axis=-1, keepdims=True) + es)
    return e * inv, es * inv, q4, k2, v2, rows, prev


def _attn_fwd(proj, sinks):
    T = proj.shape[0]

    def body(qkv_ref, sinks_ref, o_ref):
        def blk(n, carry):
            outs = [None] * (N_Q_HEADS // 2)
            for h in range(N_KV_HEADS):
                p, _, _, _, v2, rows, _ = _attn_block(qkv_ref, sinks_ref, n, h)
                o = lax.dot_general(p.astype(BF), v2, _DIMS["NN"], preferred_element_type=F32)
                for g in range(GROUP):
                    hq = GROUP * h + g
                    piece = jnp.where(_lane_half((BLOCK, 128), h), o[g * BLOCK:(g + 1) * BLOCK], 0.0)
                    if hq % 2 != h:
                        piece = pltpu.roll(piece, HEAD_DIM, 1)
                    outs[hq // 2] = piece if outs[hq // 2] is None else outs[hq // 2] + piece
            for pb in range(N_Q_HEADS // 2):
                o_ref[rows, pb * 128:(pb + 1) * 128] = outs[pb].astype(BF)
            return carry

        lax.fori_loop(0, T // BLOCK, blk, 0)

    return pl.pallas_call(
        body, name="attn_fwd", grid=(1,),
        in_specs=[pl.BlockSpec((T, GLU_OFF), lambda i: (0, 0)), pl.BlockSpec(memory_space=pltpu.SMEM)],
        out_specs=pl.BlockSpec((T, ATTN_WIDTH), lambda i: (0, 0)),
        out_shape=_sds((T, ATTN_WIDTH), BF),
        compiler_params=_params(("arbitrary",)),
    )(proj, sinks)


def _attn_bwd(proj, d_o, sinks):
    T = proj.shape[0]

    def body(qkv_ref, do_ref, sinks_ref, dqkv_ref, dsink_ref, dk_acc, dv_acc):
        dsink_ref[...] = jnp.zeros_like(dsink_ref)
        dk_acc[...] = jnp.zeros_like(dk_acc)
        dv_acc[...] = jnp.zeros_like(dv_acc)

        def blk(n, carry):
            dqs = [None] * (N_Q_HEADS // 2)
            for h in range(N_KV_HEADS):
                p, psink, q4, k2, v2, rows, prev = _attn_block(qkv_ref, sinks_ref, n, h)
                dos = []
                for g in range(GROUP):
                    hq = GROUP * h + g
                    dos.append(_to_half(do_ref[rows, (hq // 2) * 128:(hq // 2 + 1) * 128].astype(F32), hq % 2, h))
                do4 = jnp.concatenate(dos, axis=0).astype(BF)
                dp = lax.dot_general(do4, v2, _DIMS["NT"], preferred_element_type=F32)
                delta = jnp.sum(p * dp, axis=-1, keepdims=True)
                ds = (p * (dp - delta) * (HEAD_DIM ** -0.5)).astype(BF)
                dsk = psink * delta
                for g in range(GROUP):
                    hq = GROUP * h + g
                    tot = -jnp.sum(dsk[g * BLOCK:(g + 1) * BLOCK], axis=0, keepdims=True)
                    lane = lax.broadcasted_iota(jnp.int32, (1, 128), 1)
                    dsink_ref[...] += jnp.where(lane == hq, tot, 0.0)
                dq = lax.dot_general(ds, k2, _DIMS["NN"], preferred_element_type=F32)
                dk = lax.dot_general(ds, q4, _DIMS["TN"], preferred_element_type=F32)
                dv = lax.dot_general(p.astype(BF), do4, _DIMS["TN"], preferred_element_type=F32)
                dk_acc[prev, :] += dk[:BLOCK]
                dk_acc[rows, :] += dk[BLOCK:]
                dv_acc[prev, :] += dv[:BLOCK]
                dv_acc[rows, :] += dv[BLOCK:]
                for g in range(GROUP):
                    hq = GROUP * h + g
                    piece = jnp.where(_lane_half((BLOCK, 128), h), dq[g * BLOCK:(g + 1) * BLOCK], 0.0)
                    if hq % 2 != h:
                        piece = pltpu.roll(piece, HEAD_DIM, 1)
                    dqs[hq // 2] = piece if dqs[hq // 2] is None else dqs[hq // 2] + piece
            for pb in range(N_Q_HEADS // 2):
                dqkv_ref[rows, pb * 128:(pb + 1) * 128] = dqs[pb].astype(BF)
            return carry

        lax.fori_loop(0, T // BLOCK, blk, 0)
        dqkv_ref[:, ATTN_WIDTH:ATTN_WIDTH + KV_WIDTH] = dk_acc[...].astype(BF)
        dqkv_ref[:, ATTN_WIDTH + KV_WIDTH:] = dv_acc[...].astype(BF)

    return pl.pallas_call(
        body, name="attn_bwd", grid=(1,),
        in_specs=[pl.BlockSpec((T, GLU_OFF), lambda i: (0, 0)), pl.BlockSpec((T, ATTN_WIDTH), lambda i: (0, 0)),
                  pl.BlockSpec(memory_space=pltpu.SMEM)],
        out_specs=[pl.BlockSpec((T, GLU_OFF), lambda i: (0, 0)), pl.BlockSpec((1, 128), lambda i: (0, 0))],
        out_shape=[_sds((T, GLU_OFF), BF), _sds((1, 128), F32)],
        scratch_shapes=[pltpu.VMEM((T, KV_WIDTH), F32), pltpu.VMEM((T, KV_WIDTH), F32)],
        compiler_params=_params(("arbitrary",)),
    )(proj, d_o, sinks)


CHUNK = 256
_GLU_SPECS = [pl.BlockSpec((SEQ, 256), functools.partial(lambda i, c: (0, c), c=GLU_OFF // 256 + c)) for c in range(4)]


def _glu_to_pad(a0, a1, b0, b1, zpad):
    C = CONV_CHANNELS
    zpad[0:CONV_PAD, :] = jnp.zeros((CONV_PAD, C), F32)
    zpad[CONV_PAD:, 0:256] = a0[...] * jax.nn.sigmoid(b0[...])
    zpad[CONV_PAD:, 256:C] = a1[...] * jax.nn.sigmoid(b1[...])


def _conv_chunk(zpad, w_ref, cb_ref, ci):
    base = ci * CHUNK + CONV_PAD - (CONV_WIDTH - 1)
    u = jnp.broadcast_to(cb_ref[...], (CHUNK, CONV_CHANNELS))
    for k in range(CONV_WIDTH):
        u = u + w_ref[k:k + 1, :] * zpad[base + k:base + k + CHUNK, :]
    return u


def _ln_parts(u):
    mu = jnp.mean(u, axis=-1, keepdims=True)
    xc = u - mu
    rstd = lax.rsqrt(jnp.mean(xc * xc, axis=-1, keepdims=True) + EPS)
    return xc * rstd, rstd


def _conv_fwd(proj, conv_w, conv_b, ln_g, ln_b, carry=None):
    T, C = proj.shape[0], CONV_CHANNELS
    vec = pl.BlockSpec((1, C), lambda i: (0, 0))
    c_in, c_out, c_sems = _carry_io(carry)

    def body(*refs):
        a0, a1, b0, b1, w_ref, cb_ref, g_ref, be_ref = refs[:8]
        ci_refs = refs[8:8 + len(c_in)]
        c_ref = refs[8 + len(c_in)]
        co_refs = refs[9 + len(c_in):9 + len(c_in) + len(c_out)]
        zpad = refs[9 + len(c_in) + len(c_out)]
        cs_refs = refs[10 + len(c_in) + len(c_out):]
        if carry is not None:
            carry.start(ci_refs, co_refs, cs_refs)
        _glu_to_pad(a0, a1, b0, b1, zpad)
        for ci in range(T // CHUNK):
            xh, _ = _ln_parts(_conv_chunk(zpad, w_ref, cb_ref, ci))
            ln = xh * g_ref[...] + be_ref[...]
            c_ref[ci * CHUNK:(ci + 1) * CHUNK, :] = (ln * jax.nn.sigmoid(ln)).astype(BF)
        if carry is not None:
            carry.finish(ci_refs, co_refs, cs_refs)

    res = pl.pallas_call(
        body, name="conv_fwd", grid=(1,),
        in_specs=[*_GLU_SPECS, pl.BlockSpec((CONV_PAD, C), lambda i: (0, 0)), vec, vec, vec, *[ANY] * len(c_in)],
        out_specs=[pl.BlockSpec((T, C), lambda i: (0, 0)), *[ANY] * len(c_out)],
        out_shape=[_sds((T, C), BF), *c_out],
        scratch_shapes=[pltpu.VMEM((T + CONV_PAD, C), F32), *c_sems],
        compiler_params=_params(("arbitrary",)),
    )(proj, proj, proj, proj, conv_w, conv_b, ln_g, ln_b, *c_in)
    return res[0], res[1:]


def _conv_bwd(proj, d_c, conv_w, conv_b, ln_g, ln_b, carry=None):
    T, C = proj.shape[0], CONV_CHANNELS
    vec = pl.BlockSpec((1, C), lambda i: (0, 0))
    wspec = pl.BlockSpec((CONV_PAD, C), lambda i: (0, 0))
    c_in, c_out, c_sems = _carry_io(carry)

    def body(*refs):
        a0, a1, b0, b1, dc_ref, w_ref, cb_ref, g_ref, be_ref = refs[:9]
        ci_refs = refs[9:9 + len(c_in)]
        o = 9 + len(c_in)
        dglu_ref, dw_ref, dcb_ref, dg_ref, dbe_ref = refs[o:o + 5]
        co_refs = refs[o + 5:o + 5 + len(c_out)]
        zpad, dupad = refs[o + 5 + len(c_out):o + 7 + len(c_out)]
        cs_refs = refs[o + 7 + len(c_out):]
        if carry is not None:
            carry.start(ci_refs, co_refs, cs_refs)
        _glu_to_pad(a0, a1, b0, b1, zpad)
        dupad[T:, :] = jnp.zeros((CONV_PAD, C), F32)
        dw_ref[...] = jnp.zeros_like(dw_ref)
        dcb_ref[...] = jnp.zeros_like(dcb_ref)
        dg_ref[...] = jnp.zeros_like(dg_ref)
        dbe_ref[...] = jnp.zeros_like(dbe_ref)
        for ci in range(T // CHUNK):
            rows = slice(ci * CHUNK, (ci + 1) * CHUNK)
            xh, rstd = _ln_parts(_conv_chunk(zpad, w_ref, cb_ref, ci))
            ln = xh * g_ref[...] + be_ref[...]
            sg = jax.nn.sigmoid(ln)
            dln = dc_ref[rows, :].astype(F32) * (sg * (1.0 + ln * (1.0 - sg)))
            dg_ref[...] += jnp.sum(dln * xh, axis=0, keepdims=True)
            dbe_ref[...] += jnp.sum(dln, axis=0, keepdims=True)
            dxh = dln * g_ref[...]
            du = rstd * (dxh - jnp.mean(dxh, axis=-1, keepdims=True)
                         - xh * jnp.mean(dxh * xh, axis=-1, keepdims=True))
            dupad[rows, :] = du
            dcb_ref[...] += jnp.sum(du, axis=0, keepdims=True)
            base = ci * CHUNK + CONV_PAD - (CONV_WIDTH - 1)
            for k in range(CONV_WIDTH):
                dw_ref[k:k + 1, :] += jnp.sum(du * zpad[base + k:base + k + CHUNK, :], axis=0, keepdims=True)
        for ci in range(T // CHUNK):
            rows = slice(ci * CHUNK, (ci + 1) * CHUNK)
            dz = jnp.zeros((CHUNK, C), F32)
            for k in range(CONV_WIDTH):
                off = ci * CHUNK + (CONV_WIDTH - 1) - k
                dz = dz + w_ref[k:k + 1, :] * dupad[off:off + CHUNK, :]
            for half, (a, b) in enumerate(((a0, b0), (a1, b1))):
                sb = jax.nn.sigmoid(b[rows, :])
                dzh = dz[:, half * 256:(half + 1) * 256]
                dglu_ref[rows, half * 256:(half + 1) * 256] = (dzh * sb).astype(BF)
                dglu_ref[rows, C + half * 256:C + (half + 1) * 256] = (dzh * a[rows, :] * sb * (1.0 - sb)).astype(BF)
        if carry is not None:
            carry.finish(ci_refs, co_refs, cs_refs)

    res = pl.pallas_call(
        body, name="conv_bwd", grid=(1,),
        in_specs=[*_GLU_SPECS, pl.BlockSpec((T, C), lambda i: (0, 0)), wspec, vec, vec, vec, *[ANY] * len(c_in)],
        out_specs=[pl.BlockSpec((T, 2 * C), lambda i: (0, 0)), wspec, vec, vec, vec, *[ANY] * len(c_out)],
        out_shape=[_sds((T, 2 * C), BF), _sds((CONV_PAD, C), F32), _sds((1, C), F32), _sds((1, C), F32),
                   _sds((1, C), F32), *c_out],
        scratch_shapes=[pltpu.VMEM((T + CONV_PAD, C), F32), pltpu.VMEM((T + CONV_PAD, C), F32), *c_sems],
        compiler_params=_params(("arbitrary",)),
    )(proj, proj, proj, proj, d_c, conv_w, conv_b, ln_g, ln_b, *c_in)
    return res[:5], res[5:]


_GATE_BLK = GATE_OFF // 256


def _ffn_in_swiglu(h2, wf_t, carry=None):
    T, D = h2.shape
    tm, tn = 512, D_FF // 2
    nj, ni = D_FF // tn, T // tm
    c_in, c_out, c_sems = _carry_io(carry)

    def body(*refs):
        a_ref, bg_ref, bu_ref = refs[:3]
        ci_refs = refs[3:3 + len(c_in)]
        act_ref, g_ref, u_ref = refs[3 + len(c_in):6 + len(c_in)]
        co_refs = refs[6 + len(c_in):6 + len(c_in) + len(c_out)]
        cs_refs = refs[6 + len(c_in) + len(c_out):]
        j, i = pl.program_id(0), pl.program_id(1)
        if carry is not None:
            @pl.when((j == 0) & (i == 0))
            def _():
                carry.start(ci_refs, co_refs, cs_refs)
        a = a_ref[...]
        g = lax.dot_general(a, bg_ref[...], _DIMS["NT"], preferred_element_type=F32)
        u = lax.dot_general(a, bu_ref[...], _DIMS["NT"], preferred_element_type=F32)
        act_ref[...] = (g * jax.nn.sigmoid(g) * u).astype(BF)
        g_ref[...] = g.astype(BF)
        u_ref[...] = u.astype(BF)
        if carry is not None:
            @pl.when((j == nj - 1) & (i == ni - 1))
            def _():
                carry.finish(ci_refs, co_refs, cs_refs)

    t = pl.BlockSpec((tm, tn), lambda j, i: (i, j))
    res = pl.pallas_call(
        body, name="ffn_in_swiglu", grid=(nj, ni),
        in_specs=[pl.BlockSpec((tm, D), lambda j, i: (i, 0)), pl.BlockSpec((tn, D), lambda j, i: (j, 0)),
                  pl.BlockSpec((tn, D), lambda j, i: (nj + j, 0)), *[ANY] * len(c_in)],
        out_specs=[t, t, t, *[ANY] * len(c_out)], out_shape=[*[_sds((T, D_FF), BF)] * 3, *c_out],
        scratch_shapes=c_sems,
        compiler_params=_params(("arbitrary", "arbitrary")),
    )(h2, wf_t, wf_t, *c_in)
    return res[:3], res[3:]


def _local_step(x, target, small, wi_t, conv_w, plan):
    T, D = x.shape
    tm = 1024

    def carried(res, carry, done):
        if carry is None:
            return res
        outs, got = res
        done(got)
        return outs

    h, r1 = _rms_fwd("rms_mix", x, small["g_mix_norm"])

    def ep_add(acc, ex, outs, ids, scr):
        outs[0][...] = acc + ex[0][...]

    tn_in = IN_WIDTH // 3
    carry = plan.gather_carry("proj_in")
    proj, = carried(_matmul("proj_in", [h], wi_t, "NT", m=T, n=IN_WIDTH, tm=tm, tn=tn_in, epilogue=ep_add,
                            extra=[(small["b_in"], _row(tn_in))],
                            outs=[(_sds((T, IN_WIDTH), F32), _tile(tm, tn_in))], carry=carry), carry, plan.gathered)
    o = _attn_fwd(proj, small["sinks"])
    carry = plan.gather_carry("conv_fwd")
    c, got = _conv_fwd(proj, conv_w, small["conv_b"], small["ln_g"], small["ln_b"], carry=carry)
    plan.gathered(got)
    wap_t, wcp_t, w_out = plan.weight("w_attn_proj"), plan.weight("w_conv_proj"), plan.weight("w_out")
    ya, = _matmul("attn_proj", [o], wap_t, "NT", m=T, n=D, tm=tm, tn=D, epilogue=_store(F32),
                  outs=[(_sds((T, D), F32), _tile(tm, D))])

    tg = 256
    gate_specs = [pl.BlockSpec((tm, tg), lambda j, i, k: (i, _GATE_BLK + j)),
                  pl.BlockSpec((tm, tg), lambda j, i, k: (i, _GATE_BLK + D // tg + j))]

    def ep_merge(acc, ex, outs, ids, scr):
        yc = acc + ex[0][...]
        outs[0][...] = yc
        outs[1][...] = (jax.nn.sigmoid(ex[2][...]) * ex[1][...] + jax.nn.sigmoid(ex[3][...]) * yc).astype(BF)

    yc, merged = _matmul(
        "conv_proj_merge", [c], wcp_t, "NT", m=T, n=D, tm=tm, tn=tg, epilogue=ep_merge,
        extra=[(small["b_conv_proj"], _row(tg)), (ya, _tile(tm, tg)), (proj, gate_specs[0]), (proj, gate_specs[1])],
        outs=[(_sds((T, D), F32), _tile(tm, tg)), (_sds((T, D), BF), _tile(tm, tg))])
    x2, = _matmul("out_proj", [merged], w_out, "NN", m=T, n=D, tm=tm, tn=D, epilogue=ep_add,
                  extra=[(x, _tile(tm, D))], outs=[(_sds((T, D), F32), _tile(tm, D))])
    h2, r2 = _rms_fwd("rms_ffn", x2, small["g_ffn_norm"])
    wf_t = plan.weight("w_ffn_in")
    carry = plan.gather_carry("ffn_in_swiglu")
    (act, gate, up), got = _ffn_in_swiglu(h2, wf_t, carry=carry)
    plan.gathered(got)
    w_down = plan.weight("w_ffn_down")
    x3, = _matmul("ffn_down", [act], w_down, "NN", m=T, n=D, tm=512, tn=D, epilogue=ep_add,
                  extra=[(x2, _tile(512, D))], outs=[(_sds((T, D), F32), _tile(512, D))])
    dx3, dx3_b, dg_final, loss = _final(x3, small["g_final"], target)

    tn_ff = D_FF // 2

    def ep_swiglu_bwd(acc, ex, outs, ids, scr):
        g, u = ex[0][...].astype(F32), ex[1][...].astype(F32)
        sg = jax.nn.sigmoid(g)
        outs[0][...] = (acc * u * sg * (1.0 + g * (1.0 - sg))).astype(BF)
        outs[1][...] = (acc * g * sg).astype(BF)

    dgate, dup = _matmul(
        "ffn_down_bwd", [dx3_b], w_down, "NT", m=T, n=D_FF, tm=512, tn=tn_ff, epilogue=ep_swiglu_bwd,
        extra=[(gate, _tile(512, tn_ff)), (up, _tile(512, tn_ff))],
        outs=[(_sds((T, D_FF), BF), _tile(512, tn_ff)), (_sds((T, D_FF), BF), _tile(512, tn_ff))])

    def dw(name, a, b, rows, cols, row_off=0, alias=None, total_rows=None, colsum=False):
        total_rows = rows if total_rows is None else total_rows
        tmw = rows if rows <= 1024 else D_FF // 2
        by_dma = row_off % tmw != 0

        def ep(acc, ex, outs, ids, scr):
            if by_dma:
                scr[0][...] = acc.astype(BF)
                pltpu.sync_copy(scr[0], outs[0].at[pl.ds(pl.multiple_of(row_off + ids[1] * tmw, 256), tmw)])
            else:
                outs[0][...] = acc.astype(BF)
            if colsum:
                outs[1][...] = jnp.sum(ex[0][...].astype(F32), axis=0, keepdims=True)

        blk = row_off // tmw
        spec = pl.BlockSpec(memory_space=pl.ANY) if by_dma else pl.BlockSpec((tmw, cols), lambda j, i, k: (blk + i, j))
        outs = [(_sds((total_rows, cols), BF), spec)]
        extra = []
        if colsum:
            extra = [(a, pl.BlockSpec((T, tmw), lambda j, i, k: (0, i)))]
            outs.append((_sds((1, rows), F32), pl.BlockSpec((1, tmw), lambda j, i, k: (0, i))))
        res = _matmul(name, [a], b, "TN", m=rows, n=cols, tm=tmw, tn=cols, epilogue=ep, extra=extra, outs=outs,
                      alias=None if alias is None else (alias, 0),
                      scratch=[pltpu.VMEM((tmw, cols), BF)] if by_dma else [])
        return res if colsum else res[0]

    plan.grad_ready("down", dict(w_ffn_down=dw("ffn_down_dw", act, dx3_b, D_FF, D)))

    def ep_rms_bwd(acc, ex, outs, ids, scr):
        dx, dg = _rms_bwd(acc, ex[0][...], ex[1][...], ex[2][...])
        dx = ex[3][...] + dx
        outs[0][...] = dx
        outs[1][...] = dx.astype(BF)
        _accumulate_rows(outs[2], dg, ids[1] == 0)

    def rms_bwd_io(tm_, xin, r, g, dres):
        return dict(
            extra=[(xin, _tile(tm_, D)), (r, pl.BlockSpec((tm_, 1), lambda j, i, k: (i, 0))), (g, _row(D)),
                   (dres, _tile(tm_, D))],
            outs=[(_sds((T, D), F32), _tile(tm_, D)), (_sds((T, D), BF), _tile(tm_, D)), (_sds((1, D), F32), _row(D))])

    carry = plan.send_carry()
    dx2, dx2_b, dg_ffn = carried(
        _matmul("ffn_in_bwd", [dgate, dup], wf_t, "NN", m=T, n=D, tm=512, tn=D, tk=D_FF, epilogue=ep_rms_bwd,
                carry=carry, **rms_bwd_io(512, x2, r2, small["g_ffn_norm"], dx3)), carry, plan.sent)
    gwf_t = dw("ffn_in_dw_gate", dgate, h2, D_FF, D, total_rows=2 * D_FF)
    gwf_t = dw("ffn_in_dw_up", dup, h2, D_FF, D, row_off=D_FF, alias=gwf_t, total_rows=2 * D_FF)

    def ep_merge_bwd(acc, ex, outs, ids, scr):
        s0 = jax.nn.sigmoid(ex[2][...])
        s1 = jax.nn.sigmoid(ex[3][...])
        outs[0][...] = (acc * s0).astype(BF)
        outs[1][...] = (acc * s1).astype(BF)
        outs[2][...] = (acc * ex[0][...] * s0 * (1.0 - s0)).astype(BF)
        outs[3][...] = (acc * ex[1][...] * s1 * (1.0 - s1)).astype(BF)

    dya, dyc, dg0, dg1 = _matmul(
        "out_proj_bwd_merge", [dx2_b], w_out, "NT", m=T, n=D, tm=tm, tn=tg, epilogue=ep_merge_bwd,
        extra=[(ya, _tile(tm, tg)), (yc, _tile(tm, tg)), (proj, gate_specs[0]), (proj, gate_specs[1])],
        outs=[(_sds((T, D), BF), _tile(tm, tg))] * 4)
    gw_out = dw("out_proj_dw", merged, dx2_b, D, D)
    d_o, = _matmul("attn_proj_bwd", [dya], wap_t, "NN", m=T, n=ATTN_WIDTH, tm=tm, tn=ATTN_WIDTH,
                   epilogue=_store(BF), outs=[(_sds((T, ATTN_WIDTH), BF), _tile(tm, ATTN_WIDTH))])
    d_c, = _matmul("conv_proj_bwd", [dyc], wcp_t, "NN", m=T, n=CONV_CHANNELS, tm=tm, tn=CONV_CHANNELS,
                   epilogue=_store(BF), outs=[(_sds((T, CONV_CHANNELS), BF), _tile(tm, CONV_CHANNELS))])
    gwap_t = dw("attn_proj_dw", dya, o, D, ATTN_WIDTH)
    gwcp_t, db_cp = dw("conv_proj_dw", dyc, c, D, CONV_CHANNELS, colsum=True)
    plan.grad_ready("mid", dict(w_ffn_in=gwf_t, w_out=gw_out, w_attn_proj=gwap_t, w_conv_proj=gwcp_t))
    carry = plan.send_carry()
    (dglu, dcw, dcb, dlng, dlnb), got = _conv_bwd(proj, d_c, conv_w, small["conv_b"], small["ln_g"], small["ln_b"],
                                                  carry=carry)
    plan.sent(got)
    dqkv, dsinks = _attn_bwd(proj, d_o, small["sinks"])

    segs = [dqkv, dglu, dg0, dg1]
    gwi_t, off, db_in = None, 0, []
    for s, seg in enumerate(segs):
        gwi_t, db = dw(f"proj_in_dw{s}", seg, h, seg.shape[1], D, row_off=off, alias=gwi_t, total_rows=IN_WIDTH,
                       colsum=True)
        db_in.append(db)
        off += seg.shape[1]
    plan.grad_ready("in", dict(w_in=gwi_t))
    carry = plan.send_carry()
    dx, _, dg_mix = carried(
        _matmul("proj_in_bwd", segs, wi_t, "NN", m=T, n=D, tm=512, tn=D, epilogue=ep_rms_bwd, carry=carry,
                **rms_bwd_io(512, x, r1, small["g_mix_norm"], dx2)), carry, plan.sent)

    parts = dict(g_mix_norm=dg_mix, b_in=db_in, sinks=dsinks, conv_w=dcw, conv_b=dcb, ln_g=dlng, ln_b=dlnb,
                 b_conv_proj=db_cp, g_ffn_norm=dg_ffn, g_final=dg_final, loss=loss)
    return dx, parts


def _place():
    x, y, c = lax.axis_index("x"), lax.axis_index("y"), lax.axis_index("c")
    return x, y, c, [(1 - x, y), (x, 1 - y), (1 - x, 1 - y)]


def _gather_copies(x_refs, out_refs, rows_per, send_sems, recv_sems, local_sems):
    x, y, c, chips = _place()
    me, sibling = (x, y, c), (x, y, 1 - c)

    def rows(a, px, py, pc):
        return out_refs[a].at[pl.ds((4 * px + 2 * py + pc) * rows_per[a], rows_per[a])]

    def copy(a, k, block, to, src=None):
        return pltpu.make_async_remote_copy(
            src_ref=rows(a, *block) if src is None else src, dst_ref=rows(a, *block),
            send_sem=send_sems.at[7 * a + k], recv_sem=recv_sems.at[7 * a + k], device_id=to, device_id_type=MESH)

    def local(a):
        return pltpu.make_async_copy(x_refs[a], rows(a, *me), local_sems.at[a])

    def first(a):
        return [copy(a, 0, me, sibling, src=x_refs[a])] + [copy(a, 1 + j, me, (*chip, c), src=x_refs[a])
                                                          for j, chip in enumerate(chips)]

    def arrive(a, j):
        return copy(a, 1 + j, (*chips[j], c), me)

    def passed(a, j):
        return copy(a, 4 + j, (*chips[j], c), sibling)

    def from_sibling(a):
        return [copy(a, 0, sibling, me)] + [copy(a, 4 + j, (*chip, 1 - c), me) for j, chip in enumerate(chips)]

    return len(x_refs), local, first, arrive, passed, from_sibling


def _gather_start(*refs):
    n, local, first, _, _, _ = _gather_copies(*refs)
    for a in range(n):
        local(a).start()
        for cp in first(a):
            cp.start()


def _gather_finish(*refs):
    n, local, first, arrive, passed, from_sibling = _gather_copies(*refs)
    for a in range(n):
        for j in range(3):
            arrive(a, j).wait_recv()
            passed(a, j).start()
    for a in range(n):
        for cp in from_sibling(a):
            cp.wait_recv()
    for a in range(n):
        for cp in first(a) + [passed(a, j) for j in range(3)]:
            cp.wait_send()
        local(a).wait()


def _gather_blocks(*refs):
    _gather_start(*refs)
    _gather_finish(*refs)


def _gather_sems(n):
    return [pltpu.SemaphoreType.DMA((7 * n,)), pltpu.SemaphoreType.DMA((7 * n,)), pltpu.SemaphoreType.DMA((n,))]


def _gather_carry(shards):
    rows_per = [s.shape[0] for s in shards]
    return _Carry(shards, [_sds((N_DEV * s.shape[0],) + s.shape[1:], s.dtype) for s in shards],
                  _gather_sems(len(shards)),
                  lambda ins, outs, sems: _gather_start(ins, outs, rows_per, *sems),
                  lambda ins, outs, sems: _gather_finish(ins, outs, rows_per, *sems))


def _all_gather(shards):
    n = len(shards)
    carry = _gather_carry(shards)

    def body(*refs):
        carry.start(refs[:n], refs[n:2 * n], refs[2 * n:])
        carry.finish(refs[:n], refs[n:2 * n], refs[2 * n:])

    return pl.pallas_call(
        body, name="weights_all_gather", in_specs=[ANY] * n, out_specs=[ANY] * n, out_shape=carry.out_shapes,
        scratch_shapes=carry.sems,
    )(*shards)


def _swap_halves(name, grads):
    n = len(grads)

    def body(*refs):
        g_refs, out_refs, send_sems, recv_sems = refs[:n], refs[n:2 * n], refs[2 * n], refs[2 * n + 1]
        x, y, c, _ = _place()
        copies = []
        for a in range(n):
            for p in range(4):
                cp = pltpu.make_async_remote_copy(
                    src_ref=g_refs[a].at[2 * p + 1 - c], dst_ref=out_refs[a].at[p],
                    send_sem=send_sems.at[4 * a + p], recv_sem=recv_sems.at[4 * a + p],
                    device_id=(x, y, 1 - c), device_id_type=MESH)
                cp.start()
                copies.append(cp)
        for cp in copies:
            cp.wait()

    return pl.pallas_call(
        body, name=name,
        in_specs=[ANY] * n, out_specs=[ANY] * n,
        out_shape=[_sds((4,) + g.shape[1:], g.dtype) for g in grads],
        scratch_shapes=[pltpu.SemaphoreType.DMA((4 * n,)), pltpu.SemaphoreType.DMA((4 * n,))],
    )(*grads)


def _chip_sum(name, g, got, c):
    _, rows, cols = g.shape

    def body(c_ref, g_ref, got_ref, o_ref):
        o_ref[...] = (g_ref[...].astype(F32) + got_ref[...].astype(F32)).astype(BF)

    return pl.pallas_call(
        body, name=name,
        grid_spec=pltpu.PrefetchScalarGridSpec(
            num_scalar_prefetch=1, grid=(4,),
            in_specs=[pl.BlockSpec((1, rows, cols), lambda p, c_ref: (2 * p + c_ref[0], 0, 0)),
                      pl.BlockSpec((1, rows, cols), lambda p, c_ref: (p, 0, 0))],
            out_specs=pl.BlockSpec((1, rows, cols), lambda p, c_ref: (p, 0, 0))),
        out_shape=_sds((4, rows, cols), BF),
        compiler_params=_params(("arbitrary",)),
    )(c, g, got)


def _send_carry(sums):
    n = len(sums)

    def copies(s_refs, out_refs, sems):
        send_sems, recv_sems = sems
        x, y, c, chips = _place()
        return [pltpu.make_async_remote_copy(
            src_ref=s_refs[a].at[2 * px + py], dst_ref=out_refs[a].at[k],
            send_sem=send_sems.at[3 * a + k], recv_sem=recv_sems.at[3 * a + k],
            device_id=(px, py, c), device_id_type=MESH) for a in range(n) for k, (px, py) in enumerate(chips)]

    def start(ins, outs, sems):
        for cp in copies(ins, outs, sems):
            cp.start()

    def finish(ins, outs, sems):
        for cp in copies(ins, outs, sems):
            cp.wait()

    return _Carry(sums, [_sds((3,) + s.shape[1:], s.dtype) for s in sums],
                  [pltpu.SemaphoreType.DMA((3 * n,)), pltpu.SemaphoreType.DMA((3 * n,))], start, finish)


def _grad_total(name, g, got, got3, ids):
    _, rows, cols = g.shape

    def body(ids_ref, g_ref, got_ref, got3_ref, o_ref):
        tot = g_ref[0].astype(F32) + got_ref[0].astype(F32)
        for k in range(3):
            tot = tot + got3_ref[k].astype(F32)
        o_ref[...] = tot

    return pl.pallas_call(
        body, name=name,
        grid_spec=pltpu.PrefetchScalarGridSpec(
            num_scalar_prefetch=1, grid=(1,),
            in_specs=[pl.BlockSpec((1, rows, cols), lambda i, ids_ref: (ids_ref[0], 0, 0)),
                      pl.BlockSpec((1, rows, cols), lambda i, ids_ref: (ids_ref[1], 0, 0)),
                      pl.BlockSpec((3, rows, cols), lambda i, ids_ref: (0, 0, 0))],
            out_specs=pl.BlockSpec((rows, cols), lambda i, ids_ref: (0, 0))),
        out_shape=_sds((rows, cols), F32),
        compiler_params=_params(("arbitrary",)),
    )(ids, g, got, got3)


def _adam_math(w, g, m, v):
    m = ADAM_B1 * m + (1.0 - ADAM_B1) * g
    v = ADAM_B2 * v + (1.0 - ADAM_B2) * (g * g)
    m_hat = m / (1.0 - ADAM_B1 ** ADAM_STEP)
    v_hat = v / (1.0 - ADAM_B2 ** ADAM_STEP)
    delta = -ADAM_LR * (m_hat / (jnp.sqrt(v_hat) + ADAM_EPS) + ADAM_WD * w)
    return delta, m, v


def _adamw(name, w, g, m, v):
    rows, cols = w.shape
    tr = 256 if rows % 256 == 0 else rows

    def body(w_ref, g_ref, m_ref, v_ref, d_ref, nm_ref, nv_ref):
        d_ref[...], nm_ref[...], nv_ref[...] = _adam_math(w_ref[...], g_ref[...], m_ref[...], v_ref[...])

    t = pl.BlockSpec((tr, cols), lambda i: (i, 0))
    return pl.pallas_call(
        body, name=name, grid=(rows // tr,), in_specs=[t] * 4, out_specs=[t] * 3,
        out_shape=[_sds((rows, cols), F32)] * 3, compiler_params=_params(("arbitrary",)),
    )(w, g, m, v)


SMALL_NAMES = ["g_mix_norm", "b_in", "sinks", "conv_b", "ln_g", "ln_b", "b_conv_proj", "g_ffn_norm", "g_final"]
_PACK_ROWS = 32


def _small_all_reduce(parts):
    C = CONV_CHANNELS
    part_list = [parts["g_mix_norm"], *parts["b_in"], parts["sinks"], parts["conv_b"], parts["ln_g"], parts["ln_b"],
                 parts["b_conv_proj"], parts["g_ffn_norm"], parts["g_final"], parts["loss"], parts["conv_w"]]
    n_part = len(part_list)

    def body(*refs):
        (p_mix, p_b0, p_b1, p_b2, p_b3, p_sink, p_cb, p_lg, p_lb, p_bcp, p_ffn, p_fin, p_loss, p_cw) = refs[:n_part]
        tot_ref, pack, gathered, send_sems, recv_sems, local_sems = refs[n_part:]
        pack[...] = jnp.zeros_like(pack)
        pack[0:1, :] = p_mix[...]
        pack[1:2, 0:GLU_OFF] = p_b0[...]
        pack[2:3, :] = p_b1[...]
        pack[3:4, :] = p_b2[...]
        pack[4:5, :] = p_b3[...]
        pack[5:6, 0:128] = p_sink[...]
        pack[6:7, 0:C] = p_cb[...]
        pack[6:7, C:2 * C] = p_lg[...]
        pack[7:8, 0:C] = p_lb[...]
        pack[8:9, :] = p_bcp[...]
        pack[9:10, :] = p_ffn[...]
        pack[10:11, :] = p_fin[...]
        pack[11:12, 0:128] = jnp.broadcast_to(p_loss[...], (1, 128))
        pack[12:28, 0:C] = p_cw[0:16, :]
        pack[12:28, C:2 * C] = p_cw[16:32, :]
        _gather_blocks([pack], [gathered], [_PACK_ROWS], send_sems, recv_sems, local_sems)
        tot = gathered[0:_PACK_ROWS, :]
        for d in range(1, N_DEV):
            tot = tot + gathered[d * _PACK_ROWS:(d + 1) * _PACK_ROWS, :]
        tot_ref[...] = tot

    vm = pl.BlockSpec(memory_space=pltpu.VMEM)
    return pl.pallas_call(
        body, name="small_all_reduce",
        in_specs=[vm] * n_part, out_specs=vm, out_shape=_sds((_PACK_ROWS, D_MODEL), F32),
        scratch_shapes=[pltpu.VMEM((_PACK_ROWS, D_MODEL), F32), pltpu.VMEM((N_DEV * _PACK_ROWS, D_MODEL), F32),
                        pltpu.SemaphoreType.DMA((7,)), pltpu.SemaphoreType.DMA((7,)), pltpu.SemaphoreType.DMA((1,))],
        compiler_params=pltpu.CompilerParams(vmem_limit_bytes=VMEM_LIMIT_BYTES),
    )(*part_list)


def _small_adamw(tot, small_w, small_m, small_v):
    C = CONV_CHANNELS
    names = SMALL_NAMES
    widths = [small_w[k].shape[1] for k in names]
    n_small = len(names)

    def body(*refs):
        tot_ref = refs[0]
        w_refs = refs[1:1 + n_small]
        m_refs = refs[1 + n_small:1 + 2 * n_small]
        v_refs = refs[1 + 2 * n_small:1 + 3 * n_small]
        o = 1 + 3 * n_small
        loss_ref, cw_ref = refs[o], refs[o + 1]
        out_refs = refs[o + 2:o + 2 + 4 * n_small]
        tot = tot_ref[...]
        loss_ref[...] = tot[11:12, 0:1]
        cw_ref[0:16, :] = tot[12:28, 0:C]
        cw_ref[16:32, :] = tot[12:28, C:2 * C]
        grads = dict(
            g_mix_norm=tot[0:1, :],
            b_in=jnp.concatenate([tot[1:2, 0:GLU_OFF], tot[2:3, :], tot[3:4, :], tot[4:5, :]], axis=1),
            sinks=tot[5:6, 0:N_Q_HEADS], conv_b=tot[6:7, 0:C], ln_g=tot[6:7, C:2 * C], ln_b=tot[7:8, 0:C],
            b_conv_proj=tot[8:9, :], g_ffn_norm=tot[9:10, :], g_final=tot[10:11, :])
        for s, k in enumerate(names):
            g = grads[k]
            d, nm, nv = _adam_math(w_refs[s][...], g, m_refs[s][...], v_refs[s][...])
            out_refs[4 * s][...] = g
            out_refs[4 * s + 1][...] = d
            out_refs[4 * s + 2][...] = nm
            out_refs[4 * s + 3][...] = nv

    vm = pl.BlockSpec(memory_space=pltpu.VMEM)
    args = [tot, *[small_w[k] for k in names], *[small_m[k] for k in names], *[small_v[k] for k in names]]
    out_shape = [_sds((1, 1), F32), _sds((CONV_PAD, C), F32)]
    for wd in widths:
        out_shape += [_sds((1, wd), F32)] * 4
    res = pl.pallas_call(
        body, name="small_adamw",
        in_specs=[vm] * len(args), out_specs=[vm] * len(out_shape), out_shape=out_shape,
        compiler_params=pltpu.CompilerParams(vmem_limit_bytes=VMEM_LIMIT_BYTES),
    )(*args)
    return res[0], res[1], {k: res[2 + 4 * s:6 + 4 * s] for s, k in enumerate(names)}


BIG = dict(w_in=True, w_attn_proj=True, w_conv_proj=True, w_out=False, w_ffn_in=True, w_ffn_down=False)
WEIGHT_NAMES = ["g_mix_norm", "w_in", "b_in", "sinks", "conv_w", "conv_b", "ln_g", "ln_b", "w_attn_proj",
                "w_conv_proj", "b_conv_proj", "w_out", "g_ffn_norm", "w_ffn_in", "w_ffn_down", "g_final"]


class _Plan:
    GATHERS = dict(proj_in=["w_attn_proj", "w_conv_proj", "w_out"], conv_fwd=["w_ffn_in"],
                   ffn_in_swiglu=["w_ffn_down"])

    def __init__(self, shards, c1):
        self.shards, self.c1 = shards, c1
        self.full, self.slots, self.got, self.sums, self.got3 = {}, {}, {}, {}, {}
        self.names, self.pending = [], []

    def gather_carry(self, call):
        self.names = self.GATHERS[call]
        return _gather_carry([self.shards[k] for k in self.names])

    def gathered(self, outs):
        self.full.update(zip(self.names, outs))

    def weight(self, name):
        return self.full[name]

    def grad_ready(self, group, grads):
        names = list(grads)
        slots = [g.reshape(N_DEV, g.shape[0] // N_DEV, g.shape[1]) for g in grads.values()]
        got = _swap_halves(f"grad_swap_halves_{group}", slots)
        for k, g, r in zip(names, slots, got):
            self.slots[k], self.got[k] = g, r
            self.sums[k] = _chip_sum(f"chip_sum_{k}", g, r, self.c1)
        self.pending += names

    def send_carry(self):
        self.names, self.pending = self.pending, []
        return _send_carry([self.sums[k] for k in self.names])

    def sent(self, outs):
        self.got3.update(zip(self.names, outs))


def kernel(x, g_mix_norm, w_in, b_in, sinks, conv_w, conv_b, ln_g, ln_b, w_attn_proj, w_conv_proj, b_conv_proj, w_out, g_ffn_norm, w_ffn_in, w_ffn_down, g_final, loss_target, m_g_mix_norm, m_w_in, m_b_in, m_sinks, m_conv_w, m_conv_b, m_ln_g, m_ln_b, m_w_attn_proj, m_w_conv_proj, m_b_conv_proj, m_w_out, m_g_ffn_norm, m_w_ffn_in, m_w_ffn_down, m_g_final, v_g_mix_norm, v_w_in, v_b_in, v_sinks, v_conv_w, v_conv_b, v_ln_g, v_ln_b, v_w_attn_proj, v_w_conv_proj, v_b_conv_proj, v_w_out, v_g_ffn_norm, v_w_ffn_in, v_w_ffn_down, v_g_final):
    w = dict(g_mix_norm=g_mix_norm, w_in=w_in, b_in=b_in, sinks=sinks, conv_w=conv_w, conv_b=conv_b, ln_g=ln_g,
             ln_b=ln_b, w_attn_proj=w_attn_proj, w_conv_proj=w_conv_proj, b_conv_proj=b_conv_proj, w_out=w_out,
             g_ffn_norm=g_ffn_norm, w_ffn_in=w_ffn_in, w_ffn_down=w_ffn_down, g_final=g_final)
    m = dict(g_mix_norm=m_g_mix_norm, w_in=m_w_in, b_in=m_b_in, sinks=m_sinks, conv_w=m_conv_w, conv_b=m_conv_b,
             ln_g=m_ln_g, ln_b=m_ln_b, w_attn_proj=m_w_attn_proj, w_conv_proj=m_w_conv_proj,
             b_conv_proj=m_b_conv_proj, w_out=m_w_out, g_ffn_norm=m_g_ffn_norm, w_ffn_in=m_w_ffn_in,
             w_ffn_down=m_w_ffn_down, g_final=m_g_final)
    v = dict(g_mix_norm=v_g_mix_norm, w_in=v_w_in, b_in=v_b_in, sinks=v_sinks, conv_w=v_conv_w, conv_b=v_conv_b,
             ln_g=v_ln_g, ln_b=v_ln_b, w_attn_proj=v_w_attn_proj, w_conv_proj=v_w_conv_proj,
             b_conv_proj=v_b_conv_proj, w_out=v_w_out, g_ffn_norm=v_g_ffn_norm, w_ffn_in=v_w_ffn_in,
             w_ffn_down=v_w_ffn_down, g_final=v_g_final)
    ax, ay, ac = lax.axis_index("x"), lax.axis_index("y"), lax.axis_index("c")
    me = 4 * ax + 2 * ay + ac
    chip = 2 * ax + ay

    shards = {k: (w[k][0].T if tr else w[k][0]).astype(BF) for k, tr in BIG.items()}
    cw_shard = jnp.pad(conv_w[0].T, ((0, 0), (0, 1))).reshape(16, 128)
    wi_t, cw_full = _all_gather([shards["w_in"], cw_shard])
    conv_full = cw_full.reshape(CONV_CHANNELS, CONV_PAD).T

    as_row = lambda a: a.reshape(1, -1)
    small_w = {k: as_row(w[k]) for k in SMALL_NAMES}
    small_m = {k: as_row(m[k]) for k in SMALL_NAMES}
    small_v = {k: as_row(v[k]) for k in SMALL_NAMES}
    plan = _Plan(shards, ac.reshape(1).astype(jnp.int32))
    dx, parts = _local_step(x[0], loss_target[0], small_w, wi_t, conv_full, plan)

    ids = jnp.stack([me, chip]).astype(jnp.int32)
    grads, delta, new_m, new_v = {}, {}, {}, {}
    for k in BIG:
        tot = _grad_total(f"grad_total_{k}", plan.slots[k], plan.got[k], plan.got3[k], ids)
        tot = tot.T if BIG[k] else tot
        d, nm, nv = _adamw(f"adamw_{k}", w[k][0], tot, m[k][0], v[k][0])
        grads[k], delta[k], new_m[k], new_v[k] = tot[None], d[None], nm[None], nv[None]

    loss, cw_grad, small_out = _small_adamw(_small_all_reduce(parts), small_w, small_m, small_v)
    for k in SMALL_NAMES:
        g, d, nm, nv = (a.reshape(w[k].shape) for a in small_out[k])
        grads[k], delta[k], new_m[k], new_v[k] = g, d, nm, nv
    cw_mine = lax.dynamic_slice(cw_grad, (0, me * 64), (CONV_WIDTH, 64))
    d, nm, nv = _adamw("adamw_conv_w", conv_w[0], cw_mine, m_conv_w[0], v_conv_w[0])
    grads["conv_w"], delta["conv_w"], new_m["conv_w"], new_v["conv_w"] = cw_mine[None], d[None], nm[None], nv[None]

    return (loss.reshape(()), dx[None], *[grads[k] for k in WEIGHT_NAMES], *[delta[k] for k in WEIGHT_NAMES],
            *[new_m[k] for k in WEIGHT_NAMES], *[new_v[k] for k in WEIGHT_NAMES])
```

```python
import functools

import jax
import jax.numpy as jnp
from jax import lax
from jax.experimental import pallas as pl
from jax.experimental.pallas import tpu as pltpu

F32 = jnp.float32
BF = jnp.bfloat16

SEQ = 2048
D_MODEL = 1024
HEAD_DIM = 64
N_Q_HEADS = 8
N_KV_HEADS = 2
GROUP = N_Q_HEADS // N_KV_HEADS
BLOCK = 128
ATTN_WIDTH = 512
KV_WIDTH = 128
CONV_CHANNELS = 512
CONV_WIDTH = 31
CONV_PAD = 32
GLU_OFF = 768
GATE_OFF = 1792
IN_WIDTH = 3840
D_FF = 2816
EPS = 1e-5
NEG = -1e30
N_DEV = 8

ADAM_LR = 0.001
ADAM_B1 = 0.9
ADAM_B2 = 0.999
ADAM_EPS = 1e-08
ADAM_WD = 0.01
ADAM_STEP = 10

VMEM_LIMIT_BYTES = 56 * 1024 * 1024
MESH = pl.DeviceIdType.MESH
ANY = pl.BlockSpec(memory_space=pl.ANY)

_DIMS = {"NN": (((1,), (0,)), ((), ())), "NT": (((1,), (1,)), ((), ())), "TN": (((0,), (0,)), ((), ()))}


def _params(sem):
    return pltpu.CompilerParams(dimension_semantics=sem, vmem_limit_bytes=VMEM_LIMIT_BYTES)


class _Carry:
    def __init__(self, arrays, out_shapes, sems, start, finish):
        self.arrays, self.out_shapes, self.sems, self.start, self.finish = arrays, out_shapes, sems, start, finish


def _carry_io(carry):
    if carry is None:
        return [], [], []
    return list(carry.arrays), list(carry.out_shapes), list(carry.sems)


def _matmul(name, a_list, b, mode, *, m, n, tm, tn, tk=None, epilogue, extra=(), outs, b_off=(0, 0), alias=None,
            scratch=(), carry=None):
    seg_k = [a.shape[0] if mode == "TN" else a.shape[1] for a in a_list]
    whole = tk is None
    seg_nk = [1] * len(a_list) if whole else [ks // tk for ks in seg_k]
    nk = 1 if whole else sum(seg_nk)
    starts = [sum(seg_nk[:s]) for s in range(len(seg_nk))]
    k_starts = [sum(seg_k[:s]) for s in range(len(seg_k))]
    k_tot = sum(seg_k)
    n_a, n_extra, n_out = len(a_list), len(extra), len(outs)

    a_specs = []
    for st, ns, ks in zip(starts, seg_nk, seg_k):
        if mode == "TN":
            a_specs.append(pl.BlockSpec((ks if whole else tk, tm), lambda j, i, k: (k, i)))
        elif whole:
            a_specs.append(pl.BlockSpec((tm, ks), lambda j, i, k: (i, 0)))
        else:
            a_specs.append(pl.BlockSpec((tm, tk), functools.partial(
                lambda j, i, k, st, ns: (i, jnp.clip(k - st, 0, ns - 1)), st=st, ns=ns)))
    bk = k_tot if whole else tk
    if mode == "NT":
        b_spec = pl.BlockSpec((tn, bk), lambda j, i, k: (b_off[0] + j, b_off[1] + k))
    else:
        b_spec = pl.BlockSpec((bk, tn), lambda j, i, k: (b_off[0] + k, b_off[1] + j))
    n_alias = 0 if alias is None else 1
    c_in, c_out, c_sems = _carry_io(carry)
    n_acc = 0 if whole else 1
    nj, ni = n // tn, m // tm

    def body(*refs):
        pos = [n_a, 1, n_alias, n_extra, len(c_in), n_out, len(c_out), n_acc, len(scratch), len(c_sems)]
        cuts = [sum(pos[:q]) for q in range(len(pos) + 1)]
        a_refs, (b_ref,), _, ex, ci_refs, out_refs, co_refs, acc_refs, scr, cs_refs = (
            refs[cuts[q]:cuts[q + 1]] for q in range(len(pos)))
        j, i, k = pl.program_id(0), pl.program_id(1), pl.program_id(2)
        ids = (j, i)
        if carry is not None:
            @pl.when((j == 0) & (i == 0) & (k == 0))
            def _():
                carry.start(ci_refs, co_refs, cs_refs)

        def dot(a_ref, bv):
            return lax.dot_general(a_ref[...].astype(BF), bv.astype(BF), _DIMS[mode], preferred_element_type=F32)

        if whole:
            tot = None
            for a_ref, k0, ks in zip(a_refs, k_starts, seg_k):
                if n_a == 1:
                    bv = b_ref[...]
                else:
                    bv = b_ref[:, k0:k0 + ks] if mode == "NT" else b_ref[k0:k0 + ks, :]
                part = dot(a_ref, bv)
                tot = part if tot is None else tot + part
            epilogue(tot, ex, out_refs, ids, scr)
        else:
            acc, = acc_refs

            @pl.when(k == 0)
            def _():
                acc[...] = jnp.zeros_like(acc)

            for a_ref, st, ns in zip(a_refs, starts, seg_nk):
                if n_a == 1:
                    acc[...] += dot(a_ref, b_ref[...])
                else:
                    @pl.when((k >= st) & (k < st + ns))
                    def _(a_ref=a_ref):
                        acc[...] += dot(a_ref, b_ref[...])

            @pl.when(k == nk - 1)
            def _():
                epilogue(acc[...], ex, out_refs, ids, scr)

        if carry is not None:
            @pl.when((j == nj - 1) & (i == ni - 1) & (k == nk - 1))
            def _():
                carry.finish(ci_refs, co_refs, cs_refs)

    in_specs = [*a_specs, b_spec]
    args = [*a_list, b]
    io_alias = {}
    if alias is not None:
        in_specs.append(pl.BlockSpec(memory_space=pl.ANY))
        args.append(alias[0])
        io_alias = {n_a + 1: alias[1]}
    in_specs += [s for _, s in extra] + [pl.BlockSpec(memory_space=pl.ANY)] * len(c_in)
    args += [x for x, _ in extra] + c_in
    res = pl.pallas_call(
        body, name=name, grid=(nj, ni, nk), in_specs=in_specs,
        out_specs=[s for _, s in outs] + [pl.BlockSpec(memory_space=pl.ANY)] * len(c_out),
        out_shape=[o for o, _ in outs] + c_out,
        scratch_shapes=[*([] if whole else [pltpu.VMEM((tm, tn), F32)]), *scratch, *c_sems],
        input_output_aliases=io_alias,
        compiler_params=_params(("arbitrary", "arbitrary", "arbitrary")),
    )(*args)
    return res if carry is None else (res[:n_out], res[n_out:])


def _tile(tm, tn):
    return pl.BlockSpec((tm, tn), lambda j, i, k: (i, j))


def _row(tn):
    return pl.BlockSpec((1, tn), lambda j, i, k: (0, j))


def _store(dtype):
    def ep(acc, ex, outs, ids, scr):
        outs[0][...] = acc.astype(dtype)
    return ep


def _sds(shape, dtype):
    return jax.ShapeDtypeStruct(shape, dtype)


def _rms_fwd(name, x, g):
    T, D = x.shape
    tm = 256

    def body(x_ref, g_ref, h_ref, r_ref):
        xv = x_ref[...]
        r = lax.rsqrt(jnp.mean(xv * xv, axis=-1, keepdims=True) + EPS)
        h_ref[...] = (xv * r * g_ref[...]).astype(BF)
        r_ref[...] = r

    return pl.pallas_call(
        body, name=name, grid=(T // tm,),
        in_specs=[pl.BlockSpec((tm, D), lambda i: (i, 0)), pl.BlockSpec((1, D), lambda i: (0, 0))],
        out_specs=[pl.BlockSpec((tm, D), lambda i: (i, 0)), pl.BlockSpec((tm, 1), lambda i: (i, 0))],
        out_shape=[_sds((T, D), BF), _sds((T, 1), F32)],
        compiler_params=_params(("arbitrary",)),
    )(x, g)


def _rms_bwd(dh, xv, r, g):
    xh = xv * r
    dxh = dh * g
    dx = r * (dxh - xh * jnp.mean(dxh * xh, axis=-1, keepdims=True))
    return dx, jnp.sum(dh * xh, axis=0, keepdims=True)


def _accumulate_rows(ref, val, first):
    @pl.when(first)
    def _():
        ref[...] = val

    @pl.when(jnp.logical_not(first))
    def _():
        ref[...] += val


def _final(x3, g_final, target):
    T, D = x3.shape
    tm = 256

    def body(x_ref, g_ref, t_ref, dx_ref, dxb_ref, dg_ref, loss_ref):
        i = pl.program_id(0)
        xv = x_ref[...]
        g = g_ref[...]
        r = lax.rsqrt(jnp.mean(xv * xv, axis=-1, keepdims=True) + EPS)
        err = xv * r * g - t_ref[...]
        dy = err * (1.0 / D)
        dx, dg = _rms_bwd(dy, xv, r, g)
        dx_ref[...] = dx
        dxb_ref[...] = dx.astype(BF)
        part = 0.5 * jnp.sum(jnp.mean(err * err, axis=-1, keepdims=True), axis=0, keepdims=True)
        _accumulate_rows(dg_ref, dg, i == 0)
        _accumulate_rows(loss_ref, part, i == 0)

    return pl.pallas_call(
        body, name="final_loss", grid=(T // tm,),
        in_specs=[pl.BlockSpec((tm, D), lambda i: (i, 0)), pl.BlockSpec((1, D), lambda i: (0, 0)),
                  pl.BlockSpec((tm, D), lambda i: (i, 0))],
        out_specs=[pl.BlockSpec((tm, D), lambda i: (i, 0)), pl.BlockSpec((tm, D), lambda i: (i, 0)),
                   pl.BlockSpec((1, D), lambda i: (0, 0)), pl.BlockSpec((1, 1), lambda i: (0, 0))],
        out_shape=[_sds((T, D), F32), _sds((T, D), BF), _sds((1, D), F32), _sds((1, 1), F32)],
        compiler_params=_params(("arbitrary",)),
    )(x3, g_final, target)


def _lane_half(shape, h):
    lane = lax.broadcasted_iota(jnp.int32, shape, 1)
    return (lane >= HEAD_DIM * h) & (lane < HEAD_DIM * (h + 1))


def _to_half(v, w, h):
    if w != h:
        v = pltpu.roll(v, HEAD_DIM, 1)
    return jnp.where(_lane_half(v.shape, h), v, 0.0)


def _attn_block(qkv_ref, sinks_ref, n, h):
    r0 = pl.multiple_of(n * BLOCK, BLOCK)
    p0 = pl.multiple_of(jnp.maximum(n - 1, 0) * BLOCK, BLOCK)
    rows = pl.ds(r0, BLOCK)
    prev = pl.ds(p0, BLOCK)
    k2 = jnp.concatenate([qkv_ref[prev, ATTN_WIDTH:ATTN_WIDTH + KV_WIDTH],
                          qkv_ref[rows, ATTN_WIDTH:ATTN_WIDTH + KV_WIDTH]], axis=0).astype(BF)
    v2 = jnp.concatenate([qkv_ref[prev, ATTN_WIDTH + KV_WIDTH:ATTN_WIDTH + 2 * KV_WIDTH],
                          qkv_ref[rows, ATTN_WIDTH + KV_WIDTH:ATTN_WIDTH + 2 * KV_WIDTH]], axis=0).astype(BF)
    qs = []
    for g in range(GROUP):
        hq = GROUP * h + g
        blk = qkv_ref[rows, (hq // 2) * 128:(hq // 2 + 1) * 128]
        qs.append(_to_half(blk, hq % 2, h))
    q4 = jnp.concatenate(qs, axis=0).astype(BF)
    s = lax.dot_general(q4, k2, _DIMS["NT"], preferred_element_type=F32) * (HEAD_DIM ** -0.5)
    shape = s.shape
    row = lax.broadcasted_iota(jnp.int32, shape, 0)
    qi = row & (BLOCK - 1)
    kj = lax.broadcasted_iota(jnp.int32, shape, 1)
    diff = qi + BLOCK - kj
    valid = (diff >= 0) & (diff < BLOCK) & ((kj >= BLOCK) | (n > 0))
    s = jnp.where(valid, s, NEG)
    row1 = lax.broadcasted_iota(jnp.int32, (shape[0], 1), 0)
    sink = jnp.zeros((shape[0], 1), F32)
    for g in range(GROUP):
        sink = jnp.where((row1 >= g * BLOCK) & (row1 < (g + 1) * BLOCK), sinks_ref[0, GROUP * h + g], sink)
    m = jnp.maximum(jnp.max(s, axis=-1, keepdims=True), sink)
    e = jnp.exp(s - m)
    es = jnp.exp(sink - m)
    inv = 1.0 / (jnp.sum(e, axis=-1, keepdims=True) + es)
    return e * inv, es * inv, q4, k2, v2, rows, prev


def _attn_fwd(proj, sinks):
    T = proj.shape[0]

    def body(qkv_ref, sinks_ref, o_ref):
        def blk(n, carry):
            outs = [None] * (N_Q_HEADS // 2)
            for h in range(N_KV_HEADS):
                p, _, _, _, v2, rows, _ = _attn_block(qkv_ref, sinks_ref, n, h)
                o = lax.dot_general(p.astype(BF), v2, _DIMS["NN"], preferred_element_type=F32)
                for g in range(GROUP):
                    hq = GROUP * h + g
                    piece = jnp.where(_lane_half((BLOCK, 128), h), o[g * BLOCK:(g + 1) * BLOCK], 0.0)
                    if hq % 2 != h:
                        piece = pltpu.roll(piece, HEAD_DIM, 1)
                    outs[hq // 2] = piece if outs[hq // 2] is None else outs[hq // 2] + piece
            for pb in range(N_Q_HEADS // 2):
                o_ref[rows, pb * 128:(pb + 1) * 128] = outs[pb].astype(BF)
            return carry

        lax.fori_loop(0, T // BLOCK, blk, 0)

    return pl.pallas_call(
        body, name="attn_fwd", grid=(1,),
        in_specs=[pl.BlockSpec((T, GLU_OFF), lambda i: (0, 0)), pl.BlockSpec(memory_space=pltpu.SMEM)],
        out_specs=pl.BlockSpec((T, ATTN_WIDTH), lambda i: (0, 0)),
        out_shape=_sds((T, ATTN_WIDTH), BF),
        compiler_params=_params(("arbitrary",)),
    )(proj, sinks)


def _attn_bwd(proj, d_o, sinks, carry=None):
    T = proj.shape[0]
    c_in, c_out, c_sems = _carry_io(carry)

    def body(*refs):
        qkv_ref, do_ref, sinks_ref = refs[:3]
        ci_refs = refs[3:3 + len(c_in)]
        dqkv_ref, dsink_ref = refs[3 + len(c_in):5 + len(c_in)]
        co_refs = refs[5 + len(c_in):5 + len(c_in) + len(c_out)]
        dk_acc, dv_acc = refs[5 + len(c_in) + len(c_out):7 + len(c_in) + len(c_out)]
        cs_refs = refs[7 + len(c_in) + len(c_out):]
        if carry is not None:
            carry.start(ci_refs, co_refs, cs_refs)
        dsink_ref[...] = jnp.zeros_like(dsink_ref)
        dk_acc[...] = jnp.zeros_like(dk_acc)
        dv_acc[...] = jnp.zeros_like(dv_acc)

        def blk(n, carry):
            dqs = [None] * (N_Q_HEADS // 2)
            for h in range(N_KV_HEADS):
                p, psink, q4, k2, v2, rows, prev = _attn_block(qkv_ref, sinks_ref, n, h)
                dos = []
                for g in range(GROUP):
                    hq = GROUP * h + g
                    dos.append(_to_half(do_ref[rows, (hq // 2) * 128:(hq // 2 + 1) * 128].astype(F32), hq % 2, h))
                do4 = jnp.concatenate(dos, axis=0).astype(BF)
                dp = lax.dot_general(do4, v2, _DIMS["NT"], preferred_element_type=F32)
                delta = jnp.sum(p * dp, axis=-1, keepdims=True)
                ds = (p * (dp - delta) * (HEAD_DIM ** -0.5)).astype(BF)
                dsk = psink * delta
                for g in range(GROUP):
                    hq = GROUP * h + g
                    tot = -jnp.sum(dsk[g * BLOCK:(g + 1) * BLOCK], axis=0, keepdims=True)
                    lane = lax.broadcasted_iota(jnp.int32, (1, 128), 1)
                    dsink_ref[...] += jnp.where(lane == hq, tot, 0.0)
                dq = lax.dot_general(ds, k2, _DIMS["NN"], preferred_element_type=F32)
                dk = lax.dot_general(ds, q4, _DIMS["TN"], preferred_element_type=F32)
                dv = lax.dot_general(p.astype(BF), do4, _DIMS["TN"], preferred_element_type=F32)
                dk_acc[prev, :] += dk[:BLOCK]
                dk_acc[rows, :] += dk[BLOCK:]
                dv_acc[prev, :] += dv[:BLOCK]
                dv_acc[rows, :] += dv[BLOCK:]
                for g in range(GROUP):
                    hq = GROUP * h + g
                    piece = jnp.where(_lane_half((BLOCK, 128), h), dq[g * BLOCK:(g + 1) * BLOCK], 0.0)
                    if hq % 2 != h:
                        piece = pltpu.roll(piece, HEAD_DIM, 1)
                    dqs[hq // 2] = piece if dqs[hq // 2] is None else dqs[hq // 2] + piece
            for pb in range(N_Q_HEADS // 2):
                dqkv_ref[rows, pb * 128:(pb + 1) * 128] = dqs[pb].astype(BF)
            return carry

        lax.fori_loop(0, T // BLOCK, blk, 0)
        dqkv_ref[:, ATTN_WIDTH:ATTN_WIDTH + KV_WIDTH] = dk_acc[...].astype(BF)
        dqkv_ref[:, ATTN_WIDTH + KV_WIDTH:] = dv_acc[...].astype(BF)
        if carry is not None:
            carry.finish(ci_refs, co_refs, cs_refs)

    res = pl.pallas_call(
        body, name="attn_bwd", grid=(1,),
        in_specs=[pl.BlockSpec((T, GLU_OFF), lambda i: (0, 0)), pl.BlockSpec((T, ATTN_WIDTH), lambda i: (0, 0)),
                  pl.BlockSpec(memory_space=pltpu.SMEM), *[ANY] * len(c_in)],
        out_specs=[pl.BlockSpec((T, GLU_OFF), lambda i: (0, 0)), pl.BlockSpec((1, 128), lambda i: (0, 0)),
                   *[ANY] * len(c_out)],
        out_shape=[_sds((T, GLU_OFF), BF), _sds((1, 128), F32), *c_out],
        scratch_shapes=[pltpu.VMEM((T, KV_WIDTH), F32), pltpu.VMEM((T, KV_WIDTH), F32), *c_sems],
        compiler_params=_params(("arbitrary",)),
    )(proj, d_o, sinks, *c_in)
    return res[:2], res[2:]


CHUNK = 256
SUB = 32
WIN = CHUNK + 32
PAD_ROWS = SEQ + 2 * CONV_PAD
_GLU_SPECS = [pl.BlockSpec((SEQ, 256), functools.partial(lambda i, c: (0, c), c=GLU_OFF // 256 + c)) for c in range(4)]


def _glu_to_pad(a0, a1, b0, b1, zpad):
    C = CONV_CHANNELS
    zpad[0:CONV_PAD, :] = jnp.zeros((CONV_PAD, C), F32)
    zpad[CONV_PAD + SEQ:, :] = jnp.zeros((CONV_PAD, C), F32)
    zpad[CONV_PAD:CONV_PAD + SEQ, 0:256] = a0[...] * jax.nn.sigmoid(b0[...])
    zpad[CONV_PAD:CONV_PAD + SEQ, 256:C] = a1[...] * jax.nn.sigmoid(b1[...])


def _tap_windows(src, base, win):
    for b in range(8):
        win[b, 0:WIN - 8, :] = src[base + b:base + b + WIN - 8, :]


def _taps(win, w_ref, init, out, flip):
    def sub(si, carry):
        r0 = pl.multiple_of(si * SUB, SUB)
        acc = jnp.broadcast_to(init, (SUB, CONV_CHANNELS))
        for k in range(CONV_WIDTH):
            wk = (CONV_WIDTH - 1 - k) if flip else k
            acc = acc + w_ref[wk:wk + 1, :] * win[k % 8, pl.ds(r0 + 8 * (k // 8), SUB), :]
        out[pl.ds(r0, SUB), :] = acc
        return carry

    lax.fori_loop(0, CHUNK // SUB, sub, 0)


def _tap_grads(win, du, dwacc):
    def sub(si, carry):
        r0 = pl.multiple_of(si * SUB, SUB)
        d = du[pl.ds(r0, SUB), :]
        for k in range(CONV_WIDTH):
            p = d * win[k % 8, pl.ds(r0 + 8 * (k // 8), SUB), :]
            dwacc[8 * k:8 * k + 8, :] += (p[0:8] + p[8:16]) + (p[16:24] + p[24:32])
        return carry

    lax.fori_loop(0, CHUNK // SUB, sub, 0)


def _ln_parts(u):
    mu = jnp.mean(u, axis=-1, keepdims=True)
    xc = u - mu
    rstd = lax.rsqrt(jnp.mean(xc * xc, axis=-1, keepdims=True) + EPS)
    return xc * rstd, rstd


def _conv_fwd(proj, conv_w, conv_b, ln_g, ln_b, carry=None):
    T, C = proj.shape[0], CONV_CHANNELS
    vec = pl.BlockSpec((1, C), lambda i: (0, 0))
    c_in, c_out, c_sems = _carry_io(carry)

    def body(*refs):
        a0, a1, b0, b1, w_ref, cb_ref, g_ref, be_ref = refs[:8]
        ci_refs = refs[8:8 + len(c_in)]
        c_ref = refs[8 + len(c_in)]
        co_refs = refs[9 + len(c_in):9 + len(c_in) + len(c_out)]
        zpad, win, ubuf = refs[9 + len(c_in) + len(c_out):12 + len(c_in) + len(c_out)]
        cs_refs = refs[12 + len(c_in) + len(c_out):]
        if carry is not None:
            carry.start(ci_refs, co_refs, cs_refs)
        _glu_to_pad(a0, a1, b0, b1, zpad)
        for ci in range(T // CHUNK):
            _tap_windows(zpad, ci * CHUNK + CONV_PAD - (CONV_WIDTH - 1), win)
            _taps(win, w_ref, cb_ref[...], ubuf, False)
            xh, _ = _ln_parts(ubuf[...])
            ln = xh * g_ref[...] + be_ref[...]
            c_ref[ci * CHUNK:(ci + 1) * CHUNK, :] = (ln * jax.nn.sigmoid(ln)).astype(BF)
        if carry is not None:
            carry.finish(ci_refs, co_refs, cs_refs)

    res = pl.pallas_call(
        body, name="conv_fwd", grid=(1,),
        in_specs=[*_GLU_SPECS, pl.BlockSpec((CONV_PAD, C), lambda i: (0, 0)), vec, vec, vec, *[ANY] * len(c_in)],
        out_specs=[pl.BlockSpec((T, C), lambda i: (0, 0)), *[ANY] * len(c_out)],
        out_shape=[_sds((T, C), BF), *c_out],
        scratch_shapes=[pltpu.VMEM((PAD_ROWS, C), F32), pltpu.VMEM((8, WIN, C), F32), pltpu.VMEM((CHUNK, C), F32),
                        *c_sems],
        compiler_params=_params(("arbitrary",)),
    )(proj, proj, proj, proj, conv_w, conv_b, ln_g, ln_b, *c_in)
    return res[0], res[1:]


def _conv_bwd(proj, d_c, conv_w, conv_b, ln_g, ln_b, carry=None):
    T, C = proj.shape[0], CONV_CHANNELS
    vec = pl.BlockSpec((1, C), lambda i: (0, 0))
    wspec = pl.BlockSpec((CONV_PAD, C), lambda i: (0, 0))
    c_in, c_out, c_sems = _carry_io(carry)

    def body(*refs):
        a0, a1, b0, b1, dc_ref, w_ref, cb_ref, g_ref, be_ref = refs[:9]
        ci_refs = refs[9:9 + len(c_in)]
        o = 9 + len(c_in)
        dglu_ref, dw_ref, dcb_ref, dg_ref, dbe_ref = refs[o:o + 5]
        co_refs = refs[o + 5:o + 5 + len(c_out)]
        zpad, dupad, win, ubuf, dwacc = refs[o + 5 + len(c_out):o + 10 + len(c_out)]
        cs_refs = refs[o + 10 + len(c_out):]
        if carry is not None:
            carry.start(ci_refs, co_refs, cs_refs)
        _glu_to_pad(a0, a1, b0, b1, zpad)
        dupad[T:, :] = jnp.zeros((2 * CONV_PAD, C), F32)
        dwacc[...] = jnp.zeros_like(dwacc)
        dcb_ref[...] = jnp.zeros_like(dcb_ref)
        dg_ref[...] = jnp.zeros_like(dg_ref)
        dbe_ref[...] = jnp.zeros_like(dbe_ref)
        for ci in range(T // CHUNK):
            rows = slice(ci * CHUNK, (ci + 1) * CHUNK)
            _tap_windows(zpad, ci * CHUNK + CONV_PAD - (CONV_WIDTH - 1), win)
            _taps(win, w_ref, cb_ref[...], ubuf, False)
            xh, rstd = _ln_parts(ubuf[...])
            ln = xh * g_ref[...] + be_ref[...]
            sg = jax.nn.sigmoid(ln)
            dln = dc_ref[rows, :].astype(F32) * (sg * (1.0 + ln * (1.0 - sg)))
            dg_ref[...] += jnp.sum(dln * xh, axis=0, keepdims=True)
            dbe_ref[...] += jnp.sum(dln, axis=0, keepdims=True)
            dxh = dln * g_ref[...]
            du = rstd * (dxh - jnp.mean(dxh, axis=-1, keepdims=True)
                         - xh * jnp.mean(dxh * xh, axis=-1, keepdims=True))
            dupad[rows, :] = du
            dcb_ref[...] += jnp.sum(du, axis=0, keepdims=True)
            _tap_grads(win, dupad.at[rows, :], dwacc)
        for k in range(CONV_WIDTH):
            dw_ref[k:k + 1, :] = jnp.sum(dwacc[8 * k:8 * k + 8, :], axis=0, keepdims=True)
        dw_ref[CONV_WIDTH:, :] = jnp.zeros((CONV_PAD - CONV_WIDTH, C), F32)
        for ci in range(T // CHUNK):
            rows = slice(ci * CHUNK, (ci + 1) * CHUNK)
            _tap_windows(dupad, ci * CHUNK, win)
            _taps(win, w_ref, jnp.zeros((1, C), F32), ubuf, True)
            dz = ubuf[...]
            for half, (a, b) in enumerate(((a0, b0), (a1, b1))):
                sb = jax.nn.sigmoid(b[rows, :])
                dzh = dz[:, half * 256:(half + 1) * 256]
                dglu_ref[rows, half * 256:(half + 1) * 256] = (dzh * sb).astype(BF)
                dglu_ref[rows, C + half * 256:C + (half + 1) * 256] = (dzh * a[rows, :] * sb * (1.0 - sb)).astype(BF)
        if carry is not None:
            carry.finish(ci_refs, co_refs, cs_refs)

    res = pl.pallas_call(
        body, name="conv_bwd", grid=(1,),
        in_specs=[*_GLU_SPECS, pl.BlockSpec((T, C), lambda i: (0, 0)), wspec, vec, vec, vec, *[ANY] * len(c_in)],
        out_specs=[pl.BlockSpec((T, 2 * C), lambda i: (0, 0)), wspec, vec, vec, vec, *[ANY] * len(c_out)],
        out_shape=[_sds((T, 2 * C), BF), _sds((CONV_PAD, C), F32), _sds((1, C), F32), _sds((1, C), F32),
                   _sds((1, C), F32), *c_out],
        scratch_shapes=[pltpu.VMEM((PAD_ROWS, C), F32), pltpu.VMEM((PAD_ROWS, C), F32), pltpu.VMEM((8, WIN, C), F32),
                        pltpu.VMEM((CHUNK, C), F32), pltpu.VMEM((8 * CONV_PAD, C), F32), *c_sems],
        compiler_params=_params(("arbitrary",)),
    )(proj, proj, proj, proj, d_c, conv_w, conv_b, ln_g, ln_b, *c_in)
    return res[:5], res[5:]


_GATE_BLK = GATE_OFF // 256


def _ffn_in_swiglu(h2, wf_t, carry=None):
    T, D = h2.shape
    tm, tn = 512, D_FF // 2
    nj, ni = D_FF // tn, T // tm
    c_in, c_out, c_sems = _carry_io(carry)

    def body(*refs):
        a_ref, bg_ref, bu_ref = refs[:3]
        ci_refs = refs[3:3 + len(c_in)]
        act_ref, g_ref, u_ref = refs[3 + len(c_in):6 + len(c_in)]
        co_refs = refs[6 + len(c_in):6 + len(c_in) + len(c_out)]
        cs_refs = refs[6 + len(c_in) + len(c_out):]
        j, i = pl.program_id(0), pl.program_id(1)
        if carry is not None:
            @pl.when((j == 0) & (i == 0))
            def _():
                carry.start(ci_refs, co_refs, cs_refs)
        a = a_ref[...]
        g = lax.dot_general(a, bg_ref[...], _DIMS["NT"], preferred_element_type=F32)
        u = lax.dot_general(a, bu_ref[...], _DIMS["NT"], preferred_element_type=F32)
        act_ref[...] = (g * jax.nn.sigmoid(g) * u).astype(BF)
        g_ref[...] = g.astype(BF)
        u_ref[...] = u.astype(BF)
        if carry is not None:
            @pl.when((j == nj - 1) & (i == ni - 1))
            def _():
                carry.finish(ci_refs, co_refs, cs_refs)

    t = pl.BlockSpec((tm, tn), lambda j, i: (i, j))
    res = pl.pallas_call(
        body, name="ffn_in_swiglu", grid=(nj, ni),
        in_specs=[pl.BlockSpec((tm, D), lambda j, i: (i, 0)), pl.BlockSpec((tn, D), lambda j, i: (j, 0)),
                  pl.BlockSpec((tn, D), lambda j, i: (nj + j, 0)), *[ANY] * len(c_in)],
        out_specs=[t, t, t, *[ANY] * len(c_out)], out_shape=[*[_sds((T, D_FF), BF)] * 3, *c_out],
        scratch_shapes=c_sems,
        compiler_params=_params(("arbitrary", "arbitrary")),
    )(h2, wf_t, wf_t, *c_in)
    return res[:3], res[3:]


def _local_step(x, target, small, wi_t, conv_w, plan):
    T, D = x.shape
    tm = 1024

    def carried(call, res, carry):
        if carry is None:
            return res
        outs, got = res
        plan.done(call, got)
        return outs

    h, r1 = _rms_fwd("rms_mix", x, small["g_mix_norm"])

    def ep_add(acc, ex, outs, ids, scr):
        outs[0][...] = acc + ex[0][...]

    tn_in = IN_WIDTH // 3
    carry = plan.carry("proj_in")
    proj, = carried("proj_in", _matmul("proj_in", [h], wi_t, "NT", m=T, n=IN_WIDTH, tm=tm, tn=tn_in, epilogue=ep_add,
                                       extra=[(small["b_in"], _row(tn_in))],
                                       outs=[(_sds((T, IN_WIDTH), F32), _tile(tm, tn_in))], carry=carry), carry)
    o = _attn_fwd(proj, small["sinks"])
    c, got = _conv_fwd(proj, conv_w, small["conv_b"], small["ln_g"], small["ln_b"], carry=plan.carry("conv_fwd"))
    plan.done("conv_fwd", got)
    wap_t, wcp_t, w_out = plan.weight("w_attn_proj"), plan.weight("w_conv_proj"), plan.weight("w_out")
    ya, = _matmul("attn_proj", [o], wap_t, "NT", m=T, n=D, tm=tm, tn=D, epilogue=_store(F32),
                  outs=[(_sds((T, D), F32), _tile(tm, D))])

    tg = 256
    gate_specs = [pl.BlockSpec((tm, tg), lambda j, i, k: (i, _GATE_BLK + j)),
                  pl.BlockSpec((tm, tg), lambda j, i, k: (i, _GATE_BLK + D // tg + j))]

    def ep_merge(acc, ex, outs, ids, scr):
        yc = acc + ex[0][...]
        outs[0][...] = yc
        outs[1][...] = (jax.nn.sigmoid(ex[2][...]) * ex[1][...] + jax.nn.sigmoid(ex[3][...]) * yc).astype(BF)

    yc, merged = _matmul(
        "conv_proj_merge", [c], wcp_t, "NT", m=T, n=D, tm=tm, tn=tg, epilogue=ep_merge,
        extra=[(small["b_conv_proj"], _row(tg)), (ya, _tile(tm, tg)), (proj, gate_specs[0]), (proj, gate_specs[1])],
        outs=[(_sds((T, D), F32), _tile(tm, tg)), (_sds((T, D), BF), _tile(tm, tg))])
    x2, = _matmul("out_proj", [merged], w_out, "NN", m=T, n=D, tm=tm, tn=D, epilogue=ep_add,
                  extra=[(x, _tile(tm, D))], outs=[(_sds((T, D), F32), _tile(tm, D))])
    h2, r2 = _rms_fwd("rms_ffn", x2, small["g_ffn_norm"])
    wf_t = plan.weight("w_ffn_in")
    (act, gate, up), got = _ffn_in_swiglu(h2, wf_t, carry=plan.carry("ffn_in_swiglu"))
    plan.done("ffn_in_swiglu", got)
    w_down = plan.weight("w_ffn_down")
    x3, = _matmul("ffn_down", [act], w_down, "NN", m=T, n=D, tm=512, tn=D, epilogue=ep_add,
                  extra=[(x2, _tile(512, D))], outs=[(_sds((T, D), F32), _tile(512, D))])
    dx3, dx3_b, dg_final, loss = _final(x3, small["g_final"], target)

    tn_ff = D_FF // 2

    def ep_swiglu_bwd(acc, ex, outs, ids, scr):
        g, u = ex[0][...].astype(F32), ex[1][...].astype(F32)
        sg = jax.nn.sigmoid(g)
        outs[0][...] = (acc * u * sg * (1.0 + g * (1.0 - sg))).astype(BF)
        outs[1][...] = (acc * g * sg).astype(BF)

    dgate, dup = _matmul(
        "ffn_down_bwd", [dx3_b], w_down, "NT", m=T, n=D_FF, tm=512, tn=tn_ff, epilogue=ep_swiglu_bwd,
        extra=[(gate, _tile(512, tn_ff)), (up, _tile(512, tn_ff))],
        outs=[(_sds((T, D_FF), BF), _tile(512, tn_ff)), (_sds((T, D_FF), BF), _tile(512, tn_ff))])

    def dw(name, a, b, rows, cols, row_off=0, alias=None, total_rows=None, colsum=False):
        total_rows = rows if total_rows is None else total_rows
        tmw = rows if rows <= 1024 else D_FF // 2
        by_dma = row_off % tmw != 0

        def ep(acc, ex, outs, ids, scr):
            if by_dma:
                scr[0][...] = acc.astype(BF)
                pltpu.sync_copy(scr[0], outs[0].at[pl.ds(pl.multiple_of(row_off + ids[1] * tmw, 256), tmw)])
            else:
                outs[0][...] = acc.astype(BF)
            if colsum:
                outs[1][...] = jnp.sum(ex[0][...].astype(F32), axis=0, keepdims=True)

        blk = row_off // tmw
        spec = pl.BlockSpec(memory_space=pl.ANY) if by_dma else pl.BlockSpec((tmw, cols), lambda j, i, k: (blk + i, j))
        outs = [(_sds((total_rows, cols), BF), spec)]
        extra = []
        if colsum:
            extra = [(a, pl.BlockSpec((T, tmw), lambda j, i, k: (0, i)))]
            outs.append((_sds((1, rows), F32), pl.BlockSpec((1, tmw), lambda j, i, k: (0, i))))
        res = _matmul(name, [a], b, "TN", m=rows, n=cols, tm=tmw, tn=cols, epilogue=ep, extra=extra, outs=outs,
                      alias=None if alias is None else (alias, 0),
                      scratch=[pltpu.VMEM((tmw, cols), BF)] if by_dma else [])
        return res if colsum else res[0]

    plan.grad_ready(dict(w_ffn_down=dw("ffn_down_dw", act, dx3_b, D_FF, D)))

    def ep_rms_bwd(acc, ex, outs, ids, scr):
        dx, dg = _rms_bwd(acc, ex[0][...], ex[1][...], ex[2][...])
        dx = ex[3][...] + dx
        outs[0][...] = dx
        outs[1][...] = dx.astype(BF)
        _accumulate_rows(outs[2], dg, ids[1] == 0)

    def rms_bwd_io(tm_, xin, r, g, dres):
        return dict(
            extra=[(xin, _tile(tm_, D)), (r, pl.BlockSpec((tm_, 1), lambda j, i, k: (i, 0))), (g, _row(D)),
                   (dres, _tile(tm_, D))],
            outs=[(_sds((T, D), F32), _tile(tm_, D)), (_sds((T, D), BF), _tile(tm_, D)), (_sds((1, D), F32), _row(D))])

    carry = plan.carry("ffn_in_bwd")
    dx2, dx2_b, dg_ffn = carried(
        "ffn_in_bwd",
        _matmul("ffn_in_bwd", [dgate, dup], wf_t, "NN", m=T, n=D, tm=512, tn=D, tk=D_FF, epilogue=ep_rms_bwd,
                carry=carry, **rms_bwd_io(512, x2, r2, small["g_ffn_norm"], dx3)), carry)
    gwf_t = dw("ffn_in_dw_gate", dgate, h2, D_FF, D, total_rows=2 * D_FF)
    gwf_t = dw("ffn_in_dw_up", dup, h2, D_FF, D, row_off=D_FF, alias=gwf_t, total_rows=2 * D_FF)
    plan.grad_ready(dict(w_ffn_in=gwf_t))

    def ep_merge_bwd(acc, ex, outs, ids, scr):
        s0 = jax.nn.sigmoid(ex[2][...])
        s1 = jax.nn.sigmoid(ex[3][...])
        outs[0][...] = (acc * s0).astype(BF)
        outs[1][...] = (acc * s1).astype(BF)
        outs[2][...] = (acc * ex[0][...] * s0 * (1.0 - s0)).astype(BF)
        outs[3][...] = (acc * ex[1][...] * s1 * (1.0 - s1)).astype(BF)

    carry = plan.carry("out_proj_bwd_merge")
    dya, dyc, dg0, dg1 = carried(
        "out_proj_bwd_merge",
        _matmul("out_proj_bwd_merge", [dx2_b], w_out, "NT", m=T, n=D, tm=tm, tn=tg, epilogue=ep_merge_bwd,
                extra=[(ya, _tile(tm, tg)), (yc, _tile(tm, tg)), (proj, gate_specs[0]), (proj, gate_specs[1])],
                outs=[(_sds((T, D), BF), _tile(tm, tg))] * 4, carry=carry), carry)
    gw_out = dw("out_proj_dw", merged, dx2_b, D, D)
    d_o, = _matmul("attn_proj_bwd", [dya], wap_t, "NN", m=T, n=ATTN_WIDTH, tm=tm, tn=ATTN_WIDTH,
                   epilogue=_store(BF), outs=[(_sds((T, ATTN_WIDTH), BF), _tile(tm, ATTN_WIDTH))])
    d_c, = _matmul("conv_proj_bwd", [dyc], wcp_t, "NN", m=T, n=CONV_CHANNELS, tm=tm, tn=CONV_CHANNELS,
                   epilogue=_store(BF), outs=[(_sds((T, CONV_CHANNELS), BF), _tile(tm, CONV_CHANNELS))])
    gwap_t = dw("attn_proj_dw", dya, o, D, ATTN_WIDTH)
    gwcp_t, db_cp = dw("conv_proj_dw", dyc, c, D, CONV_CHANNELS, colsum=True)
    plan.grad_ready(dict(w_out=gw_out, w_attn_proj=gwap_t, w_conv_proj=gwcp_t))
    (dglu, dcw, dcb, dlng, dlnb), got = _conv_bwd(proj, d_c, conv_w, small["conv_b"], small["ln_g"], small["ln_b"],
                                                  carry=plan.carry("conv_bwd"))
    plan.done("conv_bwd", got)
    (dqkv, dsinks), got = _attn_bwd(proj, d_o, small["sinks"], carry=plan.carry("attn_bwd"))
    plan.done("attn_bwd", got)

    segs = [dqkv, dglu, dg0, dg1]
    gwi_t, off, db_in = None, 0, []
    for s, seg in enumerate(segs):
        gwi_t, db = dw(f"proj_in_dw{s}", seg, h, seg.shape[1], D, row_off=off, alias=gwi_t, total_rows=IN_WIDTH,
                       colsum=True)
        db_in.append(db)
        off += seg.shape[1]
    plan.grad_ready(dict(w_in=gwi_t))
    plan.alone("swap_inp")
    carry = plan.carry("proj_in_bwd")
    dx, _, dg_mix = carried(
        "proj_in_bwd",
        _matmul("proj_in_bwd", segs, wi_t, "NN", m=T, n=D, tm=512, tn=D, epilogue=ep_rms_bwd, carry=carry,
                **rms_bwd_io(512, x, r1, small["g_mix_norm"], dx2)), carry)

    parts = dict(g_mix_norm=dg_mix, b_in=db_in, sinks=dsinks, conv_w=dcw, conv_b=dcb, ln_g=dlng, ln_b=dlnb,
                 b_conv_proj=db_cp, g_ffn_norm=dg_ffn, g_final=dg_final, loss=loss)
    return dx, parts


def _place():
    x, y, c = lax.axis_index("x"), lax.axis_index("y"), lax.axis_index("c")
    return x, y, c, [(1 - x, y), (x, 1 - y), (1 - x, 1 - y)]


def _gather_copies(x_refs, out_refs, rows_per, send_sems, recv_sems, local_sems):
    x, y, c, chips = _place()
    me, sibling = (x, y, c), (x, y, 1 - c)

    def rows(a, px, py, pc):
        return out_refs[a].at[pl.ds((4 * px + 2 * py + pc) * rows_per[a], rows_per[a])]

    def copy(a, k, block, to, src=None):
        return pltpu.make_async_remote_copy(
            src_ref=rows(a, *block) if src is None else src, dst_ref=rows(a, *block),
            send_sem=send_sems.at[7 * a + k], recv_sem=recv_sems.at[7 * a + k], device_id=to, device_id_type=MESH)

    def local(a):
        return pltpu.make_async_copy(x_refs[a], rows(a, *me), local_sems.at[a])

    def first(a):
        return [copy(a, 0, me, sibling, src=x_refs[a])] + [copy(a, 1 + j, me, (*chip, c), src=x_refs[a])
                                                          for j, chip in enumerate(chips)]

    def arrive(a, j):
        return copy(a, 1 + j, (*chips[j], c), me)

    def passed(a, j):
        return copy(a, 4 + j, (*chips[j], c), sibling)

    def from_sibling(a):
        return [copy(a, 0, sibling, me)] + [copy(a, 4 + j, (*chip, 1 - c), me) for j, chip in enumerate(chips)]

    return len(x_refs), local, first, arrive, passed, from_sibling


def _gather_start(*refs):
    n, local, first, _, _, _ = _gather_copies(*refs)
    for a in range(n):
        local(a).start()
        for cp in first(a):
            cp.start()


def _gather_finish(*refs):
    n, local, first, arrive, passed, from_sibling = _gather_copies(*refs)
    for a in range(n):
        for j in range(3):
            arrive(a, j).wait_recv()
            passed(a, j).start()
    for a in range(n):
        for cp in from_sibling(a):
            cp.wait_recv()
    for a in range(n):
        for cp in first(a) + [passed(a, j) for j in range(3)]:
            cp.wait_send()
        local(a).wait()


def _gather_blocks(*refs):
    _gather_start(*refs)
    _gather_finish(*refs)


def _gather_sems(n):
    return [pltpu.SemaphoreType.DMA((7 * n,)), pltpu.SemaphoreType.DMA((7 * n,)), pltpu.SemaphoreType.DMA((n,))]


def _gather_carry(shards):
    rows_per = [s.shape[0] for s in shards]
    return _Carry(shards, [_sds((N_DEV * s.shape[0],) + s.shape[1:], s.dtype) for s in shards],
                  _gather_sems(len(shards)),
                  lambda ins, outs, sems: _gather_start(ins, outs, rows_per, *sems),
                  lambda ins, outs, sems: _gather_finish(ins, outs, rows_per, *sems))


def _all_gather(shards):
    return _run_carry("weights_all_gather", _gather_carry(shards))


def _swap_carry(grads):
    n = len(grads)

    def copies(g_refs, out_refs, sems):
        send_sems, recv_sems = sems
        x, y, c, _ = _place()
        return [pltpu.make_async_remote_copy(
            src_ref=g_refs[a].at[2 * p + 1 - c], dst_ref=out_refs[a].at[p],
            send_sem=send_sems.at[4 * a + p], recv_sem=recv_sems.at[4 * a + p],
            device_id=(x, y, 1 - c), device_id_type=MESH) for a in range(n) for p in range(4)]

    def start(ins, outs, sems):
        for cp in copies(ins, outs, sems):
            cp.start()

    def finish(ins, outs, sems):
        for cp in copies(ins, outs, sems):
            cp.wait()

    return _Carry(grads, [_sds((4,) + g.shape[1:], g.dtype) for g in grads],
                  [pltpu.SemaphoreType.DMA((4 * n,)), pltpu.SemaphoreType.DMA((4 * n,))], start, finish)


def _join(carries):
    carries = [c for c in carries if c is not None]
    if not carries:
        return None
    n_in = [len(c.arrays) for c in carries]
    n_out = [len(c.out_shapes) for c in carries]
    n_sem = [len(c.sems) for c in carries]

    def parts(refs, counts):
        cuts = [sum(counts[:q]) for q in range(len(counts) + 1)]
        return [refs[cuts[q]:cuts[q + 1]] for q in range(len(counts))]

    def start(ins, outs, sems):
        for c, i, o, s in zip(carries, parts(ins, n_in), parts(outs, n_out), parts(sems, n_sem)):
            c.start(i, o, s)

    def finish(ins, outs, sems):
        for c, i, o, s in zip(carries, parts(ins, n_in), parts(outs, n_out), parts(sems, n_sem)):
            c.finish(i, o, s)

    return _Carry([a for c in carries for a in c.arrays], [o for c in carries for o in c.out_shapes],
                  [s for c in carries for s in c.sems], start, finish)


def _run_carry(name, carry):
    n_in, n_out = len(carry.arrays), len(carry.out_shapes)

    def body(*refs):
        carry.start(refs[:n_in], refs[n_in:n_in + n_out], refs[n_in + n_out:])
        carry.finish(refs[:n_in], refs[n_in:n_in + n_out], refs[n_in + n_out:])

    return pl.pallas_call(body, name=name, in_specs=[ANY] * n_in, out_specs=[ANY] * n_out,
                          out_shape=carry.out_shapes, scratch_shapes=carry.sems)(*carry.arrays)


def _chip_sum(name, g, got, c):
    _, rows, cols = g.shape

    def body(c_ref, g_ref, got_ref, o_ref):
        o_ref[...] = (g_ref[...].astype(F32) + got_ref[...].astype(F32)).astype(BF)

    return pl.pallas_call(
        body, name=name,
        grid_spec=pltpu.PrefetchScalarGridSpec(
            num_scalar_prefetch=1, grid=(4,),
            in_specs=[pl.BlockSpec((1, rows, cols), lambda p, c_ref: (2 * p + c_ref[0], 0, 0)),
                      pl.BlockSpec((1, rows, cols), lambda p, c_ref: (p, 0, 0))],
            out_specs=pl.BlockSpec((1, rows, cols), lambda p, c_ref: (p, 0, 0))),
        out_shape=_sds((4, rows, cols), BF),
        compiler_params=_params(("arbitrary",)),
    )(c, g, got)


def _send_carry(sums):
    n = len(sums)

    def copies(s_refs, out_refs, sems):
        send_sems, recv_sems = sems
        x, y, c, chips = _place()
        return [pltpu.make_async_remote_copy(
            src_ref=s_refs[a].at[2 * px + py], dst_ref=out_refs[a].at[k],
            send_sem=send_sems.at[3 * a + k], recv_sem=recv_sems.at[3 * a + k],
            device_id=(px, py, c), device_id_type=MESH) for a in range(n) for k, (px, py) in enumerate(chips)]

    def start(ins, outs, sems):
        for cp in copies(ins, outs, sems):
            cp.start()

    def finish(ins, outs, sems):
        for cp in copies(ins, outs, sems):
            cp.wait()

    return _Carry(sums, [_sds((3,) + s.shape[1:], s.dtype) for s in sums],
                  [pltpu.SemaphoreType.DMA((3 * n,)), pltpu.SemaphoreType.DMA((3 * n,))], start, finish)


def _grad_total(name, g, got, got3, ids):
    _, rows, cols = g.shape

    def body(ids_ref, g_ref, got_ref, got3_ref, o_ref):
        tot = g_ref[0].astype(F32) + got_ref[0].astype(F32)
        for k in range(3):
            tot = tot + got3_ref[k].astype(F32)
        o_ref[...] = tot

    return pl.pallas_call(
        body, name=name,
        grid_spec=pltpu.PrefetchScalarGridSpec(
            num_scalar_prefetch=1, grid=(1,),
            in_specs=[pl.BlockSpec((1, rows, cols), lambda i, ids_ref: (ids_ref[0], 0, 0)),
                      pl.BlockSpec((1, rows, cols), lambda i, ids_ref: (ids_ref[1], 0, 0)),
                      pl.BlockSpec((3, rows, cols), lambda i, ids_ref: (0, 0, 0))],
            out_specs=pl.BlockSpec((rows, cols), lambda i, ids_ref: (0, 0))),
        out_shape=_sds((rows, cols), F32),
        compiler_params=_params(("arbitrary",)),
    )(ids, g, got, got3)


def _adam_math(w, g, m, v):
    m = ADAM_B1 * m + (1.0 - ADAM_B1) * g
    v = ADAM_B2 * v + (1.0 - ADAM_B2) * (g * g)
    m_hat = m / (1.0 - ADAM_B1 ** ADAM_STEP)
    v_hat = v / (1.0 - ADAM_B2 ** ADAM_STEP)
    delta = -ADAM_LR * (m_hat / (jnp.sqrt(v_hat) + ADAM_EPS) + ADAM_WD * w)
    return delta, m, v


def _adamw(name, w, g, m, v):
    rows, cols = w.shape
    tr = 256 if rows % 256 == 0 else rows

    def body(w_ref, g_ref, m_ref, v_ref, d_ref, nm_ref, nv_ref):
        d_ref[...], nm_ref[...], nv_ref[...] = _adam_math(w_ref[...], g_ref[...], m_ref[...], v_ref[...])

    t = pl.BlockSpec((tr, cols), lambda i: (i, 0))
    return pl.pallas_call(
        body, name=name, grid=(rows // tr,), in_specs=[t] * 4, out_specs=[t] * 3,
        out_shape=[_sds((rows, cols), F32)] * 3, compiler_params=_params(("arbitrary",)),
    )(w, g, m, v)


SMALL_NAMES = ["g_mix_norm", "b_in", "sinks", "conv_b", "ln_g", "ln_b", "b_conv_proj", "g_ffn_norm", "g_final"]
_PACK_ROWS = 32


def _small_all_reduce(parts):
    C = CONV_CHANNELS
    part_list = [parts["g_mix_norm"], *parts["b_in"], parts["sinks"], parts["conv_b"], parts["ln_g"], parts["ln_b"],
                 parts["b_conv_proj"], parts["g_ffn_norm"], parts["g_final"], parts["loss"], parts["conv_w"]]
    n_part = len(part_list)

    def body(*refs):
        (p_mix, p_b0, p_b1, p_b2, p_b3, p_sink, p_cb, p_lg, p_lb, p_bcp, p_ffn, p_fin, p_loss, p_cw) = refs[:n_part]
        tot_ref, pack, gathered, send_sems, recv_sems, local_sems = refs[n_part:]
        pack[...] = jnp.zeros_like(pack)
        pack[0:1, :] = p_mix[...]
        pack[1:2, 0:GLU_OFF] = p_b0[...]
        pack[2:3, :] = p_b1[...]
        pack[3:4, :] = p_b2[...]
        pack[4:5, :] = p_b3[...]
        pack[5:6, 0:128] = p_sink[...]
        pack[6:7, 0:C] = p_cb[...]
        pack[6:7, C:2 * C] = p_lg[...]
        pack[7:8, 0:C] = p_lb[...]
        pack[8:9, :] = p_bcp[...]
        pack[9:10, :] = p_ffn[...]
        pack[10:11, :] = p_fin[...]
        pack[11:12, 0:128] = jnp.broadcast_to(p_loss[...], (1, 128))
        pack[12:28, 0:C] = p_cw[0:16, :]
        pack[12:28, C:2 * C] = p_cw[16:32, :]
        _gather_blocks([pack], [gathered], [_PACK_ROWS], send_sems, recv_sems, local_sems)
        tot = gathered[0:_PACK_ROWS, :]
        for d in range(1, N_DEV):
            tot = tot + gathered[d * _PACK_ROWS:(d + 1) * _PACK_ROWS, :]
        tot_ref[...] = tot

    vm = pl.BlockSpec(memory_space=pltpu.VMEM)
    return pl.pallas_call(
        body, name="small_all_reduce",
        in_specs=[vm] * n_part, out_specs=vm, out_shape=_sds((_PACK_ROWS, D_MODEL), F32),
        scratch_shapes=[pltpu.VMEM((_PACK_ROWS, D_MODEL), F32), pltpu.VMEM((N_DEV * _PACK_ROWS, D_MODEL), F32),
                        pltpu.SemaphoreType.DMA((7,)), pltpu.SemaphoreType.DMA((7,)), pltpu.SemaphoreType.DMA((1,))],
        compiler_params=pltpu.CompilerParams(vmem_limit_bytes=VMEM_LIMIT_BYTES),
    )(*part_list)


def _small_adamw(tot, small_w, small_m, small_v):
    C = CONV_CHANNELS
    names = SMALL_NAMES
    widths = [small_w[k].shape[1] for k in names]
    n_small = len(names)

    def body(*refs):
        tot_ref = refs[0]
        w_refs = refs[1:1 + n_small]
        m_refs = refs[1 + n_small:1 + 2 * n_small]
        v_refs = refs[1 + 2 * n_small:1 + 3 * n_small]
        o = 1 + 3 * n_small
        loss_ref, cw_ref = refs[o], refs[o + 1]
        out_refs = refs[o + 2:o + 2 + 4 * n_small]
        tot = tot_ref[...]
        loss_ref[...] = tot[11:12, 0:1]
        cw_ref[0:16, :] = tot[12:28, 0:C]
        cw_ref[16:32, :] = tot[12:28, C:2 * C]
        grads = dict(
            g_mix_norm=tot[0:1, :],
            b_in=jnp.concatenate([tot[1:2, 0:GLU_OFF], tot[2:3, :], tot[3:4, :], tot[4:5, :]], axis=1),
            sinks=tot[5:6, 0:N_Q_HEADS], conv_b=tot[6:7, 0:C], ln_g=tot[6:7, C:2 * C], ln_b=tot[7:8, 0:C],
            b_conv_proj=tot[8:9, :], g_ffn_norm=tot[9:10, :], g_final=tot[10:11, :])
        for s, k in enumerate(names):
            g = grads[k]
            d, nm, nv = _adam_math(w_refs[s][...], g, m_refs[s][...], v_refs[s][...])
            out_refs[4 * s][...] = g
            out_refs[4 * s + 1][...] = d
            out_refs[4 * s + 2][...] = nm
            out_refs[4 * s + 3][...] = nv

    vm = pl.BlockSpec(memory_space=pltpu.VMEM)
    args = [tot, *[small_w[k] for k in names], *[small_m[k] for k in names], *[small_v[k] for k in names]]
    out_shape = [_sds((1, 1), F32), _sds((CONV_PAD, C), F32)]
    for wd in widths:
        out_shape += [_sds((1, wd), F32)] * 4
    res = pl.pallas_call(
        body, name="small_adamw",
        in_specs=[vm] * len(args), out_specs=[vm] * len(out_shape), out_shape=out_shape,
        compiler_params=pltpu.CompilerParams(vmem_limit_bytes=VMEM_LIMIT_BYTES),
    )(*args)
    return res[0], res[1], {k: res[2 + 4 * s:6 + 4 * s] for s, k in enumerate(names)}


BIG = dict(w_in=True, w_attn_proj=True, w_conv_proj=True, w_out=False, w_ffn_in=True, w_ffn_down=False)
WEIGHT_NAMES = ["g_mix_norm", "w_in", "b_in", "sinks", "conv_w", "conv_b", "ln_g", "ln_b", "w_attn_proj",
                "w_conv_proj", "b_conv_proj", "w_out", "g_ffn_norm", "w_ffn_in", "w_ffn_down", "g_final"]


class _Plan:
    GROUPS = dict(down=["w_ffn_down"], ffn=["w_ffn_in"], mix=["w_out", "w_attn_proj", "w_conv_proj"], inp=["w_in"])
    RIDES = dict(
        proj_in=[("gather", ["w_attn_proj", "w_conv_proj", "w_out"])], conv_fwd=[("gather", ["w_ffn_in"])],
        ffn_in_swiglu=[("gather", ["w_ffn_down"])],
        ffn_in_bwd=[("swap", "down")], out_proj_bwd_merge=[("send", "down"), ("swap", "ffn")],
        conv_bwd=[("send", "ffn"), ("swap", "mix")], attn_bwd=[("send", "mix")],
        swap_inp=[("swap", "inp")], proj_in_bwd=[("send", "inp")])

    def __init__(self, shards, c1):
        self.shards, self.c1 = shards, c1
        self.full, self.slots, self.got, self.sums, self.got3 = {}, {}, {}, {}, {}

    def weight(self, name):
        return self.full[name]

    def grad_ready(self, grads):
        for k, g in grads.items():
            self.slots[k] = g.reshape(N_DEV, g.shape[0] // N_DEV, g.shape[1])

    def _one(self, kind, what):
        if kind == "gather":
            return _gather_carry([self.shards[k] for k in what])
        names = self.GROUPS[what]
        return _swap_carry([self.slots[k] for k in names]) if kind == "swap" else _send_carry([self.sums[k] for k in names])

    def carry(self, call):
        return _join([self._one(kind, what) for kind, what in self.RIDES[call]])

    def done(self, call, outs):
        outs = list(outs)
        for kind, what in self.RIDES[call]:
            names = what if kind == "gather" else self.GROUPS[what]
            mine, outs = outs[:len(names)], outs[len(names):]
            if kind == "gather":
                self.full.update(zip(names, mine))
            elif kind == "send":
                self.got3.update(zip(names, mine))
            else:
                for k, r in zip(names, mine):
                    self.got[k] = r
                    self.sums[k] = _chip_sum(f"chip_sum_{k}", self.slots[k], r, self.c1)

    def alone(self, call):
        self.done(call, _run_carry(call, self.carry(call)))


def kernel(x, g_mix_norm, w_in, b_in, sinks, conv_w, conv_b, ln_g, ln_b, w_attn_proj, w_conv_proj, b_conv_proj, w_out, g_ffn_norm, w_ffn_in, w_ffn_down, g_final, loss_target, m_g_mix_norm, m_w_in, m_b_in, m_sinks, m_conv_w, m_conv_b, m_ln_g, m_ln_b, m_w_attn_proj, m_w_conv_proj, m_b_conv_proj, m_w_out, m_g_ffn_norm, m_w_ffn_in, m_w_ffn_down, m_g_final, v_g_mix_norm, v_w_in, v_b_in, v_sinks, v_conv_w, v_conv_b, v_ln_g, v_ln_b, v_w_attn_proj, v_w_conv_proj, v_b_conv_proj, v_w_out, v_g_ffn_norm, v_w_ffn_in, v_w_ffn_down, v_g_final):
    w = dict(g_mix_norm=g_mix_norm, w_in=w_in, b_in=b_in, sinks=sinks, conv_w=conv_w, conv_b=conv_b, ln_g=ln_g,
             ln_b=ln_b, w_attn_proj=w_attn_proj, w_conv_proj=w_conv_proj, b_conv_proj=b_conv_proj, w_out=w_out,
             g_ffn_norm=g_ffn_norm, w_ffn_in=w_ffn_in, w_ffn_down=w_ffn_down, g_final=g_final)
    m = dict(g_mix_norm=m_g_mix_norm, w_in=m_w_in, b_in=m_b_in, sinks=m_sinks, conv_w=m_conv_w, conv_b=m_conv_b,
             ln_g=m_ln_g, ln_b=m_ln_b, w_attn_proj=m_w_attn_proj, w_conv_proj=m_w_conv_proj,
             b_conv_proj=m_b_conv_proj, w_out=m_w_out, g_ffn_norm=m_g_ffn_norm, w_ffn_in=m_w_ffn_in,
             w_ffn_down=m_w_ffn_down, g_final=m_g_final)
    v = dict(g_mix_norm=v_g_mix_norm, w_in=v_w_in, b_in=v_b_in, sinks=v_sinks, conv_w=v_conv_w, conv_b=v_conv_b,
             ln_g=v_ln_g, ln_b=v_ln_b, w_attn_proj=v_w_attn_proj, w_conv_proj=v_w_conv_proj,
             b_conv_proj=v_b_conv_proj, w_out=v_w_out, g_ffn_norm=v_g_ffn_norm, w_ffn_in=v_w_ffn_in,
             w_ffn_down=v_w_ffn_down, g_final=v_g_final)
    ax, ay, ac = lax.axis_index("x"), lax.axis_index("y"), lax.axis_index("c")
    me = 4 * ax + 2 * ay + ac
    chip = 2 * ax + ay

    shards = {k: (w[k][0].T if tr else w[k][0]).astype(BF) for k, tr in BIG.items()}
    cw_shard = jnp.pad(conv_w[0].T, ((0, 0), (0, 1))).reshape(16, 128)
    wi_t, cw_full = _all_gather([shards["w_in"], cw_shard])
    conv_full = cw_full.reshape(CONV_CHANNELS, CONV_PAD).T

    as_row = lambda a: a.reshape(1, -1)
    small_w = {k: as_row(w[k]) for k in SMALL_NAMES}
    small_m = {k: as_row(m[k]) for k in SMALL_NAMES}
    small_v = {k: as_row(v[k]) for k in SMALL_NAMES}
    plan = _Plan(shards, ac.reshape(1).astype(jnp.int32))
    dx, parts = _local_step(x[0], loss_target[0], small_w, wi_t, conv_full, plan)

    ids = jnp.stack([me, chip]).astype(jnp.int32)
    grads, delta, new_m, new_v = {}, {}, {}, {}
    for k in BIG:
        tot = _grad_total(f"grad_total_{k}", plan.slots[k], plan.got[k], plan.got3[k], ids)
        tot = tot.T if BIG[k] else tot
        d, nm, nv = _adamw(f"adamw_{k}", w[k][0], tot, m[k][0], v[k][0])
        grads[k], delta[k], new_m[k], new_v[k] = tot[None], d[None], nm[None], nv[None]

    loss, cw_grad, small_out = _small_adamw(_small_all_reduce(parts), small_w, small_m, small_v)
    for k in SMALL_NAMES:
        g, d, nm, nv = (a.reshape(w[k].shape) for a in small_out[k])
        grads[k], delta[k], new_m[k], new_v[k] = g, d, nm, nv
    cw_mine = lax.dynamic_slice(cw_grad, (0, me * 64), (CONV_WIDTH, 64))
    d, nm, nv = _adamw("adamw_conv_w", conv_w[0], cw_mine, m_conv_w[0], v_conv_w[0])
    grads["conv_w"], delta["conv_w"], new_m["conv_w"], new_v["conv_w"] = cw_mine[None], d[None], nm[None], nv[None]

    return (loss.reshape(()), dx[None], *[grads[k] for k in WEIGHT_NAMES], *[delta[k] for k in WEIGHT_NAMES],
            *[new_m[k] for k in WEIGHT_NAMES], *[new_v[k] for k in WEIGHT_NAMES])
```

```python
import functools

import jax
import jax.numpy as jnp
from jax import lax
from jax.experimental import pallas as pl
from jax.experimental.pallas import tpu as pltpu

F32 = jnp.float32
BF = jnp.bfloat16

SEQ = 2048
D_MODEL = 1024
HEAD_DIM = 64
N_Q_HEADS = 8
N_KV_HEADS = 2
GROUP = N_Q_HEADS // N_KV_HEADS
BLOCK = 128
ATTN_WIDTH = 512
KV_WIDTH = 128
CONV_CHANNELS = 512
CONV_WIDTH = 31
CONV_PAD = 32
GLU_OFF = 768
GATE_OFF = 1792
IN_WIDTH = 3840
D_FF = 2816
EPS = 1e-5
NEG = -1e30
N_DEV = 8

ADAM_LR = 0.001
ADAM_B1 = 0.9
ADAM_B2 = 0.999
ADAM_EPS = 1e-08
ADAM_WD = 0.01
ADAM_STEP = 10

VMEM_LIMIT_BYTES = 56 * 1024 * 1024
MESH = pl.DeviceIdType.MESH
ANY = pl.BlockSpec(memory_space=pl.ANY)

_DIMS = {"NN": (((1,), (0,)), ((), ())), "NT": (((1,), (1,)), ((), ())), "TN": (((0,), (0,)), ((), ()))}


def _params(sem):
    return pltpu.CompilerParams(dimension_semantics=sem, vmem_limit_bytes=VMEM_LIMIT_BYTES)


class _Carry:
    def __init__(self, arrays, out_shapes, sems, start, finish):
        self.arrays, self.out_shapes, self.sems, self.start, self.finish = arrays, out_shapes, sems, start, finish


def _carry_io(carry):
    if carry is None:
        return [], [], []
    return list(carry.arrays), list(carry.out_shapes), list(carry.sems)


def _matmul(name, a_list, b, mode, *, m, n, tm, tn, tk=None, epilogue, extra=(), outs, b_off=(0, 0), alias=None,
            scratch=(), carry=None):
    seg_k = [a.shape[0] if mode == "TN" else a.shape[1] for a in a_list]
    whole = tk is None
    seg_nk = [1] * len(a_list) if whole else [ks // tk for ks in seg_k]
    nk = 1 if whole else sum(seg_nk)
    starts = [sum(seg_nk[:s]) for s in range(len(seg_nk))]
    k_starts = [sum(seg_k[:s]) for s in range(len(seg_k))]
    k_tot = sum(seg_k)
    n_a, n_extra, n_out = len(a_list), len(extra), len(outs)

    a_specs = []
    for st, ns, ks in zip(starts, seg_nk, seg_k):
        if mode == "TN":
            a_specs.append(pl.BlockSpec((ks if whole else tk, tm), lambda j, i, k: (k, i)))
        elif whole:
            a_specs.append(pl.BlockSpec((tm, ks), lambda j, i, k: (i, 0)))
        else:
            a_specs.append(pl.BlockSpec((tm, tk), functools.partial(
                lambda j, i, k, st, ns: (i, jnp.clip(k - st, 0, ns - 1)), st=st, ns=ns)))
    bk = k_tot if whole else tk
    if mode == "NT":
        b_spec = pl.BlockSpec((tn, bk), lambda j, i, k: (b_off[0] + j, b_off[1] + k))
    else:
        b_spec = pl.BlockSpec((bk, tn), lambda j, i, k: (b_off[0] + k, b_off[1] + j))
    n_alias = 0 if alias is None else 1
    c_in, c_out, c_sems = _carry_io(carry)
    n_acc = 0 if whole else 1
    nj, ni = n // tn, m // tm

    def body(*refs):
        pos = [n_a, 1, n_alias, n_extra, len(c_in), n_out, len(c_out), n_acc, len(scratch), len(c_sems)]
        cuts = [sum(pos[:q]) for q in range(len(pos) + 1)]
        a_refs, (b_ref,), _, ex, ci_refs, out_refs, co_refs, acc_refs, scr, cs_refs = (
            refs[cuts[q]:cuts[q + 1]] for q in range(len(pos)))
        j, i, k = pl.program_id(0), pl.program_id(1), pl.program_id(2)
        ids = (j, i)
        if carry is not None:
            @pl.when((j == 0) & (i == 0) & (k == 0))
            def _():
                carry.start(ci_refs, co_refs, cs_refs)

        def dot(a_ref, bv):
            return lax.dot_general(a_ref[...].astype(BF), bv.astype(BF), _DIMS[mode], preferred_element_type=F32)

        if whole:
            tot = None
            for a_ref, k0, ks in zip(a_refs, k_starts, seg_k):
                if n_a == 1:
                    bv = b_ref[...]
                else:
                    bv = b_ref[:, k0:k0 + ks] if mode == "NT" else b_ref[k0:k0 + ks, :]
                part = dot(a_ref, bv)
                tot = part if tot is None else tot + part
            epilogue(tot, ex, out_refs, ids, scr)
        else:
            acc, = acc_refs

            @pl.when(k == 0)
            def _():
                acc[...] = jnp.zeros_like(acc)

            for a_ref, st, ns in zip(a_refs, starts, seg_nk):
                if n_a == 1:
                    acc[...] += dot(a_ref, b_ref[...])
                else:
                    @pl.when((k >= st) & (k < st + ns))
                    def _(a_ref=a_ref):
                        acc[...] += dot(a_ref, b_ref[...])

            @pl.when(k == nk - 1)
            def _():
                epilogue(acc[...], ex, out_refs, ids, scr)

        if carry is not None:
            @pl.when((j == nj - 1) & (i == ni - 1) & (k == nk - 1))
            def _():
                carry.finish(ci_refs, co_refs, cs_refs)

    in_specs = [*a_specs, b_spec]
    args = [*a_list, b]
    io_alias = {}
    if alias is not None:
        in_specs.append(pl.BlockSpec(memory_space=pl.ANY))
        args.append(alias[0])
        io_alias = {n_a + 1: alias[1]}
    in_specs += [s for _, s in extra] + [pl.BlockSpec(memory_space=pl.ANY)] * len(c_in)
    args += [x for x, _ in extra] + c_in
    res = pl.pallas_call(
        body, name=name, grid=(nj, ni, nk), in_specs=in_specs,
        out_specs=[s for _, s in outs] + [pl.BlockSpec(memory_space=pl.ANY)] * len(c_out),
        out_shape=[o for o, _ in outs] + c_out,
        scratch_shapes=[*([] if whole else [pltpu.VMEM((tm, tn), F32)]), *scratch, *c_sems],
        input_output_aliases=io_alias,
        compiler_params=_params(("arbitrary", "arbitrary", "arbitrary")),
    )(*args)
    return res if carry is None else (res[:n_out], res[n_out:])


def _tile(tm, tn):
    return pl.BlockSpec((tm, tn), lambda j, i, k: (i, j))


def _row(tn):
    return pl.BlockSpec((1, tn), lambda j, i, k: (0, j))


def _store(dtype):
    def ep(acc, ex, outs, ids, scr):
        outs[0][...] = acc.astype(dtype)
    return ep


def _sds(shape, dtype):
    return jax.ShapeDtypeStruct(shape, dtype)


def _rms_fwd(name, x, g):
    T, D = x.shape
    tm = 512

    def body(x_ref, g_ref, h_ref, r_ref):
        xv = x_ref[...]
        r = lax.rsqrt(jnp.mean(xv * xv, axis=-1, keepdims=True) + EPS)
        h_ref[...] = (xv * r * g_ref[...]).astype(BF)
        r_ref[...] = r

    return pl.pallas_call(
        body, name=name, grid=(T // tm,),
        in_specs=[pl.BlockSpec((tm, D), lambda i: (i, 0)), pl.BlockSpec((1, D), lambda i: (0, 0))],
        out_specs=[pl.BlockSpec((tm, D), lambda i: (i, 0)), pl.BlockSpec((tm, 1), lambda i: (i, 0))],
        out_shape=[_sds((T, D), BF), _sds((T, 1), F32)],
        compiler_params=_params(("arbitrary",)),
    )(x, g)


def _rms_bwd(dh, xv, r, g):
    xh = xv * r
    dxh = dh * g
    dx = r * (dxh - xh * jnp.mean(dxh * xh, axis=-1, keepdims=True))
    return dx, jnp.sum(dh * xh, axis=0, keepdims=True)


def _accumulate_rows(ref, val, first):
    @pl.when(first)
    def _():
        ref[...] = val

    @pl.when(jnp.logical_not(first))
    def _():
        ref[...] += val


def _final(x3, g_final, target):
    T, D = x3.shape
    tm = 512

    def body(x_ref, g_ref, t_ref, dx_ref, dxb_ref, dg_ref, loss_ref):
        i = pl.program_id(0)
        xv = x_ref[...]
        g = g_ref[...]
        r = lax.rsqrt(jnp.mean(xv * xv, axis=-1, keepdims=True) + EPS)
        err = xv * r * g - t_ref[...]
        dy = err * (1.0 / D)
        dx, dg = _rms_bwd(dy, xv, r, g)
        dx_ref[...] = dx
        dxb_ref[...] = dx.astype(BF)
        part = 0.5 * jnp.sum(jnp.mean(err * err, axis=-1, keepdims=True), axis=0, keepdims=True)
        _accumulate_rows(dg_ref, dg, i == 0)
        _accumulate_rows(loss_ref, part, i == 0)

    return pl.pallas_call(
        body, name="final_loss", grid=(T // tm,),
        in_specs=[pl.BlockSpec((tm, D), lambda i: (i, 0)), pl.BlockSpec((1, D), lambda i: (0, 0)),
                  pl.BlockSpec((tm, D), lambda i: (i, 0))],
        out_specs=[pl.BlockSpec((tm, D), lambda i: (i, 0)), pl.BlockSpec((tm, D), lambda i: (i, 0)),
                   pl.BlockSpec((1, D), lambda i: (0, 0)), pl.BlockSpec((1, 1), lambda i: (0, 0))],
        out_shape=[_sds((T, D), F32), _sds((T, D), BF), _sds((1, D), F32), _sds((1, 1), F32)],
        compiler_params=_params(("arbitrary",)),
    )(x3, g_final, target)


def _lane_half(shape, h):
    lane = lax.broadcasted_iota(jnp.int32, shape, 1)
    return (lane >= HEAD_DIM * h) & (lane < HEAD_DIM * (h + 1))


def _to_half(v, w, h):
    if w != h:
        v = pltpu.roll(v, HEAD_DIM, 1)
    return jnp.where(_lane_half(v.shape, h), v, 0.0)


def _attn_block(qkv_ref, sinks_ref, n, h):
    r0 = pl.multiple_of(n * BLOCK, BLOCK)
    p0 = pl.multiple_of(jnp.maximum(n - 1, 0) * BLOCK, BLOCK)
    rows = pl.ds(r0, BLOCK)
    prev = pl.ds(p0, BLOCK)
    k2 = jnp.concatenate([qkv_ref[prev, ATTN_WIDTH:ATTN_WIDTH + KV_WIDTH],
                          qkv_ref[rows, ATTN_WIDTH:ATTN_WIDTH + KV_WIDTH]], axis=0).astype(BF)
    v2 = jnp.concatenate([qkv_ref[prev, ATTN_WIDTH + KV_WIDTH:ATTN_WIDTH + 2 * KV_WIDTH],
                          qkv_ref[rows, ATTN_WIDTH + KV_WIDTH:ATTN_WIDTH + 2 * KV_WIDTH]], axis=0).astype(BF)
    qs = []
    for g in range(GROUP):
        hq = GROUP * h + g
        blk = qkv_ref[rows, (hq // 2) * 128:(hq // 2 + 1) * 128]
        qs.append(_to_half(blk, hq % 2, h))
    q4 = jnp.concatenate(qs, axis=0).astype(BF)
    s = lax.dot_general(q4, k2, _DIMS["NT"], preferred_element_type=F32) * (HEAD_DIM ** -0.5)
    shape = s.shape
    row = lax.broadcasted_iota(jnp.int32, shape, 0)
    qi = row & (BLOCK - 1)
    kj = lax.broadcasted_iota(jnp.int32, shape, 1)
    diff = qi + BLOCK - kj
    valid = (diff >= 0) & (diff < BLOCK) & ((kj >= BLOCK) | (n > 0))
    s = jnp.where(valid, s, NEG)
    row1 = lax.broadcasted_iota(jnp.int32, (shape[0], 1), 0)
    sink = jnp.zeros((shape[0], 1), F32)
    for g in range(GROUP):
        sink = jnp.where((row1 >= g * BLOCK) & (row1 < (g + 1) * BLOCK), sinks_ref[0, GROUP * h + g], sink)
    m = jnp.maximum(jnp.max(s, axis=-1, keepdims=True), sink)
    e = jnp.exp(s - m)
    es = jnp.exp(sink - m)
    inv = 1.0 / (jnp.sum(e, axis=-1, keepdims=True) + es)
    return e * inv, es * inv, q4, k2, v2, rows, prev


def _attn_fwd(proj, sinks, carry=None):
    T = proj.shape[0]
    c_in, c_out, c_sems = _carry_io(carry)

    def body(*refs):
        qkv_ref, sinks_ref = refs[:2]
        ci_refs = refs[2:2 + len(c_in)]
        o_ref = refs[2 + len(c_in)]
        co_refs = refs[3 + len(c_in):3 + len(c_in) + len(c_out)]
        cs_refs = refs[3 + len(c_in) + len(c_out):]
        if carry is not None:
            carry.start(ci_refs, co_refs, cs_refs)

        def blk(n, z):
            outs = [None] * (N_Q_HEADS // 2)
            for h in range(N_KV_HEADS):
                p, _, _, _, v2, rows, _ = _attn_block(qkv_ref, sinks_ref, n, h)
                o = lax.dot_general(p.astype(BF), v2, _DIMS["NN"], preferred_element_type=F32)
                for g in range(GROUP):
                    hq = GROUP * h + g
                    piece = jnp.where(_lane_half((BLOCK, 128), h), o[g * BLOCK:(g + 1) * BLOCK], 0.0)
                    if hq % 2 != h:
                        piece = pltpu.roll(piece, HEAD_DIM, 1)
                    outs[hq // 2] = piece if outs[hq // 2] is None else outs[hq // 2] + piece
            for pb in range(N_Q_HEADS // 2):
                o_ref[rows, pb * 128:(pb + 1) * 128] = outs[pb].astype(BF)
            return z

        lax.fori_loop(0, T // BLOCK, blk, 0)
        if carry is not None:
            carry.finish(ci_refs, co_refs, cs_refs)

    res = pl.pallas_call(
        body, name="attn_fwd", grid=(1,),
        in_specs=[pl.BlockSpec((T, GLU_OFF), lambda i: (0, 0)), pl.BlockSpec(memory_space=pltpu.SMEM),
                  *[ANY] * len(c_in)],
        out_specs=[pl.BlockSpec((T, ATTN_WIDTH), lambda i: (0, 0)), *[ANY] * len(c_out)],
        out_shape=[_sds((T, ATTN_WIDTH), BF), *c_out], scratch_shapes=c_sems,
        compiler_params=_params(("arbitrary",)),
    )(proj, sinks, *c_in)
    return res[0], res[1:]


def _attn_bwd(proj, d_o, sinks, carry=None):
    T = proj.shape[0]
    c_in, c_out, c_sems = _carry_io(carry)

    def body(*refs):
        qkv_ref, do_ref, sinks_ref = refs[:3]
        ci_refs = refs[3:3 + len(c_in)]
        dqkv_ref, dsink_ref = refs[3 + len(c_in):5 + len(c_in)]
        co_refs = refs[5 + len(c_in):5 + len(c_in) + len(c_out)]
        dk_acc, dv_acc = refs[5 + len(c_in) + len(c_out):7 + len(c_in) + len(c_out)]
        cs_refs = refs[7 + len(c_in) + len(c_out):]
        if carry is not None:
            carry.start(ci_refs, co_refs, cs_refs)
        dsink_ref[...] = jnp.zeros_like(dsink_ref)
        dk_acc[...] = jnp.zeros_like(dk_acc)
        dv_acc[...] = jnp.zeros_like(dv_acc)

        def blk(n, carry):
            dqs = [None] * (N_Q_HEADS // 2)
            for h in range(N_KV_HEADS):
                p, psink, q4, k2, v2, rows, prev = _attn_block(qkv_ref, sinks_ref, n, h)
                dos = []
                for g in range(GROUP):
                    hq = GROUP * h + g
                    dos.append(_to_half(do_ref[rows, (hq // 2) * 128:(hq // 2 + 1) * 128].astype(F32), hq % 2, h))
                do4 = jnp.concatenate(dos, axis=0).astype(BF)
                dp = lax.dot_general(do4, v2, _DIMS["NT"], preferred_element_type=F32)
                delta = jnp.sum(p * dp, axis=-1, keepdims=True)
                ds = (p * (dp - delta) * (HEAD_DIM ** -0.5)).astype(BF)
                dsk = psink * delta
                for g in range(GROUP):
                    hq = GROUP * h + g
                    tot = -jnp.sum(dsk[g * BLOCK:(g + 1) * BLOCK], axis=0, keepdims=True)
                    lane = lax.broadcasted_iota(jnp.int32, (1, 128), 1)
                    dsink_ref[...] += jnp.where(lane == hq, tot, 0.0)
                dq = lax.dot_general(ds, k2, _DIMS["NN"], preferred_element_type=F32)
                dk = lax.dot_general(ds, q4, _DIMS["TN"], preferred_element_type=F32)
                dv = lax.dot_general(p.astype(BF), do4, _DIMS["TN"], preferred_element_type=F32)
                dk_acc[prev, :] += dk[:BLOCK]
                dk_acc[rows, :] += dk[BLOCK:]
                dv_acc[prev, :] += dv[:BLOCK]
                dv_acc[rows, :] += dv[BLOCK:]
                for g in range(GROUP):
                    hq = GROUP * h + g
                    piece = jnp.where(_lane_half((BLOCK, 128), h), dq[g * BLOCK:(g + 1) * BLOCK], 0.0)
                    if hq % 2 != h:
                        piece = pltpu.roll(piece, HEAD_DIM, 1)
                    dqs[hq // 2] = piece if dqs[hq // 2] is None else dqs[hq // 2] + piece
            for pb in range(N_Q_HEADS // 2):
                dqkv_ref[rows, pb * 128:(pb + 1) * 128] = dqs[pb].astype(BF)
            return carry

        lax.fori_loop(0, T // BLOCK, blk, 0)
        dqkv_ref[:, ATTN_WIDTH:ATTN_WIDTH + KV_WIDTH] = dk_acc[...].astype(BF)
        dqkv_ref[:, ATTN_WIDTH + KV_WIDTH:] = dv_acc[...].astype(BF)
        if carry is not None:
            carry.finish(ci_refs, co_refs, cs_refs)

    res = pl.pallas_call(
        body, name="attn_bwd", grid=(1,),
        in_specs=[pl.BlockSpec((T, GLU_OFF), lambda i: (0, 0)), pl.BlockSpec((T, ATTN_WIDTH), lambda i: (0, 0)),
                  pl.BlockSpec(memory_space=pltpu.SMEM), *[ANY] * len(c_in)],
        out_specs=[pl.BlockSpec((T, GLU_OFF), lambda i: (0, 0)), pl.BlockSpec((1, 128), lambda i: (0, 0)),
                   *[ANY] * len(c_out)],
        out_shape=[_sds((T, GLU_OFF), BF), _sds((1, 128), F32), *c_out],
        scratch_shapes=[pltpu.VMEM((T, KV_WIDTH), F32), pltpu.VMEM((T, KV_WIDTH), F32), *c_sems],
        compiler_params=_params(("arbitrary",)),
    )(proj, d_o, sinks, *c_in)
    return res[:2], res[2:]


CHUNK = 256
SUB = 32
WIN = CHUNK + 32
PAD_ROWS = SEQ + 2 * CONV_PAD
_GLU_SPECS = [pl.BlockSpec((SEQ, 256), functools.partial(lambda i, c: (0, c), c=GLU_OFF // 256 + c)) for c in range(4)]


def _glu_to_pad(a0, a1, b0, b1, zpad):
    C = CONV_CHANNELS
    zpad[0:CONV_PAD, :] = jnp.zeros((CONV_PAD, C), F32)
    zpad[CONV_PAD + SEQ:, :] = jnp.zeros((CONV_PAD, C), F32)
    zpad[CONV_PAD:CONV_PAD + SEQ, 0:256] = a0[...] * jax.nn.sigmoid(b0[...])
    zpad[CONV_PAD:CONV_PAD + SEQ, 256:C] = a1[...] * jax.nn.sigmoid(b1[...])


def _tap_windows(src, base, win):
    for b in range(8):
        win[b, 0:WIN - 8, :] = src[base + b:base + b + WIN - 8, :]


def _taps(win, w_ref, init, out, flip):
    def sub(si, carry):
        r0 = pl.multiple_of(si * SUB, SUB)
        acc = jnp.broadcast_to(init, (SUB, CONV_CHANNELS))
        for k in range(CONV_WIDTH):
            wk = (CONV_WIDTH - 1 - k) if flip else k
            acc = acc + w_ref[wk:wk + 1, :] * win[k % 8, pl.ds(r0 + 8 * (k // 8), SUB), :]
        out[pl.ds(r0, SUB), :] = acc
        return carry

    lax.fori_loop(0, CHUNK // SUB, sub, 0)


def _tap_grads(win, du, dwacc):
    def sub(si, carry):
        r0 = pl.multiple_of(si * SUB, SUB)
        d = du[pl.ds(r0, SUB), :]
        for k in range(CONV_WIDTH):
            p = d * win[k % 8, pl.ds(r0 + 8 * (k // 8), SUB), :]
            dwacc[8 * k:8 * k + 8, :] += (p[0:8] + p[8:16]) + (p[16:24] + p[24:32])
        return carry

    lax.fori_loop(0, CHUNK // SUB, sub, 0)


def _ln_parts(u):
    mu = jnp.mean(u, axis=-1, keepdims=True)
    xc = u - mu
    rstd = lax.rsqrt(jnp.mean(xc * xc, axis=-1, keepdims=True) + EPS)
    return xc * rstd, rstd


def _conv_fwd(proj, conv_w, conv_b, ln_g, ln_b, carry=None):
    T, C = proj.shape[0], CONV_CHANNELS
    vec = pl.BlockSpec((1, C), lambda i: (0, 0))
    c_in, c_out, c_sems = _carry_io(carry)

    def body(*refs):
        a0, a1, b0, b1, w_ref, cb_ref, g_ref, be_ref = refs[:8]
        ci_refs = refs[8:8 + len(c_in)]
        c_ref = refs[8 + len(c_in)]
        co_refs = refs[9 + len(c_in):9 + len(c_in) + len(c_out)]
        zpad, win, ubuf = refs[9 + len(c_in) + len(c_out):12 + len(c_in) + len(c_out)]
        cs_refs = refs[12 + len(c_in) + len(c_out):]
        if carry is not None:
            carry.start(ci_refs, co_refs, cs_refs)
        _glu_to_pad(a0, a1, b0, b1, zpad)
        for ci in range(T // CHUNK):
            _tap_windows(zpad, ci * CHUNK + CONV_PAD - (CONV_WIDTH - 1), win)
            _taps(win, w_ref, cb_ref[...], ubuf, False)
            xh, _ = _ln_parts(ubuf[...])
            ln = xh * g_ref[...] + be_ref[...]
            c_ref[ci * CHUNK:(ci + 1) * CHUNK, :] = (ln * jax.nn.sigmoid(ln)).astype(BF)
        if carry is not None:
            carry.finish(ci_refs, co_refs, cs_refs)

    res = pl.pallas_call(
        body, name="conv_fwd", grid=(1,),
        in_specs=[*_GLU_SPECS, pl.BlockSpec((CONV_PAD, C), lambda i: (0, 0)), vec, vec, vec, *[ANY] * len(c_in)],
        out_specs=[pl.BlockSpec((T, C), lambda i: (0, 0)), *[ANY] * len(c_out)],
        out_shape=[_sds((T, C), BF), *c_out],
        scratch_shapes=[pltpu.VMEM((PAD_ROWS, C), F32), pltpu.VMEM((8, WIN, C), F32), pltpu.VMEM((CHUNK, C), F32),
                        *c_sems],
        compiler_params=_params(("arbitrary",)),
    )(proj, proj, proj, proj, conv_w, conv_b, ln_g, ln_b, *c_in)
    return res[0], res[1:]


def _conv_bwd(proj, d_c, conv_w, conv_b, ln_g, ln_b, carry=None):
    T, C = proj.shape[0], CONV_CHANNELS
    vec = pl.BlockSpec((1, C), lambda i: (0, 0))
    wspec = pl.BlockSpec((CONV_PAD, C), lambda i: (0, 0))
    c_in, c_out, c_sems = _carry_io(carry)

    def body(*refs):
        a0, a1, b0, b1, dc_ref, w_ref, cb_ref, g_ref, be_ref = refs[:9]
        ci_refs = refs[9:9 + len(c_in)]
        o = 9 + len(c_in)
        dglu_ref, dw_ref, dcb_ref, dg_ref, dbe_ref = refs[o:o + 5]
        co_refs = refs[o + 5:o + 5 + len(c_out)]
        zpad, dupad, win, ubuf, dwacc = refs[o + 5 + len(c_out):o + 10 + len(c_out)]
        cs_refs = refs[o + 10 + len(c_out):]
        if carry is not None:
            carry.start(ci_refs, co_refs, cs_refs)
        _glu_to_pad(a0, a1, b0, b1, zpad)
        dupad[T:, :] = jnp.zeros((2 * CONV_PAD, C), F32)
        dwacc[...] = jnp.zeros_like(dwacc)
        dcb_ref[...] = jnp.zeros_like(dcb_ref)
        dg_ref[...] = jnp.zeros_like(dg_ref)
        dbe_ref[...] = jnp.zeros_like(dbe_ref)
        for ci in range(T // CHUNK):
            rows = slice(ci * CHUNK, (ci + 1) * CHUNK)
            _tap_windows(zpad, ci * CHUNK + CONV_PAD - (CONV_WIDTH - 1), win)
            _taps(win, w_ref, cb_ref[...], ubuf, False)
            xh, rstd = _ln_parts(ubuf[...])
            ln = xh * g_ref[...] + be_ref[...]
            sg = jax.nn.sigmoid(ln)
            dln = dc_ref[rows, :].astype(F32) * (sg * (1.0 + ln * (1.0 - sg)))
            dg_ref[...] += jnp.sum(dln * xh, axis=0, keepdims=True)
            dbe_ref[...] += jnp.sum(dln, axis=0, keepdims=True)
            dxh = dln * g_ref[...]
            du = rstd * (dxh - jnp.mean(dxh, axis=-1, keepdims=True)
                         - xh * jnp.mean(dxh * xh, axis=-1, keepdims=True))
            dupad[rows, :] = du
            dcb_ref[...] += jnp.sum(du, axis=0, keepdims=True)
            _tap_grads(win, dupad.at[rows, :], dwacc)
        for k in range(CONV_WIDTH):
            dw_ref[k:k + 1, :] = jnp.sum(dwacc[8 * k:8 * k + 8, :], axis=0, keepdims=True)
        dw_ref[CONV_WIDTH:, :] = jnp.zeros((CONV_PAD - CONV_WIDTH, C), F32)
        for ci in range(T // CHUNK):
            rows = slice(ci * CHUNK, (ci + 1) * CHUNK)
            _tap_windows(dupad, ci * CHUNK, win)
            _taps(win, w_ref, jnp.zeros((1, C), F32), ubuf, True)
            dz = ubuf[...]
            for half, (a, b) in enumerate(((a0, b0), (a1, b1))):
                sb = jax.nn.sigmoid(b[rows, :])
                dzh = dz[:, half * 256:(half + 1) * 256]
                dglu_ref[rows, half * 256:(half + 1) * 256] = (dzh * sb).astype(BF)
                dglu_ref[rows, C + half * 256:C + (half + 1) * 256] = (dzh * a[rows, :] * sb * (1.0 - sb)).astype(BF)
        if carry is not None:
            carry.finish(ci_refs, co_refs, cs_refs)

    res = pl.pallas_call(
        body, name="conv_bwd", grid=(1,),
        in_specs=[*_GLU_SPECS, pl.BlockSpec((T, C), lambda i: (0, 0)), wspec, vec, vec, vec, *[ANY] * len(c_in)],
        out_specs=[pl.BlockSpec((T, 2 * C), lambda i: (0, 0)), wspec, vec, vec, vec, *[ANY] * len(c_out)],
        out_shape=[_sds((T, 2 * C), BF), _sds((CONV_PAD, C), F32), _sds((1, C), F32), _sds((1, C), F32),
                   _sds((1, C), F32), *c_out],
        scratch_shapes=[pltpu.VMEM((PAD_ROWS, C), F32), pltpu.VMEM((PAD_ROWS, C), F32), pltpu.VMEM((8, WIN, C), F32),
                        pltpu.VMEM((CHUNK, C), F32), pltpu.VMEM((8 * CONV_PAD, C), F32), *c_sems],
        compiler_params=_params(("arbitrary",)),
    )(proj, proj, proj, proj, d_c, conv_w, conv_b, ln_g, ln_b, *c_in)
    return res[:5], res[5:]


_GATE_BLK = GATE_OFF // 256


def _ffn_in_swiglu(h2, wf_t, carry=None):
    T, D = h2.shape
    tm, tn = 512, D_FF // 2
    nj, ni = D_FF // tn, T // tm
    c_in, c_out, c_sems = _carry_io(carry)

    def body(*refs):
        a_ref, bg_ref, bu_ref = refs[:3]
        ci_refs = refs[3:3 + len(c_in)]
        act_ref, g_ref, u_ref = refs[3 + len(c_in):6 + len(c_in)]
        co_refs = refs[6 + len(c_in):6 + len(c_in) + len(c_out)]
        cs_refs = refs[6 + len(c_in) + len(c_out):]
        j, i = pl.program_id(0), pl.program_id(1)
        if carry is not None:
            @pl.when((j == 0) & (i == 0))
            def _():
                carry.start(ci_refs, co_refs, cs_refs)
        a = a_ref[...]
        g = lax.dot_general(a, bg_ref[...], _DIMS["NT"], preferred_element_type=F32)
        u = lax.dot_general(a, bu_ref[...], _DIMS["NT"], preferred_element_type=F32)
        act_ref[...] = (g * jax.nn.sigmoid(g) * u).astype(BF)
        g_ref[...] = g.astype(BF)
        u_ref[...] = u.astype(BF)
        if carry is not None:
            @pl.when((j == nj - 1) & (i == ni - 1))
            def _():
                carry.finish(ci_refs, co_refs, cs_refs)

    t = pl.BlockSpec((tm, tn), lambda j, i: (i, j))
    res = pl.pallas_call(
        body, name="ffn_in_swiglu", grid=(nj, ni),
        in_specs=[pl.BlockSpec((tm, D), lambda j, i: (i, 0)), pl.BlockSpec((tn, D), lambda j, i: (j, 0)),
                  pl.BlockSpec((tn, D), lambda j, i: (nj + j, 0)), *[ANY] * len(c_in)],
        out_specs=[t, t, t, *[ANY] * len(c_out)], out_shape=[*[_sds((T, D_FF), BF)] * 3, *c_out],
        scratch_shapes=c_sems,
        compiler_params=_params(("arbitrary", "arbitrary")),
    )(h2, wf_t, wf_t, *c_in)
    return res[:3], res[3:]


def _local_step(x, target, small, wi_t, conv_w, plan):
    T, D = x.shape
    tm = 1024

    def carried(call, res, carry):
        if carry is None:
            return res
        outs, got = res
        plan.done(call, got)
        return outs

    h, r1 = _rms_fwd("rms_mix", x, small["g_mix_norm"])

    def ep_add(acc, ex, outs, ids, scr):
        outs[0][...] = acc + ex[0][...]

    tn_in = IN_WIDTH // 3
    carry = plan.carry("proj_in")
    proj, = carried("proj_in", _matmul("proj_in", [h], wi_t, "NT", m=T, n=IN_WIDTH, tm=tm, tn=tn_in, epilogue=ep_add,
                                       extra=[(small["b_in"], _row(tn_in))],
                                       outs=[(_sds((T, IN_WIDTH), F32), _tile(tm, tn_in))], carry=carry), carry)
    o, got = _attn_fwd(proj, small["sinks"], carry=plan.carry("attn_fwd"))
    plan.done("attn_fwd", got)
    c, got = _conv_fwd(proj, conv_w, small["conv_b"], small["ln_g"], small["ln_b"], carry=plan.carry("conv_fwd"))
    plan.done("conv_fwd", got)
    wap_t, wcp_t, w_out = plan.weight("w_attn_proj"), plan.weight("w_conv_proj"), plan.weight("w_out")
    ya, = _matmul("attn_proj", [o], wap_t, "NT", m=T, n=D, tm=tm, tn=D, epilogue=_store(F32),
                  outs=[(_sds((T, D), F32), _tile(tm, D))])

    tg = 256
    gate_specs = [pl.BlockSpec((tm, tg), lambda j, i, k: (i, _GATE_BLK + j)),
                  pl.BlockSpec((tm, tg), lambda j, i, k: (i, _GATE_BLK + D // tg + j))]

    def ep_merge(acc, ex, outs, ids, scr):
        yc = acc + ex[0][...]
        outs[0][...] = yc
        outs[1][...] = (jax.nn.sigmoid(ex[2][...]) * ex[1][...] + jax.nn.sigmoid(ex[3][...]) * yc).astype(BF)

    carry = plan.carry("conv_proj_merge")
    yc, merged = carried("conv_proj_merge", _matmul(
        "conv_proj_merge", [c], wcp_t, "NT", m=T, n=D, tm=tm, tn=tg, epilogue=ep_merge,
        extra=[(small["b_conv_proj"], _row(tg)), (ya, _tile(tm, tg)), (proj, gate_specs[0]), (proj, gate_specs[1])],
        outs=[(_sds((T, D), F32), _tile(tm, tg)), (_sds((T, D), BF), _tile(tm, tg))], carry=carry), carry)
    carry = plan.carry("out_proj")
    x2, = carried("out_proj", _matmul("out_proj", [merged], w_out, "NN", m=T, n=D, tm=tm, tn=D, epilogue=ep_add,
                                      extra=[(x, _tile(tm, D))], outs=[(_sds((T, D), F32), _tile(tm, D))],
                                      carry=carry), carry)
    h2, r2 = _rms_fwd("rms_ffn", x2, small["g_ffn_norm"])
    wf_t = plan.weight("w_ffn_in")
    (act, gate, up), got = _ffn_in_swiglu(h2, wf_t, carry=plan.carry("ffn_in_swiglu"))
    plan.done("ffn_in_swiglu", got)
    w_down = plan.weight("w_ffn_down")
    x3, = _matmul("ffn_down", [act], w_down, "NN", m=T, n=D, tm=512, tn=D, epilogue=ep_add,
                  extra=[(x2, _tile(512, D))], outs=[(_sds((T, D), F32), _tile(512, D))])
    dx3, dx3_b, dg_final, loss = _final(x3, small["g_final"], target)

    tn_ff = D_FF // 2

    def ep_swiglu_bwd(acc, ex, outs, ids, scr):
        g, u = ex[0][...].astype(F32), ex[1][...].astype(F32)
        sg = jax.nn.sigmoid(g)
        outs[0][...] = (acc * u * sg * (1.0 + g * (1.0 - sg))).astype(BF)
        outs[1][...] = (acc * g * sg).astype(BF)

    dgate, dup = _matmul(
        "ffn_down_bwd", [dx3_b], w_down, "NT", m=T, n=D_FF, tm=512, tn=tn_ff, epilogue=ep_swiglu_bwd,
        extra=[(gate, _tile(512, tn_ff)), (up, _tile(512, tn_ff))],
        outs=[(_sds((T, D_FF), BF), _tile(512, tn_ff)), (_sds((T, D_FF), BF), _tile(512, tn_ff))])

    def dw(name, a, b, rows, cols, row_off=0, alias=None, total_rows=None, colsum=False):
        total_rows = rows if total_rows is None else total_rows
        tmw = rows if rows <= 1024 else D_FF // 2
        by_dma = row_off % tmw != 0

        def ep(acc, ex, outs, ids, scr):
            if by_dma:
                scr[0][...] = acc.astype(BF)
                pltpu.sync_copy(scr[0], outs[0].at[pl.ds(pl.multiple_of(row_off + ids[1] * tmw, 256), tmw)])
            else:
                outs[0][...] = acc.astype(BF)
            if colsum:
                outs[1][...] = jnp.sum(ex[0][...].astype(F32), axis=0, keepdims=True)

        blk = row_off // tmw
        spec = pl.BlockSpec(memory_space=pl.ANY) if by_dma else pl.BlockSpec((tmw, cols), lambda j, i, k: (blk + i, j))
        outs = [(_sds((total_rows, cols), BF), spec)]
        extra = []
        if colsum:
            extra = [(a, pl.BlockSpec((T, tmw), lambda j, i, k: (0, i)))]
            outs.append((_sds((1, rows), F32), pl.BlockSpec((1, tmw), lambda j, i, k: (0, i))))
        carry = plan.carry(name)
        res = carried(name, _matmul(name, [a], b, "TN", m=rows, n=cols, tm=tmw, tn=cols, epilogue=ep, extra=extra,
                                    outs=outs, alias=None if alias is None else (alias, 0),
                                    scratch=[pltpu.VMEM((tmw, cols), BF)] if by_dma else [], carry=carry), carry)
        return res if colsum else res[0]

    plan.grad_ready(dict(w_ffn_down=dw("ffn_down_dw", act, dx3_b, D_FF, D)))

    def ep_rms_bwd(acc, ex, outs, ids, scr):
        dx, dg = _rms_bwd(acc, ex[0][...], ex[1][...], ex[2][...])
        dx = ex[3][...] + dx
        outs[0][...] = dx
        outs[1][...] = dx.astype(BF)
        _accumulate_rows(outs[2], dg, ids[1] == 0)

    def rms_bwd_io(tm_, xin, r, g, dres):
        return dict(
            extra=[(xin, _tile(tm_, D)), (r, pl.BlockSpec((tm_, 1), lambda j, i, k: (i, 0))), (g, _row(D)),
                   (dres, _tile(tm_, D))],
            outs=[(_sds((T, D), F32), _tile(tm_, D)), (_sds((T, D), BF), _tile(tm_, D)), (_sds((1, D), F32), _row(D))])

    carry = plan.carry("ffn_in_bwd")
    dx2, dx2_b, dg_ffn = carried(
        "ffn_in_bwd",
        _matmul("ffn_in_bwd", [dgate, dup], wf_t, "NN", m=T, n=D, tm=512, tn=D, tk=D_FF, epilogue=ep_rms_bwd,
                carry=carry, **rms_bwd_io(512, x2, r2, small["g_ffn_norm"], dx3)), carry)
    gwf_t = dw("ffn_in_dw_gate", dgate, h2, D_FF, D, total_rows=2 * D_FF)
    gwf_t = dw("ffn_in_dw_up", dup, h2, D_FF, D, row_off=D_FF, alias=gwf_t, total_rows=2 * D_FF)
    plan.grad_ready(dict(w_ffn_in=gwf_t))

    def ep_merge_bwd(acc, ex, outs, ids, scr):
        s0 = jax.nn.sigmoid(ex[2][...])
        s1 = jax.nn.sigmoid(ex[3][...])
        outs[0][...] = (acc * s0).astype(BF)
        outs[1][...] = (acc * s1).astype(BF)
        outs[2][...] = (acc * ex[0][...] * s0 * (1.0 - s0)).astype(BF)
        outs[3][...] = (acc * ex[1][...] * s1 * (1.0 - s1)).astype(BF)

    carry = plan.carry("out_proj_bwd_merge")
    dya, dyc, dg0, dg1 = carried(
        "out_proj_bwd_merge",
        _matmul("out_proj_bwd_merge", [dx2_b], w_out, "NT", m=T, n=D, tm=tm, tn=tg, epilogue=ep_merge_bwd,
                extra=[(ya, _tile(tm, tg)), (yc, _tile(tm, tg)), (proj, gate_specs[0]), (proj, gate_specs[1])],
                outs=[(_sds((T, D), BF), _tile(tm, tg))] * 4, carry=carry), carry)
    gw_out = dw("out_proj_dw", merged, dx2_b, D, D)
    d_o, = _matmul("attn_proj_bwd", [dya], wap_t, "NN", m=T, n=ATTN_WIDTH, tm=tm, tn=ATTN_WIDTH,
                   epilogue=_store(BF), outs=[(_sds((T, ATTN_WIDTH), BF), _tile(tm, ATTN_WIDTH))])
    d_c, = _matmul("conv_proj_bwd", [dyc], wcp_t, "NN", m=T, n=CONV_CHANNELS, tm=tm, tn=CONV_CHANNELS,
                   epilogue=_store(BF), outs=[(_sds((T, CONV_CHANNELS), BF), _tile(tm, CONV_CHANNELS))])
    gwap_t = dw("attn_proj_dw", dya, o, D, ATTN_WIDTH)
    gwcp_t, db_cp = dw("conv_proj_dw", dyc, c, D, CONV_CHANNELS, colsum=True)
    plan.grad_ready(dict(w_out=gw_out, w_attn_proj=gwap_t, w_conv_proj=gwcp_t))
    (dglu, dcw, dcb, dlng, dlnb), got = _conv_bwd(proj, d_c, conv_w, small["conv_b"], small["ln_g"], small["ln_b"],
                                                  carry=plan.carry("conv_bwd"))
    plan.done("conv_bwd", got)
    (dqkv, dsinks), got = _attn_bwd(proj, d_o, small["sinks"], carry=plan.carry("attn_bwd"))
    plan.done("attn_bwd", got)

    segs = [dqkv, dglu, dg0, dg1]
    gwi_t, off, db_in = None, 0, []
    for s, seg in enumerate(segs):
        gwi_t, db = dw(f"proj_in_dw{s}", seg, h, seg.shape[1], D, row_off=off, alias=gwi_t, total_rows=IN_WIDTH,
                       colsum=True)
        db_in.append(db)
        off += seg.shape[1]
    plan.grad_ready(dict(w_in=gwi_t))
    plan.alone("swap_inp")
    carry = plan.carry("proj_in_bwd")
    dx, _, dg_mix = carried(
        "proj_in_bwd",
        _matmul("proj_in_bwd", segs, wi_t, "NN", m=T, n=D, tm=512, tn=D, epilogue=ep_rms_bwd, carry=carry,
                **rms_bwd_io(512, x, r1, small["g_mix_norm"], dx2)), carry)

    parts = dict(g_mix_norm=dg_mix, b_in=db_in, sinks=dsinks, conv_w=dcw, conv_b=dcb, ln_g=dlng, ln_b=dlnb,
                 b_conv_proj=db_cp, g_ffn_norm=dg_ffn, g_final=dg_final, loss=loss)
    return dx, parts


def _place():
    x, y, c = lax.axis_index("x"), lax.axis_index("y"), lax.axis_index("c")
    return x, y, c, [(1 - x, y), (x, 1 - y), (1 - x, 1 - y)]


def _gather_copies(x_refs, out_refs, rows_per, send_sems, recv_sems, local_sems):
    x, y, c, chips = _place()
    me, sibling = (x, y, c), (x, y, 1 - c)

    def rows(a, px, py, pc):
        return out_refs[a].at[pl.ds((4 * px + 2 * py + pc) * rows_per[a], rows_per[a])]

    def copy(a, k, block, to, src=None):
        return pltpu.make_async_remote_copy(
            src_ref=rows(a, *block) if src is None else src, dst_ref=rows(a, *block),
            send_sem=send_sems.at[7 * a + k], recv_sem=recv_sems.at[7 * a + k], device_id=to, device_id_type=MESH)

    def local(a):
        return pltpu.make_async_copy(x_refs[a], rows(a, *me), local_sems.at[a])

    def first(a):
        return [copy(a, 0, me, sibling, src=x_refs[a])] + [copy(a, 1 + j, me, (*chip, c), src=x_refs[a])
                                                          for j, chip in enumerate(chips)]

    def arrive(a, j):
        return copy(a, 1 + j, (*chips[j], c), me)

    def passed(a, j):
        return copy(a, 4 + j, (*chips[j], c), sibling)

    def from_sibling(a):
        return [copy(a, 0, sibling, me)] + [copy(a, 4 + j, (*chip, 1 - c), me) for j, chip in enumerate(chips)]

    return len(x_refs), local, first, arrive, passed, from_sibling


def _gather_start(*refs):
    n, local, first, _, _, _ = _gather_copies(*refs)
    for a in range(n):
        local(a).start()
        for cp in first(a):
            cp.start()


def _gather_finish(*refs):
    n, local, first, arrive, passed, from_sibling = _gather_copies(*refs)
    for a in range(n):
        for j in range(3):
            arrive(a, j).wait_recv()
            passed(a, j).start()
    for a in range(n):
        for cp in from_sibling(a):
            cp.wait_recv()
    for a in range(n):
        for cp in first(a) + [passed(a, j) for j in range(3)]:
            cp.wait_send()
        local(a).wait()


def _gather_blocks(*refs):
    _gather_start(*refs)
    _gather_finish(*refs)


def _gather_sems(n):
    return [pltpu.SemaphoreType.DMA((7 * n,)), pltpu.SemaphoreType.DMA((7 * n,)), pltpu.SemaphoreType.DMA((n,))]


def _gather_carry(shards):
    rows_per = [s.shape[0] for s in shards]
    return _Carry(shards, [_sds((N_DEV * s.shape[0],) + s.shape[1:], s.dtype) for s in shards],
                  _gather_sems(len(shards)),
                  lambda ins, outs, sems: _gather_start(ins, outs, rows_per, *sems),
                  lambda ins, outs, sems: _gather_finish(ins, outs, rows_per, *sems))


def _all_gather(shards):
    return _run_carry("weights_all_gather", _gather_carry(shards))


def _swap_carry(grads):
    n = len(grads)

    def copies(g_refs, out_refs, sems):
        send_sems, recv_sems = sems
        x, y, c, _ = _place()
        return [pltpu.make_async_remote_copy(
            src_ref=g_refs[a].at[2 * p + 1 - c], dst_ref=out_refs[a].at[p],
            send_sem=send_sems.at[4 * a + p], recv_sem=recv_sems.at[4 * a + p],
            device_id=(x, y, 1 - c), device_id_type=MESH) for a in range(n) for p in range(4)]

    def start(ins, outs, sems):
        for cp in copies(ins, outs, sems):
            cp.start()

    def finish(ins, outs, sems):
        for cp in copies(ins, outs, sems):
            cp.wait()

    return _Carry(grads, [_sds((4,) + g.shape[1:], g.dtype) for g in grads],
                  [pltpu.SemaphoreType.DMA((4 * n,)), pltpu.SemaphoreType.DMA((4 * n,))], start, finish)


def _join(carries):
    carries = [c for c in carries if c is not None]
    if not carries:
        return None
    n_in = [len(c.arrays) for c in carries]
    n_out = [len(c.out_shapes) for c in carries]
    n_sem = [len(c.sems) for c in carries]

    def parts(refs, counts):
        cuts = [sum(counts[:q]) for q in range(len(counts) + 1)]
        return [refs[cuts[q]:cuts[q + 1]] for q in range(len(counts))]

    def start(ins, outs, sems):
        for c, i, o, s in zip(carries, parts(ins, n_in), parts(outs, n_out), parts(sems, n_sem)):
            c.start(i, o, s)

    def finish(ins, outs, sems):
        for c, i, o, s in zip(carries, parts(ins, n_in), parts(outs, n_out), parts(sems, n_sem)):
            c.finish(i, o, s)

    return _Carry([a for c in carries for a in c.arrays], [o for c in carries for o in c.out_shapes],
                  [s for c in carries for s in c.sems], start, finish)


def _run_carry(name, carry):
    n_in, n_out = len(carry.arrays), len(carry.out_shapes)

    def body(*refs):
        carry.start(refs[:n_in], refs[n_in:n_in + n_out], refs[n_in + n_out:])
        carry.finish(refs[:n_in], refs[n_in:n_in + n_out], refs[n_in + n_out:])

    return pl.pallas_call(body, name=name, in_specs=[ANY] * n_in, out_specs=[ANY] * n_out,
                          out_shape=carry.out_shapes, scratch_shapes=carry.sems)(*carry.arrays)


def _chip_sum(name, g, got, c):
    _, rows, cols = g.shape

    def body(c_ref, g_ref, got_ref, o_ref):
        o_ref[...] = (g_ref[...].astype(F32) + got_ref[...].astype(F32)).astype(BF)

    return pl.pallas_call(
        body, name=name,
        grid_spec=pltpu.PrefetchScalarGridSpec(
            num_scalar_prefetch=1, grid=(4,),
            in_specs=[pl.BlockSpec((1, rows, cols), lambda p, c_ref: (2 * p + c_ref[0], 0, 0)),
                      pl.BlockSpec((1, rows, cols), lambda p, c_ref: (p, 0, 0))],
            out_specs=pl.BlockSpec((1, rows, cols), lambda p, c_ref: (p, 0, 0))),
        out_shape=_sds((4, rows, cols), BF),
        compiler_params=_params(("arbitrary",)),
    )(c, g, got)


def _send_carry(sums, ks):
    n, nk = len(sums), len(ks)

    def copies(s_refs, out_refs, sems):
        send_sems, recv_sems = sems
        x, y, c, chips = _place()
        return [pltpu.make_async_remote_copy(
            src_ref=s_refs[a].at[2 * chips[k][0] + chips[k][1]], dst_ref=out_refs[a].at[q],
            send_sem=send_sems.at[nk * a + q], recv_sem=recv_sems.at[nk * a + q],
            device_id=(*chips[k], c), device_id_type=MESH) for a in range(n) for q, k in enumerate(ks)]

    def start(ins, outs, sems):
        for cp in copies(ins, outs, sems):
            cp.start()

    def finish(ins, outs, sems):
        for cp in copies(ins, outs, sems):
            cp.wait()

    return _Carry(sums, [_sds((nk,) + s.shape[1:], s.dtype) for s in sums],
                  [pltpu.SemaphoreType.DMA((nk * n,)), pltpu.SemaphoreType.DMA((nk * n,))], start, finish)


def _grad_total(name, g, got, got3, ids):
    _, rows, cols = g.shape
    n3 = len(got3)

    def body(ids_ref, g_ref, got_ref, *rest):
        o_ref = rest[n3]
        tot = g_ref[0].astype(F32) + got_ref[0].astype(F32)
        for r_ref in rest[:n3]:
            for q in range(r_ref.shape[0]):
                tot = tot + r_ref[q].astype(F32)
        o_ref[...] = tot

    return pl.pallas_call(
        body, name=name,
        grid_spec=pltpu.PrefetchScalarGridSpec(
            num_scalar_prefetch=1, grid=(1,),
            in_specs=[pl.BlockSpec((1, rows, cols), lambda i, ids_ref: (ids_ref[0], 0, 0)),
                      pl.BlockSpec((1, rows, cols), lambda i, ids_ref: (ids_ref[1], 0, 0)),
                      *[pl.BlockSpec(r.shape, lambda i, ids_ref: (0, 0, 0)) for r in got3]],
            out_specs=pl.BlockSpec((rows, cols), lambda i, ids_ref: (0, 0))),
        out_shape=_sds((rows, cols), F32),
        compiler_params=_params(("arbitrary",)),
    )(ids, g, got, *got3)


def _adam_math(w, g, m, v):
    m = ADAM_B1 * m + (1.0 - ADAM_B1) * g
    v = ADAM_B2 * v + (1.0 - ADAM_B2) * (g * g)
    m_hat = m / (1.0 - ADAM_B1 ** ADAM_STEP)
    v_hat = v / (1.0 - ADAM_B2 ** ADAM_STEP)
    delta = -ADAM_LR * (m_hat / (jnp.sqrt(v_hat) + ADAM_EPS) + ADAM_WD * w)
    return delta, m, v


def _adamw(name, w, g, m, v):
    rows, cols = w.shape
    tr = 256 if rows % 256 == 0 else rows

    def body(w_ref, g_ref, m_ref, v_ref, d_ref, nm_ref, nv_ref):
        d_ref[...], nm_ref[...], nv_ref[...] = _adam_math(w_ref[...], g_ref[...], m_ref[...], v_ref[...])

    t = pl.BlockSpec((tr, cols), lambda i: (i, 0))
    return pl.pallas_call(
        body, name=name, grid=(rows // tr,), in_specs=[t] * 4, out_specs=[t] * 3,
        out_shape=[_sds((rows, cols), F32)] * 3, compiler_params=_params(("arbitrary",)),
    )(w, g, m, v)


SMALL_NAMES = ["g_mix_norm", "b_in", "sinks", "conv_b", "ln_g", "ln_b", "b_conv_proj", "g_ffn_norm", "g_final"]
_PACK_ROWS = 32


def _small_all_reduce(parts):
    C = CONV_CHANNELS
    part_list = [parts["g_mix_norm"], *parts["b_in"], parts["sinks"], parts["conv_b"], parts["ln_g"], parts["ln_b"],
                 parts["b_conv_proj"], parts["g_ffn_norm"], parts["g_final"], parts["loss"], parts["conv_w"]]
    n_part = len(part_list)

    def body(*refs):
        (p_mix, p_b0, p_b1, p_b2, p_b3, p_sink, p_cb, p_lg, p_lb, p_bcp, p_ffn, p_fin, p_loss, p_cw) = refs[:n_part]
        tot_ref, pack, gathered, send_sems, recv_sems, local_sems = refs[n_part:]
        pack[...] = jnp.zeros_like(pack)
        pack[0:1, :] = p_mix[...]
        pack[1:2, 0:GLU_OFF] = p_b0[...]
        pack[2:3, :] = p_b1[...]
        pack[3:4, :] = p_b2[...]
        pack[4:5, :] = p_b3[...]
        pack[5:6, 0:128] = p_sink[...]
        pack[6:7, 0:C] = p_cb[...]
        pack[6:7, C:2 * C] = p_lg[...]
        pack[7:8, 0:C] = p_lb[...]
        pack[8:9, :] = p_bcp[...]
        pack[9:10, :] = p_ffn[...]
        pack[10:11, :] = p_fin[...]
        pack[11:12, 0:128] = jnp.broadcast_to(p_loss[...], (1, 128))
        pack[12:28, 0:C] = p_cw[0:16, :]
        pack[12:28, C:2 * C] = p_cw[16:32, :]
        _gather_blocks([pack], [gathered], [_PACK_ROWS], send_sems, recv_sems, local_sems)
        tot = gathered[0:_PACK_ROWS, :]
        for d in range(1, N_DEV):
            tot = tot + gathered[d * _PACK_ROWS:(d + 1) * _PACK_ROWS, :]
        tot_ref[...] = tot

    vm = pl.BlockSpec(memory_space=pltpu.VMEM)
    return pl.pallas_call(
        body, name="small_all_reduce",
        in_specs=[vm] * n_part, out_specs=vm, out_shape=_sds((_PACK_ROWS, D_MODEL), F32),
        scratch_shapes=[pltpu.VMEM((_PACK_ROWS, D_MODEL), F32), pltpu.VMEM((N_DEV * _PACK_ROWS, D_MODEL), F32),
                        pltpu.SemaphoreType.DMA((7,)), pltpu.SemaphoreType.DMA((7,)), pltpu.SemaphoreType.DMA((1,))],
        compiler_params=pltpu.CompilerParams(vmem_limit_bytes=VMEM_LIMIT_BYTES),
    )(*part_list)


def _small_adamw(tot, small_w, small_m, small_v):
    C = CONV_CHANNELS
    names = SMALL_NAMES
    widths = [small_w[k].shape[1] for k in names]
    n_small = len(names)

    def body(*refs):
        tot_ref = refs[0]
        w_refs = refs[1:1 + n_small]
        m_refs = refs[1 + n_small:1 + 2 * n_small]
        v_refs = refs[1 + 2 * n_small:1 + 3 * n_small]
        o = 1 + 3 * n_small
        loss_ref, cw_ref = refs[o], refs[o + 1]
        out_refs = refs[o + 2:o + 2 + 4 * n_small]
        tot = tot_ref[...]
        loss_ref[...] = tot[11:12, 0:1]
        cw_ref[0:16, :] = tot[12:28, 0:C]
        cw_ref[16:32, :] = tot[12:28, C:2 * C]
        grads = dict(
            g_mix_norm=tot[0:1, :],
            b_in=jnp.concatenate([tot[1:2, 0:GLU_OFF], tot[2:3, :], tot[3:4, :], tot[4:5, :]], axis=1),
            sinks=tot[5:6, 0:N_Q_HEADS], conv_b=tot[6:7, 0:C], ln_g=tot[6:7, C:2 * C], ln_b=tot[7:8, 0:C],
            b_conv_proj=tot[8:9, :], g_ffn_norm=tot[9:10, :], g_final=tot[10:11, :])
        for s, k in enumerate(names):
            g = grads[k]
            d, nm, nv = _adam_math(w_refs[s][...], g, m_refs[s][...], v_refs[s][...])
            out_refs[4 * s][...] = g
            out_refs[4 * s + 1][...] = d
            out_refs[4 * s + 2][...] = nm
            out_refs[4 * s + 3][...] = nv

    vm = pl.BlockSpec(memory_space=pltpu.VMEM)
    args = [tot, *[small_w[k] for k in names], *[small_m[k] for k in names], *[small_v[k] for k in names]]
    out_shape = [_sds((1, 1), F32), _sds((CONV_PAD, C), F32)]
    for wd in widths:
        out_shape += [_sds((1, wd), F32)] * 4
    res = pl.pallas_call(
        body, name="small_adamw",
        in_specs=[vm] * len(args), out_specs=[vm] * len(out_shape), out_shape=out_shape,
        compiler_params=pltpu.CompilerParams(vmem_limit_bytes=VMEM_LIMIT_BYTES),
    )(*args)
    return res[0], res[1], {k: res[2 + 4 * s:6 + 4 * s] for s, k in enumerate(names)}


BIG = dict(w_in=True, w_attn_proj=True, w_conv_proj=True, w_out=False, w_ffn_in=True, w_ffn_down=False)
WEIGHT_NAMES = ["g_mix_norm", "w_in", "b_in", "sinks", "conv_w", "conv_b", "ln_g", "ln_b", "w_attn_proj",
                "w_conv_proj", "b_conv_proj", "w_out", "g_ffn_norm", "w_ffn_in", "w_ffn_down", "g_final"]


class _Plan:
    GROUPS = dict(down=["w_ffn_down"], ffn=["w_ffn_in"], mix_o=["w_out"], mix_p=["w_attn_proj", "w_conv_proj"],
                  inp=["w_in"])
    NEAR, FAR, ALL = (0, 1), (2,), (0, 1, 2)
    FFN_PIECES = ["w_ffn_in.0", "w_ffn_in.1", "w_ffn_in.2", "w_ffn_in.3"]
    RIDES = dict(
        proj_in=[("gather", ["w_attn_proj", "w_conv_proj", "w_out"])],
        attn_fwd=[("gather", ["w_ffn_in.0"])], conv_fwd=[("gather", ["w_ffn_in.1"])],
        conv_proj_merge=[("gather", ["w_ffn_in.2"])], out_proj=[("gather", ["w_ffn_in.3"])],
        ffn_in_swiglu=[("gather", ["w_ffn_down"])],
        ffn_in_bwd=[("swap", "down")], ffn_in_dw_gate=[("send", "down", NEAR)], ffn_in_dw_up=[("send", "down", FAR)],
        out_proj_bwd_merge=[("swap", "ffn")],
        conv_bwd=[("send", "ffn", NEAR), ("swap", "mix_o"), ("swap", "mix_p")], attn_bwd=[("send", "ffn", FAR)],
        proj_in_dw1=[("send", "mix_o", ALL)], proj_in_dw2=[("send", "mix_p", ALL)],
        swap_inp=[("swap", "inp")], proj_in_bwd=[("send", "inp", ALL)])

    def __init__(self, shards, c1):
        self.shards, self.c1 = dict(shards), c1
        q = D_MODEL // len(self.FFN_PIECES)
        for i, k in enumerate(self.FFN_PIECES):
            self.shards[k] = shards["w_ffn_in"][:, i * q:(i + 1) * q]
        self.full, self.slots, self.got, self.sums, self.got3 = {}, {}, {}, {}, {}

    def weight(self, name):
        if name == "w_ffn_in":
            return jnp.concatenate([self.full[k] for k in self.FFN_PIECES], axis=1)
        return self.full[name]

    def grad_ready(self, grads):
        for k, g in grads.items():
            self.slots[k] = g.reshape(N_DEV, g.shape[0] // N_DEV, g.shape[1])

    def _one(self, kind, what, ks=None):
        if kind == "gather":
            return _gather_carry([self.shards[k] for k in what])
        names = self.GROUPS[what]
        if kind == "swap":
            return _swap_carry([self.slots[k] for k in names])
        return _send_carry([self.sums[k] for k in names], ks)

    def carry(self, call):
        return _join([self._one(*ride) for ride in self.RIDES.get(call, [])])

    def done(self, call, outs):
        outs = list(outs)
        for kind, what, *_ in self.RIDES[call]:
            names = what if kind == "gather" else self.GROUPS[what]
            mine, outs = outs[:len(names)], outs[len(names):]
            if kind == "gather":
                self.full.update(zip(names, mine))
            elif kind == "send":
                for k, r in zip(names, mine):
                    self.got3.setdefault(k, []).append(r)
            else:
                for k, r in zip(names, mine):
                    self.got[k] = r
                    self.sums[k] = _chip_sum(f"chip_sum_{k}", self.slots[k], r, self.c1)

    def alone(self, call):
        self.done(call, _run_carry(call, self.carry(call)))


def kernel(x, g_mix_norm, w_in, b_in, sinks, conv_w, conv_b, ln_g, ln_b, w_attn_proj, w_conv_proj, b_conv_proj, w_out, g_ffn_norm, w_ffn_in, w_ffn_down, g_final, loss_target, m_g_mix_norm, m_w_in, m_b_in, m_sinks, m_conv_w, m_conv_b, m_ln_g, m_ln_b, m_w_attn_proj, m_w_conv_proj, m_b_conv_proj, m_w_out, m_g_ffn_norm, m_w_ffn_in, m_w_ffn_down, m_g_final, v_g_mix_norm, v_w_in, v_b_in, v_sinks, v_conv_w, v_conv_b, v_ln_g, v_ln_b, v_w_attn_proj, v_w_conv_proj, v_b_conv_proj, v_w_out, v_g_ffn_norm, v_w_ffn_in, v_w_ffn_down, v_g_final):
    w = dict(g_mix_norm=g_mix_norm, w_in=w_in, b_in=b_in, sinks=sinks, conv_w=conv_w, conv_b=conv_b, ln_g=ln_g,
             ln_b=ln_b, w_attn_proj=w_attn_proj, w_conv_proj=w_conv_proj, b_conv_proj=b_conv_proj, w_out=w_out,
             g_ffn_norm=g_ffn_norm, w_ffn_in=w_ffn_in, w_ffn_down=w_ffn_down, g_final=g_final)
    m = dict(g_mix_norm=m_g_mix_norm, w_in=m_w_in, b_in=m_b_in, sinks=m_sinks, conv_w=m_conv_w, conv_b=m_conv_b,
             ln_g=m_ln_g, ln_b=m_ln_b, w_attn_proj=m_w_attn_proj, w_conv_proj=m_w_conv_proj,
             b_conv_proj=m_b_conv_proj, w_out=m_w_out, g_ffn_norm=m_g_ffn_norm, w_ffn_in=m_w_ffn_in,
             w_ffn_down=m_w_ffn_down, g_final=m_g_final)
    v = dict(g_mix_norm=v_g_mix_norm, w_in=v_w_in, b_in=v_b_in, sinks=v_sinks, conv_w=v_conv_w, conv_b=v_conv_b,
             ln_g=v_ln_g, ln_b=v_ln_b, w_attn_proj=v_w_attn_proj, w_conv_proj=v_w_conv_proj,
             b_conv_proj=v_b_conv_proj, w_out=v_w_out, g_ffn_norm=v_g_ffn_norm, w_ffn_in=v_w_ffn_in,
             w_ffn_down=v_w_ffn_down, g_final=v_g_final)
    ax, ay, ac = lax.axis_index("x"), lax.axis_index("y"), lax.axis_index("c")
    me = 4 * ax + 2 * ay + ac
    chip = 2 * ax + ay

    shards = {k: (w[k][0].T if tr else w[k][0]).astype(BF) for k, tr in BIG.items()}
    cw_shard = jnp.pad(conv_w[0].T, ((0, 0), (0, 1))).reshape(16, 128)
    wi_t, cw_full = _all_gather([shards["w_in"], cw_shard])
    conv_full = cw_full.reshape(CONV_CHANNELS, CONV_PAD).T

    as_row = lambda a: a.reshape(1, -1)
    small_w = {k: as_row(w[k]) for k in SMALL_NAMES}
    small_m = {k: as_row(m[k]) for k in SMALL_NAMES}
    small_v = {k: as_row(v[k]) for k in SMALL_NAMES}
    plan = _Plan(shards, ac.reshape(1).astype(jnp.int32))
    dx, parts = _local_step(x[0], loss_target[0], small_w, wi_t, conv_full, plan)

    ids = jnp.stack([me, chip]).astype(jnp.int32)
    grads, delta, new_m, new_v = {}, {}, {}, {}
    for k in BIG:
        tot = _grad_total(f"grad_total_{k}", plan.slots[k], plan.got[k], plan.got3[k], ids)
        tot = tot.T if BIG[k] else tot
        d, nm, nv = _adamw(f"adamw_{k}", w[k][0], tot, m[k][0], v[k][0])
        grads[k], delta[k], new_m[k], new_v[k] = tot[None], d[None], nm[None], nv[None]

    loss, cw_grad, small_out = _small_adamw(_small_all_reduce(parts), small_w, small_m, small_v)
    for k in SMALL_NAMES:
        g, d, nm, nv = (a.reshape(w[k].shape) for a in small_out[k])
        grads[k], delta[k], new_m[k], new_v[k] = g, d, nm, nv
    cw_mine = lax.dynamic_slice(cw_grad, (0, me * 64), (CONV_WIDTH, 64))
    d, nm, nv = _adamw("adamw_conv_w", conv_w[0], cw_mine, m_conv_w[0], v_conv_w[0])
    grads["conv_w"], delta["conv_w"], new_m["conv_w"], new_v["conv_w"] = cw_mine[None], d[None], nm[None], nv[None]

    return (loss.reshape(()), dx[None], *[grads[k] for k in WEIGHT_NAMES], *[delta[k] for k in WEIGHT_NAMES],
            *[new_m[k] for k in WEIGHT_NAMES], *[new_v[k] for k in WEIGHT_NAMES])
```

```python
import functools

import jax
import jax.numpy as jnp
from jax import lax
from jax.experimental import pallas as pl
from jax.experimental.pallas import tpu as pltpu
from jax.experimental.pallas import tpu_sc as plsc

F32 = jnp.float32
BF = jnp.bfloat16

SEQ = 2048
D_MODEL = 1024
HEAD_DIM = 64
N_Q_HEADS = 8
N_KV_HEADS = 2
GROUP = N_Q_HEADS // N_KV_HEADS
BLOCK = 128
ATTN_WIDTH = 512
KV_WIDTH = 128
CONV_CHANNELS = 512
CONV_WIDTH = 31
CONV_PAD = 32
GLU_OFF = 768
GATE_OFF = 1792
IN_WIDTH = 3840
D_FF = 2816
EPS = 1e-5
NEG = -1e30
N_DEV = 8

ADAM_LR = 0.001
ADAM_B1 = 0.9
ADAM_B2 = 0.999
ADAM_EPS = 1e-08
ADAM_WD = 0.01
ADAM_STEP = 10

VMEM_LIMIT_BYTES = 56 * 1024 * 1024
MESH = pl.DeviceIdType.MESH
ANY = pl.BlockSpec(memory_space=pl.ANY)

_DIMS = {"NN": (((1,), (0,)), ((), ())), "NT": (((1,), (1,)), ((), ())), "TN": (((0,), (0,)), ((), ()))}


def _params(sem):
    return pltpu.CompilerParams(dimension_semantics=sem, vmem_limit_bytes=VMEM_LIMIT_BYTES)


class _Carry:
    def __init__(self, arrays, out_shapes, sems, start, finish, peers=None):
        self.arrays, self.out_shapes, self.sems, self.start, self.finish = arrays, out_shapes, sems, start, finish
        self.peers = peers


def _carry_io(carry):
    if carry is None:
        return [], [], []
    return list(carry.arrays), list(carry.out_shapes), list(carry.sems)


def _matmul(name, a_list, b, mode, *, m, n, tm, tn, tk=None, epilogue, extra=(), outs, b_off=(0, 0), alias=None,
            scratch=(), carry=None):
    seg_k = [a.shape[0] if mode == "TN" else a.shape[1] for a in a_list]
    whole = tk is None
    seg_nk = [1] * len(a_list) if whole else [ks // tk for ks in seg_k]
    nk = 1 if whole else sum(seg_nk)
    starts = [sum(seg_nk[:s]) for s in range(len(seg_nk))]
    k_starts = [sum(seg_k[:s]) for s in range(len(seg_k))]
    k_tot = sum(seg_k)
    n_a, n_extra, n_out = len(a_list), len(extra), len(outs)

    a_specs = []
    for st, ns, ks in zip(starts, seg_nk, seg_k):
        if mode == "TN":
            a_specs.append(pl.BlockSpec((ks if whole else tk, tm), lambda j, i, k: (k, i)))
        elif whole:
            a_specs.append(pl.BlockSpec((tm, ks), lambda j, i, k: (i, 0)))
        else:
            a_specs.append(pl.BlockSpec((tm, tk), functools.partial(
                lambda j, i, k, st, ns: (i, jnp.clip(k - st, 0, ns - 1)), st=st, ns=ns)))
    bk = k_tot if whole else tk
    if mode == "NT":
        b_spec = pl.BlockSpec((tn, bk), lambda j, i, k: (b_off[0] + j, b_off[1] + k))
    else:
        b_spec = pl.BlockSpec((bk, tn), lambda j, i, k: (b_off[0] + k, b_off[1] + j))
    n_alias = 0 if alias is None else 1
    c_in, c_out, c_sems = _carry_io(carry)
    n_acc = 0 if whole else 1
    nj, ni = n // tn, m // tm

    def body(*refs):
        pos = [n_a, 1, n_alias, n_extra, len(c_in), n_out, len(c_out), n_acc, len(scratch), len(c_sems)]
        cuts = [sum(pos[:q]) for q in range(len(pos) + 1)]
        a_refs, (b_ref,), _, ex, ci_refs, out_refs, co_refs, acc_refs, scr, cs_refs = (
            refs[cuts[q]:cuts[q + 1]] for q in range(len(pos)))
        j, i, k = pl.program_id(0), pl.program_id(1), pl.program_id(2)
        ids = (j, i)
        if carry is not None:
            @pl.when((j == 0) & (i == 0) & (k == 0))
            def _():
                carry.start(ci_refs, co_refs, cs_refs)

        def dot(a_ref, bv):
            return lax.dot_general(a_ref[...].astype(BF), bv.astype(BF), _DIMS[mode], preferred_element_type=F32)

        if whole:
            tot = None
            for a_ref, k0, ks in zip(a_refs, k_starts, seg_k):
                if n_a == 1:
                    bv = b_ref[...]
                else:
                    bv = b_ref[:, k0:k0 + ks] if mode == "NT" else b_ref[k0:k0 + ks, :]
                part = dot(a_ref, bv)
                tot = part if tot is None else tot + part
            epilogue(tot, ex, out_refs, ids, scr)
        else:
            acc, = acc_refs

            @pl.when(k == 0)
            def _():
                acc[...] = jnp.zeros_like(acc)

            for a_ref, st, ns in zip(a_refs, starts, seg_nk):
                if n_a == 1:
                    acc[...] += dot(a_ref, b_ref[...])
                else:
                    @pl.when((k >= st) & (k < st + ns))
                    def _(a_ref=a_ref):
                        acc[...] += dot(a_ref, b_ref[...])

            @pl.when(k == nk - 1)
            def _():
                epilogue(acc[...], ex, out_refs, ids, scr)

        if carry is not None:
            @pl.when((j == nj - 1) & (i == ni - 1) & (k == nk - 1))
            def _():
                carry.finish(ci_refs, co_refs, cs_refs)

    in_specs = [*a_specs, b_spec]
    args = [*a_list, b]
    io_alias = {}
    if alias is not None:
        in_specs.append(pl.BlockSpec(memory_space=pl.ANY))
        args.append(alias[0])
        io_alias = {n_a + 1: alias[1]}
    in_specs += [s for _, s in extra] + [pl.BlockSpec(memory_space=pl.ANY)] * len(c_in)
    args += [x for x, _ in extra] + c_in
    res = pl.pallas_call(
        body, name=name, grid=(nj, ni, nk), in_specs=in_specs,
        out_specs=[s for _, s in outs] + [pl.BlockSpec(memory_space=pl.ANY)] * len(c_out),
        out_shape=[o for o, _ in outs] + c_out,
        scratch_shapes=[*([] if whole else [pltpu.VMEM((tm, tn), F32)]), *scratch, *c_sems],
        input_output_aliases=io_alias,
        compiler_params=_params(("arbitrary", "arbitrary", "arbitrary")),
    )(*args)
    return res if carry is None else (res[:n_out], res[n_out:])


def _tile(tm, tn):
    return pl.BlockSpec((tm, tn), lambda j, i, k: (i, j))


def _row(tn):
    return pl.BlockSpec((1, tn), lambda j, i, k: (0, j))


def _store(dtype):
    def ep(acc, ex, outs, ids, scr):
        outs[0][...] = acc.astype(dtype)
    return ep


def _sds(shape, dtype):
    return jax.ShapeDtypeStruct(shape, dtype)


def _rms_fwd(name, x, g):
    T, D = x.shape
    tm = 512

    def body(x_ref, g_ref, h_ref, r_ref):
        xv = x_ref[...]
        r = lax.rsqrt(jnp.mean(xv * xv, axis=-1, keepdims=True) + EPS)
        h_ref[...] = (xv * r * g_ref[...]).astype(BF)
        r_ref[...] = r

    return pl.pallas_call(
        body, name=name, grid=(T // tm,),
        in_specs=[pl.BlockSpec((tm, D), lambda i: (i, 0)), pl.BlockSpec((1, D), lambda i: (0, 0))],
        out_specs=[pl.BlockSpec((tm, D), lambda i: (i, 0)), pl.BlockSpec((tm, 1), lambda i: (i, 0))],
        out_shape=[_sds((T, D), BF), _sds((T, 1), F32)],
        compiler_params=_params(("arbitrary",)),
    )(x, g)


def _rms_bwd(dh, xv, r, g):
    xh = xv * r
    dxh = dh * g
    dx = r * (dxh - xh * jnp.mean(dxh * xh, axis=-1, keepdims=True))
    return dx, jnp.sum(dh * xh, axis=0, keepdims=True)


def _accumulate_rows(ref, val, first):
    @pl.when(first)
    def _():
        ref[...] = val

    @pl.when(jnp.logical_not(first))
    def _():
        ref[...] += val


def _final(x3, g_final, target):
    T, D = x3.shape
    tm = 512

    def body(x_ref, g_ref, t_ref, dx_ref, dxb_ref, dg_ref, loss_ref):
        i = pl.program_id(0)
        xv = x_ref[...]
        g = g_ref[...]
        r = lax.rsqrt(jnp.mean(xv * xv, axis=-1, keepdims=True) + EPS)
        err = xv * r * g - t_ref[...]
        dy = err * (1.0 / D)
        dx, dg = _rms_bwd(dy, xv, r, g)
        dx_ref[...] = dx
        dxb_ref[...] = dx.astype(BF)
        part = 0.5 * jnp.sum(jnp.mean(err * err, axis=-1, keepdims=True), axis=0, keepdims=True)
        _accumulate_rows(dg_ref, dg, i == 0)
        _accumulate_rows(loss_ref, part, i == 0)

    return pl.pallas_call(
        body, name="final_loss", grid=(T // tm,),
        in_specs=[pl.BlockSpec((tm, D), lambda i: (i, 0)), pl.BlockSpec((1, D), lambda i: (0, 0)),
                  pl.BlockSpec((tm, D), lambda i: (i, 0))],
        out_specs=[pl.BlockSpec((tm, D), lambda i: (i, 0)), pl.BlockSpec((tm, D), lambda i: (i, 0)),
                   pl.BlockSpec((1, D), lambda i: (0, 0)), pl.BlockSpec((1, 1), lambda i: (0, 0))],
        out_shape=[_sds((T, D), F32), _sds((T, D), BF), _sds((1, D), F32), _sds((1, 1), F32)],
        compiler_params=_params(("arbitrary",)),
    )(x3, g_final, target)


def _lane_half(shape, h):
    lane = lax.broadcasted_iota(jnp.int32, shape, 1)
    return (lane >= HEAD_DIM * h) & (lane < HEAD_DIM * (h + 1))


def _to_half(v, w, h):
    if w != h:
        v = pltpu.roll(v, HEAD_DIM, 1)
    return jnp.where(_lane_half(v.shape, h), v, 0.0)


def _attn_block(qkv_ref, sinks_ref, n, h):
    r0 = pl.multiple_of(n * BLOCK, BLOCK)
    p0 = pl.multiple_of(jnp.maximum(n - 1, 0) * BLOCK, BLOCK)
    rows = pl.ds(r0, BLOCK)
    prev = pl.ds(p0, BLOCK)
    k2 = jnp.concatenate([qkv_ref[prev, ATTN_WIDTH:ATTN_WIDTH + KV_WIDTH],
                          qkv_ref[rows, ATTN_WIDTH:ATTN_WIDTH + KV_WIDTH]], axis=0).astype(BF)
    v2 = jnp.concatenate([qkv_ref[prev, ATTN_WIDTH + KV_WIDTH:ATTN_WIDTH + 2 * KV_WIDTH],
                          qkv_ref[rows, ATTN_WIDTH + KV_WIDTH:ATTN_WIDTH + 2 * KV_WIDTH]], axis=0).astype(BF)
    qs = []
    for g in range(GROUP):
        hq = GROUP * h + g
        blk = qkv_ref[rows, (hq // 2) * 128:(hq // 2 + 1) * 128]
        qs.append(_to_half(blk, hq % 2, h))
    q4 = jnp.concatenate(qs, axis=0).astype(BF)
    s = lax.dot_general(q4, k2, _DIMS["NT"], preferred_element_type=F32) * (HEAD_DIM ** -0.5)
    shape = s.shape
    row = lax.broadcasted_iota(jnp.int32, shape, 0)
    qi = row & (BLOCK - 1)
    kj = lax.broadcasted_iota(jnp.int32, shape, 1)
    diff = qi + BLOCK - kj
    valid = (diff >= 0) & (diff < BLOCK) & ((kj >= BLOCK) | (n > 0))
    s = jnp.where(valid, s, NEG)
    row1 = lax.broadcasted_iota(jnp.int32, (shape[0], 1), 0)
    sink = jnp.zeros((shape[0], 1), F32)
    for g in range(GROUP):
        sink = jnp.where((row1 >= g * BLOCK) & (row1 < (g + 1) * BLOCK), sinks_ref[0, GROUP * h + g], sink)
    m = jnp.maximum(jnp.max(s, axis=-1, keepdims=True), sink)
    e = jnp.exp(s - m)
    es = jnp.exp(sink - m)
    inv = 1.0 / (jnp.sum(e, axis=-1, keepdims=True) + es)
    return e * inv, es * inv, q4, k2, v2, rows, prev


def _attn_fwd(proj, sinks, carry=None):
    T = proj.shape[0]
    c_in, c_out, c_sems = _carry_io(carry)

    def body(*refs):
        qkv_ref, sinks_ref = refs[:2]
        ci_refs = refs[2:2 + len(c_in)]
        o_ref = refs[2 + len(c_in)]
        co_refs = refs[3 + len(c_in):3 + len(c_in) + len(c_out)]
        cs_refs = refs[3 + len(c_in) + len(c_out):]
        if carry is not None:
            carry.start(ci_refs, co_refs, cs_refs)

        def blk(n, z):
            outs = [None] * (N_Q_HEADS // 2)
            for h in range(N_KV_HEADS):
                p, _, _, _, v2, rows, _ = _attn_block(qkv_ref, sinks_ref, n, h)
                o = lax.dot_general(p.astype(BF), v2, _DIMS["NN"], preferred_element_type=F32)
                for g in range(GROUP):
                    hq = GROUP * h + g
                    piece = jnp.where(_lane_half((BLOCK, 128), h), o[g * BLOCK:(g + 1) * BLOCK], 0.0)
                    if hq % 2 != h:
                        piece = pltpu.roll(piece, HEAD_DIM, 1)
                    outs[hq // 2] = piece if outs[hq // 2] is None else outs[hq // 2] + piece
            for pb in range(N_Q_HEADS // 2):
                o_ref[rows, pb * 128:(pb + 1) * 128] = outs[pb].astype(BF)
            return z

        lax.fori_loop(0, T // BLOCK, blk, 0)
        if carry is not None:
            carry.finish(ci_refs, co_refs, cs_refs)

    res = pl.pallas_call(
        body, name="attn_fwd", grid=(1,),
        in_specs=[pl.BlockSpec((T, GLU_OFF), lambda i: (0, 0)), pl.BlockSpec(memory_space=pltpu.SMEM),
                  *[ANY] * len(c_in)],
        out_specs=[pl.BlockSpec((T, ATTN_WIDTH), lambda i: (0, 0)), *[ANY] * len(c_out)],
        out_shape=[_sds((T, ATTN_WIDTH), BF), *c_out], scratch_shapes=c_sems,
        compiler_params=_params(("arbitrary",)),
    )(proj, sinks, *c_in)
    return res[0], res[1:]


def _attn_bwd(proj, d_o, sinks, carry=None):
    T = proj.shape[0]
    c_in, c_out, c_sems = _carry_io(carry)

    def body(*refs):
        qkv_ref, do_ref, sinks_ref = refs[:3]
        ci_refs = refs[3:3 + len(c_in)]
        dqkv_ref, dsink_ref = refs[3 + len(c_in):5 + len(c_in)]
        co_refs = refs[5 + len(c_in):5 + len(c_in) + len(c_out)]
        dk_acc, dv_acc = refs[5 + len(c_in) + len(c_out):7 + len(c_in) + len(c_out)]
        cs_refs = refs[7 + len(c_in) + len(c_out):]
        if carry is not None:
            carry.start(ci_refs, co_refs, cs_refs)
        dsink_ref[...] = jnp.zeros_like(dsink_ref)
        dk_acc[...] = jnp.zeros_like(dk_acc)
        dv_acc[...] = jnp.zeros_like(dv_acc)

        def blk(n, carry):
            dqs = [None] * (N_Q_HEADS // 2)
            for h in range(N_KV_HEADS):
                p, psink, q4, k2, v2, rows, prev = _attn_block(qkv_ref, sinks_ref, n, h)
                dos = []
                for g in range(GROUP):
                    hq = GROUP * h + g
                    dos.append(_to_half(do_ref[rows, (hq // 2) * 128:(hq // 2 + 1) * 128].astype(F32), hq % 2, h))
                do4 = jnp.concatenate(dos, axis=0).astype(BF)
                dp = lax.dot_general(do4, v2, _DIMS["NT"], preferred_element_type=F32)
                delta = jnp.sum(p * dp, axis=-1, keepdims=True)
                ds = (p * (dp - delta) * (HEAD_DIM ** -0.5)).astype(BF)
                dsk = psink * delta
                for g in range(GROUP):
                    hq = GROUP * h + g
                    tot = -jnp.sum(dsk[g * BLOCK:(g + 1) * BLOCK], axis=0, keepdims=True)
                    lane = lax.broadcasted_iota(jnp.int32, (1, 128), 1)
                    dsink_ref[...] += jnp.where(lane == hq, tot, 0.0)
                dq = lax.dot_general(ds, k2, _DIMS["NN"], preferred_element_type=F32)
                dk = lax.dot_general(ds, q4, _DIMS["TN"], preferred_element_type=F32)
                dv = lax.dot_general(p.astype(BF), do4, _DIMS["TN"], preferred_element_type=F32)
                dk_acc[prev, :] += dk[:BLOCK]
                dk_acc[rows, :] += dk[BLOCK:]
                dv_acc[prev, :] += dv[:BLOCK]
                dv_acc[rows, :] += dv[BLOCK:]
                for g in range(GROUP):
                    hq = GROUP * h + g
                    piece = jnp.where(_lane_half((BLOCK, 128), h), dq[g * BLOCK:(g + 1) * BLOCK], 0.0)
                    if hq % 2 != h:
                        piece = pltpu.roll(piece, HEAD_DIM, 1)
                    dqs[hq // 2] = piece if dqs[hq // 2] is None else dqs[hq // 2] + piece
            for pb in range(N_Q_HEADS // 2):
                dqkv_ref[rows, pb * 128:(pb + 1) * 128] = dqs[pb].astype(BF)
            return carry

        lax.fori_loop(0, T // BLOCK, blk, 0)
        dqkv_ref[:, ATTN_WIDTH:ATTN_WIDTH + KV_WIDTH] = dk_acc[...].astype(BF)
        dqkv_ref[:, ATTN_WIDTH + KV_WIDTH:] = dv_acc[...].astype(BF)
        if carry is not None:
            carry.finish(ci_refs, co_refs, cs_refs)

    res = pl.pallas_call(
        body, name="attn_bwd", grid=(1,),
        in_specs=[pl.BlockSpec((T, GLU_OFF), lambda i: (0, 0)), pl.BlockSpec((T, ATTN_WIDTH), lambda i: (0, 0)),
                  pl.BlockSpec(memory_space=pltpu.SMEM), *[ANY] * len(c_in)],
        out_specs=[pl.BlockSpec((T, GLU_OFF), lambda i: (0, 0)), pl.BlockSpec((1, 128), lambda i: (0, 0)),
                   *[ANY] * len(c_out)],
        out_shape=[_sds((T, GLU_OFF), BF), _sds((1, 128), F32), *c_out],
        scratch_shapes=[pltpu.VMEM((T, KV_WIDTH), F32), pltpu.VMEM((T, KV_WIDTH), F32), *c_sems],
        compiler_params=_params(("arbitrary",)),
    )(proj, d_o, sinks, *c_in)
    return res[:2], res[2:]


CHUNK = 256
SUB = 32
WIN = CHUNK + 32
PAD_ROWS = SEQ + 2 * CONV_PAD
_GLU_SPECS = [pl.BlockSpec((SEQ, 256), functools.partial(lambda i, c: (0, c), c=GLU_OFF // 256 + c)) for c in range(4)]


def _glu_to_pad(a0, a1, b0, b1, zpad):
    C = CONV_CHANNELS
    zpad[0:CONV_PAD, :] = jnp.zeros((CONV_PAD, C), F32)
    zpad[CONV_PAD + SEQ:, :] = jnp.zeros((CONV_PAD, C), F32)
    zpad[CONV_PAD:CONV_PAD + SEQ, 0:256] = a0[...] * jax.nn.sigmoid(b0[...])
    zpad[CONV_PAD:CONV_PAD + SEQ, 256:C] = a1[...] * jax.nn.sigmoid(b1[...])


def _tap_windows(src, base, win):
    for b in range(8):
        win[b, 0:WIN - 8, :] = src[base + b:base + b + WIN - 8, :]


def _taps(win, w_ref, init, out, flip):
    def sub(si, carry):
        r0 = pl.multiple_of(si * SUB, SUB)
        acc = jnp.broadcast_to(init, (SUB, CONV_CHANNELS))
        for k in range(CONV_WIDTH):
            wk = (CONV_WIDTH - 1 - k) if flip else k
            acc = acc + w_ref[wk:wk + 1, :] * win[k % 8, pl.ds(r0 + 8 * (k // 8), SUB), :]
        out[pl.ds(r0, SUB), :] = acc
        return carry

    lax.fori_loop(0, CHUNK // SUB, sub, 0)


def _tap_grads(win, du, dwacc):
    def sub(si, carry):
        r0 = pl.multiple_of(si * SUB, SUB)
        d = du[pl.ds(r0, SUB), :]
        for k in range(CONV_WIDTH):
            p = d * win[k % 8, pl.ds(r0 + 8 * (k // 8), SUB), :]
            dwacc[8 * k:8 * k + 8, :] += (p[0:8] + p[8:16]) + (p[16:24] + p[24:32])
        return carry

    lax.fori_loop(0, CHUNK // SUB, sub, 0)


def _ln_parts(u):
    mu = jnp.mean(u, axis=-1, keepdims=True)
    xc = u - mu
    rstd = lax.rsqrt(jnp.mean(xc * xc, axis=-1, keepdims=True) + EPS)
    return xc * rstd, rstd


def _conv_fwd(proj, conv_w, conv_b, ln_g, ln_b, carry=None):
    T, C = proj.shape[0], CONV_CHANNELS
    vec = pl.BlockSpec((1, C), lambda i: (0, 0))
    c_in, c_out, c_sems = _carry_io(carry)

    def body(*refs):
        a0, a1, b0, b1, w_ref, cb_ref, g_ref, be_ref = refs[:8]
        ci_refs = refs[8:8 + len(c_in)]
        c_ref = refs[8 + len(c_in)]
        co_refs = refs[9 + len(c_in):9 + len(c_in) + len(c_out)]
        zpad, win, ubuf = refs[9 + len(c_in) + len(c_out):12 + len(c_in) + len(c_out)]
        cs_refs = refs[12 + len(c_in) + len(c_out):]
        if carry is not None:
            carry.start(ci_refs, co_refs, cs_refs)
        _glu_to_pad(a0, a1, b0, b1, zpad)
        for ci in range(T // CHUNK):
            _tap_windows(zpad, ci * CHUNK + CONV_PAD - (CONV_WIDTH - 1), win)
            _taps(win, w_ref, cb_ref[...], ubuf, False)
            xh, _ = _ln_parts(ubuf[...])
            ln = xh * g_ref[...] + be_ref[...]
            c_ref[ci * CHUNK:(ci + 1) * CHUNK, :] = (ln * jax.nn.sigmoid(ln)).astype(BF)
        if carry is not None:
            carry.finish(ci_refs, co_refs, cs_refs)

    res = pl.pallas_call(
        body, name="conv_fwd", grid=(1,),
        in_specs=[*_GLU_SPECS, pl.BlockSpec((CONV_PAD, C), lambda i: (0, 0)), vec, vec, vec, *[ANY] * len(c_in)],
        out_specs=[pl.BlockSpec((T, C), lambda i: (0, 0)), *[ANY] * len(c_out)],
        out_shape=[_sds((T, C), BF), *c_out],
        scratch_shapes=[pltpu.VMEM((PAD_ROWS, C), F32), pltpu.VMEM((8, WIN, C), F32), pltpu.VMEM((CHUNK, C), F32),
                        *c_sems],
        compiler_params=_params(("arbitrary",)),
    )(proj, proj, proj, proj, conv_w, conv_b, ln_g, ln_b, *c_in)
    return res[0], res[1:]


def _conv_bwd(proj, d_c, conv_w, conv_b, ln_g, ln_b, carry=None):
    T, C = proj.shape[0], CONV_CHANNELS
    vec = pl.BlockSpec((1, C), lambda i: (0, 0))
    wspec = pl.BlockSpec((CONV_PAD, C), lambda i: (0, 0))
    c_in, c_out, c_sems = _carry_io(carry)

    def body(*refs):
        a0, a1, b0, b1, dc_ref, w_ref, cb_ref, g_ref, be_ref = refs[:9]
        ci_refs = refs[9:9 + len(c_in)]
        o = 9 + len(c_in)
        dglu_ref, dw_ref, dcb_ref, dg_ref, dbe_ref = refs[o:o + 5]
        co_refs = refs[o + 5:o + 5 + len(c_out)]
        zpad, dupad, win, ubuf, dwacc = refs[o + 5 + len(c_out):o + 10 + len(c_out)]
        cs_refs = refs[o + 10 + len(c_out):]
        if carry is not None:
            carry.start(ci_refs, co_refs, cs_refs)
        _glu_to_pad(a0, a1, b0, b1, zpad)
        dupad[T:, :] = jnp.zeros((2 * CONV_PAD, C), F32)
        dwacc[...] = jnp.zeros_like(dwacc)
        dcb_ref[...] = jnp.zeros_like(dcb_ref)
        dg_ref[...] = jnp.zeros_like(dg_ref)
        dbe_ref[...] = jnp.zeros_like(dbe_ref)
        for ci in range(T // CHUNK):
            rows = slice(ci * CHUNK, (ci + 1) * CHUNK)
            _tap_windows(zpad, ci * CHUNK + CONV_PAD - (CONV_WIDTH - 1), win)
            _taps(win, w_ref, cb_ref[...], ubuf, False)
            xh, rstd = _ln_parts(ubuf[...])
            ln = xh * g_ref[...] + be_ref[...]
            sg = jax.nn.sigmoid(ln)
            dln = dc_ref[rows, :].astype(F32) * (sg * (1.0 + ln * (1.0 - sg)))
            dg_ref[...] += jnp.sum(dln * xh, axis=0, keepdims=True)
            dbe_ref[...] += jnp.sum(dln, axis=0, keepdims=True)
            dxh = dln * g_ref[...]
            du = rstd * (dxh - jnp.mean(dxh, axis=-1, keepdims=True)
                         - xh * jnp.mean(dxh * xh, axis=-1, keepdims=True))
            dupad[rows, :] = du
            dcb_ref[...] += jnp.sum(du, axis=0, keepdims=True)
            _tap_grads(win, dupad.at[rows, :], dwacc)
        for k in range(CONV_WIDTH):
            dw_ref[k:k + 1, :] = jnp.sum(dwacc[8 * k:8 * k + 8, :], axis=0, keepdims=True)
        dw_ref[CONV_WIDTH:, :] = jnp.zeros((CONV_PAD - CONV_WIDTH, C), F32)
        for ci in range(T // CHUNK):
            rows = slice(ci * CHUNK, (ci + 1) * CHUNK)
            _tap_windows(dupad, ci * CHUNK, win)
            _taps(win, w_ref, jnp.zeros((1, C), F32), ubuf, True)
            dz = ubuf[...]
            for half, (a, b) in enumerate(((a0, b0), (a1, b1))):
                sb = jax.nn.sigmoid(b[rows, :])
                dzh = dz[:, half * 256:(half + 1) * 256]
                dglu_ref[rows, half * 256:(half + 1) * 256] = (dzh * sb).astype(BF)
                dglu_ref[rows, C + half * 256:C + (half + 1) * 256] = (dzh * a[rows, :] * sb * (1.0 - sb)).astype(BF)
        if carry is not None:
            carry.finish(ci_refs, co_refs, cs_refs)

    res = pl.pallas_call(
        body, name="conv_bwd", grid=(1,),
        in_specs=[*_GLU_SPECS, pl.BlockSpec((T, C), lambda i: (0, 0)), wspec, vec, vec, vec, *[ANY] * len(c_in)],
        out_specs=[pl.BlockSpec((T, 2 * C), lambda i: (0, 0)), wspec, vec, vec, vec, *[ANY] * len(c_out)],
        out_shape=[_sds((T, 2 * C), BF), _sds((CONV_PAD, C), F32), _sds((1, C), F32), _sds((1, C), F32),
                   _sds((1, C), F32), *c_out],
        scratch_shapes=[pltpu.VMEM((PAD_ROWS, C), F32), pltpu.VMEM((PAD_ROWS, C), F32), pltpu.VMEM((8, WIN, C), F32),
                        pltpu.VMEM((CHUNK, C), F32), pltpu.VMEM((8 * CONV_PAD, C), F32), *c_sems],
        compiler_params=_params(("arbitrary",)),
    )(proj, proj, proj, proj, d_c, conv_w, conv_b, ln_g, ln_b, *c_in)
    return res[:5], res[5:]


_GATE_BLK = GATE_OFF // 256


def _ffn_in_swiglu(h2, wf_t, carry=None):
    T, D = h2.shape
    tm, tn = 512, D_FF // 2
    nj, ni = D_FF // tn, T // tm
    c_in, c_out, c_sems = _carry_io(carry)

    def body(*refs):
        a_ref, bg_ref, bu_ref = refs[:3]
        ci_refs = refs[3:3 + len(c_in)]
        act_ref, g_ref, u_ref = refs[3 + len(c_in):6 + len(c_in)]
        co_refs = refs[6 + len(c_in):6 + len(c_in) + len(c_out)]
        cs_refs = refs[6 + len(c_in) + len(c_out):]
        j, i = pl.program_id(0), pl.program_id(1)
        if carry is not None:
            @pl.when((j == 0) & (i == 0))
            def _():
                carry.start(ci_refs, co_refs, cs_refs)
        a = a_ref[...]
        g = lax.dot_general(a, bg_ref[...], _DIMS["NT"], preferred_element_type=F32)
        u = lax.dot_general(a, bu_ref[...], _DIMS["NT"], preferred_element_type=F32)
        act_ref[...] = (g * jax.nn.sigmoid(g) * u).astype(BF)
        g_ref[...] = g.astype(BF)
        u_ref[...] = u.astype(BF)
        if carry is not None:
            @pl.when((j == nj - 1) & (i == ni - 1))
            def _():
                carry.finish(ci_refs, co_refs, cs_refs)

    t = pl.BlockSpec((tm, tn), lambda j, i: (i, j))
    res = pl.pallas_call(
        body, name="ffn_in_swiglu", grid=(nj, ni),
        in_specs=[pl.BlockSpec((tm, D), lambda j, i: (i, 0)), pl.BlockSpec((tn, D), lambda j, i: (j, 0)),
                  pl.BlockSpec((tn, D), lambda j, i: (nj + j, 0)), *[ANY] * len(c_in)],
        out_specs=[t, t, t, *[ANY] * len(c_out)], out_shape=[*[_sds((T, D_FF), BF)] * 3, *c_out],
        scratch_shapes=c_sems,
        compiler_params=_params(("arbitrary", "arbitrary")),
    )(h2, wf_t, wf_t, *c_in)
    return res[:3], res[3:]


def _local_step(x, target, small, wi_t, conv_w, plan):
    T, D = x.shape
    tm = 1024

    def carried(call, res, carry):
        if carry is None:
            return res
        outs, got = res
        plan.done(call, got)
        return outs

    h, r1 = _rms_fwd("rms_mix", x, small["g_mix_norm"])

    def ep_add(acc, ex, outs, ids, scr):
        outs[0][...] = acc + ex[0][...]

    tn_in = IN_WIDTH // 3
    carry = plan.carry("proj_in")
    proj, = carried("proj_in", _matmul("proj_in", [h], wi_t, "NT", m=T, n=IN_WIDTH, tm=tm, tn=tn_in, epilogue=ep_add,
                                       extra=[(small["b_in"], _row(tn_in))],
                                       outs=[(_sds((T, IN_WIDTH), F32), _tile(tm, tn_in))], carry=carry), carry)
    o, got = _attn_fwd(proj, small["sinks"], carry=plan.carry("attn_fwd"))
    plan.done("attn_fwd", got)
    c, got = _conv_fwd(proj, conv_w, small["conv_b"], small["ln_g"], small["ln_b"], carry=plan.carry("conv_fwd"))
    plan.done("conv_fwd", got)
    wap_t, wcp_t, w_out = plan.weight("w_attn_proj"), plan.weight("w_conv_proj"), plan.weight("w_out")
    ya, = _matmul("attn_proj", [o], wap_t, "NT", m=T, n=D, tm=tm, tn=D, epilogue=_store(F32),
                  outs=[(_sds((T, D), F32), _tile(tm, D))])

    tg = 256
    gate_specs = [pl.BlockSpec((tm, tg), lambda j, i, k: (i, _GATE_BLK + j)),
                  pl.BlockSpec((tm, tg), lambda j, i, k: (i, _GATE_BLK + D // tg + j))]

    def ep_merge(acc, ex, outs, ids, scr):
        yc = acc + ex[0][...]
        outs[0][...] = yc
        outs[1][...] = (jax.nn.sigmoid(ex[2][...]) * ex[1][...] + jax.nn.sigmoid(ex[3][...]) * yc).astype(BF)

    carry = plan.carry("conv_proj_merge")
    yc, merged = carried("conv_proj_merge", _matmul(
        "conv_proj_merge", [c], wcp_t, "NT", m=T, n=D, tm=tm, tn=tg, epilogue=ep_merge,
        extra=[(small["b_conv_proj"], _row(tg)), (ya, _tile(tm, tg)), (proj, gate_specs[0]), (proj, gate_specs[1])],
        outs=[(_sds((T, D), F32), _tile(tm, tg)), (_sds((T, D), BF), _tile(tm, tg))], carry=carry), carry)
    carry = plan.carry("out_proj")
    x2, = carried("out_proj", _matmul("out_proj", [merged], w_out, "NN", m=T, n=D, tm=tm, tn=D, epilogue=ep_add,
                                      extra=[(x, _tile(tm, D))], outs=[(_sds((T, D), F32), _tile(tm, D))],
                                      carry=carry), carry)
    h2, r2 = _rms_fwd("rms_ffn", x2, small["g_ffn_norm"])
    wf_t = plan.weight("w_ffn_in")
    (act, gate, up), got = _ffn_in_swiglu(h2, wf_t, carry=plan.carry("ffn_in_swiglu"))
    plan.done("ffn_in_swiglu", got)
    w_down = plan.weight("w_ffn_down")
    x3, = _matmul("ffn_down", [act], w_down, "NN", m=T, n=D, tm=512, tn=D, epilogue=ep_add,
                  extra=[(x2, _tile(512, D))], outs=[(_sds((T, D), F32), _tile(512, D))])
    dx3, dx3_b, dg_final, loss = _final(x3, small["g_final"], target)

    tn_ff = D_FF // 2

    def ep_swiglu_bwd(acc, ex, outs, ids, scr):
        g, u = ex[0][...].astype(F32), ex[1][...].astype(F32)
        sg = jax.nn.sigmoid(g)
        outs[0][...] = (acc * u * sg * (1.0 + g * (1.0 - sg))).astype(BF)
        outs[1][...] = (acc * g * sg).astype(BF)

    dgate, dup = _matmul(
        "ffn_down_bwd", [dx3_b], w_down, "NT", m=T, n=D_FF, tm=512, tn=tn_ff, epilogue=ep_swiglu_bwd,
        extra=[(gate, _tile(512, tn_ff)), (up, _tile(512, tn_ff))],
        outs=[(_sds((T, D_FF), BF), _tile(512, tn_ff)), (_sds((T, D_FF), BF), _tile(512, tn_ff))])

    def dw(name, a, b, rows, cols, row_off=0, alias=None, total_rows=None, colsum=False):
        total_rows = rows if total_rows is None else total_rows
        tmw = rows if rows <= 1024 else D_FF // 2
        by_dma = row_off % tmw != 0

        def ep(acc, ex, outs, ids, scr):
            if by_dma:
                scr[0][...] = acc.astype(BF)
                pltpu.sync_copy(scr[0], outs[0].at[pl.ds(pl.multiple_of(row_off + ids[1] * tmw, 256), tmw)])
            else:
                outs[0][...] = acc.astype(BF)
            if colsum:
                outs[1][...] = jnp.sum(ex[0][...].astype(F32), axis=0, keepdims=True)

        blk = row_off // tmw
        spec = pl.BlockSpec(memory_space=pl.ANY) if by_dma else pl.BlockSpec((tmw, cols), lambda j, i, k: (blk + i, j))
        outs = [(_sds((total_rows, cols), BF), spec)]
        extra = []
        if colsum:
            extra = [(a, pl.BlockSpec((T, tmw), lambda j, i, k: (0, i)))]
            outs.append((_sds((1, rows), F32), pl.BlockSpec((1, tmw), lambda j, i, k: (0, i))))
        carry = plan.carry(name)
        res = carried(name, _matmul(name, [a], b, "TN", m=rows, n=cols, tm=tmw, tn=cols, epilogue=ep, extra=extra,
                                    outs=outs, alias=None if alias is None else (alias, 0),
                                    scratch=[pltpu.VMEM((tmw, cols), BF)] if by_dma else [], carry=carry), carry)
        return res if colsum else res[0]

    plan.grad_ready(dict(w_ffn_down=dw("ffn_down_dw", act, dx3_b, D_FF, D)))

    def ep_rms_bwd(acc, ex, outs, ids, scr):
        dx, dg = _rms_bwd(acc, ex[0][...], ex[1][...], ex[2][...])
        dx = ex[3][...] + dx
        outs[0][...] = dx
        outs[1][...] = dx.astype(BF)
        _accumulate_rows(outs[2], dg, ids[1] == 0)

    def rms_bwd_io(tm_, xin, r, g, dres):
        return dict(
            extra=[(xin, _tile(tm_, D)), (r, pl.BlockSpec((tm_, 1), lambda j, i, k: (i, 0))), (g, _row(D)),
                   (dres, _tile(tm_, D))],
            outs=[(_sds((T, D), F32), _tile(tm_, D)), (_sds((T, D), BF), _tile(tm_, D)), (_sds((1, D), F32), _row(D))])

    carry = plan.carry("ffn_in_bwd")
    dx2, dx2_b, dg_ffn = carried(
        "ffn_in_bwd",
        _matmul("ffn_in_bwd", [dgate, dup], wf_t, "NN", m=T, n=D, tm=512, tn=D, tk=D_FF, epilogue=ep_rms_bwd,
                carry=carry, **rms_bwd_io(512, x2, r2, small["g_ffn_norm"], dx3)), carry)
    plan.launch("send_down")
    gwf_t = dw("ffn_in_dw_gate", dgate, h2, D_FF, D, total_rows=2 * D_FF)
    gwf_t = dw("ffn_in_dw_up", dup, h2, D_FF, D, row_off=D_FF, alias=gwf_t, total_rows=2 * D_FF)
    plan.grad_ready(dict(w_ffn_in=gwf_t))

    def ep_merge_bwd(acc, ex, outs, ids, scr):
        s0 = jax.nn.sigmoid(ex[2][...])
        s1 = jax.nn.sigmoid(ex[3][...])
        outs[0][...] = (acc * s0).astype(BF)
        outs[1][...] = (acc * s1).astype(BF)
        outs[2][...] = (acc * ex[0][...] * s0 * (1.0 - s0)).astype(BF)
        outs[3][...] = (acc * ex[1][...] * s1 * (1.0 - s1)).astype(BF)

    carry = plan.carry("out_proj_bwd_merge")
    dya, dyc, dg0, dg1 = carried(
        "out_proj_bwd_merge",
        _matmul("out_proj_bwd_merge", [dx2_b], w_out, "NT", m=T, n=D, tm=tm, tn=tg, epilogue=ep_merge_bwd,
                extra=[(ya, _tile(tm, tg)), (yc, _tile(tm, tg)), (proj, gate_specs[0]), (proj, gate_specs[1])],
                outs=[(_sds((T, D), BF), _tile(tm, tg))] * 4, carry=carry), carry)
    plan.launch("send_ffn")
    gw_out = dw("out_proj_dw", merged, dx2_b, D, D)
    d_o, = _matmul("attn_proj_bwd", [dya], wap_t, "NN", m=T, n=ATTN_WIDTH, tm=tm, tn=ATTN_WIDTH,
                   epilogue=_store(BF), outs=[(_sds((T, ATTN_WIDTH), BF), _tile(tm, ATTN_WIDTH))])
    d_c, = _matmul("conv_proj_bwd", [dyc], wcp_t, "NN", m=T, n=CONV_CHANNELS, tm=tm, tn=CONV_CHANNELS,
                   epilogue=_store(BF), outs=[(_sds((T, CONV_CHANNELS), BF), _tile(tm, CONV_CHANNELS))])
    gwap_t = dw("attn_proj_dw", dya, o, D, ATTN_WIDTH)
    gwcp_t, db_cp = dw("conv_proj_dw", dyc, c, D, CONV_CHANNELS, colsum=True)
    plan.grad_ready(dict(w_out=gw_out, w_attn_proj=gwap_t, w_conv_proj=gwcp_t))
    (dglu, dcw, dcb, dlng, dlnb), got = _conv_bwd(proj, d_c, conv_w, small["conv_b"], small["ln_g"], small["ln_b"],
                                                  carry=plan.carry("conv_bwd"))
    plan.done("conv_bwd", got)
    (dqkv, dsinks), got = _attn_bwd(proj, d_o, small["sinks"], carry=plan.carry("attn_bwd"))
    plan.done("attn_bwd", got)

    segs = [dqkv, dglu, dg0, dg1]
    gwi_t, off, db_in = None, 0, []
    for s, seg in enumerate(segs):
        gwi_t, db = dw(f"proj_in_dw{s}", seg, h, seg.shape[1], D, row_off=off, alias=gwi_t, total_rows=IN_WIDTH,
                       colsum=True)
        db_in.append(db)
        off += seg.shape[1]
    plan.grad_ready(dict(w_in=gwi_t))
    plan.alone("swap_inp")
    plan.launch("send_inp")
    carry = plan.carry("proj_in_bwd")
    dx, _, dg_mix = carried(
        "proj_in_bwd",
        _matmul("proj_in_bwd", segs, wi_t, "NN", m=T, n=D, tm=512, tn=D, epilogue=ep_rms_bwd, carry=carry,
                **rms_bwd_io(512, x, r1, small["g_mix_norm"], dx2)), carry)

    parts = dict(g_mix_norm=dg_mix, b_in=db_in, sinks=dsinks, conv_w=dcw, conv_b=dcb, ln_g=dlng, ln_b=dlnb,
                 b_conv_proj=db_cp, g_ffn_norm=dg_ffn, g_final=dg_final, loss=loss)
    return dx, parts


def _place():
    x, y, c = lax.axis_index("x"), lax.axis_index("y"), lax.axis_index("c")
    return x, y, c, [(1 - x, y), (x, 1 - y), (1 - x, 1 - y)]


def _gather_copies(x_refs, out_refs, rows_per, send_sems, recv_sems, local_sems):
    x, y, c, chips = _place()
    me, sibling = (x, y, c), (x, y, 1 - c)

    def rows(a, px, py, pc):
        return out_refs[a].at[pl.ds((4 * px + 2 * py + pc) * rows_per[a], rows_per[a])]

    def copy(a, k, block, to, src=None):
        return pltpu.make_async_remote_copy(
            src_ref=rows(a, *block) if src is None else src, dst_ref=rows(a, *block),
            send_sem=send_sems.at[7 * a + k], recv_sem=recv_sems.at[7 * a + k], device_id=to, device_id_type=MESH)

    def local(a):
        return pltpu.make_async_copy(x_refs[a], rows(a, *me), local_sems.at[a])

    def first(a):
        return [copy(a, 0, me, sibling, src=x_refs[a])] + [copy(a, 1 + j, me, (*chip, c), src=x_refs[a])
                                                          for j, chip in enumerate(chips)]

    def arrive(a, j):
        return copy(a, 1 + j, (*chips[j], c), me)

    def passed(a, j):
        return copy(a, 4 + j, (*chips[j], c), sibling)

    def from_sibling(a):
        return [copy(a, 0, sibling, me)] + [copy(a, 4 + j, (*chip, 1 - c), me) for j, chip in enumerate(chips)]

    return len(x_refs), local, first, arrive, passed, from_sibling


def _gather_start(*refs):
    n, local, first, _, _, _ = _gather_copies(*refs)
    for a in range(n):
        local(a).start()
        for cp in first(a):
            cp.start()


def _gather_finish(*refs):
    n, local, first, arrive, passed, from_sibling = _gather_copies(*refs)
    for a in range(n):
        for j in range(3):
            arrive(a, j).wait_recv()
            passed(a, j).start()
    for a in range(n):
        for cp in from_sibling(a):
            cp.wait_recv()
    for a in range(n):
        for cp in first(a) + [passed(a, j) for j in range(3)]:
            cp.wait_send()
        local(a).wait()


def _gather_blocks(*refs):
    _gather_start(*refs)
    _gather_finish(*refs)


def _gather_peers():
    x, y, c, chips = _place()
    return [(x, y, 1 - c)] + [(*chip, c) for chip in chips]


def _gather_sems(n):
    return [pltpu.SemaphoreType.DMA((7 * n,)), pltpu.SemaphoreType.DMA((7 * n,)), pltpu.SemaphoreType.DMA((n,))]


def _gather_carry(shards):
    rows_per = [s.shape[0] for s in shards]
    return _Carry(shards, [_sds((N_DEV * s.shape[0],) + s.shape[1:], s.dtype) for s in shards],
                  _gather_sems(len(shards)),
                  lambda ins, outs, sems: _gather_start(ins, outs, rows_per, *sems),
                  lambda ins, outs, sems: _gather_finish(ins, outs, rows_per, *sems), _gather_peers)


def _all_gather(shards):
    return _run_carry("weights_all_gather", _gather_carry(shards))


def _swap_carry(grads):
    n = len(grads)

    def copies(g_refs, out_refs, sems):
        send_sems, recv_sems = sems
        x, y, c, _ = _place()
        return [pltpu.make_async_remote_copy(
            src_ref=g_refs[a].at[2 * p + 1 - c], dst_ref=out_refs[a].at[p],
            send_sem=send_sems.at[4 * a + p], recv_sem=recv_sems.at[4 * a + p],
            device_id=(x, y, 1 - c), device_id_type=MESH) for a in range(n) for p in range(4)]

    def start(ins, outs, sems):
        for cp in copies(ins, outs, sems):
            cp.start()

    def finish(ins, outs, sems):
        for cp in copies(ins, outs, sems):
            cp.wait()

    def peers():
        x, y, c, _ = _place()
        return [(x, y, 1 - c)]

    return _Carry(grads, [_sds((4,) + g.shape[1:], g.dtype) for g in grads],
                  [pltpu.SemaphoreType.DMA((4 * n,)), pltpu.SemaphoreType.DMA((4 * n,))], start, finish, peers)


def _join(carries):
    carries = [c for c in carries if c is not None]
    if not carries:
        return None
    n_in = [len(c.arrays) for c in carries]
    n_out = [len(c.out_shapes) for c in carries]
    n_sem = [len(c.sems) for c in carries]

    def parts(refs, counts):
        cuts = [sum(counts[:q]) for q in range(len(counts) + 1)]
        return [refs[cuts[q]:cuts[q + 1]] for q in range(len(counts))]

    def start(ins, outs, sems):
        for c, i, o, s in zip(carries, parts(ins, n_in), parts(outs, n_out), parts(sems, n_sem)):
            c.start(i, o, s)

    def finish(ins, outs, sems):
        for c, i, o, s in zip(carries, parts(ins, n_in), parts(outs, n_out), parts(sems, n_sem)):
            c.finish(i, o, s)

    return _Carry([a for c in carries for a in c.arrays], [o for c in carries for o in c.out_shapes],
                  [s for c in carries for s in c.sems], start, finish)


def _run_carry(name, carry):
    n_in, n_out = len(carry.arrays), len(carry.out_shapes)

    def body(*refs):
        carry.start(refs[:n_in], refs[n_in:n_in + n_out], refs[n_in + n_out:])
        carry.finish(refs[:n_in], refs[n_in:n_in + n_out], refs[n_in + n_out:])

    return pl.pallas_call(body, name=name, in_specs=[ANY] * n_in, out_specs=[ANY] * n_out,
                          out_shape=carry.out_shapes, scratch_shapes=carry.sems)(*carry.arrays)


def _run_carry_async(name, carry, collective_id):
    ins = [jax.new_ref(a, memory_space=pltpu.MemorySpace.HBM) for a in carry.arrays]
    outs = [jax.empty_ref(o, memory_space=pltpu.MemorySpace.HBM) for o in carry.out_shapes]

    @pl.kernel(mesh=plsc.ScalarSubcoreMesh(axis_name="sequencer", num_cores=1), name=name,
               scratch_types=tuple(carry.sems), compiler_params=pltpu.CompilerParams(collective_id=collective_id))
    def launch(*sems):
        barrier = pltpu.get_barrier_semaphore()
        peers = carry.peers()
        for peer in peers:
            pl.semaphore_signal(barrier, inc=1, device_id=peer, device_id_type=MESH)
        pl.semaphore_wait(barrier, len(peers))
        carry.start(ins, outs, sems)
        carry.finish(ins, outs, sems)

    launch()
    return [o[...] for o in outs]


def _chip_sum(name, g, got, c):
    _, rows, cols = g.shape

    def body(c_ref, g_ref, got_ref, o_ref):
        o_ref[...] = (g_ref[...].astype(F32) + got_ref[...].astype(F32)).astype(BF)

    return pl.pallas_call(
        body, name=name,
        grid_spec=pltpu.PrefetchScalarGridSpec(
            num_scalar_prefetch=1, grid=(4,),
            in_specs=[pl.BlockSpec((1, rows, cols), lambda p, c_ref: (2 * p + c_ref[0], 0, 0)),
                      pl.BlockSpec((1, rows, cols), lambda p, c_ref: (p, 0, 0))],
            out_specs=pl.BlockSpec((1, rows, cols), lambda p, c_ref: (p, 0, 0))),
        out_shape=_sds((4, rows, cols), BF),
        compiler_params=_params(("arbitrary",)),
    )(c, g, got)


def _send_carry(sums, ks):
    n, nk = len(sums), len(ks)

    def copies(s_refs, out_refs, sems):
        send_sems, recv_sems = sems
        x, y, c, chips = _place()
        return [pltpu.make_async_remote_copy(
            src_ref=s_refs[a].at[2 * chips[k][0] + chips[k][1]], dst_ref=out_refs[a].at[q],
            send_sem=send_sems.at[nk * a + q], recv_sem=recv_sems.at[nk * a + q],
            device_id=(*chips[k], c), device_id_type=MESH) for a in range(n) for q, k in enumerate(ks)]

    def start(ins, outs, sems):
        for cp in copies(ins, outs, sems):
            cp.start()

    def finish(ins, outs, sems):
        for cp in copies(ins, outs, sems):
            cp.wait()

    def peers():
        x, y, c, chips = _place()
        return [(*chips[k], c) for k in ks]

    return _Carry(sums, [_sds((nk,) + s.shape[1:], s.dtype) for s in sums],
                  [pltpu.SemaphoreType.DMA((nk * n,)), pltpu.SemaphoreType.DMA((nk * n,))], start, finish, peers)


def _grad_total(name, g, got, got3, ids):
    _, rows, cols = g.shape
    n3 = len(got3)

    def body(ids_ref, g_ref, got_ref, *rest):
        o_ref = rest[n3]
        tot = g_ref[0].astype(F32) + got_ref[0].astype(F32)
        for r_ref in rest[:n3]:
            for q in range(r_ref.shape[0]):
                tot = tot + r_ref[q].astype(F32)
        o_ref[...] = tot

    return pl.pallas_call(
        body, name=name,
        grid_spec=pltpu.PrefetchScalarGridSpec(
            num_scalar_prefetch=1, grid=(1,),
            in_specs=[pl.BlockSpec((1, rows, cols), lambda i, ids_ref: (ids_ref[0], 0, 0)),
                      pl.BlockSpec((1, rows, cols), lambda i, ids_ref: (ids_ref[1], 0, 0)),
                      *[pl.BlockSpec(r.shape, lambda i, ids_ref: (0, 0, 0)) for r in got3]],
            out_specs=pl.BlockSpec((rows, cols), lambda i, ids_ref: (0, 0))),
        out_shape=_sds((rows, cols), F32),
        compiler_params=_params(("arbitrary",)),
    )(ids, g, got, *got3)


def _adam_math(w, g, m, v):
    m = ADAM_B1 * m + (1.0 - ADAM_B1) * g
    v = ADAM_B2 * v + (1.0 - ADAM_B2) * (g * g)
    m_hat = m / (1.0 - ADAM_B1 ** ADAM_STEP)
    v_hat = v / (1.0 - ADAM_B2 ** ADAM_STEP)
    delta = -ADAM_LR * (m_hat / (jnp.sqrt(v_hat) + ADAM_EPS) + ADAM_WD * w)
    return delta, m, v


def _adamw(name, w, g, m, v):
    rows, cols = w.shape
    tr = 256 if rows % 256 == 0 else rows

    def body(w_ref, g_ref, m_ref, v_ref, d_ref, nm_ref, nv_ref):
        d_ref[...], nm_ref[...], nv_ref[...] = _adam_math(w_ref[...], g_ref[...], m_ref[...], v_ref[...])

    t = pl.BlockSpec((tr, cols), lambda i: (i, 0))
    return pl.pallas_call(
        body, name=name, grid=(rows // tr,), in_specs=[t] * 4, out_specs=[t] * 3,
        out_shape=[_sds((rows, cols), F32)] * 3, compiler_params=_params(("arbitrary",)),
    )(w, g, m, v)


SMALL_NAMES = ["g_mix_norm", "b_in", "sinks", "conv_b", "ln_g", "ln_b", "b_conv_proj", "g_ffn_norm", "g_final"]
_PACK_ROWS = 32


def _small_all_reduce(parts):
    C = CONV_CHANNELS
    part_list = [parts["g_mix_norm"], *parts["b_in"], parts["sinks"], parts["conv_b"], parts["ln_g"], parts["ln_b"],
                 parts["b_conv_proj"], parts["g_ffn_norm"], parts["g_final"], parts["loss"], parts["conv_w"]]
    n_part = len(part_list)

    def body(*refs):
        (p_mix, p_b0, p_b1, p_b2, p_b3, p_sink, p_cb, p_lg, p_lb, p_bcp, p_ffn, p_fin, p_loss, p_cw) = refs[:n_part]
        tot_ref, pack, gathered, send_sems, recv_sems, local_sems = refs[n_part:]
        pack[...] = jnp.zeros_like(pack)
        pack[0:1, :] = p_mix[...]
        pack[1:2, 0:GLU_OFF] = p_b0[...]
        pack[2:3, :] = p_b1[...]
        pack[3:4, :] = p_b2[...]
        pack[4:5, :] = p_b3[...]
        pack[5:6, 0:128] = p_sink[...]
        pack[6:7, 0:C] = p_cb[...]
        pack[6:7, C:2 * C] = p_lg[...]
        pack[7:8, 0:C] = p_lb[...]
        pack[8:9, :] = p_bcp[...]
        pack[9:10, :] = p_ffn[...]
        pack[10:11, :] = p_fin[...]
        pack[11:12, 0:128] = jnp.broadcast_to(p_loss[...], (1, 128))
        pack[12:28, 0:C] = p_cw[0:16, :]
        pack[12:28, C:2 * C] = p_cw[16:32, :]
        _gather_blocks([pack], [gathered], [_PACK_ROWS], send_sems, recv_sems, local_sems)
        tot = gathered[0:_PACK_ROWS, :]
        for d in range(1, N_DEV):
            tot = tot + gathered[d * _PACK_ROWS:(d + 1) * _PACK_ROWS, :]
        tot_ref[...] = tot

    vm = pl.BlockSpec(memory_space=pltpu.VMEM)
    return pl.pallas_call(
        body, name="small_all_reduce",
        in_specs=[vm] * n_part, out_specs=vm, out_shape=_sds((_PACK_ROWS, D_MODEL), F32),
        scratch_shapes=[pltpu.VMEM((_PACK_ROWS, D_MODEL), F32), pltpu.VMEM((N_DEV * _PACK_ROWS, D_MODEL), F32),
                        pltpu.SemaphoreType.DMA((7,)), pltpu.SemaphoreType.DMA((7,)), pltpu.SemaphoreType.DMA((1,))],
        compiler_params=pltpu.CompilerParams(vmem_limit_bytes=VMEM_LIMIT_BYTES),
    )(*part_list)


def _small_adamw(tot, small_w, small_m, small_v):
    C = CONV_CHANNELS
    names = SMALL_NAMES
    widths = [small_w[k].shape[1] for k in names]
    n_small = len(names)

    def body(*refs):
        tot_ref = refs[0]
        w_refs = refs[1:1 + n_small]
        m_refs = refs[1 + n_small:1 + 2 * n_small]
        v_refs = refs[1 + 2 * n_small:1 + 3 * n_small]
        o = 1 + 3 * n_small
        loss_ref, cw_ref = refs[o], refs[o + 1]
        out_refs = refs[o + 2:o + 2 + 4 * n_small]
        tot = tot_ref[...]
        loss_ref[...] = tot[11:12, 0:1]
        cw_ref[0:16, :] = tot[12:28, 0:C]
        cw_ref[16:32, :] = tot[12:28, C:2 * C]
        grads = dict(
            g_mix_norm=tot[0:1, :],
            b_in=jnp.concatenate([tot[1:2, 0:GLU_OFF], tot[2:3, :], tot[3:4, :], tot[4:5, :]], axis=1),
            sinks=tot[5:6, 0:N_Q_HEADS], conv_b=tot[6:7, 0:C], ln_g=tot[6:7, C:2 * C], ln_b=tot[7:8, 0:C],
            b_conv_proj=tot[8:9, :], g_ffn_norm=tot[9:10, :], g_final=tot[10:11, :])
        for s, k in enumerate(names):
            g = grads[k]
            d, nm, nv = _adam_math(w_refs[s][...], g, m_refs[s][...], v_refs[s][...])
            out_refs[4 * s][...] = g
            out_refs[4 * s + 1][...] = d
            out_refs[4 * s + 2][...] = nm
            out_refs[4 * s + 3][...] = nv

    vm = pl.BlockSpec(memory_space=pltpu.VMEM)
    args = [tot, *[small_w[k] for k in names], *[small_m[k] for k in names], *[small_v[k] for k in names]]
    out_shape = [_sds((1, 1), F32), _sds((CONV_PAD, C), F32)]
    for wd in widths:
        out_shape += [_sds((1, wd), F32)] * 4
    res = pl.pallas_call(
        body, name="small_adamw",
        in_specs=[vm] * len(args), out_specs=[vm] * len(out_shape), out_shape=out_shape,
        compiler_params=pltpu.CompilerParams(vmem_limit_bytes=VMEM_LIMIT_BYTES),
    )(*args)
    return res[0], res[1], {k: res[2 + 4 * s:6 + 4 * s] for s, k in enumerate(names)}


BIG = dict(w_in=True, w_attn_proj=True, w_conv_proj=True, w_out=False, w_ffn_in=True, w_ffn_down=False)
WEIGHT_NAMES = ["g_mix_norm", "w_in", "b_in", "sinks", "conv_w", "conv_b", "ln_g", "ln_b", "w_attn_proj",
                "w_conv_proj", "b_conv_proj", "w_out", "g_ffn_norm", "w_ffn_in", "w_ffn_down", "g_final"]


class _Plan:
    GROUPS = dict(down=["w_ffn_down"], ffn=["w_ffn_in"], mix=["w_out", "w_attn_proj", "w_conv_proj"], inp=["w_in"])
    ALL = (0, 1, 2)
    RIDES = dict(
        proj_in=[("gather", ["w_attn_proj", "w_conv_proj", "w_out"])],
        gather_ffn=[("gather", ["w_ffn_in", "w_ffn_down"])],
        ffn_in_bwd=[("swap", "down")], send_down=[("send", "down", ALL)],
        out_proj_bwd_merge=[("swap", "ffn")], send_ffn=[("send", "ffn", ALL)],
        conv_bwd=[("swap", "mix")], attn_bwd=[("send", "mix", ALL)],
        swap_inp=[("swap", "inp")], send_inp=[("send", "inp", ALL)])
    ASYNC = dict(gather_ffn=1, send_down=2, send_ffn=3, send_inp=4)

    def __init__(self, shards, c1):
        self.shards, self.c1 = shards, c1
        self.full, self.slots, self.got, self.sums, self.got3 = {}, {}, {}, {}, {}

    def weight(self, name):
        return self.full[name]

    def grad_ready(self, grads):
        for k, g in grads.items():
            self.slots[k] = g.reshape(N_DEV, g.shape[0] // N_DEV, g.shape[1])

    def _one(self, kind, what, ks=None):
        if kind == "gather":
            return _gather_carry([self.shards[k] for k in what])
        names = self.GROUPS[what]
        if kind == "swap":
            return _swap_carry([self.slots[k] for k in names])
        return _send_carry([self.sums[k] for k in names], ks)

    def carry(self, call):
        return _join([self._one(*ride) for ride in self.RIDES.get(call, [])])

    def done(self, call, outs):
        outs = list(outs)
        for kind, what, *_ in self.RIDES.get(call, []):
            names = what if kind == "gather" else self.GROUPS[what]
            mine, outs = outs[:len(names)], outs[len(names):]
            if kind == "gather":
                self.full.update(zip(names, mine))
            elif kind == "send":
                for k, r in zip(names, mine):
                    self.got3.setdefault(k, []).append(r)
            else:
                for k, r in zip(names, mine):
                    self.got[k] = r
                    self.sums[k] = _chip_sum(f"chip_sum_{k}", self.slots[k], r, self.c1)

    def alone(self, call):
        self.done(call, _run_carry(call, self.carry(call)))

    def launch(self, call, after=None):
        carry = self._one(*self.RIDES[call][0])
        if after is not None:
            carry.arrays = list(lax.optimization_barrier((tuple(carry.arrays), after))[0])
        self.done(call, _run_carry_async(call, carry, self.ASYNC[call]))


def kernel(x, g_mix_norm, w_in, b_in, sinks, conv_w, conv_b, ln_g, ln_b, w_attn_proj, w_conv_proj, b_conv_proj, w_out, g_ffn_norm, w_ffn_in, w_ffn_down, g_final, loss_target, m_g_mix_norm, m_w_in, m_b_in, m_sinks, m_conv_w, m_conv_b, m_ln_g, m_ln_b, m_w_attn_proj, m_w_conv_proj, m_b_conv_proj, m_w_out, m_g_ffn_norm, m_w_ffn_in, m_w_ffn_down, m_g_final, v_g_mix_norm, v_w_in, v_b_in, v_sinks, v_conv_w, v_conv_b, v_ln_g, v_ln_b, v_w_attn_proj, v_w_conv_proj, v_b_conv_proj, v_w_out, v_g_ffn_norm, v_w_ffn_in, v_w_ffn_down, v_g_final):
    w = dict(g_mix_norm=g_mix_norm, w_in=w_in, b_in=b_in, sinks=sinks, conv_w=conv_w, conv_b=conv_b, ln_g=ln_g,
             ln_b=ln_b, w_attn_proj=w_attn_proj, w_conv_proj=w_conv_proj, b_conv_proj=b_conv_proj, w_out=w_out,
             g_ffn_norm=g_ffn_norm, w_ffn_in=w_ffn_in, w_ffn_down=w_ffn_down, g_final=g_final)
    m = dict(g_mix_norm=m_g_mix_norm, w_in=m_w_in, b_in=m_b_in, sinks=m_sinks, conv_w=m_conv_w, conv_b=m_conv_b,
             ln_g=m_ln_g, ln_b=m_ln_b, w_attn_proj=m_w_attn_proj, w_conv_proj=m_w_conv_proj,
             b_conv_proj=m_b_conv_proj, w_out=m_w_out, g_ffn_norm=m_g_ffn_norm, w_ffn_in=m_w_ffn_in,
             w_ffn_down=m_w_ffn_down, g_final=m_g_final)
    v = dict(g_mix_norm=v_g_mix_norm, w_in=v_w_in, b_in=v_b_in, sinks=v_sinks, conv_w=v_conv_w, conv_b=v_conv_b,
             ln_g=v_ln_g, ln_b=v_ln_b, w_attn_proj=v_w_attn_proj, w_conv_proj=v_w_conv_proj,
             b_conv_proj=v_b_conv_proj, w_out=v_w_out, g_ffn_norm=v_g_ffn_norm, w_ffn_in=v_w_ffn_in,
             w_ffn_down=v_w_ffn_down, g_final=v_g_final)
    ax, ay, ac = lax.axis_index("x"), lax.axis_index("y"), lax.axis_index("c")
    me = 4 * ax + 2 * ay + ac
    chip = 2 * ax + ay

    shards = {k: (w[k][0].T if tr else w[k][0]).astype(BF) for k, tr in BIG.items()}
    cw_shard = jnp.pad(conv_w[0].T, ((0, 0), (0, 1))).reshape(16, 128)
    wi_t, cw_full = _all_gather([shards["w_in"], cw_shard])
    conv_full = cw_full.reshape(CONV_CHANNELS, CONV_PAD).T

    as_row = lambda a: a.reshape(1, -1)
    small_w = {k: as_row(w[k]) for k in SMALL_NAMES}
    small_m = {k: as_row(m[k]) for k in SMALL_NAMES}
    small_v = {k: as_row(v[k]) for k in SMALL_NAMES}
    plan = _Plan(shards, ac.reshape(1).astype(jnp.int32))
    plan.launch("gather_ffn", after=wi_t)
    dx, parts = _local_step(x[0], loss_target[0], small_w, wi_t, conv_full, plan)

    ids = jnp.stack([me, chip]).astype(jnp.int32)
    grads, delta, new_m, new_v = {}, {}, {}, {}
    for k in sorted(BIG, key=lambda k: k == "w_in"):
        tot = _grad_total(f"grad_total_{k}", plan.slots[k], plan.got[k], plan.got3[k], ids)
        tot = tot.T if BIG[k] else tot
        d, nm, nv = _adamw(f"adamw_{k}", w[k][0], tot, m[k][0], v[k][0])
        grads[k], delta[k], new_m[k], new_v[k] = tot[None], d[None], nm[None], nv[None]

    loss, cw_grad, small_out = _small_adamw(_small_all_reduce(parts), small_w, small_m, small_v)
    for k in SMALL_NAMES:
        g, d, nm, nv = (a.reshape(w[k].shape) for a in small_out[k])
        grads[k], delta[k], new_m[k], new_v[k] = g, d, nm, nv
    cw_mine = lax.dynamic_slice(cw_grad, (0, me * 64), (CONV_WIDTH, 64))
    d, nm, nv = _adamw("adamw_conv_w", conv_w[0], cw_mine, m_conv_w[0], v_conv_w[0])
    grads["conv_w"], delta["conv_w"], new_m["conv_w"], new_v["conv_w"] = cw_mine[None], d[None], nm[None], nv[None]

    return (loss.reshape(()), dx[None], *[grads[k] for k in WEIGHT_NAMES], *[delta[k] for k in WEIGHT_NAMES],
            *[new_m[k] for k in WEIGHT_NAMES], *[new_v[k] for k in WEIGHT_NAMES])
```

```python
import functools

import jax
import jax.numpy as jnp
from jax import lax
from jax.experimental import pallas as pl
from jax.experimental.pallas import tpu as pltpu
from jax.experimental.pallas import tpu_sc as plsc

F32 = jnp.float32
BF = jnp.bfloat16

SEQ = 2048
D_MODEL = 1024
HEAD_DIM = 64
N_Q_HEADS = 8
N_KV_HEADS = 2
GROUP = N_Q_HEADS // N_KV_HEADS
BLOCK = 128
ATTN_WIDTH = 512
KV_WIDTH = 128
CONV_CHANNELS = 512
CONV_WIDTH = 31
CONV_PAD = 32
GLU_OFF = 768
GATE_OFF = 1792
IN_WIDTH = 3840
D_FF = 2816
EPS = 1e-5
NEG = -1e30
N_DEV = 8

ADAM_LR = 0.001
ADAM_B1 = 0.9
ADAM_B2 = 0.999
ADAM_EPS = 1e-08
ADAM_WD = 0.01
ADAM_STEP = 10

VMEM_LIMIT_BYTES = 56 * 1024 * 1024
MESH = pl.DeviceIdType.MESH
ANY = pl.BlockSpec(memory_space=pl.ANY)

_DIMS = {"NN": (((1,), (0,)), ((), ())), "NT": (((1,), (1,)), ((), ())), "TN": (((0,), (0,)), ((), ()))}


def _params(sem):
    return pltpu.CompilerParams(dimension_semantics=sem, vmem_limit_bytes=VMEM_LIMIT_BYTES)


class _Carry:
    def __init__(self, arrays, out_shapes, sems, start, finish, peers=None):
        self.arrays, self.out_shapes, self.sems, self.start, self.finish = arrays, out_shapes, sems, start, finish
        self.peers = peers


def _carry_io(carry):
    if carry is None:
        return [], [], []
    return list(carry.arrays), list(carry.out_shapes), list(carry.sems)


def _matmul(name, a_list, b, mode, *, m, n, tm, tn, tk=None, epilogue, extra=(), outs, b_off=(0, 0), alias=None,
            scratch=(), carry=None):
    seg_k = [a.shape[0] if mode == "TN" else a.shape[1] for a in a_list]
    whole = tk is None
    seg_nk = [1] * len(a_list) if whole else [ks // tk for ks in seg_k]
    nk = 1 if whole else sum(seg_nk)
    starts = [sum(seg_nk[:s]) for s in range(len(seg_nk))]
    k_starts = [sum(seg_k[:s]) for s in range(len(seg_k))]
    k_tot = sum(seg_k)
    n_a, n_extra, n_out = len(a_list), len(extra), len(outs)

    a_specs = []
    for st, ns, ks in zip(starts, seg_nk, seg_k):
        if mode == "TN":
            a_specs.append(pl.BlockSpec((ks if whole else tk, tm), lambda j, i, k: (k, i)))
        elif whole:
            a_specs.append(pl.BlockSpec((tm, ks), lambda j, i, k: (i, 0)))
        else:
            a_specs.append(pl.BlockSpec((tm, tk), functools.partial(
                lambda j, i, k, st, ns: (i, jnp.clip(k - st, 0, ns - 1)), st=st, ns=ns)))
    bk = k_tot if whole else tk
    if mode == "NT":
        b_spec = pl.BlockSpec((tn, bk), lambda j, i, k: (b_off[0] + j, b_off[1] + k))
    else:
        b_spec = pl.BlockSpec((bk, tn), lambda j, i, k: (b_off[0] + k, b_off[1] + j))
    n_alias = 0 if alias is None else 1
    c_in, c_out, c_sems = _carry_io(carry)
    n_acc = 0 if whole else 1
    nj, ni = n // tn, m // tm

    def body(*refs):
        pos = [n_a, 1, n_alias, n_extra, len(c_in), n_out, len(c_out), n_acc, len(scratch), len(c_sems)]
        cuts = [sum(pos[:q]) for q in range(len(pos) + 1)]
        a_refs, (b_ref,), _, ex, ci_refs, out_refs, co_refs, acc_refs, scr, cs_refs = (
            refs[cuts[q]:cuts[q + 1]] for q in range(len(pos)))
        j, i, k = pl.program_id(0), pl.program_id(1), pl.program_id(2)
        ids = (j, i)
        if carry is not None:
            @pl.when((j == 0) & (i == 0) & (k == 0))
            def _():
                carry.start(ci_refs, co_refs, cs_refs)

        def dot(a_ref, bv):
            return lax.dot_general(a_ref[...].astype(BF), bv.astype(BF), _DIMS[mode], preferred_element_type=F32)

        if whole:
            tot = None
            for a_ref, k0, ks in zip(a_refs, k_starts, seg_k):
                if n_a == 1:
                    bv = b_ref[...]
                else:
                    bv = b_ref[:, k0:k0 + ks] if mode == "NT" else b_ref[k0:k0 + ks, :]
                part = dot(a_ref, bv)
                tot = part if tot is None else tot + part
            epilogue(tot, ex, out_refs, ids, scr)
        else:
            acc, = acc_refs

            @pl.when(k == 0)
            def _():
                acc[...] = jnp.zeros_like(acc)

            for a_ref, st, ns in zip(a_refs, starts, seg_nk):
                if n_a == 1:
                    acc[...] += dot(a_ref, b_ref[...])
                else:
                    @pl.when((k >= st) & (k < st + ns))
                    def _(a_ref=a_ref):
                        acc[...] += dot(a_ref, b_ref[...])

            @pl.when(k == nk - 1)
            def _():
                epilogue(acc[...], ex, out_refs, ids, scr)

        if carry is not None:
            @pl.when((j == nj - 1) & (i == ni - 1) & (k == nk - 1))
            def _():
                carry.finish(ci_refs, co_refs, cs_refs)

    in_specs = [*a_specs, b_spec]
    args = [*a_list, b]
    io_alias = {}
    if alias is not None:
        in_specs.append(pl.BlockSpec(memory_space=pl.ANY))
        args.append(alias[0])
        io_alias = {n_a + 1: alias[1]}
    in_specs += [s for _, s in extra] + [pl.BlockSpec(memory_space=pl.ANY)] * len(c_in)
    args += [x for x, _ in extra] + c_in
    res = pl.pallas_call(
        body, name=name, grid=(nj, ni, nk), in_specs=in_specs,
        out_specs=[s for _, s in outs] + [pl.BlockSpec(memory_space=pl.ANY)] * len(c_out),
        out_shape=[o for o, _ in outs] + c_out,
        scratch_shapes=[*([] if whole else [pltpu.VMEM((tm, tn), F32)]), *scratch, *c_sems],
        input_output_aliases=io_alias,
        compiler_params=_params(("arbitrary", "arbitrary", "arbitrary")),
    )(*args)
    return res if carry is None else (res[:n_out], res[n_out:])


def _tile(tm, tn):
    return pl.BlockSpec((tm, tn), lambda j, i, k: (i, j))


def _row(tn):
    return pl.BlockSpec((1, tn), lambda j, i, k: (0, j))


def _store(dtype):
    def ep(acc, ex, outs, ids, scr):
        outs[0][...] = acc.astype(dtype)
    return ep


def _sds(shape, dtype):
    return jax.ShapeDtypeStruct(shape, dtype)


def _rms_fwd(name, x, g):
    T, D = x.shape
    tm = 512

    def body(x_ref, g_ref, h_ref, r_ref):
        xv = x_ref[...]
        r = lax.rsqrt(jnp.mean(xv * xv, axis=-1, keepdims=True) + EPS)
        h_ref[...] = (xv * r * g_ref[...]).astype(BF)
        r_ref[...] = r

    return pl.pallas_call(
        body, name=name, grid=(T // tm,),
        in_specs=[pl.BlockSpec((tm, D), lambda i: (i, 0)), pl.BlockSpec((1, D), lambda i: (0, 0))],
        out_specs=[pl.BlockSpec((tm, D), lambda i: (i, 0)), pl.BlockSpec((tm, 1), lambda i: (i, 0))],
        out_shape=[_sds((T, D), BF), _sds((T, 1), F32)],
        compiler_params=_params(("arbitrary",)),
    )(x, g)


def _rms_bwd(dh, xv, r, g):
    xh = xv * r
    dxh = dh * g
    dx = r * (dxh - xh * jnp.mean(dxh * xh, axis=-1, keepdims=True))
    return dx, jnp.sum(dh * xh, axis=0, keepdims=True)


def _accumulate_rows(ref, val, first):
    @pl.when(first)
    def _():
        ref[...] = val

    @pl.when(jnp.logical_not(first))
    def _():
        ref[...] += val


def _final(x3, g_final, target):
    T, D = x3.shape
    tm = 512

    def body(x_ref, g_ref, t_ref, dx_ref, dxb_ref, dg_ref, loss_ref):
        i = pl.program_id(0)
        xv = x_ref[...]
        g = g_ref[...]
        r = lax.rsqrt(jnp.mean(xv * xv, axis=-1, keepdims=True) + EPS)
        err = xv * r * g - t_ref[...]
        dy = err * (1.0 / D)
        dx, dg = _rms_bwd(dy, xv, r, g)
        dx_ref[...] = dx
        dxb_ref[...] = dx.astype(BF)
        part = 0.5 * jnp.sum(jnp.mean(err * err, axis=-1, keepdims=True), axis=0, keepdims=True)
        _accumulate_rows(dg_ref, dg, i == 0)
        _accumulate_rows(loss_ref, part, i == 0)

    return pl.pallas_call(
        body, name="final_loss", grid=(T // tm,),
        in_specs=[pl.BlockSpec((tm, D), lambda i: (i, 0)), pl.BlockSpec((1, D), lambda i: (0, 0)),
                  pl.BlockSpec((tm, D), lambda i: (i, 0))],
        out_specs=[pl.BlockSpec((tm, D), lambda i: (i, 0)), pl.BlockSpec((tm, D), lambda i: (i, 0)),
                   pl.BlockSpec((1, D), lambda i: (0, 0)), pl.BlockSpec((1, 1), lambda i: (0, 0))],
        out_shape=[_sds((T, D), F32), _sds((T, D), BF), _sds((1, D), F32), _sds((1, 1), F32)],
        compiler_params=_params(("arbitrary",)),
    )(x3, g_final, target)


def _lane_half(shape, h):
    lane = lax.broadcasted_iota(jnp.int32, shape, 1)
    return (lane >= HEAD_DIM * h) & (lane < HEAD_DIM * (h + 1))


def _to_half(v, w, h):
    if w != h:
        v = pltpu.roll(v, HEAD_DIM, 1)
    return jnp.where(_lane_half(v.shape, h), v, 0.0)


def _attn_block(qkv_ref, sinks_ref, n, h):
    r0 = pl.multiple_of(n * BLOCK, BLOCK)
    p0 = pl.multiple_of(jnp.maximum(n - 1, 0) * BLOCK, BLOCK)
    rows = pl.ds(r0, BLOCK)
    prev = pl.ds(p0, BLOCK)
    k2 = jnp.concatenate([qkv_ref[prev, ATTN_WIDTH:ATTN_WIDTH + KV_WIDTH],
                          qkv_ref[rows, ATTN_WIDTH:ATTN_WIDTH + KV_WIDTH]], axis=0).astype(BF)
    v2 = jnp.concatenate([qkv_ref[prev, ATTN_WIDTH + KV_WIDTH:ATTN_WIDTH + 2 * KV_WIDTH],
                          qkv_ref[rows, ATTN_WIDTH + KV_WIDTH:ATTN_WIDTH + 2 * KV_WIDTH]], axis=0).astype(BF)
    qs = []
    for g in range(GROUP):
        hq = GROUP * h + g
        blk = qkv_ref[rows, (hq // 2) * 128:(hq // 2 + 1) * 128]
        qs.append(_to_half(blk, hq % 2, h))
    q4 = jnp.concatenate(qs, axis=0).astype(BF)
    s = lax.dot_general(q4, k2, _DIMS["NT"], preferred_element_type=F32) * (HEAD_DIM ** -0.5)
    shape = s.shape
    row = lax.broadcasted_iota(jnp.int32, shape, 0)
    qi = row & (BLOCK - 1)
    kj = lax.broadcasted_iota(jnp.int32, shape, 1)
    diff = qi + BLOCK - kj
    valid = (diff >= 0) & (diff < BLOCK) & ((kj >= BLOCK) | (n > 0))
    s = jnp.where(valid, s, NEG)
    row1 = lax.broadcasted_iota(jnp.int32, (shape[0], 1), 0)
    sink = jnp.zeros((shape[0], 1), F32)
    for g in range(GROUP):
        sink = jnp.where((row1 >= g * BLOCK) & (row1 < (g + 1) * BLOCK), sinks_ref[0, GROUP * h + g], sink)
    m = jnp.maximum(jnp.max(s, axis=-1, keepdims=True), sink)
    e = jnp.exp(s - m)
    es = jnp.exp(sink - m)
    inv = 1.0 / (jnp.sum(e, axis=-1, keepdims=True) + es)
    return e * inv, es * inv, q4, k2, v2, rows, prev


def _attn_fwd(proj, sinks, carry=None):
    T = proj.shape[0]
    c_in, c_out, c_sems = _carry_io(carry)

    def body(*refs):
        qkv_ref, sinks_ref = refs[:2]
        ci_refs = refs[2:2 + len(c_in)]
        o_ref = refs[2 + len(c_in)]
        co_refs = refs[3 + len(c_in):3 + len(c_in) + len(c_out)]
        cs_refs = refs[3 + len(c_in) + len(c_out):]
        if carry is not None:
            carry.start(ci_refs, co_refs, cs_refs)

        def blk(n, z):
            outs = [None] * (N_Q_HEADS // 2)
            for h in range(N_KV_HEADS):
                p, _, _, _, v2, rows, _ = _attn_block(qkv_ref, sinks_ref, n, h)
                o = lax.dot_general(p.astype(BF), v2, _DIMS["NN"], preferred_element_type=F32)
                for g in range(GROUP):
                    hq = GROUP * h + g
                    piece = jnp.where(_lane_half((BLOCK, 128), h), o[g * BLOCK:(g + 1) * BLOCK], 0.0)
                    if hq % 2 != h:
                        piece = pltpu.roll(piece, HEAD_DIM, 1)
                    outs[hq // 2] = piece if outs[hq // 2] is None else outs[hq // 2] + piece
            for pb in range(N_Q_HEADS // 2):
                o_ref[rows, pb * 128:(pb + 1) * 128] = outs[pb].astype(BF)
            return z

        lax.fori_loop(0, T // BLOCK, blk, 0)
        if carry is not None:
            carry.finish(ci_refs, co_refs, cs_refs)

    res = pl.pallas_call(
        body, name="attn_fwd", grid=(1,),
        in_specs=[pl.BlockSpec((T, GLU_OFF), lambda i: (0, 0)), pl.BlockSpec(memory_space=pltpu.SMEM),
                  *[ANY] * len(c_in)],
        out_specs=[pl.BlockSpec((T, ATTN_WIDTH), lambda i: (0, 0)), *[ANY] * len(c_out)],
        out_shape=[_sds((T, ATTN_WIDTH), BF), *c_out], scratch_shapes=c_sems,
        compiler_params=_params(("arbitrary",)),
    )(proj, sinks, *c_in)
    return res[0], res[1:]


def _attn_bwd(proj, d_o, sinks, carry=None):
    T = proj.shape[0]
    c_in, c_out, c_sems = _carry_io(carry)

    def body(*refs):
        qkv_ref, do_ref, sinks_ref = refs[:3]
        ci_refs = refs[3:3 + len(c_in)]
        dqkv_ref, dsink_ref = refs[3 + len(c_in):5 + len(c_in)]
        co_refs = refs[5 + len(c_in):5 + len(c_in) + len(c_out)]
        dk_acc, dv_acc = refs[5 + len(c_in) + len(c_out):7 + len(c_in) + len(c_out)]
        cs_refs = refs[7 + len(c_in) + len(c_out):]
        if carry is not None:
            carry.start(ci_refs, co_refs, cs_refs)
        dsink_ref[...] = jnp.zeros_like(dsink_ref)
        dk_acc[...] = jnp.zeros_like(dk_acc)
        dv_acc[...] = jnp.zeros_like(dv_acc)

        def blk(n, carry):
            dqs = [None] * (N_Q_HEADS // 2)
            for h in range(N_KV_HEADS):
                p, psink, q4, k2, v2, rows, prev = _attn_block(qkv_ref, sinks_ref, n, h)
                dos = []
                for g in range(GROUP):
                    hq = GROUP * h + g
                    dos.append(_to_half(do_ref[rows, (hq // 2) * 128:(hq // 2 + 1) * 128].astype(F32), hq % 2, h))
                do4 = jnp.concatenate(dos, axis=0).astype(BF)
                dp = lax.dot_general(do4, v2, _DIMS["NT"], preferred_element_type=F32)
                delta = jnp.sum(p * dp, axis=-1, keepdims=True)
                ds = (p * (dp - delta) * (HEAD_DIM ** -0.5)).astype(BF)
                dsk = psink * delta
                for g in range(GROUP):
                    hq = GROUP * h + g
                    tot = -jnp.sum(dsk[g * BLOCK:(g + 1) * BLOCK], axis=0, keepdims=True)
                    lane = lax.broadcasted_iota(jnp.int32, (1, 128), 1)
                    dsink_ref[...] += jnp.where(lane == hq, tot, 0.0)
                dq = lax.dot_general(ds, k2, _DIMS["NN"], preferred_element_type=F32)
                dk = lax.dot_general(ds, q4, _DIMS["TN"], preferred_element_type=F32)
                dv = lax.dot_general(p.astype(BF), do4, _DIMS["TN"], preferred_element_type=F32)
                dk_acc[prev, :] += dk[:BLOCK]
                dk_acc[rows, :] += dk[BLOCK:]
                dv_acc[prev, :] += dv[:BLOCK]
                dv_acc[rows, :] += dv[BLOCK:]
                for g in range(GROUP):
                    hq = GROUP * h + g
                    piece = jnp.where(_lane_half((BLOCK, 128), h), dq[g * BLOCK:(g + 1) * BLOCK], 0.0)
                    if hq % 2 != h:
                        piece = pltpu.roll(piece, HEAD_DIM, 1)
                    dqs[hq // 2] = piece if dqs[hq // 2] is None else dqs[hq // 2] + piece
            for pb in range(N_Q_HEADS // 2):
                dqkv_ref[rows, pb * 128:(pb + 1) * 128] = dqs[pb].astype(BF)
            return carry

        lax.fori_loop(0, T // BLOCK, blk, 0)
        dqkv_ref[:, ATTN_WIDTH:ATTN_WIDTH + KV_WIDTH] = dk_acc[...].astype(BF)
        dqkv_ref[:, ATTN_WIDTH + KV_WIDTH:] = dv_acc[...].astype(BF)
        if carry is not None:
            carry.finish(ci_refs, co_refs, cs_refs)

    res = pl.pallas_call(
        body, name="attn_bwd", grid=(1,),
        in_specs=[pl.BlockSpec((T, GLU_OFF), lambda i: (0, 0)), pl.BlockSpec((T, ATTN_WIDTH), lambda i: (0, 0)),
                  pl.BlockSpec(memory_space=pltpu.SMEM), *[ANY] * len(c_in)],
        out_specs=[pl.BlockSpec((T, GLU_OFF), lambda i: (0, 0)), pl.BlockSpec((1, 128), lambda i: (0, 0)),
                   *[ANY] * len(c_out)],
        out_shape=[_sds((T, GLU_OFF), BF), _sds((1, 128), F32), *c_out],
        scratch_shapes=[pltpu.VMEM((T, KV_WIDTH), F32), pltpu.VMEM((T, KV_WIDTH), F32), *c_sems],
        compiler_params=_params(("arbitrary",)),
    )(proj, d_o, sinks, *c_in)
    return res[:2], res[2:]


CHUNK = 256
SUB = 32
WIN = CHUNK + 32
PAD_ROWS = SEQ + 2 * CONV_PAD
_GLU_SPECS = [pl.BlockSpec((SEQ, 256), functools.partial(lambda i, c: (0, c), c=GLU_OFF // 256 + c)) for c in range(4)]


def _glu_to_pad(a0, a1, b0, b1, zpad):
    C = CONV_CHANNELS
    zpad[0:CONV_PAD, :] = jnp.zeros((CONV_PAD, C), F32)
    zpad[CONV_PAD + SEQ:, :] = jnp.zeros((CONV_PAD, C), F32)
    zpad[CONV_PAD:CONV_PAD + SEQ, 0:256] = a0[...] * jax.nn.sigmoid(b0[...])
    zpad[CONV_PAD:CONV_PAD + SEQ, 256:C] = a1[...] * jax.nn.sigmoid(b1[...])


def _tap_windows(src, base, win):
    for b in range(8):
        win[b, 0:WIN - 8, :] = src[base + b:base + b + WIN - 8, :]


def _taps(win, w_ref, init, out, flip):
    def sub(si, carry):
        r0 = pl.multiple_of(si * SUB, SUB)
        acc = jnp.broadcast_to(init, (SUB, CONV_CHANNELS))
        for k in range(CONV_WIDTH):
            wk = (CONV_WIDTH - 1 - k) if flip else k
            acc = acc + w_ref[wk:wk + 1, :] * win[k % 8, pl.ds(r0 + 8 * (k // 8), SUB), :]
        out[pl.ds(r0, SUB), :] = acc
        return carry

    lax.fori_loop(0, CHUNK // SUB, sub, 0)


def _tap_grads(win, du, dwacc):
    def sub(si, carry):
        r0 = pl.multiple_of(si * SUB, SUB)
        d = du[pl.ds(r0, SUB), :]
        for k in range(CONV_WIDTH):
            p = d * win[k % 8, pl.ds(r0 + 8 * (k // 8), SUB), :]
            dwacc[8 * k:8 * k + 8, :] += (p[0:8] + p[8:16]) + (p[16:24] + p[24:32])
        return carry

    lax.fori_loop(0, CHUNK // SUB, sub, 0)


def _ln_parts(u):
    mu = jnp.mean(u, axis=-1, keepdims=True)
    xc = u - mu
    rstd = lax.rsqrt(jnp.mean(xc * xc, axis=-1, keepdims=True) + EPS)
    return xc * rstd, rstd


def _conv_fwd(proj, conv_w, conv_b, ln_g, ln_b, carry=None):
    T, C = proj.shape[0], CONV_CHANNELS
    vec = pl.BlockSpec((1, C), lambda i: (0, 0))
    c_in, c_out, c_sems = _carry_io(carry)

    def body(*refs):
        a0, a1, b0, b1, w_ref, cb_ref, g_ref, be_ref = refs[:8]
        ci_refs = refs[8:8 + len(c_in)]
        c_ref = refs[8 + len(c_in)]
        co_refs = refs[9 + len(c_in):9 + len(c_in) + len(c_out)]
        zpad, win, ubuf = refs[9 + len(c_in) + len(c_out):12 + len(c_in) + len(c_out)]
        cs_refs = refs[12 + len(c_in) + len(c_out):]
        if carry is not None:
            carry.start(ci_refs, co_refs, cs_refs)
        _glu_to_pad(a0, a1, b0, b1, zpad)
        for ci in range(T // CHUNK):
            _tap_windows(zpad, ci * CHUNK + CONV_PAD - (CONV_WIDTH - 1), win)
            _taps(win, w_ref, cb_ref[...], ubuf, False)
            xh, _ = _ln_parts(ubuf[...])
            ln = xh * g_ref[...] + be_ref[...]
            c_ref[ci * CHUNK:(ci + 1) * CHUNK, :] = (ln * jax.nn.sigmoid(ln)).astype(BF)
        if carry is not None:
            carry.finish(ci_refs, co_refs, cs_refs)

    res = pl.pallas_call(
        body, name="conv_fwd", grid=(1,),
        in_specs=[*_GLU_SPECS, pl.BlockSpec((CONV_PAD, C), lambda i: (0, 0)), vec, vec, vec, *[ANY] * len(c_in)],
        out_specs=[pl.BlockSpec((T, C), lambda i: (0, 0)), *[ANY] * len(c_out)],
        out_shape=[_sds((T, C), BF), *c_out],
        scratch_shapes=[pltpu.VMEM((PAD_ROWS, C), F32), pltpu.VMEM((8, WIN, C), F32), pltpu.VMEM((CHUNK, C), F32),
                        *c_sems],
        compiler_params=_params(("arbitrary",)),
    )(proj, proj, proj, proj, conv_w, conv_b, ln_g, ln_b, *c_in)
    return res[0], res[1:]


def _conv_bwd(proj, d_c, conv_w, conv_b, ln_g, ln_b, carry=None):
    T, C = proj.shape[0], CONV_CHANNELS
    vec = pl.BlockSpec((1, C), lambda i: (0, 0))
    wspec = pl.BlockSpec((CONV_PAD, C), lambda i: (0, 0))
    c_in, c_out, c_sems = _carry_io(carry)

    def body(*refs):
        a0, a1, b0, b1, dc_ref, w_ref, cb_ref, g_ref, be_ref = refs[:9]
        ci_refs = refs[9:9 + len(c_in)]
        o = 9 + len(c_in)
        dglu_ref, dw_ref, dcb_ref, dg_ref, dbe_ref = refs[o:o + 5]
        co_refs = refs[o + 5:o + 5 + len(c_out)]
        zpad, dupad, win, ubuf, dwacc = refs[o + 5 + len(c_out):o + 10 + len(c_out)]
        cs_refs = refs[o + 10 + len(c_out):]
        if carry is not None:
            carry.start(ci_refs, co_refs, cs_refs)
        _glu_to_pad(a0, a1, b0, b1, zpad)
        dupad[T:, :] = jnp.zeros((2 * CONV_PAD, C), F32)
        dwacc[...] = jnp.zeros_like(dwacc)
        dcb_ref[...] = jnp.zeros_like(dcb_ref)
        dg_ref[...] = jnp.zeros_like(dg_ref)
        dbe_ref[...] = jnp.zeros_like(dbe_ref)
        for ci in range(T // CHUNK):
            rows = slice(ci * CHUNK, (ci + 1) * CHUNK)
            _tap_windows(zpad, ci * CHUNK + CONV_PAD - (CONV_WIDTH - 1), win)
            _taps(win, w_ref, cb_ref[...], ubuf, False)
            xh, rstd = _ln_parts(ubuf[...])
            ln = xh * g_ref[...] + be_ref[...]
            sg = jax.nn.sigmoid(ln)
            dln = dc_ref[rows, :].astype(F32) * (sg * (1.0 + ln * (1.0 - sg)))
            dg_ref[...] += jnp.sum(dln * xh, axis=0, keepdims=True)
            dbe_ref[...] += jnp.sum(dln, axis=0, keepdims=True)
            dxh = dln * g_ref[...]
            du = rstd * (dxh - jnp.mean(dxh, axis=-1, keepdims=True)
                         - xh * jnp.mean(dxh * xh, axis=-1, keepdims=True))
            dupad[rows, :] = du
            dcb_ref[...] += jnp.sum(du, axis=0, keepdims=True)
            _tap_grads(win, dupad.at[rows, :], dwacc)
        for k in range(CONV_WIDTH):
            dw_ref[k:k + 1, :] = jnp.sum(dwacc[8 * k:8 * k + 8, :], axis=0, keepdims=True)
        dw_ref[CONV_WIDTH:, :] = jnp.zeros((CONV_PAD - CONV_WIDTH, C), F32)
        for ci in range(T // CHUNK):
            rows = slice(ci * CHUNK, (ci + 1) * CHUNK)
            _tap_windows(dupad, ci * CHUNK, win)
            _taps(win, w_ref, jnp.zeros((1, C), F32), ubuf, True)
            dz = ubuf[...]
            for half, (a, b) in enumerate(((a0, b0), (a1, b1))):
                sb = jax.nn.sigmoid(b[rows, :])
                dzh = dz[:, half * 256:(half + 1) * 256]
                dglu_ref[rows, half * 256:(half + 1) * 256] = (dzh * sb).astype(BF)
                dglu_ref[rows, C + half * 256:C + (half + 1) * 256] = (dzh * a[rows, :] * sb * (1.0 - sb)).astype(BF)
        if carry is not None:
            carry.finish(ci_refs, co_refs, cs_refs)

    res = pl.pallas_call(
        body, name="conv_bwd", grid=(1,),
        in_specs=[*_GLU_SPECS, pl.BlockSpec((T, C), lambda i: (0, 0)), wspec, vec, vec, vec, *[ANY] * len(c_in)],
        out_specs=[pl.BlockSpec((T, 2 * C), lambda i: (0, 0)), wspec, vec, vec, vec, *[ANY] * len(c_out)],
        out_shape=[_sds((T, 2 * C), BF), _sds((CONV_PAD, C), F32), _sds((1, C), F32), _sds((1, C), F32),
                   _sds((1, C), F32), *c_out],
        scratch_shapes=[pltpu.VMEM((PAD_ROWS, C), F32), pltpu.VMEM((PAD_ROWS, C), F32), pltpu.VMEM((8, WIN, C), F32),
                        pltpu.VMEM((CHUNK, C), F32), pltpu.VMEM((8 * CONV_PAD, C), F32), *c_sems],
        compiler_params=_params(("arbitrary",)),
    )(proj, proj, proj, proj, d_c, conv_w, conv_b, ln_g, ln_b, *c_in)
    return res[:5], res[5:]


_GATE_BLK = GATE_OFF // 256


def _ffn_in_swiglu(h2, wf_t, carry=None):
    T, D = h2.shape
    tm, tn = 512, D_FF // 2
    nj, ni = D_FF // tn, T // tm
    c_in, c_out, c_sems = _carry_io(carry)

    def body(*refs):
        a_ref, bg_ref, bu_ref = refs[:3]
        ci_refs = refs[3:3 + len(c_in)]
        act_ref, g_ref, u_ref = refs[3 + len(c_in):6 + len(c_in)]
        co_refs = refs[6 + len(c_in):6 + len(c_in) + len(c_out)]
        cs_refs = refs[6 + len(c_in) + len(c_out):]
        j, i = pl.program_id(0), pl.program_id(1)
        if carry is not None:
            @pl.when((j == 0) & (i == 0))
            def _():
                carry.start(ci_refs, co_refs, cs_refs)
        a = a_ref[...]
        g = lax.dot_general(a, bg_ref[...], _DIMS["NT"], preferred_element_type=F32)
        u = lax.dot_general(a, bu_ref[...], _DIMS["NT"], preferred_element_type=F32)
        act_ref[...] = (g * jax.nn.sigmoid(g) * u).astype(BF)
        g_ref[...] = g.astype(BF)
        u_ref[...] = u.astype(BF)
        if carry is not None:
            @pl.when((j == nj - 1) & (i == ni - 1))
            def _():
                carry.finish(ci_refs, co_refs, cs_refs)

    t = pl.BlockSpec((tm, tn), lambda j, i: (i, j))
    res = pl.pallas_call(
        body, name="ffn_in_swiglu", grid=(nj, ni),
        in_specs=[pl.BlockSpec((tm, D), lambda j, i: (i, 0)), pl.BlockSpec((tn, D), lambda j, i: (j, 0)),
                  pl.BlockSpec((tn, D), lambda j, i: (nj + j, 0)), *[ANY] * len(c_in)],
        out_specs=[t, t, t, *[ANY] * len(c_out)], out_shape=[*[_sds((T, D_FF), BF)] * 3, *c_out],
        scratch_shapes=c_sems,
        compiler_params=_params(("arbitrary", "arbitrary")),
    )(h2, wf_t, wf_t, *c_in)
    return res[:3], res[3:]


def _local_step(x, target, small, wi_t, conv_w, plan):
    T, D = x.shape
    tm = 1024

    def carried(call, res, carry):
        if carry is None:
            return res
        outs, got = res
        plan.done(call, got)
        return outs

    h, r1 = _rms_fwd("rms_mix", x, small["g_mix_norm"])

    def ep_add(acc, ex, outs, ids, scr):
        outs[0][...] = acc + ex[0][...]

    tn_in = IN_WIDTH // 3
    carry = plan.carry("proj_in")
    proj, = carried("proj_in", _matmul("proj_in", [h], wi_t, "NT", m=T, n=IN_WIDTH, tm=tm, tn=tn_in, epilogue=ep_add,
                                       extra=[(small["b_in"], _row(tn_in))],
                                       outs=[(_sds((T, IN_WIDTH), F32), _tile(tm, tn_in))], carry=carry), carry)
    plan.launch("gather_ffn", after=proj)
    o, got = _attn_fwd(proj, small["sinks"], carry=plan.carry("attn_fwd"))
    plan.done("attn_fwd", got)
    c, got = _conv_fwd(proj, conv_w, small["conv_b"], small["ln_g"], small["ln_b"], carry=plan.carry("conv_fwd"))
    plan.done("conv_fwd", got)
    wap_t, wcp_t, w_out = plan.weight("w_attn_proj"), plan.weight("w_conv_proj"), plan.weight("w_out")
    ya, = _matmul("attn_proj", [o], wap_t, "NT", m=T, n=D, tm=tm, tn=D, epilogue=_store(F32),
                  outs=[(_sds((T, D), F32), _tile(tm, D))])

    tg = 256
    gate_specs = [pl.BlockSpec((tm, tg), lambda j, i, k: (i, _GATE_BLK + j)),
                  pl.BlockSpec((tm, tg), lambda j, i, k: (i, _GATE_BLK + D // tg + j))]

    def ep_merge(acc, ex, outs, ids, scr):
        yc = acc + ex[0][...]
        outs[0][...] = yc
        outs[1][...] = (jax.nn.sigmoid(ex[2][...]) * ex[1][...] + jax.nn.sigmoid(ex[3][...]) * yc).astype(BF)

    carry = plan.carry("conv_proj_merge")
    yc, merged = carried("conv_proj_merge", _matmul(
        "conv_proj_merge", [c], wcp_t, "NT", m=T, n=D, tm=tm, tn=tg, epilogue=ep_merge,
        extra=[(small["b_conv_proj"], _row(tg)), (ya, _tile(tm, tg)), (proj, gate_specs[0]), (proj, gate_specs[1])],
        outs=[(_sds((T, D), F32), _tile(tm, tg)), (_sds((T, D), BF), _tile(tm, tg))], carry=carry), carry)
    carry = plan.carry("out_proj")
    x2, = carried("out_proj", _matmul("out_proj", [merged], w_out, "NN", m=T, n=D, tm=tm, tn=D, epilogue=ep_add,
                                      extra=[(x, _tile(tm, D))], outs=[(_sds((T, D), F32), _tile(tm, D))],
                                      carry=carry), carry)
    plan.launch("gather_down", after=x2)
    h2, r2 = _rms_fwd("rms_ffn", x2, small["g_ffn_norm"])
    wf_t = plan.weight("w_ffn_in")
    (act, gate, up), got = _ffn_in_swiglu(h2, wf_t, carry=plan.carry("ffn_in_swiglu"))
    plan.done("ffn_in_swiglu", got)
    w_down = plan.weight("w_ffn_down")
    x3, = _matmul("ffn_down", [act], w_down, "NN", m=T, n=D, tm=512, tn=D, epilogue=ep_add,
                  extra=[(x2, _tile(512, D))], outs=[(_sds((T, D), F32), _tile(512, D))])
    dx3, dx3_b, dg_final, loss = _final(x3, small["g_final"], target)

    tn_ff = D_FF // 2

    def ep_swiglu_bwd(acc, ex, outs, ids, scr):
        g, u = ex[0][...].astype(F32), ex[1][...].astype(F32)
        sg = jax.nn.sigmoid(g)
        outs[0][...] = (acc * u * sg * (1.0 + g * (1.0 - sg))).astype(BF)
        outs[1][...] = (acc * g * sg).astype(BF)

    dgate, dup = _matmul(
        "ffn_down_bwd", [dx3_b], w_down, "NT", m=T, n=D_FF, tm=512, tn=tn_ff, epilogue=ep_swiglu_bwd,
        extra=[(gate, _tile(512, tn_ff)), (up, _tile(512, tn_ff))],
        outs=[(_sds((T, D_FF), BF), _tile(512, tn_ff)), (_sds((T, D_FF), BF), _tile(512, tn_ff))])

    def dw(name, a, b, rows, cols, row_off=0, alias=None, total_rows=None, colsum=False):
        total_rows = rows if total_rows is None else total_rows
        tmw = rows if rows <= 1024 else D_FF // 2
        by_dma = row_off % tmw != 0

        def ep(acc, ex, outs, ids, scr):
            if by_dma:
                scr[0][...] = acc.astype(BF)
                pltpu.sync_copy(scr[0], outs[0].at[pl.ds(pl.multiple_of(row_off + ids[1] * tmw, 256), tmw)])
            else:
                outs[0][...] = acc.astype(BF)
            if colsum:
                outs[1][...] = jnp.sum(ex[0][...].astype(F32), axis=0, keepdims=True)

        blk = row_off // tmw
        spec = pl.BlockSpec(memory_space=pl.ANY) if by_dma else pl.BlockSpec((tmw, cols), lambda j, i, k: (blk + i, j))
        outs = [(_sds((total_rows, cols), BF), spec)]
        extra = []
        if colsum:
            extra = [(a, pl.BlockSpec((T, tmw), lambda j, i, k: (0, i)))]
            outs.append((_sds((1, rows), F32), pl.BlockSpec((1, tmw), lambda j, i, k: (0, i))))
        carry = plan.carry(name)
        res = carried(name, _matmul(name, [a], b, "TN", m=rows, n=cols, tm=tmw, tn=cols, epilogue=ep, extra=extra,
                                    outs=outs, alias=None if alias is None else (alias, 0),
                                    scratch=[pltpu.VMEM((tmw, cols), BF)] if by_dma else [], carry=carry), carry)
        return res if colsum else res[0]

    plan.grad_ready(dict(w_ffn_down=dw("ffn_down_dw", act, dx3_b, D_FF, D)))

    def ep_rms_bwd(acc, ex, outs, ids, scr):
        dx, dg = _rms_bwd(acc, ex[0][...], ex[1][...], ex[2][...])
        dx = ex[3][...] + dx
        outs[0][...] = dx
        outs[1][...] = dx.astype(BF)
        _accumulate_rows(outs[2], dg, ids[1] == 0)

    def rms_bwd_io(tm_, xin, r, g, dres):
        return dict(
            extra=[(xin, _tile(tm_, D)), (r, pl.BlockSpec((tm_, 1), lambda j, i, k: (i, 0))), (g, _row(D)),
                   (dres, _tile(tm_, D))],
            outs=[(_sds((T, D), F32), _tile(tm_, D)), (_sds((T, D), BF), _tile(tm_, D)), (_sds((1, D), F32), _row(D))])

    carry = plan.carry("ffn_in_bwd")
    dx2, dx2_b, dg_ffn = carried(
        "ffn_in_bwd",
        _matmul("ffn_in_bwd", [dgate, dup], wf_t, "NN", m=T, n=D, tm=512, tn=D, tk=D_FF, epilogue=ep_rms_bwd,
                carry=carry, **rms_bwd_io(512, x2, r2, small["g_ffn_norm"], dx3)), carry)
    plan.launch("send_down")
    gwf_t = dw("ffn_in_dw_gate", dgate, h2, D_FF, D, total_rows=2 * D_FF)
    gwf_t = dw("ffn_in_dw_up", dup, h2, D_FF, D, row_off=D_FF, alias=gwf_t, total_rows=2 * D_FF)
    plan.grad_ready(dict(w_ffn_in=gwf_t))

    def ep_merge_bwd(acc, ex, outs, ids, scr):
        s0 = jax.nn.sigmoid(ex[2][...])
        s1 = jax.nn.sigmoid(ex[3][...])
        outs[0][...] = (acc * s0).astype(BF)
        outs[1][...] = (acc * s1).astype(BF)
        outs[2][...] = (acc * ex[0][...] * s0 * (1.0 - s0)).astype(BF)
        outs[3][...] = (acc * ex[1][...] * s1 * (1.0 - s1)).astype(BF)

    carry = plan.carry("out_proj_bwd_merge")
    dya, dyc, dg0, dg1 = carried(
        "out_proj_bwd_merge",
        _matmul("out_proj_bwd_merge", [dx2_b], w_out, "NT", m=T, n=D, tm=tm, tn=tg, epilogue=ep_merge_bwd,
                extra=[(ya, _tile(tm, tg)), (yc, _tile(tm, tg)), (proj, gate_specs[0]), (proj, gate_specs[1])],
                outs=[(_sds((T, D), BF), _tile(tm, tg))] * 4, carry=carry), carry)
    plan.launch("send_ffn")
    gw_out = dw("out_proj_dw", merged, dx2_b, D, D)
    d_o, = _matmul("attn_proj_bwd", [dya], wap_t, "NN", m=T, n=ATTN_WIDTH, tm=tm, tn=ATTN_WIDTH,
                   epilogue=_store(BF), outs=[(_sds((T, ATTN_WIDTH), BF), _tile(tm, ATTN_WIDTH))])
    d_c, = _matmul("conv_proj_bwd", [dyc], wcp_t, "NN", m=T, n=CONV_CHANNELS, tm=tm, tn=CONV_CHANNELS,
                   epilogue=_store(BF), outs=[(_sds((T, CONV_CHANNELS), BF), _tile(tm, CONV_CHANNELS))])
    gwap_t = dw("attn_proj_dw", dya, o, D, ATTN_WIDTH)
    gwcp_t, db_cp = dw("conv_proj_dw", dyc, c, D, CONV_CHANNELS, colsum=True)
    plan.grad_ready(dict(w_out=gw_out, w_attn_proj=gwap_t, w_conv_proj=gwcp_t))
    (dglu, dcw, dcb, dlng, dlnb), got = _conv_bwd(proj, d_c, conv_w, small["conv_b"], small["ln_g"], small["ln_b"],
                                                  carry=plan.carry("conv_bwd"))
    plan.done("conv_bwd", got)
    plan.launch("send_mix")
    (dqkv, dsinks), got = _attn_bwd(proj, d_o, small["sinks"], carry=plan.carry("attn_bwd"))
    plan.done("attn_bwd", got)

    segs = [dqkv, dglu, dg0, dg1]
    gwi_t, off, db_in = None, 0, []
    for s, seg in enumerate(segs):
        gwi_t, db = dw(f"proj_in_dw{s}", seg, h, seg.shape[1], D, row_off=off, alias=gwi_t, total_rows=IN_WIDTH,
                       colsum=True)
        db_in.append(db)
        off += seg.shape[1]
    plan.grad_ready(dict(w_in=gwi_t))
    plan.alone("swap_inp")
    plan.launch("send_inp")
    carry = plan.carry("proj_in_bwd")
    dx, _, dg_mix = carried(
        "proj_in_bwd",
        _matmul("proj_in_bwd", segs, wi_t, "NN", m=T, n=D, tm=512, tn=D, epilogue=ep_rms_bwd, carry=carry,
                **rms_bwd_io(512, x, r1, small["g_mix_norm"], dx2)), carry)

    parts = dict(g_mix_norm=dg_mix, b_in=db_in, sinks=dsinks, conv_w=dcw, conv_b=dcb, ln_g=dlng, ln_b=dlnb,
                 b_conv_proj=db_cp, g_ffn_norm=dg_ffn, g_final=dg_final, loss=loss)
    return dx, parts


def _place():
    x, y, c = lax.axis_index("x"), lax.axis_index("y"), lax.axis_index("c")
    return x, y, c, [(1 - x, y), (x, 1 - y), (1 - x, 1 - y)]


def _gather_copies(x_refs, out_refs, rows_per, send_sems, recv_sems, local_sems):
    x, y, c, chips = _place()
    me, sibling = (x, y, c), (x, y, 1 - c)

    def rows(a, px, py, pc):
        return out_refs[a].at[pl.ds((4 * px + 2 * py + pc) * rows_per[a], rows_per[a])]

    def copy(a, k, block, to, src=None):
        return pltpu.make_async_remote_copy(
            src_ref=rows(a, *block) if src is None else src, dst_ref=rows(a, *block),
            send_sem=send_sems.at[7 * a + k], recv_sem=recv_sems.at[7 * a + k], device_id=to, device_id_type=MESH)

    def local(a):
        return pltpu.make_async_copy(x_refs[a], rows(a, *me), local_sems.at[a])

    def first(a):
        return [copy(a, 0, me, sibling, src=x_refs[a])] + [copy(a, 1 + j, me, (*chip, c), src=x_refs[a])
                                                          for j, chip in enumerate(chips)]

    def arrive(a, j):
        return copy(a, 1 + j, (*chips[j], c), me)

    def passed(a, j):
        return copy(a, 4 + j, (*chips[j], c), sibling)

    def from_sibling(a):
        return [copy(a, 0, sibling, me)] + [copy(a, 4 + j, (*chip, 1 - c), me) for j, chip in enumerate(chips)]

    return len(x_refs), local, first, arrive, passed, from_sibling


def _gather_start(*refs):
    n, local, first, _, _, _ = _gather_copies(*refs)
    for a in range(n):
        local(a).start()
        for cp in first(a):
            cp.start()


def _gather_finish(*refs):
    n, local, first, arrive, passed, from_sibling = _gather_copies(*refs)
    for a in range(n):
        for j in range(3):
            arrive(a, j).wait_recv()
            passed(a, j).start()
    for a in range(n):
        for cp in from_sibling(a):
            cp.wait_recv()
    for a in range(n):
        for cp in first(a) + [passed(a, j) for j in range(3)]:
            cp.wait_send()
        local(a).wait()


def _gather_blocks(*refs):
    _gather_start(*refs)
    _gather_finish(*refs)


def _gather_peers():
    x, y, c, chips = _place()
    return [(x, y, 1 - c)] + [(*chip, c) for chip in chips]


def _gather_sems(n):
    return [pltpu.SemaphoreType.DMA((7 * n,)), pltpu.SemaphoreType.DMA((7 * n,)), pltpu.SemaphoreType.DMA((n,))]


def _gather_carry(shards):
    rows_per = [s.shape[0] for s in shards]
    return _Carry(shards, [_sds((N_DEV * s.shape[0],) + s.shape[1:], s.dtype) for s in shards],
                  _gather_sems(len(shards)),
                  lambda ins, outs, sems: _gather_start(ins, outs, rows_per, *sems),
                  lambda ins, outs, sems: _gather_finish(ins, outs, rows_per, *sems), _gather_peers)


def _all_gather(shards):
    return _run_carry("weights_all_gather", _gather_carry(shards))


def _swap_carry(grads):
    n = len(grads)

    def copies(g_refs, out_refs, sems):
        send_sems, recv_sems = sems
        x, y, c, _ = _place()
        return [pltpu.make_async_remote_copy(
            src_ref=g_refs[a].at[2 * p + 1 - c], dst_ref=out_refs[a].at[p],
            send_sem=send_sems.at[4 * a + p], recv_sem=recv_sems.at[4 * a + p],
            device_id=(x, y, 1 - c), device_id_type=MESH) for a in range(n) for p in range(4)]

    def start(ins, outs, sems):
        for cp in copies(ins, outs, sems):
            cp.start()

    def finish(ins, outs, sems):
        for cp in copies(ins, outs, sems):
            cp.wait()

    def peers():
        x, y, c, _ = _place()
        return [(x, y, 1 - c)]

    return _Carry(grads, [_sds((4,) + g.shape[1:], g.dtype) for g in grads],
                  [pltpu.SemaphoreType.DMA((4 * n,)), pltpu.SemaphoreType.DMA((4 * n,))], start, finish, peers)


def _join(carries):
    carries = [c for c in carries if c is not None]
    if not carries:
        return None
    n_in = [len(c.arrays) for c in carries]
    n_out = [len(c.out_shapes) for c in carries]
    n_sem = [len(c.sems) for c in carries]

    def parts(refs, counts):
        cuts = [sum(counts[:q]) for q in range(len(counts) + 1)]
        return [refs[cuts[q]:cuts[q + 1]] for q in range(len(counts))]

    def start(ins, outs, sems):
        for c, i, o, s in zip(carries, parts(ins, n_in), parts(outs, n_out), parts(sems, n_sem)):
            c.start(i, o, s)

    def finish(ins, outs, sems):
        for c, i, o, s in zip(carries, parts(ins, n_in), parts(outs, n_out), parts(sems, n_sem)):
            c.finish(i, o, s)

    return _Carry([a for c in carries for a in c.arrays], [o for c in carries for o in c.out_shapes],
                  [s for c in carries for s in c.sems], start, finish)


def _run_carry(name, carry):
    n_in, n_out = len(carry.arrays), len(carry.out_shapes)

    def body(*refs):
        carry.start(refs[:n_in], refs[n_in:n_in + n_out], refs[n_in + n_out:])
        carry.finish(refs[:n_in], refs[n_in:n_in + n_out], refs[n_in + n_out:])

    return pl.pallas_call(body, name=name, in_specs=[ANY] * n_in, out_specs=[ANY] * n_out,
                          out_shape=carry.out_shapes, scratch_shapes=carry.sems)(*carry.arrays)


def _run_carry_async(name, carry, collective_id):
    ins = [jax.new_ref(a, memory_space=pltpu.MemorySpace.HBM) for a in carry.arrays]
    outs = [jax.empty_ref(o, memory_space=pltpu.MemorySpace.HBM) for o in carry.out_shapes]

    @pl.kernel(mesh=plsc.ScalarSubcoreMesh(axis_name="sequencer", num_cores=1), name=name,
               scratch_types=tuple(carry.sems), compiler_params=pltpu.CompilerParams(collective_id=collective_id))
    def launch(*sems):
        barrier = pltpu.get_barrier_semaphore()
        peers = carry.peers()
        for peer in peers:
            pl.semaphore_signal(barrier, inc=1, device_id=peer, device_id_type=MESH)
        pl.semaphore_wait(barrier, len(peers))
        carry.start(ins, outs, sems)
        carry.finish(ins, outs, sems)

    launch()
    return [o[...] for o in outs]


def _chip_sum(name, g, got, c):
    _, rows, cols = g.shape

    def body(c_ref, g_ref, got_ref, o_ref):
        o_ref[...] = (g_ref[...].astype(F32) + got_ref[...].astype(F32)).astype(BF)

    return pl.pallas_call(
        body, name=name,
        grid_spec=pltpu.PrefetchScalarGridSpec(
            num_scalar_prefetch=1, grid=(4,),
            in_specs=[pl.BlockSpec((1, rows, cols), lambda p, c_ref: (2 * p + c_ref[0], 0, 0)),
                      pl.BlockSpec((1, rows, cols), lambda p, c_ref: (p, 0, 0))],
            out_specs=pl.BlockSpec((1, rows, cols), lambda p, c_ref: (p, 0, 0))),
        out_shape=_sds((4, rows, cols), BF),
        compiler_params=_params(("arbitrary",)),
    )(c, g, got)


def _send_carry(sums, ks):
    n, nk = len(sums), len(ks)

    def copies(s_refs, out_refs, sems):
        send_sems, recv_sems = sems
        x, y, c, chips = _place()
        return [pltpu.make_async_remote_copy(
            src_ref=s_refs[a].at[2 * chips[k][0] + chips[k][1]], dst_ref=out_refs[a].at[q],
            send_sem=send_sems.at[nk * a + q], recv_sem=recv_sems.at[nk * a + q],
            device_id=(*chips[k], c), device_id_type=MESH) for a in range(n) for q, k in enumerate(ks)]

    def start(ins, outs, sems):
        for cp in copies(ins, outs, sems):
            cp.start()

    def finish(ins, outs, sems):
        for cp in copies(ins, outs, sems):
            cp.wait()

    def peers():
        x, y, c, chips = _place()
        return [(*chips[k], c) for k in ks]

    return _Carry(sums, [_sds((nk,) + s.shape[1:], s.dtype) for s in sums],
                  [pltpu.SemaphoreType.DMA((nk * n,)), pltpu.SemaphoreType.DMA((nk * n,))], start, finish, peers)


def _grad_total(name, g, got, got3, ids):
    _, rows, cols = g.shape
    n3 = len(got3)

    def body(ids_ref, g_ref, got_ref, *rest):
        o_ref = rest[n3]
        tot = g_ref[0].astype(F32) + got_ref[0].astype(F32)
        for r_ref in rest[:n3]:
            for q in range(r_ref.shape[0]):
                tot = tot + r_ref[q].astype(F32)
        o_ref[...] = tot

    return pl.pallas_call(
        body, name=name,
        grid_spec=pltpu.PrefetchScalarGridSpec(
            num_scalar_prefetch=1, grid=(1,),
            in_specs=[pl.BlockSpec((1, rows, cols), lambda i, ids_ref: (ids_ref[0], 0, 0)),
                      pl.BlockSpec((1, rows, cols), lambda i, ids_ref: (ids_ref[1], 0, 0)),
                      *[pl.BlockSpec(r.shape, lambda i, ids_ref: (0, 0, 0)) for r in got3]],
            out_specs=pl.BlockSpec((rows, cols), lambda i, ids_ref: (0, 0))),
        out_shape=_sds((rows, cols), F32),
        compiler_params=_params(("arbitrary",)),
    )(ids, g, got, *got3)


def _adam_math(w, g, m, v):
    m = ADAM_B1 * m + (1.0 - ADAM_B1) * g
    v = ADAM_B2 * v + (1.0 - ADAM_B2) * (g * g)
    m_hat = m / (1.0 - ADAM_B1 ** ADAM_STEP)
    v_hat = v / (1.0 - ADAM_B2 ** ADAM_STEP)
    delta = -ADAM_LR * (m_hat / (jnp.sqrt(v_hat) + ADAM_EPS) + ADAM_WD * w)
    return delta, m, v


def _adamw(name, w, g, m, v):
    rows, cols = w.shape
    tr = 256 if rows % 256 == 0 else rows

    def body(w_ref, g_ref, m_ref, v_ref, d_ref, nm_ref, nv_ref):
        d_ref[...], nm_ref[...], nv_ref[...] = _adam_math(w_ref[...], g_ref[...], m_ref[...], v_ref[...])

    t = pl.BlockSpec((tr, cols), lambda i: (i, 0))
    return pl.pallas_call(
        body, name=name, grid=(rows // tr,), in_specs=[t] * 4, out_specs=[t] * 3,
        out_shape=[_sds((rows, cols), F32)] * 3, compiler_params=_params(("arbitrary",)),
    )(w, g, m, v)


SMALL_NAMES = ["g_mix_norm", "b_in", "sinks", "conv_b", "ln_g", "ln_b", "b_conv_proj", "g_ffn_norm", "g_final"]
_PACK_ROWS = 32


def _small_pack(parts):
    C = CONV_CHANNELS
    part_list = [parts["g_mix_norm"], *parts["b_in"], parts["sinks"], parts["conv_b"], parts["ln_g"], parts["ln_b"],
                 parts["b_conv_proj"], parts["g_ffn_norm"], parts["g_final"], parts["loss"], parts["conv_w"]]

    def body(p_mix, p_b0, p_b1, p_b2, p_b3, p_sink, p_cb, p_lg, p_lb, p_bcp, p_ffn, p_fin, p_loss, p_cw, pack):
        pack[...] = jnp.zeros_like(pack)
        pack[0:1, :] = p_mix[...]
        pack[1:2, 0:GLU_OFF] = p_b0[...]
        pack[2:3, :] = p_b1[...]
        pack[3:4, :] = p_b2[...]
        pack[4:5, :] = p_b3[...]
        pack[5:6, 0:128] = p_sink[...]
        pack[6:7, 0:C] = p_cb[...]
        pack[6:7, C:2 * C] = p_lg[...]
        pack[7:8, 0:C] = p_lb[...]
        pack[8:9, :] = p_bcp[...]
        pack[9:10, :] = p_ffn[...]
        pack[10:11, :] = p_fin[...]
        pack[11:12, 0:128] = jnp.broadcast_to(p_loss[...], (1, 128))
        pack[12:28, 0:C] = p_cw[0:16, :]
        pack[12:28, C:2 * C] = p_cw[16:32, :]

    vm = pl.BlockSpec(memory_space=pltpu.VMEM)
    return pl.pallas_call(body, name="small_pack", in_specs=[vm] * len(part_list), out_specs=vm,
                          out_shape=_sds((_PACK_ROWS, D_MODEL), F32))(*part_list)


def _small_adamw(gathered, small_w, small_m, small_v):
    C = CONV_CHANNELS
    names = SMALL_NAMES
    widths = [small_w[k].shape[1] for k in names]
    n_small = len(names)

    def body(*refs):
        tot_ref = refs[0]
        w_refs = refs[1:1 + n_small]
        m_refs = refs[1 + n_small:1 + 2 * n_small]
        v_refs = refs[1 + 2 * n_small:1 + 3 * n_small]
        o = 1 + 3 * n_small
        loss_ref, cw_ref = refs[o], refs[o + 1]
        out_refs = refs[o + 2:o + 2 + 4 * n_small]
        tot = tot_ref[0:_PACK_ROWS, :]
        for d in range(1, N_DEV):
            tot = tot + tot_ref[d * _PACK_ROWS:(d + 1) * _PACK_ROWS, :]
        loss_ref[...] = tot[11:12, 0:1]
        cw_ref[0:16, :] = tot[12:28, 0:C]
        cw_ref[16:32, :] = tot[12:28, C:2 * C]
        grads = dict(
            g_mix_norm=tot[0:1, :],
            b_in=jnp.concatenate([tot[1:2, 0:GLU_OFF], tot[2:3, :], tot[3:4, :], tot[4:5, :]], axis=1),
            sinks=tot[5:6, 0:N_Q_HEADS], conv_b=tot[6:7, 0:C], ln_g=tot[6:7, C:2 * C], ln_b=tot[7:8, 0:C],
            b_conv_proj=tot[8:9, :], g_ffn_norm=tot[9:10, :], g_final=tot[10:11, :])
        for s, k in enumerate(names):
            g = grads[k]
            d, nm, nv = _adam_math(w_refs[s][...], g, m_refs[s][...], v_refs[s][...])
            out_refs[4 * s][...] = g
            out_refs[4 * s + 1][...] = d
            out_refs[4 * s + 2][...] = nm
            out_refs[4 * s + 3][...] = nv

    vm = pl.BlockSpec(memory_space=pltpu.VMEM)
    args = [gathered, *[small_w[k] for k in names], *[small_m[k] for k in names], *[small_v[k] for k in names]]
    out_shape = [_sds((1, 1), F32), _sds((CONV_PAD, C), F32)]
    for wd in widths:
        out_shape += [_sds((1, wd), F32)] * 4
    res = pl.pallas_call(
        body, name="small_adamw",
        in_specs=[vm] * len(args), out_specs=[vm] * len(out_shape), out_shape=out_shape,
        compiler_params=pltpu.CompilerParams(vmem_limit_bytes=VMEM_LIMIT_BYTES),
    )(*args)
    return res[0], res[1], {k: res[2 + 4 * s:6 + 4 * s] for s, k in enumerate(names)}


BIG = dict(w_in=True, w_attn_proj=True, w_conv_proj=True, w_out=False, w_ffn_in=True, w_ffn_down=False)
WEIGHT_NAMES = ["g_mix_norm", "w_in", "b_in", "sinks", "conv_w", "conv_b", "ln_g", "ln_b", "w_attn_proj",
                "w_conv_proj", "b_conv_proj", "w_out", "g_ffn_norm", "w_ffn_in", "w_ffn_down", "g_final"]


class _Plan:
    GROUPS = dict(down=["w_ffn_down"], ffn=["w_ffn_in"], mix=["w_out", "w_attn_proj", "w_conv_proj"], inp=["w_in"])
    ALL = (0, 1, 2)
    RIDES = dict(
        gather_mix=[("gather", ["w_attn_proj", "w_conv_proj", "w_out"])], gather_ffn=[("gather", ["w_ffn_in"])],
        gather_down=[("gather", ["w_ffn_down"])],
        ffn_in_bwd=[("swap", "down")], send_down=[("send", "down", ALL)],
        out_proj_bwd_merge=[("swap", "ffn")], send_ffn=[("send", "ffn", ALL)],
        conv_bwd=[("swap", "mix")], send_mix=[("send", "mix", ALL)],
        swap_inp=[("swap", "inp")], send_inp=[("send", "inp", ALL)])
    ASYNC = dict(gather_mix=1, gather_ffn=2, gather_down=3, send_down=4, send_ffn=5, send_mix=6, send_inp=7)

    def __init__(self, shards, c1):
        self.shards, self.c1 = shards, c1
        self.full, self.slots, self.got, self.sums, self.got3 = {}, {}, {}, {}, {}

    def weight(self, name):
        return self.full[name]

    def grad_ready(self, grads):
        for k, g in grads.items():
            self.slots[k] = g.reshape(N_DEV, g.shape[0] // N_DEV, g.shape[1])

    def _one(self, kind, what, ks=None):
        if kind == "gather":
            return _gather_carry([self.shards[k] for k in what])
        names = self.GROUPS[what]
        if kind == "swap":
            return _swap_carry([self.slots[k] for k in names])
        return _send_carry([self.sums[k] for k in names], ks)

    def carry(self, call):
        return _join([self._one(*ride) for ride in self.RIDES.get(call, [])])

    def done(self, call, outs):
        outs = list(outs)
        for kind, what, *_ in self.RIDES.get(call, []):
            names = what if kind == "gather" else self.GROUPS[what]
            mine, outs = outs[:len(names)], outs[len(names):]
            if kind == "gather":
                self.full.update(zip(names, mine))
            elif kind == "send":
                for k, r in zip(names, mine):
                    self.got3.setdefault(k, []).append(r)
            else:
                for k, r in zip(names, mine):
                    self.got[k] = r
                    self.sums[k] = _chip_sum(f"chip_sum_{k}", self.slots[k], r, self.c1)

    def alone(self, call):
        self.done(call, _run_carry(call, self.carry(call)))

    def launch(self, call, after=None):
        carry = self._one(*self.RIDES[call][0])
        if after is not None:
            carry.arrays = list(lax.optimization_barrier((tuple(carry.arrays), after))[0])
        self.done(call, _run_carry_async(call, carry, self.ASYNC[call]))


def kernel(x, g_mix_norm, w_in, b_in, sinks, conv_w, conv_b, ln_g, ln_b, w_attn_proj, w_conv_proj, b_conv_proj, w_out, g_ffn_norm, w_ffn_in, w_ffn_down, g_final, loss_target, m_g_mix_norm, m_w_in, m_b_in, m_sinks, m_conv_w, m_conv_b, m_ln_g, m_ln_b, m_w_attn_proj, m_w_conv_proj, m_b_conv_proj, m_w_out, m_g_ffn_norm, m_w_ffn_in, m_w_ffn_down, m_g_final, v_g_mix_norm, v_w_in, v_b_in, v_sinks, v_conv_w, v_conv_b, v_ln_g, v_ln_b, v_w_attn_proj, v_w_conv_proj, v_b_conv_proj, v_w_out, v_g_ffn_norm, v_w_ffn_in, v_w_ffn_down, v_g_final):
    w = dict(g_mix_norm=g_mix_norm, w_in=w_in, b_in=b_in, sinks=sinks, conv_w=conv_w, conv_b=conv_b, ln_g=ln_g,
             ln_b=ln_b, w_attn_proj=w_attn_proj, w_conv_proj=w_conv_proj, b_conv_proj=b_conv_proj, w_out=w_out,
             g_ffn_norm=g_ffn_norm, w_ffn_in=w_ffn_in, w_ffn_down=w_ffn_down, g_final=g_final)
    m = dict(g_mix_norm=m_g_mix_norm, w_in=m_w_in, b_in=m_b_in, sinks=m_sinks, conv_w=m_conv_w, conv_b=m_conv_b,
             ln_g=m_ln_g, ln_b=m_ln_b, w_attn_proj=m_w_attn_proj, w_conv_proj=m_w_conv_proj,
             b_conv_proj=m_b_conv_proj, w_out=m_w_out, g_ffn_norm=m_g_ffn_norm, w_ffn_in=m_w_ffn_in,
             w_ffn_down=m_w_ffn_down, g_final=m_g_final)
    v = dict(g_mix_norm=v_g_mix_norm, w_in=v_w_in, b_in=v_b_in, sinks=v_sinks, conv_w=v_conv_w, conv_b=v_conv_b,
             ln_g=v_ln_g, ln_b=v_ln_b, w_attn_proj=v_w_attn_proj, w_conv_proj=v_w_conv_proj,
             b_conv_proj=v_b_conv_proj, w_out=v_w_out, g_ffn_norm=v_g_ffn_norm, w_ffn_in=v_w_ffn_in,
             w_ffn_down=v_w_ffn_down, g_final=v_g_final)
    ax, ay, ac = lax.axis_index("x"), lax.axis_index("y"), lax.axis_index("c")
    me = 4 * ax + 2 * ay + ac
    chip = 2 * ax + ay

    shards = {k: (w[k][0].T if tr else w[k][0]).astype(BF) for k, tr in BIG.items()}
    cw_shard = jnp.pad(conv_w[0].T, ((0, 0), (0, 1))).reshape(16, 128)
    wi_t, cw_full = _all_gather([shards["w_in"], cw_shard])
    conv_full = cw_full.reshape(CONV_CHANNELS, CONV_PAD).T

    as_row = lambda a: a.reshape(1, -1)
    small_w = {k: as_row(w[k]) for k in SMALL_NAMES}
    small_m = {k: as_row(m[k]) for k in SMALL_NAMES}
    small_v = {k: as_row(v[k]) for k in SMALL_NAMES}
    plan = _Plan(shards, ac.reshape(1).astype(jnp.int32))
    plan.launch("gather_mix", after=wi_t)
    dx, parts = _local_step(x[0], loss_target[0], small_w, wi_t, conv_full, plan)

    small_gathered, = _run_carry_async("small_gather", _gather_carry([_small_pack(parts)]), 8)

    ids = jnp.stack([me, chip]).astype(jnp.int32)
    grads, delta, new_m, new_v = {}, {}, {}, {}
    for k in sorted(BIG, key=lambda k: k == "w_in"):
        tot = _grad_total(f"grad_total_{k}", plan.slots[k], plan.got[k], plan.got3[k], ids)
        tot = tot.T if BIG[k] else tot
        d, nm, nv = _adamw(f"adamw_{k}", w[k][0], tot, m[k][0], v[k][0])
        grads[k], delta[k], new_m[k], new_v[k] = tot[None], d[None], nm[None], nv[None]

    loss, cw_grad, small_out = _small_adamw(small_gathered, small_w, small_m, small_v)
    for k in SMALL_NAMES:
        g, d, nm, nv = (a.reshape(w[k].shape) for a in small_out[k])
        grads[k], delta[k], new_m[k], new_v[k] = g, d, nm, nv
    cw_mine = lax.dynamic_slice(cw_grad, (0, me * 64), (CONV_WIDTH, 64))
    d, nm, nv = _adamw("adamw_conv_w", conv_w[0], cw_mine, m_conv_w[0], v_conv_w[0])
    grads["conv_w"], delta["conv_w"], new_m["conv_w"], new_v["conv_w"] = cw_mine[None], d[None], nm[None], nv[None]

    return (loss.reshape(()), dx[None], *[grads[k] for k in WEIGHT_NAMES], *[delta[k] for k in WEIGHT_NAMES],
            *[new_m[k] for k in WEIGHT_NAMES], *[new_v[k] for k in WEIGHT_NAMES])
```

```python
import functools

import jax
import jax.numpy as jnp
from jax import lax
from jax.experimental import pallas as pl
from jax.experimental.pallas import tpu as pltpu
from jax.experimental.pallas import tpu_sc as plsc

F32 = jnp.float32
BF = jnp.bfloat16

SEQ = 2048
D_MODEL = 1024
HEAD_DIM = 64
N_Q_HEADS = 8
N_KV_HEADS = 2
GROUP = N_Q_HEADS // N_KV_HEADS
BLOCK = 128
ATTN_WIDTH = 512
KV_WIDTH = 128
CONV_CHANNELS = 512
CONV_WIDTH = 31
CONV_PAD = 32
GLU_OFF = 768
GATE_OFF = 1792
IN_WIDTH = 3840
D_FF = 2816
EPS = 1e-5
NEG = -1e30
N_DEV = 8

ADAM_LR = 0.001
ADAM_B1 = 0.9
ADAM_B2 = 0.999
ADAM_EPS = 1e-08
ADAM_WD = 0.01
ADAM_STEP = 10

VMEM_LIMIT_BYTES = 56 * 1024 * 1024
MESH = pl.DeviceIdType.MESH
ANY = pl.BlockSpec(memory_space=pl.ANY)

_DIMS = {"NN": (((1,), (0,)), ((), ())), "NT": (((1,), (1,)), ((), ())), "TN": (((0,), (0,)), ((), ()))}


def _params(sem):
    return pltpu.CompilerParams(dimension_semantics=sem, vmem_limit_bytes=VMEM_LIMIT_BYTES)


class _Carry:
    def __init__(self, arrays, out_shapes, sems, start, finish, peers=None):
        self.arrays, self.out_shapes, self.sems, self.start, self.finish = arrays, out_shapes, sems, start, finish
        self.peers = peers


def _carry_io(carry):
    if carry is None:
        return [], [], []
    return list(carry.arrays), list(carry.out_shapes), list(carry.sems)


def _matmul(name, a_list, b, mode, *, m, n, tm, tn, tk=None, epilogue, extra=(), outs, b_off=(0, 0), alias=None,
            scratch=(), carry=None):
    seg_k = [a.shape[0] if mode == "TN" else a.shape[1] for a in a_list]
    whole = tk is None
    seg_nk = [1] * len(a_list) if whole else [ks // tk for ks in seg_k]
    nk = 1 if whole else sum(seg_nk)
    starts = [sum(seg_nk[:s]) for s in range(len(seg_nk))]
    k_starts = [sum(seg_k[:s]) for s in range(len(seg_k))]
    k_tot = sum(seg_k)
    n_a, n_extra, n_out = len(a_list), len(extra), len(outs)

    a_specs = []
    for st, ns, ks in zip(starts, seg_nk, seg_k):
        if mode == "TN":
            a_specs.append(pl.BlockSpec((ks if whole else tk, tm), lambda j, i, k: (k, i)))
        elif whole:
            a_specs.append(pl.BlockSpec((tm, ks), lambda j, i, k: (i, 0)))
        else:
            a_specs.append(pl.BlockSpec((tm, tk), functools.partial(
                lambda j, i, k, st, ns: (i, jnp.clip(k - st, 0, ns - 1)), st=st, ns=ns)))
    bk = k_tot if whole else tk
    if mode == "NT":
        b_spec = pl.BlockSpec((tn, bk), lambda j, i, k: (b_off[0] + j, b_off[1] + k))
    else:
        b_spec = pl.BlockSpec((bk, tn), lambda j, i, k: (b_off[0] + k, b_off[1] + j))
    n_alias = 0 if alias is None else 1
    c_in, c_out, c_sems = _carry_io(carry)
    n_acc = 0 if whole else 1
    nj, ni = n // tn, m // tm

    def body(*refs):
        pos = [n_a, 1, n_alias, n_extra, len(c_in), n_out, len(c_out), n_acc, len(scratch), len(c_sems)]
        cuts = [sum(pos[:q]) for q in range(len(pos) + 1)]
        a_refs, (b_ref,), _, ex, ci_refs, out_refs, co_refs, acc_refs, scr, cs_refs = (
            refs[cuts[q]:cuts[q + 1]] for q in range(len(pos)))
        j, i, k = pl.program_id(0), pl.program_id(1), pl.program_id(2)
        ids = (j, i)
        if carry is not None:
            @pl.when((j == 0) & (i == 0) & (k == 0))
            def _():
                carry.start(ci_refs, co_refs, cs_refs)

        def dot(a_ref, bv):
            return lax.dot_general(a_ref[...].astype(BF), bv.astype(BF), _DIMS[mode], preferred_element_type=F32)

        if whole:
            tot = None
            for a_ref, k0, ks in zip(a_refs, k_starts, seg_k):
                if n_a == 1:
                    bv = b_ref[...]
                else:
                    bv = b_ref[:, k0:k0 + ks] if mode == "NT" else b_ref[k0:k0 + ks, :]
                part = dot(a_ref, bv)
                tot = part if tot is None else tot + part
            epilogue(tot, ex, out_refs, ids, scr)
        else:
            acc, = acc_refs

            @pl.when(k == 0)
            def _():
                acc[...] = jnp.zeros_like(acc)

            for a_ref, st, ns in zip(a_refs, starts, seg_nk):
                if n_a == 1:
                    acc[...] += dot(a_ref, b_ref[...])
                else:
                    @pl.when((k >= st) & (k < st + ns))
                    def _(a_ref=a_ref):
                        acc[...] += dot(a_ref, b_ref[...])

            @pl.when(k == nk - 1)
            def _():
                epilogue(acc[...], ex, out_refs, ids, scr)

        if carry is not None:
            @pl.when((j == nj - 1) & (i == ni - 1) & (k == nk - 1))
            def _():
                carry.finish(ci_refs, co_refs, cs_refs)

    in_specs = [*a_specs, b_spec]
    args = [*a_list, b]
    io_alias = {}
    if alias is not None:
        in_specs.append(pl.BlockSpec(memory_space=pl.ANY))
        args.append(alias[0])
        io_alias = {n_a + 1: alias[1]}
    in_specs += [s for _, s in extra] + [pl.BlockSpec(memory_space=pl.ANY)] * len(c_in)
    args += [x for x, _ in extra] + c_in
    res = pl.pallas_call(
        body, name=name, grid=(nj, ni, nk), in_specs=in_specs,
        out_specs=[s for _, s in outs] + [pl.BlockSpec(memory_space=pl.ANY)] * len(c_out),
        out_shape=[o for o, _ in outs] + c_out,
        scratch_shapes=[*([] if whole else [pltpu.VMEM((tm, tn), F32)]), *scratch, *c_sems],
        input_output_aliases=io_alias,
        compiler_params=_params(("arbitrary", "arbitrary", "arbitrary")),
    )(*args)
    return res if carry is None else (res[:n_out], res[n_out:])


def _tile(tm, tn):
    return pl.BlockSpec((tm, tn), lambda j, i, k: (i, j))


def _row(tn):
    return pl.BlockSpec((1, tn), lambda j, i, k: (0, j))


def _store(dtype):
    def ep(acc, ex, outs, ids, scr):
        outs[0][...] = acc.astype(dtype)
    return ep


def _sds(shape, dtype):
    return jax.ShapeDtypeStruct(shape, dtype)


def _rms_fwd(name, x, g):
    T, D = x.shape
    tm = 512

    def body(x_ref, g_ref, h_ref, r_ref):
        xv = x_ref[...]
        r = lax.rsqrt(jnp.mean(xv * xv, axis=-1, keepdims=True) + EPS)
        h_ref[...] = (xv * r * g_ref[...]).astype(BF)
        r_ref[...] = r

    return pl.pallas_call(
        body, name=name, grid=(T // tm,),
        in_specs=[pl.BlockSpec((tm, D), lambda i: (i, 0)), pl.BlockSpec((1, D), lambda i: (0, 0))],
        out_specs=[pl.BlockSpec((tm, D), lambda i: (i, 0)), pl.BlockSpec((tm, 1), lambda i: (i, 0))],
        out_shape=[_sds((T, D), BF), _sds((T, 1), F32)],
        compiler_params=_params(("arbitrary",)),
    )(x, g)


def _rms_bwd(dh, xv, r, g):
    xh = xv * r
    dxh = dh * g
    dx = r * (dxh - xh * jnp.mean(dxh * xh, axis=-1, keepdims=True))
    return dx, jnp.sum(dh * xh, axis=0, keepdims=True)


def _accumulate_rows(ref, val, first):
    @pl.when(first)
    def _():
        ref[...] = val

    @pl.when(jnp.logical_not(first))
    def _():
        ref[...] += val


def _final(x3, g_final, target):
    T, D = x3.shape
    tm = 512

    def body(x_ref, g_ref, t_ref, dx_ref, dxb_ref, dg_ref, loss_ref):
        i = pl.program_id(0)
        xv = x_ref[...]
        g = g_ref[...]
        r = lax.rsqrt(jnp.mean(xv * xv, axis=-1, keepdims=True) + EPS)
        err = xv * r * g - t_ref[...]
        dy = err * (1.0 / D)
        dx, dg = _rms_bwd(dy, xv, r, g)
        dx_ref[...] = dx
        dxb_ref[...] = dx.astype(BF)
        part = 0.5 * jnp.sum(jnp.mean(err * err, axis=-1, keepdims=True), axis=0, keepdims=True)
        _accumulate_rows(dg_ref, dg, i == 0)
        _accumulate_rows(loss_ref, part, i == 0)

    return pl.pallas_call(
        body, name="final_loss", grid=(T // tm,),
        in_specs=[pl.BlockSpec((tm, D), lambda i: (i, 0)), pl.BlockSpec((1, D), lambda i: (0, 0)),
                  pl.BlockSpec((tm, D), lambda i: (i, 0))],
        out_specs=[pl.BlockSpec((tm, D), lambda i: (i, 0)), pl.BlockSpec((tm, D), lambda i: (i, 0)),
                   pl.BlockSpec((1, D), lambda i: (0, 0)), pl.BlockSpec((1, 1), lambda i: (0, 0))],
        out_shape=[_sds((T, D), F32), _sds((T, D), BF), _sds((1, D), F32), _sds((1, 1), F32)],
        compiler_params=_params(("arbitrary",)),
    )(x3, g_final, target)


def _lane_half(shape, h):
    lane = lax.broadcasted_iota(jnp.int32, shape, 1)
    return (lane >= HEAD_DIM * h) & (lane < HEAD_DIM * (h + 1))


def _to_half(v, w, h):
    if w != h:
        v = pltpu.roll(v, HEAD_DIM, 1)
    return jnp.where(_lane_half(v.shape, h), v, 0.0)


def _attn_block(qkv_ref, sinks_ref, n, h):
    r0 = pl.multiple_of(n * BLOCK, BLOCK)
    p0 = pl.multiple_of(jnp.maximum(n - 1, 0) * BLOCK, BLOCK)
    rows = pl.ds(r0, BLOCK)
    prev = pl.ds(p0, BLOCK)
    k2 = jnp.concatenate([qkv_ref[prev, ATTN_WIDTH:ATTN_WIDTH + KV_WIDTH],
                          qkv_ref[rows, ATTN_WIDTH:ATTN_WIDTH + KV_WIDTH]], axis=0).astype(BF)
    v2 = jnp.concatenate([qkv_ref[prev, ATTN_WIDTH + KV_WIDTH:ATTN_WIDTH + 2 * KV_WIDTH],
                          qkv_ref[rows, ATTN_WIDTH + KV_WIDTH:ATTN_WIDTH + 2 * KV_WIDTH]], axis=0).astype(BF)
    qs = []
    for g in range(GROUP):
        hq = GROUP * h + g
        blk = qkv_ref[rows, (hq // 2) * 128:(hq // 2 + 1) * 128]
        qs.append(_to_half(blk, hq % 2, h))
    q4 = jnp.concatenate(qs, axis=0).astype(BF)
    s = lax.dot_general(q4, k2, _DIMS["NT"], preferred_element_type=F32) * (HEAD_DIM ** -0.5)
    shape = s.shape
    row = lax.broadcasted_iota(jnp.int32, shape, 0)
    qi = row & (BLOCK - 1)
    kj = lax.broadcasted_iota(jnp.int32, shape, 1)
    diff = qi + BLOCK - kj
    valid = (diff >= 0) & (diff < BLOCK) & ((kj >= BLOCK) | (n > 0))
    s = jnp.where(valid, s, NEG)
    row1 = lax.broadcasted_iota(jnp.int32, (shape[0], 1), 0)
    sink = jnp.zeros((shape[0], 1), F32)
    for g in range(GROUP):
        sink = jnp.where((row1 >= g * BLOCK) & (row1 < (g + 1) * BLOCK), sinks_ref[0, GROUP * h + g], sink)
    m = jnp.maximum(jnp.max(s, axis=-1, keepdims=True), sink)
    e = jnp.exp(s - m)
    es = jnp.exp(sink - m)
    inv = 1.0 / (jnp.sum(e, axis=-1, keepdims=True) + es)
    return e * inv, es * inv, q4, k2, v2, rows, prev


def _attn_fwd(proj, sinks, carry=None):
    T = proj.shape[0]
    c_in, c_out, c_sems = _carry_io(carry)

    def body(*refs):
        qkv_ref, sinks_ref = refs[:2]
        ci_refs = refs[2:2 + len(c_in)]
        o_ref = refs[2 + len(c_in)]
        co_refs = refs[3 + len(c_in):3 + len(c_in) + len(c_out)]
        cs_refs = refs[3 + len(c_in) + len(c_out):]
        if carry is not None:
            carry.start(ci_refs, co_refs, cs_refs)

        def blk(n, z):
            outs = [None] * (N_Q_HEADS // 2)
            for h in range(N_KV_HEADS):
                p, _, _, _, v2, rows, _ = _attn_block(qkv_ref, sinks_ref, n, h)
                o = lax.dot_general(p.astype(BF), v2, _DIMS["NN"], preferred_element_type=F32)
                for g in range(GROUP):
                    hq = GROUP * h + g
                    piece = jnp.where(_lane_half((BLOCK, 128), h), o[g * BLOCK:(g + 1) * BLOCK], 0.0)
                    if hq % 2 != h:
                        piece = pltpu.roll(piece, HEAD_DIM, 1)
                    outs[hq // 2] = piece if outs[hq // 2] is None else outs[hq // 2] + piece
            for pb in range(N_Q_HEADS // 2):
                o_ref[rows, pb * 128:(pb + 1) * 128] = outs[pb].astype(BF)
            return z

        lax.fori_loop(0, T // BLOCK, blk, 0)
        if carry is not None:
            carry.finish(ci_refs, co_refs, cs_refs)

    res = pl.pallas_call(
        body, name="attn_fwd", grid=(1,),
        in_specs=[pl.BlockSpec((T, GLU_OFF), lambda i: (0, 0)), pl.BlockSpec(memory_space=pltpu.SMEM),
                  *[ANY] * len(c_in)],
        out_specs=[pl.BlockSpec((T, ATTN_WIDTH), lambda i: (0, 0)), *[ANY] * len(c_out)],
        out_shape=[_sds((T, ATTN_WIDTH), BF), *c_out], scratch_shapes=c_sems,
        compiler_params=_params(("arbitrary",)),
    )(proj, sinks, *c_in)
    return res[0], res[1:]


def _attn_bwd(proj, d_o, sinks, carry=None):
    T = proj.shape[0]
    c_in, c_out, c_sems = _carry_io(carry)

    def body(*refs):
        qkv_ref, do_ref, sinks_ref = refs[:3]
        ci_refs = refs[3:3 + len(c_in)]
        dqkv_ref, dsink_ref = refs[3 + len(c_in):5 + len(c_in)]
        co_refs = refs[5 + len(c_in):5 + len(c_in) + len(c_out)]
        dk_acc, dv_acc = refs[5 + len(c_in) + len(c_out):7 + len(c_in) + len(c_out)]
        cs_refs = refs[7 + len(c_in) + len(c_out):]
        if carry is not None:
            carry.start(ci_refs, co_refs, cs_refs)
        dsink_ref[...] = jnp.zeros_like(dsink_ref)
        dk_acc[...] = jnp.zeros_like(dk_acc)
        dv_acc[...] = jnp.zeros_like(dv_acc)

        def blk(n, carry):
            dqs = [None] * (N_Q_HEADS // 2)
            for h in range(N_KV_HEADS):
                p, psink, q4, k2, v2, rows, prev = _attn_block(qkv_ref, sinks_ref, n, h)
                dos = []
                for g in range(GROUP):
                    hq = GROUP * h + g
                    dos.append(_to_half(do_ref[rows, (hq // 2) * 128:(hq // 2 + 1) * 128].astype(F32), hq % 2, h))
                do4 = jnp.concatenate(dos, axis=0).astype(BF)
                dp = lax.dot_general(do4, v2, _DIMS["NT"], preferred_element_type=F32)
                delta = jnp.sum(p * dp, axis=-1, keepdims=True)
                ds = (p * (dp - delta) * (HEAD_DIM ** -0.5)).astype(BF)
                dsk = psink * delta
                for g in range(GROUP):
                    hq = GROUP * h + g
                    tot = -jnp.sum(dsk[g * BLOCK:(g + 1) * BLOCK], axis=0, keepdims=True)
                    lane = lax.broadcasted_iota(jnp.int32, (1, 128), 1)
                    dsink_ref[...] += jnp.where(lane == hq, tot, 0.0)
                dq = lax.dot_general(ds, k2, _DIMS["NN"], preferred_element_type=F32)
                dk = lax.dot_general(ds, q4, _DIMS["TN"], preferred_element_type=F32)
                dv = lax.dot_general(p.astype(BF), do4, _DIMS["TN"], preferred_element_type=F32)
                dk_acc[prev, :] += dk[:BLOCK]
                dk_acc[rows, :] += dk[BLOCK:]
                dv_acc[prev, :] += dv[:BLOCK]
                dv_acc[rows, :] += dv[BLOCK:]
                for g in range(GROUP):
                    hq = GROUP * h + g
                    piece = jnp.where(_lane_half((BLOCK, 128), h), dq[g * BLOCK:(g + 1) * BLOCK], 0.0)
                    if hq % 2 != h:
                        piece = pltpu.roll(piece, HEAD_DIM, 1)
                    dqs[hq // 2] = piece if dqs[hq // 2] is None else dqs[hq // 2] + piece
            for pb in range(N_Q_HEADS // 2):
                dqkv_ref[rows, pb * 128:(pb + 1) * 128] = dqs[pb].astype(BF)
            return carry

        lax.fori_loop(0, T // BLOCK, blk, 0)
        dqkv_ref[:, ATTN_WIDTH:ATTN_WIDTH + KV_WIDTH] = dk_acc[...].astype(BF)
        dqkv_ref[:, ATTN_WIDTH + KV_WIDTH:] = dv_acc[...].astype(BF)
        if carry is not None:
            carry.finish(ci_refs, co_refs, cs_refs)

    res = pl.pallas_call(
        body, name="attn_bwd", grid=(1,),
        in_specs=[pl.BlockSpec((T, GLU_OFF), lambda i: (0, 0)), pl.BlockSpec((T, ATTN_WIDTH), lambda i: (0, 0)),
                  pl.BlockSpec(memory_space=pltpu.SMEM), *[ANY] * len(c_in)],
        out_specs=[pl.BlockSpec((T, GLU_OFF), lambda i: (0, 0)), pl.BlockSpec((1, 128), lambda i: (0, 0)),
                   *[ANY] * len(c_out)],
        out_shape=[_sds((T, GLU_OFF), BF), _sds((1, 128), F32), *c_out],
        scratch_shapes=[pltpu.VMEM((T, KV_WIDTH), F32), pltpu.VMEM((T, KV_WIDTH), F32), *c_sems],
        compiler_params=_params(("arbitrary",)),
    )(proj, d_o, sinks, *c_in)
    return res[:2], res[2:]


CHUNK = 256
SUB = 32
WIN = CHUNK + 32
PAD_ROWS = SEQ + 2 * CONV_PAD
_GLU_SPECS = [pl.BlockSpec((SEQ, 256), functools.partial(lambda i, c: (0, c), c=GLU_OFF // 256 + c)) for c in range(4)]


def _glu_to_pad(a0, a1, b0, b1, zpad):
    C = CONV_CHANNELS
    zpad[0:CONV_PAD, :] = jnp.zeros((CONV_PAD, C), F32)
    zpad[CONV_PAD + SEQ:, :] = jnp.zeros((CONV_PAD, C), F32)
    zpad[CONV_PAD:CONV_PAD + SEQ, 0:256] = a0[...] * jax.nn.sigmoid(b0[...])
    zpad[CONV_PAD:CONV_PAD + SEQ, 256:C] = a1[...] * jax.nn.sigmoid(b1[...])


def _tap_windows(src, base, win):
    for b in range(8):
        win[b, 0:WIN - 8, :] = src[base + b:base + b + WIN - 8, :]


def _taps(win, w_ref, init, out, flip):
    def sub(si, carry):
        r0 = pl.multiple_of(si * SUB, SUB)
        acc = jnp.broadcast_to(init, (SUB, CONV_CHANNELS))
        for k in range(CONV_WIDTH):
            wk = (CONV_WIDTH - 1 - k) if flip else k
            acc = acc + w_ref[wk:wk + 1, :] * win[k % 8, pl.ds(r0 + 8 * (k // 8), SUB), :]
        out[pl.ds(r0, SUB), :] = acc
        return carry

    lax.fori_loop(0, CHUNK // SUB, sub, 0)


def _tap_grads(win, du, dwacc):
    def sub(si, carry):
        r0 = pl.multiple_of(si * SUB, SUB)
        d = du[pl.ds(r0, SUB), :]
        for k in range(CONV_WIDTH):
            p = d * win[k % 8, pl.ds(r0 + 8 * (k // 8), SUB), :]
            dwacc[8 * k:8 * k + 8, :] += (p[0:8] + p[8:16]) + (p[16:24] + p[24:32])
        return carry

    lax.fori_loop(0, CHUNK // SUB, sub, 0)


def _ln_parts(u):
    mu = jnp.mean(u, axis=-1, keepdims=True)
    xc = u - mu
    rstd = lax.rsqrt(jnp.mean(xc * xc, axis=-1, keepdims=True) + EPS)
    return xc * rstd, rstd


def _conv_fwd(proj, conv_w, conv_b, ln_g, ln_b, carry=None):
    T, C = proj.shape[0], CONV_CHANNELS
    vec = pl.BlockSpec((1, C), lambda i: (0, 0))
    c_in, c_out, c_sems = _carry_io(carry)

    def body(*refs):
        a0, a1, b0, b1, w_ref, cb_ref, g_ref, be_ref = refs[:8]
        ci_refs = refs[8:8 + len(c_in)]
        c_ref = refs[8 + len(c_in)]
        co_refs = refs[9 + len(c_in):9 + len(c_in) + len(c_out)]
        zpad, win, ubuf = refs[9 + len(c_in) + len(c_out):12 + len(c_in) + len(c_out)]
        cs_refs = refs[12 + len(c_in) + len(c_out):]
        if carry is not None:
            carry.start(ci_refs, co_refs, cs_refs)
        _glu_to_pad(a0, a1, b0, b1, zpad)
        for ci in range(T // CHUNK):
            _tap_windows(zpad, ci * CHUNK + CONV_PAD - (CONV_WIDTH - 1), win)
            _taps(win, w_ref, cb_ref[...], ubuf, False)
            xh, _ = _ln_parts(ubuf[...])
            ln = xh * g_ref[...] + be_ref[...]
            c_ref[ci * CHUNK:(ci + 1) * CHUNK, :] = (ln * jax.nn.sigmoid(ln)).astype(BF)
        if carry is not None:
            carry.finish(ci_refs, co_refs, cs_refs)

    res = pl.pallas_call(
        body, name="conv_fwd", grid=(1,),
        in_specs=[*_GLU_SPECS, pl.BlockSpec((CONV_PAD, C), lambda i: (0, 0)), vec, vec, vec, *[ANY] * len(c_in)],
        out_specs=[pl.BlockSpec((T, C), lambda i: (0, 0)), *[ANY] * len(c_out)],
        out_shape=[_sds((T, C), BF), *c_out],
        scratch_shapes=[pltpu.VMEM((PAD_ROWS, C), F32), pltpu.VMEM((8, WIN, C), F32), pltpu.VMEM((CHUNK, C), F32),
                        *c_sems],
        compiler_params=_params(("arbitrary",)),
    )(proj, proj, proj, proj, conv_w, conv_b, ln_g, ln_b, *c_in)
    return res[0], res[1:]


def _conv_bwd(proj, d_c, conv_w, conv_b, ln_g, ln_b, carry=None):
    T, C = proj.shape[0], CONV_CHANNELS
    vec = pl.BlockSpec((1, C), lambda i: (0, 0))
    wspec = pl.BlockSpec((CONV_PAD, C), lambda i: (0, 0))
    c_in, c_out, c_sems = _carry_io(carry)

    def body(*refs):
        a0, a1, b0, b1, dc_ref, w_ref, cb_ref, g_ref, be_ref = refs[:9]
        ci_refs = refs[9:9 + len(c_in)]
        o = 9 + len(c_in)
        dglu_ref, dw_ref, dcb_ref, dg_ref, dbe_ref = refs[o:o + 5]
        co_refs = refs[o + 5:o + 5 + len(c_out)]
        zpad, dupad, win, ubuf, dwacc = refs[o + 5 + len(c_out):o + 10 + len(c_out)]
        cs_refs = refs[o + 10 + len(c_out):]
        if carry is not None:
            carry.start(ci_refs, co_refs, cs_refs)
        _glu_to_pad(a0, a1, b0, b1, zpad)
        dupad[T:, :] = jnp.zeros((2 * CONV_PAD, C), F32)
        dwacc[...] = jnp.zeros_like(dwacc)
        dcb_ref[...] = jnp.zeros_like(dcb_ref)
        dg_ref[...] = jnp.zeros_like(dg_ref)
        dbe_ref[...] = jnp.zeros_like(dbe_ref)
        for ci in range(T // CHUNK):
            rows = slice(ci * CHUNK, (ci + 1) * CHUNK)
            _tap_windows(zpad, ci * CHUNK + CONV_PAD - (CONV_WIDTH - 1), win)
            _taps(win, w_ref, cb_ref[...], ubuf, False)
            xh, rstd = _ln_parts(ubuf[...])
            ln = xh * g_ref[...] + be_ref[...]
            sg = jax.nn.sigmoid(ln)
            dln = dc_ref[rows, :].astype(F32) * (sg * (1.0 + ln * (1.0 - sg)))
            dg_ref[...] += jnp.sum(dln * xh, axis=0, keepdims=True)
            dbe_ref[...] += jnp.sum(dln, axis=0, keepdims=True)
            dxh = dln * g_ref[...]
            du = rstd * (dxh - jnp.mean(dxh, axis=-1, keepdims=True)
                         - xh * jnp.mean(dxh * xh, axis=-1, keepdims=True))
            dupad[rows, :] = du
            dcb_ref[...] += jnp.sum(du, axis=0, keepdims=True)
            _tap_grads(win, dupad.at[rows, :], dwacc)
        for k in range(CONV_WIDTH):
            dw_ref[k:k + 1, :] = jnp.sum(dwacc[8 * k:8 * k + 8, :], axis=0, keepdims=True)
        dw_ref[CONV_WIDTH:, :] = jnp.zeros((CONV_PAD - CONV_WIDTH, C), F32)
        for ci in range(T // CHUNK):
            rows = slice(ci * CHUNK, (ci + 1) * CHUNK)
            _tap_windows(dupad, ci * CHUNK, win)
            _taps(win, w_ref, jnp.zeros((1, C), F32), ubuf, True)
            dz = ubuf[...]
            for half, (a, b) in enumerate(((a0, b0), (a1, b1))):
                sb = jax.nn.sigmoid(b[rows, :])
                dzh = dz[:, half * 256:(half + 1) * 256]
                dglu_ref[rows, half * 256:(half + 1) * 256] = (dzh * sb).astype(BF)
                dglu_ref[rows, C + half * 256:C + (half + 1) * 256] = (dzh * a[rows, :] * sb * (1.0 - sb)).astype(BF)
        if carry is not None:
            carry.finish(ci_refs, co_refs, cs_refs)

    res = pl.pallas_call(
        body, name="conv_bwd", grid=(1,),
        in_specs=[*_GLU_SPECS, pl.BlockSpec((T, C), lambda i: (0, 0)), wspec, vec, vec, vec, *[ANY] * len(c_in)],
        out_specs=[pl.BlockSpec((T, 2 * C), lambda i: (0, 0)), wspec, vec, vec, vec, *[ANY] * len(c_out)],
        out_shape=[_sds((T, 2 * C), BF), _sds((CONV_PAD, C), F32), _sds((1, C), F32), _sds((1, C), F32),
                   _sds((1, C), F32), *c_out],
        scratch_shapes=[pltpu.VMEM((PAD_ROWS, C), F32), pltpu.VMEM((PAD_ROWS, C), F32), pltpu.VMEM((8, WIN, C), F32),
                        pltpu.VMEM((CHUNK, C), F32), pltpu.VMEM((8 * CONV_PAD, C), F32), *c_sems],
        compiler_params=_params(("arbitrary",)),
    )(proj, proj, proj, proj, d_c, conv_w, conv_b, ln_g, ln_b, *c_in)
    return res[:5], res[5:]


_GATE_BLK = GATE_OFF // 256


def _ffn_in_swiglu(h2, wf_t, carry=None):
    T, D = h2.shape
    tm, tn = 512, D_FF // 2
    nj, ni = D_FF // tn, T // tm
    c_in, c_out, c_sems = _carry_io(carry)

    def body(*refs):
        a_ref, bg_ref, bu_ref = refs[:3]
        ci_refs = refs[3:3 + len(c_in)]
        act_ref, g_ref, u_ref = refs[3 + len(c_in):6 + len(c_in)]
        co_refs = refs[6 + len(c_in):6 + len(c_in) + len(c_out)]
        cs_refs = refs[6 + len(c_in) + len(c_out):]
        j, i = pl.program_id(0), pl.program_id(1)
        if carry is not None:
            @pl.when((j == 0) & (i == 0))
            def _():
                carry.start(ci_refs, co_refs, cs_refs)
        a = a_ref[...]
        g = lax.dot_general(a, bg_ref[...], _DIMS["NT"], preferred_element_type=F32)
        u = lax.dot_general(a, bu_ref[...], _DIMS["NT"], preferred_element_type=F32)
        act_ref[...] = (g * jax.nn.sigmoid(g) * u).astype(BF)
        g_ref[...] = g.astype(BF)
        u_ref[...] = u.astype(BF)
        if carry is not None:
            @pl.when((j == nj - 1) & (i == ni - 1))
            def _():
                carry.finish(ci_refs, co_refs, cs_refs)

    t = pl.BlockSpec((tm, tn), lambda j, i: (i, j))
    res = pl.pallas_call(
        body, name="ffn_in_swiglu", grid=(nj, ni),
        in_specs=[pl.BlockSpec((tm, D), lambda j, i: (i, 0)), pl.BlockSpec((tn, D), lambda j, i: (j, 0)),
                  pl.BlockSpec((tn, D), lambda j, i: (nj + j, 0)), *[ANY] * len(c_in)],
        out_specs=[t, t, t, *[ANY] * len(c_out)], out_shape=[*[_sds((T, D_FF), BF)] * 3, *c_out],
        scratch_shapes=c_sems,
        compiler_params=_params(("arbitrary", "arbitrary")),
    )(h2, wf_t, wf_t, *c_in)
    return res[:3], res[3:]


def _proj_in_dw(segs, h):
    T, D = h.shape
    tb = 256
    nblk = [seg.shape[1] // tb for seg in segs]
    starts = [sum(nblk[:q]) for q in range(len(segs))]
    n_seg = len(segs)

    def body(*refs):
        seg_refs, h_ref, o_ref, cs_ref = refs[:n_seg], refs[n_seg], refs[n_seg + 1], refs[n_seg + 2]
        i = pl.program_id(0)
        for seg_ref, st, nb in zip(seg_refs, starts, nblk):
            @pl.when((i >= st) & (i < st + nb))
            def _(seg_ref=seg_ref):
                a = seg_ref[...]
                o_ref[...] = lax.dot_general(a, h_ref[...], _DIMS["TN"], preferred_element_type=F32).astype(BF)
                cs_ref[...] = jnp.sum(a.astype(F32), axis=0, keepdims=True)

    in_specs = [pl.BlockSpec((T, tb), functools.partial(lambda i, st, nb: (0, jnp.clip(i - st, 0, nb - 1)), st=st, nb=nb))
                for st, nb in zip(starts, nblk)]
    return pl.pallas_call(
        body, name="proj_in_dw", grid=(sum(nblk),),
        in_specs=[*in_specs, pl.BlockSpec((T, D), lambda i: (0, 0))],
        out_specs=[pl.BlockSpec((tb, D), lambda i: (i, 0)), pl.BlockSpec((1, tb), lambda i: (0, i))],
        out_shape=[_sds((sum(nblk) * tb, D), BF), _sds((1, sum(nblk) * tb), F32)],
        compiler_params=_params(("arbitrary",)),
    )(*segs, h)


def _local_step(x, target, small, wi_t, conv_w, plan):
    T, D = x.shape
    tm = 1024

    def carried(call, res, carry):
        if carry is None:
            return res
        outs, got = res
        plan.done(call, got)
        return outs

    h, r1 = _rms_fwd("rms_mix", x, small["g_mix_norm"])

    def ep_add(acc, ex, outs, ids, scr):
        outs[0][...] = acc + ex[0][...]

    tn_in = IN_WIDTH // 3
    carry = plan.carry("proj_in")
    proj, = carried("proj_in", _matmul("proj_in", [h], wi_t, "NT", m=T, n=IN_WIDTH, tm=tm, tn=tn_in, epilogue=ep_add,
                                       extra=[(small["b_in"], _row(tn_in))],
                                       outs=[(_sds((T, IN_WIDTH), F32), _tile(tm, tn_in))], carry=carry), carry)
    plan.launch("gather_ffn", after=proj)
    o, got = _attn_fwd(proj, small["sinks"], carry=plan.carry("attn_fwd"))
    plan.done("attn_fwd", got)
    c, got = _conv_fwd(proj, conv_w, small["conv_b"], small["ln_g"], small["ln_b"], carry=plan.carry("conv_fwd"))
    plan.done("conv_fwd", got)
    wap_t, wcp_t, w_out = plan.weight("w_attn_proj"), plan.weight("w_conv_proj"), plan.weight("w_out")
    ya, = _matmul("attn_proj", [o], wap_t, "NT", m=T, n=D, tm=tm, tn=D, epilogue=_store(F32),
                  outs=[(_sds((T, D), F32), _tile(tm, D))])

    tg = 256
    gate_specs = [pl.BlockSpec((tm, tg), lambda j, i, k: (i, _GATE_BLK + j)),
                  pl.BlockSpec((tm, tg), lambda j, i, k: (i, _GATE_BLK + D // tg + j))]

    def ep_merge(acc, ex, outs, ids, scr):
        yc = acc + ex[0][...]
        outs[0][...] = yc
        outs[1][...] = (jax.nn.sigmoid(ex[2][...]) * ex[1][...] + jax.nn.sigmoid(ex[3][...]) * yc).astype(BF)

    carry = plan.carry("conv_proj_merge")
    yc, merged = carried("conv_proj_merge", _matmul(
        "conv_proj_merge", [c], wcp_t, "NT", m=T, n=D, tm=tm, tn=tg, epilogue=ep_merge,
        extra=[(small["b_conv_proj"], _row(tg)), (ya, _tile(tm, tg)), (proj, gate_specs[0]), (proj, gate_specs[1])],
        outs=[(_sds((T, D), F32), _tile(tm, tg)), (_sds((T, D), BF), _tile(tm, tg))], carry=carry), carry)
    carry = plan.carry("out_proj")
    x2, = carried("out_proj", _matmul("out_proj", [merged], w_out, "NN", m=T, n=D, tm=tm, tn=D, epilogue=ep_add,
                                      extra=[(x, _tile(tm, D))], outs=[(_sds((T, D), F32), _tile(tm, D))],
                                      carry=carry), carry)
    plan.launch("gather_down", after=x2)
    h2, r2 = _rms_fwd("rms_ffn", x2, small["g_ffn_norm"])
    wf_t = plan.weight("w_ffn_in")
    (act, gate, up), got = _ffn_in_swiglu(h2, wf_t, carry=plan.carry("ffn_in_swiglu"))
    plan.done("ffn_in_swiglu", got)
    w_down = plan.weight("w_ffn_down")
    x3, = _matmul("ffn_down", [act], w_down, "NN", m=T, n=D, tm=512, tn=D, epilogue=ep_add,
                  extra=[(x2, _tile(512, D))], outs=[(_sds((T, D), F32), _tile(512, D))])
    dx3, dx3_b, dg_final, loss = _final(x3, small["g_final"], target)

    tn_ff = D_FF // 2

    def ep_swiglu_bwd(acc, ex, outs, ids, scr):
        g, u = ex[0][...].astype(F32), ex[1][...].astype(F32)
        sg = jax.nn.sigmoid(g)
        outs[0][...] = (acc * u * sg * (1.0 + g * (1.0 - sg))).astype(BF)
        outs[1][...] = (acc * g * sg).astype(BF)

    dgate, dup = _matmul(
        "ffn_down_bwd", [dx3_b], w_down, "NT", m=T, n=D_FF, tm=512, tn=tn_ff, epilogue=ep_swiglu_bwd,
        extra=[(gate, _tile(512, tn_ff)), (up, _tile(512, tn_ff))],
        outs=[(_sds((T, D_FF), BF), _tile(512, tn_ff)), (_sds((T, D_FF), BF), _tile(512, tn_ff))])

    def dw(name, a, b, rows, cols, row_off=0, alias=None, total_rows=None, colsum=False):
        total_rows = rows if total_rows is None else total_rows
        tmw = rows if rows <= 1024 else D_FF // 2
        blk, rem = divmod(row_off, tmw)
        assert rem == 0

        def ep(acc, ex, outs, ids, scr):
            outs[0][...] = acc.astype(BF)
            if colsum:
                outs[1][...] = jnp.sum(ex[0][...].astype(F32), axis=0, keepdims=True)

        outs = [(_sds((total_rows, cols), BF), pl.BlockSpec((tmw, cols), lambda j, i, k: (blk + i, j)))]
        extra = []
        if colsum:
            extra = [(a, pl.BlockSpec((T, tmw), lambda j, i, k: (0, i)))]
            outs.append((_sds((1, rows), F32), pl.BlockSpec((1, tmw), lambda j, i, k: (0, i))))
        carry = plan.carry(name)
        res = carried(name, _matmul(name, [a], b, "TN", m=rows, n=cols, tm=tmw, tn=cols, epilogue=ep, extra=extra,
                                    outs=outs, alias=None if alias is None else (alias, 0), carry=carry), carry)
        return res if colsum else res[0]

    plan.grad_ready(dict(w_ffn_down=dw("ffn_down_dw", act, dx3_b, D_FF, D)))
    plan.launch("swap_down", settle=False)

    def ep_rms_bwd(acc, ex, outs, ids, scr):
        dx, dg = _rms_bwd(acc, ex[0][...], ex[1][...], ex[2][...])
        dx = ex[3][...] + dx
        outs[0][...] = dx
        outs[1][...] = dx.astype(BF)
        _accumulate_rows(outs[2], dg, ids[1] == 0)

    def rms_bwd_io(tm_, xin, r, g, dres):
        return dict(
            extra=[(xin, _tile(tm_, D)), (r, pl.BlockSpec((tm_, 1), lambda j, i, k: (i, 0))), (g, _row(D)),
                   (dres, _tile(tm_, D))],
            outs=[(_sds((T, D), F32), _tile(tm_, D)), (_sds((T, D), BF), _tile(tm_, D)), (_sds((1, D), F32), _row(D))])

    carry = plan.carry("ffn_in_bwd")
    dx2, dx2_b, dg_ffn = carried(
        "ffn_in_bwd",
        _matmul("ffn_in_bwd", [dgate, dup], wf_t, "NN", m=T, n=D, tm=512, tn=D, tk=D_FF, epilogue=ep_rms_bwd,
                carry=carry, **rms_bwd_io(512, x2, r2, small["g_ffn_norm"], dx3)), carry)
    plan.settle("swap_down")
    plan.launch("send_down")
    gwf_t = dw("ffn_in_dw_gate", dgate, h2, D_FF, D, total_rows=2 * D_FF)
    gwf_t = dw("ffn_in_dw_up", dup, h2, D_FF, D, row_off=D_FF, alias=gwf_t, total_rows=2 * D_FF)
    plan.grad_ready(dict(w_ffn_in=gwf_t))
    plan.launch("swap_ffn", settle=False)

    def ep_merge_bwd(acc, ex, outs, ids, scr):
        s0 = jax.nn.sigmoid(ex[2][...])
        s1 = jax.nn.sigmoid(ex[3][...])
        outs[0][...] = (acc * s0).astype(BF)
        outs[1][...] = (acc * s1).astype(BF)
        outs[2][...] = (acc * ex[0][...] * s0 * (1.0 - s0)).astype(BF)
        outs[3][...] = (acc * ex[1][...] * s1 * (1.0 - s1)).astype(BF)

    carry = plan.carry("out_proj_bwd_merge")
    dya, dyc, dg0, dg1 = carried(
        "out_proj_bwd_merge",
        _matmul("out_proj_bwd_merge", [dx2_b], w_out, "NT", m=T, n=D, tm=tm, tn=tg, epilogue=ep_merge_bwd,
                extra=[(ya, _tile(tm, tg)), (yc, _tile(tm, tg)), (proj, gate_specs[0]), (proj, gate_specs[1])],
                outs=[(_sds((T, D), BF), _tile(tm, tg))] * 4, carry=carry), carry)
    plan.settle("swap_ffn")
    plan.launch("send_ffn")
    gw_out = dw("out_proj_dw", merged, dx2_b, D, D)
    d_o, = _matmul("attn_proj_bwd", [dya], wap_t, "NN", m=T, n=ATTN_WIDTH, tm=tm, tn=ATTN_WIDTH,
                   epilogue=_store(BF), outs=[(_sds((T, ATTN_WIDTH), BF), _tile(tm, ATTN_WIDTH))])
    d_c, = _matmul("conv_proj_bwd", [dyc], wcp_t, "NN", m=T, n=CONV_CHANNELS, tm=tm, tn=CONV_CHANNELS,
                   epilogue=_store(BF), outs=[(_sds((T, CONV_CHANNELS), BF), _tile(tm, CONV_CHANNELS))])
    gwap_t = dw("attn_proj_dw", dya, o, D, ATTN_WIDTH)
    gwcp_t, db_cp = dw("conv_proj_dw", dyc, c, D, CONV_CHANNELS, colsum=True)
    plan.grad_ready(dict(w_out=gw_out, w_attn_proj=gwap_t, w_conv_proj=gwcp_t))
    plan.launch("swap_mix", settle=False)
    (dglu, dcw, dcb, dlng, dlnb), got = _conv_bwd(proj, d_c, conv_w, small["conv_b"], small["ln_g"], small["ln_b"],
                                                  carry=plan.carry("conv_bwd"))
    plan.settle("swap_mix")
    plan.launch("send_mix")
    (dqkv, dsinks), got = _attn_bwd(proj, d_o, small["sinks"], carry=plan.carry("attn_bwd"))
    plan.done("attn_bwd", got)

    segs = [dqkv, dglu, dg0, dg1]
    gwi_t, db_in = _proj_in_dw(segs, h)
    plan.grad_ready(dict(w_in=gwi_t))
    plan.alone("swap_inp")
    plan.launch("send_inp")
    carry = plan.carry("proj_in_bwd")
    dx, _, dg_mix = carried(
        "proj_in_bwd",
        _matmul("proj_in_bwd", segs, wi_t, "NN", m=T, n=D, tm=512, tn=D, epilogue=ep_rms_bwd, carry=carry,
                **rms_bwd_io(512, x, r1, small["g_mix_norm"], dx2)), carry)

    parts = dict(g_mix_norm=dg_mix, b_in=db_in, sinks=dsinks, conv_w=dcw, conv_b=dcb, ln_g=dlng, ln_b=dlnb,
                 b_conv_proj=db_cp, g_ffn_norm=dg_ffn, g_final=dg_final, loss=loss)
    return dx, parts


def _place():
    x, y, c = lax.axis_index("x"), lax.axis_index("y"), lax.axis_index("c")
    return x, y, c, [(1 - x, y), (x, 1 - y), (1 - x, 1 - y)]


def _gather_copies(x_refs, out_refs, rows_per, send_sems, recv_sems, local_sems):
    x, y, c, chips = _place()
    me, sibling = (x, y, c), (x, y, 1 - c)

    def rows(a, px, py, pc):
        return out_refs[a].at[pl.ds((4 * px + 2 * py + pc) * rows_per[a], rows_per[a])]

    def copy(a, k, block, to, src=None):
        return pltpu.make_async_remote_copy(
            src_ref=rows(a, *block) if src is None else src, dst_ref=rows(a, *block),
            send_sem=send_sems.at[7 * a + k], recv_sem=recv_sems.at[7 * a + k], device_id=to, device_id_type=MESH)

    def local(a):
        return pltpu.make_async_copy(x_refs[a], rows(a, *me), local_sems.at[a])

    def first(a):
        return [copy(a, 0, me, sibling, src=x_refs[a])] + [copy(a, 1 + j, me, (*chip, c), src=x_refs[a])
                                                          for j, chip in enumerate(chips)]

    def arrive(a, j):
        return copy(a, 1 + j, (*chips[j], c), me)

    def passed(a, j):
        return copy(a, 4 + j, (*chips[j], c), sibling)

    def from_sibling(a):
        return [copy(a, 0, sibling, me)] + [copy(a, 4 + j, (*chip, 1 - c), me) for j, chip in enumerate(chips)]

    return len(x_refs), local, first, arrive, passed, from_sibling


def _gather_start(*refs):
    n, local, first, _, _, _ = _gather_copies(*refs)
    for a in range(n):
        local(a).start()
        for cp in first(a):
            cp.start()


def _gather_finish(*refs):
    n, local, first, arrive, passed, from_sibling = _gather_copies(*refs)
    for a in range(n):
        for j in range(3):
            arrive(a, j).wait_recv()
            passed(a, j).start()
    for a in range(n):
        for cp in from_sibling(a):
            cp.wait_recv()
    for a in range(n):
        for cp in first(a) + [passed(a, j) for j in range(3)]:
            cp.wait_send()
        local(a).wait()


def _gather_peers():
    x, y, c, chips = _place()
    return [(x, y, 1 - c)] + [(*chip, c) for chip in chips]


def _gather_sems(n):
    return [pltpu.SemaphoreType.DMA((7 * n,)), pltpu.SemaphoreType.DMA((7 * n,)), pltpu.SemaphoreType.DMA((n,))]


def _gather_carry(shards):
    rows_per = [s.shape[0] for s in shards]
    return _Carry(shards, [_sds((N_DEV * s.shape[0],) + s.shape[1:], s.dtype) for s in shards],
                  _gather_sems(len(shards)),
                  lambda ins, outs, sems: _gather_start(ins, outs, rows_per, *sems),
                  lambda ins, outs, sems: _gather_finish(ins, outs, rows_per, *sems), _gather_peers)


def _swap_carry(grads):
    n = len(grads)

    def copies(g_refs, out_refs, sems):
        send_sems, recv_sems = sems
        x, y, c, _ = _place()
        return [pltpu.make_async_remote_copy(
            src_ref=g_refs[a].at[2 * p + 1 - c], dst_ref=out_refs[a].at[p],
            send_sem=send_sems.at[4 * a + p], recv_sem=recv_sems.at[4 * a + p],
            device_id=(x, y, 1 - c), device_id_type=MESH) for a in range(n) for p in range(4)]

    def start(ins, outs, sems):
        for cp in copies(ins, outs, sems):
            cp.start()

    def finish(ins, outs, sems):
        for cp in copies(ins, outs, sems):
            cp.wait()

    def peers():
        x, y, c, _ = _place()
        return [(x, y, 1 - c)]

    return _Carry(grads, [_sds((4,) + g.shape[1:], g.dtype) for g in grads],
                  [pltpu.SemaphoreType.DMA((4 * n,)), pltpu.SemaphoreType.DMA((4 * n,))], start, finish, peers)


def _join(carries):
    carries = [c for c in carries if c is not None]
    if not carries:
        return None
    n_in = [len(c.arrays) for c in carries]
    n_out = [len(c.out_shapes) for c in carries]
    n_sem = [len(c.sems) for c in carries]

    def parts(refs, counts):
        cuts = [sum(counts[:q]) for q in range(len(counts) + 1)]
        return [refs[cuts[q]:cuts[q + 1]] for q in range(len(counts))]

    def start(ins, outs, sems):
        for c, i, o, s in zip(carries, parts(ins, n_in), parts(outs, n_out), parts(sems, n_sem)):
            c.start(i, o, s)

    def finish(ins, outs, sems):
        for c, i, o, s in zip(carries, parts(ins, n_in), parts(outs, n_out), parts(sems, n_sem)):
            c.finish(i, o, s)

    return _Carry([a for c in carries for a in c.arrays], [o for c in carries for o in c.out_shapes],
                  [s for c in carries for s in c.sems], start, finish)


def _run_carry(name, carry):
    n_in, n_out = len(carry.arrays), len(carry.out_shapes)

    def body(*refs):
        carry.start(refs[:n_in], refs[n_in:n_in + n_out], refs[n_in + n_out:])
        carry.finish(refs[:n_in], refs[n_in:n_in + n_out], refs[n_in + n_out:])

    return pl.pallas_call(body, name=name, in_specs=[ANY] * n_in, out_specs=[ANY] * n_out,
                          out_shape=carry.out_shapes, scratch_shapes=carry.sems)(*carry.arrays)


def _run_carry_async(name, carry, collective_id):
    ins = [jax.new_ref(a, memory_space=pltpu.MemorySpace.HBM) for a in carry.arrays]
    outs = [jax.empty_ref(o, memory_space=pltpu.MemorySpace.HBM) for o in carry.out_shapes]

    @pl.kernel(mesh=plsc.ScalarSubcoreMesh(axis_name="sequencer", num_cores=1), name=name,
               scratch_types=tuple(carry.sems), compiler_params=pltpu.CompilerParams(collective_id=collective_id))
    def launch(*sems):
        barrier = pltpu.get_barrier_semaphore()
        peers = carry.peers()
        for peer in peers:
            pl.semaphore_signal(barrier, inc=1, device_id=peer, device_id_type=MESH)
        pl.semaphore_wait(barrier, len(peers))
        carry.start(ins, outs, sems)
        carry.finish(ins, outs, sems)

    launch()
    return [o[...] for o in outs]


def _chip_sum(name, g, got, c):
    _, rows, cols = g.shape

    def body(c_ref, g_ref, got_ref, o_ref):
        o_ref[...] = (g_ref[...].astype(F32) + got_ref[...].astype(F32)).astype(BF)

    return pl.pallas_call(
        body, name=name,
        grid_spec=pltpu.PrefetchScalarGridSpec(
            num_scalar_prefetch=1, grid=(4,),
            in_specs=[pl.BlockSpec((1, rows, cols), lambda p, c_ref: (2 * p + c_ref[0], 0, 0)),
                      pl.BlockSpec((1, rows, cols), lambda p, c_ref: (p, 0, 0))],
            out_specs=pl.BlockSpec((1, rows, cols), lambda p, c_ref: (p, 0, 0))),
        out_shape=_sds((4, rows, cols), BF),
        compiler_params=_params(("arbitrary",)),
    )(c, g, got)


def _send_carry(sums, ks):
    n, nk = len(sums), len(ks)

    def copies(s_refs, out_refs, sems):
        send_sems, recv_sems = sems
        x, y, c, chips = _place()
        return [pltpu.make_async_remote_copy(
            src_ref=s_refs[a].at[2 * chips[k][0] + chips[k][1]], dst_ref=out_refs[a].at[q],
            send_sem=send_sems.at[nk * a + q], recv_sem=recv_sems.at[nk * a + q],
            device_id=(*chips[k], c), device_id_type=MESH) for a in range(n) for q, k in enumerate(ks)]

    def start(ins, outs, sems):
        for cp in copies(ins, outs, sems):
            cp.start()

    def finish(ins, outs, sems):
        for cp in copies(ins, outs, sems):
            cp.wait()

    def peers():
        x, y, c, chips = _place()
        return [(*chips[k], c) for k in ks]

    return _Carry(sums, [_sds((nk,) + s.shape[1:], s.dtype) for s in sums],
                  [pltpu.SemaphoreType.DMA((nk * n,)), pltpu.SemaphoreType.DMA((nk * n,))], start, finish, peers)


def _grad_total(name, g, got, got3, ids):
    _, rows, cols = g.shape
    n3 = len(got3)

    def body(ids_ref, g_ref, got_ref, *rest):
        o_ref = rest[n3]
        tot = g_ref[0].astype(F32) + got_ref[0].astype(F32)
        for r_ref in rest[:n3]:
            for q in range(r_ref.shape[0]):
                tot = tot + r_ref[q].astype(F32)
        o_ref[...] = tot

    return pl.pallas_call(
        body, name=name,
        grid_spec=pltpu.PrefetchScalarGridSpec(
            num_scalar_prefetch=1, grid=(1,),
            in_specs=[pl.BlockSpec((1, rows, cols), lambda i, ids_ref: (ids_ref[0], 0, 0)),
                      pl.BlockSpec((1, rows, cols), lambda i, ids_ref: (ids_ref[1], 0, 0)),
                      *[pl.BlockSpec(r.shape, lambda i, ids_ref: (0, 0, 0)) for r in got3]],
            out_specs=pl.BlockSpec((rows, cols), lambda i, ids_ref: (0, 0))),
        out_shape=_sds((rows, cols), F32),
        compiler_params=_params(("arbitrary",)),
    )(ids, g, got, *got3)


def _adam_math(w, g, m, v):
    m = ADAM_B1 * m + (1.0 - ADAM_B1) * g
    v = ADAM_B2 * v + (1.0 - ADAM_B2) * (g * g)
    m_hat = m / (1.0 - ADAM_B1 ** ADAM_STEP)
    v_hat = v / (1.0 - ADAM_B2 ** ADAM_STEP)
    delta = -ADAM_LR * (m_hat / (jnp.sqrt(v_hat) + ADAM_EPS) + ADAM_WD * w)
    return delta, m, v


def _adamw(name, w, g, m, v):
    rows, cols = w.shape
    tr = 256 if rows % 256 == 0 else rows

    def body(w_ref, g_ref, m_ref, v_ref, d_ref, nm_ref, nv_ref):
        d_ref[...], nm_ref[...], nv_ref[...] = _adam_math(w_ref[...], g_ref[...], m_ref[...], v_ref[...])

    t = pl.BlockSpec((tr, cols), lambda i: (i, 0))
    return pl.pallas_call(
        body, name=name, grid=(rows // tr,), in_specs=[t] * 4, out_specs=[t] * 3,
        out_shape=[_sds((rows, cols), F32)] * 3, compiler_params=_params(("arbitrary",)),
    )(w, g, m, v)


SMALL_NAMES = ["g_mix_norm", "b_in", "sinks", "conv_b", "ln_g", "ln_b", "b_conv_proj", "g_ffn_norm", "g_final"]
_PACK_ROWS = 32


def _small_pack(parts):
    C = CONV_CHANNELS
    part_list = [parts["g_mix_norm"], parts["b_in"], parts["sinks"], parts["conv_b"], parts["ln_g"], parts["ln_b"],
                 parts["b_conv_proj"], parts["g_ffn_norm"], parts["g_final"], parts["loss"], parts["conv_w"]]

    def body(p_mix, p_b, p_sink, p_cb, p_lg, p_lb, p_bcp, p_ffn, p_fin, p_loss, p_cw, pack):
        pack[...] = jnp.zeros_like(pack)
        pack[0:1, :] = p_mix[...]
        pack[1:2, 0:GLU_OFF] = p_b[:, 0:GLU_OFF]
        pack[2:3, :] = p_b[:, GLU_OFF:GATE_OFF]
        pack[3:4, :] = p_b[:, GATE_OFF:GATE_OFF + D_MODEL]
        pack[4:5, :] = p_b[:, GATE_OFF + D_MODEL:]
        pack[5:6, 0:128] = p_sink[...]
        pack[6:7, 0:C] = p_cb[...]
        pack[6:7, C:2 * C] = p_lg[...]
        pack[7:8, 0:C] = p_lb[...]
        pack[8:9, :] = p_bcp[...]
        pack[9:10, :] = p_ffn[...]
        pack[10:11, :] = p_fin[...]
        pack[11:12, 0:128] = jnp.broadcast_to(p_loss[...], (1, 128))
        pack[12:28, 0:C] = p_cw[0:16, :]
        pack[12:28, C:2 * C] = p_cw[16:32, :]

    vm = pl.BlockSpec(memory_space=pltpu.VMEM)
    return pl.pallas_call(body, name="small_pack", in_specs=[vm] * len(part_list), out_specs=vm,
                          out_shape=_sds((_PACK_ROWS, D_MODEL), F32))(*part_list)


def _small_adamw(gathered, small_w, small_m, small_v):
    C = CONV_CHANNELS
    names = SMALL_NAMES
    widths = [small_w[k].shape[1] for k in names]
    n_small = len(names)

    def body(*refs):
        tot_ref = refs[0]
        w_refs = refs[1:1 + n_small]
        m_refs = refs[1 + n_small:1 + 2 * n_small]
        v_refs = refs[1 + 2 * n_small:1 + 3 * n_small]
        o = 1 + 3 * n_small
        loss_ref, cw_ref = refs[o], refs[o + 1]
        out_refs = refs[o + 2:o + 2 + 4 * n_small]
        tot = tot_ref[0:_PACK_ROWS, :]
        for d in range(1, N_DEV):
            tot = tot + tot_ref[d * _PACK_ROWS:(d + 1) * _PACK_ROWS, :]
        loss_ref[...] = tot[11:12, 0:1]
        cw_ref[0:16, :] = tot[12:28, 0:C]
        cw_ref[16:32, :] = tot[12:28, C:2 * C]
        grads = dict(
            g_mix_norm=tot[0:1, :],
            b_in=jnp.concatenate([tot[1:2, 0:GLU_OFF], tot[2:3, :], tot[3:4, :], tot[4:5, :]], axis=1),
            sinks=tot[5:6, 0:N_Q_HEADS], conv_b=tot[6:7, 0:C], ln_g=tot[6:7, C:2 * C], ln_b=tot[7:8, 0:C],
            b_conv_proj=tot[8:9, :], g_ffn_norm=tot[9:10, :], g_final=tot[10:11, :])
        for s, k in enumerate(names):
            g = grads[k]
            d, nm, nv = _adam_math(w_refs[s][...], g, m_refs[s][...], v_refs[s][...])
            out_refs[4 * s][...] = g
            out_refs[4 * s + 1][...] = d
            out_refs[4 * s + 2][...] = nm
            out_refs[4 * s + 3][...] = nv

    vm = pl.BlockSpec(memory_space=pltpu.VMEM)
    args = [gathered, *[small_w[k] for k in names], *[small_m[k] for k in names], *[small_v[k] for k in names]]
    out_shape = [_sds((1, 1), F32), _sds((CONV_PAD, C), F32)]
    for wd in widths:
        out_shape += [_sds((1, wd), F32)] * 4
    res = pl.pallas_call(
        body, name="small_adamw",
        in_specs=[vm] * len(args), out_specs=[vm] * len(out_shape), out_shape=out_shape,
        compiler_params=pltpu.CompilerParams(vmem_limit_bytes=VMEM_LIMIT_BYTES),
    )(*args)
    return res[0], res[1], {k: res[2 + 4 * s:6 + 4 * s] for s, k in enumerate(names)}


BIG = dict(w_in=True, w_attn_proj=True, w_conv_proj=True, w_out=False, w_ffn_in=True, w_ffn_down=False)
WEIGHT_NAMES = ["g_mix_norm", "w_in", "b_in", "sinks", "conv_w", "conv_b", "ln_g", "ln_b", "w_attn_proj",
                "w_conv_proj", "b_conv_proj", "w_out", "g_ffn_norm", "w_ffn_in", "w_ffn_down", "g_final"]


class _Plan:
    GROUPS = dict(down=["w_ffn_down"], ffn=["w_ffn_in"], mix=["w_out", "w_attn_proj", "w_conv_proj"], inp=["w_in"])
    ALL = (0, 1, 2)
    RIDES = dict(
        gather_mix=[("gather", ["w_attn_proj", "w_conv_proj", "w_out"])], gather_ffn=[("gather", ["w_ffn_in"])],
        gather_down=[("gather", ["w_ffn_down"])],
        swap_down=[("swap", "down")], send_down=[("send", "down", ALL)],
        swap_ffn=[("swap", "ffn")], send_ffn=[("send", "ffn", ALL)],
        swap_mix=[("swap", "mix")], send_mix=[("send", "mix", ALL)],
        swap_inp=[("swap", "inp")], send_inp=[("send", "inp", ALL)])
    ASYNC = dict(gather_mix=1, gather_ffn=2, gather_down=3, send_down=4, send_ffn=5, send_mix=6, send_inp=7,
                 swap_down=10, swap_ffn=11, swap_mix=12)

    def __init__(self, shards, c1):
        self.shards, self.c1 = shards, c1
        self.full, self.slots, self.got, self.sums, self.got3, self.flying = {}, {}, {}, {}, {}, {}

    def weight(self, name):
        return self.full[name]

    def grad_ready(self, grads):
        for k, g in grads.items():
            self.slots[k] = g.reshape(N_DEV, g.shape[0] // N_DEV, g.shape[1])

    def _one(self, kind, what, ks=None):
        if kind == "gather":
            return _gather_carry([self.shards[k] for k in what])
        names = self.GROUPS[what]
        if kind == "swap":
            return _swap_carry([self.slots[k] for k in names])
        return _send_carry([self.sums[k] for k in names], ks)

    def carry(self, call):
        return _join([self._one(*ride) for ride in self.RIDES.get(call, [])])

    def done(self, call, outs):
        outs = list(outs)
        for kind, what, *_ in self.RIDES.get(call, []):
            names = what if kind == "gather" else self.GROUPS[what]
            mine, outs = outs[:len(names)], outs[len(names):]
            if kind == "gather":
                self.full.update(zip(names, mine))
            elif kind == "send":
                for k, r in zip(names, mine):
                    self.got3.setdefault(k, []).append(r)
            else:
                for k, r in zip(names, mine):
                    self.got[k] = r
                    self.sums[k] = _chip_sum(f"chip_sum_{k}", self.slots[k], r, self.c1)

    def alone(self, call):
        self.done(call, _run_carry(call, self.carry(call)))

    def launch(self, call, after=None, settle=True):
        carry = self._one(*self.RIDES[call][0])
        if after is not None:
            carry.arrays = list(lax.optimization_barrier((tuple(carry.arrays), after))[0])
        self.flying[call] = _run_carry_async(call, carry, self.ASYNC[call])
        if settle:
            self.settle(call)

    def settle(self, call):
        self.done(call, self.flying.pop(call))


def kernel(x, g_mix_norm, w_in, b_in, sinks, conv_w, conv_b, ln_g, ln_b, w_attn_proj, w_conv_proj, b_conv_proj, w_out, g_ffn_norm, w_ffn_in, w_ffn_down, g_final, loss_target, m_g_mix_norm, m_w_in, m_b_in, m_sinks, m_conv_w, m_conv_b, m_ln_g, m_ln_b, m_w_attn_proj, m_w_conv_proj, m_b_conv_proj, m_w_out, m_g_ffn_norm, m_w_ffn_in, m_w_ffn_down, m_g_final, v_g_mix_norm, v_w_in, v_b_in, v_sinks, v_conv_w, v_conv_b, v_ln_g, v_ln_b, v_w_attn_proj, v_w_conv_proj, v_b_conv_proj, v_w_out, v_g_ffn_norm, v_w_ffn_in, v_w_ffn_down, v_g_final):
    w = dict(g_mix_norm=g_mix_norm, w_in=w_in, b_in=b_in, sinks=sinks, conv_w=conv_w, conv_b=conv_b, ln_g=ln_g,
             ln_b=ln_b, w_attn_proj=w_attn_proj, w_conv_proj=w_conv_proj, b_conv_proj=b_conv_proj, w_out=w_out,
             g_ffn_norm=g_ffn_norm, w_ffn_in=w_ffn_in, w_ffn_down=w_ffn_down, g_final=g_final)
    m = dict(g_mix_norm=m_g_mix_norm, w_in=m_w_in, b_in=m_b_in, sinks=m_sinks, conv_w=m_conv_w, conv_b=m_conv_b,
             ln_g=m_ln_g, ln_b=m_ln_b, w_attn_proj=m_w_attn_proj, w_conv_proj=m_w_conv_proj,
             b_conv_proj=m_b_conv_proj, w_out=m_w_out, g_ffn_norm=m_g_ffn_norm, w_ffn_in=m_w_ffn_in,
             w_ffn_down=m_w_ffn_down, g_final=m_g_final)
    v = dict(g_mix_norm=v_g_mix_norm, w_in=v_w_in, b_in=v_b_in, sinks=v_sinks, conv_w=v_conv_w, conv_b=v_conv_b,
             ln_g=v_ln_g, ln_b=v_ln_b, w_attn_proj=v_w_attn_proj, w_conv_proj=v_w_conv_proj,
             b_conv_proj=v_b_conv_proj, w_out=v_w_out, g_ffn_norm=v_g_ffn_norm, w_ffn_in=v_w_ffn_in,
             w_ffn_down=v_w_ffn_down, g_final=v_g_final)
    ax, ay, ac = lax.axis_index("x"), lax.axis_index("y"), lax.axis_index("c")
    me = 4 * ax + 2 * ay + ac
    chip = 2 * ax + ay

    shards = {k: (w[k][0].T if tr else w[k][0]).astype(BF) for k, tr in BIG.items()}
    cw_shard = jnp.pad(conv_w[0].T, ((0, 0), (0, 1))).reshape(16, 128)
    wi_t, cw_full = _run_carry_async("gather_in", _gather_carry([shards["w_in"], cw_shard]), 9)
    conv_full = cw_full.reshape(CONV_CHANNELS, CONV_PAD).T

    as_row = lambda a: a.reshape(1, -1)
    small_w = {k: as_row(w[k]) for k in SMALL_NAMES}
    small_m = {k: as_row(m[k]) for k in SMALL_NAMES}
    small_v = {k: as_row(v[k]) for k in SMALL_NAMES}
    plan = _Plan(shards, ac.reshape(1).astype(jnp.int32))
    plan.launch("gather_mix", after=wi_t)
    dx, parts = _local_step(x[0], loss_target[0], small_w, wi_t, conv_full, plan)

    small_gathered, = _run_carry_async("small_gather", _gather_carry([_small_pack(parts)]), 8)

    ids = jnp.stack([me, chip]).astype(jnp.int32)
    grads, delta, new_m, new_v = {}, {}, {}, {}
    for k in sorted(BIG, key=lambda k: k == "w_in"):
        tot = _grad_total(f"grad_total_{k}", plan.slots[k], plan.got[k], plan.got3[k], ids)
        tot = tot.T if BIG[k] else tot
        d, nm, nv = _adamw(f"adamw_{k}", w[k][0], tot, m[k][0], v[k][0])
        grads[k], delta[k], new_m[k], new_v[k] = tot[None], d[None], nm[None], nv[None]

    loss, cw_grad, small_out = _small_adamw(small_gathered, small_w, small_m, small_v)
    for k in SMALL_NAMES:
        g, d, nm, nv = (a.reshape(w[k].shape) for a in small_out[k])
        grads[k], delta[k], new_m[k], new_v[k] = g, d, nm, nv
    cw_mine = lax.dynamic_slice(cw_grad, (0, me * 64), (CONV_WIDTH, 64))
    d, nm, nv = _adamw("adamw_conv_w", conv_w[0], cw_mine, m_conv_w[0], v_conv_w[0])
    grads["conv_w"], delta["conv_w"], new_m["conv_w"], new_v["conv_w"] = cw_mine[None], d[None], nm[None], nv[None]

    return (loss.reshape(()), dx[None], *[grads[k] for k in WEIGHT_NAMES], *[delta[k] for k in WEIGHT_NAMES],
            *[new_m[k] for k in WEIGHT_NAMES], *[new_v[k] for k in WEIGHT_NAMES])
```

```python
import functools

import jax
import jax.numpy as jnp
from jax import lax
from jax.experimental import pallas as pl
from jax.experimental.pallas import tpu as pltpu
from jax.experimental.pallas import tpu_sc as plsc

F32 = jnp.float32
BF = jnp.bfloat16

SEQ = 2048
D_MODEL = 1024
HEAD_DIM = 64
N_Q_HEADS = 8
N_KV_HEADS = 2
GROUP = N_Q_HEADS // N_KV_HEADS
BLOCK = 128
ATTN_WIDTH = 512
KV_WIDTH = 128
CONV_CHANNELS = 512
CONV_WIDTH = 31
CONV_PAD = 32
GLU_OFF = 768
GATE_OFF = 1792
IN_WIDTH = 3840
D_FF = 2816
EPS = 1e-5
NEG = -1e30
N_DEV = 8

ADAM_LR = 0.001
ADAM_B1 = 0.9
ADAM_B2 = 0.999
ADAM_EPS = 1e-08
ADAM_WD = 0.01
ADAM_STEP = 10

VMEM_LIMIT_BYTES = 56 * 1024 * 1024
MESH = pl.DeviceIdType.MESH
ANY = pl.BlockSpec(memory_space=pl.ANY)

_DIMS = {"NN": (((1,), (0,)), ((), ())), "NT": (((1,), (1,)), ((), ())), "TN": (((0,), (0,)), ((), ()))}


def _params(sem):
    return pltpu.CompilerParams(dimension_semantics=sem, vmem_limit_bytes=VMEM_LIMIT_BYTES)


class _Carry:
    def __init__(self, arrays, out_shapes, sems, start, finish, peers=None):
        self.arrays, self.out_shapes, self.sems, self.start, self.finish = arrays, out_shapes, sems, start, finish
        self.peers = peers


def _carry_io(carry):
    if carry is None:
        return [], [], []
    return list(carry.arrays), list(carry.out_shapes), list(carry.sems)


def _matmul(name, a_list, b, mode, *, m, n, tm, tn, tk=None, epilogue, extra=(), outs, b_off=(0, 0), alias=None,
            scratch=(), carry=None):
    seg_k = [a.shape[0] if mode == "TN" else a.shape[1] for a in a_list]
    whole = tk is None
    seg_nk = [1] * len(a_list) if whole else [ks // tk for ks in seg_k]
    nk = 1 if whole else sum(seg_nk)
    starts = [sum(seg_nk[:s]) for s in range(len(seg_nk))]
    k_starts = [sum(seg_k[:s]) for s in range(len(seg_k))]
    k_tot = sum(seg_k)
    n_a, n_extra, n_out = len(a_list), len(extra), len(outs)

    a_specs = []
    for st, ns, ks in zip(starts, seg_nk, seg_k):
        if mode == "TN":
            a_specs.append(pl.BlockSpec((ks if whole else tk, tm), lambda j, i, k: (k, i)))
        elif whole:
            a_specs.append(pl.BlockSpec((tm, ks), lambda j, i, k: (i, 0)))
        else:
            a_specs.append(pl.BlockSpec((tm, tk), functools.partial(
                lambda j, i, k, st, ns: (i, jnp.clip(k - st, 0, ns - 1)), st=st, ns=ns)))
    bk = k_tot if whole else tk
    if mode == "NT":
        b_spec = pl.BlockSpec((tn, bk), lambda j, i, k: (b_off[0] + j, b_off[1] + k))
    else:
        b_spec = pl.BlockSpec((bk, tn), lambda j, i, k: (b_off[0] + k, b_off[1] + j))
    n_alias = 0 if alias is None else 1
    c_in, c_out, c_sems = _carry_io(carry)
    n_acc = 0 if whole else 1
    nj, ni = n // tn, m // tm

    def body(*refs):
        pos = [n_a, 1, n_alias, n_extra, len(c_in), n_out, len(c_out), n_acc, len(scratch), len(c_sems)]
        cuts = [sum(pos[:q]) for q in range(len(pos) + 1)]
        a_refs, (b_ref,), _, ex, ci_refs, out_refs, co_refs, acc_refs, scr, cs_refs = (
            refs[cuts[q]:cuts[q + 1]] for q in range(len(pos)))
        j, i, k = pl.program_id(0), pl.program_id(1), pl.program_id(2)
        ids = (j, i)
        if carry is not None:
            @pl.when((j == 0) & (i == 0) & (k == 0))
            def _():
                carry.start(ci_refs, co_refs, cs_refs)

        def dot(a_ref, bv):
            return lax.dot_general(a_ref[...].astype(BF), bv.astype(BF), _DIMS[mode], preferred_element_type=F32)

        if whole:
            tot = None
            for a_ref, k0, ks in zip(a_refs, k_starts, seg_k):
                if n_a == 1:
                    bv = b_ref[...]
                else:
                    bv = b_ref[:, k0:k0 + ks] if mode == "NT" else b_ref[k0:k0 + ks, :]
                part = dot(a_ref, bv)
                tot = part if tot is None else tot + part
            epilogue(tot, ex, out_refs, ids, scr)
        else:
            acc, = acc_refs

            @pl.when(k == 0)
            def _():
                acc[...] = jnp.zeros_like(acc)

            for a_ref, st, ns in zip(a_refs, starts, seg_nk):
                if n_a == 1:
                    acc[...] += dot(a_ref, b_ref[...])
                else:
                    @pl.when((k >= st) & (k < st + ns))
                    def _(a_ref=a_ref):
                        acc[...] += dot(a_ref, b_ref[...])

            @pl.when(k == nk - 1)
            def _():
                epilogue(acc[...], ex, out_refs, ids, scr)

        if carry is not None:
            @pl.when((j == nj - 1) & (i == ni - 1) & (k == nk - 1))
            def _():
                carry.finish(ci_refs, co_refs, cs_refs)

    in_specs = [*a_specs, b_spec]
    args = [*a_list, b]
    io_alias = {}
    if alias is not None:
        in_specs.append(pl.BlockSpec(memory_space=pl.ANY))
        args.append(alias[0])
        io_alias = {n_a + 1: alias[1]}
    in_specs += [s for _, s in extra] + [pl.BlockSpec(memory_space=pl.ANY)] * len(c_in)
    args += [x for x, _ in extra] + c_in
    res = pl.pallas_call(
        body, name=name, grid=(nj, ni, nk), in_specs=in_specs,
        out_specs=[s for _, s in outs] + [pl.BlockSpec(memory_space=pl.ANY)] * len(c_out),
        out_shape=[o for o, _ in outs] + c_out,
        scratch_shapes=[*([] if whole else [pltpu.VMEM((tm, tn), F32)]), *scratch, *c_sems],
        input_output_aliases=io_alias,
        compiler_params=_params(("arbitrary", "arbitrary", "arbitrary")),
    )(*args)
    return res if carry is None else (res[:n_out], res[n_out:])


def _tile(tm, tn):
    return pl.BlockSpec((tm, tn), lambda j, i, k: (i, j))


def _row(tn):
    return pl.BlockSpec((1, tn), lambda j, i, k: (0, j))


def _store(dtype):
    def ep(acc, ex, outs, ids, scr):
        outs[0][...] = acc.astype(dtype)
    return ep


def _sds(shape, dtype):
    return jax.ShapeDtypeStruct(shape, dtype)


def _rms_fwd(name, x, g):
    T, D = x.shape
    tm = 512

    def body(x_ref, g_ref, h_ref, r_ref):
        xv = x_ref[...]
        r = lax.rsqrt(jnp.mean(xv * xv, axis=-1, keepdims=True) + EPS)
        h_ref[...] = (xv * r * g_ref[...]).astype(BF)
        r_ref[...] = r

    return pl.pallas_call(
        body, name=name, grid=(T // tm,),
        in_specs=[pl.BlockSpec((tm, D), lambda i: (i, 0)), pl.BlockSpec((1, D), lambda i: (0, 0))],
        out_specs=[pl.BlockSpec((tm, D), lambda i: (i, 0)), pl.BlockSpec((tm, 1), lambda i: (i, 0))],
        out_shape=[_sds((T, D), BF), _sds((T, 1), F32)],
        compiler_params=_params(("arbitrary",)),
    )(x, g)


def _rms_bwd(dh, xv, r, g):
    xh = xv * r
    dxh = dh * g
    dx = r * (dxh - xh * jnp.mean(dxh * xh, axis=-1, keepdims=True))
    return dx, jnp.sum(dh * xh, axis=0, keepdims=True)


def _accumulate_rows(ref, val, first):
    @pl.when(first)
    def _():
        ref[...] = val

    @pl.when(jnp.logical_not(first))
    def _():
        ref[...] += val


def _final(x3, g_final, target):
    T, D = x3.shape
    tm = 512

    def body(x_ref, g_ref, t_ref, dx_ref, dxb_ref, dg_ref, loss_ref):
        i = pl.program_id(0)
        xv = x_ref[...]
        g = g_ref[...]
        r = lax.rsqrt(jnp.mean(xv * xv, axis=-1, keepdims=True) + EPS)
        err = xv * r * g - t_ref[...]
        dy = err * (1.0 / D)
        dx, dg = _rms_bwd(dy, xv, r, g)
        dx_ref[...] = dx
        dxb_ref[...] = dx.astype(BF)
        part = 0.5 * jnp.sum(jnp.mean(err * err, axis=-1, keepdims=True), axis=0, keepdims=True)
        _accumulate_rows(dg_ref, dg, i == 0)
        _accumulate_rows(loss_ref, part, i == 0)

    return pl.pallas_call(
        body, name="final_loss", grid=(T // tm,),
        in_specs=[pl.BlockSpec((tm, D), lambda i: (i, 0)), pl.BlockSpec((1, D), lambda i: (0, 0)),
                  pl.BlockSpec((tm, D), lambda i: (i, 0))],
        out_specs=[pl.BlockSpec((tm, D), lambda i: (i, 0)), pl.BlockSpec((tm, D), lambda i: (i, 0)),
                   pl.BlockSpec((1, D), lambda i: (0, 0)), pl.BlockSpec((1, 1), lambda i: (0, 0))],
        out_shape=[_sds((T, D), F32), _sds((T, D), BF), _sds((1, D), F32), _sds((1, 1), F32)],
        compiler_params=_params(("arbitrary",)),
    )(x3, g_final, target)


def _lane_half(shape, h):
    lane = lax.broadcasted_iota(jnp.int32, shape, 1)
    return (lane >= HEAD_DIM * h) & (lane < HEAD_DIM * (h + 1))


def _to_half(v, w, h):
    if w != h:
        v = pltpu.roll(v, HEAD_DIM, 1)
    return jnp.where(_lane_half(v.shape, h), v, 0.0)


def _attn_block(qkv_ref, sinks_ref, n, h):
    r0 = pl.multiple_of(n * BLOCK, BLOCK)
    p0 = pl.multiple_of(jnp.maximum(n - 1, 0) * BLOCK, BLOCK)
    rows = pl.ds(r0, BLOCK)
    prev = pl.ds(p0, BLOCK)
    k2 = jnp.concatenate([qkv_ref[prev, ATTN_WIDTH:ATTN_WIDTH + KV_WIDTH],
                          qkv_ref[rows, ATTN_WIDTH:ATTN_WIDTH + KV_WIDTH]], axis=0).astype(BF)
    v2 = jnp.concatenate([qkv_ref[prev, ATTN_WIDTH + KV_WIDTH:ATTN_WIDTH + 2 * KV_WIDTH],
                          qkv_ref[rows, ATTN_WIDTH + KV_WIDTH:ATTN_WIDTH + 2 * KV_WIDTH]], axis=0).astype(BF)
    qs = []
    for g in range(GROUP):
        hq = GROUP * h + g
        blk = qkv_ref[rows, (hq // 2) * 128:(hq // 2 + 1) * 128]
        qs.append(_to_half(blk, hq % 2, h))
    q4 = jnp.concatenate(qs, axis=0).astype(BF)
    s = lax.dot_general(q4, k2, _DIMS["NT"], preferred_element_type=F32) * (HEAD_DIM ** -0.5)
    shape = s.shape
    row = lax.broadcasted_iota(jnp.int32, shape, 0)
    qi = row & (BLOCK - 1)
    kj = lax.broadcasted_iota(jnp.int32, shape, 1)
    diff = qi + BLOCK - kj
    valid = (diff >= 0) & (diff < BLOCK) & ((kj >= BLOCK) | (n > 0))
    s = jnp.where(valid, s, NEG)
    row1 = lax.broadcasted_iota(jnp.int32, (shape[0], 1), 0)
    sink = jnp.zeros((shape[0], 1), F32)
    for g in range(GROUP):
        sink = jnp.where((row1 >= g * BLOCK) & (row1 < (g + 1) * BLOCK), sinks_ref[0, GROUP * h + g], sink)
    m = jnp.maximum(jnp.max(s, axis=-1, keepdims=True), sink)
    e = jnp.exp(s - m)
    es = jnp.exp(sink - m)
    inv = 1.0 / (jnp.sum(e, axis=-1, keepdims=True) + es)
    return e * inv, es * inv, q4, k2, v2, rows, prev


def _attn_fwd(proj, sinks, carry=None):
    T = proj.shape[0]
    c_in, c_out, c_sems = _carry_io(carry)

    def body(*refs):
        qkv_ref, sinks_ref = refs[:2]
        ci_refs = refs[2:2 + len(c_in)]
        o_ref = refs[2 + len(c_in)]
        co_refs = refs[3 + len(c_in):3 + len(c_in) + len(c_out)]
        cs_refs = refs[3 + len(c_in) + len(c_out):]
        if carry is not None:
            carry.start(ci_refs, co_refs, cs_refs)

        def blk(n, z):
            outs = [None] * (N_Q_HEADS // 2)
            for h in range(N_KV_HEADS):
                p, _, _, _, v2, rows, _ = _attn_block(qkv_ref, sinks_ref, n, h)
                o = lax.dot_general(p.astype(BF), v2, _DIMS["NN"], preferred_element_type=F32)
                for g in range(GROUP):
                    hq = GROUP * h + g
                    piece = jnp.where(_lane_half((BLOCK, 128), h), o[g * BLOCK:(g + 1) * BLOCK], 0.0)
                    if hq % 2 != h:
                        piece = pltpu.roll(piece, HEAD_DIM, 1)
                    outs[hq // 2] = piece if outs[hq // 2] is None else outs[hq // 2] + piece
            for pb in range(N_Q_HEADS // 2):
                o_ref[rows, pb * 128:(pb + 1) * 128] = outs[pb].astype(BF)
            return z

        lax.fori_loop(0, T // BLOCK, blk, 0)
        if carry is not None:
            carry.finish(ci_refs, co_refs, cs_refs)

    res = pl.pallas_call(
        body, name="attn_fwd", grid=(1,),
        in_specs=[pl.BlockSpec((T, GLU_OFF), lambda i: (0, 0)), pl.BlockSpec(memory_space=pltpu.SMEM),
                  *[ANY] * len(c_in)],
        out_specs=[pl.BlockSpec((T, ATTN_WIDTH), lambda i: (0, 0)), *[ANY] * len(c_out)],
        out_shape=[_sds((T, ATTN_WIDTH), BF), *c_out], scratch_shapes=c_sems,
        compiler_params=_params(("arbitrary",)),
    )(proj, sinks, *c_in)
    return res[0], res[1:]


def _attn_bwd(proj, d_o, sinks, carry=None):
    T = proj.shape[0]
    c_in, c_out, c_sems = _carry_io(carry)

    def body(*refs):
        qkv_ref, do_ref, sinks_ref = refs[:3]
        ci_refs = refs[3:3 + len(c_in)]
        dqkv_ref, dsink_ref = refs[3 + len(c_in):5 + len(c_in)]
        co_refs = refs[5 + len(c_in):5 + len(c_in) + len(c_out)]
        dk_acc, dv_acc = refs[5 + len(c_in) + len(c_out):7 + len(c_in) + len(c_out)]
        cs_refs = refs[7 + len(c_in) + len(c_out):]
        if carry is not None:
            carry.start(ci_refs, co_refs, cs_refs)
        dsink_ref[...] = jnp.zeros_like(dsink_ref)
        dk_acc[...] = jnp.zeros_like(dk_acc)
        dv_acc[...] = jnp.zeros_like(dv_acc)

        def blk(n, carry):
            dqs = [None] * (N_Q_HEADS // 2)
            for h in range(N_KV_HEADS):
                p, psink, q4, k2, v2, rows, prev = _attn_block(qkv_ref, sinks_ref, n, h)
                dos = []
                for g in range(GROUP):
                    hq = GROUP * h + g
                    dos.append(_to_half(do_ref[rows, (hq // 2) * 128:(hq // 2 + 1) * 128].astype(F32), hq % 2, h))
                do4 = jnp.concatenate(dos, axis=0).astype(BF)
                dp = lax.dot_general(do4, v2, _DIMS["NT"], preferred_element_type=F32)
                delta = jnp.sum(p * dp, axis=-1, keepdims=True)
                ds = (p * (dp - delta) * (HEAD_DIM ** -0.5)).astype(BF)
                dsk = psink * delta
                for g in range(GROUP):
                    hq = GROUP * h + g
                    tot = -jnp.sum(dsk[g * BLOCK:(g + 1) * BLOCK], axis=0, keepdims=True)
                    lane = lax.broadcasted_iota(jnp.int32, (1, 128), 1)
                    dsink_ref[...] += jnp.where(lane == hq, tot, 0.0)
                dq = lax.dot_general(ds, k2, _DIMS["NN"], preferred_element_type=F32)
                dk = lax.dot_general(ds, q4, _DIMS["TN"], preferred_element_type=F32)
                dv = lax.dot_general(p.astype(BF), do4, _DIMS["TN"], preferred_element_type=F32)
                dk_acc[prev, :] += dk[:BLOCK]
                dk_acc[rows, :] += dk[BLOCK:]
                dv_acc[prev, :] += dv[:BLOCK]
                dv_acc[rows, :] += dv[BLOCK:]
                for g in range(GROUP):
                    hq = GROUP * h + g
                    piece = jnp.where(_lane_half((BLOCK, 128), h), dq[g * BLOCK:(g + 1) * BLOCK], 0.0)
                    if hq % 2 != h:
                        piece = pltpu.roll(piece, HEAD_DIM, 1)
                    dqs[hq // 2] = piece if dqs[hq // 2] is None else dqs[hq // 2] + piece
            for pb in range(N_Q_HEADS // 2):
                dqkv_ref[rows, pb * 128:(pb + 1) * 128] = dqs[pb].astype(BF)
            return carry

        lax.fori_loop(0, T // BLOCK, blk, 0)
        dqkv_ref[:, ATTN_WIDTH:ATTN_WIDTH + KV_WIDTH] = dk_acc[...].astype(BF)
        dqkv_ref[:, ATTN_WIDTH + KV_WIDTH:] = dv_acc[...].astype(BF)
        if carry is not None:
            carry.finish(ci_refs, co_refs, cs_refs)

    res = pl.pallas_call(
        body, name="attn_bwd", grid=(1,),
        in_specs=[pl.BlockSpec((T, GLU_OFF), lambda i: (0, 0)), pl.BlockSpec((T, ATTN_WIDTH), lambda i: (0, 0)),
                  pl.BlockSpec(memory_space=pltpu.SMEM), *[ANY] * len(c_in)],
        out_specs=[pl.BlockSpec((T, GLU_OFF), lambda i: (0, 0)), pl.BlockSpec((1, 128), lambda i: (0, 0)),
                   *[ANY] * len(c_out)],
        out_shape=[_sds((T, GLU_OFF), BF), _sds((1, 128), F32), *c_out],
        scratch_shapes=[pltpu.VMEM((T, KV_WIDTH), F32), pltpu.VMEM((T, KV_WIDTH), F32), *c_sems],
        compiler_params=_params(("arbitrary",)),
    )(proj, d_o, sinks, *c_in)
    return res[:2], res[2:]


CHUNK = 256
SUB = 32
WIN = CHUNK + 32
PAD_ROWS = SEQ + 2 * CONV_PAD
_GLU_SPECS = [pl.BlockSpec((SEQ, 256), functools.partial(lambda i, c: (0, c), c=GLU_OFF // 256 + c)) for c in range(4)]


def _glu_to_pad(a0, a1, b0, b1, zpad):
    C = CONV_CHANNELS
    zpad[0:CONV_PAD, :] = jnp.zeros((CONV_PAD, C), F32)
    zpad[CONV_PAD + SEQ:, :] = jnp.zeros((CONV_PAD, C), F32)
    zpad[CONV_PAD:CONV_PAD + SEQ, 0:256] = a0[...] * jax.nn.sigmoid(b0[...])
    zpad[CONV_PAD:CONV_PAD + SEQ, 256:C] = a1[...] * jax.nn.sigmoid(b1[...])


def _tap_windows(src, base, win):
    for b in range(8):
        win[b, 0:WIN - 8, :] = src[base + b:base + b + WIN - 8, :]


def _taps(win, w_ref, init, out, flip):
    def sub(si, carry):
        r0 = pl.multiple_of(si * SUB, SUB)
        acc = jnp.broadcast_to(init, (SUB, CONV_CHANNELS))
        for k in range(CONV_WIDTH):
            wk = (CONV_WIDTH - 1 - k) if flip else k
            acc = acc + w_ref[wk:wk + 1, :] * win[k % 8, pl.ds(r0 + 8 * (k // 8), SUB), :]
        out[pl.ds(r0, SUB), :] = acc
        return carry

    lax.fori_loop(0, CHUNK // SUB, sub, 0)


def _tap_grads(win, du, dwacc):
    def sub(si, carry):
        r0 = pl.multiple_of(si * SUB, SUB)
        d = du[pl.ds(r0, SUB), :]
        for k in range(CONV_WIDTH):
            p = d * win[k % 8, pl.ds(r0 + 8 * (k // 8), SUB), :]
            dwacc[8 * k:8 * k + 8, :] += (p[0:8] + p[8:16]) + (p[16:24] + p[24:32])
        return carry

    lax.fori_loop(0, CHUNK // SUB, sub, 0)


def _ln_parts(u):
    mu = jnp.mean(u, axis=-1, keepdims=True)
    xc = u - mu
    rstd = lax.rsqrt(jnp.mean(xc * xc, axis=-1, keepdims=True) + EPS)
    return xc * rstd, rstd


def _conv_fwd(proj, conv_w, conv_b, ln_g, ln_b, carry=None):
    T, C = proj.shape[0], CONV_CHANNELS
    vec = pl.BlockSpec((1, C), lambda i: (0, 0))
    c_in, c_out, c_sems = _carry_io(carry)

    def body(*refs):
        a0, a1, b0, b1, w_ref, cb_ref, g_ref, be_ref = refs[:8]
        ci_refs = refs[8:8 + len(c_in)]
        c_ref = refs[8 + len(c_in)]
        co_refs = refs[9 + len(c_in):9 + len(c_in) + len(c_out)]
        zpad, win, ubuf = refs[9 + len(c_in) + len(c_out):12 + len(c_in) + len(c_out)]
        cs_refs = refs[12 + len(c_in) + len(c_out):]
        if carry is not None:
            carry.start(ci_refs, co_refs, cs_refs)
        _glu_to_pad(a0, a1, b0, b1, zpad)
        for ci in range(T // CHUNK):
            _tap_windows(zpad, ci * CHUNK + CONV_PAD - (CONV_WIDTH - 1), win)
            _taps(win, w_ref, cb_ref[...], ubuf, False)
            xh, _ = _ln_parts(ubuf[...])
            ln = xh * g_ref[...] + be_ref[...]
            c_ref[ci * CHUNK:(ci + 1) * CHUNK, :] = (ln * jax.nn.sigmoid(ln)).astype(BF)
        if carry is not None:
            carry.finish(ci_refs, co_refs, cs_refs)

    res = pl.pallas_call(
        body, name="conv_fwd", grid=(1,),
        in_specs=[*_GLU_SPECS, pl.BlockSpec((CONV_PAD, C), lambda i: (0, 0)), vec, vec, vec, *[ANY] * len(c_in)],
        out_specs=[pl.BlockSpec((T, C), lambda i: (0, 0)), *[ANY] * len(c_out)],
        out_shape=[_sds((T, C), BF), *c_out],
        scratch_shapes=[pltpu.VMEM((PAD_ROWS, C), F32), pltpu.VMEM((8, WIN, C), F32), pltpu.VMEM((CHUNK, C), F32),
                        *c_sems],
        compiler_params=_params(("arbitrary",)),
    )(proj, proj, proj, proj, conv_w, conv_b, ln_g, ln_b, *c_in)
    return res[0], res[1:]


def _conv_bwd(proj, d_c, conv_w, conv_b, ln_g, ln_b, carry=None):
    T, C = proj.shape[0], CONV_CHANNELS
    vec = pl.BlockSpec((1, C), lambda i: (0, 0))
    wspec = pl.BlockSpec((CONV_PAD, C), lambda i: (0, 0))
    c_in, c_out, c_sems = _carry_io(carry)

    def body(*refs):
        a0, a1, b0, b1, dc_ref, w_ref, cb_ref, g_ref, be_ref = refs[:9]
        ci_refs = refs[9:9 + len(c_in)]
        o = 9 + len(c_in)
        dglu_ref, dw_ref, dcb_ref, dg_ref, dbe_ref = refs[o:o + 5]
        co_refs = refs[o + 5:o + 5 + len(c_out)]
        zpad, dupad, win, ubuf, dwacc = refs[o + 5 + len(c_out):o + 10 + len(c_out)]
        cs_refs = refs[o + 10 + len(c_out):]
        if carry is not None:
            carry.start(ci_refs, co_refs, cs_refs)
        _glu_to_pad(a0, a1, b0, b1, zpad)
        dupad[T:, :] = jnp.zeros((2 * CONV_PAD, C), F32)
        dwacc[...] = jnp.zeros_like(dwacc)
        dcb_ref[...] = jnp.zeros_like(dcb_ref)
        dg_ref[...] = jnp.zeros_like(dg_ref)
        dbe_ref[...] = jnp.zeros_like(dbe_ref)
        for ci in range(T // CHUNK):
            rows = slice(ci * CHUNK, (ci + 1) * CHUNK)
            _tap_windows(zpad, ci * CHUNK + CONV_PAD - (CONV_WIDTH - 1), win)
            _taps(win, w_ref, cb_ref[...], ubuf, False)
            xh, rstd = _ln_parts(ubuf[...])
            ln = xh * g_ref[...] + be_ref[...]
            sg = jax.nn.sigmoid(ln)
            dln = dc_ref[rows, :].astype(F32) * (sg * (1.0 + ln * (1.0 - sg)))
            dg_ref[...] += jnp.sum(dln * xh, axis=0, keepdims=True)
            dbe_ref[...] += jnp.sum(dln, axis=0, keepdims=True)
            dxh = dln * g_ref[...]
            du = rstd * (dxh - jnp.mean(dxh, axis=-1, keepdims=True)
                         - xh * jnp.mean(dxh * xh, axis=-1, keepdims=True))
            dupad[rows, :] = du
            dcb_ref[...] += jnp.sum(du, axis=0, keepdims=True)
            _tap_grads(win, dupad.at[rows, :], dwacc)
        for k in range(CONV_WIDTH):
            dw_ref[k:k + 1, :] = jnp.sum(dwacc[8 * k:8 * k + 8, :], axis=0, keepdims=True)
        dw_ref[CONV_WIDTH:, :] = jnp.zeros((CONV_PAD - CONV_WIDTH, C), F32)
        for ci in range(T // CHUNK):
            rows = slice(ci * CHUNK, (ci + 1) * CHUNK)
            _tap_windows(dupad, ci * CHUNK, win)
            _taps(win, w_ref, jnp.zeros((1, C), F32), ubuf, True)
            dz = ubuf[...]
            for half, (a, b) in enumerate(((a0, b0), (a1, b1))):
                sb = jax.nn.sigmoid(b[rows, :])
                dzh = dz[:, half * 256:(half + 1) * 256]
                dglu_ref[rows, half * 256:(half + 1) * 256] = (dzh * sb).astype(BF)
                dglu_ref[rows, C + half * 256:C + (half + 1) * 256] = (dzh * a[rows, :] * sb * (1.0 - sb)).astype(BF)
        if carry is not None:
            carry.finish(ci_refs, co_refs, cs_refs)

    res = pl.pallas_call(
        body, name="conv_bwd", grid=(1,),
        in_specs=[*_GLU_SPECS, pl.BlockSpec((T, C), lambda i: (0, 0)), wspec, vec, vec, vec, *[ANY] * len(c_in)],
        out_specs=[pl.BlockSpec((T, 2 * C), lambda i: (0, 0)), wspec, vec, vec, vec, *[ANY] * len(c_out)],
        out_shape=[_sds((T, 2 * C), BF), _sds((CONV_PAD, C), F32), _sds((1, C), F32), _sds((1, C), F32),
                   _sds((1, C), F32), *c_out],
        scratch_shapes=[pltpu.VMEM((PAD_ROWS, C), F32), pltpu.VMEM((PAD_ROWS, C), F32), pltpu.VMEM((8, WIN, C), F32),
                        pltpu.VMEM((CHUNK, C), F32), pltpu.VMEM((8 * CONV_PAD, C), F32), *c_sems],
        compiler_params=_params(("arbitrary",)),
    )(proj, proj, proj, proj, d_c, conv_w, conv_b, ln_g, ln_b, *c_in)
    return res[:5], res[5:]


_GATE_BLK = GATE_OFF // 256


def _ffn_in_swiglu(h2, wf_t, carry=None):
    T, D = h2.shape
    tm, tn = 512, D_FF // 2
    nj, ni = D_FF // tn, T // tm
    c_in, c_out, c_sems = _carry_io(carry)

    def body(*refs):
        a_ref, bg_ref, bu_ref = refs[:3]
        ci_refs = refs[3:3 + len(c_in)]
        act_ref, g_ref, u_ref = refs[3 + len(c_in):6 + len(c_in)]
        co_refs = refs[6 + len(c_in):6 + len(c_in) + len(c_out)]
        cs_refs = refs[6 + len(c_in) + len(c_out):]
        j, i = pl.program_id(0), pl.program_id(1)
        if carry is not None:
            @pl.when((j == 0) & (i == 0))
            def _():
                carry.start(ci_refs, co_refs, cs_refs)
        a = a_ref[...]
        for c0, c1 in ((0, 768), (768, tn)):
            g = lax.dot_general(a, bg_ref[c0:c1, :], _DIMS["NT"], preferred_element_type=F32)
            u = lax.dot_general(a, bu_ref[c0:c1, :], _DIMS["NT"], preferred_element_type=F32)
            act_ref[:, c0:c1] = (g * jax.nn.sigmoid(g) * u).astype(BF)
            g_ref[:, c0:c1] = g.astype(BF)
            u_ref[:, c0:c1] = u.astype(BF)
        if carry is not None:
            @pl.when((j == nj - 1) & (i == ni - 1))
            def _():
                carry.finish(ci_refs, co_refs, cs_refs)

    t = pl.BlockSpec((tm, tn), lambda j, i: (i, j))
    res = pl.pallas_call(
        body, name="ffn_in_swiglu", grid=(nj, ni),
        in_specs=[pl.BlockSpec((tm, D), lambda j, i: (i, 0)), pl.BlockSpec((tn, D), lambda j, i: (j, 0)),
                  pl.BlockSpec((tn, D), lambda j, i: (nj + j, 0)), *[ANY] * len(c_in)],
        out_specs=[t, t, t, *[ANY] * len(c_out)], out_shape=[*[_sds((T, D_FF), BF)] * 3, *c_out],
        scratch_shapes=c_sems,
        compiler_params=_params(("arbitrary", "arbitrary")),
    )(h2, wf_t, wf_t, *c_in)
    return res[:3], res[3:]


def _proj_in_dw(segs, h):
    T, D = h.shape
    tb = 256
    nblk = [seg.shape[1] // tb for seg in segs]
    starts = [sum(nblk[:q]) for q in range(len(segs))]
    n_seg = len(segs)

    def body(*refs):
        seg_refs, h_ref, o_ref, cs_ref = refs[:n_seg], refs[n_seg], refs[n_seg + 1], refs[n_seg + 2]
        i = pl.program_id(0)
        for seg_ref, st, nb in zip(seg_refs, starts, nblk):
            @pl.when((i >= st) & (i < st + nb))
            def _(seg_ref=seg_ref):
                a = seg_ref[...]
                o_ref[...] = lax.dot_general(a, h_ref[...], _DIMS["TN"], preferred_element_type=F32).astype(BF)
                cs_ref[...] = jnp.sum(a.astype(F32), axis=0, keepdims=True)

    in_specs = [pl.BlockSpec((T, tb), functools.partial(lambda i, st, nb: (0, jnp.clip(i - st, 0, nb - 1)), st=st, nb=nb))
                for st, nb in zip(starts, nblk)]
    return pl.pallas_call(
        body, name="proj_in_dw", grid=(sum(nblk),),
        in_specs=[*in_specs, pl.BlockSpec((T, D), lambda i: (0, 0))],
        out_specs=[pl.BlockSpec((tb, D), lambda i: (i, 0)), pl.BlockSpec((1, tb), lambda i: (0, i))],
        out_shape=[_sds((sum(nblk) * tb, D), BF), _sds((1, sum(nblk) * tb), F32)],
        compiler_params=_params(("arbitrary",)),
    )(*segs, h)


def _local_step(x, target, small, wi_t, conv_w, plan):
    T, D = x.shape
    tm = 1024

    def carried(call, res, carry):
        if carry is None:
            return res
        outs, got = res
        plan.done(call, got)
        return outs

    h, r1 = _rms_fwd("rms_mix", x, small["g_mix_norm"])

    def ep_add(acc, ex, outs, ids, scr):
        outs[0][...] = acc + ex[0][...]

    tn_in = IN_WIDTH // 3
    carry = plan.carry("proj_in")
    proj, = carried("proj_in", _matmul("proj_in", [h], wi_t, "NT", m=T, n=IN_WIDTH, tm=tm, tn=tn_in, epilogue=ep_add,
                                       extra=[(small["b_in"], _row(tn_in))],
                                       outs=[(_sds((T, IN_WIDTH), F32), _tile(tm, tn_in))], carry=carry), carry)
    plan.launch("gather_ffn", after=proj)
    o, got = _attn_fwd(proj, small["sinks"], carry=plan.carry("attn_fwd"))
    plan.done("attn_fwd", got)
    c, got = _conv_fwd(proj, conv_w, small["conv_b"], small["ln_g"], small["ln_b"], carry=plan.carry("conv_fwd"))
    plan.done("conv_fwd", got)
    wap_t, wcp_t, w_out = plan.weight("w_attn_proj"), plan.weight("w_conv_proj"), plan.weight("w_out")
    ya, = _matmul("attn_proj", [o], wap_t, "NT", m=T, n=D, tm=tm, tn=D, epilogue=_store(F32),
                  outs=[(_sds((T, D), F32), _tile(tm, D))])

    tg = 256
    gate_specs = [pl.BlockSpec((tm, tg), lambda j, i, k: (i, _GATE_BLK + j)),
                  pl.BlockSpec((tm, tg), lambda j, i, k: (i, _GATE_BLK + D // tg + j))]

    def ep_merge(acc, ex, outs, ids, scr):
        yc = acc + ex[0][...]
        outs[0][...] = yc
        outs[1][...] = (jax.nn.sigmoid(ex[2][...]) * ex[1][...] + jax.nn.sigmoid(ex[3][...]) * yc).astype(BF)

    carry = plan.carry("conv_proj_merge")
    yc, merged = carried("conv_proj_merge", _matmul(
        "conv_proj_merge", [c], wcp_t, "NT", m=T, n=D, tm=tm, tn=tg, epilogue=ep_merge,
        extra=[(small["b_conv_proj"], _row(tg)), (ya, _tile(tm, tg)), (proj, gate_specs[0]), (proj, gate_specs[1])],
        outs=[(_sds((T, D), F32), _tile(tm, tg)), (_sds((T, D), BF), _tile(tm, tg))], carry=carry), carry)
    carry = plan.carry("out_proj")
    x2, = carried("out_proj", _matmul("out_proj", [merged], w_out, "NN", m=T, n=D, tm=tm, tn=D, epilogue=ep_add,
                                      extra=[(x, _tile(tm, D))], outs=[(_sds((T, D), F32), _tile(tm, D))],
                                      carry=carry), carry)
    plan.launch("gather_down", after=x2)
    h2, r2 = _rms_fwd("rms_ffn", x2, small["g_ffn_norm"])
    wf_t = plan.weight("w_ffn_in")
    (act, gate, up), got = _ffn_in_swiglu(h2, wf_t, carry=plan.carry("ffn_in_swiglu"))
    plan.done("ffn_in_swiglu", got)
    w_down = plan.weight("w_ffn_down")
    x3, = _matmul("ffn_down", [act], w_down, "NN", m=T, n=D, tm=512, tn=D, epilogue=ep_add,
                  extra=[(x2, _tile(512, D))], outs=[(_sds((T, D), F32), _tile(512, D))])
    dx3, dx3_b, dg_final, loss = _final(x3, small["g_final"], target)

    tn_ff = D_FF // 2

    def ep_swiglu_bwd(acc, ex, outs, ids, scr):
        g, u = ex[0][...].astype(F32), ex[1][...].astype(F32)
        sg = jax.nn.sigmoid(g)
        outs[0][...] = (acc * u * sg * (1.0 + g * (1.0 - sg))).astype(BF)
        outs[1][...] = (acc * g * sg).astype(BF)

    dgate, dup = _matmul(
        "ffn_down_bwd", [dx3_b], w_down, "NT", m=T, n=D_FF, tm=512, tn=tn_ff, epilogue=ep_swiglu_bwd,
        extra=[(gate, _tile(512, tn_ff)), (up, _tile(512, tn_ff))],
        outs=[(_sds((T, D_FF), BF), _tile(512, tn_ff)), (_sds((T, D_FF), BF), _tile(512, tn_ff))])

    def dw(name, a, b, rows, cols, row_off=0, alias=None, total_rows=None, colsum=False):
        total_rows = rows if total_rows is None else total_rows
        tmw = rows if rows <= 1024 else D_FF // 2
        blk, rem = divmod(row_off, tmw)
        assert rem == 0

        def ep(acc, ex, outs, ids, scr):
            outs[0][...] = acc.astype(BF)
            if colsum:
                outs[1][...] = jnp.sum(ex[0][...].astype(F32), axis=0, keepdims=True)

        outs = [(_sds((total_rows, cols), BF), pl.BlockSpec((tmw, cols), lambda j, i, k: (blk + i, j)))]
        extra = []
        if colsum:
            extra = [(a, pl.BlockSpec((T, tmw), lambda j, i, k: (0, i)))]
            outs.append((_sds((1, rows), F32), pl.BlockSpec((1, tmw), lambda j, i, k: (0, i))))
        carry = plan.carry(name)
        res = carried(name, _matmul(name, [a], b, "TN", m=rows, n=cols, tm=tmw, tn=cols, epilogue=ep, extra=extra,
                                    outs=outs, alias=None if alias is None else (alias, 0), carry=carry), carry)
        return res if colsum else res[0]

    plan.grad_ready(dict(w_ffn_down=dw("ffn_down_dw", act, dx3_b, D_FF, D)))

    def ep_rms_bwd(acc, ex, outs, ids, scr):
        dx, dg = _rms_bwd(acc, ex[0][...], ex[1][...], ex[2][...])
        dx = ex[3][...] + dx
        outs[0][...] = dx
        outs[1][...] = dx.astype(BF)
        _accumulate_rows(outs[2], dg, ids[1] == 0)

    def rms_bwd_io(tm_, xin, r, g, dres):
        return dict(
            extra=[(xin, _tile(tm_, D)), (r, pl.BlockSpec((tm_, 1), lambda j, i, k: (i, 0))), (g, _row(D)),
                   (dres, _tile(tm_, D))],
            outs=[(_sds((T, D), F32), _tile(tm_, D)), (_sds((T, D), BF), _tile(tm_, D)), (_sds((1, D), F32), _row(D))])

    carry = plan.carry("ffn_in_bwd")
    dx2, dx2_b, dg_ffn = carried(
        "ffn_in_bwd",
        _matmul("ffn_in_bwd", [dgate, dup], wf_t, "NN", m=T, n=D, tm=512, tn=D, tk=D_FF, epilogue=ep_rms_bwd,
                carry=carry, **rms_bwd_io(512, x2, r2, small["g_ffn_norm"], dx3)), carry)
    plan.launch("send_down")
    gwf_t = dw("ffn_in_dw_gate", dgate, h2, D_FF, D, total_rows=2 * D_FF)
    gwf_t = dw("ffn_in_dw_up", dup, h2, D_FF, D, row_off=D_FF, alias=gwf_t, total_rows=2 * D_FF)
    plan.grad_ready(dict(w_ffn_in=gwf_t))

    def ep_merge_bwd(acc, ex, outs, ids, scr):
        s0 = jax.nn.sigmoid(ex[2][...])
        s1 = jax.nn.sigmoid(ex[3][...])
        outs[0][...] = (acc * s0).astype(BF)
        outs[1][...] = (acc * s1).astype(BF)
        outs[2][...] = (acc * ex[0][...] * s0 * (1.0 - s0)).astype(BF)
        outs[3][...] = (acc * ex[1][...] * s1 * (1.0 - s1)).astype(BF)

    carry = plan.carry("out_proj_bwd_merge")
    dya, dyc, dg0, dg1 = carried(
        "out_proj_bwd_merge",
        _matmul("out_proj_bwd_merge", [dx2_b], w_out, "NT", m=T, n=D, tm=tm, tn=tg, epilogue=ep_merge_bwd,
                extra=[(ya, _tile(tm, tg)), (yc, _tile(tm, tg)), (proj, gate_specs[0]), (proj, gate_specs[1])],
                outs=[(_sds((T, D), BF), _tile(tm, tg))] * 4, carry=carry), carry)
    plan.launch("send_ffn")
    gw_out = dw("out_proj_dw", merged, dx2_b, D, D)
    d_o, = _matmul("attn_proj_bwd", [dya], wap_t, "NN", m=T, n=ATTN_WIDTH, tm=tm, tn=ATTN_WIDTH,
                   epilogue=_store(BF), outs=[(_sds((T, ATTN_WIDTH), BF), _tile(tm, ATTN_WIDTH))])
    d_c, = _matmul("conv_proj_bwd", [dyc], wcp_t, "NN", m=T, n=CONV_CHANNELS, tm=tm, tn=CONV_CHANNELS,
                   epilogue=_store(BF), outs=[(_sds((T, CONV_CHANNELS), BF), _tile(tm, CONV_CHANNELS))])
    gwap_t = dw("attn_proj_dw", dya, o, D, ATTN_WIDTH)
    gwcp_t, db_cp = dw("conv_proj_dw", dyc, c, D, CONV_CHANNELS, colsum=True)
    plan.grad_ready(dict(w_out=gw_out, w_attn_proj=gwap_t, w_conv_proj=gwcp_t))
    (dglu, dcw, dcb, dlng, dlnb), got = _conv_bwd(proj, d_c, conv_w, small["conv_b"], small["ln_g"], small["ln_b"],
                                                  carry=plan.carry("conv_bwd"))
    plan.done("conv_bwd", got)
    plan.launch("send_mix")
    (dqkv, dsinks), got = _attn_bwd(proj, d_o, small["sinks"], carry=plan.carry("attn_bwd"))
    plan.done("attn_bwd", got)

    segs = [dqkv, dglu, dg0, dg1]
    gwi_t, db_in = _proj_in_dw(segs, h)
    plan.grad_ready(dict(w_in=gwi_t))
    plan.alone("swap_inp")
    plan.launch("send_inp")
    carry = plan.carry("proj_in_bwd")
    dx, _, dg_mix = carried(
        "proj_in_bwd",
        _matmul("proj_in_bwd", segs, wi_t, "NN", m=T, n=D, tm=512, tn=D, epilogue=ep_rms_bwd, carry=carry,
                **rms_bwd_io(512, x, r1, small["g_mix_norm"], dx2)), carry)

    parts = dict(g_mix_norm=dg_mix, b_in=db_in, sinks=dsinks, conv_w=dcw, conv_b=dcb, ln_g=dlng, ln_b=dlnb,
                 b_conv_proj=db_cp, g_ffn_norm=dg_ffn, g_final=dg_final, loss=loss)
    return dx, parts


def _place():
    x, y, c = lax.axis_index("x"), lax.axis_index("y"), lax.axis_index("c")
    return x, y, c, [(1 - x, y), (x, 1 - y), (1 - x, 1 - y)]


def _gather_copies(x_refs, out_refs, rows_per, send_sems, recv_sems, local_sems):
    x, y, c, chips = _place()
    me, sibling = (x, y, c), (x, y, 1 - c)

    def rows(a, px, py, pc):
        return out_refs[a].at[pl.ds((4 * px + 2 * py + pc) * rows_per[a], rows_per[a])]

    def copy(a, k, block, to, src=None):
        return pltpu.make_async_remote_copy(
            src_ref=rows(a, *block) if src is None else src, dst_ref=rows(a, *block),
            send_sem=send_sems.at[7 * a + k], recv_sem=recv_sems.at[7 * a + k], device_id=to, device_id_type=MESH)

    def local(a):
        return pltpu.make_async_copy(x_refs[a], rows(a, *me), local_sems.at[a])

    def first(a):
        return [copy(a, 0, me, sibling, src=x_refs[a])] + [copy(a, 1 + j, me, (*chip, c), src=x_refs[a])
                                                          for j, chip in enumerate(chips)]

    def arrive(a, j):
        return copy(a, 1 + j, (*chips[j], c), me)

    def passed(a, j):
        return copy(a, 4 + j, (*chips[j], c), sibling)

    def from_sibling(a):
        return [copy(a, 0, sibling, me)] + [copy(a, 4 + j, (*chip, 1 - c), me) for j, chip in enumerate(chips)]

    return len(x_refs), local, first, arrive, passed, from_sibling


def _gather_start(*refs):
    n, local, first, _, _, _ = _gather_copies(*refs)
    for a in range(n):
        local(a).start()
        for cp in first(a):
            cp.start()


def _gather_finish(*refs):
    n, local, first, arrive, passed, from_sibling = _gather_copies(*refs)
    for a in range(n):
        for j in range(3):
            arrive(a, j).wait_recv()
            passed(a, j).start()
    for a in range(n):
        for cp in from_sibling(a):
            cp.wait_recv()
    for a in range(n):
        for cp in first(a) + [passed(a, j) for j in range(3)]:
            cp.wait_send()
        local(a).wait()


def _gather_peers():
    x, y, c, chips = _place()
    return [(x, y, 1 - c)] + [(*chip, c) for chip in chips]


def _gather_sems(n):
    return [pltpu.SemaphoreType.DMA((7 * n,)), pltpu.SemaphoreType.DMA((7 * n,)), pltpu.SemaphoreType.DMA((n,))]


def _gather_carry(shards):
    rows_per = [s.shape[0] for s in shards]
    return _Carry(shards, [_sds((N_DEV * s.shape[0],) + s.shape[1:], s.dtype) for s in shards],
                  _gather_sems(len(shards)),
                  lambda ins, outs, sems: _gather_start(ins, outs, rows_per, *sems),
                  lambda ins, outs, sems: _gather_finish(ins, outs, rows_per, *sems), _gather_peers)


def _swap_carry(grads):
    n = len(grads)

    def copies(g_refs, out_refs, sems):
        send_sems, recv_sems = sems
        x, y, c, _ = _place()
        return [pltpu.make_async_remote_copy(
            src_ref=g_refs[a].at[2 * p + 1 - c], dst_ref=out_refs[a].at[p],
            send_sem=send_sems.at[4 * a + p], recv_sem=recv_sems.at[4 * a + p],
            device_id=(x, y, 1 - c), device_id_type=MESH) for a in range(n) for p in range(4)]

    def start(ins, outs, sems):
        for cp in copies(ins, outs, sems):
            cp.start()

    def finish(ins, outs, sems):
        for cp in copies(ins, outs, sems):
            cp.wait()

    def peers():
        x, y, c, _ = _place()
        return [(x, y, 1 - c)]

    return _Carry(grads, [_sds((4,) + g.shape[1:], g.dtype) for g in grads],
                  [pltpu.SemaphoreType.DMA((4 * n,)), pltpu.SemaphoreType.DMA((4 * n,))], start, finish, peers)


def _join(carries):
    carries = [c for c in carries if c is not None]
    if not carries:
        return None
    n_in = [len(c.arrays) for c in carries]
    n_out = [len(c.out_shapes) for c in carries]
    n_sem = [len(c.sems) for c in carries]

    def parts(refs, counts):
        cuts = [sum(counts[:q]) for q in range(len(counts) + 1)]
        return [refs[cuts[q]:cuts[q + 1]] for q in range(len(counts))]

    def start(ins, outs, sems):
        for c, i, o, s in zip(carries, parts(ins, n_in), parts(outs, n_out), parts(sems, n_sem)):
            c.start(i, o, s)

    def finish(ins, outs, sems):
        for c, i, o, s in zip(carries, parts(ins, n_in), parts(outs, n_out), parts(sems, n_sem)):
            c.finish(i, o, s)

    return _Carry([a for c in carries for a in c.arrays], [o for c in carries for o in c.out_shapes],
                  [s for c in carries for s in c.sems], start, finish)


def _run_carry(name, carry):
    n_in, n_out = len(carry.arrays), len(carry.out_shapes)

    def body(*refs):
        carry.start(refs[:n_in], refs[n_in:n_in + n_out], refs[n_in + n_out:])
        carry.finish(refs[:n_in], refs[n_in:n_in + n_out], refs[n_in + n_out:])

    return pl.pallas_call(body, name=name, in_specs=[ANY] * n_in, out_specs=[ANY] * n_out,
                          out_shape=carry.out_shapes, scratch_shapes=carry.sems)(*carry.arrays)


def _run_carry_async(name, carry, collective_id):
    ins = [jax.new_ref(a, memory_space=pltpu.MemorySpace.HBM) for a in carry.arrays]
    outs = [jax.empty_ref(o, memory_space=pltpu.MemorySpace.HBM) for o in carry.out_shapes]

    @pl.kernel(mesh=plsc.ScalarSubcoreMesh(axis_name="sequencer", num_cores=1), name=name,
               scratch_types=tuple(carry.sems), compiler_params=pltpu.CompilerParams(collective_id=collective_id))
    def launch(*sems):
        barrier = pltpu.get_barrier_semaphore()
        peers = carry.peers()
        for peer in peers:
            pl.semaphore_signal(barrier, inc=1, device_id=peer, device_id_type=MESH)
        pl.semaphore_wait(barrier, len(peers))
        carry.start(ins, outs, sems)
        carry.finish(ins, outs, sems)

    launch()
    return [o[...] for o in outs]


def _chip_sum(name, g, got, c):
    _, rows, cols = g.shape

    def body(c_ref, g_ref, got_ref, o_ref):
        o_ref[...] = (g_ref[...].astype(F32) + got_ref[...].astype(F32)).astype(BF)

    return pl.pallas_call(
        body, name=name,
        grid_spec=pltpu.PrefetchScalarGridSpec(
            num_scalar_prefetch=1, grid=(4,),
            in_specs=[pl.BlockSpec((1, rows, cols), lambda p, c_ref: (2 * p + c_ref[0], 0, 0)),
                      pl.BlockSpec((1, rows, cols), lambda p, c_ref: (p, 0, 0))],
            out_specs=pl.BlockSpec((1, rows, cols), lambda p, c_ref: (p, 0, 0))),
        out_shape=_sds((4, rows, cols), BF),
        compiler_params=_params(("arbitrary",)),
    )(c, g, got)


def _send_carry(sums, ks):
    n, nk = len(sums), len(ks)

    def copies(s_refs, out_refs, sems):
        send_sems, recv_sems = sems
        x, y, c, chips = _place()
        return [pltpu.make_async_remote_copy(
            src_ref=s_refs[a].at[2 * chips[k][0] + chips[k][1]], dst_ref=out_refs[a].at[q],
            send_sem=send_sems.at[nk * a + q], recv_sem=recv_sems.at[nk * a + q],
            device_id=(*chips[k], c), device_id_type=MESH) for a in range(n) for q, k in enumerate(ks)]

    def start(ins, outs, sems):
        for cp in copies(ins, outs, sems):
            cp.start()

    def finish(ins, outs, sems):
        for cp in copies(ins, outs, sems):
            cp.wait()

    def peers():
        x, y, c, chips = _place()
        return [(*chips[k], c) for k in ks]

    return _Carry(sums, [_sds((nk,) + s.shape[1:], s.dtype) for s in sums],
                  [pltpu.SemaphoreType.DMA((nk * n,)), pltpu.SemaphoreType.DMA((nk * n,))], start, finish, peers)


def _grad_total(name, g, got, got3, ids):
    _, rows, cols = g.shape
    n3 = len(got3)

    def body(ids_ref, g_ref, got_ref, *rest):
        o_ref = rest[n3]
        tot = g_ref[0].astype(F32) + got_ref[0].astype(F32)
        for r_ref in rest[:n3]:
            for q in range(r_ref.shape[0]):
                tot = tot + r_ref[q].astype(F32)
        o_ref[...] = tot

    return pl.pallas_call(
        body, name=name,
        grid_spec=pltpu.PrefetchScalarGridSpec(
            num_scalar_prefetch=1, grid=(1,),
            in_specs=[pl.BlockSpec((1, rows, cols), lambda i, ids_ref: (ids_ref[0], 0, 0)),
                      pl.BlockSpec((1, rows, cols), lambda i, ids_ref: (ids_ref[1], 0, 0)),
                      *[pl.BlockSpec(r.shape, lambda i, ids_ref: (0, 0, 0)) for r in got3]],
            out_specs=pl.BlockSpec((rows, cols), lambda i, ids_ref: (0, 0))),
        out_shape=_sds((rows, cols), F32),
        compiler_params=_params(("arbitrary",)),
    )(ids, g, got, *got3)


def _adam_math(w, g, m, v):
    m = ADAM_B1 * m + (1.0 - ADAM_B1) * g
    v = ADAM_B2 * v + (1.0 - ADAM_B2) * (g * g)
    m_hat = m / (1.0 - ADAM_B1 ** ADAM_STEP)
    v_hat = v / (1.0 - ADAM_B2 ** ADAM_STEP)
    delta = -ADAM_LR * (m_hat / (jnp.sqrt(v_hat) + ADAM_EPS) + ADAM_WD * w)
    return delta, m, v


def _adamw(name, w, g, m, v):
    rows, cols = w.shape
    tr = 256 if rows % 256 == 0 else rows

    def body(w_ref, g_ref, m_ref, v_ref, d_ref, nm_ref, nv_ref):
        d_ref[...], nm_ref[...], nv_ref[...] = _adam_math(w_ref[...], g_ref[...], m_ref[...], v_ref[...])

    t = pl.BlockSpec((tr, cols), lambda i: (i, 0))
    return pl.pallas_call(
        body, name=name, grid=(rows // tr,), in_specs=[t] * 4, out_specs=[t] * 3,
        out_shape=[_sds((rows, cols), F32)] * 3, compiler_params=_params(("arbitrary",)),
    )(w, g, m, v)


SMALL_NAMES = ["g_mix_norm", "b_in", "sinks", "conv_b", "ln_g", "ln_b", "b_conv_proj", "g_ffn_norm", "g_final"]
_PACK_ROWS = 32


def _small_pack(parts):
    C = CONV_CHANNELS
    part_list = [parts["g_mix_norm"], parts["b_in"], parts["sinks"], parts["conv_b"], parts["ln_g"], parts["ln_b"],
                 parts["b_conv_proj"], parts["g_ffn_norm"], parts["g_final"], parts["loss"], parts["conv_w"]]

    def body(p_mix, p_b, p_sink, p_cb, p_lg, p_lb, p_bcp, p_ffn, p_fin, p_loss, p_cw, pack):
        pack[...] = jnp.zeros_like(pack)
        pack[0:1, :] = p_mix[...]
        pack[1:2, 0:GLU_OFF] = p_b[:, 0:GLU_OFF]
        pack[2:3, :] = p_b[:, GLU_OFF:GATE_OFF]
        pack[3:4, :] = p_b[:, GATE_OFF:GATE_OFF + D_MODEL]
        pack[4:5, :] = p_b[:, GATE_OFF + D_MODEL:]
        pack[5:6, 0:128] = p_sink[...]
        pack[6:7, 0:C] = p_cb[...]
        pack[6:7, C:2 * C] = p_lg[...]
        pack[7:8, 0:C] = p_lb[...]
        pack[8:9, :] = p_bcp[...]
        pack[9:10, :] = p_ffn[...]
        pack[10:11, :] = p_fin[...]
        pack[11:12, 0:128] = jnp.broadcast_to(p_loss[...], (1, 128))
        pack[12:28, 0:C] = p_cw[0:16, :]
        pack[12:28, C:2 * C] = p_cw[16:32, :]

    vm = pl.BlockSpec(memory_space=pltpu.VMEM)
    return pl.pallas_call(body, name="small_pack", in_specs=[vm] * len(part_list), out_specs=vm,
                          out_shape=_sds((_PACK_ROWS, D_MODEL), F32))(*part_list)


def _small_adamw(gathered, small_w, small_m, small_v):
    C = CONV_CHANNELS
    names = SMALL_NAMES
    widths = [small_w[k].shape[1] for k in names]
    n_small = len(names)

    def body(*refs):
        tot_ref = refs[0]
        w_refs = refs[1:1 + n_small]
        m_refs = refs[1 + n_small:1 + 2 * n_small]
        v_refs = refs[1 + 2 * n_small:1 + 3 * n_small]
        o = 1 + 3 * n_small
        loss_ref, cw_ref = refs[o], refs[o + 1]
        out_refs = refs[o + 2:o + 2 + 4 * n_small]
        tot = tot_ref[0:_PACK_ROWS, :]
        for d in range(1, N_DEV):
            tot = tot + tot_ref[d * _PACK_ROWS:(d + 1) * _PACK_ROWS, :]
        loss_ref[...] = tot[11:12, 0:1]
        cw_ref[0:16, :] = tot[12:28, 0:C]
        cw_ref[16:32, :] = tot[12:28, C:2 * C]
        grads = dict(
            g_mix_norm=tot[0:1, :],
            b_in=jnp.concatenate([tot[1:2, 0:GLU_OFF], tot[2:3, :], tot[3:4, :], tot[4:5, :]], axis=1),
            sinks=tot[5:6, 0:N_Q_HEADS], conv_b=tot[6:7, 0:C], ln_g=tot[6:7, C:2 * C], ln_b=tot[7:8, 0:C],
            b_conv_proj=tot[8:9, :], g_ffn_norm=tot[9:10, :], g_final=tot[10:11, :])
        for s, k in enumerate(names):
            g = grads[k]
            d, nm, nv = _adam_math(w_refs[s][...], g, m_refs[s][...], v_refs[s][...])
            out_refs[4 * s][...] = g
            out_refs[4 * s + 1][...] = d
            out_refs[4 * s + 2][...] = nm
            out_refs[4 * s + 3][...] = nv

    vm = pl.BlockSpec(memory_space=pltpu.VMEM)
    args = [gathered, *[small_w[k] for k in names], *[small_m[k] for k in names], *[small_v[k] for k in names]]
    out_shape = [_sds((1, 1), F32), _sds((CONV_PAD, C), F32)]
    for wd in widths:
        out_shape += [_sds((1, wd), F32)] * 4
    res = pl.pallas_call(
        body, name="small_adamw",
        in_specs=[vm] * len(args), out_specs=[vm] * len(out_shape), out_shape=out_shape,
        compiler_params=pltpu.CompilerParams(vmem_limit_bytes=VMEM_LIMIT_BYTES),
    )(*args)
    return res[0], res[1], {k: res[2 + 4 * s:6 + 4 * s] for s, k in enumerate(names)}


BIG = dict(w_in=True, w_attn_proj=True, w_conv_proj=True, w_out=False, w_ffn_in=True, w_ffn_down=False)
WEIGHT_NAMES = ["g_mix_norm", "w_in", "b_in", "sinks", "conv_w", "conv_b", "ln_g", "ln_b", "w_attn_proj",
                "w_conv_proj", "b_conv_proj", "w_out", "g_ffn_norm", "w_ffn_in", "w_ffn_down", "g_final"]


class _Plan:
    GROUPS = dict(down=["w_ffn_down"], ffn=["w_ffn_in"], mix=["w_out", "w_attn_proj", "w_conv_proj"], inp=["w_in"])
    ALL = (0, 1, 2)
    RIDES = dict(
        gather_mix=[("gather", ["w_attn_proj", "w_conv_proj", "w_out"])], gather_ffn=[("gather", ["w_ffn_in"])],
        gather_down=[("gather", ["w_ffn_down"])],
        ffn_in_bwd=[("swap", "down")], send_down=[("send", "down", ALL)],
        out_proj_bwd_merge=[("swap", "ffn")], send_ffn=[("send", "ffn", ALL)],
        conv_bwd=[("swap", "mix")], send_mix=[("send", "mix", ALL)],
        swap_inp=[("swap", "inp")], send_inp=[("send", "inp", ALL)])
    ASYNC = dict(gather_mix=1, gather_ffn=2, gather_down=3, send_down=4, send_ffn=5, send_mix=6, send_inp=7)

    def __init__(self, shards, c1):
        self.shards, self.c1 = shards, c1
        self.full, self.slots, self.got, self.sums, self.got3 = {}, {}, {}, {}, {}

    def weight(self, name):
        return self.full[name]

    def grad_ready(self, grads):
        for k, g in grads.items():
            self.slots[k] = g.reshape(N_DEV, g.shape[0] // N_DEV, g.shape[1])

    def _one(self, kind, what, ks=None):
        if kind == "gather":
            return _gather_carry([self.shards[k] for k in what])
        names = self.GROUPS[what]
        if kind == "swap":
            return _swap_carry([self.slots[k] for k in names])
        return _send_carry([self.sums[k] for k in names], ks)

    def carry(self, call):
        return _join([self._one(*ride) for ride in self.RIDES.get(call, [])])

    def done(self, call, outs):
        outs = list(outs)
        for kind, what, *_ in self.RIDES.get(call, []):
            names = what if kind == "gather" else self.GROUPS[what]
            mine, outs = outs[:len(names)], outs[len(names):]
            if kind == "gather":
                self.full.update(zip(names, mine))
            elif kind == "send":
                for k, r in zip(names, mine):
                    self.got3.setdefault(k, []).append(r)
            else:
                for k, r in zip(names, mine):
                    self.got[k] = r
                    self.sums[k] = _chip_sum(f"chip_sum_{k}", self.slots[k], r, self.c1)

    def alone(self, call):
        self.done(call, _run_carry(call, self.carry(call)))

    def launch(self, call, after=None):
        carry = self._one(*self.RIDES[call][0])
        if after is not None:
            carry.arrays = list(lax.optimization_barrier((tuple(carry.arrays), after))[0])
        self.done(call, _run_carry_async(call, carry, self.ASYNC[call]))


def kernel(x, g_mix_norm, w_in, b_in, sinks, conv_w, conv_b, ln_g, ln_b, w_attn_proj, w_conv_proj, b_conv_proj, w_out, g_ffn_norm, w_ffn_in, w_ffn_down, g_final, loss_target, m_g_mix_norm, m_w_in, m_b_in, m_sinks, m_conv_w, m_conv_b, m_ln_g, m_ln_b, m_w_attn_proj, m_w_conv_proj, m_b_conv_proj, m_w_out, m_g_ffn_norm, m_w_ffn_in, m_w_ffn_down, m_g_final, v_g_mix_norm, v_w_in, v_b_in, v_sinks, v_conv_w, v_conv_b, v_ln_g, v_ln_b, v_w_attn_proj, v_w_conv_proj, v_b_conv_proj, v_w_out, v_g_ffn_norm, v_w_ffn_in, v_w_ffn_down, v_g_final):
    w = dict(g_mix_norm=g_mix_norm, w_in=w_in, b_in=b_in, sinks=sinks, conv_w=conv_w, conv_b=conv_b, ln_g=ln_g,
             ln_b=ln_b, w_attn_proj=w_attn_proj, w_conv_proj=w_conv_proj, b_conv_proj=b_conv_proj, w_out=w_out,
             g_ffn_norm=g_ffn_norm, w_ffn_in=w_ffn_in, w_ffn_down=w_ffn_down, g_final=g_final)
    m = dict(g_mix_norm=m_g_mix_norm, w_in=m_w_in, b_in=m_b_in, sinks=m_sinks, conv_w=m_conv_w, conv_b=m_conv_b,
             ln_g=m_ln_g, ln_b=m_ln_b, w_attn_proj=m_w_attn_proj, w_conv_proj=m_w_conv_proj,
             b_conv_proj=m_b_conv_proj, w_out=m_w_out, g_ffn_norm=m_g_ffn_norm, w_ffn_in=m_w_ffn_in,
             w_ffn_down=m_w_ffn_down, g_final=m_g_final)
    v = dict(g_mix_norm=v_g_mix_norm, w_in=v_w_in, b_in=v_b_in, sinks=v_sinks, conv_w=v_conv_w, conv_b=v_conv_b,
             ln_g=v_ln_g, ln_b=v_ln_b, w_attn_proj=v_w_attn_proj, w_conv_proj=v_w_conv_proj,
             b_conv_proj=v_b_conv_proj, w_out=v_w_out, g_ffn_norm=v_g_ffn_norm, w_ffn_in=v_w_ffn_in,
             w_ffn_down=v_w_ffn_down, g_final=v_g_final)
    ax, ay, ac = lax.axis_index("x"), lax.axis_index("y"), lax.axis_index("c")
    me = 4 * ax + 2 * ay + ac
    chip = 2 * ax + ay

    shards = {k: (w[k][0].T if tr else w[k][0]).astype(BF) for k, tr in BIG.items()}
    cw_shard = jnp.pad(conv_w[0].T, ((0, 0), (0, 1))).reshape(16, 128)
    wi_t, cw_full = _run_carry("weights_all_gather", _gather_carry([shards["w_in"], cw_shard]))
    conv_full = cw_full.reshape(CONV_CHANNELS, CONV_PAD).T

    as_row = lambda a: a.reshape(1, -1)
    small_w = {k: as_row(w[k]) for k in SMALL_NAMES}
    small_m = {k: as_row(m[k]) for k in SMALL_NAMES}
    small_v = {k: as_row(v[k]) for k in SMALL_NAMES}
    plan = _Plan(shards, ac.reshape(1).astype(jnp.int32))
    plan.launch("gather_mix", after=wi_t)
    dx, parts = _local_step(x[0], loss_target[0], small_w, wi_t, conv_full, plan)

    small_gathered, = _run_carry_async("small_gather", _gather_carry([_small_pack(parts)]), 8)

    ids = jnp.stack([me, chip]).astype(jnp.int32)
    grads, delta, new_m, new_v = {}, {}, {}, {}
    for k in sorted(BIG, key=lambda k: k == "w_in"):
        tot = _grad_total(f"grad_total_{k}", plan.slots[k], plan.got[k], plan.got3[k], ids)
        tot = tot.T if BIG[k] else tot
        d, nm, nv = _adamw(f"adamw_{k}", w[k][0], tot, m[k][0], v[k][0])
        grads[k], delta[k], new_m[k], new_v[k] = tot[None], d[None], nm[None], nv[None]

    loss, cw_grad, small_out = _small_adamw(small_gathered, small_w, small_m, small_v)
    for k in SMALL_NAMES:
        g, d, nm, nv = (a.reshape(w[k].shape) for a in small_out[k])
        grads[k], delta[k], new_m[k], new_v[k] = g, d, nm, nv
    cw_mine = lax.dynamic_slice(cw_grad, (0, me * 64), (CONV_WIDTH, 64))
    d, nm, nv = _adamw("adamw_conv_w", conv_w[0], cw_mine, m_conv_w[0], v_conv_w[0])
    grads["conv_w"], delta["conv_w"], new_m["conv_w"], new_v["conv_w"] = cw_mine[None], d[None], nm[None], nv[None]

    return (loss.reshape(()), dx[None], *[grads[k] for k in WEIGHT_NAMES], *[delta[k] for k in WEIGHT_NAMES],
            *[new_m[k] for k in WEIGHT_NAMES], *[new_v[k] for k in WEIGHT_NAMES])
```

```python
import functools

import jax
import jax.numpy as jnp
from jax import lax
from jax.experimental import pallas as pl
from jax.experimental.pallas import tpu as pltpu
from jax.experimental.pallas import tpu_sc as plsc

F32 = jnp.float32
BF = jnp.bfloat16

SEQ = 2048
D_MODEL = 1024
HEAD_DIM = 64
N_Q_HEADS = 8
N_KV_HEADS = 2
GROUP = N_Q_HEADS // N_KV_HEADS
BLOCK = 128
ATTN_WIDTH = 512
KV_WIDTH = 128
CONV_CHANNELS = 512
CONV_WIDTH = 31
CONV_PAD = 32
GLU_OFF = 768
GATE_OFF = 1792
IN_WIDTH = 3840
D_FF = 2816
EPS = 1e-5
NEG = -1e30
N_DEV = 8

ADAM_LR = 0.001
ADAM_B1 = 0.9
ADAM_B2 = 0.999
ADAM_EPS = 1e-08
ADAM_WD = 0.01
ADAM_STEP = 10

VMEM_LIMIT_BYTES = 56 * 1024 * 1024
MESH = pl.DeviceIdType.MESH
ANY = pl.BlockSpec(memory_space=pl.ANY)

_DIMS = {"NN": (((1,), (0,)), ((), ())), "NT": (((1,), (1,)), ((), ())), "TN": (((0,), (0,)), ((), ()))}


def _params(sem):
    return pltpu.CompilerParams(dimension_semantics=sem, vmem_limit_bytes=VMEM_LIMIT_BYTES)


class _Carry:
    def __init__(self, arrays, out_shapes, sems, start, finish, peers=None):
        self.arrays, self.out_shapes, self.sems, self.start, self.finish = arrays, out_shapes, sems, start, finish
        self.peers = peers


def _carry_io(carry):
    if carry is None:
        return [], [], []
    return list(carry.arrays), list(carry.out_shapes), list(carry.sems)


def _matmul(name, a_list, b, mode, *, m, n, tm, tn, tk=None, epilogue, extra=(), outs, b_off=(0, 0), alias=None,
            scratch=(), carry=None):
    seg_k = [a.shape[0] if mode == "TN" else a.shape[1] for a in a_list]
    whole = tk is None
    seg_nk = [1] * len(a_list) if whole else [ks // tk for ks in seg_k]
    nk = 1 if whole else sum(seg_nk)
    starts = [sum(seg_nk[:s]) for s in range(len(seg_nk))]
    k_starts = [sum(seg_k[:s]) for s in range(len(seg_k))]
    k_tot = sum(seg_k)
    n_a, n_extra, n_out = len(a_list), len(extra), len(outs)

    a_specs = []
    for st, ns, ks in zip(starts, seg_nk, seg_k):
        if mode == "TN":
            a_specs.append(pl.BlockSpec((ks if whole else tk, tm), lambda j, i, k: (k, i)))
        elif whole:
            a_specs.append(pl.BlockSpec((tm, ks), lambda j, i, k: (i, 0)))
        else:
            a_specs.append(pl.BlockSpec((tm, tk), functools.partial(
                lambda j, i, k, st, ns: (i, jnp.clip(k - st, 0, ns - 1)), st=st, ns=ns)))
    bk = k_tot if whole else tk
    if mode == "NT":
        b_spec = pl.BlockSpec((tn, bk), lambda j, i, k: (b_off[0] + j, b_off[1] + k))
    else:
        b_spec = pl.BlockSpec((bk, tn), lambda j, i, k: (b_off[0] + k, b_off[1] + j))
    n_alias = 0 if alias is None else 1
    c_in, c_out, c_sems = _carry_io(carry)
    n_acc = 0 if whole else 1
    nj, ni = n // tn, m // tm

    def body(*refs):
        pos = [n_a, 1, n_alias, n_extra, len(c_in), n_out, len(c_out), n_acc, len(scratch), len(c_sems)]
        cuts = [sum(pos[:q]) for q in range(len(pos) + 1)]
        a_refs, (b_ref,), _, ex, ci_refs, out_refs, co_refs, acc_refs, scr, cs_refs = (
            refs[cuts[q]:cuts[q + 1]] for q in range(len(pos)))
        j, i, k = pl.program_id(0), pl.program_id(1), pl.program_id(2)
        ids = (j, i)
        if carry is not None:
            @pl.when((j == 0) & (i == 0) & (k == 0))
            def _():
                carry.start(ci_refs, co_refs, cs_refs)

        def dot(a_ref, bv):
            return lax.dot_general(a_ref[...].astype(BF), bv.astype(BF), _DIMS[mode], preferred_element_type=F32)

        if whole:
            tot = None
            for a_ref, k0, ks in zip(a_refs, k_starts, seg_k):
                if n_a == 1:
                    bv = b_ref[...]
                else:
                    bv = b_ref[:, k0:k0 + ks] if mode == "NT" else b_ref[k0:k0 + ks, :]
                part = dot(a_ref, bv)
                tot = part if tot is None else tot + part
            epilogue(tot, ex, out_refs, ids, scr)
        else:
            acc, = acc_refs

            @pl.when(k == 0)
            def _():
                acc[...] = jnp.zeros_like(acc)

            for a_ref, st, ns in zip(a_refs, starts, seg_nk):
                if n_a == 1:
                    acc[...] += dot(a_ref, b_ref[...])
                else:
                    @pl.when((k >= st) & (k < st + ns))
                    def _(a_ref=a_ref):
                        acc[...] += dot(a_ref, b_ref[...])

            @pl.when(k == nk - 1)
            def _():
                epilogue(acc[...], ex, out_refs, ids, scr)

        if carry is not None:
            @pl.when((j == nj - 1) & (i == ni - 1) & (k == nk - 1))
            def _():
                carry.finish(ci_refs, co_refs, cs_refs)

    in_specs = [*a_specs, b_spec]
    args = [*a_list, b]
    io_alias = {}
    if alias is not None:
        in_specs.append(pl.BlockSpec(memory_space=pl.ANY))
        args.append(alias[0])
        io_alias = {n_a + 1: alias[1]}
    in_specs += [s for _, s in extra] + [pl.BlockSpec(memory_space=pl.ANY)] * len(c_in)
    args += [x for x, _ in extra] + c_in
    res = pl.pallas_call(
        body, name=name, grid=(nj, ni, nk), in_specs=in_specs,
        out_specs=[s for _, s in outs] + [pl.BlockSpec(memory_space=pl.ANY)] * len(c_out),
        out_shape=[o for o, _ in outs] + c_out,
        scratch_shapes=[*([] if whole else [pltpu.VMEM((tm, tn), F32)]), *scratch, *c_sems],
        input_output_aliases=io_alias,
        compiler_params=_params(("arbitrary", "arbitrary", "arbitrary")),
    )(*args)
    return res if carry is None else (res[:n_out], res[n_out:])


def _tile(tm, tn):
    return pl.BlockSpec((tm, tn), lambda j, i, k: (i, j))


def _row(tn):
    return pl.BlockSpec((1, tn), lambda j, i, k: (0, j))


def _store(dtype):
    def ep(acc, ex, outs, ids, scr):
        outs[0][...] = acc.astype(dtype)
    return ep


def _sds(shape, dtype):
    return jax.ShapeDtypeStruct(shape, dtype)


def _rms_fwd(name, x, g):
    T, D = x.shape
    tm = 512

    def body(x_ref, g_ref, h_ref, r_ref):
        xv = x_ref[...]
        r = lax.rsqrt(jnp.mean(xv * xv, axis=-1, keepdims=True) + EPS)
        h_ref[...] = (xv * r * g_ref[...]).astype(BF)
        r_ref[...] = r

    return pl.pallas_call(
        body, name=name, grid=(T // tm,),
        in_specs=[pl.BlockSpec((tm, D), lambda i: (i, 0)), pl.BlockSpec((1, D), lambda i: (0, 0))],
        out_specs=[pl.BlockSpec((tm, D), lambda i: (i, 0)), pl.BlockSpec((tm, 1), lambda i: (i, 0))],
        out_shape=[_sds((T, D), BF), _sds((T, 1), F32)],
        compiler_params=_params(("arbitrary",)),
    )(x, g)


def _rms_bwd(dh, xv, r, g):
    xh = xv * r
    dxh = dh * g
    dx = r * (dxh - xh * jnp.mean(dxh * xh, axis=-1, keepdims=True))
    return dx, jnp.sum(dh * xh, axis=0, keepdims=True)


def _accumulate_rows(ref, val, first):
    @pl.when(first)
    def _():
        ref[...] = val

    @pl.when(jnp.logical_not(first))
    def _():
        ref[...] += val


def _loss_head(xv, g, target):
    r = lax.rsqrt(jnp.mean(xv * xv, axis=-1, keepdims=True) + EPS)
    err = xv * r * g - target
    dx, dg = _rms_bwd(err * (1.0 / xv.shape[-1]), xv, r, g)
    part = 0.5 * jnp.sum(jnp.mean(err * err, axis=-1, keepdims=True), axis=0, keepdims=True)
    return dx, dg, part


def _lane_half(shape, h):
    lane = lax.broadcasted_iota(jnp.int32, shape, 1)
    return (lane >= HEAD_DIM * h) & (lane < HEAD_DIM * (h + 1))


def _to_half(v, w, h):
    if w != h:
        v = pltpu.roll(v, HEAD_DIM, 1)
    return jnp.where(_lane_half(v.shape, h), v, 0.0)


def _attn_block(qkv_ref, sinks_ref, n, h):
    r0 = pl.multiple_of(n * BLOCK, BLOCK)
    p0 = pl.multiple_of(jnp.maximum(n - 1, 0) * BLOCK, BLOCK)
    rows = pl.ds(r0, BLOCK)
    prev = pl.ds(p0, BLOCK)
    k2 = jnp.concatenate([qkv_ref[prev, ATTN_WIDTH:ATTN_WIDTH + KV_WIDTH],
                          qkv_ref[rows, ATTN_WIDTH:ATTN_WIDTH + KV_WIDTH]], axis=0).astype(BF)
    v2 = jnp.concatenate([qkv_ref[prev, ATTN_WIDTH + KV_WIDTH:ATTN_WIDTH + 2 * KV_WIDTH],
                          qkv_ref[rows, ATTN_WIDTH + KV_WIDTH:ATTN_WIDTH + 2 * KV_WIDTH]], axis=0).astype(BF)
    qs = []
    for g in range(GROUP):
        hq = GROUP * h + g
        blk = qkv_ref[rows, (hq // 2) * 128:(hq // 2 + 1) * 128]
        qs.append(_to_half(blk, hq % 2, h))
    q4 = jnp.concatenate(qs, axis=0).astype(BF)
    s = lax.dot_general(q4, k2, _DIMS["NT"], preferred_element_type=F32) * (HEAD_DIM ** -0.5)
    shape = s.shape
    row = lax.broadcasted_iota(jnp.int32, shape, 0)
    qi = row & (BLOCK - 1)
    kj = lax.broadcasted_iota(jnp.int32, shape, 1)
    diff = qi + BLOCK - kj
    valid = (diff >= 0) & (diff < BLOCK) & ((kj >= BLOCK) | (n > 0))
    s = jnp.where(valid, s, NEG)
    row1 = lax.broadcasted_iota(jnp.int32, (shape[0], 1), 0)
    sink = jnp.zeros((shape[0], 1), F32)
    for g in range(GROUP):
        sink = jnp.where((row1 >= g * BLOCK) & (row1 < (g + 1) * BLOCK), sinks_ref[0, GROUP * h + g], sink)
    m = jnp.maximum(jnp.max(s, axis=-1, keepdims=True), sink)
    e = jnp.exp(s - m)
    es = jnp.exp(sink - m)
    inv = 1.0 / (jnp.sum(e, axis=-1, keepdims=True) + es)
    return e * inv, es * inv, q4, k2, v2, rows, prev


def _attn_fwd(proj, sinks, carry=None):
    T = proj.shape[0]
    c_in, c_out, c_sems = _carry_io(carry)

    def body(*refs):
        qkv_ref, sinks_ref = refs[:2]
        ci_refs = refs[2:2 + len(c_in)]
        o_ref = refs[2 + len(c_in)]
        co_refs = refs[3 + len(c_in):3 + len(c_in) + len(c_out)]
        cs_refs = refs[3 + len(c_in) + len(c_out):]
        if carry is not None:
            carry.start(ci_refs, co_refs, cs_refs)

        def blk(n, z):
            outs = [None] * (N_Q_HEADS // 2)
            for h in range(N_KV_HEADS):
                p, _, _, _, v2, rows, _ = _attn_block(qkv_ref, sinks_ref, n, h)
                o = lax.dot_general(p.astype(BF), v2, _DIMS["NN"], preferred_element_type=F32)
                for g in range(GROUP):
                    hq = GROUP * h + g
                    piece = jnp.where(_lane_half((BLOCK, 128), h), o[g * BLOCK:(g + 1) * BLOCK], 0.0)
                    if hq % 2 != h:
                        piece = pltpu.roll(piece, HEAD_DIM, 1)
                    outs[hq // 2] = piece if outs[hq // 2] is None else outs[hq // 2] + piece
            for pb in range(N_Q_HEADS // 2):
                o_ref[rows, pb * 128:(pb + 1) * 128] = outs[pb].astype(BF)
            return z

        lax.fori_loop(0, T // BLOCK, blk, 0)
        if carry is not None:
            carry.finish(ci_refs, co_refs, cs_refs)

    res = pl.pallas_call(
        body, name="attn_fwd", grid=(1,),
        in_specs=[pl.BlockSpec((T, GLU_OFF), lambda i: (0, 0)), pl.BlockSpec(memory_space=pltpu.SMEM),
                  *[ANY] * len(c_in)],
        out_specs=[pl.BlockSpec((T, ATTN_WIDTH), lambda i: (0, 0)), *[ANY] * len(c_out)],
        out_shape=[_sds((T, ATTN_WIDTH), BF), *c_out], scratch_shapes=c_sems,
        compiler_params=_params(("arbitrary",)),
    )(proj, sinks, *c_in)
    return res[0], res[1:]


def _attn_bwd(proj, d_o, sinks, carry=None):
    T = proj.shape[0]
    c_in, c_out, c_sems = _carry_io(carry)

    def body(*refs):
        qkv_ref, do_ref, sinks_ref = refs[:3]
        ci_refs = refs[3:3 + len(c_in)]
        dqkv_ref, dsink_ref = refs[3 + len(c_in):5 + len(c_in)]
        co_refs = refs[5 + len(c_in):5 + len(c_in) + len(c_out)]
        dk_acc, dv_acc = refs[5 + len(c_in) + len(c_out):7 + len(c_in) + len(c_out)]
        cs_refs = refs[7 + len(c_in) + len(c_out):]
        if carry is not None:
            carry.start(ci_refs, co_refs, cs_refs)
        dsink_ref[...] = jnp.zeros_like(dsink_ref)
        dk_acc[...] = jnp.zeros_like(dk_acc)
        dv_acc[...] = jnp.zeros_like(dv_acc)

        def blk(n, carry):
            dqs = [None] * (N_Q_HEADS // 2)
            for h in range(N_KV_HEADS):
                p, psink, q4, k2, v2, rows, prev = _attn_block(qkv_ref, sinks_ref, n, h)
                dos = []
                for g in range(GROUP):
                    hq = GROUP * h + g
                    dos.append(_to_half(do_ref[rows, (hq // 2) * 128:(hq // 2 + 1) * 128].astype(F32), hq % 2, h))
                do4 = jnp.concatenate(dos, axis=0).astype(BF)
                dp = lax.dot_general(do4, v2, _DIMS["NT"], preferred_element_type=F32)
                delta = jnp.sum(p * dp, axis=-1, keepdims=True)
                ds = (p * (dp - delta) * (HEAD_DIM ** -0.5)).astype(BF)
                dsk = psink * delta
                for g in range(GROUP):
                    hq = GROUP * h + g
                    tot = -jnp.sum(dsk[g * BLOCK:(g + 1) * BLOCK], axis=0, keepdims=True)
                    lane = lax.broadcasted_iota(jnp.int32, (1, 128), 1)
                    dsink_ref[...] += jnp.where(lane == hq, tot, 0.0)
                dq = lax.dot_general(ds, k2, _DIMS["NN"], preferred_element_type=F32)
                dk = lax.dot_general(ds, q4, _DIMS["TN"], preferred_element_type=F32)
                dv = lax.dot_general(p.astype(BF), do4, _DIMS["TN"], preferred_element_type=F32)
                dk_acc[prev, :] += dk[:BLOCK]
                dk_acc[rows, :] += dk[BLOCK:]
                dv_acc[prev, :] += dv[:BLOCK]
                dv_acc[rows, :] += dv[BLOCK:]
                for g in range(GROUP):
                    hq = GROUP * h + g
                    piece = jnp.where(_lane_half((BLOCK, 128), h), dq[g * BLOCK:(g + 1) * BLOCK], 0.0)
                    if hq % 2 != h:
                        piece = pltpu.roll(piece, HEAD_DIM, 1)
                    dqs[hq // 2] = piece if dqs[hq // 2] is None else dqs[hq // 2] + piece
            for pb in range(N_Q_HEADS // 2):
                dqkv_ref[rows, pb * 128:(pb + 1) * 128] = dqs[pb].astype(BF)
            return carry

        lax.fori_loop(0, T // BLOCK, blk, 0)
        dqkv_ref[:, ATTN_WIDTH:ATTN_WIDTH + KV_WIDTH] = dk_acc[...].astype(BF)
        dqkv_ref[:, ATTN_WIDTH + KV_WIDTH:] = dv_acc[...].astype(BF)
        if carry is not None:
            carry.finish(ci_refs, co_refs, cs_refs)

    res = pl.pallas_call(
        body, name="attn_bwd", grid=(1,),
        in_specs=[pl.BlockSpec((T, GLU_OFF), lambda i: (0, 0)), pl.BlockSpec((T, ATTN_WIDTH), lambda i: (0, 0)),
                  pl.BlockSpec(memory_space=pltpu.SMEM), *[ANY] * len(c_in)],
        out_specs=[pl.BlockSpec((T, GLU_OFF), lambda i: (0, 0)), pl.BlockSpec((1, 128), lambda i: (0, 0)),
                   *[ANY] * len(c_out)],
        out_shape=[_sds((T, GLU_OFF), BF), _sds((1, 128), F32), *c_out],
        scratch_shapes=[pltpu.VMEM((T, KV_WIDTH), F32), pltpu.VMEM((T, KV_WIDTH), F32), *c_sems],
        compiler_params=_params(("arbitrary",)),
    )(proj, d_o, sinks, *c_in)
    return res[:2], res[2:]


CHUNK = 256
SUB = 32
WIN = CHUNK + 32
PAD_ROWS = SEQ + 2 * CONV_PAD
_GLU_SPECS = [pl.BlockSpec((SEQ, 256), functools.partial(lambda i, c: (0, c), c=GLU_OFF // 256 + c)) for c in range(4)]


def _glu_to_pad(a0, a1, b0, b1, zpad):
    C = CONV_CHANNELS
    zpad[0:CONV_PAD, :] = jnp.zeros((CONV_PAD, C), F32)
    zpad[CONV_PAD + SEQ:, :] = jnp.zeros((CONV_PAD, C), F32)
    zpad[CONV_PAD:CONV_PAD + SEQ, 0:256] = a0[...] * jax.nn.sigmoid(b0[...])
    zpad[CONV_PAD:CONV_PAD + SEQ, 256:C] = a1[...] * jax.nn.sigmoid(b1[...])


def _tap_windows(src, base, win):
    for b in range(8):
        win[b, 0:WIN - 8, :] = src[base + b:base + b + WIN - 8, :]


def _taps(win, w_ref, init, out, flip):
    def sub(si, carry):
        r0 = pl.multiple_of(si * SUB, SUB)
        acc = jnp.broadcast_to(init, (SUB, CONV_CHANNELS))
        for k in range(CONV_WIDTH):
            wk = (CONV_WIDTH - 1 - k) if flip else k
            acc = acc + w_ref[wk:wk + 1, :] * win[k % 8, pl.ds(r0 + 8 * (k // 8), SUB), :]
        out[pl.ds(r0, SUB), :] = acc
        return carry

    lax.fori_loop(0, CHUNK // SUB, sub, 0)


def _tap_grads(win, du, dwacc):
    def sub(si, carry):
        r0 = pl.multiple_of(si * SUB, SUB)
        d = du[pl.ds(r0, SUB), :]
        for k in range(CONV_WIDTH):
            p = d * win[k % 8, pl.ds(r0 + 8 * (k // 8), SUB), :]
            dwacc[8 * k:8 * k + 8, :] += (p[0:8] + p[8:16]) + (p[16:24] + p[24:32])
        return carry

    lax.fori_loop(0, CHUNK // SUB, sub, 0)


def _ln_parts(u):
    mu = jnp.mean(u, axis=-1, keepdims=True)
    xc = u - mu
    rstd = lax.rsqrt(jnp.mean(xc * xc, axis=-1, keepdims=True) + EPS)
    return xc * rstd, rstd


def _conv_fwd(proj, conv_w, conv_b, ln_g, ln_b, carry=None):
    T, C = proj.shape[0], CONV_CHANNELS
    vec = pl.BlockSpec((1, C), lambda i: (0, 0))
    c_in, c_out, c_sems = _carry_io(carry)

    def body(*refs):
        a0, a1, b0, b1, w_ref, cb_ref, g_ref, be_ref = refs[:8]
        ci_refs = refs[8:8 + len(c_in)]
        c_ref = refs[8 + len(c_in)]
        co_refs = refs[9 + len(c_in):9 + len(c_in) + len(c_out)]
        zpad, win, ubuf = refs[9 + len(c_in) + len(c_out):12 + len(c_in) + len(c_out)]
        cs_refs = refs[12 + len(c_in) + len(c_out):]
        if carry is not None:
            carry.start(ci_refs, co_refs, cs_refs)
        _glu_to_pad(a0, a1, b0, b1, zpad)
        for ci in range(T // CHUNK):
            _tap_windows(zpad, ci * CHUNK + CONV_PAD - (CONV_WIDTH - 1), win)
            _taps(win, w_ref, cb_ref[...], ubuf, False)
            xh, _ = _ln_parts(ubuf[...])
            ln = xh * g_ref[...] + be_ref[...]
            c_ref[ci * CHUNK:(ci + 1) * CHUNK, :] = (ln * jax.nn.sigmoid(ln)).astype(BF)
        if carry is not None:
            carry.finish(ci_refs, co_refs, cs_refs)

    res = pl.pallas_call(
        body, name="conv_fwd", grid=(1,),
        in_specs=[*_GLU_SPECS, pl.BlockSpec((CONV_PAD, C), lambda i: (0, 0)), vec, vec, vec, *[ANY] * len(c_in)],
        out_specs=[pl.BlockSpec((T, C), lambda i: (0, 0)), *[ANY] * len(c_out)],
        out_shape=[_sds((T, C), BF), *c_out],
        scratch_shapes=[pltpu.VMEM((PAD_ROWS, C), F32), pltpu.VMEM((8, WIN, C), F32), pltpu.VMEM((CHUNK, C), F32),
                        *c_sems],
        compiler_params=_params(("arbitrary",)),
    )(proj, proj, proj, proj, conv_w, conv_b, ln_g, ln_b, *c_in)
    return res[0], res[1:]


def _conv_bwd(proj, d_c, conv_w, conv_b, ln_g, ln_b, carry=None):
    T, C = proj.shape[0], CONV_CHANNELS
    vec = pl.BlockSpec((1, C), lambda i: (0, 0))
    wspec = pl.BlockSpec((CONV_PAD, C), lambda i: (0, 0))
    c_in, c_out, c_sems = _carry_io(carry)

    def body(*refs):
        a0, a1, b0, b1, dc_ref, w_ref, cb_ref, g_ref, be_ref = refs[:9]
        ci_refs = refs[9:9 + len(c_in)]
        o = 9 + len(c_in)
        dglu_ref, dw_ref, dcb_ref, dg_ref, dbe_ref = refs[o:o + 5]
        co_refs = refs[o + 5:o + 5 + len(c_out)]
        zpad, dupad, win, ubuf, dwacc = refs[o + 5 + len(c_out):o + 10 + len(c_out)]
        cs_refs = refs[o + 10 + len(c_out):]
        if carry is not None:
            carry.start(ci_refs, co_refs, cs_refs)
        _glu_to_pad(a0, a1, b0, b1, zpad)
        dupad[T:, :] = jnp.zeros((2 * CONV_PAD, C), F32)
        dwacc[...] = jnp.zeros_like(dwacc)
        dcb_ref[...] = jnp.zeros_like(dcb_ref)
        dg_ref[...] = jnp.zeros_like(dg_ref)
        dbe_ref[...] = jnp.zeros_like(dbe_ref)
        for ci in range(T // CHUNK):
            rows = slice(ci * CHUNK, (ci + 1) * CHUNK)
            _tap_windows(zpad, ci * CHUNK + CONV_PAD - (CONV_WIDTH - 1), win)
            _taps(win, w_ref, cb_ref[...], ubuf, False)
            xh, rstd = _ln_parts(ubuf[...])
            ln = xh * g_ref[...] + be_ref[...]
            sg = jax.nn.sigmoid(ln)
            dln = dc_ref[rows, :].astype(F32) * (sg * (1.0 + ln * (1.0 - sg)))
            dg_ref[...] += jnp.sum(dln * xh, axis=0, keepdims=True)
            dbe_ref[...] += jnp.sum(dln, axis=0, keepdims=True)
            dxh = dln * g_ref[...]
            du = rstd * (dxh - jnp.mean(dxh, axis=-1, keepdims=True)
                         - xh * jnp.mean(dxh * xh, axis=-1, keepdims=True))
            dupad[rows, :] = du
            dcb_ref[...] += jnp.sum(du, axis=0, keepdims=True)
            _tap_grads(win, dupad.at[rows, :], dwacc)
        for k in range(CONV_WIDTH):
            dw_ref[k:k + 1, :] = jnp.sum(dwacc[8 * k:8 * k + 8, :], axis=0, keepdims=True)
        dw_ref[CONV_WIDTH:, :] = jnp.zeros((CONV_PAD - CONV_WIDTH, C), F32)
        for ci in range(T // CHUNK):
            rows = slice(ci * CHUNK, (ci + 1) * CHUNK)
            _tap_windows(dupad, ci * CHUNK, win)
            _taps(win, w_ref, jnp.zeros((1, C), F32), ubuf, True)
            dz = ubuf[...]
            for half, (a, b) in enumerate(((a0, b0), (a1, b1))):
                sb = jax.nn.sigmoid(b[rows, :])
                dzh = dz[:, half * 256:(half + 1) * 256]
                dglu_ref[rows, half * 256:(half + 1) * 256] = (dzh * sb).astype(BF)
                dglu_ref[rows, C + half * 256:C + (half + 1) * 256] = (dzh * a[rows, :] * sb * (1.0 - sb)).astype(BF)
        if carry is not None:
            carry.finish(ci_refs, co_refs, cs_refs)

    res = pl.pallas_call(
        body, name="conv_bwd", grid=(1,),
        in_specs=[*_GLU_SPECS, pl.BlockSpec((T, C), lambda i: (0, 0)), wspec, vec, vec, vec, *[ANY] * len(c_in)],
        out_specs=[pl.BlockSpec((T, 2 * C), lambda i: (0, 0)), wspec, vec, vec, vec, *[ANY] * len(c_out)],
        out_shape=[_sds((T, 2 * C), BF), _sds((CONV_PAD, C), F32), _sds((1, C), F32), _sds((1, C), F32),
                   _sds((1, C), F32), *c_out],
        scratch_shapes=[pltpu.VMEM((PAD_ROWS, C), F32), pltpu.VMEM((PAD_ROWS, C), F32), pltpu.VMEM((8, WIN, C), F32),
                        pltpu.VMEM((CHUNK, C), F32), pltpu.VMEM((8 * CONV_PAD, C), F32), *c_sems],
        compiler_params=_params(("arbitrary",)),
    )(proj, proj, proj, proj, d_c, conv_w, conv_b, ln_g, ln_b, *c_in)
    return res[:5], res[5:]


_GATE_BLK = GATE_OFF // 256


def _ffn_in_swiglu(h2, wf_t, carry=None):
    T, D = h2.shape
    tm, tn = 512, D_FF // 2
    nj, ni = D_FF // tn, T // tm
    c_in, c_out, c_sems = _carry_io(carry)

    def body(*refs):
        a_ref, bg_ref, bu_ref = refs[:3]
        ci_refs = refs[3:3 + len(c_in)]
        act_ref, g_ref, u_ref = refs[3 + len(c_in):6 + len(c_in)]
        co_refs = refs[6 + len(c_in):6 + len(c_in) + len(c_out)]
        cs_refs = refs[6 + len(c_in) + len(c_out):]
        j, i = pl.program_id(0), pl.program_id(1)
        if carry is not None:
            @pl.when((j == 0) & (i == 0))
            def _():
                carry.start(ci_refs, co_refs, cs_refs)
        a = a_ref[...]
        for c0, c1 in ((0, 768), (768, tn)):
            g = lax.dot_general(a, bg_ref[c0:c1, :], _DIMS["NT"], preferred_element_type=F32)
            u = lax.dot_general(a, bu_ref[c0:c1, :], _DIMS["NT"], preferred_element_type=F32)
            act_ref[:, c0:c1] = (g * jax.nn.sigmoid(g) * u).astype(BF)
            g_ref[:, c0:c1] = g.astype(BF)
            u_ref[:, c0:c1] = u.astype(BF)
        if carry is not None:
            @pl.when((j == nj - 1) & (i == ni - 1))
            def _():
                carry.finish(ci_refs, co_refs, cs_refs)

    t = pl.BlockSpec((tm, tn), lambda j, i: (i, j))
    res = pl.pallas_call(
        body, name="ffn_in_swiglu", grid=(nj, ni),
        in_specs=[pl.BlockSpec((tm, D), lambda j, i: (i, 0)), pl.BlockSpec((tn, D), lambda j, i: (j, 0)),
                  pl.BlockSpec((tn, D), lambda j, i: (nj + j, 0)), *[ANY] * len(c_in)],
        out_specs=[t, t, t, *[ANY] * len(c_out)], out_shape=[*[_sds((T, D_FF), BF)] * 3, *c_out],
        scratch_shapes=c_sems,
        compiler_params=_params(("arbitrary", "arbitrary")),
    )(h2, wf_t, wf_t, *c_in)
    return res[:3], res[3:]


def _proj_in_dw(segs, h):
    T, D = h.shape
    tb = 256
    nblk = [seg.shape[1] // tb for seg in segs]
    starts = [sum(nblk[:q]) for q in range(len(segs))]
    n_seg = len(segs)

    def body(*refs):
        seg_refs, h_ref, o_ref, cs_ref = refs[:n_seg], refs[n_seg], refs[n_seg + 1], refs[n_seg + 2]
        i = pl.program_id(0)
        for seg_ref, st, nb in zip(seg_refs, starts, nblk):
            @pl.when((i >= st) & (i < st + nb))
            def _(seg_ref=seg_ref):
                a = seg_ref[...]
                o_ref[...] = lax.dot_general(a, h_ref[...], _DIMS["TN"], preferred_element_type=F32).astype(BF)
                cs_ref[...] = jnp.sum(a.astype(F32), axis=0, keepdims=True)

    in_specs = [pl.BlockSpec((T, tb), functools.partial(lambda i, st, nb: (0, jnp.clip(i - st, 0, nb - 1)), st=st, nb=nb))
                for st, nb in zip(starts, nblk)]
    return pl.pallas_call(
        body, name="proj_in_dw", grid=(sum(nblk),),
        in_specs=[*in_specs, pl.BlockSpec((T, D), lambda i: (0, 0))],
        out_specs=[pl.BlockSpec((tb, D), lambda i: (i, 0)), pl.BlockSpec((1, tb), lambda i: (0, i))],
        out_shape=[_sds((sum(nblk) * tb, D), BF), _sds((1, sum(nblk) * tb), F32)],
        compiler_params=_params(("arbitrary",)),
    )(*segs, h)


def _local_step(x, target, small, wi_t, conv_w, plan):
    T, D = x.shape
    tm = 1024

    def carried(call, res, carry):
        if carry is None:
            return res
        outs, got = res
        plan.done(call, got)
        return outs

    h, r1 = _rms_fwd("rms_mix", x, small["g_mix_norm"])

    def ep_add(acc, ex, outs, ids, scr):
        outs[0][...] = acc + ex[0][...]

    tn_in = IN_WIDTH // 3
    carry = plan.carry("proj_in")
    proj, = carried("proj_in", _matmul("proj_in", [h], wi_t, "NT", m=T, n=IN_WIDTH, tm=tm, tn=tn_in, epilogue=ep_add,
                                       extra=[(small["b_in"], _row(tn_in))],
                                       outs=[(_sds((T, IN_WIDTH), F32), _tile(tm, tn_in))], carry=carry), carry)
    plan.launch("gather_ffn", after=proj)
    o, got = _attn_fwd(proj, small["sinks"], carry=plan.carry("attn_fwd"))
    plan.done("attn_fwd", got)
    c, got = _conv_fwd(proj, conv_w, small["conv_b"], small["ln_g"], small["ln_b"], carry=plan.carry("conv_fwd"))
    plan.done("conv_fwd", got)
    wap_t, wcp_t, w_out = plan.weight("w_attn_proj"), plan.weight("w_conv_proj"), plan.weight("w_out")
    ya, = _matmul("attn_proj", [o], wap_t, "NT", m=T, n=D, tm=tm, tn=D, epilogue=_store(F32),
                  outs=[(_sds((T, D), F32), _tile(tm, D))])

    tg = 256
    gate_specs = [pl.BlockSpec((tm, tg), lambda j, i, k: (i, _GATE_BLK + j)),
                  pl.BlockSpec((tm, tg), lambda j, i, k: (i, _GATE_BLK + D // tg + j))]

    def ep_merge(acc, ex, outs, ids, scr):
        yc = acc + ex[0][...]
        outs[0][...] = yc
        outs[1][...] = (jax.nn.sigmoid(ex[2][...]) * ex[1][...] + jax.nn.sigmoid(ex[3][...]) * yc).astype(BF)

    carry = plan.carry("conv_proj_merge")
    yc, merged = carried("conv_proj_merge", _matmul(
        "conv_proj_merge", [c], wcp_t, "NT", m=T, n=D, tm=tm, tn=tg, epilogue=ep_merge,
        extra=[(small["b_conv_proj"], _row(tg)), (ya, _tile(tm, tg)), (proj, gate_specs[0]), (proj, gate_specs[1])],
        outs=[(_sds((T, D), F32), _tile(tm, tg)), (_sds((T, D), BF), _tile(tm, tg))], carry=carry), carry)
    def ep_residual_rms(acc, ex, outs, ids, scr):
        x2v = acc + ex[0][...]
        r = lax.rsqrt(jnp.mean(x2v * x2v, axis=-1, keepdims=True) + EPS)
        outs[0][...] = x2v
        outs[1][...] = (x2v * r * ex[1][...]).astype(BF)
        outs[2][...] = r

    carry = plan.carry("out_proj")
    x2, h2, r2 = carried("out_proj", _matmul(
        "out_proj_rms", [merged], w_out, "NN", m=T, n=D, tm=512, tn=D, epilogue=ep_residual_rms,
        extra=[(x, _tile(512, D)), (small["g_ffn_norm"], _row(D))],
        outs=[(_sds((T, D), F32), _tile(512, D)), (_sds((T, D), BF), _tile(512, D)),
              (_sds((T, 1), F32), pl.BlockSpec((512, 1), lambda j, i, k: (i, 0)))], carry=carry), carry)
    plan.launch("gather_down", after=x2)
    wf_t = plan.weight("w_ffn_in")
    (act, gate, up), got = _ffn_in_swiglu(h2, wf_t, carry=plan.carry("ffn_in_swiglu"))
    plan.done("ffn_in_swiglu", got)
    w_down = plan.weight("w_ffn_down")
    def ep_residual_loss(acc, ex, outs, ids, scr):
        dx, dg, part = _loss_head(acc + ex[0][...], ex[1][...], ex[2][...])
        outs[0][...] = dx
        outs[1][...] = dx.astype(BF)
        _accumulate_rows(outs[2], dg, ids[1] == 0)
        _accumulate_rows(outs[3], part, ids[1] == 0)

    dx3, dx3_b, dg_final, loss = _matmul(
        "ffn_down_loss", [act], w_down, "NN", m=T, n=D, tm=512, tn=D, epilogue=ep_residual_loss,
        extra=[(x2, _tile(512, D)), (small["g_final"], _row(D)), (target, _tile(512, D))],
        outs=[(_sds((T, D), F32), _tile(512, D)), (_sds((T, D), BF), _tile(512, D)), (_sds((1, D), F32), _row(D)),
              (_sds((1, 1), F32), pl.BlockSpec((1, 1), lambda j, i, k: (0, 0)))])

    tn_ff = D_FF // 2

    def ep_swiglu_bwd(acc, ex, outs, ids, scr):
        g, u = ex[0][...].astype(F32), ex[1][...].astype(F32)
        sg = jax.nn.sigmoid(g)
        outs[0][...] = (acc * u * sg * (1.0 + g * (1.0 - sg))).astype(BF)
        outs[1][...] = (acc * g * sg).astype(BF)

    dgate, dup = _matmul(
        "ffn_down_bwd", [dx3_b], w_down, "NT", m=T, n=D_FF, tm=512, tn=tn_ff, epilogue=ep_swiglu_bwd,
        extra=[(gate, _tile(512, tn_ff)), (up, _tile(512, tn_ff))],
        outs=[(_sds((T, D_FF), BF), _tile(512, tn_ff)), (_sds((T, D_FF), BF), _tile(512, tn_ff))])

    def dw(name, a, b, rows, cols, row_off=0, alias=None, total_rows=None, colsum=False):
        total_rows = rows if total_rows is None else total_rows
        tmw = rows if rows <= 1024 else D_FF // 2
        blk, rem = divmod(row_off, tmw)
        assert rem == 0

        def ep(acc, ex, outs, ids, scr):
            outs[0][...] = acc.astype(BF)
            if colsum:
                outs[1][...] = jnp.sum(ex[0][...].astype(F32), axis=0, keepdims=True)

        outs = [(_sds((total_rows, cols), BF), pl.BlockSpec((tmw, cols), lambda j, i, k: (blk + i, j)))]
        extra = []
        if colsum:
            extra = [(a, pl.BlockSpec((T, tmw), lambda j, i, k: (0, i)))]
            outs.append((_sds((1, rows), F32), pl.BlockSpec((1, tmw), lambda j, i, k: (0, i))))
        carry = plan.carry(name)
        res = carried(name, _matmul(name, [a], b, "TN", m=rows, n=cols, tm=tmw, tn=cols, epilogue=ep, extra=extra,
                                    outs=outs, alias=None if alias is None else (alias, 0), carry=carry), carry)
        return res if colsum else res[0]

    plan.grad_ready(dict(w_ffn_down=dw("ffn_down_dw", act, dx3_b, D_FF, D)))

    def ep_rms_bwd(acc, ex, outs, ids, scr):
        dx, dg = _rms_bwd(acc, ex[0][...], ex[1][...], ex[2][...])
        dx = ex[3][...] + dx
        outs[0][...] = dx
        outs[1][...] = dx.astype(BF)
        _accumulate_rows(outs[2], dg, ids[1] == 0)

    def rms_bwd_io(tm_, xin, r, g, dres):
        return dict(
            extra=[(xin, _tile(tm_, D)), (r, pl.BlockSpec((tm_, 1), lambda j, i, k: (i, 0))), (g, _row(D)),
                   (dres, _tile(tm_, D))],
            outs=[(_sds((T, D), F32), _tile(tm_, D)), (_sds((T, D), BF), _tile(tm_, D)), (_sds((1, D), F32), _row(D))])

    carry = plan.carry("ffn_in_bwd")
    dx2, dx2_b, dg_ffn = carried(
        "ffn_in_bwd",
        _matmul("ffn_in_bwd", [dgate, dup], wf_t, "NN", m=T, n=D, tm=512, tn=D, tk=D_FF, epilogue=ep_rms_bwd,
                carry=carry, **rms_bwd_io(512, x2, r2, small["g_ffn_norm"], dx3)), carry)
    plan.launch("send_down")
    gwf_t = dw("ffn_in_dw_gate", dgate, h2, D_FF, D, total_rows=2 * D_FF)
    gwf_t = dw("ffn_in_dw_up", dup, h2, D_FF, D, row_off=D_FF, alias=gwf_t, total_rows=2 * D_FF)
    plan.grad_ready(dict(w_ffn_in=gwf_t))

    def ep_merge_bwd(acc, ex, outs, ids, scr):
        s0 = jax.nn.sigmoid(ex[2][...])
        s1 = jax.nn.sigmoid(ex[3][...])
        outs[0][...] = (acc * s0).astype(BF)
        outs[1][...] = (acc * s1).astype(BF)
        outs[2][...] = (acc * ex[0][...] * s0 * (1.0 - s0)).astype(BF)
        outs[3][...] = (acc * ex[1][...] * s1 * (1.0 - s1)).astype(BF)

    carry = plan.carry("out_proj_bwd_merge")
    dya, dyc, dg0, dg1 = carried(
        "out_proj_bwd_merge",
        _matmul("out_proj_bwd_merge", [dx2_b], w_out, "NT", m=T, n=D, tm=tm, tn=tg, epilogue=ep_merge_bwd,
                extra=[(ya, _tile(tm, tg)), (yc, _tile(tm, tg)), (proj, gate_specs[0]), (proj, gate_specs[1])],
                outs=[(_sds((T, D), BF), _tile(tm, tg))] * 4, carry=carry), carry)
    plan.launch("send_ffn")
    gw_out = dw("out_proj_dw", merged, dx2_b, D, D)
    d_o, = _matmul("attn_proj_bwd", [dya], wap_t, "NN", m=T, n=ATTN_WIDTH, tm=tm, tn=ATTN_WIDTH,
                   epilogue=_store(BF), outs=[(_sds((T, ATTN_WIDTH), BF), _tile(tm, ATTN_WIDTH))])
    d_c, = _matmul("conv_proj_bwd", [dyc], wcp_t, "NN", m=T, n=CONV_CHANNELS, tm=tm, tn=CONV_CHANNELS,
                   epilogue=_store(BF), outs=[(_sds((T, CONV_CHANNELS), BF), _tile(tm, CONV_CHANNELS))])
    gwap_t = dw("attn_proj_dw", dya, o, D, ATTN_WIDTH)
    gwcp_t, db_cp = dw("conv_proj_dw", dyc, c, D, CONV_CHANNELS, colsum=True)
    plan.grad_ready(dict(w_out=gw_out, w_attn_proj=gwap_t, w_conv_proj=gwcp_t))
    (dglu, dcw, dcb, dlng, dlnb), got = _conv_bwd(proj, d_c, conv_w, small["conv_b"], small["ln_g"], small["ln_b"],
                                                  carry=plan.carry("conv_bwd"))
    plan.done("conv_bwd", got)
    plan.launch("send_mix")
    (dqkv, dsinks), got = _attn_bwd(proj, d_o, small["sinks"], carry=plan.carry("attn_bwd"))
    plan.done("attn_bwd", got)

    segs = [dqkv, dglu, dg0, dg1]
    gwi_t, db_in = _proj_in_dw(segs, h)
    plan.grad_ready(dict(w_in=gwi_t))
    plan.alone("swap_inp")
    plan.launch("send_inp")
    carry = plan.carry("proj_in_bwd")
    dx, _, dg_mix = carried(
        "proj_in_bwd",
        _matmul("proj_in_bwd", segs, wi_t, "NN", m=T, n=D, tm=512, tn=D, epilogue=ep_rms_bwd, carry=carry,
                **rms_bwd_io(512, x, r1, small["g_mix_norm"], dx2)), carry)

    parts = dict(g_mix_norm=dg_mix, b_in=db_in, sinks=dsinks, conv_w=dcw, conv_b=dcb, ln_g=dlng, ln_b=dlnb,
                 b_conv_proj=db_cp, g_ffn_norm=dg_ffn, g_final=dg_final, loss=loss)
    return dx, parts


def _place():
    x, y, c = lax.axis_index("x"), lax.axis_index("y"), lax.axis_index("c")
    return x, y, c, [(1 - x, y), (x, 1 - y), (1 - x, 1 - y)]


def _gather_copies(x_refs, out_refs, rows_per, send_sems, recv_sems, local_sems):
    x, y, c, chips = _place()
    me, sibling = (x, y, c), (x, y, 1 - c)

    def rows(a, px, py, pc):
        return out_refs[a].at[pl.ds((4 * px + 2 * py + pc) * rows_per[a], rows_per[a])]

    def copy(a, k, block, to, src=None):
        return pltpu.make_async_remote_copy(
            src_ref=rows(a, *block) if src is None else src, dst_ref=rows(a, *block),
            send_sem=send_sems.at[7 * a + k], recv_sem=recv_sems.at[7 * a + k], device_id=to, device_id_type=MESH)

    def local(a):
        return pltpu.make_async_copy(x_refs[a], rows(a, *me), local_sems.at[a])

    def first(a):
        return [copy(a, 0, me, sibling, src=x_refs[a])] + [copy(a, 1 + j, me, (*chip, c), src=x_refs[a])
                                                          for j, chip in enumerate(chips)]

    def arrive(a, j):
        return copy(a, 1 + j, (*chips[j], c), me)

    def passed(a, j):
        return copy(a, 4 + j, (*chips[j], c), sibling)

    def from_sibling(a):
        return [copy(a, 0, sibling, me)] + [copy(a, 4 + j, (*chip, 1 - c), me) for j, chip in enumerate(chips)]

    return len(x_refs), local, first, arrive, passed, from_sibling


def _gather_start(*refs):
    n, local, first, _, _, _ = _gather_copies(*refs)
    for a in range(n):
        local(a).start()
        for cp in first(a):
            cp.start()


def _gather_finish(*refs):
    n, local, first, arrive, passed, from_sibling = _gather_copies(*refs)
    for a in range(n):
        for j in range(3):
            arrive(a, j).wait_recv()
            passed(a, j).start()
    for a in range(n):
        for cp in from_sibling(a):
            cp.wait_recv()
    for a in range(n):
        for cp in first(a) + [passed(a, j) for j in range(3)]:
            cp.wait_send()
        local(a).wait()


def _gather_peers():
    x, y, c, chips = _place()
    return [(x, y, 1 - c)] + [(*chip, c) for chip in chips]


def _gather_sems(n):
    return [pltpu.SemaphoreType.DMA((7 * n,)), pltpu.SemaphoreType.DMA((7 * n,)), pltpu.SemaphoreType.DMA((n,))]


def _gather_carry(shards):
    rows_per = [s.shape[0] for s in shards]
    return _Carry(shards, [_sds((N_DEV * s.shape[0],) + s.shape[1:], s.dtype) for s in shards],
                  _gather_sems(len(shards)),
                  lambda ins, outs, sems: _gather_start(ins, outs, rows_per, *sems),
                  lambda ins, outs, sems: _gather_finish(ins, outs, rows_per, *sems), _gather_peers)


def _swap_carry(grads):
    n = len(grads)

    def copies(g_refs, out_refs, sems):
        send_sems, recv_sems = sems
        x, y, c, _ = _place()
        return [pltpu.make_async_remote_copy(
            src_ref=g_refs[a].at[2 * p + 1 - c], dst_ref=out_refs[a].at[p],
            send_sem=send_sems.at[4 * a + p], recv_sem=recv_sems.at[4 * a + p],
            device_id=(x, y, 1 - c), device_id_type=MESH) for a in range(n) for p in range(4)]

    def start(ins, outs, sems):
        for cp in copies(ins, outs, sems):
            cp.start()

    def finish(ins, outs, sems):
        for cp in copies(ins, outs, sems):
            cp.wait()

    def peers():
        x, y, c, _ = _place()
        return [(x, y, 1 - c)]

    return _Carry(grads, [_sds((4,) + g.shape[1:], g.dtype) for g in grads],
                  [pltpu.SemaphoreType.DMA((4 * n,)), pltpu.SemaphoreType.DMA((4 * n,))], start, finish, peers)


def _join(carries):
    carries = [c for c in carries if c is not None]
    if not carries:
        return None
    n_in = [len(c.arrays) for c in carries]
    n_out = [len(c.out_shapes) for c in carries]
    n_sem = [len(c.sems) for c in carries]

    def parts(refs, counts):
        cuts = [sum(counts[:q]) for q in range(len(counts) + 1)]
        return [refs[cuts[q]:cuts[q + 1]] for q in range(len(counts))]

    def start(ins, outs, sems):
        for c, i, o, s in zip(carries, parts(ins, n_in), parts(outs, n_out), parts(sems, n_sem)):
            c.start(i, o, s)

    def finish(ins, outs, sems):
        for c, i, o, s in zip(carries, parts(ins, n_in), parts(outs, n_out), parts(sems, n_sem)):
            c.finish(i, o, s)

    return _Carry([a for c in carries for a in c.arrays], [o for c in carries for o in c.out_shapes],
                  [s for c in carries for s in c.sems], start, finish)


def _run_carry(name, carry):
    n_in, n_out = len(carry.arrays), len(carry.out_shapes)

    def body(*refs):
        carry.start(refs[:n_in], refs[n_in:n_in + n_out], refs[n_in + n_out:])
        carry.finish(refs[:n_in], refs[n_in:n_in + n_out], refs[n_in + n_out:])

    return pl.pallas_call(body, name=name, in_specs=[ANY] * n_in, out_specs=[ANY] * n_out,
                          out_shape=carry.out_shapes, scratch_shapes=carry.sems)(*carry.arrays)


def _run_carry_async(name, carry, collective_id):
    ins = [jax.new_ref(a, memory_space=pltpu.MemorySpace.HBM) for a in carry.arrays]
    outs = [jax.empty_ref(o, memory_space=pltpu.MemorySpace.HBM) for o in carry.out_shapes]

    @pl.kernel(mesh=plsc.ScalarSubcoreMesh(axis_name="sequencer", num_cores=1), name=name,
               scratch_types=tuple(carry.sems), compiler_params=pltpu.CompilerParams(collective_id=collective_id))
    def launch(*sems):
        barrier = pltpu.get_barrier_semaphore()
        peers = carry.peers()
        for peer in peers:
            pl.semaphore_signal(barrier, inc=1, device_id=peer, device_id_type=MESH)
        pl.semaphore_wait(barrier, len(peers))
        carry.start(ins, outs, sems)
        carry.finish(ins, outs, sems)

    launch()
    return [o[...] for o in outs]


def _chip_sum(name, g, got, c):
    _, rows, cols = g.shape

    def body(c_ref, g_ref, got_ref, o_ref):
        o_ref[...] = (g_ref[...].astype(F32) + got_ref[...].astype(F32)).astype(BF)

    return pl.pallas_call(
        body, name=name,
        grid_spec=pltpu.PrefetchScalarGridSpec(
            num_scalar_prefetch=1, grid=(4,),
            in_specs=[pl.BlockSpec((1, rows, cols), lambda p, c_ref: (2 * p + c_ref[0], 0, 0)),
                      pl.BlockSpec((1, rows, cols), lambda p, c_ref: (p, 0, 0))],
            out_specs=pl.BlockSpec((1, rows, cols), lambda p, c_ref: (p, 0, 0))),
        out_shape=_sds((4, rows, cols), BF),
        compiler_params=_params(("arbitrary",)),
    )(c, g, got)


def _send_carry(sums, ks):
    n, nk = len(sums), len(ks)

    def copies(s_refs, out_refs, sems):
        send_sems, recv_sems = sems
        x, y, c, chips = _place()
        return [pltpu.make_async_remote_copy(
            src_ref=s_refs[a].at[2 * chips[k][0] + chips[k][1]], dst_ref=out_refs[a].at[q],
            send_sem=send_sems.at[nk * a + q], recv_sem=recv_sems.at[nk * a + q],
            device_id=(*chips[k], c), device_id_type=MESH) for a in range(n) for q, k in enumerate(ks)]

    def start(ins, outs, sems):
        for cp in copies(ins, outs, sems):
            cp.start()

    def finish(ins, outs, sems):
        for cp in copies(ins, outs, sems):
            cp.wait()

    def peers():
        x, y, c, chips = _place()
        return [(*chips[k], c) for k in ks]

    return _Carry(sums, [_sds((nk,) + s.shape[1:], s.dtype) for s in sums],
                  [pltpu.SemaphoreType.DMA((nk * n,)), pltpu.SemaphoreType.DMA((nk * n,))], start, finish, peers)


def _grad_total(name, g, got, got3, ids):
    _, rows, cols = g.shape
    n3 = len(got3)

    def body(ids_ref, g_ref, got_ref, *rest):
        o_ref = rest[n3]
        tot = g_ref[0].astype(F32) + got_ref[0].astype(F32)
        for r_ref in rest[:n3]:
            for q in range(r_ref.shape[0]):
                tot = tot + r_ref[q].astype(F32)
        o_ref[...] = tot

    return pl.pallas_call(
        body, name=name,
        grid_spec=pltpu.PrefetchScalarGridSpec(
            num_scalar_prefetch=1, grid=(1,),
            in_specs=[pl.BlockSpec((1, rows, cols), lambda i, ids_ref: (ids_ref[0], 0, 0)),
                      pl.BlockSpec((1, rows, cols), lambda i, ids_ref: (ids_ref[1], 0, 0)),
                      *[pl.BlockSpec(r.shape, lambda i, ids_ref: (0, 0, 0)) for r in got3]],
            out_specs=pl.BlockSpec((rows, cols), lambda i, ids_ref: (0, 0))),
        out_shape=_sds((rows, cols), F32),
        compiler_params=_params(("arbitrary",)),
    )(ids, g, got, *got3)


def _adam_math(w, g, m, v):
    m = ADAM_B1 * m + (1.0 - ADAM_B1) * g
    v = ADAM_B2 * v + (1.0 - ADAM_B2) * (g * g)
    m_hat = m / (1.0 - ADAM_B1 ** ADAM_STEP)
    v_hat = v / (1.0 - ADAM_B2 ** ADAM_STEP)
    delta = -ADAM_LR * (m_hat / (jnp.sqrt(v_hat) + ADAM_EPS) + ADAM_WD * w)
    return delta, m, v


def _adamw(name, w, g, m, v):
    rows, cols = w.shape
    tr = 256 if rows % 256 == 0 else rows

    def body(w_ref, g_ref, m_ref, v_ref, d_ref, nm_ref, nv_ref):
        d_ref[...], nm_ref[...], nv_ref[...] = _adam_math(w_ref[...], g_ref[...], m_ref[...], v_ref[...])

    t = pl.BlockSpec((tr, cols), lambda i: (i, 0))
    return pl.pallas_call(
        body, name=name, grid=(rows // tr,), in_specs=[t] * 4, out_specs=[t] * 3,
        out_shape=[_sds((rows, cols), F32)] * 3, compiler_params=_params(("arbitrary",)),
    )(w, g, m, v)


SMALL_NAMES = ["g_mix_norm", "b_in", "sinks", "conv_b", "ln_g", "ln_b", "b_conv_proj", "g_ffn_norm", "g_final"]
_PACK_ROWS = 32


def _small_pack(parts):
    C = CONV_CHANNELS
    part_list = [parts["g_mix_norm"], parts["b_in"], parts["sinks"], parts["conv_b"], parts["ln_g"], parts["ln_b"],
                 parts["b_conv_proj"], parts["g_ffn_norm"], parts["g_final"], parts["loss"], parts["conv_w"]]

    def body(p_mix, p_b, p_sink, p_cb, p_lg, p_lb, p_bcp, p_ffn, p_fin, p_loss, p_cw, pack):
        pack[...] = jnp.zeros_like(pack)
        pack[0:1, :] = p_mix[...]
        pack[1:2, 0:GLU_OFF] = p_b[:, 0:GLU_OFF]
        pack[2:3, :] = p_b[:, GLU_OFF:GATE_OFF]
        pack[3:4, :] = p_b[:, GATE_OFF:GATE_OFF + D_MODEL]
        pack[4:5, :] = p_b[:, GATE_OFF + D_MODEL:]
        pack[5:6, 0:128] = p_sink[...]
        pack[6:7, 0:C] = p_cb[...]
        pack[6:7, C:2 * C] = p_lg[...]
        pack[7:8, 0:C] = p_lb[...]
        pack[8:9, :] = p_bcp[...]
        pack[9:10, :] = p_ffn[...]
        pack[10:11, :] = p_fin[...]
        pack[11:12, 0:128] = jnp.broadcast_to(p_loss[...], (1, 128))
        pack[12:28, 0:C] = p_cw[0:16, :]
        pack[12:28, C:2 * C] = p_cw[16:32, :]

    vm = pl.BlockSpec(memory_space=pltpu.VMEM)
    return pl.pallas_call(body, name="small_pack", in_specs=[vm] * len(part_list), out_specs=vm,
                          out_shape=_sds((_PACK_ROWS, D_MODEL), F32))(*part_list)


def _small_adamw(gathered, small_w, small_m, small_v):
    C = CONV_CHANNELS
    names = SMALL_NAMES
    widths = [small_w[k].shape[1] for k in names]
    n_small = len(names)

    def body(*refs):
        tot_ref = refs[0]
        w_refs = refs[1:1 + n_small]
        m_refs = refs[1 + n_small:1 + 2 * n_small]
        v_refs = refs[1 + 2 * n_small:1 + 3 * n_small]
        o = 1 + 3 * n_small
        loss_ref, cw_ref = refs[o], refs[o + 1]
        out_refs = refs[o + 2:o + 2 + 4 * n_small]
        tot = tot_ref[0:_PACK_ROWS, :]
        for d in range(1, N_DEV):
            tot = tot + tot_ref[d * _PACK_ROWS:(d + 1) * _PACK_ROWS, :]
        loss_ref[...] = tot[11:12, 0:1]
        cw_ref[0:16, :] = tot[12:28, 0:C]
        cw_ref[16:32, :] = tot[12:28, C:2 * C]
        grads = dict(
            g_mix_norm=tot[0:1, :],
            b_in=jnp.concatenate([tot[1:2, 0:GLU_OFF], tot[2:3, :], tot[3:4, :], tot[4:5, :]], axis=1),
            sinks=tot[5:6, 0:N_Q_HEADS], conv_b=tot[6:7, 0:C], ln_g=tot[6:7, C:2 * C], ln_b=tot[7:8, 0:C],
            b_conv_proj=tot[8:9, :], g_ffn_norm=tot[9:10, :], g_final=tot[10:11, :])
        for s, k in enumerate(names):
            g = grads[k]
            d, nm, nv = _adam_math(w_refs[s][...], g, m_refs[s][...], v_refs[s][...])
            out_refs[4 * s][...] = g
            out_refs[4 * s + 1][...] = d
            out_refs[4 * s + 2][...] = nm
            out_refs[4 * s + 3][...] = nv

    vm = pl.BlockSpec(memory_space=pltpu.VMEM)
    args = [gathered, *[small_w[k] for k in names], *[small_m[k] for k in names], *[small_v[k] for k in names]]
    out_shape = [_sds((1, 1), F32), _sds((CONV_PAD, C), F32)]
    for wd in widths:
        out_shape += [_sds((1, wd), F32)] * 4
    res = pl.pallas_call(
        body, name="small_adamw",
        in_specs=[vm] * len(args), out_specs=[vm] * len(out_shape), out_shape=out_shape,
        compiler_params=pltpu.CompilerParams(vmem_limit_bytes=VMEM_LIMIT_BYTES),
    )(*args)
    return res[0], res[1], {k: res[2 + 4 * s:6 + 4 * s] for s, k in enumerate(names)}


BIG = dict(w_in=True, w_attn_proj=True, w_conv_proj=True, w_out=False, w_ffn_in=True, w_ffn_down=False)
WEIGHT_NAMES = ["g_mix_norm", "w_in", "b_in", "sinks", "conv_w", "conv_b", "ln_g", "ln_b", "w_attn_proj",
                "w_conv_proj", "b_conv_proj", "w_out", "g_ffn_norm", "w_ffn_in", "w_ffn_down", "g_final"]


class _Plan:
    GROUPS = dict(down=["w_ffn_down"], ffn=["w_ffn_in"], mix=["w_out", "w_attn_proj", "w_conv_proj"], inp=["w_in"])
    ALL = (0, 1, 2)
    RIDES = dict(
        gather_mix=[("gather", ["w_attn_proj", "w_conv_proj", "w_out"])], gather_ffn=[("gather", ["w_ffn_in"])],
        gather_down=[("gather", ["w_ffn_down"])],
        ffn_in_bwd=[("swap", "down")], send_down=[("send", "down", ALL)],
        out_proj_bwd_merge=[("swap", "ffn")], send_ffn=[("send", "ffn", ALL)],
        conv_bwd=[("swap", "mix")], send_mix=[("send", "mix", ALL)],
        swap_inp=[("swap", "inp")], send_inp=[("send", "inp", ALL)])
    ASYNC = dict(gather_mix=1, gather_ffn=2, gather_down=3, send_down=4, send_ffn=5, send_mix=6, send_inp=7)

    def __init__(self, shards, c1):
        self.shards, self.c1 = shards, c1
        self.full, self.slots, self.got, self.sums, self.got3 = {}, {}, {}, {}, {}

    def weight(self, name):
        return self.full[name]

    def grad_ready(self, grads):
        for k, g in grads.items():
            self.slots[k] = g.reshape(N_DEV, g.shape[0] // N_DEV, g.shape[1])

    def _one(self, kind, what, ks=None):
        if kind == "gather":
            return _gather_carry([self.shards[k] for k in what])
        names = self.GROUPS[what]
        if kind == "swap":
            return _swap_carry([self.slots[k] for k in names])
        return _send_carry([self.sums[k] for k in names], ks)

    def carry(self, call):
        return _join([self._one(*ride) for ride in self.RIDES.get(call, [])])

    def done(self, call, outs):
        outs = list(outs)
        for kind, what, *_ in self.RIDES.get(call, []):
            names = what if kind == "gather" else self.GROUPS[what]
            mine, outs = outs[:len(names)], outs[len(names):]
            if kind == "gather":
                self.full.update(zip(names, mine))
            elif kind == "send":
                for k, r in zip(names, mine):
                    self.got3.setdefault(k, []).append(r)
            else:
                for k, r in zip(names, mine):
                    self.got[k] = r
                    self.sums[k] = _chip_sum(f"chip_sum_{k}", self.slots[k], r, self.c1)

    def alone(self, call):
        self.done(call, _run_carry(call, self.carry(call)))

    def launch(self, call, after=None):
        carry = self._one(*self.RIDES[call][0])
        if after is not None:
            carry.arrays = list(lax.optimization_barrier((tuple(carry.arrays), after))[0])
        self.done(call, _run_carry_async(call, carry, self.ASYNC[call]))


def kernel(x, g_mix_norm, w_in, b_in, sinks, conv_w, conv_b, ln_g, ln_b, w_attn_proj, w_conv_proj, b_conv_proj, w_out, g_ffn_norm, w_ffn_in, w_ffn_down, g_final, loss_target, m_g_mix_norm, m_w_in, m_b_in, m_sinks, m_conv_w, m_conv_b, m_ln_g, m_ln_b, m_w_attn_proj, m_w_conv_proj, m_b_conv_proj, m_w_out, m_g_ffn_norm, m_w_ffn_in, m_w_ffn_down, m_g_final, v_g_mix_norm, v_w_in, v_b_in, v_sinks, v_conv_w, v_conv_b, v_ln_g, v_ln_b, v_w_attn_proj, v_w_conv_proj, v_b_conv_proj, v_w_out, v_g_ffn_norm, v_w_ffn_in, v_w_ffn_down, v_g_final):
    w = dict(g_mix_norm=g_mix_norm, w_in=w_in, b_in=b_in, sinks=sinks, conv_w=conv_w, conv_b=conv_b, ln_g=ln_g,
             ln_b=ln_b, w_attn_proj=w_attn_proj, w_conv_proj=w_conv_proj, b_conv_proj=b_conv_proj, w_out=w_out,
             g_ffn_norm=g_ffn_norm, w_ffn_in=w_ffn_in, w_ffn_down=w_ffn_down, g_final=g_final)
    m = dict(g_mix_norm=m_g_mix_norm, w_in=m_w_in, b_in=m_b_in, sinks=m_sinks, conv_w=m_conv_w, conv_b=m_conv_b,
             ln_g=m_ln_g, ln_b=m_ln_b, w_attn_proj=m_w_attn_proj, w_conv_proj=m_w_conv_proj,
             b_conv_proj=m_b_conv_proj, w_out=m_w_out, g_ffn_norm=m_g_ffn_norm, w_ffn_in=m_w_ffn_in,
             w_ffn_down=m_w_ffn_down, g_final=m_g_final)
    v = dict(g_mix_norm=v_g_mix_norm, w_in=v_w_in, b_in=v_b_in, sinks=v_sinks, conv_w=v_conv_w, conv_b=v_conv_b,
             ln_g=v_ln_g, ln_b=v_ln_b, w_attn_proj=v_w_attn_proj, w_conv_proj=v_w_conv_proj,
             b_conv_proj=v_b_conv_proj, w_out=v_w_out, g_ffn_norm=v_g_ffn_norm, w_ffn_in=v_w_ffn_in,
             w_ffn_down=v_w_ffn_down, g_final=v_g_final)
    ax, ay, ac = lax.axis_index("x"), lax.axis_index("y"), lax.axis_index("c")
    me = 4 * ax + 2 * ay + ac
    chip = 2 * ax + ay

    shards = {k: (w[k][0].T if tr else w[k][0]).astype(BF) for k, tr in BIG.items()}
    cw_shard = jnp.pad(conv_w[0].T, ((0, 0), (0, 1))).reshape(16, 128)
    wi_t, cw_full = _run_carry("weights_all_gather", _gather_carry([shards["w_in"], cw_shard]))
    conv_full = cw_full.reshape(CONV_CHANNELS, CONV_PAD).T

    as_row = lambda a: a.reshape(1, -1)
    small_w = {k: as_row(w[k]) for k in SMALL_NAMES}
    small_m = {k: as_row(m[k]) for k in SMALL_NAMES}
    small_v = {k: as_row(v[k]) for k in SMALL_NAMES}
    plan = _Plan(shards, ac.reshape(1).astype(jnp.int32))
    plan.launch("gather_mix", after=wi_t)
    dx, parts = _local_step(x[0], loss_target[0], small_w, wi_t, conv_full, plan)

    small_gathered, = _run_carry_async("small_gather", _gather_carry([_small_pack(parts)]), 8)

    ids = jnp.stack([me, chip]).astype(jnp.int32)
    grads, delta, new_m, new_v = {}, {}, {}, {}
    for k in sorted(BIG, key=lambda k: k == "w_in"):
        tot = _grad_total(f"grad_total_{k}", plan.slots[k], plan.got[k], plan.got3[k], ids)
        tot = tot.T if BIG[k] else tot
        d, nm, nv = _adamw(f"adamw_{k}", w[k][0], tot, m[k][0], v[k][0])
        grads[k], delta[k], new_m[k], new_v[k] = tot[None], d[None], nm[None], nv[None]

    loss, cw_grad, small_out = _small_adamw(small_gathered, small_w, small_m, small_v)
    for k in SMALL_NAMES:
        g, d, nm, nv = (a.reshape(w[k].shape) for a in small_out[k])
        grads[k], delta[k], new_m[k], new_v[k] = g, d, nm, nv
    cw_mine = lax.dynamic_slice(cw_grad, (0, me * 64), (CONV_WIDTH, 64))
    d, nm, nv = _adamw("adamw_conv_w", conv_w[0], cw_mine, m_conv_w[0], v_conv_w[0])
    grads["conv_w"], delta["conv_w"], new_m["conv_w"], new_v["conv_w"] = cw_mine[None], d[None], nm[None], nv[None]

    return (loss.reshape(()), dx[None], *[grads[k] for k in WEIGHT_NAMES], *[delta[k] for k in WEIGHT_NAMES],
            *[new_m[k] for k in WEIGHT_NAMES], *[new_v[k] for k in WEIGHT_NAMES])
```

```python
import functools

import jax
import jax.numpy as jnp
from jax import lax
from jax.experimental import pallas as pl
from jax.experimental.pallas import tpu as pltpu
from jax.experimental.pallas import tpu_sc as plsc

F32 = jnp.float32
BF = jnp.bfloat16

SEQ = 2048
D_MODEL = 1024
HEAD_DIM = 64
N_Q_HEADS = 8
N_KV_HEADS = 2
GROUP = N_Q_HEADS // N_KV_HEADS
BLOCK = 128
ATTN_WIDTH = 512
KV_WIDTH = 128
CONV_CHANNELS = 512
CONV_WIDTH = 31
CONV_PAD = 32
GLU_OFF = 768
GATE_OFF = 1792
IN_WIDTH = 3840
D_FF = 2816
EPS = 1e-5
NEG = -1e30
N_DEV = 8

ADAM_LR = 0.001
ADAM_B1 = 0.9
ADAM_B2 = 0.999
ADAM_EPS = 1e-08
ADAM_WD = 0.01
ADAM_STEP = 10

VMEM_LIMIT_BYTES = 56 * 1024 * 1024
MESH = pl.DeviceIdType.MESH
ANY = pl.BlockSpec(memory_space=pl.ANY)

_DIMS = {"NN": (((1,), (0,)), ((), ())), "NT": (((1,), (1,)), ((), ())), "TN": (((0,), (0,)), ((), ()))}


def _params(sem):
    return pltpu.CompilerParams(dimension_semantics=sem, vmem_limit_bytes=VMEM_LIMIT_BYTES)


class _Carry:
    def __init__(self, arrays, out_shapes, sems, start, finish, peers=None):
        self.arrays, self.out_shapes, self.sems, self.start, self.finish = arrays, out_shapes, sems, start, finish
        self.peers = peers


def _carry_io(carry):
    if carry is None:
        return [], [], []
    return list(carry.arrays), list(carry.out_shapes), list(carry.sems)


def _matmul(name, a_list, b, mode, *, m, n, tm, tn, tk=None, epilogue, extra=(), outs, b_off=(0, 0), alias=None,
            scratch=(), carry=None):
    seg_k = [a.shape[0] if mode == "TN" else a.shape[1] for a in a_list]
    whole = tk is None
    seg_nk = [1] * len(a_list) if whole else [ks // tk for ks in seg_k]
    nk = 1 if whole else sum(seg_nk)
    starts = [sum(seg_nk[:s]) for s in range(len(seg_nk))]
    k_starts = [sum(seg_k[:s]) for s in range(len(seg_k))]
    k_tot = sum(seg_k)
    n_a, n_extra, n_out = len(a_list), len(extra), len(outs)

    a_specs = []
    for st, ns, ks in zip(starts, seg_nk, seg_k):
        if mode == "TN":
            a_specs.append(pl.BlockSpec((ks if whole else tk, tm), lambda j, i, k: (k, i)))
        elif whole:
            a_specs.append(pl.BlockSpec((tm, ks), lambda j, i, k: (i, 0)))
        else:
            a_specs.append(pl.BlockSpec((tm, tk), functools.partial(
                lambda j, i, k, st, ns: (i, jnp.clip(k - st, 0, ns - 1)), st=st, ns=ns)))
    bk = k_tot if whole else tk
    if mode == "NT":
        b_spec = pl.BlockSpec((tn, bk), lambda j, i, k: (b_off[0] + j, b_off[1] + k))
    else:
        b_spec = pl.BlockSpec((bk, tn), lambda j, i, k: (b_off[0] + k, b_off[1] + j))
    n_alias = 0 if alias is None else 1
    c_in, c_out, c_sems = _carry_io(carry)
    n_acc = 0 if whole else 1
    nj, ni = n // tn, m // tm

    def body(*refs):
        pos = [n_a, 1, n_alias, n_extra, len(c_in), n_out, len(c_out), n_acc, len(scratch), len(c_sems)]
        cuts = [sum(pos[:q]) for q in range(len(pos) + 1)]
        a_refs, (b_ref,), _, ex, ci_refs, out_refs, co_refs, acc_refs, scr, cs_refs = (
            refs[cuts[q]:cuts[q + 1]] for q in range(len(pos)))
        j, i, k = pl.program_id(0), pl.program_id(1), pl.program_id(2)
        ids = (j, i)
        if carry is not None:
            @pl.when((j == 0) & (i == 0) & (k == 0))
            def _():
                carry.start(ci_refs, co_refs, cs_refs)

        def dot(a_ref, bv):
            return lax.dot_general(a_ref[...].astype(BF), bv.astype(BF), _DIMS[mode], preferred_element_type=F32)

        if whole:
            tot = None
            for a_ref, k0, ks in zip(a_refs, k_starts, seg_k):
                if n_a == 1:
                    bv = b_ref[...]
                else:
                    bv = b_ref[:, k0:k0 + ks] if mode == "NT" else b_ref[k0:k0 + ks, :]
                part = dot(a_ref, bv)
                tot = part if tot is None else tot + part
            epilogue(tot, ex, out_refs, ids, scr)
        else:
            acc, = acc_refs

            @pl.when(k == 0)
            def _():
                acc[...] = jnp.zeros_like(acc)

            for a_ref, st, ns in zip(a_refs, starts, seg_nk):
                if n_a == 1:
                    acc[...] += dot(a_ref, b_ref[...])
                else:
                    @pl.when((k >= st) & (k < st + ns))
                    def _(a_ref=a_ref):
                        acc[...] += dot(a_ref, b_ref[...])

            @pl.when(k == nk - 1)
            def _():
                epilogue(acc[...], ex, out_refs, ids, scr)

        if carry is not None:
            @pl.when((j == nj - 1) & (i == ni - 1) & (k == nk - 1))
            def _():
                carry.finish(ci_refs, co_refs, cs_refs)

    in_specs = [*a_specs, b_spec]
    args = [*a_list, b]
    io_alias = {}
    if alias is not None:
        in_specs.append(pl.BlockSpec(memory_space=pl.ANY))
        args.append(alias[0])
        io_alias = {n_a + 1: alias[1]}
    in_specs += [s for _, s in extra] + [pl.BlockSpec(memory_space=pl.ANY)] * len(c_in)
    args += [x for x, _ in extra] + c_in
    res = pl.pallas_call(
        body, name=name, grid=(nj, ni, nk), in_specs=in_specs,
        out_specs=[s for _, s in outs] + [pl.BlockSpec(memory_space=pl.ANY)] * len(c_out),
        out_shape=[o for o, _ in outs] + c_out,
        scratch_shapes=[*([] if whole else [pltpu.VMEM((tm, tn), F32)]), *scratch, *c_sems],
        input_output_aliases=io_alias,
        compiler_params=_params(("arbitrary", "arbitrary", "arbitrary")),
    )(*args)
    return res if carry is None else (res[:n_out], res[n_out:])


def _tile(tm, tn):
    return pl.BlockSpec((tm, tn), lambda j, i, k: (i, j))


def _row(tn):
    return pl.BlockSpec((1, tn), lambda j, i, k: (0, j))


def _store(dtype):
    def ep(acc, ex, outs, ids, scr):
        outs[0][...] = acc.astype(dtype)
    return ep


def _sds(shape, dtype):
    return jax.ShapeDtypeStruct(shape, dtype)


def _rms_fwd(name, x, g):
    T, D = x.shape
    tm = 512

    def body(x_ref, g_ref, h_ref, r_ref):
        xv = x_ref[...]
        r = lax.rsqrt(jnp.mean(xv * xv, axis=-1, keepdims=True) + EPS)
        h_ref[...] = (xv * r * g_ref[...]).astype(BF)
        r_ref[...] = r

    return pl.pallas_call(
        body, name=name, grid=(T // tm,),
        in_specs=[pl.BlockSpec((tm, D), lambda i: (i, 0)), pl.BlockSpec((1, D), lambda i: (0, 0))],
        out_specs=[pl.BlockSpec((tm, D), lambda i: (i, 0)), pl.BlockSpec((tm, 1), lambda i: (i, 0))],
        out_shape=[_sds((T, D), BF), _sds((T, 1), F32)],
        compiler_params=_params(("arbitrary",)),
    )(x, g)


def _rms_bwd(dh, xv, r, g):
    xh = xv * r
    dxh = dh * g
    dx = r * (dxh - xh * jnp.mean(dxh * xh, axis=-1, keepdims=True))
    return dx, jnp.sum(dh * xh, axis=0, keepdims=True)


def _accumulate_rows(ref, val, first):
    @pl.when(first)
    def _():
        ref[...] = val

    @pl.when(jnp.logical_not(first))
    def _():
        ref[...] += val


def _loss_head(xv, g, target):
    r = lax.rsqrt(jnp.mean(xv * xv, axis=-1, keepdims=True) + EPS)
    err = xv * r * g - target
    dx, dg = _rms_bwd(err * (1.0 / xv.shape[-1]), xv, r, g)
    part = 0.5 * jnp.sum(jnp.mean(err * err, axis=-1, keepdims=True), axis=0, keepdims=True)
    return dx, dg, part


def _lane_half(shape, h):
    lane = lax.broadcasted_iota(jnp.int32, shape, 1)
    return (lane >= HEAD_DIM * h) & (lane < HEAD_DIM * (h + 1))


def _to_half(v, w, h):
    if w != h:
        v = pltpu.roll(v, HEAD_DIM, 1)
    return jnp.where(_lane_half(v.shape, h), v, 0.0)


def _attn_block(qkv_ref, sinks_ref, n, h):
    r0 = pl.multiple_of(n * BLOCK, BLOCK)
    p0 = pl.multiple_of(jnp.maximum(n - 1, 0) * BLOCK, BLOCK)
    rows = pl.ds(r0, BLOCK)
    prev = pl.ds(p0, BLOCK)
    k2 = jnp.concatenate([qkv_ref[prev, ATTN_WIDTH:ATTN_WIDTH + KV_WIDTH],
                          qkv_ref[rows, ATTN_WIDTH:ATTN_WIDTH + KV_WIDTH]], axis=0)
    v2 = jnp.concatenate([qkv_ref[prev, ATTN_WIDTH + KV_WIDTH:ATTN_WIDTH + 2 * KV_WIDTH],
                          qkv_ref[rows, ATTN_WIDTH + KV_WIDTH:ATTN_WIDTH + 2 * KV_WIDTH]], axis=0)
    qs = []
    for g in range(GROUP):
        hq = GROUP * h + g
        blk = qkv_ref[rows, (hq // 2) * 128:(hq // 2 + 1) * 128].astype(F32)
        qs.append(_to_half(blk, hq % 2, h))
    q4 = jnp.concatenate(qs, axis=0).astype(BF)
    s = lax.dot_general(q4, k2, _DIMS["NT"], preferred_element_type=F32) * (HEAD_DIM ** -0.5)
    shape = s.shape
    row = lax.broadcasted_iota(jnp.int32, shape, 0)
    qi = row & (BLOCK - 1)
    kj = lax.broadcasted_iota(jnp.int32, shape, 1)
    diff = qi + BLOCK - kj
    valid = (diff >= 0) & (diff < BLOCK) & ((kj >= BLOCK) | (n > 0))
    s = jnp.where(valid, s, NEG)
    row1 = lax.broadcasted_iota(jnp.int32, (shape[0], 1), 0)
    sink = jnp.zeros((shape[0], 1), F32)
    for g in range(GROUP):
        sink = jnp.where((row1 >= g * BLOCK) & (row1 < (g + 1) * BLOCK), sinks_ref[0, GROUP * h + g], sink)
    m = jnp.maximum(jnp.max(s, axis=-1, keepdims=True), sink)
    e = jnp.exp(s - m)
    es = jnp.exp(sink - m)
    inv = 1.0 / (jnp.sum(e, axis=-1, keepdims=True) + es)
    return e * inv, es * inv, q4, k2, v2, rows, prev


def _attn_fwd(proj, sinks, carry=None):
    T = proj.shape[0]
    c_in, c_out, c_sems = _carry_io(carry)

    def body(*refs):
        qkv_ref, sinks_ref = refs[:2]
        ci_refs = refs[2:2 + len(c_in)]
        o_ref = refs[2 + len(c_in)]
        co_refs = refs[3 + len(c_in):3 + len(c_in) + len(c_out)]
        cs_refs = refs[3 + len(c_in) + len(c_out):]
        if carry is not None:
            carry.start(ci_refs, co_refs, cs_refs)

        def blk(n, z):
            outs = [None] * (N_Q_HEADS // 2)
            for h in range(N_KV_HEADS):
                p, _, _, _, v2, rows, _ = _attn_block(qkv_ref, sinks_ref, n, h)
                o = lax.dot_general(p.astype(BF), v2, _DIMS["NN"], preferred_element_type=F32)
                for g in range(GROUP):
                    hq = GROUP * h + g
                    piece = jnp.where(_lane_half((BLOCK, 128), h), o[g * BLOCK:(g + 1) * BLOCK], 0.0)
                    if hq % 2 != h:
                        piece = pltpu.roll(piece, HEAD_DIM, 1)
                    outs[hq // 2] = piece if outs[hq // 2] is None else outs[hq // 2] + piece
            for pb in range(N_Q_HEADS // 2):
                o_ref[rows, pb * 128:(pb + 1) * 128] = outs[pb].astype(BF)
            return z

        lax.fori_loop(0, T // BLOCK, blk, 0)
        if carry is not None:
            carry.finish(ci_refs, co_refs, cs_refs)

    res = pl.pallas_call(
        body, name="attn_fwd", grid=(1,),
        in_specs=[pl.BlockSpec((T, GLU_OFF), lambda i: (0, 0)), pl.BlockSpec(memory_space=pltpu.SMEM),
                  *[ANY] * len(c_in)],
        out_specs=[pl.BlockSpec((T, ATTN_WIDTH), lambda i: (0, 0)), *[ANY] * len(c_out)],
        out_shape=[_sds((T, ATTN_WIDTH), BF), *c_out], scratch_shapes=c_sems,
        compiler_params=_params(("arbitrary",)),
    )(proj, sinks, *c_in)
    return res[0], res[1:]


def _attn_bwd(proj, d_o, sinks, carry=None):
    T = proj.shape[0]
    c_in, c_out, c_sems = _carry_io(carry)

    def body(*refs):
        qkv_ref, do_ref, sinks_ref = refs[:3]
        ci_refs = refs[3:3 + len(c_in)]
        dqkv_ref, dsink_ref = refs[3 + len(c_in):5 + len(c_in)]
        co_refs = refs[5 + len(c_in):5 + len(c_in) + len(c_out)]
        dk_acc, dv_acc = refs[5 + len(c_in) + len(c_out):7 + len(c_in) + len(c_out)]
        cs_refs = refs[7 + len(c_in) + len(c_out):]
        if carry is not None:
            carry.start(ci_refs, co_refs, cs_refs)
        dsink_ref[...] = jnp.zeros_like(dsink_ref)
        dk_acc[...] = jnp.zeros_like(dk_acc)
        dv_acc[...] = jnp.zeros_like(dv_acc)

        def blk(n, carry):
            dqs = [None] * (N_Q_HEADS // 2)
            for h in range(N_KV_HEADS):
                p, psink, q4, k2, v2, rows, prev = _attn_block(qkv_ref, sinks_ref, n, h)
                dos = []
                for g in range(GROUP):
                    hq = GROUP * h + g
                    dos.append(_to_half(do_ref[rows, (hq // 2) * 128:(hq // 2 + 1) * 128].astype(F32), hq % 2, h))
                do4 = jnp.concatenate(dos, axis=0).astype(BF)
                dp = lax.dot_general(do4, v2, _DIMS["NT"], preferred_element_type=F32)
                delta = jnp.sum(p * dp, axis=-1, keepdims=True)
                ds = (p * (dp - delta) * (HEAD_DIM ** -0.5)).astype(BF)
                dsk = psink * delta
                for g in range(GROUP):
                    hq = GROUP * h + g
                    tot = -jnp.sum(dsk[g * BLOCK:(g + 1) * BLOCK], axis=0, keepdims=True)
                    lane = lax.broadcasted_iota(jnp.int32, (1, 128), 1)
                    dsink_ref[...] += jnp.where(lane == hq, tot, 0.0)
                dq = lax.dot_general(ds, k2, _DIMS["NN"], preferred_element_type=F32)
                dk = lax.dot_general(ds, q4, _DIMS["TN"], preferred_element_type=F32)
                dv = lax.dot_general(p.astype(BF), do4, _DIMS["TN"], preferred_element_type=F32)
                dk_acc[prev, :] += dk[:BLOCK]
                dk_acc[rows, :] += dk[BLOCK:]
                dv_acc[prev, :] += dv[:BLOCK]
                dv_acc[rows, :] += dv[BLOCK:]
                for g in range(GROUP):
                    hq = GROUP * h + g
                    piece = jnp.where(_lane_half((BLOCK, 128), h), dq[g * BLOCK:(g + 1) * BLOCK], 0.0)
                    if hq % 2 != h:
                        piece = pltpu.roll(piece, HEAD_DIM, 1)
                    dqs[hq // 2] = piece if dqs[hq // 2] is None else dqs[hq // 2] + piece
            for pb in range(N_Q_HEADS // 2):
                dqkv_ref[rows, pb * 128:(pb + 1) * 128] = dqs[pb].astype(BF)
            return carry

        lax.fori_loop(0, T // BLOCK, blk, 0)
        dqkv_ref[:, ATTN_WIDTH:ATTN_WIDTH + KV_WIDTH] = dk_acc[...].astype(BF)
        dqkv_ref[:, ATTN_WIDTH + KV_WIDTH:] = dv_acc[...].astype(BF)
        if carry is not None:
            carry.finish(ci_refs, co_refs, cs_refs)

    res = pl.pallas_call(
        body, name="attn_bwd", grid=(1,),
        in_specs=[pl.BlockSpec((T, GLU_OFF), lambda i: (0, 0)), pl.BlockSpec((T, ATTN_WIDTH), lambda i: (0, 0)),
                  pl.BlockSpec(memory_space=pltpu.SMEM), *[ANY] * len(c_in)],
        out_specs=[pl.BlockSpec((T, GLU_OFF), lambda i: (0, 0)), pl.BlockSpec((1, 128), lambda i: (0, 0)),
                   *[ANY] * len(c_out)],
        out_shape=[_sds((T, GLU_OFF), BF), _sds((1, 128), F32), *c_out],
        scratch_shapes=[pltpu.VMEM((T, KV_WIDTH), F32), pltpu.VMEM((T, KV_WIDTH), F32), *c_sems],
        compiler_params=_params(("arbitrary",)),
    )(proj, d_o, sinks, *c_in)
    return res[:2], res[2:]


CHUNK = 256
SUB = 32
WIN = CHUNK + 32
PAD_ROWS = SEQ + 2 * CONV_PAD
_GLU_SPECS = [pl.BlockSpec((SEQ, 256), functools.partial(lambda i, c: (0, c), c=GLU_OFF // 256 + c)) for c in range(4)]


def _glu_to_pad(a0, a1, b0, b1, zpad):
    C = CONV_CHANNELS
    zpad[0:CONV_PAD, :] = jnp.zeros((CONV_PAD, C), F32)
    zpad[CONV_PAD + SEQ:, :] = jnp.zeros((CONV_PAD, C), F32)
    zpad[CONV_PAD:CONV_PAD + SEQ, 0:256] = a0[...].astype(F32) * jax.nn.sigmoid(b0[...].astype(F32))
    zpad[CONV_PAD:CONV_PAD + SEQ, 256:C] = a1[...].astype(F32) * jax.nn.sigmoid(b1[...].astype(F32))


def _tap_windows(src, base, win):
    for b in range(8):
        win[b, 0:WIN - 8, :] = src[base + b:base + b + WIN - 8, :]


def _taps(win, w_ref, init, out, flip):
    def sub(si, carry):
        r0 = pl.multiple_of(si * SUB, SUB)
        acc = jnp.broadcast_to(init, (SUB, CONV_CHANNELS))
        for k in range(CONV_WIDTH):
            wk = (CONV_WIDTH - 1 - k) if flip else k
            acc = acc + w_ref[wk:wk + 1, :] * win[k % 8, pl.ds(r0 + 8 * (k // 8), SUB), :]
        out[pl.ds(r0, SUB), :] = acc
        return carry

    lax.fori_loop(0, CHUNK // SUB, sub, 0)


def _tap_grads(win, du, dwacc):
    def sub(si, carry):
        r0 = pl.multiple_of(si * SUB, SUB)
        d = du[pl.ds(r0, SUB), :]
        for k in range(CONV_WIDTH):
            p = d * win[k % 8, pl.ds(r0 + 8 * (k // 8), SUB), :]
            dwacc[8 * k:8 * k + 8, :] += (p[0:8] + p[8:16]) + (p[16:24] + p[24:32])
        return carry

    lax.fori_loop(0, CHUNK // SUB, sub, 0)


def _ln_parts(u):
    mu = jnp.mean(u, axis=-1, keepdims=True)
    xc = u - mu
    rstd = lax.rsqrt(jnp.mean(xc * xc, axis=-1, keepdims=True) + EPS)
    return xc * rstd, rstd


def _conv_fwd(proj, conv_w, conv_b, ln_g, ln_b, carry=None):
    T, C = proj.shape[0], CONV_CHANNELS
    vec = pl.BlockSpec((1, C), lambda i: (0, 0))
    c_in, c_out, c_sems = _carry_io(carry)

    def body(*refs):
        a0, a1, b0, b1, w_ref, cb_ref, g_ref, be_ref = refs[:8]
        ci_refs = refs[8:8 + len(c_in)]
        c_ref = refs[8 + len(c_in)]
        co_refs = refs[9 + len(c_in):9 + len(c_in) + len(c_out)]
        zpad, win, ubuf = refs[9 + len(c_in) + len(c_out):12 + len(c_in) + len(c_out)]
        cs_refs = refs[12 + len(c_in) + len(c_out):]
        if carry is not None:
            carry.start(ci_refs, co_refs, cs_refs)
        _glu_to_pad(a0, a1, b0, b1, zpad)
        for ci in range(T // CHUNK):
            _tap_windows(zpad, ci * CHUNK + CONV_PAD - (CONV_WIDTH - 1), win)
            _taps(win, w_ref, cb_ref[...], ubuf, False)
            xh, _ = _ln_parts(ubuf[...])
            ln = xh * g_ref[...] + be_ref[...]
            c_ref[ci * CHUNK:(ci + 1) * CHUNK, :] = (ln * jax.nn.sigmoid(ln)).astype(BF)
        if carry is not None:
            carry.finish(ci_refs, co_refs, cs_refs)

    res = pl.pallas_call(
        body, name="conv_fwd", grid=(1,),
        in_specs=[*_GLU_SPECS, pl.BlockSpec((CONV_PAD, C), lambda i: (0, 0)), vec, vec, vec, *[ANY] * len(c_in)],
        out_specs=[pl.BlockSpec((T, C), lambda i: (0, 0)), *[ANY] * len(c_out)],
        out_shape=[_sds((T, C), BF), *c_out],
        scratch_shapes=[pltpu.VMEM((PAD_ROWS, C), F32), pltpu.VMEM((8, WIN, C), F32), pltpu.VMEM((CHUNK, C), F32),
                        *c_sems],
        compiler_params=_params(("arbitrary",)),
    )(proj, proj, proj, proj, conv_w, conv_b, ln_g, ln_b, *c_in)
    return res[0], res[1:]


def _conv_bwd(proj, d_c, conv_w, conv_b, ln_g, ln_b, carry=None):
    T, C = proj.shape[0], CONV_CHANNELS
    vec = pl.BlockSpec((1, C), lambda i: (0, 0))
    wspec = pl.BlockSpec((CONV_PAD, C), lambda i: (0, 0))
    c_in, c_out, c_sems = _carry_io(carry)

    def body(*refs):
        a0, a1, b0, b1, dc_ref, w_ref, cb_ref, g_ref, be_ref = refs[:9]
        ci_refs = refs[9:9 + len(c_in)]
        o = 9 + len(c_in)
        dglu_ref, dw_ref, dcb_ref, dg_ref, dbe_ref = refs[o:o + 5]
        co_refs = refs[o + 5:o + 5 + len(c_out)]
        zpad, dupad, win, ubuf, dwacc = refs[o + 5 + len(c_out):o + 10 + len(c_out)]
        cs_refs = refs[o + 10 + len(c_out):]
        if carry is not None:
            carry.start(ci_refs, co_refs, cs_refs)
        _glu_to_pad(a0, a1, b0, b1, zpad)
        dupad[T:, :] = jnp.zeros((2 * CONV_PAD, C), F32)
        dwacc[...] = jnp.zeros_like(dwacc)
        dcb_ref[...] = jnp.zeros_like(dcb_ref)
        dg_ref[...] = jnp.zeros_like(dg_ref)
        dbe_ref[...] = jnp.zeros_like(dbe_ref)
        for ci in range(T // CHUNK):
            rows = slice(ci * CHUNK, (ci + 1) * CHUNK)
            _tap_windows(zpad, ci * CHUNK + CONV_PAD - (CONV_WIDTH - 1), win)
            _taps(win, w_ref, cb_ref[...], ubuf, False)
            xh, rstd = _ln_parts(ubuf[...])
            ln = xh * g_ref[...] + be_ref[...]
            sg = jax.nn.sigmoid(ln)
            dln = dc_ref[rows, :].astype(F32) * (sg * (1.0 + ln * (1.0 - sg)))
            dg_ref[...] += jnp.sum(dln * xh, axis=0, keepdims=True)
            dbe_ref[...] += jnp.sum(dln, axis=0, keepdims=True)
            dxh = dln * g_ref[...]
            du = rstd * (dxh - jnp.mean(dxh, axis=-1, keepdims=True)
                         - xh * jnp.mean(dxh * xh, axis=-1, keepdims=True))
            dupad[rows, :] = du
            dcb_ref[...] += jnp.sum(du, axis=0, keepdims=True)
            _tap_grads(win, dupad.at[rows, :], dwacc)
        for k in range(CONV_WIDTH):
            dw_ref[k:k + 1, :] = jnp.sum(dwacc[8 * k:8 * k + 8, :], axis=0, keepdims=True)
        dw_ref[CONV_WIDTH:, :] = jnp.zeros((CONV_PAD - CONV_WIDTH, C), F32)
        for ci in range(T // CHUNK):
            rows = slice(ci * CHUNK, (ci + 1) * CHUNK)
            _tap_windows(dupad, ci * CHUNK, win)
            _taps(win, w_ref, jnp.zeros((1, C), F32), ubuf, True)
            dz = ubuf[...]
            for half, (a, b) in enumerate(((a0, b0), (a1, b1))):
                sb = jax.nn.sigmoid(b[rows, :].astype(F32))
                dzh = dz[:, half * 256:(half + 1) * 256]
                dglu_ref[rows, half * 256:(half + 1) * 256] = (dzh * sb).astype(BF)
                dglu_ref[rows, C + half * 256:C + (half + 1) * 256] = (
                    dzh * a[rows, :].astype(F32) * sb * (1.0 - sb)).astype(BF)
        if carry is not None:
            carry.finish(ci_refs, co_refs, cs_refs)

    res = pl.pallas_call(
        body, name="conv_bwd", grid=(1,),
        in_specs=[*_GLU_SPECS, pl.BlockSpec((T, C), lambda i: (0, 0)), wspec, vec, vec, vec, *[ANY] * len(c_in)],
        out_specs=[pl.BlockSpec((T, 2 * C), lambda i: (0, 0)), wspec, vec, vec, vec, *[ANY] * len(c_out)],
        out_shape=[_sds((T, 2 * C), BF), _sds((CONV_PAD, C), F32), _sds((1, C), F32), _sds((1, C), F32),
                   _sds((1, C), F32), *c_out],
        scratch_shapes=[pltpu.VMEM((PAD_ROWS, C), F32), pltpu.VMEM((PAD_ROWS, C), F32), pltpu.VMEM((8, WIN, C), F32),
                        pltpu.VMEM((CHUNK, C), F32), pltpu.VMEM((8 * CONV_PAD, C), F32), *c_sems],
        compiler_params=_params(("arbitrary",)),
    )(proj, proj, proj, proj, d_c, conv_w, conv_b, ln_g, ln_b, *c_in)
    return res[:5], res[5:]


_GATE_BLK = GATE_OFF // 256


def _ffn_in_swiglu(h2, wf_t, carry=None):
    T, D = h2.shape
    tm, tn = 512, D_FF // 2
    nj, ni = D_FF // tn, T // tm
    c_in, c_out, c_sems = _carry_io(carry)

    def body(*refs):
        a_ref, bg_ref, bu_ref = refs[:3]
        ci_refs = refs[3:3 + len(c_in)]
        act_ref, g_ref, u_ref = refs[3 + len(c_in):6 + len(c_in)]
        co_refs = refs[6 + len(c_in):6 + len(c_in) + len(c_out)]
        cs_refs = refs[6 + len(c_in) + len(c_out):]
        j, i = pl.program_id(0), pl.program_id(1)
        if carry is not None:
            @pl.when((j == 0) & (i == 0))
            def _():
                carry.start(ci_refs, co_refs, cs_refs)
        a = a_ref[...]
        for c0, c1 in ((0, 768), (768, tn)):
            g = lax.dot_general(a, bg_ref[c0:c1, :], _DIMS["NT"], preferred_element_type=F32)
            u = lax.dot_general(a, bu_ref[c0:c1, :], _DIMS["NT"], preferred_element_type=F32)
            act_ref[:, c0:c1] = (g * jax.nn.sigmoid(g) * u).astype(BF)
            g_ref[:, c0:c1] = g.astype(BF)
            u_ref[:, c0:c1] = u.astype(BF)
        if carry is not None:
            @pl.when((j == nj - 1) & (i == ni - 1))
            def _():
                carry.finish(ci_refs, co_refs, cs_refs)

    t = pl.BlockSpec((tm, tn), lambda j, i: (i, j))
    res = pl.pallas_call(
        body, name="ffn_in_swiglu", grid=(nj, ni),
        in_specs=[pl.BlockSpec((tm, D), lambda j, i: (i, 0)), pl.BlockSpec((tn, D), lambda j, i: (j, 0)),
                  pl.BlockSpec((tn, D), lambda j, i: (nj + j, 0)), *[ANY] * len(c_in)],
        out_specs=[t, t, t, *[ANY] * len(c_out)], out_shape=[*[_sds((T, D_FF), BF)] * 3, *c_out],
        scratch_shapes=c_sems,
        compiler_params=_params(("arbitrary", "arbitrary")),
    )(h2, wf_t, wf_t, *c_in)
    return res[:3], res[3:]


def _proj_in_dw(segs, h):
    T, D = h.shape
    tb = 256
    nblk = [seg.shape[1] // tb for seg in segs]
    starts = [sum(nblk[:q]) for q in range(len(segs))]
    n_seg = len(segs)

    def body(*refs):
        seg_refs, h_ref, o_ref, cs_ref = refs[:n_seg], refs[n_seg], refs[n_seg + 1], refs[n_seg + 2]
        i = pl.program_id(0)
        for seg_ref, st, nb in zip(seg_refs, starts, nblk):
            @pl.when((i >= st) & (i < st + nb))
            def _(seg_ref=seg_ref):
                a = seg_ref[...]
                o_ref[...] = lax.dot_general(a, h_ref[...], _DIMS["TN"], preferred_element_type=F32).astype(BF)
                cs_ref[...] = jnp.sum(a.astype(F32), axis=0, keepdims=True)

    in_specs = [pl.BlockSpec((T, tb), functools.partial(lambda i, st, nb: (0, jnp.clip(i - st, 0, nb - 1)), st=st, nb=nb))
                for st, nb in zip(starts, nblk)]
    return pl.pallas_call(
        body, name="proj_in_dw", grid=(sum(nblk),),
        in_specs=[*in_specs, pl.BlockSpec((T, D), lambda i: (0, 0))],
        out_specs=[pl.BlockSpec((tb, D), lambda i: (i, 0)), pl.BlockSpec((1, tb), lambda i: (0, i))],
        out_shape=[_sds((sum(nblk) * tb, D), BF), _sds((1, sum(nblk) * tb), F32)],
        compiler_params=_params(("arbitrary",)),
    )(*segs, h)


def _local_step(x, target, small, wi_t, conv_w, plan):
    T, D = x.shape
    tm = 1024

    def carried(call, res, carry):
        if carry is None:
            return res
        outs, got = res
        plan.done(call, got)
        return outs

    h, r1 = _rms_fwd("rms_mix", x, small["g_mix_norm"])

    def ep_add(acc, ex, outs, ids, scr):
        outs[0][...] = acc + ex[0][...]

    tn_in = IN_WIDTH // 3
    carry = plan.carry("proj_in")
    def ep_bias_bf16(acc, ex, outs, ids, scr):
        outs[0][...] = (acc + ex[0][...]).astype(BF)

    proj, = carried("proj_in", _matmul("proj_in", [h], wi_t, "NT", m=T, n=IN_WIDTH, tm=tm, tn=tn_in,
                                       epilogue=ep_bias_bf16, extra=[(small["b_in"], _row(tn_in))],
                                       outs=[(_sds((T, IN_WIDTH), BF), _tile(tm, tn_in))], carry=carry), carry)
    plan.launch("gather_ffn", after=proj)
    o, got = _attn_fwd(proj, small["sinks"], carry=plan.carry("attn_fwd"))
    plan.done("attn_fwd", got)
    c, got = _conv_fwd(proj, conv_w, small["conv_b"], small["ln_g"], small["ln_b"], carry=plan.carry("conv_fwd"))
    plan.done("conv_fwd", got)
    wap_t, wcp_t, w_out = plan.weight("w_attn_proj"), plan.weight("w_conv_proj"), plan.weight("w_out")
    ya, = _matmul("attn_proj", [o], wap_t, "NT", m=T, n=D, tm=tm, tn=D, epilogue=_store(F32),
                  outs=[(_sds((T, D), F32), _tile(tm, D))])

    tg = 256
    gate_specs = [pl.BlockSpec((tm, tg), lambda j, i, k: (i, _GATE_BLK + j)),
                  pl.BlockSpec((tm, tg), lambda j, i, k: (i, _GATE_BLK + D // tg + j))]

    def ep_merge(acc, ex, outs, ids, scr):
        yc = acc + ex[0][...]
        outs[0][...] = yc
        outs[1][...] = (jax.nn.sigmoid(ex[2][...].astype(F32)) * ex[1][...]
                        + jax.nn.sigmoid(ex[3][...].astype(F32)) * yc).astype(BF)

    carry = plan.carry("conv_proj_merge")
    yc, merged = carried("conv_proj_merge", _matmul(
        "conv_proj_merge", [c], wcp_t, "NT", m=T, n=D, tm=tm, tn=tg, epilogue=ep_merge,
        extra=[(small["b_conv_proj"], _row(tg)), (ya, _tile(tm, tg)), (proj, gate_specs[0]), (proj, gate_specs[1])],
        outs=[(_sds((T, D), F32), _tile(tm, tg)), (_sds((T, D), BF), _tile(tm, tg))], carry=carry), carry)
    def ep_residual_rms(acc, ex, outs, ids, scr):
        x2v = acc + ex[0][...]
        r = lax.rsqrt(jnp.mean(x2v * x2v, axis=-1, keepdims=True) + EPS)
        outs[0][...] = x2v
        outs[1][...] = (x2v * r * ex[1][...]).astype(BF)
        outs[2][...] = r

    carry = plan.carry("out_proj")
    x2, h2, r2 = carried("out_proj", _matmul(
        "out_proj_rms", [merged], w_out, "NN", m=T, n=D, tm=512, tn=D, epilogue=ep_residual_rms,
        extra=[(x, _tile(512, D)), (small["g_ffn_norm"], _row(D))],
        outs=[(_sds((T, D), F32), _tile(512, D)), (_sds((T, D), BF), _tile(512, D)),
              (_sds((T, 1), F32), pl.BlockSpec((512, 1), lambda j, i, k: (i, 0)))], carry=carry), carry)
    plan.launch("gather_down", after=x2)
    wf_t = plan.weight("w_ffn_in")
    (act, gate, up), got = _ffn_in_swiglu(h2, wf_t, carry=plan.carry("ffn_in_swiglu"))
    plan.done("ffn_in_swiglu", got)
    w_down = plan.weight("w_ffn_down")
    def ep_residual_loss(acc, ex, outs, ids, scr):
        dx, dg, part = _loss_head(acc + ex[0][...], ex[1][...], ex[2][...])
        outs[0][...] = dx
        outs[1][...] = dx.astype(BF)
        _accumulate_rows(outs[2], dg, ids[1] == 0)
        _accumulate_rows(outs[3], part, ids[1] == 0)

    dx3, dx3_b, dg_final, loss = _matmul(
        "ffn_down_loss", [act], w_down, "NN", m=T, n=D, tm=512, tn=D, epilogue=ep_residual_loss,
        extra=[(x2, _tile(512, D)), (small["g_final"], _row(D)), (target, _tile(512, D))],
        outs=[(_sds((T, D), F32), _tile(512, D)), (_sds((T, D), BF), _tile(512, D)), (_sds((1, D), F32), _row(D)),
              (_sds((1, 1), F32), pl.BlockSpec((1, 1), lambda j, i, k: (0, 0)))])

    tn_ff = D_FF // 2

    def ep_swiglu_bwd(acc, ex, outs, ids, scr):
        g, u = ex[0][...].astype(F32), ex[1][...].astype(F32)
        sg = jax.nn.sigmoid(g)
        outs[0][...] = (acc * u * sg * (1.0 + g * (1.0 - sg))).astype(BF)
        outs[1][...] = (acc * g * sg).astype(BF)

    dgate, dup = _matmul(
        "ffn_down_bwd", [dx3_b], w_down, "NT", m=T, n=D_FF, tm=512, tn=tn_ff, epilogue=ep_swiglu_bwd,
        extra=[(gate, _tile(512, tn_ff)), (up, _tile(512, tn_ff))],
        outs=[(_sds((T, D_FF), BF), _tile(512, tn_ff)), (_sds((T, D_FF), BF), _tile(512, tn_ff))])

    def dw(name, a, b, rows, cols, row_off=0, alias=None, total_rows=None, colsum=False):
        total_rows = rows if total_rows is None else total_rows
        tmw = rows if rows <= 1024 else D_FF // 2
        blk, rem = divmod(row_off, tmw)
        assert rem == 0

        def ep(acc, ex, outs, ids, scr):
            outs[0][...] = acc.astype(BF)
            if colsum:
                outs[1][...] = jnp.sum(ex[0][...].astype(F32), axis=0, keepdims=True)

        outs = [(_sds((total_rows, cols), BF), pl.BlockSpec((tmw, cols), lambda j, i, k: (blk + i, j)))]
        extra = []
        if colsum:
            extra = [(a, pl.BlockSpec((T, tmw), lambda j, i, k: (0, i)))]
            outs.append((_sds((1, rows), F32), pl.BlockSpec((1, tmw), lambda j, i, k: (0, i))))
        carry = plan.carry(name)
        res = carried(name, _matmul(name, [a], b, "TN", m=rows, n=cols, tm=tmw, tn=cols, epilogue=ep, extra=extra,
                                    outs=outs, alias=None if alias is None else (alias, 0), carry=carry), carry)
        return res if colsum else res[0]

    plan.grad_ready(dict(w_ffn_down=dw("ffn_down_dw", act, dx3_b, D_FF, D)))

    def ep_rms_bwd(acc, ex, outs, ids, scr):
        dx, dg = _rms_bwd(acc, ex[0][...], ex[1][...], ex[2][...])
        dx = ex[3][...] + dx
        outs[0][...] = dx
        outs[1][...] = dx.astype(BF)
        _accumulate_rows(outs[2], dg, ids[1] == 0)

    def rms_bwd_io(tm_, xin, r, g, dres):
        return dict(
            extra=[(xin, _tile(tm_, D)), (r, pl.BlockSpec((tm_, 1), lambda j, i, k: (i, 0))), (g, _row(D)),
                   (dres, _tile(tm_, D))],
            outs=[(_sds((T, D), F32), _tile(tm_, D)), (_sds((T, D), BF), _tile(tm_, D)), (_sds((1, D), F32), _row(D))])

    carry = plan.carry("ffn_in_bwd")
    dx2, dx2_b, dg_ffn = carried(
        "ffn_in_bwd",
        _matmul("ffn_in_bwd", [dgate, dup], wf_t, "NN", m=T, n=D, tm=tm, tn=D, tk=D_FF // 2, epilogue=ep_rms_bwd,
                carry=carry, **rms_bwd_io(tm, x2, r2, small["g_ffn_norm"], dx3)), carry)
    plan.launch("send_down")
    gwf_t = dw("ffn_in_dw_gate", dgate, h2, D_FF, D, total_rows=2 * D_FF)
    gwf_t = dw("ffn_in_dw_up", dup, h2, D_FF, D, row_off=D_FF, alias=gwf_t, total_rows=2 * D_FF)
    plan.grad_ready(dict(w_ffn_in=gwf_t))

    def ep_merge_bwd(acc, ex, outs, ids, scr):
        s0 = jax.nn.sigmoid(ex[2][...].astype(F32))
        s1 = jax.nn.sigmoid(ex[3][...].astype(F32))
        outs[0][...] = (acc * s0).astype(BF)
        outs[1][...] = (acc * s1).astype(BF)
        outs[2][...] = (acc * ex[0][...] * s0 * (1.0 - s0)).astype(BF)
        outs[3][...] = (acc * ex[1][...] * s1 * (1.0 - s1)).astype(BF)

    carry = plan.carry("out_proj_bwd_merge")
    dya, dyc, dg0, dg1 = carried(
        "out_proj_bwd_merge",
        _matmul("out_proj_bwd_merge", [dx2_b], w_out, "NT", m=T, n=D, tm=tm, tn=tg, epilogue=ep_merge_bwd,
                extra=[(ya, _tile(tm, tg)), (yc, _tile(tm, tg)), (proj, gate_specs[0]), (proj, gate_specs[1])],
                outs=[(_sds((T, D), BF), _tile(tm, tg))] * 4, carry=carry), carry)
    plan.launch("send_ffn")
    gw_out = dw("out_proj_dw", merged, dx2_b, D, D)
    d_o, = _matmul("attn_proj_bwd", [dya], wap_t, "NN", m=T, n=ATTN_WIDTH, tm=tm, tn=ATTN_WIDTH,
                   epilogue=_store(BF), outs=[(_sds((T, ATTN_WIDTH), BF), _tile(tm, ATTN_WIDTH))])
    d_c, = _matmul("conv_proj_bwd", [dyc], wcp_t, "NN", m=T, n=CONV_CHANNELS, tm=tm, tn=CONV_CHANNELS,
                   epilogue=_store(BF), outs=[(_sds((T, CONV_CHANNELS), BF), _tile(tm, CONV_CHANNELS))])
    gwap_t = dw("attn_proj_dw", dya, o, D, ATTN_WIDTH)
    gwcp_t, db_cp = dw("conv_proj_dw", dyc, c, D, CONV_CHANNELS, colsum=True)
    plan.grad_ready(dict(w_out=gw_out, w_attn_proj=gwap_t, w_conv_proj=gwcp_t))
    (dglu, dcw, dcb, dlng, dlnb), got = _conv_bwd(proj, d_c, conv_w, small["conv_b"], small["ln_g"], small["ln_b"],
                                                  carry=plan.carry("conv_bwd"))
    plan.done("conv_bwd", got)
    plan.launch("send_mix")
    (dqkv, dsinks), got = _attn_bwd(proj, d_o, small["sinks"], carry=plan.carry("attn_bwd"))
    plan.done("attn_bwd", got)

    segs = [dqkv, dglu, dg0, dg1]
    gwi_t, db_in = _proj_in_dw(segs, h)
    plan.grad_ready(dict(w_in=gwi_t))
    plan.alone("swap_inp")
    plan.launch("send_inp")
    carry = plan.carry("proj_in_bwd")
    dx, _, dg_mix = carried(
        "proj_in_bwd",
        _matmul("proj_in_bwd", segs, wi_t, "NN", m=T, n=D, tm=512, tn=D, epilogue=ep_rms_bwd, carry=carry,
                **rms_bwd_io(512, x, r1, small["g_mix_norm"], dx2)), carry)

    parts = dict(g_mix_norm=dg_mix, b_in=db_in, sinks=dsinks, conv_w=dcw, conv_b=dcb, ln_g=dlng, ln_b=dlnb,
                 b_conv_proj=db_cp, g_ffn_norm=dg_ffn, g_final=dg_final, loss=loss)
    return dx, parts


def _place():
    x, y, c = lax.axis_index("x"), lax.axis_index("y"), lax.axis_index("c")
    return x, y, c, [(1 - x, y), (x, 1 - y), (1 - x, 1 - y)]


def _gather_copies(x_refs, out_refs, rows_per, send_sems, recv_sems, local_sems):
    x, y, c, chips = _place()
    me, sibling = (x, y, c), (x, y, 1 - c)

    def rows(a, px, py, pc):
        return out_refs[a].at[pl.ds((4 * px + 2 * py + pc) * rows_per[a], rows_per[a])]

    def copy(a, k, block, to, src=None):
        return pltpu.make_async_remote_copy(
            src_ref=rows(a, *block) if src is None else src, dst_ref=rows(a, *block),
            send_sem=send_sems.at[7 * a + k], recv_sem=recv_sems.at[7 * a + k], device_id=to, device_id_type=MESH)

    def local(a):
        return pltpu.make_async_copy(x_refs[a], rows(a, *me), local_sems.at[a])

    def first(a):
        return [copy(a, 0, me, sibling, src=x_refs[a])] + [copy(a, 1 + j, me, (*chip, c), src=x_refs[a])
                                                          for j, chip in enumerate(chips)]

    def arrive(a, j):
        return copy(a, 1 + j, (*chips[j], c), me)

    def passed(a, j):
        return copy(a, 4 + j, (*chips[j], c), sibling)

    def from_sibling(a):
        return [copy(a, 0, sibling, me)] + [copy(a, 4 + j, (*chip, 1 - c), me) for j, chip in enumerate(chips)]

    return len(x_refs), local, first, arrive, passed, from_sibling


def _gather_start(*refs):
    n, local, first, _, _, _ = _gather_copies(*refs)
    for a in range(n):
        local(a).start()
        for cp in first(a):
            cp.start()


def _gather_finish(*refs):
    n, local, first, arrive, passed, from_sibling = _gather_copies(*refs)
    for a in range(n):
        for j in range(3):
            arrive(a, j).wait_recv()
            passed(a, j).start()
    for a in range(n):
        for cp in from_sibling(a):
            cp.wait_recv()
    for a in range(n):
        for cp in first(a) + [passed(a, j) for j in range(3)]:
            cp.wait_send()
        local(a).wait()


def _gather_peers():
    x, y, c, chips = _place()
    return [(x, y, 1 - c)] + [(*chip, c) for chip in chips]


def _gather_sems(n):
    return [pltpu.SemaphoreType.DMA((7 * n,)), pltpu.SemaphoreType.DMA((7 * n,)), pltpu.SemaphoreType.DMA((n,))]


def _gather_carry(shards):
    rows_per = [s.shape[0] for s in shards]
    return _Carry(shards, [_sds((N_DEV * s.shape[0],) + s.shape[1:], s.dtype) for s in shards],
                  _gather_sems(len(shards)),
                  lambda ins, outs, sems: _gather_start(ins, outs, rows_per, *sems),
                  lambda ins, outs, sems: _gather_finish(ins, outs, rows_per, *sems), _gather_peers)


def _swap_carry(grads):
    n = len(grads)

    def copies(g_refs, out_refs, sems):
        send_sems, recv_sems = sems
        x, y, c, _ = _place()
        return [pltpu.make_async_remote_copy(
            src_ref=g_refs[a].at[2 * p + 1 - c], dst_ref=out_refs[a].at[p],
            send_sem=send_sems.at[4 * a + p], recv_sem=recv_sems.at[4 * a + p],
            device_id=(x, y, 1 - c), device_id_type=MESH) for a in range(n) for p in range(4)]

    def start(ins, outs, sems):
        for cp in copies(ins, outs, sems):
            cp.start()

    def finish(ins, outs, sems):
        for cp in copies(ins, outs, sems):
            cp.wait()

    def peers():
        x, y, c, _ = _place()
        return [(x, y, 1 - c)]

    return _Carry(grads, [_sds((4,) + g.shape[1:], g.dtype) for g in grads],
                  [pltpu.SemaphoreType.DMA((4 * n,)), pltpu.SemaphoreType.DMA((4 * n,))], start, finish, peers)


def _join(carries):
    carries = [c for c in carries if c is not None]
    if not carries:
        return None
    n_in = [len(c.arrays) for c in carries]
    n_out = [len(c.out_shapes) for c in carries]
    n_sem = [len(c.sems) for c in carries]

    def parts(refs, counts):
        cuts = [sum(counts[:q]) for q in range(len(counts) + 1)]
        return [refs[cuts[q]:cuts[q + 1]] for q in range(len(counts))]

    def start(ins, outs, sems):
        for c, i, o, s in zip(carries, parts(ins, n_in), parts(outs, n_out), parts(sems, n_sem)):
            c.start(i, o, s)

    def finish(ins, outs, sems):
        for c, i, o, s in zip(carries, parts(ins, n_in), parts(outs, n_out), parts(sems, n_sem)):
            c.finish(i, o, s)

    return _Carry([a for c in carries for a in c.arrays], [o for c in carries for o in c.out_shapes],
                  [s for c in carries for s in c.sems], start, finish)


def _run_carry(name, carry):
    n_in, n_out = len(carry.arrays), len(carry.out_shapes)

    def body(*refs):
        carry.start(refs[:n_in], refs[n_in:n_in + n_out], refs[n_in + n_out:])
        carry.finish(refs[:n_in], refs[n_in:n_in + n_out], refs[n_in + n_out:])

    return pl.pallas_call(body, name=name, in_specs=[ANY] * n_in, out_specs=[ANY] * n_out,
                          out_shape=carry.out_shapes, scratch_shapes=carry.sems)(*carry.arrays)


def _run_carry_async(name, carry, collective_id):
    ins = [jax.new_ref(a, memory_space=pltpu.MemorySpace.HBM) for a in carry.arrays]
    outs = [jax.empty_ref(o, memory_space=pltpu.MemorySpace.HBM) for o in carry.out_shapes]

    @pl.kernel(mesh=plsc.ScalarSubcoreMesh(axis_name="sequencer", num_cores=1), name=name,
               scratch_types=tuple(carry.sems), compiler_params=pltpu.CompilerParams(collective_id=collective_id))
    def launch(*sems):
        barrier = pltpu.get_barrier_semaphore()
        peers = carry.peers()
        for peer in peers:
            pl.semaphore_signal(barrier, inc=1, device_id=peer, device_id_type=MESH)
        pl.semaphore_wait(barrier, len(peers))
        carry.start(ins, outs, sems)
        carry.finish(ins, outs, sems)

    launch()
    return [o[...] for o in outs]


def _chip_sum(name, g, got, c):
    _, rows, cols = g.shape

    def body(c_ref, g_ref, got_ref, o_ref):
        o_ref[...] = (g_ref[...].astype(F32) + got_ref[...].astype(F32)).astype(BF)

    return pl.pallas_call(
        body, name=name,
        grid_spec=pltpu.PrefetchScalarGridSpec(
            num_scalar_prefetch=1, grid=(4,),
            in_specs=[pl.BlockSpec((1, rows, cols), lambda p, c_ref: (2 * p + c_ref[0], 0, 0)),
                      pl.BlockSpec((1, rows, cols), lambda p, c_ref: (p, 0, 0))],
            out_specs=pl.BlockSpec((1, rows, cols), lambda p, c_ref: (p, 0, 0))),
        out_shape=_sds((4, rows, cols), BF),
        compiler_params=_params(("arbitrary",)),
    )(c, g, got)


def _send_carry(sums, ks):
    n, nk = len(sums), len(ks)

    def copies(s_refs, out_refs, sems):
        send_sems, recv_sems = sems
        x, y, c, chips = _place()
        return [pltpu.make_async_remote_copy(
            src_ref=s_refs[a].at[2 * chips[k][0] + chips[k][1]], dst_ref=out_refs[a].at[q],
            send_sem=send_sems.at[nk * a + q], recv_sem=recv_sems.at[nk * a + q],
            device_id=(*chips[k], c), device_id_type=MESH) for a in range(n) for q, k in enumerate(ks)]

    def start(ins, outs, sems):
        for cp in copies(ins, outs, sems):
            cp.start()

    def finish(ins, outs, sems):
        for cp in copies(ins, outs, sems):
            cp.wait()

    def peers():
        x, y, c, chips = _place()
        return [(*chips[k], c) for k in ks]

    return _Carry(sums, [_sds((nk,) + s.shape[1:], s.dtype) for s in sums],
                  [pltpu.SemaphoreType.DMA((nk * n,)), pltpu.SemaphoreType.DMA((nk * n,))], start, finish, peers)


def _grad_total(name, g, got, got3, ids):
    _, rows, cols = g.shape
    n3 = len(got3)

    def body(ids_ref, g_ref, got_ref, *rest):
        o_ref = rest[n3]
        tot = g_ref[0].astype(F32) + got_ref[0].astype(F32)
        for r_ref in rest[:n3]:
            for q in range(r_ref.shape[0]):
                tot = tot + r_ref[q].astype(F32)
        o_ref[...] = tot

    return pl.pallas_call(
        body, name=name,
        grid_spec=pltpu.PrefetchScalarGridSpec(
            num_scalar_prefetch=1, grid=(1,),
            in_specs=[pl.BlockSpec((1, rows, cols), lambda i, ids_ref: (ids_ref[0], 0, 0)),
                      pl.BlockSpec((1, rows, cols), lambda i, ids_ref: (ids_ref[1], 0, 0)),
                      *[pl.BlockSpec(r.shape, lambda i, ids_ref: (0, 0, 0)) for r in got3]],
            out_specs=pl.BlockSpec((rows, cols), lambda i, ids_ref: (0, 0))),
        out_shape=_sds((rows, cols), F32),
        compiler_params=_params(("arbitrary",)),
    )(ids, g, got, *got3)


def _adam_math(w, g, m, v):
    m = ADAM_B1 * m + (1.0 - ADAM_B1) * g
    v = ADAM_B2 * v + (1.0 - ADAM_B2) * (g * g)
    m_hat = m / (1.0 - ADAM_B1 ** ADAM_STEP)
    v_hat = v / (1.0 - ADAM_B2 ** ADAM_STEP)
    delta = -ADAM_LR * (m_hat / (jnp.sqrt(v_hat) + ADAM_EPS) + ADAM_WD * w)
    return delta, m, v


def _adamw(name, w, g, m, v):
    rows, cols = w.shape
    tr = 256 if rows % 256 == 0 else rows

    def body(w_ref, g_ref, m_ref, v_ref, d_ref, nm_ref, nv_ref):
        d_ref[...], nm_ref[...], nv_ref[...] = _adam_math(w_ref[...], g_ref[...], m_ref[...], v_ref[...])

    t = pl.BlockSpec((tr, cols), lambda i: (i, 0))
    return pl.pallas_call(
        body, name=name, grid=(rows // tr,), in_specs=[t] * 4, out_specs=[t] * 3,
        out_shape=[_sds((rows, cols), F32)] * 3, compiler_params=_params(("arbitrary",)),
    )(w, g, m, v)


SMALL_NAMES = ["g_mix_norm", "b_in", "sinks", "conv_b", "ln_g", "ln_b", "b_conv_proj", "g_ffn_norm", "g_final"]
_PACK_ROWS = 32


def _small_pack(parts):
    C = CONV_CHANNELS
    part_list = [parts["g_mix_norm"], parts["b_in"], parts["sinks"], parts["conv_b"], parts["ln_g"], parts["ln_b"],
                 parts["b_conv_proj"], parts["g_ffn_norm"], parts["g_final"], parts["loss"], parts["conv_w"]]

    def body(p_mix, p_b, p_sink, p_cb, p_lg, p_lb, p_bcp, p_ffn, p_fin, p_loss, p_cw, pack):
        pack[...] = jnp.zeros_like(pack)
        pack[0:1, :] = p_mix[...]
        pack[1:2, 0:GLU_OFF] = p_b[:, 0:GLU_OFF]
        pack[2:3, :] = p_b[:, GLU_OFF:GATE_OFF]
        pack[3:4, :] = p_b[:, GATE_OFF:GATE_OFF + D_MODEL]
        pack[4:5, :] = p_b[:, GATE_OFF + D_MODEL:]
        pack[5:6, 0:128] = p_sink[...]
        pack[6:7, 0:C] = p_cb[...]
        pack[6:7, C:2 * C] = p_lg[...]
        pack[7:8, 0:C] = p_lb[...]
        pack[8:9, :] = p_bcp[...]
        pack[9:10, :] = p_ffn[...]
        pack[10:11, :] = p_fin[...]
        pack[11:12, 0:128] = jnp.broadcast_to(p_loss[...], (1, 128))
        pack[12:28, 0:C] = p_cw[0:16, :]
        pack[12:28, C:2 * C] = p_cw[16:32, :]

    vm = pl.BlockSpec(memory_space=pltpu.VMEM)
    return pl.pallas_call(body, name="small_pack", in_specs=[vm] * len(part_list), out_specs=vm,
                          out_shape=_sds((_PACK_ROWS, D_MODEL), F32))(*part_list)


def _small_adamw(gathered, small_w, small_m, small_v):
    C = CONV_CHANNELS
    names = SMALL_NAMES
    widths = [small_w[k].shape[1] for k in names]
    n_small = len(names)

    def body(*refs):
        tot_ref = refs[0]
        w_refs = refs[1:1 + n_small]
        m_refs = refs[1 + n_small:1 + 2 * n_small]
        v_refs = refs[1 + 2 * n_small:1 + 3 * n_small]
        o = 1 + 3 * n_small
        loss_ref, cw_ref = refs[o], refs[o + 1]
        out_refs = refs[o + 2:o + 2 + 4 * n_small]
        tot = tot_ref[0:_PACK_ROWS, :]
        for d in range(1, N_DEV):
            tot = tot + tot_ref[d * _PACK_ROWS:(d + 1) * _PACK_ROWS, :]
        loss_ref[...] = tot[11:12, 0:1]
        cw_ref[0:16, :] = tot[12:28, 0:C]
        cw_ref[16:32, :] = tot[12:28, C:2 * C]
        grads = dict(
            g_mix_norm=tot[0:1, :],
            b_in=jnp.concatenate([tot[1:2, 0:GLU_OFF], tot[2:3, :], tot[3:4, :], tot[4:5, :]], axis=1),
            sinks=tot[5:6, 0:N_Q_HEADS], conv_b=tot[6:7, 0:C], ln_g=tot[6:7, C:2 * C], ln_b=tot[7:8, 0:C],
            b_conv_proj=tot[8:9, :], g_ffn_norm=tot[9:10, :], g_final=tot[10:11, :])
        for s, k in enumerate(names):
            g = grads[k]
            d, nm, nv = _adam_math(w_refs[s][...], g, m_refs[s][...], v_refs[s][...])
            out_refs[4 * s][...] = g
            out_refs[4 * s + 1][...] = d
            out_refs[4 * s + 2][...] = nm
            out_refs[4 * s + 3][...] = nv

    vm = pl.BlockSpec(memory_space=pltpu.VMEM)
    args = [gathered, *[small_w[k] for k in names], *[small_m[k] for k in names], *[small_v[k] for k in names]]
    out_shape = [_sds((1, 1), F32), _sds((CONV_PAD, C), F32)]
    for wd in widths:
        out_shape += [_sds((1, wd), F32)] * 4
    res = pl.pallas_call(
        body, name="small_adamw",
        in_specs=[vm] * len(args), out_specs=[vm] * len(out_shape), out_shape=out_shape,
        compiler_params=pltpu.CompilerParams(vmem_limit_bytes=VMEM_LIMIT_BYTES),
    )(*args)
    return res[0], res[1], {k: res[2 + 4 * s:6 + 4 * s] for s, k in enumerate(names)}


BIG = dict(w_in=True, w_attn_proj=True, w_conv_proj=True, w_out=False, w_ffn_in=True, w_ffn_down=False)
WEIGHT_NAMES = ["g_mix_norm", "w_in", "b_in", "sinks", "conv_w", "conv_b", "ln_g", "ln_b", "w_attn_proj",
                "w_conv_proj", "b_conv_proj", "w_out", "g_ffn_norm", "w_ffn_in", "w_ffn_down", "g_final"]


class _Plan:
    GROUPS = dict(down=["w_ffn_down"], ffn=["w_ffn_in"], mix=["w_out", "w_attn_proj", "w_conv_proj"], inp=["w_in"])
    ALL = (0, 1, 2)
    RIDES = dict(
        gather_mix=[("gather", ["w_attn_proj", "w_conv_proj", "w_out"])], gather_ffn=[("gather", ["w_ffn_in"])],
        gather_down=[("gather", ["w_ffn_down"])],
        ffn_in_bwd=[("swap", "down")], send_down=[("send", "down", ALL)],
        out_proj_bwd_merge=[("swap", "ffn")], send_ffn=[("send", "ffn", ALL)],
        conv_bwd=[("swap", "mix")], send_mix=[("send", "mix", ALL)],
        swap_inp=[("swap", "inp")], send_inp=[("send", "inp", ALL)])
    ASYNC = dict(gather_mix=1, gather_ffn=2, gather_down=3, send_down=4, send_ffn=5, send_mix=6, send_inp=7)

    def __init__(self, shards, c1):
        self.shards, self.c1 = shards, c1
        self.full, self.slots, self.got, self.sums, self.got3 = {}, {}, {}, {}, {}

    def weight(self, name):
        return self.full[name]

    def grad_ready(self, grads):
        for k, g in grads.items():
            self.slots[k] = g.reshape(N_DEV, g.shape[0] // N_DEV, g.shape[1])

    def _one(self, kind, what, ks=None):
        if kind == "gather":
            return _gather_carry([self.shards[k] for k in what])
        names = self.GROUPS[what]
        if kind == "swap":
            return _swap_carry([self.slots[k] for k in names])
        return _send_carry([self.sums[k] for k in names], ks)

    def carry(self, call):
        return _join([self._one(*ride) for ride in self.RIDES.get(call, [])])

    def done(self, call, outs):
        outs = list(outs)
        for kind, what, *_ in self.RIDES.get(call, []):
            names = what if kind == "gather" else self.GROUPS[what]
            mine, outs = outs[:len(names)], outs[len(names):]
            if kind == "gather":
                self.full.update(zip(names, mine))
            elif kind == "send":
                for k, r in zip(names, mine):
                    self.got3.setdefault(k, []).append(r)
            else:
                for k, r in zip(names, mine):
                    self.got[k] = r
                    self.sums[k] = _chip_sum(f"chip_sum_{k}", self.slots[k], r, self.c1)

    def alone(self, call):
        self.done(call, _run_carry(call, self.carry(call)))

    def launch(self, call, after=None):
        carry = self._one(*self.RIDES[call][0])
        if after is not None:
            carry.arrays = list(lax.optimization_barrier((tuple(carry.arrays), after))[0])
        self.done(call, _run_carry_async(call, carry, self.ASYNC[call]))


def kernel(x, g_mix_norm, w_in, b_in, sinks, conv_w, conv_b, ln_g, ln_b, w_attn_proj, w_conv_proj, b_conv_proj, w_out, g_ffn_norm, w_ffn_in, w_ffn_down, g_final, loss_target, m_g_mix_norm, m_w_in, m_b_in, m_sinks, m_conv_w, m_conv_b, m_ln_g, m_ln_b, m_w_attn_proj, m_w_conv_proj, m_b_conv_proj, m_w_out, m_g_ffn_norm, m_w_ffn_in, m_w_ffn_down, m_g_final, v_g_mix_norm, v_w_in, v_b_in, v_sinks, v_conv_w, v_conv_b, v_ln_g, v_ln_b, v_w_attn_proj, v_w_conv_proj, v_b_conv_proj, v_w_out, v_g_ffn_norm, v_w_ffn_in, v_w_ffn_down, v_g_final):
    w = dict(g_mix_norm=g_mix_norm, w_in=w_in, b_in=b_in, sinks=sinks, conv_w=conv_w, conv_b=conv_b, ln_g=ln_g,
             ln_b=ln_b, w_attn_proj=w_attn_proj, w_conv_proj=w_conv_proj, b_conv_proj=b_conv_proj, w_out=w_out,
             g_ffn_norm=g_ffn_norm, w_ffn_in=w_ffn_in, w_ffn_down=w_ffn_down, g_final=g_final)
    m = dict(g_mix_norm=m_g_mix_norm, w_in=m_w_in, b_in=m_b_in, sinks=m_sinks, conv_w=m_conv_w, conv_b=m_conv_b,
             ln_g=m_ln_g, ln_b=m_ln_b, w_attn_proj=m_w_attn_proj, w_conv_proj=m_w_conv_proj,
             b_conv_proj=m_b_conv_proj, w_out=m_w_out, g_ffn_norm=m_g_ffn_norm, w_ffn_in=m_w_ffn_in,
             w_ffn_down=m_w_ffn_down, g_final=m_g_final)
    v = dict(g_mix_norm=v_g_mix_norm, w_in=v_w_in, b_in=v_b_in, sinks=v_sinks, conv_w=v_conv_w, conv_b=v_conv_b,
             ln_g=v_ln_g, ln_b=v_ln_b, w_attn_proj=v_w_attn_proj, w_conv_proj=v_w_conv_proj,
             b_conv_proj=v_b_conv_proj, w_out=v_w_out, g_ffn_norm=v_g_ffn_norm, w_ffn_in=v_w_ffn_in,
             w_ffn_down=v_w_ffn_down, g_final=v_g_final)
    ax, ay, ac = lax.axis_index("x"), lax.axis_index("y"), lax.axis_index("c")
    me = 4 * ax + 2 * ay + ac
    chip = 2 * ax + ay

    shards = {k: (w[k][0].T if tr else w[k][0]).astype(BF) for k, tr in BIG.items()}
    cw_shard = jnp.pad(conv_w[0].T, ((0, 0), (0, 1))).reshape(16, 128)
    wi_t, cw_full = _run_carry("weights_all_gather", _gather_carry([shards["w_in"], cw_shard]))
    conv_full = cw_full.reshape(CONV_CHANNELS, CONV_PAD).T

    as_row = lambda a: a.reshape(1, -1)
    small_w = {k: as_row(w[k]) for k in SMALL_NAMES}
    small_m = {k: as_row(m[k]) for k in SMALL_NAMES}
    small_v = {k: as_row(v[k]) for k in SMALL_NAMES}
    plan = _Plan(shards, ac.reshape(1).astype(jnp.int32))
    plan.launch("gather_mix", after=wi_t)
    dx, parts = _local_step(x[0], loss_target[0], small_w, wi_t, conv_full, plan)

    small_gathered, = _run_carry_async("small_gather", _gather_carry([_small_pack(parts)]), 8)

    ids = jnp.stack([me, chip]).astype(jnp.int32)
    grads, delta, new_m, new_v = {}, {}, {}, {}
    for k in sorted(BIG, key=lambda k: k == "w_in"):
        tot = _grad_total(f"grad_total_{k}", plan.slots[k], plan.got[k], plan.got3[k], ids)
        tot = tot.T if BIG[k] else tot
        d, nm, nv = _adamw(f"adamw_{k}", w[k][0], tot, m[k][0], v[k][0])
        grads[k], delta[k], new_m[k], new_v[k] = tot[None], d[None], nm[None], nv[None]

    loss, cw_grad, small_out = _small_adamw(small_gathered, small_w, small_m, small_v)
    for k in SMALL_NAMES:
        g, d, nm, nv = (a.reshape(w[k].shape) for a in small_out[k])
        grads[k], delta[k], new_m[k], new_v[k] = g, d, nm, nv
    cw_mine = lax.dynamic_slice(cw_grad, (0, me * 64), (CONV_WIDTH, 64))
    d, nm, nv = _adamw("adamw_conv_w", conv_w[0], cw_mine, m_conv_w[0], v_conv_w[0])
    grads["conv_w"], delta["conv_w"], new_m["conv_w"], new_v["conv_w"] = cw_mine[None], d[None], nm[None], nv[None]

    return (loss.reshape(()), dx[None], *[grads[k] for k in WEIGHT_NAMES], *[delta[k] for k in WEIGHT_NAMES],
            *[new_m[k] for k in WEIGHT_NAMES], *[new_v[k] for k in WEIGHT_NAMES])
```

```python
import functools

import jax
import jax.numpy as jnp
from jax import lax
from jax.experimental import pallas as pl
from jax.experimental.pallas import tpu as pltpu
from jax.experimental.pallas import tpu_sc as plsc

F32 = jnp.float32
BF = jnp.bfloat16

SEQ = 2048
D_MODEL = 1024
HEAD_DIM = 64
N_Q_HEADS = 8
N_KV_HEADS = 2
GROUP = N_Q_HEADS // N_KV_HEADS
BLOCK = 128
ATTN_WIDTH = 512
KV_WIDTH = 128
CONV_CHANNELS = 512
CONV_WIDTH = 31
CONV_PAD = 32
GLU_OFF = 768
GATE_OFF = 1792
IN_WIDTH = 3840
D_FF = 2816
EPS = 1e-5
NEG = -1e30
N_DEV = 8

ADAM_LR = 0.001
ADAM_B1 = 0.9
ADAM_B2 = 0.999
ADAM_EPS = 1e-08
ADAM_WD = 0.01
ADAM_STEP = 10

VMEM_LIMIT_BYTES = 56 * 1024 * 1024
MESH = pl.DeviceIdType.MESH
ANY = pl.BlockSpec(memory_space=pl.ANY)

_DIMS = {"NN": (((1,), (0,)), ((), ())), "NT": (((1,), (1,)), ((), ())), "TN": (((0,), (0,)), ((), ()))}


def _params(sem):
    return pltpu.CompilerParams(dimension_semantics=sem, vmem_limit_bytes=VMEM_LIMIT_BYTES)


class _Carry:
    def __init__(self, arrays, out_shapes, sems, start, finish, peers=None):
        self.arrays, self.out_shapes, self.sems, self.start, self.finish = arrays, out_shapes, sems, start, finish
        self.peers = peers


def _carry_io(carry):
    if carry is None:
        return [], [], []
    return list(carry.arrays), list(carry.out_shapes), list(carry.sems)


def _matmul(name, a_list, b, mode, *, m, n, tm, tn, tk=None, epilogue, extra=(), outs, b_off=(0, 0), alias=None,
            scratch=(), carry=None):
    seg_k = [a.shape[0] if mode == "TN" else a.shape[1] for a in a_list]
    whole = tk is None
    seg_nk = [1] * len(a_list) if whole else [ks // tk for ks in seg_k]
    nk = 1 if whole else sum(seg_nk)
    starts = [sum(seg_nk[:s]) for s in range(len(seg_nk))]
    k_starts = [sum(seg_k[:s]) for s in range(len(seg_k))]
    k_tot = sum(seg_k)
    n_a, n_extra, n_out = len(a_list), len(extra), len(outs)

    a_specs = []
    for st, ns, ks in zip(starts, seg_nk, seg_k):
        if mode == "TN":
            a_specs.append(pl.BlockSpec((ks if whole else tk, tm), lambda j, i, k: (k, i)))
        elif whole:
            a_specs.append(pl.BlockSpec((tm, ks), lambda j, i, k: (i, 0)))
        else:
            a_specs.append(pl.BlockSpec((tm, tk), functools.partial(
                lambda j, i, k, st, ns: (i, jnp.clip(k - st, 0, ns - 1)), st=st, ns=ns)))
    bk = k_tot if whole else tk
    if mode == "NT":
        b_spec = pl.BlockSpec((tn, bk), lambda j, i, k: (b_off[0] + j, b_off[1] + k))
    else:
        b_spec = pl.BlockSpec((bk, tn), lambda j, i, k: (b_off[0] + k, b_off[1] + j))
    n_alias = 0 if alias is None else 1
    c_in, c_out, c_sems = _carry_io(carry)
    n_acc = 0 if whole else 1
    nj, ni = n // tn, m // tm

    def body(*refs):
        pos = [n_a, 1, n_alias, n_extra, len(c_in), n_out, len(c_out), n_acc, len(scratch), len(c_sems)]
        cuts = [sum(pos[:q]) for q in range(len(pos) + 1)]
        a_refs, (b_ref,), _, ex, ci_refs, out_refs, co_refs, acc_refs, scr, cs_refs = (
            refs[cuts[q]:cuts[q + 1]] for q in range(len(pos)))
        j, i, k = pl.program_id(0), pl.program_id(1), pl.program_id(2)
        ids = (j, i)
        if carry is not None:
            @pl.when((j == 0) & (i == 0) & (k == 0))
            def _():
                carry.start(ci_refs, co_refs, cs_refs)

        def dot(a_ref, bv):
            return lax.dot_general(a_ref[...].astype(BF), bv.astype(BF), _DIMS[mode], preferred_element_type=F32)

        if whole:
            tot = None
            for a_ref, k0, ks in zip(a_refs, k_starts, seg_k):
                if n_a == 1:
                    bv = b_ref[...]
                else:
                    bv = b_ref[:, k0:k0 + ks] if mode == "NT" else b_ref[k0:k0 + ks, :]
                part = dot(a_ref, bv)
                tot = part if tot is None else tot + part
            epilogue(tot, ex, out_refs, ids, scr)
        else:
            acc, = acc_refs

            @pl.when(k == 0)
            def _():
                acc[...] = jnp.zeros_like(acc)

            for a_ref, st, ns in zip(a_refs, starts, seg_nk):
                if n_a == 1:
                    acc[...] += dot(a_ref, b_ref[...])
                else:
                    @pl.when((k >= st) & (k < st + ns))
                    def _(a_ref=a_ref):
                        acc[...] += dot(a_ref, b_ref[...])

            @pl.when(k == nk - 1)
            def _():
                epilogue(acc[...], ex, out_refs, ids, scr)

        if carry is not None:
            @pl.when((j == nj - 1) & (i == ni - 1) & (k == nk - 1))
            def _():
                carry.finish(ci_refs, co_refs, cs_refs)

    in_specs = [*a_specs, b_spec]
    args = [*a_list, b]
    io_alias = {}
    if alias is not None:
        in_specs.append(pl.BlockSpec(memory_space=pl.ANY))
        args.append(alias[0])
        io_alias = {n_a + 1: alias[1]}
    in_specs += [s for _, s in extra] + [pl.BlockSpec(memory_space=pl.ANY)] * len(c_in)
    args += [x for x, _ in extra] + c_in
    res = pl.pallas_call(
        body, name=name, grid=(nj, ni, nk), in_specs=in_specs,
        out_specs=[s for _, s in outs] + [pl.BlockSpec(memory_space=pl.ANY)] * len(c_out),
        out_shape=[o for o, _ in outs] + c_out,
        scratch_shapes=[*([] if whole else [pltpu.VMEM((tm, tn), F32)]), *scratch, *c_sems],
        input_output_aliases=io_alias,
        compiler_params=_params(("arbitrary", "arbitrary", "arbitrary")),
    )(*args)
    return res if carry is None else (res[:n_out], res[n_out:])


def _tile(tm, tn):
    return pl.BlockSpec((tm, tn), lambda j, i, k: (i, j))


def _row(tn):
    return pl.BlockSpec((1, tn), lambda j, i, k: (0, j))


def _store(dtype):
    def ep(acc, ex, outs, ids, scr):
        outs[0][...] = acc.astype(dtype)
    return ep


def _sds(shape, dtype):
    return jax.ShapeDtypeStruct(shape, dtype)


def _rms_fwd(name, x, g):
    T, D = x.shape
    tm = 512

    def body(x_ref, g_ref, h_ref, r_ref):
        xv = x_ref[...]
        r = lax.rsqrt(jnp.mean(xv * xv, axis=-1, keepdims=True) + EPS)
        h_ref[...] = (xv * r * g_ref[...]).astype(BF)
        r_ref[...] = r

    return pl.pallas_call(
        body, name=name, grid=(T // tm,),
        in_specs=[pl.BlockSpec((tm, D), lambda i: (i, 0)), pl.BlockSpec((1, D), lambda i: (0, 0))],
        out_specs=[pl.BlockSpec((tm, D), lambda i: (i, 0)), pl.BlockSpec((tm, 1), lambda i: (i, 0))],
        out_shape=[_sds((T, D), BF), _sds((T, 1), F32)],
        compiler_params=_params(("arbitrary",)),
    )(x, g)


def _rms_bwd(dh, xv, r, g):
    xh = xv * r
    dxh = dh * g
    dx = r * (dxh - xh * jnp.mean(dxh * xh, axis=-1, keepdims=True))
    return dx, jnp.sum(dh * xh, axis=0, keepdims=True)


def _accumulate_rows(ref, val, first):
    @pl.when(first)
    def _():
        ref[...] = val

    @pl.when(jnp.logical_not(first))
    def _():
        ref[...] += val


def _loss_head(xv, g, target):
    r = lax.rsqrt(jnp.mean(xv * xv, axis=-1, keepdims=True) + EPS)
    err = xv * r * g - target
    dx, dg = _rms_bwd(err * (1.0 / xv.shape[-1]), xv, r, g)
    part = 0.5 * jnp.sum(jnp.mean(err * err, axis=-1, keepdims=True), axis=0, keepdims=True)
    return dx, dg, part


def _lane_half(shape, h):
    lane = lax.broadcasted_iota(jnp.int32, shape, 1)
    return (lane >= HEAD_DIM * h) & (lane < HEAD_DIM * (h + 1))


def _to_half(v, w, h):
    if w != h:
        v = pltpu.roll(v, HEAD_DIM, 1)
    return jnp.where(_lane_half(v.shape, h), v, 0.0)


def _attn_block(qkv_ref, sinks_ref, n, h):
    r0 = pl.multiple_of(n * BLOCK, BLOCK)
    p0 = pl.multiple_of(jnp.maximum(n - 1, 0) * BLOCK, BLOCK)
    rows = pl.ds(r0, BLOCK)
    prev = pl.ds(p0, BLOCK)
    k2 = jnp.concatenate([qkv_ref[prev, ATTN_WIDTH:ATTN_WIDTH + KV_WIDTH],
                          qkv_ref[rows, ATTN_WIDTH:ATTN_WIDTH + KV_WIDTH]], axis=0)
    v2 = jnp.concatenate([qkv_ref[prev, ATTN_WIDTH + KV_WIDTH:ATTN_WIDTH + 2 * KV_WIDTH],
                          qkv_ref[rows, ATTN_WIDTH + KV_WIDTH:ATTN_WIDTH + 2 * KV_WIDTH]], axis=0)
    qs = []
    for g in range(GROUP):
        hq = GROUP * h + g
        blk = qkv_ref[rows, (hq // 2) * 128:(hq // 2 + 1) * 128].astype(F32)
        qs.append(_to_half(blk, hq % 2, h))
    q4 = jnp.concatenate(qs, axis=0).astype(BF)
    s = lax.dot_general(q4, k2, _DIMS["NT"], preferred_element_type=F32) * (HEAD_DIM ** -0.5)
    shape = s.shape
    row = lax.broadcasted_iota(jnp.int32, shape, 0)
    qi = row & (BLOCK - 1)
    kj = lax.broadcasted_iota(jnp.int32, shape, 1)
    diff = qi + BLOCK - kj
    valid = (diff >= 0) & (diff < BLOCK) & ((kj >= BLOCK) | (n > 0))
    s = jnp.where(valid, s, NEG)
    row1 = lax.broadcasted_iota(jnp.int32, (shape[0], 1), 0)
    sink = jnp.zeros((shape[0], 1), F32)
    for g in range(GROUP):
        sink = jnp.where((row1 >= g * BLOCK) & (row1 < (g + 1) * BLOCK), sinks_ref[0, GROUP * h + g], sink)
    m = jnp.maximum(jnp.max(s, axis=-1, keepdims=True), sink)
    e = jnp.exp(s - m)
    es = jnp.exp(sink - m)
    inv = 1.0 / (jnp.sum(e, axis=-1, keepdims=True) + es)
    return e * inv, es * inv, q4, k2, v2, rows, prev


def _attn_fwd(proj, sinks, carry=None):
    T = proj.shape[0]
    c_in, c_out, c_sems = _carry_io(carry)

    def body(*refs):
        qkv_ref, sinks_ref = refs[:2]
        ci_refs = refs[2:2 + len(c_in)]
        o_ref = refs[2 + len(c_in)]
        co_refs = refs[3 + len(c_in):3 + len(c_in) + len(c_out)]
        cs_refs = refs[3 + len(c_in) + len(c_out):]
        if carry is not None:
            carry.start(ci_refs, co_refs, cs_refs)

        def blk(n, z):
            outs = [None] * (N_Q_HEADS // 2)
            for h in range(N_KV_HEADS):
                p, _, _, _, v2, rows, _ = _attn_block(qkv_ref, sinks_ref, n, h)
                o = lax.dot_general(p.astype(BF), v2, _DIMS["NN"], preferred_element_type=F32)
                for g in range(GROUP):
                    hq = GROUP * h + g
                    piece = jnp.where(_lane_half((BLOCK, 128), h), o[g * BLOCK:(g + 1) * BLOCK], 0.0)
                    if hq % 2 != h:
                        piece = pltpu.roll(piece, HEAD_DIM, 1)
                    outs[hq // 2] = piece if outs[hq // 2] is None else outs[hq // 2] + piece
            for pb in range(N_Q_HEADS // 2):
                o_ref[rows, pb * 128:(pb + 1) * 128] = outs[pb].astype(BF)
            return z

        lax.fori_loop(0, T // BLOCK, blk, 0)
        if carry is not None:
            carry.finish(ci_refs, co_refs, cs_refs)

    res = pl.pallas_call(
        body, name="attn_fwd", grid=(1,),
        in_specs=[pl.BlockSpec((T, GLU_OFF), lambda i: (0, 0)), pl.BlockSpec(memory_space=pltpu.SMEM),
                  *[ANY] * len(c_in)],
        out_specs=[pl.BlockSpec((T, ATTN_WIDTH), lambda i: (0, 0)), *[ANY] * len(c_out)],
        out_shape=[_sds((T, ATTN_WIDTH), BF), *c_out], scratch_shapes=c_sems,
        compiler_params=_params(("arbitrary",)),
    )(proj, sinks, *c_in)
    return res[0], res[1:]


def _attn_bwd(proj, d_o, sinks, carry=None):
    T = proj.shape[0]
    c_in, c_out, c_sems = _carry_io(carry)

    def body(*refs):
        qkv_ref, do_ref, sinks_ref = refs[:3]
        ci_refs = refs[3:3 + len(c_in)]
        dqkv_ref, dsink_ref = refs[3 + len(c_in):5 + len(c_in)]
        co_refs = refs[5 + len(c_in):5 + len(c_in) + len(c_out)]
        dk_acc, dv_acc = refs[5 + len(c_in) + len(c_out):7 + len(c_in) + len(c_out)]
        cs_refs = refs[7 + len(c_in) + len(c_out):]
        if carry is not None:
            carry.start(ci_refs, co_refs, cs_refs)
        dsink_ref[...] = jnp.zeros_like(dsink_ref)
        dk_acc[...] = jnp.zeros_like(dk_acc)
        dv_acc[...] = jnp.zeros_like(dv_acc)

        def blk(n, carry):
            dqs = [None] * (N_Q_HEADS // 2)
            for h in range(N_KV_HEADS):
                p, psink, q4, k2, v2, rows, prev = _attn_block(qkv_ref, sinks_ref, n, h)
                dos = []
                for g in range(GROUP):
                    hq = GROUP * h + g
                    dos.append(_to_half(do_ref[rows, (hq // 2) * 128:(hq // 2 + 1) * 128].astype(F32), hq % 2, h))
                do4 = jnp.concatenate(dos, axis=0).astype(BF)
                dp = lax.dot_general(do4, v2, _DIMS["NT"], preferred_element_type=F32)
                delta = jnp.sum(p * dp, axis=-1, keepdims=True)
                ds = (p * (dp - delta) * (HEAD_DIM ** -0.5)).astype(BF)
                dsk = psink * delta
                for g in range(GROUP):
                    hq = GROUP * h + g
                    tot = -jnp.sum(dsk[g * BLOCK:(g + 1) * BLOCK], axis=0, keepdims=True)
                    lane = lax.broadcasted_iota(jnp.int32, (1, 128), 1)
                    dsink_ref[...] += jnp.where(lane == hq, tot, 0.0)
                dq = lax.dot_general(ds, k2, _DIMS["NN"], preferred_element_type=F32)
                dk = lax.dot_general(ds, q4, _DIMS["TN"], preferred_element_type=F32)
                dv = lax.dot_general(p.astype(BF), do4, _DIMS["TN"], preferred_element_type=F32)
                dk_acc[prev, :] += dk[:BLOCK]
                dk_acc[rows, :] += dk[BLOCK:]
                dv_acc[prev, :] += dv[:BLOCK]
                dv_acc[rows, :] += dv[BLOCK:]
                for g in range(GROUP):
                    hq = GROUP * h + g
                    piece = jnp.where(_lane_half((BLOCK, 128), h), dq[g * BLOCK:(g + 1) * BLOCK], 0.0)
                    if hq % 2 != h:
                        piece = pltpu.roll(piece, HEAD_DIM, 1)
                    dqs[hq // 2] = piece if dqs[hq // 2] is None else dqs[hq // 2] + piece
            for pb in range(N_Q_HEADS // 2):
                dqkv_ref[rows, pb * 128:(pb + 1) * 128] = dqs[pb].astype(BF)
            return carry

        lax.fori_loop(0, T // BLOCK, blk, 0)
        dqkv_ref[:, ATTN_WIDTH:ATTN_WIDTH + KV_WIDTH] = dk_acc[...].astype(BF)
        dqkv_ref[:, ATTN_WIDTH + KV_WIDTH:] = dv_acc[...].astype(BF)
        if carry is not None:
            carry.finish(ci_refs, co_refs, cs_refs)

    res = pl.pallas_call(
        body, name="attn_bwd", grid=(1,),
        in_specs=[pl.BlockSpec((T, GLU_OFF), lambda i: (0, 0)), pl.BlockSpec((T, ATTN_WIDTH), lambda i: (0, 0)),
                  pl.BlockSpec(memory_space=pltpu.SMEM), *[ANY] * len(c_in)],
        out_specs=[pl.BlockSpec((T, GLU_OFF), lambda i: (0, 0)), pl.BlockSpec((1, 128), lambda i: (0, 0)),
                   *[ANY] * len(c_out)],
        out_shape=[_sds((T, GLU_OFF), BF), _sds((1, 128), F32), *c_out],
        scratch_shapes=[pltpu.VMEM((T, KV_WIDTH), F32), pltpu.VMEM((T, KV_WIDTH), F32), *c_sems],
        compiler_params=_params(("arbitrary",)),
    )(proj, d_o, sinks, *c_in)
    return res[:2], res[2:]


CHUNK = 256
SUB = 32
WIN = CHUNK + 32
PAD_ROWS = SEQ + 2 * CONV_PAD
_GLU_SPECS = [pl.BlockSpec((SEQ, 256), functools.partial(lambda i, c: (0, c), c=GLU_OFF // 256 + c)) for c in range(4)]


def _glu_to_pad(a0, a1, b0, b1, zpad):
    C = CONV_CHANNELS
    zpad[0:CONV_PAD, :] = jnp.zeros((CONV_PAD, C), F32)
    zpad[CONV_PAD + SEQ:, :] = jnp.zeros((CONV_PAD, C), F32)
    zpad[CONV_PAD:CONV_PAD + SEQ, 0:256] = a0[...].astype(F32) * jax.nn.sigmoid(b0[...].astype(F32))
    zpad[CONV_PAD:CONV_PAD + SEQ, 256:C] = a1[...].astype(F32) * jax.nn.sigmoid(b1[...].astype(F32))


def _tap_windows(src, base, win):
    for b in range(8):
        win[b, 0:WIN - 8, :] = src[base + b:base + b + WIN - 8, :]


def _taps(win, w_ref, init, out, flip):
    def sub(si, carry):
        r0 = pl.multiple_of(si * SUB, SUB)
        acc = jnp.broadcast_to(init, (SUB, CONV_CHANNELS))
        for k in range(CONV_WIDTH):
            wk = (CONV_WIDTH - 1 - k) if flip else k
            acc = acc + w_ref[wk:wk + 1, :] * win[k % 8, pl.ds(r0 + 8 * (k // 8), SUB), :]
        out[pl.ds(r0, SUB), :] = acc
        return carry

    lax.fori_loop(0, CHUNK // SUB, sub, 0)


def _tap_grads(win, du, dwacc):
    def sub(si, carry):
        r0 = pl.multiple_of(si * SUB, SUB)
        d = du[pl.ds(r0, SUB), :]
        for k in range(CONV_WIDTH):
            p = d * win[k % 8, pl.ds(r0 + 8 * (k // 8), SUB), :]
            dwacc[8 * k:8 * k + 8, :] += (p[0:8] + p[8:16]) + (p[16:24] + p[24:32])
        return carry

    lax.fori_loop(0, CHUNK // SUB, sub, 0)


def _ln_parts(u):
    mu = jnp.mean(u, axis=-1, keepdims=True)
    xc = u - mu
    rstd = lax.rsqrt(jnp.mean(xc * xc, axis=-1, keepdims=True) + EPS)
    return xc * rstd, rstd


def _conv_fwd(proj, conv_w, conv_b, ln_g, ln_b, carry=None):
    T, C = proj.shape[0], CONV_CHANNELS
    vec = pl.BlockSpec((1, C), lambda i: (0, 0))
    c_in, c_out, c_sems = _carry_io(carry)

    def body(*refs):
        a0, a1, b0, b1, w_ref, cb_ref, g_ref, be_ref = refs[:8]
        ci_refs = refs[8:8 + len(c_in)]
        c_ref = refs[8 + len(c_in)]
        co_refs = refs[9 + len(c_in):9 + len(c_in) + len(c_out)]
        zpad, win, ubuf = refs[9 + len(c_in) + len(c_out):12 + len(c_in) + len(c_out)]
        cs_refs = refs[12 + len(c_in) + len(c_out):]
        if carry is not None:
            carry.start(ci_refs, co_refs, cs_refs)
        _glu_to_pad(a0, a1, b0, b1, zpad)
        for ci in range(T // CHUNK):
            _tap_windows(zpad, ci * CHUNK + CONV_PAD - (CONV_WIDTH - 1), win)
            _taps(win, w_ref, cb_ref[...], ubuf, False)
            xh, _ = _ln_parts(ubuf[...])
            ln = xh * g_ref[...] + be_ref[...]
            c_ref[ci * CHUNK:(ci + 1) * CHUNK, :] = (ln * jax.nn.sigmoid(ln)).astype(BF)
        if carry is not None:
            carry.finish(ci_refs, co_refs, cs_refs)

    res = pl.pallas_call(
        body, name="conv_fwd", grid=(1,),
        in_specs=[*_GLU_SPECS, pl.BlockSpec((CONV_PAD, C), lambda i: (0, 0)), vec, vec, vec, *[ANY] * len(c_in)],
        out_specs=[pl.BlockSpec((T, C), lambda i: (0, 0)), *[ANY] * len(c_out)],
        out_shape=[_sds((T, C), BF), *c_out],
        scratch_shapes=[pltpu.VMEM((PAD_ROWS, C), F32), pltpu.VMEM((8, WIN, C), F32), pltpu.VMEM((CHUNK, C), F32),
                        *c_sems],
        compiler_params=_params(("arbitrary",)),
    )(proj, proj, proj, proj, conv_w, conv_b, ln_g, ln_b, *c_in)
    return res[0], res[1:]


def _conv_bwd(proj, d_c, conv_w, conv_b, ln_g, ln_b, carry=None):
    T, C = proj.shape[0], CONV_CHANNELS
    vec = pl.BlockSpec((1, C), lambda i: (0, 0))
    wspec = pl.BlockSpec((CONV_PAD, C), lambda i: (0, 0))
    c_in, c_out, c_sems = _carry_io(carry)

    def body(*refs):
        a0, a1, b0, b1, dc_ref, w_ref, cb_ref, g_ref, be_ref = refs[:9]
        ci_refs = refs[9:9 + len(c_in)]
        o = 9 + len(c_in)
        dglu_ref, dw_ref, dcb_ref, dg_ref, dbe_ref = refs[o:o + 5]
        co_refs = refs[o + 5:o + 5 + len(c_out)]
        zpad, dupad, win, ubuf, dwacc = refs[o + 5 + len(c_out):o + 10 + len(c_out)]
        cs_refs = refs[o + 10 + len(c_out):]
        if carry is not None:
            carry.start(ci_refs, co_refs, cs_refs)
        _glu_to_pad(a0, a1, b0, b1, zpad)
        dupad[T:, :] = jnp.zeros((2 * CONV_PAD, C), F32)
        dwacc[...] = jnp.zeros_like(dwacc)
        dcb_ref[...] = jnp.zeros_like(dcb_ref)
        dg_ref[...] = jnp.zeros_like(dg_ref)
        dbe_ref[...] = jnp.zeros_like(dbe_ref)
        for ci in range(T // CHUNK):
            rows = slice(ci * CHUNK, (ci + 1) * CHUNK)
            _tap_windows(zpad, ci * CHUNK + CONV_PAD - (CONV_WIDTH - 1), win)
            _taps(win, w_ref, cb_ref[...], ubuf, False)
            xh, rstd = _ln_parts(ubuf[...])
            ln = xh * g_ref[...] + be_ref[...]
            sg = jax.nn.sigmoid(ln)
            dln = dc_ref[rows, :].astype(F32) * (sg * (1.0 + ln * (1.0 - sg)))
            dg_ref[...] += jnp.sum(dln * xh, axis=0, keepdims=True)
            dbe_ref[...] += jnp.sum(dln, axis=0, keepdims=True)
            dxh = dln * g_ref[...]
            du = rstd * (dxh - jnp.mean(dxh, axis=-1, keepdims=True)
                         - xh * jnp.mean(dxh * xh, axis=-1, keepdims=True))
            dupad[rows, :] = du
            dcb_ref[...] += jnp.sum(du, axis=0, keepdims=True)
            _tap_grads(win, dupad.at[rows, :], dwacc)
        for k in range(CONV_WIDTH):
            dw_ref[k:k + 1, :] = jnp.sum(dwacc[8 * k:8 * k + 8, :], axis=0, keepdims=True)
        dw_ref[CONV_WIDTH:, :] = jnp.zeros((CONV_PAD - CONV_WIDTH, C), F32)
        for ci in range(T // CHUNK):
            rows = slice(ci * CHUNK, (ci + 1) * CHUNK)
            _tap_windows(dupad, ci * CHUNK, win)
            _taps(win, w_ref, jnp.zeros((1, C), F32), ubuf, True)
            dz = ubuf[...]
            for half, (a, b) in enumerate(((a0, b0), (a1, b1))):
                sb = jax.nn.sigmoid(b[rows, :].astype(F32))
                dzh = dz[:, half * 256:(half + 1) * 256]
                dglu_ref[rows, half * 256:(half + 1) * 256] = (dzh * sb).astype(BF)
                dglu_ref[rows, C + half * 256:C + (half + 1) * 256] = (
                    dzh * a[rows, :].astype(F32) * sb * (1.0 - sb)).astype(BF)
        if carry is not None:
            carry.finish(ci_refs, co_refs, cs_refs)

    res = pl.pallas_call(
        body, name="conv_bwd", grid=(1,),
        in_specs=[*_GLU_SPECS, pl.BlockSpec((T, C), lambda i: (0, 0)), wspec, vec, vec, vec, *[ANY] * len(c_in)],
        out_specs=[pl.BlockSpec((T, 2 * C), lambda i: (0, 0)), wspec, vec, vec, vec, *[ANY] * len(c_out)],
        out_shape=[_sds((T, 2 * C), BF), _sds((CONV_PAD, C), F32), _sds((1, C), F32), _sds((1, C), F32),
                   _sds((1, C), F32), *c_out],
        scratch_shapes=[pltpu.VMEM((PAD_ROWS, C), F32), pltpu.VMEM((PAD_ROWS, C), F32), pltpu.VMEM((8, WIN, C), F32),
                        pltpu.VMEM((CHUNK, C), F32), pltpu.VMEM((8 * CONV_PAD, C), F32), *c_sems],
        compiler_params=_params(("arbitrary",)),
    )(proj, proj, proj, proj, d_c, conv_w, conv_b, ln_g, ln_b, *c_in)
    return res[:5], res[5:]


_GATE_BLK = GATE_OFF // 256


def _ffn_in_swiglu(h2, wf_t, carry=None):
    T, D = h2.shape
    tm, tn = 512, D_FF // 2
    nj, ni = D_FF // tn, T // tm
    c_in, c_out, c_sems = _carry_io(carry)

    def body(*refs):
        a_ref, bg_ref, bu_ref = refs[:3]
        ci_refs = refs[3:3 + len(c_in)]
        act_ref, g_ref, u_ref = refs[3 + len(c_in):6 + len(c_in)]
        co_refs = refs[6 + len(c_in):6 + len(c_in) + len(c_out)]
        cs_refs = refs[6 + len(c_in) + len(c_out):]
        j, i = pl.program_id(0), pl.program_id(1)
        if carry is not None:
            @pl.when((j == 0) & (i == 0))
            def _():
                carry.start(ci_refs, co_refs, cs_refs)
        a = a_ref[...]
        for c0, c1 in ((0, 768), (768, tn)):
            g = lax.dot_general(a, bg_ref[c0:c1, :], _DIMS["NT"], preferred_element_type=F32)
            u = lax.dot_general(a, bu_ref[c0:c1, :], _DIMS["NT"], preferred_element_type=F32)
            act_ref[:, c0:c1] = (g * jax.nn.sigmoid(g) * u).astype(BF)
            g_ref[:, c0:c1] = g.astype(BF)
            u_ref[:, c0:c1] = u.astype(BF)
        if carry is not None:
            @pl.when((j == nj - 1) & (i == ni - 1))
            def _():
                carry.finish(ci_refs, co_refs, cs_refs)

    t = pl.BlockSpec((tm, tn), lambda j, i: (i, j))
    res = pl.pallas_call(
        body, name="ffn_in_swiglu", grid=(nj, ni),
        in_specs=[pl.BlockSpec((tm, D), lambda j, i: (i, 0)), pl.BlockSpec((tn, D), lambda j, i: (j, 0)),
                  pl.BlockSpec((tn, D), lambda j, i: (nj + j, 0)), *[ANY] * len(c_in)],
        out_specs=[t, t, t, *[ANY] * len(c_out)], out_shape=[*[_sds((T, D_FF), BF)] * 3, *c_out],
        scratch_shapes=c_sems,
        compiler_params=_params(("arbitrary", "arbitrary")),
    )(h2, wf_t, wf_t, *c_in)
    return res[:3], res[3:]


def _proj_merge(o, c, wap_t, wcp_t, b_cp, proj):
    T, D = o.shape[0], wap_t.shape[0]
    tm, tg = 1024, 256
    nj = D // tg

    def body(o_ref, c_ref, wa_ref, wc_ref, b_ref, g0_ref, g1_ref, ya_ref, yc_ref, m_ref):
        ya = lax.dot_general(o_ref[...], wa_ref[...], _DIMS["NT"], preferred_element_type=F32)
        yc = lax.dot_general(c_ref[...], wc_ref[...], _DIMS["NT"], preferred_element_type=F32) + b_ref[...]
        ya_ref[...] = ya.astype(BF)
        yc_ref[...] = yc.astype(BF)
        m_ref[...] = (jax.nn.sigmoid(g0_ref[...].astype(F32)) * ya + jax.nn.sigmoid(g1_ref[...].astype(F32)) * yc).astype(BF)

    act = pl.BlockSpec((tm, o.shape[1]), lambda j, i: (i, 0))
    wgt = pl.BlockSpec((tg, o.shape[1]), lambda j, i: (j, 0))
    t = pl.BlockSpec((tm, tg), lambda j, i: (i, j))
    return pl.pallas_call(
        body, name="proj_merge", grid=(nj, T // tm),
        in_specs=[act, act, wgt, wgt, pl.BlockSpec((1, tg), lambda j, i: (0, j)),
                  pl.BlockSpec((tm, tg), lambda j, i: (i, _GATE_BLK + j)),
                  pl.BlockSpec((tm, tg), lambda j, i: (i, _GATE_BLK + nj + j))],
        out_specs=[t, t, t], out_shape=[_sds((T, D), BF)] * 3,
        compiler_params=_params(("arbitrary", "arbitrary")),
    )(o, c, wap_t, wcp_t, b_cp, proj, proj)


def _proj_in_dw(segs, h):
    T, D = h.shape
    tb = 256
    nblk = [seg.shape[1] // tb for seg in segs]
    starts = [sum(nblk[:q]) for q in range(len(segs))]
    n_seg = len(segs)

    def body(*refs):
        seg_refs, h_ref, o_ref, cs_ref = refs[:n_seg], refs[n_seg], refs[n_seg + 1], refs[n_seg + 2]
        i = pl.program_id(0)
        for seg_ref, st, nb in zip(seg_refs, starts, nblk):
            @pl.when((i >= st) & (i < st + nb))
            def _(seg_ref=seg_ref):
                a = seg_ref[...]
                o_ref[...] = lax.dot_general(a, h_ref[...], _DIMS["TN"], preferred_element_type=F32).astype(BF)
                cs_ref[...] = jnp.sum(a.astype(F32), axis=0, keepdims=True)

    in_specs = [pl.BlockSpec((T, tb), functools.partial(lambda i, st, nb: (0, jnp.clip(i - st, 0, nb - 1)), st=st, nb=nb))
                for st, nb in zip(starts, nblk)]
    return pl.pallas_call(
        body, name="proj_in_dw", grid=(sum(nblk),),
        in_specs=[*in_specs, pl.BlockSpec((T, D), lambda i: (0, 0))],
        out_specs=[pl.BlockSpec((tb, D), lambda i: (i, 0)), pl.BlockSpec((1, tb), lambda i: (0, i))],
        out_shape=[_sds((sum(nblk) * tb, D), BF), _sds((1, sum(nblk) * tb), F32)],
        compiler_params=_params(("arbitrary",)),
    )(*segs, h)


def _local_step(x, target, small, wi_t, conv_w, plan):
    T, D = x.shape
    tm = 1024

    def carried(call, res, carry):
        if carry is None:
            return res
        outs, got = res
        plan.done(call, got)
        return outs

    h, r1 = _rms_fwd("rms_mix", x, small["g_mix_norm"])

    def ep_add(acc, ex, outs, ids, scr):
        outs[0][...] = acc + ex[0][...]

    tn_in = IN_WIDTH // 3
    carry = plan.carry("proj_in")
    def ep_bias_bf16(acc, ex, outs, ids, scr):
        outs[0][...] = (acc + ex[0][...]).astype(BF)

    proj, = carried("proj_in", _matmul("proj_in", [h], wi_t, "NT", m=T, n=IN_WIDTH, tm=tm, tn=tn_in,
                                       epilogue=ep_bias_bf16, extra=[(small["b_in"], _row(tn_in))],
                                       outs=[(_sds((T, IN_WIDTH), BF), _tile(tm, tn_in))], carry=carry), carry)
    plan.launch("gather_ffn", after=proj)
    o, got = _attn_fwd(proj, small["sinks"], carry=plan.carry("attn_fwd"))
    plan.done("attn_fwd", got)
    c, got = _conv_fwd(proj, conv_w, small["conv_b"], small["ln_g"], small["ln_b"], carry=plan.carry("conv_fwd"))
    plan.done("conv_fwd", got)
    wap_t, wcp_t, w_out = plan.weight("w_attn_proj"), plan.weight("w_conv_proj"), plan.weight("w_out")
    ya, yc, merged = _proj_merge(o, c, wap_t, wcp_t, small["b_conv_proj"], proj)

    tg = 256
    gate_specs = [pl.BlockSpec((tm, tg), lambda j, i, k: (i, _GATE_BLK + j)),
                  pl.BlockSpec((tm, tg), lambda j, i, k: (i, _GATE_BLK + D // tg + j))]

    def ep_residual_rms(acc, ex, outs, ids, scr):
        x2v = acc + ex[0][...]
        r = lax.rsqrt(jnp.mean(x2v * x2v, axis=-1, keepdims=True) + EPS)
        outs[0][...] = x2v
        outs[1][...] = (x2v * r * ex[1][...]).astype(BF)
        outs[2][...] = r

    carry = plan.carry("out_proj")
    x2, h2, r2 = carried("out_proj", _matmul(
        "out_proj_rms", [merged], w_out, "NN", m=T, n=D, tm=512, tn=D, epilogue=ep_residual_rms,
        extra=[(x, _tile(512, D)), (small["g_ffn_norm"], _row(D))],
        outs=[(_sds((T, D), F32), _tile(512, D)), (_sds((T, D), BF), _tile(512, D)),
              (_sds((T, 1), F32), pl.BlockSpec((512, 1), lambda j, i, k: (i, 0)))], carry=carry), carry)
    plan.launch("gather_down", after=x2)
    wf_t = plan.weight("w_ffn_in")
    (act, gate, up), got = _ffn_in_swiglu(h2, wf_t, carry=plan.carry("ffn_in_swiglu"))
    plan.done("ffn_in_swiglu", got)
    w_down = plan.weight("w_ffn_down")
    def ep_residual_loss(acc, ex, outs, ids, scr):
        dx, dg, part = _loss_head(acc + ex[0][...], ex[1][...], ex[2][...])
        outs[0][...] = dx
        outs[1][...] = dx.astype(BF)
        _accumulate_rows(outs[2], dg, ids[1] == 0)
        _accumulate_rows(outs[3], part, ids[1] == 0)

    dx3, dx3_b, dg_final, loss = _matmul(
        "ffn_down_loss", [act], w_down, "NN", m=T, n=D, tm=512, tn=D, epilogue=ep_residual_loss,
        extra=[(x2, _tile(512, D)), (small["g_final"], _row(D)), (target, _tile(512, D))],
        outs=[(_sds((T, D), F32), _tile(512, D)), (_sds((T, D), BF), _tile(512, D)), (_sds((1, D), F32), _row(D)),
              (_sds((1, 1), F32), pl.BlockSpec((1, 1), lambda j, i, k: (0, 0)))])

    tn_ff = D_FF // 2

    def ep_swiglu_bwd(acc, ex, outs, ids, scr):
        g, u = ex[0][...].astype(F32), ex[1][...].astype(F32)
        sg = jax.nn.sigmoid(g)
        outs[0][...] = (acc * u * sg * (1.0 + g * (1.0 - sg))).astype(BF)
        outs[1][...] = (acc * g * sg).astype(BF)

    dgate, dup = _matmul(
        "ffn_down_bwd", [dx3_b], w_down, "NT", m=T, n=D_FF, tm=512, tn=tn_ff, epilogue=ep_swiglu_bwd,
        extra=[(gate, _tile(512, tn_ff)), (up, _tile(512, tn_ff))],
        outs=[(_sds((T, D_FF), BF), _tile(512, tn_ff)), (_sds((T, D_FF), BF), _tile(512, tn_ff))])

    def dw(name, a, b, rows, cols, row_off=0, alias=None, total_rows=None, colsum=False):
        total_rows = rows if total_rows is None else total_rows
        tmw = rows if rows <= 1024 else D_FF // 2
        blk, rem = divmod(row_off, tmw)
        assert rem == 0

        def ep(acc, ex, outs, ids, scr):
            outs[0][...] = acc.astype(BF)
            if colsum:
                outs[1][...] = jnp.sum(ex[0][...].astype(F32), axis=0, keepdims=True)

        outs = [(_sds((total_rows, cols), BF), pl.BlockSpec((tmw, cols), lambda j, i, k: (blk + i, j)))]
        extra = []
        if colsum:
            extra = [(a, pl.BlockSpec((T, tmw), lambda j, i, k: (0, i)))]
            outs.append((_sds((1, rows), F32), pl.BlockSpec((1, tmw), lambda j, i, k: (0, i))))
        carry = plan.carry(name)
        res = carried(name, _matmul(name, [a], b, "TN", m=rows, n=cols, tm=tmw, tn=cols, epilogue=ep, extra=extra,
                                    outs=outs, alias=None if alias is None else (alias, 0), carry=carry), carry)
        return res if colsum else res[0]

    plan.grad_ready(dict(w_ffn_down=dw("ffn_down_dw", act, dx3_b, D_FF, D)))

    def ep_rms_bwd(acc, ex, outs, ids, scr):
        dx, dg = _rms_bwd(acc, ex[0][...], ex[1][...], ex[2][...])
        dx = ex[3][...] + dx
        outs[0][...] = dx
        outs[1][...] = dx.astype(BF)
        _accumulate_rows(outs[2], dg, ids[1] == 0)

    def rms_bwd_io(tm_, xin, r, g, dres):
        return dict(
            extra=[(xin, _tile(tm_, D)), (r, pl.BlockSpec((tm_, 1), lambda j, i, k: (i, 0))), (g, _row(D)),
                   (dres, _tile(tm_, D))],
            outs=[(_sds((T, D), F32), _tile(tm_, D)), (_sds((T, D), BF), _tile(tm_, D)), (_sds((1, D), F32), _row(D))])

    carry = plan.carry("ffn_in_bwd")
    dx2, dx2_b, dg_ffn = carried(
        "ffn_in_bwd",
        _matmul("ffn_in_bwd", [dgate, dup], wf_t, "NN", m=T, n=D, tm=tm, tn=D, tk=D_FF // 2, epilogue=ep_rms_bwd,
                carry=carry, **rms_bwd_io(tm, x2, r2, small["g_ffn_norm"], dx3)), carry)
    plan.launch("send_down")
    gwf_t = dw("ffn_in_dw_gate", dgate, h2, D_FF, D, total_rows=2 * D_FF)
    gwf_t = dw("ffn_in_dw_up", dup, h2, D_FF, D, row_off=D_FF, alias=gwf_t, total_rows=2 * D_FF)
    plan.grad_ready(dict(w_ffn_in=gwf_t))

    def ep_merge_bwd(acc, ex, outs, ids, scr):
        s0 = jax.nn.sigmoid(ex[2][...].astype(F32))
        s1 = jax.nn.sigmoid(ex[3][...].astype(F32))
        outs[0][...] = (acc * s0).astype(BF)
        outs[1][...] = (acc * s1).astype(BF)
        outs[2][...] = (acc * ex[0][...].astype(F32) * s0 * (1.0 - s0)).astype(BF)
        outs[3][...] = (acc * ex[1][...].astype(F32) * s1 * (1.0 - s1)).astype(BF)

    carry = plan.carry("out_proj_bwd_merge")
    dya, dyc, dg0, dg1 = carried(
        "out_proj_bwd_merge",
        _matmul("out_proj_bwd_merge", [dx2_b], w_out, "NT", m=T, n=D, tm=tm, tn=tg, epilogue=ep_merge_bwd,
                extra=[(ya, _tile(tm, tg)), (yc, _tile(tm, tg)), (proj, gate_specs[0]), (proj, gate_specs[1])],
                outs=[(_sds((T, D), BF), _tile(tm, tg))] * 4, carry=carry), carry)
    plan.launch("send_ffn")
    gw_out = dw("out_proj_dw", merged, dx2_b, D, D)
    d_o, = _matmul("attn_proj_bwd", [dya], wap_t, "NN", m=T, n=ATTN_WIDTH, tm=tm, tn=ATTN_WIDTH,
                   epilogue=_store(BF), outs=[(_sds((T, ATTN_WIDTH), BF), _tile(tm, ATTN_WIDTH))])
    d_c, = _matmul("conv_proj_bwd", [dyc], wcp_t, "NN", m=T, n=CONV_CHANNELS, tm=tm, tn=CONV_CHANNELS,
                   epilogue=_store(BF), outs=[(_sds((T, CONV_CHANNELS), BF), _tile(tm, CONV_CHANNELS))])
    gwap_t = dw("attn_proj_dw", dya, o, D, ATTN_WIDTH)
    gwcp_t, db_cp = dw("conv_proj_dw", dyc, c, D, CONV_CHANNELS, colsum=True)
    plan.grad_ready(dict(w_out=gw_out, w_attn_proj=gwap_t, w_conv_proj=gwcp_t))
    (dglu, dcw, dcb, dlng, dlnb), got = _conv_bwd(proj, d_c, conv_w, small["conv_b"], small["ln_g"], small["ln_b"],
                                                  carry=plan.carry("conv_bwd"))
    plan.done("conv_bwd", got)
    plan.launch("send_mix")
    (dqkv, dsinks), got = _attn_bwd(proj, d_o, small["sinks"], carry=plan.carry("attn_bwd"))
    plan.done("attn_bwd", got)

    segs = [dqkv, dglu, dg0, dg1]
    gwi_t, db_in = _proj_in_dw(segs, h)
    plan.grad_ready(dict(w_in=gwi_t))
    plan.alone("swap_inp")
    plan.launch("send_inp")
    carry = plan.carry("proj_in_bwd")
    dx, _, dg_mix = carried(
        "proj_in_bwd",
        _matmul("proj_in_bwd", segs, wi_t, "NN", m=T, n=D, tm=512, tn=D, epilogue=ep_rms_bwd, carry=carry,
                **rms_bwd_io(512, x, r1, small["g_mix_norm"], dx2)), carry)

    parts = dict(g_mix_norm=dg_mix, b_in=db_in, sinks=dsinks, conv_w=dcw, conv_b=dcb, ln_g=dlng, ln_b=dlnb,
                 b_conv_proj=db_cp, g_ffn_norm=dg_ffn, g_final=dg_final, loss=loss)
    return dx, parts


def _place():
    x, y, c = lax.axis_index("x"), lax.axis_index("y"), lax.axis_index("c")
    return x, y, c, [(1 - x, y), (x, 1 - y), (1 - x, 1 - y)]


def _gather_copies(x_refs, out_refs, rows_per, send_sems, recv_sems, local_sems):
    x, y, c, chips = _place()
    me, sibling = (x, y, c), (x, y, 1 - c)

    def rows(a, px, py, pc):
        return out_refs[a].at[pl.ds((4 * px + 2 * py + pc) * rows_per[a], rows_per[a])]

    def copy(a, k, block, to, src=None):
        return pltpu.make_async_remote_copy(
            src_ref=rows(a, *block) if src is None else src, dst_ref=rows(a, *block),
            send_sem=send_sems.at[7 * a + k], recv_sem=recv_sems.at[7 * a + k], device_id=to, device_id_type=MESH)

    def local(a):
        return pltpu.make_async_copy(x_refs[a], rows(a, *me), local_sems.at[a])

    def first(a):
        return [copy(a, 0, me, sibling, src=x_refs[a])] + [copy(a, 1 + j, me, (*chip, c), src=x_refs[a])
                                                          for j, chip in enumerate(chips)]

    def arrive(a, j):
        return copy(a, 1 + j, (*chips[j], c), me)

    def passed(a, j):
        return copy(a, 4 + j, (*chips[j], c), sibling)

    def from_sibling(a):
        return [copy(a, 0, sibling, me)] + [copy(a, 4 + j, (*chip, 1 - c), me) for j, chip in enumerate(chips)]

    return len(x_refs), local, first, arrive, passed, from_sibling


def _gather_start(*refs):
    n, local, first, _, _, _ = _gather_copies(*refs)
    for a in range(n):
        local(a).start()
        for cp in first(a):
            cp.start()


def _gather_finish(*refs):
    n, local, first, arrive, passed, from_sibling = _gather_copies(*refs)
    for a in range(n):
        for j in range(3):
            arrive(a, j).wait_recv()
            passed(a, j).start()
    for a in range(n):
        for cp in from_sibling(a):
            cp.wait_recv()
    for a in range(n):
        for cp in first(a) + [passed(a, j) for j in range(3)]:
            cp.wait_send()
        local(a).wait()


def _gather_peers():
    x, y, c, chips = _place()
    return [(x, y, 1 - c)] + [(*chip, c) for chip in chips]


def _gather_sems(n):
    return [pltpu.SemaphoreType.DMA((7 * n,)), pltpu.SemaphoreType.DMA((7 * n,)), pltpu.SemaphoreType.DMA((n,))]


def _gather_carry(shards):
    rows_per = [s.shape[0] for s in shards]
    return _Carry(shards, [_sds((N_DEV * s.shape[0],) + s.shape[1:], s.dtype) for s in shards],
                  _gather_sems(len(shards)),
                  lambda ins, outs, sems: _gather_start(ins, outs, rows_per, *sems),
                  lambda ins, outs, sems: _gather_finish(ins, outs, rows_per, *sems), _gather_peers)


def _first_gather(shards):
    n = len(shards)
    rows_per = [s.shape[0] for s in shards]

    def body(*refs):
        x_refs, out_refs = refs[:n], refs[n:2 * n]
        send_sems, recv_sems, local_sems = refs[2 * n:]
        x, y, c, chips = _place()
        me, sibling = (x, y, c), (x, y, 1 - c)
        near_x, near_y, far = (*chips[0], c), (*chips[1], c), (*chips[2], c)

        def rows(a, dev, part):
            h = rows_per[a] // 2
            lo, size = {"all": (0, 2 * h), "low": (0, h), "high": (h, h)}[part]
            return out_refs[a].at[pl.ds((4 * dev[0] + 2 * dev[1] + dev[2]) * rows_per[a] + lo, size)]

        def copy(a, k, block, part, to, src=None):
            return pltpu.make_async_remote_copy(
                src_ref=rows(a, block, part) if src is None else src, dst_ref=rows(a, block, part),
                send_sem=send_sems.at[9 * a + k], recv_sem=recv_sems.at[9 * a + k], device_id=to, device_id_type=MESH)

        other = lambda dev: (dev[0], dev[1], 1 - c)
        sent = []
        for a in range(n):
            pltpu.make_async_copy(x_refs[a], rows(a, me, "all"), local_sems.at[a]).start()
            sent += [copy(a, 0, me, "all", sibling, src=x_refs[a]), copy(a, 1, me, "all", near_x, src=x_refs[a]),
                     copy(a, 2, me, "all", near_y, src=x_refs[a])]
        for cp in sent:
            cp.start()
        for a in range(n):
            copy(a, 1, near_x, "all", me).wait_recv()
            copy(a, 2, near_y, "all", me).wait_recv()
            passed = [copy(a, 3, near_y, "high", near_x), copy(a, 4, near_x, "low", near_y),
                      copy(a, 5, near_x, "all", sibling), copy(a, 6, near_y, "all", sibling)]
            for cp in passed:
                cp.start()
            sent += passed
        for a in range(n):
            copy(a, 3, far, "high", me).wait_recv()
            copy(a, 4, far, "low", me).wait_recv()
            passed = [copy(a, 7, far, "high", sibling), copy(a, 8, far, "low", sibling)]
            for cp in passed:
                cp.start()
            sent += passed
        for a in range(n):
            copy(a, 0, sibling, "all", me).wait_recv()
            copy(a, 5, other(near_x), "all", me).wait_recv()
            copy(a, 6, other(near_y), "all", me).wait_recv()
            copy(a, 7, other(far), "high", me).wait_recv()
            copy(a, 8, other(far), "low", me).wait_recv()
        for cp in sent:
            cp.wait_send()
        for a in range(n):
            pltpu.make_async_copy(x_refs[a], rows(a, me, "all"), local_sems.at[a]).wait()

    return pl.pallas_call(
        body, name="weights_first_gather", in_specs=[ANY] * n, out_specs=[ANY] * n,
        out_shape=[_sds((N_DEV * s.shape[0],) + s.shape[1:], s.dtype) for s in shards],
        scratch_shapes=[pltpu.SemaphoreType.DMA((9 * n,)), pltpu.SemaphoreType.DMA((9 * n,)),
                        pltpu.SemaphoreType.DMA((n,))],
    )(*shards)


def _swap_carry(grads):
    n = len(grads)

    def copies(g_refs, out_refs, sems):
        send_sems, recv_sems = sems
        x, y, c, _ = _place()
        return [pltpu.make_async_remote_copy(
            src_ref=g_refs[a].at[2 * p + 1 - c], dst_ref=out_refs[a].at[p],
            send_sem=send_sems.at[4 * a + p], recv_sem=recv_sems.at[4 * a + p],
            device_id=(x, y, 1 - c), device_id_type=MESH) for a in range(n) for p in range(4)]

    def start(ins, outs, sems):
        for cp in copies(ins, outs, sems):
            cp.start()

    def finish(ins, outs, sems):
        for cp in copies(ins, outs, sems):
            cp.wait()

    def peers():
        x, y, c, _ = _place()
        return [(x, y, 1 - c)]

    return _Carry(grads, [_sds((4,) + g.shape[1:], g.dtype) for g in grads],
                  [pltpu.SemaphoreType.DMA((4 * n,)), pltpu.SemaphoreType.DMA((4 * n,))], start, finish, peers)


def _join(carries):
    carries = [c for c in carries if c is not None]
    if not carries:
        return None
    n_in = [len(c.arrays) for c in carries]
    n_out = [len(c.out_shapes) for c in carries]
    n_sem = [len(c.sems) for c in carries]

    def parts(refs, counts):
        cuts = [sum(counts[:q]) for q in range(len(counts) + 1)]
        return [refs[cuts[q]:cuts[q + 1]] for q in range(len(counts))]

    def start(ins, outs, sems):
        for c, i, o, s in zip(carries, parts(ins, n_in), parts(outs, n_out), parts(sems, n_sem)):
            c.start(i, o, s)

    def finish(ins, outs, sems):
        for c, i, o, s in zip(carries, parts(ins, n_in), parts(outs, n_out), parts(sems, n_sem)):
            c.finish(i, o, s)

    return _Carry([a for c in carries for a in c.arrays], [o for c in carries for o in c.out_shapes],
                  [s for c in carries for s in c.sems], start, finish)


def _run_carry(name, carry):
    n_in, n_out = len(carry.arrays), len(carry.out_shapes)

    def body(*refs):
        carry.start(refs[:n_in], refs[n_in:n_in + n_out], refs[n_in + n_out:])
        carry.finish(refs[:n_in], refs[n_in:n_in + n_out], refs[n_in + n_out:])

    return pl.pallas_call(body, name=name, in_specs=[ANY] * n_in, out_specs=[ANY] * n_out,
                          out_shape=carry.out_shapes, scratch_shapes=carry.sems)(*carry.arrays)


def _run_carry_async(name, carry, collective_id):
    ins = [jax.new_ref(a, memory_space=pltpu.MemorySpace.HBM) for a in carry.arrays]
    outs = [jax.empty_ref(o, memory_space=pltpu.MemorySpace.HBM) for o in carry.out_shapes]

    @pl.kernel(mesh=plsc.ScalarSubcoreMesh(axis_name="sequencer", num_cores=1), name=name,
               scratch_types=tuple(carry.sems), compiler_params=pltpu.CompilerParams(collective_id=collective_id))
    def launch(*sems):
        barrier = pltpu.get_barrier_semaphore()
        peers = carry.peers()
        for peer in peers:
            pl.semaphore_signal(barrier, inc=1, device_id=peer, device_id_type=MESH)
        pl.semaphore_wait(barrier, len(peers))
        carry.start(ins, outs, sems)
        carry.finish(ins, outs, sems)

    launch()
    return [o[...] for o in outs]


def _chip_sum(name, g, got, c):
    _, rows, cols = g.shape

    def body(c_ref, g_ref, got_ref, o_ref):
        o_ref[...] = (g_ref[...].astype(F32) + got_ref[...].astype(F32)).astype(BF)

    return pl.pallas_call(
        body, name=name,
        grid_spec=pltpu.PrefetchScalarGridSpec(
            num_scalar_prefetch=1, grid=(4,),
            in_specs=[pl.BlockSpec((1, rows, cols), lambda p, c_ref: (2 * p + c_ref[0], 0, 0)),
                      pl.BlockSpec((1, rows, cols), lambda p, c_ref: (p, 0, 0))],
            out_specs=pl.BlockSpec((1, rows, cols), lambda p, c_ref: (p, 0, 0))),
        out_shape=_sds((4, rows, cols), BF),
        compiler_params=_params(("arbitrary",)),
    )(c, g, got)


def _send_carry(sums, ks):
    n, nk = len(sums), len(ks)

    def copies(s_refs, out_refs, sems):
        send_sems, recv_sems = sems
        x, y, c, chips = _place()
        return [pltpu.make_async_remote_copy(
            src_ref=s_refs[a].at[2 * chips[k][0] + chips[k][1]], dst_ref=out_refs[a].at[q],
            send_sem=send_sems.at[nk * a + q], recv_sem=recv_sems.at[nk * a + q],
            device_id=(*chips[k], c), device_id_type=MESH) for a in range(n) for q, k in enumerate(ks)]

    def start(ins, outs, sems):
        for cp in copies(ins, outs, sems):
            cp.start()

    def finish(ins, outs, sems):
        for cp in copies(ins, outs, sems):
            cp.wait()

    def peers():
        x, y, c, chips = _place()
        return [(*chips[k], c) for k in ks]

    return _Carry(sums, [_sds((nk,) + s.shape[1:], s.dtype) for s in sums],
                  [pltpu.SemaphoreType.DMA((nk * n,)), pltpu.SemaphoreType.DMA((nk * n,))], start, finish, peers)


def _grad_total(name, g, got, got3, ids):
    _, rows, cols = g.shape
    n3 = len(got3)

    def body(ids_ref, g_ref, got_ref, *rest):
        o_ref = rest[n3]
        tot = g_ref[0].astype(F32) + got_ref[0].astype(F32)
        for r_ref in rest[:n3]:
            for q in range(r_ref.shape[0]):
                tot = tot + r_ref[q].astype(F32)
        o_ref[...] = tot

    return pl.pallas_call(
        body, name=name,
        grid_spec=pltpu.PrefetchScalarGridSpec(
            num_scalar_prefetch=1, grid=(1,),
            in_specs=[pl.BlockSpec((1, rows, cols), lambda i, ids_ref: (ids_ref[0], 0, 0)),
                      pl.BlockSpec((1, rows, cols), lambda i, ids_ref: (ids_ref[1], 0, 0)),
                      *[pl.BlockSpec(r.shape, lambda i, ids_ref: (0, 0, 0)) for r in got3]],
            out_specs=pl.BlockSpec((rows, cols), lambda i, ids_ref: (0, 0))),
        out_shape=_sds((rows, cols), F32),
        compiler_params=_params(("arbitrary",)),
    )(ids, g, got, *got3)


def _adam_math(w, g, m, v):
    m = ADAM_B1 * m + (1.0 - ADAM_B1) * g
    v = ADAM_B2 * v + (1.0 - ADAM_B2) * (g * g)
    m_hat = m / (1.0 - ADAM_B1 ** ADAM_STEP)
    v_hat = v / (1.0 - ADAM_B2 ** ADAM_STEP)
    delta = -ADAM_LR * (m_hat / (jnp.sqrt(v_hat) + ADAM_EPS) + ADAM_WD * w)
    return delta, m, v


def _adamw(name, w, g, m, v):
    rows, cols = w.shape
    tr = 256 if rows % 256 == 0 else rows

    def body(w_ref, g_ref, m_ref, v_ref, d_ref, nm_ref, nv_ref):
        d_ref[...], nm_ref[...], nv_ref[...] = _adam_math(w_ref[...], g_ref[...], m_ref[...], v_ref[...])

    t = pl.BlockSpec((tr, cols), lambda i: (i, 0))
    return pl.pallas_call(
        body, name=name, grid=(rows // tr,), in_specs=[t] * 4, out_specs=[t] * 3,
        out_shape=[_sds((rows, cols), F32)] * 3, compiler_params=_params(("arbitrary",)),
    )(w, g, m, v)


SMALL_NAMES = ["g_mix_norm", "b_in", "sinks", "conv_b", "ln_g", "ln_b", "b_conv_proj", "g_ffn_norm", "g_final"]
_PACK_ROWS = 32


def _small_pack(parts):
    C = CONV_CHANNELS
    part_list = [parts["g_mix_norm"], parts["b_in"], parts["sinks"], parts["conv_b"], parts["ln_g"], parts["ln_b"],
                 parts["b_conv_proj"], parts["g_ffn_norm"], parts["g_final"], parts["loss"], parts["conv_w"]]

    def body(p_mix, p_b, p_sink, p_cb, p_lg, p_lb, p_bcp, p_ffn, p_fin, p_loss, p_cw, pack):
        pack[...] = jnp.zeros_like(pack)
        pack[0:1, :] = p_mix[...]
        pack[1:2, 0:GLU_OFF] = p_b[:, 0:GLU_OFF]
        pack[2:3, :] = p_b[:, GLU_OFF:GATE_OFF]
        pack[3:4, :] = p_b[:, GATE_OFF:GATE_OFF + D_MODEL]
        pack[4:5, :] = p_b[:, GATE_OFF + D_MODEL:]
        pack[5:6, 0:128] = p_sink[...]
        pack[6:7, 0:C] = p_cb[...]
        pack[6:7, C:2 * C] = p_lg[...]
        pack[7:8, 0:C] = p_lb[...]
        pack[8:9, :] = p_bcp[...]
        pack[9:10, :] = p_ffn[...]
        pack[10:11, :] = p_fin[...]
        pack[11:12, 0:128] = jnp.broadcast_to(p_loss[...], (1, 128))
        pack[12:28, 0:C] = p_cw[0:16, :]
        pack[12:28, C:2 * C] = p_cw[16:32, :]

    vm = pl.BlockSpec(memory_space=pltpu.VMEM)
    return pl.pallas_call(body, name="small_pack", in_specs=[vm] * len(part_list), out_specs=vm,
                          out_shape=_sds((_PACK_ROWS, D_MODEL), F32))(*part_list)


def _small_adamw(gathered, small_w, small_m, small_v):
    C = CONV_CHANNELS
    names = SMALL_NAMES
    widths = [small_w[k].shape[1] for k in names]
    n_small = len(names)

    def body(*refs):
        tot_ref = refs[0]
        w_refs = refs[1:1 + n_small]
        m_refs = refs[1 + n_small:1 + 2 * n_small]
        v_refs = refs[1 + 2 * n_small:1 + 3 * n_small]
        o = 1 + 3 * n_small
        loss_ref, cw_ref = refs[o], refs[o + 1]
        out_refs = refs[o + 2:o + 2 + 4 * n_small]
        tot = tot_ref[0:_PACK_ROWS, :]
        for d in range(1, N_DEV):
            tot = tot + tot_ref[d * _PACK_ROWS:(d + 1) * _PACK_ROWS, :]
        loss_ref[...] = tot[11:12, 0:1]
        cw_ref[0:16, :] = tot[12:28, 0:C]
        cw_ref[16:32, :] = tot[12:28, C:2 * C]
        grads = dict(
            g_mix_norm=tot[0:1, :],
            b_in=jnp.concatenate([tot[1:2, 0:GLU_OFF], tot[2:3, :], tot[3:4, :], tot[4:5, :]], axis=1),
            sinks=tot[5:6, 0:N_Q_HEADS], conv_b=tot[6:7, 0:C], ln_g=tot[6:7, C:2 * C], ln_b=tot[7:8, 0:C],
            b_conv_proj=tot[8:9, :], g_ffn_norm=tot[9:10, :], g_final=tot[10:11, :])
        for s, k in enumerate(names):
            g = grads[k]
            d, nm, nv = _adam_math(w_refs[s][...], g, m_refs[s][...], v_refs[s][...])
            out_refs[4 * s][...] = g
            out_refs[4 * s + 1][...] = d
            out_refs[4 * s + 2][...] = nm
            out_refs[4 * s + 3][...] = nv

    vm = pl.BlockSpec(memory_space=pltpu.VMEM)
    args = [gathered, *[small_w[k] for k in names], *[small_m[k] for k in names], *[small_v[k] for k in names]]
    out_shape = [_sds((1, 1), F32), _sds((CONV_PAD, C), F32)]
    for wd in widths:
        out_shape += [_sds((1, wd), F32)] * 4
    res = pl.pallas_call(
        body, name="small_adamw",
        in_specs=[vm] * len(args), out_specs=[vm] * len(out_shape), out_shape=out_shape,
        compiler_params=pltpu.CompilerParams(vmem_limit_bytes=VMEM_LIMIT_BYTES),
    )(*args)
    return res[0], res[1], {k: res[2 + 4 * s:6 + 4 * s] for s, k in enumerate(names)}


BIG = dict(w_in=True, w_attn_proj=True, w_conv_proj=True, w_out=False, w_ffn_in=True, w_ffn_down=False)
WEIGHT_NAMES = ["g_mix_norm", "w_in", "b_in", "sinks", "conv_w", "conv_b", "ln_g", "ln_b", "w_attn_proj",
                "w_conv_proj", "b_conv_proj", "w_out", "g_ffn_norm", "w_ffn_in", "w_ffn_down", "g_final"]


class _Plan:
    GROUPS = dict(down=["w_ffn_down"], ffn=["w_ffn_in"], mix=["w_out", "w_attn_proj", "w_conv_proj"], inp=["w_in"])
    ALL = (0, 1, 2)
    RIDES = dict(
        gather_mix=[("gather", ["w_attn_proj", "w_conv_proj", "w_out"])], gather_ffn=[("gather", ["w_ffn_in"])],
        gather_down=[("gather", ["w_ffn_down"])],
        ffn_in_bwd=[("swap", "down")], send_down=[("send", "down", ALL)],
        out_proj_bwd_merge=[("swap", "ffn")], send_ffn=[("send", "ffn", ALL)],
        conv_bwd=[("swap", "mix")], send_mix=[("send", "mix", ALL)],
        swap_inp=[("swap", "inp")], send_inp=[("send", "inp", ALL)])
    ASYNC = dict(gather_mix=1, gather_ffn=2, gather_down=3, send_down=4, send_ffn=5, send_mix=6, send_inp=7)

    def __init__(self, shards, c1):
        self.shards, self.c1 = shards, c1
        self.full, self.slots, self.got, self.sums, self.got3 = {}, {}, {}, {}, {}

    def weight(self, name):
        return self.full[name]

    def grad_ready(self, grads):
        for k, g in grads.items():
            self.slots[k] = g.reshape(N_DEV, g.shape[0] // N_DEV, g.shape[1])

    def _one(self, kind, what, ks=None):
        if kind == "gather":
            return _gather_carry([self.shards[k] for k in what])
        names = self.GROUPS[what]
        if kind == "swap":
            return _swap_carry([self.slots[k] for k in names])
        return _send_carry([self.sums[k] for k in names], ks)

    def carry(self, call):
        return _join([self._one(*ride) for ride in self.RIDES.get(call, [])])

    def done(self, call, outs):
        outs = list(outs)
        for kind, what, *_ in self.RIDES.get(call, []):
            names = what if kind == "gather" else self.GROUPS[what]
            mine, outs = outs[:len(names)], outs[len(names):]
            if kind == "gather":
                self.full.update(zip(names, mine))
            elif kind == "send":
                for k, r in zip(names, mine):
                    self.got3.setdefault(k, []).append(r)
            else:
                for k, r in zip(names, mine):
                    self.got[k] = r
                    self.sums[k] = _chip_sum(f"chip_sum_{k}", self.slots[k], r, self.c1)

    def alone(self, call):
        self.done(call, _run_carry(call, self.carry(call)))

    def launch(self, call, after=None):
        carry = self._one(*self.RIDES[call][0])
        if after is not None:
            carry.arrays = list(lax.optimization_barrier((tuple(carry.arrays), after))[0])
        self.done(call, _run_carry_async(call, carry, self.ASYNC[call]))


def kernel(x, g_mix_norm, w_in, b_in, sinks, conv_w, conv_b, ln_g, ln_b, w_attn_proj, w_conv_proj, b_conv_proj, w_out, g_ffn_norm, w_ffn_in, w_ffn_down, g_final, loss_target, m_g_mix_norm, m_w_in, m_b_in, m_sinks, m_conv_w, m_conv_b, m_ln_g, m_ln_b, m_w_attn_proj, m_w_conv_proj, m_b_conv_proj, m_w_out, m_g_ffn_norm, m_w_ffn_in, m_w_ffn_down, m_g_final, v_g_mix_norm, v_w_in, v_b_in, v_sinks, v_conv_w, v_conv_b, v_ln_g, v_ln_b, v_w_attn_proj, v_w_conv_proj, v_b_conv_proj, v_w_out, v_g_ffn_norm, v_w_ffn_in, v_w_ffn_down, v_g_final):
    w = dict(g_mix_norm=g_mix_norm, w_in=w_in, b_in=b_in, sinks=sinks, conv_w=conv_w, conv_b=conv_b, ln_g=ln_g,
             ln_b=ln_b, w_attn_proj=w_attn_proj, w_conv_proj=w_conv_proj, b_conv_proj=b_conv_proj, w_out=w_out,
             g_ffn_norm=g_ffn_norm, w_ffn_in=w_ffn_in, w_ffn_down=w_ffn_down, g_final=g_final)
    m = dict(g_mix_norm=m_g_mix_norm, w_in=m_w_in, b_in=m_b_in, sinks=m_sinks, conv_w=m_conv_w, conv_b=m_conv_b,
             ln_g=m_ln_g, ln_b=m_ln_b, w_attn_proj=m_w_attn_proj, w_conv_proj=m_w_conv_proj,
             b_conv_proj=m_b_conv_proj, w_out=m_w_out, g_ffn_norm=m_g_ffn_norm, w_ffn_in=m_w_ffn_in,
             w_ffn_down=m_w_ffn_down, g_final=m_g_final)
    v = dict(g_mix_norm=v_g_mix_norm, w_in=v_w_in, b_in=v_b_in, sinks=v_sinks, conv_w=v_conv_w, conv_b=v_conv_b,
             ln_g=v_ln_g, ln_b=v_ln_b, w_attn_proj=v_w_attn_proj, w_conv_proj=v_w_conv_proj,
             b_conv_proj=v_b_conv_proj, w_out=v_w_out, g_ffn_norm=v_g_ffn_norm, w_ffn_in=v_w_ffn_in,
             w_ffn_down=v_w_ffn_down, g_final=v_g_final)
    ax, ay, ac = lax.axis_index("x"), lax.axis_index("y"), lax.axis_index("c")
    me = 4 * ax + 2 * ay + ac
    chip = 2 * ax + ay

    shards = {k: (w[k][0].T if tr else w[k][0]).astype(BF) for k, tr in BIG.items()}
    cw_shard = jnp.pad(conv_w[0].T, ((0, 0), (0, 1))).reshape(16, 128)
    wi_t, cw_full = _first_gather([shards["w_in"], cw_shard])
    conv_full = cw_full.reshape(CONV_CHANNELS, CONV_PAD).T

    as_row = lambda a: a.reshape(1, -1)
    small_w = {k: as_row(w[k]) for k in SMALL_NAMES}
    small_m = {k: as_row(m[k]) for k in SMALL_NAMES}
    small_v = {k: as_row(v[k]) for k in SMALL_NAMES}
    plan = _Plan(shards, ac.reshape(1).astype(jnp.int32))
    plan.launch("gather_mix", after=wi_t)
    dx, parts = _local_step(x[0], loss_target[0], small_w, wi_t, conv_full, plan)

    small_gathered, = _run_carry_async("small_gather", _gather_carry([_small_pack(parts)]), 8)

    ids = jnp.stack([me, chip]).astype(jnp.int32)
    grads, delta, new_m, new_v = {}, {}, {}, {}
    for k in sorted(BIG, key=lambda k: k == "w_in"):
        tot = _grad_total(f"grad_total_{k}", plan.slots[k], plan.got[k], plan.got3[k], ids)
        tot = tot.T if BIG[k] else tot
        d, nm, nv = _adamw(f"adamw_{k}", w[k][0], tot, m[k][0], v[k][0])
        grads[k], delta[k], new_m[k], new_v[k] = tot[None], d[None], nm[None], nv[None]

    loss, cw_grad, small_out = _small_adamw(small_gathered, small_w, small_m, small_v)
    for k in SMALL_NAMES:
        g, d, nm, nv = (a.reshape(w[k].shape) for a in small_out[k])
        grads[k], delta[k], new_m[k], new_v[k] = g, d, nm, nv
    cw_mine = lax.dynamic_slice(cw_grad, (0, me * 64), (CONV_WIDTH, 64))
    d, nm, nv = _adamw("adamw_conv_w", conv_w[0], cw_mine, m_conv_w[0], v_conv_w[0])
    grads["conv_w"], delta["conv_w"], new_m["conv_w"], new_v["conv_w"] = cw_mine[None], d[None], nm[None], nv[None]

    return (loss.reshape(()), dx[None], *[grads[k] for k in WEIGHT_NAMES], *[delta[k] for k in WEIGHT_NAMES],
            *[new_m[k] for k in WEIGHT_NAMES], *[new_v[k] for k in WEIGHT_NAMES])
```

```python
import functools

import jax
import jax.numpy as jnp
from jax import lax
from jax.experimental import pallas as pl
from jax.experimental.pallas import tpu as pltpu
from jax.experimental.pallas import tpu_sc as plsc

F32 = jnp.float32
BF = jnp.bfloat16

SEQ = 2048
D_MODEL = 1024
HEAD_DIM = 64
N_Q_HEADS = 8
N_KV_HEADS = 2
GROUP = N_Q_HEADS // N_KV_HEADS
BLOCK = 128
ATTN_WIDTH = 512
KV_WIDTH = 128
CONV_CHANNELS = 512
CONV_WIDTH = 31
CONV_PAD = 32
GLU_OFF = 768
GATE_OFF = 1792
IN_WIDTH = 3840
D_FF = 2816
EPS = 1e-5
NEG = -1e30
N_DEV = 8

ADAM_LR = 0.001
ADAM_B1 = 0.9
ADAM_B2 = 0.999
ADAM_EPS = 1e-08
ADAM_WD = 0.01
ADAM_STEP = 10

VMEM_LIMIT_BYTES = 56 * 1024 * 1024
MESH = pl.DeviceIdType.MESH
ANY = pl.BlockSpec(memory_space=pl.ANY)

_DIMS = {"NN": (((1,), (0,)), ((), ())), "NT": (((1,), (1,)), ((), ())), "TN": (((0,), (0,)), ((), ()))}


def _params(sem):
    return pltpu.CompilerParams(dimension_semantics=sem, vmem_limit_bytes=VMEM_LIMIT_BYTES)


class _Carry:
    def __init__(self, arrays, out_shapes, sems, start, finish, peers=None):
        self.arrays, self.out_shapes, self.sems, self.start, self.finish = arrays, out_shapes, sems, start, finish
        self.peers = peers


def _carry_io(carry):
    if carry is None:
        return [], [], []
    return list(carry.arrays), list(carry.out_shapes), list(carry.sems)


def _matmul(name, a_list, b, mode, *, m, n, tm, tn, tk=None, epilogue, extra=(), outs, b_off=(0, 0), alias=None,
            scratch=(), carry=None):
    seg_k = [a.shape[0] if mode == "TN" else a.shape[1] for a in a_list]
    whole = tk is None
    seg_nk = [1] * len(a_list) if whole else [ks // tk for ks in seg_k]
    nk = 1 if whole else sum(seg_nk)
    starts = [sum(seg_nk[:s]) for s in range(len(seg_nk))]
    k_starts = [sum(seg_k[:s]) for s in range(len(seg_k))]
    k_tot = sum(seg_k)
    n_a, n_extra, n_out = len(a_list), len(extra), len(outs)

    a_specs = []
    for st, ns, ks in zip(starts, seg_nk, seg_k):
        if mode == "TN":
            a_specs.append(pl.BlockSpec((ks if whole else tk, tm), lambda j, i, k: (k, i)))
        elif whole:
            a_specs.append(pl.BlockSpec((tm, ks), lambda j, i, k: (i, 0)))
        else:
            a_specs.append(pl.BlockSpec((tm, tk), functools.partial(
                lambda j, i, k, st, ns: (i, jnp.clip(k - st, 0, ns - 1)), st=st, ns=ns)))
    bk = k_tot if whole else tk
    if mode == "NT":
        b_spec = pl.BlockSpec((tn, bk), lambda j, i, k: (b_off[0] + j, b_off[1] + k))
    else:
        b_spec = pl.BlockSpec((bk, tn), lambda j, i, k: (b_off[0] + k, b_off[1] + j))
    n_alias = 0 if alias is None else 1
    c_in, c_out, c_sems = _carry_io(carry)
    n_acc = 0 if whole else 1
    nj, ni = n // tn, m // tm

    def body(*refs):
        pos = [n_a, 1, n_alias, n_extra, len(c_in), n_out, len(c_out), n_acc, len(scratch), len(c_sems)]
        cuts = [sum(pos[:q]) for q in range(len(pos) + 1)]
        a_refs, (b_ref,), _, ex, ci_refs, out_refs, co_refs, acc_refs, scr, cs_refs = (
            refs[cuts[q]:cuts[q + 1]] for q in range(len(pos)))
        j, i, k = pl.program_id(0), pl.program_id(1), pl.program_id(2)
        ids = (j, i)
        if carry is not None:
            @pl.when((j == 0) & (i == 0) & (k == 0))
            def _():
                carry.start(ci_refs, co_refs, cs_refs)

        def dot(a_ref, bv):
            return lax.dot_general(a_ref[...].astype(BF), bv.astype(BF), _DIMS[mode], preferred_element_type=F32)

        if whole:
            tot = None
            for a_ref, k0, ks in zip(a_refs, k_starts, seg_k):
                if n_a == 1:
                    bv = b_ref[...]
                else:
                    bv = b_ref[:, k0:k0 + ks] if mode == "NT" else b_ref[k0:k0 + ks, :]
                part = dot(a_ref, bv)
                tot = part if tot is None else tot + part
            epilogue(tot, ex, out_refs, ids, scr)
        else:
            acc, = acc_refs

            @pl.when(k == 0)
            def _():
                acc[...] = jnp.zeros_like(acc)

            for a_ref, st, ns in zip(a_refs, starts, seg_nk):
                if n_a == 1:
                    acc[...] += dot(a_ref, b_ref[...])
                else:
                    @pl.when((k >= st) & (k < st + ns))
                    def _(a_ref=a_ref):
                        acc[...] += dot(a_ref, b_ref[...])

            @pl.when(k == nk - 1)
            def _():
                epilogue(acc[...], ex, out_refs, ids, scr)

        if carry is not None:
            @pl.when((j == nj - 1) & (i == ni - 1) & (k == nk - 1))
            def _():
                carry.finish(ci_refs, co_refs, cs_refs)

    in_specs = [*a_specs, b_spec]
    args = [*a_list, b]
    io_alias = {}
    if alias is not None:
        in_specs.append(pl.BlockSpec(memory_space=pl.ANY))
        args.append(alias[0])
        io_alias = {n_a + 1: alias[1]}
    in_specs += [s for _, s in extra] + [pl.BlockSpec(memory_space=pl.ANY)] * len(c_in)
    args += [x for x, _ in extra] + c_in
    res = pl.pallas_call(
        body, name=name, grid=(nj, ni, nk), in_specs=in_specs,
        out_specs=[s for _, s in outs] + [pl.BlockSpec(memory_space=pl.ANY)] * len(c_out),
        out_shape=[o for o, _ in outs] + c_out,
        scratch_shapes=[*([] if whole else [pltpu.VMEM((tm, tn), F32)]), *scratch, *c_sems],
        input_output_aliases=io_alias,
        compiler_params=_params(("arbitrary", "arbitrary", "arbitrary")),
    )(*args)
    return res if carry is None else (res[:n_out], res[n_out:])


def _tile(tm, tn):
    return pl.BlockSpec((tm, tn), lambda j, i, k: (i, j))


def _row(tn):
    return pl.BlockSpec((1, tn), lambda j, i, k: (0, j))


def _store(dtype):
    def ep(acc, ex, outs, ids, scr):
        outs[0][...] = acc.astype(dtype)
    return ep


def _sds(shape, dtype):
    return jax.ShapeDtypeStruct(shape, dtype)


def _rms_fwd(name, x, g):
    T, D = x.shape
    tm = 512

    def body(x_ref, g_ref, h_ref, r_ref):
        xv = x_ref[...]
        r = lax.rsqrt(jnp.mean(xv * xv, axis=-1, keepdims=True) + EPS)
        h_ref[...] = (xv * r * g_ref[...]).astype(BF)
        r_ref[...] = r

    return pl.pallas_call(
        body, name=name, grid=(T // tm,),
        in_specs=[pl.BlockSpec((tm, D), lambda i: (i, 0)), pl.BlockSpec((1, D), lambda i: (0, 0))],
        out_specs=[pl.BlockSpec((tm, D), lambda i: (i, 0)), pl.BlockSpec((tm, 1), lambda i: (i, 0))],
        out_shape=[_sds((T, D), BF), _sds((T, 1), F32)],
        compiler_params=_params(("arbitrary",)),
    )(x, g)


def _rms_bwd(dh, xv, r, g):
    xh = xv * r
    dxh = dh * g
    dx = r * (dxh - xh * jnp.mean(dxh * xh, axis=-1, keepdims=True))
    return dx, jnp.sum(dh * xh, axis=0, keepdims=True)


def _accumulate_rows(ref, val, first):
    @pl.when(first)
    def _():
        ref[...] = val

    @pl.when(jnp.logical_not(first))
    def _():
        ref[...] += val


def _loss_head(xv, g, target):
    r = lax.rsqrt(jnp.mean(xv * xv, axis=-1, keepdims=True) + EPS)
    err = xv * r * g - target
    dx, dg = _rms_bwd(err * (1.0 / xv.shape[-1]), xv, r, g)
    part = 0.5 * jnp.sum(jnp.mean(err * err, axis=-1, keepdims=True), axis=0, keepdims=True)
    return dx, dg, part


def _lane_half(shape, h):
    lane = lax.broadcasted_iota(jnp.int32, shape, 1)
    return (lane >= HEAD_DIM * h) & (lane < HEAD_DIM * (h + 1))


def _to_half(v, w, h):
    if w != h:
        v = pltpu.roll(v, HEAD_DIM, 1)
    return jnp.where(_lane_half(v.shape, h), v, 0.0)


def _attn_block(qkv_ref, sinks_ref, n, h):
    r0 = pl.multiple_of(n * BLOCK, BLOCK)
    p0 = pl.multiple_of(jnp.maximum(n - 1, 0) * BLOCK, BLOCK)
    rows = pl.ds(r0, BLOCK)
    prev = pl.ds(p0, BLOCK)
    k2 = jnp.concatenate([qkv_ref[prev, ATTN_WIDTH:ATTN_WIDTH + KV_WIDTH],
                          qkv_ref[rows, ATTN_WIDTH:ATTN_WIDTH + KV_WIDTH]], axis=0)
    v2 = jnp.concatenate([qkv_ref[prev, ATTN_WIDTH + KV_WIDTH:ATTN_WIDTH + 2 * KV_WIDTH],
                          qkv_ref[rows, ATTN_WIDTH + KV_WIDTH:ATTN_WIDTH + 2 * KV_WIDTH]], axis=0)
    qs = []
    for g in range(GROUP):
        hq = GROUP * h + g
        blk = qkv_ref[rows, (hq // 2) * 128:(hq // 2 + 1) * 128].astype(F32)
        qs.append(_to_half(blk, hq % 2, h))
    q4 = jnp.concatenate(qs, axis=0).astype(BF)
    s = lax.dot_general(q4, k2, _DIMS["NT"], preferred_element_type=F32) * (HEAD_DIM ** -0.5)
    shape = s.shape
    row = lax.broadcasted_iota(jnp.int32, shape, 0)
    qi = row & (BLOCK - 1)
    kj = lax.broadcasted_iota(jnp.int32, shape, 1)
    diff = qi + BLOCK - kj
    valid = (diff >= 0) & (diff < BLOCK) & ((kj >= BLOCK) | (n > 0))
    s = jnp.where(valid, s, NEG)
    row1 = lax.broadcasted_iota(jnp.int32, (shape[0], 1), 0)
    sink = jnp.zeros((shape[0], 1), F32)
    for g in range(GROUP):
        sink = jnp.where((row1 >= g * BLOCK) & (row1 < (g + 1) * BLOCK), sinks_ref[0, GROUP * h + g], sink)
    m = jnp.maximum(jnp.max(s, axis=-1, keepdims=True), sink)
    e = jnp.exp(s - m)
    es = jnp.exp(sink - m)
    inv = 1.0 / (jnp.sum(e, axis=-1, keepdims=True) + es)
    return e * inv, es * inv, q4, k2, v2, rows, prev


def _attn_fwd(proj, sinks, carry=None):
    T = proj.shape[0]
    c_in, c_out, c_sems = _carry_io(carry)

    def body(*refs):
        qkv_ref, sinks_ref = refs[:2]
        ci_refs = refs[2:2 + len(c_in)]
        o_ref = refs[2 + len(c_in)]
        co_refs = refs[3 + len(c_in):3 + len(c_in) + len(c_out)]
        cs_refs = refs[3 + len(c_in) + len(c_out):]
        if carry is not None:
            carry.start(ci_refs, co_refs, cs_refs)

        def blk(n, z):
            outs = [None] * (N_Q_HEADS // 2)
            for h in range(N_KV_HEADS):
                p, _, _, _, v2, rows, _ = _attn_block(qkv_ref, sinks_ref, n, h)
                o = lax.dot_general(p.astype(BF), v2, _DIMS["NN"], preferred_element_type=F32)
                for g in range(GROUP):
                    hq = GROUP * h + g
                    piece = jnp.where(_lane_half((BLOCK, 128), h), o[g * BLOCK:(g + 1) * BLOCK], 0.0)
                    if hq % 2 != h:
                        piece = pltpu.roll(piece, HEAD_DIM, 1)
                    outs[hq // 2] = piece if outs[hq // 2] is None else outs[hq // 2] + piece
            for pb in range(N_Q_HEADS // 2):
                o_ref[rows, pb * 128:(pb + 1) * 128] = outs[pb].astype(BF)
            return z

        lax.fori_loop(0, T // BLOCK, blk, 0)
        if carry is not None:
            carry.finish(ci_refs, co_refs, cs_refs)

    res = pl.pallas_call(
        body, name="attn_fwd", grid=(1,),
        in_specs=[pl.BlockSpec((T, GLU_OFF), lambda i: (0, 0)), pl.BlockSpec(memory_space=pltpu.SMEM),
                  *[ANY] * len(c_in)],
        out_specs=[pl.BlockSpec((T, ATTN_WIDTH), lambda i: (0, 0)), *[ANY] * len(c_out)],
        out_shape=[_sds((T, ATTN_WIDTH), BF), *c_out], scratch_shapes=c_sems,
        compiler_params=_params(("arbitrary",)),
    )(proj, sinks, *c_in)
    return res[0], res[1:]


def _attn_bwd(proj, d_o, sinks, carry=None):
    T = proj.shape[0]
    c_in, c_out, c_sems = _carry_io(carry)

    def body(*refs):
        qkv_ref, do_ref, sinks_ref = refs[:3]
        ci_refs = refs[3:3 + len(c_in)]
        dqkv_ref, dsink_ref = refs[3 + len(c_in):5 + len(c_in)]
        co_refs = refs[5 + len(c_in):5 + len(c_in) + len(c_out)]
        dk_acc, dv_acc = refs[5 + len(c_in) + len(c_out):7 + len(c_in) + len(c_out)]
        cs_refs = refs[7 + len(c_in) + len(c_out):]
        if carry is not None:
            carry.start(ci_refs, co_refs, cs_refs)
        dsink_ref[...] = jnp.zeros_like(dsink_ref)
        dk_acc[...] = jnp.zeros_like(dk_acc)
        dv_acc[...] = jnp.zeros_like(dv_acc)

        def blk(n, carry):
            dqs = [None] * (N_Q_HEADS // 2)
            for h in range(N_KV_HEADS):
                p, psink, q4, k2, v2, rows, prev = _attn_block(qkv_ref, sinks_ref, n, h)
                dos = []
                for g in range(GROUP):
                    hq = GROUP * h + g
                    dos.append(_to_half(do_ref[rows, (hq // 2) * 128:(hq // 2 + 1) * 128].astype(F32), hq % 2, h))
                do4 = jnp.concatenate(dos, axis=0).astype(BF)
                dp = lax.dot_general(do4, v2, _DIMS["NT"], preferred_element_type=F32)
                delta = jnp.sum(p * dp, axis=-1, keepdims=True)
                ds = (p * (dp - delta) * (HEAD_DIM ** -0.5)).astype(BF)
                dsk = psink * delta
                for g in range(GROUP):
                    hq = GROUP * h + g
                    tot = -jnp.sum(dsk[g * BLOCK:(g + 1) * BLOCK], axis=0, keepdims=True)
                    lane = lax.broadcasted_iota(jnp.int32, (1, 128), 1)
                    dsink_ref[...] += jnp.where(lane == hq, tot, 0.0)
                dq = lax.dot_general(ds, k2, _DIMS["NN"], preferred_element_type=F32)
                dk = lax.dot_general(ds, q4, _DIMS["TN"], preferred_element_type=F32)
                dv = lax.dot_general(p.astype(BF), do4, _DIMS["TN"], preferred_element_type=F32)
                dk_acc[prev, :] += dk[:BLOCK]
                dk_acc[rows, :] += dk[BLOCK:]
                dv_acc[prev, :] += dv[:BLOCK]
                dv_acc[rows, :] += dv[BLOCK:]
                for g in range(GROUP):
                    hq = GROUP * h + g
                    piece = jnp.where(_lane_half((BLOCK, 128), h), dq[g * BLOCK:(g + 1) * BLOCK], 0.0)
                    if hq % 2 != h:
                        piece = pltpu.roll(piece, HEAD_DIM, 1)
                    dqs[hq // 2] = piece if dqs[hq // 2] is None else dqs[hq // 2] + piece
            for pb in range(N_Q_HEADS // 2):
                dqkv_ref[rows, pb * 128:(pb + 1) * 128] = dqs[pb].astype(BF)
            return carry

        lax.fori_loop(0, T // BLOCK, blk, 0)
        dqkv_ref[:, ATTN_WIDTH:ATTN_WIDTH + KV_WIDTH] = dk_acc[...].astype(BF)
        dqkv_ref[:, ATTN_WIDTH + KV_WIDTH:] = dv_acc[...].astype(BF)
        if carry is not None:
            carry.finish(ci_refs, co_refs, cs_refs)

    res = pl.pallas_call(
        body, name="attn_bwd", grid=(1,),
        in_specs=[pl.BlockSpec((T, GLU_OFF), lambda i: (0, 0)), pl.BlockSpec((T, ATTN_WIDTH), lambda i: (0, 0)),
                  pl.BlockSpec(memory_space=pltpu.SMEM), *[ANY] * len(c_in)],
        out_specs=[pl.BlockSpec((T, GLU_OFF), lambda i: (0, 0)), pl.BlockSpec((1, 128), lambda i: (0, 0)),
                   *[ANY] * len(c_out)],
        out_shape=[_sds((T, GLU_OFF), BF), _sds((1, 128), F32), *c_out],
        scratch_shapes=[pltpu.VMEM((T, KV_WIDTH), F32), pltpu.VMEM((T, KV_WIDTH), F32), *c_sems],
        compiler_params=_params(("arbitrary",)),
    )(proj, d_o, sinks, *c_in)
    return res[:2], res[2:]


CHUNK = 256
SUB = 32
WIN = CHUNK + 32
PAD_ROWS = SEQ + 2 * CONV_PAD
_GLU_SPECS = [pl.BlockSpec((SEQ, 256), functools.partial(lambda i, c: (0, c), c=GLU_OFF // 256 + c)) for c in range(4)]


def _glu_to_pad(a0, a1, b0, b1, zpad):
    C = CONV_CHANNELS
    zpad[0:CONV_PAD, :] = jnp.zeros((CONV_PAD, C), F32)
    zpad[CONV_PAD + SEQ:, :] = jnp.zeros((CONV_PAD, C), F32)
    zpad[CONV_PAD:CONV_PAD + SEQ, 0:256] = a0[...].astype(F32) * jax.nn.sigmoid(b0[...].astype(F32))
    zpad[CONV_PAD:CONV_PAD + SEQ, 256:C] = a1[...].astype(F32) * jax.nn.sigmoid(b1[...].astype(F32))


def _tap_windows(src, base, win):
    for b in range(8):
        win[b, 0:WIN - 8, :] = src[base + b:base + b + WIN - 8, :]


def _taps(win, w_ref, init, out, flip):
    def sub(si, carry):
        r0 = pl.multiple_of(si * SUB, SUB)
        acc = jnp.broadcast_to(init, (SUB, CONV_CHANNELS))
        for k in range(CONV_WIDTH):
            wk = (CONV_WIDTH - 1 - k) if flip else k
            acc = acc + w_ref[wk:wk + 1, :] * win[k % 8, pl.ds(r0 + 8 * (k // 8), SUB), :]
        out[pl.ds(r0, SUB), :] = acc
        return carry

    lax.fori_loop(0, CHUNK // SUB, sub, 0)


def _tap_grads(win, du, dwacc):
    def sub(si, carry):
        r0 = pl.multiple_of(si * SUB, SUB)
        d = du[pl.ds(r0, SUB), :]
        for k in range(CONV_WIDTH):
            p = d * win[k % 8, pl.ds(r0 + 8 * (k // 8), SUB), :]
            dwacc[8 * k:8 * k + 8, :] += (p[0:8] + p[8:16]) + (p[16:24] + p[24:32])
        return carry

    lax.fori_loop(0, CHUNK // SUB, sub, 0)


def _ln_parts(u):
    mu = jnp.mean(u, axis=-1, keepdims=True)
    xc = u - mu
    rstd = lax.rsqrt(jnp.mean(xc * xc, axis=-1, keepdims=True) + EPS)
    return xc * rstd, rstd


def _conv_fwd(proj, conv_w, conv_b, ln_g, ln_b, carry=None):
    T, C = proj.shape[0], CONV_CHANNELS
    vec = pl.BlockSpec((1, C), lambda i: (0, 0))
    c_in, c_out, c_sems = _carry_io(carry)

    def body(*refs):
        a0, a1, b0, b1, w_ref, cb_ref, g_ref, be_ref = refs[:8]
        ci_refs = refs[8:8 + len(c_in)]
        c_ref = refs[8 + len(c_in)]
        co_refs = refs[9 + len(c_in):9 + len(c_in) + len(c_out)]
        zpad, win, ubuf = refs[9 + len(c_in) + len(c_out):12 + len(c_in) + len(c_out)]
        cs_refs = refs[12 + len(c_in) + len(c_out):]
        if carry is not None:
            carry.start(ci_refs, co_refs, cs_refs)
        _glu_to_pad(a0, a1, b0, b1, zpad)
        for ci in range(T // CHUNK):
            _tap_windows(zpad, ci * CHUNK + CONV_PAD - (CONV_WIDTH - 1), win)
            _taps(win, w_ref, cb_ref[...], ubuf, False)
            xh, _ = _ln_parts(ubuf[...])
            ln = xh * g_ref[...] + be_ref[...]
            c_ref[ci * CHUNK:(ci + 1) * CHUNK, :] = (ln * jax.nn.sigmoid(ln)).astype(BF)
        if carry is not None:
            carry.finish(ci_refs, co_refs, cs_refs)

    res = pl.pallas_call(
        body, name="conv_fwd", grid=(1,),
        in_specs=[*_GLU_SPECS, pl.BlockSpec((CONV_PAD, C), lambda i: (0, 0)), vec, vec, vec, *[ANY] * len(c_in)],
        out_specs=[pl.BlockSpec((T, C), lambda i: (0, 0)), *[ANY] * len(c_out)],
        out_shape=[_sds((T, C), BF), *c_out],
        scratch_shapes=[pltpu.VMEM((PAD_ROWS, C), F32), pltpu.VMEM((8, WIN, C), F32), pltpu.VMEM((CHUNK, C), F32),
                        *c_sems],
        compiler_params=_params(("arbitrary",)),
    )(proj, proj, proj, proj, conv_w, conv_b, ln_g, ln_b, *c_in)
    return res[0], res[1:]


def _conv_bwd(proj, d_c, conv_w, conv_b, ln_g, ln_b, carry=None):
    T, C = proj.shape[0], CONV_CHANNELS
    vec = pl.BlockSpec((1, C), lambda i: (0, 0))
    wspec = pl.BlockSpec((CONV_PAD, C), lambda i: (0, 0))
    c_in, c_out, c_sems = _carry_io(carry)

    def body(*refs):
        a0, a1, b0, b1, dc_ref, w_ref, cb_ref, g_ref, be_ref = refs[:9]
        ci_refs = refs[9:9 + len(c_in)]
        o = 9 + len(c_in)
        dglu_ref, dw_ref, dcb_ref, dg_ref, dbe_ref = refs[o:o + 5]
        co_refs = refs[o + 5:o + 5 + len(c_out)]
        zpad, dupad, win, ubuf, dwacc = refs[o + 5 + len(c_out):o + 10 + len(c_out)]
        cs_refs = refs[o + 10 + len(c_out):]
        if carry is not None:
            carry.start(ci_refs, co_refs, cs_refs)
        _glu_to_pad(a0, a1, b0, b1, zpad)
        dupad[T:, :] = jnp.zeros((2 * CONV_PAD, C), F32)
        dwacc[...] = jnp.zeros_like(dwacc)
        dcb_ref[...] = jnp.zeros_like(dcb_ref)
        dg_ref[...] = jnp.zeros_like(dg_ref)
        dbe_ref[...] = jnp.zeros_like(dbe_ref)
        for ci in range(T // CHUNK):
            rows = slice(ci * CHUNK, (ci + 1) * CHUNK)
            _tap_windows(zpad, ci * CHUNK + CONV_PAD - (CONV_WIDTH - 1), win)
            _taps(win, w_ref, cb_ref[...], ubuf, False)
            xh, rstd = _ln_parts(ubuf[...])
            ln = xh * g_ref[...] + be_ref[...]
            sg = jax.nn.sigmoid(ln)
            dln = dc_ref[rows, :].astype(F32) * (sg * (1.0 + ln * (1.0 - sg)))
            dg_ref[...] += jnp.sum(dln * xh, axis=0, keepdims=True)
            dbe_ref[...] += jnp.sum(dln, axis=0, keepdims=True)
            dxh = dln * g_ref[...]
            du = rstd * (dxh - jnp.mean(dxh, axis=-1, keepdims=True)
                         - xh * jnp.mean(dxh * xh, axis=-1, keepdims=True))
            dupad[rows, :] = du
            dcb_ref[...] += jnp.sum(du, axis=0, keepdims=True)
            _tap_grads(win, dupad.at[rows, :], dwacc)
        for k in range(CONV_WIDTH):
            dw_ref[k:k + 1, :] = jnp.sum(dwacc[8 * k:8 * k + 8, :], axis=0, keepdims=True)
        dw_ref[CONV_WIDTH:, :] = jnp.zeros((CONV_PAD - CONV_WIDTH, C), F32)
        for ci in range(T // CHUNK):
            rows = slice(ci * CHUNK, (ci + 1) * CHUNK)
            _tap_windows(dupad, ci * CHUNK, win)
            _taps(win, w_ref, jnp.zeros((1, C), F32), ubuf, True)
            dz = ubuf[...]
            for half, (a, b) in enumerate(((a0, b0), (a1, b1))):
                sb = jax.nn.sigmoid(b[rows, :].astype(F32))
                dzh = dz[:, half * 256:(half + 1) * 256]
                dglu_ref[rows, half * 256:(half + 1) * 256] = (dzh * sb).astype(BF)
                dglu_ref[rows, C + half * 256:C + (half + 1) * 256] = (
                    dzh * a[rows, :].astype(F32) * sb * (1.0 - sb)).astype(BF)
        if carry is not None:
            carry.finish(ci_refs, co_refs, cs_refs)

    res = pl.pallas_call(
        body, name="conv_bwd", grid=(1,),
        in_specs=[*_GLU_SPECS, pl.BlockSpec((T, C), lambda i: (0, 0)), wspec, vec, vec, vec, *[ANY] * len(c_in)],
        out_specs=[pl.BlockSpec((T, 2 * C), lambda i: (0, 0)), wspec, vec, vec, vec, *[ANY] * len(c_out)],
        out_shape=[_sds((T, 2 * C), BF), _sds((CONV_PAD, C), F32), _sds((1, C), F32), _sds((1, C), F32),
                   _sds((1, C), F32), *c_out],
        scratch_shapes=[pltpu.VMEM((PAD_ROWS, C), F32), pltpu.VMEM((PAD_ROWS, C), F32), pltpu.VMEM((8, WIN, C), F32),
                        pltpu.VMEM((CHUNK, C), F32), pltpu.VMEM((8 * CONV_PAD, C), F32), *c_sems],
        compiler_params=_params(("arbitrary",)),
    )(proj, proj, proj, proj, d_c, conv_w, conv_b, ln_g, ln_b, *c_in)
    return res[:5], res[5:]


_GATE_BLK = GATE_OFF // 256


def _ffn_in_swiglu(h2, wf_t, carry=None):
    T, D = h2.shape
    tm, tn = 512, D_FF // 2
    nj, ni = D_FF // tn, T // tm
    c_in, c_out, c_sems = _carry_io(carry)

    def body(*refs):
        a_ref, bg_ref, bu_ref = refs[:3]
        ci_refs = refs[3:3 + len(c_in)]
        act_ref, g_ref, u_ref = refs[3 + len(c_in):6 + len(c_in)]
        co_refs = refs[6 + len(c_in):6 + len(c_in) + len(c_out)]
        cs_refs = refs[6 + len(c_in) + len(c_out):]
        j, i = pl.program_id(0), pl.program_id(1)
        if carry is not None:
            @pl.when((j == 0) & (i == 0))
            def _():
                carry.start(ci_refs, co_refs, cs_refs)
        a = a_ref[...]
        for c0, c1 in ((0, 768), (768, tn)):
            g = lax.dot_general(a, bg_ref[c0:c1, :], _DIMS["NT"], preferred_element_type=F32)
            u = lax.dot_general(a, bu_ref[c0:c1, :], _DIMS["NT"], preferred_element_type=F32)
            act_ref[:, c0:c1] = (g * jax.nn.sigmoid(g) * u).astype(BF)
            g_ref[:, c0:c1] = g.astype(BF)
            u_ref[:, c0:c1] = u.astype(BF)
        if carry is not None:
            @pl.when((j == nj - 1) & (i == ni - 1))
            def _():
                carry.finish(ci_refs, co_refs, cs_refs)

    t = pl.BlockSpec((tm, tn), lambda j, i: (i, j))
    res = pl.pallas_call(
        body, name="ffn_in_swiglu", grid=(nj, ni),
        in_specs=[pl.BlockSpec((tm, D), lambda j, i: (i, 0)), pl.BlockSpec((tn, D), lambda j, i: (j, 0)),
                  pl.BlockSpec((tn, D), lambda j, i: (nj + j, 0)), *[ANY] * len(c_in)],
        out_specs=[t, t, t, *[ANY] * len(c_out)], out_shape=[*[_sds((T, D_FF), BF)] * 3, *c_out],
        scratch_shapes=c_sems,
        compiler_params=_params(("arbitrary", "arbitrary")),
    )(h2, wf_t, wf_t, *c_in)
    return res[:3], res[3:]


def _proj_merge(o, c, wap_t, wcp_t, b_cp, proj):
    T, D = o.shape[0], wap_t.shape[0]
    tm, tg = 1024, 256
    nj = D // tg

    def body(o_ref, c_ref, wa_ref, wc_ref, b_ref, g0_ref, g1_ref, ya_ref, yc_ref, m_ref):
        ya = lax.dot_general(o_ref[...], wa_ref[...], _DIMS["NT"], preferred_element_type=F32)
        yc = lax.dot_general(c_ref[...], wc_ref[...], _DIMS["NT"], preferred_element_type=F32) + b_ref[...]
        ya_ref[...] = ya.astype(BF)
        yc_ref[...] = yc.astype(BF)
        m_ref[...] = (jax.nn.sigmoid(g0_ref[...].astype(F32)) * ya + jax.nn.sigmoid(g1_ref[...].astype(F32)) * yc).astype(BF)

    act = pl.BlockSpec((tm, o.shape[1]), lambda j, i: (i, 0))
    wgt = pl.BlockSpec((tg, o.shape[1]), lambda j, i: (j, 0))
    t = pl.BlockSpec((tm, tg), lambda j, i: (i, j))
    return pl.pallas_call(
        body, name="proj_merge", grid=(nj, T // tm),
        in_specs=[act, act, wgt, wgt, pl.BlockSpec((1, tg), lambda j, i: (0, j)),
                  pl.BlockSpec((tm, tg), lambda j, i: (i, _GATE_BLK + j)),
                  pl.BlockSpec((tm, tg), lambda j, i: (i, _GATE_BLK + nj + j))],
        out_specs=[t, t, t], out_shape=[_sds((T, D), BF)] * 3,
        compiler_params=_params(("arbitrary", "arbitrary")),
    )(o, c, wap_t, wcp_t, b_cp, proj, proj)


def _proj_in_dw(segs, h):
    T, D = h.shape
    tb = 256
    nblk = [seg.shape[1] // tb for seg in segs]
    starts = [sum(nblk[:q]) for q in range(len(segs))]
    n_seg = len(segs)

    def body(*refs):
        seg_refs, h_ref, o_ref, cs_ref = refs[:n_seg], refs[n_seg], refs[n_seg + 1], refs[n_seg + 2]
        i = pl.program_id(0)
        for seg_ref, st, nb in zip(seg_refs, starts, nblk):
            @pl.when((i >= st) & (i < st + nb))
            def _(seg_ref=seg_ref):
                a = seg_ref[...]
                o_ref[...] = lax.dot_general(a, h_ref[...], _DIMS["TN"], preferred_element_type=F32).astype(BF)
                cs_ref[...] = jnp.sum(a.astype(F32), axis=0, keepdims=True)

    in_specs = [pl.BlockSpec((T, tb), functools.partial(lambda i, st, nb: (0, jnp.clip(i - st, 0, nb - 1)), st=st, nb=nb))
                for st, nb in zip(starts, nblk)]
    return pl.pallas_call(
        body, name="proj_in_dw", grid=(sum(nblk),),
        in_specs=[*in_specs, pl.BlockSpec((T, D), lambda i: (0, 0))],
        out_specs=[pl.BlockSpec((tb, D), lambda i: (i, 0)), pl.BlockSpec((1, tb), lambda i: (0, i))],
        out_shape=[_sds((sum(nblk) * tb, D), BF), _sds((1, sum(nblk) * tb), F32)],
        compiler_params=_params(("arbitrary",)),
    )(*segs, h)


def _local_step(x, target, small, wi_t, conv_w, plan):
    T, D = x.shape
    tm = 1024

    def carried(call, res, carry):
        if carry is None:
            return res
        outs, got = res
        plan.done(call, got)
        return outs

    h, r1 = _rms_fwd("rms_mix", x, small["g_mix_norm"])

    def ep_add(acc, ex, outs, ids, scr):
        outs[0][...] = acc + ex[0][...]

    tn_in = IN_WIDTH // 3
    carry = plan.carry("proj_in")
    def ep_bias_bf16(acc, ex, outs, ids, scr):
        outs[0][...] = (acc + ex[0][...]).astype(BF)

    proj, = carried("proj_in", _matmul("proj_in", [h], wi_t, "NT", m=T, n=IN_WIDTH, tm=tm, tn=tn_in,
                                       epilogue=ep_bias_bf16, extra=[(small["b_in"], _row(tn_in))],
                                       outs=[(_sds((T, IN_WIDTH), BF), _tile(tm, tn_in))], carry=carry), carry)
    plan.launch("gather_ffn", after=proj)
    o, got = _attn_fwd(proj, small["sinks"], carry=plan.carry("attn_fwd"))
    plan.done("attn_fwd", got)
    c, got = _conv_fwd(proj, conv_w, small["conv_b"], small["ln_g"], small["ln_b"], carry=plan.carry("conv_fwd"))
    plan.done("conv_fwd", got)
    wap_t, wcp_t, w_out = plan.weight("w_attn_proj"), plan.weight("w_conv_proj"), plan.weight("w_out")
    ya, yc, merged = _proj_merge(o, c, wap_t, wcp_t, small["b_conv_proj"], proj)

    tg = 256
    gate_specs = [pl.BlockSpec((tm, tg), lambda j, i, k: (i, _GATE_BLK + j)),
                  pl.BlockSpec((tm, tg), lambda j, i, k: (i, _GATE_BLK + D // tg + j))]

    def ep_residual_rms(acc, ex, outs, ids, scr):
        x2v = acc + ex[0][...]
        r = lax.rsqrt(jnp.mean(x2v * x2v, axis=-1, keepdims=True) + EPS)
        outs[0][...] = x2v
        outs[1][...] = (x2v * r * ex[1][...]).astype(BF)
        outs[2][...] = r

    carry = plan.carry("out_proj")
    x2, h2, r2 = carried("out_proj", _matmul(
        "out_proj_rms", [merged], w_out, "NN", m=T, n=D, tm=512, tn=D, epilogue=ep_residual_rms,
        extra=[(x, _tile(512, D)), (small["g_ffn_norm"], _row(D))],
        outs=[(_sds((T, D), F32), _tile(512, D)), (_sds((T, D), BF), _tile(512, D)),
              (_sds((T, 1), F32), pl.BlockSpec((512, 1), lambda j, i, k: (i, 0)))], carry=carry), carry)
    plan.launch("gather_down", after=x2)
    wf_t = plan.weight("w_ffn_in")
    (act, gate, up), got = _ffn_in_swiglu(h2, wf_t, carry=plan.carry("ffn_in_swiglu"))
    plan.done("ffn_in_swiglu", got)
    w_down = plan.weight("w_ffn_down")
    def ep_residual_loss(acc, ex, outs, ids, scr):
        dx, dg, part = _loss_head(acc + ex[0][...], ex[1][...], ex[2][...])
        outs[0][...] = dx
        outs[1][...] = dx.astype(BF)
        _accumulate_rows(outs[2], dg, ids[1] == 0)
        _accumulate_rows(outs[3], part, ids[1] == 0)

    dx3, dx3_b, dg_final, loss = _matmul(
        "ffn_down_loss", [act], w_down, "NN", m=T, n=D, tm=512, tn=D, epilogue=ep_residual_loss,
        extra=[(x2, _tile(512, D)), (small["g_final"], _row(D)), (target, _tile(512, D))],
        outs=[(_sds((T, D), F32), _tile(512, D)), (_sds((T, D), BF), _tile(512, D)), (_sds((1, D), F32), _row(D)),
              (_sds((1, 1), F32), pl.BlockSpec((1, 1), lambda j, i, k: (0, 0)))])

    tn_ff = D_FF // 2

    def ep_swiglu_bwd(acc, ex, outs, ids, scr):
        g, u = ex[0][...].astype(F32), ex[1][...].astype(F32)
        sg = jax.nn.sigmoid(g)
        outs[0][...] = (acc * u * sg * (1.0 + g * (1.0 - sg))).astype(BF)
        outs[1][...] = (acc * g * sg).astype(BF)

    dgate, dup = _matmul(
        "ffn_down_bwd", [dx3_b], w_down, "NT", m=T, n=D_FF, tm=512, tn=tn_ff, epilogue=ep_swiglu_bwd,
        extra=[(gate, _tile(512, tn_ff)), (up, _tile(512, tn_ff))],
        outs=[(_sds((T, D_FF), BF), _tile(512, tn_ff)), (_sds((T, D_FF), BF), _tile(512, tn_ff))])

    def dw(name, a, b, rows, cols, row_off=0, alias=None, total_rows=None, colsum=False):
        total_rows = rows if total_rows is None else total_rows
        tmw = rows if rows <= 1024 else D_FF // 2
        blk, rem = divmod(row_off, tmw)
        assert rem == 0

        def ep(acc, ex, outs, ids, scr):
            outs[0][...] = acc.astype(BF)
            if colsum:
                outs[1][...] = jnp.sum(ex[0][...].astype(F32), axis=0, keepdims=True)

        outs = [(_sds((total_rows, cols), BF), pl.BlockSpec((tmw, cols), lambda j, i, k: (blk + i, j)))]
        extra = []
        if colsum:
            extra = [(a, pl.BlockSpec((T, tmw), lambda j, i, k: (0, i)))]
            outs.append((_sds((1, rows), F32), pl.BlockSpec((1, tmw), lambda j, i, k: (0, i))))
        carry = plan.carry(name)
        res = carried(name, _matmul(name, [a], b, "TN", m=rows, n=cols, tm=tmw, tn=cols, epilogue=ep, extra=extra,
                                    outs=outs, alias=None if alias is None else (alias, 0), carry=carry), carry)
        return res if colsum else res[0]

    plan.grad_ready(dict(w_ffn_down=dw("ffn_down_dw", act, dx3_b, D_FF, D)))

    def ep_rms_bwd(acc, ex, outs, ids, scr):
        dx, dg = _rms_bwd(acc, ex[0][...], ex[1][...], ex[2][...])
        dx = ex[3][...] + dx
        outs[0][...] = dx
        outs[1][...] = dx.astype(BF)
        _accumulate_rows(outs[2], dg, ids[1] == 0)

    def rms_bwd_io(tm_, xin, r, g, dres):
        return dict(
            extra=[(xin, _tile(tm_, D)), (r, pl.BlockSpec((tm_, 1), lambda j, i, k: (i, 0))), (g, _row(D)),
                   (dres, _tile(tm_, D))],
            outs=[(_sds((T, D), F32), _tile(tm_, D)), (_sds((T, D), BF), _tile(tm_, D)), (_sds((1, D), F32), _row(D))])

    carry = plan.carry("ffn_in_bwd")
    dx2, dx2_b, dg_ffn = carried(
        "ffn_in_bwd",
        _matmul("ffn_in_bwd", [dgate, dup], wf_t, "NN", m=T, n=D, tm=tm, tn=D, tk=D_FF // 2, epilogue=ep_rms_bwd,
                carry=carry, **rms_bwd_io(tm, x2, r2, small["g_ffn_norm"], dx3)), carry)
    plan.launch("send_down")
    gwf_t = dw("ffn_in_dw_gate", dgate, h2, D_FF, D, total_rows=2 * D_FF)
    gwf_t = dw("ffn_in_dw_up", dup, h2, D_FF, D, row_off=D_FF, alias=gwf_t, total_rows=2 * D_FF)
    plan.grad_ready(dict(w_ffn_in=gwf_t))

    def ep_merge_bwd(acc, ex, outs, ids, scr):
        s0 = jax.nn.sigmoid(ex[2][...].astype(F32))
        s1 = jax.nn.sigmoid(ex[3][...].astype(F32))
        outs[0][...] = (acc * s0).astype(BF)
        outs[1][...] = (acc * s1).astype(BF)
        outs[2][...] = (acc * ex[0][...].astype(F32) * s0 * (1.0 - s0)).astype(BF)
        outs[3][...] = (acc * ex[1][...].astype(F32) * s1 * (1.0 - s1)).astype(BF)

    carry = plan.carry("out_proj_bwd_merge")
    dya, dyc, dg0, dg1 = carried(
        "out_proj_bwd_merge",
        _matmul("out_proj_bwd_merge", [dx2_b], w_out, "NT", m=T, n=D, tm=tm, tn=tg, epilogue=ep_merge_bwd,
                extra=[(ya, _tile(tm, tg)), (yc, _tile(tm, tg)), (proj, gate_specs[0]), (proj, gate_specs[1])],
                outs=[(_sds((T, D), BF), _tile(tm, tg))] * 4, carry=carry), carry)
    plan.launch("send_ffn")
    gw_out = dw("out_proj_dw", merged, dx2_b, D, D)
    d_o, = _matmul("attn_proj_bwd", [dya], wap_t, "NN", m=T, n=ATTN_WIDTH, tm=tm, tn=ATTN_WIDTH,
                   epilogue=_store(BF), outs=[(_sds((T, ATTN_WIDTH), BF), _tile(tm, ATTN_WIDTH))])
    d_c, = _matmul("conv_proj_bwd", [dyc], wcp_t, "NN", m=T, n=CONV_CHANNELS, tm=tm, tn=CONV_CHANNELS,
                   epilogue=_store(BF), outs=[(_sds((T, CONV_CHANNELS), BF), _tile(tm, CONV_CHANNELS))])
    gwap_t = dw("attn_proj_dw", dya, o, D, ATTN_WIDTH)
    gwcp_t, db_cp = dw("conv_proj_dw", dyc, c, D, CONV_CHANNELS, colsum=True)
    plan.grad_ready(dict(w_out=gw_out, w_attn_proj=gwap_t, w_conv_proj=gwcp_t))
    (dglu, dcw, dcb, dlng, dlnb), got = _conv_bwd(proj, d_c, conv_w, small["conv_b"], small["ln_g"], small["ln_b"],
                                                  carry=plan.carry("conv_bwd"))
    plan.done("conv_bwd", got)
    plan.launch("send_mix")
    (dqkv, dsinks), got = _attn_bwd(proj, d_o, small["sinks"], carry=plan.carry("attn_bwd"))
    plan.done("attn_bwd", got)

    segs = [dqkv, dglu, dg0, dg1]
    gwi_t, db_in = _proj_in_dw(segs, h)
    plan.grad_ready(dict(w_in=gwi_t))
    plan.alone("swap_inp")
    plan.launch("send_inp")
    carry = plan.carry("proj_in_bwd")
    dx, _, dg_mix = carried(
        "proj_in_bwd",
        _matmul("proj_in_bwd", segs, wi_t, "NN", m=T, n=D, tm=512, tn=D, epilogue=ep_rms_bwd, carry=carry,
                **rms_bwd_io(512, x, r1, small["g_mix_norm"], dx2)), carry)

    parts = dict(g_mix_norm=dg_mix, b_in=db_in, sinks=dsinks, conv_w=dcw, conv_b=dcb, ln_g=dlng, ln_b=dlnb,
                 b_conv_proj=db_cp, g_ffn_norm=dg_ffn, g_final=dg_final, loss=loss)
    return dx, parts


def _place():
    x, y, c = lax.axis_index("x"), lax.axis_index("y"), lax.axis_index("c")
    return x, y, c, [(1 - x, y), (x, 1 - y), (1 - x, 1 - y)]


def _gather_copies(x_refs, out_refs, rows_per, send_sems, recv_sems, local_sems):
    x, y, c, chips = _place()
    me, sibling = (x, y, c), (x, y, 1 - c)

    def rows(a, px, py, pc):
        return out_refs[a].at[pl.ds((4 * px + 2 * py + pc) * rows_per[a], rows_per[a])]

    def copy(a, k, block, to, src=None):
        return pltpu.make_async_remote_copy(
            src_ref=rows(a, *block) if src is None else src, dst_ref=rows(a, *block),
            send_sem=send_sems.at[7 * a + k], recv_sem=recv_sems.at[7 * a + k], device_id=to, device_id_type=MESH)

    def local(a):
        return pltpu.make_async_copy(x_refs[a], rows(a, *me), local_sems.at[a])

    def first(a):
        return [copy(a, 0, me, sibling, src=x_refs[a])] + [copy(a, 1 + j, me, (*chip, c), src=x_refs[a])
                                                          for j, chip in enumerate(chips)]

    def arrive(a, j):
        return copy(a, 1 + j, (*chips[j], c), me)

    def passed(a, j):
        return copy(a, 4 + j, (*chips[j], c), sibling)

    def from_sibling(a):
        return [copy(a, 0, sibling, me)] + [copy(a, 4 + j, (*chip, 1 - c), me) for j, chip in enumerate(chips)]

    return len(x_refs), local, first, arrive, passed, from_sibling


def _gather_start(*refs):
    n, local, first, _, _, _ = _gather_copies(*refs)
    for a in range(n):
        local(a).start()
        for cp in first(a):
            cp.start()


def _gather_finish(*refs):
    n, local, first, arrive, passed, from_sibling = _gather_copies(*refs)
    for a in range(n):
        for j in range(3):
            arrive(a, j).wait_recv()
            passed(a, j).start()
    for a in range(n):
        for cp in from_sibling(a):
            cp.wait_recv()
    for a in range(n):
        for cp in first(a) + [passed(a, j) for j in range(3)]:
            cp.wait_send()
        local(a).wait()


def _gather_peers():
    x, y, c, chips = _place()
    return [(x, y, 1 - c)] + [(*chip, c) for chip in chips]


def _gather_sems(n):
    return [pltpu.SemaphoreType.DMA((7 * n,)), pltpu.SemaphoreType.DMA((7 * n,)), pltpu.SemaphoreType.DMA((n,))]


def _gather_carry(shards):
    rows_per = [s.shape[0] for s in shards]
    return _Carry(shards, [_sds((N_DEV * s.shape[0],) + s.shape[1:], s.dtype) for s in shards],
                  _gather_sems(len(shards)),
                  lambda ins, outs, sems: _gather_start(ins, outs, rows_per, *sems),
                  lambda ins, outs, sems: _gather_finish(ins, outs, rows_per, *sems), _gather_peers)


def _first_gather(shards):
    n = len(shards)
    rows_per = [s.shape[0] for s in shards]

    def body(*refs):
        x_refs, out_refs = refs[:n], refs[n:2 * n]
        send_sems, recv_sems, local_sems = refs[2 * n:]
        x, y, c, chips = _place()
        me, sibling = (x, y, c), (x, y, 1 - c)
        near_x, near_y, far = (*chips[0], c), (*chips[1], c), (*chips[2], c)

        def rows(a, dev, part):
            h = rows_per[a] // 2
            lo, size = {"all": (0, 2 * h), "low": (0, h), "high": (h, h)}[part]
            return out_refs[a].at[pl.ds((4 * dev[0] + 2 * dev[1] + dev[2]) * rows_per[a] + lo, size)]

        def copy(a, k, block, part, to, src=None):
            return pltpu.make_async_remote_copy(
                src_ref=rows(a, block, part) if src is None else src, dst_ref=rows(a, block, part),
                send_sem=send_sems.at[9 * a + k], recv_sem=recv_sems.at[9 * a + k], device_id=to, device_id_type=MESH)

        other = lambda dev: (dev[0], dev[1], 1 - c)
        sent = []
        for a in range(n):
            pltpu.make_async_copy(x_refs[a], rows(a, me, "all"), local_sems.at[a]).start()
            sent += [copy(a, 0, me, "all", sibling, src=x_refs[a]), copy(a, 1, me, "all", near_x, src=x_refs[a]),
                     copy(a, 2, me, "all", near_y, src=x_refs[a])]
        for cp in sent:
            cp.start()
        for a in range(n):
            copy(a, 1, near_x, "all", me).wait_recv()
            copy(a, 2, near_y, "all", me).wait_recv()
            passed = [copy(a, 3, near_y, "high", near_x), copy(a, 4, near_x, "low", near_y),
                      copy(a, 5, near_x, "all", sibling), copy(a, 6, near_y, "all", sibling)]
            for cp in passed:
                cp.start()
            sent += passed
        for a in range(n):
            copy(a, 3, far, "high", me).wait_recv()
            copy(a, 4, far, "low", me).wait_recv()
            passed = [copy(a, 7, far, "high", sibling), copy(a, 8, far, "low", sibling)]
            for cp in passed:
                cp.start()
            sent += passed
        for a in range(n):
            copy(a, 0, sibling, "all", me).wait_recv()
            copy(a, 5, other(near_x), "all", me).wait_recv()
            copy(a, 6, other(near_y), "all", me).wait_recv()
            copy(a, 7, other(far), "high", me).wait_recv()
            copy(a, 8, other(far), "low", me).wait_recv()
        for cp in sent:
            cp.wait_send()
        for a in range(n):
            pltpu.make_async_copy(x_refs[a], rows(a, me, "all"), local_sems.at[a]).wait()

    return pl.pallas_call(
        body, name="weights_first_gather", in_specs=[ANY] * n, out_specs=[ANY] * n,
        out_shape=[_sds((N_DEV * s.shape[0],) + s.shape[1:], s.dtype) for s in shards],
        scratch_shapes=[pltpu.SemaphoreType.DMA((9 * n,)), pltpu.SemaphoreType.DMA((9 * n,)),
                        pltpu.SemaphoreType.DMA((n,))],
    )(*shards)


def _swap_carry(grads):
    n = len(grads)

    def copies(g_refs, out_refs, sems):
        send_sems, recv_sems = sems
        x, y, c, _ = _place()
        return [pltpu.make_async_remote_copy(
            src_ref=g_refs[a].at[2 * p + 1 - c], dst_ref=out_refs[a].at[p],
            send_sem=send_sems.at[4 * a + p], recv_sem=recv_sems.at[4 * a + p],
            device_id=(x, y, 1 - c), device_id_type=MESH) for a in range(n) for p in range(4)]

    def start(ins, outs, sems):
        for cp in copies(ins, outs, sems):
            cp.start()

    def finish(ins, outs, sems):
        for cp in copies(ins, outs, sems):
            cp.wait()

    def peers():
        x, y, c, _ = _place()
        return [(x, y, 1 - c)]

    return _Carry(grads, [_sds((4,) + g.shape[1:], g.dtype) for g in grads],
                  [pltpu.SemaphoreType.DMA((4 * n,)), pltpu.SemaphoreType.DMA((4 * n,))], start, finish, peers)


def _join(carries):
    carries = [c for c in carries if c is not None]
    if not carries:
        return None
    n_in = [len(c.arrays) for c in carries]
    n_out = [len(c.out_shapes) for c in carries]
    n_sem = [len(c.sems) for c in carries]

    def parts(refs, counts):
        cuts = [sum(counts[:q]) for q in range(len(counts) + 1)]
        return [refs[cuts[q]:cuts[q + 1]] for q in range(len(counts))]

    def start(ins, outs, sems):
        for c, i, o, s in zip(carries, parts(ins, n_in), parts(outs, n_out), parts(sems, n_sem)):
            c.start(i, o, s)

    def finish(ins, outs, sems):
        for c, i, o, s in zip(carries, parts(ins, n_in), parts(outs, n_out), parts(sems, n_sem)):
            c.finish(i, o, s)

    return _Carry([a for c in carries for a in c.arrays], [o for c in carries for o in c.out_shapes],
                  [s for c in carries for s in c.sems], start, finish)


def _run_carry(name, carry):
    n_in, n_out = len(carry.arrays), len(carry.out_shapes)

    def body(*refs):
        carry.start(refs[:n_in], refs[n_in:n_in + n_out], refs[n_in + n_out:])
        carry.finish(refs[:n_in], refs[n_in:n_in + n_out], refs[n_in + n_out:])

    return pl.pallas_call(body, name=name, in_specs=[ANY] * n_in, out_specs=[ANY] * n_out,
                          out_shape=carry.out_shapes, scratch_shapes=carry.sems)(*carry.arrays)


def _run_carry_async(name, carry, collective_id):
    ins = [jax.new_ref(a, memory_space=pltpu.MemorySpace.HBM) for a in carry.arrays]
    outs = [jax.empty_ref(o, memory_space=pltpu.MemorySpace.HBM) for o in carry.out_shapes]

    @pl.kernel(mesh=plsc.ScalarSubcoreMesh(axis_name="sequencer", num_cores=1), name=name,
               scratch_types=tuple(carry.sems), compiler_params=pltpu.CompilerParams(collective_id=collective_id))
    def launch(*sems):
        barrier = pltpu.get_barrier_semaphore()
        peers = carry.peers()
        for peer in peers:
            pl.semaphore_signal(barrier, inc=1, device_id=peer, device_id_type=MESH)
        pl.semaphore_wait(barrier, len(peers))
        carry.start(ins, outs, sems)
        carry.finish(ins, outs, sems)

    launch()
    return [o[...] for o in outs]


def _chip_sum(name, g, got, c):
    _, rows, cols = g.shape

    def body(c_ref, g_ref, got_ref, o_ref):
        o_ref[...] = (g_ref[...].astype(F32) + got_ref[...].astype(F32)).astype(BF)

    return pl.pallas_call(
        body, name=name,
        grid_spec=pltpu.PrefetchScalarGridSpec(
            num_scalar_prefetch=1, grid=(4,),
            in_specs=[pl.BlockSpec((1, rows, cols), lambda p, c_ref: (2 * p + c_ref[0], 0, 0)),
                      pl.BlockSpec((1, rows, cols), lambda p, c_ref: (p, 0, 0))],
            out_specs=pl.BlockSpec((1, rows, cols), lambda p, c_ref: (p, 0, 0))),
        out_shape=_sds((4, rows, cols), BF),
        compiler_params=_params(("arbitrary",)),
    )(c, g, got)


def _send_carry(sums, ks):
    n, nk = len(sums), len(ks)

    def copies(s_refs, out_refs, sems):
        send_sems, recv_sems = sems
        x, y, c, chips = _place()
        return [pltpu.make_async_remote_copy(
            src_ref=s_refs[a].at[2 * chips[k][0] + chips[k][1]], dst_ref=out_refs[a].at[q],
            send_sem=send_sems.at[nk * a + q], recv_sem=recv_sems.at[nk * a + q],
            device_id=(*chips[k], c), device_id_type=MESH) for a in range(n) for q, k in enumerate(ks)]

    def start(ins, outs, sems):
        for cp in copies(ins, outs, sems):
            cp.start()

    def finish(ins, outs, sems):
        for cp in copies(ins, outs, sems):
            cp.wait()

    def peers():
        x, y, c, chips = _place()
        return [(*chips[k], c) for k in ks]

    return _Carry(sums, [_sds((nk,) + s.shape[1:], s.dtype) for s in sums],
                  [pltpu.SemaphoreType.DMA((nk * n,)), pltpu.SemaphoreType.DMA((nk * n,))], start, finish, peers)


def _adam_math(w, g, m, v):
    m = ADAM_B1 * m + (1.0 - ADAM_B1) * g
    v = ADAM_B2 * v + (1.0 - ADAM_B2) * (g * g)
    m_hat = m / (1.0 - ADAM_B1 ** ADAM_STEP)
    v_hat = v / (1.0 - ADAM_B2 ** ADAM_STEP)
    delta = -ADAM_LR * (m_hat / (jnp.sqrt(v_hat) + ADAM_EPS) + ADAM_WD * w)
    return delta, m, v


def _adamw(name, w, g, m, v):
    rows, cols = w.shape
    tr = 256 if rows % 256 == 0 else rows

    def body(w_ref, g_ref, m_ref, v_ref, d_ref, nm_ref, nv_ref):
        d_ref[...], nm_ref[...], nv_ref[...] = _adam_math(w_ref[...], g_ref[...], m_ref[...], v_ref[...])

    t = pl.BlockSpec((tr, cols), lambda i: (i, 0))
    return pl.pallas_call(
        body, name=name, grid=(rows // tr,), in_specs=[t] * 4, out_specs=[t] * 3,
        out_shape=[_sds((rows, cols), F32)] * 3, compiler_params=_params(("arbitrary",)),
    )(w, g, m, v)


def _grad_adamw(name, g, got, got3, ids, w, m, v):
    _, rows, cols = g.shape
    n3 = len(got3)
    tr = rows // 2 if rows >= 256 else rows

    def body(ids_ref, g_ref, got_ref, *rest):
        w_ref, m_ref, v_ref, o_ref, d_ref, nm_ref, nv_ref = rest[n3:]
        tot = g_ref[0].astype(F32) + got_ref[0].astype(F32)
        for r_ref in rest[:n3]:
            for q in range(r_ref.shape[0]):
                tot = tot + r_ref[q].astype(F32)
        o_ref[...] = tot
        d_ref[...], nm_ref[...], nv_ref[...] = _adam_math(w_ref[...], tot, m_ref[...], v_ref[...])

    tile = pl.BlockSpec((tr, cols), lambda i, ids_ref: (i, 0))
    return pl.pallas_call(
        body, name=name,
        grid_spec=pltpu.PrefetchScalarGridSpec(
            num_scalar_prefetch=1, grid=(rows // tr,),
            in_specs=[pl.BlockSpec((1, tr, cols), lambda i, ids_ref: (ids_ref[0], i, 0)),
                      pl.BlockSpec((1, tr, cols), lambda i, ids_ref: (ids_ref[1], i, 0)),
                      *[pl.BlockSpec((r.shape[0], tr, cols), lambda i, ids_ref: (0, i, 0)) for r in got3],
                      tile, tile, tile],
            out_specs=[tile] * 4),
        out_shape=[_sds((rows, cols), F32)] * 4,
        compiler_params=_params(("arbitrary",)),
    )(ids, g, got, *got3, w, m, v)


SMALL_NAMES = ["g_mix_norm", "b_in", "sinks", "conv_b", "ln_g", "ln_b", "b_conv_proj", "g_ffn_norm", "g_final"]
_PACK_ROWS = 32


def _small_pack(parts):
    C = CONV_CHANNELS
    part_list = [parts["g_mix_norm"], parts["b_in"], parts["sinks"], parts["conv_b"], parts["ln_g"], parts["ln_b"],
                 parts["b_conv_proj"], parts["g_ffn_norm"], parts["g_final"], parts["loss"], parts["conv_w"]]

    def body(p_mix, p_b, p_sink, p_cb, p_lg, p_lb, p_bcp, p_ffn, p_fin, p_loss, p_cw, pack):
        pack[...] = jnp.zeros_like(pack)
        pack[0:1, :] = p_mix[...]
        pack[1:2, 0:GLU_OFF] = p_b[:, 0:GLU_OFF]
        pack[2:3, :] = p_b[:, GLU_OFF:GATE_OFF]
        pack[3:4, :] = p_b[:, GATE_OFF:GATE_OFF + D_MODEL]
        pack[4:5, :] = p_b[:, GATE_OFF + D_MODEL:]
        pack[5:6, 0:128] = p_sink[...]
        pack[6:7, 0:C] = p_cb[...]
        pack[6:7, C:2 * C] = p_lg[...]
        pack[7:8, 0:C] = p_lb[...]
        pack[8:9, :] = p_bcp[...]
        pack[9:10, :] = p_ffn[...]
        pack[10:11, :] = p_fin[...]
        pack[11:12, 0:128] = jnp.broadcast_to(p_loss[...], (1, 128))
        pack[12:28, 0:C] = p_cw[0:16, :]
        pack[12:28, C:2 * C] = p_cw[16:32, :]

    vm = pl.BlockSpec(memory_space=pltpu.VMEM)
    return pl.pallas_call(body, name="small_pack", in_specs=[vm] * len(part_list), out_specs=vm,
                          out_shape=_sds((_PACK_ROWS, D_MODEL), F32))(*part_list)


def _small_adamw(gathered, small_w, small_m, small_v):
    C = CONV_CHANNELS
    names = SMALL_NAMES
    widths = [small_w[k].shape[1] for k in names]
    n_small = len(names)

    def body(*refs):
        tot_ref = refs[0]
        w_refs = refs[1:1 + n_small]
        m_refs = refs[1 + n_small:1 + 2 * n_small]
        v_refs = refs[1 + 2 * n_small:1 + 3 * n_small]
        o = 1 + 3 * n_small
        loss_ref, cw_ref = refs[o], refs[o + 1]
        out_refs = refs[o + 2:o + 2 + 4 * n_small]
        tot = tot_ref[0:_PACK_ROWS, :]
        for d in range(1, N_DEV):
            tot = tot + tot_ref[d * _PACK_ROWS:(d + 1) * _PACK_ROWS, :]
        loss_ref[...] = tot[11:12, 0:1]
        cw_ref[0:16, :] = tot[12:28, 0:C]
        cw_ref[16:32, :] = tot[12:28, C:2 * C]
        grads = dict(
            g_mix_norm=tot[0:1, :],
            b_in=jnp.concatenate([tot[1:2, 0:GLU_OFF], tot[2:3, :], tot[3:4, :], tot[4:5, :]], axis=1),
            sinks=tot[5:6, 0:N_Q_HEADS], conv_b=tot[6:7, 0:C], ln_g=tot[6:7, C:2 * C], ln_b=tot[7:8, 0:C],
            b_conv_proj=tot[8:9, :], g_ffn_norm=tot[9:10, :], g_final=tot[10:11, :])
        for s, k in enumerate(names):
            g = grads[k]
            d, nm, nv = _adam_math(w_refs[s][...], g, m_refs[s][...], v_refs[s][...])
            out_refs[4 * s][...] = g
            out_refs[4 * s + 1][...] = d
            out_refs[4 * s + 2][...] = nm
            out_refs[4 * s + 3][...] = nv

    vm = pl.BlockSpec(memory_space=pltpu.VMEM)
    args = [gathered, *[small_w[k] for k in names], *[small_m[k] for k in names], *[small_v[k] for k in names]]
    out_shape = [_sds((1, 1), F32), _sds((CONV_PAD, C), F32)]
    for wd in widths:
        out_shape += [_sds((1, wd), F32)] * 4
    res = pl.pallas_call(
        body, name="small_adamw",
        in_specs=[vm] * len(args), out_specs=[vm] * len(out_shape), out_shape=out_shape,
        compiler_params=pltpu.CompilerParams(vmem_limit_bytes=VMEM_LIMIT_BYTES),
    )(*args)
    return res[0], res[1], {k: res[2 + 4 * s:6 + 4 * s] for s, k in enumerate(names)}


BIG = dict(w_in=True, w_attn_proj=True, w_conv_proj=True, w_out=False, w_ffn_in=True, w_ffn_down=False)
WEIGHT_NAMES = ["g_mix_norm", "w_in", "b_in", "sinks", "conv_w", "conv_b", "ln_g", "ln_b", "w_attn_proj",
                "w_conv_proj", "b_conv_proj", "w_out", "g_ffn_norm", "w_ffn_in", "w_ffn_down", "g_final"]


class _Plan:
    GROUPS = dict(down=["w_ffn_down"], ffn=["w_ffn_in"], mix=["w_out", "w_attn_proj", "w_conv_proj"], inp=["w_in"])
    ALL = (0, 1, 2)
    RIDES = dict(
        gather_mix=[("gather", ["w_attn_proj", "w_conv_proj", "w_out"])], gather_ffn=[("gather", ["w_ffn_in"])],
        gather_down=[("gather", ["w_ffn_down"])],
        ffn_in_bwd=[("swap", "down")], send_down=[("send", "down", ALL)],
        out_proj_bwd_merge=[("swap", "ffn")], send_ffn=[("send", "ffn", ALL)],
        conv_bwd=[("swap", "mix")], send_mix=[("send", "mix", ALL)],
        swap_inp=[("swap", "inp")], send_inp=[("send", "inp", ALL)])
    ASYNC = dict(gather_mix=1, gather_ffn=2, gather_down=3, send_down=4, send_ffn=5, send_mix=6, send_inp=7)

    def __init__(self, shards, c1):
        self.shards, self.c1 = shards, c1
        self.full, self.slots, self.got, self.sums, self.got3 = {}, {}, {}, {}, {}

    def weight(self, name):
        return self.full[name]

    def grad_ready(self, grads):
        for k, g in grads.items():
            self.slots[k] = g.reshape(N_DEV, g.shape[0] // N_DEV, g.shape[1])

    def _one(self, kind, what, ks=None):
        if kind == "gather":
            return _gather_carry([self.shards[k] for k in what])
        names = self.GROUPS[what]
        if kind == "swap":
            return _swap_carry([self.slots[k] for k in names])
        return _send_carry([self.sums[k] for k in names], ks)

    def carry(self, call):
        return _join([self._one(*ride) for ride in self.RIDES.get(call, [])])

    def done(self, call, outs):
        outs = list(outs)
        for kind, what, *_ in self.RIDES.get(call, []):
            names = what if kind == "gather" else self.GROUPS[what]
            mine, outs = outs[:len(names)], outs[len(names):]
            if kind == "gather":
                self.full.update(zip(names, mine))
            elif kind == "send":
                for k, r in zip(names, mine):
                    self.got3.setdefault(k, []).append(r)
            else:
                for k, r in zip(names, mine):
                    self.got[k] = r
                    self.sums[k] = _chip_sum(f"chip_sum_{k}", self.slots[k], r, self.c1)

    def alone(self, call):
        self.done(call, _run_carry(call, self.carry(call)))

    def launch(self, call, after=None):
        carry = self._one(*self.RIDES[call][0])
        if after is not None:
            carry.arrays = list(lax.optimization_barrier((tuple(carry.arrays), after))[0])
        self.done(call, _run_carry_async(call, carry, self.ASYNC[call]))


def kernel(x, g_mix_norm, w_in, b_in, sinks, conv_w, conv_b, ln_g, ln_b, w_attn_proj, w_conv_proj, b_conv_proj, w_out, g_ffn_norm, w_ffn_in, w_ffn_down, g_final, loss_target, m_g_mix_norm, m_w_in, m_b_in, m_sinks, m_conv_w, m_conv_b, m_ln_g, m_ln_b, m_w_attn_proj, m_w_conv_proj, m_b_conv_proj, m_w_out, m_g_ffn_norm, m_w_ffn_in, m_w_ffn_down, m_g_final, v_g_mix_norm, v_w_in, v_b_in, v_sinks, v_conv_w, v_conv_b, v_ln_g, v_ln_b, v_w_attn_proj, v_w_conv_proj, v_b_conv_proj, v_w_out, v_g_ffn_norm, v_w_ffn_in, v_w_ffn_down, v_g_final):
    w = dict(g_mix_norm=g_mix_norm, w_in=w_in, b_in=b_in, sinks=sinks, conv_w=conv_w, conv_b=conv_b, ln_g=ln_g,
             ln_b=ln_b, w_attn_proj=w_attn_proj, w_conv_proj=w_conv_proj, b_conv_proj=b_conv_proj, w_out=w_out,
             g_ffn_norm=g_ffn_norm, w_ffn_in=w_ffn_in, w_ffn_down=w_ffn_down, g_final=g_final)
    m = dict(g_mix_norm=m_g_mix_norm, w_in=m_w_in, b_in=m_b_in, sinks=m_sinks, conv_w=m_conv_w, conv_b=m_conv_b,
             ln_g=m_ln_g, ln_b=m_ln_b, w_attn_proj=m_w_attn_proj, w_conv_proj=m_w_conv_proj,
             b_conv_proj=m_b_conv_proj, w_out=m_w_out, g_ffn_norm=m_g_ffn_norm, w_ffn_in=m_w_ffn_in,
             w_ffn_down=m_w_ffn_down, g_final=m_g_final)
    v = dict(g_mix_norm=v_g_mix_norm, w_in=v_w_in, b_in=v_b_in, sinks=v_sinks, conv_w=v_conv_w, conv_b=v_conv_b,
             ln_g=v_ln_g, ln_b=v_ln_b, w_attn_proj=v_w_attn_proj, w_conv_proj=v_w_conv_proj,
             b_conv_proj=v_b_conv_proj, w_out=v_w_out, g_ffn_norm=v_g_ffn_norm, w_ffn_in=v_w_ffn_in,
             w_ffn_down=v_w_ffn_down, g_final=v_g_final)
    ax, ay, ac = lax.axis_index("x"), lax.axis_index("y"), lax.axis_index("c")
    me = 4 * ax + 2 * ay + ac
    chip = 2 * ax + ay

    shards = {k: (w[k][0].T if tr else w[k][0]).astype(BF) for k, tr in BIG.items()}
    cw_shard = jnp.pad(conv_w[0].T, ((0, 0), (0, 1))).reshape(16, 128)
    wi_t, cw_full = _first_gather([shards["w_in"], cw_shard])
    conv_full = cw_full.reshape(CONV_CHANNELS, CONV_PAD).T

    as_row = lambda a: a.reshape(1, -1)
    small_w = {k: as_row(w[k]) for k in SMALL_NAMES}
    small_m = {k: as_row(m[k]) for k in SMALL_NAMES}
    small_v = {k: as_row(v[k]) for k in SMALL_NAMES}
    plan = _Plan(shards, ac.reshape(1).astype(jnp.int32))
    plan.launch("gather_mix", after=wi_t)
    dx, parts = _local_step(x[0], loss_target[0], small_w, wi_t, conv_full, plan)

    small_gathered, = _run_carry_async("small_gather", _gather_carry([_small_pack(parts)]), 8)

    ids = jnp.stack([me, chip]).astype(jnp.int32)
    grads, delta, new_m, new_v = {}, {}, {}, {}
    for k in sorted(BIG, key=lambda k: k == "w_in"):
        flip = (lambda a: a.T) if BIG[k] else (lambda a: a)
        outs = _grad_adamw(f"grad_adamw_{k}", plan.slots[k], plan.got[k], plan.got3[k], ids,
                           flip(w[k][0]), flip(m[k][0]), flip(v[k][0]))
        grads[k], delta[k], new_m[k], new_v[k] = (flip(a)[None] for a in outs)

    loss, cw_grad, small_out = _small_adamw(small_gathered, small_w, small_m, small_v)
    for k in SMALL_NAMES:
        g, d, nm, nv = (a.reshape(w[k].shape) for a in small_out[k])
        grads[k], delta[k], new_m[k], new_v[k] = g, d, nm, nv
    cw_mine = lax.dynamic_slice(cw_grad, (0, me * 64), (CONV_WIDTH, 64))
    d, nm, nv = _adamw("adamw_conv_w", conv_w[0], cw_mine, m_conv_w[0], v_conv_w[0])
    grads["conv_w"], delta["conv_w"], new_m["conv_w"], new_v["conv_w"] = cw_mine[None], d[None], nm[None], nv[None]

    return (loss.reshape(()), dx[None], *[grads[k] for k in WEIGHT_NAMES], *[delta[k] for k in WEIGHT_NAMES],
            *[new_m[k] for k in WEIGHT_NAMES], *[new_v[k] for k in WEIGHT_NAMES])
```

```python
import functools

import jax
import jax.numpy as jnp
from jax import lax
from jax.experimental import pallas as pl
from jax.experimental.pallas import tpu as pltpu
from jax.experimental.pallas import tpu_sc as plsc

F32 = jnp.float32
BF = jnp.bfloat16

SEQ = 2048
D_MODEL = 1024
HEAD_DIM = 64
N_Q_HEADS = 8
N_KV_HEADS = 2
GROUP = N_Q_HEADS // N_KV_HEADS
BLOCK = 128
ATTN_WIDTH = 512
KV_WIDTH = 128
CONV_CHANNELS = 512
CONV_WIDTH = 31
CONV_PAD = 32
GLU_OFF = 768
GATE_OFF = 1792
IN_WIDTH = 3840
D_FF = 2816
EPS = 1e-5
NEG = -1e30
N_DEV = 8

ADAM_LR = 0.001
ADAM_B1 = 0.9
ADAM_B2 = 0.999
ADAM_EPS = 1e-08
ADAM_WD = 0.01
ADAM_STEP = 10

VMEM_LIMIT_BYTES = 56 * 1024 * 1024
MESH = pl.DeviceIdType.MESH
ANY = pl.BlockSpec(memory_space=pl.ANY)

_DIMS = {"NN": (((1,), (0,)), ((), ())), "NT": (((1,), (1,)), ((), ())), "TN": (((0,), (0,)), ((), ()))}


def _params(sem):
    return pltpu.CompilerParams(dimension_semantics=sem, vmem_limit_bytes=VMEM_LIMIT_BYTES)


class _Carry:
    def __init__(self, arrays, out_shapes, sems, start, finish, peers=None):
        self.arrays, self.out_shapes, self.sems, self.start, self.finish = arrays, out_shapes, sems, start, finish
        self.peers = peers


def _carry_io(carry):
    if carry is None:
        return [], [], []
    return list(carry.arrays), list(carry.out_shapes), list(carry.sems)


def _matmul(name, a_list, b, mode, *, m, n, tm, tn, tk=None, epilogue, extra=(), outs, b_off=(0, 0), alias=None,
            scratch=(), carry=None):
    seg_k = [a.shape[0] if mode == "TN" else a.shape[1] for a in a_list]
    whole = tk is None
    seg_nk = [1] * len(a_list) if whole else [ks // tk for ks in seg_k]
    nk = 1 if whole else sum(seg_nk)
    starts = [sum(seg_nk[:s]) for s in range(len(seg_nk))]
    k_starts = [sum(seg_k[:s]) for s in range(len(seg_k))]
    k_tot = sum(seg_k)
    n_a, n_extra, n_out = len(a_list), len(extra), len(outs)

    a_specs = []
    for st, ns, ks in zip(starts, seg_nk, seg_k):
        if mode == "TN":
            a_specs.append(pl.BlockSpec((ks if whole else tk, tm), lambda j, i, k: (k, i)))
        elif whole:
            a_specs.append(pl.BlockSpec((tm, ks), lambda j, i, k: (i, 0)))
        else:
            a_specs.append(pl.BlockSpec((tm, tk), functools.partial(
                lambda j, i, k, st, ns: (i, jnp.clip(k - st, 0, ns - 1)), st=st, ns=ns)))
    bk = k_tot if whole else tk
    if mode == "NT":
        b_spec = pl.BlockSpec((tn, bk), lambda j, i, k: (b_off[0] + j, b_off[1] + k))
    else:
        b_spec = pl.BlockSpec((bk, tn), lambda j, i, k: (b_off[0] + k, b_off[1] + j))
    n_alias = 0 if alias is None else 1
    c_in, c_out, c_sems = _carry_io(carry)
    n_acc = 0 if whole else 1
    nj, ni = n // tn, m // tm

    def body(*refs):
        pos = [n_a, 1, n_alias, n_extra, len(c_in), n_out, len(c_out), n_acc, len(scratch), len(c_sems)]
        cuts = [sum(pos[:q]) for q in range(len(pos) + 1)]
        a_refs, (b_ref,), _, ex, ci_refs, out_refs, co_refs, acc_refs, scr, cs_refs = (
            refs[cuts[q]:cuts[q + 1]] for q in range(len(pos)))
        j, i, k = pl.program_id(0), pl.program_id(1), pl.program_id(2)
        ids = (j, i)
        if carry is not None:
            @pl.when((j == 0) & (i == 0) & (k == 0))
            def _():
                carry.start(ci_refs, co_refs, cs_refs)

        def dot(a_ref, bv):
            return lax.dot_general(a_ref[...].astype(BF), bv.astype(BF), _DIMS[mode], preferred_element_type=F32)

        if whole:
            tot = None
            for a_ref, k0, ks in zip(a_refs, k_starts, seg_k):
                if n_a == 1:
                    bv = b_ref[...]
                else:
                    bv = b_ref[:, k0:k0 + ks] if mode == "NT" else b_ref[k0:k0 + ks, :]
                part = dot(a_ref, bv)
                tot = part if tot is None else tot + part
            epilogue(tot, ex, out_refs, ids, scr)
        else:
            acc, = acc_refs

            @pl.when(k == 0)
            def _():
                acc[...] = jnp.zeros_like(acc)

            for a_ref, st, ns in zip(a_refs, starts, seg_nk):
                if n_a == 1:
                    acc[...] += dot(a_ref, b_ref[...])
                else:
                    @pl.when((k >= st) & (k < st + ns))
                    def _(a_ref=a_ref):
                        acc[...] += dot(a_ref, b_ref[...])

            @pl.when(k == nk - 1)
            def _():
                epilogue(acc[...], ex, out_refs, ids, scr)

        if carry is not None:
            @pl.when((j == nj - 1) & (i == ni - 1) & (k == nk - 1))
            def _():
                carry.finish(ci_refs, co_refs, cs_refs)

    in_specs = [*a_specs, b_spec]
    args = [*a_list, b]
    io_alias = {}
    if alias is not None:
        in_specs.append(pl.BlockSpec(memory_space=pl.ANY))
        args.append(alias[0])
        io_alias = {n_a + 1: alias[1]}
    in_specs += [s for _, s in extra] + [pl.BlockSpec(memory_space=pl.ANY)] * len(c_in)
    args += [x for x, _ in extra] + c_in
    res = pl.pallas_call(
        body, name=name, grid=(nj, ni, nk), in_specs=in_specs,
        out_specs=[s for _, s in outs] + [pl.BlockSpec(memory_space=pl.ANY)] * len(c_out),
        out_shape=[o for o, _ in outs] + c_out,
        scratch_shapes=[*([] if whole else [pltpu.VMEM((tm, tn), F32)]), *scratch, *c_sems],
        input_output_aliases=io_alias,
        compiler_params=_params(("arbitrary", "arbitrary", "arbitrary")),
    )(*args)
    return res if carry is None else (res[:n_out], res[n_out:])


def _tile(tm, tn):
    return pl.BlockSpec((tm, tn), lambda j, i, k: (i, j))


def _row(tn):
    return pl.BlockSpec((1, tn), lambda j, i, k: (0, j))


def _store(dtype):
    def ep(acc, ex, outs, ids, scr):
        outs[0][...] = acc.astype(dtype)
    return ep


def _sds(shape, dtype):
    return jax.ShapeDtypeStruct(shape, dtype)


def _rms_fwd(name, x, g):
    T, D = x.shape
    tm = 512

    def body(x_ref, g_ref, h_ref, r_ref):
        xv = x_ref[...]
        r = lax.rsqrt(jnp.mean(xv * xv, axis=-1, keepdims=True) + EPS)
        h_ref[...] = (xv * r * g_ref[...]).astype(BF)
        r_ref[...] = r

    return pl.pallas_call(
        body, name=name, grid=(T // tm,),
        in_specs=[pl.BlockSpec((tm, D), lambda i: (i, 0)), pl.BlockSpec((1, D), lambda i: (0, 0))],
        out_specs=[pl.BlockSpec((tm, D), lambda i: (i, 0)), pl.BlockSpec((tm, 1), lambda i: (i, 0))],
        out_shape=[_sds((T, D), BF), _sds((T, 1), F32)],
        compiler_params=_params(("arbitrary",)),
    )(x, g)


def _rms_bwd(dh, xv, r, g):
    xh = xv * r
    dxh = dh * g
    dx = r * (dxh - xh * jnp.mean(dxh * xh, axis=-1, keepdims=True))
    return dx, jnp.sum(dh * xh, axis=0, keepdims=True)


def _accumulate_rows(ref, val, first):
    @pl.when(first)
    def _():
        ref[...] = val

    @pl.when(jnp.logical_not(first))
    def _():
        ref[...] += val


def _loss_head(xv, g, target):
    r = lax.rsqrt(jnp.mean(xv * xv, axis=-1, keepdims=True) + EPS)
    err = xv * r * g - target
    dx, dg = _rms_bwd(err * (1.0 / xv.shape[-1]), xv, r, g)
    part = 0.5 * jnp.sum(jnp.mean(err * err, axis=-1, keepdims=True), axis=0, keepdims=True)
    return dx, dg, part


def _lane_half(shape, h):
    lane = lax.broadcasted_iota(jnp.int32, shape, 1)
    return (lane >= HEAD_DIM * h) & (lane < HEAD_DIM * (h + 1))


def _to_half(v, w, h):
    if w != h:
        v = pltpu.roll(v, HEAD_DIM, 1)
    return jnp.where(_lane_half(v.shape, h), v, 0.0)


def _attn_block(qkv_ref, sinks_ref, n, h):
    r0 = pl.multiple_of(n * BLOCK, BLOCK)
    p0 = pl.multiple_of(jnp.maximum(n - 1, 0) * BLOCK, BLOCK)
    rows = pl.ds(r0, BLOCK)
    prev = pl.ds(p0, BLOCK)
    k2 = jnp.concatenate([qkv_ref[prev, ATTN_WIDTH:ATTN_WIDTH + KV_WIDTH],
                          qkv_ref[rows, ATTN_WIDTH:ATTN_WIDTH + KV_WIDTH]], axis=0)
    v2 = jnp.concatenate([qkv_ref[prev, ATTN_WIDTH + KV_WIDTH:ATTN_WIDTH + 2 * KV_WIDTH],
                          qkv_ref[rows, ATTN_WIDTH + KV_WIDTH:ATTN_WIDTH + 2 * KV_WIDTH]], axis=0)
    qs = []
    for g in range(GROUP):
        hq = GROUP * h + g
        blk = qkv_ref[rows, (hq // 2) * 128:(hq // 2 + 1) * 128].astype(F32)
        qs.append(_to_half(blk, hq % 2, h))
    q4 = jnp.concatenate(qs, axis=0).astype(BF)
    s = lax.dot_general(q4, k2, _DIMS["NT"], preferred_element_type=F32) * (HEAD_DIM ** -0.5)
    shape = s.shape
    row = lax.broadcasted_iota(jnp.int32, shape, 0)
    qi = row & (BLOCK - 1)
    kj = lax.broadcasted_iota(jnp.int32, shape, 1)
    diff = qi + BLOCK - kj
    valid = (diff >= 0) & (diff < BLOCK) & ((kj >= BLOCK) | (n > 0))
    s = jnp.where(valid, s, NEG)
    row1 = lax.broadcasted_iota(jnp.int32, (shape[0], 1), 0)
    sink = jnp.zeros((shape[0], 1), F32)
    for g in range(GROUP):
        sink = jnp.where((row1 >= g * BLOCK) & (row1 < (g + 1) * BLOCK), sinks_ref[0, GROUP * h + g], sink)
    m = jnp.maximum(jnp.max(s, axis=-1, keepdims=True), sink)
    e = jnp.exp(s - m)
    es = jnp.exp(sink - m)
    inv = 1.0 / (jnp.sum(e, axis=-1, keepdims=True) + es)
    return e * inv, es * inv, q4, k2, v2, rows, prev


def _attn_fwd(proj, sinks, carry=None):
    T = proj.shape[0]
    c_in, c_out, c_sems = _carry_io(carry)

    def body(*refs):
        qkv_ref, sinks_ref = refs[:2]
        ci_refs = refs[2:2 + len(c_in)]
        o_ref = refs[2 + len(c_in)]
        co_refs = refs[3 + len(c_in):3 + len(c_in) + len(c_out)]
        cs_refs = refs[3 + len(c_in) + len(c_out):]
        if carry is not None:
            carry.start(ci_refs, co_refs, cs_refs)

        def blk(n, z):
            outs = [None] * (N_Q_HEADS // 2)
            for h in range(N_KV_HEADS):
                p, _, _, _, v2, rows, _ = _attn_block(qkv_ref, sinks_ref, n, h)
                o = lax.dot_general(p.astype(BF), v2, _DIMS["NN"], preferred_element_type=F32)
                for g in range(GROUP):
                    hq = GROUP * h + g
                    piece = jnp.where(_lane_half((BLOCK, 128), h), o[g * BLOCK:(g + 1) * BLOCK], 0.0)
                    if hq % 2 != h:
                        piece = pltpu.roll(piece, HEAD_DIM, 1)
                    outs[hq // 2] = piece if outs[hq // 2] is None else outs[hq // 2] + piece
            for pb in range(N_Q_HEADS // 2):
                o_ref[rows, pb * 128:(pb + 1) * 128] = outs[pb].astype(BF)
            return z

        lax.fori_loop(0, T // BLOCK, blk, 0)
        if carry is not None:
            carry.finish(ci_refs, co_refs, cs_refs)

    res = pl.pallas_call(
        body, name="attn_fwd", grid=(1,),
        in_specs=[pl.BlockSpec((T, GLU_OFF), lambda i: (0, 0)), pl.BlockSpec(memory_space=pltpu.SMEM),
                  *[ANY] * len(c_in)],
        out_specs=[pl.BlockSpec((T, ATTN_WIDTH), lambda i: (0, 0)), *[ANY] * len(c_out)],
        out_shape=[_sds((T, ATTN_WIDTH), BF), *c_out], scratch_shapes=c_sems,
        compiler_params=_params(("arbitrary",)),
    )(proj, sinks, *c_in)
    return res[0], res[1:]


def _attn_bwd(proj, d_o, sinks, carry=None):
    T = proj.shape[0]
    c_in, c_out, c_sems = _carry_io(carry)

    def body(*refs):
        qkv_ref, do_ref, sinks_ref = refs[:3]
        ci_refs = refs[3:3 + len(c_in)]
        dqkv_ref, dsink_ref = refs[3 + len(c_in):5 + len(c_in)]
        co_refs = refs[5 + len(c_in):5 + len(c_in) + len(c_out)]
        dk_acc, dv_acc = refs[5 + len(c_in) + len(c_out):7 + len(c_in) + len(c_out)]
        cs_refs = refs[7 + len(c_in) + len(c_out):]
        if carry is not None:
            carry.start(ci_refs, co_refs, cs_refs)
        dsink_ref[...] = jnp.zeros_like(dsink_ref)
        dk_acc[...] = jnp.zeros_like(dk_acc)
        dv_acc[...] = jnp.zeros_like(dv_acc)

        def blk(n, carry):
            dqs = [None] * (N_Q_HEADS // 2)
            for h in range(N_KV_HEADS):
                p, psink, q4, k2, v2, rows, prev = _attn_block(qkv_ref, sinks_ref, n, h)
                dos = []
                for g in range(GROUP):
                    hq = GROUP * h + g
                    dos.append(_to_half(do_ref[rows, (hq // 2) * 128:(hq // 2 + 1) * 128].astype(F32), hq % 2, h))
                do4 = jnp.concatenate(dos, axis=0).astype(BF)
                dp = lax.dot_general(do4, v2, _DIMS["NT"], preferred_element_type=F32)
                delta = jnp.sum(p * dp, axis=-1, keepdims=True)
                ds = (p * (dp - delta) * (HEAD_DIM ** -0.5)).astype(BF)
                dsk = psink * delta
                for g in range(GROUP):
                    hq = GROUP * h + g
                    tot = -jnp.sum(dsk[g * BLOCK:(g + 1) * BLOCK], axis=0, keepdims=True)
                    lane = lax.broadcasted_iota(jnp.int32, (1, 128), 1)
                    dsink_ref[...] += jnp.where(lane == hq, tot, 0.0)
                dq = lax.dot_general(ds, k2, _DIMS["NN"], preferred_element_type=F32)
                dk = lax.dot_general(ds, q4, _DIMS["TN"], preferred_element_type=F32)
                dv = lax.dot_general(p.astype(BF), do4, _DIMS["TN"], preferred_element_type=F32)
                dk_acc[prev, :] += dk[:BLOCK]
                dk_acc[rows, :] += dk[BLOCK:]
                dv_acc[prev, :] += dv[:BLOCK]
                dv_acc[rows, :] += dv[BLOCK:]
                for g in range(GROUP):
                    hq = GROUP * h + g
                    piece = jnp.where(_lane_half((BLOCK, 128), h), dq[g * BLOCK:(g + 1) * BLOCK], 0.0)
                    if hq % 2 != h:
                        piece = pltpu.roll(piece, HEAD_DIM, 1)
                    dqs[hq // 2] = piece if dqs[hq // 2] is None else dqs[hq // 2] + piece
            for pb in range(N_Q_HEADS // 2):
                dqkv_ref[rows, pb * 128:(pb + 1) * 128] = dqs[pb].astype(BF)
            return carry

        lax.fori_loop(0, T // BLOCK, blk, 0)
        dqkv_ref[:, ATTN_WIDTH:ATTN_WIDTH + KV_WIDTH] = dk_acc[...].astype(BF)
        dqkv_ref[:, ATTN_WIDTH + KV_WIDTH:] = dv_acc[...].astype(BF)
        if carry is not None:
            carry.finish(ci_refs, co_refs, cs_refs)

    res = pl.pallas_call(
        body, name="attn_bwd", grid=(1,),
        in_specs=[pl.BlockSpec((T, GLU_OFF), lambda i: (0, 0)), pl.BlockSpec((T, ATTN_WIDTH), lambda i: (0, 0)),
                  pl.BlockSpec(memory_space=pltpu.SMEM), *[ANY] * len(c_in)],
        out_specs=[pl.BlockSpec((T, GLU_OFF), lambda i: (0, 0)), pl.BlockSpec((1, 128), lambda i: (0, 0)),
                   *[ANY] * len(c_out)],
        out_shape=[_sds((T, GLU_OFF), BF), _sds((1, 128), F32), *c_out],
        scratch_shapes=[pltpu.VMEM((T, KV_WIDTH), F32), pltpu.VMEM((T, KV_WIDTH), F32), *c_sems],
        compiler_params=_params(("arbitrary",)),
    )(proj, d_o, sinks, *c_in)
    return res[:2], res[2:]


CHUNK = 256
SUB = 32
WIN = CHUNK + 32
PAD_ROWS = SEQ + 2 * CONV_PAD
_GLU_SPECS = [pl.BlockSpec((SEQ, 256), functools.partial(lambda i, c: (0, c), c=GLU_OFF // 256 + c)) for c in range(4)]


def _glu_to_pad(a0, a1, b0, b1, zpad):
    C = CONV_CHANNELS
    zpad[0:CONV_PAD, :] = jnp.zeros((CONV_PAD, C), F32)
    zpad[CONV_PAD + SEQ:, :] = jnp.zeros((CONV_PAD, C), F32)
    zpad[CONV_PAD:CONV_PAD + SEQ, 0:256] = a0[...].astype(F32) * jax.nn.sigmoid(b0[...].astype(F32))
    zpad[CONV_PAD:CONV_PAD + SEQ, 256:C] = a1[...].astype(F32) * jax.nn.sigmoid(b1[...].astype(F32))


def _tap_windows(src, base, win):
    for b in range(8):
        win[b, 0:WIN - 8, :] = src[base + b:base + b + WIN - 8, :]


def _taps(win, w_ref, init, out, flip):
    def sub(si, carry):
        r0 = pl.multiple_of(si * SUB, SUB)
        acc = jnp.broadcast_to(init, (SUB, CONV_CHANNELS))
        for k in range(CONV_WIDTH):
            wk = (CONV_WIDTH - 1 - k) if flip else k
            acc = acc + w_ref[wk:wk + 1, :] * win[k % 8, pl.ds(r0 + 8 * (k // 8), SUB), :]
        out[pl.ds(r0, SUB), :] = acc
        return carry

    lax.fori_loop(0, CHUNK // SUB, sub, 0)


def _tap_grads(win, du, dwacc):
    def sub(si, carry):
        r0 = pl.multiple_of(si * SUB, SUB)
        d = du[pl.ds(r0, SUB), :]
        for k in range(CONV_WIDTH):
            p = d * win[k % 8, pl.ds(r0 + 8 * (k // 8), SUB), :]
            dwacc[8 * k:8 * k + 8, :] += (p[0:8] + p[8:16]) + (p[16:24] + p[24:32])
        return carry

    lax.fori_loop(0, CHUNK // SUB, sub, 0)


def _ln_parts(u):
    mu = jnp.mean(u, axis=-1, keepdims=True)
    xc = u - mu
    rstd = lax.rsqrt(jnp.mean(xc * xc, axis=-1, keepdims=True) + EPS)
    return xc * rstd, rstd


def _conv_fwd(proj, conv_w, conv_b, ln_g, ln_b, carry=None):
    T, C = proj.shape[0], CONV_CHANNELS
    vec = pl.BlockSpec((1, C), lambda i: (0, 0))
    c_in, c_out, c_sems = _carry_io(carry)

    def body(*refs):
        a0, a1, b0, b1, w_ref, cb_ref, g_ref, be_ref = refs[:8]
        ci_refs = refs[8:8 + len(c_in)]
        c_ref = refs[8 + len(c_in)]
        co_refs = refs[9 + len(c_in):9 + len(c_in) + len(c_out)]
        zpad, win, ubuf = refs[9 + len(c_in) + len(c_out):12 + len(c_in) + len(c_out)]
        cs_refs = refs[12 + len(c_in) + len(c_out):]
        if carry is not None:
            carry.start(ci_refs, co_refs, cs_refs)
        _glu_to_pad(a0, a1, b0, b1, zpad)
        for ci in range(T // CHUNK):
            _tap_windows(zpad, ci * CHUNK + CONV_PAD - (CONV_WIDTH - 1), win)
            _taps(win, w_ref, cb_ref[...], ubuf, False)
            xh, _ = _ln_parts(ubuf[...])
            ln = xh * g_ref[...] + be_ref[...]
            c_ref[ci * CHUNK:(ci + 1) * CHUNK, :] = (ln * jax.nn.sigmoid(ln)).astype(BF)
        if carry is not None:
            carry.finish(ci_refs, co_refs, cs_refs)

    res = pl.pallas_call(
        body, name="conv_fwd", grid=(1,),
        in_specs=[*_GLU_SPECS, pl.BlockSpec((CONV_PAD, C), lambda i: (0, 0)), vec, vec, vec, *[ANY] * len(c_in)],
        out_specs=[pl.BlockSpec((T, C), lambda i: (0, 0)), *[ANY] * len(c_out)],
        out_shape=[_sds((T, C), BF), *c_out],
        scratch_shapes=[pltpu.VMEM((PAD_ROWS, C), F32), pltpu.VMEM((8, WIN, C), F32), pltpu.VMEM((CHUNK, C), F32),
                        *c_sems],
        compiler_params=_params(("arbitrary",)),
    )(proj, proj, proj, proj, conv_w, conv_b, ln_g, ln_b, *c_in)
    return res[0], res[1:]


def _conv_bwd(proj, d_c, conv_w, conv_b, ln_g, ln_b, carry=None):
    T, C = proj.shape[0], CONV_CHANNELS
    vec = pl.BlockSpec((1, C), lambda i: (0, 0))
    wspec = pl.BlockSpec((CONV_PAD, C), lambda i: (0, 0))
    c_in, c_out, c_sems = _carry_io(carry)

    def body(*refs):
        a0, a1, b0, b1, dc_ref, w_ref, cb_ref, g_ref, be_ref = refs[:9]
        ci_refs = refs[9:9 + len(c_in)]
        o = 9 + len(c_in)
        dglu_ref, dw_ref, dcb_ref, dg_ref, dbe_ref = refs[o:o + 5]
        co_refs = refs[o + 5:o + 5 + len(c_out)]
        zpad, dupad, win, ubuf, dwacc = refs[o + 5 + len(c_out):o + 10 + len(c_out)]
        cs_refs = refs[o + 10 + len(c_out):]
        if carry is not None:
            carry.start(ci_refs, co_refs, cs_refs)
        _glu_to_pad(a0, a1, b0, b1, zpad)
        dupad[T:, :] = jnp.zeros((2 * CONV_PAD, C), F32)
        dwacc[...] = jnp.zeros_like(dwacc)
        dcb_ref[...] = jnp.zeros_like(dcb_ref)
        dg_ref[...] = jnp.zeros_like(dg_ref)
        dbe_ref[...] = jnp.zeros_like(dbe_ref)
        for ci in range(T // CHUNK):
            rows = slice(ci * CHUNK, (ci + 1) * CHUNK)
            _tap_windows(zpad, ci * CHUNK + CONV_PAD - (CONV_WIDTH - 1), win)
            _taps(win, w_ref, cb_ref[...], ubuf, False)
            xh, rstd = _ln_parts(ubuf[...])
            ln = xh * g_ref[...] + be_ref[...]
            sg = jax.nn.sigmoid(ln)
            dln = dc_ref[rows, :].astype(F32) * (sg * (1.0 + ln * (1.0 - sg)))
            dg_ref[...] += jnp.sum(dln * xh, axis=0, keepdims=True)
            dbe_ref[...] += jnp.sum(dln, axis=0, keepdims=True)
            dxh = dln * g_ref[...]
            du = rstd * (dxh - jnp.mean(dxh, axis=-1, keepdims=True)
                         - xh * jnp.mean(dxh * xh, axis=-1, keepdims=True))
            dupad[rows, :] = du
            dcb_ref[...] += jnp.sum(du, axis=0, keepdims=True)
            _tap_grads(win, dupad.at[rows, :], dwacc)
        for k in range(CONV_WIDTH):
            dw_ref[k:k + 1, :] = jnp.sum(dwacc[8 * k:8 * k + 8, :], axis=0, keepdims=True)
        dw_ref[CONV_WIDTH:, :] = jnp.zeros((CONV_PAD - CONV_WIDTH, C), F32)
        for ci in range(T // CHUNK):
            rows = slice(ci * CHUNK, (ci + 1) * CHUNK)
            _tap_windows(dupad, ci * CHUNK, win)
            _taps(win, w_ref, jnp.zeros((1, C), F32), ubuf, True)
            dz = ubuf[...]
            for half, (a, b) in enumerate(((a0, b0), (a1, b1))):
                sb = jax.nn.sigmoid(b[rows, :].astype(F32))
                dzh = dz[:, half * 256:(half + 1) * 256]
                dglu_ref[rows, half * 256:(half + 1) * 256] = (dzh * sb).astype(BF)
                dglu_ref[rows, C + half * 256:C + (half + 1) * 256] = (
                    dzh * a[rows, :].astype(F32) * sb * (1.0 - sb)).astype(BF)
        if carry is not None:
            carry.finish(ci_refs, co_refs, cs_refs)

    res = pl.pallas_call(
        body, name="conv_bwd", grid=(1,),
        in_specs=[*_GLU_SPECS, pl.BlockSpec((T, C), lambda i: (0, 0)), wspec, vec, vec, vec, *[ANY] * len(c_in)],
        out_specs=[pl.BlockSpec((T, 2 * C), lambda i: (0, 0)), wspec, vec, vec, vec, *[ANY] * len(c_out)],
        out_shape=[_sds((T, 2 * C), BF), _sds((CONV_PAD, C), F32), _sds((1, C), F32), _sds((1, C), F32),
                   _sds((1, C), F32), *c_out],
        scratch_shapes=[pltpu.VMEM((PAD_ROWS, C), F32), pltpu.VMEM((PAD_ROWS, C), F32), pltpu.VMEM((8, WIN, C), F32),
                        pltpu.VMEM((CHUNK, C), F32), pltpu.VMEM((8 * CONV_PAD, C), F32), *c_sems],
        compiler_params=_params(("arbitrary",)),
    )(proj, proj, proj, proj, d_c, conv_w, conv_b, ln_g, ln_b, *c_in)
    return res[:5], res[5:]


_GATE_BLK = GATE_OFF // 256


def _ffn_in_swiglu(h2, wf_t, carry=None):
    T, D = h2.shape
    tm, tn = 512, D_FF // 2
    nj, ni = D_FF // tn, T // tm
    c_in, c_out, c_sems = _carry_io(carry)

    def body(*refs):
        a_ref, bg_ref, bu_ref = refs[:3]
        ci_refs = refs[3:3 + len(c_in)]
        act_ref, g_ref, u_ref = refs[3 + len(c_in):6 + len(c_in)]
        co_refs = refs[6 + len(c_in):6 + len(c_in) + len(c_out)]
        cs_refs = refs[6 + len(c_in) + len(c_out):]
        j, i = pl.program_id(0), pl.program_id(1)
        if carry is not None:
            @pl.when((j == 0) & (i == 0))
            def _():
                carry.start(ci_refs, co_refs, cs_refs)
        a = a_ref[...]
        for c0, c1 in ((0, 768), (768, tn)):
            g = lax.dot_general(a, bg_ref[c0:c1, :], _DIMS["NT"], preferred_element_type=F32)
            u = lax.dot_general(a, bu_ref[c0:c1, :], _DIMS["NT"], preferred_element_type=F32)
            act_ref[:, c0:c1] = (g * jax.nn.sigmoid(g) * u).astype(BF)
            g_ref[:, c0:c1] = g.astype(BF)
            u_ref[:, c0:c1] = u.astype(BF)
        if carry is not None:
            @pl.when((j == nj - 1) & (i == ni - 1))
            def _():
                carry.finish(ci_refs, co_refs, cs_refs)

    t = pl.BlockSpec((tm, tn), lambda j, i: (i, j))
    res = pl.pallas_call(
        body, name="ffn_in_swiglu", grid=(nj, ni),
        in_specs=[pl.BlockSpec((tm, D), lambda j, i: (i, 0)), pl.BlockSpec((tn, D), lambda j, i: (j, 0)),
                  pl.BlockSpec((tn, D), lambda j, i: (nj + j, 0)), *[ANY] * len(c_in)],
        out_specs=[t, t, t, *[ANY] * len(c_out)], out_shape=[*[_sds((T, D_FF), BF)] * 3, *c_out],
        scratch_shapes=c_sems,
        compiler_params=_params(("arbitrary", "arbitrary")),
    )(h2, wf_t, wf_t, *c_in)
    return res[:3], res[3:]


def _proj_merge(o, c, wap_t, wcp_t, b_cp, proj):
    T, D = o.shape[0], wap_t.shape[0]
    tm, tg = 1024, 256
    nj = D // tg

    def body(o_ref, c_ref, wa_ref, wc_ref, b_ref, g0_ref, g1_ref, ya_ref, yc_ref, m_ref):
        ya = lax.dot_general(o_ref[...], wa_ref[...], _DIMS["NT"], preferred_element_type=F32)
        yc = lax.dot_general(c_ref[...], wc_ref[...], _DIMS["NT"], preferred_element_type=F32) + b_ref[...]
        ya_ref[...] = ya.astype(BF)
        yc_ref[...] = yc.astype(BF)
        m_ref[...] = (jax.nn.sigmoid(g0_ref[...].astype(F32)) * ya + jax.nn.sigmoid(g1_ref[...].astype(F32)) * yc).astype(BF)

    act = pl.BlockSpec((tm, o.shape[1]), lambda j, i: (i, 0))
    wgt = pl.BlockSpec((tg, o.shape[1]), lambda j, i: (j, 0))
    t = pl.BlockSpec((tm, tg), lambda j, i: (i, j))
    return pl.pallas_call(
        body, name="proj_merge", grid=(nj, T // tm),
        in_specs=[act, act, wgt, wgt, pl.BlockSpec((1, tg), lambda j, i: (0, j)),
                  pl.BlockSpec((tm, tg), lambda j, i: (i, _GATE_BLK + j)),
                  pl.BlockSpec((tm, tg), lambda j, i: (i, _GATE_BLK + nj + j))],
        out_specs=[t, t, t], out_shape=[_sds((T, D), BF)] * 3,
        compiler_params=_params(("arbitrary", "arbitrary")),
    )(o, c, wap_t, wcp_t, b_cp, proj, proj)


def _proj_in_dw(segs, h):
    T, D = h.shape
    tb = 256
    nblk = [seg.shape[1] // tb for seg in segs]
    starts = [sum(nblk[:q]) for q in range(len(segs))]
    n_seg = len(segs)

    def body(*refs):
        seg_refs, h_ref, o_ref, cs_ref = refs[:n_seg], refs[n_seg], refs[n_seg + 1], refs[n_seg + 2]
        i = pl.program_id(0)
        for seg_ref, st, nb in zip(seg_refs, starts, nblk):
            @pl.when((i >= st) & (i < st + nb))
            def _(seg_ref=seg_ref):
                a = seg_ref[...]
                o_ref[...] = lax.dot_general(a, h_ref[...], _DIMS["TN"], preferred_element_type=F32).astype(BF)
                cs_ref[...] = jnp.sum(a.astype(F32), axis=0, keepdims=True)

    in_specs = [pl.BlockSpec((T, tb), functools.partial(lambda i, st, nb: (0, jnp.clip(i - st, 0, nb - 1)), st=st, nb=nb))
                for st, nb in zip(starts, nblk)]
    return pl.pallas_call(
        body, name="proj_in_dw", grid=(sum(nblk),),
        in_specs=[*in_specs, pl.BlockSpec((T, D), lambda i: (0, 0))],
        out_specs=[pl.BlockSpec((tb, D), lambda i: (i, 0)), pl.BlockSpec((1, tb), lambda i: (0, i))],
        out_shape=[_sds((sum(nblk) * tb, D), BF), _sds((1, sum(nblk) * tb), F32)],
        compiler_params=_params(("arbitrary",)),
    )(*segs, h)


def _local_step(x, target, small, wi_t, conv_w, plan):
    T, D = x.shape
    tm = 1024

    def carried(call, res, carry):
        if carry is None:
            return res
        outs, got = res
        plan.done(call, got)
        return outs

    h, r1 = _rms_fwd("rms_mix", x, small["g_mix_norm"])

    def ep_add(acc, ex, outs, ids, scr):
        outs[0][...] = acc + ex[0][...]

    tn_in = IN_WIDTH // 3
    carry = plan.carry("proj_in")
    def ep_bias_bf16(acc, ex, outs, ids, scr):
        outs[0][...] = (acc + ex[0][...]).astype(BF)

    proj, = carried("proj_in", _matmul("proj_in", [h], wi_t, "NT", m=T, n=IN_WIDTH, tm=tm, tn=tn_in,
                                       epilogue=ep_bias_bf16, extra=[(small["b_in"], _row(tn_in))],
                                       outs=[(_sds((T, IN_WIDTH), BF), _tile(tm, tn_in))], carry=carry), carry)
    plan.launch("gather_ffn", after=proj)
    o, got = _attn_fwd(proj, small["sinks"], carry=plan.carry("attn_fwd"))
    plan.done("attn_fwd", got)
    c, got = _conv_fwd(proj, conv_w, small["conv_b"], small["ln_g"], small["ln_b"], carry=plan.carry("conv_fwd"))
    plan.done("conv_fwd", got)
    wap_t, wcp_t, w_out = plan.weight("w_attn_proj"), plan.weight("w_conv_proj"), plan.weight("w_out")
    ya, yc, merged = _proj_merge(o, c, wap_t, wcp_t, small["b_conv_proj"], proj)

    tg = 256
    gate_specs = [pl.BlockSpec((tm, tg), lambda j, i, k: (i, _GATE_BLK + j)),
                  pl.BlockSpec((tm, tg), lambda j, i, k: (i, _GATE_BLK + D // tg + j))]

    def ep_residual_rms(acc, ex, outs, ids, scr):
        x2v = acc + ex[0][...]
        r = lax.rsqrt(jnp.mean(x2v * x2v, axis=-1, keepdims=True) + EPS)
        outs[0][...] = x2v
        outs[1][...] = (x2v * r * ex[1][...]).astype(BF)
        outs[2][...] = r

    carry = plan.carry("out_proj")
    x2, h2, r2 = carried("out_proj", _matmul(
        "out_proj_rms", [merged], w_out, "NN", m=T, n=D, tm=512, tn=D, epilogue=ep_residual_rms,
        extra=[(x, _tile(512, D)), (small["g_ffn_norm"], _row(D))],
        outs=[(_sds((T, D), F32), _tile(512, D)), (_sds((T, D), BF), _tile(512, D)),
              (_sds((T, 1), F32), pl.BlockSpec((512, 1), lambda j, i, k: (i, 0)))], carry=carry), carry)
    plan.launch("gather_down", after=x2)
    wf_t = plan.weight("w_ffn_in")
    (act, gate, up), got = _ffn_in_swiglu(h2, wf_t, carry=plan.carry("ffn_in_swiglu"))
    plan.done("ffn_in_swiglu", got)
    w_down = plan.weight("w_ffn_down")
    def ep_residual_loss(acc, ex, outs, ids, scr):
        dx, dg, part = _loss_head(acc + ex[0][...], ex[1][...], ex[2][...])
        outs[0][...] = dx
        outs[1][...] = dx.astype(BF)
        _accumulate_rows(outs[2], dg, ids[1] == 0)
        _accumulate_rows(outs[3], part, ids[1] == 0)

    dx3, dx3_b, dg_final, loss = _matmul(
        "ffn_down_loss", [act], w_down, "NN", m=T, n=D, tm=512, tn=D, epilogue=ep_residual_loss,
        extra=[(x2, _tile(512, D)), (small["g_final"], _row(D)), (target, _tile(512, D))],
        outs=[(_sds((T, D), F32), _tile(512, D)), (_sds((T, D), BF), _tile(512, D)), (_sds((1, D), F32), _row(D)),
              (_sds((1, 1), F32), pl.BlockSpec((1, 1), lambda j, i, k: (0, 0)))])

    tn_ff = D_FF // 2

    def ep_swiglu_bwd(acc, ex, outs, ids, scr):
        g, u = ex[0][...].astype(F32), ex[1][...].astype(F32)
        sg = jax.nn.sigmoid(g)
        outs[0][...] = (acc * u * sg * (1.0 + g * (1.0 - sg))).astype(BF)
        outs[1][...] = (acc * g * sg).astype(BF)

    dgate, dup = _matmul(
        "ffn_down_bwd", [dx3_b], w_down, "NT", m=T, n=D_FF, tm=512, tn=tn_ff, epilogue=ep_swiglu_bwd,
        extra=[(gate, _tile(512, tn_ff)), (up, _tile(512, tn_ff))],
        outs=[(_sds((T, D_FF), BF), _tile(512, tn_ff)), (_sds((T, D_FF), BF), _tile(512, tn_ff))])

    def dw(name, a, b, rows, cols, row_off=0, alias=None, total_rows=None, colsum=False):
        total_rows = rows if total_rows is None else total_rows
        tmw = rows if rows <= 1024 else D_FF // 2
        blk, rem = divmod(row_off, tmw)
        assert rem == 0

        def ep(acc, ex, outs, ids, scr):
            outs[0][...] = acc.astype(BF)
            if colsum:
                outs[1][...] = jnp.sum(ex[0][...].astype(F32), axis=0, keepdims=True)

        outs = [(_sds((total_rows, cols), BF), pl.BlockSpec((tmw, cols), lambda j, i, k: (blk + i, j)))]
        extra = []
        if colsum:
            extra = [(a, pl.BlockSpec((T, tmw), lambda j, i, k: (0, i)))]
            outs.append((_sds((1, rows), F32), pl.BlockSpec((1, tmw), lambda j, i, k: (0, i))))
        carry = plan.carry(name)
        res = carried(name, _matmul(name, [a], b, "TN", m=rows, n=cols, tm=tmw, tn=cols, epilogue=ep, extra=extra,
                                    outs=outs, alias=None if alias is None else (alias, 0), carry=carry), carry)
        return res if colsum else res[0]

    plan.grad_ready(dict(w_ffn_down=dw("ffn_down_dw", act, dx3_b, D_FF, D)))

    def ep_rms_bwd(acc, ex, outs, ids, scr):
        dx, dg = _rms_bwd(acc, ex[0][...], ex[1][...], ex[2][...])
        dx = ex[3][...] + dx
        outs[0][...] = dx
        outs[1][...] = dx.astype(BF)
        _accumulate_rows(outs[2], dg, ids[1] == 0)

    def rms_bwd_io(tm_, xin, r, g, dres):
        return dict(
            extra=[(xin, _tile(tm_, D)), (r, pl.BlockSpec((tm_, 1), lambda j, i, k: (i, 0))), (g, _row(D)),
                   (dres, _tile(tm_, D))],
            outs=[(_sds((T, D), F32), _tile(tm_, D)), (_sds((T, D), BF), _tile(tm_, D)), (_sds((1, D), F32), _row(D))])

    carry = plan.carry("ffn_in_bwd")
    dx2, dx2_b, dg_ffn = carried(
        "ffn_in_bwd",
        _matmul("ffn_in_bwd", [dgate, dup], wf_t, "NN", m=T, n=D, tm=tm, tn=D, tk=D_FF // 2, epilogue=ep_rms_bwd,
                carry=carry, **rms_bwd_io(tm, x2, r2, small["g_ffn_norm"], dx3)), carry)
    plan.launch("send_down")
    gwf_t = dw("ffn_in_dw_gate", dgate, h2, D_FF, D, total_rows=2 * D_FF)
    gwf_t = dw("ffn_in_dw_up", dup, h2, D_FF, D, row_off=D_FF, alias=gwf_t, total_rows=2 * D_FF)
    plan.grad_ready(dict(w_ffn_in=gwf_t))

    def ep_merge_bwd(acc, ex, outs, ids, scr):
        s0 = jax.nn.sigmoid(ex[2][...].astype(F32))
        s1 = jax.nn.sigmoid(ex[3][...].astype(F32))
        outs[0][...] = (acc * s0).astype(BF)
        outs[1][...] = (acc * s1).astype(BF)
        outs[2][...] = (acc * ex[0][...].astype(F32) * s0 * (1.0 - s0)).astype(BF)
        outs[3][...] = (acc * ex[1][...].astype(F32) * s1 * (1.0 - s1)).astype(BF)

    carry = plan.carry("out_proj_bwd_merge")
    dya, dyc, dg0, dg1 = carried(
        "out_proj_bwd_merge",
        _matmul("out_proj_bwd_merge", [dx2_b], w_out, "NT", m=T, n=D, tm=tm, tn=tg, epilogue=ep_merge_bwd,
                extra=[(ya, _tile(tm, tg)), (yc, _tile(tm, tg)), (proj, gate_specs[0]), (proj, gate_specs[1])],
                outs=[(_sds((T, D), BF), _tile(tm, tg))] * 4, carry=carry), carry)
    plan.launch("send_ffn")
    gw_out = dw("out_proj_dw", merged, dx2_b, D, D)
    d_o, = _matmul("attn_proj_bwd", [dya], wap_t, "NN", m=T, n=ATTN_WIDTH, tm=tm, tn=ATTN_WIDTH,
                   epilogue=_store(BF), outs=[(_sds((T, ATTN_WIDTH), BF), _tile(tm, ATTN_WIDTH))])
    d_c, = _matmul("conv_proj_bwd", [dyc], wcp_t, "NN", m=T, n=CONV_CHANNELS, tm=tm, tn=CONV_CHANNELS,
                   epilogue=_store(BF), outs=[(_sds((T, CONV_CHANNELS), BF), _tile(tm, CONV_CHANNELS))])
    gwap_t = dw("attn_proj_dw", dya, o, D, ATTN_WIDTH)
    gwcp_t, db_cp = dw("conv_proj_dw", dyc, c, D, CONV_CHANNELS, colsum=True)
    plan.grad_ready(dict(w_out=gw_out, w_attn_proj=gwap_t, w_conv_proj=gwcp_t))
    (dglu, dcw, dcb, dlng, dlnb), got = _conv_bwd(proj, d_c, conv_w, small["conv_b"], small["ln_g"], small["ln_b"],
                                                  carry=plan.carry("conv_bwd"))
    plan.done("conv_bwd", got)
    plan.launch("send_mix")
    (dqkv, dsinks), got = _attn_bwd(proj, d_o, small["sinks"], carry=plan.carry("attn_bwd"))
    plan.done("attn_bwd", got)

    segs = [dqkv, dglu, dg0, dg1]
    gwi_t, db_in = _proj_in_dw(segs, h)
    plan.grad_ready(dict(w_in=gwi_t))
    plan.alone("swap_inp")
    plan.launch("send_inp")
    carry = plan.carry("proj_in_bwd")
    dx, _, dg_mix = carried(
        "proj_in_bwd",
        _matmul("proj_in_bwd", segs, wi_t, "NN", m=T, n=D, tm=512, tn=D, epilogue=ep_rms_bwd, carry=carry,
                **rms_bwd_io(512, x, r1, small["g_mix_norm"], dx2)), carry)

    parts = dict(g_mix_norm=dg_mix, b_in=db_in, sinks=dsinks, conv_w=dcw, conv_b=dcb, ln_g=dlng, ln_b=dlnb,
                 b_conv_proj=db_cp, g_ffn_norm=dg_ffn, g_final=dg_final, loss=loss)
    return dx, parts


def _place():
    x, y, c = lax.axis_index("x"), lax.axis_index("y"), lax.axis_index("c")
    return x, y, c, [(1 - x, y), (x, 1 - y), (1 - x, 1 - y)]


def _gather_copies(x_refs, out_refs, rows_per, send_sems, recv_sems, local_sems):
    x, y, c, chips = _place()
    me, sibling = (x, y, c), (x, y, 1 - c)

    def rows(a, px, py, pc):
        return out_refs[a].at[pl.ds((4 * px + 2 * py + pc) * rows_per[a], rows_per[a])]

    def copy(a, k, block, to, src=None):
        return pltpu.make_async_remote_copy(
            src_ref=rows(a, *block) if src is None else src, dst_ref=rows(a, *block),
            send_sem=send_sems.at[7 * a + k], recv_sem=recv_sems.at[7 * a + k], device_id=to, device_id_type=MESH)

    def local(a):
        return pltpu.make_async_copy(x_refs[a], rows(a, *me), local_sems.at[a])

    def first(a):
        return [copy(a, 0, me, sibling, src=x_refs[a])] + [copy(a, 1 + j, me, (*chip, c), src=x_refs[a])
                                                          for j, chip in enumerate(chips)]

    def arrive(a, j):
        return copy(a, 1 + j, (*chips[j], c), me)

    def passed(a, j):
        return copy(a, 4 + j, (*chips[j], c), sibling)

    def from_sibling(a):
        return [copy(a, 0, sibling, me)] + [copy(a, 4 + j, (*chip, 1 - c), me) for j, chip in enumerate(chips)]

    return len(x_refs), local, first, arrive, passed, from_sibling


def _gather_start(*refs):
    n, local, first, _, _, _ = _gather_copies(*refs)
    for a in range(n):
        local(a).start()
        for cp in first(a):
            cp.start()


def _gather_finish(*refs):
    n, local, first, arrive, passed, from_sibling = _gather_copies(*refs)
    for a in range(n):
        for j in range(3):
            arrive(a, j).wait_recv()
            passed(a, j).start()
    for a in range(n):
        for cp in from_sibling(a):
            cp.wait_recv()
    for a in range(n):
        for cp in first(a) + [passed(a, j) for j in range(3)]:
            cp.wait_send()
        local(a).wait()


def _gather_peers():
    x, y, c, chips = _place()
    return [(x, y, 1 - c)] + [(*chip, c) for chip in chips]


def _gather_sems(n):
    return [pltpu.SemaphoreType.DMA((7 * n,)), pltpu.SemaphoreType.DMA((7 * n,)), pltpu.SemaphoreType.DMA((n,))]


def _gather_carry(shards):
    rows_per = [s.shape[0] for s in shards]
    return _Carry(shards, [_sds((N_DEV * s.shape[0],) + s.shape[1:], s.dtype) for s in shards],
                  _gather_sems(len(shards)),
                  lambda ins, outs, sems: _gather_start(ins, outs, rows_per, *sems),
                  lambda ins, outs, sems: _gather_finish(ins, outs, rows_per, *sems), _gather_peers)


def _first_gather(shards):
    n = len(shards)
    rows_per = [s.shape[0] for s in shards]

    def body(*refs):
        x_refs, out_refs = refs[:n], refs[n:2 * n]
        send_sems, recv_sems, local_sems = refs[2 * n:]
        x, y, c, chips = _place()
        me, sibling = (x, y, c), (x, y, 1 - c)
        near_x, near_y, far = (*chips[0], c), (*chips[1], c), (*chips[2], c)

        def rows(a, dev, part):
            h = rows_per[a] // 2
            lo, size = {"all": (0, 2 * h), "low": (0, h), "high": (h, h)}[part]
            return out_refs[a].at[pl.ds((4 * dev[0] + 2 * dev[1] + dev[2]) * rows_per[a] + lo, size)]

        def copy(a, k, block, part, to, src=None):
            return pltpu.make_async_remote_copy(
                src_ref=rows(a, block, part) if src is None else src, dst_ref=rows(a, block, part),
                send_sem=send_sems.at[9 * a + k], recv_sem=recv_sems.at[9 * a + k], device_id=to, device_id_type=MESH)

        other = lambda dev: (dev[0], dev[1], 1 - c)
        sent = []
        for a in range(n):
            pltpu.make_async_copy(x_refs[a], rows(a, me, "all"), local_sems.at[a]).start()
            sent += [copy(a, 0, me, "all", sibling, src=x_refs[a]), copy(a, 1, me, "all", near_x, src=x_refs[a]),
                     copy(a, 2, me, "all", near_y, src=x_refs[a])]
        for cp in sent:
            cp.start()
        for a in range(n):
            copy(a, 1, near_x, "all", me).wait_recv()
            copy(a, 2, near_y, "all", me).wait_recv()
            passed = [copy(a, 3, near_y, "high", near_x), copy(a, 4, near_x, "low", near_y),
                      copy(a, 5, near_x, "all", sibling), copy(a, 6, near_y, "all", sibling)]
            for cp in passed:
                cp.start()
            sent += passed
        for a in range(n):
            copy(a, 3, far, "high", me).wait_recv()
            copy(a, 4, far, "low", me).wait_recv()
            passed = [copy(a, 7, far, "high", sibling), copy(a, 8, far, "low", sibling)]
            for cp in passed:
                cp.start()
            sent += passed
        for a in range(n):
            copy(a, 0, sibling, "all", me).wait_recv()
            copy(a, 5, other(near_x), "all", me).wait_recv()
            copy(a, 6, other(near_y), "all", me).wait_recv()
            copy(a, 7, other(far), "high", me).wait_recv()
            copy(a, 8, other(far), "low", me).wait_recv()
        for cp in sent:
            cp.wait_send()
        for a in range(n):
            pltpu.make_async_copy(x_refs[a], rows(a, me, "all"), local_sems.at[a]).wait()

    return pl.pallas_call(
        body, name="weights_first_gather", in_specs=[ANY] * n, out_specs=[ANY] * n,
        out_shape=[_sds((N_DEV * s.shape[0],) + s.shape[1:], s.dtype) for s in shards],
        scratch_shapes=[pltpu.SemaphoreType.DMA((9 * n,)), pltpu.SemaphoreType.DMA((9 * n,)),
                        pltpu.SemaphoreType.DMA((n,))],
    )(*shards)


def _swap_carry(grads):
    n = len(grads)

    def copies(g_refs, out_refs, sems):
        send_sems, recv_sems = sems
        x, y, c, _ = _place()
        return [pltpu.make_async_remote_copy(
            src_ref=g_refs[a].at[2 * p + 1 - c], dst_ref=out_refs[a].at[p],
            send_sem=send_sems.at[4 * a + p], recv_sem=recv_sems.at[4 * a + p],
            device_id=(x, y, 1 - c), device_id_type=MESH) for a in range(n) for p in range(4)]

    def start(ins, outs, sems):
        for cp in copies(ins, outs, sems):
            cp.start()

    def finish(ins, outs, sems):
        for cp in copies(ins, outs, sems):
            cp.wait()

    def peers():
        x, y, c, _ = _place()
        return [(x, y, 1 - c)]

    return _Carry(grads, [_sds((4,) + g.shape[1:], g.dtype) for g in grads],
                  [pltpu.SemaphoreType.DMA((4 * n,)), pltpu.SemaphoreType.DMA((4 * n,))], start, finish, peers)


def _join(carries):
    carries = [c for c in carries if c is not None]
    if not carries:
        return None
    n_in = [len(c.arrays) for c in carries]
    n_out = [len(c.out_shapes) for c in carries]
    n_sem = [len(c.sems) for c in carries]

    def parts(refs, counts):
        cuts = [sum(counts[:q]) for q in range(len(counts) + 1)]
        return [refs[cuts[q]:cuts[q + 1]] for q in range(len(counts))]

    def start(ins, outs, sems):
        for c, i, o, s in zip(carries, parts(ins, n_in), parts(outs, n_out), parts(sems, n_sem)):
            c.start(i, o, s)

    def finish(ins, outs, sems):
        for c, i, o, s in zip(carries, parts(ins, n_in), parts(outs, n_out), parts(sems, n_sem)):
            c.finish(i, o, s)

    return _Carry([a for c in carries for a in c.arrays], [o for c in carries for o in c.out_shapes],
                  [s for c in carries for s in c.sems], start, finish)


def _run_carry(name, carry):
    n_in, n_out = len(carry.arrays), len(carry.out_shapes)

    def body(*refs):
        carry.start(refs[:n_in], refs[n_in:n_in + n_out], refs[n_in + n_out:])
        carry.finish(refs[:n_in], refs[n_in:n_in + n_out], refs[n_in + n_out:])

    return pl.pallas_call(body, name=name, in_specs=[ANY] * n_in, out_specs=[ANY] * n_out,
                          out_shape=carry.out_shapes, scratch_shapes=carry.sems)(*carry.arrays)


def _run_carry_async(name, carry, collective_id):
    ins = [jax.new_ref(a, memory_space=pltpu.MemorySpace.HBM) for a in carry.arrays]
    outs = [jax.empty_ref(o, memory_space=pltpu.MemorySpace.HBM) for o in carry.out_shapes]

    @pl.kernel(mesh=plsc.ScalarSubcoreMesh(axis_name="sequencer", num_cores=1), name=name,
               scratch_types=tuple(carry.sems), compiler_params=pltpu.CompilerParams(collective_id=collective_id))
    def launch(*sems):
        barrier = pltpu.get_barrier_semaphore()
        peers = carry.peers()
        for peer in peers:
            pl.semaphore_signal(barrier, inc=1, device_id=peer, device_id_type=MESH)
        pl.semaphore_wait(barrier, len(peers))
        carry.start(ins, outs, sems)
        carry.finish(ins, outs, sems)

    launch()
    return [o[...] for o in outs]


def _chip_sum(name, g, got, c):
    _, rows, cols = g.shape

    def body(c_ref, g_ref, got_ref, o_ref):
        o_ref[...] = (g_ref[...].astype(F32) + got_ref[...].astype(F32)).astype(BF)

    return pl.pallas_call(
        body, name=name,
        grid_spec=pltpu.PrefetchScalarGridSpec(
            num_scalar_prefetch=1, grid=(4,),
            in_specs=[pl.BlockSpec((1, rows, cols), lambda p, c_ref: (2 * p + c_ref[0], 0, 0)),
                      pl.BlockSpec((1, rows, cols), lambda p, c_ref: (p, 0, 0))],
            out_specs=pl.BlockSpec((1, rows, cols), lambda p, c_ref: (p, 0, 0))),
        out_shape=_sds((4, rows, cols), BF),
        compiler_params=_params(("arbitrary",)),
    )(c, g, got)


def _send_carry(sums, ks):
    n, nk = len(sums), len(ks)

    def copies(s_refs, out_refs, sems):
        send_sems, recv_sems = sems
        x, y, c, chips = _place()
        return [pltpu.make_async_remote_copy(
            src_ref=s_refs[a].at[2 * chips[k][0] + chips[k][1]], dst_ref=out_refs[a].at[q],
            send_sem=send_sems.at[nk * a + q], recv_sem=recv_sems.at[nk * a + q],
            device_id=(*chips[k], c), device_id_type=MESH) for a in range(n) for q, k in enumerate(ks)]

    def start(ins, outs, sems):
        for cp in copies(ins, outs, sems):
            cp.start()

    def finish(ins, outs, sems):
        for cp in copies(ins, outs, sems):
            cp.wait()

    def peers():
        x, y, c, chips = _place()
        return [(*chips[k], c) for k in ks]

    return _Carry(sums, [_sds((nk,) + s.shape[1:], s.dtype) for s in sums],
                  [pltpu.SemaphoreType.DMA((nk * n,)), pltpu.SemaphoreType.DMA((nk * n,))], start, finish, peers)


def _adam_math(w, g, m, v):
    m = ADAM_B1 * m + (1.0 - ADAM_B1) * g
    v = ADAM_B2 * v + (1.0 - ADAM_B2) * (g * g)
    m_hat = m / (1.0 - ADAM_B1 ** ADAM_STEP)
    v_hat = v / (1.0 - ADAM_B2 ** ADAM_STEP)
    delta = -ADAM_LR * (m_hat / (jnp.sqrt(v_hat) + ADAM_EPS) + ADAM_WD * w)
    return delta, m, v


def _adamw(name, w, g, m, v):
    rows, cols = w.shape
    tr = 256 if rows % 256 == 0 else rows

    def body(w_ref, g_ref, m_ref, v_ref, d_ref, nm_ref, nv_ref):
        d_ref[...], nm_ref[...], nv_ref[...] = _adam_math(w_ref[...], g_ref[...], m_ref[...], v_ref[...])

    t = pl.BlockSpec((tr, cols), lambda i: (i, 0))
    return pl.pallas_call(
        body, name=name, grid=(rows // tr,), in_specs=[t] * 4, out_specs=[t] * 3,
        out_shape=[_sds((rows, cols), F32)] * 3, compiler_params=_params(("arbitrary",)),
    )(w, g, m, v)


def _grad_adamw(name, g, got, got3, ids, w, m, v):
    _, rows, cols = g.shape
    n3 = len(got3)
    tr = rows // 2 if rows >= 256 else rows

    def body(ids_ref, g_ref, got_ref, *rest):
        w_ref, m_ref, v_ref, o_ref, d_ref, nm_ref, nv_ref = rest[n3:]
        tot = g_ref[0].astype(F32) + got_ref[0].astype(F32)
        for r_ref in rest[:n3]:
            for q in range(r_ref.shape[0]):
                tot = tot + r_ref[q].astype(F32)
        o_ref[...] = tot
        d_ref[...], nm_ref[...], nv_ref[...] = _adam_math(w_ref[...], tot, m_ref[...], v_ref[...])

    tile = pl.BlockSpec((tr, cols), lambda i, ids_ref: (i, 0))
    return pl.pallas_call(
        body, name=name,
        grid_spec=pltpu.PrefetchScalarGridSpec(
            num_scalar_prefetch=1, grid=(rows // tr,),
            in_specs=[pl.BlockSpec((1, tr, cols), lambda i, ids_ref: (ids_ref[0], i, 0)),
                      pl.BlockSpec((1, tr, cols), lambda i, ids_ref: (ids_ref[1], i, 0)),
                      *[pl.BlockSpec((r.shape[0], tr, cols), lambda i, ids_ref: (0, i, 0)) for r in got3],
                      tile, tile, tile],
            out_specs=[tile] * 4),
        out_shape=[_sds((rows, cols), F32)] * 4,
        compiler_params=_params(("arbitrary",)),
    )(ids, g, got, *got3, w, m, v)


SMALL_NAMES = ["g_mix_norm", "b_in", "sinks", "conv_b", "ln_g", "ln_b", "b_conv_proj", "g_ffn_norm", "g_final"]
_PACK_ROWS = 32


def _small_pack(parts):
    C = CONV_CHANNELS
    part_list = [parts["g_mix_norm"], parts["b_in"], parts["sinks"], parts["conv_b"], parts["ln_g"], parts["ln_b"],
                 parts["b_conv_proj"], parts["g_ffn_norm"], parts["g_final"], parts["loss"], parts["conv_w"]]

    def body(p_mix, p_b, p_sink, p_cb, p_lg, p_lb, p_bcp, p_ffn, p_fin, p_loss, p_cw, pack):
        pack[...] = jnp.zeros_like(pack)
        pack[0:1, :] = p_mix[...]
        pack[1:2, 0:GLU_OFF] = p_b[:, 0:GLU_OFF]
        pack[2:3, :] = p_b[:, GLU_OFF:GATE_OFF]
        pack[3:4, :] = p_b[:, GATE_OFF:GATE_OFF + D_MODEL]
        pack[4:5, :] = p_b[:, GATE_OFF + D_MODEL:]
        pack[5:6, 0:128] = p_sink[...]
        pack[6:7, 0:C] = p_cb[...]
        pack[6:7, C:2 * C] = p_lg[...]
        pack[7:8, 0:C] = p_lb[...]
        pack[8:9, :] = p_bcp[...]
        pack[9:10, :] = p_ffn[...]
        pack[10:11, :] = p_fin[...]
        pack[11:12, 0:128] = jnp.broadcast_to(p_loss[...], (1, 128))
        pack[12:28, 0:C] = p_cw[0:16, :]
        pack[12:28, C:2 * C] = p_cw[16:32, :]

    vm = pl.BlockSpec(memory_space=pltpu.VMEM)
    return pl.pallas_call(body, name="small_pack", in_specs=[vm] * len(part_list), out_specs=vm,
                          out_shape=_sds((_PACK_ROWS, D_MODEL), F32))(*part_list)


def _small_adamw(gathered, small_w, small_m, small_v):
    C = CONV_CHANNELS
    names = SMALL_NAMES
    widths = [small_w[k].shape[1] for k in names]
    n_small = len(names)

    def body(*refs):
        tot_ref = refs[0]
        w_refs = refs[1:1 + n_small]
        m_refs = refs[1 + n_small:1 + 2 * n_small]
        v_refs = refs[1 + 2 * n_small:1 + 3 * n_small]
        o = 1 + 3 * n_small
        loss_ref, cw_ref = refs[o], refs[o + 1]
        out_refs = refs[o + 2:o + 2 + 4 * n_small]
        tot = tot_ref[0:_PACK_ROWS, :]
        for d in range(1, N_DEV):
            tot = tot + tot_ref[d * _PACK_ROWS:(d + 1) * _PACK_ROWS, :]
        loss_ref[...] = tot[11:12, 0:1]
        cw_ref[0:16, :] = tot[12:28, 0:C]
        cw_ref[16:32, :] = tot[12:28, C:2 * C]
        grads = dict(
            g_mix_norm=tot[0:1, :],
            b_in=jnp.concatenate([tot[1:2, 0:GLU_OFF], tot[2:3, :], tot[3:4, :], tot[4:5, :]], axis=1),
            sinks=tot[5:6, 0:N_Q_HEADS], conv_b=tot[6:7, 0:C], ln_g=tot[6:7, C:2 * C], ln_b=tot[7:8, 0:C],
            b_conv_proj=tot[8:9, :], g_ffn_norm=tot[9:10, :], g_final=tot[10:11, :])
        for s, k in enumerate(names):
            g = grads[k]
            d, nm, nv = _adam_math(w_refs[s][...], g, m_refs[s][...], v_refs[s][...])
            out_refs[4 * s][...] = g
            out_refs[4 * s + 1][...] = d
            out_refs[4 * s + 2][...] = nm
            out_refs[4 * s + 3][...] = nv

    vm = pl.BlockSpec(memory_space=pltpu.VMEM)
    args = [gathered, *[small_w[k] for k in names], *[small_m[k] for k in names], *[small_v[k] for k in names]]
    out_shape = [_sds((1, 1), F32), _sds((CONV_PAD, C), F32)]
    for wd in widths:
        out_shape += [_sds((1, wd), F32)] * 4
    res = pl.pallas_call(
        body, name="small_adamw",
        in_specs=[vm] * len(args), out_specs=[vm] * len(out_shape), out_shape=out_shape,
        compiler_params=pltpu.CompilerParams(vmem_limit_bytes=VMEM_LIMIT_BYTES),
    )(*args)
    return res[0], res[1], {k: res[2 + 4 * s:6 + 4 * s] for s, k in enumerate(names)}


BIG = dict(w_in=True, w_attn_proj=True, w_conv_proj=True, w_out=False, w_ffn_in=True, w_ffn_down=False)
WEIGHT_NAMES = ["g_mix_norm", "w_in", "b_in", "sinks", "conv_w", "conv_b", "ln_g", "ln_b", "w_attn_proj",
                "w_conv_proj", "b_conv_proj", "w_out", "g_ffn_norm", "w_ffn_in", "w_ffn_down", "g_final"]


class _Plan:
    GROUPS = dict(down=["w_ffn_down"], ffn=["w_ffn_in"], mix=["w_out", "w_attn_proj", "w_conv_proj"], inp=["w_in"])
    ALL = (0, 1, 2)
    RIDES = dict(
        gather_mix=[("gather", ["w_attn_proj", "w_conv_proj", "w_out"])], gather_ffn=[("gather", ["w_ffn_in"])],
        gather_down=[("gather", ["w_ffn_down"])],
        ffn_in_bwd=[("swap", "down")], send_down=[("send", "down", ALL)],
        out_proj_bwd_merge=[("swap", "ffn")], send_ffn=[("send", "ffn", ALL)],
        conv_bwd=[("swap", "mix")], send_mix=[("send", "mix", ALL)],
        swap_inp=[("swap", "inp")], send_inp=[("send", "inp", ALL)])
    ASYNC = dict(gather_mix=1, gather_ffn=2, gather_down=3, send_down=4, send_ffn=5, send_mix=6, send_inp=7)

    def __init__(self, shards, c1):
        self.shards, self.c1 = shards, c1
        self.full, self.slots, self.got, self.sums, self.got3 = {}, {}, {}, {}, {}

    def weight(self, name):
        return self.full[name]

    def grad_ready(self, grads):
        for k, g in grads.items():
            self.slots[k] = g.reshape(N_DEV, g.shape[0] // N_DEV, g.shape[1])

    def _one(self, kind, what, ks=None):
        if kind == "gather":
            return _gather_carry([self.shards[k] for k in what])
        names = self.GROUPS[what]
        if kind == "swap":
            return _swap_carry([self.slots[k] for k in names])
        return _send_carry([self.sums[k] for k in names], ks)

    def carry(self, call):
        return _join([self._one(*ride) for ride in self.RIDES.get(call, [])])

    def done(self, call, outs):
        outs = list(outs)
        for kind, what, *_ in self.RIDES.get(call, []):
            names = what if kind == "gather" else self.GROUPS[what]
            mine, outs = outs[:len(names)], outs[len(names):]
            if kind == "gather":
                self.full.update(zip(names, mine))
            elif kind == "send":
                for k, r in zip(names, mine):
                    self.got3.setdefault(k, []).append(r)
            else:
                for k, r in zip(names, mine):
                    self.got[k] = r
                    self.sums[k] = _chip_sum(f"chip_sum_{k}", self.slots[k], r, self.c1)

    def alone(self, call):
        self.done(call, _run_carry(call, self.carry(call)))

    def launch(self, call, after=None):
        carry = self._one(*self.RIDES[call][0])
        if after is not None:
            carry.arrays = list(lax.optimization_barrier((tuple(carry.arrays), after))[0])
        self.done(call, _run_carry_async(call, carry, self.ASYNC[call]))


def kernel(x, g_mix_norm, w_in, b_in, sinks, conv_w, conv_b, ln_g, ln_b, w_attn_proj, w_conv_proj, b_conv_proj, w_out, g_ffn_norm, w_ffn_in, w_ffn_down, g_final, loss_target, m_g_mix_norm, m_w_in, m_b_in, m_sinks, m_conv_w, m_conv_b, m_ln_g, m_ln_b, m_w_attn_proj, m_w_conv_proj, m_b_conv_proj, m_w_out, m_g_ffn_norm, m_w_ffn_in, m_w_ffn_down, m_g_final, v_g_mix_norm, v_w_in, v_b_in, v_sinks, v_conv_w, v_conv_b, v_ln_g, v_ln_b, v_w_attn_proj, v_w_conv_proj, v_b_conv_proj, v_w_out, v_g_ffn_norm, v_w_ffn_in, v_w_ffn_down, v_g_final):
    w = dict(g_mix_norm=g_mix_norm, w_in=w_in, b_in=b_in, sinks=sinks, conv_w=conv_w, conv_b=conv_b, ln_g=ln_g,
             ln_b=ln_b, w_attn_proj=w_attn_proj, w_conv_proj=w_conv_proj, b_conv_proj=b_conv_proj, w_out=w_out,
             g_ffn_norm=g_ffn_norm, w_ffn_in=w_ffn_in, w_ffn_down=w_ffn_down, g_final=g_final)
    m = dict(g_mix_norm=m_g_mix_norm, w_in=m_w_in, b_in=m_b_in, sinks=m_sinks, conv_w=m_conv_w, conv_b=m_conv_b,
             ln_g=m_ln_g, ln_b=m_ln_b, w_attn_proj=m_w_attn_proj, w_conv_proj=m_w_conv_proj,
             b_conv_proj=m_b_conv_proj, w_out=m_w_out, g_ffn_norm=m_g_ffn_norm, w_ffn_in=m_w_ffn_in,
             w_ffn_down=m_w_ffn_down, g_final=m_g_final)
    v = dict(g_mix_norm=v_g_mix_norm, w_in=v_w_in, b_in=v_b_in, sinks=v_sinks, conv_w=v_conv_w, conv_b=v_conv_b,
             ln_g=v_ln_g, ln_b=v_ln_b, w_attn_proj=v_w_attn_proj, w_conv_proj=v_w_conv_proj,
             b_conv_proj=v_b_conv_proj, w_out=v_w_out, g_ffn_norm=v_g_ffn_norm, w_ffn_in=v_w_ffn_in,
             w_ffn_down=v_w_ffn_down, g_final=v_g_final)
    ax, ay, ac = lax.axis_index("x"), lax.axis_index("y"), lax.axis_index("c")
    me = 4 * ax + 2 * ay + ac
    chip = 2 * ax + ay

    shards = {k: (w[k][0].T if tr else w[k][0]).astype(BF) for k, tr in BIG.items()}
    cw_shard = jnp.pad(conv_w[0].T, ((0, 0), (0, 1))).reshape(16, 128)
    wi_t, cw_full = _first_gather([shards["w_in"], cw_shard])
    conv_full = cw_full.reshape(CONV_CHANNELS, CONV_PAD).T

    as_row = lambda a: a.reshape(1, -1)
    small_w = {k: as_row(w[k]) for k in SMALL_NAMES}
    small_m = {k: as_row(m[k]) for k in SMALL_NAMES}
    small_v = {k: as_row(v[k]) for k in SMALL_NAMES}
    plan = _Plan(shards, ac.reshape(1).astype(jnp.int32))
    plan.launch("gather_mix", after=wi_t)
    dx, parts = _local_step(x[0], loss_target[0], small_w, wi_t, conv_full, plan)

    small_gathered, = _run_carry_async("small_gather", _gather_carry([_small_pack(parts)]), 8)

    ids = jnp.stack([me, chip]).astype(jnp.int32)
    grads, delta, new_m, new_v, after = {}, {}, {}, {}, dx
    for k in sorted(BIG, key=lambda k: k == "w_in"):
        flip = (lambda a: a.T) if BIG[k] else (lambda a: a)
        wk = lax.optimization_barrier((w[k][0], after))[0]
        outs = _grad_adamw(f"grad_adamw_{k}", plan.slots[k], plan.got[k], plan.got3[k], ids,
                           flip(wk), flip(m[k][0]), flip(v[k][0]))
        after = outs[0]
        grads[k], delta[k], new_m[k], new_v[k] = (flip(a)[None] for a in outs)

    loss, cw_grad, small_out = _small_adamw(small_gathered, small_w, small_m, small_v)
    for k in SMALL_NAMES:
        g, d, nm, nv = (a.reshape(w[k].shape) for a in small_out[k])
        grads[k], delta[k], new_m[k], new_v[k] = g, d, nm, nv
    cw_mine = lax.dynamic_slice(cw_grad, (0, me * 64), (CONV_WIDTH, 64))
    d, nm, nv = _adamw("adamw_conv_w", conv_w[0], cw_mine, m_conv_w[0], v_conv_w[0])
    grads["conv_w"], delta["conv_w"], new_m["conv_w"], new_v["conv_w"] = cw_mine[None], d[None], nm[None], nv[None]

    return (loss.reshape(()), dx[None], *[grads[k] for k in WEIGHT_NAMES], *[delta[k] for k in WEIGHT_NAMES],
            *[new_m[k] for k in WEIGHT_NAMES], *[new_v[k] for k in WEIGHT_NAMES])
```

```python
import functools

import jax
import jax.numpy as jnp
from jax import lax
from jax.experimental import pallas as pl
from jax.experimental.pallas import tpu as pltpu
from jax.experimental.pallas import tpu_sc as plsc

F32 = jnp.float32
BF = jnp.bfloat16

SEQ = 2048
D_MODEL = 1024
HEAD_DIM = 64
N_Q_HEADS = 8
N_KV_HEADS = 2
GROUP = N_Q_HEADS // N_KV_HEADS
BLOCK = 128
ATTN_WIDTH = 512
KV_WIDTH = 128
CONV_CHANNELS = 512
CONV_WIDTH = 31
CONV_PAD = 32
GLU_OFF = 768
GATE_OFF = 1792
IN_WIDTH = 3840
D_FF = 2816
EPS = 1e-5
NEG = -1e30
N_DEV = 8

ADAM_LR = 0.001
ADAM_B1 = 0.9
ADAM_B2 = 0.999
ADAM_EPS = 1e-08
ADAM_WD = 0.01
ADAM_STEP = 10

VMEM_LIMIT_BYTES = 56 * 1024 * 1024
MESH = pl.DeviceIdType.MESH
ANY = pl.BlockSpec(memory_space=pl.ANY)

_DIMS = {"NN": (((1,), (0,)), ((), ())), "NT": (((1,), (1,)), ((), ())), "TN": (((0,), (0,)), ((), ()))}


def _params(sem):
    return pltpu.CompilerParams(dimension_semantics=sem, vmem_limit_bytes=VMEM_LIMIT_BYTES)


class _Carry:
    def __init__(self, arrays, out_shapes, sems, start, finish, peers=None):
        self.arrays, self.out_shapes, self.sems, self.start, self.finish = arrays, out_shapes, sems, start, finish
        self.peers = peers


def _carry_io(carry):
    if carry is None:
        return [], [], []
    return list(carry.arrays), list(carry.out_shapes), list(carry.sems)


def _matmul(name, a_list, b, mode, *, m, n, tm, tn, tk=None, epilogue, extra=(), outs, b_off=(0, 0), alias=None,
            scratch=(), carry=None):
    seg_k = [a.shape[0] if mode == "TN" else a.shape[1] for a in a_list]
    whole = tk is None
    seg_nk = [1] * len(a_list) if whole else [ks // tk for ks in seg_k]
    nk = 1 if whole else sum(seg_nk)
    starts = [sum(seg_nk[:s]) for s in range(len(seg_nk))]
    k_starts = [sum(seg_k[:s]) for s in range(len(seg_k))]
    k_tot = sum(seg_k)
    n_a, n_extra, n_out = len(a_list), len(extra), len(outs)

    a_specs = []
    for st, ns, ks in zip(starts, seg_nk, seg_k):
        if mode == "TN":
            a_specs.append(pl.BlockSpec((ks if whole else tk, tm), lambda j, i, k: (k, i)))
        elif whole:
            a_specs.append(pl.BlockSpec((tm, ks), lambda j, i, k: (i, 0)))
        else:
            a_specs.append(pl.BlockSpec((tm, tk), functools.partial(
                lambda j, i, k, st, ns: (i, jnp.clip(k - st, 0, ns - 1)), st=st, ns=ns)))
    bk = k_tot if whole else tk
    if mode == "NT":
        b_spec = pl.BlockSpec((tn, bk), lambda j, i, k: (b_off[0] + j, b_off[1] + k))
    else:
        b_spec = pl.BlockSpec((bk, tn), lambda j, i, k: (b_off[0] + k, b_off[1] + j))
    n_alias = 0 if alias is None else 1
    c_in, c_out, c_sems = _carry_io(carry)
    n_acc = 0 if whole else 1
    nj, ni = n // tn, m // tm

    def body(*refs):
        pos = [n_a, 1, n_alias, n_extra, len(c_in), n_out, len(c_out), n_acc, len(scratch), len(c_sems)]
        cuts = [sum(pos[:q]) for q in range(len(pos) + 1)]
        a_refs, (b_ref,), _, ex, ci_refs, out_refs, co_refs, acc_refs, scr, cs_refs = (
            refs[cuts[q]:cuts[q + 1]] for q in range(len(pos)))
        j, i, k = pl.program_id(0), pl.program_id(1), pl.program_id(2)
        ids = (j, i)
        if carry is not None:
            @pl.when((j == 0) & (i == 0) & (k == 0))
            def _():
                carry.start(ci_refs, co_refs, cs_refs)

        def dot(a_ref, bv):
            return lax.dot_general(a_ref[...].astype(BF), bv.astype(BF), _DIMS[mode], preferred_element_type=F32)

        if whole:
            tot = None
            for a_ref, k0, ks in zip(a_refs, k_starts, seg_k):
                if n_a == 1:
                    bv = b_ref[...]
                else:
                    bv = b_ref[:, k0:k0 + ks] if mode == "NT" else b_ref[k0:k0 + ks, :]
                part = dot(a_ref, bv)
                tot = part if tot is None else tot + part
            epilogue(tot, ex, out_refs, ids, scr)
        else:
            acc, = acc_refs

            @pl.when(k == 0)
            def _():
                acc[...] = jnp.zeros_like(acc)

            for a_ref, st, ns in zip(a_refs, starts, seg_nk):
                if n_a == 1:
                    acc[...] += dot(a_ref, b_ref[...])
                else:
                    @pl.when((k >= st) & (k < st + ns))
                    def _(a_ref=a_ref):
                        acc[...] += dot(a_ref, b_ref[...])

            @pl.when(k == nk - 1)
            def _():
                epilogue(acc[...], ex, out_refs, ids, scr)

        if carry is not None:
            @pl.when((j == nj - 1) & (i == ni - 1) & (k == nk - 1))
            def _():
                carry.finish(ci_refs, co_refs, cs_refs)

    in_specs = [*a_specs, b_spec]
    args = [*a_list, b]
    io_alias = {}
    if alias is not None:
        in_specs.append(pl.BlockSpec(memory_space=pl.ANY))
        args.append(alias[0])
        io_alias = {n_a + 1: alias[1]}
    in_specs += [s for _, s in extra] + [pl.BlockSpec(memory_space=pl.ANY)] * len(c_in)
    args += [x for x, _ in extra] + c_in
    res = pl.pallas_call(
        body, name=name, grid=(nj, ni, nk), in_specs=in_specs,
        out_specs=[s for _, s in outs] + [pl.BlockSpec(memory_space=pl.ANY)] * len(c_out),
        out_shape=[o for o, _ in outs] + c_out,
        scratch_shapes=[*([] if whole else [pltpu.VMEM((tm, tn), F32)]), *scratch, *c_sems],
        input_output_aliases=io_alias,
        compiler_params=_params(("arbitrary", "arbitrary", "arbitrary")),
    )(*args)
    return res if carry is None else (res[:n_out], res[n_out:])


def _tile(tm, tn):
    return pl.BlockSpec((tm, tn), lambda j, i, k: (i, j))


def _row(tn):
    return pl.BlockSpec((1, tn), lambda j, i, k: (0, j))


def _store(dtype):
    def ep(acc, ex, outs, ids, scr):
        outs[0][...] = acc.astype(dtype)
    return ep


def _sds(shape, dtype):
    return jax.ShapeDtypeStruct(shape, dtype)


def _rms_fwd(name, x, g):
    T, D = x.shape
    tm = 512

    def body(x_ref, g_ref, h_ref, r_ref):
        xv = x_ref[...]
        r = lax.rsqrt(jnp.mean(xv * xv, axis=-1, keepdims=True) + EPS)
        h_ref[...] = (xv * r * g_ref[...]).astype(BF)
        r_ref[...] = r

    return pl.pallas_call(
        body, name=name, grid=(T // tm,),
        in_specs=[pl.BlockSpec((tm, D), lambda i: (i, 0)), pl.BlockSpec((1, D), lambda i: (0, 0))],
        out_specs=[pl.BlockSpec((tm, D), lambda i: (i, 0)), pl.BlockSpec((tm, 1), lambda i: (i, 0))],
        out_shape=[_sds((T, D), BF), _sds((T, 1), F32)],
        compiler_params=_params(("arbitrary",)),
    )(x, g)


def _rms_bwd(dh, xv, r, g):
    xh = xv * r
    dxh = dh * g
    dx = r * (dxh - xh * jnp.mean(dxh * xh, axis=-1, keepdims=True))
    return dx, jnp.sum(dh * xh, axis=0, keepdims=True)


def _accumulate_rows(ref, val, first):
    @pl.when(first)
    def _():
        ref[...] = val

    @pl.when(jnp.logical_not(first))
    def _():
        ref[...] += val


def _loss_head(xv, g, target):
    r = lax.rsqrt(jnp.mean(xv * xv, axis=-1, keepdims=True) + EPS)
    err = xv * r * g - target
    dx, dg = _rms_bwd(err * (1.0 / xv.shape[-1]), xv, r, g)
    part = 0.5 * jnp.sum(jnp.mean(err * err, axis=-1, keepdims=True), axis=0, keepdims=True)
    return dx, dg, part


def _lane_half(shape, h):
    lane = lax.broadcasted_iota(jnp.int32, shape, 1)
    return (lane >= HEAD_DIM * h) & (lane < HEAD_DIM * (h + 1))


def _to_half(v, w, h):
    if w != h:
        v = pltpu.roll(v, HEAD_DIM, 1)
    return jnp.where(_lane_half(v.shape, h), v, 0.0)


def _attn_block(qkv_ref, sinks_ref, n, h):
    r0 = pl.multiple_of(n * BLOCK, BLOCK)
    p0 = pl.multiple_of(jnp.maximum(n - 1, 0) * BLOCK, BLOCK)
    rows = pl.ds(r0, BLOCK)
    prev = pl.ds(p0, BLOCK)
    k2 = jnp.concatenate([qkv_ref[prev, ATTN_WIDTH:ATTN_WIDTH + KV_WIDTH],
                          qkv_ref[rows, ATTN_WIDTH:ATTN_WIDTH + KV_WIDTH]], axis=0)
    v2 = jnp.concatenate([qkv_ref[prev, ATTN_WIDTH + KV_WIDTH:ATTN_WIDTH + 2 * KV_WIDTH],
                          qkv_ref[rows, ATTN_WIDTH + KV_WIDTH:ATTN_WIDTH + 2 * KV_WIDTH]], axis=0)
    qs = []
    for g in range(GROUP):
        hq = GROUP * h + g
        blk = qkv_ref[rows, (hq // 2) * 128:(hq // 2 + 1) * 128].astype(F32)
        qs.append(_to_half(blk, hq % 2, h))
    q4 = jnp.concatenate(qs, axis=0).astype(BF)
    s = lax.dot_general(q4, k2, _DIMS["NT"], preferred_element_type=F32) * (HEAD_DIM ** -0.5)
    shape = s.shape
    row = lax.broadcasted_iota(jnp.int32, shape, 0)
    qi = row & (BLOCK - 1)
    kj = lax.broadcasted_iota(jnp.int32, shape, 1)
    diff = qi + BLOCK - kj
    valid = (diff >= 0) & (diff < BLOCK) & ((kj >= BLOCK) | (n > 0))
    s = jnp.where(valid, s, NEG)
    row1 = lax.broadcasted_iota(jnp.int32, (shape[0], 1), 0)
    sink = jnp.zeros((shape[0], 1), F32)
    for g in range(GROUP):
        sink = jnp.where((row1 >= g * BLOCK) & (row1 < (g + 1) * BLOCK), sinks_ref[0, GROUP * h + g], sink)
    m = jnp.maximum(jnp.max(s, axis=-1, keepdims=True), sink)
    e = jnp.exp(s - m)
    es = jnp.exp(sink - m)
    inv = 1.0 / (jnp.sum(e, axis=-1, keepdims=True) + es)
    return e * inv, es * inv, q4, k2, v2, rows, prev


def _attn_fwd(proj, sinks, carry=None):
    T = proj.shape[0]
    c_in, c_out, c_sems = _carry_io(carry)

    def body(*refs):
        qkv_ref, sinks_ref = refs[:2]
        ci_refs = refs[2:2 + len(c_in)]
        o_ref = refs[2 + len(c_in)]
        co_refs = refs[3 + len(c_in):3 + len(c_in) + len(c_out)]
        cs_refs = refs[3 + len(c_in) + len(c_out):]
        if carry is not None:
            carry.start(ci_refs, co_refs, cs_refs)

        def blk(n, z):
            outs = [None] * (N_Q_HEADS // 2)
            for h in range(N_KV_HEADS):
                p, _, _, _, v2, rows, _ = _attn_block(qkv_ref, sinks_ref, n, h)
                o = lax.dot_general(p.astype(BF), v2, _DIMS["NN"], preferred_element_type=F32)
                for g in range(GROUP):
                    hq = GROUP * h + g
                    piece = jnp.where(_lane_half((BLOCK, 128), h), o[g * BLOCK:(g + 1) * BLOCK], 0.0)
                    if hq % 2 != h:
                        piece = pltpu.roll(piece, HEAD_DIM, 1)
                    outs[hq // 2] = piece if outs[hq // 2] is None else outs[hq // 2] + piece
            for pb in range(N_Q_HEADS // 2):
                o_ref[rows, pb * 128:(pb + 1) * 128] = outs[pb].astype(BF)
            return z

        lax.fori_loop(0, T // BLOCK, blk, 0)
        if carry is not None:
            carry.finish(ci_refs, co_refs, cs_refs)

    res = pl.pallas_call(
        body, name="attn_fwd", grid=(1,),
        in_specs=[pl.BlockSpec((T, GLU_OFF), lambda i: (0, 0)), pl.BlockSpec(memory_space=pltpu.SMEM),
                  *[ANY] * len(c_in)],
        out_specs=[pl.BlockSpec((T, ATTN_WIDTH), lambda i: (0, 0)), *[ANY] * len(c_out)],
        out_shape=[_sds((T, ATTN_WIDTH), BF), *c_out], scratch_shapes=c_sems,
        compiler_params=_params(("arbitrary",)),
    )(proj, sinks, *c_in)
    return res[0], res[1:]


def _attn_bwd(proj, d_o, sinks, carry=None):
    T = proj.shape[0]
    c_in, c_out, c_sems = _carry_io(carry)

    def body(*refs):
        qkv_ref, do_ref, sinks_ref = refs[:3]
        ci_refs = refs[3:3 + len(c_in)]
        dqkv_ref, dsink_ref = refs[3 + len(c_in):5 + len(c_in)]
        co_refs = refs[5 + len(c_in):5 + len(c_in) + len(c_out)]
        dk_acc, dv_acc = refs[5 + len(c_in) + len(c_out):7 + len(c_in) + len(c_out)]
        cs_refs = refs[7 + len(c_in) + len(c_out):]
        if carry is not None:
            carry.start(ci_refs, co_refs, cs_refs)
        dsink_ref[...] = jnp.zeros_like(dsink_ref)
        dk_acc[...] = jnp.zeros_like(dk_acc)
        dv_acc[...] = jnp.zeros_like(dv_acc)

        def blk(n, carry):
            dqs = [None] * (N_Q_HEADS // 2)
            for h in range(N_KV_HEADS):
                p, psink, q4, k2, v2, rows, prev = _attn_block(qkv_ref, sinks_ref, n, h)
                dos = []
                for g in range(GROUP):
                    hq = GROUP * h + g
                    dos.append(_to_half(do_ref[rows, (hq // 2) * 128:(hq // 2 + 1) * 128].astype(F32), hq % 2, h))
                do4 = jnp.concatenate(dos, axis=0).astype(BF)
                dp = lax.dot_general(do4, v2, _DIMS["NT"], preferred_element_type=F32)
                delta = jnp.sum(p * dp, axis=-1, keepdims=True)
                ds = (p * (dp - delta) * (HEAD_DIM ** -0.5)).astype(BF)
                dsk = psink * delta
                for g in range(GROUP):
                    hq = GROUP * h + g
                    tot = -jnp.sum(dsk[g * BLOCK:(g + 1) * BLOCK], axis=0, keepdims=True)
                    lane = lax.broadcasted_iota(jnp.int32, (1, 128), 1)
                    dsink_ref[...] += jnp.where(lane == hq, tot, 0.0)
                dq = lax.dot_general(ds, k2, _DIMS["NN"], preferred_element_type=F32)
                dk = lax.dot_general(ds, q4, _DIMS["TN"], preferred_element_type=F32)
                dv = lax.dot_general(p.astype(BF), do4, _DIMS["TN"], preferred_element_type=F32)
                dk_acc[prev, :] += dk[:BLOCK]
                dk_acc[rows, :] += dk[BLOCK:]
                dv_acc[prev, :] += dv[:BLOCK]
                dv_acc[rows, :] += dv[BLOCK:]
                for g in range(GROUP):
                    hq = GROUP * h + g
                    piece = jnp.where(_lane_half((BLOCK, 128), h), dq[g * BLOCK:(g + 1) * BLOCK], 0.0)
                    if hq % 2 != h:
                        piece = pltpu.roll(piece, HEAD_DIM, 1)
                    dqs[hq // 2] = piece if dqs[hq // 2] is None else dqs[hq // 2] + piece
            for pb in range(N_Q_HEADS // 2):
                dqkv_ref[rows, pb * 128:(pb + 1) * 128] = dqs[pb].astype(BF)
            return carry

        lax.fori_loop(0, T // BLOCK, blk, 0)
        dqkv_ref[:, ATTN_WIDTH:ATTN_WIDTH + KV_WIDTH] = dk_acc[...].astype(BF)
        dqkv_ref[:, ATTN_WIDTH + KV_WIDTH:] = dv_acc[...].astype(BF)
        if carry is not None:
            carry.finish(ci_refs, co_refs, cs_refs)

    res = pl.pallas_call(
        body, name="attn_bwd", grid=(1,),
        in_specs=[pl.BlockSpec((T, GLU_OFF), lambda i: (0, 0)), pl.BlockSpec((T, ATTN_WIDTH), lambda i: (0, 0)),
                  pl.BlockSpec(memory_space=pltpu.SMEM), *[ANY] * len(c_in)],
        out_specs=[pl.BlockSpec((T, GLU_OFF), lambda i: (0, 0)), pl.BlockSpec((1, 128), lambda i: (0, 0)),
                   *[ANY] * len(c_out)],
        out_shape=[_sds((T, GLU_OFF), BF), _sds((1, 128), F32), *c_out],
        scratch_shapes=[pltpu.VMEM((T, KV_WIDTH), F32), pltpu.VMEM((T, KV_WIDTH), F32), *c_sems],
        compiler_params=_params(("arbitrary",)),
    )(proj, d_o, sinks, *c_in)
    return res[:2], res[2:]


CHUNK = 256
SUB = 32
WIN = CHUNK + 32
PAD_ROWS = SEQ + 2 * CONV_PAD
_GLU_SPECS = [pl.BlockSpec((SEQ, 256), functools.partial(lambda i, c: (0, c), c=GLU_OFF // 256 + c)) for c in range(4)]


def _glu_to_pad(a0, a1, b0, b1, zpad):
    C = CONV_CHANNELS
    zpad[0:CONV_PAD, :] = jnp.zeros((CONV_PAD, C), F32)
    zpad[CONV_PAD + SEQ:, :] = jnp.zeros((CONV_PAD, C), F32)
    zpad[CONV_PAD:CONV_PAD + SEQ, 0:256] = a0[...].astype(F32) * jax.nn.sigmoid(b0[...].astype(F32))
    zpad[CONV_PAD:CONV_PAD + SEQ, 256:C] = a1[...].astype(F32) * jax.nn.sigmoid(b1[...].astype(F32))


def _tap_windows(src, base, win):
    for b in range(8):
        win[b, 0:WIN - 8, :] = src[base + b:base + b + WIN - 8, :]


def _taps(win, w_ref, init, out, flip):
    def sub(si, carry):
        r0 = pl.multiple_of(si * SUB, SUB)
        acc = jnp.broadcast_to(init, (SUB, CONV_CHANNELS))
        for k in range(CONV_WIDTH):
            wk = (CONV_WIDTH - 1 - k) if flip else k
            acc = acc + w_ref[wk:wk + 1, :] * win[k % 8, pl.ds(r0 + 8 * (k // 8), SUB), :]
        out[pl.ds(r0, SUB), :] = acc
        return carry

    lax.fori_loop(0, CHUNK // SUB, sub, 0)


def _tap_grads(win, du, dwacc):
    def sub(si, carry):
        r0 = pl.multiple_of(si * SUB, SUB)
        d = du[pl.ds(r0, SUB), :]
        for k in range(CONV_WIDTH):
            p = d * win[k % 8, pl.ds(r0 + 8 * (k // 8), SUB), :]
            dwacc[8 * k:8 * k + 8, :] += (p[0:8] + p[8:16]) + (p[16:24] + p[24:32])
        return carry

    lax.fori_loop(0, CHUNK // SUB, sub, 0)


def _ln_parts(u):
    mu = jnp.mean(u, axis=-1, keepdims=True)
    xc = u - mu
    rstd = lax.rsqrt(jnp.mean(xc * xc, axis=-1, keepdims=True) + EPS)
    return xc * rstd, rstd


def _conv_fwd(proj, conv_w, conv_b, ln_g, ln_b, carry=None):
    T, C = proj.shape[0], CONV_CHANNELS
    vec = pl.BlockSpec((1, C), lambda i: (0, 0))
    c_in, c_out, c_sems = _carry_io(carry)

    def body(*refs):
        a0, a1, b0, b1, w_ref, cb_ref, g_ref, be_ref = refs[:8]
        ci_refs = refs[8:8 + len(c_in)]
        c_ref = refs[8 + len(c_in)]
        co_refs = refs[9 + len(c_in):9 + len(c_in) + len(c_out)]
        zpad, win, ubuf = refs[9 + len(c_in) + len(c_out):12 + len(c_in) + len(c_out)]
        cs_refs = refs[12 + len(c_in) + len(c_out):]
        if carry is not None:
            carry.start(ci_refs, co_refs, cs_refs)
        _glu_to_pad(a0, a1, b0, b1, zpad)
        for ci in range(T // CHUNK):
            _tap_windows(zpad, ci * CHUNK + CONV_PAD - (CONV_WIDTH - 1), win)
            _taps(win, w_ref, cb_ref[...], ubuf, False)
            xh, _ = _ln_parts(ubuf[...])
            ln = xh * g_ref[...] + be_ref[...]
            c_ref[ci * CHUNK:(ci + 1) * CHUNK, :] = (ln * jax.nn.sigmoid(ln)).astype(BF)
        if carry is not None:
            carry.finish(ci_refs, co_refs, cs_refs)

    res = pl.pallas_call(
        body, name="conv_fwd", grid=(1,),
        in_specs=[*_GLU_SPECS, pl.BlockSpec((CONV_PAD, C), lambda i: (0, 0)), vec, vec, vec, *[ANY] * len(c_in)],
        out_specs=[pl.BlockSpec((T, C), lambda i: (0, 0)), *[ANY] * len(c_out)],
        out_shape=[_sds((T, C), BF), *c_out],
        scratch_shapes=[pltpu.VMEM((PAD_ROWS, C), F32), pltpu.VMEM((8, WIN, C), F32), pltpu.VMEM((CHUNK, C), F32),
                        *c_sems],
        compiler_params=_params(("arbitrary",)),
    )(proj, proj, proj, proj, conv_w, conv_b, ln_g, ln_b, *c_in)
    return res[0], res[1:]


def _conv_bwd(proj, d_c, conv_w, conv_b, ln_g, ln_b, carry=None):
    T, C = proj.shape[0], CONV_CHANNELS
    vec = pl.BlockSpec((1, C), lambda i: (0, 0))
    wspec = pl.BlockSpec((CONV_PAD, C), lambda i: (0, 0))
    c_in, c_out, c_sems = _carry_io(carry)

    def body(*refs):
        a0, a1, b0, b1, dc_ref, w_ref, cb_ref, g_ref, be_ref = refs[:9]
        ci_refs = refs[9:9 + len(c_in)]
        o = 9 + len(c_in)
        dglu_ref, dw_ref, dcb_ref, dg_ref, dbe_ref = refs[o:o + 5]
        co_refs = refs[o + 5:o + 5 + len(c_out)]
        zpad, dupad, win, ubuf, dwacc = refs[o + 5 + len(c_out):o + 10 + len(c_out)]
        cs_refs = refs[o + 10 + len(c_out):]
        if carry is not None:
            carry.start(ci_refs, co_refs, cs_refs)
        _glu_to_pad(a0, a1, b0, b1, zpad)
        dupad[T:, :] = jnp.zeros((2 * CONV_PAD, C), F32)
        dwacc[...] = jnp.zeros_like(dwacc)
        dcb_ref[...] = jnp.zeros_like(dcb_ref)
        dg_ref[...] = jnp.zeros_like(dg_ref)
        dbe_ref[...] = jnp.zeros_like(dbe_ref)
        for ci in range(T // CHUNK):
            rows = slice(ci * CHUNK, (ci + 1) * CHUNK)
            _tap_windows(zpad, ci * CHUNK + CONV_PAD - (CONV_WIDTH - 1), win)
            _taps(win, w_ref, cb_ref[...], ubuf, False)
            xh, rstd = _ln_parts(ubuf[...])
            ln = xh * g_ref[...] + be_ref[...]
            sg = jax.nn.sigmoid(ln)
            dln = dc_ref[rows, :].astype(F32) * (sg * (1.0 + ln * (1.0 - sg)))
            dg_ref[...] += jnp.sum(dln * xh, axis=0, keepdims=True)
            dbe_ref[...] += jnp.sum(dln, axis=0, keepdims=True)
            dxh = dln * g_ref[...]
            du = rstd * (dxh - jnp.mean(dxh, axis=-1, keepdims=True)
                         - xh * jnp.mean(dxh * xh, axis=-1, keepdims=True))
            dupad[rows, :] = du
            dcb_ref[...] += jnp.sum(du, axis=0, keepdims=True)
            _tap_grads(win, dupad.at[rows, :], dwacc)
        for k in range(CONV_WIDTH):
            dw_ref[k:k + 1, :] = jnp.sum(dwacc[8 * k:8 * k + 8, :], axis=0, keepdims=True)
        dw_ref[CONV_WIDTH:, :] = jnp.zeros((CONV_PAD - CONV_WIDTH, C), F32)
        for ci in range(T // CHUNK):
            rows = slice(ci * CHUNK, (ci + 1) * CHUNK)
            _tap_windows(dupad, ci * CHUNK, win)
            _taps(win, w_ref, jnp.zeros((1, C), F32), ubuf, True)
            dz = ubuf[...]
            for half, (a, b) in enumerate(((a0, b0), (a1, b1))):
                sb = jax.nn.sigmoid(b[rows, :].astype(F32))
                dzh = dz[:, half * 256:(half + 1) * 256]
                dglu_ref[rows, half * 256:(half + 1) * 256] = (dzh * sb).astype(BF)
                dglu_ref[rows, C + half * 256:C + (half + 1) * 256] = (
                    dzh * a[rows, :].astype(F32) * sb * (1.0 - sb)).astype(BF)
        if carry is not None:
            carry.finish(ci_refs, co_refs, cs_refs)

    res = pl.pallas_call(
        body, name="conv_bwd", grid=(1,),
        in_specs=[*_GLU_SPECS, pl.BlockSpec((T, C), lambda i: (0, 0)), wspec, vec, vec, vec, *[ANY] * len(c_in)],
        out_specs=[pl.BlockSpec((T, 2 * C), lambda i: (0, 0)), wspec, vec, vec, vec, *[ANY] * len(c_out)],
        out_shape=[_sds((T, 2 * C), BF), _sds((CONV_PAD, C), F32), _sds((1, C), F32), _sds((1, C), F32),
                   _sds((1, C), F32), *c_out],
        scratch_shapes=[pltpu.VMEM((PAD_ROWS, C), F32), pltpu.VMEM((PAD_ROWS, C), F32), pltpu.VMEM((8, WIN, C), F32),
                        pltpu.VMEM((CHUNK, C), F32), pltpu.VMEM((8 * CONV_PAD, C), F32), *c_sems],
        compiler_params=_params(("arbitrary",)),
    )(proj, proj, proj, proj, d_c, conv_w, conv_b, ln_g, ln_b, *c_in)
    return res[:5], res[5:]


_GATE_BLK = GATE_OFF // 256


def _ffn_in_swiglu(h2, wf_t, carry=None):
    T, D = h2.shape
    tm, tn = 512, D_FF // 2
    nj, ni = D_FF // tn, T // tm
    c_in, c_out, c_sems = _carry_io(carry)

    def body(*refs):
        a_ref, bg_ref, bu_ref = refs[:3]
        ci_refs = refs[3:3 + len(c_in)]
        act_ref, g_ref, u_ref = refs[3 + len(c_in):6 + len(c_in)]
        co_refs = refs[6 + len(c_in):6 + len(c_in) + len(c_out)]
        cs_refs = refs[6 + len(c_in) + len(c_out):]
        j, i = pl.program_id(0), pl.program_id(1)
        if carry is not None:
            @pl.when((j == 0) & (i == 0))
            def _():
                carry.start(ci_refs, co_refs, cs_refs)
        a = a_ref[...]
        for c0, c1 in ((0, 768), (768, tn)):
            g = lax.dot_general(a, bg_ref[c0:c1, :], _DIMS["NT"], preferred_element_type=F32)
            u = lax.dot_general(a, bu_ref[c0:c1, :], _DIMS["NT"], preferred_element_type=F32)
            act_ref[:, c0:c1] = (g * jax.nn.sigmoid(g) * u).astype(BF)
            g_ref[:, c0:c1] = g.astype(BF)
            u_ref[:, c0:c1] = u.astype(BF)
        if carry is not None:
            @pl.when((j == nj - 1) & (i == ni - 1))
            def _():
                carry.finish(ci_refs, co_refs, cs_refs)

    t = pl.BlockSpec((tm, tn), lambda j, i: (i, j))
    res = pl.pallas_call(
        body, name="ffn_in_swiglu", grid=(nj, ni),
        in_specs=[pl.BlockSpec((tm, D), lambda j, i: (i, 0)), pl.BlockSpec((tn, D), lambda j, i: (j, 0)),
                  pl.BlockSpec((tn, D), lambda j, i: (nj + j, 0)), *[ANY] * len(c_in)],
        out_specs=[t, t, t, *[ANY] * len(c_out)], out_shape=[*[_sds((T, D_FF), BF)] * 3, *c_out],
        scratch_shapes=c_sems,
        compiler_params=_params(("arbitrary", "arbitrary")),
    )(h2, wf_t, wf_t, *c_in)
    return res[:3], res[3:]


def _proj_merge(o, c, wap_t, wcp_t, b_cp, proj):
    T, D = o.shape[0], wap_t.shape[0]
    tm, tg = 1024, 256
    nj = D // tg

    def body(o_ref, c_ref, wa_ref, wc_ref, b_ref, g0_ref, g1_ref, ya_ref, yc_ref, m_ref):
        ya = lax.dot_general(o_ref[...], wa_ref[...], _DIMS["NT"], preferred_element_type=F32)
        yc = lax.dot_general(c_ref[...], wc_ref[...], _DIMS["NT"], preferred_element_type=F32) + b_ref[...]
        ya_ref[...] = ya.astype(BF)
        yc_ref[...] = yc.astype(BF)
        m_ref[...] = (jax.nn.sigmoid(g0_ref[...].astype(F32)) * ya + jax.nn.sigmoid(g1_ref[...].astype(F32)) * yc).astype(BF)

    act = pl.BlockSpec((tm, o.shape[1]), lambda j, i: (i, 0))
    wgt = pl.BlockSpec((tg, o.shape[1]), lambda j, i: (j, 0))
    t = pl.BlockSpec((tm, tg), lambda j, i: (i, j))
    return pl.pallas_call(
        body, name="proj_merge", grid=(nj, T // tm),
        in_specs=[act, act, wgt, wgt, pl.BlockSpec((1, tg), lambda j, i: (0, j)),
                  pl.BlockSpec((tm, tg), lambda j, i: (i, _GATE_BLK + j)),
                  pl.BlockSpec((tm, tg), lambda j, i: (i, _GATE_BLK + nj + j))],
        out_specs=[t, t, t], out_shape=[_sds((T, D), BF)] * 3,
        compiler_params=_params(("arbitrary", "arbitrary")),
    )(o, c, wap_t, wcp_t, b_cp, proj, proj)


def _proj_in_dw(segs, h):
    T, D = h.shape
    tb = 256
    nblk = [seg.shape[1] // tb for seg in segs]
    starts = [sum(nblk[:q]) for q in range(len(segs))]
    n_seg = len(segs)

    def body(*refs):
        seg_refs, h_ref, o_ref, cs_ref = refs[:n_seg], refs[n_seg], refs[n_seg + 1], refs[n_seg + 2]
        i = pl.program_id(0)
        for seg_ref, st, nb in zip(seg_refs, starts, nblk):
            @pl.when((i >= st) & (i < st + nb))
            def _(seg_ref=seg_ref):
                a = seg_ref[...]
                o_ref[...] = lax.dot_general(a, h_ref[...], _DIMS["TN"], preferred_element_type=F32).astype(BF)
                cs_ref[...] = jnp.sum(a.astype(F32), axis=0, keepdims=True)

    in_specs = [pl.BlockSpec((T, tb), functools.partial(lambda i, st, nb: (0, jnp.clip(i - st, 0, nb - 1)), st=st, nb=nb))
                for st, nb in zip(starts, nblk)]
    return pl.pallas_call(
        body, name="proj_in_dw", grid=(sum(nblk),),
        in_specs=[*in_specs, pl.BlockSpec((T, D), lambda i: (0, 0))],
        out_specs=[pl.BlockSpec((tb, D), lambda i: (i, 0)), pl.BlockSpec((1, tb), lambda i: (0, i))],
        out_shape=[_sds((sum(nblk) * tb, D), BF), _sds((1, sum(nblk) * tb), F32)],
        compiler_params=_params(("arbitrary",)),
    )(*segs, h)


def _local_step(x, target, small, wi_t, conv_w, plan):
    T, D = x.shape
    tm = 1024

    def carried(call, res, carry):
        if carry is None:
            return res
        outs, got = res
        plan.done(call, got)
        return outs

    h, r1 = _rms_fwd("rms_mix", x, small["g_mix_norm"])

    def ep_add(acc, ex, outs, ids, scr):
        outs[0][...] = acc + ex[0][...]

    tn_in = IN_WIDTH // 3
    carry = plan.carry("proj_in")
    def ep_bias_bf16(acc, ex, outs, ids, scr):
        outs[0][...] = (acc + ex[0][...]).astype(BF)

    proj, = carried("proj_in", _matmul("proj_in", [h], wi_t, "NT", m=T, n=IN_WIDTH, tm=tm, tn=tn_in,
                                       epilogue=ep_bias_bf16, extra=[(small["b_in"], _row(tn_in))],
                                       outs=[(_sds((T, IN_WIDTH), BF), _tile(tm, tn_in))], carry=carry), carry)
    plan.launch("gather_ffn", after=proj)
    o, got = _attn_fwd(proj, small["sinks"], carry=plan.carry("attn_fwd"))
    plan.done("attn_fwd", got)
    c, got = _conv_fwd(proj, conv_w, small["conv_b"], small["ln_g"], small["ln_b"], carry=plan.carry("conv_fwd"))
    plan.done("conv_fwd", got)
    wap_t, wcp_t, w_out = plan.weight("w_attn_proj"), plan.weight("w_conv_proj"), plan.weight("w_out")
    ya, yc, merged = _proj_merge(o, c, wap_t, wcp_t, small["b_conv_proj"], proj)

    tg = 256
    gate_specs = [pl.BlockSpec((tm, tg), lambda j, i, k: (i, _GATE_BLK + j)),
                  pl.BlockSpec((tm, tg), lambda j, i, k: (i, _GATE_BLK + D // tg + j))]

    def ep_residual_rms(acc, ex, outs, ids, scr):
        x2v = acc + ex[0][...]
        r = lax.rsqrt(jnp.mean(x2v * x2v, axis=-1, keepdims=True) + EPS)
        outs[0][...] = x2v
        outs[1][...] = (x2v * r * ex[1][...]).astype(BF)
        outs[2][...] = r

    carry = plan.carry("out_proj")
    x2, h2, r2 = carried("out_proj", _matmul(
        "out_proj_rms", [merged], w_out, "NN", m=T, n=D, tm=512, tn=D, epilogue=ep_residual_rms,
        extra=[(x, _tile(512, D)), (small["g_ffn_norm"], _row(D))],
        outs=[(_sds((T, D), F32), _tile(512, D)), (_sds((T, D), BF), _tile(512, D)),
              (_sds((T, 1), F32), pl.BlockSpec((512, 1), lambda j, i, k: (i, 0)))], carry=carry), carry)
    plan.launch("gather_down", after=x2)
    wf_t = plan.weight("w_ffn_in")
    (act, gate, up), got = _ffn_in_swiglu(h2, wf_t, carry=plan.carry("ffn_in_swiglu"))
    plan.done("ffn_in_swiglu", got)
    w_down = plan.weight("w_ffn_down")
    def ep_residual_loss(acc, ex, outs, ids, scr):
        dx, dg, part = _loss_head(acc + ex[0][...], ex[1][...], ex[2][...])
        outs[0][...] = dx
        outs[1][...] = dx.astype(BF)
        _accumulate_rows(outs[2], dg, ids[1] == 0)
        _accumulate_rows(outs[3], part, ids[1] == 0)

    dx3, dx3_b, dg_final, loss = _matmul(
        "ffn_down_loss", [act], w_down, "NN", m=T, n=D, tm=512, tn=D, epilogue=ep_residual_loss,
        extra=[(x2, _tile(512, D)), (small["g_final"], _row(D)), (target, _tile(512, D))],
        outs=[(_sds((T, D), F32), _tile(512, D)), (_sds((T, D), BF), _tile(512, D)), (_sds((1, D), F32), _row(D)),
              (_sds((1, 1), F32), pl.BlockSpec((1, 1), lambda j, i, k: (0, 0)))])

    tn_ff = D_FF // 2

    def ep_swiglu_bwd(acc, ex, outs, ids, scr):
        g, u = ex[0][...].astype(F32), ex[1][...].astype(F32)
        sg = jax.nn.sigmoid(g)
        outs[0][...] = (acc * u * sg * (1.0 + g * (1.0 - sg))).astype(BF)
        outs[1][...] = (acc * g * sg).astype(BF)

    dgate, dup = _matmul(
        "ffn_down_bwd", [dx3_b], w_down, "NT", m=T, n=D_FF, tm=512, tn=tn_ff, epilogue=ep_swiglu_bwd,
        extra=[(gate, _tile(512, tn_ff)), (up, _tile(512, tn_ff))],
        outs=[(_sds((T, D_FF), BF), _tile(512, tn_ff)), (_sds((T, D_FF), BF), _tile(512, tn_ff))])

    def dw(name, a, b, rows, cols, row_off=0, alias=None, total_rows=None, colsum=False):
        total_rows = rows if total_rows is None else total_rows
        tmw = rows if rows <= 1024 else D_FF // 2
        blk, rem = divmod(row_off, tmw)
        assert rem == 0

        def ep(acc, ex, outs, ids, scr):
            outs[0][...] = acc.astype(BF)
            if colsum:
                outs[1][...] = jnp.sum(ex[0][...].astype(F32), axis=0, keepdims=True)

        outs = [(_sds((total_rows, cols), BF), pl.BlockSpec((tmw, cols), lambda j, i, k: (blk + i, j)))]
        extra = []
        if colsum:
            extra = [(a, pl.BlockSpec((T, tmw), lambda j, i, k: (0, i)))]
            outs.append((_sds((1, rows), F32), pl.BlockSpec((1, tmw), lambda j, i, k: (0, i))))
        carry = plan.carry(name)
        res = carried(name, _matmul(name, [a], b, "TN", m=rows, n=cols, tm=tmw, tn=cols, epilogue=ep, extra=extra,
                                    outs=outs, alias=None if alias is None else (alias, 0), carry=carry), carry)
        return res if colsum else res[0]

    plan.grad_ready(dict(w_ffn_down=dw("ffn_down_dw", act, dx3_b, D_FF, D)))

    def ep_rms_bwd(acc, ex, outs, ids, scr):
        dx, dg = _rms_bwd(acc, ex[0][...], ex[1][...], ex[2][...])
        dx = ex[3][...] + dx
        outs[0][...] = dx
        outs[1][...] = dx.astype(BF)
        _accumulate_rows(outs[2], dg, ids[1] == 0)

    def rms_bwd_io(tm_, xin, r, g, dres):
        return dict(
            extra=[(xin, _tile(tm_, D)), (r, pl.BlockSpec((tm_, 1), lambda j, i, k: (i, 0))), (g, _row(D)),
                   (dres, _tile(tm_, D))],
            outs=[(_sds((T, D), F32), _tile(tm_, D)), (_sds((T, D), BF), _tile(tm_, D)), (_sds((1, D), F32), _row(D))])

    carry = plan.carry("ffn_in_bwd")
    dx2, dx2_b, dg_ffn = carried(
        "ffn_in_bwd",
        _matmul("ffn_in_bwd", [dgate, dup], wf_t, "NN", m=T, n=D, tm=tm, tn=D, tk=D_FF // 2, epilogue=ep_rms_bwd,
                carry=carry, **rms_bwd_io(tm, x2, r2, small["g_ffn_norm"], dx3)), carry)
    plan.launch("send_down")
    gwf_t = dw("ffn_in_dw_gate", dgate, h2, D_FF, D, total_rows=2 * D_FF)
    gwf_t = dw("ffn_in_dw_up", dup, h2, D_FF, D, row_off=D_FF, alias=gwf_t, total_rows=2 * D_FF)
    plan.grad_ready(dict(w_ffn_in=gwf_t))

    def ep_merge_bwd(acc, ex, outs, ids, scr):
        s0 = jax.nn.sigmoid(ex[2][...].astype(F32))
        s1 = jax.nn.sigmoid(ex[3][...].astype(F32))
        outs[0][...] = (acc * s0).astype(BF)
        outs[1][...] = (acc * s1).astype(BF)
        outs[2][...] = (acc * ex[0][...].astype(F32) * s0 * (1.0 - s0)).astype(BF)
        outs[3][...] = (acc * ex[1][...].astype(F32) * s1 * (1.0 - s1)).astype(BF)

    carry = plan.carry("out_proj_bwd_merge")
    dya, dyc, dg0, dg1 = carried(
        "out_proj_bwd_merge",
        _matmul("out_proj_bwd_merge", [dx2_b], w_out, "NT", m=T, n=D, tm=tm, tn=tg, epilogue=ep_merge_bwd,
                extra=[(ya, _tile(tm, tg)), (yc, _tile(tm, tg)), (proj, gate_specs[0]), (proj, gate_specs[1])],
                outs=[(_sds((T, D), BF), _tile(tm, tg))] * 4, carry=carry), carry)
    plan.launch("send_ffn")
    gw_out = dw("out_proj_dw", merged, dx2_b, D, D)
    d_o, = _matmul("attn_proj_bwd", [dya], wap_t, "NN", m=T, n=ATTN_WIDTH, tm=tm, tn=ATTN_WIDTH,
                   epilogue=_store(BF), outs=[(_sds((T, ATTN_WIDTH), BF), _tile(tm, ATTN_WIDTH))])
    d_c, = _matmul("conv_proj_bwd", [dyc], wcp_t, "NN", m=T, n=CONV_CHANNELS, tm=tm, tn=CONV_CHANNELS,
                   epilogue=_store(BF), outs=[(_sds((T, CONV_CHANNELS), BF), _tile(tm, CONV_CHANNELS))])
    gwap_t = dw("attn_proj_dw", dya, o, D, ATTN_WIDTH)
    gwcp_t, db_cp = dw("conv_proj_dw", dyc, c, D, CONV_CHANNELS, colsum=True)
    plan.grad_ready(dict(w_out=gw_out, w_attn_proj=gwap_t, w_conv_proj=gwcp_t))
    (dglu, dcw, dcb, dlng, dlnb), got = _conv_bwd(proj, d_c, conv_w, small["conv_b"], small["ln_g"], small["ln_b"],
                                                  carry=plan.carry("conv_bwd"))
    plan.done("conv_bwd", got)
    plan.launch("send_mix")
    (dqkv, dsinks), got = _attn_bwd(proj, d_o, small["sinks"], carry=plan.carry("attn_bwd"))
    plan.done("attn_bwd", got)

    segs = [dqkv, dglu, dg0, dg1]
    gwi_t, db_in = _proj_in_dw(segs, h)
    plan.grad_ready(dict(w_in=gwi_t))
    plan.alone("swap_inp")
    plan.launch("send_inp")
    carry = plan.carry("proj_in_bwd")
    dx, _, dg_mix = carried(
        "proj_in_bwd",
        _matmul("proj_in_bwd", segs, wi_t, "NN", m=T, n=D, tm=512, tn=D, epilogue=ep_rms_bwd, carry=carry,
                **rms_bwd_io(512, x, r1, small["g_mix_norm"], plan.behind("inp", dx2))), carry)

    parts = dict(g_mix_norm=dg_mix, b_in=db_in, sinks=dsinks, conv_w=dcw, conv_b=dcb, ln_g=dlng, ln_b=dlnb,
                 b_conv_proj=db_cp, g_ffn_norm=dg_ffn, g_final=dg_final, loss=loss)
    return dx, parts


def _place():
    x, y, c = lax.axis_index("x"), lax.axis_index("y"), lax.axis_index("c")
    return x, y, c, [(1 - x, y), (x, 1 - y), (1 - x, 1 - y)]


def _gather_copies(x_refs, out_refs, rows_per, send_sems, recv_sems, local_sems):
    x, y, c, chips = _place()
    me, sibling = (x, y, c), (x, y, 1 - c)

    def rows(a, px, py, pc):
        return out_refs[a].at[pl.ds((4 * px + 2 * py + pc) * rows_per[a], rows_per[a])]

    def copy(a, k, block, to, src=None):
        return pltpu.make_async_remote_copy(
            src_ref=rows(a, *block) if src is None else src, dst_ref=rows(a, *block),
            send_sem=send_sems.at[7 * a + k], recv_sem=recv_sems.at[7 * a + k], device_id=to, device_id_type=MESH)

    def local(a):
        return pltpu.make_async_copy(x_refs[a], rows(a, *me), local_sems.at[a])

    def first(a):
        return [copy(a, 0, me, sibling, src=x_refs[a])] + [copy(a, 1 + j, me, (*chip, c), src=x_refs[a])
                                                          for j, chip in enumerate(chips)]

    def arrive(a, j):
        return copy(a, 1 + j, (*chips[j], c), me)

    def passed(a, j):
        return copy(a, 4 + j, (*chips[j], c), sibling)

    def from_sibling(a):
        return [copy(a, 0, sibling, me)] + [copy(a, 4 + j, (*chip, 1 - c), me) for j, chip in enumerate(chips)]

    return len(x_refs), local, first, arrive, passed, from_sibling


def _gather_start(*refs):
    n, local, first, _, _, _ = _gather_copies(*refs)
    for a in range(n):
        local(a).start()
        for cp in first(a):
            cp.start()


def _gather_finish(*refs):
    n, local, first, arrive, passed, from_sibling = _gather_copies(*refs)
    for a in range(n):
        for j in range(3):
            arrive(a, j).wait_recv()
            passed(a, j).start()
    for a in range(n):
        for cp in from_sibling(a):
            cp.wait_recv()
    for a in range(n):
        for cp in first(a) + [passed(a, j) for j in range(3)]:
            cp.wait_send()
        local(a).wait()


def _gather_peers():
    x, y, c, chips = _place()
    return [(x, y, 1 - c)] + [(*chip, c) for chip in chips]


def _gather_sems(n):
    return [pltpu.SemaphoreType.DMA((7 * n,)), pltpu.SemaphoreType.DMA((7 * n,)), pltpu.SemaphoreType.DMA((n,))]


def _gather_carry(shards):
    rows_per = [s.shape[0] for s in shards]
    return _Carry(shards, [_sds((N_DEV * s.shape[0],) + s.shape[1:], s.dtype) for s in shards],
                  _gather_sems(len(shards)),
                  lambda ins, outs, sems: _gather_start(ins, outs, rows_per, *sems),
                  lambda ins, outs, sems: _gather_finish(ins, outs, rows_per, *sems), _gather_peers)


def _first_gather(shards):
    n = len(shards)
    rows_per = [s.shape[0] for s in shards]

    def body(*refs):
        x_refs, out_refs = refs[:n], refs[n:2 * n]
        send_sems, recv_sems, local_sems = refs[2 * n:]
        x, y, c, chips = _place()
        me, sibling = (x, y, c), (x, y, 1 - c)
        near_x, near_y, far = (*chips[0], c), (*chips[1], c), (*chips[2], c)

        def rows(a, dev, part):
            h = rows_per[a] // 2
            lo, size = {"all": (0, 2 * h), "low": (0, h), "high": (h, h)}[part]
            return out_refs[a].at[pl.ds((4 * dev[0] + 2 * dev[1] + dev[2]) * rows_per[a] + lo, size)]

        def copy(a, k, block, part, to, src=None):
            return pltpu.make_async_remote_copy(
                src_ref=rows(a, block, part) if src is None else src, dst_ref=rows(a, block, part),
                send_sem=send_sems.at[9 * a + k], recv_sem=recv_sems.at[9 * a + k], device_id=to, device_id_type=MESH)

        other = lambda dev: (dev[0], dev[1], 1 - c)
        sent = []
        for a in range(n):
            pltpu.make_async_copy(x_refs[a], rows(a, me, "all"), local_sems.at[a]).start()
            sent += [copy(a, 0, me, "all", sibling, src=x_refs[a]), copy(a, 1, me, "all", near_x, src=x_refs[a]),
                     copy(a, 2, me, "all", near_y, src=x_refs[a])]
        for cp in sent:
            cp.start()
        for a in range(n):
            copy(a, 1, near_x, "all", me).wait_recv()
            copy(a, 2, near_y, "all", me).wait_recv()
            passed = [copy(a, 3, near_y, "high", near_x), copy(a, 4, near_x, "low", near_y),
                      copy(a, 5, near_x, "all", sibling), copy(a, 6, near_y, "all", sibling)]
            for cp in passed:
                cp.start()
            sent += passed
        for a in range(n):
            copy(a, 3, far, "high", me).wait_recv()
            copy(a, 4, far, "low", me).wait_recv()
            passed = [copy(a, 7, far, "high", sibling), copy(a, 8, far, "low", sibling)]
            for cp in passed:
                cp.start()
            sent += passed
        for a in range(n):
            copy(a, 0, sibling, "all", me).wait_recv()
            copy(a, 5, other(near_x), "all", me).wait_recv()
            copy(a, 6, other(near_y), "all", me).wait_recv()
            copy(a, 7, other(far), "high", me).wait_recv()
            copy(a, 8, other(far), "low", me).wait_recv()
        for cp in sent:
            cp.wait_send()
        for a in range(n):
            pltpu.make_async_copy(x_refs[a], rows(a, me, "all"), local_sems.at[a]).wait()

    return pl.pallas_call(
        body, name="weights_first_gather", in_specs=[ANY] * n, out_specs=[ANY] * n,
        out_shape=[_sds((N_DEV * s.shape[0],) + s.shape[1:], s.dtype) for s in shards],
        scratch_shapes=[pltpu.SemaphoreType.DMA((9 * n,)), pltpu.SemaphoreType.DMA((9 * n,)),
                        pltpu.SemaphoreType.DMA((n,))],
    )(*shards)


def _swap_carry(grads):
    n = len(grads)

    def copies(g_refs, out_refs, sems):
        send_sems, recv_sems = sems
        x, y, c, _ = _place()
        return [pltpu.make_async_remote_copy(
            src_ref=g_refs[a].at[2 * p + 1 - c], dst_ref=out_refs[a].at[p],
            send_sem=send_sems.at[4 * a + p], recv_sem=recv_sems.at[4 * a + p],
            device_id=(x, y, 1 - c), device_id_type=MESH) for a in range(n) for p in range(4)]

    def start(ins, outs, sems):
        for cp in copies(ins, outs, sems):
            cp.start()

    def finish(ins, outs, sems):
        for cp in copies(ins, outs, sems):
            cp.wait()

    def peers():
        x, y, c, _ = _place()
        return [(x, y, 1 - c)]

    return _Carry(grads, [_sds((4,) + g.shape[1:], g.dtype) for g in grads],
                  [pltpu.SemaphoreType.DMA((4 * n,)), pltpu.SemaphoreType.DMA((4 * n,))], start, finish, peers)


def _join(carries):
    carries = [c for c in carries if c is not None]
    if not carries:
        return None
    n_in = [len(c.arrays) for c in carries]
    n_out = [len(c.out_shapes) for c in carries]
    n_sem = [len(c.sems) for c in carries]

    def parts(refs, counts):
        cuts = [sum(counts[:q]) for q in range(len(counts) + 1)]
        return [refs[cuts[q]:cuts[q + 1]] for q in range(len(counts))]

    def start(ins, outs, sems):
        for c, i, o, s in zip(carries, parts(ins, n_in), parts(outs, n_out), parts(sems, n_sem)):
            c.start(i, o, s)

    def finish(ins, outs, sems):
        for c, i, o, s in zip(carries, parts(ins, n_in), parts(outs, n_out), parts(sems, n_sem)):
            c.finish(i, o, s)

    return _Carry([a for c in carries for a in c.arrays], [o for c in carries for o in c.out_shapes],
                  [s for c in carries for s in c.sems], start, finish)


def _run_carry(name, carry):
    n_in, n_out = len(carry.arrays), len(carry.out_shapes)

    def body(*refs):
        carry.start(refs[:n_in], refs[n_in:n_in + n_out], refs[n_in + n_out:])
        carry.finish(refs[:n_in], refs[n_in:n_in + n_out], refs[n_in + n_out:])

    return pl.pallas_call(body, name=name, in_specs=[ANY] * n_in, out_specs=[ANY] * n_out,
                          out_shape=carry.out_shapes, scratch_shapes=carry.sems)(*carry.arrays)


def _run_carry_async(name, carry, collective_id):
    ins = [jax.new_ref(a, memory_space=pltpu.MemorySpace.HBM) for a in carry.arrays]
    outs = [jax.empty_ref(o, memory_space=pltpu.MemorySpace.HBM) for o in carry.out_shapes]

    @pl.kernel(mesh=plsc.ScalarSubcoreMesh(axis_name="sequencer", num_cores=1), name=name,
               scratch_types=tuple(carry.sems), compiler_params=pltpu.CompilerParams(collective_id=collective_id))
    def launch(*sems):
        barrier = pltpu.get_barrier_semaphore()
        peers = carry.peers()
        for peer in peers:
            pl.semaphore_signal(barrier, inc=1, device_id=peer, device_id_type=MESH)
        pl.semaphore_wait(barrier, len(peers))
        carry.start(ins, outs, sems)
        carry.finish(ins, outs, sems)

    launch()
    return [o[...] for o in outs]


def _chip_sum(name, g, got, c):
    _, rows, cols = g.shape

    def body(c_ref, g_ref, got_ref, o_ref):
        o_ref[...] = (g_ref[...].astype(F32) + got_ref[...].astype(F32)).astype(BF)

    return pl.pallas_call(
        body, name=name,
        grid_spec=pltpu.PrefetchScalarGridSpec(
            num_scalar_prefetch=1, grid=(4,),
            in_specs=[pl.BlockSpec((1, rows, cols), lambda p, c_ref: (2 * p + c_ref[0], 0, 0)),
                      pl.BlockSpec((1, rows, cols), lambda p, c_ref: (p, 0, 0))],
            out_specs=pl.BlockSpec((1, rows, cols), lambda p, c_ref: (p, 0, 0))),
        out_shape=_sds((4, rows, cols), BF),
        compiler_params=_params(("arbitrary",)),
    )(c, g, got)


def _send_carry(sums, ks):
    n, nk = len(sums), len(ks)

    def copies(s_refs, out_refs, sems):
        send_sems, recv_sems = sems
        x, y, c, chips = _place()
        return [pltpu.make_async_remote_copy(
            src_ref=s_refs[a].at[2 * chips[k][0] + chips[k][1]], dst_ref=out_refs[a].at[q],
            send_sem=send_sems.at[nk * a + q], recv_sem=recv_sems.at[nk * a + q],
            device_id=(*chips[k], c), device_id_type=MESH) for a in range(n) for q, k in enumerate(ks)]

    def start(ins, outs, sems):
        for cp in copies(ins, outs, sems):
            cp.start()

    def finish(ins, outs, sems):
        for cp in copies(ins, outs, sems):
            cp.wait()

    def peers():
        x, y, c, chips = _place()
        return [(*chips[k], c) for k in ks]

    return _Carry(sums, [_sds((nk,) + s.shape[1:], s.dtype) for s in sums],
                  [pltpu.SemaphoreType.DMA((nk * n,)), pltpu.SemaphoreType.DMA((nk * n,))], start, finish, peers)


def _adam_math(w, g, m, v):
    m = ADAM_B1 * m + (1.0 - ADAM_B1) * g
    v = ADAM_B2 * v + (1.0 - ADAM_B2) * (g * g)
    m_hat = m / (1.0 - ADAM_B1 ** ADAM_STEP)
    v_hat = v / (1.0 - ADAM_B2 ** ADAM_STEP)
    delta = -ADAM_LR * (m_hat / (jnp.sqrt(v_hat) + ADAM_EPS) + ADAM_WD * w)
    return delta, m, v


def _adamw(name, w, g, m, v):
    rows, cols = w.shape
    tr = 256 if rows % 256 == 0 else rows

    def body(w_ref, g_ref, m_ref, v_ref, d_ref, nm_ref, nv_ref):
        d_ref[...], nm_ref[...], nv_ref[...] = _adam_math(w_ref[...], g_ref[...], m_ref[...], v_ref[...])

    t = pl.BlockSpec((tr, cols), lambda i: (i, 0))
    return pl.pallas_call(
        body, name=name, grid=(rows // tr,), in_specs=[t] * 4, out_specs=[t] * 3,
        out_shape=[_sds((rows, cols), F32)] * 3, compiler_params=_params(("arbitrary",)),
    )(w, g, m, v)


def _grad_adamw(name, g, got, got3, ids, w, m, v):
    _, rows, cols = g.shape
    n3 = len(got3)
    tr = rows // 2 if rows >= 256 else rows

    def body(ids_ref, g_ref, got_ref, *rest):
        w_ref, m_ref, v_ref, o_ref, d_ref, nm_ref, nv_ref = rest[n3:]
        tot = g_ref[0].astype(F32) + got_ref[0].astype(F32)
        for r_ref in rest[:n3]:
            for q in range(r_ref.shape[0]):
                tot = tot + r_ref[q].astype(F32)
        o_ref[...] = tot
        d_ref[...], nm_ref[...], nv_ref[...] = _adam_math(w_ref[...], tot, m_ref[...], v_ref[...])

    tile = pl.BlockSpec((tr, cols), lambda i, ids_ref: (i, 0))
    return pl.pallas_call(
        body, name=name,
        grid_spec=pltpu.PrefetchScalarGridSpec(
            num_scalar_prefetch=1, grid=(rows // tr,),
            in_specs=[pl.BlockSpec((1, tr, cols), lambda i, ids_ref: (ids_ref[0], i, 0)),
                      pl.BlockSpec((1, tr, cols), lambda i, ids_ref: (ids_ref[1], i, 0)),
                      *[pl.BlockSpec((r.shape[0], tr, cols), lambda i, ids_ref: (0, i, 0)) for r in got3],
                      tile, tile, tile],
            out_specs=[tile] * 4),
        out_shape=[_sds((rows, cols), F32)] * 4,
        compiler_params=_params(("arbitrary",)),
    )(ids, g, got, *got3, w, m, v)


SMALL_NAMES = ["g_mix_norm", "b_in", "sinks", "conv_b", "ln_g", "ln_b", "b_conv_proj", "g_ffn_norm", "g_final"]
_PACK_ROWS = 32


def _small_pack(parts):
    C = CONV_CHANNELS
    part_list = [parts["g_mix_norm"], parts["b_in"], parts["sinks"], parts["conv_b"], parts["ln_g"], parts["ln_b"],
                 parts["b_conv_proj"], parts["g_ffn_norm"], parts["g_final"], parts["loss"], parts["conv_w"]]

    def body(p_mix, p_b, p_sink, p_cb, p_lg, p_lb, p_bcp, p_ffn, p_fin, p_loss, p_cw, pack):
        pack[...] = jnp.zeros_like(pack)
        pack[0:1, :] = p_mix[...]
        pack[1:2, 0:GLU_OFF] = p_b[:, 0:GLU_OFF]
        pack[2:3, :] = p_b[:, GLU_OFF:GATE_OFF]
        pack[3:4, :] = p_b[:, GATE_OFF:GATE_OFF + D_MODEL]
        pack[4:5, :] = p_b[:, GATE_OFF + D_MODEL:]
        pack[5:6, 0:128] = p_sink[...]
        pack[6:7, 0:C] = p_cb[...]
        pack[6:7, C:2 * C] = p_lg[...]
        pack[7:8, 0:C] = p_lb[...]
        pack[8:9, :] = p_bcp[...]
        pack[9:10, :] = p_ffn[...]
        pack[10:11, :] = p_fin[...]
        pack[11:12, 0:128] = jnp.broadcast_to(p_loss[...], (1, 128))
        pack[12:28, 0:C] = p_cw[0:16, :]
        pack[12:28, C:2 * C] = p_cw[16:32, :]

    vm = pl.BlockSpec(memory_space=pltpu.VMEM)
    return pl.pallas_call(body, name="small_pack", in_specs=[vm] * len(part_list), out_specs=vm,
                          out_shape=_sds((_PACK_ROWS, D_MODEL), F32))(*part_list)


def _small_adamw(gathered, small_w, small_m, small_v):
    C = CONV_CHANNELS
    names = SMALL_NAMES
    widths = [small_w[k].shape[1] for k in names]
    n_small = len(names)

    def body(*refs):
        tot_ref = refs[0]
        w_refs = refs[1:1 + n_small]
        m_refs = refs[1 + n_small:1 + 2 * n_small]
        v_refs = refs[1 + 2 * n_small:1 + 3 * n_small]
        o = 1 + 3 * n_small
        loss_ref, cw_ref = refs[o], refs[o + 1]
        out_refs = refs[o + 2:o + 2 + 4 * n_small]
        tot = tot_ref[0:_PACK_ROWS, :]
        for d in range(1, N_DEV):
            tot = tot + tot_ref[d * _PACK_ROWS:(d + 1) * _PACK_ROWS, :]
        loss_ref[...] = tot[11:12, 0:1]
        cw_ref[0:16, :] = tot[12:28, 0:C]
        cw_ref[16:32, :] = tot[12:28, C:2 * C]
        grads = dict(
            g_mix_norm=tot[0:1, :],
            b_in=jnp.concatenate([tot[1:2, 0:GLU_OFF], tot[2:3, :], tot[3:4, :], tot[4:5, :]], axis=1),
            sinks=tot[5:6, 0:N_Q_HEADS], conv_b=tot[6:7, 0:C], ln_g=tot[6:7, C:2 * C], ln_b=tot[7:8, 0:C],
            b_conv_proj=tot[8:9, :], g_ffn_norm=tot[9:10, :], g_final=tot[10:11, :])
        for s, k in enumerate(names):
            g = grads[k]
            d, nm, nv = _adam_math(w_refs[s][...], g, m_refs[s][...], v_refs[s][...])
            out_refs[4 * s][...] = g
            out_refs[4 * s + 1][...] = d
            out_refs[4 * s + 2][...] = nm
            out_refs[4 * s + 3][...] = nv

    vm = pl.BlockSpec(memory_space=pltpu.VMEM)
    args = [gathered, *[small_w[k] for k in names], *[small_m[k] for k in names], *[small_v[k] for k in names]]
    out_shape = [_sds((1, 1), F32), _sds((CONV_PAD, C), F32)]
    for wd in widths:
        out_shape += [_sds((1, wd), F32)] * 4
    res = pl.pallas_call(
        body, name="small_adamw",
        in_specs=[vm] * len(args), out_specs=[vm] * len(out_shape), out_shape=out_shape,
        compiler_params=pltpu.CompilerParams(vmem_limit_bytes=VMEM_LIMIT_BYTES),
    )(*args)
    return res[0], res[1], {k: res[2 + 4 * s:6 + 4 * s] for s, k in enumerate(names)}


BIG = dict(w_in=True, w_attn_proj=True, w_conv_proj=True, w_out=False, w_ffn_in=True, w_ffn_down=False)
WEIGHT_NAMES = ["g_mix_norm", "w_in", "b_in", "sinks", "conv_w", "conv_b", "ln_g", "ln_b", "w_attn_proj",
                "w_conv_proj", "b_conv_proj", "w_out", "g_ffn_norm", "w_ffn_in", "w_ffn_down", "g_final"]


class _Plan:
    GROUPS = dict(down=["w_ffn_down"], ffn=["w_ffn_in"], mix=["w_out", "w_attn_proj", "w_conv_proj"], inp=["w_in"])
    ALL = (0, 1, 2)
    RIDES = dict(
        gather_mix=[("gather", ["w_attn_proj", "w_conv_proj", "w_out"])], gather_ffn=[("gather", ["w_ffn_in"])],
        gather_down=[("gather", ["w_ffn_down"])],
        ffn_in_bwd=[("swap", "down")], send_down=[("send", "down", ALL)],
        out_proj_bwd_merge=[("swap", "ffn")], send_ffn=[("send", "ffn", ALL)],
        conv_bwd=[("swap", "mix")], send_mix=[("send", "mix", ALL)],
        swap_inp=[("swap", "inp")], send_inp=[("send", "inp", ALL)])
    ASYNC = dict(gather_mix=1, gather_ffn=2, gather_down=3, send_down=4, send_ffn=5, send_mix=6, send_inp=7)

    def __init__(self, shards, c1):
        self.shards, self.c1 = shards, c1
        self.full, self.slots, self.got, self.sums, self.got3 = {}, {}, {}, {}, {}

    def weight(self, name):
        return self.full[name]

    def grad_ready(self, grads):
        for k, g in grads.items():
            self.slots[k] = g.reshape(N_DEV, g.shape[0] // N_DEV, g.shape[1])

    def _one(self, kind, what, ks=None):
        if kind == "gather":
            return _gather_carry([self.shards[k] for k in what])
        names = self.GROUPS[what]
        if kind == "swap":
            return _swap_carry([self.slots[k] for k in names])
        return _send_carry([self.sums[k] for k in names], ks)

    def carry(self, call):
        return _join([self._one(*ride) for ride in self.RIDES.get(call, [])])

    def done(self, call, outs):
        outs = list(outs)
        for kind, what, *_ in self.RIDES.get(call, []):
            names = what if kind == "gather" else self.GROUPS[what]
            mine, outs = outs[:len(names)], outs[len(names):]
            if kind == "gather":
                self.full.update(zip(names, mine))
            elif kind == "send":
                for k, r in zip(names, mine):
                    self.got3.setdefault(k, []).append(r)
            else:
                for k, r in zip(names, mine):
                    self.got[k] = r
                    self.sums[k] = _chip_sum(f"chip_sum_{k}", self.slots[k], r, self.c1)

    def alone(self, call):
        self.done(call, _run_carry(call, self.carry(call)))

    def behind(self, group, x):
        return lax.optimization_barrier((x, tuple(self.sums[k] for k in self.GROUPS[group])))[0]

    def launch(self, call, after=None):
        carry = self._one(*self.RIDES[call][0])
        if after is not None:
            carry.arrays = list(lax.optimization_barrier((tuple(carry.arrays), after))[0])
        self.done(call, _run_carry_async(call, carry, self.ASYNC[call]))


def kernel(x, g_mix_norm, w_in, b_in, sinks, conv_w, conv_b, ln_g, ln_b, w_attn_proj, w_conv_proj, b_conv_proj, w_out, g_ffn_norm, w_ffn_in, w_ffn_down, g_final, loss_target, m_g_mix_norm, m_w_in, m_b_in, m_sinks, m_conv_w, m_conv_b, m_ln_g, m_ln_b, m_w_attn_proj, m_w_conv_proj, m_b_conv_proj, m_w_out, m_g_ffn_norm, m_w_ffn_in, m_w_ffn_down, m_g_final, v_g_mix_norm, v_w_in, v_b_in, v_sinks, v_conv_w, v_conv_b, v_ln_g, v_ln_b, v_w_attn_proj, v_w_conv_proj, v_b_conv_proj, v_w_out, v_g_ffn_norm, v_w_ffn_in, v_w_ffn_down, v_g_final):
    w = dict(g_mix_norm=g_mix_norm, w_in=w_in, b_in=b_in, sinks=sinks, conv_w=conv_w, conv_b=conv_b, ln_g=ln_g,
             ln_b=ln_b, w_attn_proj=w_attn_proj, w_conv_proj=w_conv_proj, b_conv_proj=b_conv_proj, w_out=w_out,
             g_ffn_norm=g_ffn_norm, w_ffn_in=w_ffn_in, w_ffn_down=w_ffn_down, g_final=g_final)
    m = dict(g_mix_norm=m_g_mix_norm, w_in=m_w_in, b_in=m_b_in, sinks=m_sinks, conv_w=m_conv_w, conv_b=m_conv_b,
             ln_g=m_ln_g, ln_b=m_ln_b, w_attn_proj=m_w_attn_proj, w_conv_proj=m_w_conv_proj,
             b_conv_proj=m_b_conv_proj, w_out=m_w_out, g_ffn_norm=m_g_ffn_norm, w_ffn_in=m_w_ffn_in,
             w_ffn_down=m_w_ffn_down, g_final=m_g_final)
    v = dict(g_mix_norm=v_g_mix_norm, w_in=v_w_in, b_in=v_b_in, sinks=v_sinks, conv_w=v_conv_w, conv_b=v_conv_b,
             ln_g=v_ln_g, ln_b=v_ln_b, w_attn_proj=v_w_attn_proj, w_conv_proj=v_w_conv_proj,
             b_conv_proj=v_b_conv_proj, w_out=v_w_out, g_ffn_norm=v_g_ffn_norm, w_ffn_in=v_w_ffn_in,
             w_ffn_down=v_w_ffn_down, g_final=v_g_final)
    ax, ay, ac = lax.axis_index("x"), lax.axis_index("y"), lax.axis_index("c")
    me = 4 * ax + 2 * ay + ac
    chip = 2 * ax + ay

    shards = {k: (w[k][0].T if tr else w[k][0]).astype(BF) for k, tr in BIG.items()}
    cw_shard = jnp.pad(conv_w[0].T, ((0, 0), (0, 1))).reshape(16, 128)
    wi_t, cw_full = _first_gather([shards["w_in"], cw_shard])
    conv_full = cw_full.reshape(CONV_CHANNELS, CONV_PAD).T

    as_row = lambda a: a.reshape(1, -1)
    small_w = {k: as_row(w[k]) for k in SMALL_NAMES}
    small_m = {k: as_row(m[k]) for k in SMALL_NAMES}
    small_v = {k: as_row(v[k]) for k in SMALL_NAMES}
    plan = _Plan(shards, ac.reshape(1).astype(jnp.int32))
    plan.launch("gather_mix", after=wi_t)
    dx, parts = _local_step(x[0], loss_target[0], small_w, wi_t, conv_full, plan)

    small_gathered, = _run_carry_async("small_gather", _gather_carry([_small_pack(parts)]), 8)

    ids = jnp.stack([me, chip]).astype(jnp.int32)
    grads, delta, new_m, new_v, after = {}, {}, {}, {}, dx
    for k in sorted(BIG, key=lambda k: k == "w_in"):
        flip = (lambda a: a.T) if BIG[k] else (lambda a: a)
        wk = lax.optimization_barrier((w[k][0], after))[0]
        outs = _grad_adamw(f"grad_adamw_{k}", plan.slots[k], plan.got[k], plan.got3[k], ids,
                           flip(wk), flip(m[k][0]), flip(v[k][0]))
        after = outs[0]
        grads[k], delta[k], new_m[k], new_v[k] = (flip(a)[None] for a in outs)

    loss, cw_grad, small_out = _small_adamw(small_gathered, small_w, small_m, small_v)
    for k in SMALL_NAMES:
        g, d, nm, nv = (a.reshape(w[k].shape) for a in small_out[k])
        grads[k], delta[k], new_m[k], new_v[k] = g, d, nm, nv
    cw_mine = lax.dynamic_slice(cw_grad, (0, me * 64), (CONV_WIDTH, 64))
    d, nm, nv = _adamw("adamw_conv_w", conv_w[0], cw_mine, m_conv_w[0], v_conv_w[0])
    grads["conv_w"], delta["conv_w"], new_m["conv_w"], new_v["conv_w"] = cw_mine[None], d[None], nm[None], nv[None]

    return (loss.reshape(()), dx[None], *[grads[k] for k in WEIGHT_NAMES], *[delta[k] for k in WEIGHT_NAMES],
            *[new_m[k] for k in WEIGHT_NAMES], *[new_v[k] for k in WEIGHT_NAMES])
```

```python
import functools

import jax
import jax.numpy as jnp
from jax import lax
from jax.experimental import pallas as pl
from jax.experimental.pallas import tpu as pltpu
from jax.experimental.pallas import tpu_sc as plsc

F32 = jnp.float32
BF = jnp.bfloat16

SEQ = 2048
D_MODEL = 1024
HEAD_DIM = 64
N_Q_HEADS = 8
N_KV_HEADS = 2
GROUP = N_Q_HEADS // N_KV_HEADS
BLOCK = 128
ATTN_WIDTH = 512
KV_WIDTH = 128
CONV_CHANNELS = 512
CONV_WIDTH = 31
CONV_PAD = 32
GLU_OFF = 768
GATE_OFF = 1792
IN_WIDTH = 3840
D_FF = 2816
EPS = 1e-5
NEG = -1e30
N_DEV = 8

ADAM_LR = 0.001
ADAM_B1 = 0.9
ADAM_B2 = 0.999
ADAM_EPS = 1e-08
ADAM_WD = 0.01
ADAM_STEP = 10

VMEM_LIMIT_BYTES = 56 * 1024 * 1024
MESH = pl.DeviceIdType.MESH
ANY = pl.BlockSpec(memory_space=pl.ANY)

_DIMS = {"NN": (((1,), (0,)), ((), ())), "NT": (((1,), (1,)), ((), ())), "TN": (((0,), (0,)), ((), ()))}


def _params(sem):
    return pltpu.CompilerParams(dimension_semantics=sem, vmem_limit_bytes=VMEM_LIMIT_BYTES)


class _Carry:
    def __init__(self, arrays, out_shapes, sems, start, finish, peers=None):
        self.arrays, self.out_shapes, self.sems, self.start, self.finish = arrays, out_shapes, sems, start, finish
        self.peers = peers


def _carry_io(carry):
    if carry is None:
        return [], [], []
    return list(carry.arrays), list(carry.out_shapes), list(carry.sems)


def _matmul(name, a_list, b, mode, *, m, n, tm, tn, tk=None, epilogue, extra=(), outs, b_off=(0, 0), alias=None,
            scratch=(), carry=None):
    seg_k = [a.shape[0] if mode == "TN" else a.shape[1] for a in a_list]
    whole = tk is None
    seg_nk = [1] * len(a_list) if whole else [ks // tk for ks in seg_k]
    nk = 1 if whole else sum(seg_nk)
    starts = [sum(seg_nk[:s]) for s in range(len(seg_nk))]
    k_starts = [sum(seg_k[:s]) for s in range(len(seg_k))]
    k_tot = sum(seg_k)
    n_a, n_extra, n_out = len(a_list), len(extra), len(outs)

    a_specs = []
    for st, ns, ks in zip(starts, seg_nk, seg_k):
        if mode == "TN":
            a_specs.append(pl.BlockSpec((ks if whole else tk, tm), lambda j, i, k: (k, i)))
        elif whole:
            a_specs.append(pl.BlockSpec((tm, ks), lambda j, i, k: (i, 0)))
        else:
            a_specs.append(pl.BlockSpec((tm, tk), functools.partial(
                lambda j, i, k, st, ns: (i, jnp.clip(k - st, 0, ns - 1)), st=st, ns=ns)))
    bk = k_tot if whole else tk
    if mode == "NT":
        b_spec = pl.BlockSpec((tn, bk), lambda j, i, k: (b_off[0] + j, b_off[1] + k))
    else:
        b_spec = pl.BlockSpec((bk, tn), lambda j, i, k: (b_off[0] + k, b_off[1] + j))
    n_alias = 0 if alias is None else 1
    c_in, c_out, c_sems = _carry_io(carry)
    n_acc = 0 if whole else 1
    nj, ni = n // tn, m // tm

    def body(*refs):
        pos = [n_a, 1, n_alias, n_extra, len(c_in), n_out, len(c_out), n_acc, len(scratch), len(c_sems)]
        cuts = [sum(pos[:q]) for q in range(len(pos) + 1)]
        a_refs, (b_ref,), _, ex, ci_refs, out_refs, co_refs, acc_refs, scr, cs_refs = (
            refs[cuts[q]:cuts[q + 1]] for q in range(len(pos)))
        j, i, k = pl.program_id(0), pl.program_id(1), pl.program_id(2)
        ids = (j, i)
        if carry is not None:
            @pl.when((j == 0) & (i == 0) & (k == 0))
            def _():
                carry.start(ci_refs, co_refs, cs_refs)

        def dot(a_ref, bv):
            return lax.dot_general(a_ref[...].astype(BF), bv.astype(BF), _DIMS[mode], preferred_element_type=F32)

        if whole:
            tot = None
            for a_ref, k0, ks in zip(a_refs, k_starts, seg_k):
                if n_a == 1:
                    bv = b_ref[...]
                else:
                    bv = b_ref[:, k0:k0 + ks] if mode == "NT" else b_ref[k0:k0 + ks, :]
                part = dot(a_ref, bv)
                tot = part if tot is None else tot + part
            epilogue(tot, ex, out_refs, ids, scr)
        else:
            acc, = acc_refs

            @pl.when(k == 0)
            def _():
                acc[...] = jnp.zeros_like(acc)

            for a_ref, st, ns in zip(a_refs, starts, seg_nk):
                if n_a == 1:
                    acc[...] += dot(a_ref, b_ref[...])
                else:
                    @pl.when((k >= st) & (k < st + ns))
                    def _(a_ref=a_ref):
                        acc[...] += dot(a_ref, b_ref[...])

            @pl.when(k == nk - 1)
            def _():
                epilogue(acc[...], ex, out_refs, ids, scr)

        if carry is not None:
            @pl.when((j == nj - 1) & (i == ni - 1) & (k == nk - 1))
            def _():
                carry.finish(ci_refs, co_refs, cs_refs)

    in_specs = [*a_specs, b_spec]
    args = [*a_list, b]
    io_alias = {}
    if alias is not None:
        in_specs.append(pl.BlockSpec(memory_space=pl.ANY))
        args.append(alias[0])
        io_alias = {n_a + 1: alias[1]}
    in_specs += [s for _, s in extra] + [pl.BlockSpec(memory_space=pl.ANY)] * len(c_in)
    args += [x for x, _ in extra] + c_in
    res = pl.pallas_call(
        body, name=name, grid=(nj, ni, nk), in_specs=in_specs,
        out_specs=[s for _, s in outs] + [pl.BlockSpec(memory_space=pl.ANY)] * len(c_out),
        out_shape=[o for o, _ in outs] + c_out,
        scratch_shapes=[*([] if whole else [pltpu.VMEM((tm, tn), F32)]), *scratch, *c_sems],
        input_output_aliases=io_alias,
        compiler_params=_params(("arbitrary", "arbitrary", "arbitrary")),
    )(*args)
    return res if carry is None else (res[:n_out], res[n_out:])


def _tile(tm, tn):
    return pl.BlockSpec((tm, tn), lambda j, i, k: (i, j))


def _row(tn):
    return pl.BlockSpec((1, tn), lambda j, i, k: (0, j))


def _store(dtype):
    def ep(acc, ex, outs, ids, scr):
        outs[0][...] = acc.astype(dtype)
    return ep


def _sds(shape, dtype):
    return jax.ShapeDtypeStruct(shape, dtype)


def _rms_fwd(name, x, g):
    T, D = x.shape
    tm = 512

    def body(x_ref, g_ref, h_ref, r_ref):
        xv = x_ref[...]
        r = lax.rsqrt(jnp.mean(xv * xv, axis=-1, keepdims=True) + EPS)
        h_ref[...] = (xv * r * g_ref[...]).astype(BF)
        r_ref[...] = r

    return pl.pallas_call(
        body, name=name, grid=(T // tm,),
        in_specs=[pl.BlockSpec((tm, D), lambda i: (i, 0)), pl.BlockSpec((1, D), lambda i: (0, 0))],
        out_specs=[pl.BlockSpec((tm, D), lambda i: (i, 0)), pl.BlockSpec((tm, 1), lambda i: (i, 0))],
        out_shape=[_sds((T, D), BF), _sds((T, 1), F32)],
        compiler_params=_params(("arbitrary",)),
    )(x, g)


def _rms_bwd(dh, xv, r, g):
    xh = xv * r
    dxh = dh * g
    dx = r * (dxh - xh * jnp.mean(dxh * xh, axis=-1, keepdims=True))
    return dx, jnp.sum(dh * xh, axis=0, keepdims=True)


def _accumulate_rows(ref, val, first):
    @pl.when(first)
    def _():
        ref[...] = val

    @pl.when(jnp.logical_not(first))
    def _():
        ref[...] += val


def _loss_head(xv, g, target):
    r = lax.rsqrt(jnp.mean(xv * xv, axis=-1, keepdims=True) + EPS)
    err = xv * r * g - target
    dx, dg = _rms_bwd(err * (1.0 / xv.shape[-1]), xv, r, g)
    part = 0.5 * jnp.sum(jnp.mean(err * err, axis=-1, keepdims=True), axis=0, keepdims=True)
    return dx, dg, part


def _lane_half(shape, h):
    lane = lax.broadcasted_iota(jnp.int32, shape, 1)
    return (lane >= HEAD_DIM * h) & (lane < HEAD_DIM * (h + 1))


def _to_half(v, w, h):
    if w != h:
        v = pltpu.roll(v, HEAD_DIM, 1)
    return jnp.where(_lane_half(v.shape, h), v, 0.0)


def _attn_block(qkv_ref, sinks_ref, n, h):
    r0 = pl.multiple_of(n * BLOCK, BLOCK)
    p0 = pl.multiple_of(jnp.maximum(n - 1, 0) * BLOCK, BLOCK)
    rows = pl.ds(r0, BLOCK)
    prev = pl.ds(p0, BLOCK)
    k2 = jnp.concatenate([qkv_ref[prev, ATTN_WIDTH:ATTN_WIDTH + KV_WIDTH],
                          qkv_ref[rows, ATTN_WIDTH:ATTN_WIDTH + KV_WIDTH]], axis=0)
    v2 = jnp.concatenate([qkv_ref[prev, ATTN_WIDTH + KV_WIDTH:ATTN_WIDTH + 2 * KV_WIDTH],
                          qkv_ref[rows, ATTN_WIDTH + KV_WIDTH:ATTN_WIDTH + 2 * KV_WIDTH]], axis=0)
    qs = []
    for g in range(GROUP):
        hq = GROUP * h + g
        blk = qkv_ref[rows, (hq // 2) * 128:(hq // 2 + 1) * 128].astype(F32)
        qs.append(_to_half(blk, hq % 2, h))
    q4 = jnp.concatenate(qs, axis=0).astype(BF)
    s = lax.dot_general(q4, k2, _DIMS["NT"], preferred_element_type=F32) * (HEAD_DIM ** -0.5)
    shape = s.shape
    row = lax.broadcasted_iota(jnp.int32, shape, 0)
    qi = row & (BLOCK - 1)
    kj = lax.broadcasted_iota(jnp.int32, shape, 1)
    diff = qi + BLOCK - kj
    valid = (diff >= 0) & (diff < BLOCK) & ((kj >= BLOCK) | (n > 0))
    s = jnp.where(valid, s, NEG)
    row1 = lax.broadcasted_iota(jnp.int32, (shape[0], 1), 0)
    sink = jnp.zeros((shape[0], 1), F32)
    for g in range(GROUP):
        sink = jnp.where((row1 >= g * BLOCK) & (row1 < (g + 1) * BLOCK), sinks_ref[0, GROUP * h + g], sink)
    m = jnp.maximum(jnp.max(s, axis=-1, keepdims=True), sink)
    e = jnp.exp(s - m)
    es = jnp.exp(sink - m)
    inv = 1.0 / (jnp.sum(e, axis=-1, keepdims=True) + es)
    return e * inv, es * inv, q4, k2, v2, rows, prev


def _attn_fwd(proj, sinks, carry=None):
    T = proj.shape[0]
    c_in, c_out, c_sems = _carry_io(carry)

    def body(*refs):
        qkv_ref, sinks_ref = refs[:2]
        ci_refs = refs[2:2 + len(c_in)]
        o_ref = refs[2 + len(c_in)]
        co_refs = refs[3 + len(c_in):3 + len(c_in) + len(c_out)]
        cs_refs = refs[3 + len(c_in) + len(c_out):]
        if carry is not None:
            carry.start(ci_refs, co_refs, cs_refs)

        def blk(n, z):
            outs = [None] * (N_Q_HEADS // 2)
            for h in range(N_KV_HEADS):
                p, _, _, _, v2, rows, _ = _attn_block(qkv_ref, sinks_ref, n, h)
                o = lax.dot_general(p.astype(BF), v2, _DIMS["NN"], preferred_element_type=F32)
                for g in range(GROUP):
                    hq = GROUP * h + g
                    piece = jnp.where(_lane_half((BLOCK, 128), h), o[g * BLOCK:(g + 1) * BLOCK], 0.0)
                    if hq % 2 != h:
                        piece = pltpu.roll(piece, HEAD_DIM, 1)
                    outs[hq // 2] = piece if outs[hq // 2] is None else outs[hq // 2] + piece
            for pb in range(N_Q_HEADS // 2):
                o_ref[rows, pb * 128:(pb + 1) * 128] = outs[pb].astype(BF)
            return z

        lax.fori_loop(0, T // BLOCK, blk, 0)
        if carry is not None:
            carry.finish(ci_refs, co_refs, cs_refs)

    res = pl.pallas_call(
        body, name="attn_fwd", grid=(1,),
        in_specs=[pl.BlockSpec((T, GLU_OFF), lambda i: (0, 0)), pl.BlockSpec(memory_space=pltpu.SMEM),
                  *[ANY] * len(c_in)],
        out_specs=[pl.BlockSpec((T, ATTN_WIDTH), lambda i: (0, 0)), *[ANY] * len(c_out)],
        out_shape=[_sds((T, ATTN_WIDTH), BF), *c_out], scratch_shapes=c_sems,
        compiler_params=_params(("arbitrary",)),
    )(proj, sinks, *c_in)
    return res[0], res[1:]


def _attn_bwd(proj, d_o, sinks, carry=None):
    T = proj.shape[0]
    c_in, c_out, c_sems = _carry_io(carry)

    def body(*refs):
        qkv_ref, do_ref, sinks_ref = refs[:3]
        ci_refs = refs[3:3 + len(c_in)]
        dqkv_ref, dsink_ref = refs[3 + len(c_in):5 + len(c_in)]
        co_refs = refs[5 + len(c_in):5 + len(c_in) + len(c_out)]
        dk_acc, dv_acc = refs[5 + len(c_in) + len(c_out):7 + len(c_in) + len(c_out)]
        cs_refs = refs[7 + len(c_in) + len(c_out):]
        if carry is not None:
            carry.start(ci_refs, co_refs, cs_refs)
        dsink_ref[...] = jnp.zeros_like(dsink_ref)
        dk_acc[...] = jnp.zeros_like(dk_acc)
        dv_acc[...] = jnp.zeros_like(dv_acc)

        def blk(n, carry):
            dqs = [None] * (N_Q_HEADS // 2)
            for h in range(N_KV_HEADS):
                p, psink, q4, k2, v2, rows, prev = _attn_block(qkv_ref, sinks_ref, n, h)
                dos = []
                for g in range(GROUP):
                    hq = GROUP * h + g
                    dos.append(_to_half(do_ref[rows, (hq // 2) * 128:(hq // 2 + 1) * 128].astype(F32), hq % 2, h))
                do4 = jnp.concatenate(dos, axis=0).astype(BF)
                dp = lax.dot_general(do4, v2, _DIMS["NT"], preferred_element_type=F32)
                delta = jnp.sum(p * dp, axis=-1, keepdims=True)
                ds = (p * (dp - delta) * (HEAD_DIM ** -0.5)).astype(BF)
                dsk = psink * delta
                for g in range(GROUP):
                    hq = GROUP * h + g
                    tot = -jnp.sum(dsk[g * BLOCK:(g + 1) * BLOCK], axis=0, keepdims=True)
                    lane = lax.broadcasted_iota(jnp.int32, (1, 128), 1)
                    dsink_ref[...] += jnp.where(lane == hq, tot, 0.0)
                dq = lax.dot_general(ds, k2, _DIMS["NN"], preferred_element_type=F32)
                dk = lax.dot_general(ds, q4, _DIMS["TN"], preferred_element_type=F32)
                dv = lax.dot_general(p.astype(BF), do4, _DIMS["TN"], preferred_element_type=F32)
                dk_acc[prev, :] += dk[:BLOCK]
                dk_acc[rows, :] += dk[BLOCK:]
                dv_acc[prev, :] += dv[:BLOCK]
                dv_acc[rows, :] += dv[BLOCK:]
                for g in range(GROUP):
                    hq = GROUP * h + g
                    piece = jnp.where(_lane_half((BLOCK, 128), h), dq[g * BLOCK:(g + 1) * BLOCK], 0.0)
                    if hq % 2 != h:
                        piece = pltpu.roll(piece, HEAD_DIM, 1)
                    dqs[hq // 2] = piece if dqs[hq // 2] is None else dqs[hq // 2] + piece
            for pb in range(N_Q_HEADS // 2):
                dqkv_ref[rows, pb * 128:(pb + 1) * 128] = dqs[pb].astype(BF)
            return carry

        lax.fori_loop(0, T // BLOCK, blk, 0)
        dqkv_ref[:, ATTN_WIDTH:ATTN_WIDTH + KV_WIDTH] = dk_acc[...].astype(BF)
        dqkv_ref[:, ATTN_WIDTH + KV_WIDTH:] = dv_acc[...].astype(BF)
        if carry is not None:
            carry.finish(ci_refs, co_refs, cs_refs)

    res = pl.pallas_call(
        body, name="attn_bwd", grid=(1,),
        in_specs=[pl.BlockSpec((T, GLU_OFF), lambda i: (0, 0)), pl.BlockSpec((T, ATTN_WIDTH), lambda i: (0, 0)),
                  pl.BlockSpec(memory_space=pltpu.SMEM), *[ANY] * len(c_in)],
        out_specs=[pl.BlockSpec((T, GLU_OFF), lambda i: (0, 0)), pl.BlockSpec((1, 128), lambda i: (0, 0)),
                   *[ANY] * len(c_out)],
        out_shape=[_sds((T, GLU_OFF), BF), _sds((1, 128), F32), *c_out],
        scratch_shapes=[pltpu.VMEM((T, KV_WIDTH), F32), pltpu.VMEM((T, KV_WIDTH), F32), *c_sems],
        compiler_params=_params(("arbitrary",)),
    )(proj, d_o, sinks, *c_in)
    return res[:2], res[2:]


CHUNK = 256
SUB = 32
WIN = CHUNK + 32
PAD_ROWS = SEQ + 2 * CONV_PAD
_GLU_SPECS = [pl.BlockSpec((SEQ, 256), functools.partial(lambda i, c: (0, c), c=GLU_OFF // 256 + c)) for c in range(4)]


def _glu_to_pad(a0, a1, b0, b1, zpad):
    C = CONV_CHANNELS
    zpad[0:CONV_PAD, :] = jnp.zeros((CONV_PAD, C), F32)
    zpad[CONV_PAD + SEQ:, :] = jnp.zeros((CONV_PAD, C), F32)
    zpad[CONV_PAD:CONV_PAD + SEQ, 0:256] = a0[...].astype(F32) * jax.nn.sigmoid(b0[...].astype(F32))
    zpad[CONV_PAD:CONV_PAD + SEQ, 256:C] = a1[...].astype(F32) * jax.nn.sigmoid(b1[...].astype(F32))


def _tap_windows(src, base, win):
    for b in range(8):
        win[b, 0:WIN - 8, :] = src[base + b:base + b + WIN - 8, :]


def _taps(win, w_ref, init, out, flip):
    def sub(si, carry):
        r0 = pl.multiple_of(si * SUB, SUB)
        acc = jnp.broadcast_to(init, (SUB, CONV_CHANNELS))
        for k in range(CONV_WIDTH):
            wk = (CONV_WIDTH - 1 - k) if flip else k
            acc = acc + w_ref[wk:wk + 1, :] * win[k % 8, pl.ds(r0 + 8 * (k // 8), SUB), :]
        out[pl.ds(r0, SUB), :] = acc
        return carry

    lax.fori_loop(0, CHUNK // SUB, sub, 0)


def _tap_grads(win, du, dwacc):
    def sub(si, carry):
        r0 = pl.multiple_of(si * SUB, SUB)
        d = du[pl.ds(r0, SUB), :]
        for k in range(CONV_WIDTH):
            p = d * win[k % 8, pl.ds(r0 + 8 * (k // 8), SUB), :]
            dwacc[8 * k:8 * k + 8, :] += (p[0:8] + p[8:16]) + (p[16:24] + p[24:32])
        return carry

    lax.fori_loop(0, CHUNK // SUB, sub, 0)


def _ln_parts(u):
    mu = jnp.mean(u, axis=-1, keepdims=True)
    xc = u - mu
    rstd = lax.rsqrt(jnp.mean(xc * xc, axis=-1, keepdims=True) + EPS)
    return xc * rstd, rstd


def _conv_fwd(proj, conv_w, conv_b, ln_g, ln_b, carry=None):
    T, C = proj.shape[0], CONV_CHANNELS
    vec = pl.BlockSpec((1, C), lambda i: (0, 0))
    c_in, c_out, c_sems = _carry_io(carry)

    def body(*refs):
        a0, a1, b0, b1, w_ref, cb_ref, g_ref, be_ref = refs[:8]
        ci_refs = refs[8:8 + len(c_in)]
        c_ref = refs[8 + len(c_in)]
        co_refs = refs[9 + len(c_in):9 + len(c_in) + len(c_out)]
        zpad, win, ubuf = refs[9 + len(c_in) + len(c_out):12 + len(c_in) + len(c_out)]
        cs_refs = refs[12 + len(c_in) + len(c_out):]
        if carry is not None:
            carry.start(ci_refs, co_refs, cs_refs)
        _glu_to_pad(a0, a1, b0, b1, zpad)
        for ci in range(T // CHUNK):
            _tap_windows(zpad, ci * CHUNK + CONV_PAD - (CONV_WIDTH - 1), win)
            _taps(win, w_ref, cb_ref[...], ubuf, False)
            xh, _ = _ln_parts(ubuf[...])
            ln = xh * g_ref[...] + be_ref[...]
            c_ref[ci * CHUNK:(ci + 1) * CHUNK, :] = (ln * jax.nn.sigmoid(ln)).astype(BF)
        if carry is not None:
            carry.finish(ci_refs, co_refs, cs_refs)

    res = pl.pallas_call(
        body, name="conv_fwd", grid=(1,),
        in_specs=[*_GLU_SPECS, pl.BlockSpec((CONV_PAD, C), lambda i: (0, 0)), vec, vec, vec, *[ANY] * len(c_in)],
        out_specs=[pl.BlockSpec((T, C), lambda i: (0, 0)), *[ANY] * len(c_out)],
        out_shape=[_sds((T, C), BF), *c_out],
        scratch_shapes=[pltpu.VMEM((PAD_ROWS, C), F32), pltpu.VMEM((8, WIN, C), F32), pltpu.VMEM((CHUNK, C), F32),
                        *c_sems],
        compiler_params=_params(("arbitrary",)),
    )(proj, proj, proj, proj, conv_w, conv_b, ln_g, ln_b, *c_in)
    return res[0], res[1:]


def _conv_bwd(proj, d_c, conv_w, conv_b, ln_g, ln_b, carry=None):
    T, C = proj.shape[0], CONV_CHANNELS
    vec = pl.BlockSpec((1, C), lambda i: (0, 0))
    wspec = pl.BlockSpec((CONV_PAD, C), lambda i: (0, 0))
    c_in, c_out, c_sems = _carry_io(carry)

    def body(*refs):
        a0, a1, b0, b1, dc_ref, w_ref, cb_ref, g_ref, be_ref = refs[:9]
        ci_refs = refs[9:9 + len(c_in)]
        o = 9 + len(c_in)
        dglu_ref, dw_ref, dcb_ref, dg_ref, dbe_ref = refs[o:o + 5]
        co_refs = refs[o + 5:o + 5 + len(c_out)]
        zpad, dupad, win, ubuf, dwacc = refs[o + 5 + len(c_out):o + 10 + len(c_out)]
        cs_refs = refs[o + 10 + len(c_out):]
        if carry is not None:
            carry.start(ci_refs, co_refs, cs_refs)
        _glu_to_pad(a0, a1, b0, b1, zpad)
        dupad[T:, :] = jnp.zeros((2 * CONV_PAD, C), F32)
        dwacc[...] = jnp.zeros_like(dwacc)
        dcb_ref[...] = jnp.zeros_like(dcb_ref)
        dg_ref[...] = jnp.zeros_like(dg_ref)
        dbe_ref[...] = jnp.zeros_like(dbe_ref)
        for ci in range(T // CHUNK):
            rows = slice(ci * CHUNK, (ci + 1) * CHUNK)
            _tap_windows(zpad, ci * CHUNK + CONV_PAD - (CONV_WIDTH - 1), win)
            _taps(win, w_ref, cb_ref[...], ubuf, False)
            xh, rstd = _ln_parts(ubuf[...])
            ln = xh * g_ref[...] + be_ref[...]
            sg = jax.nn.sigmoid(ln)
            dln = dc_ref[rows, :].astype(F32) * (sg * (1.0 + ln * (1.0 - sg)))
            dg_ref[...] += jnp.sum(dln * xh, axis=0, keepdims=True)
            dbe_ref[...] += jnp.sum(dln, axis=0, keepdims=True)
            dxh = dln * g_ref[...]
            du = rstd * (dxh - jnp.mean(dxh, axis=-1, keepdims=True)
                         - xh * jnp.mean(dxh * xh, axis=-1, keepdims=True))
            dupad[rows, :] = du
            dcb_ref[...] += jnp.sum(du, axis=0, keepdims=True)
            _tap_grads(win, dupad.at[rows, :], dwacc)
        for k in range(CONV_WIDTH):
            dw_ref[k:k + 1, :] = jnp.sum(dwacc[8 * k:8 * k + 8, :], axis=0, keepdims=True)
        dw_ref[CONV_WIDTH:, :] = jnp.zeros((CONV_PAD - CONV_WIDTH, C), F32)
        for ci in range(T // CHUNK):
            rows = slice(ci * CHUNK, (ci + 1) * CHUNK)
            _tap_windows(dupad, ci * CHUNK, win)
            _taps(win, w_ref, jnp.zeros((1, C), F32), ubuf, True)
            dz = ubuf[...]
            for half, (a, b) in enumerate(((a0, b0), (a1, b1))):
                sb = jax.nn.sigmoid(b[rows, :].astype(F32))
                dzh = dz[:, half * 256:(half + 1) * 256]
                dglu_ref[rows, half * 256:(half + 1) * 256] = (dzh * sb).astype(BF)
                dglu_ref[rows, C + half * 256:C + (half + 1) * 256] = (
                    dzh * a[rows, :].astype(F32) * sb * (1.0 - sb)).astype(BF)
        if carry is not None:
            carry.finish(ci_refs, co_refs, cs_refs)

    res = pl.pallas_call(
        body, name="conv_bwd", grid=(1,),
        in_specs=[*_GLU_SPECS, pl.BlockSpec((T, C), lambda i: (0, 0)), wspec, vec, vec, vec, *[ANY] * len(c_in)],
        out_specs=[pl.BlockSpec((T, 2 * C), lambda i: (0, 0)), wspec, vec, vec, vec, *[ANY] * len(c_out)],
        out_shape=[_sds((T, 2 * C), BF), _sds((CONV_PAD, C), F32), _sds((1, C), F32), _sds((1, C), F32),
                   _sds((1, C), F32), *c_out],
        scratch_shapes=[pltpu.VMEM((PAD_ROWS, C), F32), pltpu.VMEM((PAD_ROWS, C), F32), pltpu.VMEM((8, WIN, C), F32),
                        pltpu.VMEM((CHUNK, C), F32), pltpu.VMEM((8 * CONV_PAD, C), F32), *c_sems],
        compiler_params=_params(("arbitrary",)),
    )(proj, proj, proj, proj, d_c, conv_w, conv_b, ln_g, ln_b, *c_in)
    return res[:5], res[5:]


_GATE_BLK = GATE_OFF // 256


def _ffn_in_swiglu(h2, wf_t, carry=None):
    T, D = h2.shape
    tm, tn = 1024, D_FF // 2
    nj, ni = D_FF // tn, T // tm
    c_in, c_out, c_sems = _carry_io(carry)

    def body(*refs):
        a_ref, bg_ref, bu_ref = refs[:3]
        ci_refs = refs[3:3 + len(c_in)]
        act_ref, g_ref, u_ref = refs[3 + len(c_in):6 + len(c_in)]
        co_refs = refs[6 + len(c_in):6 + len(c_in) + len(c_out)]
        cs_refs = refs[6 + len(c_in) + len(c_out):]
        j, i = pl.program_id(0), pl.program_id(1)
        if carry is not None:
            @pl.when((j == 0) & (i == 0))
            def _():
                carry.start(ci_refs, co_refs, cs_refs)
        a = a_ref[...]
        for c0, c1 in ((0, 768), (768, tn)):
            g = lax.dot_general(a, bg_ref[c0:c1, :], _DIMS["NT"], preferred_element_type=F32)
            u = lax.dot_general(a, bu_ref[c0:c1, :], _DIMS["NT"], preferred_element_type=F32)
            act_ref[:, c0:c1] = (g * jax.nn.sigmoid(g) * u).astype(BF)
            g_ref[:, c0:c1] = g.astype(BF)
            u_ref[:, c0:c1] = u.astype(BF)
        if carry is not None:
            @pl.when((j == nj - 1) & (i == ni - 1))
            def _():
                carry.finish(ci_refs, co_refs, cs_refs)

    t = pl.BlockSpec((tm, tn), lambda j, i: (i, j))
    res = pl.pallas_call(
        body, name="ffn_in_swiglu", grid=(nj, ni),
        in_specs=[pl.BlockSpec((tm, D), lambda j, i: (i, 0)), pl.BlockSpec((tn, D), lambda j, i: (j, 0)),
                  pl.BlockSpec((tn, D), lambda j, i: (nj + j, 0)), *[ANY] * len(c_in)],
        out_specs=[t, t, t, *[ANY] * len(c_out)], out_shape=[*[_sds((T, D_FF), BF)] * 3, *c_out],
        scratch_shapes=c_sems,
        compiler_params=_params(("arbitrary", "arbitrary")),
    )(h2, wf_t, wf_t, *c_in)
    return res[:3], res[3:]


def _proj_merge(o, c, wap_t, wcp_t, b_cp, proj):
    T, D = o.shape[0], wap_t.shape[0]
    tm, tg = 1024, 256
    nj = D // tg

    def body(o_ref, c_ref, wa_ref, wc_ref, b_ref, g0_ref, g1_ref, ya_ref, yc_ref, m_ref):
        ya = lax.dot_general(o_ref[...], wa_ref[...], _DIMS["NT"], preferred_element_type=F32)
        yc = lax.dot_general(c_ref[...], wc_ref[...], _DIMS["NT"], preferred_element_type=F32) + b_ref[...]
        ya_ref[...] = ya.astype(BF)
        yc_ref[...] = yc.astype(BF)
        m_ref[...] = (jax.nn.sigmoid(g0_ref[...].astype(F32)) * ya + jax.nn.sigmoid(g1_ref[...].astype(F32)) * yc).astype(BF)

    act = pl.BlockSpec((tm, o.shape[1]), lambda j, i: (i, 0))
    wgt = pl.BlockSpec((tg, o.shape[1]), lambda j, i: (j, 0))
    t = pl.BlockSpec((tm, tg), lambda j, i: (i, j))
    return pl.pallas_call(
        body, name="proj_merge", grid=(nj, T // tm),
        in_specs=[act, act, wgt, wgt, pl.BlockSpec((1, tg), lambda j, i: (0, j)),
                  pl.BlockSpec((tm, tg), lambda j, i: (i, _GATE_BLK + j)),
                  pl.BlockSpec((tm, tg), lambda j, i: (i, _GATE_BLK + nj + j))],
        out_specs=[t, t, t], out_shape=[_sds((T, D), BF)] * 3,
        compiler_params=_params(("arbitrary", "arbitrary")),
    )(o, c, wap_t, wcp_t, b_cp, proj, proj)


def _proj_in_dw(segs, h):
    T, D = h.shape
    tb = 256
    nblk = [seg.shape[1] // tb for seg in segs]
    starts = [sum(nblk[:q]) for q in range(len(segs))]
    n_seg = len(segs)

    def body(*refs):
        seg_refs, h_ref, o_ref, cs_ref = refs[:n_seg], refs[n_seg], refs[n_seg + 1], refs[n_seg + 2]
        i = pl.program_id(0)
        for seg_ref, st, nb in zip(seg_refs, starts, nblk):
            @pl.when((i >= st) & (i < st + nb))
            def _(seg_ref=seg_ref):
                a = seg_ref[...]
                o_ref[...] = lax.dot_general(a, h_ref[...], _DIMS["TN"], preferred_element_type=F32).astype(BF)
                cs_ref[...] = jnp.sum(a.astype(F32), axis=0, keepdims=True)

    in_specs = [pl.BlockSpec((T, tb), functools.partial(lambda i, st, nb: (0, jnp.clip(i - st, 0, nb - 1)), st=st, nb=nb))
                for st, nb in zip(starts, nblk)]
    return pl.pallas_call(
        body, name="proj_in_dw", grid=(sum(nblk),),
        in_specs=[*in_specs, pl.BlockSpec((T, D), lambda i: (0, 0))],
        out_specs=[pl.BlockSpec((tb, D), lambda i: (i, 0)), pl.BlockSpec((1, tb), lambda i: (0, i))],
        out_shape=[_sds((sum(nblk) * tb, D), BF), _sds((1, sum(nblk) * tb), F32)],
        compiler_params=_params(("arbitrary",)),
    )(*segs, h)


def _local_step(x, target, small, wi_t, conv_w, plan):
    T, D = x.shape
    tm = 1024

    def carried(call, res, carry):
        if carry is None:
            return res
        outs, got = res
        plan.done(call, got)
        return outs

    h, r1 = _rms_fwd("rms_mix", x, small["g_mix_norm"])

    def ep_add(acc, ex, outs, ids, scr):
        outs[0][...] = acc + ex[0][...]

    tn_in = IN_WIDTH // 3
    carry = plan.carry("proj_in")
    def ep_bias_bf16(acc, ex, outs, ids, scr):
        outs[0][...] = (acc + ex[0][...]).astype(BF)

    proj, = carried("proj_in", _matmul("proj_in", [h], wi_t, "NT", m=T, n=IN_WIDTH, tm=tm, tn=tn_in,
                                       epilogue=ep_bias_bf16, extra=[(small["b_in"], _row(tn_in))],
                                       outs=[(_sds((T, IN_WIDTH), BF), _tile(tm, tn_in))], carry=carry), carry)
    plan.launch("gather_ffn", after=proj)
    o, got = _attn_fwd(proj, small["sinks"], carry=plan.carry("attn_fwd"))
    plan.done("attn_fwd", got)
    c, got = _conv_fwd(proj, conv_w, small["conv_b"], small["ln_g"], small["ln_b"], carry=plan.carry("conv_fwd"))
    plan.done("conv_fwd", got)
    wap_t, wcp_t, w_out = plan.weight("w_attn_proj"), plan.weight("w_conv_proj"), plan.weight("w_out")
    ya, yc, merged = _proj_merge(o, c, wap_t, wcp_t, small["b_conv_proj"], proj)

    tg = 256
    gate_specs = [pl.BlockSpec((tm, tg), lambda j, i, k: (i, _GATE_BLK + j)),
                  pl.BlockSpec((tm, tg), lambda j, i, k: (i, _GATE_BLK + D // tg + j))]

    def ep_residual_rms(acc, ex, outs, ids, scr):
        x2v = acc + ex[0][...]
        r = lax.rsqrt(jnp.mean(x2v * x2v, axis=-1, keepdims=True) + EPS)
        outs[0][...] = x2v
        outs[1][...] = (x2v * r * ex[1][...]).astype(BF)
        outs[2][...] = r

    carry = plan.carry("out_proj")
    x2, h2, r2 = carried("out_proj", _matmul(
        "out_proj_rms", [merged], w_out, "NN", m=T, n=D, tm=512, tn=D, epilogue=ep_residual_rms,
        extra=[(x, _tile(512, D)), (small["g_ffn_norm"], _row(D))],
        outs=[(_sds((T, D), F32), _tile(512, D)), (_sds((T, D), BF), _tile(512, D)),
              (_sds((T, 1), F32), pl.BlockSpec((512, 1), lambda j, i, k: (i, 0)))], carry=carry), carry)
    plan.launch("gather_down", after=x2)
    wf_t = plan.weight("w_ffn_in")
    (act, gate, up), got = _ffn_in_swiglu(h2, wf_t, carry=plan.carry("ffn_in_swiglu"))
    plan.done("ffn_in_swiglu", got)
    w_down = plan.weight("w_ffn_down")
    def ep_residual_loss(acc, ex, outs, ids, scr):
        dx, dg, part = _loss_head(acc + ex[0][...], ex[1][...], ex[2][...])
        outs[0][...] = dx
        outs[1][...] = dx.astype(BF)
        _accumulate_rows(outs[2], dg, ids[1] == 0)
        _accumulate_rows(outs[3], part, ids[1] == 0)

    dx3, dx3_b, dg_final, loss = _matmul(
        "ffn_down_loss", [act], w_down, "NN", m=T, n=D, tm=512, tn=D, epilogue=ep_residual_loss,
        extra=[(x2, _tile(512, D)), (small["g_final"], _row(D)), (target, _tile(512, D))],
        outs=[(_sds((T, D), F32), _tile(512, D)), (_sds((T, D), BF), _tile(512, D)), (_sds((1, D), F32), _row(D)),
              (_sds((1, 1), F32), pl.BlockSpec((1, 1), lambda j, i, k: (0, 0)))])

    tn_ff = D_FF // 2

    def ep_swiglu_bwd(acc, ex, outs, ids, scr):
        g, u = ex[0][...].astype(F32), ex[1][...].astype(F32)
        sg = jax.nn.sigmoid(g)
        outs[0][...] = (acc * u * sg * (1.0 + g * (1.0 - sg))).astype(BF)
        outs[1][...] = (acc * g * sg).astype(BF)

    dgate, dup = _matmul(
        "ffn_down_bwd", [dx3_b], w_down, "NT", m=T, n=D_FF, tm=tm, tn=tn_ff, epilogue=ep_swiglu_bwd,
        extra=[(gate, _tile(tm, tn_ff)), (up, _tile(tm, tn_ff))],
        outs=[(_sds((T, D_FF), BF), _tile(tm, tn_ff)), (_sds((T, D_FF), BF), _tile(tm, tn_ff))])

    def dw(name, a, b, rows, cols, row_off=0, alias=None, total_rows=None, colsum=False):
        total_rows = rows if total_rows is None else total_rows
        tmw = rows if rows <= 1024 else 256
        blk, rem = divmod(row_off, tmw)
        assert rem == 0

        def ep(acc, ex, outs, ids, scr):
            outs[0][...] = acc.astype(BF)
            if colsum:
                outs[1][...] = jnp.sum(ex[0][...].astype(F32), axis=0, keepdims=True)

        outs = [(_sds((total_rows, cols), BF), pl.BlockSpec((tmw, cols), lambda j, i, k: (blk + i, j)))]
        extra = []
        if colsum:
            extra = [(a, pl.BlockSpec((T, tmw), lambda j, i, k: (0, i)))]
            outs.append((_sds((1, rows), F32), pl.BlockSpec((1, tmw), lambda j, i, k: (0, i))))
        carry = plan.carry(name)
        res = carried(name, _matmul(name, [a], b, "TN", m=rows, n=cols, tm=tmw, tn=cols, epilogue=ep, extra=extra,
                                    outs=outs, alias=None if alias is None else (alias, 0), carry=carry), carry)
        return res if colsum else res[0]

    plan.grad_ready(dict(w_ffn_down=dw("ffn_down_dw", act, dx3_b, D_FF, D)))

    def ep_rms_bwd(acc, ex, outs, ids, scr):
        dx, dg = _rms_bwd(acc, ex[0][...], ex[1][...], ex[2][...])
        dx = ex[3][...] + dx
        outs[0][...] = dx
        outs[1][...] = dx.astype(BF)
        _accumulate_rows(outs[2], dg, ids[1] == 0)

    def rms_bwd_io(tm_, xin, r, g, dres):
        return dict(
            extra=[(xin, _tile(tm_, D)), (r, pl.BlockSpec((tm_, 1), lambda j, i, k: (i, 0))), (g, _row(D)),
                   (dres, _tile(tm_, D))],
            outs=[(_sds((T, D), F32), _tile(tm_, D)), (_sds((T, D), BF), _tile(tm_, D)), (_sds((1, D), F32), _row(D))])

    carry = plan.carry("ffn_in_bwd")
    dx2, dx2_b, dg_ffn = carried(
        "ffn_in_bwd",
        _matmul("ffn_in_bwd", [dgate, dup], wf_t, "NN", m=T, n=D, tm=tm, tn=D, tk=D_FF // 2, epilogue=ep_rms_bwd,
                carry=carry, **rms_bwd_io(tm, x2, r2, small["g_ffn_norm"], dx3)), carry)
    plan.launch("send_down")
    gwf_t = dw("ffn_in_dw_gate", dgate, h2, D_FF, D, total_rows=2 * D_FF)
    gwf_t = dw("ffn_in_dw_up", dup, h2, D_FF, D, row_off=D_FF, alias=gwf_t, total_rows=2 * D_FF)
    plan.grad_ready(dict(w_ffn_in=gwf_t))

    def ep_merge_bwd(acc, ex, outs, ids, scr):
        s0 = jax.nn.sigmoid(ex[2][...].astype(F32))
        s1 = jax.nn.sigmoid(ex[3][...].astype(F32))
        outs[0][...] = (acc * s0).astype(BF)
        outs[1][...] = (acc * s1).astype(BF)
        outs[2][...] = (acc * ex[0][...].astype(F32) * s0 * (1.0 - s0)).astype(BF)
        outs[3][...] = (acc * ex[1][...].astype(F32) * s1 * (1.0 - s1)).astype(BF)

    carry = plan.carry("out_proj_bwd_merge")
    dya, dyc, dg0, dg1 = carried(
        "out_proj_bwd_merge",
        _matmul("out_proj_bwd_merge", [dx2_b], w_out, "NT", m=T, n=D, tm=tm, tn=tg, epilogue=ep_merge_bwd,
                extra=[(ya, _tile(tm, tg)), (yc, _tile(tm, tg)), (proj, gate_specs[0]), (proj, gate_specs[1])],
                outs=[(_sds((T, D), BF), _tile(tm, tg))] * 4, carry=carry), carry)
    plan.launch("send_ffn")
    gw_out = dw("out_proj_dw", merged, dx2_b, D, D)
    d_o, = _matmul("attn_proj_bwd", [dya], wap_t, "NN", m=T, n=ATTN_WIDTH, tm=tm, tn=ATTN_WIDTH,
                   epilogue=_store(BF), outs=[(_sds((T, ATTN_WIDTH), BF), _tile(tm, ATTN_WIDTH))])
    d_c, = _matmul("conv_proj_bwd", [dyc], wcp_t, "NN", m=T, n=CONV_CHANNELS, tm=tm, tn=CONV_CHANNELS,
                   epilogue=_store(BF), outs=[(_sds((T, CONV_CHANNELS), BF), _tile(tm, CONV_CHANNELS))])
    gwap_t = dw("attn_proj_dw", dya, o, D, ATTN_WIDTH)
    gwcp_t, db_cp = dw("conv_proj_dw", dyc, c, D, CONV_CHANNELS, colsum=True)
    plan.grad_ready(dict(w_out=gw_out, w_attn_proj=gwap_t, w_conv_proj=gwcp_t))
    (dglu, dcw, dcb, dlng, dlnb), got = _conv_bwd(proj, d_c, conv_w, small["conv_b"], small["ln_g"], small["ln_b"],
                                                  carry=plan.carry("conv_bwd"))
    plan.done("conv_bwd", got)
    plan.launch("send_mix")
    (dqkv, dsinks), got = _attn_bwd(proj, d_o, small["sinks"], carry=plan.carry("attn_bwd"))
    plan.done("attn_bwd", got)

    segs = [dqkv, dglu, dg0, dg1]
    gwi_t, db_in = _proj_in_dw(segs, h)
    plan.grad_ready(dict(w_in=gwi_t))
    plan.alone("swap_inp")
    plan.launch("send_inp")
    carry = plan.carry("proj_in_bwd")
    dx, _, dg_mix = carried(
        "proj_in_bwd",
        _matmul("proj_in_bwd", segs, wi_t, "NN", m=T, n=D, tm=512, tn=D, epilogue=ep_rms_bwd, carry=carry,
                **rms_bwd_io(512, x, r1, small["g_mix_norm"], plan.behind("inp", dx2))), carry)

    parts = dict(g_mix_norm=dg_mix, b_in=db_in, sinks=dsinks, conv_w=dcw, conv_b=dcb, ln_g=dlng, ln_b=dlnb,
                 b_conv_proj=db_cp, g_ffn_norm=dg_ffn, g_final=dg_final, loss=loss)
    return dx, parts


def _place():
    x, y, c = lax.axis_index("x"), lax.axis_index("y"), lax.axis_index("c")
    return x, y, c, [(1 - x, y), (x, 1 - y), (1 - x, 1 - y)]


def _gather_copies(x_refs, out_refs, rows_per, send_sems, recv_sems, local_sems):
    x, y, c, chips = _place()
    me, sibling = (x, y, c), (x, y, 1 - c)

    def rows(a, px, py, pc):
        return out_refs[a].at[pl.ds((4 * px + 2 * py + pc) * rows_per[a], rows_per[a])]

    def copy(a, k, block, to, src=None):
        return pltpu.make_async_remote_copy(
            src_ref=rows(a, *block) if src is None else src, dst_ref=rows(a, *block),
            send_sem=send_sems.at[7 * a + k], recv_sem=recv_sems.at[7 * a + k], device_id=to, device_id_type=MESH)

    def local(a):
        return pltpu.make_async_copy(x_refs[a], rows(a, *me), local_sems.at[a])

    def first(a):
        return [copy(a, 0, me, sibling, src=x_refs[a])] + [copy(a, 1 + j, me, (*chip, c), src=x_refs[a])
                                                          for j, chip in enumerate(chips)]

    def arrive(a, j):
        return copy(a, 1 + j, (*chips[j], c), me)

    def passed(a, j):
        return copy(a, 4 + j, (*chips[j], c), sibling)

    def from_sibling(a):
        return [copy(a, 0, sibling, me)] + [copy(a, 4 + j, (*chip, 1 - c), me) for j, chip in enumerate(chips)]

    return len(x_refs), local, first, arrive, passed, from_sibling


def _gather_start(*refs):
    n, local, first, _, _, _ = _gather_copies(*refs)
    for a in range(n):
        local(a).start()
        for cp in first(a):
            cp.start()


def _gather_finish(*refs):
    n, local, first, arrive, passed, from_sibling = _gather_copies(*refs)
    for a in range(n):
        for j in range(3):
            arrive(a, j).wait_recv()
            passed(a, j).start()
    for a in range(n):
        for cp in from_sibling(a):
            cp.wait_recv()
    for a in range(n):
        for cp in first(a) + [passed(a, j) for j in range(3)]:
            cp.wait_send()
        local(a).wait()


def _gather_peers():
    x, y, c, chips = _place()
    return [(x, y, 1 - c)] + [(*chip, c) for chip in chips]


def _gather_sems(n):
    return [pltpu.SemaphoreType.DMA((7 * n,)), pltpu.SemaphoreType.DMA((7 * n,)), pltpu.SemaphoreType.DMA((n,))]


def _gather_carry(shards):
    rows_per = [s.shape[0] for s in shards]
    return _Carry(shards, [_sds((N_DEV * s.shape[0],) + s.shape[1:], s.dtype) for s in shards],
                  _gather_sems(len(shards)),
                  lambda ins, outs, sems: _gather_start(ins, outs, rows_per, *sems),
                  lambda ins, outs, sems: _gather_finish(ins, outs, rows_per, *sems), _gather_peers)


def _first_gather(shards):
    n = len(shards)
    rows_per = [s.shape[0] for s in shards]

    def body(*refs):
        x_refs, out_refs = refs[:n], refs[n:2 * n]
        send_sems, recv_sems, local_sems = refs[2 * n:]
        x, y, c, chips = _place()
        me, sibling = (x, y, c), (x, y, 1 - c)
        near_x, near_y, far = (*chips[0], c), (*chips[1], c), (*chips[2], c)

        def rows(a, dev, part):
            h = rows_per[a] // 2
            lo, size = {"all": (0, 2 * h), "low": (0, h), "high": (h, h)}[part]
            return out_refs[a].at[pl.ds((4 * dev[0] + 2 * dev[1] + dev[2]) * rows_per[a] + lo, size)]

        def copy(a, k, block, part, to, src=None):
            return pltpu.make_async_remote_copy(
                src_ref=rows(a, block, part) if src is None else src, dst_ref=rows(a, block, part),
                send_sem=send_sems.at[9 * a + k], recv_sem=recv_sems.at[9 * a + k], device_id=to, device_id_type=MESH)

        other = lambda dev: (dev[0], dev[1], 1 - c)
        sent = []
        for a in range(n):
            pltpu.make_async_copy(x_refs[a], rows(a, me, "all"), local_sems.at[a]).start()
            sent += [copy(a, 0, me, "all", sibling, src=x_refs[a]), copy(a, 1, me, "all", near_x, src=x_refs[a]),
                     copy(a, 2, me, "all", near_y, src=x_refs[a])]
        for cp in sent:
            cp.start()
        for a in range(n):
            copy(a, 1, near_x, "all", me).wait_recv()
            copy(a, 2, near_y, "all", me).wait_recv()
            passed = [copy(a, 3, near_y, "high", near_x), copy(a, 4, near_x, "low", near_y),
                      copy(a, 5, near_x, "all", sibling), copy(a, 6, near_y, "all", sibling)]
            for cp in passed:
                cp.start()
            sent += passed
        for a in range(n):
            copy(a, 3, far, "high", me).wait_recv()
            copy(a, 4, far, "low", me).wait_recv()
            passed = [copy(a, 7, far, "high", sibling), copy(a, 8, far, "low", sibling)]
            for cp in passed:
                cp.start()
            sent += passed
        for a in range(n):
            copy(a, 0, sibling, "all", me).wait_recv()
            copy(a, 5, other(near_x), "all", me).wait_recv()
            copy(a, 6, other(near_y), "all", me).wait_recv()
            copy(a, 7, other(far), "high", me).wait_recv()
            copy(a, 8, other(far), "low", me).wait_recv()
        for cp in sent:
            cp.wait_send()
        for a in range(n):
            pltpu.make_async_copy(x_refs[a], rows(a, me, "all"), local_sems.at[a]).wait()

    return pl.pallas_call(
        body, name="weights_first_gather", in_specs=[ANY] * n, out_specs=[ANY] * n,
        out_shape=[_sds((N_DEV * s.shape[0],) + s.shape[1:], s.dtype) for s in shards],
        scratch_shapes=[pltpu.SemaphoreType.DMA((9 * n,)), pltpu.SemaphoreType.DMA((9 * n,)),
                        pltpu.SemaphoreType.DMA((n,))],
    )(*shards)


def _swap_carry(grads):
    n = len(grads)

    def copies(g_refs, out_refs, sems):
        send_sems, recv_sems = sems
        x, y, c, _ = _place()
        return [pltpu.make_async_remote_copy(
            src_ref=g_refs[a].at[2 * p + 1 - c], dst_ref=out_refs[a].at[p],
            send_sem=send_sems.at[4 * a + p], recv_sem=recv_sems.at[4 * a + p],
            device_id=(x, y, 1 - c), device_id_type=MESH) for a in range(n) for p in range(4)]

    def start(ins, outs, sems):
        for cp in copies(ins, outs, sems):
            cp.start()

    def finish(ins, outs, sems):
        for cp in copies(ins, outs, sems):
            cp.wait()

    def peers():
        x, y, c, _ = _place()
        return [(x, y, 1 - c)]

    return _Carry(grads, [_sds((4,) + g.shape[1:], g.dtype) for g in grads],
                  [pltpu.SemaphoreType.DMA((4 * n,)), pltpu.SemaphoreType.DMA((4 * n,))], start, finish, peers)


def _join(carries):
    carries = [c for c in carries if c is not None]
    if not carries:
        return None
    n_in = [len(c.arrays) for c in carries]
    n_out = [len(c.out_shapes) for c in carries]
    n_sem = [len(c.sems) for c in carries]

    def parts(refs, counts):
        cuts = [sum(counts[:q]) for q in range(len(counts) + 1)]
        return [refs[cuts[q]:cuts[q + 1]] for q in range(len(counts))]

    def start(ins, outs, sems):
        for c, i, o, s in zip(carries, parts(ins, n_in), parts(outs, n_out), parts(sems, n_sem)):
            c.start(i, o, s)

    def finish(ins, outs, sems):
        for c, i, o, s in zip(carries, parts(ins, n_in), parts(outs, n_out), parts(sems, n_sem)):
            c.finish(i, o, s)

    return _Carry([a for c in carries for a in c.arrays], [o for c in carries for o in c.out_shapes],
                  [s for c in carries for s in c.sems], start, finish)


def _run_carry(name, carry):
    n_in, n_out = len(carry.arrays), len(carry.out_shapes)

    def body(*refs):
        carry.start(refs[:n_in], refs[n_in:n_in + n_out], refs[n_in + n_out:])
        carry.finish(refs[:n_in], refs[n_in:n_in + n_out], refs[n_in + n_out:])

    return pl.pallas_call(body, name=name, in_specs=[ANY] * n_in, out_specs=[ANY] * n_out,
                          out_shape=carry.out_shapes, scratch_shapes=carry.sems)(*carry.arrays)


def _run_carry_async(name, carry, collective_id):
    ins = [jax.new_ref(a, memory_space=pltpu.MemorySpace.HBM) for a in carry.arrays]
    outs = [jax.empty_ref(o, memory_space=pltpu.MemorySpace.HBM) for o in carry.out_shapes]

    @pl.kernel(mesh=plsc.ScalarSubcoreMesh(axis_name="sequencer", num_cores=1), name=name,
               scratch_types=tuple(carry.sems), compiler_params=pltpu.CompilerParams(collective_id=collective_id))
    def launch(*sems):
        barrier = pltpu.get_barrier_semaphore()
        peers = carry.peers()
        for peer in peers:
            pl.semaphore_signal(barrier, inc=1, device_id=peer, device_id_type=MESH)
        pl.semaphore_wait(barrier, len(peers))
        carry.start(ins, outs, sems)
        carry.finish(ins, outs, sems)

    launch()
    return [o[...] for o in outs]


def _chip_sum(name, g, got, c):
    _, rows, cols = g.shape

    def body(c_ref, g_ref, got_ref, o_ref):
        o_ref[...] = (g_ref[...].astype(F32) + got_ref[...].astype(F32)).astype(BF)

    return pl.pallas_call(
        body, name=name,
        grid_spec=pltpu.PrefetchScalarGridSpec(
            num_scalar_prefetch=1, grid=(4,),
            in_specs=[pl.BlockSpec((1, rows, cols), lambda p, c_ref: (2 * p + c_ref[0], 0, 0)),
                      pl.BlockSpec((1, rows, cols), lambda p, c_ref: (p, 0, 0))],
            out_specs=pl.BlockSpec((1, rows, cols), lambda p, c_ref: (p, 0, 0))),
        out_shape=_sds((4, rows, cols), BF),
        compiler_params=_params(("arbitrary",)),
    )(c, g, got)


def _send_carry(sums, ks):
    n, nk = len(sums), len(ks)

    def copies(s_refs, out_refs, sems):
        send_sems, recv_sems = sems
        x, y, c, chips = _place()
        return [pltpu.make_async_remote_copy(
            src_ref=s_refs[a].at[2 * chips[k][0] + chips[k][1]], dst_ref=out_refs[a].at[q],
            send_sem=send_sems.at[nk * a + q], recv_sem=recv_sems.at[nk * a + q],
            device_id=(*chips[k], c), device_id_type=MESH) for a in range(n) for q, k in enumerate(ks)]

    def start(ins, outs, sems):
        for cp in copies(ins, outs, sems):
            cp.start()

    def finish(ins, outs, sems):
        for cp in copies(ins, outs, sems):
            cp.wait()

    def peers():
        x, y, c, chips = _place()
        return [(*chips[k], c) for k in ks]

    return _Carry(sums, [_sds((nk,) + s.shape[1:], s.dtype) for s in sums],
                  [pltpu.SemaphoreType.DMA((nk * n,)), pltpu.SemaphoreType.DMA((nk * n,))], start, finish, peers)


def _adam_math(w, g, m, v):
    m = ADAM_B1 * m + (1.0 - ADAM_B1) * g
    v = ADAM_B2 * v + (1.0 - ADAM_B2) * (g * g)
    m_hat = m / (1.0 - ADAM_B1 ** ADAM_STEP)
    v_hat = v / (1.0 - ADAM_B2 ** ADAM_STEP)
    delta = -ADAM_LR * (m_hat / (jnp.sqrt(v_hat) + ADAM_EPS) + ADAM_WD * w)
    return delta, m, v


def _adamw(name, w, g, m, v):
    rows, cols = w.shape
    tr = 256 if rows % 256 == 0 else rows

    def body(w_ref, g_ref, m_ref, v_ref, d_ref, nm_ref, nv_ref):
        d_ref[...], nm_ref[...], nv_ref[...] = _adam_math(w_ref[...], g_ref[...], m_ref[...], v_ref[...])

    t = pl.BlockSpec((tr, cols), lambda i: (i, 0))
    return pl.pallas_call(
        body, name=name, grid=(rows // tr,), in_specs=[t] * 4, out_specs=[t] * 3,
        out_shape=[_sds((rows, cols), F32)] * 3, compiler_params=_params(("arbitrary",)),
    )(w, g, m, v)


def _grad_adamw(name, g, got, got3, ids, w, m, v):
    _, rows, cols = g.shape
    n3 = len(got3)
    tr = rows // 2 if rows >= 256 else rows

    def body(ids_ref, g_ref, got_ref, *rest):
        w_ref, m_ref, v_ref, o_ref, d_ref, nm_ref, nv_ref = rest[n3:]
        tot = g_ref[0].astype(F32) + got_ref[0].astype(F32)
        for r_ref in rest[:n3]:
            for q in range(r_ref.shape[0]):
                tot = tot + r_ref[q].astype(F32)
        o_ref[...] = tot
        d_ref[...], nm_ref[...], nv_ref[...] = _adam_math(w_ref[...], tot, m_ref[...], v_ref[...])

    tile = pl.BlockSpec((tr, cols), lambda i, ids_ref: (i, 0))
    return pl.pallas_call(
        body, name=name,
        grid_spec=pltpu.PrefetchScalarGridSpec(
            num_scalar_prefetch=1, grid=(rows // tr,),
            in_specs=[pl.BlockSpec((1, tr, cols), lambda i, ids_ref: (ids_ref[0], i, 0)),
                      pl.BlockSpec((1, tr, cols), lambda i, ids_ref: (ids_ref[1], i, 0)),
                      *[pl.BlockSpec((r.shape[0], tr, cols), lambda i, ids_ref: (0, i, 0)) for r in got3],
                      tile, tile, tile],
            out_specs=[tile] * 4),
        out_shape=[_sds((rows, cols), F32)] * 4,
        compiler_params=_params(("arbitrary",)),
    )(ids, g, got, *got3, w, m, v)


SMALL_NAMES = ["g_mix_norm", "b_in", "sinks", "conv_b", "ln_g", "ln_b", "b_conv_proj", "g_ffn_norm", "g_final"]
_PACK_ROWS = 32


def _small_pack(parts):
    C = CONV_CHANNELS
    part_list = [parts["g_mix_norm"], parts["b_in"], parts["sinks"], parts["conv_b"], parts["ln_g"], parts["ln_b"],
                 parts["b_conv_proj"], parts["g_ffn_norm"], parts["g_final"], parts["loss"], parts["conv_w"]]

    def body(p_mix, p_b, p_sink, p_cb, p_lg, p_lb, p_bcp, p_ffn, p_fin, p_loss, p_cw, pack):
        pack[...] = jnp.zeros_like(pack)
        pack[0:1, :] = p_mix[...]
        pack[1:2, 0:GLU_OFF] = p_b[:, 0:GLU_OFF]
        pack[2:3, :] = p_b[:, GLU_OFF:GATE_OFF]
        pack[3:4, :] = p_b[:, GATE_OFF:GATE_OFF + D_MODEL]
        pack[4:5, :] = p_b[:, GATE_OFF + D_MODEL:]
        pack[5:6, 0:128] = p_sink[...]
        pack[6:7, 0:C] = p_cb[...]
        pack[6:7, C:2 * C] = p_lg[...]
        pack[7:8, 0:C] = p_lb[...]
        pack[8:9, :] = p_bcp[...]
        pack[9:10, :] = p_ffn[...]
        pack[10:11, :] = p_fin[...]
        pack[11:12, 0:128] = jnp.broadcast_to(p_loss[...], (1, 128))
        pack[12:28, 0:C] = p_cw[0:16, :]
        pack[12:28, C:2 * C] = p_cw[16:32, :]

    vm = pl.BlockSpec(memory_space=pltpu.VMEM)
    return pl.pallas_call(body, name="small_pack", in_specs=[vm] * len(part_list), out_specs=vm,
                          out_shape=_sds((_PACK_ROWS, D_MODEL), F32))(*part_list)


def _small_adamw(gathered, small_w, small_m, small_v):
    C = CONV_CHANNELS
    names = SMALL_NAMES
    widths = [small_w[k].shape[1] for k in names]
    n_small = len(names)

    def body(*refs):
        tot_ref = refs[0]
        w_refs = refs[1:1 + n_small]
        m_refs = refs[1 + n_small:1 + 2 * n_small]
        v_refs = refs[1 + 2 * n_small:1 + 3 * n_small]
        o = 1 + 3 * n_small
        loss_ref, cw_ref = refs[o], refs[o + 1]
        out_refs = refs[o + 2:o + 2 + 4 * n_small]
        tot = tot_ref[0:_PACK_ROWS, :]
        for d in range(1, N_DEV):
            tot = tot + tot_ref[d * _PACK_ROWS:(d + 1) * _PACK_ROWS, :]
        loss_ref[...] = tot[11:12, 0:1]
        cw_ref[0:16, :] = tot[12:28, 0:C]
        cw_ref[16:32, :] = tot[12:28, C:2 * C]
        grads = dict(
            g_mix_norm=tot[0:1, :],
            b_in=jnp.concatenate([tot[1:2, 0:GLU_OFF], tot[2:3, :], tot[3:4, :], tot[4:5, :]], axis=1),
            sinks=tot[5:6, 0:N_Q_HEADS], conv_b=tot[6:7, 0:C], ln_g=tot[6:7, C:2 * C], ln_b=tot[7:8, 0:C],
            b_conv_proj=tot[8:9, :], g_ffn_norm=tot[9:10, :], g_final=tot[10:11, :])
        for s, k in enumerate(names):
            g = grads[k]
            d, nm, nv = _adam_math(w_refs[s][...], g, m_refs[s][...], v_refs[s][...])
            out_refs[4 * s][...] = g
            out_refs[4 * s + 1][...] = d
            out_refs[4 * s + 2][...] = nm
            out_refs[4 * s + 3][...] = nv

    vm = pl.BlockSpec(memory_space=pltpu.VMEM)
    args = [gathered, *[small_w[k] for k in names], *[small_m[k] for k in names], *[small_v[k] for k in names]]
    out_shape = [_sds((1, 1), F32), _sds((CONV_PAD, C), F32)]
    for wd in widths:
        out_shape += [_sds((1, wd), F32)] * 4
    res = pl.pallas_call(
        body, name="small_adamw",
        in_specs=[vm] * len(args), out_specs=[vm] * len(out_shape), out_shape=out_shape,
        compiler_params=pltpu.CompilerParams(vmem_limit_bytes=VMEM_LIMIT_BYTES),
    )(*args)
    return res[0], res[1], {k: res[2 + 4 * s:6 + 4 * s] for s, k in enumerate(names)}


BIG = dict(w_in=True, w_attn_proj=True, w_conv_proj=True, w_out=False, w_ffn_in=True, w_ffn_down=False)
WEIGHT_NAMES = ["g_mix_norm", "w_in", "b_in", "sinks", "conv_w", "conv_b", "ln_g", "ln_b", "w_attn_proj",
                "w_conv_proj", "b_conv_proj", "w_out", "g_ffn_norm", "w_ffn_in", "w_ffn_down", "g_final"]


class _Plan:
    GROUPS = dict(down=["w_ffn_down"], ffn=["w_ffn_in"], mix=["w_out", "w_attn_proj", "w_conv_proj"], inp=["w_in"])
    ALL = (0, 1, 2)
    RIDES = dict(
        gather_mix=[("gather", ["w_attn_proj", "w_conv_proj", "w_out"])], gather_ffn=[("gather", ["w_ffn_in"])],
        gather_down=[("gather", ["w_ffn_down"])],
        ffn_in_bwd=[("swap", "down")], send_down=[("send", "down", ALL)],
        out_proj_bwd_merge=[("swap", "ffn")], send_ffn=[("send", "ffn", ALL)],
        conv_bwd=[("swap", "mix")], send_mix=[("send", "mix", ALL)],
        swap_inp=[("swap", "inp")], send_inp=[("send", "inp", ALL)])
    ASYNC = dict(gather_mix=1, gather_ffn=2, gather_down=3, send_down=4, send_ffn=5, send_mix=6, send_inp=7)

    def __init__(self, shards, c1):
        self.shards, self.c1 = shards, c1
        self.full, self.slots, self.got, self.sums, self.got3 = {}, {}, {}, {}, {}

    def weight(self, name):
        return self.full[name]

    def grad_ready(self, grads):
        for k, g in grads.items():
            self.slots[k] = g.reshape(N_DEV, g.shape[0] // N_DEV, g.shape[1])

    def _one(self, kind, what, ks=None):
        if kind == "gather":
            return _gather_carry([self.shards[k] for k in what])
        names = self.GROUPS[what]
        if kind == "swap":
            return _swap_carry([self.slots[k] for k in names])
        return _send_carry([self.sums[k] for k in names], ks)

    def carry(self, call):
        return _join([self._one(*ride) for ride in self.RIDES.get(call, [])])

    def done(self, call, outs):
        outs = list(outs)
        for kind, what, *_ in self.RIDES.get(call, []):
            names = what if kind == "gather" else self.GROUPS[what]
            mine, outs = outs[:len(names)], outs[len(names):]
            if kind == "gather":
                self.full.update(zip(names, mine))
            elif kind == "send":
                for k, r in zip(names, mine):
                    self.got3.setdefault(k, []).append(r)
            else:
                for k, r in zip(names, mine):
                    self.got[k] = r
                    self.sums[k] = _chip_sum(f"chip_sum_{k}", self.slots[k], r, self.c1)

    def alone(self, call):
        self.done(call, _run_carry(call, self.carry(call)))

    def behind(self, group, x):
        return lax.optimization_barrier((x, tuple(self.sums[k] for k in self.GROUPS[group])))[0]

    def launch(self, call, after=None):
        carry = self._one(*self.RIDES[call][0])
        if after is not None:
            carry.arrays = list(lax.optimization_barrier((tuple(carry.arrays), after))[0])
        self.done(call, _run_carry_async(call, carry, self.ASYNC[call]))


def kernel(x, g_mix_norm, w_in, b_in, sinks, conv_w, conv_b, ln_g, ln_b, w_attn_proj, w_conv_proj, b_conv_proj, w_out, g_ffn_norm, w_ffn_in, w_ffn_down, g_final, loss_target, m_g_mix_norm, m_w_in, m_b_in, m_sinks, m_conv_w, m_conv_b, m_ln_g, m_ln_b, m_w_attn_proj, m_w_conv_proj, m_b_conv_proj, m_w_out, m_g_ffn_norm, m_w_ffn_in, m_w_ffn_down, m_g_final, v_g_mix_norm, v_w_in, v_b_in, v_sinks, v_conv_w, v_conv_b, v_ln_g, v_ln_b, v_w_attn_proj, v_w_conv_proj, v_b_conv_proj, v_w_out, v_g_ffn_norm, v_w_ffn_in, v_w_ffn_down, v_g_final):
    w = dict(g_mix_norm=g_mix_norm, w_in=w_in, b_in=b_in, sinks=sinks, conv_w=conv_w, conv_b=conv_b, ln_g=ln_g,
             ln_b=ln_b, w_attn_proj=w_attn_proj, w_conv_proj=w_conv_proj, b_conv_proj=b_conv_proj, w_out=w_out,
             g_ffn_norm=g_ffn_norm, w_ffn_in=w_ffn_in, w_ffn_down=w_ffn_down, g_final=g_final)
    m = dict(g_mix_norm=m_g_mix_norm, w_in=m_w_in, b_in=m_b_in, sinks=m_sinks, conv_w=m_conv_w, conv_b=m_conv_b,
             ln_g=m_ln_g, ln_b=m_ln_b, w_attn_proj=m_w_attn_proj, w_conv_proj=m_w_conv_proj,
             b_conv_proj=m_b_conv_proj, w_out=m_w_out, g_ffn_norm=m_g_ffn_norm, w_ffn_in=m_w_ffn_in,
             w_ffn_down=m_w_ffn_down, g_final=m_g_final)
    v = dict(g_mix_norm=v_g_mix_norm, w_in=v_w_in, b_in=v_b_in, sinks=v_sinks, conv_w=v_conv_w, conv_b=v_conv_b,
             ln_g=v_ln_g, ln_b=v_ln_b, w_attn_proj=v_w_attn_proj, w_conv_proj=v_w_conv_proj,
             b_conv_proj=v_b_conv_proj, w_out=v_w_out, g_ffn_norm=v_g_ffn_norm, w_ffn_in=v_w_ffn_in,
             w_ffn_down=v_w_ffn_down, g_final=v_g_final)
    ax, ay, ac = lax.axis_index("x"), lax.axis_index("y"), lax.axis_index("c")
    me = 4 * ax + 2 * ay + ac
    chip = 2 * ax + ay

    shards = {k: (w[k][0].T if tr else w[k][0]).astype(BF) for k, tr in BIG.items()}
    cw_shard = jnp.pad(conv_w[0].T, ((0, 0), (0, 1))).reshape(16, 128)
    wi_t, cw_full = _first_gather([shards["w_in"], cw_shard])
    conv_full = cw_full.reshape(CONV_CHANNELS, CONV_PAD).T

    as_row = lambda a: a.reshape(1, -1)
    small_w = {k: as_row(w[k]) for k in SMALL_NAMES}
    small_m = {k: as_row(m[k]) for k in SMALL_NAMES}
    small_v = {k: as_row(v[k]) for k in SMALL_NAMES}
    plan = _Plan(shards, ac.reshape(1).astype(jnp.int32))
    plan.launch("gather_mix", after=wi_t)
    dx, parts = _local_step(x[0], loss_target[0], small_w, wi_t, conv_full, plan)

    ids = jnp.stack([me, chip]).astype(jnp.int32)
    grads, delta, new_m, new_v, after = {}, {}, {}, {}, dx
    packed = _small_pack(parts)
    for k in sorted(BIG, key=lambda k: k == "w_in"):
        if k == "w_in":
            packed = lax.optimization_barrier((packed, after))[0]
            small_gathered, = _run_carry_async("small_gather", _gather_carry([packed]), 8)
        flip = (lambda a: a.T) if BIG[k] else (lambda a: a)
        wk = lax.optimization_barrier((w[k][0], after))[0]
        outs = _grad_adamw(f"grad_adamw_{k}", plan.slots[k], plan.got[k], plan.got3[k], ids,
                           flip(wk), flip(m[k][0]), flip(v[k][0]))
        after = outs[0]
        grads[k], delta[k], new_m[k], new_v[k] = (flip(a)[None] for a in outs)

    loss, cw_grad, small_out = _small_adamw(small_gathered, small_w, small_m, small_v)
    for k in SMALL_NAMES:
        g, d, nm, nv = (a.reshape(w[k].shape) for a in small_out[k])
        grads[k], delta[k], new_m[k], new_v[k] = g, d, nm, nv
    cw_mine = lax.dynamic_slice(cw_grad, (0, me * 64), (CONV_WIDTH, 64))
    d, nm, nv = _adamw("adamw_conv_w", conv_w[0], cw_mine, m_conv_w[0], v_conv_w[0])
    grads["conv_w"], delta["conv_w"], new_m["conv_w"], new_v["conv_w"] = cw_mine[None], d[None], nm[None], nv[None]

    return (loss.reshape(()), dx[None], *[grads[k] for k in WEIGHT_NAMES], *[delta[k] for k in WEIGHT_NAMES],
            *[new_m[k] for k in WEIGHT_NAMES], *[new_v[k] for k in WEIGHT_NAMES])
```

```python
import functools

import jax
import jax.numpy as jnp
from jax import lax
from jax.experimental import pallas as pl
from jax.experimental.pallas import tpu as pltpu
from jax.experimental.pallas import tpu_sc as plsc

F32 = jnp.float32
BF = jnp.bfloat16

SEQ = 2048
D_MODEL = 1024
HEAD_DIM = 64
N_Q_HEADS = 8
N_KV_HEADS = 2
GROUP = N_Q_HEADS // N_KV_HEADS
BLOCK = 128
ATTN_WIDTH = 512
KV_WIDTH = 128
CONV_CHANNELS = 512
CONV_WIDTH = 31
CONV_PAD = 32
GLU_OFF = 768
GATE_OFF = 1792
IN_WIDTH = 3840
D_FF = 2816
EPS = 1e-5
NEG = -1e30
N_DEV = 8

ADAM_LR = 0.001
ADAM_B1 = 0.9
ADAM_B2 = 0.999
ADAM_EPS = 1e-08
ADAM_WD = 0.01
ADAM_STEP = 10

VMEM_LIMIT_BYTES = 56 * 1024 * 1024
MESH = pl.DeviceIdType.MESH
ANY = pl.BlockSpec(memory_space=pl.ANY)

_DIMS = {"NN": (((1,), (0,)), ((), ())), "NT": (((1,), (1,)), ((), ())), "TN": (((0,), (0,)), ((), ()))}


def _params(sem):
    return pltpu.CompilerParams(dimension_semantics=sem, vmem_limit_bytes=VMEM_LIMIT_BYTES)


class _Carry:
    def __init__(self, arrays, out_shapes, sems, start, finish, peers=None):
        self.arrays, self.out_shapes, self.sems, self.start, self.finish = arrays, out_shapes, sems, start, finish
        self.peers = peers


def _carry_io(carry):
    if carry is None:
        return [], [], []
    return list(carry.arrays), list(carry.out_shapes), list(carry.sems)


def _matmul(name, a_list, b, mode, *, m, n, tm, tn, tk=None, epilogue, extra=(), outs, b_off=(0, 0), alias=None,
            scratch=(), carry=None):
    seg_k = [a.shape[0] if mode == "TN" else a.shape[1] for a in a_list]
    whole = tk is None
    seg_nk = [1] * len(a_list) if whole else [ks // tk for ks in seg_k]
    nk = 1 if whole else sum(seg_nk)
    starts = [sum(seg_nk[:s]) for s in range(len(seg_nk))]
    k_starts = [sum(seg_k[:s]) for s in range(len(seg_k))]
    k_tot = sum(seg_k)
    n_a, n_extra, n_out = len(a_list), len(extra), len(outs)

    a_specs = []
    for st, ns, ks in zip(starts, seg_nk, seg_k):
        if mode == "TN":
            a_specs.append(pl.BlockSpec((ks if whole else tk, tm), lambda j, i, k: (k, i)))
        elif whole:
            a_specs.append(pl.BlockSpec((tm, ks), lambda j, i, k: (i, 0)))
        else:
            a_specs.append(pl.BlockSpec((tm, tk), functools.partial(
                lambda j, i, k, st, ns: (i, jnp.clip(k - st, 0, ns - 1)), st=st, ns=ns)))
    bk = k_tot if whole else tk
    if mode == "NT":
        b_spec = pl.BlockSpec((tn, bk), lambda j, i, k: (b_off[0] + j, b_off[1] + k))
    else:
        b_spec = pl.BlockSpec((bk, tn), lambda j, i, k: (b_off[0] + k, b_off[1] + j))
    n_alias = 0 if alias is None else 1
    c_in, c_out, c_sems = _carry_io(carry)
    n_acc = 0 if whole else 1
    nj, ni = n // tn, m // tm

    def body(*refs):
        pos = [n_a, 1, n_alias, n_extra, len(c_in), n_out, len(c_out), n_acc, len(scratch), len(c_sems)]
        cuts = [sum(pos[:q]) for q in range(len(pos) + 1)]
        a_refs, (b_ref,), _, ex, ci_refs, out_refs, co_refs, acc_refs, scr, cs_refs = (
            refs[cuts[q]:cuts[q + 1]] for q in range(len(pos)))
        j, i, k = pl.program_id(0), pl.program_id(1), pl.program_id(2)
        ids = (j, i)
        if carry is not None:
            @pl.when((j == 0) & (i == 0) & (k == 0))
            def _():
                carry.start(ci_refs, co_refs, cs_refs)

        def dot(a_ref, bv):
            return lax.dot_general(a_ref[...].astype(BF), bv.astype(BF), _DIMS[mode], preferred_element_type=F32)

        if whole:
            tot = None
            for a_ref, k0, ks in zip(a_refs, k_starts, seg_k):
                if n_a == 1:
                    bv = b_ref[...]
                else:
                    bv = b_ref[:, k0:k0 + ks] if mode == "NT" else b_ref[k0:k0 + ks, :]
                part = dot(a_ref, bv)
                tot = part if tot is None else tot + part
            epilogue(tot, ex, out_refs, ids, scr)
        else:
            acc, = acc_refs

            @pl.when(k == 0)
            def _():
                acc[...] = jnp.zeros_like(acc)

            for a_ref, st, ns in zip(a_refs, starts, seg_nk):
                if n_a == 1:
                    acc[...] += dot(a_ref, b_ref[...])
                else:
                    @pl.when((k >= st) & (k < st + ns))
                    def _(a_ref=a_ref):
                        acc[...] += dot(a_ref, b_ref[...])

            @pl.when(k == nk - 1)
            def _():
                epilogue(acc[...], ex, out_refs, ids, scr)

        if carry is not None:
            @pl.when((j == nj - 1) & (i == ni - 1) & (k == nk - 1))
            def _():
                carry.finish(ci_refs, co_refs, cs_refs)

    in_specs = [*a_specs, b_spec]
    args = [*a_list, b]
    io_alias = {}
    if alias is not None:
        in_specs.append(pl.BlockSpec(memory_space=pl.ANY))
        args.append(alias[0])
        io_alias = {n_a + 1: alias[1]}
    in_specs += [s for _, s in extra] + [pl.BlockSpec(memory_space=pl.ANY)] * len(c_in)
    args += [x for x, _ in extra] + c_in
    res = pl.pallas_call(
        body, name=name, grid=(nj, ni, nk), in_specs=in_specs,
        out_specs=[s for _, s in outs] + [pl.BlockSpec(memory_space=pl.ANY)] * len(c_out),
        out_shape=[o for o, _ in outs] + c_out,
        scratch_shapes=[*([] if whole else [pltpu.VMEM((tm, tn), F32)]), *scratch, *c_sems],
        input_output_aliases=io_alias,
        compiler_params=_params(("arbitrary", "arbitrary", "arbitrary")),
    )(*args)
    return res if carry is None else (res[:n_out], res[n_out:])


def _tile(tm, tn):
    return pl.BlockSpec((tm, tn), lambda j, i, k: (i, j))


def _row(tn):
    return pl.BlockSpec((1, tn), lambda j, i, k: (0, j))


def _store(dtype):
    def ep(acc, ex, outs, ids, scr):
        outs[0][...] = acc.astype(dtype)
    return ep


def _sds(shape, dtype):
    return jax.ShapeDtypeStruct(shape, dtype)


def _rms_fwd(name, x, g):
    T, D = x.shape
    tm = 512

    def body(x_ref, g_ref, h_ref, r_ref):
        xv = x_ref[...]
        r = lax.rsqrt(jnp.mean(xv * xv, axis=-1, keepdims=True) + EPS)
        h_ref[...] = (xv * r * g_ref[...]).astype(BF)
        r_ref[...] = r

    return pl.pallas_call(
        body, name=name, grid=(T // tm,),
        in_specs=[pl.BlockSpec((tm, D), lambda i: (i, 0)), pl.BlockSpec((1, D), lambda i: (0, 0))],
        out_specs=[pl.BlockSpec((tm, D), lambda i: (i, 0)), pl.BlockSpec((tm, 1), lambda i: (i, 0))],
        out_shape=[_sds((T, D), BF), _sds((T, 1), F32)],
        compiler_params=_params(("arbitrary",)),
    )(x, g)


def _rms_bwd(dh, xv, r, g):
    xh = xv * r
    dxh = dh * g
    dx = r * (dxh - xh * jnp.mean(dxh * xh, axis=-1, keepdims=True))
    return dx, jnp.sum(dh * xh, axis=0, keepdims=True)


def _accumulate_rows(ref, val, first):
    @pl.when(first)
    def _():
        ref[...] = val

    @pl.when(jnp.logical_not(first))
    def _():
        ref[...] += val


def _loss_head(xv, g, target):
    r = lax.rsqrt(jnp.mean(xv * xv, axis=-1, keepdims=True) + EPS)
    err = xv * r * g - target
    dx, dg = _rms_bwd(err * (1.0 / xv.shape[-1]), xv, r, g)
    part = 0.5 * jnp.sum(jnp.mean(err * err, axis=-1, keepdims=True), axis=0, keepdims=True)
    return dx, dg, part


def _lane_half(shape, h):
    lane = lax.broadcasted_iota(jnp.int32, shape, 1)
    return (lane >= HEAD_DIM * h) & (lane < HEAD_DIM * (h + 1))


def _to_half(v, w, h):
    if w != h:
        v = pltpu.roll(v, HEAD_DIM, 1)
    return jnp.where(_lane_half(v.shape, h), v, 0.0)


def _attn_block(qkv_ref, sinks_ref, n, h):
    r0 = pl.multiple_of(n * BLOCK, BLOCK)
    p0 = pl.multiple_of(jnp.maximum(n - 1, 0) * BLOCK, BLOCK)
    rows = pl.ds(r0, BLOCK)
    prev = pl.ds(p0, BLOCK)
    k2 = jnp.concatenate([qkv_ref[prev, ATTN_WIDTH:ATTN_WIDTH + KV_WIDTH],
                          qkv_ref[rows, ATTN_WIDTH:ATTN_WIDTH + KV_WIDTH]], axis=0)
    v2 = jnp.concatenate([qkv_ref[prev, ATTN_WIDTH + KV_WIDTH:ATTN_WIDTH + 2 * KV_WIDTH],
                          qkv_ref[rows, ATTN_WIDTH + KV_WIDTH:ATTN_WIDTH + 2 * KV_WIDTH]], axis=0)
    qs = []
    for g in range(GROUP):
        hq = GROUP * h + g
        blk = qkv_ref[rows, (hq // 2) * 128:(hq // 2 + 1) * 128].astype(F32)
        qs.append(_to_half(blk, hq % 2, h))
    q4 = jnp.concatenate(qs, axis=0).astype(BF)
    s = lax.dot_general(q4, k2, _DIMS["NT"], preferred_element_type=F32) * (HEAD_DIM ** -0.5)
    shape = s.shape
    row = lax.broadcasted_iota(jnp.int32, shape, 0)
    qi = row & (BLOCK - 1)
    kj = lax.broadcasted_iota(jnp.int32, shape, 1)
    diff = qi + BLOCK - kj
    valid = (diff >= 0) & (diff < BLOCK) & ((kj >= BLOCK) | (n > 0))
    s = jnp.where(valid, s, NEG)
    row1 = lax.broadcasted_iota(jnp.int32, (shape[0], 1), 0)
    sink = jnp.zeros((shape[0], 1), F32)
    for g in range(GROUP):
        sink = jnp.where((row1 >= g * BLOCK) & (row1 < (g + 1) * BLOCK), sinks_ref[0, GROUP * h + g], sink)
    m = jnp.maximum(jnp.max(s, axis=-1, keepdims=True), sink)
    e = jnp.exp(s - m)
    es = jnp.exp(sink - m)
    inv = 1.0 / (jnp.sum(e, axis=-1, keepdims=True) + es)
    return e * inv, es * inv, q4, k2, v2, rows, prev


def _attn_fwd(proj, sinks, carry=None):
    T = proj.shape[0]
    c_in, c_out, c_sems = _carry_io(carry)

    def body(*refs):
        qkv_ref, sinks_ref = refs[:2]
        ci_refs = refs[2:2 + len(c_in)]
        o_ref = refs[2 + len(c_in)]
        co_refs = refs[3 + len(c_in):3 + len(c_in) + len(c_out)]
        cs_refs = refs[3 + len(c_in) + len(c_out):]
        if carry is not None:
            carry.start(ci_refs, co_refs, cs_refs)

        def blk(n, z):
            outs = [None] * (N_Q_HEADS // 2)
            for h in range(N_KV_HEADS):
                p, _, _, _, v2, rows, _ = _attn_block(qkv_ref, sinks_ref, n, h)
                o = lax.dot_general(p.astype(BF), v2, _DIMS["NN"], preferred_element_type=F32)
                for g in range(GROUP):
                    hq = GROUP * h + g
                    piece = jnp.where(_lane_half((BLOCK, 128), h), o[g * BLOCK:(g + 1) * BLOCK], 0.0)
                    if hq % 2 != h:
                        piece = pltpu.roll(piece, HEAD_DIM, 1)
                    outs[hq // 2] = piece if outs[hq // 2] is None else outs[hq // 2] + piece
            for pb in range(N_Q_HEADS // 2):
                o_ref[rows, pb * 128:(pb + 1) * 128] = outs[pb].astype(BF)
            return z

        lax.fori_loop(0, T // BLOCK, blk, 0)
        if carry is not None:
            carry.finish(ci_refs, co_refs, cs_refs)

    res = pl.pallas_call(
        body, name="attn_fwd", grid=(1,),
        in_specs=[pl.BlockSpec((T, GLU_OFF), lambda i: (0, 0)), pl.BlockSpec(memory_space=pltpu.SMEM),
                  *[ANY] * len(c_in)],
        out_specs=[pl.BlockSpec((T, ATTN_WIDTH), lambda i: (0, 0)), *[ANY] * len(c_out)],
        out_shape=[_sds((T, ATTN_WIDTH), BF), *c_out], scratch_shapes=c_sems,
        compiler_params=_params(("arbitrary",)),
    )(proj, sinks, *c_in)
    return res[0], res[1:]


def _attn_bwd(proj, d_o, sinks, carry=None):
    T = proj.shape[0]
    c_in, c_out, c_sems = _carry_io(carry)

    def body(*refs):
        qkv_ref, do_ref, sinks_ref = refs[:3]
        ci_refs = refs[3:3 + len(c_in)]
        dqkv_ref, dsink_ref = refs[3 + len(c_in):5 + len(c_in)]
        co_refs = refs[5 + len(c_in):5 + len(c_in) + len(c_out)]
        dk_acc, dv_acc = refs[5 + len(c_in) + len(c_out):7 + len(c_in) + len(c_out)]
        cs_refs = refs[7 + len(c_in) + len(c_out):]
        if carry is not None:
            carry.start(ci_refs, co_refs, cs_refs)
        dsink_ref[...] = jnp.zeros_like(dsink_ref)
        dk_acc[...] = jnp.zeros_like(dk_acc)
        dv_acc[...] = jnp.zeros_like(dv_acc)

        def blk(n, carry):
            dqs = [None] * (N_Q_HEADS // 2)
            for h in range(N_KV_HEADS):
                p, psink, q4, k2, v2, rows, prev = _attn_block(qkv_ref, sinks_ref, n, h)
                dos = []
                for g in range(GROUP):
                    hq = GROUP * h + g
                    dos.append(_to_half(do_ref[rows, (hq // 2) * 128:(hq // 2 + 1) * 128].astype(F32), hq % 2, h))
                do4 = jnp.concatenate(dos, axis=0).astype(BF)
                dp = lax.dot_general(do4, v2, _DIMS["NT"], preferred_element_type=F32)
                delta = jnp.sum(p * dp, axis=-1, keepdims=True)
                ds = (p * (dp - delta) * (HEAD_DIM ** -0.5)).astype(BF)
                dsk = psink * delta
                for g in range(GROUP):
                    hq = GROUP * h + g
                    tot = -jnp.sum(dsk[g * BLOCK:(g + 1) * BLOCK], axis=0, keepdims=True)
                    lane = lax.broadcasted_iota(jnp.int32, (1, 128), 1)
                    dsink_ref[...] += jnp.where(lane == hq, tot, 0.0)
                dq = lax.dot_general(ds, k2, _DIMS["NN"], preferred_element_type=F32)
                dk = lax.dot_general(ds, q4, _DIMS["TN"], preferred_element_type=F32)
                dv = lax.dot_general(p.astype(BF), do4, _DIMS["TN"], preferred_element_type=F32)
                dk_acc[prev, :] += dk[:BLOCK]
                dk_acc[rows, :] += dk[BLOCK:]
                dv_acc[prev, :] += dv[:BLOCK]
                dv_acc[rows, :] += dv[BLOCK:]
                for g in range(GROUP):
                    hq = GROUP * h + g
                    piece = jnp.where(_lane_half((BLOCK, 128), h), dq[g * BLOCK:(g + 1) * BLOCK], 0.0)
                    if hq % 2 != h:
                        piece = pltpu.roll(piece, HEAD_DIM, 1)
                    dqs[hq // 2] = piece if dqs[hq // 2] is None else dqs[hq // 2] + piece
            for pb in range(N_Q_HEADS // 2):
                dqkv_ref[rows, pb * 128:(pb + 1) * 128] = dqs[pb].astype(BF)
            return carry

        lax.fori_loop(0, T // BLOCK, blk, 0)
        dqkv_ref[:, ATTN_WIDTH:ATTN_WIDTH + KV_WIDTH] = dk_acc[...].astype(BF)
        dqkv_ref[:, ATTN_WIDTH + KV_WIDTH:] = dv_acc[...].astype(BF)
        if carry is not None:
            carry.finish(ci_refs, co_refs, cs_refs)

    res = pl.pallas_call(
        body, name="attn_bwd", grid=(1,),
        in_specs=[pl.BlockSpec((T, GLU_OFF), lambda i: (0, 0)), pl.BlockSpec((T, ATTN_WIDTH), lambda i: (0, 0)),
                  pl.BlockSpec(memory_space=pltpu.SMEM), *[ANY] * len(c_in)],
        out_specs=[pl.BlockSpec((T, GLU_OFF), lambda i: (0, 0)), pl.BlockSpec((1, 128), lambda i: (0, 0)),
                   *[ANY] * len(c_out)],
        out_shape=[_sds((T, GLU_OFF), BF), _sds((1, 128), F32), *c_out],
        scratch_shapes=[pltpu.VMEM((T, KV_WIDTH), F32), pltpu.VMEM((T, KV_WIDTH), F32), *c_sems],
        compiler_params=_params(("arbitrary",)),
    )(proj, d_o, sinks, *c_in)
    return res[:2], res[2:]


CHUNK = 256
SUB = 32
WIN = CHUNK + 32
PAD_ROWS = SEQ + 2 * CONV_PAD
_GLU_SPECS = [pl.BlockSpec((SEQ, 256), functools.partial(lambda i, c: (0, c), c=GLU_OFF // 256 + c)) for c in range(4)]


def _glu_to_pad(a0, a1, b0, b1, zpad):
    C = CONV_CHANNELS
    zpad[0:CONV_PAD, :] = jnp.zeros((CONV_PAD, C), F32)
    zpad[CONV_PAD + SEQ:, :] = jnp.zeros((CONV_PAD, C), F32)
    zpad[CONV_PAD:CONV_PAD + SEQ, 0:256] = a0[...].astype(F32) * jax.nn.sigmoid(b0[...].astype(F32))
    zpad[CONV_PAD:CONV_PAD + SEQ, 256:C] = a1[...].astype(F32) * jax.nn.sigmoid(b1[...].astype(F32))


def _tap_windows(src, base, win):
    for b in range(8):
        win[b, 0:WIN - 8, :] = src[base + b:base + b + WIN - 8, :]


def _taps(win, w_ref, init, out, flip):
    def sub(si, carry):
        r0 = pl.multiple_of(si * SUB, SUB)
        acc = jnp.broadcast_to(init, (SUB, CONV_CHANNELS))
        for k in range(CONV_WIDTH):
            wk = (CONV_WIDTH - 1 - k) if flip else k
            acc = acc + w_ref[wk:wk + 1, :] * win[k % 8, pl.ds(r0 + 8 * (k // 8), SUB), :]
        out[pl.ds(r0, SUB), :] = acc
        return carry

    lax.fori_loop(0, CHUNK // SUB, sub, 0)


def _tap_grads(win, du, dwacc):
    def sub(si, carry):
        r0 = pl.multiple_of(si * SUB, SUB)
        d = du[pl.ds(r0, SUB), :]
        for k in range(CONV_WIDTH):
            p = d * win[k % 8, pl.ds(r0 + 8 * (k // 8), SUB), :]
            dwacc[8 * k:8 * k + 8, :] += (p[0:8] + p[8:16]) + (p[16:24] + p[24:32])
        return carry

    lax.fori_loop(0, CHUNK // SUB, sub, 0)


def _ln_parts(u):
    mu = jnp.mean(u, axis=-1, keepdims=True)
    xc = u - mu
    rstd = lax.rsqrt(jnp.mean(xc * xc, axis=-1, keepdims=True) + EPS)
    return xc * rstd, rstd


def _conv_fwd(proj, conv_w, conv_b, ln_g, ln_b, carry=None):
    T, C = proj.shape[0], CONV_CHANNELS
    vec = pl.BlockSpec((1, C), lambda i: (0, 0))
    c_in, c_out, c_sems = _carry_io(carry)

    def body(*refs):
        a0, a1, b0, b1, w_ref, cb_ref, g_ref, be_ref = refs[:8]
        ci_refs = refs[8:8 + len(c_in)]
        c_ref = refs[8 + len(c_in)]
        co_refs = refs[9 + len(c_in):9 + len(c_in) + len(c_out)]
        zpad, win, ubuf = refs[9 + len(c_in) + len(c_out):12 + len(c_in) + len(c_out)]
        cs_refs = refs[12 + len(c_in) + len(c_out):]
        if carry is not None:
            carry.start(ci_refs, co_refs, cs_refs)
        _glu_to_pad(a0, a1, b0, b1, zpad)
        for ci in range(T // CHUNK):
            _tap_windows(zpad, ci * CHUNK + CONV_PAD - (CONV_WIDTH - 1), win)
            _taps(win, w_ref, cb_ref[...], ubuf, False)
            xh, _ = _ln_parts(ubuf[...])
            ln = xh * g_ref[...] + be_ref[...]
            c_ref[ci * CHUNK:(ci + 1) * CHUNK, :] = (ln * jax.nn.sigmoid(ln)).astype(BF)
        if carry is not None:
            carry.finish(ci_refs, co_refs, cs_refs)

    res = pl.pallas_call(
        body, name="conv_fwd", grid=(1,),
        in_specs=[*_GLU_SPECS, pl.BlockSpec((CONV_PAD, C), lambda i: (0, 0)), vec, vec, vec, *[ANY] * len(c_in)],
        out_specs=[pl.BlockSpec((T, C), lambda i: (0, 0)), *[ANY] * len(c_out)],
        out_shape=[_sds((T, C), BF), *c_out],
        scratch_shapes=[pltpu.VMEM((PAD_ROWS, C), F32), pltpu.VMEM((8, WIN, C), F32), pltpu.VMEM((CHUNK, C), F32),
                        *c_sems],
        compiler_params=_params(("arbitrary",)),
    )(proj, proj, proj, proj, conv_w, conv_b, ln_g, ln_b, *c_in)
    return res[0], res[1:]


def _conv_bwd(proj, d_c, conv_w, conv_b, ln_g, ln_b, carry=None):
    T, C = proj.shape[0], CONV_CHANNELS
    vec = pl.BlockSpec((1, C), lambda i: (0, 0))
    wspec = pl.BlockSpec((CONV_PAD, C), lambda i: (0, 0))
    c_in, c_out, c_sems = _carry_io(carry)

    def body(*refs):
        a0, a1, b0, b1, dc_ref, w_ref, cb_ref, g_ref, be_ref = refs[:9]
        ci_refs = refs[9:9 + len(c_in)]
        o = 9 + len(c_in)
        dglu_ref, dw_ref, dcb_ref, dg_ref, dbe_ref = refs[o:o + 5]
        co_refs = refs[o + 5:o + 5 + len(c_out)]
        zpad, dupad, win, ubuf, dwacc = refs[o + 5 + len(c_out):o + 10 + len(c_out)]
        cs_refs = refs[o + 10 + len(c_out):]
        if carry is not None:
            carry.start(ci_refs, co_refs, cs_refs)
        _glu_to_pad(a0, a1, b0, b1, zpad)
        dupad[T:, :] = jnp.zeros((2 * CONV_PAD, C), F32)
        dwacc[...] = jnp.zeros_like(dwacc)
        dcb_ref[...] = jnp.zeros_like(dcb_ref)
        dg_ref[...] = jnp.zeros_like(dg_ref)
        dbe_ref[...] = jnp.zeros_like(dbe_ref)
        for ci in range(T // CHUNK):
            rows = slice(ci * CHUNK, (ci + 1) * CHUNK)
            _tap_windows(zpad, ci * CHUNK + CONV_PAD - (CONV_WIDTH - 1), win)
            _taps(win, w_ref, cb_ref[...], ubuf, False)
            xh, rstd = _ln_parts(ubuf[...])
            ln = xh * g_ref[...] + be_ref[...]
            sg = jax.nn.sigmoid(ln)
            dln = dc_ref[rows, :].astype(F32) * (sg * (1.0 + ln * (1.0 - sg)))
            dg_ref[...] += jnp.sum(dln * xh, axis=0, keepdims=True)
            dbe_ref[...] += jnp.sum(dln, axis=0, keepdims=True)
            dxh = dln * g_ref[...]
            du = rstd * (dxh - jnp.mean(dxh, axis=-1, keepdims=True)
                         - xh * jnp.mean(dxh * xh, axis=-1, keepdims=True))
            dupad[rows, :] = du
            dcb_ref[...] += jnp.sum(du, axis=0, keepdims=True)
            _tap_grads(win, dupad.at[rows, :], dwacc)
        for k in range(CONV_WIDTH):
            dw_ref[k:k + 1, :] = jnp.sum(dwacc[8 * k:8 * k + 8, :], axis=0, keepdims=True)
        dw_ref[CONV_WIDTH:, :] = jnp.zeros((CONV_PAD - CONV_WIDTH, C), F32)
        for ci in range(T // CHUNK):
            rows = slice(ci * CHUNK, (ci + 1) * CHUNK)
            _tap_windows(dupad, ci * CHUNK, win)
            _taps(win, w_ref, jnp.zeros((1, C), F32), ubuf, True)
            dz = ubuf[...]
            for half, (a, b) in enumerate(((a0, b0), (a1, b1))):
                sb = jax.nn.sigmoid(b[rows, :].astype(F32))
                dzh = dz[:, half * 256:(half + 1) * 256]
                dglu_ref[rows, half * 256:(half + 1) * 256] = (dzh * sb).astype(BF)
                dglu_ref[rows, C + half * 256:C + (half + 1) * 256] = (
                    dzh * a[rows, :].astype(F32) * sb * (1.0 - sb)).astype(BF)
        if carry is not None:
            carry.finish(ci_refs, co_refs, cs_refs)

    res = pl.pallas_call(
        body, name="conv_bwd", grid=(1,),
        in_specs=[*_GLU_SPECS, pl.BlockSpec((T, C), lambda i: (0, 0)), wspec, vec, vec, vec, *[ANY] * len(c_in)],
        out_specs=[pl.BlockSpec((T, 2 * C), lambda i: (0, 0)), wspec, vec, vec, vec, *[ANY] * len(c_out)],
        out_shape=[_sds((T, 2 * C), BF), _sds((CONV_PAD, C), F32), _sds((1, C), F32), _sds((1, C), F32),
                   _sds((1, C), F32), *c_out],
        scratch_shapes=[pltpu.VMEM((PAD_ROWS, C), F32), pltpu.VMEM((PAD_ROWS, C), F32), pltpu.VMEM((8, WIN, C), F32),
                        pltpu.VMEM((CHUNK, C), F32), pltpu.VMEM((8 * CONV_PAD, C), F32), *c_sems],
        compiler_params=_params(("arbitrary",)),
    )(proj, proj, proj, proj, d_c, conv_w, conv_b, ln_g, ln_b, *c_in)
    return res[:5], res[5:]


_GATE_BLK = GATE_OFF // 256


def _ffn_in_swiglu(h2, wf_t, carry=None):
    T, D = h2.shape
    tm, tn = 1024, D_FF // 2
    nj, ni = D_FF // tn, T // tm
    c_in, c_out, c_sems = _carry_io(carry)

    def body(*refs):
        a_ref, bg_ref, bu_ref = refs[:3]
        ci_refs = refs[3:3 + len(c_in)]
        act_ref, g_ref, u_ref = refs[3 + len(c_in):6 + len(c_in)]
        co_refs = refs[6 + len(c_in):6 + len(c_in) + len(c_out)]
        cs_refs = refs[6 + len(c_in) + len(c_out):]
        j, i = pl.program_id(0), pl.program_id(1)
        if carry is not None:
            @pl.when((j == 0) & (i == 0))
            def _():
                carry.start(ci_refs, co_refs, cs_refs)
        a = a_ref[...]
        for c0, c1 in ((0, 768), (768, tn)):
            g = lax.dot_general(a, bg_ref[c0:c1, :], _DIMS["NT"], preferred_element_type=F32)
            u = lax.dot_general(a, bu_ref[c0:c1, :], _DIMS["NT"], preferred_element_type=F32)
            act_ref[:, c0:c1] = (g * jax.nn.sigmoid(g) * u).astype(BF)
            g_ref[:, c0:c1] = g.astype(BF)
            u_ref[:, c0:c1] = u.astype(BF)
        if carry is not None:
            @pl.when((j == nj - 1) & (i == ni - 1))
            def _():
                carry.finish(ci_refs, co_refs, cs_refs)

    t = pl.BlockSpec((tm, tn), lambda j, i: (i, j))
    res = pl.pallas_call(
        body, name="ffn_in_swiglu", grid=(nj, ni),
        in_specs=[pl.BlockSpec((tm, D), lambda j, i: (i, 0)), pl.BlockSpec((tn, D), lambda j, i: (j, 0)),
                  pl.BlockSpec((tn, D), lambda j, i: (nj + j, 0)), *[ANY] * len(c_in)],
        out_specs=[t, t, t, *[ANY] * len(c_out)], out_shape=[*[_sds((T, D_FF), BF)] * 3, *c_out],
        scratch_shapes=c_sems,
        compiler_params=_params(("arbitrary", "arbitrary")),
    )(h2, wf_t, wf_t, *c_in)
    return res[:3], res[3:]


def _proj_merge(o, c, wap_t, wcp_t, b_cp, proj):
    T, D = o.shape[0], wap_t.shape[0]
    tm, tg = 1024, 256
    nj = D // tg

    def body(o_ref, c_ref, wa_ref, wc_ref, b_ref, g0_ref, g1_ref, ya_ref, yc_ref, m_ref):
        ya = lax.dot_general(o_ref[...], wa_ref[...], _DIMS["NT"], preferred_element_type=F32)
        yc = lax.dot_general(c_ref[...], wc_ref[...], _DIMS["NT"], preferred_element_type=F32) + b_ref[...]
        ya_ref[...] = ya.astype(BF)
        yc_ref[...] = yc.astype(BF)
        m_ref[...] = (jax.nn.sigmoid(g0_ref[...].astype(F32)) * ya + jax.nn.sigmoid(g1_ref[...].astype(F32)) * yc).astype(BF)

    act = pl.BlockSpec((tm, o.shape[1]), lambda j, i: (i, 0))
    wgt = pl.BlockSpec((tg, o.shape[1]), lambda j, i: (j, 0))
    t = pl.BlockSpec((tm, tg), lambda j, i: (i, j))
    return pl.pallas_call(
        body, name="proj_merge", grid=(nj, T // tm),
        in_specs=[act, act, wgt, wgt, pl.BlockSpec((1, tg), lambda j, i: (0, j)),
                  pl.BlockSpec((tm, tg), lambda j, i: (i, _GATE_BLK + j)),
                  pl.BlockSpec((tm, tg), lambda j, i: (i, _GATE_BLK + nj + j))],
        out_specs=[t, t, t], out_shape=[_sds((T, D), BF)] * 3,
        compiler_params=_params(("arbitrary", "arbitrary")),
    )(o, c, wap_t, wcp_t, b_cp, proj, proj)


def _proj_in_dw(segs, h):
    T, D = h.shape
    tb = 256
    nblk = [seg.shape[1] // tb for seg in segs]
    starts = [sum(nblk[:q]) for q in range(len(segs))]
    n_seg = len(segs)

    def body(*refs):
        seg_refs, h_ref, o_ref, cs_ref = refs[:n_seg], refs[n_seg], refs[n_seg + 1], refs[n_seg + 2]
        i = pl.program_id(0)
        for seg_ref, st, nb in zip(seg_refs, starts, nblk):
            @pl.when((i >= st) & (i < st + nb))
            def _(seg_ref=seg_ref):
                a = seg_ref[...]
                o_ref[...] = lax.dot_general(a, h_ref[...], _DIMS["TN"], preferred_element_type=F32).astype(BF)
                cs_ref[...] = jnp.sum(a.astype(F32), axis=0, keepdims=True)

    in_specs = [pl.BlockSpec((T, tb), functools.partial(lambda i, st, nb: (0, jnp.clip(i - st, 0, nb - 1)), st=st, nb=nb))
                for st, nb in zip(starts, nblk)]
    return pl.pallas_call(
        body, name="proj_in_dw", grid=(sum(nblk),),
        in_specs=[*in_specs, pl.BlockSpec((T, D), lambda i: (0, 0))],
        out_specs=[pl.BlockSpec((tb, D), lambda i: (i, 0)), pl.BlockSpec((1, tb), lambda i: (0, i))],
        out_shape=[_sds((sum(nblk) * tb, D), BF), _sds((1, sum(nblk) * tb), F32)],
        compiler_params=_params(("arbitrary",)),
    )(*segs, h)


def _local_step(x, target, small, wi_t, conv_w, plan):
    T, D = x.shape
    tm = 1024

    def carried(call, res, carry):
        if carry is None:
            return res
        outs, got = res
        plan.done(call, got)
        return outs

    h, r1 = _rms_fwd("rms_mix", x, small["g_mix_norm"])

    def ep_add(acc, ex, outs, ids, scr):
        outs[0][...] = acc + ex[0][...]

    tn_in = IN_WIDTH // 3
    carry = plan.carry("proj_in")
    def ep_bias_bf16(acc, ex, outs, ids, scr):
        outs[0][...] = (acc + ex[0][...]).astype(BF)

    proj, = carried("proj_in", _matmul("proj_in", [h], wi_t, "NT", m=T, n=IN_WIDTH, tm=tm, tn=tn_in,
                                       epilogue=ep_bias_bf16, extra=[(small["b_in"], _row(tn_in))],
                                       outs=[(_sds((T, IN_WIDTH), BF), _tile(tm, tn_in))], carry=carry), carry)
    plan.launch("gather_ffn", after=proj)
    o, got = _attn_fwd(proj, small["sinks"], carry=plan.carry("attn_fwd"))
    plan.done("attn_fwd", got)
    c, got = _conv_fwd(proj, conv_w, small["conv_b"], small["ln_g"], small["ln_b"], carry=plan.carry("conv_fwd"))
    plan.done("conv_fwd", got)
    wap_t, wcp_t, w_out = plan.weight("w_attn_proj"), plan.weight("w_conv_proj"), plan.weight("w_out")
    ya, yc, merged = _proj_merge(o, c, wap_t, wcp_t, small["b_conv_proj"], proj)

    tg = 256
    gate_specs = [pl.BlockSpec((tm, tg), lambda j, i, k: (i, _GATE_BLK + j)),
                  pl.BlockSpec((tm, tg), lambda j, i, k: (i, _GATE_BLK + D // tg + j))]

    def ep_residual_rms(acc, ex, outs, ids, scr):
        x2v = acc + ex[0][...]
        r = lax.rsqrt(jnp.mean(x2v * x2v, axis=-1, keepdims=True) + EPS)
        outs[0][...] = x2v
        outs[1][...] = (x2v * r * ex[1][...]).astype(BF)
        outs[2][...] = r

    carry = plan.carry("out_proj")
    x2, h2, r2 = carried("out_proj", _matmul(
        "out_proj_rms", [merged], w_out, "NN", m=T, n=D, tm=512, tn=D, epilogue=ep_residual_rms,
        extra=[(x, _tile(512, D)), (small["g_ffn_norm"], _row(D))],
        outs=[(_sds((T, D), F32), _tile(512, D)), (_sds((T, D), BF), _tile(512, D)),
              (_sds((T, 1), F32), pl.BlockSpec((512, 1), lambda j, i, k: (i, 0)))], carry=carry), carry)
    plan.launch("gather_down", after=x2)
    wf_t = plan.weight("w_ffn_in")
    (act, gate, up), got = _ffn_in_swiglu(h2, wf_t, carry=plan.carry("ffn_in_swiglu"))
    plan.done("ffn_in_swiglu", got)
    w_down = plan.weight("w_ffn_down")
    def ep_residual_loss(acc, ex, outs, ids, scr):
        dx, dg, part = _loss_head(acc + ex[0][...], ex[1][...], ex[2][...])
        outs[0][...] = dx
        outs[1][...] = dx.astype(BF)
        _accumulate_rows(outs[2], dg, ids[1] == 0)
        _accumulate_rows(outs[3], part, ids[1] == 0)

    dx3, dx3_b, dg_final, loss = _matmul(
        "ffn_down_loss", [act], w_down, "NN", m=T, n=D, tm=512, tn=D, epilogue=ep_residual_loss,
        extra=[(x2, _tile(512, D)), (small["g_final"], _row(D)), (target, _tile(512, D))],
        outs=[(_sds((T, D), F32), _tile(512, D)), (_sds((T, D), BF), _tile(512, D)), (_sds((1, D), F32), _row(D)),
              (_sds((1, 1), F32), pl.BlockSpec((1, 1), lambda j, i, k: (0, 0)))])

    tn_ff = D_FF // 2

    def ep_swiglu_bwd(acc, ex, outs, ids, scr):
        g, u = ex[0][...].astype(F32), ex[1][...].astype(F32)
        sg = jax.nn.sigmoid(g)
        outs[0][...] = (acc * u * sg * (1.0 + g * (1.0 - sg))).astype(BF)
        outs[1][...] = (acc * g * sg).astype(BF)

    dgate, dup = _matmul(
        "ffn_down_bwd", [dx3_b], w_down, "NT", m=T, n=D_FF, tm=tm, tn=tn_ff, epilogue=ep_swiglu_bwd,
        extra=[(gate, _tile(tm, tn_ff)), (up, _tile(tm, tn_ff))],
        outs=[(_sds((T, D_FF), BF), _tile(tm, tn_ff)), (_sds((T, D_FF), BF), _tile(tm, tn_ff))])

    def dw(name, a, b, rows, cols, row_off=0, alias=None, total_rows=None, colsum=False):
        total_rows = rows if total_rows is None else total_rows
        tmw = rows if rows <= 1024 else D_FF // 2
        blk, rem = divmod(row_off, tmw)
        assert rem == 0

        def ep(acc, ex, outs, ids, scr):
            outs[0][...] = acc.astype(BF)
            if colsum:
                outs[1][...] = jnp.sum(ex[0][...].astype(F32), axis=0, keepdims=True)

        outs = [(_sds((total_rows, cols), BF), pl.BlockSpec((tmw, cols), lambda j, i, k: (blk + i, j)))]
        extra = []
        if colsum:
            extra = [(a, pl.BlockSpec((T, tmw), lambda j, i, k: (0, i)))]
            outs.append((_sds((1, rows), F32), pl.BlockSpec((1, tmw), lambda j, i, k: (0, i))))
        carry = plan.carry(name)
        res = carried(name, _matmul(name, [a], b, "TN", m=rows, n=cols, tm=tmw, tn=cols, epilogue=ep, extra=extra,
                                    outs=outs, alias=None if alias is None else (alias, 0), carry=carry), carry)
        return res if colsum else res[0]

    plan.grad_ready(dict(w_ffn_down=dw("ffn_down_dw", act, dx3_b, D_FF, D)))

    def ep_rms_bwd(acc, ex, outs, ids, scr):
        dx, dg = _rms_bwd(acc, ex[0][...], ex[1][...], ex[2][...])
        dx = ex[3][...] + dx
        outs[0][...] = dx
        outs[1][...] = dx.astype(BF)
        _accumulate_rows(outs[2], dg, ids[1] == 0)

    def rms_bwd_io(tm_, xin, r, g, dres):
        return dict(
            extra=[(xin, _tile(tm_, D)), (r, pl.BlockSpec((tm_, 1), lambda j, i, k: (i, 0))), (g, _row(D)),
                   (dres, _tile(tm_, D))],
            outs=[(_sds((T, D), F32), _tile(tm_, D)), (_sds((T, D), BF), _tile(tm_, D)), (_sds((1, D), F32), _row(D))])

    carry = plan.carry("ffn_in_bwd")
    dx2, dx2_b, dg_ffn = carried(
        "ffn_in_bwd",
        _matmul("ffn_in_bwd", [dgate, dup], wf_t, "NN", m=T, n=D, tm=tm, tn=D, tk=D_FF // 2, epilogue=ep_rms_bwd,
                carry=carry, **rms_bwd_io(tm, x2, r2, small["g_ffn_norm"], dx3)), carry)
    plan.launch("send_down")
    gwf_t = dw("ffn_in_dw_gate", dgate, h2, D_FF, D, total_rows=2 * D_FF)
    gwf_t = dw("ffn_in_dw_up", dup, h2, D_FF, D, row_off=D_FF, alias=gwf_t, total_rows=2 * D_FF)
    plan.grad_ready(dict(w_ffn_in=gwf_t))

    def ep_merge_bwd(acc, ex, outs, ids, scr):
        s0 = jax.nn.sigmoid(ex[2][...].astype(F32))
        s1 = jax.nn.sigmoid(ex[3][...].astype(F32))
        outs[0][...] = (acc * s0).astype(BF)
        outs[1][...] = (acc * s1).astype(BF)
        outs[2][...] = (acc * ex[0][...].astype(F32) * s0 * (1.0 - s0)).astype(BF)
        outs[3][...] = (acc * ex[1][...].astype(F32) * s1 * (1.0 - s1)).astype(BF)

    carry = plan.carry("out_proj_bwd_merge")
    dya, dyc, dg0, dg1 = carried(
        "out_proj_bwd_merge",
        _matmul("out_proj_bwd_merge", [dx2_b], w_out, "NT", m=T, n=D, tm=tm, tn=tg, epilogue=ep_merge_bwd,
                extra=[(ya, _tile(tm, tg)), (yc, _tile(tm, tg)), (proj, gate_specs[0]), (proj, gate_specs[1])],
                outs=[(_sds((T, D), BF), _tile(tm, tg))] * 4, carry=carry), carry)
    plan.launch("send_ffn")
    gw_out = dw("out_proj_dw", merged, dx2_b, D, D)
    d_o, = _matmul("attn_proj_bwd", [dya], wap_t, "NN", m=T, n=ATTN_WIDTH, tm=tm, tn=ATTN_WIDTH,
                   epilogue=_store(BF), outs=[(_sds((T, ATTN_WIDTH), BF), _tile(tm, ATTN_WIDTH))])
    d_c, = _matmul("conv_proj_bwd", [dyc], wcp_t, "NN", m=T, n=CONV_CHANNELS, tm=tm, tn=CONV_CHANNELS,
                   epilogue=_store(BF), outs=[(_sds((T, CONV_CHANNELS), BF), _tile(tm, CONV_CHANNELS))])
    gwap_t = dw("attn_proj_dw", dya, o, D, ATTN_WIDTH)
    gwcp_t, db_cp = dw("conv_proj_dw", dyc, c, D, CONV_CHANNELS, colsum=True)
    plan.grad_ready(dict(w_out=gw_out, w_attn_proj=gwap_t, w_conv_proj=gwcp_t))
    (dglu, dcw, dcb, dlng, dlnb), got = _conv_bwd(proj, d_c, conv_w, small["conv_b"], small["ln_g"], small["ln_b"],
                                                  carry=plan.carry("conv_bwd"))
    plan.done("conv_bwd", got)
    plan.launch("send_mix")
    (dqkv, dsinks), got = _attn_bwd(proj, d_o, small["sinks"], carry=plan.carry("attn_bwd"))
    plan.done("attn_bwd", got)

    segs = [dqkv, dglu, dg0, dg1]
    gwi_t, db_in = _proj_in_dw(segs, h)
    plan.grad_ready(dict(w_in=gwi_t))
    plan.alone("swap_inp")
    plan.launch("send_inp")
    carry = plan.carry("proj_in_bwd")
    dx, _, dg_mix = carried(
        "proj_in_bwd",
        _matmul("proj_in_bwd", segs, wi_t, "NN", m=T, n=D, tm=512, tn=D, epilogue=ep_rms_bwd, carry=carry,
                **rms_bwd_io(512, x, r1, small["g_mix_norm"], plan.behind("inp", dx2))), carry)

    parts = dict(g_mix_norm=dg_mix, b_in=db_in, sinks=dsinks, conv_w=dcw, conv_b=dcb, ln_g=dlng, ln_b=dlnb,
                 b_conv_proj=db_cp, g_ffn_norm=dg_ffn, g_final=dg_final, loss=loss)
    return dx, parts


def _place():
    x, y, c = lax.axis_index("x"), lax.axis_index("y"), lax.axis_index("c")
    return x, y, c, [(1 - x, y), (x, 1 - y), (1 - x, 1 - y)]


def _gather_copies(x_refs, out_refs, rows_per, send_sems, recv_sems, local_sems):
    x, y, c, chips = _place()
    me, sibling = (x, y, c), (x, y, 1 - c)

    def rows(a, px, py, pc):
        return out_refs[a].at[pl.ds((4 * px + 2 * py + pc) * rows_per[a], rows_per[a])]

    def copy(a, k, block, to, src=None):
        return pltpu.make_async_remote_copy(
            src_ref=rows(a, *block) if src is None else src, dst_ref=rows(a, *block),
            send_sem=send_sems.at[7 * a + k], recv_sem=recv_sems.at[7 * a + k], device_id=to, device_id_type=MESH)

    def local(a):
        return pltpu.make_async_copy(x_refs[a], rows(a, *me), local_sems.at[a])

    def first(a):
        return [copy(a, 0, me, sibling, src=x_refs[a])] + [copy(a, 1 + j, me, (*chip, c), src=x_refs[a])
                                                          for j, chip in enumerate(chips)]

    def arrive(a, j):
        return copy(a, 1 + j, (*chips[j], c), me)

    def passed(a, j):
        return copy(a, 4 + j, (*chips[j], c), sibling)

    def from_sibling(a):
        return [copy(a, 0, sibling, me)] + [copy(a, 4 + j, (*chip, 1 - c), me) for j, chip in enumerate(chips)]

    return len(x_refs), local, first, arrive, passed, from_sibling


def _gather_start(*refs):
    n, local, first, _, _, _ = _gather_copies(*refs)
    for a in range(n):
        local(a).start()
        for cp in first(a):
            cp.start()


def _gather_finish(*refs):
    n, local, first, arrive, passed, from_sibling = _gather_copies(*refs)
    for a in range(n):
        for j in range(3):
            arrive(a, j).wait_recv()
            passed(a, j).start()
    for a in range(n):
        for cp in from_sibling(a):
            cp.wait_recv()
    for a in range(n):
        for cp in first(a) + [passed(a, j) for j in range(3)]:
            cp.wait_send()
        local(a).wait()


def _gather_peers():
    x, y, c, chips = _place()
    return [(x, y, 1 - c)] + [(*chip, c) for chip in chips]


def _gather_sems(n):
    return [pltpu.SemaphoreType.DMA((7 * n,)), pltpu.SemaphoreType.DMA((7 * n,)), pltpu.SemaphoreType.DMA((n,))]


def _gather_carry(shards):
    rows_per = [s.shape[0] for s in shards]
    return _Carry(shards, [_sds((N_DEV * s.shape[0],) + s.shape[1:], s.dtype) for s in shards],
                  _gather_sems(len(shards)),
                  lambda ins, outs, sems: _gather_start(ins, outs, rows_per, *sems),
                  lambda ins, outs, sems: _gather_finish(ins, outs, rows_per, *sems), _gather_peers)


def _first_gather(shards):
    n = len(shards)
    rows_per = [s.shape[0] for s in shards]

    def body(*refs):
        x_refs, out_refs = refs[:n], refs[n:2 * n]
        send_sems, recv_sems, local_sems = refs[2 * n:]
        x, y, c, chips = _place()
        me, sibling = (x, y, c), (x, y, 1 - c)
        near_x, near_y, far = (*chips[0], c), (*chips[1], c), (*chips[2], c)

        def rows(a, dev, part):
            h = rows_per[a] // 2
            lo, size = {"all": (0, 2 * h), "low": (0, h), "high": (h, h)}[part]
            return out_refs[a].at[pl.ds((4 * dev[0] + 2 * dev[1] + dev[2]) * rows_per[a] + lo, size)]

        def copy(a, k, block, part, to, src=None):
            return pltpu.make_async_remote_copy(
                src_ref=rows(a, block, part) if src is None else src, dst_ref=rows(a, block, part),
                send_sem=send_sems.at[9 * a + k], recv_sem=recv_sems.at[9 * a + k], device_id=to, device_id_type=MESH)

        other = lambda dev: (dev[0], dev[1], 1 - c)
        sent = []
        for a in range(n):
            pltpu.make_async_copy(x_refs[a], rows(a, me, "all"), local_sems.at[a]).start()
            sent += [copy(a, 0, me, "all", sibling, src=x_refs[a]), copy(a, 1, me, "all", near_x, src=x_refs[a]),
                     copy(a, 2, me, "all", near_y, src=x_refs[a])]
        for cp in sent:
            cp.start()
        for a in range(n):
            copy(a, 1, near_x, "all", me).wait_recv()
            copy(a, 2, near_y, "all", me).wait_recv()
            passed = [copy(a, 3, near_y, "high", near_x), copy(a, 4, near_x, "low", near_y),
                      copy(a, 5, near_x, "all", sibling), copy(a, 6, near_y, "all", sibling)]
            for cp in passed:
                cp.start()
            sent += passed
        for a in range(n):
            copy(a, 3, far, "high", me).wait_recv()
            copy(a, 4, far, "low", me).wait_recv()
            passed = [copy(a, 7, far, "high", sibling), copy(a, 8, far, "low", sibling)]
            for cp in passed:
                cp.start()
            sent += passed
        for a in range(n):
            copy(a, 0, sibling, "all", me).wait_recv()
            copy(a, 5, other(near_x), "all", me).wait_recv()
            copy(a, 6, other(near_y), "all", me).wait_recv()
            copy(a, 7, other(far), "high", me).wait_recv()
            copy(a, 8, other(far), "low", me).wait_recv()
        for cp in sent:
            cp.wait_send()
        for a in range(n):
            pltpu.make_async_copy(x_refs[a], rows(a, me, "all"), local_sems.at[a]).wait()

    return pl.pallas_call(
        body, name="weights_first_gather", in_specs=[ANY] * n, out_specs=[ANY] * n,
        out_shape=[_sds((N_DEV * s.shape[0],) + s.shape[1:], s.dtype) for s in shards],
        scratch_shapes=[pltpu.SemaphoreType.DMA((9 * n,)), pltpu.SemaphoreType.DMA((9 * n,)),
                        pltpu.SemaphoreType.DMA((n,))],
    )(*shards)


def _swap_carry(grads):
    n = len(grads)

    def copies(g_refs, out_refs, sems):
        send_sems, recv_sems = sems
        x, y, c, _ = _place()
        return [pltpu.make_async_remote_copy(
            src_ref=g_refs[a].at[2 * p + 1 - c], dst_ref=out_refs[a].at[p],
            send_sem=send_sems.at[4 * a + p], recv_sem=recv_sems.at[4 * a + p],
            device_id=(x, y, 1 - c), device_id_type=MESH) for a in range(n) for p in range(4)]

    def start(ins, outs, sems):
        for cp in copies(ins, outs, sems):
            cp.start()

    def finish(ins, outs, sems):
        for cp in copies(ins, outs, sems):
            cp.wait()

    def peers():
        x, y, c, _ = _place()
        return [(x, y, 1 - c)]

    return _Carry(grads, [_sds((4,) + g.shape[1:], g.dtype) for g in grads],
                  [pltpu.SemaphoreType.DMA((4 * n,)), pltpu.SemaphoreType.DMA((4 * n,))], start, finish, peers)


def _join(carries):
    carries = [c for c in carries if c is not None]
    if not carries:
        return None
    n_in = [len(c.arrays) for c in carries]
    n_out = [len(c.out_shapes) for c in carries]
    n_sem = [len(c.sems) for c in carries]

    def parts(refs, counts):
        cuts = [sum(counts[:q]) for q in range(len(counts) + 1)]
        return [refs[cuts[q]:cuts[q + 1]] for q in range(len(counts))]

    def start(ins, outs, sems):
        for c, i, o, s in zip(carries, parts(ins, n_in), parts(outs, n_out), parts(sems, n_sem)):
            c.start(i, o, s)

    def finish(ins, outs, sems):
        for c, i, o, s in zip(carries, parts(ins, n_in), parts(outs, n_out), parts(sems, n_sem)):
            c.finish(i, o, s)

    return _Carry([a for c in carries for a in c.arrays], [o for c in carries for o in c.out_shapes],
                  [s for c in carries for s in c.sems], start, finish)


def _run_carry(name, carry):
    n_in, n_out = len(carry.arrays), len(carry.out_shapes)

    def body(*refs):
        carry.start(refs[:n_in], refs[n_in:n_in + n_out], refs[n_in + n_out:])
        carry.finish(refs[:n_in], refs[n_in:n_in + n_out], refs[n_in + n_out:])

    return pl.pallas_call(body, name=name, in_specs=[ANY] * n_in, out_specs=[ANY] * n_out,
                          out_shape=carry.out_shapes, scratch_shapes=carry.sems)(*carry.arrays)


def _run_carry_async(name, carry, collective_id):
    ins = [jax.new_ref(a, memory_space=pltpu.MemorySpace.HBM) for a in carry.arrays]
    outs = [jax.empty_ref(o, memory_space=pltpu.MemorySpace.HBM) for o in carry.out_shapes]

    @pl.kernel(mesh=plsc.ScalarSubcoreMesh(axis_name="sequencer", num_cores=1), name=name,
               scratch_types=tuple(carry.sems), compiler_params=pltpu.CompilerParams(collective_id=collective_id))
    def launch(*sems):
        barrier = pltpu.get_barrier_semaphore()
        peers = carry.peers()
        for peer in peers:
            pl.semaphore_signal(barrier, inc=1, device_id=peer, device_id_type=MESH)
        pl.semaphore_wait(barrier, len(peers))
        carry.start(ins, outs, sems)
        carry.finish(ins, outs, sems)

    launch()
    return [o[...] for o in outs]


def _chip_sums(name, gs, gots, c):
    n = len(gs)

    def body(c_ref, *refs):
        for g_ref, got_ref, o_ref in zip(refs[:n], refs[n:2 * n], refs[2 * n:]):
            o_ref[...] = (g_ref[...].astype(F32) + got_ref[...].astype(F32)).astype(BF)

    mine = [pl.BlockSpec((1,) + g.shape[1:], lambda p, c_ref: (2 * p + c_ref[0], 0, 0)) for g in gs]
    slot = [pl.BlockSpec((1,) + g.shape[1:], lambda p, c_ref: (p, 0, 0)) for g in gs]
    return pl.pallas_call(
        body, name=name,
        grid_spec=pltpu.PrefetchScalarGridSpec(num_scalar_prefetch=1, grid=(4,), in_specs=[*mine, *slot],
                                               out_specs=slot),
        out_shape=[_sds((4,) + g.shape[1:], BF) for g in gs],
        compiler_params=_params(("arbitrary",)),
    )(c, *gs, *gots)


def _send_carry(sums, ks):
    n, nk = len(sums), len(ks)

    def copies(s_refs, out_refs, sems):
        send_sems, recv_sems = sems
        x, y, c, chips = _place()
        return [pltpu.make_async_remote_copy(
            src_ref=s_refs[a].at[2 * chips[k][0] + chips[k][1]], dst_ref=out_refs[a].at[q],
            send_sem=send_sems.at[nk * a + q], recv_sem=recv_sems.at[nk * a + q],
            device_id=(*chips[k], c), device_id_type=MESH) for a in range(n) for q, k in enumerate(ks)]

    def start(ins, outs, sems):
        for cp in copies(ins, outs, sems):
            cp.start()

    def finish(ins, outs, sems):
        for cp in copies(ins, outs, sems):
            cp.wait()

    def peers():
        x, y, c, chips = _place()
        return [(*chips[k], c) for k in ks]

    return _Carry(sums, [_sds((nk,) + s.shape[1:], s.dtype) for s in sums],
                  [pltpu.SemaphoreType.DMA((nk * n,)), pltpu.SemaphoreType.DMA((nk * n,))], start, finish, peers)


def _adam_math(w, g, m, v):
    m = ADAM_B1 * m + (1.0 - ADAM_B1) * g
    v = ADAM_B2 * v + (1.0 - ADAM_B2) * (g * g)
    m_hat = m / (1.0 - ADAM_B1 ** ADAM_STEP)
    v_hat = v / (1.0 - ADAM_B2 ** ADAM_STEP)
    delta = -ADAM_LR * (m_hat / (jnp.sqrt(v_hat) + ADAM_EPS) + ADAM_WD * w)
    return delta, m, v


def _adamw(name, w, g, m, v):
    rows, cols = w.shape
    tr = 256 if rows % 256 == 0 else rows

    def body(w_ref, g_ref, m_ref, v_ref, d_ref, nm_ref, nv_ref):
        d_ref[...], nm_ref[...], nv_ref[...] = _adam_math(w_ref[...], g_ref[...], m_ref[...], v_ref[...])

    t = pl.BlockSpec((tr, cols), lambda i: (i, 0))
    return pl.pallas_call(
        body, name=name, grid=(rows // tr,), in_specs=[t] * 4, out_specs=[t] * 3,
        out_shape=[_sds((rows, cols), F32)] * 3, compiler_params=_params(("arbitrary",)),
    )(w, g, m, v)


def _grad_adamw(name, g, got, got3, ids, w, m, v):
    _, rows, cols = g.shape
    n3 = len(got3)
    tr = rows // 2 if rows >= 256 else rows

    def body(ids_ref, g_ref, got_ref, *rest):
        w_ref, m_ref, v_ref, o_ref, d_ref, nm_ref, nv_ref = rest[n3:]
        tot = g_ref[0].astype(F32) + got_ref[0].astype(F32)
        for r_ref in rest[:n3]:
            for q in range(r_ref.shape[0]):
                tot = tot + r_ref[q].astype(F32)
        o_ref[...] = tot
        d_ref[...], nm_ref[...], nv_ref[...] = _adam_math(w_ref[...], tot, m_ref[...], v_ref[...])

    tile = pl.BlockSpec((tr, cols), lambda i, ids_ref: (i, 0))
    return pl.pallas_call(
        body, name=name,
        grid_spec=pltpu.PrefetchScalarGridSpec(
            num_scalar_prefetch=1, grid=(rows // tr,),
            in_specs=[pl.BlockSpec((1, tr, cols), lambda i, ids_ref: (ids_ref[0], i, 0)),
                      pl.BlockSpec((1, tr, cols), lambda i, ids_ref: (ids_ref[1], i, 0)),
                      *[pl.BlockSpec((r.shape[0], tr, cols), lambda i, ids_ref: (0, i, 0)) for r in got3],
                      tile, tile, tile],
            out_specs=[tile] * 4),
        out_shape=[_sds((rows, cols), F32)] * 4,
        compiler_params=_params(("arbitrary",)),
    )(ids, g, got, *got3, w, m, v)


SMALL_NAMES = ["g_mix_norm", "b_in", "sinks", "conv_b", "ln_g", "ln_b", "b_conv_proj", "g_ffn_norm", "g_final"]
_PACK_ROWS = 32


def _small_pack(parts):
    C = CONV_CHANNELS
    part_list = [parts["g_mix_norm"], parts["b_in"], parts["sinks"], parts["conv_b"], parts["ln_g"], parts["ln_b"],
                 parts["b_conv_proj"], parts["g_ffn_norm"], parts["g_final"], parts["loss"], parts["conv_w"]]

    def body(p_mix, p_b, p_sink, p_cb, p_lg, p_lb, p_bcp, p_ffn, p_fin, p_loss, p_cw, pack):
        pack[...] = jnp.zeros_like(pack)
        pack[0:1, :] = p_mix[...]
        pack[1:2, 0:GLU_OFF] = p_b[:, 0:GLU_OFF]
        pack[2:3, :] = p_b[:, GLU_OFF:GATE_OFF]
        pack[3:4, :] = p_b[:, GATE_OFF:GATE_OFF + D_MODEL]
        pack[4:5, :] = p_b[:, GATE_OFF + D_MODEL:]
        pack[5:6, 0:128] = p_sink[...]
        pack[6:7, 0:C] = p_cb[...]
        pack[6:7, C:2 * C] = p_lg[...]
        pack[7:8, 0:C] = p_lb[...]
        pack[8:9, :] = p_bcp[...]
        pack[9:10, :] = p_ffn[...]
        pack[10:11, :] = p_fin[...]
        pack[11:12, 0:128] = jnp.broadcast_to(p_loss[...], (1, 128))
        pack[12:28, 0:C] = p_cw[0:16, :]
        pack[12:28, C:2 * C] = p_cw[16:32, :]

    vm = pl.BlockSpec(memory_space=pltpu.VMEM)
    return pl.pallas_call(body, name="small_pack", in_specs=[vm] * len(part_list), out_specs=vm,
                          out_shape=_sds((_PACK_ROWS, D_MODEL), F32))(*part_list)


def _small_adamw(gathered, small_w, small_m, small_v):
    C = CONV_CHANNELS
    names = SMALL_NAMES
    widths = [small_w[k].shape[1] for k in names]
    n_small = len(names)

    def body(*refs):
        tot_ref = refs[0]
        w_refs = refs[1:1 + n_small]
        m_refs = refs[1 + n_small:1 + 2 * n_small]
        v_refs = refs[1 + 2 * n_small:1 + 3 * n_small]
        o = 1 + 3 * n_small
        loss_ref, cw_ref = refs[o], refs[o + 1]
        out_refs = refs[o + 2:o + 2 + 4 * n_small]
        tot = tot_ref[0:_PACK_ROWS, :]
        for d in range(1, N_DEV):
            tot = tot + tot_ref[d * _PACK_ROWS:(d + 1) * _PACK_ROWS, :]
        loss_ref[...] = tot[11:12, 0:1]
        cw_ref[0:16, :] = tot[12:28, 0:C]
        cw_ref[16:32, :] = tot[12:28, C:2 * C]
        grads = dict(
            g_mix_norm=tot[0:1, :],
            b_in=jnp.concatenate([tot[1:2, 0:GLU_OFF], tot[2:3, :], tot[3:4, :], tot[4:5, :]], axis=1),
            sinks=tot[5:6, 0:N_Q_HEADS], conv_b=tot[6:7, 0:C], ln_g=tot[6:7, C:2 * C], ln_b=tot[7:8, 0:C],
            b_conv_proj=tot[8:9, :], g_ffn_norm=tot[9:10, :], g_final=tot[10:11, :])
        for s, k in enumerate(names):
            g = grads[k]
            d, nm, nv = _adam_math(w_refs[s][...], g, m_refs[s][...], v_refs[s][...])
            out_refs[4 * s][...] = g
            out_refs[4 * s + 1][...] = d
            out_refs[4 * s + 2][...] = nm
            out_refs[4 * s + 3][...] = nv

    vm = pl.BlockSpec(memory_space=pltpu.VMEM)
    args = [gathered, *[small_w[k] for k in names], *[small_m[k] for k in names], *[small_v[k] for k in names]]
    out_shape = [_sds((1, 1), F32), _sds((CONV_PAD, C), F32)]
    for wd in widths:
        out_shape += [_sds((1, wd), F32)] * 4
    res = pl.pallas_call(
        body, name="small_adamw",
        in_specs=[vm] * len(args), out_specs=[vm] * len(out_shape), out_shape=out_shape,
        compiler_params=pltpu.CompilerParams(vmem_limit_bytes=VMEM_LIMIT_BYTES),
    )(*args)
    return res[0], res[1], {k: res[2 + 4 * s:6 + 4 * s] for s, k in enumerate(names)}


BIG = dict(w_in=True, w_attn_proj=True, w_conv_proj=True, w_out=False, w_ffn_in=True, w_ffn_down=False)
WEIGHT_NAMES = ["g_mix_norm", "w_in", "b_in", "sinks", "conv_w", "conv_b", "ln_g", "ln_b", "w_attn_proj",
                "w_conv_proj", "b_conv_proj", "w_out", "g_ffn_norm", "w_ffn_in", "w_ffn_down", "g_final"]


class _Plan:
    GROUPS = dict(down=["w_ffn_down"], ffn=["w_ffn_in"], mix=["w_out", "w_attn_proj", "w_conv_proj"], inp=["w_in"])
    ALL = (0, 1, 2)
    RIDES = dict(
        gather_mix=[("gather", ["w_attn_proj", "w_conv_proj", "w_out"])], gather_ffn=[("gather", ["w_ffn_in"])],
        gather_down=[("gather", ["w_ffn_down"])],
        ffn_in_bwd=[("swap", "down")], send_down=[("send", "down", ALL)],
        out_proj_bwd_merge=[("swap", "ffn")], send_ffn=[("send", "ffn", ALL)],
        conv_bwd=[("swap", "mix")], send_mix=[("send", "mix", ALL)],
        swap_inp=[("swap", "inp")], send_inp=[("send", "inp", ALL)])
    ASYNC = dict(gather_mix=1, gather_ffn=2, gather_down=3, send_down=4, send_ffn=5, send_mix=6, send_inp=7)

    def __init__(self, shards, c1):
        self.shards, self.c1 = shards, c1
        self.full, self.slots, self.got, self.sums, self.got3 = {}, {}, {}, {}, {}

    def weight(self, name):
        return self.full[name]

    def grad_ready(self, grads):
        for k, g in grads.items():
            self.slots[k] = g.reshape(N_DEV, g.shape[0] // N_DEV, g.shape[1])

    def _one(self, kind, what, ks=None):
        if kind == "gather":
            return _gather_carry([self.shards[k] for k in what])
        names = self.GROUPS[what]
        if kind == "swap":
            return _swap_carry([self.slots[k] for k in names])
        return _send_carry([self.sums[k] for k in names], ks)

    def carry(self, call):
        return _join([self._one(*ride) for ride in self.RIDES.get(call, [])])

    def done(self, call, outs):
        outs = list(outs)
        for kind, what, *_ in self.RIDES.get(call, []):
            names = what if kind == "gather" else self.GROUPS[what]
            mine, outs = outs[:len(names)], outs[len(names):]
            if kind == "gather":
                self.full.update(zip(names, mine))
            elif kind == "send":
                for k, r in zip(names, mine):
                    self.got3.setdefault(k, []).append(r)
            else:
                self.got.update(zip(names, mine))
                self.sums.update(zip(names, _chip_sums(f"chip_sums_{what}", [self.slots[k] for k in names], mine, self.c1)))

    def alone(self, call):
        self.done(call, _run_carry(call, self.carry(call)))

    def behind(self, group, x):
        return lax.optimization_barrier((x, tuple(self.sums[k] for k in self.GROUPS[group])))[0]

    def launch(self, call, after=None):
        carry = self._one(*self.RIDES[call][0])
        if after is not None:
            carry.arrays = list(lax.optimization_barrier((tuple(carry.arrays), after))[0])
        self.done(call, _run_carry_async(call, carry, self.ASYNC[call]))


def kernel(x, g_mix_norm, w_in, b_in, sinks, conv_w, conv_b, ln_g, ln_b, w_attn_proj, w_conv_proj, b_conv_proj, w_out, g_ffn_norm, w_ffn_in, w_ffn_down, g_final, loss_target, m_g_mix_norm, m_w_in, m_b_in, m_sinks, m_conv_w, m_conv_b, m_ln_g, m_ln_b, m_w_attn_proj, m_w_conv_proj, m_b_conv_proj, m_w_out, m_g_ffn_norm, m_w_ffn_in, m_w_ffn_down, m_g_final, v_g_mix_norm, v_w_in, v_b_in, v_sinks, v_conv_w, v_conv_b, v_ln_g, v_ln_b, v_w_attn_proj, v_w_conv_proj, v_b_conv_proj, v_w_out, v_g_ffn_norm, v_w_ffn_in, v_w_ffn_down, v_g_final):
    w = dict(g_mix_norm=g_mix_norm, w_in=w_in, b_in=b_in, sinks=sinks, conv_w=conv_w, conv_b=conv_b, ln_g=ln_g,
             ln_b=ln_b, w_attn_proj=w_attn_proj, w_conv_proj=w_conv_proj, b_conv_proj=b_conv_proj, w_out=w_out,
             g_ffn_norm=g_ffn_norm, w_ffn_in=w_ffn_in, w_ffn_down=w_ffn_down, g_final=g_final)
    m = dict(g_mix_norm=m_g_mix_norm, w_in=m_w_in, b_in=m_b_in, sinks=m_sinks, conv_w=m_conv_w, conv_b=m_conv_b,
             ln_g=m_ln_g, ln_b=m_ln_b, w_attn_proj=m_w_attn_proj, w_conv_proj=m_w_conv_proj,
             b_conv_proj=m_b_conv_proj, w_out=m_w_out, g_ffn_norm=m_g_ffn_norm, w_ffn_in=m_w_ffn_in,
             w_ffn_down=m_w_ffn_down, g_final=m_g_final)
    v = dict(g_mix_norm=v_g_mix_norm, w_in=v_w_in, b_in=v_b_in, sinks=v_sinks, conv_w=v_conv_w, conv_b=v_conv_b,
             ln_g=v_ln_g, ln_b=v_ln_b, w_attn_proj=v_w_attn_proj, w_conv_proj=v_w_conv_proj,
             b_conv_proj=v_b_conv_proj, w_out=v_w_out, g_ffn_norm=v_g_ffn_norm, w_ffn_in=v_w_ffn_in,
             w_ffn_down=v_w_ffn_down, g_final=v_g_final)
    ax, ay, ac = lax.axis_index("x"), lax.axis_index("y"), lax.axis_index("c")
    me = 4 * ax + 2 * ay + ac
    chip = 2 * ax + ay

    shards = {k: (w[k][0].T if tr else w[k][0]).astype(BF) for k, tr in BIG.items()}
    cw_shard = jnp.pad(conv_w[0].T, ((0, 0), (0, 1))).reshape(16, 128)
    wi_t, cw_full = _first_gather([shards["w_in"], cw_shard])
    conv_full = cw_full.reshape(CONV_CHANNELS, CONV_PAD).T

    as_row = lambda a: a.reshape(1, -1)
    small_w = {k: as_row(w[k]) for k in SMALL_NAMES}
    small_m = {k: as_row(m[k]) for k in SMALL_NAMES}
    small_v = {k: as_row(v[k]) for k in SMALL_NAMES}
    plan = _Plan(shards, ac.reshape(1).astype(jnp.int32))
    plan.launch("gather_mix", after=wi_t)
    dx, parts = _local_step(x[0], loss_target[0], small_w, wi_t, conv_full, plan)

    ids = jnp.stack([me, chip]).astype(jnp.int32)
    grads, delta, new_m, new_v, after = {}, {}, {}, {}, dx
    packed = _small_pack(parts)
    for k in sorted(BIG, key=lambda k: k == "w_in"):
        if k == "w_in":
            packed = lax.optimization_barrier((packed, after))[0]
            small_gathered, = _run_carry_async("small_gather", _gather_carry([packed]), 8)
        flip = (lambda a: a.T) if BIG[k] else (lambda a: a)
        wk = lax.optimization_barrier((w[k][0], after))[0]
        outs = _grad_adamw(f"grad_adamw_{k}", plan.slots[k], plan.got[k], plan.got3[k], ids,
                           flip(wk), flip(m[k][0]), flip(v[k][0]))
        after = outs[0]
        grads[k], delta[k], new_m[k], new_v[k] = (flip(a)[None] for a in outs)

    loss, cw_grad, small_out = _small_adamw(small_gathered, small_w, small_m, small_v)
    for k in SMALL_NAMES:
        g, d, nm, nv = (a.reshape(w[k].shape) for a in small_out[k])
        grads[k], delta[k], new_m[k], new_v[k] = g, d, nm, nv
    cw_mine = lax.dynamic_slice(cw_grad, (0, me * 64), (CONV_WIDTH, 64))
    d, nm, nv = _adamw("adamw_conv_w", conv_w[0], cw_mine, m_conv_w[0], v_conv_w[0])
    grads["conv_w"], delta["conv_w"], new_m["conv_w"], new_v["conv_w"] = cw_mine[None], d[None], nm[None], nv[None]

    return (loss.reshape(()), dx[None], *[grads[k] for k in WEIGHT_NAMES], *[delta[k] for k in WEIGHT_NAMES],
            *[new_m[k] for k in WEIGHT_NAMES], *[new_v[k] for k in WEIGHT_NAMES])
```

```python
import functools

import jax
import jax.numpy as jnp
from jax import lax
from jax.experimental import pallas as pl
from jax.experimental.pallas import tpu as pltpu
from jax.experimental.pallas import tpu_sc as plsc

F32 = jnp.float32
BF = jnp.bfloat16

SEQ = 2048
D_MODEL = 1024
HEAD_DIM = 64
N_Q_HEADS = 8
N_KV_HEADS = 2
GROUP = N_Q_HEADS // N_KV_HEADS
BLOCK = 128
ATTN_WIDTH = 512
KV_WIDTH = 128
CONV_CHANNELS = 512
CONV_WIDTH = 31
CONV_PAD = 32
GLU_OFF = 768
GATE_OFF = 1792
IN_WIDTH = 3840
D_FF = 2816
EPS = 1e-5
NEG = -1e30
N_DEV = 8

ADAM_LR = 0.001
ADAM_B1 = 0.9
ADAM_B2 = 0.999
ADAM_EPS = 1e-08
ADAM_WD = 0.01
ADAM_STEP = 10

VMEM_LIMIT_BYTES = 56 * 1024 * 1024
MESH = pl.DeviceIdType.MESH
ANY = pl.BlockSpec(memory_space=pl.ANY)

_DIMS = {"NN": (((1,), (0,)), ((), ())), "NT": (((1,), (1,)), ((), ())), "TN": (((0,), (0,)), ((), ()))}


def _params(sem):
    return pltpu.CompilerParams(dimension_semantics=sem, vmem_limit_bytes=VMEM_LIMIT_BYTES)


class _Carry:
    def __init__(self, arrays, out_shapes, sems, start, finish, peers=None):
        self.arrays, self.out_shapes, self.sems, self.start, self.finish = arrays, out_shapes, sems, start, finish
        self.peers = peers


def _carry_io(carry):
    if carry is None:
        return [], [], []
    return list(carry.arrays), list(carry.out_shapes), list(carry.sems)


def _matmul(name, a_list, b, mode, *, m, n, tm, tn, tk=None, epilogue, extra=(), outs, b_off=(0, 0), alias=None,
            scratch=(), carry=None):
    seg_k = [a.shape[0] if mode == "TN" else a.shape[1] for a in a_list]
    whole = tk is None
    seg_nk = [1] * len(a_list) if whole else [ks // tk for ks in seg_k]
    nk = 1 if whole else sum(seg_nk)
    starts = [sum(seg_nk[:s]) for s in range(len(seg_nk))]
    k_starts = [sum(seg_k[:s]) for s in range(len(seg_k))]
    k_tot = sum(seg_k)
    n_a, n_extra, n_out = len(a_list), len(extra), len(outs)

    a_specs = []
    for st, ns, ks in zip(starts, seg_nk, seg_k):
        if mode == "TN":
            a_specs.append(pl.BlockSpec((ks if whole else tk, tm), lambda j, i, k: (k, i)))
        elif whole:
            a_specs.append(pl.BlockSpec((tm, ks), lambda j, i, k: (i, 0)))
        else:
            a_specs.append(pl.BlockSpec((tm, tk), functools.partial(
                lambda j, i, k, st, ns: (i, jnp.clip(k - st, 0, ns - 1)), st=st, ns=ns)))
    bk = k_tot if whole else tk
    if mode == "NT":
        b_spec = pl.BlockSpec((tn, bk), lambda j, i, k: (b_off[0] + j, b_off[1] + k))
    else:
        b_spec = pl.BlockSpec((bk, tn), lambda j, i, k: (b_off[0] + k, b_off[1] + j))
    n_alias = 0 if alias is None else 1
    c_in, c_out, c_sems = _carry_io(carry)
    n_acc = 0 if whole else 1
    nj, ni = n // tn, m // tm

    def body(*refs):
        pos = [n_a, 1, n_alias, n_extra, len(c_in), n_out, len(c_out), n_acc, len(scratch), len(c_sems)]
        cuts = [sum(pos[:q]) for q in range(len(pos) + 1)]
        a_refs, (b_ref,), _, ex, ci_refs, out_refs, co_refs, acc_refs, scr, cs_refs = (
            refs[cuts[q]:cuts[q + 1]] for q in range(len(pos)))
        j, i, k = pl.program_id(0), pl.program_id(1), pl.program_id(2)
        ids = (j, i)
        if carry is not None:
            @pl.when((j == 0) & (i == 0) & (k == 0))
            def _():
                carry.start(ci_refs, co_refs, cs_refs)

        def dot(a_ref, bv):
            return lax.dot_general(a_ref[...].astype(BF), bv.astype(BF), _DIMS[mode], preferred_element_type=F32)

        if whole:
            tot = None
            for a_ref, k0, ks in zip(a_refs, k_starts, seg_k):
                if n_a == 1:
                    bv = b_ref[...]
                else:
                    bv = b_ref[:, k0:k0 + ks] if mode == "NT" else b_ref[k0:k0 + ks, :]
                part = dot(a_ref, bv)
                tot = part if tot is None else tot + part
            epilogue(tot, ex, out_refs, ids, scr)
        else:
            acc, = acc_refs

            @pl.when(k == 0)
            def _():
                acc[...] = jnp.zeros_like(acc)

            for a_ref, st, ns in zip(a_refs, starts, seg_nk):
                if n_a == 1:
                    acc[...] += dot(a_ref, b_ref[...])
                else:
                    @pl.when((k >= st) & (k < st + ns))
                    def _(a_ref=a_ref):
                        acc[...] += dot(a_ref, b_ref[...])

            @pl.when(k == nk - 1)
            def _():
                epilogue(acc[...], ex, out_refs, ids, scr)

        if carry is not None:
            @pl.when((j == nj - 1) & (i == ni - 1) & (k == nk - 1))
            def _():
                carry.finish(ci_refs, co_refs, cs_refs)

    in_specs = [*a_specs, b_spec]
    args = [*a_list, b]
    io_alias = {}
    if alias is not None:
        in_specs.append(pl.BlockSpec(memory_space=pl.ANY))
        args.append(alias[0])
        io_alias = {n_a + 1: alias[1]}
    in_specs += [s for _, s in extra] + [pl.BlockSpec(memory_space=pl.ANY)] * len(c_in)
    args += [x for x, _ in extra] + c_in
    res = pl.pallas_call(
        body, name=name, grid=(nj, ni, nk), in_specs=in_specs,
        out_specs=[s for _, s in outs] + [pl.BlockSpec(memory_space=pl.ANY)] * len(c_out),
        out_shape=[o for o, _ in outs] + c_out,
        scratch_shapes=[*([] if whole else [pltpu.VMEM((tm, tn), F32)]), *scratch, *c_sems],
        input_output_aliases=io_alias,
        compiler_params=_params(("arbitrary", "arbitrary", "arbitrary")),
    )(*args)
    return res if carry is None else (res[:n_out], res[n_out:])


def _tile(tm, tn):
    return pl.BlockSpec((tm, tn), lambda j, i, k: (i, j))


def _row(tn):
    return pl.BlockSpec((1, tn), lambda j, i, k: (0, j))


def _store(dtype):
    def ep(acc, ex, outs, ids, scr):
        outs[0][...] = acc.astype(dtype)
    return ep


def _sds(shape, dtype):
    return jax.ShapeDtypeStruct(shape, dtype)


def _rms_fwd(name, x, g):
    T, D = x.shape
    tm = 512

    def body(x_ref, g_ref, h_ref, r_ref):
        xv = x_ref[...]
        r = lax.rsqrt(jnp.mean(xv * xv, axis=-1, keepdims=True) + EPS)
        h_ref[...] = (xv * r * g_ref[...]).astype(BF)
        r_ref[...] = r

    return pl.pallas_call(
        body, name=name, grid=(T // tm,),
        in_specs=[pl.BlockSpec((tm, D), lambda i: (i, 0)), pl.BlockSpec((1, D), lambda i: (0, 0))],
        out_specs=[pl.BlockSpec((tm, D), lambda i: (i, 0)), pl.BlockSpec((tm, 1), lambda i: (i, 0))],
        out_shape=[_sds((T, D), BF), _sds((T, 1), F32)],
        compiler_params=_params(("arbitrary",)),
    )(x, g)


def _rms_bwd(dh, xv, r, g):
    xh = xv * r
    dxh = dh * g
    dx = r * (dxh - xh * jnp.mean(dxh * xh, axis=-1, keepdims=True))
    return dx, jnp.sum(dh * xh, axis=0, keepdims=True)


def _accumulate_rows(ref, val, first):
    @pl.when(first)
    def _():
        ref[...] = val

    @pl.when(jnp.logical_not(first))
    def _():
        ref[...] += val


def _loss_head(xv, g, target):
    r = lax.rsqrt(jnp.mean(xv * xv, axis=-1, keepdims=True) + EPS)
    err = xv * r * g - target
    dx, dg = _rms_bwd(err * (1.0 / xv.shape[-1]), xv, r, g)
    part = 0.5 * jnp.sum(jnp.mean(err * err, axis=-1, keepdims=True), axis=0, keepdims=True)
    return dx, dg, part


def _lane_half(shape, h):
    lane = lax.broadcasted_iota(jnp.int32, shape, 1)
    return (lane >= HEAD_DIM * h) & (lane < HEAD_DIM * (h + 1))


def _to_half(v, w, h):
    if w != h:
        v = pltpu.roll(v, HEAD_DIM, 1)
    return jnp.where(_lane_half(v.shape, h), v, 0.0)


def _attn_block(qkv_ref, sinks_ref, n, h):
    r0 = pl.multiple_of(n * BLOCK, BLOCK)
    p0 = pl.multiple_of(jnp.maximum(n - 1, 0) * BLOCK, BLOCK)
    rows = pl.ds(r0, BLOCK)
    prev = pl.ds(p0, BLOCK)
    k2 = jnp.concatenate([qkv_ref[prev, ATTN_WIDTH:ATTN_WIDTH + KV_WIDTH],
                          qkv_ref[rows, ATTN_WIDTH:ATTN_WIDTH + KV_WIDTH]], axis=0)
    v2 = jnp.concatenate([qkv_ref[prev, ATTN_WIDTH + KV_WIDTH:ATTN_WIDTH + 2 * KV_WIDTH],
                          qkv_ref[rows, ATTN_WIDTH + KV_WIDTH:ATTN_WIDTH + 2 * KV_WIDTH]], axis=0)
    qs = []
    for g in range(GROUP):
        hq = GROUP * h + g
        blk = qkv_ref[rows, (hq // 2) * 128:(hq // 2 + 1) * 128].astype(F32)
        qs.append(_to_half(blk, hq % 2, h))
    q4 = jnp.concatenate(qs, axis=0).astype(BF)
    s = lax.dot_general(q4, k2, _DIMS["NT"], preferred_element_type=F32) * (HEAD_DIM ** -0.5)
    shape = s.shape
    row = lax.broadcasted_iota(jnp.int32, shape, 0)
    qi = row & (BLOCK - 1)
    kj = lax.broadcasted_iota(jnp.int32, shape, 1)
    diff = qi + BLOCK - kj
    valid = (diff >= 0) & (diff < BLOCK) & ((kj >= BLOCK) | (n > 0))
    s = jnp.where(valid, s, NEG)
    row1 = lax.broadcasted_iota(jnp.int32, (shape[0], 1), 0)
    sink = jnp.zeros((shape[0], 1), F32)
    for g in range(GROUP):
        sink = jnp.where((row1 >= g * BLOCK) & (row1 < (g + 1) * BLOCK), sinks_ref[0, GROUP * h + g], sink)
    m = jnp.maximum(jnp.max(s, axis=-1, keepdims=True), sink)
    e = jnp.exp(s - m)
    es = jnp.exp(sink - m)
    inv = 1.0 / (jnp.sum(e, axis=-1, keepdims=True) + es)
    return e * inv, es * inv, q4, k2, v2, rows, prev


def _attn_fwd(proj, sinks, carry=None):
    T = proj.shape[0]
    c_in, c_out, c_sems = _carry_io(carry)

    def body(*refs):
        qkv_ref, sinks_ref = refs[:2]
        ci_refs = refs[2:2 + len(c_in)]
        o_ref = refs[2 + len(c_in)]
        co_refs = refs[3 + len(c_in):3 + len(c_in) + len(c_out)]
        cs_refs = refs[3 + len(c_in) + len(c_out):]
        if carry is not None:
            carry.start(ci_refs, co_refs, cs_refs)

        def blk(n, z):
            outs = [None] * (N_Q_HEADS // 2)
            for h in range(N_KV_HEADS):
                p, _, _, _, v2, rows, _ = _attn_block(qkv_ref, sinks_ref, n, h)
                o = lax.dot_general(p.astype(BF), v2, _DIMS["NN"], preferred_element_type=F32)
                for g in range(GROUP):
                    hq = GROUP * h + g
                    piece = jnp.where(_lane_half((BLOCK, 128), h), o[g * BLOCK:(g + 1) * BLOCK], 0.0)
                    if hq % 2 != h:
                        piece = pltpu.roll(piece, HEAD_DIM, 1)
                    outs[hq // 2] = piece if outs[hq // 2] is None else outs[hq // 2] + piece
            for pb in range(N_Q_HEADS // 2):
                o_ref[rows, pb * 128:(pb + 1) * 128] = outs[pb].astype(BF)
            return z

        lax.fori_loop(0, T // BLOCK, blk, 0)
        if carry is not None:
            carry.finish(ci_refs, co_refs, cs_refs)

    res = pl.pallas_call(
        body, name="attn_fwd", grid=(1,),
        in_specs=[pl.BlockSpec((T, GLU_OFF), lambda i: (0, 0)), pl.BlockSpec(memory_space=pltpu.SMEM),
                  *[ANY] * len(c_in)],
        out_specs=[pl.BlockSpec((T, ATTN_WIDTH), lambda i: (0, 0)), *[ANY] * len(c_out)],
        out_shape=[_sds((T, ATTN_WIDTH), BF), *c_out], scratch_shapes=c_sems,
        compiler_params=_params(("arbitrary",)),
    )(proj, sinks, *c_in)
    return res[0], res[1:]


def _attn_bwd(proj, d_o, sinks, carry=None):
    T = proj.shape[0]
    c_in, c_out, c_sems = _carry_io(carry)

    def body(*refs):
        qkv_ref, do_ref, sinks_ref = refs[:3]
        ci_refs = refs[3:3 + len(c_in)]
        dqkv_ref, dsink_ref = refs[3 + len(c_in):5 + len(c_in)]
        co_refs = refs[5 + len(c_in):5 + len(c_in) + len(c_out)]
        dk_acc, dv_acc = refs[5 + len(c_in) + len(c_out):7 + len(c_in) + len(c_out)]
        cs_refs = refs[7 + len(c_in) + len(c_out):]
        if carry is not None:
            carry.start(ci_refs, co_refs, cs_refs)
        dsink_ref[...] = jnp.zeros_like(dsink_ref)
        dk_acc[...] = jnp.zeros_like(dk_acc)
        dv_acc[...] = jnp.zeros_like(dv_acc)

        def blk(n, carry):
            dqs = [None] * (N_Q_HEADS // 2)
            for h in range(N_KV_HEADS):
                p, psink, q4, k2, v2, rows, prev = _attn_block(qkv_ref, sinks_ref, n, h)
                dos = []
                for g in range(GROUP):
                    hq = GROUP * h + g
                    dos.append(_to_half(do_ref[rows, (hq // 2) * 128:(hq // 2 + 1) * 128].astype(F32), hq % 2, h))
                do4 = jnp.concatenate(dos, axis=0).astype(BF)
                dp = lax.dot_general(do4, v2, _DIMS["NT"], preferred_element_type=F32)
                delta = jnp.sum(p * dp, axis=-1, keepdims=True)
                ds = (p * (dp - delta) * (HEAD_DIM ** -0.5)).astype(BF)
                dsk = psink * delta
                for g in range(GROUP):
                    hq = GROUP * h + g
                    tot = -jnp.sum(dsk[g * BLOCK:(g + 1) * BLOCK], axis=0, keepdims=True)
                    lane = lax.broadcasted_iota(jnp.int32, (1, 128), 1)
                    dsink_ref[...] += jnp.where(lane == hq, tot, 0.0)
                dq = lax.dot_general(ds, k2, _DIMS["NN"], preferred_element_type=F32)
                dk = lax.dot_general(ds, q4, _DIMS["TN"], preferred_element_type=F32)
                dv = lax.dot_general(p.astype(BF), do4, _DIMS["TN"], preferred_element_type=F32)
                dk_acc[prev, :] += dk[:BLOCK]
                dk_acc[rows, :] += dk[BLOCK:]
                dv_acc[prev, :] += dv[:BLOCK]
                dv_acc[rows, :] += dv[BLOCK:]
                for g in range(GROUP):
                    hq = GROUP * h + g
                    piece = jnp.where(_lane_half((BLOCK, 128), h), dq[g * BLOCK:(g + 1) * BLOCK], 0.0)
                    if hq % 2 != h:
                        piece = pltpu.roll(piece, HEAD_DIM, 1)
                    dqs[hq // 2] = piece if dqs[hq // 2] is None else dqs[hq // 2] + piece
            for pb in range(N_Q_HEADS // 2):
                dqkv_ref[rows, pb * 128:(pb + 1) * 128] = dqs[pb].astype(BF)
            return carry

        lax.fori_loop(0, T // BLOCK, blk, 0)
        dqkv_ref[:, ATTN_WIDTH:ATTN_WIDTH + KV_WIDTH] = dk_acc[...].astype(BF)
        dqkv_ref[:, ATTN_WIDTH + KV_WIDTH:] = dv_acc[...].astype(BF)
        if carry is not None:
            carry.finish(ci_refs, co_refs, cs_refs)

    res = pl.pallas_call(
        body, name="attn_bwd", grid=(1,),
        in_specs=[pl.BlockSpec((T, GLU_OFF), lambda i: (0, 0)), pl.BlockSpec((T, ATTN_WIDTH), lambda i: (0, 0)),
                  pl.BlockSpec(memory_space=pltpu.SMEM), *[ANY] * len(c_in)],
        out_specs=[pl.BlockSpec((T, GLU_OFF), lambda i: (0, 0)), pl.BlockSpec((1, 128), lambda i: (0, 0)),
                   *[ANY] * len(c_out)],
        out_shape=[_sds((T, GLU_OFF), BF), _sds((1, 128), F32), *c_out],
        scratch_shapes=[pltpu.VMEM((T, KV_WIDTH), F32), pltpu.VMEM((T, KV_WIDTH), F32), *c_sems],
        compiler_params=_params(("arbitrary",)),
    )(proj, d_o, sinks, *c_in)
    return res[:2], res[2:]


CHUNK = 256
SUB = 32
WIN = CHUNK + 32
PAD_ROWS = SEQ + 2 * CONV_PAD
_GLU_SPECS = [pl.BlockSpec((SEQ, 256), functools.partial(lambda i, c: (0, c), c=GLU_OFF // 256 + c)) for c in range(4)]


def _glu_to_pad(a0, a1, b0, b1, zpad):
    C = CONV_CHANNELS
    zpad[0:CONV_PAD, :] = jnp.zeros((CONV_PAD, C), F32)
    zpad[CONV_PAD + SEQ:, :] = jnp.zeros((CONV_PAD, C), F32)
    zpad[CONV_PAD:CONV_PAD + SEQ, 0:256] = a0[...].astype(F32) * jax.nn.sigmoid(b0[...].astype(F32))
    zpad[CONV_PAD:CONV_PAD + SEQ, 256:C] = a1[...].astype(F32) * jax.nn.sigmoid(b1[...].astype(F32))


def _tap_windows(src, base, win):
    for b in range(8):
        win[b, 0:WIN - 8, :] = src[base + b:base + b + WIN - 8, :]


def _taps(win, w_ref, init, out, flip):
    def sub(si, carry):
        r0 = pl.multiple_of(si * SUB, SUB)
        acc = jnp.broadcast_to(init, (SUB, CONV_CHANNELS))
        for k in range(CONV_WIDTH):
            wk = (CONV_WIDTH - 1 - k) if flip else k
            acc = acc + w_ref[wk:wk + 1, :] * win[k % 8, pl.ds(r0 + 8 * (k // 8), SUB), :]
        out[pl.ds(r0, SUB), :] = acc
        return carry

    lax.fori_loop(0, CHUNK // SUB, sub, 0)


def _tap_grads(win, du, dwacc):
    def sub(si, carry):
        r0 = pl.multiple_of(si * SUB, SUB)
        d = du[pl.ds(r0, SUB), :]
        for k in range(CONV_WIDTH):
            p = d * win[k % 8, pl.ds(r0 + 8 * (k // 8), SUB), :]
            dwacc[8 * k:8 * k + 8, :] += (p[0:8] + p[8:16]) + (p[16:24] + p[24:32])
        return carry

    lax.fori_loop(0, CHUNK // SUB, sub, 0)


def _ln_parts(u):
    mu = jnp.mean(u, axis=-1, keepdims=True)
    xc = u - mu
    rstd = lax.rsqrt(jnp.mean(xc * xc, axis=-1, keepdims=True) + EPS)
    return xc * rstd, rstd


def _conv_fwd(proj, conv_w, conv_b, ln_g, ln_b, carry=None):
    T, C = proj.shape[0], CONV_CHANNELS
    vec = pl.BlockSpec((1, C), lambda i: (0, 0))
    c_in, c_out, c_sems = _carry_io(carry)

    def body(*refs):
        a0, a1, b0, b1, w_ref, cb_ref, g_ref, be_ref = refs[:8]
        ci_refs = refs[8:8 + len(c_in)]
        c_ref, u_ref = refs[8 + len(c_in):10 + len(c_in)]
        co_refs = refs[10 + len(c_in):10 + len(c_in) + len(c_out)]
        zpad, win, ubuf = refs[10 + len(c_in) + len(c_out):13 + len(c_in) + len(c_out)]
        cs_refs = refs[13 + len(c_in) + len(c_out):]
        if carry is not None:
            carry.start(ci_refs, co_refs, cs_refs)
        _glu_to_pad(a0, a1, b0, b1, zpad)
        for ci in range(T // CHUNK):
            _tap_windows(zpad, ci * CHUNK + CONV_PAD - (CONV_WIDTH - 1), win)
            _taps(win, w_ref, cb_ref[...], ubuf, False)
            u = ubuf[...]
            u_ref[ci * CHUNK:(ci + 1) * CHUNK, :] = u
            xh, _ = _ln_parts(u)
            ln = xh * g_ref[...] + be_ref[...]
            c_ref[ci * CHUNK:(ci + 1) * CHUNK, :] = (ln * jax.nn.sigmoid(ln)).astype(BF)
        if carry is not None:
            carry.finish(ci_refs, co_refs, cs_refs)

    res = pl.pallas_call(
        body, name="conv_fwd", grid=(1,),
        in_specs=[*_GLU_SPECS, pl.BlockSpec((CONV_PAD, C), lambda i: (0, 0)), vec, vec, vec, *[ANY] * len(c_in)],
        out_specs=[pl.BlockSpec((T, C), lambda i: (0, 0)), pl.BlockSpec((T, C), lambda i: (0, 0)), *[ANY] * len(c_out)],
        out_shape=[_sds((T, C), BF), _sds((T, C), F32), *c_out],
        scratch_shapes=[pltpu.VMEM((PAD_ROWS, C), F32), pltpu.VMEM((8, WIN, C), F32), pltpu.VMEM((CHUNK, C), F32),
                        *c_sems],
        compiler_params=_params(("arbitrary",)),
    )(proj, proj, proj, proj, conv_w, conv_b, ln_g, ln_b, *c_in)
    return res[:2], res[2:]


def _conv_bwd(proj, u, d_c, conv_w, conv_b, ln_g, ln_b, carry=None):
    T, C = proj.shape[0], CONV_CHANNELS
    vec = pl.BlockSpec((1, C), lambda i: (0, 0))
    wspec = pl.BlockSpec((CONV_PAD, C), lambda i: (0, 0))
    c_in, c_out, c_sems = _carry_io(carry)

    def body(*refs):
        a0, a1, b0, b1, u_ref, dc_ref, w_ref, cb_ref, g_ref, be_ref = refs[:10]
        ci_refs = refs[10:10 + len(c_in)]
        o = 10 + len(c_in)
        dglu_ref, dw_ref, dcb_ref, dg_ref, dbe_ref = refs[o:o + 5]
        co_refs = refs[o + 5:o + 5 + len(c_out)]
        zpad, dupad, win, ubuf, dwacc = refs[o + 5 + len(c_out):o + 10 + len(c_out)]
        cs_refs = refs[o + 10 + len(c_out):]
        if carry is not None:
            carry.start(ci_refs, co_refs, cs_refs)
        _glu_to_pad(a0, a1, b0, b1, zpad)
        dupad[T:, :] = jnp.zeros((2 * CONV_PAD, C), F32)
        dwacc[...] = jnp.zeros_like(dwacc)
        dcb_ref[...] = jnp.zeros_like(dcb_ref)
        dg_ref[...] = jnp.zeros_like(dg_ref)
        dbe_ref[...] = jnp.zeros_like(dbe_ref)
        for ci in range(T // CHUNK):
            rows = slice(ci * CHUNK, (ci + 1) * CHUNK)
            _tap_windows(zpad, ci * CHUNK + CONV_PAD - (CONV_WIDTH - 1), win)
            xh, rstd = _ln_parts(u_ref[rows, :])
            ln = xh * g_ref[...] + be_ref[...]
            sg = jax.nn.sigmoid(ln)
            dln = dc_ref[rows, :].astype(F32) * (sg * (1.0 + ln * (1.0 - sg)))
            dg_ref[...] += jnp.sum(dln * xh, axis=0, keepdims=True)
            dbe_ref[...] += jnp.sum(dln, axis=0, keepdims=True)
            dxh = dln * g_ref[...]
            du = rstd * (dxh - jnp.mean(dxh, axis=-1, keepdims=True)
                         - xh * jnp.mean(dxh * xh, axis=-1, keepdims=True))
            dupad[rows, :] = du
            dcb_ref[...] += jnp.sum(du, axis=0, keepdims=True)
            _tap_grads(win, dupad.at[rows, :], dwacc)
        for k in range(CONV_WIDTH):
            dw_ref[k:k + 1, :] = jnp.sum(dwacc[8 * k:8 * k + 8, :], axis=0, keepdims=True)
        dw_ref[CONV_WIDTH:, :] = jnp.zeros((CONV_PAD - CONV_WIDTH, C), F32)
        for ci in range(T // CHUNK):
            rows = slice(ci * CHUNK, (ci + 1) * CHUNK)
            _tap_windows(dupad, ci * CHUNK, win)
            _taps(win, w_ref, jnp.zeros((1, C), F32), ubuf, True)
            dz = ubuf[...]
            for half, (a, b) in enumerate(((a0, b0), (a1, b1))):
                sb = jax.nn.sigmoid(b[rows, :].astype(F32))
                dzh = dz[:, half * 256:(half + 1) * 256]
                dglu_ref[rows, half * 256:(half + 1) * 256] = (dzh * sb).astype(BF)
                dglu_ref[rows, C + half * 256:C + (half + 1) * 256] = (
                    dzh * a[rows, :].astype(F32) * sb * (1.0 - sb)).astype(BF)
        if carry is not None:
            carry.finish(ci_refs, co_refs, cs_refs)

    res = pl.pallas_call(
        body, name="conv_bwd", grid=(1,),
        in_specs=[*_GLU_SPECS, pl.BlockSpec((T, C), lambda i: (0, 0)), pl.BlockSpec((T, C), lambda i: (0, 0)), wspec,
                  vec, vec, vec, *[ANY] * len(c_in)],
        out_specs=[pl.BlockSpec((T, 2 * C), lambda i: (0, 0)), wspec, vec, vec, vec, *[ANY] * len(c_out)],
        out_shape=[_sds((T, 2 * C), BF), _sds((CONV_PAD, C), F32), _sds((1, C), F32), _sds((1, C), F32),
                   _sds((1, C), F32), *c_out],
        scratch_shapes=[pltpu.VMEM((PAD_ROWS, C), F32), pltpu.VMEM((PAD_ROWS, C), F32), pltpu.VMEM((8, WIN, C), F32),
                        pltpu.VMEM((CHUNK, C), F32), pltpu.VMEM((8 * CONV_PAD, C), F32), *c_sems],
        compiler_params=_params(("arbitrary",)),
    )(proj, proj, proj, proj, u, d_c, conv_w, conv_b, ln_g, ln_b, *c_in)
    return res[:5], res[5:]


_GATE_BLK = GATE_OFF // 256


def _ffn_in_swiglu(h2, wf_t, carry=None):
    T, D = h2.shape
    tm, tn = 1024, D_FF // 2
    nj, ni = D_FF // tn, T // tm
    c_in, c_out, c_sems = _carry_io(carry)

    def body(*refs):
        a_ref, bg_ref, bu_ref = refs[:3]
        ci_refs = refs[3:3 + len(c_in)]
        act_ref, g_ref, u_ref = refs[3 + len(c_in):6 + len(c_in)]
        co_refs = refs[6 + len(c_in):6 + len(c_in) + len(c_out)]
        cs_refs = refs[6 + len(c_in) + len(c_out):]
        j, i = pl.program_id(0), pl.program_id(1)
        if carry is not None:
            @pl.when((j == 0) & (i == 0))
            def _():
                carry.start(ci_refs, co_refs, cs_refs)
        a = a_ref[...]
        for c0, c1 in ((0, 768), (768, tn)):
            g = lax.dot_general(a, bg_ref[c0:c1, :], _DIMS["NT"], preferred_element_type=F32)
            u = lax.dot_general(a, bu_ref[c0:c1, :], _DIMS["NT"], preferred_element_type=F32)
            act_ref[:, c0:c1] = (g * jax.nn.sigmoid(g) * u).astype(BF)
            g_ref[:, c0:c1] = g.astype(BF)
            u_ref[:, c0:c1] = u.astype(BF)
        if carry is not None:
            @pl.when((j == nj - 1) & (i == ni - 1))
            def _():
                carry.finish(ci_refs, co_refs, cs_refs)

    t = pl.BlockSpec((tm, tn), lambda j, i: (i, j))
    res = pl.pallas_call(
        body, name="ffn_in_swiglu", grid=(nj, ni),
        in_specs=[pl.BlockSpec((tm, D), lambda j, i: (i, 0)), pl.BlockSpec((tn, D), lambda j, i: (j, 0)),
                  pl.BlockSpec((tn, D), lambda j, i: (nj + j, 0)), *[ANY] * len(c_in)],
        out_specs=[t, t, t, *[ANY] * len(c_out)], out_shape=[*[_sds((T, D_FF), BF)] * 3, *c_out],
        scratch_shapes=c_sems,
        compiler_params=_params(("arbitrary", "arbitrary")),
    )(h2, wf_t, wf_t, *c_in)
    return res[:3], res[3:]


def _proj_merge(o, c, wap_t, wcp_t, b_cp, proj):
    T, D = o.shape[0], wap_t.shape[0]
    tm, tg = 1024, 256
    nj = D // tg

    def body(o_ref, c_ref, wa_ref, wc_ref, b_ref, g0_ref, g1_ref, ya_ref, yc_ref, m_ref):
        ya = lax.dot_general(o_ref[...], wa_ref[...], _DIMS["NT"], preferred_element_type=F32)
        yc = lax.dot_general(c_ref[...], wc_ref[...], _DIMS["NT"], preferred_element_type=F32) + b_ref[...]
        ya_ref[...] = ya.astype(BF)
        yc_ref[...] = yc.astype(BF)
        m_ref[...] = (jax.nn.sigmoid(g0_ref[...].astype(F32)) * ya + jax.nn.sigmoid(g1_ref[...].astype(F32)) * yc).astype(BF)

    act = pl.BlockSpec((tm, o.shape[1]), lambda j, i: (i, 0))
    wgt = pl.BlockSpec((tg, o.shape[1]), lambda j, i: (j, 0))
    t = pl.BlockSpec((tm, tg), lambda j, i: (i, j))
    return pl.pallas_call(
        body, name="proj_merge", grid=(nj, T // tm),
        in_specs=[act, act, wgt, wgt, pl.BlockSpec((1, tg), lambda j, i: (0, j)),
                  pl.BlockSpec((tm, tg), lambda j, i: (i, _GATE_BLK + j)),
                  pl.BlockSpec((tm, tg), lambda j, i: (i, _GATE_BLK + nj + j))],
        out_specs=[t, t, t], out_shape=[_sds((T, D), BF)] * 3,
        compiler_params=_params(("arbitrary", "arbitrary")),
    )(o, c, wap_t, wcp_t, b_cp, proj, proj)


def _proj_in_dw(segs, h):
    T, D = h.shape
    tb = 256
    nblk = [seg.shape[1] // tb for seg in segs]
    starts = [sum(nblk[:q]) for q in range(len(segs))]
    n_seg = len(segs)

    def body(*refs):
        seg_refs, h_ref, o_ref, cs_ref = refs[:n_seg], refs[n_seg], refs[n_seg + 1], refs[n_seg + 2]
        i = pl.program_id(0)
        for seg_ref, st, nb in zip(seg_refs, starts, nblk):
            @pl.when((i >= st) & (i < st + nb))
            def _(seg_ref=seg_ref):
                a = seg_ref[...]
                o_ref[...] = lax.dot_general(a, h_ref[...], _DIMS["TN"], preferred_element_type=F32).astype(BF)
                cs_ref[...] = jnp.sum(a.astype(F32), axis=0, keepdims=True)

    in_specs = [pl.BlockSpec((T, tb), functools.partial(lambda i, st, nb: (0, jnp.clip(i - st, 0, nb - 1)), st=st, nb=nb))
                for st, nb in zip(starts, nblk)]
    return pl.pallas_call(
        body, name="proj_in_dw", grid=(sum(nblk),),
        in_specs=[*in_specs, pl.BlockSpec((T, D), lambda i: (0, 0))],
        out_specs=[pl.BlockSpec((tb, D), lambda i: (i, 0)), pl.BlockSpec((1, tb), lambda i: (0, i))],
        out_shape=[_sds((sum(nblk) * tb, D), BF), _sds((1, sum(nblk) * tb), F32)],
        compiler_params=_params(("arbitrary",)),
    )(*segs, h)


def _local_step(x, target, small, wi_t, conv_w, plan):
    T, D = x.shape
    tm = 1024

    def carried(call, res, carry):
        if carry is None:
            return res
        outs, got = res
        plan.done(call, got)
        return outs

    h, r1 = _rms_fwd("rms_mix", x, small["g_mix_norm"])

    def ep_add(acc, ex, outs, ids, scr):
        outs[0][...] = acc + ex[0][...]

    tn_in = IN_WIDTH // 2
    carry = plan.carry("proj_in")
    def ep_bias_bf16(acc, ex, outs, ids, scr):
        outs[0][...] = (acc + ex[0][...]).astype(BF)

    proj, = carried("proj_in", _matmul("proj_in", [h], wi_t, "NT", m=T, n=IN_WIDTH, tm=tm, tn=tn_in,
                                       epilogue=ep_bias_bf16, extra=[(small["b_in"], _row(tn_in))],
                                       outs=[(_sds((T, IN_WIDTH), BF), _tile(tm, tn_in))], carry=carry), carry)
    plan.launch("gather_ffn", after=proj)
    o, got = _attn_fwd(proj, small["sinks"], carry=plan.carry("attn_fwd"))
    plan.done("attn_fwd", got)
    (c, u_conv), got = _conv_fwd(proj, conv_w, small["conv_b"], small["ln_g"], small["ln_b"],
                                 carry=plan.carry("conv_fwd"))
    plan.done("conv_fwd", got)
    wap_t, wcp_t, w_out = plan.weight("w_attn_proj"), plan.weight("w_conv_proj"), plan.weight("w_out")
    ya, yc, merged = _proj_merge(o, c, wap_t, wcp_t, small["b_conv_proj"], proj)

    tg = 256
    gate_specs = [pl.BlockSpec((tm, tg), lambda j, i, k: (i, _GATE_BLK + j)),
                  pl.BlockSpec((tm, tg), lambda j, i, k: (i, _GATE_BLK + D // tg + j))]

    def ep_residual_rms(acc, ex, outs, ids, scr):
        x2v = acc + ex[0][...]
        r = lax.rsqrt(jnp.mean(x2v * x2v, axis=-1, keepdims=True) + EPS)
        outs[0][...] = x2v
        outs[1][...] = (x2v * r * ex[1][...]).astype(BF)
        outs[2][...] = r

    carry = plan.carry("out_proj")
    x2, h2, r2 = carried("out_proj", _matmul(
        "out_proj_rms", [merged], w_out, "NN", m=T, n=D, tm=512, tn=D, epilogue=ep_residual_rms,
        extra=[(x, _tile(512, D)), (small["g_ffn_norm"], _row(D))],
        outs=[(_sds((T, D), F32), _tile(512, D)), (_sds((T, D), BF), _tile(512, D)),
              (_sds((T, 1), F32), pl.BlockSpec((512, 1), lambda j, i, k: (i, 0)))], carry=carry), carry)
    plan.launch("gather_down", after=x2)
    wf_t = plan.weight("w_ffn_in")
    (act, gate, up), got = _ffn_in_swiglu(h2, wf_t, carry=plan.carry("ffn_in_swiglu"))
    plan.done("ffn_in_swiglu", got)
    w_down = plan.weight("w_ffn_down")
    def ep_residual_loss(acc, ex, outs, ids, scr):
        dx, dg, part = _loss_head(acc + ex[0][...], ex[1][...], ex[2][...])
        outs[0][...] = dx
        outs[1][...] = dx.astype(BF)
        _accumulate_rows(outs[2], dg, ids[1] == 0)
        _accumulate_rows(outs[3], part, ids[1] == 0)

    dx3, dx3_b, dg_final, loss = _matmul(
        "ffn_down_loss", [act], w_down, "NN", m=T, n=D, tm=512, tn=D, epilogue=ep_residual_loss,
        extra=[(x2, _tile(512, D)), (small["g_final"], _row(D)), (target, _tile(512, D))],
        outs=[(_sds((T, D), F32), _tile(512, D)), (_sds((T, D), BF), _tile(512, D)), (_sds((1, D), F32), _row(D)),
              (_sds((1, 1), F32), pl.BlockSpec((1, 1), lambda j, i, k: (0, 0)))])

    tn_ff = D_FF // 2

    def ep_swiglu_bwd(acc, ex, outs, ids, scr):
        g, u = ex[0][...].astype(F32), ex[1][...].astype(F32)
        sg = jax.nn.sigmoid(g)
        outs[0][...] = (acc * u * sg * (1.0 + g * (1.0 - sg))).astype(BF)
        outs[1][...] = (acc * g * sg).astype(BF)

    dgate, dup = _matmul(
        "ffn_down_bwd", [dx3_b], w_down, "NT", m=T, n=D_FF, tm=tm, tn=tn_ff, epilogue=ep_swiglu_bwd,
        extra=[(gate, _tile(tm, tn_ff)), (up, _tile(tm, tn_ff))],
        outs=[(_sds((T, D_FF), BF), _tile(tm, tn_ff)), (_sds((T, D_FF), BF), _tile(tm, tn_ff))])

    def dw(name, a, b, rows, cols, row_off=0, alias=None, total_rows=None, colsum=False):
        total_rows = rows if total_rows is None else total_rows
        tmw = rows if rows <= 1024 else D_FF // 2
        blk, rem = divmod(row_off, tmw)
        assert rem == 0

        def ep(acc, ex, outs, ids, scr):
            outs[0][...] = acc.astype(BF)
            if colsum:
                outs[1][...] = jnp.sum(ex[0][...].astype(F32), axis=0, keepdims=True)

        outs = [(_sds((total_rows, cols), BF), pl.BlockSpec((tmw, cols), lambda j, i, k: (blk + i, j)))]
        extra = []
        if colsum:
            extra = [(a, pl.BlockSpec((T, tmw), lambda j, i, k: (0, i)))]
            outs.append((_sds((1, rows), F32), pl.BlockSpec((1, tmw), lambda j, i, k: (0, i))))
        carry = plan.carry(name)
        res = carried(name, _matmul(name, [a], b, "TN", m=rows, n=cols, tm=tmw, tn=cols, epilogue=ep, extra=extra,
                                    outs=outs, alias=None if alias is None else (alias, 0), carry=carry), carry)
        return res if colsum else res[0]

    plan.grad_ready(dict(w_ffn_down=dw("ffn_down_dw", act, dx3_b, D_FF, D)))

    def ep_rms_bwd(acc, ex, outs, ids, scr):
        dx, dg = _rms_bwd(acc, ex[0][...], ex[1][...], ex[2][...])
        dx = ex[3][...] + dx
        outs[0][...] = dx
        outs[1][...] = dx.astype(BF)
        _accumulate_rows(outs[2], dg, ids[1] == 0)

    def rms_bwd_io(tm_, xin, r, g, dres):
        return dict(
            extra=[(xin, _tile(tm_, D)), (r, pl.BlockSpec((tm_, 1), lambda j, i, k: (i, 0))), (g, _row(D)),
                   (dres, _tile(tm_, D))],
            outs=[(_sds((T, D), F32), _tile(tm_, D)), (_sds((T, D), BF), _tile(tm_, D)), (_sds((1, D), F32), _row(D))])

    carry = plan.carry("ffn_in_bwd")
    dx2, dx2_b, dg_ffn = carried(
        "ffn_in_bwd",
        _matmul("ffn_in_bwd", [dgate, dup], wf_t, "NN", m=T, n=D, tm=tm, tn=D, tk=D_FF // 2, epilogue=ep_rms_bwd,
                carry=carry, **rms_bwd_io(tm, x2, r2, small["g_ffn_norm"], dx3)), carry)
    plan.launch("send_down")
    gwf_t = dw("ffn_in_dw_gate", dgate, h2, D_FF, D, total_rows=2 * D_FF)
    gwf_t = dw("ffn_in_dw_up", dup, h2, D_FF, D, row_off=D_FF, alias=gwf_t, total_rows=2 * D_FF)
    plan.grad_ready(dict(w_ffn_in=gwf_t))

    def ep_merge_bwd(acc, ex, outs, ids, scr):
        s0 = jax.nn.sigmoid(ex[2][...].astype(F32))
        s1 = jax.nn.sigmoid(ex[3][...].astype(F32))
        outs[0][...] = (acc * s0).astype(BF)
        outs[1][...] = (acc * s1).astype(BF)
        outs[2][...] = (acc * ex[0][...].astype(F32) * s0 * (1.0 - s0)).astype(BF)
        outs[3][...] = (acc * ex[1][...].astype(F32) * s1 * (1.0 - s1)).astype(BF)

    carry = plan.carry("out_proj_bwd_merge")
    dya, dyc, dg0, dg1 = carried(
        "out_proj_bwd_merge",
        _matmul("out_proj_bwd_merge", [dx2_b], w_out, "NT", m=T, n=D, tm=tm, tn=tg, epilogue=ep_merge_bwd,
                extra=[(ya, _tile(tm, tg)), (yc, _tile(tm, tg)), (proj, gate_specs[0]), (proj, gate_specs[1])],
                outs=[(_sds((T, D), BF), _tile(tm, tg))] * 4, carry=carry), carry)
    plan.launch("send_ffn")
    gw_out = dw("out_proj_dw", merged, dx2_b, D, D)
    d_o, = _matmul("attn_proj_bwd", [dya], wap_t, "NN", m=T, n=ATTN_WIDTH, tm=tm, tn=ATTN_WIDTH,
                   epilogue=_store(BF), outs=[(_sds((T, ATTN_WIDTH), BF), _tile(tm, ATTN_WIDTH))])
    d_c, = _matmul("conv_proj_bwd", [dyc], wcp_t, "NN", m=T, n=CONV_CHANNELS, tm=tm, tn=CONV_CHANNELS,
                   epilogue=_store(BF), outs=[(_sds((T, CONV_CHANNELS), BF), _tile(tm, CONV_CHANNELS))])
    gwap_t = dw("attn_proj_dw", dya, o, D, ATTN_WIDTH)
    gwcp_t, db_cp = dw("conv_proj_dw", dyc, c, D, CONV_CHANNELS, colsum=True)
    plan.grad_ready(dict(w_out=gw_out, w_attn_proj=gwap_t, w_conv_proj=gwcp_t))
    (dglu, dcw, dcb, dlng, dlnb), got = _conv_bwd(proj, u_conv, d_c, conv_w, small["conv_b"], small["ln_g"],
                                                  small["ln_b"], carry=plan.carry("conv_bwd"))
    plan.done("conv_bwd", got)
    plan.launch("send_mix")
    (dqkv, dsinks), got = _attn_bwd(proj, d_o, small["sinks"], carry=plan.carry("attn_bwd"))
    plan.done("attn_bwd", got)

    segs = [dqkv, dglu, dg0, dg1]
    gwi_t, db_in = _proj_in_dw(segs, h)
    plan.grad_ready(dict(w_in=gwi_t))
    plan.alone("swap_inp")
    plan.launch("send_inp")
    carry = plan.carry("proj_in_bwd")
    dx, _, dg_mix = carried(
        "proj_in_bwd",
        _matmul("proj_in_bwd", segs, wi_t, "NN", m=T, n=D, tm=512, tn=D, epilogue=ep_rms_bwd, carry=carry,
                **rms_bwd_io(512, x, r1, small["g_mix_norm"], plan.behind("inp", dx2))), carry)

    parts = dict(g_mix_norm=dg_mix, b_in=db_in, sinks=dsinks, conv_w=dcw, conv_b=dcb, ln_g=dlng, ln_b=dlnb,
                 b_conv_proj=db_cp, g_ffn_norm=dg_ffn, g_final=dg_final, loss=loss)
    return dx, parts


def _place():
    x, y, c = lax.axis_index("x"), lax.axis_index("y"), lax.axis_index("c")
    return x, y, c, [(1 - x, y), (x, 1 - y), (1 - x, 1 - y)]


def _gather_copies(x_refs, out_refs, rows_per, send_sems, recv_sems, local_sems):
    x, y, c, chips = _place()
    me, sibling = (x, y, c), (x, y, 1 - c)

    def rows(a, px, py, pc):
        return out_refs[a].at[pl.ds((4 * px + 2 * py + pc) * rows_per[a], rows_per[a])]

    def copy(a, k, block, to, src=None):
        return pltpu.make_async_remote_copy(
            src_ref=rows(a, *block) if src is None else src, dst_ref=rows(a, *block),
            send_sem=send_sems.at[7 * a + k], recv_sem=recv_sems.at[7 * a + k], device_id=to, device_id_type=MESH)

    def local(a):
        return pltpu.make_async_copy(x_refs[a], rows(a, *me), local_sems.at[a])

    def first(a):
        return [copy(a, 0, me, sibling, src=x_refs[a])] + [copy(a, 1 + j, me, (*chip, c), src=x_refs[a])
                                                          for j, chip in enumerate(chips)]

    def arrive(a, j):
        return copy(a, 1 + j, (*chips[j], c), me)

    def passed(a, j):
        return copy(a, 4 + j, (*chips[j], c), sibling)

    def from_sibling(a):
        return [copy(a, 0, sibling, me)] + [copy(a, 4 + j, (*chip, 1 - c), me) for j, chip in enumerate(chips)]

    return len(x_refs), local, first, arrive, passed, from_sibling


def _gather_start(*refs):
    n, local, first, _, _, _ = _gather_copies(*refs)
    for a in range(n):
        local(a).start()
        for cp in first(a):
            cp.start()


def _gather_finish(*refs):
    n, local, first, arrive, passed, from_sibling = _gather_copies(*refs)
    for a in range(n):
        for j in range(3):
            arrive(a, j).wait_recv()
            passed(a, j).start()
    for a in range(n):
        for cp in from_sibling(a):
            cp.wait_recv()
    for a in range(n):
        for cp in first(a) + [passed(a, j) for j in range(3)]:
            cp.wait_send()
        local(a).wait()


def _gather_peers():
    x, y, c, chips = _place()
    return [(x, y, 1 - c)] + [(*chip, c) for chip in chips]


def _gather_sems(n):
    return [pltpu.SemaphoreType.DMA((7 * n,)), pltpu.SemaphoreType.DMA((7 * n,)), pltpu.SemaphoreType.DMA((n,))]


def _gather_carry(shards):
    rows_per = [s.shape[0] for s in shards]
    return _Carry(shards, [_sds((N_DEV * s.shape[0],) + s.shape[1:], s.dtype) for s in shards],
                  _gather_sems(len(shards)),
                  lambda ins, outs, sems: _gather_start(ins, outs, rows_per, *sems),
                  lambda ins, outs, sems: _gather_finish(ins, outs, rows_per, *sems), _gather_peers)


def _first_gather(shards):
    n = len(shards)
    rows_per = [s.shape[0] for s in shards]

    def body(*refs):
        x_refs, out_refs = refs[:n], refs[n:2 * n]
        send_sems, recv_sems, local_sems = refs[2 * n:]
        x, y, c, chips = _place()
        me, sibling = (x, y, c), (x, y, 1 - c)
        near_x, near_y, far = (*chips[0], c), (*chips[1], c), (*chips[2], c)

        def rows(a, dev, part):
            h = rows_per[a] // 2
            lo, size = {"all": (0, 2 * h), "low": (0, h), "high": (h, h)}[part]
            return out_refs[a].at[pl.ds((4 * dev[0] + 2 * dev[1] + dev[2]) * rows_per[a] + lo, size)]

        def copy(a, k, block, part, to, src=None):
            return pltpu.make_async_remote_copy(
                src_ref=rows(a, block, part) if src is None else src, dst_ref=rows(a, block, part),
                send_sem=send_sems.at[9 * a + k], recv_sem=recv_sems.at[9 * a + k], device_id=to, device_id_type=MESH)

        other = lambda dev: (dev[0], dev[1], 1 - c)
        sent = []
        for a in range(n):
            pltpu.make_async_copy(x_refs[a], rows(a, me, "all"), local_sems.at[a]).start()
            sent += [copy(a, 0, me, "all", sibling, src=x_refs[a]), copy(a, 1, me, "all", near_x, src=x_refs[a]),
                     copy(a, 2, me, "all", near_y, src=x_refs[a])]
        for cp in sent:
            cp.start()
        for a in range(n):
            copy(a, 1, near_x, "all", me).wait_recv()
            copy(a, 2, near_y, "all", me).wait_recv()
            passed = [copy(a, 3, near_y, "high", near_x), copy(a, 4, near_x, "low", near_y),
                      copy(a, 5, near_x, "all", sibling), copy(a, 6, near_y, "all", sibling)]
            for cp in passed:
                cp.start()
            sent += passed
        for a in range(n):
            copy(a, 3, far, "high", me).wait_recv()
            copy(a, 4, far, "low", me).wait_recv()
            passed = [copy(a, 7, far, "high", sibling), copy(a, 8, far, "low", sibling)]
            for cp in passed:
                cp.start()
            sent += passed
        for a in range(n):
            copy(a, 0, sibling, "all", me).wait_recv()
            copy(a, 5, other(near_x), "all", me).wait_recv()
            copy(a, 6, other(near_y), "all", me).wait_recv()
            copy(a, 7, other(far), "high", me).wait_recv()
            copy(a, 8, other(far), "low", me).wait_recv()
        for cp in sent:
            cp.wait_send()
        for a in range(n):
            pltpu.make_async_copy(x_refs[a], rows(a, me, "all"), local_sems.at[a]).wait()

    return pl.pallas_call(
        body, name="weights_first_gather", in_specs=[ANY] * n, out_specs=[ANY] * n,
        out_shape=[_sds((N_DEV * s.shape[0],) + s.shape[1:], s.dtype) for s in shards],
        scratch_shapes=[pltpu.SemaphoreType.DMA((9 * n,)), pltpu.SemaphoreType.DMA((9 * n,)),
                        pltpu.SemaphoreType.DMA((n,))],
    )(*shards)


def _swap_carry(grads):
    n = len(grads)

    def copies(g_refs, out_refs, sems):
        send_sems, recv_sems = sems
        x, y, c, _ = _place()
        return [pltpu.make_async_remote_copy(
            src_ref=g_refs[a].at[2 * p + 1 - c], dst_ref=out_refs[a].at[p],
            send_sem=send_sems.at[4 * a + p], recv_sem=recv_sems.at[4 * a + p],
            device_id=(x, y, 1 - c), device_id_type=MESH) for a in range(n) for p in range(4)]

    def start(ins, outs, sems):
        for cp in copies(ins, outs, sems):
            cp.start()

    def finish(ins, outs, sems):
        for cp in copies(ins, outs, sems):
            cp.wait()

    def peers():
        x, y, c, _ = _place()
        return [(x, y, 1 - c)]

    return _Carry(grads, [_sds((4,) + g.shape[1:], g.dtype) for g in grads],
                  [pltpu.SemaphoreType.DMA((4 * n,)), pltpu.SemaphoreType.DMA((4 * n,))], start, finish, peers)


def _join(carries):
    carries = [c for c in carries if c is not None]
    if not carries:
        return None
    n_in = [len(c.arrays) for c in carries]
    n_out = [len(c.out_shapes) for c in carries]
    n_sem = [len(c.sems) for c in carries]

    def parts(refs, counts):
        cuts = [sum(counts[:q]) for q in range(len(counts) + 1)]
        return [refs[cuts[q]:cuts[q + 1]] for q in range(len(counts))]

    def start(ins, outs, sems):
        for c, i, o, s in zip(carries, parts(ins, n_in), parts(outs, n_out), parts(sems, n_sem)):
            c.start(i, o, s)

    def finish(ins, outs, sems):
        for c, i, o, s in zip(carries, parts(ins, n_in), parts(outs, n_out), parts(sems, n_sem)):
            c.finish(i, o, s)

    return _Carry([a for c in carries for a in c.arrays], [o for c in carries for o in c.out_shapes],
                  [s for c in carries for s in c.sems], start, finish)


def _run_carry(name, carry):
    n_in, n_out = len(carry.arrays), len(carry.out_shapes)

    def body(*refs):
        carry.start(refs[:n_in], refs[n_in:n_in + n_out], refs[n_in + n_out:])
        carry.finish(refs[:n_in], refs[n_in:n_in + n_out], refs[n_in + n_out:])

    return pl.pallas_call(body, name=name, in_specs=[ANY] * n_in, out_specs=[ANY] * n_out,
                          out_shape=carry.out_shapes, scratch_shapes=carry.sems)(*carry.arrays)


def _run_carry_async(name, carry, collective_id):
    ins = [jax.new_ref(a, memory_space=pltpu.MemorySpace.HBM) for a in carry.arrays]
    outs = [jax.empty_ref(o, memory_space=pltpu.MemorySpace.HBM) for o in carry.out_shapes]

    @pl.kernel(mesh=plsc.ScalarSubcoreMesh(axis_name="sequencer", num_cores=1), name=name,
               scratch_types=tuple(carry.sems), compiler_params=pltpu.CompilerParams(collective_id=collective_id))
    def launch(*sems):
        barrier = pltpu.get_barrier_semaphore()
        peers = carry.peers()
        for peer in peers:
            pl.semaphore_signal(barrier, inc=1, device_id=peer, device_id_type=MESH)
        pl.semaphore_wait(barrier, len(peers))
        carry.start(ins, outs, sems)
        carry.finish(ins, outs, sems)

    launch()
    return [o[...] for o in outs]


def _chip_sums(name, gs, gots, c):
    n = len(gs)

    def body(c_ref, *refs):
        for g_ref, got_ref, o_ref in zip(refs[:n], refs[n:2 * n], refs[2 * n:]):
            o_ref[...] = (g_ref[...].astype(F32) + got_ref[...].astype(F32)).astype(BF)

    mine = [pl.BlockSpec((1,) + g.shape[1:], lambda p, c_ref: (2 * p + c_ref[0], 0, 0)) for g in gs]
    slot = [pl.BlockSpec((1,) + g.shape[1:], lambda p, c_ref: (p, 0, 0)) for g in gs]
    return pl.pallas_call(
        body, name=name,
        grid_spec=pltpu.PrefetchScalarGridSpec(num_scalar_prefetch=1, grid=(4,), in_specs=[*mine, *slot],
                                               out_specs=slot),
        out_shape=[_sds((4,) + g.shape[1:], BF) for g in gs],
        compiler_params=_params(("arbitrary",)),
    )(c, *gs, *gots)


def _send_carry(sums, ks):
    n, nk = len(sums), len(ks)

    def copies(s_refs, out_refs, sems):
        send_sems, recv_sems = sems
        x, y, c, chips = _place()
        return [pltpu.make_async_remote_copy(
            src_ref=s_refs[a].at[2 * chips[k][0] + chips[k][1]], dst_ref=out_refs[a].at[q],
            send_sem=send_sems.at[nk * a + q], recv_sem=recv_sems.at[nk * a + q],
            device_id=(*chips[k], c), device_id_type=MESH) for a in range(n) for q, k in enumerate(ks)]

    def start(ins, outs, sems):
        for cp in copies(ins, outs, sems):
            cp.start()

    def finish(ins, outs, sems):
        for cp in copies(ins, outs, sems):
            cp.wait()

    def peers():
        x, y, c, chips = _place()
        return [(*chips[k], c) for k in ks]

    return _Carry(sums, [_sds((nk,) + s.shape[1:], s.dtype) for s in sums],
                  [pltpu.SemaphoreType.DMA((nk * n,)), pltpu.SemaphoreType.DMA((nk * n,))], start, finish, peers)


def _adam_math(w, g, m, v):
    m = ADAM_B1 * m + (1.0 - ADAM_B1) * g
    v = ADAM_B2 * v + (1.0 - ADAM_B2) * (g * g)
    m_hat = m / (1.0 - ADAM_B1 ** ADAM_STEP)
    v_hat = v / (1.0 - ADAM_B2 ** ADAM_STEP)
    delta = -ADAM_LR * (m_hat / (jnp.sqrt(v_hat) + ADAM_EPS) + ADAM_WD * w)
    return delta, m, v


def _adamw(name, w, g, m, v):
    rows, cols = w.shape
    tr = 256 if rows % 256 == 0 else rows

    def body(w_ref, g_ref, m_ref, v_ref, d_ref, nm_ref, nv_ref):
        d_ref[...], nm_ref[...], nv_ref[...] = _adam_math(w_ref[...], g_ref[...], m_ref[...], v_ref[...])

    t = pl.BlockSpec((tr, cols), lambda i: (i, 0))
    return pl.pallas_call(
        body, name=name, grid=(rows // tr,), in_specs=[t] * 4, out_specs=[t] * 3,
        out_shape=[_sds((rows, cols), F32)] * 3, compiler_params=_params(("arbitrary",)),
    )(w, g, m, v)


def _grad_adamw(name, g, got, got3, ids, w, m, v):
    _, rows, cols = g.shape
    n3 = len(got3)
    tr = rows // 2 if rows >= 256 else rows

    def body(ids_ref, g_ref, got_ref, *rest):
        w_ref, m_ref, v_ref, o_ref, d_ref, nm_ref, nv_ref = rest[n3:]
        tot = g_ref[0].astype(F32) + got_ref[0].astype(F32)
        for r_ref in rest[:n3]:
            for q in range(r_ref.shape[0]):
                tot = tot + r_ref[q].astype(F32)
        o_ref[...] = tot
        d_ref[...], nm_ref[...], nv_ref[...] = _adam_math(w_ref[...], tot, m_ref[...], v_ref[...])

    tile = pl.BlockSpec((tr, cols), lambda i, ids_ref: (i, 0))
    return pl.pallas_call(
        body, name=name,
        grid_spec=pltpu.PrefetchScalarGridSpec(
            num_scalar_prefetch=1, grid=(rows // tr,),
            in_specs=[pl.BlockSpec((1, tr, cols), lambda i, ids_ref: (ids_ref[0], i, 0)),
                      pl.BlockSpec((1, tr, cols), lambda i, ids_ref: (ids_ref[1], i, 0)),
                      *[pl.BlockSpec((r.shape[0], tr, cols), lambda i, ids_ref: (0, i, 0)) for r in got3],
                      tile, tile, tile],
            out_specs=[tile] * 4),
        out_shape=[_sds((rows, cols), F32)] * 4,
        compiler_params=_params(("arbitrary",)),
    )(ids, g, got, *got3, w, m, v)


SMALL_NAMES = ["g_mix_norm", "b_in", "sinks", "conv_b", "ln_g", "ln_b", "b_conv_proj", "g_ffn_norm", "g_final"]
_PACK_ROWS = 32


def _small_pack(parts):
    C = CONV_CHANNELS
    part_list = [parts["g_mix_norm"], parts["b_in"], parts["sinks"], parts["conv_b"], parts["ln_g"], parts["ln_b"],
                 parts["b_conv_proj"], parts["g_ffn_norm"], parts["g_final"], parts["loss"], parts["conv_w"]]

    def body(p_mix, p_b, p_sink, p_cb, p_lg, p_lb, p_bcp, p_ffn, p_fin, p_loss, p_cw, pack):
        pack[...] = jnp.zeros_like(pack)
        pack[0:1, :] = p_mix[...]
        pack[1:2, 0:GLU_OFF] = p_b[:, 0:GLU_OFF]
        pack[2:3, :] = p_b[:, GLU_OFF:GATE_OFF]
        pack[3:4, :] = p_b[:, GATE_OFF:GATE_OFF + D_MODEL]
        pack[4:5, :] = p_b[:, GATE_OFF + D_MODEL:]
        pack[5:6, 0:128] = p_sink[...]
        pack[6:7, 0:C] = p_cb[...]
        pack[6:7, C:2 * C] = p_lg[...]
        pack[7:8, 0:C] = p_lb[...]
        pack[8:9, :] = p_bcp[...]
        pack[9:10, :] = p_ffn[...]
        pack[10:11, :] = p_fin[...]
        pack[11:12, 0:128] = jnp.broadcast_to(p_loss[...], (1, 128))
        pack[12:28, 0:C] = p_cw[0:16, :]
        pack[12:28, C:2 * C] = p_cw[16:32, :]

    vm = pl.BlockSpec(memory_space=pltpu.VMEM)
    return pl.pallas_call(body, name="small_pack", in_specs=[vm] * len(part_list), out_specs=vm,
                          out_shape=_sds((_PACK_ROWS, D_MODEL), F32))(*part_list)


def _small_adamw(gathered, small_w, small_m, small_v):
    C = CONV_CHANNELS
    names = SMALL_NAMES
    widths = [small_w[k].shape[1] for k in names]
    n_small = len(names)

    def body(*refs):
        tot_ref = refs[0]
        w_refs = refs[1:1 + n_small]
        m_refs = refs[1 + n_small:1 + 2 * n_small]
        v_refs = refs[1 + 2 * n_small:1 + 3 * n_small]
        o = 1 + 3 * n_small
        loss_ref, cw_ref = refs[o], refs[o + 1]
        out_refs = refs[o + 2:o + 2 + 4 * n_small]
        tot = tot_ref[0:_PACK_ROWS, :]
        for d in range(1, N_DEV):
            tot = tot + tot_ref[d * _PACK_ROWS:(d + 1) * _PACK_ROWS, :]
        loss_ref[...] = tot[11:12, 0:1]
        cw_ref[0:16, :] = tot[12:28, 0:C]
        cw_ref[16:32, :] = tot[12:28, C:2 * C]
        grads = dict(
            g_mix_norm=tot[0:1, :],
            b_in=jnp.concatenate([tot[1:2, 0:GLU_OFF], tot[2:3, :], tot[3:4, :], tot[4:5, :]], axis=1),
            sinks=tot[5:6, 0:N_Q_HEADS], conv_b=tot[6:7, 0:C], ln_g=tot[6:7, C:2 * C], ln_b=tot[7:8, 0:C],
            b_conv_proj=tot[8:9, :], g_ffn_norm=tot[9:10, :], g_final=tot[10:11, :])
        for s, k in enumerate(names):
            g = grads[k]
            d, nm, nv = _adam_math(w_refs[s][...], g, m_refs[s][...], v_refs[s][...])
            out_refs[4 * s][...] = g
            out_refs[4 * s + 1][...] = d
            out_refs[4 * s + 2][...] = nm
            out_refs[4 * s + 3][...] = nv

    vm = pl.BlockSpec(memory_space=pltpu.VMEM)
    args = [gathered, *[small_w[k] for k in names], *[small_m[k] for k in names], *[small_v[k] for k in names]]
    out_shape = [_sds((1, 1), F32), _sds((CONV_PAD, C), F32)]
    for wd in widths:
        out_shape += [_sds((1, wd), F32)] * 4
    res = pl.pallas_call(
        body, name="small_adamw",
        in_specs=[vm] * len(args), out_specs=[vm] * len(out_shape), out_shape=out_shape,
        compiler_params=pltpu.CompilerParams(vmem_limit_bytes=VMEM_LIMIT_BYTES),
    )(*args)
    return res[0], res[1], {k: res[2 + 4 * s:6 + 4 * s] for s, k in enumerate(names)}


BIG = dict(w_in=True, w_attn_proj=True, w_conv_proj=True, w_out=False, w_ffn_in=True, w_ffn_down=False)
WEIGHT_NAMES = ["g_mix_norm", "w_in", "b_in", "sinks", "conv_w", "conv_b", "ln_g", "ln_b", "w_attn_proj",
                "w_conv_proj", "b_conv_proj", "w_out", "g_ffn_norm", "w_ffn_in", "w_ffn_down", "g_final"]


class _Plan:
    GROUPS = dict(down=["w_ffn_down"], ffn=["w_ffn_in"], mix=["w_out", "w_attn_proj", "w_conv_proj"], inp=["w_in"])
    ALL = (0, 1, 2)
    RIDES = dict(
        gather_mix=[("gather", ["w_attn_proj", "w_conv_proj", "w_out"])], gather_ffn=[("gather", ["w_ffn_in"])],
        gather_down=[("gather", ["w_ffn_down"])],
        ffn_in_bwd=[("swap", "down")], send_down=[("send", "down", ALL)],
        out_proj_bwd_merge=[("swap", "ffn")], send_ffn=[("send", "ffn", ALL)],
        conv_bwd=[("swap", "mix")], send_mix=[("send", "mix", ALL)],
        swap_inp=[("swap", "inp")], send_inp=[("send", "inp", ALL)])
    ASYNC = dict(gather_mix=1, gather_ffn=2, gather_down=3, send_down=4, send_ffn=5, send_mix=6, send_inp=7)

    def __init__(self, shards, c1):
        self.shards, self.c1 = shards, c1
        self.full, self.slots, self.got, self.sums, self.got3 = {}, {}, {}, {}, {}

    def weight(self, name):
        return self.full[name]

    def grad_ready(self, grads):
        for k, g in grads.items():
            self.slots[k] = g.reshape(N_DEV, g.shape[0] // N_DEV, g.shape[1])

    def _one(self, kind, what, ks=None):
        if kind == "gather":
            return _gather_carry([self.shards[k] for k in what])
        names = self.GROUPS[what]
        if kind == "swap":
            return _swap_carry([self.slots[k] for k in names])
        return _send_carry([self.sums[k] for k in names], ks)

    def carry(self, call):
        return _join([self._one(*ride) for ride in self.RIDES.get(call, [])])

    def done(self, call, outs):
        outs = list(outs)
        for kind, what, *_ in self.RIDES.get(call, []):
            names = what if kind == "gather" else self.GROUPS[what]
            mine, outs = outs[:len(names)], outs[len(names):]
            if kind == "gather":
                self.full.update(zip(names, mine))
            elif kind == "send":
                for k, r in zip(names, mine):
                    self.got3.setdefault(k, []).append(r)
            else:
                self.got.update(zip(names, mine))
                self.sums.update(zip(names, _chip_sums(f"chip_sums_{what}", [self.slots[k] for k in names], mine, self.c1)))

    def alone(self, call):
        self.done(call, _run_carry(call, self.carry(call)))

    def behind(self, group, x):
        return lax.optimization_barrier((x, tuple(self.sums[k] for k in self.GROUPS[group])))[0]

    def launch(self, call, after=None):
        carry = self._one(*self.RIDES[call][0])
        if after is not None:
            carry.arrays = list(lax.optimization_barrier((tuple(carry.arrays), after))[0])
        self.done(call, _run_carry_async(call, carry, self.ASYNC[call]))


def kernel(x, g_mix_norm, w_in, b_in, sinks, conv_w, conv_b, ln_g, ln_b, w_attn_proj, w_conv_proj, b_conv_proj, w_out, g_ffn_norm, w_ffn_in, w_ffn_down, g_final, loss_target, m_g_mix_norm, m_w_in, m_b_in, m_sinks, m_conv_w, m_conv_b, m_ln_g, m_ln_b, m_w_attn_proj, m_w_conv_proj, m_b_conv_proj, m_w_out, m_g_ffn_norm, m_w_ffn_in, m_w_ffn_down, m_g_final, v_g_mix_norm, v_w_in, v_b_in, v_sinks, v_conv_w, v_conv_b, v_ln_g, v_ln_b, v_w_attn_proj, v_w_conv_proj, v_b_conv_proj, v_w_out, v_g_ffn_norm, v_w_ffn_in, v_w_ffn_down, v_g_final):
    w = dict(g_mix_norm=g_mix_norm, w_in=w_in, b_in=b_in, sinks=sinks, conv_w=conv_w, conv_b=conv_b, ln_g=ln_g,
             ln_b=ln_b, w_attn_proj=w_attn_proj, w_conv_proj=w_conv_proj, b_conv_proj=b_conv_proj, w_out=w_out,
             g_ffn_norm=g_ffn_norm, w_ffn_in=w_ffn_in, w_ffn_down=w_ffn_down, g_final=g_final)
    m = dict(g_mix_norm=m_g_mix_norm, w_in=m_w_in, b_in=m_b_in, sinks=m_sinks, conv_w=m_conv_w, conv_b=m_conv_b,
             ln_g=m_ln_g, ln_b=m_ln_b, w_attn_proj=m_w_attn_proj, w_conv_proj=m_w_conv_proj,
             b_conv_proj=m_b_conv_proj, w_out=m_w_out, g_ffn_norm=m_g_ffn_norm, w_ffn_in=m_w_ffn_in,
             w_ffn_down=m_w_ffn_down, g_final=m_g_final)
    v = dict(g_mix_norm=v_g_mix_norm, w_in=v_w_in, b_in=v_b_in, sinks=v_sinks, conv_w=v_conv_w, conv_b=v_conv_b,
             ln_g=v_ln_g, ln_b=v_ln_b, w_attn_proj=v_w_attn_proj, w_conv_proj=v_w_conv_proj,
             b_conv_proj=v_b_conv_proj, w_out=v_w_out, g_ffn_norm=v_g_ffn_norm, w_ffn_in=v_w_ffn_in,
             w_ffn_down=v_w_ffn_down, g_final=v_g_final)
    ax, ay, ac = lax.axis_index("x"), lax.axis_index("y"), lax.axis_index("c")
    me = 4 * ax + 2 * ay + ac
    chip = 2 * ax + ay

    shards = {k: (w[k][0].T if tr else w[k][0]).astype(BF) for k, tr in BIG.items()}
    cw_shard = jnp.pad(conv_w[0].T, ((0, 0), (0, 1))).reshape(16, 128)
    wi_t, cw_full = _first_gather([shards["w_in"], cw_shard])
    conv_full = cw_full.reshape(CONV_CHANNELS, CONV_PAD).T

    as_row = lambda a: a.reshape(1, -1)
    small_w = {k: as_row(w[k]) for k in SMALL_NAMES}
    small_m = {k: as_row(m[k]) for k in SMALL_NAMES}
    small_v = {k: as_row(v[k]) for k in SMALL_NAMES}
    plan = _Plan(shards, ac.reshape(1).astype(jnp.int32))
    plan.launch("gather_mix", after=wi_t)
    dx, parts = _local_step(x[0], loss_target[0], small_w, wi_t, conv_full, plan)

    ids = jnp.stack([me, chip]).astype(jnp.int32)
    grads, delta, new_m, new_v, after = {}, {}, {}, {}, dx
    packed = _small_pack(parts)
    for k in sorted(BIG, key=lambda k: k == "w_in"):
        if k == "w_in":
            packed = lax.optimization_barrier((packed, after))[0]
            small_gathered, = _run_carry_async("small_gather", _gather_carry([packed]), 8)
        flip = (lambda a: a.T) if BIG[k] else (lambda a: a)
        wk = lax.optimization_barrier((w[k][0], after))[0]
        outs = _grad_adamw(f"grad_adamw_{k}", plan.slots[k], plan.got[k], plan.got3[k], ids,
                           flip(wk), flip(m[k][0]), flip(v[k][0]))
        after = outs[0]
        grads[k], delta[k], new_m[k], new_v[k] = (flip(a)[None] for a in outs)

    loss, cw_grad, small_out = _small_adamw(small_gathered, small_w, small_m, small_v)
    for k in SMALL_NAMES:
        g, d, nm, nv = (a.reshape(w[k].shape) for a in small_out[k])
        grads[k], delta[k], new_m[k], new_v[k] = g, d, nm, nv
    cw_mine = lax.dynamic_slice(cw_grad, (0, me * 64), (CONV_WIDTH, 64))
    d, nm, nv = _adamw("adamw_conv_w", conv_w[0], cw_mine, m_conv_w[0], v_conv_w[0])
    grads["conv_w"], delta["conv_w"], new_m["conv_w"], new_v["conv_w"] = cw_mine[None], d[None], nm[None], nv[None]

    return (loss.reshape(()), dx[None], *[grads[k] for k in WEIGHT_NAMES], *[delta[k] for k in WEIGHT_NAMES],
            *[new_m[k] for k in WEIGHT_NAMES], *[new_v[k] for k in WEIGHT_NAMES])
```

```python
import functools

import jax
import jax.numpy as jnp
from jax import lax
from jax.experimental import pallas as pl
from jax.experimental.pallas import tpu as pltpu
from jax.experimental.pallas import tpu_sc as plsc

F32 = jnp.float32
BF = jnp.bfloat16

SEQ = 2048
D_MODEL = 1024
HEAD_DIM = 64
N_Q_HEADS = 8
N_KV_HEADS = 2
GROUP = N_Q_HEADS // N_KV_HEADS
BLOCK = 128
ATTN_WIDTH = 512
KV_WIDTH = 128
CONV_CHANNELS = 512
CONV_WIDTH = 31
CONV_PAD = 32
GLU_OFF = 768
GATE_OFF = 1792
IN_WIDTH = 3840
D_FF = 2816
EPS = 1e-5
NEG = -1e30
N_DEV = 8

ADAM_LR = 0.001
ADAM_B1 = 0.9
ADAM_B2 = 0.999
ADAM_EPS = 1e-08
ADAM_WD = 0.01
ADAM_STEP = 10

VMEM_LIMIT_BYTES = 56 * 1024 * 1024
MESH = pl.DeviceIdType.MESH
ANY = pl.BlockSpec(memory_space=pl.ANY)

_DIMS = {"NN": (((1,), (0,)), ((), ())), "NT": (((1,), (1,)), ((), ())), "TN": (((0,), (0,)), ((), ()))}


def _params(sem):
    return pltpu.CompilerParams(dimension_semantics=sem, vmem_limit_bytes=VMEM_LIMIT_BYTES)


class _Carry:
    def __init__(self, arrays, out_shapes, sems, start, finish, peers=None):
        self.arrays, self.out_shapes, self.sems, self.start, self.finish = arrays, out_shapes, sems, start, finish
        self.peers = peers


def _carry_io(carry):
    if carry is None:
        return [], [], []
    return list(carry.arrays), list(carry.out_shapes), list(carry.sems)


def _matmul(name, a_list, b, mode, *, m, n, tm, tn, tk=None, epilogue, extra=(), outs, b_off=(0, 0), alias=None,
            scratch=(), carry=None):
    seg_k = [a.shape[0] if mode == "TN" else a.shape[1] for a in a_list]
    whole = tk is None
    seg_nk = [1] * len(a_list) if whole else [ks // tk for ks in seg_k]
    nk = 1 if whole else sum(seg_nk)
    starts = [sum(seg_nk[:s]) for s in range(len(seg_nk))]
    k_starts = [sum(seg_k[:s]) for s in range(len(seg_k))]
    k_tot = sum(seg_k)
    n_a, n_extra, n_out = len(a_list), len(extra), len(outs)

    a_specs = []
    for st, ns, ks in zip(starts, seg_nk, seg_k):
        if mode == "TN":
            a_specs.append(pl.BlockSpec((ks if whole else tk, tm), lambda j, i, k: (k, i)))
        elif whole:
            a_specs.append(pl.BlockSpec((tm, ks), lambda j, i, k: (i, 0)))
        else:
            a_specs.append(pl.BlockSpec((tm, tk), functools.partial(
                lambda j, i, k, st, ns: (i, jnp.clip(k - st, 0, ns - 1)), st=st, ns=ns)))
    bk = k_tot if whole else tk
    if mode == "NT":
        b_spec = pl.BlockSpec((tn, bk), lambda j, i, k: (b_off[0] + j, b_off[1] + k))
    else:
        b_spec = pl.BlockSpec((bk, tn), lambda j, i, k: (b_off[0] + k, b_off[1] + j))
    n_alias = 0 if alias is None else 1
    c_in, c_out, c_sems = _carry_io(carry)
    n_acc = 0 if whole else 1
    nj, ni = n // tn, m // tm

    def body(*refs):
        pos = [n_a, 1, n_alias, n_extra, len(c_in), n_out, len(c_out), n_acc, len(scratch), len(c_sems)]
        cuts = [sum(pos[:q]) for q in range(len(pos) + 1)]
        a_refs, (b_ref,), _, ex, ci_refs, out_refs, co_refs, acc_refs, scr, cs_refs = (
            refs[cuts[q]:cuts[q + 1]] for q in range(len(pos)))
        j, i, k = pl.program_id(0), pl.program_id(1), pl.program_id(2)
        ids = (j, i)
        if carry is not None:
            @pl.when((j == 0) & (i == 0) & (k == 0))
            def _():
                carry.start(ci_refs, co_refs, cs_refs)

        def dot(a_ref, bv):
            return lax.dot_general(a_ref[...].astype(BF), bv.astype(BF), _DIMS[mode], preferred_element_type=F32)

        if whole:
            tot = None
            for a_ref, k0, ks in zip(a_refs, k_starts, seg_k):
                if n_a == 1:
                    bv = b_ref[...]
                else:
                    bv = b_ref[:, k0:k0 + ks] if mode == "NT" else b_ref[k0:k0 + ks, :]
                part = dot(a_ref, bv)
                tot = part if tot is None else tot + part
            epilogue(tot, ex, out_refs, ids, scr)
        else:
            acc, = acc_refs

            @pl.when(k == 0)
            def _():
                acc[...] = jnp.zeros_like(acc)

            for a_ref, st, ns in zip(a_refs, starts, seg_nk):
                if n_a == 1:
                    acc[...] += dot(a_ref, b_ref[...])
                else:
                    @pl.when((k >= st) & (k < st + ns))
                    def _(a_ref=a_ref):
                        acc[...] += dot(a_ref, b_ref[...])

            @pl.when(k == nk - 1)
            def _():
                epilogue(acc[...], ex, out_refs, ids, scr)

        if carry is not None:
            @pl.when((j == nj - 1) & (i == ni - 1) & (k == nk - 1))
            def _():
                carry.finish(ci_refs, co_refs, cs_refs)

    in_specs = [*a_specs, b_spec]
    args = [*a_list, b]
    io_alias = {}
    if alias is not None:
        in_specs.append(pl.BlockSpec(memory_space=pl.ANY))
        args.append(alias[0])
        io_alias = {n_a + 1: alias[1]}
    in_specs += [s for _, s in extra] + [pl.BlockSpec(memory_space=pl.ANY)] * len(c_in)
    args += [x for x, _ in extra] + c_in
    res = pl.pallas_call(
        body, name=name, grid=(nj, ni, nk), in_specs=in_specs,
        out_specs=[s for _, s in outs] + [pl.BlockSpec(memory_space=pl.ANY)] * len(c_out),
        out_shape=[o for o, _ in outs] + c_out,
        scratch_shapes=[*([] if whole else [pltpu.VMEM((tm, tn), F32)]), *scratch, *c_sems],
        input_output_aliases=io_alias,
        compiler_params=_params(("arbitrary", "arbitrary", "arbitrary")),
    )(*args)
    return res if carry is None else (res[:n_out], res[n_out:])


def _tile(tm, tn):
    return pl.BlockSpec((tm, tn), lambda j, i, k: (i, j))


def _row(tn):
    return pl.BlockSpec((1, tn), lambda j, i, k: (0, j))


def _store(dtype):
    def ep(acc, ex, outs, ids, scr):
        outs[0][...] = acc.astype(dtype)
    return ep


def _sds(shape, dtype):
    return jax.ShapeDtypeStruct(shape, dtype)


def _rms_bwd(dh, xv, r, g):
    xh = xv * r
    dxh = dh * g
    dx = r * (dxh - xh * jnp.mean(dxh * xh, axis=-1, keepdims=True))
    return dx, jnp.sum(dh * xh, axis=0, keepdims=True)


def _accumulate_rows(ref, val, first):
    @pl.when(first)
    def _():
        ref[...] = val

    @pl.when(jnp.logical_not(first))
    def _():
        ref[...] += val


def _loss_head(xv, g, target):
    r = lax.rsqrt(jnp.mean(xv * xv, axis=-1, keepdims=True) + EPS)
    err = xv * r * g - target
    dx, dg = _rms_bwd(err * (1.0 / xv.shape[-1]), xv, r, g)
    part = 0.5 * jnp.sum(jnp.mean(err * err, axis=-1, keepdims=True), axis=0, keepdims=True)
    return dx, dg, part


def _lane_half(shape, h):
    lane = lax.broadcasted_iota(jnp.int32, shape, 1)
    return (lane >= HEAD_DIM * h) & (lane < HEAD_DIM * (h + 1))


def _to_half(v, w, h):
    if w != h:
        v = pltpu.roll(v, HEAD_DIM, 1)
    return jnp.where(_lane_half(v.shape, h), v, 0.0)


def _attn_block(qkv_ref, sinks_ref, n, h):
    r0 = pl.multiple_of(n * BLOCK, BLOCK)
    p0 = pl.multiple_of(jnp.maximum(n - 1, 0) * BLOCK, BLOCK)
    rows = pl.ds(r0, BLOCK)
    prev = pl.ds(p0, BLOCK)
    k2 = jnp.concatenate([qkv_ref[prev, ATTN_WIDTH:ATTN_WIDTH + KV_WIDTH],
                          qkv_ref[rows, ATTN_WIDTH:ATTN_WIDTH + KV_WIDTH]], axis=0)
    v2 = jnp.concatenate([qkv_ref[prev, ATTN_WIDTH + KV_WIDTH:ATTN_WIDTH + 2 * KV_WIDTH],
                          qkv_ref[rows, ATTN_WIDTH + KV_WIDTH:ATTN_WIDTH + 2 * KV_WIDTH]], axis=0)
    qs = []
    for g in range(GROUP):
        hq = GROUP * h + g
        blk = qkv_ref[rows, (hq // 2) * 128:(hq // 2 + 1) * 128].astype(F32)
        qs.append(_to_half(blk, hq % 2, h))
    q4 = jnp.concatenate(qs, axis=0).astype(BF)
    s = lax.dot_general(q4, k2, _DIMS["NT"], preferred_element_type=F32) * (HEAD_DIM ** -0.5)
    shape = s.shape
    row = lax.broadcasted_iota(jnp.int32, shape, 0)
    qi = row & (BLOCK - 1)
    kj = lax.broadcasted_iota(jnp.int32, shape, 1)
    diff = qi + BLOCK - kj
    valid = (diff >= 0) & (diff < BLOCK) & ((kj >= BLOCK) | (n > 0))
    s = jnp.where(valid, s, NEG)
    row1 = lax.broadcasted_iota(jnp.int32, (shape[0], 1), 0)
    sink = jnp.zeros((shape[0], 1), F32)
    for g in range(GROUP):
        sink = jnp.where((row1 >= g * BLOCK) & (row1 < (g + 1) * BLOCK), sinks_ref[0, GROUP * h + g], sink)
    m = jnp.maximum(jnp.max(s, axis=-1, keepdims=True), sink)
    e = jnp.exp(s - m)
    es = jnp.exp(sink - m)
    inv = 1.0 / (jnp.sum(e, axis=-1, keepdims=True) + es)
    return e * inv, es * inv, q4, k2, v2, rows, prev


def _attn_fwd(proj, sinks, carry=None):
    T = proj.shape[0]
    c_in, c_out, c_sems = _carry_io(carry)

    def body(*refs):
        qkv_ref, sinks_ref = refs[:2]
        ci_refs = refs[2:2 + len(c_in)]
        o_ref = refs[2 + len(c_in)]
        co_refs = refs[3 + len(c_in):3 + len(c_in) + len(c_out)]
        cs_refs = refs[3 + len(c_in) + len(c_out):]
        if carry is not None:
            carry.start(ci_refs, co_refs, cs_refs)

        def blk(n, z):
            outs = [None] * (N_Q_HEADS // 2)
            for h in range(N_KV_HEADS):
                p, _, _, _, v2, rows, _ = _attn_block(qkv_ref, sinks_ref, n, h)
                o = lax.dot_general(p.astype(BF), v2, _DIMS["NN"], preferred_element_type=F32)
                for g in range(GROUP):
                    hq = GROUP * h + g
                    piece = jnp.where(_lane_half((BLOCK, 128), h), o[g * BLOCK:(g + 1) * BLOCK], 0.0)
                    if hq % 2 != h:
                        piece = pltpu.roll(piece, HEAD_DIM, 1)
                    outs[hq // 2] = piece if outs[hq // 2] is None else outs[hq // 2] + piece
            for pb in range(N_Q_HEADS // 2):
                o_ref[rows, pb * 128:(pb + 1) * 128] = outs[pb].astype(BF)
            return z

        lax.fori_loop(0, T // BLOCK, blk, 0)
        if carry is not None:
            carry.finish(ci_refs, co_refs, cs_refs)

    res = pl.pallas_call(
        body, name="attn_fwd", grid=(1,),
        in_specs=[pl.BlockSpec((T, GLU_OFF), lambda i: (0, 0)), pl.BlockSpec(memory_space=pltpu.SMEM),
                  *[ANY] * len(c_in)],
        out_specs=[pl.BlockSpec((T, ATTN_WIDTH), lambda i: (0, 0)), *[ANY] * len(c_out)],
        out_shape=[_sds((T, ATTN_WIDTH), BF), *c_out], scratch_shapes=c_sems,
        compiler_params=_params(("arbitrary",)),
    )(proj, sinks, *c_in)
    return res[0], res[1:]


def _attn_bwd(proj, d_o, sinks, carry=None):
    T = proj.shape[0]
    c_in, c_out, c_sems = _carry_io(carry)

    def body(*refs):
        qkv_ref, do_ref, sinks_ref = refs[:3]
        ci_refs = refs[3:3 + len(c_in)]
        dqkv_ref, dsink_ref = refs[3 + len(c_in):5 + len(c_in)]
        co_refs = refs[5 + len(c_in):5 + len(c_in) + len(c_out)]
        dk_acc, dv_acc = refs[5 + len(c_in) + len(c_out):7 + len(c_in) + len(c_out)]
        cs_refs = refs[7 + len(c_in) + len(c_out):]
        if carry is not None:
            carry.start(ci_refs, co_refs, cs_refs)
        dsink_ref[...] = jnp.zeros_like(dsink_ref)
        dk_acc[...] = jnp.zeros_like(dk_acc)
        dv_acc[...] = jnp.zeros_like(dv_acc)

        def blk(n, carry):
            dqs = [None] * (N_Q_HEADS // 2)
            for h in range(N_KV_HEADS):
                p, psink, q4, k2, v2, rows, prev = _attn_block(qkv_ref, sinks_ref, n, h)
                dos = []
                for g in range(GROUP):
                    hq = GROUP * h + g
                    dos.append(_to_half(do_ref[rows, (hq // 2) * 128:(hq // 2 + 1) * 128].astype(F32), hq % 2, h))
                do4 = jnp.concatenate(dos, axis=0).astype(BF)
                dp = lax.dot_general(do4, v2, _DIMS["NT"], preferred_element_type=F32)
                delta = jnp.sum(p * dp, axis=-1, keepdims=True)
                ds = (p * (dp - delta) * (HEAD_DIM ** -0.5)).astype(BF)
                dsk = psink * delta
                for g in range(GROUP):
                    hq = GROUP * h + g
                    tot = -jnp.sum(dsk[g * BLOCK:(g + 1) * BLOCK], axis=0, keepdims=True)
                    lane = lax.broadcasted_iota(jnp.int32, (1, 128), 1)
                    dsink_ref[...] += jnp.where(lane == hq, tot, 0.0)
                dq = lax.dot_general(ds, k2, _DIMS["NN"], preferred_element_type=F32)
                dk = lax.dot_general(ds, q4, _DIMS["TN"], preferred_element_type=F32)
                dv = lax.dot_general(p.astype(BF), do4, _DIMS["TN"], preferred_element_type=F32)
                dk_acc[prev, :] += dk[:BLOCK]
                dk_acc[rows, :] += dk[BLOCK:]
                dv_acc[prev, :] += dv[:BLOCK]
                dv_acc[rows, :] += dv[BLOCK:]
                for g in range(GROUP):
                    hq = GROUP * h + g
                    piece = jnp.where(_lane_half((BLOCK, 128), h), dq[g * BLOCK:(g + 1) * BLOCK], 0.0)
                    if hq % 2 != h:
                        piece = pltpu.roll(piece, HEAD_DIM, 1)
                    dqs[hq // 2] = piece if dqs[hq // 2] is None else dqs[hq // 2] + piece
            for pb in range(N_Q_HEADS // 2):
                dqkv_ref[rows, pb * 128:(pb + 1) * 128] = dqs[pb].astype(BF)
            return carry

        lax.fori_loop(0, T // BLOCK, blk, 0)
        dqkv_ref[:, ATTN_WIDTH:ATTN_WIDTH + KV_WIDTH] = dk_acc[...].astype(BF)
        dqkv_ref[:, ATTN_WIDTH + KV_WIDTH:] = dv_acc[...].astype(BF)
        if carry is not None:
            carry.finish(ci_refs, co_refs, cs_refs)

    res = pl.pallas_call(
        body, name="attn_bwd", grid=(1,),
        in_specs=[pl.BlockSpec((T, GLU_OFF), lambda i: (0, 0)), pl.BlockSpec((T, ATTN_WIDTH), lambda i: (0, 0)),
                  pl.BlockSpec(memory_space=pltpu.SMEM), *[ANY] * len(c_in)],
        out_specs=[pl.BlockSpec((T, GLU_OFF), lambda i: (0, 0)), pl.BlockSpec((1, 128), lambda i: (0, 0)),
                   *[ANY] * len(c_out)],
        out_shape=[_sds((T, GLU_OFF), BF), _sds((1, 128), F32), *c_out],
        scratch_shapes=[pltpu.VMEM((T, KV_WIDTH), F32), pltpu.VMEM((T, KV_WIDTH), F32), *c_sems],
        compiler_params=_params(("arbitrary",)),
    )(proj, d_o, sinks, *c_in)
    return res[:2], res[2:]


CHUNK = 256
SUB = 32
WIN = CHUNK + 32
PAD_ROWS = SEQ + 2 * CONV_PAD
_GLU_SPECS = [pl.BlockSpec((SEQ, 256), functools.partial(lambda i, c: (0, c), c=GLU_OFF // 256 + c)) for c in range(4)]


def _glu_to_pad(a0, a1, b0, b1, zpad):
    C = CONV_CHANNELS
    zpad[0:CONV_PAD, :] = jnp.zeros((CONV_PAD, C), F32)
    zpad[CONV_PAD + SEQ:, :] = jnp.zeros((CONV_PAD, C), F32)
    zpad[CONV_PAD:CONV_PAD + SEQ, 0:256] = a0[...].astype(F32) * jax.nn.sigmoid(b0[...].astype(F32))
    zpad[CONV_PAD:CONV_PAD + SEQ, 256:C] = a1[...].astype(F32) * jax.nn.sigmoid(b1[...].astype(F32))


def _tap_windows(src, base, win):
    for b in range(8):
        win[b, 0:WIN - 8, :] = src[base + b:base + b + WIN - 8, :]


def _taps(win, w_ref, init, out, flip):
    def sub(si, carry):
        r0 = pl.multiple_of(si * SUB, SUB)
        acc = jnp.broadcast_to(init, (SUB, CONV_CHANNELS))
        for k in range(CONV_WIDTH):
            wk = (CONV_WIDTH - 1 - k) if flip else k
            acc = acc + w_ref[wk:wk + 1, :] * win[k % 8, pl.ds(r0 + 8 * (k // 8), SUB), :]
        out[pl.ds(r0, SUB), :] = acc
        return carry

    lax.fori_loop(0, CHUNK // SUB, sub, 0)


def _tap_grads(win, du, dwacc):
    def sub(si, carry):
        r0 = pl.multiple_of(si * SUB, SUB)
        d = du[pl.ds(r0, SUB), :]
        for k in range(CONV_WIDTH):
            p = d * win[k % 8, pl.ds(r0 + 8 * (k // 8), SUB), :]
            dwacc[8 * k:8 * k + 8, :] += (p[0:8] + p[8:16]) + (p[16:24] + p[24:32])
        return carry

    lax.fori_loop(0, CHUNK // SUB, sub, 0)


def _ln_parts(u):
    mu = jnp.mean(u, axis=-1, keepdims=True)
    xc = u - mu
    rstd = lax.rsqrt(jnp.mean(xc * xc, axis=-1, keepdims=True) + EPS)
    return xc * rstd, rstd


def _conv_fwd(proj, conv_w, conv_b, ln_g, ln_b, carry=None):
    T, C = proj.shape[0], CONV_CHANNELS
    vec = pl.BlockSpec((1, C), lambda i: (0, 0))
    c_in, c_out, c_sems = _carry_io(carry)

    def body(*refs):
        a0, a1, b0, b1, w_ref, cb_ref, g_ref, be_ref = refs[:8]
        ci_refs = refs[8:8 + len(c_in)]
        c_ref, u_ref = refs[8 + len(c_in):10 + len(c_in)]
        co_refs = refs[10 + len(c_in):10 + len(c_in) + len(c_out)]
        zpad, win, ubuf = refs[10 + len(c_in) + len(c_out):13 + len(c_in) + len(c_out)]
        cs_refs = refs[13 + len(c_in) + len(c_out):]
        if carry is not None:
            carry.start(ci_refs, co_refs, cs_refs)
        _glu_to_pad(a0, a1, b0, b1, zpad)
        for ci in range(T // CHUNK):
            _tap_windows(zpad, ci * CHUNK + CONV_PAD - (CONV_WIDTH - 1), win)
            _taps(win, w_ref, cb_ref[...], ubuf, False)
            u = ubuf[...]
            u_ref[ci * CHUNK:(ci + 1) * CHUNK, :] = u
            xh, _ = _ln_parts(u)
            ln = xh * g_ref[...] + be_ref[...]
            c_ref[ci * CHUNK:(ci + 1) * CHUNK, :] = (ln * jax.nn.sigmoid(ln)).astype(BF)
        if carry is not None:
            carry.finish(ci_refs, co_refs, cs_refs)

    res = pl.pallas_call(
        body, name="conv_fwd", grid=(1,),
        in_specs=[*_GLU_SPECS, pl.BlockSpec((CONV_PAD, C), lambda i: (0, 0)), vec, vec, vec, *[ANY] * len(c_in)],
        out_specs=[pl.BlockSpec((T, C), lambda i: (0, 0)), pl.BlockSpec((T, C), lambda i: (0, 0)), *[ANY] * len(c_out)],
        out_shape=[_sds((T, C), BF), _sds((T, C), F32), *c_out],
        scratch_shapes=[pltpu.VMEM((PAD_ROWS, C), F32), pltpu.VMEM((8, WIN, C), F32), pltpu.VMEM((CHUNK, C), F32),
                        *c_sems],
        compiler_params=_params(("arbitrary",)),
    )(proj, proj, proj, proj, conv_w, conv_b, ln_g, ln_b, *c_in)
    return res[:2], res[2:]


def _conv_bwd(proj, u, d_c, conv_w, conv_b, ln_g, ln_b, carry=None):
    T, C = proj.shape[0], CONV_CHANNELS
    vec = pl.BlockSpec((1, C), lambda i: (0, 0))
    wspec = pl.BlockSpec((CONV_PAD, C), lambda i: (0, 0))
    c_in, c_out, c_sems = _carry_io(carry)

    def body(*refs):
        a0, a1, b0, b1, u_ref, dc_ref, w_ref, cb_ref, g_ref, be_ref = refs[:10]
        ci_refs = refs[10:10 + len(c_in)]
        o = 10 + len(c_in)
        dglu_ref, dw_ref, dcb_ref, dg_ref, dbe_ref = refs[o:o + 5]
        co_refs = refs[o + 5:o + 5 + len(c_out)]
        zpad, dupad, win, ubuf, dwacc = refs[o + 5 + len(c_out):o + 10 + len(c_out)]
        cs_refs = refs[o + 10 + len(c_out):]
        if carry is not None:
            carry.start(ci_refs, co_refs, cs_refs)
        _glu_to_pad(a0, a1, b0, b1, zpad)
        dupad[T:, :] = jnp.zeros((2 * CONV_PAD, C), F32)
        dwacc[...] = jnp.zeros_like(dwacc)
        dcb_ref[...] = jnp.zeros_like(dcb_ref)
        dg_ref[...] = jnp.zeros_like(dg_ref)
        dbe_ref[...] = jnp.zeros_like(dbe_ref)
        for ci in range(T // CHUNK):
            rows = slice(ci * CHUNK, (ci + 1) * CHUNK)
            _tap_windows(zpad, ci * CHUNK + CONV_PAD - (CONV_WIDTH - 1), win)
            xh, rstd = _ln_parts(u_ref[rows, :])
            ln = xh * g_ref[...] + be_ref[...]
            sg = jax.nn.sigmoid(ln)
            dln = dc_ref[rows, :].astype(F32) * (sg * (1.0 + ln * (1.0 - sg)))
            dg_ref[...] += jnp.sum(dln * xh, axis=0, keepdims=True)
            dbe_ref[...] += jnp.sum(dln, axis=0, keepdims=True)
            dxh = dln * g_ref[...]
            du = rstd * (dxh - jnp.mean(dxh, axis=-1, keepdims=True)
                         - xh * jnp.mean(dxh * xh, axis=-1, keepdims=True))
            dupad[rows, :] = du
            dcb_ref[...] += jnp.sum(du, axis=0, keepdims=True)
            _tap_grads(win, dupad.at[rows, :], dwacc)
        for k in range(CONV_WIDTH):
            dw_ref[k:k + 1, :] = jnp.sum(dwacc[8 * k:8 * k + 8, :], axis=0, keepdims=True)
        dw_ref[CONV_WIDTH:, :] = jnp.zeros((CONV_PAD - CONV_WIDTH, C), F32)
        for ci in range(T // CHUNK):
            rows = slice(ci * CHUNK, (ci + 1) * CHUNK)
            _tap_windows(dupad, ci * CHUNK, win)
            _taps(win, w_ref, jnp.zeros((1, C), F32), ubuf, True)
            dz = ubuf[...]
            for half, (a, b) in enumerate(((a0, b0), (a1, b1))):
                sb = jax.nn.sigmoid(b[rows, :].astype(F32))
                dzh = dz[:, half * 256:(half + 1) * 256]
                dglu_ref[rows, half * 256:(half + 1) * 256] = (dzh * sb).astype(BF)
                dglu_ref[rows, C + half * 256:C + (half + 1) * 256] = (
                    dzh * a[rows, :].astype(F32) * sb * (1.0 - sb)).astype(BF)
        if carry is not None:
            carry.finish(ci_refs, co_refs, cs_refs)

    res = pl.pallas_call(
        body, name="conv_bwd", grid=(1,),
        in_specs=[*_GLU_SPECS, pl.BlockSpec((T, C), lambda i: (0, 0)), pl.BlockSpec((T, C), lambda i: (0, 0)), wspec,
                  vec, vec, vec, *[ANY] * len(c_in)],
        out_specs=[pl.BlockSpec((T, 2 * C), lambda i: (0, 0)), wspec, vec, vec, vec, *[ANY] * len(c_out)],
        out_shape=[_sds((T, 2 * C), BF), _sds((CONV_PAD, C), F32), _sds((1, C), F32), _sds((1, C), F32),
                   _sds((1, C), F32), *c_out],
        scratch_shapes=[pltpu.VMEM((PAD_ROWS, C), F32), pltpu.VMEM((PAD_ROWS, C), F32), pltpu.VMEM((8, WIN, C), F32),
                        pltpu.VMEM((CHUNK, C), F32), pltpu.VMEM((8 * CONV_PAD, C), F32), *c_sems],
        compiler_params=_params(("arbitrary",)),
    )(proj, proj, proj, proj, u, d_c, conv_w, conv_b, ln_g, ln_b, *c_in)
    return res[:5], res[5:]


_GATE_BLK = GATE_OFF // 256


def _ffn_in_swiglu(h2, wf_t, carry=None):
    T, D = h2.shape
    tm, tn = 1024, D_FF // 2
    nj, ni = D_FF // tn, T // tm
    c_in, c_out, c_sems = _carry_io(carry)

    def body(*refs):
        a_ref, bg_ref, bu_ref = refs[:3]
        ci_refs = refs[3:3 + len(c_in)]
        act_ref, g_ref, u_ref = refs[3 + len(c_in):6 + len(c_in)]
        co_refs = refs[6 + len(c_in):6 + len(c_in) + len(c_out)]
        cs_refs = refs[6 + len(c_in) + len(c_out):]
        j, i = pl.program_id(0), pl.program_id(1)
        if carry is not None:
            @pl.when((j == 0) & (i == 0))
            def _():
                carry.start(ci_refs, co_refs, cs_refs)
        a = a_ref[...]
        for c0, c1 in ((0, 768), (768, tn)):
            g = lax.dot_general(a, bg_ref[c0:c1, :], _DIMS["NT"], preferred_element_type=F32)
            u = lax.dot_general(a, bu_ref[c0:c1, :], _DIMS["NT"], preferred_element_type=F32)
            act_ref[:, c0:c1] = (g * jax.nn.sigmoid(g) * u).astype(BF)
            g_ref[:, c0:c1] = g.astype(BF)
            u_ref[:, c0:c1] = u.astype(BF)
        if carry is not None:
            @pl.when((j == nj - 1) & (i == ni - 1))
            def _():
                carry.finish(ci_refs, co_refs, cs_refs)

    t = pl.BlockSpec((tm, tn), lambda j, i: (i, j))
    res = pl.pallas_call(
        body, name="ffn_in_swiglu", grid=(nj, ni),
        in_specs=[pl.BlockSpec((tm, D), lambda j, i: (i, 0)), pl.BlockSpec((tn, D), lambda j, i: (j, 0)),
                  pl.BlockSpec((tn, D), lambda j, i: (nj + j, 0)), *[ANY] * len(c_in)],
        out_specs=[t, t, t, *[ANY] * len(c_out)], out_shape=[*[_sds((T, D_FF), BF)] * 3, *c_out],
        scratch_shapes=c_sems,
        compiler_params=_params(("arbitrary", "arbitrary")),
    )(h2, wf_t, wf_t, *c_in)
    return res[:3], res[3:]


def _proj_merge(o, c, wap_t, wcp_t, b_cp, proj):
    T, D = o.shape[0], wap_t.shape[0]
    tm, tg = 1024, 256
    nj = D // tg

    def body(o_ref, c_ref, wa_ref, wc_ref, b_ref, g0_ref, g1_ref, ya_ref, yc_ref, m_ref):
        ya = lax.dot_general(o_ref[...], wa_ref[...], _DIMS["NT"], preferred_element_type=F32)
        yc = lax.dot_general(c_ref[...], wc_ref[...], _DIMS["NT"], preferred_element_type=F32) + b_ref[...]
        ya_ref[...] = ya.astype(BF)
        yc_ref[...] = yc.astype(BF)
        m_ref[...] = (jax.nn.sigmoid(g0_ref[...].astype(F32)) * ya + jax.nn.sigmoid(g1_ref[...].astype(F32)) * yc).astype(BF)

    act = pl.BlockSpec((tm, o.shape[1]), lambda j, i: (i, 0))
    wgt = pl.BlockSpec((tg, o.shape[1]), lambda j, i: (j, 0))
    t = pl.BlockSpec((tm, tg), lambda j, i: (i, j))
    return pl.pallas_call(
        body, name="proj_merge", grid=(nj, T // tm),
        in_specs=[act, act, wgt, wgt, pl.BlockSpec((1, tg), lambda j, i: (0, j)),
                  pl.BlockSpec((tm, tg), lambda j, i: (i, _GATE_BLK + j)),
                  pl.BlockSpec((tm, tg), lambda j, i: (i, _GATE_BLK + nj + j))],
        out_specs=[t, t, t], out_shape=[_sds((T, D), BF)] * 3,
        compiler_params=_params(("arbitrary", "arbitrary")),
    )(o, c, wap_t, wcp_t, b_cp, proj, proj)


def _proj_in_dw(segs, h):
    T, D = h.shape
    tb = 256
    nblk = [seg.shape[1] // tb for seg in segs]
    starts = [sum(nblk[:q]) for q in range(len(segs))]
    n_seg = len(segs)

    def body(*refs):
        seg_refs, h_ref, o_ref, cs_ref = refs[:n_seg], refs[n_seg], refs[n_seg + 1], refs[n_seg + 2]
        i = pl.program_id(0)
        for seg_ref, st, nb in zip(seg_refs, starts, nblk):
            @pl.when((i >= st) & (i < st + nb))
            def _(seg_ref=seg_ref):
                a = seg_ref[...]
                o_ref[...] = lax.dot_general(a, h_ref[...], _DIMS["TN"], preferred_element_type=F32).astype(BF)
                cs_ref[...] = jnp.sum(a.astype(F32), axis=0, keepdims=True)

    in_specs = [pl.BlockSpec((T, tb), functools.partial(lambda i, st, nb: (0, jnp.clip(i - st, 0, nb - 1)), st=st, nb=nb))
                for st, nb in zip(starts, nblk)]
    return pl.pallas_call(
        body, name="proj_in_dw", grid=(sum(nblk),),
        in_specs=[*in_specs, pl.BlockSpec((T, D), lambda i: (0, 0))],
        out_specs=[pl.BlockSpec((tb, D), lambda i: (i, 0)), pl.BlockSpec((1, tb), lambda i: (0, i))],
        out_shape=[_sds((sum(nblk) * tb, D), BF), _sds((1, sum(nblk) * tb), F32)],
        compiler_params=_params(("arbitrary",)),
    )(*segs, h)


def _local_step(x, h, r1, target, small, wi_t, conv_w, plan):
    T, D = x.shape
    tm = 1024

    def carried(call, res, carry):
        if carry is None:
            return res
        outs, got = res
        plan.done(call, got)
        return outs


    def ep_add(acc, ex, outs, ids, scr):
        outs[0][...] = acc + ex[0][...]

    tn_in = IN_WIDTH // 2
    carry = plan.carry("proj_in")
    def ep_bias_bf16(acc, ex, outs, ids, scr):
        outs[0][...] = (acc + ex[0][...]).astype(BF)

    proj, = carried("proj_in", _matmul("proj_in", [h], wi_t, "NT", m=T, n=IN_WIDTH, tm=tm, tn=tn_in,
                                       epilogue=ep_bias_bf16, extra=[(small["b_in"], _row(tn_in))],
                                       outs=[(_sds((T, IN_WIDTH), BF), _tile(tm, tn_in))], carry=carry), carry)
    plan.launch("gather_ffn", after=proj)
    o, got = _attn_fwd(proj, small["sinks"], carry=plan.carry("attn_fwd"))
    plan.done("attn_fwd", got)
    (c, u_conv), got = _conv_fwd(proj, conv_w, small["conv_b"], small["ln_g"], small["ln_b"],
                                 carry=plan.carry("conv_fwd"))
    plan.done("conv_fwd", got)
    wap_t, wcp_t, w_out = plan.weight("w_attn_proj"), plan.weight("w_conv_proj"), plan.weight("w_out")
    ya, yc, merged = _proj_merge(o, c, wap_t, wcp_t, small["b_conv_proj"], proj)

    tg = 256
    gate_specs = [pl.BlockSpec((tm, tg), lambda j, i, k: (i, _GATE_BLK + j)),
                  pl.BlockSpec((tm, tg), lambda j, i, k: (i, _GATE_BLK + D // tg + j))]

    def ep_residual_rms(acc, ex, outs, ids, scr):
        x2v = acc + ex[0][...]
        r = lax.rsqrt(jnp.mean(x2v * x2v, axis=-1, keepdims=True) + EPS)
        outs[0][...] = x2v
        outs[1][...] = (x2v * r * ex[1][...]).astype(BF)
        outs[2][...] = r

    carry = plan.carry("out_proj")
    x2, h2, r2 = carried("out_proj", _matmul(
        "out_proj_rms", [merged], w_out, "NN", m=T, n=D, tm=512, tn=D, epilogue=ep_residual_rms,
        extra=[(x, _tile(512, D)), (small["g_ffn_norm"], _row(D))],
        outs=[(_sds((T, D), F32), _tile(512, D)), (_sds((T, D), BF), _tile(512, D)),
              (_sds((T, 1), F32), pl.BlockSpec((512, 1), lambda j, i, k: (i, 0)))], carry=carry), carry)
    plan.launch("gather_down", after=x2)
    wf_t = plan.weight("w_ffn_in")
    (act, gate, up), got = _ffn_in_swiglu(h2, wf_t, carry=plan.carry("ffn_in_swiglu"))
    plan.done("ffn_in_swiglu", got)
    w_down = plan.weight("w_ffn_down")
    def ep_residual_loss(acc, ex, outs, ids, scr):
        dx, dg, part = _loss_head(acc + ex[0][...], ex[1][...], ex[2][...])
        outs[0][...] = dx
        outs[1][...] = dx.astype(BF)
        _accumulate_rows(outs[2], dg, ids[1] == 0)
        _accumulate_rows(outs[3], part, ids[1] == 0)

    dx3, dx3_b, dg_final, loss = _matmul(
        "ffn_down_loss", [act], w_down, "NN", m=T, n=D, tm=512, tn=D, epilogue=ep_residual_loss,
        extra=[(x2, _tile(512, D)), (small["g_final"], _row(D)), (target, _tile(512, D))],
        outs=[(_sds((T, D), F32), _tile(512, D)), (_sds((T, D), BF), _tile(512, D)), (_sds((1, D), F32), _row(D)),
              (_sds((1, 1), F32), pl.BlockSpec((1, 1), lambda j, i, k: (0, 0)))])

    tn_ff = D_FF // 2

    def ep_swiglu_bwd(acc, ex, outs, ids, scr):
        g, u = ex[0][...].astype(F32), ex[1][...].astype(F32)
        sg = jax.nn.sigmoid(g)
        outs[0][...] = (acc * u * sg * (1.0 + g * (1.0 - sg))).astype(BF)
        outs[1][...] = (acc * g * sg).astype(BF)

    dgate, dup = _matmul(
        "ffn_down_bwd", [dx3_b], w_down, "NT", m=T, n=D_FF, tm=tm, tn=tn_ff, epilogue=ep_swiglu_bwd,
        extra=[(gate, _tile(tm, tn_ff)), (up, _tile(tm, tn_ff))],
        outs=[(_sds((T, D_FF), BF), _tile(tm, tn_ff)), (_sds((T, D_FF), BF), _tile(tm, tn_ff))])

    def dw(name, a, b, rows, cols, row_off=0, alias=None, total_rows=None, colsum=False):
        total_rows = rows if total_rows is None else total_rows
        tmw = rows if rows <= 1024 else D_FF // 2
        blk, rem = divmod(row_off, tmw)
        assert rem == 0

        def ep(acc, ex, outs, ids, scr):
            outs[0][...] = acc.astype(BF)
            if colsum:
                outs[1][...] = jnp.sum(ex[0][...].astype(F32), axis=0, keepdims=True)

        outs = [(_sds((total_rows, cols), BF), pl.BlockSpec((tmw, cols), lambda j, i, k: (blk + i, j)))]
        extra = []
        if colsum:
            extra = [(a, pl.BlockSpec((T, tmw), lambda j, i, k: (0, i)))]
            outs.append((_sds((1, rows), F32), pl.BlockSpec((1, tmw), lambda j, i, k: (0, i))))
        carry = plan.carry(name)
        res = carried(name, _matmul(name, [a], b, "TN", m=rows, n=cols, tm=tmw, tn=cols, epilogue=ep, extra=extra,
                                    outs=outs, alias=None if alias is None else (alias, 0), carry=carry), carry)
        return res if colsum else res[0]

    plan.grad_ready(dict(w_ffn_down=dw("ffn_down_dw", act, dx3_b, D_FF, D)))

    def ep_rms_bwd(acc, ex, outs, ids, scr):
        dx, dg = _rms_bwd(acc, ex[0][...], ex[1][...], ex[2][...])
        dx = ex[3][...] + dx
        outs[0][...] = dx
        outs[1][...] = dx.astype(BF)
        _accumulate_rows(outs[2], dg, ids[1] == 0)

    def rms_bwd_io(tm_, xin, r, g, dres):
        return dict(
            extra=[(xin, _tile(tm_, D)), (r, pl.BlockSpec((tm_, 1), lambda j, i, k: (i, 0))), (g, _row(D)),
                   (dres, _tile(tm_, D))],
            outs=[(_sds((T, D), F32), _tile(tm_, D)), (_sds((T, D), BF), _tile(tm_, D)), (_sds((1, D), F32), _row(D))])

    carry = plan.carry("ffn_in_bwd")
    dx2, dx2_b, dg_ffn = carried(
        "ffn_in_bwd",
        _matmul("ffn_in_bwd", [dgate, dup], wf_t, "NN", m=T, n=D, tm=tm, tn=D, tk=D_FF // 2, epilogue=ep_rms_bwd,
                carry=carry, **rms_bwd_io(tm, x2, r2, small["g_ffn_norm"], dx3)), carry)
    plan.launch("send_down")
    gwf_t = dw("ffn_in_dw_gate", dgate, h2, D_FF, D, total_rows=2 * D_FF)
    gwf_t = dw("ffn_in_dw_up", dup, h2, D_FF, D, row_off=D_FF, alias=gwf_t, total_rows=2 * D_FF)
    plan.grad_ready(dict(w_ffn_in=gwf_t))

    def ep_merge_bwd(acc, ex, outs, ids, scr):
        s0 = jax.nn.sigmoid(ex[2][...].astype(F32))
        s1 = jax.nn.sigmoid(ex[3][...].astype(F32))
        outs[0][...] = (acc * s0).astype(BF)
        outs[1][...] = (acc * s1).astype(BF)
        outs[2][...] = (acc * ex[0][...].astype(F32) * s0 * (1.0 - s0)).astype(BF)
        outs[3][...] = (acc * ex[1][...].astype(F32) * s1 * (1.0 - s1)).astype(BF)

    carry = plan.carry("out_proj_bwd_merge")
    dya, dyc, dg0, dg1 = carried(
        "out_proj_bwd_merge",
        _matmul("out_proj_bwd_merge", [dx2_b], w_out, "NT", m=T, n=D, tm=tm, tn=tg, epilogue=ep_merge_bwd,
                extra=[(ya, _tile(tm, tg)), (yc, _tile(tm, tg)), (proj, gate_specs[0]), (proj, gate_specs[1])],
                outs=[(_sds((T, D), BF), _tile(tm, tg))] * 4, carry=carry), carry)
    plan.launch("send_ffn")
    gw_out = dw("out_proj_dw", merged, dx2_b, D, D)
    d_o, = _matmul("attn_proj_bwd", [dya], wap_t, "NN", m=T, n=ATTN_WIDTH, tm=tm, tn=ATTN_WIDTH,
                   epilogue=_store(BF), outs=[(_sds((T, ATTN_WIDTH), BF), _tile(tm, ATTN_WIDTH))])
    d_c, = _matmul("conv_proj_bwd", [dyc], wcp_t, "NN", m=T, n=CONV_CHANNELS, tm=tm, tn=CONV_CHANNELS,
                   epilogue=_store(BF), outs=[(_sds((T, CONV_CHANNELS), BF), _tile(tm, CONV_CHANNELS))])
    gwap_t = dw("attn_proj_dw", dya, o, D, ATTN_WIDTH)
    gwcp_t, db_cp = dw("conv_proj_dw", dyc, c, D, CONV_CHANNELS, colsum=True)
    plan.grad_ready(dict(w_out=gw_out, w_attn_proj=gwap_t, w_conv_proj=gwcp_t))
    (dglu, dcw, dcb, dlng, dlnb), got = _conv_bwd(proj, u_conv, d_c, conv_w, small["conv_b"], small["ln_g"],
                                                  small["ln_b"], carry=plan.carry("conv_bwd"))
    plan.done("conv_bwd", got)
    plan.launch("send_mix")
    (dqkv, dsinks), got = _attn_bwd(proj, d_o, small["sinks"], carry=plan.carry("attn_bwd"))
    plan.done("attn_bwd", got)

    segs = [dqkv, dglu, dg0, dg1]
    gwi_t, db_in = _proj_in_dw(segs, h)
    plan.grad_ready(dict(w_in=gwi_t))
    plan.alone("swap_inp")
    plan.launch("send_inp")
    carry = plan.carry("proj_in_bwd")
    dx, _, dg_mix = carried(
        "proj_in_bwd",
        _matmul("proj_in_bwd", segs, wi_t, "NN", m=T, n=D, tm=512, tn=D, epilogue=ep_rms_bwd, carry=carry,
                **rms_bwd_io(512, x, r1, small["g_mix_norm"], plan.behind("inp", dx2))), carry)

    parts = dict(g_mix_norm=dg_mix, b_in=db_in, sinks=dsinks, conv_w=dcw, conv_b=dcb, ln_g=dlng, ln_b=dlnb,
                 b_conv_proj=db_cp, g_ffn_norm=dg_ffn, g_final=dg_final, loss=loss)
    return dx, parts


def _place():
    x, y, c = lax.axis_index("x"), lax.axis_index("y"), lax.axis_index("c")
    return x, y, c, [(1 - x, y), (x, 1 - y), (1 - x, 1 - y)]


def _gather_copies(x_refs, out_refs, rows_per, send_sems, recv_sems, local_sems):
    x, y, c, chips = _place()
    me, sibling = (x, y, c), (x, y, 1 - c)

    def rows(a, px, py, pc):
        return out_refs[a].at[pl.ds((4 * px + 2 * py + pc) * rows_per[a], rows_per[a])]

    def copy(a, k, block, to, src=None):
        return pltpu.make_async_remote_copy(
            src_ref=rows(a, *block) if src is None else src, dst_ref=rows(a, *block),
            send_sem=send_sems.at[7 * a + k], recv_sem=recv_sems.at[7 * a + k], device_id=to, device_id_type=MESH)

    def local(a):
        return pltpu.make_async_copy(x_refs[a], rows(a, *me), local_sems.at[a])

    def first(a):
        return [copy(a, 0, me, sibling, src=x_refs[a])] + [copy(a, 1 + j, me, (*chip, c), src=x_refs[a])
                                                          for j, chip in enumerate(chips)]

    def arrive(a, j):
        return copy(a, 1 + j, (*chips[j], c), me)

    def passed(a, j):
        return copy(a, 4 + j, (*chips[j], c), sibling)

    def from_sibling(a):
        return [copy(a, 0, sibling, me)] + [copy(a, 4 + j, (*chip, 1 - c), me) for j, chip in enumerate(chips)]

    return len(x_refs), local, first, arrive, passed, from_sibling


def _gather_start(*refs):
    n, local, first, _, _, _ = _gather_copies(*refs)
    for a in range(n):
        local(a).start()
        for cp in first(a):
            cp.start()


def _gather_finish(*refs):
    n, local, first, arrive, passed, from_sibling = _gather_copies(*refs)
    for a in range(n):
        for j in range(3):
            arrive(a, j).wait_recv()
            passed(a, j).start()
    for a in range(n):
        for cp in from_sibling(a):
            cp.wait_recv()
    for a in range(n):
        for cp in first(a) + [passed(a, j) for j in range(3)]:
            cp.wait_send()
        local(a).wait()


def _gather_peers():
    x, y, c, chips = _place()
    return [(x, y, 1 - c)] + [(*chip, c) for chip in chips]


def _gather_sems(n):
    return [pltpu.SemaphoreType.DMA((7 * n,)), pltpu.SemaphoreType.DMA((7 * n,)), pltpu.SemaphoreType.DMA((n,))]


def _gather_carry(shards):
    rows_per = [s.shape[0] for s in shards]
    return _Carry(shards, [_sds((N_DEV * s.shape[0],) + s.shape[1:], s.dtype) for s in shards],
                  _gather_sems(len(shards)),
                  lambda ins, outs, sems: _gather_start(ins, outs, rows_per, *sems),
                  lambda ins, outs, sems: _gather_finish(ins, outs, rows_per, *sems), _gather_peers)


def _first_gather(shards, x, g):
    n = len(shards)
    rows_per = [s.shape[0] for s in shards]
    T, D = x.shape

    def body(*refs):
        x_refs, (xin_ref, g_ref), out_refs, (h_ref, r_ref) = refs[:n], refs[n:n + 2], refs[n + 2:2 * n + 2], refs[2 * n + 2:2 * n + 4]
        send_sems, recv_sems, local_sems = refs[2 * n + 4:]
        x, y, c, chips = _place()
        me, sibling = (x, y, c), (x, y, 1 - c)
        near_x, near_y, far = (*chips[0], c), (*chips[1], c), (*chips[2], c)

        def rows(a, dev, part):
            h = rows_per[a] // 2
            lo, size = {"all": (0, 2 * h), "low": (0, h), "high": (h, h)}[part]
            return out_refs[a].at[pl.ds((4 * dev[0] + 2 * dev[1] + dev[2]) * rows_per[a] + lo, size)]

        def copy(a, k, block, part, to, src=None):
            return pltpu.make_async_remote_copy(
                src_ref=rows(a, block, part) if src is None else src, dst_ref=rows(a, block, part),
                send_sem=send_sems.at[9 * a + k], recv_sem=recv_sems.at[9 * a + k], device_id=to, device_id_type=MESH)

        other = lambda dev: (dev[0], dev[1], 1 - c)
        sent = []
        for a in range(n):
            pltpu.make_async_copy(x_refs[a], rows(a, me, "all"), local_sems.at[a]).start()
            sent += [copy(a, 0, me, "all", sibling, src=x_refs[a]), copy(a, 1, me, "all", near_x, src=x_refs[a]),
                     copy(a, 2, me, "all", near_y, src=x_refs[a])]
        for cp in sent:
            cp.start()
        for i in range(T // CHUNK):
            rws = slice(i * CHUNK, (i + 1) * CHUNK)
            xv = xin_ref[rws, :]
            r = lax.rsqrt(jnp.mean(xv * xv, axis=-1, keepdims=True) + EPS)
            h_ref[rws, :] = (xv * r * g_ref[...]).astype(BF)
            r_ref[rws, :] = r
        for a in range(n):
            copy(a, 1, near_x, "all", me).wait_recv()
            copy(a, 2, near_y, "all", me).wait_recv()
            passed = [copy(a, 3, near_y, "high", near_x), copy(a, 4, near_x, "low", near_y),
                      copy(a, 5, near_x, "all", sibling), copy(a, 6, near_y, "all", sibling)]
            for cp in passed:
                cp.start()
            sent += passed
        for a in range(n):
            copy(a, 3, far, "high", me).wait_recv()
            copy(a, 4, far, "low", me).wait_recv()
            passed = [copy(a, 7, far, "high", sibling), copy(a, 8, far, "low", sibling)]
            for cp in passed:
                cp.start()
            sent += passed
        for a in range(n):
            copy(a, 0, sibling, "all", me).wait_recv()
            copy(a, 5, other(near_x), "all", me).wait_recv()
            copy(a, 6, other(near_y), "all", me).wait_recv()
            copy(a, 7, other(far), "high", me).wait_recv()
            copy(a, 8, other(far), "low", me).wait_recv()
        for cp in sent:
            cp.wait_send()
        for a in range(n):
            pltpu.make_async_copy(x_refs[a], rows(a, me, "all"), local_sems.at[a]).wait()

    vm = pl.BlockSpec(memory_space=pltpu.VMEM)
    return pl.pallas_call(
        body, name="weights_first_gather", in_specs=[*[ANY] * n, vm, vm], out_specs=[*[ANY] * n, vm, vm],
        out_shape=[*[_sds((N_DEV * s.shape[0],) + s.shape[1:], s.dtype) for s in shards], _sds((T, D), BF),
                   _sds((T, 1), F32)],
        scratch_shapes=[pltpu.SemaphoreType.DMA((9 * n,)), pltpu.SemaphoreType.DMA((9 * n,)),
                        pltpu.SemaphoreType.DMA((n,))],
        compiler_params=pltpu.CompilerParams(vmem_limit_bytes=VMEM_LIMIT_BYTES),
    )(*shards, x, g)


def _swap_carry(grads):
    n = len(grads)

    def copies(g_refs, out_refs, sems):
        send_sems, recv_sems = sems
        x, y, c, _ = _place()
        return [pltpu.make_async_remote_copy(
            src_ref=g_refs[a].at[2 * p + 1 - c], dst_ref=out_refs[a].at[p],
            send_sem=send_sems.at[4 * a + p], recv_sem=recv_sems.at[4 * a + p],
            device_id=(x, y, 1 - c), device_id_type=MESH) for a in range(n) for p in range(4)]

    def start(ins, outs, sems):
        for cp in copies(ins, outs, sems):
            cp.start()

    def finish(ins, outs, sems):
        for cp in copies(ins, outs, sems):
            cp.wait()

    def peers():
        x, y, c, _ = _place()
        return [(x, y, 1 - c)]

    return _Carry(grads, [_sds((4,) + g.shape[1:], g.dtype) for g in grads],
                  [pltpu.SemaphoreType.DMA((4 * n,)), pltpu.SemaphoreType.DMA((4 * n,))], start, finish, peers)


def _join(carries):
    carries = [c for c in carries if c is not None]
    if not carries:
        return None
    n_in = [len(c.arrays) for c in carries]
    n_out = [len(c.out_shapes) for c in carries]
    n_sem = [len(c.sems) for c in carries]

    def parts(refs, counts):
        cuts = [sum(counts[:q]) for q in range(len(counts) + 1)]
        return [refs[cuts[q]:cuts[q + 1]] for q in range(len(counts))]

    def start(ins, outs, sems):
        for c, i, o, s in zip(carries, parts(ins, n_in), parts(outs, n_out), parts(sems, n_sem)):
            c.start(i, o, s)

    def finish(ins, outs, sems):
        for c, i, o, s in zip(carries, parts(ins, n_in), parts(outs, n_out), parts(sems, n_sem)):
            c.finish(i, o, s)

    return _Carry([a for c in carries for a in c.arrays], [o for c in carries for o in c.out_shapes],
                  [s for c in carries for s in c.sems], start, finish)


def _run_carry(name, carry):
    n_in, n_out = len(carry.arrays), len(carry.out_shapes)

    def body(*refs):
        carry.start(refs[:n_in], refs[n_in:n_in + n_out], refs[n_in + n_out:])
        carry.finish(refs[:n_in], refs[n_in:n_in + n_out], refs[n_in + n_out:])

    return pl.pallas_call(body, name=name, in_specs=[ANY] * n_in, out_specs=[ANY] * n_out,
                          out_shape=carry.out_shapes, scratch_shapes=carry.sems)(*carry.arrays)


def _run_carry_async(name, carry, collective_id):
    ins = [jax.new_ref(a, memory_space=pltpu.MemorySpace.HBM) for a in carry.arrays]
    outs = [jax.empty_ref(o, memory_space=pltpu.MemorySpace.HBM) for o in carry.out_shapes]

    @pl.kernel(mesh=plsc.ScalarSubcoreMesh(axis_name="sequencer", num_cores=1), name=name,
               scratch_types=tuple(carry.sems), compiler_params=pltpu.CompilerParams(collective_id=collective_id))
    def launch(*sems):
        barrier = pltpu.get_barrier_semaphore()
        peers = carry.peers()
        for peer in peers:
            pl.semaphore_signal(barrier, inc=1, device_id=peer, device_id_type=MESH)
        pl.semaphore_wait(barrier, len(peers))
        carry.start(ins, outs, sems)
        carry.finish(ins, outs, sems)

    launch()
    return [o[...] for o in outs]


def _chip_sums(name, gs, gots, c):
    n = len(gs)

    def body(c_ref, *refs):
        for g_ref, got_ref, o_ref in zip(refs[:n], refs[n:2 * n], refs[2 * n:]):
            o_ref[...] = (g_ref[...].astype(F32) + got_ref[...].astype(F32)).astype(BF)

    mine = [pl.BlockSpec((1,) + g.shape[1:], lambda p, c_ref: (2 * p + c_ref[0], 0, 0)) for g in gs]
    slot = [pl.BlockSpec((1,) + g.shape[1:], lambda p, c_ref: (p, 0, 0)) for g in gs]
    return pl.pallas_call(
        body, name=name,
        grid_spec=pltpu.PrefetchScalarGridSpec(num_scalar_prefetch=1, grid=(4,), in_specs=[*mine, *slot],
                                               out_specs=slot),
        out_shape=[_sds((4,) + g.shape[1:], BF) for g in gs],
        compiler_params=_params(("arbitrary",)),
    )(c, *gs, *gots)


def _send_carry(sums, ks):
    n, nk = len(sums), len(ks)

    def copies(s_refs, out_refs, sems):
        send_sems, recv_sems = sems
        x, y, c, chips = _place()
        return [pltpu.make_async_remote_copy(
            src_ref=s_refs[a].at[2 * chips[k][0] + chips[k][1]], dst_ref=out_refs[a].at[q],
            send_sem=send_sems.at[nk * a + q], recv_sem=recv_sems.at[nk * a + q],
            device_id=(*chips[k], c), device_id_type=MESH) for a in range(n) for q, k in enumerate(ks)]

    def start(ins, outs, sems):
        for cp in copies(ins, outs, sems):
            cp.start()

    def finish(ins, outs, sems):
        for cp in copies(ins, outs, sems):
            cp.wait()

    def peers():
        x, y, c, chips = _place()
        return [(*chips[k], c) for k in ks]

    return _Carry(sums, [_sds((nk,) + s.shape[1:], s.dtype) for s in sums],
                  [pltpu.SemaphoreType.DMA((nk * n,)), pltpu.SemaphoreType.DMA((nk * n,))], start, finish, peers)


def _adam_math(w, g, m, v):
    m = ADAM_B1 * m + (1.0 - ADAM_B1) * g
    v = ADAM_B2 * v + (1.0 - ADAM_B2) * (g * g)
    m_hat = m / (1.0 - ADAM_B1 ** ADAM_STEP)
    v_hat = v / (1.0 - ADAM_B2 ** ADAM_STEP)
    delta = -ADAM_LR * (m_hat / (jnp.sqrt(v_hat) + ADAM_EPS) + ADAM_WD * w)
    return delta, m, v


def _adamw(name, w, g, m, v):
    rows, cols = w.shape
    tr = 256 if rows % 256 == 0 else rows

    def body(w_ref, g_ref, m_ref, v_ref, d_ref, nm_ref, nv_ref):
        d_ref[...], nm_ref[...], nv_ref[...] = _adam_math(w_ref[...], g_ref[...], m_ref[...], v_ref[...])

    t = pl.BlockSpec((tr, cols), lambda i: (i, 0))
    return pl.pallas_call(
        body, name=name, grid=(rows // tr,), in_specs=[t] * 4, out_specs=[t] * 3,
        out_shape=[_sds((rows, cols), F32)] * 3, compiler_params=_params(("arbitrary",)),
    )(w, g, m, v)


def _grad_adamw(name, g, got, got3, ids, w, m, v):
    _, rows, cols = g.shape
    n3 = len(got3)
    tr = rows // 2 if rows >= 256 else rows

    def body(ids_ref, g_ref, got_ref, *rest):
        w_ref, m_ref, v_ref, o_ref, d_ref, nm_ref, nv_ref = rest[n3:]
        tot = g_ref[0].astype(F32) + got_ref[0].astype(F32)
        for r_ref in rest[:n3]:
            for q in range(r_ref.shape[0]):
                tot = tot + r_ref[q].astype(F32)
        o_ref[...] = tot
        d_ref[...], nm_ref[...], nv_ref[...] = _adam_math(w_ref[...], tot, m_ref[...], v_ref[...])

    tile = pl.BlockSpec((tr, cols), lambda i, ids_ref: (i, 0))
    return pl.pallas_call(
        body, name=name,
        grid_spec=pltpu.PrefetchScalarGridSpec(
            num_scalar_prefetch=1, grid=(rows // tr,),
            in_specs=[pl.BlockSpec((1, tr, cols), lambda i, ids_ref: (ids_ref[0], i, 0)),
                      pl.BlockSpec((1, tr, cols), lambda i, ids_ref: (ids_ref[1], i, 0)),
                      *[pl.BlockSpec((r.shape[0], tr, cols), lambda i, ids_ref: (0, i, 0)) for r in got3],
                      tile, tile, tile],
            out_specs=[tile] * 4),
        out_shape=[_sds((rows, cols), F32)] * 4,
        compiler_params=_params(("arbitrary",)),
    )(ids, g, got, *got3, w, m, v)


SMALL_NAMES = ["g_mix_norm", "b_in", "sinks", "conv_b", "ln_g", "ln_b", "b_conv_proj", "g_ffn_norm", "g_final"]
_PACK_ROWS = 32


def _small_pack(parts):
    C = CONV_CHANNELS
    part_list = [parts["g_mix_norm"], parts["b_in"], parts["sinks"], parts["conv_b"], parts["ln_g"], parts["ln_b"],
                 parts["b_conv_proj"], parts["g_ffn_norm"], parts["g_final"], parts["loss"], parts["conv_w"]]

    def body(p_mix, p_b, p_sink, p_cb, p_lg, p_lb, p_bcp, p_ffn, p_fin, p_loss, p_cw, pack):
        pack[...] = jnp.zeros_like(pack)
        pack[0:1, :] = p_mix[...]
        pack[1:2, 0:GLU_OFF] = p_b[:, 0:GLU_OFF]
        pack[2:3, :] = p_b[:, GLU_OFF:GATE_OFF]
        pack[3:4, :] = p_b[:, GATE_OFF:GATE_OFF + D_MODEL]
        pack[4:5, :] = p_b[:, GATE_OFF + D_MODEL:]
        pack[5:6, 0:128] = p_sink[...]
        pack[6:7, 0:C] = p_cb[...]
        pack[6:7, C:2 * C] = p_lg[...]
        pack[7:8, 0:C] = p_lb[...]
        pack[8:9, :] = p_bcp[...]
        pack[9:10, :] = p_ffn[...]
        pack[10:11, :] = p_fin[...]
        pack[11:12, 0:128] = jnp.broadcast_to(p_loss[...], (1, 128))
        pack[12:28, 0:C] = p_cw[0:16, :]
        pack[12:28, C:2 * C] = p_cw[16:32, :]

    vm = pl.BlockSpec(memory_space=pltpu.VMEM)
    return pl.pallas_call(body, name="small_pack", in_specs=[vm] * len(part_list), out_specs=vm,
                          out_shape=_sds((_PACK_ROWS, D_MODEL), F32))(*part_list)


def _small_adamw(gathered, small_w, small_m, small_v):
    C = CONV_CHANNELS
    names = SMALL_NAMES
    widths = [small_w[k].shape[1] for k in names]
    n_small = len(names)

    def body(*refs):
        tot_ref = refs[0]
        w_refs = refs[1:1 + n_small]
        m_refs = refs[1 + n_small:1 + 2 * n_small]
        v_refs = refs[1 + 2 * n_small:1 + 3 * n_small]
        o = 1 + 3 * n_small
        loss_ref, cw_ref = refs[o], refs[o + 1]
        out_refs = refs[o + 2:o + 2 + 4 * n_small]
        tot = tot_ref[0:_PACK_ROWS, :]
        for d in range(1, N_DEV):
            tot = tot + tot_ref[d * _PACK_ROWS:(d + 1) * _PACK_ROWS, :]
        loss_ref[...] = tot[11:12, 0:1]
        cw_ref[0:16, :] = tot[12:28, 0:C]
        cw_ref[16:32, :] = tot[12:28, C:2 * C]
        grads = dict(
            g_mix_norm=tot[0:1, :],
            b_in=jnp.concatenate([tot[1:2, 0:GLU_OFF], tot[2:3, :], tot[3:4, :], tot[4:5, :]], axis=1),
            sinks=tot[5:6, 0:N_Q_HEADS], conv_b=tot[6:7, 0:C], ln_g=tot[6:7, C:2 * C], ln_b=tot[7:8, 0:C],
            b_conv_proj=tot[8:9, :], g_ffn_norm=tot[9:10, :], g_final=tot[10:11, :])
        for s, k in enumerate(names):
            g = grads[k]
            d, nm, nv = _adam_math(w_refs[s][...], g, m_refs[s][...], v_refs[s][...])
            out_refs[4 * s][...] = g
            out_refs[4 * s + 1][...] = d
            out_refs[4 * s + 2][...] = nm
            out_refs[4 * s + 3][...] = nv

    vm = pl.BlockSpec(memory_space=pltpu.VMEM)
    args = [gathered, *[small_w[k] for k in names], *[small_m[k] for k in names], *[small_v[k] for k in names]]
    out_shape = [_sds((1, 1), F32), _sds((CONV_PAD, C), F32)]
    for wd in widths:
        out_shape += [_sds((1, wd), F32)] * 4
    res = pl.pallas_call(
        body, name="small_adamw",
        in_specs=[vm] * len(args), out_specs=[vm] * len(out_shape), out_shape=out_shape,
        compiler_params=pltpu.CompilerParams(vmem_limit_bytes=VMEM_LIMIT_BYTES),
    )(*args)
    return res[0], res[1], {k: res[2 + 4 * s:6 + 4 * s] for s, k in enumerate(names)}


BIG = dict(w_in=True, w_attn_proj=True, w_conv_proj=True, w_out=False, w_ffn_in=True, w_ffn_down=False)
WEIGHT_NAMES = ["g_mix_norm", "w_in", "b_in", "sinks", "conv_w", "conv_b", "ln_g", "ln_b", "w_attn_proj",
                "w_conv_proj", "b_conv_proj", "w_out", "g_ffn_norm", "w_ffn_in", "w_ffn_down", "g_final"]


class _Plan:
    GROUPS = dict(down=["w_ffn_down"], ffn=["w_ffn_in"], mix=["w_out", "w_attn_proj", "w_conv_proj"], inp=["w_in"])
    ALL = (0, 1, 2)
    RIDES = dict(
        gather_mix=[("gather", ["w_attn_proj", "w_conv_proj", "w_out"])], gather_ffn=[("gather", ["w_ffn_in"])],
        gather_down=[("gather", ["w_ffn_down"])],
        ffn_in_bwd=[("swap", "down")], send_down=[("send", "down", ALL)],
        out_proj_bwd_merge=[("swap", "ffn")], send_ffn=[("send", "ffn", ALL)],
        conv_bwd=[("swap", "mix")], send_mix=[("send", "mix", ALL)],
        swap_inp=[("swap", "inp")], send_inp=[("send", "inp", ALL)])
    ASYNC = dict(gather_mix=1, gather_ffn=2, gather_down=3, send_down=4, send_ffn=5, send_mix=6, send_inp=7)

    def __init__(self, shards, c1):
        self.shards, self.c1 = shards, c1
        self.full, self.slots, self.got, self.sums, self.got3 = {}, {}, {}, {}, {}

    def weight(self, name):
        return self.full[name]

    def grad_ready(self, grads):
        for k, g in grads.items():
            self.slots[k] = g.reshape(N_DEV, g.shape[0] // N_DEV, g.shape[1])

    def _one(self, kind, what, ks=None):
        if kind == "gather":
            return _gather_carry([self.shards[k] for k in what])
        names = self.GROUPS[what]
        if kind == "swap":
            return _swap_carry([self.slots[k] for k in names])
        return _send_carry([self.sums[k] for k in names], ks)

    def carry(self, call):
        return _join([self._one(*ride) for ride in self.RIDES.get(call, [])])

    def done(self, call, outs):
        outs = list(outs)
        for kind, what, *_ in self.RIDES.get(call, []):
            names = what if kind == "gather" else self.GROUPS[what]
            mine, outs = outs[:len(names)], outs[len(names):]
            if kind == "gather":
                self.full.update(zip(names, mine))
            elif kind == "send":
                for k, r in zip(names, mine):
                    self.got3.setdefault(k, []).append(r)
            else:
                self.got.update(zip(names, mine))
                self.sums.update(zip(names, _chip_sums(f"chip_sums_{what}", [self.slots[k] for k in names], mine, self.c1)))

    def alone(self, call):
        self.done(call, _run_carry(call, self.carry(call)))

    def behind(self, group, x):
        return lax.optimization_barrier((x, tuple(self.sums[k] for k in self.GROUPS[group])))[0]

    def launch(self, call, after=None):
        carry = self._one(*self.RIDES[call][0])
        if after is not None:
            carry.arrays = list(lax.optimization_barrier((tuple(carry.arrays), after))[0])
        self.done(call, _run_carry_async(call, carry, self.ASYNC[call]))


def kernel(x, g_mix_norm, w_in, b_in, sinks, conv_w, conv_b, ln_g, ln_b, w_attn_proj, w_conv_proj, b_conv_proj, w_out, g_ffn_norm, w_ffn_in, w_ffn_down, g_final, loss_target, m_g_mix_norm, m_w_in, m_b_in, m_sinks, m_conv_w, m_conv_b, m_ln_g, m_ln_b, m_w_attn_proj, m_w_conv_proj, m_b_conv_proj, m_w_out, m_g_ffn_norm, m_w_ffn_in, m_w_ffn_down, m_g_final, v_g_mix_norm, v_w_in, v_b_in, v_sinks, v_conv_w, v_conv_b, v_ln_g, v_ln_b, v_w_attn_proj, v_w_conv_proj, v_b_conv_proj, v_w_out, v_g_ffn_norm, v_w_ffn_in, v_w_ffn_down, v_g_final):
    w = dict(g_mix_norm=g_mix_norm, w_in=w_in, b_in=b_in, sinks=sinks, conv_w=conv_w, conv_b=conv_b, ln_g=ln_g,
             ln_b=ln_b, w_attn_proj=w_attn_proj, w_conv_proj=w_conv_proj, b_conv_proj=b_conv_proj, w_out=w_out,
             g_ffn_norm=g_ffn_norm, w_ffn_in=w_ffn_in, w_ffn_down=w_ffn_down, g_final=g_final)
    m = dict(g_mix_norm=m_g_mix_norm, w_in=m_w_in, b_in=m_b_in, sinks=m_sinks, conv_w=m_conv_w, conv_b=m_conv_b,
             ln_g=m_ln_g, ln_b=m_ln_b, w_attn_proj=m_w_attn_proj, w_conv_proj=m_w_conv_proj,
             b_conv_proj=m_b_conv_proj, w_out=m_w_out, g_ffn_norm=m_g_ffn_norm, w_ffn_in=m_w_ffn_in,
             w_ffn_down=m_w_ffn_down, g_final=m_g_final)
    v = dict(g_mix_norm=v_g_mix_norm, w_in=v_w_in, b_in=v_b_in, sinks=v_sinks, conv_w=v_conv_w, conv_b=v_conv_b,
             ln_g=v_ln_g, ln_b=v_ln_b, w_attn_proj=v_w_attn_proj, w_conv_proj=v_w_conv_proj,
             b_conv_proj=v_b_conv_proj, w_out=v_w_out, g_ffn_norm=v_g_ffn_norm, w_ffn_in=v_w_ffn_in,
             w_ffn_down=v_w_ffn_down, g_final=v_g_final)
    ax, ay, ac = lax.axis_index("x"), lax.axis_index("y"), lax.axis_index("c")
    me = 4 * ax + 2 * ay + ac
    chip = 2 * ax + ay

    shards = {k: (w[k][0].T if tr else w[k][0]).astype(BF) for k, tr in BIG.items()}
    cw_shard = jnp.pad(conv_w[0].T, ((0, 0), (0, 1))).reshape(16, 128)
    wi_t, cw_full, h, r1 = _first_gather([shards["w_in"], cw_shard], x[0], g_mix_norm)
    conv_full = cw_full.reshape(CONV_CHANNELS, CONV_PAD).T

    as_row = lambda a: a.reshape(1, -1)
    small_w = {k: as_row(w[k]) for k in SMALL_NAMES}
    small_m = {k: as_row(m[k]) for k in SMALL_NAMES}
    small_v = {k: as_row(v[k]) for k in SMALL_NAMES}
    plan = _Plan(shards, ac.reshape(1).astype(jnp.int32))
    plan.launch("gather_mix", after=wi_t)
    dx, parts = _local_step(x[0], h, r1, loss_target[0], small_w, wi_t, conv_full, plan)

    ids = jnp.stack([me, chip]).astype(jnp.int32)
    grads, delta, new_m, new_v, after = {}, {}, {}, {}, dx
    packed = _small_pack(parts)
    for k in sorted(BIG, key=lambda k: k == "w_in"):
        if k == "w_in":
            packed = lax.optimization_barrier((packed, after))[0]
            small_gathered, = _run_carry_async("small_gather", _gather_carry([packed]), 8)
        flip = (lambda a: a.T) if BIG[k] else (lambda a: a)
        wk = lax.optimization_barrier((w[k][0], after))[0]
        outs = _grad_adamw(f"grad_adamw_{k}", plan.slots[k], plan.got[k], plan.got3[k], ids,
                           flip(wk), flip(m[k][0]), flip(v[k][0]))
        after = outs[0]
        grads[k], delta[k], new_m[k], new_v[k] = (flip(a)[None] for a in outs)

    loss, cw_grad, small_out = _small_adamw(small_gathered, small_w, small_m, small_v)
    for k in SMALL_NAMES:
        g, d, nm, nv = (a.reshape(w[k].shape) for a in small_out[k])
        grads[k], delta[k], new_m[k], new_v[k] = g, d, nm, nv
    cw_mine = lax.dynamic_slice(cw_grad, (0, me * 64), (CONV_WIDTH, 64))
    d, nm, nv = _adamw("adamw_conv_w", conv_w[0], cw_mine, m_conv_w[0], v_conv_w[0])
    grads["conv_w"], delta["conv_w"], new_m["conv_w"], new_v["conv_w"] = cw_mine[None], d[None], nm[None], nv[None]

    return (loss.reshape(()), dx[None], *[grads[k] for k in WEIGHT_NAMES], *[delta[k] for k in WEIGHT_NAMES],
            *[new_m[k] for k in WEIGHT_NAMES], *[new_v[k] for k in WEIGHT_NAMES])
```

```python
import functools

import jax
import jax.numpy as jnp
from jax import lax
from jax.experimental import pallas as pl
from jax.experimental.pallas import tpu as pltpu
from jax.experimental.pallas import tpu_sc as plsc

F32 = jnp.float32
BF = jnp.bfloat16

SEQ = 2048
D_MODEL = 1024
HEAD_DIM = 64
N_Q_HEADS = 8
N_KV_HEADS = 2
GROUP = N_Q_HEADS // N_KV_HEADS
BLOCK = 128
ATTN_WIDTH = 512
KV_WIDTH = 128
CONV_CHANNELS = 512
CONV_WIDTH = 31
CONV_PAD = 32
GLU_OFF = 768
GATE_OFF = 1792
IN_WIDTH = 3840
D_FF = 2816
EPS = 1e-5
NEG = -1e30
N_DEV = 8

ADAM_LR = 0.001
ADAM_B1 = 0.9
ADAM_B2 = 0.999
ADAM_EPS = 1e-08
ADAM_WD = 0.01
ADAM_STEP = 10

VMEM_LIMIT_BYTES = 56 * 1024 * 1024
MESH = pl.DeviceIdType.MESH
ANY = pl.BlockSpec(memory_space=pl.ANY)

_DIMS = {"NN": (((1,), (0,)), ((), ())), "NT": (((1,), (1,)), ((), ())), "TN": (((0,), (0,)), ((), ()))}


def _params(sem):
    return pltpu.CompilerParams(dimension_semantics=sem, vmem_limit_bytes=VMEM_LIMIT_BYTES)


class _Carry:
    def __init__(self, arrays, out_shapes, sems, start, finish, peers=None):
        self.arrays, self.out_shapes, self.sems, self.start, self.finish = arrays, out_shapes, sems, start, finish
        self.peers = peers


def _carry_io(carry):
    if carry is None:
        return [], [], []
    return list(carry.arrays), list(carry.out_shapes), list(carry.sems)


def _matmul(name, a_list, b, mode, *, m, n, tm, tn, tk=None, epilogue, extra=(), outs, b_off=(0, 0), alias=None,
            scratch=(), carry=None):
    seg_k = [a.shape[0] if mode == "TN" else a.shape[1] for a in a_list]
    whole = tk is None
    seg_nk = [1] * len(a_list) if whole else [ks // tk for ks in seg_k]
    nk = 1 if whole else sum(seg_nk)
    starts = [sum(seg_nk[:s]) for s in range(len(seg_nk))]
    k_starts = [sum(seg_k[:s]) for s in range(len(seg_k))]
    k_tot = sum(seg_k)
    n_a, n_extra, n_out = len(a_list), len(extra), len(outs)

    a_specs = []
    for st, ns, ks in zip(starts, seg_nk, seg_k):
        if mode == "TN":
            a_specs.append(pl.BlockSpec((ks if whole else tk, tm), lambda j, i, k: (k, i)))
        elif whole:
            a_specs.append(pl.BlockSpec((tm, ks), lambda j, i, k: (i, 0)))
        else:
            a_specs.append(pl.BlockSpec((tm, tk), functools.partial(
                lambda j, i, k, st, ns: (i, jnp.clip(k - st, 0, ns - 1)), st=st, ns=ns)))
    bk = k_tot if whole else tk
    if mode == "NT":
        b_spec = pl.BlockSpec((tn, bk), lambda j, i, k: (b_off[0] + j, b_off[1] + k))
    else:
        b_spec = pl.BlockSpec((bk, tn), lambda j, i, k: (b_off[0] + k, b_off[1] + j))
    n_alias = 0 if alias is None else 1
    c_in, c_out, c_sems = _carry_io(carry)
    n_acc = 0 if whole else 1
    nj, ni = n // tn, m // tm

    def body(*refs):
        pos = [n_a, 1, n_alias, n_extra, len(c_in), n_out, len(c_out), n_acc, len(scratch), len(c_sems)]
        cuts = [sum(pos[:q]) for q in range(len(pos) + 1)]
        a_refs, (b_ref,), _, ex, ci_refs, out_refs, co_refs, acc_refs, scr, cs_refs = (
            refs[cuts[q]:cuts[q + 1]] for q in range(len(pos)))
        j, i, k = pl.program_id(0), pl.program_id(1), pl.program_id(2)
        ids = (j, i)
        if carry is not None:
            @pl.when((j == 0) & (i == 0) & (k == 0))
            def _():
                carry.start(ci_refs, co_refs, cs_refs)

        def dot(a_ref, bv):
            return lax.dot_general(a_ref[...].astype(BF), bv.astype(BF), _DIMS[mode], preferred_element_type=F32)

        if whole:
            tot = None
            for a_ref, k0, ks in zip(a_refs, k_starts, seg_k):
                if n_a == 1:
                    bv = b_ref[...]
                else:
                    bv = b_ref[:, k0:k0 + ks] if mode == "NT" else b_ref[k0:k0 + ks, :]
                part = dot(a_ref, bv)
                tot = part if tot is None else tot + part
            epilogue(tot, ex, out_refs, ids, scr)
        else:
            acc, = acc_refs

            @pl.when(k == 0)
            def _():
                acc[...] = jnp.zeros_like(acc)

            for a_ref, st, ns in zip(a_refs, starts, seg_nk):
                if n_a == 1:
                    acc[...] += dot(a_ref, b_ref[...])
                else:
                    @pl.when((k >= st) & (k < st + ns))
                    def _(a_ref=a_ref):
                        acc[...] += dot(a_ref, b_ref[...])

            @pl.when(k == nk - 1)
            def _():
                epilogue(acc[...], ex, out_refs, ids, scr)

        if carry is not None:
            @pl.when((j == nj - 1) & (i == ni - 1) & (k == nk - 1))
            def _():
                carry.finish(ci_refs, co_refs, cs_refs)

    in_specs = [*a_specs, b_spec]
    args = [*a_list, b]
    io_alias = {}
    if alias is not None:
        in_specs.append(pl.BlockSpec(memory_space=pl.ANY))
        args.append(alias[0])
        io_alias = {n_a + 1: alias[1]}
    in_specs += [s for _, s in extra] + [pl.BlockSpec(memory_space=pl.ANY)] * len(c_in)
    args += [x for x, _ in extra] + c_in
    res = pl.pallas_call(
        body, name=name, grid=(nj, ni, nk), in_specs=in_specs,
        out_specs=[s for _, s in outs] + [pl.BlockSpec(memory_space=pl.ANY)] * len(c_out),
        out_shape=[o for o, _ in outs] + c_out,
        scratch_shapes=[*([] if whole else [pltpu.VMEM((tm, tn), F32)]), *scratch, *c_sems],
        input_output_aliases=io_alias,
        compiler_params=_params(("arbitrary", "arbitrary", "arbitrary")),
    )(*args)
    return res if carry is None else (res[:n_out], res[n_out:])


def _tile(tm, tn):
    return pl.BlockSpec((tm, tn), lambda j, i, k: (i, j))


def _row(tn):
    return pl.BlockSpec((1, tn), lambda j, i, k: (0, j))


def _store(dtype):
    def ep(acc, ex, outs, ids, scr):
        outs[0][...] = acc.astype(dtype)
    return ep


def _sds(shape, dtype):
    return jax.ShapeDtypeStruct(shape, dtype)


def _rms_bwd(dh, xv, r, g):
    xh = xv * r
    dxh = dh * g
    dx = r * (dxh - xh * jnp.mean(dxh * xh, axis=-1, keepdims=True))
    return dx, jnp.sum(dh * xh, axis=0, keepdims=True)


def _accumulate_rows(ref, val, first):
    @pl.when(first)
    def _():
        ref[...] = val

    @pl.when(jnp.logical_not(first))
    def _():
        ref[...] += val


def _loss_head(xv, g, target):
    r = lax.rsqrt(jnp.mean(xv * xv, axis=-1, keepdims=True) + EPS)
    err = xv * r * g - target
    dx, dg = _rms_bwd(err * (1.0 / xv.shape[-1]), xv, r, g)
    part = 0.5 * jnp.sum(jnp.mean(err * err, axis=-1, keepdims=True), axis=0, keepdims=True)
    return dx, dg, part


def _lane_half(shape, h):
    lane = lax.broadcasted_iota(jnp.int32, shape, 1)
    return (lane >= HEAD_DIM * h) & (lane < HEAD_DIM * (h + 1))


def _to_half(v, w, h):
    if w != h:
        v = pltpu.roll(v, HEAD_DIM, 1)
    return jnp.where(_lane_half(v.shape, h), v, 0.0)


def _attn_block(qkv_ref, sinks_ref, n, h):
    r0 = pl.multiple_of(n * BLOCK, BLOCK)
    p0 = pl.multiple_of(jnp.maximum(n - 1, 0) * BLOCK, BLOCK)
    rows = pl.ds(r0, BLOCK)
    prev = pl.ds(p0, BLOCK)
    k2 = jnp.concatenate([qkv_ref[prev, ATTN_WIDTH:ATTN_WIDTH + KV_WIDTH],
                          qkv_ref[rows, ATTN_WIDTH:ATTN_WIDTH + KV_WIDTH]], axis=0)
    v2 = jnp.concatenate([qkv_ref[prev, ATTN_WIDTH + KV_WIDTH:ATTN_WIDTH + 2 * KV_WIDTH],
                          qkv_ref[rows, ATTN_WIDTH + KV_WIDTH:ATTN_WIDTH + 2 * KV_WIDTH]], axis=0)
    qs = []
    for g in range(GROUP):
        hq = GROUP * h + g
        blk = qkv_ref[rows, (hq // 2) * 128:(hq // 2 + 1) * 128].astype(F32)
        qs.append(_to_half(blk, hq % 2, h))
    q4 = jnp.concatenate(qs, axis=0).astype(BF)
    s = lax.dot_general(q4, k2, _DIMS["NT"], preferred_element_type=F32) * (HEAD_DIM ** -0.5)
    shape = s.shape
    row = lax.broadcasted_iota(jnp.int32, shape, 0)
    qi = row & (BLOCK - 1)
    kj = lax.broadcasted_iota(jnp.int32, shape, 1)
    diff = qi + BLOCK - kj
    valid = (diff >= 0) & (diff < BLOCK) & ((kj >= BLOCK) | (n > 0))
    s = jnp.where(valid, s, NEG)
    row1 = lax.broadcasted_iota(jnp.int32, (shape[0], 1), 0)
    sink = jnp.zeros((shape[0], 1), F32)
    for g in range(GROUP):
        sink = jnp.where((row1 >= g * BLOCK) & (row1 < (g + 1) * BLOCK), sinks_ref[0, GROUP * h + g], sink)
    m = jnp.maximum(jnp.max(s, axis=-1, keepdims=True), sink)
    e = jnp.exp(s - m)
    es = jnp.exp(sink - m)
    inv = 1.0 / (jnp.sum(e, axis=-1, keepdims=True) + es)
    return e * inv, es * inv, q4, k2, v2, rows, prev


def _attn_fwd(proj, sinks, carry=None):
    T = proj.shape[0]
    c_in, c_out, c_sems = _carry_io(carry)

    def body(*refs):
        qkv_ref, sinks_ref = refs[:2]
        ci_refs = refs[2:2 + len(c_in)]
        o_ref = refs[2 + len(c_in)]
        co_refs = refs[3 + len(c_in):3 + len(c_in) + len(c_out)]
        cs_refs = refs[3 + len(c_in) + len(c_out):]
        if carry is not None:
            carry.start(ci_refs, co_refs, cs_refs)

        def blk(n, z):
            outs = [None] * (N_Q_HEADS // 2)
            for h in range(N_KV_HEADS):
                p, _, _, _, v2, rows, _ = _attn_block(qkv_ref, sinks_ref, n, h)
                o = lax.dot_general(p.astype(BF), v2, _DIMS["NN"], preferred_element_type=F32)
                for g in range(GROUP):
                    hq = GROUP * h + g
                    piece = jnp.where(_lane_half((BLOCK, 128), h), o[g * BLOCK:(g + 1) * BLOCK], 0.0)
                    if hq % 2 != h:
                        piece = pltpu.roll(piece, HEAD_DIM, 1)
                    outs[hq // 2] = piece if outs[hq // 2] is None else outs[hq // 2] + piece
            for pb in range(N_Q_HEADS // 2):
                o_ref[rows, pb * 128:(pb + 1) * 128] = outs[pb].astype(BF)
            return z

        lax.fori_loop(0, T // BLOCK, blk, 0)
        if carry is not None:
            carry.finish(ci_refs, co_refs, cs_refs)

    res = pl.pallas_call(
        body, name="attn_fwd", grid=(1,),
        in_specs=[pl.BlockSpec((T, GLU_OFF), lambda i: (0, 0)), pl.BlockSpec(memory_space=pltpu.SMEM),
                  *[ANY] * len(c_in)],
        out_specs=[pl.BlockSpec((T, ATTN_WIDTH), lambda i: (0, 0)), *[ANY] * len(c_out)],
        out_shape=[_sds((T, ATTN_WIDTH), BF), *c_out], scratch_shapes=c_sems,
        compiler_params=_params(("arbitrary",)),
    )(proj, sinks, *c_in)
    return res[0], res[1:]


def _attn_bwd(proj, d_o, sinks, carry=None):
    T = proj.shape[0]
    c_in, c_out, c_sems = _carry_io(carry)

    def body(*refs):
        qkv_ref, do_ref, sinks_ref = refs[:3]
        ci_refs = refs[3:3 + len(c_in)]
        dqkv_ref, dsink_ref = refs[3 + len(c_in):5 + len(c_in)]
        co_refs = refs[5 + len(c_in):5 + len(c_in) + len(c_out)]
        dk_acc, dv_acc = refs[5 + len(c_in) + len(c_out):7 + len(c_in) + len(c_out)]
        cs_refs = refs[7 + len(c_in) + len(c_out):]
        if carry is not None:
            carry.start(ci_refs, co_refs, cs_refs)
        dsink_ref[...] = jnp.zeros_like(dsink_ref)
        dk_acc[...] = jnp.zeros_like(dk_acc)
        dv_acc[...] = jnp.zeros_like(dv_acc)

        def blk(n, carry):
            dqs = [None] * (N_Q_HEADS // 2)
            for h in range(N_KV_HEADS):
                p, psink, q4, k2, v2, rows, prev = _attn_block(qkv_ref, sinks_ref, n, h)
                dos = []
                for g in range(GROUP):
                    hq = GROUP * h + g
                    dos.append(_to_half(do_ref[rows, (hq // 2) * 128:(hq // 2 + 1) * 128].astype(F32), hq % 2, h))
                do4 = jnp.concatenate(dos, axis=0).astype(BF)
                dp = lax.dot_general(do4, v2, _DIMS["NT"], preferred_element_type=F32)
                delta = jnp.sum(p * dp, axis=-1, keepdims=True)
                ds = (p * (dp - delta) * (HEAD_DIM ** -0.5)).astype(BF)
                dsk = psink * delta
                for g in range(GROUP):
                    hq = GROUP * h + g
                    tot = -jnp.sum(dsk[g * BLOCK:(g + 1) * BLOCK], axis=0, keepdims=True)
                    lane = lax.broadcasted_iota(jnp.int32, (1, 128), 1)
                    dsink_ref[...] += jnp.where(lane == hq, tot, 0.0)
                dq = lax.dot_general(ds, k2, _DIMS["NN"], preferred_element_type=F32)
                dk = lax.dot_general(ds, q4, _DIMS["TN"], preferred_element_type=F32)
                dv = lax.dot_general(p.astype(BF), do4, _DIMS["TN"], preferred_element_type=F32)
                dk_acc[prev, :] += dk[:BLOCK]
                dk_acc[rows, :] += dk[BLOCK:]
                dv_acc[prev, :] += dv[:BLOCK]
                dv_acc[rows, :] += dv[BLOCK:]
                for g in range(GROUP):
                    hq = GROUP * h + g
                    piece = jnp.where(_lane_half((BLOCK, 128), h), dq[g * BLOCK:(g + 1) * BLOCK], 0.0)
                    if hq % 2 != h:
                        piece = pltpu.roll(piece, HEAD_DIM, 1)
                    dqs[hq // 2] = piece if dqs[hq // 2] is None else dqs[hq // 2] + piece
            for pb in range(N_Q_HEADS // 2):
                dqkv_ref[rows, pb * 128:(pb + 1) * 128] = dqs[pb].astype(BF)
            return carry

        lax.fori_loop(0, T // BLOCK, blk, 0)
        dqkv_ref[:, ATTN_WIDTH:ATTN_WIDTH + KV_WIDTH] = dk_acc[...].astype(BF)
        dqkv_ref[:, ATTN_WIDTH + KV_WIDTH:] = dv_acc[...].astype(BF)
        if carry is not None:
            carry.finish(ci_refs, co_refs, cs_refs)

    res = pl.pallas_call(
        body, name="attn_bwd", grid=(1,),
        in_specs=[pl.BlockSpec((T, GLU_OFF), lambda i: (0, 0)), pl.BlockSpec((T, ATTN_WIDTH), lambda i: (0, 0)),
                  pl.BlockSpec(memory_space=pltpu.SMEM), *[ANY] * len(c_in)],
        out_specs=[pl.BlockSpec((T, GLU_OFF), lambda i: (0, 0)), pl.BlockSpec((1, 128), lambda i: (0, 0)),
                   *[ANY] * len(c_out)],
        out_shape=[_sds((T, GLU_OFF), BF), _sds((1, 128), F32), *c_out],
        scratch_shapes=[pltpu.VMEM((T, KV_WIDTH), F32), pltpu.VMEM((T, KV_WIDTH), F32), *c_sems],
        compiler_params=_params(("arbitrary",)),
    )(proj, d_o, sinks, *c_in)
    return res[:2], res[2:]


CHUNK = 256
SUB = 32
WIN = CHUNK + 32
PAD_ROWS = SEQ + 2 * CONV_PAD
_GLU_SPECS = [pl.BlockSpec((SEQ, 256), functools.partial(lambda i, c: (0, c), c=GLU_OFF // 256 + c)) for c in range(4)]


def _glu_to_pad(a0, a1, b0, b1, zpad):
    C = CONV_CHANNELS
    zpad[0:CONV_PAD, :] = jnp.zeros((CONV_PAD, C), F32)
    zpad[CONV_PAD + SEQ:, :] = jnp.zeros((CONV_PAD, C), F32)
    zpad[CONV_PAD:CONV_PAD + SEQ, 0:256] = a0[...].astype(F32) * jax.nn.sigmoid(b0[...].astype(F32))
    zpad[CONV_PAD:CONV_PAD + SEQ, 256:C] = a1[...].astype(F32) * jax.nn.sigmoid(b1[...].astype(F32))


def _tap_windows(src, base, win):
    for b in range(8):
        win[b, 0:WIN - 8, :] = src[base + b:base + b + WIN - 8, :]


def _taps(win, w_ref, init, out, flip):
    def sub(si, carry):
        r0 = pl.multiple_of(si * SUB, SUB)
        acc = jnp.broadcast_to(init, (SUB, CONV_CHANNELS))
        for k in range(CONV_WIDTH):
            wk = (CONV_WIDTH - 1 - k) if flip else k
            acc = acc + w_ref[wk:wk + 1, :] * win[k % 8, pl.ds(r0 + 8 * (k // 8), SUB), :]
        out[pl.ds(r0, SUB), :] = acc
        return carry

    lax.fori_loop(0, CHUNK // SUB, sub, 0)


def _tap_grads(win, du, dwacc):
    def sub(si, carry):
        r0 = pl.multiple_of(si * SUB, SUB)
        d = du[pl.ds(r0, SUB), :]
        for k in range(CONV_WIDTH):
            p = d * win[k % 8, pl.ds(r0 + 8 * (k // 8), SUB), :]
            dwacc[8 * k:8 * k + 8, :] += (p[0:8] + p[8:16]) + (p[16:24] + p[24:32])
        return carry

    lax.fori_loop(0, CHUNK // SUB, sub, 0)


def _ln_parts(u):
    mu = jnp.mean(u, axis=-1, keepdims=True)
    xc = u - mu
    rstd = lax.rsqrt(jnp.mean(xc * xc, axis=-1, keepdims=True) + EPS)
    return xc * rstd, rstd


def _conv_fwd(proj, conv_w, conv_b, ln_g, ln_b, carry=None):
    T, C = proj.shape[0], CONV_CHANNELS
    vec = pl.BlockSpec((1, C), lambda i: (0, 0))
    c_in, c_out, c_sems = _carry_io(carry)

    def body(*refs):
        a0, a1, b0, b1, w_ref, cb_ref, g_ref, be_ref = refs[:8]
        ci_refs = refs[8:8 + len(c_in)]
        c_ref, u_ref = refs[8 + len(c_in):10 + len(c_in)]
        co_refs = refs[10 + len(c_in):10 + len(c_in) + len(c_out)]
        zpad, win, ubuf = refs[10 + len(c_in) + len(c_out):13 + len(c_in) + len(c_out)]
        cs_refs = refs[13 + len(c_in) + len(c_out):]
        if carry is not None:
            carry.start(ci_refs, co_refs, cs_refs)
        _glu_to_pad(a0, a1, b0, b1, zpad)
        for ci in range(T // CHUNK):
            _tap_windows(zpad, ci * CHUNK + CONV_PAD - (CONV_WIDTH - 1), win)
            _taps(win, w_ref, cb_ref[...], ubuf, False)
            u = ubuf[...]
            u_ref[ci * CHUNK:(ci + 1) * CHUNK, :] = u
            xh, _ = _ln_parts(u)
            ln = xh * g_ref[...] + be_ref[...]
            c_ref[ci * CHUNK:(ci + 1) * CHUNK, :] = (ln * jax.nn.sigmoid(ln)).astype(BF)
        if carry is not None:
            carry.finish(ci_refs, co_refs, cs_refs)

    res = pl.pallas_call(
        body, name="conv_fwd", grid=(1,),
        in_specs=[*_GLU_SPECS, pl.BlockSpec((CONV_PAD, C), lambda i: (0, 0)), vec, vec, vec, *[ANY] * len(c_in)],
        out_specs=[pl.BlockSpec((T, C), lambda i: (0, 0)), pl.BlockSpec((T, C), lambda i: (0, 0)), *[ANY] * len(c_out)],
        out_shape=[_sds((T, C), BF), _sds((T, C), F32), *c_out],
        scratch_shapes=[pltpu.VMEM((PAD_ROWS, C), F32), pltpu.VMEM((8, WIN, C), F32), pltpu.VMEM((CHUNK, C), F32),
                        *c_sems],
        compiler_params=_params(("arbitrary",)),
    )(proj, proj, proj, proj, conv_w, conv_b, ln_g, ln_b, *c_in)
    return res[:2], res[2:]


def _conv_bwd(proj, u, d_c, conv_w, conv_b, ln_g, ln_b, carry=None):
    T, C = proj.shape[0], CONV_CHANNELS
    vec = pl.BlockSpec((1, C), lambda i: (0, 0))
    wspec = pl.BlockSpec((CONV_PAD, C), lambda i: (0, 0))
    c_in, c_out, c_sems = _carry_io(carry)

    def body(*refs):
        a0, a1, b0, b1, u_ref, dc_ref, w_ref, cb_ref, g_ref, be_ref = refs[:10]
        ci_refs = refs[10:10 + len(c_in)]
        o = 10 + len(c_in)
        dglu_ref, dw_ref, dcb_ref, dg_ref, dbe_ref = refs[o:o + 5]
        co_refs = refs[o + 5:o + 5 + len(c_out)]
        zpad, dupad, win, ubuf, dwacc = refs[o + 5 + len(c_out):o + 10 + len(c_out)]
        cs_refs = refs[o + 10 + len(c_out):]
        if carry is not None:
            carry.start(ci_refs, co_refs, cs_refs)
        _glu_to_pad(a0, a1, b0, b1, zpad)
        dupad[T:, :] = jnp.zeros((2 * CONV_PAD, C), F32)
        dwacc[...] = jnp.zeros_like(dwacc)
        dcb_ref[...] = jnp.zeros_like(dcb_ref)
        dg_ref[...] = jnp.zeros_like(dg_ref)
        dbe_ref[...] = jnp.zeros_like(dbe_ref)
        for ci in range(T // CHUNK):
            rows = slice(ci * CHUNK, (ci + 1) * CHUNK)
            _tap_windows(zpad, ci * CHUNK + CONV_PAD - (CONV_WIDTH - 1), win)
            xh, rstd = _ln_parts(u_ref[rows, :])
            ln = xh * g_ref[...] + be_ref[...]
            sg = jax.nn.sigmoid(ln)
            dln = dc_ref[rows, :].astype(F32) * (sg * (1.0 + ln * (1.0 - sg)))
            dg_ref[...] += jnp.sum(dln * xh, axis=0, keepdims=True)
            dbe_ref[...] += jnp.sum(dln, axis=0, keepdims=True)
            dxh = dln * g_ref[...]
            du = rstd * (dxh - jnp.mean(dxh, axis=-1, keepdims=True)
                         - xh * jnp.mean(dxh * xh, axis=-1, keepdims=True))
            dupad[rows, :] = du
            dcb_ref[...] += jnp.sum(du, axis=0, keepdims=True)
            _tap_grads(win, dupad.at[rows, :], dwacc)
        for k in range(CONV_WIDTH):
            dw_ref[k:k + 1, :] = jnp.sum(dwacc[8 * k:8 * k + 8, :], axis=0, keepdims=True)
        dw_ref[CONV_WIDTH:, :] = jnp.zeros((CONV_PAD - CONV_WIDTH, C), F32)
        for ci in range(T // CHUNK):
            rows = slice(ci * CHUNK, (ci + 1) * CHUNK)
            _tap_windows(dupad, ci * CHUNK, win)
            _taps(win, w_ref, jnp.zeros((1, C), F32), ubuf, True)
            dz = ubuf[...]
            for half, (a, b) in enumerate(((a0, b0), (a1, b1))):
                sb = jax.nn.sigmoid(b[rows, :].astype(F32))
                dzh = dz[:, half * 256:(half + 1) * 256]
                dglu_ref[rows, half * 256:(half + 1) * 256] = (dzh * sb).astype(BF)
                dglu_ref[rows, C + half * 256:C + (half + 1) * 256] = (
                    dzh * a[rows, :].astype(F32) * sb * (1.0 - sb)).astype(BF)
        if carry is not None:
            carry.finish(ci_refs, co_refs, cs_refs)

    res = pl.pallas_call(
        body, name="conv_bwd", grid=(1,),
        in_specs=[*_GLU_SPECS, pl.BlockSpec((T, C), lambda i: (0, 0)), pl.BlockSpec((T, C), lambda i: (0, 0)), wspec,
                  vec, vec, vec, *[ANY] * len(c_in)],
        out_specs=[pl.BlockSpec((T, 2 * C), lambda i: (0, 0)), wspec, vec, vec, vec, *[ANY] * len(c_out)],
        out_shape=[_sds((T, 2 * C), BF), _sds((CONV_PAD, C), F32), _sds((1, C), F32), _sds((1, C), F32),
                   _sds((1, C), F32), *c_out],
        scratch_shapes=[pltpu.VMEM((PAD_ROWS, C), F32), pltpu.VMEM((PAD_ROWS, C), F32), pltpu.VMEM((8, WIN, C), F32),
                        pltpu.VMEM((CHUNK, C), F32), pltpu.VMEM((8 * CONV_PAD, C), F32), *c_sems],
        compiler_params=_params(("arbitrary",)),
    )(proj, proj, proj, proj, u, d_c, conv_w, conv_b, ln_g, ln_b, *c_in)
    return res[:5], res[5:]


_GATE_BLK = GATE_OFF // 256


def _ffn_in_swiglu(h2, wf_t, carry=None):
    T, D = h2.shape
    tm, tn = 1024, D_FF // 2
    nj, ni = D_FF // tn, T // tm
    c_in, c_out, c_sems = _carry_io(carry)

    def body(*refs):
        a_ref, bg_ref, bu_ref = refs[:3]
        ci_refs = refs[3:3 + len(c_in)]
        act_ref, g_ref, u_ref = refs[3 + len(c_in):6 + len(c_in)]
        co_refs = refs[6 + len(c_in):6 + len(c_in) + len(c_out)]
        cs_refs = refs[6 + len(c_in) + len(c_out):]
        j, i = pl.program_id(0), pl.program_id(1)
        if carry is not None:
            @pl.when((j == 0) & (i == 0))
            def _():
                carry.start(ci_refs, co_refs, cs_refs)
        a = a_ref[...]
        for c0, c1 in ((0, 768), (768, tn)):
            g = lax.dot_general(a, bg_ref[c0:c1, :], _DIMS["NT"], preferred_element_type=F32)
            u = lax.dot_general(a, bu_ref[c0:c1, :], _DIMS["NT"], preferred_element_type=F32)
            act_ref[:, c0:c1] = (g * jax.nn.sigmoid(g) * u).astype(BF)
            g_ref[:, c0:c1] = g.astype(BF)
            u_ref[:, c0:c1] = u.astype(BF)
        if carry is not None:
            @pl.when((j == nj - 1) & (i == ni - 1))
            def _():
                carry.finish(ci_refs, co_refs, cs_refs)

    t = pl.BlockSpec((tm, tn), lambda j, i: (i, j))
    res = pl.pallas_call(
        body, name="ffn_in_swiglu", grid=(nj, ni),
        in_specs=[pl.BlockSpec((tm, D), lambda j, i: (i, 0)), pl.BlockSpec((tn, D), lambda j, i: (j, 0)),
                  pl.BlockSpec((tn, D), lambda j, i: (nj + j, 0)), *[ANY] * len(c_in)],
        out_specs=[t, t, t, *[ANY] * len(c_out)], out_shape=[*[_sds((T, D_FF), BF)] * 3, *c_out],
        scratch_shapes=c_sems,
        compiler_params=_params(("arbitrary", "arbitrary")),
    )(h2, wf_t, wf_t, *c_in)
    return res[:3], res[3:]


def _proj_merge(o, c, wap_t, wcp_t, b_cp, proj):
    T, D = o.shape[0], wap_t.shape[0]
    tm, tg = 1024, 256
    nj = D // tg

    def body(o_ref, c_ref, wa_ref, wc_ref, b_ref, g0_ref, g1_ref, ya_ref, yc_ref, m_ref):
        ya = lax.dot_general(o_ref[...], wa_ref[...], _DIMS["NT"], preferred_element_type=F32)
        yc = lax.dot_general(c_ref[...], wc_ref[...], _DIMS["NT"], preferred_element_type=F32) + b_ref[...]
        ya_ref[...] = ya.astype(BF)
        yc_ref[...] = yc.astype(BF)
        m_ref[...] = (jax.nn.sigmoid(g0_ref[...].astype(F32)) * ya + jax.nn.sigmoid(g1_ref[...].astype(F32)) * yc).astype(BF)

    act = pl.BlockSpec((tm, o.shape[1]), lambda j, i: (i, 0))
    wgt = pl.BlockSpec((tg, o.shape[1]), lambda j, i: (j, 0))
    t = pl.BlockSpec((tm, tg), lambda j, i: (i, j))
    return pl.pallas_call(
        body, name="proj_merge", grid=(nj, T // tm),
        in_specs=[act, act, wgt, wgt, pl.BlockSpec((1, tg), lambda j, i: (0, j)),
                  pl.BlockSpec((tm, tg), lambda j, i: (i, _GATE_BLK + j)),
                  pl.BlockSpec((tm, tg), lambda j, i: (i, _GATE_BLK + nj + j))],
        out_specs=[t, t, t], out_shape=[_sds((T, D), BF)] * 3,
        compiler_params=_params(("arbitrary", "arbitrary")),
    )(o, c, wap_t, wcp_t, b_cp, proj, proj)


def _stacked_dw(name, segs, h, tb):
    T, D = h.shape
    nblk = [seg.shape[1] // tb for seg in segs]
    starts = [sum(nblk[:q]) for q in range(len(segs))]
    n_seg = len(segs)

    def body(*refs):
        seg_refs, h_ref, o_ref, cs_ref = refs[:n_seg], refs[n_seg], refs[n_seg + 1], refs[n_seg + 2]
        i = pl.program_id(0)
        for seg_ref, st, nb in zip(seg_refs, starts, nblk):
            @pl.when((i >= st) & (i < st + nb))
            def _(seg_ref=seg_ref):
                a = seg_ref[...]
                o_ref[...] = lax.dot_general(a, h_ref[...], _DIMS["TN"], preferred_element_type=F32).astype(BF)
                cs_ref[...] = jnp.sum(a.astype(F32), axis=0, keepdims=True)

    in_specs = [pl.BlockSpec((T, tb), functools.partial(lambda i, st, nb: (0, jnp.clip(i - st, 0, nb - 1)), st=st, nb=nb))
                for st, nb in zip(starts, nblk)]
    return pl.pallas_call(
        body, name=name, grid=(sum(nblk),),
        in_specs=[*in_specs, pl.BlockSpec((T, D), lambda i: (0, 0))],
        out_specs=[pl.BlockSpec((tb, D), lambda i: (i, 0)), pl.BlockSpec((1, tb), lambda i: (0, i))],
        out_shape=[_sds((sum(nblk) * tb, D), BF), _sds((1, sum(nblk) * tb), F32)],
        compiler_params=_params(("arbitrary",)),
    )(*segs, h)


def _local_step(x, h, r1, target, small, wi_t, conv_w, plan):
    T, D = x.shape
    tm = 1024

    def carried(call, res, carry):
        if carry is None:
            return res
        outs, got = res
        plan.done(call, got)
        return outs


    def ep_add(acc, ex, outs, ids, scr):
        outs[0][...] = acc + ex[0][...]

    tn_in = IN_WIDTH // 2
    carry = plan.carry("proj_in")
    def ep_bias_bf16(acc, ex, outs, ids, scr):
        outs[0][...] = (acc + ex[0][...]).astype(BF)

    proj, = carried("proj_in", _matmul("proj_in", [h], wi_t, "NT", m=T, n=IN_WIDTH, tm=tm, tn=tn_in,
                                       epilogue=ep_bias_bf16, extra=[(small["b_in"], _row(tn_in))],
                                       outs=[(_sds((T, IN_WIDTH), BF), _tile(tm, tn_in))], carry=carry), carry)
    plan.launch("gather_ffn", after=proj)
    o, got = _attn_fwd(proj, small["sinks"], carry=plan.carry("attn_fwd"))
    plan.done("attn_fwd", got)
    (c, u_conv), got = _conv_fwd(proj, conv_w, small["conv_b"], small["ln_g"], small["ln_b"],
                                 carry=plan.carry("conv_fwd"))
    plan.done("conv_fwd", got)
    wap_t, wcp_t, w_out = plan.weight("w_attn_proj"), plan.weight("w_conv_proj"), plan.weight("w_out")
    ya, yc, merged = _proj_merge(o, c, wap_t, wcp_t, small["b_conv_proj"], proj)

    tg = 256
    gate_specs = [pl.BlockSpec((tm, tg), lambda j, i, k: (i, _GATE_BLK + j)),
                  pl.BlockSpec((tm, tg), lambda j, i, k: (i, _GATE_BLK + D // tg + j))]

    def ep_residual_rms(acc, ex, outs, ids, scr):
        x2v = acc + ex[0][...]
        r = lax.rsqrt(jnp.mean(x2v * x2v, axis=-1, keepdims=True) + EPS)
        outs[0][...] = x2v
        outs[1][...] = (x2v * r * ex[1][...]).astype(BF)
        outs[2][...] = r

    carry = plan.carry("out_proj")
    x2, h2, r2 = carried("out_proj", _matmul(
        "out_proj_rms", [merged], w_out, "NN", m=T, n=D, tm=512, tn=D, epilogue=ep_residual_rms,
        extra=[(x, _tile(512, D)), (small["g_ffn_norm"], _row(D))],
        outs=[(_sds((T, D), F32), _tile(512, D)), (_sds((T, D), BF), _tile(512, D)),
              (_sds((T, 1), F32), pl.BlockSpec((512, 1), lambda j, i, k: (i, 0)))], carry=carry), carry)
    plan.launch("gather_down", after=x2)
    wf_t = plan.weight("w_ffn_in")
    (act, gate, up), got = _ffn_in_swiglu(h2, wf_t, carry=plan.carry("ffn_in_swiglu"))
    plan.done("ffn_in_swiglu", got)
    w_down = plan.weight("w_ffn_down")
    def ep_residual_loss(acc, ex, outs, ids, scr):
        dx, dg, part = _loss_head(acc + ex[0][...], ex[1][...], ex[2][...])
        outs[0][...] = dx
        outs[1][...] = dx.astype(BF)
        _accumulate_rows(outs[2], dg, ids[1] == 0)
        _accumulate_rows(outs[3], part, ids[1] == 0)

    dx3, dx3_b, dg_final, loss = _matmul(
        "ffn_down_loss", [act], w_down, "NN", m=T, n=D, tm=512, tn=D, epilogue=ep_residual_loss,
        extra=[(x2, _tile(512, D)), (small["g_final"], _row(D)), (target, _tile(512, D))],
        outs=[(_sds((T, D), F32), _tile(512, D)), (_sds((T, D), BF), _tile(512, D)), (_sds((1, D), F32), _row(D)),
              (_sds((1, 1), F32), pl.BlockSpec((1, 1), lambda j, i, k: (0, 0)))])

    tn_ff = D_FF // 2

    def ep_swiglu_bwd(acc, ex, outs, ids, scr):
        g, u = ex[0][...].astype(F32), ex[1][...].astype(F32)
        sg = jax.nn.sigmoid(g)
        outs[0][...] = (acc * u * sg * (1.0 + g * (1.0 - sg))).astype(BF)
        outs[1][...] = (acc * g * sg).astype(BF)

    dgate, dup = _matmul(
        "ffn_down_bwd", [dx3_b], w_down, "NT", m=T, n=D_FF, tm=tm, tn=tn_ff, epilogue=ep_swiglu_bwd,
        extra=[(gate, _tile(tm, tn_ff)), (up, _tile(tm, tn_ff))],
        outs=[(_sds((T, D_FF), BF), _tile(tm, tn_ff)), (_sds((T, D_FF), BF), _tile(tm, tn_ff))])

    def dw(name, a, b, rows, cols, row_off=0, alias=None, total_rows=None, colsum=False):
        total_rows = rows if total_rows is None else total_rows
        tmw = rows if rows <= 1024 else D_FF // 2
        blk, rem = divmod(row_off, tmw)
        assert rem == 0

        def ep(acc, ex, outs, ids, scr):
            outs[0][...] = acc.astype(BF)
            if colsum:
                outs[1][...] = jnp.sum(ex[0][...].astype(F32), axis=0, keepdims=True)

        outs = [(_sds((total_rows, cols), BF), pl.BlockSpec((tmw, cols), lambda j, i, k: (blk + i, j)))]
        extra = []
        if colsum:
            extra = [(a, pl.BlockSpec((T, tmw), lambda j, i, k: (0, i)))]
            outs.append((_sds((1, rows), F32), pl.BlockSpec((1, tmw), lambda j, i, k: (0, i))))
        carry = plan.carry(name)
        res = carried(name, _matmul(name, [a], b, "TN", m=rows, n=cols, tm=tmw, tn=cols, epilogue=ep, extra=extra,
                                    outs=outs, alias=None if alias is None else (alias, 0), carry=carry), carry)
        return res if colsum else res[0]

    plan.grad_ready(dict(w_ffn_down=dw("ffn_down_dw", act, dx3_b, D_FF, D)))

    def ep_rms_bwd(acc, ex, outs, ids, scr):
        dx, dg = _rms_bwd(acc, ex[0][...], ex[1][...], ex[2][...])
        dx = ex[3][...] + dx
        outs[0][...] = dx
        outs[1][...] = dx.astype(BF)
        _accumulate_rows(outs[2], dg, ids[1] == 0)

    def rms_bwd_io(tm_, xin, r, g, dres):
        return dict(
            extra=[(xin, _tile(tm_, D)), (r, pl.BlockSpec((tm_, 1), lambda j, i, k: (i, 0))), (g, _row(D)),
                   (dres, _tile(tm_, D))],
            outs=[(_sds((T, D), F32), _tile(tm_, D)), (_sds((T, D), BF), _tile(tm_, D)), (_sds((1, D), F32), _row(D))])

    carry = plan.carry("ffn_in_bwd")
    dx2, dx2_b, dg_ffn = carried(
        "ffn_in_bwd",
        _matmul("ffn_in_bwd", [dgate, dup], wf_t, "NN", m=T, n=D, tm=tm, tn=D, tk=D_FF // 2, epilogue=ep_rms_bwd,
                carry=carry, **rms_bwd_io(tm, x2, r2, small["g_ffn_norm"], dx3)), carry)
    plan.launch("send_down")
    gwf_t, _ = _stacked_dw("ffn_in_dw", [dgate, dup], h2, D_FF // 2)
    plan.grad_ready(dict(w_ffn_in=gwf_t))

    def ep_merge_bwd(acc, ex, outs, ids, scr):
        s0 = jax.nn.sigmoid(ex[2][...].astype(F32))
        s1 = jax.nn.sigmoid(ex[3][...].astype(F32))
        outs[0][...] = (acc * s0).astype(BF)
        outs[1][...] = (acc * s1).astype(BF)
        outs[2][...] = (acc * ex[0][...].astype(F32) * s0 * (1.0 - s0)).astype(BF)
        outs[3][...] = (acc * ex[1][...].astype(F32) * s1 * (1.0 - s1)).astype(BF)

    carry = plan.carry("out_proj_bwd_merge")
    dya, dyc, dg0, dg1 = carried(
        "out_proj_bwd_merge",
        _matmul("out_proj_bwd_merge", [dx2_b], w_out, "NT", m=T, n=D, tm=tm, tn=tg, epilogue=ep_merge_bwd,
                extra=[(ya, _tile(tm, tg)), (yc, _tile(tm, tg)), (proj, gate_specs[0]), (proj, gate_specs[1])],
                outs=[(_sds((T, D), BF), _tile(tm, tg))] * 4, carry=carry), carry)
    plan.launch("send_ffn")
    gw_out = dw("out_proj_dw", merged, dx2_b, D, D)
    d_o, = _matmul("attn_proj_bwd", [dya], wap_t, "NN", m=T, n=ATTN_WIDTH, tm=tm, tn=ATTN_WIDTH,
                   epilogue=_store(BF), outs=[(_sds((T, ATTN_WIDTH), BF), _tile(tm, ATTN_WIDTH))])
    d_c, = _matmul("conv_proj_bwd", [dyc], wcp_t, "NN", m=T, n=CONV_CHANNELS, tm=tm, tn=CONV_CHANNELS,
                   epilogue=_store(BF), outs=[(_sds((T, CONV_CHANNELS), BF), _tile(tm, CONV_CHANNELS))])
    gwap_t = dw("attn_proj_dw", dya, o, D, ATTN_WIDTH)
    gwcp_t, db_cp = dw("conv_proj_dw", dyc, c, D, CONV_CHANNELS, colsum=True)
    plan.grad_ready(dict(w_out=gw_out, w_attn_proj=gwap_t, w_conv_proj=gwcp_t))
    (dglu, dcw, dcb, dlng, dlnb), got = _conv_bwd(proj, u_conv, d_c, conv_w, small["conv_b"], small["ln_g"],
                                                  small["ln_b"], carry=plan.carry("conv_bwd"))
    plan.done("conv_bwd", got)
    plan.launch("send_mix")
    (dqkv, dsinks), got = _attn_bwd(proj, d_o, small["sinks"], carry=plan.carry("attn_bwd"))
    plan.done("attn_bwd", got)

    segs = [dqkv, dglu, dg0, dg1]
    gwi_t, db_in = _stacked_dw("proj_in_dw", segs, h, 256)
    plan.grad_ready(dict(w_in=gwi_t))
    plan.alone("swap_inp")
    plan.launch("send_inp")
    carry = plan.carry("proj_in_bwd")
    dx, _, dg_mix = carried(
        "proj_in_bwd",
        _matmul("proj_in_bwd", segs, wi_t, "NN", m=T, n=D, tm=512, tn=D, epilogue=ep_rms_bwd, carry=carry,
                **rms_bwd_io(512, x, r1, small["g_mix_norm"], plan.behind("inp", dx2))), carry)

    parts = dict(g_mix_norm=dg_mix, b_in=db_in, sinks=dsinks, conv_w=dcw, conv_b=dcb, ln_g=dlng, ln_b=dlnb,
                 b_conv_proj=db_cp, g_ffn_norm=dg_ffn, g_final=dg_final, loss=loss)
    return dx, parts


def _place():
    x, y, c = lax.axis_index("x"), lax.axis_index("y"), lax.axis_index("c")
    return x, y, c, [(1 - x, y), (x, 1 - y), (1 - x, 1 - y)]


def _gather_copies(x_refs, out_refs, rows_per, send_sems, recv_sems, local_sems):
    x, y, c, chips = _place()
    me, sibling = (x, y, c), (x, y, 1 - c)

    def rows(a, px, py, pc):
        return out_refs[a].at[pl.ds((4 * px + 2 * py + pc) * rows_per[a], rows_per[a])]

    def copy(a, k, block, to, src=None):
        return pltpu.make_async_remote_copy(
            src_ref=rows(a, *block) if src is None else src, dst_ref=rows(a, *block),
            send_sem=send_sems.at[7 * a + k], recv_sem=recv_sems.at[7 * a + k], device_id=to, device_id_type=MESH)

    def local(a):
        return pltpu.make_async_copy(x_refs[a], rows(a, *me), local_sems.at[a])

    def first(a):
        return [copy(a, 0, me, sibling, src=x_refs[a])] + [copy(a, 1 + j, me, (*chip, c), src=x_refs[a])
                                                          for j, chip in enumerate(chips)]

    def arrive(a, j):
        return copy(a, 1 + j, (*chips[j], c), me)

    def passed(a, j):
        return copy(a, 4 + j, (*chips[j], c), sibling)

    def from_sibling(a):
        return [copy(a, 0, sibling, me)] + [copy(a, 4 + j, (*chip, 1 - c), me) for j, chip in enumerate(chips)]

    return len(x_refs), local, first, arrive, passed, from_sibling


def _gather_start(*refs):
    n, local, first, _, _, _ = _gather_copies(*refs)
    for a in range(n):
        local(a).start()
        for cp in first(a):
            cp.start()


def _gather_finish(*refs):
    n, local, first, arrive, passed, from_sibling = _gather_copies(*refs)
    for a in range(n):
        for j in range(3):
            arrive(a, j).wait_recv()
            passed(a, j).start()
    for a in range(n):
        for cp in from_sibling(a):
            cp.wait_recv()
    for a in range(n):
        for cp in first(a) + [passed(a, j) for j in range(3)]:
            cp.wait_send()
        local(a).wait()


def _gather_peers():
    x, y, c, chips = _place()
    return [(x, y, 1 - c)] + [(*chip, c) for chip in chips]


def _gather_sems(n):
    return [pltpu.SemaphoreType.DMA((7 * n,)), pltpu.SemaphoreType.DMA((7 * n,)), pltpu.SemaphoreType.DMA((n,))]


def _gather_carry(shards):
    rows_per = [s.shape[0] for s in shards]
    return _Carry(shards, [_sds((N_DEV * s.shape[0],) + s.shape[1:], s.dtype) for s in shards],
                  _gather_sems(len(shards)),
                  lambda ins, outs, sems: _gather_start(ins, outs, rows_per, *sems),
                  lambda ins, outs, sems: _gather_finish(ins, outs, rows_per, *sems), _gather_peers)


def _first_gather(shards, x, g):
    n = len(shards)
    rows_per = [s.shape[0] for s in shards]
    T, D = x.shape

    def body(*refs):
        x_refs, (xin_ref, g_ref), out_refs, (h_ref, r_ref) = refs[:n], refs[n:n + 2], refs[n + 2:2 * n + 2], refs[2 * n + 2:2 * n + 4]
        send_sems, recv_sems, local_sems = refs[2 * n + 4:]
        x, y, c, chips = _place()
        me, sibling = (x, y, c), (x, y, 1 - c)
        near_x, near_y, far = (*chips[0], c), (*chips[1], c), (*chips[2], c)

        def rows(a, dev, part):
            h = rows_per[a] // 2
            lo, size = {"all": (0, 2 * h), "low": (0, h), "high": (h, h)}[part]
            return out_refs[a].at[pl.ds((4 * dev[0] + 2 * dev[1] + dev[2]) * rows_per[a] + lo, size)]

        def copy(a, k, block, part, to, src=None):
            return pltpu.make_async_remote_copy(
                src_ref=rows(a, block, part) if src is None else src, dst_ref=rows(a, block, part),
                send_sem=send_sems.at[9 * a + k], recv_sem=recv_sems.at[9 * a + k], device_id=to, device_id_type=MESH)

        other = lambda dev: (dev[0], dev[1], 1 - c)
        sent = []
        for a in range(n):
            pltpu.make_async_copy(x_refs[a], rows(a, me, "all"), local_sems.at[a]).start()
            sent += [copy(a, 0, me, "all", sibling, src=x_refs[a]), copy(a, 1, me, "all", near_x, src=x_refs[a]),
                     copy(a, 2, me, "all", near_y, src=x_refs[a])]
        for cp in sent:
            cp.start()
        for i in range(T // CHUNK):
            rws = slice(i * CHUNK, (i + 1) * CHUNK)
            xv = xin_ref[rws, :]
            r = lax.rsqrt(jnp.mean(xv * xv, axis=-1, keepdims=True) + EPS)
            h_ref[rws, :] = (xv * r * g_ref[...]).astype(BF)
            r_ref[rws, :] = r
        for a in range(n):
            copy(a, 1, near_x, "all", me).wait_recv()
            copy(a, 2, near_y, "all", me).wait_recv()
            passed = [copy(a, 3, near_y, "high", near_x), copy(a, 4, near_x, "low", near_y),
                      copy(a, 5, near_x, "all", sibling), copy(a, 6, near_y, "all", sibling)]
            for cp in passed:
                cp.start()
            sent += passed
        for a in range(n):
            copy(a, 3, far, "high", me).wait_recv()
            copy(a, 4, far, "low", me).wait_recv()
            passed = [copy(a, 7, far, "high", sibling), copy(a, 8, far, "low", sibling)]
            for cp in passed:
                cp.start()
            sent += passed
        for a in range(n):
            copy(a, 0, sibling, "all", me).wait_recv()
            copy(a, 5, other(near_x), "all", me).wait_recv()
            copy(a, 6, other(near_y), "all", me).wait_recv()
            copy(a, 7, other(far), "high", me).wait_recv()
            copy(a, 8, other(far), "low", me).wait_recv()
        for cp in sent:
            cp.wait_send()
        for a in range(n):
            pltpu.make_async_copy(x_refs[a], rows(a, me, "all"), local_sems.at[a]).wait()

    vm = pl.BlockSpec(memory_space=pltpu.VMEM)
    return pl.pallas_call(
        body, name="weights_first_gather", in_specs=[*[ANY] * n, vm, vm], out_specs=[*[ANY] * n, vm, vm],
        out_shape=[*[_sds((N_DEV * s.shape[0],) + s.shape[1:], s.dtype) for s in shards], _sds((T, D), BF),
                   _sds((T, 1), F32)],
        scratch_shapes=[pltpu.SemaphoreType.DMA((9 * n,)), pltpu.SemaphoreType.DMA((9 * n,)),
                        pltpu.SemaphoreType.DMA((n,))],
        compiler_params=pltpu.CompilerParams(vmem_limit_bytes=VMEM_LIMIT_BYTES),
    )(*shards, x, g)


def _swap_carry(grads):
    n = len(grads)

    def copies(g_refs, out_refs, sems):
        send_sems, recv_sems = sems
        x, y, c, _ = _place()
        return [pltpu.make_async_remote_copy(
            src_ref=g_refs[a].at[2 * p + 1 - c], dst_ref=out_refs[a].at[p],
            send_sem=send_sems.at[4 * a + p], recv_sem=recv_sems.at[4 * a + p],
            device_id=(x, y, 1 - c), device_id_type=MESH) for a in range(n) for p in range(4)]

    def start(ins, outs, sems):
        for cp in copies(ins, outs, sems):
            cp.start()

    def finish(ins, outs, sems):
        for cp in copies(ins, outs, sems):
            cp.wait()

    def peers():
        x, y, c, _ = _place()
        return [(x, y, 1 - c)]

    return _Carry(grads, [_sds((4,) + g.shape[1:], g.dtype) for g in grads],
                  [pltpu.SemaphoreType.DMA((4 * n,)), pltpu.SemaphoreType.DMA((4 * n,))], start, finish, peers)


def _join(carries):
    carries = [c for c in carries if c is not None]
    if not carries:
        return None
    n_in = [len(c.arrays) for c in carries]
    n_out = [len(c.out_shapes) for c in carries]
    n_sem = [len(c.sems) for c in carries]

    def parts(refs, counts):
        cuts = [sum(counts[:q]) for q in range(len(counts) + 1)]
        return [refs[cuts[q]:cuts[q + 1]] for q in range(len(counts))]

    def start(ins, outs, sems):
        for c, i, o, s in zip(carries, parts(ins, n_in), parts(outs, n_out), parts(sems, n_sem)):
            c.start(i, o, s)

    def finish(ins, outs, sems):
        for c, i, o, s in zip(carries, parts(ins, n_in), parts(outs, n_out), parts(sems, n_sem)):
            c.finish(i, o, s)

    return _Carry([a for c in carries for a in c.arrays], [o for c in carries for o in c.out_shapes],
                  [s for c in carries for s in c.sems], start, finish)


def _run_carry(name, carry):
    n_in, n_out = len(carry.arrays), len(carry.out_shapes)

    def body(*refs):
        carry.start(refs[:n_in], refs[n_in:n_in + n_out], refs[n_in + n_out:])
        carry.finish(refs[:n_in], refs[n_in:n_in + n_out], refs[n_in + n_out:])

    return pl.pallas_call(body, name=name, in_specs=[ANY] * n_in, out_specs=[ANY] * n_out,
                          out_shape=carry.out_shapes, scratch_shapes=carry.sems)(*carry.arrays)


def _run_carry_async(name, carry, collective_id):
    ins = [jax.new_ref(a, memory_space=pltpu.MemorySpace.HBM) for a in carry.arrays]
    outs = [jax.empty_ref(o, memory_space=pltpu.MemorySpace.HBM) for o in carry.out_shapes]

    @pl.kernel(mesh=plsc.ScalarSubcoreMesh(axis_name="sequencer", num_cores=1), name=name,
               scratch_types=tuple(carry.sems), compiler_params=pltpu.CompilerParams(collective_id=collective_id))
    def launch(*sems):
        barrier = pltpu.get_barrier_semaphore()
        peers = carry.peers()
        for peer in peers:
            pl.semaphore_signal(barrier, inc=1, device_id=peer, device_id_type=MESH)
        pl.semaphore_wait(barrier, len(peers))
        carry.start(ins, outs, sems)
        carry.finish(ins, outs, sems)

    launch()
    return [o[...] for o in outs]


def _chip_sums(name, gs, gots, c):
    n = len(gs)

    def body(c_ref, *refs):
        for g_ref, got_ref, o_ref in zip(refs[:n], refs[n:2 * n], refs[2 * n:]):
            o_ref[...] = (g_ref[...].astype(F32) + got_ref[...].astype(F32)).astype(BF)

    mine = [pl.BlockSpec((1,) + g.shape[1:], lambda p, c_ref: (2 * p + c_ref[0], 0, 0)) for g in gs]
    slot = [pl.BlockSpec((1,) + g.shape[1:], lambda p, c_ref: (p, 0, 0)) for g in gs]
    return pl.pallas_call(
        body, name=name,
        grid_spec=pltpu.PrefetchScalarGridSpec(num_scalar_prefetch=1, grid=(4,), in_specs=[*mine, *slot],
                                               out_specs=slot),
        out_shape=[_sds((4,) + g.shape[1:], BF) for g in gs],
        compiler_params=_params(("arbitrary",)),
    )(c, *gs, *gots)


def _send_carry(sums, ks):
    n, nk = len(sums), len(ks)

    def copies(s_refs, out_refs, sems):
        send_sems, recv_sems = sems
        x, y, c, chips = _place()
        return [pltpu.make_async_remote_copy(
            src_ref=s_refs[a].at[2 * chips[k][0] + chips[k][1]], dst_ref=out_refs[a].at[q],
            send_sem=send_sems.at[nk * a + q], recv_sem=recv_sems.at[nk * a + q],
            device_id=(*chips[k], c), device_id_type=MESH) for a in range(n) for q, k in enumerate(ks)]

    def start(ins, outs, sems):
        for cp in copies(ins, outs, sems):
            cp.start()

    def finish(ins, outs, sems):
        for cp in copies(ins, outs, sems):
            cp.wait()

    def peers():
        x, y, c, chips = _place()
        return [(*chips[k], c) for k in ks]

    return _Carry(sums, [_sds((nk,) + s.shape[1:], s.dtype) for s in sums],
                  [pltpu.SemaphoreType.DMA((nk * n,)), pltpu.SemaphoreType.DMA((nk * n,))], start, finish, peers)


def _adam_math(w, g, m, v):
    m = ADAM_B1 * m + (1.0 - ADAM_B1) * g
    v = ADAM_B2 * v + (1.0 - ADAM_B2) * (g * g)
    m_hat = m / (1.0 - ADAM_B1 ** ADAM_STEP)
    v_hat = v / (1.0 - ADAM_B2 ** ADAM_STEP)
    delta = -ADAM_LR * (m_hat / (jnp.sqrt(v_hat) + ADAM_EPS) + ADAM_WD * w)
    return delta, m, v


def _adamw(name, w, g, m, v):
    rows, cols = w.shape
    tr = 256 if rows % 256 == 0 else rows

    def body(w_ref, g_ref, m_ref, v_ref, d_ref, nm_ref, nv_ref):
        d_ref[...], nm_ref[...], nv_ref[...] = _adam_math(w_ref[...], g_ref[...], m_ref[...], v_ref[...])

    t = pl.BlockSpec((tr, cols), lambda i: (i, 0))
    return pl.pallas_call(
        body, name=name, grid=(rows // tr,), in_specs=[t] * 4, out_specs=[t] * 3,
        out_shape=[_sds((rows, cols), F32)] * 3, compiler_params=_params(("arbitrary",)),
    )(w, g, m, v)


def _grad_adamw(name, g, got, got3, ids, w, m, v):
    _, rows, cols = g.shape
    n3 = len(got3)
    tr = rows // 2 if rows >= 256 else rows

    def body(ids_ref, g_ref, got_ref, *rest):
        w_ref, m_ref, v_ref, o_ref, d_ref, nm_ref, nv_ref = rest[n3:]
        tot = g_ref[0].astype(F32) + got_ref[0].astype(F32)
        for r_ref in rest[:n3]:
            for q in range(r_ref.shape[0]):
                tot = tot + r_ref[q].astype(F32)
        o_ref[...] = tot
        d_ref[...], nm_ref[...], nv_ref[...] = _adam_math(w_ref[...], tot, m_ref[...], v_ref[...])

    tile = pl.BlockSpec((tr, cols), lambda i, ids_ref: (i, 0))
    return pl.pallas_call(
        body, name=name,
        grid_spec=pltpu.PrefetchScalarGridSpec(
            num_scalar_prefetch=1, grid=(rows // tr,),
            in_specs=[pl.BlockSpec((1, tr, cols), lambda i, ids_ref: (ids_ref[0], i, 0)),
                      pl.BlockSpec((1, tr, cols), lambda i, ids_ref: (ids_ref[1], i, 0)),
                      *[pl.BlockSpec((r.shape[0], tr, cols), lambda i, ids_ref: (0, i, 0)) for r in got3],
                      tile, tile, tile],
            out_specs=[tile] * 4),
        out_shape=[_sds((rows, cols), F32)] * 4,
        compiler_params=_params(("arbitrary",)),
    )(ids, g, got, *got3, w, m, v)


SMALL_NAMES = ["g_mix_norm", "b_in", "sinks", "conv_b", "ln_g", "ln_b", "b_conv_proj", "g_ffn_norm", "g_final"]
_PACK_ROWS = 32


def _small_pack(parts):
    C = CONV_CHANNELS
    part_list = [parts["g_mix_norm"], parts["b_in"], parts["sinks"], parts["conv_b"], parts["ln_g"], parts["ln_b"],
                 parts["b_conv_proj"], parts["g_ffn_norm"], parts["g_final"], parts["loss"], parts["conv_w"]]

    def body(p_mix, p_b, p_sink, p_cb, p_lg, p_lb, p_bcp, p_ffn, p_fin, p_loss, p_cw, pack):
        pack[...] = jnp.zeros_like(pack)
        pack[0:1, :] = p_mix[...]
        pack[1:2, 0:GLU_OFF] = p_b[:, 0:GLU_OFF]
        pack[2:3, :] = p_b[:, GLU_OFF:GATE_OFF]
        pack[3:4, :] = p_b[:, GATE_OFF:GATE_OFF + D_MODEL]
        pack[4:5, :] = p_b[:, GATE_OFF + D_MODEL:]
        pack[5:6, 0:128] = p_sink[...]
        pack[6:7, 0:C] = p_cb[...]
        pack[6:7, C:2 * C] = p_lg[...]
        pack[7:8, 0:C] = p_lb[...]
        pack[8:9, :] = p_bcp[...]
        pack[9:10, :] = p_ffn[...]
        pack[10:11, :] = p_fin[...]
        pack[11:12, 0:128] = jnp.broadcast_to(p_loss[...], (1, 128))
        pack[12:28, 0:C] = p_cw[0:16, :]
        pack[12:28, C:2 * C] = p_cw[16:32, :]

    vm = pl.BlockSpec(memory_space=pltpu.VMEM)
    return pl.pallas_call(body, name="small_pack", in_specs=[vm] * len(part_list), out_specs=vm,
                          out_shape=_sds((_PACK_ROWS, D_MODEL), F32))(*part_list)


def _small_adamw(gathered, small_w, small_m, small_v):
    C = CONV_CHANNELS
    names = SMALL_NAMES
    widths = [small_w[k].shape[1] for k in names]
    n_small = len(names)

    def body(*refs):
        tot_ref = refs[0]
        w_refs = refs[1:1 + n_small]
        m_refs = refs[1 + n_small:1 + 2 * n_small]
        v_refs = refs[1 + 2 * n_small:1 + 3 * n_small]
        o = 1 + 3 * n_small
        loss_ref, cw_ref = refs[o], refs[o + 1]
        out_refs = refs[o + 2:o + 2 + 4 * n_small]
        tot = tot_ref[0:_PACK_ROWS, :]
        for d in range(1, N_DEV):
            tot = tot + tot_ref[d * _PACK_ROWS:(d + 1) * _PACK_ROWS, :]
        loss_ref[...] = tot[11:12, 0:1]
        cw_ref[0:16, :] = tot[12:28, 0:C]
        cw_ref[16:32, :] = tot[12:28, C:2 * C]
        grads = dict(
            g_mix_norm=tot[0:1, :],
            b_in=jnp.concatenate([tot[1:2, 0:GLU_OFF], tot[2:3, :], tot[3:4, :], tot[4:5, :]], axis=1),
            sinks=tot[5:6, 0:N_Q_HEADS], conv_b=tot[6:7, 0:C], ln_g=tot[6:7, C:2 * C], ln_b=tot[7:8, 0:C],
            b_conv_proj=tot[8:9, :], g_ffn_norm=tot[9:10, :], g_final=tot[10:11, :])
        for s, k in enumerate(names):
            g = grads[k]
            d, nm, nv = _adam_math(w_refs[s][...], g, m_refs[s][...], v_refs[s][...])
            out_refs[4 * s][...] = g
            out_refs[4 * s + 1][...] = d
            out_refs[4 * s + 2][...] = nm
            out_refs[4 * s + 3][...] = nv

    vm = pl.BlockSpec(memory_space=pltpu.VMEM)
    args = [gathered, *[small_w[k] for k in names], *[small_m[k] for k in names], *[small_v[k] for k in names]]
    out_shape = [_sds((1, 1), F32), _sds((CONV_PAD, C), F32)]
    for wd in widths:
        out_shape += [_sds((1, wd), F32)] * 4
    res = pl.pallas_call(
        body, name="small_adamw",
        in_specs=[vm] * len(args), out_specs=[vm] * len(out_shape), out_shape=out_shape,
        compiler_params=pltpu.CompilerParams(vmem_limit_bytes=VMEM_LIMIT_BYTES),
    )(*args)
    return res[0], res[1], {k: res[2 + 4 * s:6 + 4 * s] for s, k in enumerate(names)}


BIG = dict(w_in=True, w_attn_proj=True, w_conv_proj=True, w_out=False, w_ffn_in=True, w_ffn_down=False)
WEIGHT_NAMES = ["g_mix_norm", "w_in", "b_in", "sinks", "conv_w", "conv_b", "ln_g", "ln_b", "w_attn_proj",
                "w_conv_proj", "b_conv_proj", "w_out", "g_ffn_norm", "w_ffn_in", "w_ffn_down", "g_final"]


class _Plan:
    GROUPS = dict(down=["w_ffn_down"], ffn=["w_ffn_in"], mix=["w_out", "w_attn_proj", "w_conv_proj"], inp=["w_in"])
    ALL = (0, 1, 2)
    RIDES = dict(
        gather_mix=[("gather", ["w_attn_proj", "w_conv_proj", "w_out"])], gather_ffn=[("gather", ["w_ffn_in"])],
        gather_down=[("gather", ["w_ffn_down"])],
        ffn_in_bwd=[("swap", "down")], send_down=[("send", "down", ALL)],
        out_proj_bwd_merge=[("swap", "ffn")], send_ffn=[("send", "ffn", ALL)],
        conv_bwd=[("swap", "mix")], send_mix=[("send", "mix", ALL)],
        swap_inp=[("swap", "inp")], send_inp=[("send", "inp", ALL)])
    ASYNC = dict(gather_mix=1, gather_ffn=2, gather_down=3, send_down=4, send_ffn=5, send_mix=6, send_inp=7)

    def __init__(self, shards, c1):
        self.shards, self.c1 = shards, c1
        self.full, self.slots, self.got, self.sums, self.got3 = {}, {}, {}, {}, {}

    def weight(self, name):
        return self.full[name]

    def grad_ready(self, grads):
        for k, g in grads.items():
            self.slots[k] = g.reshape(N_DEV, g.shape[0] // N_DEV, g.shape[1])

    def _one(self, kind, what, ks=None):
        if kind == "gather":
            return _gather_carry([self.shards[k] for k in what])
        names = self.GROUPS[what]
        if kind == "swap":
            return _swap_carry([self.slots[k] for k in names])
        return _send_carry([self.sums[k] for k in names], ks)

    def carry(self, call):
        return _join([self._one(*ride) for ride in self.RIDES.get(call, [])])

    def done(self, call, outs):
        outs = list(outs)
        for kind, what, *_ in self.RIDES.get(call, []):
            names = what if kind == "gather" else self.GROUPS[what]
            mine, outs = outs[:len(names)], outs[len(names):]
            if kind == "gather":
                self.full.update(zip(names, mine))
            elif kind == "send":
                for k, r in zip(names, mine):
                    self.got3.setdefault(k, []).append(r)
            else:
                self.got.update(zip(names, mine))
                self.sums.update(zip(names, _chip_sums(f"chip_sums_{what}", [self.slots[k] for k in names], mine, self.c1)))

    def alone(self, call):
        self.done(call, _run_carry(call, self.carry(call)))

    def behind(self, group, x):
        return lax.optimization_barrier((x, tuple(self.sums[k] for k in self.GROUPS[group])))[0]

    def launch(self, call, after=None):
        carry = self._one(*self.RIDES[call][0])
        if after is not None:
            carry.arrays = list(lax.optimization_barrier((tuple(carry.arrays), after))[0])
        self.done(call, _run_carry_async(call, carry, self.ASYNC[call]))


def kernel(x, g_mix_norm, w_in, b_in, sinks, conv_w, conv_b, ln_g, ln_b, w_attn_proj, w_conv_proj, b_conv_proj, w_out, g_ffn_norm, w_ffn_in, w_ffn_down, g_final, loss_target, m_g_mix_norm, m_w_in, m_b_in, m_sinks, m_conv_w, m_conv_b, m_ln_g, m_ln_b, m_w_attn_proj, m_w_conv_proj, m_b_conv_proj, m_w_out, m_g_ffn_norm, m_w_ffn_in, m_w_ffn_down, m_g_final, v_g_mix_norm, v_w_in, v_b_in, v_sinks, v_conv_w, v_conv_b, v_ln_g, v_ln_b, v_w_attn_proj, v_w_conv_proj, v_b_conv_proj, v_w_out, v_g_ffn_norm, v_w_ffn_in, v_w_ffn_down, v_g_final):
    w = dict(g_mix_norm=g_mix_norm, w_in=w_in, b_in=b_in, sinks=sinks, conv_w=conv_w, conv_b=conv_b, ln_g=ln_g,
             ln_b=ln_b, w_attn_proj=w_attn_proj, w_conv_proj=w_conv_proj, b_conv_proj=b_conv_proj, w_out=w_out,
             g_ffn_norm=g_ffn_norm, w_ffn_in=w_ffn_in, w_ffn_down=w_ffn_down, g_final=g_final)
    m = dict(g_mix_norm=m_g_mix_norm, w_in=m_w_in, b_in=m_b_in, sinks=m_sinks, conv_w=m_conv_w, conv_b=m_conv_b,
             ln_g=m_ln_g, ln_b=m_ln_b, w_attn_proj=m_w_attn_proj, w_conv_proj=m_w_conv_proj,
             b_conv_proj=m_b_conv_proj, w_out=m_w_out, g_ffn_norm=m_g_ffn_norm, w_ffn_in=m_w_ffn_in,
             w_ffn_down=m_w_ffn_down, g_final=m_g_final)
    v = dict(g_mix_norm=v_g_mix_norm, w_in=v_w_in, b_in=v_b_in, sinks=v_sinks, conv_w=v_conv_w, conv_b=v_conv_b,
             ln_g=v_ln_g, ln_b=v_ln_b, w_attn_proj=v_w_attn_proj, w_conv_proj=v_w_conv_proj,
             b_conv_proj=v_b_conv_proj, w_out=v_w_out, g_ffn_norm=v_g_ffn_norm, w_ffn_in=v_w_ffn_in,
             w_ffn_down=v_w_ffn_down, g_final=v_g_final)
    ax, ay, ac = lax.axis_index("x"), lax.axis_index("y"), lax.axis_index("c")
    me = 4 * ax + 2 * ay + ac
    chip = 2 * ax + ay

    shards = {k: (w[k][0].T if tr else w[k][0]).astype(BF) for k, tr in BIG.items()}
    cw_shard = jnp.pad(conv_w[0].T, ((0, 0), (0, 1))).reshape(16, 128)
    wi_t, cw_full, h, r1 = _first_gather([shards["w_in"], cw_shard], x[0], g_mix_norm)
    conv_full = cw_full.reshape(CONV_CHANNELS, CONV_PAD).T

    as_row = lambda a: a.reshape(1, -1)
    small_w = {k: as_row(w[k]) for k in SMALL_NAMES}
    small_m = {k: as_row(m[k]) for k in SMALL_NAMES}
    small_v = {k: as_row(v[k]) for k in SMALL_NAMES}
    plan = _Plan(shards, ac.reshape(1).astype(jnp.int32))
    plan.launch("gather_mix", after=wi_t)
    dx, parts = _local_step(x[0], h, r1, loss_target[0], small_w, wi_t, conv_full, plan)

    ids = jnp.stack([me, chip]).astype(jnp.int32)
    grads, delta, new_m, new_v, after = {}, {}, {}, {}, dx
    packed = _small_pack(parts)
    for k in sorted(BIG, key=lambda k: k == "w_in"):
        if k == "w_in":
            packed = lax.optimization_barrier((packed, after))[0]
            small_gathered, = _run_carry_async("small_gather", _gather_carry([packed]), 8)
        flip = (lambda a: a.T) if BIG[k] else (lambda a: a)
        wk = lax.optimization_barrier((w[k][0], after))[0]
        outs = _grad_adamw(f"grad_adamw_{k}", plan.slots[k], plan.got[k], plan.got3[k], ids,
                           flip(wk), flip(m[k][0]), flip(v[k][0]))
        after = outs[0]
        grads[k], delta[k], new_m[k], new_v[k] = (flip(a)[None] for a in outs)

    loss, cw_grad, small_out = _small_adamw(small_gathered, small_w, small_m, small_v)
    for k in SMALL_NAMES:
        g, d, nm, nv = (a.reshape(w[k].shape) for a in small_out[k])
        grads[k], delta[k], new_m[k], new_v[k] = g, d, nm, nv
    cw_mine = lax.dynamic_slice(cw_grad, (0, me * 64), (CONV_WIDTH, 64))
    d, nm, nv = _adamw("adamw_conv_w", conv_w[0], cw_mine, m_conv_w[0], v_conv_w[0])
    grads["conv_w"], delta["conv_w"], new_m["conv_w"], new_v["conv_w"] = cw_mine[None], d[None], nm[None], nv[None]

    return (loss.reshape(()), dx[None], *[grads[k] for k in WEIGHT_NAMES], *[delta[k] for k in WEIGHT_NAMES],
            *[new_m[k] for k in WEIGHT_NAMES], *[new_v[k] for k in WEIGHT_NAMES])
```

```python
import functools

import jax
import jax.numpy as jnp
from jax import lax
from jax.experimental import pallas as pl
from jax.experimental.pallas import tpu as pltpu
from jax.experimental.pallas import tpu_sc as plsc

F32 = jnp.float32
BF = jnp.bfloat16

SEQ = 2048
D_MODEL = 1024
HEAD_DIM = 64
N_Q_HEADS = 8
N_KV_HEADS = 2
GROUP = N_Q_HEADS // N_KV_HEADS
BLOCK = 128
ATTN_WIDTH = 512
KV_WIDTH = 128
CONV_CHANNELS = 512
CONV_WIDTH = 31
CONV_PAD = 32
GLU_OFF = 768
GATE_OFF = 1792
IN_WIDTH = 3840
D_FF = 2816
EPS = 1e-5
NEG = -1e30
N_DEV = 8

ADAM_LR = 0.001
ADAM_B1 = 0.9
ADAM_B2 = 0.999
ADAM_EPS = 1e-08
ADAM_WD = 0.01
ADAM_STEP = 10

VMEM_LIMIT_BYTES = 56 * 1024 * 1024
MESH = pl.DeviceIdType.MESH
ANY = pl.BlockSpec(memory_space=pl.ANY)

_DIMS = {"NN": (((1,), (0,)), ((), ())), "NT": (((1,), (1,)), ((), ())), "TN": (((0,), (0,)), ((), ()))}


def _params(sem):
    return pltpu.CompilerParams(dimension_semantics=sem, vmem_limit_bytes=VMEM_LIMIT_BYTES)


class _Carry:
    def __init__(self, arrays, out_shapes, sems, start, finish, peers=None):
        self.arrays, self.out_shapes, self.sems, self.start, self.finish = arrays, out_shapes, sems, start, finish
        self.peers = peers


def _carry_io(carry):
    if carry is None:
        return [], [], []
    return list(carry.arrays), list(carry.out_shapes), list(carry.sems)


def _matmul(name, a_list, b, mode, *, m, n, tm, tn, tk=None, epilogue, extra=(), outs, b_off=(0, 0), alias=None,
            scratch=(), carry=None):
    seg_k = [a.shape[0] if mode == "TN" else a.shape[1] for a in a_list]
    whole = tk is None
    seg_nk = [1] * len(a_list) if whole else [ks // tk for ks in seg_k]
    nk = 1 if whole else sum(seg_nk)
    starts = [sum(seg_nk[:s]) for s in range(len(seg_nk))]
    k_starts = [sum(seg_k[:s]) for s in range(len(seg_k))]
    k_tot = sum(seg_k)
    n_a, n_extra, n_out = len(a_list), len(extra), len(outs)

    a_specs = []
    for st, ns, ks in zip(starts, seg_nk, seg_k):
        if mode == "TN":
            a_specs.append(pl.BlockSpec((ks if whole else tk, tm), lambda j, i, k: (k, i)))
        elif whole:
            a_specs.append(pl.BlockSpec((tm, ks), lambda j, i, k: (i, 0)))
        else:
            a_specs.append(pl.BlockSpec((tm, tk), functools.partial(
                lambda j, i, k, st, ns: (i, jnp.clip(k - st, 0, ns - 1)), st=st, ns=ns)))
    bk = k_tot if whole else tk
    if mode == "NT":
        b_spec = pl.BlockSpec((tn, bk), lambda j, i, k: (b_off[0] + j, b_off[1] + k))
    else:
        b_spec = pl.BlockSpec((bk, tn), lambda j, i, k: (b_off[0] + k, b_off[1] + j))
    n_alias = 0 if alias is None else 1
    c_in, c_out, c_sems = _carry_io(carry)
    n_acc = 0 if whole else 1
    nj, ni = n // tn, m // tm

    def body(*refs):
        pos = [n_a, 1, n_alias, n_extra, len(c_in), n_out, len(c_out), n_acc, len(scratch), len(c_sems)]
        cuts = [sum(pos[:q]) for q in range(len(pos) + 1)]
        a_refs, (b_ref,), _, ex, ci_refs, out_refs, co_refs, acc_refs, scr, cs_refs = (
            refs[cuts[q]:cuts[q + 1]] for q in range(len(pos)))
        j, i, k = pl.program_id(0), pl.program_id(1), pl.program_id(2)
        ids = (j, i)
        if carry is not None:
            @pl.when((j == 0) & (i == 0) & (k == 0))
            def _():
                carry.start(ci_refs, co_refs, cs_refs)

        def dot(a_ref, bv):
            return lax.dot_general(a_ref[...].astype(BF), bv.astype(BF), _DIMS[mode], preferred_element_type=F32)

        if whole:
            tot = None
            for a_ref, k0, ks in zip(a_refs, k_starts, seg_k):
                if n_a == 1:
                    bv = b_ref[...]
                else:
                    bv = b_ref[:, k0:k0 + ks] if mode == "NT" else b_ref[k0:k0 + ks, :]
                part = dot(a_ref, bv)
                tot = part if tot is None else tot + part
            epilogue(tot, ex, out_refs, ids, scr)
        else:
            acc, = acc_refs

            @pl.when(k == 0)
            def _():
                acc[...] = jnp.zeros_like(acc)

            for a_ref, st, ns in zip(a_refs, starts, seg_nk):
                if n_a == 1:
                    acc[...] += dot(a_ref, b_ref[...])
                else:
                    @pl.when((k >= st) & (k < st + ns))
                    def _(a_ref=a_ref):
                        acc[...] += dot(a_ref, b_ref[...])

            @pl.when(k == nk - 1)
            def _():
                epilogue(acc[...], ex, out_refs, ids, scr)

        if carry is not None:
            @pl.when((j == nj - 1) & (i == ni - 1) & (k == nk - 1))
            def _():
                carry.finish(ci_refs, co_refs, cs_refs)

    in_specs = [*a_specs, b_spec]
    args = [*a_list, b]
    io_alias = {}
    if alias is not None:
        in_specs.append(pl.BlockSpec(memory_space=pl.ANY))
        args.append(alias[0])
        io_alias = {n_a + 1: alias[1]}
    in_specs += [s for _, s in extra] + [pl.BlockSpec(memory_space=pl.ANY)] * len(c_in)
    args += [x for x, _ in extra] + c_in
    res = pl.pallas_call(
        body, name=name, grid=(nj, ni, nk), in_specs=in_specs,
        out_specs=[s for _, s in outs] + [pl.BlockSpec(memory_space=pl.ANY)] * len(c_out),
        out_shape=[o for o, _ in outs] + c_out,
        scratch_shapes=[*([] if whole else [pltpu.VMEM((tm, tn), F32)]), *scratch, *c_sems],
        input_output_aliases=io_alias,
        compiler_params=_params(("arbitrary", "arbitrary", "arbitrary")),
    )(*args)
    return res if carry is None else (res[:n_out], res[n_out:])


def _tile(tm, tn):
    return pl.BlockSpec((tm, tn), lambda j, i, k: (i, j))


def _row(tn):
    return pl.BlockSpec((1, tn), lambda j, i, k: (0, j))


def _store(dtype):
    def ep(acc, ex, outs, ids, scr):
        outs[0][...] = acc.astype(dtype)
    return ep


def _sds(shape, dtype):
    return jax.ShapeDtypeStruct(shape, dtype)


def _rms_bwd(dh, xv, r, g):
    xh = xv * r
    dxh = dh * g
    dx = r * (dxh - xh * jnp.mean(dxh * xh, axis=-1, keepdims=True))
    return dx, jnp.sum(dh * xh, axis=0, keepdims=True)


def _accumulate_rows(ref, val, first):
    @pl.when(first)
    def _():
        ref[...] = val

    @pl.when(jnp.logical_not(first))
    def _():
        ref[...] += val


def _loss_head(xv, g, target):
    r = lax.rsqrt(jnp.mean(xv * xv, axis=-1, keepdims=True) + EPS)
    err = xv * r * g - target
    dx, dg = _rms_bwd(err * (1.0 / xv.shape[-1]), xv, r, g)
    part = 0.5 * jnp.sum(jnp.mean(err * err, axis=-1, keepdims=True), axis=0, keepdims=True)
    return dx, dg, part


def _lane_half(shape, h):
    lane = lax.broadcasted_iota(jnp.int32, shape, 1)
    return (lane >= HEAD_DIM * h) & (lane < HEAD_DIM * (h + 1))


def _to_half(v, w, h):
    if w != h:
        v = pltpu.roll(v, HEAD_DIM, 1)
    return jnp.where(_lane_half(v.shape, h), v, 0.0)


def _attn_block(qkv_ref, sinks_ref, n, h):
    r0 = pl.multiple_of(n * BLOCK, BLOCK)
    p0 = pl.multiple_of(jnp.maximum(n - 1, 0) * BLOCK, BLOCK)
    rows = pl.ds(r0, BLOCK)
    prev = pl.ds(p0, BLOCK)
    k2 = jnp.concatenate([qkv_ref[prev, ATTN_WIDTH:ATTN_WIDTH + KV_WIDTH],
                          qkv_ref[rows, ATTN_WIDTH:ATTN_WIDTH + KV_WIDTH]], axis=0)
    v2 = jnp.concatenate([qkv_ref[prev, ATTN_WIDTH + KV_WIDTH:ATTN_WIDTH + 2 * KV_WIDTH],
                          qkv_ref[rows, ATTN_WIDTH + KV_WIDTH:ATTN_WIDTH + 2 * KV_WIDTH]], axis=0)
    qs = []
    for g in range(GROUP):
        hq = GROUP * h + g
        blk = qkv_ref[rows, (hq // 2) * 128:(hq // 2 + 1) * 128].astype(F32)
        qs.append(_to_half(blk, hq % 2, h))
    q4 = jnp.concatenate(qs, axis=0).astype(BF)
    s = lax.dot_general(q4, k2, _DIMS["NT"], preferred_element_type=F32) * (HEAD_DIM ** -0.5)
    shape = s.shape
    row = lax.broadcasted_iota(jnp.int32, shape, 0)
    qi = row & (BLOCK - 1)
    kj = lax.broadcasted_iota(jnp.int32, shape, 1)
    diff = qi + BLOCK - kj
    valid = (diff >= 0) & (diff < BLOCK) & ((kj >= BLOCK) | (n > 0))
    s = jnp.where(valid, s, NEG)
    row1 = lax.broadcasted_iota(jnp.int32, (shape[0], 1), 0)
    sink = jnp.zeros((shape[0], 1), F32)
    for g in range(GROUP):
        sink = jnp.where((row1 >= g * BLOCK) & (row1 < (g + 1) * BLOCK), sinks_ref[0, GROUP * h + g], sink)
    m = jnp.maximum(jnp.max(s, axis=-1, keepdims=True), sink)
    e = jnp.exp(s - m)
    es = jnp.exp(sink - m)
    inv = 1.0 / (jnp.sum(e, axis=-1, keepdims=True) + es)
    return e * inv, es * inv, q4, k2, v2, rows, prev


def _attn_fwd(proj, sinks, carry=None):
    T = proj.shape[0]
    c_in, c_out, c_sems = _carry_io(carry)

    def body(*refs):
        qkv_ref, sinks_ref = refs[:2]
        ci_refs = refs[2:2 + len(c_in)]
        o_ref = refs[2 + len(c_in)]
        co_refs = refs[3 + len(c_in):3 + len(c_in) + len(c_out)]
        cs_refs = refs[3 + len(c_in) + len(c_out):]
        if carry is not None:
            carry.start(ci_refs, co_refs, cs_refs)

        def blk(n, z):
            outs = [None] * (N_Q_HEADS // 2)
            for h in range(N_KV_HEADS):
                p, _, _, _, v2, rows, _ = _attn_block(qkv_ref, sinks_ref, n, h)
                o = lax.dot_general(p.astype(BF), v2, _DIMS["NN"], preferred_element_type=F32)
                for g in range(GROUP):
                    hq = GROUP * h + g
                    piece = jnp.where(_lane_half((BLOCK, 128), h), o[g * BLOCK:(g + 1) * BLOCK], 0.0)
                    if hq % 2 != h:
                        piece = pltpu.roll(piece, HEAD_DIM, 1)
                    outs[hq // 2] = piece if outs[hq // 2] is None else outs[hq // 2] + piece
            for pb in range(N_Q_HEADS // 2):
                o_ref[rows, pb * 128:(pb + 1) * 128] = outs[pb].astype(BF)
            return z

        lax.fori_loop(0, T // BLOCK, blk, 0)
        if carry is not None:
            carry.finish(ci_refs, co_refs, cs_refs)

    res = pl.pallas_call(
        body, name="attn_fwd", grid=(1,),
        in_specs=[pl.BlockSpec((T, GLU_OFF), lambda i: (0, 0)), pl.BlockSpec(memory_space=pltpu.SMEM),
                  *[ANY] * len(c_in)],
        out_specs=[pl.BlockSpec((T, ATTN_WIDTH), lambda i: (0, 0)), *[ANY] * len(c_out)],
        out_shape=[_sds((T, ATTN_WIDTH), BF), *c_out], scratch_shapes=c_sems,
        compiler_params=_params(("arbitrary",)),
    )(proj, sinks, *c_in)
    return res[0], res[1:]


def _attn_bwd(proj, d_o, sinks, carry=None):
    T = proj.shape[0]
    c_in, c_out, c_sems = _carry_io(carry)

    def body(*refs):
        qkv_ref, do_ref, sinks_ref = refs[:3]
        ci_refs = refs[3:3 + len(c_in)]
        dqkv_ref, dsink_ref = refs[3 + len(c_in):5 + len(c_in)]
        co_refs = refs[5 + len(c_in):5 + len(c_in) + len(c_out)]
        dk_acc, dv_acc = refs[5 + len(c_in) + len(c_out):7 + len(c_in) + len(c_out)]
        cs_refs = refs[7 + len(c_in) + len(c_out):]
        if carry is not None:
            carry.start(ci_refs, co_refs, cs_refs)
        dsink_ref[...] = jnp.zeros_like(dsink_ref)
        dk_acc[...] = jnp.zeros_like(dk_acc)
        dv_acc[...] = jnp.zeros_like(dv_acc)

        def blk(n, carry):
            dqs = [None] * (N_Q_HEADS // 2)
            for h in range(N_KV_HEADS):
                p, psink, q4, k2, v2, rows, prev = _attn_block(qkv_ref, sinks_ref, n, h)
                dos = []
                for g in range(GROUP):
                    hq = GROUP * h + g
                    dos.append(_to_half(do_ref[rows, (hq // 2) * 128:(hq // 2 + 1) * 128].astype(F32), hq % 2, h))
                do4 = jnp.concatenate(dos, axis=0).astype(BF)
                dp = lax.dot_general(do4, v2, _DIMS["NT"], preferred_element_type=F32)
                delta = jnp.sum(p * dp, axis=-1, keepdims=True)
                ds = (p * (dp - delta) * (HEAD_DIM ** -0.5)).astype(BF)
                dsk = psink * delta
                for g in range(GROUP):
                    hq = GROUP * h + g
                    tot = -jnp.sum(dsk[g * BLOCK:(g + 1) * BLOCK], axis=0, keepdims=True)
                    lane = lax.broadcasted_iota(jnp.int32, (1, 128), 1)
                    dsink_ref[...] += jnp.where(lane == hq, tot, 0.0)
                dq = lax.dot_general(ds, k2, _DIMS["NN"], preferred_element_type=F32)
                dk = lax.dot_general(ds, q4, _DIMS["TN"], preferred_element_type=F32)
                dv = lax.dot_general(p.astype(BF), do4, _DIMS["TN"], preferred_element_type=F32)
                dk_acc[prev, :] += dk[:BLOCK]
                dk_acc[rows, :] += dk[BLOCK:]
                dv_acc[prev, :] += dv[:BLOCK]
                dv_acc[rows, :] += dv[BLOCK:]
                for g in range(GROUP):
                    hq = GROUP * h + g
                    piece = jnp.where(_lane_half((BLOCK, 128), h), dq[g * BLOCK:(g + 1) * BLOCK], 0.0)
                    if hq % 2 != h:
                        piece = pltpu.roll(piece, HEAD_DIM, 1)
                    dqs[hq // 2] = piece if dqs[hq // 2] is None else dqs[hq // 2] + piece
            for pb in range(N_Q_HEADS // 2):
                dqkv_ref[rows, pb * 128:(pb + 1) * 128] = dqs[pb].astype(BF)
            return carry

        lax.fori_loop(0, T // BLOCK, blk, 0)
        dqkv_ref[:, ATTN_WIDTH:ATTN_WIDTH + KV_WIDTH] = dk_acc[...].astype(BF)
        dqkv_ref[:, ATTN_WIDTH + KV_WIDTH:] = dv_acc[...].astype(BF)
        if carry is not None:
            carry.finish(ci_refs, co_refs, cs_refs)

    res = pl.pallas_call(
        body, name="attn_bwd", grid=(1,),
        in_specs=[pl.BlockSpec((T, GLU_OFF), lambda i: (0, 0)), pl.BlockSpec((T, ATTN_WIDTH), lambda i: (0, 0)),
                  pl.BlockSpec(memory_space=pltpu.SMEM), *[ANY] * len(c_in)],
        out_specs=[pl.BlockSpec((T, GLU_OFF), lambda i: (0, 0)), pl.BlockSpec((1, 128), lambda i: (0, 0)),
                   *[ANY] * len(c_out)],
        out_shape=[_sds((T, GLU_OFF), BF), _sds((1, 128), F32), *c_out],
        scratch_shapes=[pltpu.VMEM((T, KV_WIDTH), F32), pltpu.VMEM((T, KV_WIDTH), F32), *c_sems],
        compiler_params=_params(("arbitrary",)),
    )(proj, d_o, sinks, *c_in)
    return res[:2], res[2:]


CHUNK = 256
SUB = 32
WIN = CHUNK + 32
PAD_ROWS = SEQ + 2 * CONV_PAD
_GLU_SPECS = [pl.BlockSpec((SEQ, 256), functools.partial(lambda i, c: (0, c), c=GLU_OFF // 256 + c)) for c in range(4)]


def _glu_to_pad(a0, a1, b0, b1, zpad):
    C = CONV_CHANNELS
    zpad[0:CONV_PAD, :] = jnp.zeros((CONV_PAD, C), F32)
    zpad[CONV_PAD + SEQ:, :] = jnp.zeros((CONV_PAD, C), F32)
    zpad[CONV_PAD:CONV_PAD + SEQ, 0:256] = a0[...].astype(F32) * jax.nn.sigmoid(b0[...].astype(F32))
    zpad[CONV_PAD:CONV_PAD + SEQ, 256:C] = a1[...].astype(F32) * jax.nn.sigmoid(b1[...].astype(F32))


def _tap_windows(src, base, win):
    for b in range(8):
        win[b, 0:WIN - 8, :] = src[base + b:base + b + WIN - 8, :]


def _taps(win, w_ref, init, out, flip):
    def sub(si, carry):
        r0 = pl.multiple_of(si * SUB, SUB)
        acc = jnp.broadcast_to(init, (SUB, CONV_CHANNELS))
        for k in range(CONV_WIDTH):
            wk = (CONV_WIDTH - 1 - k) if flip else k
            acc = acc + w_ref[wk:wk + 1, :] * win[k % 8, pl.ds(r0 + 8 * (k // 8), SUB), :]
        out[pl.ds(r0, SUB), :] = acc
        return carry

    lax.fori_loop(0, CHUNK // SUB, sub, 0)


def _tap_grads(win, du, dwacc):
    def sub(si, carry):
        r0 = pl.multiple_of(si * SUB, SUB)
        d = du[pl.ds(r0, SUB), :]
        for k in range(CONV_WIDTH):
            p = d * win[k % 8, pl.ds(r0 + 8 * (k // 8), SUB), :]
            dwacc[8 * k:8 * k + 8, :] += (p[0:8] + p[8:16]) + (p[16:24] + p[24:32])
        return carry

    lax.fori_loop(0, CHUNK // SUB, sub, 0)


def _ln_parts(u):
    mu = jnp.mean(u, axis=-1, keepdims=True)
    xc = u - mu
    rstd = lax.rsqrt(jnp.mean(xc * xc, axis=-1, keepdims=True) + EPS)
    return xc * rstd, rstd


def _conv_fwd(proj, conv_w, conv_b, ln_g, ln_b, carry=None):
    T, C = proj.shape[0], CONV_CHANNELS
    vec = pl.BlockSpec((1, C), lambda i: (0, 0))
    c_in, c_out, c_sems = _carry_io(carry)

    def body(*refs):
        a0, a1, b0, b1, w_ref, cb_ref, g_ref, be_ref = refs[:8]
        ci_refs = refs[8:8 + len(c_in)]
        c_ref, u_ref = refs[8 + len(c_in):10 + len(c_in)]
        co_refs = refs[10 + len(c_in):10 + len(c_in) + len(c_out)]
        zpad, win, ubuf = refs[10 + len(c_in) + len(c_out):13 + len(c_in) + len(c_out)]
        cs_refs = refs[13 + len(c_in) + len(c_out):]
        if carry is not None:
            carry.start(ci_refs, co_refs, cs_refs)
        _glu_to_pad(a0, a1, b0, b1, zpad)
        for ci in range(T // CHUNK):
            _tap_windows(zpad, ci * CHUNK + CONV_PAD - (CONV_WIDTH - 1), win)
            _taps(win, w_ref, cb_ref[...], ubuf, False)
            u = ubuf[...]
            u_ref[ci * CHUNK:(ci + 1) * CHUNK, :] = u
            xh, _ = _ln_parts(u)
            ln = xh * g_ref[...] + be_ref[...]
            c_ref[ci * CHUNK:(ci + 1) * CHUNK, :] = (ln * jax.nn.sigmoid(ln)).astype(BF)
        if carry is not None:
            carry.finish(ci_refs, co_refs, cs_refs)

    res = pl.pallas_call(
        body, name="conv_fwd", grid=(1,),
        in_specs=[*_GLU_SPECS, pl.BlockSpec((CONV_PAD, C), lambda i: (0, 0)), vec, vec, vec, *[ANY] * len(c_in)],
        out_specs=[pl.BlockSpec((T, C), lambda i: (0, 0)), pl.BlockSpec((T, C), lambda i: (0, 0)), *[ANY] * len(c_out)],
        out_shape=[_sds((T, C), BF), _sds((T, C), F32), *c_out],
        scratch_shapes=[pltpu.VMEM((PAD_ROWS, C), F32), pltpu.VMEM((8, WIN, C), F32), pltpu.VMEM((CHUNK, C), F32),
                        *c_sems],
        compiler_params=_params(("arbitrary",)),
    )(proj, proj, proj, proj, conv_w, conv_b, ln_g, ln_b, *c_in)
    return res[:2], res[2:]


def _conv_bwd(proj, u, d_c, conv_w, conv_b, ln_g, ln_b, carry=None):
    T, C = proj.shape[0], CONV_CHANNELS
    vec = pl.BlockSpec((1, C), lambda i: (0, 0))
    wspec = pl.BlockSpec((CONV_PAD, C), lambda i: (0, 0))
    c_in, c_out, c_sems = _carry_io(carry)

    def body(*refs):
        a0, a1, b0, b1, u_ref, dc_ref, w_ref, cb_ref, g_ref, be_ref = refs[:10]
        ci_refs = refs[10:10 + len(c_in)]
        o = 10 + len(c_in)
        dglu_ref, dw_ref, dcb_ref, dg_ref, dbe_ref = refs[o:o + 5]
        co_refs = refs[o + 5:o + 5 + len(c_out)]
        zpad, dupad, win, ubuf, dwacc = refs[o + 5 + len(c_out):o + 10 + len(c_out)]
        cs_refs = refs[o + 10 + len(c_out):]
        if carry is not None:
            carry.start(ci_refs, co_refs, cs_refs)
        _glu_to_pad(a0, a1, b0, b1, zpad)
        dupad[T:, :] = jnp.zeros((2 * CONV_PAD, C), F32)
        dwacc[...] = jnp.zeros_like(dwacc)
        dcb_ref[...] = jnp.zeros_like(dcb_ref)
        dg_ref[...] = jnp.zeros_like(dg_ref)
        dbe_ref[...] = jnp.zeros_like(dbe_ref)
        for ci in range(T // CHUNK):
            rows = slice(ci * CHUNK, (ci + 1) * CHUNK)
            _tap_windows(zpad, ci * CHUNK + CONV_PAD - (CONV_WIDTH - 1), win)
            xh, rstd = _ln_parts(u_ref[rows, :])
            ln = xh * g_ref[...] + be_ref[...]
            sg = jax.nn.sigmoid(ln)
            dln = dc_ref[rows, :].astype(F32) * (sg * (1.0 + ln * (1.0 - sg)))
            dg_ref[...] += jnp.sum(dln * xh, axis=0, keepdims=True)
            dbe_ref[...] += jnp.sum(dln, axis=0, keepdims=True)
            dxh = dln * g_ref[...]
            du = rstd * (dxh - jnp.mean(dxh, axis=-1, keepdims=True)
                         - xh * jnp.mean(dxh * xh, axis=-1, keepdims=True))
            dupad[rows, :] = du
            dcb_ref[...] += jnp.sum(du, axis=0, keepdims=True)
            _tap_grads(win, dupad.at[rows, :], dwacc)
        for k in range(CONV_WIDTH):
            dw_ref[k:k + 1, :] = jnp.sum(dwacc[8 * k:8 * k + 8, :], axis=0, keepdims=True)
        dw_ref[CONV_WIDTH:, :] = jnp.zeros((CONV_PAD - CONV_WIDTH, C), F32)
        for ci in range(T // CHUNK):
            rows = slice(ci * CHUNK, (ci + 1) * CHUNK)
            _tap_windows(dupad, ci * CHUNK, win)
            _taps(win, w_ref, jnp.zeros((1, C), F32), ubuf, True)
            dz = ubuf[...]
            for half, (a, b) in enumerate(((a0, b0), (a1, b1))):
                sb = jax.nn.sigmoid(b[rows, :].astype(F32))
                dzh = dz[:, half * 256:(half + 1) * 256]
                dglu_ref[rows, half * 256:(half + 1) * 256] = (dzh * sb).astype(BF)
                dglu_ref[rows, C + half * 256:C + (half + 1) * 256] = (
                    dzh * a[rows, :].astype(F32) * sb * (1.0 - sb)).astype(BF)
        if carry is not None:
            carry.finish(ci_refs, co_refs, cs_refs)

    res = pl.pallas_call(
        body, name="conv_bwd", grid=(1,),
        in_specs=[*_GLU_SPECS, pl.BlockSpec((T, C), lambda i: (0, 0)), pl.BlockSpec((T, C), lambda i: (0, 0)), wspec,
                  vec, vec, vec, *[ANY] * len(c_in)],
        out_specs=[pl.BlockSpec((T, 2 * C), lambda i: (0, 0)), wspec, vec, vec, vec, *[ANY] * len(c_out)],
        out_shape=[_sds((T, 2 * C), BF), _sds((CONV_PAD, C), F32), _sds((1, C), F32), _sds((1, C), F32),
                   _sds((1, C), F32), *c_out],
        scratch_shapes=[pltpu.VMEM((PAD_ROWS, C), F32), pltpu.VMEM((PAD_ROWS, C), F32), pltpu.VMEM((8, WIN, C), F32),
                        pltpu.VMEM((CHUNK, C), F32), pltpu.VMEM((8 * CONV_PAD, C), F32), *c_sems],
        compiler_params=_params(("arbitrary",)),
    )(proj, proj, proj, proj, u, d_c, conv_w, conv_b, ln_g, ln_b, *c_in)
    return res[:5], res[5:]


_GATE_BLK = GATE_OFF // 256


def _ffn_in_swiglu(h2, wf_t, carry=None):
    T, D = h2.shape
    tm, tn = 1024, D_FF // 2
    nj, ni = D_FF // tn, T // tm
    c_in, c_out, c_sems = _carry_io(carry)

    def body(*refs):
        a_ref, bg_ref, bu_ref = refs[:3]
        ci_refs = refs[3:3 + len(c_in)]
        act_ref, g_ref, u_ref = refs[3 + len(c_in):6 + len(c_in)]
        co_refs = refs[6 + len(c_in):6 + len(c_in) + len(c_out)]
        cs_refs = refs[6 + len(c_in) + len(c_out):]
        j, i = pl.program_id(0), pl.program_id(1)
        if carry is not None:
            @pl.when((j == 0) & (i == 0))
            def _():
                carry.start(ci_refs, co_refs, cs_refs)
        a = a_ref[...]
        for c0, c1 in ((0, 768), (768, tn)):
            g = lax.dot_general(a, bg_ref[c0:c1, :], _DIMS["NT"], preferred_element_type=F32)
            u = lax.dot_general(a, bu_ref[c0:c1, :], _DIMS["NT"], preferred_element_type=F32)
            act_ref[:, c0:c1] = (g * jax.nn.sigmoid(g) * u).astype(BF)
            g_ref[:, c0:c1] = g.astype(BF)
            u_ref[:, c0:c1] = u.astype(BF)
        if carry is not None:
            @pl.when((j == nj - 1) & (i == ni - 1))
            def _():
                carry.finish(ci_refs, co_refs, cs_refs)

    t = pl.BlockSpec((tm, tn), lambda j, i: (i, j))
    res = pl.pallas_call(
        body, name="ffn_in_swiglu", grid=(nj, ni),
        in_specs=[pl.BlockSpec((tm, D), lambda j, i: (i, 0)), pl.BlockSpec((tn, D), lambda j, i: (j, 0)),
                  pl.BlockSpec((tn, D), lambda j, i: (nj + j, 0)), *[ANY] * len(c_in)],
        out_specs=[t, t, t, *[ANY] * len(c_out)], out_shape=[*[_sds((T, D_FF), BF)] * 3, *c_out],
        scratch_shapes=c_sems,
        compiler_params=_params(("arbitrary", "arbitrary")),
    )(h2, wf_t, wf_t, *c_in)
    return res[:3], res[3:]


def _proj_merge(o, c, wap_t, wcp_t, b_cp, proj):
    T, D = o.shape[0], wap_t.shape[0]
    tm, tg = T, 256
    nj = D // tg

    def body(o_ref, c_ref, wa_ref, wc_ref, b_ref, g0_ref, g1_ref, ya_ref, yc_ref, m_ref):
        ya = lax.dot_general(o_ref[...], wa_ref[...], _DIMS["NT"], preferred_element_type=F32)
        yc = lax.dot_general(c_ref[...], wc_ref[...], _DIMS["NT"], preferred_element_type=F32) + b_ref[...]
        ya_ref[...] = ya.astype(BF)
        yc_ref[...] = yc.astype(BF)
        m_ref[...] = (jax.nn.sigmoid(g0_ref[...].astype(F32)) * ya + jax.nn.sigmoid(g1_ref[...].astype(F32)) * yc).astype(BF)

    act = pl.BlockSpec((tm, o.shape[1]), lambda j, i: (i, 0))
    wgt = pl.BlockSpec((tg, o.shape[1]), lambda j, i: (j, 0))
    t = pl.BlockSpec((tm, tg), lambda j, i: (i, j))
    return pl.pallas_call(
        body, name="proj_merge", grid=(nj, T // tm),
        in_specs=[act, act, wgt, wgt, pl.BlockSpec((1, tg), lambda j, i: (0, j)),
                  pl.BlockSpec((tm, tg), lambda j, i: (i, _GATE_BLK + j)),
                  pl.BlockSpec((tm, tg), lambda j, i: (i, _GATE_BLK + nj + j))],
        out_specs=[t, t, t], out_shape=[_sds((T, D), BF)] * 3,
        compiler_params=_params(("arbitrary", "arbitrary")),
    )(o, c, wap_t, wcp_t, b_cp, proj, proj)


def _stacked_dw(name, segs, h, tb):
    T, D = h.shape
    nblk = [seg.shape[1] // tb for seg in segs]
    starts = [sum(nblk[:q]) for q in range(len(segs))]
    n_seg = len(segs)

    def body(*refs):
        seg_refs, h_ref, o_ref, cs_ref = refs[:n_seg], refs[n_seg], refs[n_seg + 1], refs[n_seg + 2]
        i = pl.program_id(0)
        for seg_ref, st, nb in zip(seg_refs, starts, nblk):
            @pl.when((i >= st) & (i < st + nb))
            def _(seg_ref=seg_ref):
                a = seg_ref[...]
                o_ref[...] = lax.dot_general(a, h_ref[...], _DIMS["TN"], preferred_element_type=F32).astype(BF)
                cs_ref[...] = jnp.sum(a.astype(F32), axis=0, keepdims=True)

    in_specs = [pl.BlockSpec((T, tb), functools.partial(lambda i, st, nb: (0, jnp.clip(i - st, 0, nb - 1)), st=st, nb=nb))
                for st, nb in zip(starts, nblk)]
    return pl.pallas_call(
        body, name=name, grid=(sum(nblk),),
        in_specs=[*in_specs, pl.BlockSpec((T, D), lambda i: (0, 0))],
        out_specs=[pl.BlockSpec((tb, D), lambda i: (i, 0)), pl.BlockSpec((1, tb), lambda i: (0, i))],
        out_shape=[_sds((sum(nblk) * tb, D), BF), _sds((1, sum(nblk) * tb), F32)],
        compiler_params=_params(("arbitrary",)),
    )(*segs, h)


def _local_step(x, h, r1, target, small, wi_t, conv_w, plan):
    T, D = x.shape
    tm = 1024

    def carried(call, res, carry):
        if carry is None:
            return res
        outs, got = res
        plan.done(call, got)
        return outs


    def ep_add(acc, ex, outs, ids, scr):
        outs[0][...] = acc + ex[0][...]

    tn_in = IN_WIDTH // 2
    carry = plan.carry("proj_in")
    def ep_bias_bf16(acc, ex, outs, ids, scr):
        outs[0][...] = (acc + ex[0][...]).astype(BF)

    proj, = carried("proj_in", _matmul("proj_in", [h], wi_t, "NT", m=T, n=IN_WIDTH, tm=tm, tn=tn_in,
                                       epilogue=ep_bias_bf16, extra=[(small["b_in"], _row(tn_in))],
                                       outs=[(_sds((T, IN_WIDTH), BF), _tile(tm, tn_in))], carry=carry), carry)
    plan.launch("gather_ffn", after=proj)
    o, got = _attn_fwd(proj, small["sinks"], carry=plan.carry("attn_fwd"))
    plan.done("attn_fwd", got)
    (c, u_conv), got = _conv_fwd(proj, conv_w, small["conv_b"], small["ln_g"], small["ln_b"],
                                 carry=plan.carry("conv_fwd"))
    plan.done("conv_fwd", got)
    wap_t, wcp_t, w_out = plan.weight("w_attn_proj"), plan.weight("w_conv_proj"), plan.weight("w_out")
    ya, yc, merged = _proj_merge(o, c, wap_t, wcp_t, small["b_conv_proj"], proj)

    tg = 256
    gate_specs = [pl.BlockSpec((T, tg), lambda j, i, k: (i, _GATE_BLK + j)),
                  pl.BlockSpec((T, tg), lambda j, i, k: (i, _GATE_BLK + D // tg + j))]

    def ep_residual_rms(acc, ex, outs, ids, scr):
        x2v = acc + ex[0][...]
        r = lax.rsqrt(jnp.mean(x2v * x2v, axis=-1, keepdims=True) + EPS)
        outs[0][...] = x2v
        outs[1][...] = (x2v * r * ex[1][...]).astype(BF)
        outs[2][...] = r

    carry = plan.carry("out_proj")
    x2, h2, r2 = carried("out_proj", _matmul(
        "out_proj_rms", [merged], w_out, "NN", m=T, n=D, tm=512, tn=D, epilogue=ep_residual_rms,
        extra=[(x, _tile(512, D)), (small["g_ffn_norm"], _row(D))],
        outs=[(_sds((T, D), F32), _tile(512, D)), (_sds((T, D), BF), _tile(512, D)),
              (_sds((T, 1), F32), pl.BlockSpec((512, 1), lambda j, i, k: (i, 0)))], carry=carry), carry)
    plan.launch("gather_down", after=x2)
    wf_t = plan.weight("w_ffn_in")
    (act, gate, up), got = _ffn_in_swiglu(h2, wf_t, carry=plan.carry("ffn_in_swiglu"))
    plan.done("ffn_in_swiglu", got)
    w_down = plan.weight("w_ffn_down")
    def ep_residual_loss(acc, ex, outs, ids, scr):
        dx, dg, part = _loss_head(acc + ex[0][...], ex[1][...], ex[2][...])
        outs[0][...] = dx
        outs[1][...] = dx.astype(BF)
        _accumulate_rows(outs[2], dg, ids[1] == 0)
        _accumulate_rows(outs[3], part, ids[1] == 0)

    dx3, dx3_b, dg_final, loss = _matmul(
        "ffn_down_loss", [act], w_down, "NN", m=T, n=D, tm=512, tn=D, epilogue=ep_residual_loss,
        extra=[(x2, _tile(512, D)), (small["g_final"], _row(D)), (target, _tile(512, D))],
        outs=[(_sds((T, D), F32), _tile(512, D)), (_sds((T, D), BF), _tile(512, D)), (_sds((1, D), F32), _row(D)),
              (_sds((1, 1), F32), pl.BlockSpec((1, 1), lambda j, i, k: (0, 0)))])

    tn_ff = D_FF // 2

    def ep_swiglu_bwd(acc, ex, outs, ids, scr):
        g, u = ex[0][...].astype(F32), ex[1][...].astype(F32)
        sg = jax.nn.sigmoid(g)
        outs[0][...] = (acc * u * sg * (1.0 + g * (1.0 - sg))).astype(BF)
        outs[1][...] = (acc * g * sg).astype(BF)

    dgate, dup = _matmul(
        "ffn_down_bwd", [dx3_b], w_down, "NT", m=T, n=D_FF, tm=tm, tn=tn_ff, epilogue=ep_swiglu_bwd,
        extra=[(gate, _tile(tm, tn_ff)), (up, _tile(tm, tn_ff))],
        outs=[(_sds((T, D_FF), BF), _tile(tm, tn_ff)), (_sds((T, D_FF), BF), _tile(tm, tn_ff))])

    def dw(name, a, b, rows, cols, row_off=0, alias=None, total_rows=None, colsum=False):
        total_rows = rows if total_rows is None else total_rows
        tmw = rows if rows <= 1024 else D_FF // 2
        blk, rem = divmod(row_off, tmw)
        assert rem == 0

        def ep(acc, ex, outs, ids, scr):
            outs[0][...] = acc.astype(BF)
            if colsum:
                outs[1][...] = jnp.sum(ex[0][...].astype(F32), axis=0, keepdims=True)

        outs = [(_sds((total_rows, cols), BF), pl.BlockSpec((tmw, cols), lambda j, i, k: (blk + i, j)))]
        extra = []
        if colsum:
            extra = [(a, pl.BlockSpec((T, tmw), lambda j, i, k: (0, i)))]
            outs.append((_sds((1, rows), F32), pl.BlockSpec((1, tmw), lambda j, i, k: (0, i))))
        carry = plan.carry(name)
        res = carried(name, _matmul(name, [a], b, "TN", m=rows, n=cols, tm=tmw, tn=cols, epilogue=ep, extra=extra,
                                    outs=outs, alias=None if alias is None else (alias, 0), carry=carry), carry)
        return res if colsum else res[0]

    plan.grad_ready(dict(w_ffn_down=dw("ffn_down_dw", act, dx3_b, D_FF, D)))

    def ep_rms_bwd(acc, ex, outs, ids, scr):
        dx, dg = _rms_bwd(acc, ex[0][...], ex[1][...], ex[2][...])
        dx = ex[3][...] + dx
        outs[0][...] = dx
        outs[1][...] = dx.astype(BF)
        _accumulate_rows(outs[2], dg, ids[1] == 0)

    def rms_bwd_io(tm_, xin, r, g, dres):
        return dict(
            extra=[(xin, _tile(tm_, D)), (r, pl.BlockSpec((tm_, 1), lambda j, i, k: (i, 0))), (g, _row(D)),
                   (dres, _tile(tm_, D))],
            outs=[(_sds((T, D), F32), _tile(tm_, D)), (_sds((T, D), BF), _tile(tm_, D)), (_sds((1, D), F32), _row(D))])

    carry = plan.carry("ffn_in_bwd")
    dx2, dx2_b, dg_ffn = carried(
        "ffn_in_bwd",
        _matmul("ffn_in_bwd", [dgate, dup], wf_t, "NN", m=T, n=D, tm=tm, tn=D, tk=D_FF // 2, epilogue=ep_rms_bwd,
                carry=carry, **rms_bwd_io(tm, x2, r2, small["g_ffn_norm"], dx3)), carry)
    plan.launch("send_down")
    gwf_t, _ = _stacked_dw("ffn_in_dw", [dgate, dup], h2, D_FF // 2)
    plan.grad_ready(dict(w_ffn_in=gwf_t))

    def ep_merge_bwd(acc, ex, outs, ids, scr):
        s0 = jax.nn.sigmoid(ex[2][...].astype(F32))
        s1 = jax.nn.sigmoid(ex[3][...].astype(F32))
        outs[0][...] = (acc * s0).astype(BF)
        outs[1][...] = (acc * s1).astype(BF)
        outs[2][...] = (acc * ex[0][...].astype(F32) * s0 * (1.0 - s0)).astype(BF)
        outs[3][...] = (acc * ex[1][...].astype(F32) * s1 * (1.0 - s1)).astype(BF)

    carry = plan.carry("out_proj_bwd_merge")
    dya, dyc, dg0, dg1 = carried(
        "out_proj_bwd_merge",
        _matmul("out_proj_bwd_merge", [dx2_b], w_out, "NT", m=T, n=D, tm=T, tn=tg, epilogue=ep_merge_bwd,
                extra=[(ya, _tile(T, tg)), (yc, _tile(T, tg)), (proj, gate_specs[0]), (proj, gate_specs[1])],
                outs=[(_sds((T, D), BF), _tile(T, tg))] * 4, carry=carry), carry)
    plan.launch("send_ffn")
    gw_out = dw("out_proj_dw", merged, dx2_b, D, D)
    d_o, = _matmul("attn_proj_bwd", [dya], wap_t, "NN", m=T, n=ATTN_WIDTH, tm=tm, tn=ATTN_WIDTH,
                   epilogue=_store(BF), outs=[(_sds((T, ATTN_WIDTH), BF), _tile(tm, ATTN_WIDTH))])
    d_c, = _matmul("conv_proj_bwd", [dyc], wcp_t, "NN", m=T, n=CONV_CHANNELS, tm=tm, tn=CONV_CHANNELS,
                   epilogue=_store(BF), outs=[(_sds((T, CONV_CHANNELS), BF), _tile(tm, CONV_CHANNELS))])
    gwap_t = dw("attn_proj_dw", dya, o, D, ATTN_WIDTH)
    gwcp_t, db_cp = dw("conv_proj_dw", dyc, c, D, CONV_CHANNELS, colsum=True)
    plan.grad_ready(dict(w_out=gw_out, w_attn_proj=gwap_t, w_conv_proj=gwcp_t))
    (dglu, dcw, dcb, dlng, dlnb), got = _conv_bwd(proj, u_conv, d_c, conv_w, small["conv_b"], small["ln_g"],
                                                  small["ln_b"], carry=plan.carry("conv_bwd"))
    plan.done("conv_bwd", got)
    plan.launch("send_mix")
    (dqkv, dsinks), got = _attn_bwd(proj, d_o, small["sinks"], carry=plan.carry("attn_bwd"))
    plan.done("attn_bwd", got)

    segs = [dqkv, dglu, dg0, dg1]
    gwi_t, db_in = _stacked_dw("proj_in_dw", segs, h, 256)
    plan.grad_ready(dict(w_in=gwi_t))
    plan.alone("swap_inp")
    plan.launch("send_inp")
    carry = plan.carry("proj_in_bwd")
    dx, _, dg_mix = carried(
        "proj_in_bwd",
        _matmul("proj_in_bwd", segs, wi_t, "NN", m=T, n=D, tm=512, tn=D, epilogue=ep_rms_bwd, carry=carry,
                **rms_bwd_io(512, x, r1, small["g_mix_norm"], plan.behind("inp", dx2))), carry)

    parts = dict(g_mix_norm=dg_mix, b_in=db_in, sinks=dsinks, conv_w=dcw, conv_b=dcb, ln_g=dlng, ln_b=dlnb,
                 b_conv_proj=db_cp, g_ffn_norm=dg_ffn, g_final=dg_final, loss=loss)
    return dx, parts


def _place():
    x, y, c = lax.axis_index("x"), lax.axis_index("y"), lax.axis_index("c")
    return x, y, c, [(1 - x, y), (x, 1 - y), (1 - x, 1 - y)]


def _gather_copies(x_refs, out_refs, rows_per, send_sems, recv_sems, local_sems):
    x, y, c, chips = _place()
    me, sibling = (x, y, c), (x, y, 1 - c)

    def rows(a, px, py, pc):
        return out_refs[a].at[pl.ds((4 * px + 2 * py + pc) * rows_per[a], rows_per[a])]

    def copy(a, k, block, to, src=None):
        return pltpu.make_async_remote_copy(
            src_ref=rows(a, *block) if src is None else src, dst_ref=rows(a, *block),
            send_sem=send_sems.at[7 * a + k], recv_sem=recv_sems.at[7 * a + k], device_id=to, device_id_type=MESH)

    def local(a):
        return pltpu.make_async_copy(x_refs[a], rows(a, *me), local_sems.at[a])

    def first(a):
        return [copy(a, 0, me, sibling, src=x_refs[a])] + [copy(a, 1 + j, me, (*chip, c), src=x_refs[a])
                                                          for j, chip in enumerate(chips)]

    def arrive(a, j):
        return copy(a, 1 + j, (*chips[j], c), me)

    def passed(a, j):
        return copy(a, 4 + j, (*chips[j], c), sibling)

    def from_sibling(a):
        return [copy(a, 0, sibling, me)] + [copy(a, 4 + j, (*chip, 1 - c), me) for j, chip in enumerate(chips)]

    return len(x_refs), local, first, arrive, passed, from_sibling


def _gather_start(*refs):
    n, local, first, _, _, _ = _gather_copies(*refs)
    for a in range(n):
        local(a).start()
        for cp in first(a):
            cp.start()


def _gather_finish(*refs):
    n, local, first, arrive, passed, from_sibling = _gather_copies(*refs)
    for a in range(n):
        for j in range(3):
            arrive(a, j).wait_recv()
            passed(a, j).start()
    for a in range(n):
        for cp in from_sibling(a):
            cp.wait_recv()
    for a in range(n):
        for cp in first(a) + [passed(a, j) for j in range(3)]:
            cp.wait_send()
        local(a).wait()


def _gather_peers():
    x, y, c, chips = _place()
    return [(x, y, 1 - c)] + [(*chip, c) for chip in chips]


def _gather_sems(n):
    return [pltpu.SemaphoreType.DMA((7 * n,)), pltpu.SemaphoreType.DMA((7 * n,)), pltpu.SemaphoreType.DMA((n,))]


def _gather_carry(shards):
    rows_per = [s.shape[0] for s in shards]
    return _Carry(shards, [_sds((N_DEV * s.shape[0],) + s.shape[1:], s.dtype) for s in shards],
                  _gather_sems(len(shards)),
                  lambda ins, outs, sems: _gather_start(ins, outs, rows_per, *sems),
                  lambda ins, outs, sems: _gather_finish(ins, outs, rows_per, *sems), _gather_peers)


def _first_gather(shards, x, g):
    n = len(shards)
    rows_per = [s.shape[0] for s in shards]
    T, D = x.shape

    def body(*refs):
        x_refs, (xin_ref, g_ref), out_refs, (h_ref, r_ref) = refs[:n], refs[n:n + 2], refs[n + 2:2 * n + 2], refs[2 * n + 2:2 * n + 4]
        send_sems, recv_sems, local_sems = refs[2 * n + 4:]
        x, y, c, chips = _place()
        me, sibling = (x, y, c), (x, y, 1 - c)
        near_x, near_y, far = (*chips[0], c), (*chips[1], c), (*chips[2], c)

        def rows(a, dev, part):
            h = rows_per[a] // 2
            lo, size = {"all": (0, 2 * h), "low": (0, h), "high": (h, h)}[part]
            return out_refs[a].at[pl.ds((4 * dev[0] + 2 * dev[1] + dev[2]) * rows_per[a] + lo, size)]

        def copy(a, k, block, part, to, src=None):
            return pltpu.make_async_remote_copy(
                src_ref=rows(a, block, part) if src is None else src, dst_ref=rows(a, block, part),
                send_sem=send_sems.at[9 * a + k], recv_sem=recv_sems.at[9 * a + k], device_id=to, device_id_type=MESH)

        other = lambda dev: (dev[0], dev[1], 1 - c)
        sent = []
        for a in range(n):
            pltpu.make_async_copy(x_refs[a], rows(a, me, "all"), local_sems.at[a]).start()
            sent += [copy(a, 0, me, "all", sibling, src=x_refs[a]), copy(a, 1, me, "all", near_x, src=x_refs[a]),
                     copy(a, 2, me, "all", near_y, src=x_refs[a])]
        for cp in sent:
            cp.start()
        for i in range(T // CHUNK):
            rws = slice(i * CHUNK, (i + 1) * CHUNK)
            xv = xin_ref[rws, :]
            r = lax.rsqrt(jnp.mean(xv * xv, axis=-1, keepdims=True) + EPS)
            h_ref[rws, :] = (xv * r * g_ref[...]).astype(BF)
            r_ref[rws, :] = r
        for a in range(n):
            copy(a, 1, near_x, "all", me).wait_recv()
            copy(a, 2, near_y, "all", me).wait_recv()
            passed = [copy(a, 3, near_y, "high", near_x), copy(a, 4, near_x, "low", near_y),
                      copy(a, 5, near_x, "all", sibling), copy(a, 6, near_y, "all", sibling)]
            for cp in passed:
                cp.start()
            sent += passed
        for a in range(n):
            copy(a, 3, far, "high", me).wait_recv()
            copy(a, 4, far, "low", me).wait_recv()
            passed = [copy(a, 7, far, "high", sibling), copy(a, 8, far, "low", sibling)]
            for cp in passed:
                cp.start()
            sent += passed
        for a in range(n):
            copy(a, 0, sibling, "all", me).wait_recv()
            copy(a, 5, other(near_x), "all", me).wait_recv()
            copy(a, 6, other(near_y), "all", me).wait_recv()
            copy(a, 7, other(far), "high", me).wait_recv()
            copy(a, 8, other(far), "low", me).wait_recv()
        for cp in sent:
            cp.wait_send()
        for a in range(n):
            pltpu.make_async_copy(x_refs[a], rows(a, me, "all"), local_sems.at[a]).wait()

    vm = pl.BlockSpec(memory_space=pltpu.VMEM)
    return pl.pallas_call(
        body, name="weights_first_gather", in_specs=[*[ANY] * n, vm, vm], out_specs=[*[ANY] * n, vm, vm],
        out_shape=[*[_sds((N_DEV * s.shape[0],) + s.shape[1:], s.dtype) for s in shards], _sds((T, D), BF),
                   _sds((T, 1), F32)],
        scratch_shapes=[pltpu.SemaphoreType.DMA((9 * n,)), pltpu.SemaphoreType.DMA((9 * n,)),
                        pltpu.SemaphoreType.DMA((n,))],
        compiler_params=pltpu.CompilerParams(vmem_limit_bytes=VMEM_LIMIT_BYTES),
    )(*shards, x, g)


def _swap_carry(grads):
    n = len(grads)

    def copies(g_refs, out_refs, sems):
        send_sems, recv_sems = sems
        x, y, c, _ = _place()
        return [pltpu.make_async_remote_copy(
            src_ref=g_refs[a].at[2 * p + 1 - c], dst_ref=out_refs[a].at[p],
            send_sem=send_sems.at[4 * a + p], recv_sem=recv_sems.at[4 * a + p],
            device_id=(x, y, 1 - c), device_id_type=MESH) for a in range(n) for p in range(4)]

    def start(ins, outs, sems):
        for cp in copies(ins, outs, sems):
            cp.start()

    def finish(ins, outs, sems):
        for cp in copies(ins, outs, sems):
            cp.wait()

    def peers():
        x, y, c, _ = _place()
        return [(x, y, 1 - c)]

    return _Carry(grads, [_sds((4,) + g.shape[1:], g.dtype) for g in grads],
                  [pltpu.SemaphoreType.DMA((4 * n,)), pltpu.SemaphoreType.DMA((4 * n,))], start, finish, peers)


def _join(carries):
    carries = [c for c in carries if c is not None]
    if not carries:
        return None
    n_in = [len(c.arrays) for c in carries]
    n_out = [len(c.out_shapes) for c in carries]
    n_sem = [len(c.sems) for c in carries]

    def parts(refs, counts):
        cuts = [sum(counts[:q]) for q in range(len(counts) + 1)]
        return [refs[cuts[q]:cuts[q + 1]] for q in range(len(counts))]

    def start(ins, outs, sems):
        for c, i, o, s in zip(carries, parts(ins, n_in), parts(outs, n_out), parts(sems, n_sem)):
            c.start(i, o, s)

    def finish(ins, outs, sems):
        for c, i, o, s in zip(carries, parts(ins, n_in), parts(outs, n_out), parts(sems, n_sem)):
            c.finish(i, o, s)

    return _Carry([a for c in carries for a in c.arrays], [o for c in carries for o in c.out_shapes],
                  [s for c in carries for s in c.sems], start, finish)


def _run_carry(name, carry):
    n_in, n_out = len(carry.arrays), len(carry.out_shapes)

    def body(*refs):
        carry.start(refs[:n_in], refs[n_in:n_in + n_out], refs[n_in + n_out:])
        carry.finish(refs[:n_in], refs[n_in:n_in + n_out], refs[n_in + n_out:])

    return pl.pallas_call(body, name=name, in_specs=[ANY] * n_in, out_specs=[ANY] * n_out,
                          out_shape=carry.out_shapes, scratch_shapes=carry.sems)(*carry.arrays)


def _run_carry_async(name, carry, collective_id):
    ins = [jax.new_ref(a, memory_space=pltpu.MemorySpace.HBM) for a in carry.arrays]
    outs = [jax.empty_ref(o, memory_space=pltpu.MemorySpace.HBM) for o in carry.out_shapes]

    @pl.kernel(mesh=plsc.ScalarSubcoreMesh(axis_name="sequencer", num_cores=1), name=name,
               scratch_types=tuple(carry.sems), compiler_params=pltpu.CompilerParams(collective_id=collective_id))
    def launch(*sems):
        barrier = pltpu.get_barrier_semaphore()
        peers = carry.peers()
        for peer in peers:
            pl.semaphore_signal(barrier, inc=1, device_id=peer, device_id_type=MESH)
        pl.semaphore_wait(barrier, len(peers))
        carry.start(ins, outs, sems)
        carry.finish(ins, outs, sems)

    launch()
    return [o[...] for o in outs]


def _chip_sums(name, gs, gots, c):
    n = len(gs)

    def body(c_ref, *refs):
        for g_ref, got_ref, o_ref in zip(refs[:n], refs[n:2 * n], refs[2 * n:]):
            o_ref[...] = (g_ref[...].astype(F32) + got_ref[...].astype(F32)).astype(BF)

    mine = [pl.BlockSpec((1,) + g.shape[1:], lambda p, c_ref: (2 * p + c_ref[0], 0, 0)) for g in gs]
    slot = [pl.BlockSpec((1,) + g.shape[1:], lambda p, c_ref: (p, 0, 0)) for g in gs]
    return pl.pallas_call(
        body, name=name,
        grid_spec=pltpu.PrefetchScalarGridSpec(num_scalar_prefetch=1, grid=(4,), in_specs=[*mine, *slot],
                                               out_specs=slot),
        out_shape=[_sds((4,) + g.shape[1:], BF) for g in gs],
        compiler_params=_params(("arbitrary",)),
    )(c, *gs, *gots)


def _send_carry(sums, ks):
    n, nk = len(sums), len(ks)

    def copies(s_refs, out_refs, sems):
        send_sems, recv_sems = sems
        x, y, c, chips = _place()
        return [pltpu.make_async_remote_copy(
            src_ref=s_refs[a].at[2 * chips[k][0] + chips[k][1]], dst_ref=out_refs[a].at[q],
            send_sem=send_sems.at[nk * a + q], recv_sem=recv_sems.at[nk * a + q],
            device_id=(*chips[k], c), device_id_type=MESH) for a in range(n) for q, k in enumerate(ks)]

    def start(ins, outs, sems):
        for cp in copies(ins, outs, sems):
            cp.start()

    def finish(ins, outs, sems):
        for cp in copies(ins, outs, sems):
            cp.wait()

    def peers():
        x, y, c, chips = _place()
        return [(*chips[k], c) for k in ks]

    return _Carry(sums, [_sds((nk,) + s.shape[1:], s.dtype) for s in sums],
                  [pltpu.SemaphoreType.DMA((nk * n,)), pltpu.SemaphoreType.DMA((nk * n,))], start, finish, peers)


def _adam_math(w, g, m, v):
    m = ADAM_B1 * m + (1.0 - ADAM_B1) * g
    v = ADAM_B2 * v + (1.0 - ADAM_B2) * (g * g)
    m_hat = m / (1.0 - ADAM_B1 ** ADAM_STEP)
    v_hat = v / (1.0 - ADAM_B2 ** ADAM_STEP)
    delta = -ADAM_LR * (m_hat / (jnp.sqrt(v_hat) + ADAM_EPS) + ADAM_WD * w)
    return delta, m, v


def _adamw(name, w, g, m, v):
    rows, cols = w.shape
    tr = 256 if rows % 256 == 0 else rows

    def body(w_ref, g_ref, m_ref, v_ref, d_ref, nm_ref, nv_ref):
        d_ref[...], nm_ref[...], nv_ref[...] = _adam_math(w_ref[...], g_ref[...], m_ref[...], v_ref[...])

    t = pl.BlockSpec((tr, cols), lambda i: (i, 0))
    return pl.pallas_call(
        body, name=name, grid=(rows // tr,), in_specs=[t] * 4, out_specs=[t] * 3,
        out_shape=[_sds((rows, cols), F32)] * 3, compiler_params=_params(("arbitrary",)),
    )(w, g, m, v)


def _grad_adamw(name, g, got, got3, ids, w, m, v):
    _, rows, cols = g.shape
    n3 = len(got3)
    tr = rows // 2 if rows >= 256 else rows

    def body(ids_ref, g_ref, got_ref, *rest):
        w_ref, m_ref, v_ref, o_ref, d_ref, nm_ref, nv_ref = rest[n3:]
        tot = g_ref[0].astype(F32) + got_ref[0].astype(F32)
        for r_ref in rest[:n3]:
            for q in range(r_ref.shape[0]):
                tot = tot + r_ref[q].astype(F32)
        o_ref[...] = tot
        d_ref[...], nm_ref[...], nv_ref[...] = _adam_math(w_ref[...], tot, m_ref[...], v_ref[...])

    tile = pl.BlockSpec((tr, cols), lambda i, ids_ref: (i, 0))
    return pl.pallas_call(
        body, name=name,
        grid_spec=pltpu.PrefetchScalarGridSpec(
            num_scalar_prefetch=1, grid=(rows // tr,),
            in_specs=[pl.BlockSpec((1, tr, cols), lambda i, ids_ref: (ids_ref[0], i, 0)),
                      pl.BlockSpec((1, tr, cols), lambda i, ids_ref: (ids_ref[1], i, 0)),
                      *[pl.BlockSpec((r.shape[0], tr, cols), lambda i, ids_ref: (0, i, 0)) for r in got3],
                      tile, tile, tile],
            out_specs=[tile] * 4),
        out_shape=[_sds((rows, cols), F32)] * 4,
        compiler_params=_params(("arbitrary",)),
    )(ids, g, got, *got3, w, m, v)


SMALL_NAMES = ["g_mix_norm", "b_in", "sinks", "conv_b", "ln_g", "ln_b", "b_conv_proj", "g_ffn_norm", "g_final"]
_PACK_ROWS = 32


def _small_pack(parts):
    C = CONV_CHANNELS
    part_list = [parts["g_mix_norm"], parts["b_in"], parts["sinks"], parts["conv_b"], parts["ln_g"], parts["ln_b"],
                 parts["b_conv_proj"], parts["g_ffn_norm"], parts["g_final"], parts["loss"], parts["conv_w"]]

    def body(p_mix, p_b, p_sink, p_cb, p_lg, p_lb, p_bcp, p_ffn, p_fin, p_loss, p_cw, pack):
        pack[...] = jnp.zeros_like(pack)
        pack[0:1, :] = p_mix[...]
        pack[1:2, 0:GLU_OFF] = p_b[:, 0:GLU_OFF]
        pack[2:3, :] = p_b[:, GLU_OFF:GATE_OFF]
        pack[3:4, :] = p_b[:, GATE_OFF:GATE_OFF + D_MODEL]
        pack[4:5, :] = p_b[:, GATE_OFF + D_MODEL:]
        pack[5:6, 0:128] = p_sink[...]
        pack[6:7, 0:C] = p_cb[...]
        pack[6:7, C:2 * C] = p_lg[...]
        pack[7:8, 0:C] = p_lb[...]
        pack[8:9, :] = p_bcp[...]
        pack[9:10, :] = p_ffn[...]
        pack[10:11, :] = p_fin[...]
        pack[11:12, 0:128] = jnp.broadcast_to(p_loss[...], (1, 128))
        pack[12:28, 0:C] = p_cw[0:16, :]
        pack[12:28, C:2 * C] = p_cw[16:32, :]

    vm = pl.BlockSpec(memory_space=pltpu.VMEM)
    return pl.pallas_call(body, name="small_pack", in_specs=[vm] * len(part_list), out_specs=vm,
                          out_shape=_sds((_PACK_ROWS, D_MODEL), F32))(*part_list)


def _small_adamw(gathered, small_w, small_m, small_v):
    C = CONV_CHANNELS
    names = SMALL_NAMES
    widths = [small_w[k].shape[1] for k in names]
    n_small = len(names)

    def body(*refs):
        tot_ref = refs[0]
        w_refs = refs[1:1 + n_small]
        m_refs = refs[1 + n_small:1 + 2 * n_small]
        v_refs = refs[1 + 2 * n_small:1 + 3 * n_small]
        o = 1 + 3 * n_small
        loss_ref, cw_ref = refs[o], refs[o + 1]
        out_refs = refs[o + 2:o + 2 + 4 * n_small]
        tot = tot_ref[0:_PACK_ROWS, :]
        for d in range(1, N_DEV):
            tot = tot + tot_ref[d * _PACK_ROWS:(d + 1) * _PACK_ROWS, :]
        loss_ref[...] = tot[11:12, 0:1]
        cw_ref[0:16, :] = tot[12:28, 0:C]
        cw_ref[16:32, :] = tot[12:28, C:2 * C]
        grads = dict(
            g_mix_norm=tot[0:1, :],
            b_in=jnp.concatenate([tot[1:2, 0:GLU_OFF], tot[2:3, :], tot[3:4, :], tot[4:5, :]], axis=1),
            sinks=tot[5:6, 0:N_Q_HEADS], conv_b=tot[6:7, 0:C], ln_g=tot[6:7, C:2 * C], ln_b=tot[7:8, 0:C],
            b_conv_proj=tot[8:9, :], g_ffn_norm=tot[9:10, :], g_final=tot[10:11, :])
        for s, k in enumerate(names):
            g = grads[k]
            d, nm, nv = _adam_math(w_refs[s][...], g, m_refs[s][...], v_refs[s][...])
            out_refs[4 * s][...] = g
            out_refs[4 * s + 1][...] = d
            out_refs[4 * s + 2][...] = nm
            out_refs[4 * s + 3][...] = nv

    vm = pl.BlockSpec(memory_space=pltpu.VMEM)
    args = [gathered, *[small_w[k] for k in names], *[small_m[k] for k in names], *[small_v[k] for k in names]]
    out_shape = [_sds((1, 1), F32), _sds((CONV_PAD, C), F32)]
    for wd in widths:
        out_shape += [_sds((1, wd), F32)] * 4
    res = pl.pallas_call(
        body, name="small_adamw",
        in_specs=[vm] * len(args), out_specs=[vm] * len(out_shape), out_shape=out_shape,
        compiler_params=pltpu.CompilerParams(vmem_limit_bytes=VMEM_LIMIT_BYTES),
    )(*args)
    return res[0], res[1], {k: res[2 + 4 * s:6 + 4 * s] for s, k in enumerate(names)}


BIG = dict(w_in=True, w_attn_proj=True, w_conv_proj=True, w_out=False, w_ffn_in=True, w_ffn_down=False)
WEIGHT_NAMES = ["g_mix_norm", "w_in", "b_in", "sinks", "conv_w", "conv_b", "ln_g", "ln_b", "w_attn_proj",
                "w_conv_proj", "b_conv_proj", "w_out", "g_ffn_norm", "w_ffn_in", "w_ffn_down", "g_final"]


class _Plan:
    GROUPS = dict(down=["w_ffn_down"], ffn=["w_ffn_in"], mix=["w_out", "w_attn_proj", "w_conv_proj"], inp=["w_in"])
    ALL = (0, 1, 2)
    RIDES = dict(
        gather_mix=[("gather", ["w_attn_proj", "w_conv_proj", "w_out"])], gather_ffn=[("gather", ["w_ffn_in"])],
        gather_down=[("gather", ["w_ffn_down"])],
        ffn_in_bwd=[("swap", "down")], send_down=[("send", "down", ALL)],
        out_proj_bwd_merge=[("swap", "ffn")], send_ffn=[("send", "ffn", ALL)],
        conv_bwd=[("swap", "mix")], send_mix=[("send", "mix", ALL)],
        swap_inp=[("swap", "inp")], send_inp=[("send", "inp", ALL)])
    ASYNC = dict(gather_mix=1, gather_ffn=2, gather_down=3, send_down=4, send_ffn=5, send_mix=6, send_inp=7)

    def __init__(self, shards, c1):
        self.shards, self.c1 = shards, c1
        self.full, self.slots, self.got, self.sums, self.got3 = {}, {}, {}, {}, {}

    def weight(self, name):
        return self.full[name]

    def grad_ready(self, grads):
        for k, g in grads.items():
            self.slots[k] = g.reshape(N_DEV, g.shape[0] // N_DEV, g.shape[1])

    def _one(self, kind, what, ks=None):
        if kind == "gather":
            return _gather_carry([self.shards[k] for k in what])
        names = self.GROUPS[what]
        if kind == "swap":
            return _swap_carry([self.slots[k] for k in names])
        return _send_carry([self.sums[k] for k in names], ks)

    def carry(self, call):
        return _join([self._one(*ride) for ride in self.RIDES.get(call, [])])

    def done(self, call, outs):
        outs = list(outs)
        for kind, what, *_ in self.RIDES.get(call, []):
            names = what if kind == "gather" else self.GROUPS[what]
            mine, outs = outs[:len(names)], outs[len(names):]
            if kind == "gather":
                self.full.update(zip(names, mine))
            elif kind == "send":
                for k, r in zip(names, mine):
                    self.got3.setdefault(k, []).append(r)
            else:
                self.got.update(zip(names, mine))
                self.sums.update(zip(names, _chip_sums(f"chip_sums_{what}", [self.slots[k] for k in names], mine, self.c1)))

    def alone(self, call):
        self.done(call, _run_carry(call, self.carry(call)))

    def behind(self, group, x):
        return lax.optimization_barrier((x, tuple(self.sums[k] for k in self.GROUPS[group])))[0]

    def launch(self, call, after=None):
        carry = self._one(*self.RIDES[call][0])
        if after is not None:
            carry.arrays = list(lax.optimization_barrier((tuple(carry.arrays), after))[0])
        self.done(call, _run_carry_async(call, carry, self.ASYNC[call]))


def kernel(x, g_mix_norm, w_in, b_in, sinks, conv_w, conv_b, ln_g, ln_b, w_attn_proj, w_conv_proj, b_conv_proj, w_out, g_ffn_norm, w_ffn_in, w_ffn_down, g_final, loss_target, m_g_mix_norm, m_w_in, m_b_in, m_sinks, m_conv_w, m_conv_b, m_ln_g, m_ln_b, m_w_attn_proj, m_w_conv_proj, m_b_conv_proj, m_w_out, m_g_ffn_norm, m_w_ffn_in, m_w_ffn_down, m_g_final, v_g_mix_norm, v_w_in, v_b_in, v_sinks, v_conv_w, v_conv_b, v_ln_g, v_ln_b, v_w_attn_proj, v_w_conv_proj, v_b_conv_proj, v_w_out, v_g_ffn_norm, v_w_ffn_in, v_w_ffn_down, v_g_final):
    w = dict(g_mix_norm=g_mix_norm, w_in=w_in, b_in=b_in, sinks=sinks, conv_w=conv_w, conv_b=conv_b, ln_g=ln_g,
             ln_b=ln_b, w_attn_proj=w_attn_proj, w_conv_proj=w_conv_proj, b_conv_proj=b_conv_proj, w_out=w_out,
             g_ffn_norm=g_ffn_norm, w_ffn_in=w_ffn_in, w_ffn_down=w_ffn_down, g_final=g_final)
    m = dict(g_mix_norm=m_g_mix_norm, w_in=m_w_in, b_in=m_b_in, sinks=m_sinks, conv_w=m_conv_w, conv_b=m_conv_b,
             ln_g=m_ln_g, ln_b=m_ln_b, w_attn_proj=m_w_attn_proj, w_conv_proj=m_w_conv_proj,
             b_conv_proj=m_b_conv_proj, w_out=m_w_out, g_ffn_norm=m_g_ffn_norm, w_ffn_in=m_w_ffn_in,
             w_ffn_down=m_w_ffn_down, g_final=m_g_final)
    v = dict(g_mix_norm=v_g_mix_norm, w_in=v_w_in, b_in=v_b_in, sinks=v_sinks, conv_w=v_conv_w, conv_b=v_conv_b,
             ln_g=v_ln_g, ln_b=v_ln_b, w_attn_proj=v_w_attn_proj, w_conv_proj=v_w_conv_proj,
             b_conv_proj=v_b_conv_proj, w_out=v_w_out, g_ffn_norm=v_g_ffn_norm, w_ffn_in=v_w_ffn_in,
             w_ffn_down=v_w_ffn_down, g_final=v_g_final)
    ax, ay, ac = lax.axis_index("x"), lax.axis_index("y"), lax.axis_index("c")
    me = 4 * ax + 2 * ay + ac
    chip = 2 * ax + ay

    shards = {k: (w[k][0].T if tr else w[k][0]).astype(BF) for k, tr in BIG.items()}
    cw_shard = jnp.pad(conv_w[0].T, ((0, 0), (0, 1))).reshape(16, 128)
    wi_t, cw_full, h, r1 = _first_gather([shards["w_in"], cw_shard], x[0], g_mix_norm)
    conv_full = cw_full.reshape(CONV_CHANNELS, CONV_PAD).T

    as_row = lambda a: a.reshape(1, -1)
    small_w = {k: as_row(w[k]) for k in SMALL_NAMES}
    small_m = {k: as_row(m[k]) for k in SMALL_NAMES}
    small_v = {k: as_row(v[k]) for k in SMALL_NAMES}
    plan = _Plan(shards, ac.reshape(1).astype(jnp.int32))
    plan.launch("gather_mix", after=wi_t)
    dx, parts = _local_step(x[0], h, r1, loss_target[0], small_w, wi_t, conv_full, plan)

    ids = jnp.stack([me, chip]).astype(jnp.int32)
    grads, delta, new_m, new_v, after = {}, {}, {}, {}, dx
    packed = _small_pack(parts)
    for k in sorted(BIG, key=lambda k: k == "w_in"):
        if k == "w_in":
            packed = lax.optimization_barrier((packed, after))[0]
            small_gathered, = _run_carry_async("small_gather", _gather_carry([packed]), 8)
        flip = (lambda a: a.T) if BIG[k] else (lambda a: a)
        wk = lax.optimization_barrier((w[k][0], after))[0]
        outs = _grad_adamw(f"grad_adamw_{k}", plan.slots[k], plan.got[k], plan.got3[k], ids,
                           flip(wk), flip(m[k][0]), flip(v[k][0]))
        after = outs[0]
        grads[k], delta[k], new_m[k], new_v[k] = (flip(a)[None] for a in outs)

    loss, cw_grad, small_out = _small_adamw(small_gathered, small_w, small_m, small_v)
    for k in SMALL_NAMES:
        g, d, nm, nv = (a.reshape(w[k].shape) for a in small_out[k])
        grads[k], delta[k], new_m[k], new_v[k] = g, d, nm, nv
    cw_mine = lax.dynamic_slice(cw_grad, (0, me * 64), (CONV_WIDTH, 64))
    d, nm, nv = _adamw("adamw_conv_w", conv_w[0], cw_mine, m_conv_w[0], v_conv_w[0])
    grads["conv_w"], delta["conv_w"], new_m["conv_w"], new_v["conv_w"] = cw_mine[None], d[None], nm[None], nv[None]

    return (loss.reshape(()), dx[None], *[grads[k] for k in WEIGHT_NAMES], *[delta[k] for k in WEIGHT_NAMES],
            *[new_m[k] for k in WEIGHT_NAMES], *[new_v[k] for k in WEIGHT_NAMES])
```

```python
import functools

import jax
import jax.numpy as jnp
from jax import lax
from jax.experimental import pallas as pl
from jax.experimental.pallas import tpu as pltpu
from jax.experimental.pallas import tpu_sc as plsc

F32 = jnp.float32
BF = jnp.bfloat16

SEQ = 2048
D_MODEL = 1024
HEAD_DIM = 64
N_Q_HEADS = 8
N_KV_HEADS = 2
GROUP = N_Q_HEADS // N_KV_HEADS
BLOCK = 128
ATTN_WIDTH = 512
KV_WIDTH = 128
CONV_CHANNELS = 512
CONV_WIDTH = 31
CONV_PAD = 32
GLU_OFF = 768
GATE_OFF = 1792
IN_WIDTH = 3840
D_FF = 2816
EPS = 1e-5
NEG = -1e30
N_DEV = 8

ADAM_LR = 0.001
ADAM_B1 = 0.9
ADAM_B2 = 0.999
ADAM_EPS = 1e-08
ADAM_WD = 0.01
ADAM_STEP = 10

VMEM_LIMIT_BYTES = 56 * 1024 * 1024
MESH = pl.DeviceIdType.MESH
ANY = pl.BlockSpec(memory_space=pl.ANY)

_DIMS = {"NN": (((1,), (0,)), ((), ())), "NT": (((1,), (1,)), ((), ())), "TN": (((0,), (0,)), ((), ()))}


def _params(sem):
    return pltpu.CompilerParams(dimension_semantics=sem, vmem_limit_bytes=VMEM_LIMIT_BYTES)


class _Carry:
    def __init__(self, arrays, out_shapes, sems, start, finish, peers=None):
        self.arrays, self.out_shapes, self.sems, self.start, self.finish = arrays, out_shapes, sems, start, finish
        self.peers = peers


def _carry_io(carry):
    if carry is None:
        return [], [], []
    return list(carry.arrays), list(carry.out_shapes), list(carry.sems)


def _matmul(name, a_list, b, mode, *, m, n, tm, tn, tk=None, epilogue, extra=(), outs, b_off=(0, 0), alias=None,
            scratch=(), carry=None):
    seg_k = [a.shape[0] if mode == "TN" else a.shape[1] for a in a_list]
    whole = tk is None
    seg_nk = [1] * len(a_list) if whole else [ks // tk for ks in seg_k]
    nk = 1 if whole else sum(seg_nk)
    starts = [sum(seg_nk[:s]) for s in range(len(seg_nk))]
    k_starts = [sum(seg_k[:s]) for s in range(len(seg_k))]
    k_tot = sum(seg_k)
    n_a, n_extra, n_out = len(a_list), len(extra), len(outs)

    a_specs = []
    for st, ns, ks in zip(starts, seg_nk, seg_k):
        if mode == "TN":
            a_specs.append(pl.BlockSpec((ks if whole else tk, tm), lambda j, i, k: (k, i)))
        elif whole:
            a_specs.append(pl.BlockSpec((tm, ks), lambda j, i, k: (i, 0)))
        else:
            a_specs.append(pl.BlockSpec((tm, tk), functools.partial(
                lambda j, i, k, st, ns: (i, jnp.clip(k - st, 0, ns - 1)), st=st, ns=ns)))
    bk = k_tot if whole else tk
    if mode == "NT":
        b_spec = pl.BlockSpec((tn, bk), lambda j, i, k: (b_off[0] + j, b_off[1] + k))
    else:
        b_spec = pl.BlockSpec((bk, tn), lambda j, i, k: (b_off[0] + k, b_off[1] + j))
    n_alias = 0 if alias is None else 1
    c_in, c_out, c_sems = _carry_io(carry)
    n_acc = 0 if whole else 1
    nj, ni = n // tn, m // tm

    def body(*refs):
        pos = [n_a, 1, n_alias, n_extra, len(c_in), n_out, len(c_out), n_acc, len(scratch), len(c_sems)]
        cuts = [sum(pos[:q]) for q in range(len(pos) + 1)]
        a_refs, (b_ref,), _, ex, ci_refs, out_refs, co_refs, acc_refs, scr, cs_refs = (
            refs[cuts[q]:cuts[q + 1]] for q in range(len(pos)))
        j, i, k = pl.program_id(0), pl.program_id(1), pl.program_id(2)
        ids = (j, i)
        if carry is not None:
            @pl.when((j == 0) & (i == 0) & (k == 0))
            def _():
                carry.start(ci_refs, co_refs, cs_refs)

        def dot(a_ref, bv):
            return lax.dot_general(a_ref[...].astype(BF), bv.astype(BF), _DIMS[mode], preferred_element_type=F32)

        if whole:
            tot = None
            for a_ref, k0, ks in zip(a_refs, k_starts, seg_k):
                if n_a == 1:
                    bv = b_ref[...]
                else:
                    bv = b_ref[:, k0:k0 + ks] if mode == "NT" else b_ref[k0:k0 + ks, :]
                part = dot(a_ref, bv)
                tot = part if tot is None else tot + part
            epilogue(tot, ex, out_refs, ids, scr)
        else:
            acc, = acc_refs

            @pl.when(k == 0)
            def _():
                acc[...] = jnp.zeros_like(acc)

            for a_ref, st, ns in zip(a_refs, starts, seg_nk):
                if n_a == 1:
                    acc[...] += dot(a_ref, b_ref[...])
                else:
                    @pl.when((k >= st) & (k < st + ns))
                    def _(a_ref=a_ref):
                        acc[...] += dot(a_ref, b_ref[...])

            @pl.when(k == nk - 1)
            def _():
                epilogue(acc[...], ex, out_refs, ids, scr)

        if carry is not None:
            @pl.when((j == nj - 1) & (i == ni - 1) & (k == nk - 1))
            def _():
                carry.finish(ci_refs, co_refs, cs_refs)

    in_specs = [*a_specs, b_spec]
    args = [*a_list, b]
    io_alias = {}
    if alias is not None:
        in_specs.append(pl.BlockSpec(memory_space=pl.ANY))
        args.append(alias[0])
        io_alias = {n_a + 1: alias[1]}
    in_specs += [s for _, s in extra] + [pl.BlockSpec(memory_space=pl.ANY)] * len(c_in)
    args += [x for x, _ in extra] + c_in
    res = pl.pallas_call(
        body, name=name, grid=(nj, ni, nk), in_specs=in_specs,
        out_specs=[s for _, s in outs] + [pl.BlockSpec(memory_space=pl.ANY)] * len(c_out),
        out_shape=[o for o, _ in outs] + c_out,
        scratch_shapes=[*([] if whole else [pltpu.VMEM((tm, tn), F32)]), *scratch, *c_sems],
        input_output_aliases=io_alias,
        compiler_params=_params(("arbitrary", "arbitrary", "arbitrary")),
    )(*args)
    return res if carry is None else (res[:n_out], res[n_out:])


def _tile(tm, tn):
    return pl.BlockSpec((tm, tn), lambda j, i, k: (i, j))


def _row(tn):
    return pl.BlockSpec((1, tn), lambda j, i, k: (0, j))


def _store(dtype):
    def ep(acc, ex, outs, ids, scr):
        outs[0][...] = acc.astype(dtype)
    return ep


def _sds(shape, dtype):
    return jax.ShapeDtypeStruct(shape, dtype)


def _rms_bwd(dh, xv, r, g):
    xh = xv * r
    dxh = dh * g
    dx = r * (dxh - xh * jnp.mean(dxh * xh, axis=-1, keepdims=True))
    return dx, jnp.sum(dh * xh, axis=0, keepdims=True)


def _accumulate_rows(ref, val, first):
    @pl.when(first)
    def _():
        ref[...] = val

    @pl.when(jnp.logical_not(first))
    def _():
        ref[...] += val


def _loss_head(xv, g, target):
    r = lax.rsqrt(jnp.mean(xv * xv, axis=-1, keepdims=True) + EPS)
    err = xv * r * g - target
    dx, dg = _rms_bwd(err * (1.0 / xv.shape[-1]), xv, r, g)
    part = 0.5 * jnp.sum(jnp.mean(err * err, axis=-1, keepdims=True), axis=0, keepdims=True)
    return dx, dg, part


def _lane_half(shape, h):
    lane = lax.broadcasted_iota(jnp.int32, shape, 1)
    return (lane >= HEAD_DIM * h) & (lane < HEAD_DIM * (h + 1))


def _to_half(v, w, h):
    if w != h:
        v = pltpu.roll(v, HEAD_DIM, 1)
    return jnp.where(_lane_half(v.shape, h), v, 0.0)


def _attn_block(qkv_ref, sinks_ref, n, h):
    r0 = pl.multiple_of(n * BLOCK, BLOCK)
    p0 = pl.multiple_of(jnp.maximum(n - 1, 0) * BLOCK, BLOCK)
    rows = pl.ds(r0, BLOCK)
    prev = pl.ds(p0, BLOCK)
    k2 = jnp.concatenate([qkv_ref[prev, ATTN_WIDTH:ATTN_WIDTH + KV_WIDTH],
                          qkv_ref[rows, ATTN_WIDTH:ATTN_WIDTH + KV_WIDTH]], axis=0)
    v2 = jnp.concatenate([qkv_ref[prev, ATTN_WIDTH + KV_WIDTH:ATTN_WIDTH + 2 * KV_WIDTH],
                          qkv_ref[rows, ATTN_WIDTH + KV_WIDTH:ATTN_WIDTH + 2 * KV_WIDTH]], axis=0)
    qs = []
    for g in range(GROUP):
        hq = GROUP * h + g
        blk = qkv_ref[rows, (hq // 2) * 128:(hq // 2 + 1) * 128].astype(F32)
        qs.append(_to_half(blk, hq % 2, h))
    q4 = jnp.concatenate(qs, axis=0).astype(BF)
    s = lax.dot_general(q4, k2, _DIMS["NT"], preferred_element_type=F32) * (HEAD_DIM ** -0.5)
    shape = s.shape
    row = lax.broadcasted_iota(jnp.int32, shape, 0)
    qi = row & (BLOCK - 1)
    kj = lax.broadcasted_iota(jnp.int32, shape, 1)
    diff = qi + BLOCK - kj
    valid = (diff >= 0) & (diff < BLOCK) & ((kj >= BLOCK) | (n > 0))
    s = jnp.where(valid, s, NEG)
    row1 = lax.broadcasted_iota(jnp.int32, (shape[0], 1), 0)
    sink = jnp.zeros((shape[0], 1), F32)
    for g in range(GROUP):
        sink = jnp.where((row1 >= g * BLOCK) & (row1 < (g + 1) * BLOCK), sinks_ref[0, GROUP * h + g], sink)
    m = jnp.maximum(jnp.max(s, axis=-1, keepdims=True), sink)
    e = jnp.exp(s - m)
    es = jnp.exp(sink - m)
    inv = 1.0 / (jnp.sum(e, axis=-1, keepdims=True) + es)
    return e * inv, es * inv, q4, k2, v2, rows, prev


def _attn_fwd(proj, sinks, carry=None):
    T = proj.shape[0]
    c_in, c_out, c_sems = _carry_io(carry)

    def body(*refs):
        qkv_ref, sinks_ref = refs[:2]
        ci_refs = refs[2:2 + len(c_in)]
        o_ref = refs[2 + len(c_in)]
        co_refs = refs[3 + len(c_in):3 + len(c_in) + len(c_out)]
        cs_refs = refs[3 + len(c_in) + len(c_out):]
        if carry is not None:
            carry.start(ci_refs, co_refs, cs_refs)

        def blk(n, z):
            outs = [None] * (N_Q_HEADS // 2)
            for h in range(N_KV_HEADS):
                p, _, _, _, v2, rows, _ = _attn_block(qkv_ref, sinks_ref, n, h)
                o = lax.dot_general(p.astype(BF), v2, _DIMS["NN"], preferred_element_type=F32)
                for g in range(GROUP):
                    hq = GROUP * h + g
                    piece = jnp.where(_lane_half((BLOCK, 128), h), o[g * BLOCK:(g + 1) * BLOCK], 0.0)
                    if hq % 2 != h:
                        piece = pltpu.roll(piece, HEAD_DIM, 1)
                    outs[hq // 2] = piece if outs[hq // 2] is None else outs[hq // 2] + piece
            for pb in range(N_Q_HEADS // 2):
                o_ref[rows, pb * 128:(pb + 1) * 128] = outs[pb].astype(BF)
            return z

        lax.fori_loop(0, T // BLOCK, blk, 0)
        if carry is not None:
            carry.finish(ci_refs, co_refs, cs_refs)

    res = pl.pallas_call(
        body, name="attn_fwd", grid=(1,),
        in_specs=[pl.BlockSpec((T, GLU_OFF), lambda i: (0, 0)), pl.BlockSpec(memory_space=pltpu.SMEM),
                  *[ANY] * len(c_in)],
        out_specs=[pl.BlockSpec((T, ATTN_WIDTH), lambda i: (0, 0)), *[ANY] * len(c_out)],
        out_shape=[_sds((T, ATTN_WIDTH), BF), *c_out], scratch_shapes=c_sems,
        compiler_params=_params(("arbitrary",)),
    )(proj, sinks, *c_in)
    return res[0], res[1:]


def _attn_bwd(proj, d_o, sinks, carry=None):
    T = proj.shape[0]
    c_in, c_out, c_sems = _carry_io(carry)

    def body(*refs):
        qkv_ref, do_ref, sinks_ref = refs[:3]
        ci_refs = refs[3:3 + len(c_in)]
        dqkv_ref, dsink_ref = refs[3 + len(c_in):5 + len(c_in)]
        co_refs = refs[5 + len(c_in):5 + len(c_in) + len(c_out)]
        dk_acc, dv_acc = refs[5 + len(c_in) + len(c_out):7 + len(c_in) + len(c_out)]
        cs_refs = refs[7 + len(c_in) + len(c_out):]
        if carry is not None:
            carry.start(ci_refs, co_refs, cs_refs)
        dsink_ref[...] = jnp.zeros_like(dsink_ref)
        dk_acc[...] = jnp.zeros_like(dk_acc)
        dv_acc[...] = jnp.zeros_like(dv_acc)

        def blk(n, carry):
            dqs = [None] * (N_Q_HEADS // 2)
            for h in range(N_KV_HEADS):
                p, psink, q4, k2, v2, rows, prev = _attn_block(qkv_ref, sinks_ref, n, h)
                dos = []
                for g in range(GROUP):
                    hq = GROUP * h + g
                    dos.append(_to_half(do_ref[rows, (hq // 2) * 128:(hq // 2 + 1) * 128].astype(F32), hq % 2, h))
                do4 = jnp.concatenate(dos, axis=0).astype(BF)
                dp = lax.dot_general(do4, v2, _DIMS["NT"], preferred_element_type=F32)
                delta = jnp.sum(p * dp, axis=-1, keepdims=True)
                ds = (p * (dp - delta) * (HEAD_DIM ** -0.5)).astype(BF)
                dsk = psink * delta
                for g in range(GROUP):
                    hq = GROUP * h + g
                    tot = -jnp.sum(dsk[g * BLOCK:(g + 1) * BLOCK], axis=0, keepdims=True)
                    lane = lax.broadcasted_iota(jnp.int32, (1, 128), 1)
                    dsink_ref[...] += jnp.where(lane == hq, tot, 0.0)
                dq = lax.dot_general(ds, k2, _DIMS["NN"], preferred_element_type=F32)
                dk = lax.dot_general(ds, q4, _DIMS["TN"], preferred_element_type=F32)
                dv = lax.dot_general(p.astype(BF), do4, _DIMS["TN"], preferred_element_type=F32)
                dk_acc[prev, :] += dk[:BLOCK]
                dk_acc[rows, :] += dk[BLOCK:]
                dv_acc[prev, :] += dv[:BLOCK]
                dv_acc[rows, :] += dv[BLOCK:]
                for g in range(GROUP):
                    hq = GROUP * h + g
                    piece = jnp.where(_lane_half((BLOCK, 128), h), dq[g * BLOCK:(g + 1) * BLOCK], 0.0)
                    if hq % 2 != h:
                        piece = pltpu.roll(piece, HEAD_DIM, 1)
                    dqs[hq // 2] = piece if dqs[hq // 2] is None else dqs[hq // 2] + piece
            for pb in range(N_Q_HEADS // 2):
                dqkv_ref[rows, pb * 128:(pb + 1) * 128] = dqs[pb].astype(BF)
            return carry

        lax.fori_loop(0, T // BLOCK, blk, 0)
        dqkv_ref[:, ATTN_WIDTH:ATTN_WIDTH + KV_WIDTH] = dk_acc[...].astype(BF)
        dqkv_ref[:, ATTN_WIDTH + KV_WIDTH:] = dv_acc[...].astype(BF)
        if carry is not None:
            carry.finish(ci_refs, co_refs, cs_refs)

    res = pl.pallas_call(
        body, name="attn_bwd", grid=(1,),
        in_specs=[pl.BlockSpec((T, GLU_OFF), lambda i: (0, 0)), pl.BlockSpec((T, ATTN_WIDTH), lambda i: (0, 0)),
                  pl.BlockSpec(memory_space=pltpu.SMEM), *[ANY] * len(c_in)],
        out_specs=[pl.BlockSpec((T, GLU_OFF), lambda i: (0, 0)), pl.BlockSpec((1, 128), lambda i: (0, 0)),
                   *[ANY] * len(c_out)],
        out_shape=[_sds((T, GLU_OFF), BF), _sds((1, 128), F32), *c_out],
        scratch_shapes=[pltpu.VMEM((T, KV_WIDTH), F32), pltpu.VMEM((T, KV_WIDTH), F32), *c_sems],
        compiler_params=_params(("arbitrary",)),
    )(proj, d_o, sinks, *c_in)
    return res[:2], res[2:]


CHUNK = 256
SUB = 32
WIN = CHUNK + 32
PAD_ROWS = SEQ + 2 * CONV_PAD
_GLU_SPECS = [pl.BlockSpec((SEQ, 256), functools.partial(lambda i, c: (0, c), c=GLU_OFF // 256 + c)) for c in range(4)]


def _glu_to_pad(a0, a1, b0, b1, zpad):
    C = CONV_CHANNELS
    zpad[0:CONV_PAD, :] = jnp.zeros((CONV_PAD, C), F32)
    zpad[CONV_PAD + SEQ:, :] = jnp.zeros((CONV_PAD, C), F32)
    zpad[CONV_PAD:CONV_PAD + SEQ, 0:256] = a0[...].astype(F32) * jax.nn.sigmoid(b0[...].astype(F32))
    zpad[CONV_PAD:CONV_PAD + SEQ, 256:C] = a1[...].astype(F32) * jax.nn.sigmoid(b1[...].astype(F32))


def _tap_windows(src, base, win):
    for b in range(8):
        win[b, 0:WIN - 8, :] = src[base + b:base + b + WIN - 8, :]


def _taps(win, w_ref, init, out, flip):
    def sub(si, carry):
        r0 = pl.multiple_of(si * SUB, SUB)
        acc = jnp.broadcast_to(init, (SUB, CONV_CHANNELS))
        for k in range(CONV_WIDTH):
            wk = (CONV_WIDTH - 1 - k) if flip else k
            acc = acc + w_ref[wk:wk + 1, :] * win[k % 8, pl.ds(r0 + 8 * (k // 8), SUB), :]
        out[pl.ds(r0, SUB), :] = acc
        return carry

    lax.fori_loop(0, CHUNK // SUB, sub, 0)


def _tap_grads(win, du, dwacc):
    def sub(si, carry):
        r0 = pl.multiple_of(si * SUB, SUB)
        d = du[pl.ds(r0, SUB), :]
        for k in range(CONV_WIDTH):
            p = d * win[k % 8, pl.ds(r0 + 8 * (k // 8), SUB), :]
            dwacc[8 * k:8 * k + 8, :] += (p[0:8] + p[8:16]) + (p[16:24] + p[24:32])
        return carry

    lax.fori_loop(0, CHUNK // SUB, sub, 0)


def _ln_parts(u):
    mu = jnp.mean(u, axis=-1, keepdims=True)
    xc = u - mu
    rstd = lax.rsqrt(jnp.mean(xc * xc, axis=-1, keepdims=True) + EPS)
    return xc * rstd, rstd


def _conv_fwd(proj, conv_w, conv_b, ln_g, ln_b, carry=None):
    T, C = proj.shape[0], CONV_CHANNELS
    vec = pl.BlockSpec((1, C), lambda i: (0, 0))
    c_in, c_out, c_sems = _carry_io(carry)

    def body(*refs):
        a0, a1, b0, b1, w_ref, cb_ref, g_ref, be_ref = refs[:8]
        ci_refs = refs[8:8 + len(c_in)]
        c_ref, u_ref = refs[8 + len(c_in):10 + len(c_in)]
        co_refs = refs[10 + len(c_in):10 + len(c_in) + len(c_out)]
        zpad, win, ubuf = refs[10 + len(c_in) + len(c_out):13 + len(c_in) + len(c_out)]
        cs_refs = refs[13 + len(c_in) + len(c_out):]
        if carry is not None:
            carry.start(ci_refs, co_refs, cs_refs)
        _glu_to_pad(a0, a1, b0, b1, zpad)
        for ci in range(T // CHUNK):
            _tap_windows(zpad, ci * CHUNK + CONV_PAD - (CONV_WIDTH - 1), win)
            _taps(win, w_ref, cb_ref[...], ubuf, False)
            u = ubuf[...]
            u_ref[ci * CHUNK:(ci + 1) * CHUNK, :] = u
            xh, _ = _ln_parts(u)
            ln = xh * g_ref[...] + be_ref[...]
            c_ref[ci * CHUNK:(ci + 1) * CHUNK, :] = (ln * jax.nn.sigmoid(ln)).astype(BF)
        if carry is not None:
            carry.finish(ci_refs, co_refs, cs_refs)

    res = pl.pallas_call(
        body, name="conv_fwd", grid=(1,),
        in_specs=[*_GLU_SPECS, pl.BlockSpec((CONV_PAD, C), lambda i: (0, 0)), vec, vec, vec, *[ANY] * len(c_in)],
        out_specs=[pl.BlockSpec((T, C), lambda i: (0, 0)), pl.BlockSpec((T, C), lambda i: (0, 0)), *[ANY] * len(c_out)],
        out_shape=[_sds((T, C), BF), _sds((T, C), F32), *c_out],
        scratch_shapes=[pltpu.VMEM((PAD_ROWS, C), F32), pltpu.VMEM((8, WIN, C), F32), pltpu.VMEM((CHUNK, C), F32),
                        *c_sems],
        compiler_params=_params(("arbitrary",)),
    )(proj, proj, proj, proj, conv_w, conv_b, ln_g, ln_b, *c_in)
    return res[:2], res[2:]


def _conv_bwd(proj, u, d_c, conv_w, conv_b, ln_g, ln_b, carry=None):
    T, C = proj.shape[0], CONV_CHANNELS
    vec = pl.BlockSpec((1, C), lambda i: (0, 0))
    wspec = pl.BlockSpec((CONV_PAD, C), lambda i: (0, 0))
    c_in, c_out, c_sems = _carry_io(carry)

    def body(*refs):
        a0, a1, b0, b1, u_ref, dc_ref, w_ref, cb_ref, g_ref, be_ref = refs[:10]
        ci_refs = refs[10:10 + len(c_in)]
        o = 10 + len(c_in)
        dglu_ref, dw_ref, dcb_ref, dg_ref, dbe_ref = refs[o:o + 5]
        co_refs = refs[o + 5:o + 5 + len(c_out)]
        zpad, dupad, win, ubuf, dwacc = refs[o + 5 + len(c_out):o + 10 + len(c_out)]
        cs_refs = refs[o + 10 + len(c_out):]
        if carry is not None:
            carry.start(ci_refs, co_refs, cs_refs)
        _glu_to_pad(a0, a1, b0, b1, zpad)
        dupad[T:, :] = jnp.zeros((2 * CONV_PAD, C), F32)
        dwacc[...] = jnp.zeros_like(dwacc)
        dcb_ref[...] = jnp.zeros_like(dcb_ref)
        dg_ref[...] = jnp.zeros_like(dg_ref)
        dbe_ref[...] = jnp.zeros_like(dbe_ref)
        for ci in range(T // CHUNK):
            rows = slice(ci * CHUNK, (ci + 1) * CHUNK)
            _tap_windows(zpad, ci * CHUNK + CONV_PAD - (CONV_WIDTH - 1), win)
            xh, rstd = _ln_parts(u_ref[rows, :])
            ln = xh * g_ref[...] + be_ref[...]
            sg = jax.nn.sigmoid(ln)
            dln = dc_ref[rows, :].astype(F32) * (sg * (1.0 + ln * (1.0 - sg)))
            dg_ref[...] += jnp.sum(dln * xh, axis=0, keepdims=True)
            dbe_ref[...] += jnp.sum(dln, axis=0, keepdims=True)
            dxh = dln * g_ref[...]
            du = rstd * (dxh - jnp.mean(dxh, axis=-1, keepdims=True)
                         - xh * jnp.mean(dxh * xh, axis=-1, keepdims=True))
            dupad[rows, :] = du
            dcb_ref[...] += jnp.sum(du, axis=0, keepdims=True)
            _tap_grads(win, dupad.at[rows, :], dwacc)
        for k in range(CONV_WIDTH):
            dw_ref[k:k + 1, :] = jnp.sum(dwacc[8 * k:8 * k + 8, :], axis=0, keepdims=True)
        dw_ref[CONV_WIDTH:, :] = jnp.zeros((CONV_PAD - CONV_WIDTH, C), F32)
        for ci in range(T // CHUNK):
            rows = slice(ci * CHUNK, (ci + 1) * CHUNK)
            _tap_windows(dupad, ci * CHUNK, win)
            _taps(win, w_ref, jnp.zeros((1, C), F32), ubuf, True)
            dz = ubuf[...]
            for half, (a, b) in enumerate(((a0, b0), (a1, b1))):
                sb = jax.nn.sigmoid(b[rows, :].astype(F32))
                dzh = dz[:, half * 256:(half + 1) * 256]
                dglu_ref[rows, half * 256:(half + 1) * 256] = (dzh * sb).astype(BF)
                dglu_ref[rows, C + half * 256:C + (half + 1) * 256] = (
                    dzh * a[rows, :].astype(F32) * sb * (1.0 - sb)).astype(BF)
        if carry is not None:
            carry.finish(ci_refs, co_refs, cs_refs)

    res = pl.pallas_call(
        body, name="conv_bwd", grid=(1,),
        in_specs=[*_GLU_SPECS, pl.BlockSpec((T, C), lambda i: (0, 0)), pl.BlockSpec((T, C), lambda i: (0, 0)), wspec,
                  vec, vec, vec, *[ANY] * len(c_in)],
        out_specs=[pl.BlockSpec((T, 2 * C), lambda i: (0, 0)), wspec, vec, vec, vec, *[ANY] * len(c_out)],
        out_shape=[_sds((T, 2 * C), BF), _sds((CONV_PAD, C), F32), _sds((1, C), F32), _sds((1, C), F32),
                   _sds((1, C), F32), *c_out],
        scratch_shapes=[pltpu.VMEM((PAD_ROWS, C), F32), pltpu.VMEM((PAD_ROWS, C), F32), pltpu.VMEM((8, WIN, C), F32),
                        pltpu.VMEM((CHUNK, C), F32), pltpu.VMEM((8 * CONV_PAD, C), F32), *c_sems],
        compiler_params=_params(("arbitrary",)),
    )(proj, proj, proj, proj, u, d_c, conv_w, conv_b, ln_g, ln_b, *c_in)
    return res[:5], res[5:]


_GATE_BLK = GATE_OFF // 256


def _ffn_in_swiglu(h2, wf_t, carry=None):
    T, D = h2.shape
    tm, tn = 1024, D_FF // 2
    nj, ni = D_FF // tn, T // tm
    c_in, c_out, c_sems = _carry_io(carry)

    def body(*refs):
        a_ref, bg_ref, bu_ref = refs[:3]
        ci_refs = refs[3:3 + len(c_in)]
        act_ref, g_ref, u_ref = refs[3 + len(c_in):6 + len(c_in)]
        co_refs = refs[6 + len(c_in):6 + len(c_in) + len(c_out)]
        cs_refs = refs[6 + len(c_in) + len(c_out):]
        j, i = pl.program_id(0), pl.program_id(1)
        if carry is not None:
            @pl.when((j == 0) & (i == 0))
            def _():
                carry.start(ci_refs, co_refs, cs_refs)
        a = a_ref[...]
        for c0, c1 in ((0, 768), (768, tn)):
            g = lax.dot_general(a, bg_ref[c0:c1, :], _DIMS["NT"], preferred_element_type=F32)
            u = lax.dot_general(a, bu_ref[c0:c1, :], _DIMS["NT"], preferred_element_type=F32)
            act_ref[:, c0:c1] = (g * jax.nn.sigmoid(g) * u).astype(BF)
            g_ref[:, c0:c1] = g.astype(BF)
            u_ref[:, c0:c1] = u.astype(BF)
        if carry is not None:
            @pl.when((j == nj - 1) & (i == ni - 1))
            def _():
                carry.finish(ci_refs, co_refs, cs_refs)

    t = pl.BlockSpec((tm, tn), lambda j, i: (i, j))
    res = pl.pallas_call(
        body, name="ffn_in_swiglu", grid=(nj, ni),
        in_specs=[pl.BlockSpec((tm, D), lambda j, i: (i, 0)), pl.BlockSpec((tn, D), lambda j, i: (j, 0)),
                  pl.BlockSpec((tn, D), lambda j, i: (nj + j, 0)), *[ANY] * len(c_in)],
        out_specs=[t, t, t, *[ANY] * len(c_out)], out_shape=[*[_sds((T, D_FF), BF)] * 3, *c_out],
        scratch_shapes=c_sems,
        compiler_params=_params(("arbitrary", "arbitrary")),
    )(h2, wf_t, wf_t, *c_in)
    return res[:3], res[3:]


def _proj_merge(o, c, wap_t, wcp_t, b_cp, proj):
    T, D = o.shape[0], wap_t.shape[0]
    tm, tg = T, 256
    nj = D // tg

    def body(o_ref, c_ref, wa_ref, wc_ref, b_ref, g0_ref, g1_ref, ya_ref, yc_ref, m_ref):
        ya = lax.dot_general(o_ref[...], wa_ref[...], _DIMS["NT"], preferred_element_type=F32)
        yc = lax.dot_general(c_ref[...], wc_ref[...], _DIMS["NT"], preferred_element_type=F32) + b_ref[...]
        ya_ref[...] = ya.astype(BF)
        yc_ref[...] = yc.astype(BF)
        m_ref[...] = (jax.nn.sigmoid(g0_ref[...].astype(F32)) * ya + jax.nn.sigmoid(g1_ref[...].astype(F32)) * yc).astype(BF)

    act = pl.BlockSpec((tm, o.shape[1]), lambda j, i: (i, 0))
    wgt = pl.BlockSpec((tg, o.shape[1]), lambda j, i: (j, 0))
    t = pl.BlockSpec((tm, tg), lambda j, i: (i, j))
    return pl.pallas_call(
        body, name="proj_merge", grid=(nj, T // tm),
        in_specs=[act, act, wgt, wgt, pl.BlockSpec((1, tg), lambda j, i: (0, j)),
                  pl.BlockSpec((tm, tg), lambda j, i: (i, _GATE_BLK + j)),
                  pl.BlockSpec((tm, tg), lambda j, i: (i, _GATE_BLK + nj + j))],
        out_specs=[t, t, t], out_shape=[_sds((T, D), BF)] * 3,
        compiler_params=_params(("arbitrary", "arbitrary")),
    )(o, c, wap_t, wcp_t, b_cp, proj, proj)


def _stacked_dw(name, segs, h, tb):
    T, D = h.shape
    nblk = [seg.shape[1] // tb for seg in segs]
    starts = [sum(nblk[:q]) for q in range(len(segs))]
    n_seg = len(segs)

    def body(*refs):
        seg_refs, h_ref, o_ref, cs_ref = refs[:n_seg], refs[n_seg], refs[n_seg + 1], refs[n_seg + 2]
        i = pl.program_id(0)
        for seg_ref, st, nb in zip(seg_refs, starts, nblk):
            @pl.when((i >= st) & (i < st + nb))
            def _(seg_ref=seg_ref):
                a = seg_ref[...]
                o_ref[...] = lax.dot_general(a, h_ref[...], _DIMS["TN"], preferred_element_type=F32).astype(BF)
                cs_ref[...] = jnp.sum(a.astype(F32), axis=0, keepdims=True)

    in_specs = [pl.BlockSpec((T, tb), functools.partial(lambda i, st, nb: (0, jnp.clip(i - st, 0, nb - 1)), st=st, nb=nb))
                for st, nb in zip(starts, nblk)]
    return pl.pallas_call(
        body, name=name, grid=(sum(nblk),),
        in_specs=[*in_specs, pl.BlockSpec((T, D), lambda i: (0, 0))],
        out_specs=[pl.BlockSpec((tb, D), lambda i: (i, 0)), pl.BlockSpec((1, tb), lambda i: (0, i))],
        out_shape=[_sds((sum(nblk) * tb, D), BF), _sds((1, sum(nblk) * tb), F32)],
        compiler_params=_params(("arbitrary",)),
    )(*segs, h)


def _local_step(x, h, r1, target, small, wi_t, conv_w, plan):
    T, D = x.shape
    tm = 1024

    def carried(call, res, carry):
        if carry is None:
            return res
        outs, got = res
        plan.done(call, got)
        return outs


    def ep_add(acc, ex, outs, ids, scr):
        outs[0][...] = acc + ex[0][...]

    tn_in = IN_WIDTH // 3
    carry = plan.carry("proj_in")
    def ep_bias_bf16(acc, ex, outs, ids, scr):
        outs[0][...] = (acc + ex[0][...]).astype(BF)

    proj, = carried("proj_in", _matmul("proj_in", [h], wi_t, "NT", m=T, n=IN_WIDTH, tm=T, tn=tn_in,
                                       epilogue=ep_bias_bf16, extra=[(small["b_in"], _row(tn_in))],
                                       outs=[(_sds((T, IN_WIDTH), BF), _tile(T, tn_in))], carry=carry), carry)
    plan.launch("gather_ffn", after=proj)
    o, got = _attn_fwd(proj, small["sinks"], carry=plan.carry("attn_fwd"))
    plan.done("attn_fwd", got)
    (c, u_conv), got = _conv_fwd(proj, conv_w, small["conv_b"], small["ln_g"], small["ln_b"],
                                 carry=plan.carry("conv_fwd"))
    plan.done("conv_fwd", got)
    wap_t, wcp_t, w_out = plan.weight("w_attn_proj"), plan.weight("w_conv_proj"), plan.weight("w_out")
    ya, yc, merged = _proj_merge(o, c, wap_t, wcp_t, small["b_conv_proj"], proj)

    tg = 256
    gate_specs = [pl.BlockSpec((T, tg), lambda j, i, k: (i, _GATE_BLK + j)),
                  pl.BlockSpec((T, tg), lambda j, i, k: (i, _GATE_BLK + D // tg + j))]

    def ep_residual_rms(acc, ex, outs, ids, scr):
        x2v = acc + ex[0][...]
        r = lax.rsqrt(jnp.mean(x2v * x2v, axis=-1, keepdims=True) + EPS)
        outs[0][...] = x2v
        outs[1][...] = (x2v * r * ex[1][...]).astype(BF)
        outs[2][...] = r

    carry = plan.carry("out_proj")
    x2, h2, r2 = carried("out_proj", _matmul(
        "out_proj_rms", [merged], w_out, "NN", m=T, n=D, tm=512, tn=D, epilogue=ep_residual_rms,
        extra=[(x, _tile(512, D)), (small["g_ffn_norm"], _row(D))],
        outs=[(_sds((T, D), F32), _tile(512, D)), (_sds((T, D), BF), _tile(512, D)),
              (_sds((T, 1), F32), pl.BlockSpec((512, 1), lambda j, i, k: (i, 0)))], carry=carry), carry)
    plan.launch("gather_down", after=x2)
    wf_t = plan.weight("w_ffn_in")
    (act, gate, up), got = _ffn_in_swiglu(h2, wf_t, carry=plan.carry("ffn_in_swiglu"))
    plan.done("ffn_in_swiglu", got)
    w_down = plan.weight("w_ffn_down")
    def ep_residual_loss(acc, ex, outs, ids, scr):
        dx, dg, part = _loss_head(acc + ex[0][...], ex[1][...], ex[2][...])
        outs[0][...] = dx
        outs[1][...] = dx.astype(BF)
        _accumulate_rows(outs[2], dg, ids[1] == 0)
        _accumulate_rows(outs[3], part, ids[1] == 0)

    dx3, dx3_b, dg_final, loss = _matmul(
        "ffn_down_loss", [act], w_down, "NN", m=T, n=D, tm=512, tn=D, epilogue=ep_residual_loss,
        extra=[(x2, _tile(512, D)), (small["g_final"], _row(D)), (target, _tile(512, D))],
        outs=[(_sds((T, D), F32), _tile(512, D)), (_sds((T, D), BF), _tile(512, D)), (_sds((1, D), F32), _row(D)),
              (_sds((1, 1), F32), pl.BlockSpec((1, 1), lambda j, i, k: (0, 0)))])

    tn_ff = D_FF // 2

    def ep_swiglu_bwd(acc, ex, outs, ids, scr):
        g, u = ex[0][...].astype(F32), ex[1][...].astype(F32)
        sg = jax.nn.sigmoid(g)
        outs[0][...] = (acc * u * sg * (1.0 + g * (1.0 - sg))).astype(BF)
        outs[1][...] = (acc * g * sg).astype(BF)

    dgate, dup = _matmul(
        "ffn_down_bwd", [dx3_b], w_down, "NT", m=T, n=D_FF, tm=tm, tn=tn_ff, epilogue=ep_swiglu_bwd,
        extra=[(gate, _tile(tm, tn_ff)), (up, _tile(tm, tn_ff))],
        outs=[(_sds((T, D_FF), BF), _tile(tm, tn_ff)), (_sds((T, D_FF), BF), _tile(tm, tn_ff))])

    def dw(name, a, b, rows, cols, row_off=0, alias=None, total_rows=None, colsum=False):
        total_rows = rows if total_rows is None else total_rows
        tmw = rows if rows <= 1024 else D_FF // 2
        blk, rem = divmod(row_off, tmw)
        assert rem == 0

        def ep(acc, ex, outs, ids, scr):
            outs[0][...] = acc.astype(BF)
            if colsum:
                outs[1][...] = jnp.sum(ex[0][...].astype(F32), axis=0, keepdims=True)

        outs = [(_sds((total_rows, cols), BF), pl.BlockSpec((tmw, cols), lambda j, i, k: (blk + i, j)))]
        extra = []
        if colsum:
            extra = [(a, pl.BlockSpec((T, tmw), lambda j, i, k: (0, i)))]
            outs.append((_sds((1, rows), F32), pl.BlockSpec((1, tmw), lambda j, i, k: (0, i))))
        carry = plan.carry(name)
        res = carried(name, _matmul(name, [a], b, "TN", m=rows, n=cols, tm=tmw, tn=cols, epilogue=ep, extra=extra,
                                    outs=outs, alias=None if alias is None else (alias, 0), carry=carry), carry)
        return res if colsum else res[0]

    plan.grad_ready(dict(w_ffn_down=dw("ffn_down_dw", act, dx3_b, D_FF, D)))

    def ep_rms_bwd(acc, ex, outs, ids, scr):
        dx, dg = _rms_bwd(acc, ex[0][...], ex[1][...], ex[2][...])
        dx = ex[3][...] + dx
        outs[0][...] = dx
        outs[1][...] = dx.astype(BF)
        _accumulate_rows(outs[2], dg, ids[1] == 0)

    def rms_bwd_io(tm_, xin, r, g, dres):
        return dict(
            extra=[(xin, _tile(tm_, D)), (r, pl.BlockSpec((tm_, 1), lambda j, i, k: (i, 0))), (g, _row(D)),
                   (dres, _tile(tm_, D))],
            outs=[(_sds((T, D), F32), _tile(tm_, D)), (_sds((T, D), BF), _tile(tm_, D)), (_sds((1, D), F32), _row(D))])

    carry = plan.carry("ffn_in_bwd")
    dx2, dx2_b, dg_ffn = carried(
        "ffn_in_bwd",
        _matmul("ffn_in_bwd", [dgate, dup], wf_t, "NN", m=T, n=D, tm=tm, tn=D, tk=D_FF // 2, epilogue=ep_rms_bwd,
                carry=carry, **rms_bwd_io(tm, x2, r2, small["g_ffn_norm"], dx3)), carry)
    plan.launch("send_down")
    gwf_t, _ = _stacked_dw("ffn_in_dw", [dgate, dup], h2, D_FF // 2)
    plan.grad_ready(dict(w_ffn_in=gwf_t))

    def ep_merge_bwd(acc, ex, outs, ids, scr):
        s0 = jax.nn.sigmoid(ex[2][...].astype(F32))
        s1 = jax.nn.sigmoid(ex[3][...].astype(F32))
        outs[0][...] = (acc * s0).astype(BF)
        outs[1][...] = (acc * s1).astype(BF)
        outs[2][...] = (acc * ex[0][...].astype(F32) * s0 * (1.0 - s0)).astype(BF)
        outs[3][...] = (acc * ex[1][...].astype(F32) * s1 * (1.0 - s1)).astype(BF)

    carry = plan.carry("out_proj_bwd_merge")
    dya, dyc, dg0, dg1 = carried(
        "out_proj_bwd_merge",
        _matmul("out_proj_bwd_merge", [dx2_b], w_out, "NT", m=T, n=D, tm=T, tn=tg, epilogue=ep_merge_bwd,
                extra=[(ya, _tile(T, tg)), (yc, _tile(T, tg)), (proj, gate_specs[0]), (proj, gate_specs[1])],
                outs=[(_sds((T, D), BF), _tile(T, tg))] * 4, carry=carry), carry)
    plan.launch("send_ffn")
    gw_out = dw("out_proj_dw", merged, dx2_b, D, D)
    d_o, = _matmul("attn_proj_bwd", [dya], wap_t, "NN", m=T, n=ATTN_WIDTH, tm=tm, tn=ATTN_WIDTH,
                   epilogue=_store(BF), outs=[(_sds((T, ATTN_WIDTH), BF), _tile(tm, ATTN_WIDTH))])
    d_c, = _matmul("conv_proj_bwd", [dyc], wcp_t, "NN", m=T, n=CONV_CHANNELS, tm=tm, tn=CONV_CHANNELS,
                   epilogue=_store(BF), outs=[(_sds((T, CONV_CHANNELS), BF), _tile(tm, CONV_CHANNELS))])
    gwap_t = dw("attn_proj_dw", dya, o, D, ATTN_WIDTH)
    gwcp_t, db_cp = dw("conv_proj_dw", dyc, c, D, CONV_CHANNELS, colsum=True)
    plan.grad_ready(dict(w_out=gw_out, w_attn_proj=gwap_t, w_conv_proj=gwcp_t))
    (dglu, dcw, dcb, dlng, dlnb), got = _conv_bwd(proj, u_conv, d_c, conv_w, small["conv_b"], small["ln_g"],
                                                  small["ln_b"], carry=plan.carry("conv_bwd"))
    plan.done("conv_bwd", got)
    plan.launch("send_mix")
    (dqkv, dsinks), got = _attn_bwd(proj, d_o, small["sinks"], carry=plan.carry("attn_bwd"))
    plan.done("attn_bwd", got)

    segs = [dqkv, dglu, dg0, dg1]
    gwi_t, db_in = _stacked_dw("proj_in_dw", segs, h, 256)
    plan.grad_ready(dict(w_in=gwi_t))
    plan.alone("swap_inp")
    plan.launch("send_inp")
    carry = plan.carry("proj_in_bwd")
    dx, _, dg_mix = carried(
        "proj_in_bwd",
        _matmul("proj_in_bwd", segs, wi_t, "NN", m=T, n=D, tm=512, tn=D, epilogue=ep_rms_bwd, carry=carry,
                **rms_bwd_io(512, x, r1, small["g_mix_norm"], plan.behind("inp", dx2))), carry)

    parts = dict(g_mix_norm=dg_mix, b_in=db_in, sinks=dsinks, conv_w=dcw, conv_b=dcb, ln_g=dlng, ln_b=dlnb,
                 b_conv_proj=db_cp, g_ffn_norm=dg_ffn, g_final=dg_final, loss=loss)
    return dx, parts


def _place():
    x, y, c = lax.axis_index("x"), lax.axis_index("y"), lax.axis_index("c")
    return x, y, c, [(1 - x, y), (x, 1 - y), (1 - x, 1 - y)]


def _gather_copies(x_refs, out_refs, rows_per, send_sems, recv_sems, local_sems):
    x, y, c, chips = _place()
    me, sibling = (x, y, c), (x, y, 1 - c)

    def rows(a, px, py, pc):
        return out_refs[a].at[pl.ds((4 * px + 2 * py + pc) * rows_per[a], rows_per[a])]

    def copy(a, k, block, to, src=None):
        return pltpu.make_async_remote_copy(
            src_ref=rows(a, *block) if src is None else src, dst_ref=rows(a, *block),
            send_sem=send_sems.at[7 * a + k], recv_sem=recv_sems.at[7 * a + k], device_id=to, device_id_type=MESH)

    def local(a):
        return pltpu.make_async_copy(x_refs[a], rows(a, *me), local_sems.at[a])

    def first(a):
        return [copy(a, 0, me, sibling, src=x_refs[a])] + [copy(a, 1 + j, me, (*chip, c), src=x_refs[a])
                                                          for j, chip in enumerate(chips)]

    def arrive(a, j):
        return copy(a, 1 + j, (*chips[j], c), me)

    def passed(a, j):
        return copy(a, 4 + j, (*chips[j], c), sibling)

    def from_sibling(a):
        return [copy(a, 0, sibling, me)] + [copy(a, 4 + j, (*chip, 1 - c), me) for j, chip in enumerate(chips)]

    return len(x_refs), local, first, arrive, passed, from_sibling


def _gather_start(*refs):
    n, local, first, _, _, _ = _gather_copies(*refs)
    for a in range(n):
        local(a).start()
        for cp in first(a):
            cp.start()


def _gather_finish(*refs):
    n, local, first, arrive, passed, from_sibling = _gather_copies(*refs)
    for a in range(n):
        for j in range(3):
            arrive(a, j).wait_recv()
            passed(a, j).start()
    for a in range(n):
        for cp in from_sibling(a):
            cp.wait_recv()
    for a in range(n):
        for cp in first(a) + [passed(a, j) for j in range(3)]:
            cp.wait_send()
        local(a).wait()


def _gather_peers():
    x, y, c, chips = _place()
    return [(x, y, 1 - c)] + [(*chip, c) for chip in chips]


def _gather_sems(n):
    return [pltpu.SemaphoreType.DMA((7 * n,)), pltpu.SemaphoreType.DMA((7 * n,)), pltpu.SemaphoreType.DMA((n,))]


def _gather_carry(shards):
    rows_per = [s.shape[0] for s in shards]
    return _Carry(shards, [_sds((N_DEV * s.shape[0],) + s.shape[1:], s.dtype) for s in shards],
                  _gather_sems(len(shards)),
                  lambda ins, outs, sems: _gather_start(ins, outs, rows_per, *sems),
                  lambda ins, outs, sems: _gather_finish(ins, outs, rows_per, *sems), _gather_peers)


def _first_gather(shards, x, g):
    n = len(shards)
    rows_per = [s.shape[0] for s in shards]
    T, D = x.shape

    def body(*refs):
        x_refs, (xin_ref, g_ref), out_refs, (h_ref, r_ref) = refs[:n], refs[n:n + 2], refs[n + 2:2 * n + 2], refs[2 * n + 2:2 * n + 4]
        send_sems, recv_sems, local_sems = refs[2 * n + 4:]
        x, y, c, chips = _place()
        me, sibling = (x, y, c), (x, y, 1 - c)
        near_x, near_y, far = (*chips[0], c), (*chips[1], c), (*chips[2], c)

        def rows(a, dev, part):
            h = rows_per[a] // 2
            lo, size = {"all": (0, 2 * h), "low": (0, h), "high": (h, h)}[part]
            return out_refs[a].at[pl.ds((4 * dev[0] + 2 * dev[1] + dev[2]) * rows_per[a] + lo, size)]

        def copy(a, k, block, part, to, src=None):
            return pltpu.make_async_remote_copy(
                src_ref=rows(a, block, part) if src is None else src, dst_ref=rows(a, block, part),
                send_sem=send_sems.at[9 * a + k], recv_sem=recv_sems.at[9 * a + k], device_id=to, device_id_type=MESH)

        other = lambda dev: (dev[0], dev[1], 1 - c)
        sent = []
        for a in range(n):
            pltpu.make_async_copy(x_refs[a], rows(a, me, "all"), local_sems.at[a]).start()
            sent += [copy(a, 0, me, "all", sibling, src=x_refs[a]), copy(a, 1, me, "all", near_x, src=x_refs[a]),
                     copy(a, 2, me, "all", near_y, src=x_refs[a])]
        for cp in sent:
            cp.start()
        for i in range(T // CHUNK):
            rws = slice(i * CHUNK, (i + 1) * CHUNK)
            xv = xin_ref[rws, :]
            r = lax.rsqrt(jnp.mean(xv * xv, axis=-1, keepdims=True) + EPS)
            h_ref[rws, :] = (xv * r * g_ref[...]).astype(BF)
            r_ref[rws, :] = r
        for a in range(n):
            copy(a, 1, near_x, "all", me).wait_recv()
            copy(a, 2, near_y, "all", me).wait_recv()
            passed = [copy(a, 3, near_y, "high", near_x), copy(a, 4, near_x, "low", near_y),
                      copy(a, 5, near_x, "all", sibling), copy(a, 6, near_y, "all", sibling)]
            for cp in passed:
                cp.start()
            sent += passed
        for a in range(n):
            copy(a, 3, far, "high", me).wait_recv()
            copy(a, 4, far, "low", me).wait_recv()
            passed = [copy(a, 7, far, "high", sibling), copy(a, 8, far, "low", sibling)]
            for cp in passed:
                cp.start()
            sent += passed
        for a in range(n):
            copy(a, 0, sibling, "all", me).wait_recv()
            copy(a, 5, other(near_x), "all", me).wait_recv()
            copy(a, 6, other(near_y), "all", me).wait_recv()
            copy(a, 7, other(far), "high", me).wait_recv()
            copy(a, 8, other(far), "low", me).wait_recv()
        for cp in sent:
            cp.wait_send()
        for a in range(n):
            pltpu.make_async_copy(x_refs[a], rows(a, me, "all"), local_sems.at[a]).wait()

    vm = pl.BlockSpec(memory_space=pltpu.VMEM)
    return pl.pallas_call(
        body, name="weights_first_gather", in_specs=[*[ANY] * n, vm, vm], out_specs=[*[ANY] * n, vm, vm],
        out_shape=[*[_sds((N_DEV * s.shape[0],) + s.shape[1:], s.dtype) for s in shards], _sds((T, D), BF),
                   _sds((T, 1), F32)],
        scratch_shapes=[pltpu.SemaphoreType.DMA((9 * n,)), pltpu.SemaphoreType.DMA((9 * n,)),
                        pltpu.SemaphoreType.DMA((n,))],
        compiler_params=pltpu.CompilerParams(vmem_limit_bytes=VMEM_LIMIT_BYTES),
    )(*shards, x, g)


def _swap_carry(grads):
    n = len(grads)

    def copies(g_refs, out_refs, sems):
        send_sems, recv_sems = sems
        x, y, c, _ = _place()
        return [pltpu.make_async_remote_copy(
            src_ref=g_refs[a].at[2 * p + 1 - c], dst_ref=out_refs[a].at[p],
            send_sem=send_sems.at[4 * a + p], recv_sem=recv_sems.at[4 * a + p],
            device_id=(x, y, 1 - c), device_id_type=MESH) for a in range(n) for p in range(4)]

    def start(ins, outs, sems):
        for cp in copies(ins, outs, sems):
            cp.start()

    def finish(ins, outs, sems):
        for cp in copies(ins, outs, sems):
            cp.wait()

    def peers():
        x, y, c, _ = _place()
        return [(x, y, 1 - c)]

    return _Carry(grads, [_sds((4,) + g.shape[1:], g.dtype) for g in grads],
                  [pltpu.SemaphoreType.DMA((4 * n,)), pltpu.SemaphoreType.DMA((4 * n,))], start, finish, peers)


def _join(carries):
    carries = [c for c in carries if c is not None]
    if not carries:
        return None
    n_in = [len(c.arrays) for c in carries]
    n_out = [len(c.out_shapes) for c in carries]
    n_sem = [len(c.sems) for c in carries]

    def parts(refs, counts):
        cuts = [sum(counts[:q]) for q in range(len(counts) + 1)]
        return [refs[cuts[q]:cuts[q + 1]] for q in range(len(counts))]

    def start(ins, outs, sems):
        for c, i, o, s in zip(carries, parts(ins, n_in), parts(outs, n_out), parts(sems, n_sem)):
            c.start(i, o, s)

    def finish(ins, outs, sems):
        for c, i, o, s in zip(carries, parts(ins, n_in), parts(outs, n_out), parts(sems, n_sem)):
            c.finish(i, o, s)

    return _Carry([a for c in carries for a in c.arrays], [o for c in carries for o in c.out_shapes],
                  [s for c in carries for s in c.sems], start, finish)


def _run_carry(name, carry):
    n_in, n_out = len(carry.arrays), len(carry.out_shapes)

    def body(*refs):
        carry.start(refs[:n_in], refs[n_in:n_in + n_out], refs[n_in + n_out:])
        carry.finish(refs[:n_in], refs[n_in:n_in + n_out], refs[n_in + n_out:])

    return pl.pallas_call(body, name=name, in_specs=[ANY] * n_in, out_specs=[ANY] * n_out,
                          out_shape=carry.out_shapes, scratch_shapes=carry.sems)(*carry.arrays)


def _run_carry_async(name, carry, collective_id):
    ins = [jax.new_ref(a, memory_space=pltpu.MemorySpace.HBM) for a in carry.arrays]
    outs = [jax.empty_ref(o, memory_space=pltpu.MemorySpace.HBM) for o in carry.out_shapes]

    @pl.kernel(mesh=plsc.ScalarSubcoreMesh(axis_name="sequencer", num_cores=1), name=name,
               scratch_types=tuple(carry.sems), compiler_params=pltpu.CompilerParams(collective_id=collective_id))
    def launch(*sems):
        barrier = pltpu.get_barrier_semaphore()
        peers = carry.peers()
        for peer in peers:
            pl.semaphore_signal(barrier, inc=1, device_id=peer, device_id_type=MESH)
        pl.semaphore_wait(barrier, len(peers))
        carry.start(ins, outs, sems)
        carry.finish(ins, outs, sems)

    launch()
    return [o[...] for o in outs]


def _chip_sums(name, gs, gots, c):
    n = len(gs)

    def body(c_ref, *refs):
        for g_ref, got_ref, o_ref in zip(refs[:n], refs[n:2 * n], refs[2 * n:]):
            o_ref[...] = (g_ref[...].astype(F32) + got_ref[...].astype(F32)).astype(BF)

    mine = [pl.BlockSpec((1,) + g.shape[1:], lambda p, c_ref: (2 * p + c_ref[0], 0, 0)) for g in gs]
    slot = [pl.BlockSpec((1,) + g.shape[1:], lambda p, c_ref: (p, 0, 0)) for g in gs]
    return pl.pallas_call(
        body, name=name,
        grid_spec=pltpu.PrefetchScalarGridSpec(num_scalar_prefetch=1, grid=(4,), in_specs=[*mine, *slot],
                                               out_specs=slot),
        out_shape=[_sds((4,) + g.shape[1:], BF) for g in gs],
        compiler_params=_params(("arbitrary",)),
    )(c, *gs, *gots)


def _send_carry(sums, ks):
    n, nk = len(sums), len(ks)

    def copies(s_refs, out_refs, sems):
        send_sems, recv_sems = sems
        x, y, c, chips = _place()
        return [pltpu.make_async_remote_copy(
            src_ref=s_refs[a].at[2 * chips[k][0] + chips[k][1]], dst_ref=out_refs[a].at[q],
            send_sem=send_sems.at[nk * a + q], recv_sem=recv_sems.at[nk * a + q],
            device_id=(*chips[k], c), device_id_type=MESH) for a in range(n) for q, k in enumerate(ks)]

    def start(ins, outs, sems):
        for cp in copies(ins, outs, sems):
            cp.start()

    def finish(ins, outs, sems):
        for cp in copies(ins, outs, sems):
            cp.wait()

    def peers():
        x, y, c, chips = _place()
        return [(*chips[k], c) for k in ks]

    return _Carry(sums, [_sds((nk,) + s.shape[1:], s.dtype) for s in sums],
                  [pltpu.SemaphoreType.DMA((nk * n,)), pltpu.SemaphoreType.DMA((nk * n,))], start, finish, peers)


def _adam_math(w, g, m, v):
    m = ADAM_B1 * m + (1.0 - ADAM_B1) * g
    v = ADAM_B2 * v + (1.0 - ADAM_B2) * (g * g)
    m_hat = m / (1.0 - ADAM_B1 ** ADAM_STEP)
    v_hat = v / (1.0 - ADAM_B2 ** ADAM_STEP)
    delta = -ADAM_LR * (m_hat / (jnp.sqrt(v_hat) + ADAM_EPS) + ADAM_WD * w)
    return delta, m, v


def _adamw(name, w, g, m, v):
    rows, cols = w.shape
    tr = 256 if rows % 256 == 0 else rows

    def body(w_ref, g_ref, m_ref, v_ref, d_ref, nm_ref, nv_ref):
        d_ref[...], nm_ref[...], nv_ref[...] = _adam_math(w_ref[...], g_ref[...], m_ref[...], v_ref[...])

    t = pl.BlockSpec((tr, cols), lambda i: (i, 0))
    return pl.pallas_call(
        body, name=name, grid=(rows // tr,), in_specs=[t] * 4, out_specs=[t] * 3,
        out_shape=[_sds((rows, cols), F32)] * 3, compiler_params=_params(("arbitrary",)),
    )(w, g, m, v)


def _grad_adamw(name, g, got, got3, ids, w, m, v):
    _, rows, cols = g.shape
    n3 = len(got3)
    tr = rows // 2 if rows >= 256 else rows

    def body(ids_ref, g_ref, got_ref, *rest):
        w_ref, m_ref, v_ref, o_ref, d_ref, nm_ref, nv_ref = rest[n3:]
        tot = g_ref[0].astype(F32) + got_ref[0].astype(F32)
        for r_ref in rest[:n3]:
            for q in range(r_ref.shape[0]):
                tot = tot + r_ref[q].astype(F32)
        o_ref[...] = tot
        d_ref[...], nm_ref[...], nv_ref[...] = _adam_math(w_ref[...], tot, m_ref[...], v_ref[...])

    tile = pl.BlockSpec((tr, cols), lambda i, ids_ref: (i, 0))
    return pl.pallas_call(
        body, name=name,
        grid_spec=pltpu.PrefetchScalarGridSpec(
            num_scalar_prefetch=1, grid=(rows // tr,),
            in_specs=[pl.BlockSpec((1, tr, cols), lambda i, ids_ref: (ids_ref[0], i, 0)),
                      pl.BlockSpec((1, tr, cols), lambda i, ids_ref: (ids_ref[1], i, 0)),
                      *[pl.BlockSpec((r.shape[0], tr, cols), lambda i, ids_ref: (0, i, 0)) for r in got3],
                      tile, tile, tile],
            out_specs=[tile] * 4),
        out_shape=[_sds((rows, cols), F32)] * 4,
        compiler_params=_params(("arbitrary",)),
    )(ids, g, got, *got3, w, m, v)


SMALL_NAMES = ["g_mix_norm", "b_in", "sinks", "conv_b", "ln_g", "ln_b", "b_conv_proj", "g_ffn_norm", "g_final"]
_PACK_ROWS = 32


def _small_pack(parts):
    C = CONV_CHANNELS
    part_list = [parts["g_mix_norm"], parts["b_in"], parts["sinks"], parts["conv_b"], parts["ln_g"], parts["ln_b"],
                 parts["b_conv_proj"], parts["g_ffn_norm"], parts["g_final"], parts["loss"], parts["conv_w"]]

    def body(p_mix, p_b, p_sink, p_cb, p_lg, p_lb, p_bcp, p_ffn, p_fin, p_loss, p_cw, pack):
        pack[...] = jnp.zeros_like(pack)
        pack[0:1, :] = p_mix[...]
        pack[1:2, 0:GLU_OFF] = p_b[:, 0:GLU_OFF]
        pack[2:3, :] = p_b[:, GLU_OFF:GATE_OFF]
        pack[3:4, :] = p_b[:, GATE_OFF:GATE_OFF + D_MODEL]
        pack[4:5, :] = p_b[:, GATE_OFF + D_MODEL:]
        pack[5:6, 0:128] = p_sink[...]
        pack[6:7, 0:C] = p_cb[...]
        pack[6:7, C:2 * C] = p_lg[...]
        pack[7:8, 0:C] = p_lb[...]
        pack[8:9, :] = p_bcp[...]
        pack[9:10, :] = p_ffn[...]
        pack[10:11, :] = p_fin[...]
        pack[11:12, 0:128] = jnp.broadcast_to(p_loss[...], (1, 128))
        pack[12:28, 0:C] = p_cw[0:16, :]
        pack[12:28, C:2 * C] = p_cw[16:32, :]

    vm = pl.BlockSpec(memory_space=pltpu.VMEM)
    return pl.pallas_call(body, name="small_pack", in_specs=[vm] * len(part_list), out_specs=vm,
                          out_shape=_sds((_PACK_ROWS, D_MODEL), F32))(*part_list)


def _small_adamw(gathered, small_w, small_m, small_v):
    C = CONV_CHANNELS
    names = SMALL_NAMES
    widths = [small_w[k].shape[1] for k in names]
    n_small = len(names)

    def body(*refs):
        tot_ref = refs[0]
        w_refs = refs[1:1 + n_small]
        m_refs = refs[1 + n_small:1 + 2 * n_small]
        v_refs = refs[1 + 2 * n_small:1 + 3 * n_small]
        o = 1 + 3 * n_small
        loss_ref, cw_ref = refs[o], refs[o + 1]
        out_refs = refs[o + 2:o + 2 + 4 * n_small]
        tot = tot_ref[0:_PACK_ROWS, :]
        for d in range(1, N_DEV):
            tot = tot + tot_ref[d * _PACK_ROWS:(d + 1) * _PACK_ROWS, :]
        loss_ref[...] = tot[11:12, 0:1]
        cw_ref[0:16, :] = tot[12:28, 0:C]
        cw_ref[16:32, :] = tot[12:28, C:2 * C]
        grads = dict(
            g_mix_norm=tot[0:1, :],
            b_in=jnp.concatenate([tot[1:2, 0:GLU_OFF], tot[2:3, :], tot[3:4, :], tot[4:5, :]], axis=1),
            sinks=tot[5:6, 0:N_Q_HEADS], conv_b=tot[6:7, 0:C], ln_g=tot[6:7, C:2 * C], ln_b=tot[7:8, 0:C],
            b_conv_proj=tot[8:9, :], g_ffn_norm=tot[9:10, :], g_final=tot[10:11, :])
        for s, k in enumerate(names):
            g = grads[k]
            d, nm, nv = _adam_math(w_refs[s][...], g, m_refs[s][...], v_refs[s][...])
            out_refs[4 * s][...] = g
            out_refs[4 * s + 1][...] = d
            out_refs[4 * s + 2][...] = nm
            out_refs[4 * s + 3][...] = nv

    vm = pl.BlockSpec(memory_space=pltpu.VMEM)
    args = [gathered, *[small_w[k] for k in names], *[small_m[k] for k in names], *[small_v[k] for k in names]]
    out_shape = [_sds((1, 1), F32), _sds((CONV_PAD, C), F32)]
    for wd in widths:
        out_shape += [_sds((1, wd), F32)] * 4
    res = pl.pallas_call(
        body, name="small_adamw",
        in_specs=[vm] * len(args), out_specs=[vm] * len(out_shape), out_shape=out_shape,
        compiler_params=pltpu.CompilerParams(vmem_limit_bytes=VMEM_LIMIT_BYTES),
    )(*args)
    return res[0], res[1], {k: res[2 + 4 * s:6 + 4 * s] for s, k in enumerate(names)}


BIG = dict(w_in=True, w_attn_proj=True, w_conv_proj=True, w_out=False, w_ffn_in=True, w_ffn_down=False)
WEIGHT_NAMES = ["g_mix_norm", "w_in", "b_in", "sinks", "conv_w", "conv_b", "ln_g", "ln_b", "w_attn_proj",
                "w_conv_proj", "b_conv_proj", "w_out", "g_ffn_norm", "w_ffn_in", "w_ffn_down", "g_final"]


class _Plan:
    GROUPS = dict(down=["w_ffn_down"], ffn=["w_ffn_in"], mix=["w_out", "w_attn_proj", "w_conv_proj"], inp=["w_in"])
    ALL = (0, 1, 2)
    RIDES = dict(
        gather_mix=[("gather", ["w_attn_proj", "w_conv_proj", "w_out"])], gather_ffn=[("gather", ["w_ffn_in"])],
        gather_down=[("gather", ["w_ffn_down"])],
        ffn_in_bwd=[("swap", "down")], send_down=[("send", "down", ALL)],
        out_proj_bwd_merge=[("swap", "ffn")], send_ffn=[("send", "ffn", ALL)],
        conv_bwd=[("swap", "mix")], send_mix=[("send", "mix", ALL)],
        swap_inp=[("swap", "inp")], send_inp=[("send", "inp", ALL)])
    ASYNC = dict(gather_mix=1, gather_ffn=2, gather_down=3, send_down=4, send_ffn=5, send_mix=6, send_inp=7)

    def __init__(self, shards, c1):
        self.shards, self.c1 = shards, c1
        self.full, self.slots, self.got, self.sums, self.got3 = {}, {}, {}, {}, {}

    def weight(self, name):
        return self.full[name]

    def grad_ready(self, grads):
        for k, g in grads.items():
            self.slots[k] = g.reshape(N_DEV, g.shape[0] // N_DEV, g.shape[1])

    def _one(self, kind, what, ks=None):
        if kind == "gather":
            return _gather_carry([self.shards[k] for k in what])
        names = self.GROUPS[what]
        if kind == "swap":
            return _swap_carry([self.slots[k] for k in names])
        return _send_carry([self.sums[k] for k in names], ks)

    def carry(self, call):
        return _join([self._one(*ride) for ride in self.RIDES.get(call, [])])

    def done(self, call, outs):
        outs = list(outs)
        for kind, what, *_ in self.RIDES.get(call, []):
            names = what if kind == "gather" else self.GROUPS[what]
            mine, outs = outs[:len(names)], outs[len(names):]
            if kind == "gather":
                self.full.update(zip(names, mine))
            elif kind == "send":
                for k, r in zip(names, mine):
                    self.got3.setdefault(k, []).append(r)
            else:
                self.got.update(zip(names, mine))
                self.sums.update(zip(names, _chip_sums(f"chip_sums_{what}", [self.slots[k] for k in names], mine, self.c1)))

    def alone(self, call):
        self.done(call, _run_carry(call, self.carry(call)))

    def behind(self, group, x):
        return lax.optimization_barrier((x, tuple(self.sums[k] for k in self.GROUPS[group])))[0]

    def launch(self, call, after=None):
        carry = self._one(*self.RIDES[call][0])
        if after is not None:
            carry.arrays = list(lax.optimization_barrier((tuple(carry.arrays), after))[0])
        self.done(call, _run_carry_async(call, carry, self.ASYNC[call]))


def kernel(x, g_mix_norm, w_in, b_in, sinks, conv_w, conv_b, ln_g, ln_b, w_attn_proj, w_conv_proj, b_conv_proj, w_out, g_ffn_norm, w_ffn_in, w_ffn_down, g_final, loss_target, m_g_mix_norm, m_w_in, m_b_in, m_sinks, m_conv_w, m_conv_b, m_ln_g, m_ln_b, m_w_attn_proj, m_w_conv_proj, m_b_conv_proj, m_w_out, m_g_ffn_norm, m_w_ffn_in, m_w_ffn_down, m_g_final, v_g_mix_norm, v_w_in, v_b_in, v_sinks, v_conv_w, v_conv_b, v_ln_g, v_ln_b, v_w_attn_proj, v_w_conv_proj, v_b_conv_proj, v_w_out, v_g_ffn_norm, v_w_ffn_in, v_w_ffn_down, v_g_final):
    w = dict(g_mix_norm=g_mix_norm, w_in=w_in, b_in=b_in, sinks=sinks, conv_w=conv_w, conv_b=conv_b, ln_g=ln_g,
             ln_b=ln_b, w_attn_proj=w_attn_proj, w_conv_proj=w_conv_proj, b_conv_proj=b_conv_proj, w_out=w_out,
             g_ffn_norm=g_ffn_norm, w_ffn_in=w_ffn_in, w_ffn_down=w_ffn_down, g_final=g_final)
    m = dict(g_mix_norm=m_g_mix_norm, w_in=m_w_in, b_in=m_b_in, sinks=m_sinks, conv_w=m_conv_w, conv_b=m_conv_b,
             ln_g=m_ln_g, ln_b=m_ln_b, w_attn_proj=m_w_attn_proj, w_conv_proj=m_w_conv_proj,
             b_conv_proj=m_b_conv_proj, w_out=m_w_out, g_ffn_norm=m_g_ffn_norm, w_ffn_in=m_w_ffn_in,
             w_ffn_down=m_w_ffn_down, g_final=m_g_final)
    v = dict(g_mix_norm=v_g_mix_norm, w_in=v_w_in, b_in=v_b_in, sinks=v_sinks, conv_w=v_conv_w, conv_b=v_conv_b,
             ln_g=v_ln_g, ln_b=v_ln_b, w_attn_proj=v_w_attn_proj, w_conv_proj=v_w_conv_proj,
             b_conv_proj=v_b_conv_proj, w_out=v_w_out, g_ffn_norm=v_g_ffn_norm, w_ffn_in=v_w_ffn_in,
             w_ffn_down=v_w_ffn_down, g_final=v_g_final)
    ax, ay, ac = lax.axis_index("x"), lax.axis_index("y"), lax.axis_index("c")
    me = 4 * ax + 2 * ay + ac
    chip = 2 * ax + ay

    shards = {k: (w[k][0].T if tr else w[k][0]).astype(BF) for k, tr in BIG.items()}
    cw_shard = jnp.pad(conv_w[0].T, ((0, 0), (0, 1))).reshape(16, 128)
    wi_t, cw_full, h, r1 = _first_gather([shards["w_in"], cw_shard], x[0], g_mix_norm)
    conv_full = cw_full.reshape(CONV_CHANNELS, CONV_PAD).T

    as_row = lambda a: a.reshape(1, -1)
    small_w = {k: as_row(w[k]) for k in SMALL_NAMES}
    small_m = {k: as_row(m[k]) for k in SMALL_NAMES}
    small_v = {k: as_row(v[k]) for k in SMALL_NAMES}
    plan = _Plan(shards, ac.reshape(1).astype(jnp.int32))
    plan.launch("gather_mix", after=wi_t)
    dx, parts = _local_step(x[0], h, r1, loss_target[0], small_w, wi_t, conv_full, plan)

    ids = jnp.stack([me, chip]).astype(jnp.int32)
    grads, delta, new_m, new_v, after = {}, {}, {}, {}, dx
    packed = _small_pack(parts)
    for k in sorted(BIG, key=lambda k: k == "w_in"):
        if k == "w_in":
            packed = lax.optimization_barrier((packed, after))[0]
            small_gathered, = _run_carry_async("small_gather", _gather_carry([packed]), 8)
        flip = (lambda a: a.T) if BIG[k] else (lambda a: a)
        wk = lax.optimization_barrier((w[k][0], after))[0]
        outs = _grad_adamw(f"grad_adamw_{k}", plan.slots[k], plan.got[k], plan.got3[k], ids,
                           flip(wk), flip(m[k][0]), flip(v[k][0]))
        after = outs[0]
        grads[k], delta[k], new_m[k], new_v[k] = (flip(a)[None] for a in outs)

    loss, cw_grad, small_out = _small_adamw(small_gathered, small_w, small_m, small_v)
    for k in SMALL_NAMES:
        g, d, nm, nv = (a.reshape(w[k].shape) for a in small_out[k])
        grads[k], delta[k], new_m[k], new_v[k] = g, d, nm, nv
    cw_mine = lax.dynamic_slice(cw_grad, (0, me * 64), (CONV_WIDTH, 64))
    d, nm, nv = _adamw("adamw_conv_w", conv_w[0], cw_mine, m_conv_w[0], v_conv_w[0])
    grads["conv_w"], delta["conv_w"], new_m["conv_w"], new_v["conv_w"] = cw_mine[None], d[None], nm[None], nv[None]

    return (loss.reshape(()), dx[None], *[grads[k] for k in WEIGHT_NAMES], *[delta[k] for k in WEIGHT_NAMES],
            *[new_m[k] for k in WEIGHT_NAMES], *[new_v[k] for k in WEIGHT_NAMES])
```

```python
import functools

import jax
import jax.numpy as jnp
from jax import lax
from jax.experimental import pallas as pl
from jax.experimental.pallas import tpu as pltpu
from jax.experimental.pallas import tpu_sc as plsc

F32 = jnp.float32
BF = jnp.bfloat16

SEQ = 2048
D_MODEL = 1024
HEAD_DIM = 64
N_Q_HEADS = 8
N_KV_HEADS = 2
GROUP = N_Q_HEADS // N_KV_HEADS
BLOCK = 128
ATTN_WIDTH = 512
KV_WIDTH = 128
CONV_CHANNELS = 512
CONV_WIDTH = 31
CONV_PAD = 32
GLU_OFF = 768
GATE_OFF = 1792
IN_WIDTH = 3840
D_FF = 2816
EPS = 1e-5
NEG = -1e30
N_DEV = 8

ADAM_LR = 0.001
ADAM_B1 = 0.9
ADAM_B2 = 0.999
ADAM_EPS = 1e-08
ADAM_WD = 0.01
ADAM_STEP = 10

VMEM_LIMIT_BYTES = 56 * 1024 * 1024
MESH = pl.DeviceIdType.MESH
ANY = pl.BlockSpec(memory_space=pl.ANY)

_DIMS = {"NN": (((1,), (0,)), ((), ())), "NT": (((1,), (1,)), ((), ())), "TN": (((0,), (0,)), ((), ()))}


def _params(sem):
    return pltpu.CompilerParams(dimension_semantics=sem, vmem_limit_bytes=VMEM_LIMIT_BYTES)


class _Carry:
    def __init__(self, arrays, out_shapes, sems, start, finish, peers=None):
        self.arrays, self.out_shapes, self.sems, self.start, self.finish = arrays, out_shapes, sems, start, finish
        self.peers = peers


def _carry_io(carry):
    if carry is None:
        return [], [], []
    return list(carry.arrays), list(carry.out_shapes), list(carry.sems)


def _matmul(name, a_list, b, mode, *, m, n, tm, tn, tk=None, epilogue, extra=(), outs, b_off=(0, 0), alias=None,
            scratch=(), carry=None):
    seg_k = [a.shape[0] if mode == "TN" else a.shape[1] for a in a_list]
    whole = tk is None
    seg_nk = [1] * len(a_list) if whole else [ks // tk for ks in seg_k]
    nk = 1 if whole else sum(seg_nk)
    starts = [sum(seg_nk[:s]) for s in range(len(seg_nk))]
    k_starts = [sum(seg_k[:s]) for s in range(len(seg_k))]
    k_tot = sum(seg_k)
    n_a, n_extra, n_out = len(a_list), len(extra), len(outs)

    a_specs = []
    for st, ns, ks in zip(starts, seg_nk, seg_k):
        if mode == "TN":
            a_specs.append(pl.BlockSpec((ks if whole else tk, tm), lambda j, i, k: (k, i)))
        elif whole:
            a_specs.append(pl.BlockSpec((tm, ks), lambda j, i, k: (i, 0)))
        else:
            a_specs.append(pl.BlockSpec((tm, tk), functools.partial(
                lambda j, i, k, st, ns: (i, jnp.clip(k - st, 0, ns - 1)), st=st, ns=ns)))
    bk = k_tot if whole else tk
    if mode == "NT":
        b_spec = pl.BlockSpec((tn, bk), lambda j, i, k: (b_off[0] + j, b_off[1] + k))
    else:
        b_spec = pl.BlockSpec((bk, tn), lambda j, i, k: (b_off[0] + k, b_off[1] + j))
    n_alias = 0 if alias is None else 1
    c_in, c_out, c_sems = _carry_io(carry)
    n_acc = 0 if whole else 1
    nj, ni = n // tn, m // tm

    def body(*refs):
        pos = [n_a, 1, n_alias, n_extra, len(c_in), n_out, len(c_out), n_acc, len(scratch), len(c_sems)]
        cuts = [sum(pos[:q]) for q in range(len(pos) + 1)]
        a_refs, (b_ref,), _, ex, ci_refs, out_refs, co_refs, acc_refs, scr, cs_refs = (
            refs[cuts[q]:cuts[q + 1]] for q in range(len(pos)))
        j, i, k = pl.program_id(0), pl.program_id(1), pl.program_id(2)
        ids = (j, i)
        if carry is not None:
            @pl.when((j == 0) & (i == 0) & (k == 0))
            def _():
                carry.start(ci_refs, co_refs, cs_refs)

        def dot(a_ref, bv):
            return lax.dot_general(a_ref[...].astype(BF), bv.astype(BF), _DIMS[mode], preferred_element_type=F32)

        if whole:
            tot = None
            for a_ref, k0, ks in zip(a_refs, k_starts, seg_k):
                if n_a == 1:
                    bv = b_ref[...]
                else:
                    bv = b_ref[:, k0:k0 + ks] if mode == "NT" else b_ref[k0:k0 + ks, :]
                part = dot(a_ref, bv)
                tot = part if tot is None else tot + part
            epilogue(tot, ex, out_refs, ids, scr)
        else:
            acc, = acc_refs

            @pl.when(k == 0)
            def _():
                acc[...] = jnp.zeros_like(acc)

            for a_ref, st, ns in zip(a_refs, starts, seg_nk):
                if n_a == 1:
                    acc[...] += dot(a_ref, b_ref[...])
                else:
                    @pl.when((k >= st) & (k < st + ns))
                    def _(a_ref=a_ref):
                        acc[...] += dot(a_ref, b_ref[...])

            @pl.when(k == nk - 1)
            def _():
                epilogue(acc[...], ex, out_refs, ids, scr)

        if carry is not None:
            @pl.when((j == nj - 1) & (i == ni - 1) & (k == nk - 1))
            def _():
                carry.finish(ci_refs, co_refs, cs_refs)

    in_specs = [*a_specs, b_spec]
    args = [*a_list, b]
    io_alias = {}
    if alias is not None:
        in_specs.append(pl.BlockSpec(memory_space=pl.ANY))
        args.append(alias[0])
        io_alias = {n_a + 1: alias[1]}
    in_specs += [s for _, s in extra] + [pl.BlockSpec(memory_space=pl.ANY)] * len(c_in)
    args += [x for x, _ in extra] + c_in
    res = pl.pallas_call(
        body, name=name, grid=(nj, ni, nk), in_specs=in_specs,
        out_specs=[s for _, s in outs] + [pl.BlockSpec(memory_space=pl.ANY)] * len(c_out),
        out_shape=[o for o, _ in outs] + c_out,
        scratch_shapes=[*([] if whole else [pltpu.VMEM((tm, tn), F32)]), *scratch, *c_sems],
        input_output_aliases=io_alias,
        compiler_params=_params(("arbitrary", "arbitrary", "arbitrary")),
    )(*args)
    return res if carry is None else (res[:n_out], res[n_out:])


def _tile(tm, tn):
    return pl.BlockSpec((tm, tn), lambda j, i, k: (i, j))


def _row(tn):
    return pl.BlockSpec((1, tn), lambda j, i, k: (0, j))


def _store(dtype):
    def ep(acc, ex, outs, ids, scr):
        outs[0][...] = acc.astype(dtype)
    return ep


def _sds(shape, dtype):
    return jax.ShapeDtypeStruct(shape, dtype)


def _rms_bwd(dh, xv, r, g):
    xh = xv * r
    dxh = dh * g
    dx = r * (dxh - xh * jnp.mean(dxh * xh, axis=-1, keepdims=True))
    return dx, jnp.sum(dh * xh, axis=0, keepdims=True)


def _accumulate_rows(ref, val, first):
    @pl.when(first)
    def _():
        ref[...] = val

    @pl.when(jnp.logical_not(first))
    def _():
        ref[...] += val


def _loss_head(xv, g, target):
    r = lax.rsqrt(jnp.mean(xv * xv, axis=-1, keepdims=True) + EPS)
    err = xv * r * g - target
    dx, dg = _rms_bwd(err * (1.0 / xv.shape[-1]), xv, r, g)
    part = 0.5 * jnp.sum(jnp.mean(err * err, axis=-1, keepdims=True), axis=0, keepdims=True)
    return dx, dg, part


def _lane_half(shape, h):
    lane = lax.broadcasted_iota(jnp.int32, shape, 1)
    return (lane >= HEAD_DIM * h) & (lane < HEAD_DIM * (h + 1))


def _to_half(v, w, h):
    if w != h:
        v = pltpu.roll(v, HEAD_DIM, 1)
    return jnp.where(_lane_half(v.shape, h), v, 0.0)


def _attn_block(qkv_ref, sinks_ref, n, h):
    r0 = pl.multiple_of(n * BLOCK, BLOCK)
    p0 = pl.multiple_of(jnp.maximum(n - 1, 0) * BLOCK, BLOCK)
    rows = pl.ds(r0, BLOCK)
    prev = pl.ds(p0, BLOCK)
    k2 = jnp.concatenate([qkv_ref[prev, ATTN_WIDTH:ATTN_WIDTH + KV_WIDTH],
                          qkv_ref[rows, ATTN_WIDTH:ATTN_WIDTH + KV_WIDTH]], axis=0)
    v2 = jnp.concatenate([qkv_ref[prev, ATTN_WIDTH + KV_WIDTH:ATTN_WIDTH + 2 * KV_WIDTH],
                          qkv_ref[rows, ATTN_WIDTH + KV_WIDTH:ATTN_WIDTH + 2 * KV_WIDTH]], axis=0)
    qs = []
    for g in range(GROUP):
        hq = GROUP * h + g
        blk = qkv_ref[rows, (hq // 2) * 128:(hq // 2 + 1) * 128].astype(F32)
        qs.append(_to_half(blk, hq % 2, h))
    q4 = jnp.concatenate(qs, axis=0).astype(BF)
    s = lax.dot_general(q4, k2, _DIMS["NT"], preferred_element_type=F32) * (HEAD_DIM ** -0.5)
    shape = s.shape
    row = lax.broadcasted_iota(jnp.int32, shape, 0)
    qi = row & (BLOCK - 1)
    kj = lax.broadcasted_iota(jnp.int32, shape, 1)
    diff = qi + BLOCK - kj
    valid = (diff >= 0) & (diff < BLOCK) & ((kj >= BLOCK) | (n > 0))
    s = jnp.where(valid, s, NEG)
    row1 = lax.broadcasted_iota(jnp.int32, (shape[0], 1), 0)
    sink = jnp.zeros((shape[0], 1), F32)
    for g in range(GROUP):
        sink = jnp.where((row1 >= g * BLOCK) & (row1 < (g + 1) * BLOCK), sinks_ref[0, GROUP * h + g], sink)
    m = jnp.maximum(jnp.max(s, axis=-1, keepdims=True), sink)
    e = jnp.exp(s - m)
    es = jnp.exp(sink - m)
    inv = 1.0 / (jnp.sum(e, axis=-1, keepdims=True) + es)
    return e * inv, es * inv, q4, k2, v2, rows, prev


def _attn_fwd(proj, sinks, carry=None):
    T = proj.shape[0]
    c_in, c_out, c_sems = _carry_io(carry)

    def body(*refs):
        qkv_ref, sinks_ref = refs[:2]
        ci_refs = refs[2:2 + len(c_in)]
        o_ref = refs[2 + len(c_in)]
        co_refs = refs[3 + len(c_in):3 + len(c_in) + len(c_out)]
        cs_refs = refs[3 + len(c_in) + len(c_out):]
        if carry is not None:
            carry.start(ci_refs, co_refs, cs_refs)

        def blk(n, z):
            outs = [None] * (N_Q_HEADS // 2)
            for h in range(N_KV_HEADS):
                p, _, _, _, v2, rows, _ = _attn_block(qkv_ref, sinks_ref, n, h)
                o = lax.dot_general(p.astype(BF), v2, _DIMS["NN"], preferred_element_type=F32)
                for g in range(GROUP):
                    hq = GROUP * h + g
                    piece = jnp.where(_lane_half((BLOCK, 128), h), o[g * BLOCK:(g + 1) * BLOCK], 0.0)
                    if hq % 2 != h:
                        piece = pltpu.roll(piece, HEAD_DIM, 1)
                    outs[hq // 2] = piece if outs[hq // 2] is None else outs[hq // 2] + piece
            for pb in range(N_Q_HEADS // 2):
                o_ref[rows, pb * 128:(pb + 1) * 128] = outs[pb].astype(BF)
            return z

        lax.fori_loop(0, T // BLOCK, blk, 0)
        if carry is not None:
            carry.finish(ci_refs, co_refs, cs_refs)

    res = pl.pallas_call(
        body, name="attn_fwd", grid=(1,),
        in_specs=[pl.BlockSpec((T, GLU_OFF), lambda i: (0, 0)), pl.BlockSpec(memory_space=pltpu.SMEM),
                  *[ANY] * len(c_in)],
        out_specs=[pl.BlockSpec((T, ATTN_WIDTH), lambda i: (0, 0)), *[ANY] * len(c_out)],
        out_shape=[_sds((T, ATTN_WIDTH), BF), *c_out], scratch_shapes=c_sems,
        compiler_params=_params(("arbitrary",)),
    )(proj, sinks, *c_in)
    return res[0], res[1:]


def _attn_bwd(proj, d_o, sinks, carry=None):
    T = proj.shape[0]
    c_in, c_out, c_sems = _carry_io(carry)

    def body(*refs):
        qkv_ref, do_ref, sinks_ref = refs[:3]
        ci_refs = refs[3:3 + len(c_in)]
        dqkv_ref, dsink_ref = refs[3 + len(c_in):5 + len(c_in)]
        co_refs = refs[5 + len(c_in):5 + len(c_in) + len(c_out)]
        dk_acc, dv_acc = refs[5 + len(c_in) + len(c_out):7 + len(c_in) + len(c_out)]
        cs_refs = refs[7 + len(c_in) + len(c_out):]
        if carry is not None:
            carry.start(ci_refs, co_refs, cs_refs)
        dsink_ref[...] = jnp.zeros_like(dsink_ref)
        dk_acc[...] = jnp.zeros_like(dk_acc)
        dv_acc[...] = jnp.zeros_like(dv_acc)

        def blk(n, carry):
            dqs = [None] * (N_Q_HEADS // 2)
            for h in range(N_KV_HEADS):
                p, psink, q4, k2, v2, rows, prev = _attn_block(qkv_ref, sinks_ref, n, h)
                dos = []
                for g in range(GROUP):
                    hq = GROUP * h + g
                    dos.append(_to_half(do_ref[rows, (hq // 2) * 128:(hq // 2 + 1) * 128].astype(F32), hq % 2, h))
                do4 = jnp.concatenate(dos, axis=0).astype(BF)
                dp = lax.dot_general(do4, v2, _DIMS["NT"], preferred_element_type=F32)
                delta = jnp.sum(p * dp, axis=-1, keepdims=True)
                ds = (p * (dp - delta) * (HEAD_DIM ** -0.5)).astype(BF)
                dsk = psink * delta
                for g in range(GROUP):
                    hq = GROUP * h + g
                    tot = -jnp.sum(dsk[g * BLOCK:(g + 1) * BLOCK], axis=0, keepdims=True)
                    lane = lax.broadcasted_iota(jnp.int32, (1, 128), 1)
                    dsink_ref[...] += jnp.where(lane == hq, tot, 0.0)
                dq = lax.dot_general(ds, k2, _DIMS["NN"], preferred_element_type=F32)
                dk = lax.dot_general(ds, q4, _DIMS["TN"], preferred_element_type=F32)
                dv = lax.dot_general(p.astype(BF), do4, _DIMS["TN"], preferred_element_type=F32)
                dk_acc[prev, :] += dk[:BLOCK]
                dk_acc[rows, :] += dk[BLOCK:]
                dv_acc[prev, :] += dv[:BLOCK]
                dv_acc[rows, :] += dv[BLOCK:]
                for g in range(GROUP):
                    hq = GROUP * h + g
                    piece = jnp.where(_lane_half((BLOCK, 128), h), dq[g * BLOCK:(g + 1) * BLOCK], 0.0)
                    if hq % 2 != h:
                        piece = pltpu.roll(piece, HEAD_DIM, 1)
                    dqs[hq // 2] = piece if dqs[hq // 2] is None else dqs[hq // 2] + piece
            for pb in range(N_Q_HEADS // 2):
                dqkv_ref[rows, pb * 128:(pb + 1) * 128] = dqs[pb].astype(BF)
            return carry

        lax.fori_loop(0, T // BLOCK, blk, 0)
        dqkv_ref[:, ATTN_WIDTH:ATTN_WIDTH + KV_WIDTH] = dk_acc[...].astype(BF)
        dqkv_ref[:, ATTN_WIDTH + KV_WIDTH:] = dv_acc[...].astype(BF)
        if carry is not None:
            carry.finish(ci_refs, co_refs, cs_refs)

    res = pl.pallas_call(
        body, name="attn_bwd", grid=(1,),
        in_specs=[pl.BlockSpec((T, GLU_OFF), lambda i: (0, 0)), pl.BlockSpec((T, ATTN_WIDTH), lambda i: (0, 0)),
                  pl.BlockSpec(memory_space=pltpu.SMEM), *[ANY] * len(c_in)],
        out_specs=[pl.BlockSpec((T, GLU_OFF), lambda i: (0, 0)), pl.BlockSpec((1, 128), lambda i: (0, 0)),
                   *[ANY] * len(c_out)],
        out_shape=[_sds((T, GLU_OFF), BF), _sds((1, 128), F32), *c_out],
        scratch_shapes=[pltpu.VMEM((T, KV_WIDTH), F32), pltpu.VMEM((T, KV_WIDTH), F32), *c_sems],
        compiler_params=_params(("arbitrary",)),
    )(proj, d_o, sinks, *c_in)
    return res[:2], res[2:]


CHUNK = 256
SUB = 32
WIN = CHUNK + 32
PAD_ROWS = SEQ + 2 * CONV_PAD
_GLU_SPECS = [pl.BlockSpec((SEQ, 256), functools.partial(lambda i, c: (0, c), c=GLU_OFF // 256 + c)) for c in range(4)]


def _glu_to_pad(a0, a1, b0, b1, zpad):
    C = CONV_CHANNELS
    zpad[0:CONV_PAD, :] = jnp.zeros((CONV_PAD, C), F32)
    zpad[CONV_PAD + SEQ:, :] = jnp.zeros((CONV_PAD, C), F32)
    zpad[CONV_PAD:CONV_PAD + SEQ, 0:256] = a0[...].astype(F32) * jax.nn.sigmoid(b0[...].astype(F32))
    zpad[CONV_PAD:CONV_PAD + SEQ, 256:C] = a1[...].astype(F32) * jax.nn.sigmoid(b1[...].astype(F32))


def _tap_windows(src, base, win):
    for b in range(8):
        win[b, 0:WIN - 8, :] = src[base + b:base + b + WIN - 8, :]


def _taps(win, w_ref, init, out, flip):
    def sub(si, carry):
        r0 = pl.multiple_of(si * SUB, SUB)
        acc = jnp.broadcast_to(init, (SUB, CONV_CHANNELS))
        for k in range(CONV_WIDTH):
            wk = (CONV_WIDTH - 1 - k) if flip else k
            acc = acc + w_ref[wk:wk + 1, :] * win[k % 8, pl.ds(r0 + 8 * (k // 8), SUB), :]
        out[pl.ds(r0, SUB), :] = acc
        return carry

    lax.fori_loop(0, CHUNK // SUB, sub, 0)


def _tap_grads(win, du, dwacc):
    def sub(si, carry):
        r0 = pl.multiple_of(si * SUB, SUB)
        d = du[pl.ds(r0, SUB), :]
        for k in range(CONV_WIDTH):
            p = d * win[k % 8, pl.ds(r0 + 8 * (k // 8), SUB), :]
            dwacc[8 * k:8 * k + 8, :] += (p[0:8] + p[8:16]) + (p[16:24] + p[24:32])
        return carry

    lax.fori_loop(0, CHUNK // SUB, sub, 0)


def _ln_parts(u):
    mu = jnp.mean(u, axis=-1, keepdims=True)
    xc = u - mu
    rstd = lax.rsqrt(jnp.mean(xc * xc, axis=-1, keepdims=True) + EPS)
    return xc * rstd, rstd


def _conv_fwd(proj, conv_w, conv_b, ln_g, ln_b, carry=None):
    T, C = proj.shape[0], CONV_CHANNELS
    vec = pl.BlockSpec((1, C), lambda i: (0, 0))
    c_in, c_out, c_sems = _carry_io(carry)

    def body(*refs):
        a0, a1, b0, b1, w_ref, cb_ref, g_ref, be_ref = refs[:8]
        ci_refs = refs[8:8 + len(c_in)]
        c_ref, u_ref = refs[8 + len(c_in):10 + len(c_in)]
        co_refs = refs[10 + len(c_in):10 + len(c_in) + len(c_out)]
        zpad, win, ubuf = refs[10 + len(c_in) + len(c_out):13 + len(c_in) + len(c_out)]
        cs_refs = refs[13 + len(c_in) + len(c_out):]
        if carry is not None:
            carry.start(ci_refs, co_refs, cs_refs)
        _glu_to_pad(a0, a1, b0, b1, zpad)
        for ci in range(T // CHUNK):
            _tap_windows(zpad, ci * CHUNK + CONV_PAD - (CONV_WIDTH - 1), win)
            _taps(win, w_ref, cb_ref[...], ubuf, False)
            u = ubuf[...]
            u_ref[ci * CHUNK:(ci + 1) * CHUNK, :] = u
            xh, _ = _ln_parts(u)
            ln = xh * g_ref[...] + be_ref[...]
            c_ref[ci * CHUNK:(ci + 1) * CHUNK, :] = (ln * jax.nn.sigmoid(ln)).astype(BF)
        if carry is not None:
            carry.finish(ci_refs, co_refs, cs_refs)

    res = pl.pallas_call(
        body, name="conv_fwd", grid=(1,),
        in_specs=[*_GLU_SPECS, pl.BlockSpec((CONV_PAD, C), lambda i: (0, 0)), vec, vec, vec, *[ANY] * len(c_in)],
        out_specs=[pl.BlockSpec((T, C), lambda i: (0, 0)), pl.BlockSpec((T, C), lambda i: (0, 0)), *[ANY] * len(c_out)],
        out_shape=[_sds((T, C), BF), _sds((T, C), F32), *c_out],
        scratch_shapes=[pltpu.VMEM((PAD_ROWS, C), F32), pltpu.VMEM((8, WIN, C), F32), pltpu.VMEM((CHUNK, C), F32),
                        *c_sems],
        compiler_params=_params(("arbitrary",)),
    )(proj, proj, proj, proj, conv_w, conv_b, ln_g, ln_b, *c_in)
    return res[:2], res[2:]


def _conv_bwd(proj, u, d_c, conv_w, conv_b, ln_g, ln_b, carry=None):
    T, C = proj.shape[0], CONV_CHANNELS
    vec = pl.BlockSpec((1, C), lambda i: (0, 0))
    wspec = pl.BlockSpec((CONV_PAD, C), lambda i: (0, 0))
    c_in, c_out, c_sems = _carry_io(carry)

    def body(*refs):
        a0, a1, b0, b1, u_ref, dc_ref, w_ref, cb_ref, g_ref, be_ref = refs[:10]
        ci_refs = refs[10:10 + len(c_in)]
        o = 10 + len(c_in)
        dglu_ref, dw_ref, dcb_ref, dg_ref, dbe_ref = refs[o:o + 5]
        co_refs = refs[o + 5:o + 5 + len(c_out)]
        zpad, dupad, win, ubuf, dwacc = refs[o + 5 + len(c_out):o + 10 + len(c_out)]
        cs_refs = refs[o + 10 + len(c_out):]
        if carry is not None:
            carry.start(ci_refs, co_refs, cs_refs)
        _glu_to_pad(a0, a1, b0, b1, zpad)
        dupad[T:, :] = jnp.zeros((2 * CONV_PAD, C), F32)
        dwacc[...] = jnp.zeros_like(dwacc)
        dcb_ref[...] = jnp.zeros_like(dcb_ref)
        dg_ref[...] = jnp.zeros_like(dg_ref)
        dbe_ref[...] = jnp.zeros_like(dbe_ref)
        for ci in range(T // CHUNK):
            rows = slice(ci * CHUNK, (ci + 1) * CHUNK)
            _tap_windows(zpad, ci * CHUNK + CONV_PAD - (CONV_WIDTH - 1), win)
            xh, rstd = _ln_parts(u_ref[rows, :])
            ln = xh * g_ref[...] + be_ref[...]
            sg = jax.nn.sigmoid(ln)
            dln = dc_ref[rows, :].astype(F32) * (sg * (1.0 + ln * (1.0 - sg)))
            dg_ref[...] += jnp.sum(dln * xh, axis=0, keepdims=True)
            dbe_ref[...] += jnp.sum(dln, axis=0, keepdims=True)
            dxh = dln * g_ref[...]
            du = rstd * (dxh - jnp.mean(dxh, axis=-1, keepdims=True)
                         - xh * jnp.mean(dxh * xh, axis=-1, keepdims=True))
            dupad[rows, :] = du
            dcb_ref[...] += jnp.sum(du, axis=0, keepdims=True)
            _tap_grads(win, dupad.at[rows, :], dwacc)
        for k in range(CONV_WIDTH):
            dw_ref[k:k + 1, :] = jnp.sum(dwacc[8 * k:8 * k + 8, :], axis=0, keepdims=True)
        dw_ref[CONV_WIDTH:, :] = jnp.zeros((CONV_PAD - CONV_WIDTH, C), F32)
        for ci in range(T // CHUNK):
            rows = slice(ci * CHUNK, (ci + 1) * CHUNK)
            _tap_windows(dupad, ci * CHUNK, win)
            _taps(win, w_ref, jnp.zeros((1, C), F32), ubuf, True)
            dz = ubuf[...]
            for half, (a, b) in enumerate(((a0, b0), (a1, b1))):
                sb = jax.nn.sigmoid(b[rows, :].astype(F32))
                dzh = dz[:, half * 256:(half + 1) * 256]
                dglu_ref[rows, half * 256:(half + 1) * 256] = (dzh * sb).astype(BF)
                dglu_ref[rows, C + half * 256:C + (half + 1) * 256] = (
                    dzh * a[rows, :].astype(F32) * sb * (1.0 - sb)).astype(BF)
        if carry is not None:
            carry.finish(ci_refs, co_refs, cs_refs)

    res = pl.pallas_call(
        body, name="conv_bwd", grid=(1,),
        in_specs=[*_GLU_SPECS, pl.BlockSpec((T, C), lambda i: (0, 0)), pl.BlockSpec((T, C), lambda i: (0, 0)), wspec,
                  vec, vec, vec, *[ANY] * len(c_in)],
        out_specs=[pl.BlockSpec((T, 2 * C), lambda i: (0, 0)), wspec, vec, vec, vec, *[ANY] * len(c_out)],
        out_shape=[_sds((T, 2 * C), BF), _sds((CONV_PAD, C), F32), _sds((1, C), F32), _sds((1, C), F32),
                   _sds((1, C), F32), *c_out],
        scratch_shapes=[pltpu.VMEM((PAD_ROWS, C), F32), pltpu.VMEM((PAD_ROWS, C), F32), pltpu.VMEM((8, WIN, C), F32),
                        pltpu.VMEM((CHUNK, C), F32), pltpu.VMEM((8 * CONV_PAD, C), F32), *c_sems],
        compiler_params=_params(("arbitrary",)),
    )(proj, proj, proj, proj, u, d_c, conv_w, conv_b, ln_g, ln_b, *c_in)
    return res[:5], res[5:]


_GATE_BLK = GATE_OFF // 256


def _ffn_in_swiglu(h2, wf_t, carry=None):
    T, D = h2.shape
    tm, tn = 1024, D_FF // 2
    nj, ni = D_FF // tn, T // tm
    c_in, c_out, c_sems = _carry_io(carry)

    def body(*refs):
        a_ref, bg_ref, bu_ref = refs[:3]
        ci_refs = refs[3:3 + len(c_in)]
        act_ref, g_ref, u_ref = refs[3 + len(c_in):6 + len(c_in)]
        co_refs = refs[6 + len(c_in):6 + len(c_in) + len(c_out)]
        cs_refs = refs[6 + len(c_in) + len(c_out):]
        j, i = pl.program_id(0), pl.program_id(1)
        if carry is not None:
            @pl.when((j == 0) & (i == 0))
            def _():
                carry.start(ci_refs, co_refs, cs_refs)
        a = a_ref[...]
        for c0, c1 in ((0, 768), (768, tn)):
            g = lax.dot_general(a, bg_ref[c0:c1, :], _DIMS["NT"], preferred_element_type=F32)
            u = lax.dot_general(a, bu_ref[c0:c1, :], _DIMS["NT"], preferred_element_type=F32)
            act_ref[:, c0:c1] = (g * jax.nn.sigmoid(g) * u).astype(BF)
            g_ref[:, c0:c1] = g.astype(BF)
            u_ref[:, c0:c1] = u.astype(BF)
        if carry is not None:
            @pl.when((j == nj - 1) & (i == ni - 1))
            def _():
                carry.finish(ci_refs, co_refs, cs_refs)

    t = pl.BlockSpec((tm, tn), lambda j, i: (i, j))
    res = pl.pallas_call(
        body, name="ffn_in_swiglu", grid=(nj, ni),
        in_specs=[pl.BlockSpec((tm, D), lambda j, i: (i, 0)), pl.BlockSpec((tn, D), lambda j, i: (j, 0)),
                  pl.BlockSpec((tn, D), lambda j, i: (nj + j, 0)), *[ANY] * len(c_in)],
        out_specs=[t, t, t, *[ANY] * len(c_out)], out_shape=[*[_sds((T, D_FF), BF)] * 3, *c_out],
        scratch_shapes=c_sems,
        compiler_params=_params(("arbitrary", "arbitrary")),
    )(h2, wf_t, wf_t, *c_in)
    return res[:3], res[3:]


def _proj_merge(o, c, wap_t, wcp_t, b_cp, proj):
    T, D = o.shape[0], wap_t.shape[0]
    tm, tg = T, 256
    nj = D // tg

    def body(o_ref, c_ref, wa_ref, wc_ref, b_ref, g0_ref, g1_ref, ya_ref, yc_ref, m_ref):
        ya = lax.dot_general(o_ref[...], wa_ref[...], _DIMS["NT"], preferred_element_type=F32)
        yc = lax.dot_general(c_ref[...], wc_ref[...], _DIMS["NT"], preferred_element_type=F32) + b_ref[...]
        ya_ref[...] = ya.astype(BF)
        yc_ref[...] = yc.astype(BF)
        m_ref[...] = (jax.nn.sigmoid(g0_ref[...].astype(F32)) * ya + jax.nn.sigmoid(g1_ref[...].astype(F32)) * yc).astype(BF)

    act = pl.BlockSpec((tm, o.shape[1]), lambda j, i: (i, 0))
    wgt = pl.BlockSpec((tg, o.shape[1]), lambda j, i: (j, 0))
    t = pl.BlockSpec((tm, tg), lambda j, i: (i, j))
    return pl.pallas_call(
        body, name="proj_merge", grid=(nj, T // tm),
        in_specs=[act, act, wgt, wgt, pl.BlockSpec((1, tg), lambda j, i: (0, j)),
                  pl.BlockSpec((tm, tg), lambda j, i: (i, _GATE_BLK + j)),
                  pl.BlockSpec((tm, tg), lambda j, i: (i, _GATE_BLK + nj + j))],
        out_specs=[t, t, t], out_shape=[_sds((T, D), BF)] * 3,
        compiler_params=_params(("arbitrary", "arbitrary")),
    )(o, c, wap_t, wcp_t, b_cp, proj, proj)


def _branch_proj_bwd(dya, dyc, wap_t, wcp_t):
    T, D = dya.shape
    W = wap_t.shape[1]
    tm = 512

    def body(a_ref, c_ref, wa_ref, wc_ref, do_ref, dc_ref):
        do_ref[...] = lax.dot_general(a_ref[...], wa_ref[...], _DIMS["NN"], preferred_element_type=F32).astype(BF)
        dc_ref[...] = lax.dot_general(c_ref[...], wc_ref[...], _DIMS["NN"], preferred_element_type=F32).astype(BF)

    act = pl.BlockSpec((tm, D), lambda i: (i, 0))
    wgt = pl.BlockSpec((D, W), lambda i: (0, 0))
    out = pl.BlockSpec((tm, W), lambda i: (i, 0))
    return pl.pallas_call(
        body, name="branch_proj_bwd", grid=(T // tm,), in_specs=[act, act, wgt, wgt], out_specs=[out, out],
        out_shape=[_sds((T, W), BF)] * 2, compiler_params=_params(("arbitrary",)),
    )(dya, dyc, wap_t, wcp_t)


def _stacked_dw(name, segs, h, tb):
    T, D = h.shape
    nblk = [seg.shape[1] // tb for seg in segs]
    starts = [sum(nblk[:q]) for q in range(len(segs))]
    n_seg = len(segs)

    def body(*refs):
        seg_refs, h_ref, o_ref, cs_ref = refs[:n_seg], refs[n_seg], refs[n_seg + 1], refs[n_seg + 2]
        i = pl.program_id(0)
        for seg_ref, st, nb in zip(seg_refs, starts, nblk):
            @pl.when((i >= st) & (i < st + nb))
            def _(seg_ref=seg_ref):
                a = seg_ref[...]
                o_ref[...] = lax.dot_general(a, h_ref[...], _DIMS["TN"], preferred_element_type=F32).astype(BF)
                cs_ref[...] = jnp.sum(a.astype(F32), axis=0, keepdims=True)

    in_specs = [pl.BlockSpec((T, tb), functools.partial(lambda i, st, nb: (0, jnp.clip(i - st, 0, nb - 1)), st=st, nb=nb))
                for st, nb in zip(starts, nblk)]
    return pl.pallas_call(
        body, name=name, grid=(sum(nblk),),
        in_specs=[*in_specs, pl.BlockSpec((T, D), lambda i: (0, 0))],
        out_specs=[pl.BlockSpec((tb, D), lambda i: (i, 0)), pl.BlockSpec((1, tb), lambda i: (0, i))],
        out_shape=[_sds((sum(nblk) * tb, D), BF), _sds((1, sum(nblk) * tb), F32)],
        compiler_params=_params(("arbitrary",)),
    )(*segs, h)


def _local_step(x, h, r1, target, small, wi_t, conv_w, plan):
    T, D = x.shape
    tm = 1024

    def carried(call, res, carry):
        if carry is None:
            return res
        outs, got = res
        plan.done(call, got)
        return outs


    def ep_add(acc, ex, outs, ids, scr):
        outs[0][...] = acc + ex[0][...]

    tn_in = IN_WIDTH // 2
    carry = plan.carry("proj_in")
    def ep_bias_bf16(acc, ex, outs, ids, scr):
        outs[0][...] = (acc + ex[0][...]).astype(BF)

    proj, = carried("proj_in", _matmul("proj_in", [h], wi_t, "NT", m=T, n=IN_WIDTH, tm=tm, tn=tn_in,
                                       epilogue=ep_bias_bf16, extra=[(small["b_in"], _row(tn_in))],
                                       outs=[(_sds((T, IN_WIDTH), BF), _tile(tm, tn_in))], carry=carry), carry)
    plan.launch("gather_ffn", after=proj)
    o, got = _attn_fwd(proj, small["sinks"], carry=plan.carry("attn_fwd"))
    plan.done("attn_fwd", got)
    (c, u_conv), got = _conv_fwd(proj, conv_w, small["conv_b"], small["ln_g"], small["ln_b"],
                                 carry=plan.carry("conv_fwd"))
    plan.done("conv_fwd", got)
    wap_t, wcp_t, w_out = plan.weight("w_attn_proj"), plan.weight("w_conv_proj"), plan.weight("w_out")
    ya, yc, merged = _proj_merge(o, c, wap_t, wcp_t, small["b_conv_proj"], proj)

    tg = 256
    gate_specs = [pl.BlockSpec((T, tg), lambda j, i, k: (i, _GATE_BLK + j)),
                  pl.BlockSpec((T, tg), lambda j, i, k: (i, _GATE_BLK + D // tg + j))]

    def ep_residual_rms(acc, ex, outs, ids, scr):
        x2v = acc + ex[0][...]
        r = lax.rsqrt(jnp.mean(x2v * x2v, axis=-1, keepdims=True) + EPS)
        outs[0][...] = x2v
        outs[1][...] = (x2v * r * ex[1][...]).astype(BF)
        outs[2][...] = r

    carry = plan.carry("out_proj")
    x2, h2, r2 = carried("out_proj", _matmul(
        "out_proj_rms", [merged], w_out, "NN", m=T, n=D, tm=512, tn=D, epilogue=ep_residual_rms,
        extra=[(x, _tile(512, D)), (small["g_ffn_norm"], _row(D))],
        outs=[(_sds((T, D), F32), _tile(512, D)), (_sds((T, D), BF), _tile(512, D)),
              (_sds((T, 1), F32), pl.BlockSpec((512, 1), lambda j, i, k: (i, 0)))], carry=carry), carry)
    plan.launch("gather_down", after=x2)
    wf_t = plan.weight("w_ffn_in")
    (act, gate, up), got = _ffn_in_swiglu(h2, wf_t, carry=plan.carry("ffn_in_swiglu"))
    plan.done("ffn_in_swiglu", got)
    w_down = plan.weight("w_ffn_down")
    def ep_residual_loss(acc, ex, outs, ids, scr):
        dx, dg, part = _loss_head(acc + ex[0][...], ex[1][...], ex[2][...])
        outs[0][...] = dx
        outs[1][...] = dx.astype(BF)
        _accumulate_rows(outs[2], dg, ids[1] == 0)
        _accumulate_rows(outs[3], part, ids[1] == 0)

    dx3, dx3_b, dg_final, loss = _matmul(
        "ffn_down_loss", [act], w_down, "NN", m=T, n=D, tm=512, tn=D, epilogue=ep_residual_loss,
        extra=[(x2, _tile(512, D)), (small["g_final"], _row(D)), (target, _tile(512, D))],
        outs=[(_sds((T, D), F32), _tile(512, D)), (_sds((T, D), BF), _tile(512, D)), (_sds((1, D), F32), _row(D)),
              (_sds((1, 1), F32), pl.BlockSpec((1, 1), lambda j, i, k: (0, 0)))])

    tn_ff = D_FF // 2

    def ep_swiglu_bwd(acc, ex, outs, ids, scr):
        g, u = ex[0][...].astype(F32), ex[1][...].astype(F32)
        sg = jax.nn.sigmoid(g)
        outs[0][...] = (acc * u * sg * (1.0 + g * (1.0 - sg))).astype(BF)
        outs[1][...] = (acc * g * sg).astype(BF)

    dgate, dup = _matmul(
        "ffn_down_bwd", [dx3_b], w_down, "NT", m=T, n=D_FF, tm=tm, tn=tn_ff, epilogue=ep_swiglu_bwd,
        extra=[(gate, _tile(tm, tn_ff)), (up, _tile(tm, tn_ff))],
        outs=[(_sds((T, D_FF), BF), _tile(tm, tn_ff)), (_sds((T, D_FF), BF), _tile(tm, tn_ff))])

    def dw(name, a, b, rows, cols, row_off=0, alias=None, total_rows=None, colsum=False):
        total_rows = rows if total_rows is None else total_rows
        tmw = rows if rows <= 1024 else D_FF // 2
        blk, rem = divmod(row_off, tmw)
        assert rem == 0

        def ep(acc, ex, outs, ids, scr):
            outs[0][...] = acc.astype(BF)
            if colsum:
                outs[1][...] = jnp.sum(ex[0][...].astype(F32), axis=0, keepdims=True)

        outs = [(_sds((total_rows, cols), BF), pl.BlockSpec((tmw, cols), lambda j, i, k: (blk + i, j)))]
        extra = []
        if colsum:
            extra = [(a, pl.BlockSpec((T, tmw), lambda j, i, k: (0, i)))]
            outs.append((_sds((1, rows), F32), pl.BlockSpec((1, tmw), lambda j, i, k: (0, i))))
        carry = plan.carry(name)
        res = carried(name, _matmul(name, [a], b, "TN", m=rows, n=cols, tm=tmw, tn=cols, epilogue=ep, extra=extra,
                                    outs=outs, alias=None if alias is None else (alias, 0), carry=carry), carry)
        return res if colsum else res[0]

    plan.grad_ready(dict(w_ffn_down=dw("ffn_down_dw", act, dx3_b, D_FF, D)))

    def ep_rms_bwd(acc, ex, outs, ids, scr):
        dx, dg = _rms_bwd(acc, ex[0][...], ex[1][...], ex[2][...])
        dx = ex[3][...] + dx
        outs[0][...] = dx
        outs[1][...] = dx.astype(BF)
        _accumulate_rows(outs[2], dg, ids[1] == 0)

    def rms_bwd_io(tm_, xin, r, g, dres):
        return dict(
            extra=[(xin, _tile(tm_, D)), (r, pl.BlockSpec((tm_, 1), lambda j, i, k: (i, 0))), (g, _row(D)),
                   (dres, _tile(tm_, D))],
            outs=[(_sds((T, D), F32), _tile(tm_, D)), (_sds((T, D), BF), _tile(tm_, D)), (_sds((1, D), F32), _row(D))])

    carry = plan.carry("ffn_in_bwd")
    dx2, dx2_b, dg_ffn = carried(
        "ffn_in_bwd",
        _matmul("ffn_in_bwd", [dgate, dup], wf_t, "NN", m=T, n=D, tm=tm, tn=D, tk=D_FF // 2, epilogue=ep_rms_bwd,
                carry=carry, **rms_bwd_io(tm, x2, r2, small["g_ffn_norm"], dx3)), carry)
    plan.launch("send_down")
    gwf_t, _ = _stacked_dw("ffn_in_dw", [dgate, dup], h2, D_FF // 2)
    plan.grad_ready(dict(w_ffn_in=gwf_t))

    def ep_merge_bwd(acc, ex, outs, ids, scr):
        s0 = jax.nn.sigmoid(ex[2][...].astype(F32))
        s1 = jax.nn.sigmoid(ex[3][...].astype(F32))
        outs[0][...] = (acc * s0).astype(BF)
        outs[1][...] = (acc * s1).astype(BF)
        outs[2][...] = (acc * ex[0][...].astype(F32) * s0 * (1.0 - s0)).astype(BF)
        outs[3][...] = (acc * ex[1][...].astype(F32) * s1 * (1.0 - s1)).astype(BF)

    carry = plan.carry("out_proj_bwd_merge")
    dya, dyc, dg0, dg1 = carried(
        "out_proj_bwd_merge",
        _matmul("out_proj_bwd_merge", [dx2_b], w_out, "NT", m=T, n=D, tm=T, tn=tg, epilogue=ep_merge_bwd,
                extra=[(ya, _tile(T, tg)), (yc, _tile(T, tg)), (proj, gate_specs[0]), (proj, gate_specs[1])],
                outs=[(_sds((T, D), BF), _tile(T, tg))] * 4, carry=carry), carry)
    plan.launch("send_ffn")
    gw_out = dw("out_proj_dw", merged, dx2_b, D, D)
    d_o, d_c = _branch_proj_bwd(dya, dyc, wap_t, wcp_t)
    gwap_t = dw("attn_proj_dw", dya, o, D, ATTN_WIDTH)
    gwcp_t, db_cp = dw("conv_proj_dw", dyc, c, D, CONV_CHANNELS, colsum=True)
    plan.grad_ready(dict(w_out=gw_out, w_attn_proj=gwap_t, w_conv_proj=gwcp_t))
    (dglu, dcw, dcb, dlng, dlnb), got = _conv_bwd(proj, u_conv, d_c, conv_w, small["conv_b"], small["ln_g"],
                                                  small["ln_b"], carry=plan.carry("conv_bwd"))
    plan.done("conv_bwd", got)
    plan.launch("send_mix")
    (dqkv, dsinks), got = _attn_bwd(proj, d_o, small["sinks"], carry=plan.carry("attn_bwd"))
    plan.done("attn_bwd", got)

    segs = [dqkv, dglu, dg0, dg1]
    gwi_t, db_in = _stacked_dw("proj_in_dw", segs, h, 256)
    plan.grad_ready(dict(w_in=gwi_t))
    plan.alone("swap_inp")
    plan.launch("send_inp")
    carry = plan.carry("proj_in_bwd")
    dx, _, dg_mix = carried(
        "proj_in_bwd",
        _matmul("proj_in_bwd", segs, wi_t, "NN", m=T, n=D, tm=512, tn=D, epilogue=ep_rms_bwd, carry=carry,
                **rms_bwd_io(512, x, r1, small["g_mix_norm"], plan.behind("inp", dx2))), carry)

    parts = dict(g_mix_norm=dg_mix, b_in=db_in, sinks=dsinks, conv_w=dcw, conv_b=dcb, ln_g=dlng, ln_b=dlnb,
                 b_conv_proj=db_cp, g_ffn_norm=dg_ffn, g_final=dg_final, loss=loss)
    return dx, parts


def _place():
    x, y, c = lax.axis_index("x"), lax.axis_index("y"), lax.axis_index("c")
    return x, y, c, [(1 - x, y), (x, 1 - y), (1 - x, 1 - y)]


def _gather_copies(x_refs, out_refs, rows_per, send_sems, recv_sems, local_sems):
    x, y, c, chips = _place()
    me, sibling = (x, y, c), (x, y, 1 - c)

    def rows(a, px, py, pc):
        return out_refs[a].at[pl.ds((4 * px + 2 * py + pc) * rows_per[a], rows_per[a])]

    def copy(a, k, block, to, src=None):
        return pltpu.make_async_remote_copy(
            src_ref=rows(a, *block) if src is None else src, dst_ref=rows(a, *block),
            send_sem=send_sems.at[7 * a + k], recv_sem=recv_sems.at[7 * a + k], device_id=to, device_id_type=MESH)

    def local(a):
        return pltpu.make_async_copy(x_refs[a], rows(a, *me), local_sems.at[a])

    def first(a):
        return [copy(a, 0, me, sibling, src=x_refs[a])] + [copy(a, 1 + j, me, (*chip, c), src=x_refs[a])
                                                          for j, chip in enumerate(chips)]

    def arrive(a, j):
        return copy(a, 1 + j, (*chips[j], c), me)

    def passed(a, j):
        return copy(a, 4 + j, (*chips[j], c), sibling)

    def from_sibling(a):
        return [copy(a, 0, sibling, me)] + [copy(a, 4 + j, (*chip, 1 - c), me) for j, chip in enumerate(chips)]

    return len(x_refs), local, first, arrive, passed, from_sibling


def _gather_start(*refs):
    n, local, first, _, _, _ = _gather_copies(*refs)
    for a in range(n):
        local(a).start()
        for cp in first(a):
            cp.start()


def _gather_finish(*refs):
    n, local, first, arrive, passed, from_sibling = _gather_copies(*refs)
    for a in range(n):
        for j in range(3):
            arrive(a, j).wait_recv()
            passed(a, j).start()
    for a in range(n):
        for cp in from_sibling(a):
            cp.wait_recv()
    for a in range(n):
        for cp in first(a) + [passed(a, j) for j in range(3)]:
            cp.wait_send()
        local(a).wait()


def _gather_peers():
    x, y, c, chips = _place()
    return [(x, y, 1 - c)] + [(*chip, c) for chip in chips]


def _gather_sems(n):
    return [pltpu.SemaphoreType.DMA((7 * n,)), pltpu.SemaphoreType.DMA((7 * n,)), pltpu.SemaphoreType.DMA((n,))]


def _gather_carry(shards):
    rows_per = [s.shape[0] for s in shards]
    return _Carry(shards, [_sds((N_DEV * s.shape[0],) + s.shape[1:], s.dtype) for s in shards],
                  _gather_sems(len(shards)),
                  lambda ins, outs, sems: _gather_start(ins, outs, rows_per, *sems),
                  lambda ins, outs, sems: _gather_finish(ins, outs, rows_per, *sems), _gather_peers)


def _first_gather(shards, x, g):
    n = len(shards)
    rows_per = [s.shape[0] for s in shards]
    T, D = x.shape

    def body(*refs):
        x_refs, (xin_ref, g_ref), out_refs, (h_ref, r_ref) = refs[:n], refs[n:n + 2], refs[n + 2:2 * n + 2], refs[2 * n + 2:2 * n + 4]
        send_sems, recv_sems, local_sems = refs[2 * n + 4:]
        x, y, c, chips = _place()
        me, sibling = (x, y, c), (x, y, 1 - c)
        near_x, near_y, far = (*chips[0], c), (*chips[1], c), (*chips[2], c)

        def rows(a, dev, part):
            h = rows_per[a] // 2
            lo, size = {"all": (0, 2 * h), "low": (0, h), "high": (h, h)}[part]
            return out_refs[a].at[pl.ds((4 * dev[0] + 2 * dev[1] + dev[2]) * rows_per[a] + lo, size)]

        def copy(a, k, block, part, to, src=None):
            return pltpu.make_async_remote_copy(
                src_ref=rows(a, block, part) if src is None else src, dst_ref=rows(a, block, part),
                send_sem=send_sems.at[9 * a + k], recv_sem=recv_sems.at[9 * a + k], device_id=to, device_id_type=MESH)

        other = lambda dev: (dev[0], dev[1], 1 - c)
        sent = []
        for a in range(n):
            pltpu.make_async_copy(x_refs[a], rows(a, me, "all"), local_sems.at[a]).start()
            sent += [copy(a, 0, me, "all", sibling, src=x_refs[a]), copy(a, 1, me, "all", near_x, src=x_refs[a]),
                     copy(a, 2, me, "all", near_y, src=x_refs[a])]
        for cp in sent:
            cp.start()
        for i in range(T // CHUNK):
            rws = slice(i * CHUNK, (i + 1) * CHUNK)
            xv = xin_ref[rws, :]
            r = lax.rsqrt(jnp.mean(xv * xv, axis=-1, keepdims=True) + EPS)
            h_ref[rws, :] = (xv * r * g_ref[...]).astype(BF)
            r_ref[rws, :] = r
        for a in range(n):
            copy(a, 1, near_x, "all", me).wait_recv()
            copy(a, 2, near_y, "all", me).wait_recv()
            passed = [copy(a, 3, near_y, "high", near_x), copy(a, 4, near_x, "low", near_y),
                      copy(a, 5, near_x, "all", sibling), copy(a, 6, near_y, "all", sibling)]
            for cp in passed:
                cp.start()
            sent += passed
        for a in range(n):
            copy(a, 3, far, "high", me).wait_recv()
            copy(a, 4, far, "low", me).wait_recv()
            passed = [copy(a, 7, far, "high", sibling), copy(a, 8, far, "low", sibling)]
            for cp in passed:
                cp.start()
            sent += passed
        for a in range(n):
            copy(a, 0, sibling, "all", me).wait_recv()
            copy(a, 5, other(near_x), "all", me).wait_recv()
            copy(a, 6, other(near_y), "all", me).wait_recv()
            copy(a, 7, other(far), "high", me).wait_recv()
            copy(a, 8, other(far), "low", me).wait_recv()
        for cp in sent:
            cp.wait_send()
        for a in range(n):
            pltpu.make_async_copy(x_refs[a], rows(a, me, "all"), local_sems.at[a]).wait()

    vm = pl.BlockSpec(memory_space=pltpu.VMEM)
    return pl.pallas_call(
        body, name="weights_first_gather", in_specs=[*[ANY] * n, vm, vm], out_specs=[*[ANY] * n, vm, vm],
        out_shape=[*[_sds((N_DEV * s.shape[0],) + s.shape[1:], s.dtype) for s in shards], _sds((T, D), BF),
                   _sds((T, 1), F32)],
        scratch_shapes=[pltpu.SemaphoreType.DMA((9 * n,)), pltpu.SemaphoreType.DMA((9 * n,)),
                        pltpu.SemaphoreType.DMA((n,))],
        compiler_params=pltpu.CompilerParams(vmem_limit_bytes=VMEM_LIMIT_BYTES),
    )(*shards, x, g)


def _swap_carry(grads):
    n = len(grads)

    def copies(g_refs, out_refs, sems):
        send_sems, recv_sems = sems
        x, y, c, _ = _place()
        return [pltpu.make_async_remote_copy(
            src_ref=g_refs[a].at[2 * p + 1 - c], dst_ref=out_refs[a].at[p],
            send_sem=send_sems.at[4 * a + p], recv_sem=recv_sems.at[4 * a + p],
            device_id=(x, y, 1 - c), device_id_type=MESH) for a in range(n) for p in range(4)]

    def start(ins, outs, sems):
        for cp in copies(ins, outs, sems):
            cp.start()

    def finish(ins, outs, sems):
        for cp in copies(ins, outs, sems):
            cp.wait()

    def peers():
        x, y, c, _ = _place()
        return [(x, y, 1 - c)]

    return _Carry(grads, [_sds((4,) + g.shape[1:], g.dtype) for g in grads],
                  [pltpu.SemaphoreType.DMA((4 * n,)), pltpu.SemaphoreType.DMA((4 * n,))], start, finish, peers)


def _join(carries):
    carries = [c for c in carries if c is not None]
    if not carries:
        return None
    n_in = [len(c.arrays) for c in carries]
    n_out = [len(c.out_shapes) for c in carries]
    n_sem = [len(c.sems) for c in carries]

    def parts(refs, counts):
        cuts = [sum(counts[:q]) for q in range(len(counts) + 1)]
        return [refs[cuts[q]:cuts[q + 1]] for q in range(len(counts))]

    def start(ins, outs, sems):
        for c, i, o, s in zip(carries, parts(ins, n_in), parts(outs, n_out), parts(sems, n_sem)):
            c.start(i, o, s)

    def finish(ins, outs, sems):
        for c, i, o, s in zip(carries, parts(ins, n_in), parts(outs, n_out), parts(sems, n_sem)):
            c.finish(i, o, s)

    return _Carry([a for c in carries for a in c.arrays], [o for c in carries for o in c.out_shapes],
                  [s for c in carries for s in c.sems], start, finish)


def _run_carry(name, carry):
    n_in, n_out = len(carry.arrays), len(carry.out_shapes)

    def body(*refs):
        carry.start(refs[:n_in], refs[n_in:n_in + n_out], refs[n_in + n_out:])
        carry.finish(refs[:n_in], refs[n_in:n_in + n_out], refs[n_in + n_out:])

    return pl.pallas_call(body, name=name, in_specs=[ANY] * n_in, out_specs=[ANY] * n_out,
                          out_shape=carry.out_shapes, scratch_shapes=carry.sems)(*carry.arrays)


def _run_carry_async(name, carry, collective_id):
    ins = [jax.new_ref(a, memory_space=pltpu.MemorySpace.HBM) for a in carry.arrays]
    outs = [jax.empty_ref(o, memory_space=pltpu.MemorySpace.HBM) for o in carry.out_shapes]

    @pl.kernel(mesh=plsc.ScalarSubcoreMesh(axis_name="sequencer", num_cores=1), name=name,
               scratch_types=tuple(carry.sems), compiler_params=pltpu.CompilerParams(collective_id=collective_id))
    def launch(*sems):
        barrier = pltpu.get_barrier_semaphore()
        peers = carry.peers()
        for peer in peers:
            pl.semaphore_signal(barrier, inc=1, device_id=peer, device_id_type=MESH)
        pl.semaphore_wait(barrier, len(peers))
        carry.start(ins, outs, sems)
        carry.finish(ins, outs, sems)

    launch()
    return [o[...] for o in outs]


def _chip_sums(name, gs, gots, c):
    n = len(gs)

    def body(c_ref, *refs):
        for g_ref, got_ref, o_ref in zip(refs[:n], refs[n:2 * n], refs[2 * n:]):
            o_ref[...] = (g_ref[...].astype(F32) + got_ref[...].astype(F32)).astype(BF)

    mine = [pl.BlockSpec((1,) + g.shape[1:], lambda p, c_ref: (2 * p + c_ref[0], 0, 0)) for g in gs]
    slot = [pl.BlockSpec((1,) + g.shape[1:], lambda p, c_ref: (p, 0, 0)) for g in gs]
    return pl.pallas_call(
        body, name=name,
        grid_spec=pltpu.PrefetchScalarGridSpec(num_scalar_prefetch=1, grid=(4,), in_specs=[*mine, *slot],
                                               out_specs=slot),
        out_shape=[_sds((4,) + g.shape[1:], BF) for g in gs],
        compiler_params=_params(("arbitrary",)),
    )(c, *gs, *gots)


def _send_carry(sums, ks):
    n, nk = len(sums), len(ks)

    def copies(s_refs, out_refs, sems):
        send_sems, recv_sems = sems
        x, y, c, chips = _place()
        return [pltpu.make_async_remote_copy(
            src_ref=s_refs[a].at[2 * chips[k][0] + chips[k][1]], dst_ref=out_refs[a].at[q],
            send_sem=send_sems.at[nk * a + q], recv_sem=recv_sems.at[nk * a + q],
            device_id=(*chips[k], c), device_id_type=MESH) for a in range(n) for q, k in enumerate(ks)]

    def start(ins, outs, sems):
        for cp in copies(ins, outs, sems):
            cp.start()

    def finish(ins, outs, sems):
        for cp in copies(ins, outs, sems):
            cp.wait()

    def peers():
        x, y, c, chips = _place()
        return [(*chips[k], c) for k in ks]

    return _Carry(sums, [_sds((nk,) + s.shape[1:], s.dtype) for s in sums],
                  [pltpu.SemaphoreType.DMA((nk * n,)), pltpu.SemaphoreType.DMA((nk * n,))], start, finish, peers)


def _adam_math(w, g, m, v):
    m = ADAM_B1 * m + (1.0 - ADAM_B1) * g
    v = ADAM_B2 * v + (1.0 - ADAM_B2) * (g * g)
    m_hat = m / (1.0 - ADAM_B1 ** ADAM_STEP)
    v_hat = v / (1.0 - ADAM_B2 ** ADAM_STEP)
    delta = -ADAM_LR * (m_hat / (jnp.sqrt(v_hat) + ADAM_EPS) + ADAM_WD * w)
    return delta, m, v


def _adamw(name, w, g, m, v):
    rows, cols = w.shape
    tr = 256 if rows % 256 == 0 else rows

    def body(w_ref, g_ref, m_ref, v_ref, d_ref, nm_ref, nv_ref):
        d_ref[...], nm_ref[...], nv_ref[...] = _adam_math(w_ref[...], g_ref[...], m_ref[...], v_ref[...])

    t = pl.BlockSpec((tr, cols), lambda i: (i, 0))
    return pl.pallas_call(
        body, name=name, grid=(rows // tr,), in_specs=[t] * 4, out_specs=[t] * 3,
        out_shape=[_sds((rows, cols), F32)] * 3, compiler_params=_params(("arbitrary",)),
    )(w, g, m, v)


def _grad_adamw(name, g, got, got3, ids, w, m, v):
    _, rows, cols = g.shape
    n3 = len(got3)
    tr = rows // 2 if rows >= 256 else rows

    def body(ids_ref, g_ref, got_ref, *rest):
        w_ref, m_ref, v_ref, o_ref, d_ref, nm_ref, nv_ref = rest[n3:]
        tot = g_ref[0].astype(F32) + got_ref[0].astype(F32)
        for r_ref in rest[:n3]:
            for q in range(r_ref.shape[0]):
                tot = tot + r_ref[q].astype(F32)
        o_ref[...] = tot
        d_ref[...], nm_ref[...], nv_ref[...] = _adam_math(w_ref[...], tot, m_ref[...], v_ref[...])

    tile = pl.BlockSpec((tr, cols), lambda i, ids_ref: (i, 0))
    return pl.pallas_call(
        body, name=name,
        grid_spec=pltpu.PrefetchScalarGridSpec(
            num_scalar_prefetch=1, grid=(rows // tr,),
            in_specs=[pl.BlockSpec((1, tr, cols), lambda i, ids_ref: (ids_ref[0], i, 0)),
                      pl.BlockSpec((1, tr, cols), lambda i, ids_ref: (ids_ref[1], i, 0)),
                      *[pl.BlockSpec((r.shape[0], tr, cols), lambda i, ids_ref: (0, i, 0)) for r in got3],
                      tile, tile, tile],
            out_specs=[tile] * 4),
        out_shape=[_sds((rows, cols), F32)] * 4,
        compiler_params=_params(("arbitrary",)),
    )(ids, g, got, *got3, w, m, v)


SMALL_NAMES = ["g_mix_norm", "b_in", "sinks", "conv_b", "ln_g", "ln_b", "b_conv_proj", "g_ffn_norm", "g_final"]
_PACK_ROWS = 32


def _small_pack(parts):
    C = CONV_CHANNELS
    part_list = [parts["g_mix_norm"], parts["b_in"], parts["sinks"], parts["conv_b"], parts["ln_g"], parts["ln_b"],
                 parts["b_conv_proj"], parts["g_ffn_norm"], parts["g_final"], parts["loss"], parts["conv_w"]]

    def body(p_mix, p_b, p_sink, p_cb, p_lg, p_lb, p_bcp, p_ffn, p_fin, p_loss, p_cw, pack):
        pack[...] = jnp.zeros_like(pack)
        pack[0:1, :] = p_mix[...]
        pack[1:2, 0:GLU_OFF] = p_b[:, 0:GLU_OFF]
        pack[2:3, :] = p_b[:, GLU_OFF:GATE_OFF]
        pack[3:4, :] = p_b[:, GATE_OFF:GATE_OFF + D_MODEL]
        pack[4:5, :] = p_b[:, GATE_OFF + D_MODEL:]
        pack[5:6, 0:128] = p_sink[...]
        pack[6:7, 0:C] = p_cb[...]
        pack[6:7, C:2 * C] = p_lg[...]
        pack[7:8, 0:C] = p_lb[...]
        pack[8:9, :] = p_bcp[...]
        pack[9:10, :] = p_ffn[...]
        pack[10:11, :] = p_fin[...]
        pack[11:12, 0:128] = jnp.broadcast_to(p_loss[...], (1, 128))
        pack[12:28, 0:C] = p_cw[0:16, :]
        pack[12:28, C:2 * C] = p_cw[16:32, :]

    vm = pl.BlockSpec(memory_space=pltpu.VMEM)
    return pl.pallas_call(body, name="small_pack", in_specs=[vm] * len(part_list), out_specs=vm,
                          out_shape=_sds((_PACK_ROWS, D_MODEL), F32))(*part_list)


def _small_adamw(gathered, small_w, small_m, small_v):
    C = CONV_CHANNELS
    names = SMALL_NAMES
    widths = [small_w[k].shape[1] for k in names]
    n_small = len(names)

    def body(*refs):
        tot_ref = refs[0]
        w_refs = refs[1:1 + n_small]
        m_refs = refs[1 + n_small:1 + 2 * n_small]
        v_refs = refs[1 + 2 * n_small:1 + 3 * n_small]
        o = 1 + 3 * n_small
        loss_ref, cw_ref = refs[o], refs[o + 1]
        out_refs = refs[o + 2:o + 2 + 4 * n_small]
        tot = tot_ref[0:_PACK_ROWS, :]
        for d in range(1, N_DEV):
            tot = tot + tot_ref[d * _PACK_ROWS:(d + 1) * _PACK_ROWS, :]
        loss_ref[...] = tot[11:12, 0:1]
        cw_ref[0:16, :] = tot[12:28, 0:C]
        cw_ref[16:32, :] = tot[12:28, C:2 * C]
        grads = dict(
            g_mix_norm=tot[0:1, :],
            b_in=jnp.concatenate([tot[1:2, 0:GLU_OFF], tot[2:3, :], tot[3:4, :], tot[4:5, :]], axis=1),
            sinks=tot[5:6, 0:N_Q_HEADS], conv_b=tot[6:7, 0:C], ln_g=tot[6:7, C:2 * C], ln_b=tot[7:8, 0:C],
            b_conv_proj=tot[8:9, :], g_ffn_norm=tot[9:10, :], g_final=tot[10:11, :])
        for s, k in enumerate(names):
            g = grads[k]
            d, nm, nv = _adam_math(w_refs[s][...], g, m_refs[s][...], v_refs[s][...])
            out_refs[4 * s][...] = g
            out_refs[4 * s + 1][...] = d
            out_refs[4 * s + 2][...] = nm
            out_refs[4 * s + 3][...] = nv

    vm = pl.BlockSpec(memory_space=pltpu.VMEM)
    args = [gathered, *[small_w[k] for k in names], *[small_m[k] for k in names], *[small_v[k] for k in names]]
    out_shape = [_sds((1, 1), F32), _sds((CONV_PAD, C), F32)]
    for wd in widths:
        out_shape += [_sds((1, wd), F32)] * 4
    res = pl.pallas_call(
        body, name="small_adamw",
        in_specs=[vm] * len(args), out_specs=[vm] * len(out_shape), out_shape=out_shape,
        compiler_params=pltpu.CompilerParams(vmem_limit_bytes=VMEM_LIMIT_BYTES),
    )(*args)
    return res[0], res[1], {k: res[2 + 4 * s:6 + 4 * s] for s, k in enumerate(names)}


BIG = dict(w_in=True, w_attn_proj=True, w_conv_proj=True, w_out=False, w_ffn_in=True, w_ffn_down=False)
WEIGHT_NAMES = ["g_mix_norm", "w_in", "b_in", "sinks", "conv_w", "conv_b", "ln_g", "ln_b", "w_attn_proj",
                "w_conv_proj", "b_conv_proj", "w_out", "g_ffn_norm", "w_ffn_in", "w_ffn_down", "g_final"]


class _Plan:
    GROUPS = dict(down=["w_ffn_down"], ffn=["w_ffn_in"], mix=["w_out", "w_attn_proj", "w_conv_proj"], inp=["w_in"])
    ALL = (0, 1, 2)
    RIDES = dict(
        gather_mix=[("gather", ["w_attn_proj", "w_conv_proj", "w_out"])], gather_ffn=[("gather", ["w_ffn_in"])],
        gather_down=[("gather", ["w_ffn_down"])],
        ffn_in_bwd=[("swap", "down")], send_down=[("send", "down", ALL)],
        out_proj_bwd_merge=[("swap", "ffn")], send_ffn=[("send", "ffn", ALL)],
        conv_bwd=[("swap", "mix")], send_mix=[("send", "mix", ALL)],
        swap_inp=[("swap", "inp")], send_inp=[("send", "inp", ALL)])
    ASYNC = dict(gather_mix=1, gather_ffn=2, gather_down=3, send_down=4, send_ffn=5, send_mix=6, send_inp=7)

    def __init__(self, shards, c1):
        self.shards, self.c1 = shards, c1
        self.full, self.slots, self.got, self.sums, self.got3 = {}, {}, {}, {}, {}

    def weight(self, name):
        return self.full[name]

    def grad_ready(self, grads):
        for k, g in grads.items():
            self.slots[k] = g.reshape(N_DEV, g.shape[0] // N_DEV, g.shape[1])

    def _one(self, kind, what, ks=None):
        if kind == "gather":
            return _gather_carry([self.shards[k] for k in what])
        names = self.GROUPS[what]
        if kind == "swap":
            return _swap_carry([self.slots[k] for k in names])
        return _send_carry([self.sums[k] for k in names], ks)

    def carry(self, call):
        return _join([self._one(*ride) for ride in self.RIDES.get(call, [])])

    def done(self, call, outs):
        outs = list(outs)
        for kind, what, *_ in self.RIDES.get(call, []):
            names = what if kind == "gather" else self.GROUPS[what]
            mine, outs = outs[:len(names)], outs[len(names):]
            if kind == "gather":
                self.full.update(zip(names, mine))
            elif kind == "send":
                for k, r in zip(names, mine):
                    self.got3.setdefault(k, []).append(r)
            else:
                self.got.update(zip(names, mine))
                self.sums.update(zip(names, _chip_sums(f"chip_sums_{what}", [self.slots[k] for k in names], mine, self.c1)))

    def alone(self, call):
        self.done(call, _run_carry(call, self.carry(call)))

    def behind(self, group, x):
        return lax.optimization_barrier((x, tuple(self.sums[k] for k in self.GROUPS[group])))[0]

    def launch(self, call, after=None):
        carry = self._one(*self.RIDES[call][0])
        if after is not None:
            carry.arrays = list(lax.optimization_barrier((tuple(carry.arrays), after))[0])
        self.done(call, _run_carry_async(call, carry, self.ASYNC[call]))


def kernel(x, g_mix_norm, w_in, b_in, sinks, conv_w, conv_b, ln_g, ln_b, w_attn_proj, w_conv_proj, b_conv_proj, w_out, g_ffn_norm, w_ffn_in, w_ffn_down, g_final, loss_target, m_g_mix_norm, m_w_in, m_b_in, m_sinks, m_conv_w, m_conv_b, m_ln_g, m_ln_b, m_w_attn_proj, m_w_conv_proj, m_b_conv_proj, m_w_out, m_g_ffn_norm, m_w_ffn_in, m_w_ffn_down, m_g_final, v_g_mix_norm, v_w_in, v_b_in, v_sinks, v_conv_w, v_conv_b, v_ln_g, v_ln_b, v_w_attn_proj, v_w_conv_proj, v_b_conv_proj, v_w_out, v_g_ffn_norm, v_w_ffn_in, v_w_ffn_down, v_g_final):
    w = dict(g_mix_norm=g_mix_norm, w_in=w_in, b_in=b_in, sinks=sinks, conv_w=conv_w, conv_b=conv_b, ln_g=ln_g,
             ln_b=ln_b, w_attn_proj=w_attn_proj, w_conv_proj=w_conv_proj, b_conv_proj=b_conv_proj, w_out=w_out,
             g_ffn_norm=g_ffn_norm, w_ffn_in=w_ffn_in, w_ffn_down=w_ffn_down, g_final=g_final)
    m = dict(g_mix_norm=m_g_mix_norm, w_in=m_w_in, b_in=m_b_in, sinks=m_sinks, conv_w=m_conv_w, conv_b=m_conv_b,
             ln_g=m_ln_g, ln_b=m_ln_b, w_attn_proj=m_w_attn_proj, w_conv_proj=m_w_conv_proj,
             b_conv_proj=m_b_conv_proj, w_out=m_w_out, g_ffn_norm=m_g_ffn_norm, w_ffn_in=m_w_ffn_in,
             w_ffn_down=m_w_ffn_down, g_final=m_g_final)
    v = dict(g_mix_norm=v_g_mix_norm, w_in=v_w_in, b_in=v_b_in, sinks=v_sinks, conv_w=v_conv_w, conv_b=v_conv_b,
             ln_g=v_ln_g, ln_b=v_ln_b, w_attn_proj=v_w_attn_proj, w_conv_proj=v_w_conv_proj,
             b_conv_proj=v_b_conv_proj, w_out=v_w_out, g_ffn_norm=v_g_ffn_norm, w_ffn_in=v_w_ffn_in,
             w_ffn_down=v_w_ffn_down, g_final=v_g_final)
    ax, ay, ac = lax.axis_index("x"), lax.axis_index("y"), lax.axis_index("c")
    me = 4 * ax + 2 * ay + ac
    chip = 2 * ax + ay

    shards = {k: (w[k][0].T if tr else w[k][0]).astype(BF) for k, tr in BIG.items()}
    cw_shard = jnp.pad(conv_w[0].T, ((0, 0), (0, 1))).reshape(16, 128)
    wi_t, cw_full, h, r1 = _first_gather([shards["w_in"], cw_shard], x[0], g_mix_norm)
    conv_full = cw_full.reshape(CONV_CHANNELS, CONV_PAD).T

    as_row = lambda a: a.reshape(1, -1)
    small_w = {k: as_row(w[k]) for k in SMALL_NAMES}
    small_m = {k: as_row(m[k]) for k in SMALL_NAMES}
    small_v = {k: as_row(v[k]) for k in SMALL_NAMES}
    plan = _Plan(shards, ac.reshape(1).astype(jnp.int32))
    plan.launch("gather_mix", after=wi_t)
    dx, parts = _local_step(x[0], h, r1, loss_target[0], small_w, wi_t, conv_full, plan)

    ids = jnp.stack([me, chip]).astype(jnp.int32)
    grads, delta, new_m, new_v, after = {}, {}, {}, {}, dx
    packed = _small_pack(parts)
    for k in sorted(BIG, key=lambda k: k == "w_in"):
        if k == "w_in":
            packed = lax.optimization_barrier((packed, after))[0]
            small_gathered, = _run_carry_async("small_gather", _gather_carry([packed]), 8)
        flip = (lambda a: a.T) if BIG[k] else (lambda a: a)
        wk = lax.optimization_barrier((w[k][0], after))[0]
        outs = _grad_adamw(f"grad_adamw_{k}", plan.slots[k], plan.got[k], plan.got3[k], ids,
                           flip(wk), flip(m[k][0]), flip(v[k][0]))
        after = outs[0]
        grads[k], delta[k], new_m[k], new_v[k] = (flip(a)[None] for a in outs)

    loss, cw_grad, small_out = _small_adamw(small_gathered, small_w, small_m, small_v)
    for k in SMALL_NAMES:
        g, d, nm, nv = (a.reshape(w[k].shape) for a in small_out[k])
        grads[k], delta[k], new_m[k], new_v[k] = g, d, nm, nv
    cw_mine = lax.dynamic_slice(cw_grad, (0, me * 64), (CONV_WIDTH, 64))
    d, nm, nv = _adamw("adamw_conv_w", conv_w[0], cw_mine, m_conv_w[0], v_conv_w[0])
    grads["conv_w"], delta["conv_w"], new_m["conv_w"], new_v["conv_w"] = cw_mine[None], d[None], nm[None], nv[None]

    return (loss.reshape(()), dx[None], *[grads[k] for k in WEIGHT_NAMES], *[delta[k] for k in WEIGHT_NAMES],
            *[new_m[k] for k in WEIGHT_NAMES], *[new_v[k] for k in WEIGHT_NAMES])
```

```python
import functools

import jax
import jax.numpy as jnp
from jax import lax
from jax.experimental import pallas as pl
from jax.experimental.pallas import tpu as pltpu
from jax.experimental.pallas import tpu_sc as plsc

F32 = jnp.float32
BF = jnp.bfloat16

SEQ = 2048
D_MODEL = 1024
HEAD_DIM = 64
N_Q_HEADS = 8
N_KV_HEADS = 2
GROUP = N_Q_HEADS // N_KV_HEADS
BLOCK = 128
ATTN_WIDTH = 512
KV_WIDTH = 128
CONV_CHANNELS = 512
CONV_WIDTH = 31
CONV_PAD = 32
GLU_OFF = 768
GATE_OFF = 1792
IN_WIDTH = 3840
D_FF = 2816
EPS = 1e-5
NEG = -1e30
N_DEV = 8

ADAM_LR = 0.001
ADAM_B1 = 0.9
ADAM_B2 = 0.999
ADAM_EPS = 1e-08
ADAM_WD = 0.01
ADAM_STEP = 10

VMEM_LIMIT_BYTES = 56 * 1024 * 1024
MESH = pl.DeviceIdType.MESH
ANY = pl.BlockSpec(memory_space=pl.ANY)

_DIMS = {"NN": (((1,), (0,)), ((), ())), "NT": (((1,), (1,)), ((), ())), "TN": (((0,), (0,)), ((), ()))}


def _params(sem):
    return pltpu.CompilerParams(dimension_semantics=sem, vmem_limit_bytes=VMEM_LIMIT_BYTES)


class _Carry:
    def __init__(self, arrays, out_shapes, sems, start, finish, peers=None):
        self.arrays, self.out_shapes, self.sems, self.start, self.finish = arrays, out_shapes, sems, start, finish
        self.peers = peers


def _carry_io(carry):
    if carry is None:
        return [], [], []
    return list(carry.arrays), list(carry.out_shapes), list(carry.sems)


def _matmul(name, a_list, b, mode, *, m, n, tm, tn, tk=None, epilogue, extra=(), outs, b_off=(0, 0), alias=None,
            scratch=(), carry=None):
    seg_k = [a.shape[0] if mode == "TN" else a.shape[1] for a in a_list]
    whole = tk is None
    seg_nk = [1] * len(a_list) if whole else [ks // tk for ks in seg_k]
    nk = 1 if whole else sum(seg_nk)
    starts = [sum(seg_nk[:s]) for s in range(len(seg_nk))]
    k_starts = [sum(seg_k[:s]) for s in range(len(seg_k))]
    k_tot = sum(seg_k)
    n_a, n_extra, n_out = len(a_list), len(extra), len(outs)

    a_specs = []
    for st, ns, ks in zip(starts, seg_nk, seg_k):
        if mode == "TN":
            a_specs.append(pl.BlockSpec((ks if whole else tk, tm), lambda j, i, k: (k, i)))
        elif whole:
            a_specs.append(pl.BlockSpec((tm, ks), lambda j, i, k: (i, 0)))
        else:
            a_specs.append(pl.BlockSpec((tm, tk), functools.partial(
                lambda j, i, k, st, ns: (i, jnp.clip(k - st, 0, ns - 1)), st=st, ns=ns)))
    bk = k_tot if whole else tk
    if mode == "NT":
        b_spec = pl.BlockSpec((tn, bk), lambda j, i, k: (b_off[0] + j, b_off[1] + k))
    else:
        b_spec = pl.BlockSpec((bk, tn), lambda j, i, k: (b_off[0] + k, b_off[1] + j))
    n_alias = 0 if alias is None else 1
    c_in, c_out, c_sems = _carry_io(carry)
    n_acc = 0 if whole else 1
    nj, ni = n // tn, m // tm

    def body(*refs):
        pos = [n_a, 1, n_alias, n_extra, len(c_in), n_out, len(c_out), n_acc, len(scratch), len(c_sems)]
        cuts = [sum(pos[:q]) for q in range(len(pos) + 1)]
        a_refs, (b_ref,), _, ex, ci_refs, out_refs, co_refs, acc_refs, scr, cs_refs = (
            refs[cuts[q]:cuts[q + 1]] for q in range(len(pos)))
        j, i, k = pl.program_id(0), pl.program_id(1), pl.program_id(2)
        ids = (j, i)
        if carry is not None:
            @pl.when((j == 0) & (i == 0) & (k == 0))
            def _():
                carry.start(ci_refs, co_refs, cs_refs)

        def dot(a_ref, bv):
            return lax.dot_general(a_ref[...].astype(BF), bv.astype(BF), _DIMS[mode], preferred_element_type=F32)

        if whole:
            tot = None
            for a_ref, k0, ks in zip(a_refs, k_starts, seg_k):
                if n_a == 1:
                    bv = b_ref[...]
                else:
                    bv = b_ref[:, k0:k0 + ks] if mode == "NT" else b_ref[k0:k0 + ks, :]
                part = dot(a_ref, bv)
                tot = part if tot is None else tot + part
            epilogue(tot, ex, out_refs, ids, scr)
        else:
            acc, = acc_refs

            @pl.when(k == 0)
            def _():
                acc[...] = jnp.zeros_like(acc)

            for a_ref, st, ns in zip(a_refs, starts, seg_nk):
                if n_a == 1:
                    acc[...] += dot(a_ref, b_ref[...])
                else:
                    @pl.when((k >= st) & (k < st + ns))
                    def _(a_ref=a_ref):
                        acc[...] += dot(a_ref, b_ref[...])

            @pl.when(k == nk - 1)
            def _():
                epilogue(acc[...], ex, out_refs, ids, scr)

        if carry is not None:
            @pl.when((j == nj - 1) & (i == ni - 1) & (k == nk - 1))
            def _():
                carry.finish(ci_refs, co_refs, cs_refs)

    in_specs = [*a_specs, b_spec]
    args = [*a_list, b]
    io_alias = {}
    if alias is not None:
        in_specs.append(pl.BlockSpec(memory_space=pl.ANY))
        args.append(alias[0])
        io_alias = {n_a + 1: alias[1]}
    in_specs += [s for _, s in extra] + [pl.BlockSpec(memory_space=pl.ANY)] * len(c_in)
    args += [x for x, _ in extra] + c_in
    res = pl.pallas_call(
        body, name=name, grid=(nj, ni, nk), in_specs=in_specs,
        out_specs=[s for _, s in outs] + [pl.BlockSpec(memory_space=pl.ANY)] * len(c_out),
        out_shape=[o for o, _ in outs] + c_out,
        scratch_shapes=[*([] if whole else [pltpu.VMEM((tm, tn), F32)]), *scratch, *c_sems],
        input_output_aliases=io_alias,
        compiler_params=_params(("arbitrary", "arbitrary", "arbitrary")),
    )(*args)
    return res if carry is None else (res[:n_out], res[n_out:])


def _tile(tm, tn):
    return pl.BlockSpec((tm, tn), lambda j, i, k: (i, j))


def _row(tn):
    return pl.BlockSpec((1, tn), lambda j, i, k: (0, j))


def _store(dtype):
    def ep(acc, ex, outs, ids, scr):
        outs[0][...] = acc.astype(dtype)
    return ep


def _sds(shape, dtype):
    return jax.ShapeDtypeStruct(shape, dtype)


def _rms_bwd(dh, xv, r, g):
    xh = xv * r
    dxh = dh * g
    dx = r * (dxh - xh * jnp.mean(dxh * xh, axis=-1, keepdims=True))
    return dx, jnp.sum(dh * xh, axis=0, keepdims=True)


def _accumulate_rows(ref, val, first):
    @pl.when(first)
    def _():
        ref[...] = val

    @pl.when(jnp.logical_not(first))
    def _():
        ref[...] += val


def _loss_head(xv, g, target):
    r = lax.rsqrt(jnp.mean(xv * xv, axis=-1, keepdims=True) + EPS)
    err = xv * r * g - target
    dx, dg = _rms_bwd(err * (1.0 / xv.shape[-1]), xv, r, g)
    part = 0.5 * jnp.sum(jnp.mean(err * err, axis=-1, keepdims=True), axis=0, keepdims=True)
    return dx, dg, part


def _lane_half(shape, h):
    lane = lax.broadcasted_iota(jnp.int32, shape, 1)
    return (lane >= HEAD_DIM * h) & (lane < HEAD_DIM * (h + 1))


def _to_half(v, w, h):
    if w != h:
        v = pltpu.roll(v, HEAD_DIM, 1)
    return jnp.where(_lane_half(v.shape, h), v, 0.0)


def _attn_block(qkv_ref, sinks_ref, n, h):
    r0 = pl.multiple_of(n * BLOCK, BLOCK)
    p0 = pl.multiple_of(jnp.maximum(n - 1, 0) * BLOCK, BLOCK)
    rows = pl.ds(r0, BLOCK)
    prev = pl.ds(p0, BLOCK)
    k2 = jnp.concatenate([qkv_ref[prev, ATTN_WIDTH:ATTN_WIDTH + KV_WIDTH],
                          qkv_ref[rows, ATTN_WIDTH:ATTN_WIDTH + KV_WIDTH]], axis=0)
    v2 = jnp.concatenate([qkv_ref[prev, ATTN_WIDTH + KV_WIDTH:ATTN_WIDTH + 2 * KV_WIDTH],
                          qkv_ref[rows, ATTN_WIDTH + KV_WIDTH:ATTN_WIDTH + 2 * KV_WIDTH]], axis=0)
    qs = []
    for g in range(GROUP):
        hq = GROUP * h + g
        blk = qkv_ref[rows, (hq // 2) * 128:(hq // 2 + 1) * 128].astype(F32)
        qs.append(_to_half(blk, hq % 2, h))
    q4 = jnp.concatenate(qs, axis=0).astype(BF)
    s = lax.dot_general(q4, k2, _DIMS["NT"], preferred_element_type=F32) * (HEAD_DIM ** -0.5)
    shape = s.shape
    row = lax.broadcasted_iota(jnp.int32, shape, 0)
    qi = row & (BLOCK - 1)
    kj = lax.broadcasted_iota(jnp.int32, shape, 1)
    diff = qi + BLOCK - kj
    valid = (diff >= 0) & (diff < BLOCK) & ((kj >= BLOCK) | (n > 0))
    s = jnp.where(valid, s, NEG)
    row1 = lax.broadcasted_iota(jnp.int32, (shape[0], 1), 0)
    sink = jnp.zeros((shape[0], 1), F32)
    for g in range(GROUP):
        sink = jnp.where((row1 >= g * BLOCK) & (row1 < (g + 1) * BLOCK), sinks_ref[0, GROUP * h + g], sink)
    m = jnp.maximum(jnp.max(s, axis=-1, keepdims=True), sink)
    e = jnp.exp(s - m)
    es = jnp.exp(sink - m)
    inv = 1.0 / (jnp.sum(e, axis=-1, keepdims=True) + es)
    return e * inv, es * inv, q4, k2, v2, rows, prev


def _attn_fwd(proj, sinks, carry=None):
    T = proj.shape[0]
    c_in, c_out, c_sems = _carry_io(carry)

    def body(*refs):
        qkv_ref, sinks_ref = refs[:2]
        ci_refs = refs[2:2 + len(c_in)]
        o_ref = refs[2 + len(c_in)]
        co_refs = refs[3 + len(c_in):3 + len(c_in) + len(c_out)]
        cs_refs = refs[3 + len(c_in) + len(c_out):]
        if carry is not None:
            carry.start(ci_refs, co_refs, cs_refs)

        def blk(n, z):
            outs = [None] * (N_Q_HEADS // 2)
            for h in range(N_KV_HEADS):
                p, _, _, _, v2, rows, _ = _attn_block(qkv_ref, sinks_ref, n, h)
                o = lax.dot_general(p.astype(BF), v2, _DIMS["NN"], preferred_element_type=F32)
                for g in range(GROUP):
                    hq = GROUP * h + g
                    piece = jnp.where(_lane_half((BLOCK, 128), h), o[g * BLOCK:(g + 1) * BLOCK], 0.0)
                    if hq % 2 != h:
                        piece = pltpu.roll(piece, HEAD_DIM, 1)
                    outs[hq // 2] = piece if outs[hq // 2] is None else outs[hq // 2] + piece
            for pb in range(N_Q_HEADS // 2):
                o_ref[rows, pb * 128:(pb + 1) * 128] = outs[pb].astype(BF)
            return z

        lax.fori_loop(0, T // BLOCK, blk, 0)
        if carry is not None:
            carry.finish(ci_refs, co_refs, cs_refs)

    res = pl.pallas_call(
        body, name="attn_fwd", grid=(1,),
        in_specs=[pl.BlockSpec((T, GLU_OFF), lambda i: (0, 0)), pl.BlockSpec(memory_space=pltpu.SMEM),
                  *[ANY] * len(c_in)],
        out_specs=[pl.BlockSpec((T, ATTN_WIDTH), lambda i: (0, 0)), *[ANY] * len(c_out)],
        out_shape=[_sds((T, ATTN_WIDTH), BF), *c_out], scratch_shapes=c_sems,
        compiler_params=_params(("arbitrary",)),
    )(proj, sinks, *c_in)
    return res[0], res[1:]


def _attn_bwd(proj, d_o, sinks, carry=None):
    T = proj.shape[0]
    c_in, c_out, c_sems = _carry_io(carry)

    def body(*refs):
        qkv_ref, do_ref, sinks_ref = refs[:3]
        ci_refs = refs[3:3 + len(c_in)]
        dqkv_ref, dsink_ref = refs[3 + len(c_in):5 + len(c_in)]
        co_refs = refs[5 + len(c_in):5 + len(c_in) + len(c_out)]
        dk_acc, dv_acc = refs[5 + len(c_in) + len(c_out):7 + len(c_in) + len(c_out)]
        cs_refs = refs[7 + len(c_in) + len(c_out):]
        if carry is not None:
            carry.start(ci_refs, co_refs, cs_refs)
        dsink_ref[...] = jnp.zeros_like(dsink_ref)
        dk_acc[...] = jnp.zeros_like(dk_acc)
        dv_acc[...] = jnp.zeros_like(dv_acc)

        def blk(n, carry):
            dqs = [None] * (N_Q_HEADS // 2)
            for h in range(N_KV_HEADS):
                p, psink, q4, k2, v2, rows, prev = _attn_block(qkv_ref, sinks_ref, n, h)
                dos = []
                for g in range(GROUP):
                    hq = GROUP * h + g
                    dos.append(_to_half(do_ref[rows, (hq // 2) * 128:(hq // 2 + 1) * 128].astype(F32), hq % 2, h))
                do4 = jnp.concatenate(dos, axis=0).astype(BF)
                dp = lax.dot_general(do4, v2, _DIMS["NT"], preferred_element_type=F32)
                delta = jnp.sum(p * dp, axis=-1, keepdims=True)
                ds = (p * (dp - delta) * (HEAD_DIM ** -0.5)).astype(BF)
                dsk = psink * delta
                for g in range(GROUP):
                    hq = GROUP * h + g
                    tot = -jnp.sum(dsk[g * BLOCK:(g + 1) * BLOCK], axis=0, keepdims=True)
                    lane = lax.broadcasted_iota(jnp.int32, (1, 128), 1)
                    dsink_ref[...] += jnp.where(lane == hq, tot, 0.0)
                dq = lax.dot_general(ds, k2, _DIMS["NN"], preferred_element_type=F32)
                dk = lax.dot_general(ds, q4, _DIMS["TN"], preferred_element_type=F32)
                dv = lax.dot_general(p.astype(BF), do4, _DIMS["TN"], preferred_element_type=F32)
                dk_acc[prev, :] += dk[:BLOCK]
                dk_acc[rows, :] += dk[BLOCK:]
                dv_acc[prev, :] += dv[:BLOCK]
                dv_acc[rows, :] += dv[BLOCK:]
                for g in range(GROUP):
                    hq = GROUP * h + g
                    piece = jnp.where(_lane_half((BLOCK, 128), h), dq[g * BLOCK:(g + 1) * BLOCK], 0.0)
                    if hq % 2 != h:
                        piece = pltpu.roll(piece, HEAD_DIM, 1)
                    dqs[hq // 2] = piece if dqs[hq // 2] is None else dqs[hq // 2] + piece
            for pb in range(N_Q_HEADS // 2):
                dqkv_ref[rows, pb * 128:(pb + 1) * 128] = dqs[pb].astype(BF)
            return carry

        lax.fori_loop(0, T // BLOCK, blk, 0)
        dqkv_ref[:, ATTN_WIDTH:ATTN_WIDTH + KV_WIDTH] = dk_acc[...].astype(BF)
        dqkv_ref[:, ATTN_WIDTH + KV_WIDTH:] = dv_acc[...].astype(BF)
        if carry is not None:
            carry.finish(ci_refs, co_refs, cs_refs)

    res = pl.pallas_call(
        body, name="attn_bwd", grid=(1,),
        in_specs=[pl.BlockSpec((T, GLU_OFF), lambda i: (0, 0)), pl.BlockSpec((T, ATTN_WIDTH), lambda i: (0, 0)),
                  pl.BlockSpec(memory_space=pltpu.SMEM), *[ANY] * len(c_in)],
        out_specs=[pl.BlockSpec((T, GLU_OFF), lambda i: (0, 0)), pl.BlockSpec((1, 128), lambda i: (0, 0)),
                   *[ANY] * len(c_out)],
        out_shape=[_sds((T, GLU_OFF), BF), _sds((1, 128), F32), *c_out],
        scratch_shapes=[pltpu.VMEM((T, KV_WIDTH), F32), pltpu.VMEM((T, KV_WIDTH), F32), *c_sems],
        compiler_params=_params(("arbitrary",)),
    )(proj, d_o, sinks, *c_in)
    return res[:2], res[2:]


CHUNK = 256
SUB = 32
WIN = CHUNK + 32
PAD_ROWS = SEQ + 2 * CONV_PAD
_GLU_SPECS = [pl.BlockSpec((SEQ, 256), functools.partial(lambda i, c: (0, c), c=GLU_OFF // 256 + c)) for c in range(4)]


def _glu_to_pad(a0, a1, b0, b1, zpad):
    C = CONV_CHANNELS
    zpad[0:CONV_PAD, :] = jnp.zeros((CONV_PAD, C), F32)
    zpad[CONV_PAD + SEQ:, :] = jnp.zeros((CONV_PAD, C), F32)
    zpad[CONV_PAD:CONV_PAD + SEQ, 0:256] = a0[...].astype(F32) * jax.nn.sigmoid(b0[...].astype(F32))
    zpad[CONV_PAD:CONV_PAD + SEQ, 256:C] = a1[...].astype(F32) * jax.nn.sigmoid(b1[...].astype(F32))


def _tap_windows(src, base, win):
    for b in range(8):
        win[b, 0:WIN - 8, :] = src[base + b:base + b + WIN - 8, :]


def _taps(win, w_ref, init, out, flip):
    def sub(si, carry):
        r0 = pl.multiple_of(si * SUB, SUB)
        acc = jnp.broadcast_to(init, (SUB, CONV_CHANNELS))
        for k in range(CONV_WIDTH):
            wk = (CONV_WIDTH - 1 - k) if flip else k
            acc = acc + w_ref[wk:wk + 1, :] * win[k % 8, pl.ds(r0 + 8 * (k // 8), SUB), :]
        out[pl.ds(r0, SUB), :] = acc
        return carry

    lax.fori_loop(0, CHUNK // SUB, sub, 0)


def _tap_grads(win, du, dwacc):
    def sub(si, carry):
        r0 = pl.multiple_of(si * SUB, SUB)
        d = du[pl.ds(r0, SUB), :]
        for k in range(CONV_WIDTH):
            p = d * win[k % 8, pl.ds(r0 + 8 * (k // 8), SUB), :]
            dwacc[8 * k:8 * k + 8, :] += (p[0:8] + p[8:16]) + (p[16:24] + p[24:32])
        return carry

    lax.fori_loop(0, CHUNK // SUB, sub, 0)


def _ln_parts(u):
    mu = jnp.mean(u, axis=-1, keepdims=True)
    xc = u - mu
    rstd = lax.rsqrt(jnp.mean(xc * xc, axis=-1, keepdims=True) + EPS)
    return xc * rstd, rstd


def _conv_fwd(proj, conv_w, conv_b, ln_g, ln_b, carry=None):
    T, C = proj.shape[0], CONV_CHANNELS
    vec = pl.BlockSpec((1, C), lambda i: (0, 0))
    c_in, c_out, c_sems = _carry_io(carry)

    def body(*refs):
        a0, a1, b0, b1, w_ref, cb_ref, g_ref, be_ref = refs[:8]
        ci_refs = refs[8:8 + len(c_in)]
        c_ref, u_ref = refs[8 + len(c_in):10 + len(c_in)]
        co_refs = refs[10 + len(c_in):10 + len(c_in) + len(c_out)]
        zpad, win, ubuf = refs[10 + len(c_in) + len(c_out):13 + len(c_in) + len(c_out)]
        cs_refs = refs[13 + len(c_in) + len(c_out):]
        if carry is not None:
            carry.start(ci_refs, co_refs, cs_refs)
        _glu_to_pad(a0, a1, b0, b1, zpad)
        for ci in range(T // CHUNK):
            _tap_windows(zpad, ci * CHUNK + CONV_PAD - (CONV_WIDTH - 1), win)
            _taps(win, w_ref, cb_ref[...], ubuf, False)
            u = ubuf[...]
            u_ref[ci * CHUNK:(ci + 1) * CHUNK, :] = u
            xh, _ = _ln_parts(u)
            ln = xh * g_ref[...] + be_ref[...]
            c_ref[ci * CHUNK:(ci + 1) * CHUNK, :] = (ln * jax.nn.sigmoid(ln)).astype(BF)
        if carry is not None:
            carry.finish(ci_refs, co_refs, cs_refs)

    res = pl.pallas_call(
        body, name="conv_fwd", grid=(1,),
        in_specs=[*_GLU_SPECS, pl.BlockSpec((CONV_PAD, C), lambda i: (0, 0)), vec, vec, vec, *[ANY] * len(c_in)],
        out_specs=[pl.BlockSpec((T, C), lambda i: (0, 0)), pl.BlockSpec((T, C), lambda i: (0, 0)), *[ANY] * len(c_out)],
        out_shape=[_sds((T, C), BF), _sds((T, C), F32), *c_out],
        scratch_shapes=[pltpu.VMEM((PAD_ROWS, C), F32), pltpu.VMEM((8, WIN, C), F32), pltpu.VMEM((CHUNK, C), F32),
                        *c_sems],
        compiler_params=_params(("arbitrary",)),
    )(proj, proj, proj, proj, conv_w, conv_b, ln_g, ln_b, *c_in)
    return res[:2], res[2:]


def _conv_bwd(proj, u, d_c, conv_w, conv_b, ln_g, ln_b, carry=None):
    T, C = proj.shape[0], CONV_CHANNELS
    vec = pl.BlockSpec((1, C), lambda i: (0, 0))
    wspec = pl.BlockSpec((CONV_PAD, C), lambda i: (0, 0))
    c_in, c_out, c_sems = _carry_io(carry)

    def body(*refs):
        a0, a1, b0, b1, u_ref, dc_ref, w_ref, cb_ref, g_ref, be_ref = refs[:10]
        ci_refs = refs[10:10 + len(c_in)]
        o = 10 + len(c_in)
        dglu_ref, dw_ref, dcb_ref, dg_ref, dbe_ref = refs[o:o + 5]
        co_refs = refs[o + 5:o + 5 + len(c_out)]
        zpad, dupad, win, ubuf, dwacc = refs[o + 5 + len(c_out):o + 10 + len(c_out)]
        cs_refs = refs[o + 10 + len(c_out):]
        if carry is not None:
            carry.start(ci_refs, co_refs, cs_refs)
        _glu_to_pad(a0, a1, b0, b1, zpad)
        dupad[T:, :] = jnp.zeros((2 * CONV_PAD, C), F32)
        dwacc[...] = jnp.zeros_like(dwacc)
        dcb_ref[...] = jnp.zeros_like(dcb_ref)
        dg_ref[...] = jnp.zeros_like(dg_ref)
        dbe_ref[...] = jnp.zeros_like(dbe_ref)
        for ci in range(T // CHUNK):
            rows = slice(ci * CHUNK, (ci + 1) * CHUNK)
            _tap_windows(zpad, ci * CHUNK + CONV_PAD - (CONV_WIDTH - 1), win)
            xh, rstd = _ln_parts(u_ref[rows, :])
            ln = xh * g_ref[...] + be_ref[...]
            sg = jax.nn.sigmoid(ln)
            dln = dc_ref[rows, :].astype(F32) * (sg * (1.0 + ln * (1.0 - sg)))
            dg_ref[...] += jnp.sum(dln * xh, axis=0, keepdims=True)
            dbe_ref[...] += jnp.sum(dln, axis=0, keepdims=True)
            dxh = dln * g_ref[...]
            du = rstd * (dxh - jnp.mean(dxh, axis=-1, keepdims=True)
                         - xh * jnp.mean(dxh * xh, axis=-1, keepdims=True))
            dupad[rows, :] = du
            dcb_ref[...] += jnp.sum(du, axis=0, keepdims=True)
            _tap_grads(win, dupad.at[rows, :], dwacc)
        for k in range(CONV_WIDTH):
            dw_ref[k:k + 1, :] = jnp.sum(dwacc[8 * k:8 * k + 8, :], axis=0, keepdims=True)
        dw_ref[CONV_WIDTH:, :] = jnp.zeros((CONV_PAD - CONV_WIDTH, C), F32)
        for ci in range(T // CHUNK):
            rows = slice(ci * CHUNK, (ci + 1) * CHUNK)
            _tap_windows(dupad, ci * CHUNK, win)
            _taps(win, w_ref, jnp.zeros((1, C), F32), ubuf, True)
            dz = ubuf[...]
            for half, (a, b) in enumerate(((a0, b0), (a1, b1))):
                sb = jax.nn.sigmoid(b[rows, :].astype(F32))
                dzh = dz[:, half * 256:(half + 1) * 256]
                dglu_ref[rows, half * 256:(half + 1) * 256] = (dzh * sb).astype(BF)
                dglu_ref[rows, C + half * 256:C + (half + 1) * 256] = (
                    dzh * a[rows, :].astype(F32) * sb * (1.0 - sb)).astype(BF)
        if carry is not None:
            carry.finish(ci_refs, co_refs, cs_refs)

    res = pl.pallas_call(
        body, name="conv_bwd", grid=(1,),
        in_specs=[*_GLU_SPECS, pl.BlockSpec((T, C), lambda i: (0, 0)), pl.BlockSpec((T, C), lambda i: (0, 0)), wspec,
                  vec, vec, vec, *[ANY] * len(c_in)],
        out_specs=[pl.BlockSpec((T, 2 * C), lambda i: (0, 0)), wspec, vec, vec, vec, *[ANY] * len(c_out)],
        out_shape=[_sds((T, 2 * C), BF), _sds((CONV_PAD, C), F32), _sds((1, C), F32), _sds((1, C), F32),
                   _sds((1, C), F32), *c_out],
        scratch_shapes=[pltpu.VMEM((PAD_ROWS, C), F32), pltpu.VMEM((PAD_ROWS, C), F32), pltpu.VMEM((8, WIN, C), F32),
                        pltpu.VMEM((CHUNK, C), F32), pltpu.VMEM((8 * CONV_PAD, C), F32), *c_sems],
        compiler_params=_params(("arbitrary",)),
    )(proj, proj, proj, proj, u, d_c, conv_w, conv_b, ln_g, ln_b, *c_in)
    return res[:5], res[5:]


_GATE_BLK = GATE_OFF // 256


def _ffn_in_swiglu(h2, wf_t, carry=None):
    T, D = h2.shape
    tm, tn = 1024, D_FF // 2
    nj, ni = D_FF // tn, T // tm
    c_in, c_out, c_sems = _carry_io(carry)

    def body(*refs):
        a_ref, bg_ref, bu_ref = refs[:3]
        ci_refs = refs[3:3 + len(c_in)]
        act_ref, g_ref, u_ref = refs[3 + len(c_in):6 + len(c_in)]
        co_refs = refs[6 + len(c_in):6 + len(c_in) + len(c_out)]
        cs_refs = refs[6 + len(c_in) + len(c_out):]
        j, i = pl.program_id(0), pl.program_id(1)
        if carry is not None:
            @pl.when((j == 0) & (i == 0))
            def _():
                carry.start(ci_refs, co_refs, cs_refs)
        a = a_ref[...]
        for c0, c1 in ((0, 768), (768, tn)):
            g = lax.dot_general(a, bg_ref[c0:c1, :], _DIMS["NT"], preferred_element_type=F32)
            u = lax.dot_general(a, bu_ref[c0:c1, :], _DIMS["NT"], preferred_element_type=F32)
            act_ref[:, c0:c1] = (g * jax.nn.sigmoid(g) * u).astype(BF)
            g_ref[:, c0:c1] = g.astype(BF)
            u_ref[:, c0:c1] = u.astype(BF)
        if carry is not None:
            @pl.when((j == nj - 1) & (i == ni - 1))
            def _():
                carry.finish(ci_refs, co_refs, cs_refs)

    t = pl.BlockSpec((tm, tn), lambda j, i: (i, j))
    res = pl.pallas_call(
        body, name="ffn_in_swiglu", grid=(nj, ni),
        in_specs=[pl.BlockSpec((tm, D), lambda j, i: (i, 0)), pl.BlockSpec((tn, D), lambda j, i: (j, 0)),
                  pl.BlockSpec((tn, D), lambda j, i: (nj + j, 0)), *[ANY] * len(c_in)],
        out_specs=[t, t, t, *[ANY] * len(c_out)], out_shape=[*[_sds((T, D_FF), BF)] * 3, *c_out],
        scratch_shapes=c_sems,
        compiler_params=_params(("arbitrary", "arbitrary")),
    )(h2, wf_t, wf_t, *c_in)
    return res[:3], res[3:]


def _proj_merge(o, c, wap_t, wcp_t, b_cp, proj):
    T, D = o.shape[0], wap_t.shape[0]
    tm, tg = T, 256
    nj = D // tg

    def body(o_ref, c_ref, wa_ref, wc_ref, b_ref, g0_ref, g1_ref, ya_ref, yc_ref, m_ref):
        ya = lax.dot_general(o_ref[...], wa_ref[...], _DIMS["NT"], preferred_element_type=F32)
        yc = lax.dot_general(c_ref[...], wc_ref[...], _DIMS["NT"], preferred_element_type=F32) + b_ref[...]
        ya_ref[...] = ya.astype(BF)
        yc_ref[...] = yc.astype(BF)
        m_ref[...] = (jax.nn.sigmoid(g0_ref[...].astype(F32)) * ya + jax.nn.sigmoid(g1_ref[...].astype(F32)) * yc).astype(BF)

    act = pl.BlockSpec((tm, o.shape[1]), lambda j, i: (i, 0))
    wgt = pl.BlockSpec((tg, o.shape[1]), lambda j, i: (j, 0))
    t = pl.BlockSpec((tm, tg), lambda j, i: (i, j))
    return pl.pallas_call(
        body, name="proj_merge", grid=(nj, T // tm),
        in_specs=[act, act, wgt, wgt, pl.BlockSpec((1, tg), lambda j, i: (0, j)),
                  pl.BlockSpec((tm, tg), lambda j, i: (i, _GATE_BLK + j)),
                  pl.BlockSpec((tm, tg), lambda j, i: (i, _GATE_BLK + nj + j))],
        out_specs=[t, t, t], out_shape=[_sds((T, D), BF)] * 3,
        compiler_params=_params(("arbitrary", "arbitrary")),
    )(o, c, wap_t, wcp_t, b_cp, proj, proj)


def _stacked_dw(name, segs, h, tb):
    T, D = h.shape
    nblk = [seg.shape[1] // tb for seg in segs]
    starts = [sum(nblk[:q]) for q in range(len(segs))]
    n_seg = len(segs)

    def body(*refs):
        seg_refs, h_ref, o_ref, cs_ref = refs[:n_seg], refs[n_seg], refs[n_seg + 1], refs[n_seg + 2]
        i = pl.program_id(0)
        for seg_ref, st, nb in zip(seg_refs, starts, nblk):
            @pl.when((i >= st) & (i < st + nb))
            def _(seg_ref=seg_ref):
                a = seg_ref[...]
                o_ref[...] = lax.dot_general(a, h_ref[...], _DIMS["TN"], preferred_element_type=F32).astype(BF)
                cs_ref[...] = jnp.sum(a.astype(F32), axis=0, keepdims=True)

    in_specs = [pl.BlockSpec((T, tb), functools.partial(lambda i, st, nb: (0, jnp.clip(i - st, 0, nb - 1)), st=st, nb=nb))
                for st, nb in zip(starts, nblk)]
    return pl.pallas_call(
        body, name=name, grid=(sum(nblk),),
        in_specs=[*in_specs, pl.BlockSpec((T, D), lambda i: (0, 0))],
        out_specs=[pl.BlockSpec((tb, D), lambda i: (i, 0)), pl.BlockSpec((1, tb), lambda i: (0, i))],
        out_shape=[_sds((sum(nblk) * tb, D), BF), _sds((1, sum(nblk) * tb), F32)],
        compiler_params=_params(("arbitrary",)),
    )(*segs, h)


def _local_step(x, h, r1, target, small, wi_t, conv_w, plan):
    T, D = x.shape
    tm = 1024

    def carried(call, res, carry):
        if carry is None:
            return res
        outs, got = res
        plan.done(call, got)
        return outs


    def ep_add(acc, ex, outs, ids, scr):
        outs[0][...] = acc + ex[0][...]

    tn_in = IN_WIDTH // 2
    carry = plan.carry("proj_in")
    def ep_bias_bf16(acc, ex, outs, ids, scr):
        outs[0][...] = (acc + ex[0][...]).astype(BF)

    proj, = carried("proj_in", _matmul("proj_in", [h], wi_t, "NT", m=T, n=IN_WIDTH, tm=tm, tn=tn_in,
                                       epilogue=ep_bias_bf16, extra=[(small["b_in"], _row(tn_in))],
                                       outs=[(_sds((T, IN_WIDTH), BF), _tile(tm, tn_in))], carry=carry), carry)
    plan.launch("gather_ffn", after=proj)
    o, got = _attn_fwd(proj, small["sinks"], carry=plan.carry("attn_fwd"))
    plan.done("attn_fwd", got)
    (c, u_conv), got = _conv_fwd(proj, conv_w, small["conv_b"], small["ln_g"], small["ln_b"],
                                 carry=plan.carry("conv_fwd"))
    plan.done("conv_fwd", got)
    wap_t, wcp_t, w_out = plan.weight("w_attn_proj"), plan.weight("w_conv_proj"), plan.weight("w_out")
    ya, yc, merged = _proj_merge(o, c, wap_t, wcp_t, small["b_conv_proj"], proj)

    tg = 256
    gate_specs = [pl.BlockSpec((T, tg), lambda j, i, k: (i, _GATE_BLK + j)),
                  pl.BlockSpec((T, tg), lambda j, i, k: (i, _GATE_BLK + D // tg + j))]

    def ep_residual_rms(acc, ex, outs, ids, scr):
        x2v = acc + ex[0][...]
        r = lax.rsqrt(jnp.mean(x2v * x2v, axis=-1, keepdims=True) + EPS)
        outs[0][...] = x2v
        outs[1][...] = (x2v * r * ex[1][...]).astype(BF)
        outs[2][...] = r

    carry = plan.carry("out_proj")
    x2, h2, r2 = carried("out_proj", _matmul(
        "out_proj_rms", [merged], w_out, "NN", m=T, n=D, tm=512, tn=D, epilogue=ep_residual_rms,
        extra=[(x, _tile(512, D)), (small["g_ffn_norm"], _row(D))],
        outs=[(_sds((T, D), F32), _tile(512, D)), (_sds((T, D), BF), _tile(512, D)),
              (_sds((T, 1), F32), pl.BlockSpec((512, 1), lambda j, i, k: (i, 0)))], carry=carry), carry)
    plan.launch("gather_down", after=x2)
    wf_t = plan.weight("w_ffn_in")
    (act, gate, up), got = _ffn_in_swiglu(h2, wf_t, carry=plan.carry("ffn_in_swiglu"))
    plan.done("ffn_in_swiglu", got)
    w_down = plan.weight("w_ffn_down")
    def ep_residual_loss(acc, ex, outs, ids, scr):
        dx, dg, part = _loss_head(acc + ex[0][...], ex[1][...], ex[2][...])
        outs[0][...] = dx
        outs[1][...] = dx.astype(BF)
        _accumulate_rows(outs[2], dg, ids[1] == 0)
        _accumulate_rows(outs[3], part, ids[1] == 0)

    dx3, dx3_b, dg_final, loss = _matmul(
        "ffn_down_loss", [act], w_down, "NN", m=T, n=D, tm=512, tn=D, epilogue=ep_residual_loss,
        extra=[(x2, _tile(512, D)), (small["g_final"], _row(D)), (target, _tile(512, D))],
        outs=[(_sds((T, D), F32), _tile(512, D)), (_sds((T, D), BF), _tile(512, D)), (_sds((1, D), F32), _row(D)),
              (_sds((1, 1), F32), pl.BlockSpec((1, 1), lambda j, i, k: (0, 0)))])

    tn_ff = D_FF // 2

    def ep_swiglu_bwd(acc, ex, outs, ids, scr):
        g, u = ex[0][...].astype(F32), ex[1][...].astype(F32)
        sg = jax.nn.sigmoid(g)
        outs[0][...] = (acc * u * sg * (1.0 + g * (1.0 - sg))).astype(BF)
        outs[1][...] = (acc * g * sg).astype(BF)

    dgate, dup = _matmul(
        "ffn_down_bwd", [dx3_b], w_down, "NT", m=T, n=D_FF, tm=tm, tn=tn_ff, epilogue=ep_swiglu_bwd,
        extra=[(gate, _tile(tm, tn_ff)), (up, _tile(tm, tn_ff))],
        outs=[(_sds((T, D_FF), BF), _tile(tm, tn_ff)), (_sds((T, D_FF), BF), _tile(tm, tn_ff))])

    def dw(name, a, b, rows, cols, row_off=0, alias=None, total_rows=None, colsum=False):
        total_rows = rows if total_rows is None else total_rows
        tmw = rows if rows <= 1024 else D_FF // 2
        blk, rem = divmod(row_off, tmw)
        assert rem == 0

        def ep(acc, ex, outs, ids, scr):
            outs[0][...] = acc.astype(BF)
            if colsum:
                outs[1][...] = jnp.sum(ex[0][...].astype(F32), axis=0, keepdims=True)

        outs = [(_sds((total_rows, cols), BF), pl.BlockSpec((tmw, cols), lambda j, i, k: (blk + i, j)))]
        extra = []
        if colsum:
            extra = [(a, pl.BlockSpec((T, tmw), lambda j, i, k: (0, i)))]
            outs.append((_sds((1, rows), F32), pl.BlockSpec((1, tmw), lambda j, i, k: (0, i))))
        carry = plan.carry(name)
        res = carried(name, _matmul(name, [a], b, "TN", m=rows, n=cols, tm=tmw, tn=cols, epilogue=ep, extra=extra,
                                    outs=outs, alias=None if alias is None else (alias, 0), carry=carry), carry)
        return res if colsum else res[0]

    plan.grad_ready(dict(w_ffn_down=dw("ffn_down_dw", act, dx3_b, D_FF, D)))

    def ep_rms_bwd(acc, ex, outs, ids, scr):
        dx, dg = _rms_bwd(acc, ex[0][...], ex[1][...], ex[2][...])
        dx = ex[3][...] + dx
        outs[0][...] = dx
        outs[1][...] = dx.astype(BF)
        _accumulate_rows(outs[2], dg, ids[1] == 0)

    def rms_bwd_io(tm_, xin, r, g, dres):
        return dict(
            extra=[(xin, _tile(tm_, D)), (r, pl.BlockSpec((tm_, 1), lambda j, i, k: (i, 0))), (g, _row(D)),
                   (dres, _tile(tm_, D))],
            outs=[(_sds((T, D), F32), _tile(tm_, D)), (_sds((T, D), BF), _tile(tm_, D)), (_sds((1, D), F32), _row(D))])

    carry = plan.carry("ffn_in_bwd")
    dx2, dx2_b, dg_ffn = carried(
        "ffn_in_bwd",
        _matmul("ffn_in_bwd", [dgate, dup], wf_t, "NN", m=T, n=D, tm=256, tn=D, epilogue=ep_rms_bwd,
                carry=carry, **rms_bwd_io(256, x2, r2, small["g_ffn_norm"], dx3)), carry)
    plan.launch("send_down")
    gwf_t, _ = _stacked_dw("ffn_in_dw", [dgate, dup], h2, D_FF // 2)
    plan.grad_ready(dict(w_ffn_in=gwf_t))

    def ep_merge_bwd(acc, ex, outs, ids, scr):
        s0 = jax.nn.sigmoid(ex[2][...].astype(F32))
        s1 = jax.nn.sigmoid(ex[3][...].astype(F32))
        outs[0][...] = (acc * s0).astype(BF)
        outs[1][...] = (acc * s1).astype(BF)
        outs[2][...] = (acc * ex[0][...].astype(F32) * s0 * (1.0 - s0)).astype(BF)
        outs[3][...] = (acc * ex[1][...].astype(F32) * s1 * (1.0 - s1)).astype(BF)

    carry = plan.carry("out_proj_bwd_merge")
    dya, dyc, dg0, dg1 = carried(
        "out_proj_bwd_merge",
        _matmul("out_proj_bwd_merge", [dx2_b], w_out, "NT", m=T, n=D, tm=T, tn=tg, epilogue=ep_merge_bwd,
                extra=[(ya, _tile(T, tg)), (yc, _tile(T, tg)), (proj, gate_specs[0]), (proj, gate_specs[1])],
                outs=[(_sds((T, D), BF), _tile(T, tg))] * 4, carry=carry), carry)
    plan.launch("send_ffn")
    gw_out = dw("out_proj_dw", merged, dx2_b, D, D)
    d_o, = _matmul("attn_proj_bwd", [dya], wap_t, "NN", m=T, n=ATTN_WIDTH, tm=tm, tn=ATTN_WIDTH,
                   epilogue=_store(BF), outs=[(_sds((T, ATTN_WIDTH), BF), _tile(tm, ATTN_WIDTH))])
    d_c, = _matmul("conv_proj_bwd", [dyc], wcp_t, "NN", m=T, n=CONV_CHANNELS, tm=tm, tn=CONV_CHANNELS,
                   epilogue=_store(BF), outs=[(_sds((T, CONV_CHANNELS), BF), _tile(tm, CONV_CHANNELS))])
    gwap_t = dw("attn_proj_dw", dya, o, D, ATTN_WIDTH)
    gwcp_t, db_cp = dw("conv_proj_dw", dyc, c, D, CONV_CHANNELS, colsum=True)
    plan.grad_ready(dict(w_out=gw_out, w_attn_proj=gwap_t, w_conv_proj=gwcp_t))
    (dglu, dcw, dcb, dlng, dlnb), got = _conv_bwd(proj, u_conv, d_c, conv_w, small["conv_b"], small["ln_g"],
                                                  small["ln_b"], carry=plan.carry("conv_bwd"))
    plan.done("conv_bwd", got)
    plan.launch("send_mix")
    (dqkv, dsinks), got = _attn_bwd(proj, d_o, small["sinks"], carry=plan.carry("attn_bwd"))
    plan.done("attn_bwd", got)

    segs = [dqkv, dglu, dg0, dg1]
    gwi_t, db_in = _stacked_dw("proj_in_dw", segs, h, 256)
    plan.grad_ready(dict(w_in=gwi_t))
    plan.alone("swap_inp")
    plan.launch("send_inp")
    carry = plan.carry("proj_in_bwd")
    dx, _, dg_mix = carried(
        "proj_in_bwd",
        _matmul("proj_in_bwd", segs, wi_t, "NN", m=T, n=D, tm=512, tn=D, epilogue=ep_rms_bwd, carry=carry,
                **rms_bwd_io(512, x, r1, small["g_mix_norm"], plan.behind("inp", dx2))), carry)

    parts = dict(g_mix_norm=dg_mix, b_in=db_in, sinks=dsinks, conv_w=dcw, conv_b=dcb, ln_g=dlng, ln_b=dlnb,
                 b_conv_proj=db_cp, g_ffn_norm=dg_ffn, g_final=dg_final, loss=loss)
    return dx, parts


def _place():
    x, y, c = lax.axis_index("x"), lax.axis_index("y"), lax.axis_index("c")
    return x, y, c, [(1 - x, y), (x, 1 - y), (1 - x, 1 - y)]


def _gather_copies(x_refs, out_refs, rows_per, send_sems, recv_sems, local_sems):
    x, y, c, chips = _place()
    me, sibling = (x, y, c), (x, y, 1 - c)

    def rows(a, px, py, pc):
        return out_refs[a].at[pl.ds((4 * px + 2 * py + pc) * rows_per[a], rows_per[a])]

    def copy(a, k, block, to, src=None):
        return pltpu.make_async_remote_copy(
            src_ref=rows(a, *block) if src is None else src, dst_ref=rows(a, *block),
            send_sem=send_sems.at[7 * a + k], recv_sem=recv_sems.at[7 * a + k], device_id=to, device_id_type=MESH)

    def local(a):
        return pltpu.make_async_copy(x_refs[a], rows(a, *me), local_sems.at[a])

    def first(a):
        return [copy(a, 0, me, sibling, src=x_refs[a])] + [copy(a, 1 + j, me, (*chip, c), src=x_refs[a])
                                                          for j, chip in enumerate(chips)]

    def arrive(a, j):
        return copy(a, 1 + j, (*chips[j], c), me)

    def passed(a, j):
        return copy(a, 4 + j, (*chips[j], c), sibling)

    def from_sibling(a):
        return [copy(a, 0, sibling, me)] + [copy(a, 4 + j, (*chip, 1 - c), me) for j, chip in enumerate(chips)]

    return len(x_refs), local, first, arrive, passed, from_sibling


def _gather_start(*refs):
    n, local, first, _, _, _ = _gather_copies(*refs)
    for a in range(n):
        local(a).start()
        for cp in first(a):
            cp.start()


def _gather_finish(*refs):
    n, local, first, arrive, passed, from_sibling = _gather_copies(*refs)
    for a in range(n):
        for j in range(3):
            arrive(a, j).wait_recv()
            passed(a, j).start()
    for a in range(n):
        for cp in from_sibling(a):
            cp.wait_recv()
    for a in range(n):
        for cp in first(a) + [passed(a, j) for j in range(3)]:
            cp.wait_send()
        local(a).wait()


def _gather_peers():
    x, y, c, chips = _place()
    return [(x, y, 1 - c)] + [(*chip, c) for chip in chips]


def _gather_sems(n):
    return [pltpu.SemaphoreType.DMA((7 * n,)), pltpu.SemaphoreType.DMA((7 * n,)), pltpu.SemaphoreType.DMA((n,))]


def _gather_carry(shards):
    rows_per = [s.shape[0] for s in shards]
    return _Carry(shards, [_sds((N_DEV * s.shape[0],) + s.shape[1:], s.dtype) for s in shards],
                  _gather_sems(len(shards)),
                  lambda ins, outs, sems: _gather_start(ins, outs, rows_per, *sems),
                  lambda ins, outs, sems: _gather_finish(ins, outs, rows_per, *sems), _gather_peers)


def _first_gather(shards, x, g):
    n = len(shards)
    rows_per = [s.shape[0] for s in shards]
    T, D = x.shape

    def body(*refs):
        x_refs, (xin_ref, g_ref), out_refs, (h_ref, r_ref) = refs[:n], refs[n:n + 2], refs[n + 2:2 * n + 2], refs[2 * n + 2:2 * n + 4]
        send_sems, recv_sems, local_sems = refs[2 * n + 4:]
        x, y, c, chips = _place()
        me, sibling = (x, y, c), (x, y, 1 - c)
        near_x, near_y, far = (*chips[0], c), (*chips[1], c), (*chips[2], c)

        def rows(a, dev, part):
            h = rows_per[a] // 2
            lo, size = {"all": (0, 2 * h), "low": (0, h), "high": (h, h)}[part]
            return out_refs[a].at[pl.ds((4 * dev[0] + 2 * dev[1] + dev[2]) * rows_per[a] + lo, size)]

        def copy(a, k, block, part, to, src=None):
            return pltpu.make_async_remote_copy(
                src_ref=rows(a, block, part) if src is None else src, dst_ref=rows(a, block, part),
                send_sem=send_sems.at[9 * a + k], recv_sem=recv_sems.at[9 * a + k], device_id=to, device_id_type=MESH)

        other = lambda dev: (dev[0], dev[1], 1 - c)
        sent = []
        for a in range(n):
            pltpu.make_async_copy(x_refs[a], rows(a, me, "all"), local_sems.at[a]).start()
            sent += [copy(a, 0, me, "all", sibling, src=x_refs[a]), copy(a, 1, me, "all", near_x, src=x_refs[a]),
                     copy(a, 2, me, "all", near_y, src=x_refs[a])]
        for cp in sent:
            cp.start()
        for i in range(T // CHUNK):
            rws = slice(i * CHUNK, (i + 1) * CHUNK)
            xv = xin_ref[rws, :]
            r = lax.rsqrt(jnp.mean(xv * xv, axis=-1, keepdims=True) + EPS)
            h_ref[rws, :] = (xv * r * g_ref[...]).astype(BF)
            r_ref[rws, :] = r
        for a in range(n):
            copy(a, 1, near_x, "all", me).wait_recv()
            copy(a, 2, near_y, "all", me).wait_recv()
            passed = [copy(a, 3, near_y, "high", near_x), copy(a, 4, near_x, "low", near_y),
                      copy(a, 5, near_x, "all", sibling), copy(a, 6, near_y, "all", sibling)]
            for cp in passed:
                cp.start()
            sent += passed
        for a in range(n):
            copy(a, 3, far, "high", me).wait_recv()
            copy(a, 4, far, "low", me).wait_recv()
            passed = [copy(a, 7, far, "high", sibling), copy(a, 8, far, "low", sibling)]
            for cp in passed:
                cp.start()
            sent += passed
        for a in range(n):
            copy(a, 0, sibling, "all", me).wait_recv()
            copy(a, 5, other(near_x), "all", me).wait_recv()
            copy(a, 6, other(near_y), "all", me).wait_recv()
            copy(a, 7, other(far), "high", me).wait_recv()
            copy(a, 8, other(far), "low", me).wait_recv()
        for cp in sent:
            cp.wait_send()
        for a in range(n):
            pltpu.make_async_copy(x_refs[a], rows(a, me, "all"), local_sems.at[a]).wait()

    vm = pl.BlockSpec(memory_space=pltpu.VMEM)
    return pl.pallas_call(
        body, name="weights_first_gather", in_specs=[*[ANY] * n, vm, vm], out_specs=[*[ANY] * n, vm, vm],
        out_shape=[*[_sds((N_DEV * s.shape[0],) + s.shape[1:], s.dtype) for s in shards], _sds((T, D), BF),
                   _sds((T, 1), F32)],
        scratch_shapes=[pltpu.SemaphoreType.DMA((9 * n,)), pltpu.SemaphoreType.DMA((9 * n,)),
                        pltpu.SemaphoreType.DMA((n,))],
        compiler_params=pltpu.CompilerParams(vmem_limit_bytes=VMEM_LIMIT_BYTES),
    )(*shards, x, g)


def _swap_carry(grads):
    n = len(grads)

    def copies(g_refs, out_refs, sems):
        send_sems, recv_sems = sems
        x, y, c, _ = _place()
        return [pltpu.make_async_remote_copy(
            src_ref=g_refs[a].at[2 * p + 1 - c], dst_ref=out_refs[a].at[p],
            send_sem=send_sems.at[4 * a + p], recv_sem=recv_sems.at[4 * a + p],
            device_id=(x, y, 1 - c), device_id_type=MESH) for a in range(n) for p in range(4)]

    def start(ins, outs, sems):
        for cp in copies(ins, outs, sems):
            cp.start()

    def finish(ins, outs, sems):
        for cp in copies(ins, outs, sems):
            cp.wait()

    def peers():
        x, y, c, _ = _place()
        return [(x, y, 1 - c)]

    return _Carry(grads, [_sds((4,) + g.shape[1:], g.dtype) for g in grads],
                  [pltpu.SemaphoreType.DMA((4 * n,)), pltpu.SemaphoreType.DMA((4 * n,))], start, finish, peers)


def _join(carries):
    carries = [c for c in carries if c is not None]
    if not carries:
        return None
    n_in = [len(c.arrays) for c in carries]
    n_out = [len(c.out_shapes) for c in carries]
    n_sem = [len(c.sems) for c in carries]

    def parts(refs, counts):
        cuts = [sum(counts[:q]) for q in range(len(counts) + 1)]
        return [refs[cuts[q]:cuts[q + 1]] for q in range(len(counts))]

    def start(ins, outs, sems):
        for c, i, o, s in zip(carries, parts(ins, n_in), parts(outs, n_out), parts(sems, n_sem)):
            c.start(i, o, s)

    def finish(ins, outs, sems):
        for c, i, o, s in zip(carries, parts(ins, n_in), parts(outs, n_out), parts(sems, n_sem)):
            c.finish(i, o, s)

    return _Carry([a for c in carries for a in c.arrays], [o for c in carries for o in c.out_shapes],
                  [s for c in carries for s in c.sems], start, finish)


def _run_carry(name, carry):
    n_in, n_out = len(carry.arrays), len(carry.out_shapes)

    def body(*refs):
        carry.start(refs[:n_in], refs[n_in:n_in + n_out], refs[n_in + n_out:])
        carry.finish(refs[:n_in], refs[n_in:n_in + n_out], refs[n_in + n_out:])

    return pl.pallas_call(body, name=name, in_specs=[ANY] * n_in, out_specs=[ANY] * n_out,
                          out_shape=carry.out_shapes, scratch_shapes=carry.sems)(*carry.arrays)


def _run_carry_async(name, carry, collective_id):
    ins = [jax.new_ref(a, memory_space=pltpu.MemorySpace.HBM) for a in carry.arrays]
    outs = [jax.empty_ref(o, memory_space=pltpu.MemorySpace.HBM) for o in carry.out_shapes]

    @pl.kernel(mesh=plsc.ScalarSubcoreMesh(axis_name="sequencer", num_cores=1), name=name,
               scratch_types=tuple(carry.sems), compiler_params=pltpu.CompilerParams(collective_id=collective_id))
    def launch(*sems):
        barrier = pltpu.get_barrier_semaphore()
        peers = carry.peers()
        for peer in peers:
            pl.semaphore_signal(barrier, inc=1, device_id=peer, device_id_type=MESH)
        pl.semaphore_wait(barrier, len(peers))
        carry.start(ins, outs, sems)
        carry.finish(ins, outs, sems)

    launch()
    return [o[...] for o in outs]


def _chip_sums(name, gs, gots, c):
    n = len(gs)

    def body(c_ref, *refs):
        for g_ref, got_ref, o_ref in zip(refs[:n], refs[n:2 * n], refs[2 * n:]):
            o_ref[...] = (g_ref[...].astype(F32) + got_ref[...].astype(F32)).astype(BF)

    mine = [pl.BlockSpec((1,) + g.shape[1:], lambda p, c_ref: (2 * p + c_ref[0], 0, 0)) for g in gs]
    slot = [pl.BlockSpec((1,) + g.shape[1:], lambda p, c_ref: (p, 0, 0)) for g in gs]
    return pl.pallas_call(
        body, name=name,
        grid_spec=pltpu.PrefetchScalarGridSpec(num_scalar_prefetch=1, grid=(4,), in_specs=[*mine, *slot],
                                               out_specs=slot),
        out_shape=[_sds((4,) + g.shape[1:], BF) for g in gs],
        compiler_params=_params(("arbitrary",)),
    )(c, *gs, *gots)


def _send_carry(sums, ks):
    n, nk = len(sums), len(ks)

    def copies(s_refs, out_refs, sems):
        send_sems, recv_sems = sems
        x, y, c, chips = _place()
        return [pltpu.make_async_remote_copy(
            src_ref=s_refs[a].at[2 * chips[k][0] + chips[k][1]], dst_ref=out_refs[a].at[q],
            send_sem=send_sems.at[nk * a + q], recv_sem=recv_sems.at[nk * a + q],
            device_id=(*chips[k], c), device_id_type=MESH) for a in range(n) for q, k in enumerate(ks)]

    def start(ins, outs, sems):
        for cp in copies(ins, outs, sems):
            cp.start()

    def finish(ins, outs, sems):
        for cp in copies(ins, outs, sems):
            cp.wait()

    def peers():
        x, y, c, chips = _place()
        return [(*chips[k], c) for k in ks]

    return _Carry(sums, [_sds((nk,) + s.shape[1:], s.dtype) for s in sums],
                  [pltpu.SemaphoreType.DMA((nk * n,)), pltpu.SemaphoreType.DMA((nk * n,))], start, finish, peers)


def _adam_math(w, g, m, v):
    m = ADAM_B1 * m + (1.0 - ADAM_B1) * g
    v = ADAM_B2 * v + (1.0 - ADAM_B2) * (g * g)
    m_hat = m / (1.0 - ADAM_B1 ** ADAM_STEP)
    v_hat = v / (1.0 - ADAM_B2 ** ADAM_STEP)
    delta = -ADAM_LR * (m_hat / (jnp.sqrt(v_hat) + ADAM_EPS) + ADAM_WD * w)
    return delta, m, v


def _adamw(name, w, g, m, v):
    rows, cols = w.shape
    tr = 256 if rows % 256 == 0 else rows

    def body(w_ref, g_ref, m_ref, v_ref, d_ref, nm_ref, nv_ref):
        d_ref[...], nm_ref[...], nv_ref[...] = _adam_math(w_ref[...], g_ref[...], m_ref[...], v_ref[...])

    t = pl.BlockSpec((tr, cols), lambda i: (i, 0))
    return pl.pallas_call(
        body, name=name, grid=(rows // tr,), in_specs=[t] * 4, out_specs=[t] * 3,
        out_shape=[_sds((rows, cols), F32)] * 3, compiler_params=_params(("arbitrary",)),
    )(w, g, m, v)


def _grad_adamw(name, g, got, got3, ids, w, m, v):
    _, rows, cols = g.shape
    n3 = len(got3)
    tr = rows // 2 if rows >= 256 else rows

    def body(ids_ref, g_ref, got_ref, *rest):
        w_ref, m_ref, v_ref, o_ref, d_ref, nm_ref, nv_ref = rest[n3:]
        tot = g_ref[0].astype(F32) + got_ref[0].astype(F32)
        for r_ref in rest[:n3]:
            for q in range(r_ref.shape[0]):
                tot = tot + r_ref[q].astype(F32)
        o_ref[...] = tot
        d_ref[...], nm_ref[...], nv_ref[...] = _adam_math(w_ref[...], tot, m_ref[...], v_ref[...])

    tile = pl.BlockSpec((tr, cols), lambda i, ids_ref: (i, 0))
    return pl.pallas_call(
        body, name=name,
        grid_spec=pltpu.PrefetchScalarGridSpec(
            num_scalar_prefetch=1, grid=(rows // tr,),
            in_specs=[pl.BlockSpec((1, tr, cols), lambda i, ids_ref: (ids_ref[0], i, 0)),
                      pl.BlockSpec((1, tr, cols), lambda i, ids_ref: (ids_ref[1], i, 0)),
                      *[pl.BlockSpec((r.shape[0], tr, cols), lambda i, ids_ref: (0, i, 0)) for r in got3],
                      tile, tile, tile],
            out_specs=[tile] * 4),
        out_shape=[_sds((rows, cols), F32)] * 4,
        compiler_params=_params(("arbitrary",)),
    )(ids, g, got, *got3, w, m, v)


SMALL_NAMES = ["g_mix_norm", "b_in", "sinks", "conv_b", "ln_g", "ln_b", "b_conv_proj", "g_ffn_norm", "g_final"]
_PACK_ROWS = 32


def _small_pack(parts):
    C = CONV_CHANNELS
    part_list = [parts["g_mix_norm"], parts["b_in"], parts["sinks"], parts["conv_b"], parts["ln_g"], parts["ln_b"],
                 parts["b_conv_proj"], parts["g_ffn_norm"], parts["g_final"], parts["loss"], parts["conv_w"]]

    def body(p_mix, p_b, p_sink, p_cb, p_lg, p_lb, p_bcp, p_ffn, p_fin, p_loss, p_cw, pack):
        pack[...] = jnp.zeros_like(pack)
        pack[0:1, :] = p_mix[...]
        pack[1:2, 0:GLU_OFF] = p_b[:, 0:GLU_OFF]
        pack[2:3, :] = p_b[:, GLU_OFF:GATE_OFF]
        pack[3:4, :] = p_b[:, GATE_OFF:GATE_OFF + D_MODEL]
        pack[4:5, :] = p_b[:, GATE_OFF + D_MODEL:]
        pack[5:6, 0:128] = p_sink[...]
        pack[6:7, 0:C] = p_cb[...]
        pack[6:7, C:2 * C] = p_lg[...]
        pack[7:8, 0:C] = p_lb[...]
        pack[8:9, :] = p_bcp[...]
        pack[9:10, :] = p_ffn[...]
        pack[10:11, :] = p_fin[...]
        pack[11:12, 0:128] = jnp.broadcast_to(p_loss[...], (1, 128))
        pack[12:28, 0:C] = p_cw[0:16, :]
        pack[12:28, C:2 * C] = p_cw[16:32, :]

    vm = pl.BlockSpec(memory_space=pltpu.VMEM)
    return pl.pallas_call(body, name="small_pack", in_specs=[vm] * len(part_list), out_specs=vm,
                          out_shape=_sds((_PACK_ROWS, D_MODEL), F32))(*part_list)


def _small_adamw(gathered, small_w, small_m, small_v):
    C = CONV_CHANNELS
    names = SMALL_NAMES
    widths = [small_w[k].shape[1] for k in names]
    n_small = len(names)

    def body(*refs):
        tot_ref = refs[0]
        w_refs = refs[1:1 + n_small]
        m_refs = refs[1 + n_small:1 + 2 * n_small]
        v_refs = refs[1 + 2 * n_small:1 + 3 * n_small]
        o = 1 + 3 * n_small
        loss_ref, cw_ref = refs[o], refs[o + 1]
        out_refs = refs[o + 2:o + 2 + 4 * n_small]
        tot = tot_ref[0:_PACK_ROWS, :]
        for d in range(1, N_DEV):
            tot = tot + tot_ref[d * _PACK_ROWS:(d + 1) * _PACK_ROWS, :]
        loss_ref[...] = tot[11:12, 0:1]
        cw_ref[0:16, :] = tot[12:28, 0:C]
        cw_ref[16:32, :] = tot[12:28, C:2 * C]
        grads = dict(
            g_mix_norm=tot[0:1, :],
            b_in=jnp.concatenate([tot[1:2, 0:GLU_OFF], tot[2:3, :], tot[3:4, :], tot[4:5, :]], axis=1),
            sinks=tot[5:6, 0:N_Q_HEADS], conv_b=tot[6:7, 0:C], ln_g=tot[6:7, C:2 * C], ln_b=tot[7:8, 0:C],
            b_conv_proj=tot[8:9, :], g_ffn_norm=tot[9:10, :], g_final=tot[10:11, :])
        for s, k in enumerate(names):
            g = grads[k]
            d, nm, nv = _adam_math(w_refs[s][...], g, m_refs[s][...], v_refs[s][...])
            out_refs[4 * s][...] = g
            out_refs[4 * s + 1][...] = d
            out_refs[4 * s + 2][...] = nm
            out_refs[4 * s + 3][...] = nv

    vm = pl.BlockSpec(memory_space=pltpu.VMEM)
    args = [gathered, *[small_w[k] for k in names], *[small_m[k] for k in names], *[small_v[k] for k in names]]
    out_shape = [_sds((1, 1), F32), _sds((CONV_PAD, C), F32)]
    for wd in widths:
        out_shape += [_sds((1, wd), F32)] * 4
    res = pl.pallas_call(
        body, name="small_adamw",
        in_specs=[vm] * len(args), out_specs=[vm] * len(out_shape), out_shape=out_shape,
        compiler_params=pltpu.CompilerParams(vmem_limit_bytes=VMEM_LIMIT_BYTES),
    )(*args)
    return res[0], res[1], {k: res[2 + 4 * s:6 + 4 * s] for s, k in enumerate(names)}


BIG = dict(w_in=True, w_attn_proj=True, w_conv_proj=True, w_out=False, w_ffn_in=True, w_ffn_down=False)
WEIGHT_NAMES = ["g_mix_norm", "w_in", "b_in", "sinks", "conv_w", "conv_b", "ln_g", "ln_b", "w_attn_proj",
                "w_conv_proj", "b_conv_proj", "w_out", "g_ffn_norm", "w_ffn_in", "w_ffn_down", "g_final"]


class _Plan:
    GROUPS = dict(down=["w_ffn_down"], ffn=["w_ffn_in"], mix=["w_out", "w_attn_proj", "w_conv_proj"], inp=["w_in"])
    ALL = (0, 1, 2)
    RIDES = dict(
        gather_mix=[("gather", ["w_attn_proj", "w_conv_proj", "w_out"])], gather_ffn=[("gather", ["w_ffn_in"])],
        gather_down=[("gather", ["w_ffn_down"])],
        ffn_in_bwd=[("swap", "down")], send_down=[("send", "down", ALL)],
        out_proj_bwd_merge=[("swap", "ffn")], send_ffn=[("send", "ffn", ALL)],
        conv_bwd=[("swap", "mix")], send_mix=[("send", "mix", ALL)],
        swap_inp=[("swap", "inp")], send_inp=[("send", "inp", ALL)])
    ASYNC = dict(gather_mix=1, gather_ffn=2, gather_down=3, send_down=4, send_ffn=5, send_mix=6, send_inp=7)

    def __init__(self, shards, c1):
        self.shards, self.c1 = shards, c1
        self.full, self.slots, self.got, self.sums, self.got3 = {}, {}, {}, {}, {}

    def weight(self, name):
        return self.full[name]

    def grad_ready(self, grads):
        for k, g in grads.items():
            self.slots[k] = g.reshape(N_DEV, g.shape[0] // N_DEV, g.shape[1])

    def _one(self, kind, what, ks=None):
        if kind == "gather":
            return _gather_carry([self.shards[k] for k in what])
        names = self.GROUPS[what]
        if kind == "swap":
            return _swap_carry([self.slots[k] for k in names])
        return _send_carry([self.sums[k] for k in names], ks)

    def carry(self, call):
        return _join([self._one(*ride) for ride in self.RIDES.get(call, [])])

    def done(self, call, outs):
        outs = list(outs)
        for kind, what, *_ in self.RIDES.get(call, []):
            names = what if kind == "gather" else self.GROUPS[what]
            mine, outs = outs[:len(names)], outs[len(names):]
            if kind == "gather":
                self.full.update(zip(names, mine))
            elif kind == "send":
                for k, r in zip(names, mine):
                    self.got3.setdefault(k, []).append(r)
            else:
                self.got.update(zip(names, mine))
                self.sums.update(zip(names, _chip_sums(f"chip_sums_{what}", [self.slots[k] for k in names], mine, self.c1)))

    def alone(self, call):
        self.done(call, _run_carry(call, self.carry(call)))

    def behind(self, group, x):
        return lax.optimization_barrier((x, tuple(self.sums[k] for k in self.GROUPS[group])))[0]

    def launch(self, call, after=None):
        carry = self._one(*self.RIDES[call][0])
        if after is not None:
            carry.arrays = list(lax.optimization_barrier((tuple(carry.arrays), after))[0])
        self.done(call, _run_carry_async(call, carry, self.ASYNC[call]))


def kernel(x, g_mix_norm, w_in, b_in, sinks, conv_w, conv_b, ln_g, ln_b, w_attn_proj, w_conv_proj, b_conv_proj, w_out, g_ffn_norm, w_ffn_in, w_ffn_down, g_final, loss_target, m_g_mix_norm, m_w_in, m_b_in, m_sinks, m_conv_w, m_conv_b, m_ln_g, m_ln_b, m_w_attn_proj, m_w_conv_proj, m_b_conv_proj, m_w_out, m_g_ffn_norm, m_w_ffn_in, m_w_ffn_down, m_g_final, v_g_mix_norm, v_w_in, v_b_in, v_sinks, v_conv_w, v_conv_b, v_ln_g, v_ln_b, v_w_attn_proj, v_w_conv_proj, v_b_conv_proj, v_w_out, v_g_ffn_norm, v_w_ffn_in, v_w_ffn_down, v_g_final):
    w = dict(g_mix_norm=g_mix_norm, w_in=w_in, b_in=b_in, sinks=sinks, conv_w=conv_w, conv_b=conv_b, ln_g=ln_g,
             ln_b=ln_b, w_attn_proj=w_attn_proj, w_conv_proj=w_conv_proj, b_conv_proj=b_conv_proj, w_out=w_out,
             g_ffn_norm=g_ffn_norm, w_ffn_in=w_ffn_in, w_ffn_down=w_ffn_down, g_final=g_final)
    m = dict(g_mix_norm=m_g_mix_norm, w_in=m_w_in, b_in=m_b_in, sinks=m_sinks, conv_w=m_conv_w, conv_b=m_conv_b,
             ln_g=m_ln_g, ln_b=m_ln_b, w_attn_proj=m_w_attn_proj, w_conv_proj=m_w_conv_proj,
             b_conv_proj=m_b_conv_proj, w_out=m_w_out, g_ffn_norm=m_g_ffn_norm, w_ffn_in=m_w_ffn_in,
             w_ffn_down=m_w_ffn_down, g_final=m_g_final)
    v = dict(g_mix_norm=v_g_mix_norm, w_in=v_w_in, b_in=v_b_in, sinks=v_sinks, conv_w=v_conv_w, conv_b=v_conv_b,
             ln_g=v_ln_g, ln_b=v_ln_b, w_attn_proj=v_w_attn_proj, w_conv_proj=v_w_conv_proj,
             b_conv_proj=v_b_conv_proj, w_out=v_w_out, g_ffn_norm=v_g_ffn_norm, w_ffn_in=v_w_ffn_in,
             w_ffn_down=v_w_ffn_down, g_final=v_g_final)
    ax, ay, ac = lax.axis_index("x"), lax.axis_index("y"), lax.axis_index("c")
    me = 4 * ax + 2 * ay + ac
    chip = 2 * ax + ay

    shards = {k: (w[k][0].T if tr else w[k][0]).astype(BF) for k, tr in BIG.items()}
    cw_shard = jnp.pad(conv_w[0].T, ((0, 0), (0, 1))).reshape(16, 128)
    wi_t, cw_full, h, r1 = _first_gather([shards["w_in"], cw_shard], x[0], g_mix_norm)
    conv_full = cw_full.reshape(CONV_CHANNELS, CONV_PAD).T

    as_row = lambda a: a.reshape(1, -1)
    small_w = {k: as_row(w[k]) for k in SMALL_NAMES}
    small_m = {k: as_row(m[k]) for k in SMALL_NAMES}
    small_v = {k: as_row(v[k]) for k in SMALL_NAMES}
    plan = _Plan(shards, ac.reshape(1).astype(jnp.int32))
    plan.launch("gather_mix", after=wi_t)
    dx, parts = _local_step(x[0], h, r1, loss_target[0], small_w, wi_t, conv_full, plan)

    ids = jnp.stack([me, chip]).astype(jnp.int32)
    grads, delta, new_m, new_v, after = {}, {}, {}, {}, dx
    packed = _small_pack(parts)
    for k in sorted(BIG, key=lambda k: k == "w_in"):
        if k == "w_in":
            packed = lax.optimization_barrier((packed, after))[0]
            small_gathered, = _run_carry_async("small_gather", _gather_carry([packed]), 8)
        flip = (lambda a: a.T) if BIG[k] else (lambda a: a)
        wk = lax.optimization_barrier((w[k][0], after))[0]
        outs = _grad_adamw(f"grad_adamw_{k}", plan.slots[k], plan.got[k], plan.got3[k], ids,
                           flip(wk), flip(m[k][0]), flip(v[k][0]))
        after = outs[0]
        grads[k], delta[k], new_m[k], new_v[k] = (flip(a)[None] for a in outs)

    loss, cw_grad, small_out = _small_adamw(small_gathered, small_w, small_m, small_v)
    for k in SMALL_NAMES:
        g, d, nm, nv = (a.reshape(w[k].shape) for a in small_out[k])
        grads[k], delta[k], new_m[k], new_v[k] = g, d, nm, nv
    cw_mine = lax.dynamic_slice(cw_grad, (0, me * 64), (CONV_WIDTH, 64))
    d, nm, nv = _adamw("adamw_conv_w", conv_w[0], cw_mine, m_conv_w[0], v_conv_w[0])
    grads["conv_w"], delta["conv_w"], new_m["conv_w"], new_v["conv_w"] = cw_mine[None], d[None], nm[None], nv[None]

    return (loss.reshape(()), dx[None], *[grads[k] for k in WEIGHT_NAMES], *[delta[k] for k in WEIGHT_NAMES],
            *[new_m[k] for k in WEIGHT_NAMES], *[new_v[k] for k in WEIGHT_NAMES])
```

```python
import functools

import jax
import jax.numpy as jnp
from jax import lax
from jax.experimental import pallas as pl
from jax.experimental.pallas import tpu as pltpu
from jax.experimental.pallas import tpu_sc as plsc

F32 = jnp.float32
BF = jnp.bfloat16

SEQ = 2048
D_MODEL = 1024
HEAD_DIM = 64
N_Q_HEADS = 8
N_KV_HEADS = 2
GROUP = N_Q_HEADS // N_KV_HEADS
BLOCK = 128
ATTN_WIDTH = 512
KV_WIDTH = 128
CONV_CHANNELS = 512
CONV_WIDTH = 31
CONV_PAD = 32
GLU_OFF = 768
GATE_OFF = 1792
IN_WIDTH = 3840
D_FF = 2816
EPS = 1e-5
NEG = -1e30
N_DEV = 8

ADAM_LR = 0.001
ADAM_B1 = 0.9
ADAM_B2 = 0.999
ADAM_EPS = 1e-08
ADAM_WD = 0.01
ADAM_STEP = 10

VMEM_LIMIT_BYTES = 56 * 1024 * 1024
MESH = pl.DeviceIdType.MESH
ANY = pl.BlockSpec(memory_space=pl.ANY)

_DIMS = {"NN": (((1,), (0,)), ((), ())), "NT": (((1,), (1,)), ((), ())), "TN": (((0,), (0,)), ((), ()))}


def _params(sem):
    return pltpu.CompilerParams(dimension_semantics=sem, vmem_limit_bytes=VMEM_LIMIT_BYTES)


class _Carry:
    def __init__(self, arrays, out_shapes, sems, start, finish, peers=None):
        self.arrays, self.out_shapes, self.sems, self.start, self.finish = arrays, out_shapes, sems, start, finish
        self.peers = peers


def _carry_io(carry):
    if carry is None:
        return [], [], []
    return list(carry.arrays), list(carry.out_shapes), list(carry.sems)


def _matmul(name, a_list, b, mode, *, m, n, tm, tn, tk=None, epilogue, extra=(), outs, b_off=(0, 0), alias=None,
            scratch=(), carry=None):
    seg_k = [a.shape[0] if mode == "TN" else a.shape[1] for a in a_list]
    whole = tk is None
    seg_nk = [1] * len(a_list) if whole else [ks // tk for ks in seg_k]
    nk = 1 if whole else sum(seg_nk)
    starts = [sum(seg_nk[:s]) for s in range(len(seg_nk))]
    k_starts = [sum(seg_k[:s]) for s in range(len(seg_k))]
    k_tot = sum(seg_k)
    n_a, n_extra, n_out = len(a_list), len(extra), len(outs)

    a_specs = []
    for st, ns, ks in zip(starts, seg_nk, seg_k):
        if mode == "TN":
            a_specs.append(pl.BlockSpec((ks if whole else tk, tm), lambda j, i, k: (k, i)))
        elif whole:
            a_specs.append(pl.BlockSpec((tm, ks), lambda j, i, k: (i, 0)))
        else:
            a_specs.append(pl.BlockSpec((tm, tk), functools.partial(
                lambda j, i, k, st, ns: (i, jnp.clip(k - st, 0, ns - 1)), st=st, ns=ns)))
    bk = k_tot if whole else tk
    if mode == "NT":
        b_spec = pl.BlockSpec((tn, bk), lambda j, i, k: (b_off[0] + j, b_off[1] + k))
    else:
        b_spec = pl.BlockSpec((bk, tn), lambda j, i, k: (b_off[0] + k, b_off[1] + j))
    n_alias = 0 if alias is None else 1
    c_in, c_out, c_sems = _carry_io(carry)
    n_acc = 0 if whole else 1
    nj, ni = n // tn, m // tm

    def body(*refs):
        pos = [n_a, 1, n_alias, n_extra, len(c_in), n_out, len(c_out), n_acc, len(scratch), len(c_sems)]
        cuts = [sum(pos[:q]) for q in range(len(pos) + 1)]
        a_refs, (b_ref,), _, ex, ci_refs, out_refs, co_refs, acc_refs, scr, cs_refs = (
            refs[cuts[q]:cuts[q + 1]] for q in range(len(pos)))
        j, i, k = pl.program_id(0), pl.program_id(1), pl.program_id(2)
        ids = (j, i)
        if carry is not None:
            @pl.when((j == 0) & (i == 0) & (k == 0))
            def _():
                carry.start(ci_refs, co_refs, cs_refs)

        def dot(a_ref, bv):
            return lax.dot_general(a_ref[...].astype(BF), bv.astype(BF), _DIMS[mode], preferred_element_type=F32)

        if whole:
            tot = None
            for a_ref, k0, ks in zip(a_refs, k_starts, seg_k):
                if n_a == 1:
                    bv = b_ref[...]
                else:
                    bv = b_ref[:, k0:k0 + ks] if mode == "NT" else b_ref[k0:k0 + ks, :]
                part = dot(a_ref, bv)
                tot = part if tot is None else tot + part
            epilogue(tot, ex, out_refs, ids, scr)
        else:
            acc, = acc_refs

            @pl.when(k == 0)
            def _():
                acc[...] = jnp.zeros_like(acc)

            for a_ref, st, ns in zip(a_refs, starts, seg_nk):
                if n_a == 1:
                    acc[...] += dot(a_ref, b_ref[...])
                else:
                    @pl.when((k >= st) & (k < st + ns))
                    def _(a_ref=a_ref):
                        acc[...] += dot(a_ref, b_ref[...])

            @pl.when(k == nk - 1)
            def _():
                epilogue(acc[...], ex, out_refs, ids, scr)

        if carry is not None:
            @pl.when((j == nj - 1) & (i == ni - 1) & (k == nk - 1))
            def _():
                carry.finish(ci_refs, co_refs, cs_refs)

    in_specs = [*a_specs, b_spec]
    args = [*a_list, b]
    io_alias = {}
    if alias is not None:
        in_specs.append(pl.BlockSpec(memory_space=pl.ANY))
        args.append(alias[0])
        io_alias = {n_a + 1: alias[1]}
    in_specs += [s for _, s in extra] + [pl.BlockSpec(memory_space=pl.ANY)] * len(c_in)
    args += [x for x, _ in extra] + c_in
    res = pl.pallas_call(
        body, name=name, grid=(nj, ni, nk), in_specs=in_specs,
        out_specs=[s for _, s in outs] + [pl.BlockSpec(memory_space=pl.ANY)] * len(c_out),
        out_shape=[o for o, _ in outs] + c_out,
        scratch_shapes=[*([] if whole else [pltpu.VMEM((tm, tn), F32)]), *scratch, *c_sems],
        input_output_aliases=io_alias,
        compiler_params=_params(("arbitrary", "arbitrary", "arbitrary")),
    )(*args)
    return res if carry is None else (res[:n_out], res[n_out:])


def _tile(tm, tn):
    return pl.BlockSpec((tm, tn), lambda j, i, k: (i, j))


def _row(tn):
    return pl.BlockSpec((1, tn), lambda j, i, k: (0, j))


def _store(dtype):
    def ep(acc, ex, outs, ids, scr):
        outs[0][...] = acc.astype(dtype)
    return ep


def _sds(shape, dtype):
    return jax.ShapeDtypeStruct(shape, dtype)


def _rms_bwd(dh, xv, r, g):
    xh = xv * r
    dxh = dh * g
    dx = r * (dxh - xh * jnp.mean(dxh * xh, axis=-1, keepdims=True))
    return dx, jnp.sum(dh * xh, axis=0, keepdims=True)


def _accumulate_rows(ref, val, first):
    @pl.when(first)
    def _():
        ref[...] = val

    @pl.when(jnp.logical_not(first))
    def _():
        ref[...] += val


def _loss_head(xv, g, target):
    r = lax.rsqrt(jnp.mean(xv * xv, axis=-1, keepdims=True) + EPS)
    err = xv * r * g - target
    dx, dg = _rms_bwd(err * (1.0 / xv.shape[-1]), xv, r, g)
    part = 0.5 * jnp.sum(jnp.mean(err * err, axis=-1, keepdims=True), axis=0, keepdims=True)
    return dx, dg, part


def _lane_half(shape, h):
    lane = lax.broadcasted_iota(jnp.int32, shape, 1)
    return (lane >= HEAD_DIM * h) & (lane < HEAD_DIM * (h + 1))


def _to_half(v, w, h):
    if w != h:
        v = pltpu.roll(v, HEAD_DIM, 1)
    return jnp.where(_lane_half(v.shape, h), v, 0.0)


def _attn_block(qkv_ref, sinks_ref, n, h):
    r0 = pl.multiple_of(n * BLOCK, BLOCK)
    p0 = pl.multiple_of(jnp.maximum(n - 1, 0) * BLOCK, BLOCK)
    rows = pl.ds(r0, BLOCK)
    prev = pl.ds(p0, BLOCK)
    k2 = jnp.concatenate([qkv_ref[prev, ATTN_WIDTH:ATTN_WIDTH + KV_WIDTH],
                          qkv_ref[rows, ATTN_WIDTH:ATTN_WIDTH + KV_WIDTH]], axis=0)
    v2 = jnp.concatenate([qkv_ref[prev, ATTN_WIDTH + KV_WIDTH:ATTN_WIDTH + 2 * KV_WIDTH],
                          qkv_ref[rows, ATTN_WIDTH + KV_WIDTH:ATTN_WIDTH + 2 * KV_WIDTH]], axis=0)
    qs = []
    for g in range(GROUP):
        hq = GROUP * h + g
        blk = qkv_ref[rows, (hq // 2) * 128:(hq // 2 + 1) * 128].astype(F32)
        qs.append(_to_half(blk, hq % 2, h))
    q4 = jnp.concatenate(qs, axis=0).astype(BF)
    s = lax.dot_general(q4, k2, _DIMS["NT"], preferred_element_type=F32) * (HEAD_DIM ** -0.5)
    shape = s.shape
    row = lax.broadcasted_iota(jnp.int32, shape, 0)
    qi = row & (BLOCK - 1)
    kj = lax.broadcasted_iota(jnp.int32, shape, 1)
    diff = qi + BLOCK - kj
    valid = (diff >= 0) & (diff < BLOCK) & ((kj >= BLOCK) | (n > 0))
    s = jnp.where(valid, s, NEG)
    row1 = lax.broadcasted_iota(jnp.int32, (shape[0], 1), 0)
    sink = jnp.zeros((shape[0], 1), F32)
    for g in range(GROUP):
        sink = jnp.where((row1 >= g * BLOCK) & (row1 < (g + 1) * BLOCK), sinks_ref[0, GROUP * h + g], sink)
    m = jnp.maximum(jnp.max(s, axis=-1, keepdims=True), sink)
    e = jnp.exp(s - m)
    es = jnp.exp(sink - m)
    inv = 1.0 / (jnp.sum(e, axis=-1, keepdims=True) + es)
    return e * inv, es * inv, q4, k2, v2, rows, prev


def _attn_fwd(proj, sinks, carry=None):
    T = proj.shape[0]
    c_in, c_out, c_sems = _carry_io(carry)

    def body(*refs):
        qkv_ref, sinks_ref = refs[:2]
        ci_refs = refs[2:2 + len(c_in)]
        o_ref = refs[2 + len(c_in)]
        co_refs = refs[3 + len(c_in):3 + len(c_in) + len(c_out)]
        cs_refs = refs[3 + len(c_in) + len(c_out):]
        if carry is not None:
            carry.start(ci_refs, co_refs, cs_refs)

        def blk(n, z):
            outs = [None] * (N_Q_HEADS // 2)
            for h in range(N_KV_HEADS):
                p, _, _, _, v2, rows, _ = _attn_block(qkv_ref, sinks_ref, n, h)
                o = lax.dot_general(p.astype(BF), v2, _DIMS["NN"], preferred_element_type=F32)
                for g in range(GROUP):
                    hq = GROUP * h + g
                    piece = jnp.where(_lane_half((BLOCK, 128), h), o[g * BLOCK:(g + 1) * BLOCK], 0.0)
                    if hq % 2 != h:
                        piece = pltpu.roll(piece, HEAD_DIM, 1)
                    outs[hq // 2] = piece if outs[hq // 2] is None else outs[hq // 2] + piece
            for pb in range(N_Q_HEADS // 2):
                o_ref[rows, pb * 128:(pb + 1) * 128] = outs[pb].astype(BF)
            return z

        lax.fori_loop(0, T // BLOCK, blk, 0)
        if carry is not None:
            carry.finish(ci_refs, co_refs, cs_refs)

    res = pl.pallas_call(
        body, name="attn_fwd", grid=(1,),
        in_specs=[pl.BlockSpec((T, GLU_OFF), lambda i: (0, 0)), pl.BlockSpec(memory_space=pltpu.SMEM),
                  *[ANY] * len(c_in)],
        out_specs=[pl.BlockSpec((T, ATTN_WIDTH), lambda i: (0, 0)), *[ANY] * len(c_out)],
        out_shape=[_sds((T, ATTN_WIDTH), BF), *c_out], scratch_shapes=c_sems,
        compiler_params=_params(("arbitrary",)),
    )(proj, sinks, *c_in)
    return res[0], res[1:]


def _attn_bwd(proj, d_o, sinks, carry=None):
    T = proj.shape[0]
    c_in, c_out, c_sems = _carry_io(carry)

    def body(*refs):
        qkv_ref, do_ref, sinks_ref = refs[:3]
        ci_refs = refs[3:3 + len(c_in)]
        dqkv_ref, dsink_ref = refs[3 + len(c_in):5 + len(c_in)]
        co_refs = refs[5 + len(c_in):5 + len(c_in) + len(c_out)]
        dk_acc, dv_acc = refs[5 + len(c_in) + len(c_out):7 + len(c_in) + len(c_out)]
        cs_refs = refs[7 + len(c_in) + len(c_out):]
        if carry is not None:
            carry.start(ci_refs, co_refs, cs_refs)
        dsink_ref[...] = jnp.zeros_like(dsink_ref)
        dk_acc[...] = jnp.zeros_like(dk_acc)
        dv_acc[...] = jnp.zeros_like(dv_acc)

        def blk(n, carry):
            dqs = [None] * (N_Q_HEADS // 2)
            for h in range(N_KV_HEADS):
                p, psink, q4, k2, v2, rows, prev = _attn_block(qkv_ref, sinks_ref, n, h)
                dos = []
                for g in range(GROUP):
                    hq = GROUP * h + g
                    dos.append(_to_half(do_ref[rows, (hq // 2) * 128:(hq // 2 + 1) * 128].astype(F32), hq % 2, h))
                do4 = jnp.concatenate(dos, axis=0).astype(BF)
                dp = lax.dot_general(do4, v2, _DIMS["NT"], preferred_element_type=F32)
                delta = jnp.sum(p * dp, axis=-1, keepdims=True)
                ds = (p * (dp - delta) * (HEAD_DIM ** -0.5)).astype(BF)
                dsk = psink * delta
                for g in range(GROUP):
                    hq = GROUP * h + g
                    tot = -jnp.sum(dsk[g * BLOCK:(g + 1) * BLOCK], axis=0, keepdims=True)
                    lane = lax.broadcasted_iota(jnp.int32, (1, 128), 1)
                    dsink_ref[...] += jnp.where(lane == hq, tot, 0.0)
                dq = lax.dot_general(ds, k2, _DIMS["NN"], preferred_element_type=F32)
                dk = lax.dot_general(ds, q4, _DIMS["TN"], preferred_element_type=F32)
                dv = lax.dot_general(p.astype(BF), do4, _DIMS["TN"], preferred_element_type=F32)
                dk_acc[prev, :] += dk[:BLOCK]
                dk_acc[rows, :] += dk[BLOCK:]
                dv_acc[prev, :] += dv[:BLOCK]
                dv_acc[rows, :] += dv[BLOCK:]
                for g in range(GROUP):
                    hq = GROUP * h + g
                    piece = jnp.where(_lane_half((BLOCK, 128), h), dq[g * BLOCK:(g + 1) * BLOCK], 0.0)
                    if hq % 2 != h:
                        piece = pltpu.roll(piece, HEAD_DIM, 1)
                    dqs[hq // 2] = piece if dqs[hq // 2] is None else dqs[hq // 2] + piece
            for pb in range(N_Q_HEADS // 2):
                dqkv_ref[rows, pb * 128:(pb + 1) * 128] = dqs[pb].astype(BF)
            return carry

        lax.fori_loop(0, T // BLOCK, blk, 0)
        dqkv_ref[:, ATTN_WIDTH:ATTN_WIDTH + KV_WIDTH] = dk_acc[...].astype(BF)
        dqkv_ref[:, ATTN_WIDTH + KV_WIDTH:] = dv_acc[...].astype(BF)
        if carry is not None:
            carry.finish(ci_refs, co_refs, cs_refs)

    res = pl.pallas_call(
        body, name="attn_bwd", grid=(1,),
        in_specs=[pl.BlockSpec((T, GLU_OFF), lambda i: (0, 0)), pl.BlockSpec((T, ATTN_WIDTH), lambda i: (0, 0)),
                  pl.BlockSpec(memory_space=pltpu.SMEM), *[ANY] * len(c_in)],
        out_specs=[pl.BlockSpec((T, GLU_OFF), lambda i: (0, 0)), pl.BlockSpec((1, 128), lambda i: (0, 0)),
                   *[ANY] * len(c_out)],
        out_shape=[_sds((T, GLU_OFF), BF), _sds((1, 128), F32), *c_out],
        scratch_shapes=[pltpu.VMEM((T, KV_WIDTH), F32), pltpu.VMEM((T, KV_WIDTH), F32), *c_sems],
        compiler_params=_params(("arbitrary",)),
    )(proj, d_o, sinks, *c_in)
    return res[:2], res[2:]


CHUNK = 256
SUB = 32
WIN = CHUNK + 32
PAD_ROWS = SEQ + 2 * CONV_PAD
_GLU_SPECS = [pl.BlockSpec((SEQ, 256), functools.partial(lambda i, c: (0, c), c=GLU_OFF // 256 + c)) for c in range(4)]


def _glu_to_pad(a0, a1, b0, b1, zpad):
    C = CONV_CHANNELS
    zpad[0:CONV_PAD, :] = jnp.zeros((CONV_PAD, C), F32)
    zpad[CONV_PAD + SEQ:, :] = jnp.zeros((CONV_PAD, C), F32)
    zpad[CONV_PAD:CONV_PAD + SEQ, 0:256] = a0[...].astype(F32) * jax.nn.sigmoid(b0[...].astype(F32))
    zpad[CONV_PAD:CONV_PAD + SEQ, 256:C] = a1[...].astype(F32) * jax.nn.sigmoid(b1[...].astype(F32))


def _tap_windows(src, base, win):
    for b in range(8):
        win[b, 0:WIN - 8, :] = src[base + b:base + b + WIN - 8, :]


def _taps(win, w_ref, init, out, flip):
    def sub(si, carry):
        r0 = pl.multiple_of(si * SUB, SUB)
        acc = jnp.broadcast_to(init, (SUB, CONV_CHANNELS))
        for k in range(CONV_WIDTH):
            wk = (CONV_WIDTH - 1 - k) if flip else k
            acc = acc + w_ref[wk:wk + 1, :] * win[k % 8, pl.ds(r0 + 8 * (k // 8), SUB), :]
        out[pl.ds(r0, SUB), :] = acc
        return carry

    lax.fori_loop(0, CHUNK // SUB, sub, 0)


def _tap_grads(win, du, dwacc):
    def sub(si, carry):
        r0 = pl.multiple_of(si * SUB, SUB)
        d = du[pl.ds(r0, SUB), :]
        for k in range(CONV_WIDTH):
            p = d * win[k % 8, pl.ds(r0 + 8 * (k // 8), SUB), :]
            dwacc[8 * k:8 * k + 8, :] += (p[0:8] + p[8:16]) + (p[16:24] + p[24:32])
        return carry

    lax.fori_loop(0, CHUNK // SUB, sub, 0)


def _ln_parts(u):
    mu = jnp.mean(u, axis=-1, keepdims=True)
    xc = u - mu
    rstd = lax.rsqrt(jnp.mean(xc * xc, axis=-1, keepdims=True) + EPS)
    return xc * rstd, rstd


def _conv_fwd(proj, conv_w, conv_b, ln_g, ln_b, carry=None):
    T, C = proj.shape[0], CONV_CHANNELS
    vec = pl.BlockSpec((1, C), lambda i: (0, 0))
    c_in, c_out, c_sems = _carry_io(carry)

    def body(*refs):
        a0, a1, b0, b1, w_ref, cb_ref, g_ref, be_ref = refs[:8]
        ci_refs = refs[8:8 + len(c_in)]
        c_ref, u_ref = refs[8 + len(c_in):10 + len(c_in)]
        co_refs = refs[10 + len(c_in):10 + len(c_in) + len(c_out)]
        zpad, win, ubuf = refs[10 + len(c_in) + len(c_out):13 + len(c_in) + len(c_out)]
        cs_refs = refs[13 + len(c_in) + len(c_out):]
        if carry is not None:
            carry.start(ci_refs, co_refs, cs_refs)
        _glu_to_pad(a0, a1, b0, b1, zpad)
        for ci in range(T // CHUNK):
            _tap_windows(zpad, ci * CHUNK + CONV_PAD - (CONV_WIDTH - 1), win)
            _taps(win, w_ref, cb_ref[...], ubuf, False)
            u = ubuf[...]
            u_ref[ci * CHUNK:(ci + 1) * CHUNK, :] = u
            xh, _ = _ln_parts(u)
            ln = xh * g_ref[...] + be_ref[...]
            c_ref[ci * CHUNK:(ci + 1) * CHUNK, :] = (ln * jax.nn.sigmoid(ln)).astype(BF)
        if carry is not None:
            carry.finish(ci_refs, co_refs, cs_refs)

    res = pl.pallas_call(
        body, name="conv_fwd", grid=(1,),
        in_specs=[*_GLU_SPECS, pl.BlockSpec((CONV_PAD, C), lambda i: (0, 0)), vec, vec, vec, *[ANY] * len(c_in)],
        out_specs=[pl.BlockSpec((T, C), lambda i: (0, 0)), pl.BlockSpec((T, C), lambda i: (0, 0)), *[ANY] * len(c_out)],
        out_shape=[_sds((T, C), BF), _sds((T, C), F32), *c_out],
        scratch_shapes=[pltpu.VMEM((PAD_ROWS, C), F32), pltpu.VMEM((8, WIN, C), F32), pltpu.VMEM((CHUNK, C), F32),
                        *c_sems],
        compiler_params=_params(("arbitrary",)),
    )(proj, proj, proj, proj, conv_w, conv_b, ln_g, ln_b, *c_in)
    return res[:2], res[2:]


def _conv_bwd(proj, u, d_c, conv_w, conv_b, ln_g, ln_b, carry=None):
    T, C = proj.shape[0], CONV_CHANNELS
    vec = pl.BlockSpec((1, C), lambda i: (0, 0))
    wspec = pl.BlockSpec((CONV_PAD, C), lambda i: (0, 0))
    c_in, c_out, c_sems = _carry_io(carry)

    def body(*refs):
        a0, a1, b0, b1, u_ref, dc_ref, w_ref, cb_ref, g_ref, be_ref = refs[:10]
        ci_refs = refs[10:10 + len(c_in)]
        o = 10 + len(c_in)
        dglu_ref, dw_ref, dcb_ref, dg_ref, dbe_ref = refs[o:o + 5]
        co_refs = refs[o + 5:o + 5 + len(c_out)]
        zpad, dupad, win, ubuf, dwacc = refs[o + 5 + len(c_out):o + 10 + len(c_out)]
        cs_refs = refs[o + 10 + len(c_out):]
        if carry is not None:
            carry.start(ci_refs, co_refs, cs_refs)
        _glu_to_pad(a0, a1, b0, b1, zpad)
        dupad[T:, :] = jnp.zeros((2 * CONV_PAD, C), F32)
        dwacc[...] = jnp.zeros_like(dwacc)
        dcb_ref[...] = jnp.zeros_like(dcb_ref)
        dg_ref[...] = jnp.zeros_like(dg_ref)
        dbe_ref[...] = jnp.zeros_like(dbe_ref)
        for ci in range(T // CHUNK):
            rows = slice(ci * CHUNK, (ci + 1) * CHUNK)
            _tap_windows(zpad, ci * CHUNK + CONV_PAD - (CONV_WIDTH - 1), win)
            xh, rstd = _ln_parts(u_ref[rows, :])
            ln = xh * g_ref[...] + be_ref[...]
            sg = jax.nn.sigmoid(ln)
            dln = dc_ref[rows, :].astype(F32) * (sg * (1.0 + ln * (1.0 - sg)))
            dg_ref[...] += jnp.sum(dln * xh, axis=0, keepdims=True)
            dbe_ref[...] += jnp.sum(dln, axis=0, keepdims=True)
            dxh = dln * g_ref[...]
            du = rstd * (dxh - jnp.mean(dxh, axis=-1, keepdims=True)
                         - xh * jnp.mean(dxh * xh, axis=-1, keepdims=True))
            dupad[rows, :] = du
            dcb_ref[...] += jnp.sum(du, axis=0, keepdims=True)
            _tap_grads(win, dupad.at[rows, :], dwacc)
        for k in range(CONV_WIDTH):
            dw_ref[k:k + 1, :] = jnp.sum(dwacc[8 * k:8 * k + 8, :], axis=0, keepdims=True)
        dw_ref[CONV_WIDTH:, :] = jnp.zeros((CONV_PAD - CONV_WIDTH, C), F32)
        for ci in range(T // CHUNK):
            rows = slice(ci * CHUNK, (ci + 1) * CHUNK)
            _tap_windows(dupad, ci * CHUNK, win)
            _taps(win, w_ref, jnp.zeros((1, C), F32), ubuf, True)
            dz = ubuf[...]
            for half, (a, b) in enumerate(((a0, b0), (a1, b1))):
                sb = jax.nn.sigmoid(b[rows, :].astype(F32))
                dzh = dz[:, half * 256:(half + 1) * 256]
                dglu_ref[rows, half * 256:(half + 1) * 256] = (dzh * sb).astype(BF)
                dglu_ref[rows, C + half * 256:C + (half + 1) * 256] = (
                    dzh * a[rows, :].astype(F32) * sb * (1.0 - sb)).astype(BF)
        if carry is not None:
            carry.finish(ci_refs, co_refs, cs_refs)

    res = pl.pallas_call(
        body, name="conv_bwd", grid=(1,),
        in_specs=[*_GLU_SPECS, pl.BlockSpec((T, C), lambda i: (0, 0)), pl.BlockSpec((T, C), lambda i: (0, 0)), wspec,
                  vec, vec, vec, *[ANY] * len(c_in)],
        out_specs=[pl.BlockSpec((T, 2 * C), lambda i: (0, 0)), wspec, vec, vec, vec, *[ANY] * len(c_out)],
        out_shape=[_sds((T, 2 * C), BF), _sds((CONV_PAD, C), F32), _sds((1, C), F32), _sds((1, C), F32),
                   _sds((1, C), F32), *c_out],
        scratch_shapes=[pltpu.VMEM((PAD_ROWS, C), F32), pltpu.VMEM((PAD_ROWS, C), F32), pltpu.VMEM((8, WIN, C), F32),
                        pltpu.VMEM((CHUNK, C), F32), pltpu.VMEM((8 * CONV_PAD, C), F32), *c_sems],
        compiler_params=_params(("arbitrary",)),
    )(proj, proj, proj, proj, u, d_c, conv_w, conv_b, ln_g, ln_b, *c_in)
    return res[:5], res[5:]


_GATE_BLK = GATE_OFF // 256


def _ffn_in_swiglu(h2, wf_t, carry=None):
    T, D = h2.shape
    tm, tn = 1024, D_FF // 2
    nj, ni = D_FF // tn, T // tm
    c_in, c_out, c_sems = _carry_io(carry)

    def body(*refs):
        a_ref, bg_ref, bu_ref = refs[:3]
        ci_refs = refs[3:3 + len(c_in)]
        act_ref, g_ref, u_ref = refs[3 + len(c_in):6 + len(c_in)]
        co_refs = refs[6 + len(c_in):6 + len(c_in) + len(c_out)]
        cs_refs = refs[6 + len(c_in) + len(c_out):]
        j, i = pl.program_id(0), pl.program_id(1)
        if carry is not None:
            @pl.when((j == 0) & (i == 0))
            def _():
                carry.start(ci_refs, co_refs, cs_refs)
        a = a_ref[...]
        for c0, c1 in ((0, 768), (768, tn)):
            g = lax.dot_general(a, bg_ref[c0:c1, :], _DIMS["NT"], preferred_element_type=F32)
            u = lax.dot_general(a, bu_ref[c0:c1, :], _DIMS["NT"], preferred_element_type=F32)
            act_ref[:, c0:c1] = (g * jax.nn.sigmoid(g) * u).astype(BF)
            g_ref[:, c0:c1] = g.astype(BF)
            u_ref[:, c0:c1] = u.astype(BF)
        if carry is not None:
            @pl.when((j == nj - 1) & (i == ni - 1))
            def _():
                carry.finish(ci_refs, co_refs, cs_refs)

    t = pl.BlockSpec((tm, tn), lambda j, i: (i, j))
    res = pl.pallas_call(
        body, name="ffn_in_swiglu", grid=(nj, ni),
        in_specs=[pl.BlockSpec((tm, D), lambda j, i: (i, 0)), pl.BlockSpec((tn, D), lambda j, i: (j, 0)),
                  pl.BlockSpec((tn, D), lambda j, i: (nj + j, 0)), *[ANY] * len(c_in)],
        out_specs=[t, t, t, *[ANY] * len(c_out)], out_shape=[*[_sds((T, D_FF), BF)] * 3, *c_out],
        scratch_shapes=c_sems,
        compiler_params=_params(("arbitrary", "arbitrary")),
    )(h2, wf_t, wf_t, *c_in)
    return res[:3], res[3:]


def _proj_merge(o, c, wap_t, wcp_t, b_cp, proj):
    T, D = o.shape[0], wap_t.shape[0]
    tm, tg = T, 256
    nj = D // tg

    def body(o_ref, c_ref, wa_ref, wc_ref, b_ref, g0_ref, g1_ref, ya_ref, yc_ref, m_ref):
        ya = lax.dot_general(o_ref[...], wa_ref[...], _DIMS["NT"], preferred_element_type=F32)
        yc = lax.dot_general(c_ref[...], wc_ref[...], _DIMS["NT"], preferred_element_type=F32) + b_ref[...]
        ya_ref[...] = ya.astype(BF)
        yc_ref[...] = yc.astype(BF)
        m_ref[...] = (jax.nn.sigmoid(g0_ref[...].astype(F32)) * ya + jax.nn.sigmoid(g1_ref[...].astype(F32)) * yc).astype(BF)

    act = pl.BlockSpec((tm, o.shape[1]), lambda j, i: (i, 0))
    wgt = pl.BlockSpec((tg, o.shape[1]), lambda j, i: (j, 0))
    t = pl.BlockSpec((tm, tg), lambda j, i: (i, j))
    return pl.pallas_call(
        body, name="proj_merge", grid=(nj, T // tm),
        in_specs=[act, act, wgt, wgt, pl.BlockSpec((1, tg), lambda j, i: (0, j)),
                  pl.BlockSpec((tm, tg), lambda j, i: (i, _GATE_BLK + j)),
                  pl.BlockSpec((tm, tg), lambda j, i: (i, _GATE_BLK + nj + j))],
        out_specs=[t, t, t], out_shape=[_sds((T, D), BF)] * 3,
        compiler_params=_params(("arbitrary", "arbitrary")),
    )(o, c, wap_t, wcp_t, b_cp, proj, proj)


def _stacked_dw(name, segs, h, tb):
    T, D = h.shape
    nblk = [seg.shape[1] // tb for seg in segs]
    starts = [sum(nblk[:q]) for q in range(len(segs))]
    n_seg = len(segs)

    def body(*refs):
        seg_refs, h_ref, o_ref, cs_ref = refs[:n_seg], refs[n_seg], refs[n_seg + 1], refs[n_seg + 2]
        i = pl.program_id(0)
        for seg_ref, st, nb in zip(seg_refs, starts, nblk):
            @pl.when((i >= st) & (i < st + nb))
            def _(seg_ref=seg_ref):
                a = seg_ref[...]
                o_ref[...] = lax.dot_general(a, h_ref[...], _DIMS["TN"], preferred_element_type=F32).astype(BF)
                cs_ref[...] = jnp.sum(a.astype(F32), axis=0, keepdims=True)

    in_specs = [pl.BlockSpec((T, tb), functools.partial(lambda i, st, nb: (0, jnp.clip(i - st, 0, nb - 1)), st=st, nb=nb))
                for st, nb in zip(starts, nblk)]
    return pl.pallas_call(
        body, name=name, grid=(sum(nblk),),
        in_specs=[*in_specs, pl.BlockSpec((T, D), lambda i: (0, 0))],
        out_specs=[pl.BlockSpec((tb, D), lambda i: (i, 0)), pl.BlockSpec((1, tb), lambda i: (0, i))],
        out_shape=[_sds((sum(nblk) * tb, D), BF), _sds((1, sum(nblk) * tb), F32)],
        compiler_params=_params(("arbitrary",)),
    )(*segs, h)


def _local_step(x, h, r1, target, small, wi_t, conv_w, plan):
    T, D = x.shape
    tm = 1024

    def carried(call, res, carry):
        if carry is None:
            return res
        outs, got = res
        plan.done(call, got)
        return outs


    def ep_add(acc, ex, outs, ids, scr):
        outs[0][...] = acc + ex[0][...]

    tn_in = IN_WIDTH // 2
    carry = plan.carry("proj_in")
    def ep_bias_bf16(acc, ex, outs, ids, scr):
        outs[0][...] = (acc + ex[0][...]).astype(BF)

    proj, = carried("proj_in", _matmul("proj_in", [h], wi_t, "NT", m=T, n=IN_WIDTH, tm=tm, tn=tn_in,
                                       epilogue=ep_bias_bf16, extra=[(small["b_in"], _row(tn_in))],
                                       outs=[(_sds((T, IN_WIDTH), BF), _tile(tm, tn_in))], carry=carry), carry)
    plan.launch("gather_ffn", after=proj)
    o, got = _attn_fwd(proj, small["sinks"], carry=plan.carry("attn_fwd"))
    plan.done("attn_fwd", got)
    (c, u_conv), got = _conv_fwd(proj, conv_w, small["conv_b"], small["ln_g"], small["ln_b"],
                                 carry=plan.carry("conv_fwd"))
    plan.done("conv_fwd", got)
    wap_t, wcp_t, w_out = plan.weight("w_attn_proj"), plan.weight("w_conv_proj"), plan.weight("w_out")
    ya, yc, merged = _proj_merge(o, c, wap_t, wcp_t, small["b_conv_proj"], proj)

    tg = 256
    gate_specs = [pl.BlockSpec((T, tg), lambda j, i, k: (i, _GATE_BLK + j)),
                  pl.BlockSpec((T, tg), lambda j, i, k: (i, _GATE_BLK + D // tg + j))]

    def ep_residual_rms(acc, ex, outs, ids, scr):
        x2v = acc + ex[0][...]
        r = lax.rsqrt(jnp.mean(x2v * x2v, axis=-1, keepdims=True) + EPS)
        outs[0][...] = x2v
        outs[1][...] = (x2v * r * ex[1][...]).astype(BF)
        outs[2][...] = r

    carry = plan.carry("out_proj")
    x2, h2, r2 = carried("out_proj", _matmul(
        "out_proj_rms", [merged], w_out, "NN", m=T, n=D, tm=512, tn=D, epilogue=ep_residual_rms,
        extra=[(x, _tile(512, D)), (small["g_ffn_norm"], _row(D))],
        outs=[(_sds((T, D), F32), _tile(512, D)), (_sds((T, D), BF), _tile(512, D)),
              (_sds((T, 1), F32), pl.BlockSpec((512, 1), lambda j, i, k: (i, 0)))], carry=carry), carry)
    plan.launch("gather_down", after=x2)
    wf_t = plan.weight("w_ffn_in")
    (act, gate, up), got = _ffn_in_swiglu(h2, wf_t, carry=plan.carry("ffn_in_swiglu"))
    plan.done("ffn_in_swiglu", got)
    w_down = plan.weight("w_ffn_down")
    def ep_residual_loss(acc, ex, outs, ids, scr):
        dx, dg, part = _loss_head(acc + ex[0][...], ex[1][...], ex[2][...])
        outs[0][...] = dx
        outs[1][...] = dx.astype(BF)
        _accumulate_rows(outs[2], dg, ids[1] == 0)
        _accumulate_rows(outs[3], part, ids[1] == 0)

    dx3, dx3_b, dg_final, loss = _matmul(
        "ffn_down_loss", [act], w_down, "NN", m=T, n=D, tm=256, tn=D, epilogue=ep_residual_loss,
        extra=[(x2, _tile(256, D)), (small["g_final"], _row(D)), (target, _tile(256, D))],
        outs=[(_sds((T, D), F32), _tile(256, D)), (_sds((T, D), BF), _tile(256, D)), (_sds((1, D), F32), _row(D)),
              (_sds((1, 1), F32), pl.BlockSpec((1, 1), lambda j, i, k: (0, 0)))])

    tn_ff = D_FF // 2

    def ep_swiglu_bwd(acc, ex, outs, ids, scr):
        g, u = ex[0][...].astype(F32), ex[1][...].astype(F32)
        sg = jax.nn.sigmoid(g)
        outs[0][...] = (acc * u * sg * (1.0 + g * (1.0 - sg))).astype(BF)
        outs[1][...] = (acc * g * sg).astype(BF)

    dgate, dup = _matmul(
        "ffn_down_bwd", [dx3_b], w_down, "NT", m=T, n=D_FF, tm=tm, tn=tn_ff, epilogue=ep_swiglu_bwd,
        extra=[(gate, _tile(tm, tn_ff)), (up, _tile(tm, tn_ff))],
        outs=[(_sds((T, D_FF), BF), _tile(tm, tn_ff)), (_sds((T, D_FF), BF), _tile(tm, tn_ff))])

    def dw(name, a, b, rows, cols, row_off=0, alias=None, total_rows=None, colsum=False):
        total_rows = rows if total_rows is None else total_rows
        tmw = rows if rows <= 1024 else D_FF // 2
        blk, rem = divmod(row_off, tmw)
        assert rem == 0

        def ep(acc, ex, outs, ids, scr):
            outs[0][...] = acc.astype(BF)
            if colsum:
                outs[1][...] = jnp.sum(ex[0][...].astype(F32), axis=0, keepdims=True)

        outs = [(_sds((total_rows, cols), BF), pl.BlockSpec((tmw, cols), lambda j, i, k: (blk + i, j)))]
        extra = []
        if colsum:
            extra = [(a, pl.BlockSpec((T, tmw), lambda j, i, k: (0, i)))]
            outs.append((_sds((1, rows), F32), pl.BlockSpec((1, tmw), lambda j, i, k: (0, i))))
        carry = plan.carry(name)
        res = carried(name, _matmul(name, [a], b, "TN", m=rows, n=cols, tm=tmw, tn=cols, epilogue=ep, extra=extra,
                                    outs=outs, alias=None if alias is None else (alias, 0), carry=carry), carry)
        return res if colsum else res[0]

    plan.grad_ready(dict(w_ffn_down=dw("ffn_down_dw", act, dx3_b, D_FF, D)))

    def ep_rms_bwd(acc, ex, outs, ids, scr):
        dx, dg = _rms_bwd(acc, ex[0][...], ex[1][...], ex[2][...])
        dx = ex[3][...] + dx
        outs[0][...] = dx
        outs[1][...] = dx.astype(BF)
        _accumulate_rows(outs[2], dg, ids[1] == 0)

    def rms_bwd_io(tm_, xin, r, g, dres):
        return dict(
            extra=[(xin, _tile(tm_, D)), (r, pl.BlockSpec((tm_, 1), lambda j, i, k: (i, 0))), (g, _row(D)),
                   (dres, _tile(tm_, D))],
            outs=[(_sds((T, D), F32), _tile(tm_, D)), (_sds((T, D), BF), _tile(tm_, D)), (_sds((1, D), F32), _row(D))])

    carry = plan.carry("ffn_in_bwd")
    dx2, dx2_b, dg_ffn = carried(
        "ffn_in_bwd",
        _matmul("ffn_in_bwd", [dgate, dup], wf_t, "NN", m=T, n=D, tm=256, tn=D, epilogue=ep_rms_bwd,
                carry=carry, **rms_bwd_io(256, x2, r2, small["g_ffn_norm"], dx3)), carry)
    plan.launch("send_down")
    gwf_t, _ = _stacked_dw("ffn_in_dw", [dgate, dup], h2, D_FF // 2)
    plan.grad_ready(dict(w_ffn_in=gwf_t))

    def ep_merge_bwd(acc, ex, outs, ids, scr):
        s0 = jax.nn.sigmoid(ex[2][...].astype(F32))
        s1 = jax.nn.sigmoid(ex[3][...].astype(F32))
        outs[0][...] = (acc * s0).astype(BF)
        outs[1][...] = (acc * s1).astype(BF)
        outs[2][...] = (acc * ex[0][...].astype(F32) * s0 * (1.0 - s0)).astype(BF)
        outs[3][...] = (acc * ex[1][...].astype(F32) * s1 * (1.0 - s1)).astype(BF)

    carry = plan.carry("out_proj_bwd_merge")
    dya, dyc, dg0, dg1 = carried(
        "out_proj_bwd_merge",
        _matmul("out_proj_bwd_merge", [dx2_b], w_out, "NT", m=T, n=D, tm=T, tn=tg, epilogue=ep_merge_bwd,
                extra=[(ya, _tile(T, tg)), (yc, _tile(T, tg)), (proj, gate_specs[0]), (proj, gate_specs[1])],
                outs=[(_sds((T, D), BF), _tile(T, tg))] * 4, carry=carry), carry)
    plan.launch("send_ffn")
    gw_out = dw("out_proj_dw", merged, dx2_b, D, D)
    d_o, = _matmul("attn_proj_bwd", [dya], wap_t, "NN", m=T, n=ATTN_WIDTH, tm=tm, tn=ATTN_WIDTH,
                   epilogue=_store(BF), outs=[(_sds((T, ATTN_WIDTH), BF), _tile(tm, ATTN_WIDTH))])
    d_c, = _matmul("conv_proj_bwd", [dyc], wcp_t, "NN", m=T, n=CONV_CHANNELS, tm=tm, tn=CONV_CHANNELS,
                   epilogue=_store(BF), outs=[(_sds((T, CONV_CHANNELS), BF), _tile(tm, CONV_CHANNELS))])
    gwap_t = dw("attn_proj_dw", dya, o, D, ATTN_WIDTH)
    gwcp_t, db_cp = dw("conv_proj_dw", dyc, c, D, CONV_CHANNELS, colsum=True)
    plan.grad_ready(dict(w_out=gw_out, w_attn_proj=gwap_t, w_conv_proj=gwcp_t))
    (dglu, dcw, dcb, dlng, dlnb), got = _conv_bwd(proj, u_conv, d_c, conv_w, small["conv_b"], small["ln_g"],
                                                  small["ln_b"], carry=plan.carry("conv_bwd"))
    plan.done("conv_bwd", got)
    plan.launch("send_mix")
    (dqkv, dsinks), got = _attn_bwd(proj, d_o, small["sinks"], carry=plan.carry("attn_bwd"))
    plan.done("attn_bwd", got)

    segs = [dqkv, dglu, dg0, dg1]
    gwi_t, db_in = _stacked_dw("proj_in_dw", segs, h, 256)
    plan.grad_ready(dict(w_in=gwi_t))
    plan.alone("swap_inp")
    plan.launch("send_inp")
    carry = plan.carry("proj_in_bwd")
    dx, _, dg_mix = carried(
        "proj_in_bwd",
        _matmul("proj_in_bwd", segs, wi_t, "NN", m=T, n=D, tm=256, tn=D, epilogue=ep_rms_bwd, carry=carry,
                **rms_bwd_io(256, x, r1, small["g_mix_norm"], plan.behind("inp", dx2))), carry)

    parts = dict(g_mix_norm=dg_mix, b_in=db_in, sinks=dsinks, conv_w=dcw, conv_b=dcb, ln_g=dlng, ln_b=dlnb,
                 b_conv_proj=db_cp, g_ffn_norm=dg_ffn, g_final=dg_final, loss=loss)
    return dx, parts


def _place():
    x, y, c = lax.axis_index("x"), lax.axis_index("y"), lax.axis_index("c")
    return x, y, c, [(1 - x, y), (x, 1 - y), (1 - x, 1 - y)]


def _gather_copies(x_refs, out_refs, rows_per, send_sems, recv_sems, local_sems):
    x, y, c, chips = _place()
    me, sibling = (x, y, c), (x, y, 1 - c)

    def rows(a, px, py, pc):
        return out_refs[a].at[pl.ds((4 * px + 2 * py + pc) * rows_per[a], rows_per[a])]

    def copy(a, k, block, to, src=None):
        return pltpu.make_async_remote_copy(
            src_ref=rows(a, *block) if src is None else src, dst_ref=rows(a, *block),
            send_sem=send_sems.at[7 * a + k], recv_sem=recv_sems.at[7 * a + k], device_id=to, device_id_type=MESH)

    def local(a):
        return pltpu.make_async_copy(x_refs[a], rows(a, *me), local_sems.at[a])

    def first(a):
        return [copy(a, 0, me, sibling, src=x_refs[a])] + [copy(a, 1 + j, me, (*chip, c), src=x_refs[a])
                                                          for j, chip in enumerate(chips)]

    def arrive(a, j):
        return copy(a, 1 + j, (*chips[j], c), me)

    def passed(a, j):
        return copy(a, 4 + j, (*chips[j], c), sibling)

    def from_sibling(a):
        return [copy(a, 0, sibling, me)] + [copy(a, 4 + j, (*chip, 1 - c), me) for j, chip in enumerate(chips)]

    return len(x_refs), local, first, arrive, passed, from_sibling


def _gather_start(*refs):
    n, local, first, _, _, _ = _gather_copies(*refs)
    for a in range(n):
        local(a).start()
        for cp in first(a):
            cp.start()


def _gather_finish(*refs):
    n, local, first, arrive, passed, from_sibling = _gather_copies(*refs)
    for a in range(n):
        for j in range(3):
            arrive(a, j).wait_recv()
            passed(a, j).start()
    for a in range(n):
        for cp in from_sibling(a):
            cp.wait_recv()
    for a in range(n):
        for cp in first(a) + [passed(a, j) for j in range(3)]:
            cp.wait_send()
        local(a).wait()


def _gather_peers():
    x, y, c, chips = _place()
    return [(x, y, 1 - c)] + [(*chip, c) for chip in chips]


def _gather_sems(n):
    return [pltpu.SemaphoreType.DMA((7 * n,)), pltpu.SemaphoreType.DMA((7 * n,)), pltpu.SemaphoreType.DMA((n,))]


def _gather_carry(shards):
    rows_per = [s.shape[0] for s in shards]
    return _Carry(shards, [_sds((N_DEV * s.shape[0],) + s.shape[1:], s.dtype) for s in shards],
                  _gather_sems(len(shards)),
                  lambda ins, outs, sems: _gather_start(ins, outs, rows_per, *sems),
                  lambda ins, outs, sems: _gather_finish(ins, outs, rows_per, *sems), _gather_peers)


def _first_gather(shards, x, g):
    n = len(shards)
    rows_per = [s.shape[0] for s in shards]
    T, D = x.shape

    def body(*refs):
        x_refs, (xin_ref, g_ref), out_refs, (h_ref, r_ref) = refs[:n], refs[n:n + 2], refs[n + 2:2 * n + 2], refs[2 * n + 2:2 * n + 4]
        send_sems, recv_sems, local_sems = refs[2 * n + 4:]
        x, y, c, chips = _place()
        me, sibling = (x, y, c), (x, y, 1 - c)
        near_x, near_y, far = (*chips[0], c), (*chips[1], c), (*chips[2], c)

        def rows(a, dev, part):
            h = rows_per[a] // 2
            lo, size = {"all": (0, 2 * h), "low": (0, h), "high": (h, h)}[part]
            return out_refs[a].at[pl.ds((4 * dev[0] + 2 * dev[1] + dev[2]) * rows_per[a] + lo, size)]

        def copy(a, k, block, part, to, src=None):
            return pltpu.make_async_remote_copy(
                src_ref=rows(a, block, part) if src is None else src, dst_ref=rows(a, block, part),
                send_sem=send_sems.at[9 * a + k], recv_sem=recv_sems.at[9 * a + k], device_id=to, device_id_type=MESH)

        other = lambda dev: (dev[0], dev[1], 1 - c)
        sent = []
        for a in range(n):
            pltpu.make_async_copy(x_refs[a], rows(a, me, "all"), local_sems.at[a]).start()
            sent += [copy(a, 0, me, "all", sibling, src=x_refs[a]), copy(a, 1, me, "all", near_x, src=x_refs[a]),
                     copy(a, 2, me, "all", near_y, src=x_refs[a])]
        for cp in sent:
            cp.start()
        for i in range(T // CHUNK):
            rws = slice(i * CHUNK, (i + 1) * CHUNK)
            xv = xin_ref[rws, :]
            r = lax.rsqrt(jnp.mean(xv * xv, axis=-1, keepdims=True) + EPS)
            h_ref[rws, :] = (xv * r * g_ref[...]).astype(BF)
            r_ref[rws, :] = r
        for a in range(n):
            copy(a, 1, near_x, "all", me).wait_recv()
            copy(a, 2, near_y, "all", me).wait_recv()
            passed = [copy(a, 3, near_y, "high", near_x), copy(a, 4, near_x, "low", near_y),
                      copy(a, 5, near_x, "all", sibling), copy(a, 6, near_y, "all", sibling)]
            for cp in passed:
                cp.start()
            sent += passed
        for a in range(n):
            copy(a, 3, far, "high", me).wait_recv()
            copy(a, 4, far, "low", me).wait_recv()
            passed = [copy(a, 7, far, "high", sibling), copy(a, 8, far, "low", sibling)]
            for cp in passed:
                cp.start()
            sent += passed
        for a in range(n):
            copy(a, 0, sibling, "all", me).wait_recv()
            copy(a, 5, other(near_x), "all", me).wait_recv()
            copy(a, 6, other(near_y), "all", me).wait_recv()
            copy(a, 7, other(far), "high", me).wait_recv()
            copy(a, 8, other(far), "low", me).wait_recv()
        for cp in sent:
            cp.wait_send()
        for a in range(n):
            pltpu.make_async_copy(x_refs[a], rows(a, me, "all"), local_sems.at[a]).wait()

    vm = pl.BlockSpec(memory_space=pltpu.VMEM)
    return pl.pallas_call(
        body, name="weights_first_gather", in_specs=[*[ANY] * n, vm, vm], out_specs=[*[ANY] * n, vm, vm],
        out_shape=[*[_sds((N_DEV * s.shape[0],) + s.shape[1:], s.dtype) for s in shards], _sds((T, D), BF),
                   _sds((T, 1), F32)],
        scratch_shapes=[pltpu.SemaphoreType.DMA((9 * n,)), pltpu.SemaphoreType.DMA((9 * n,)),
                        pltpu.SemaphoreType.DMA((n,))],
        compiler_params=pltpu.CompilerParams(vmem_limit_bytes=VMEM_LIMIT_BYTES),
    )(*shards, x, g)


def _swap_carry(grads):
    n = len(grads)

    def copies(g_refs, out_refs, sems):
        send_sems, recv_sems = sems
        x, y, c, _ = _place()
        return [pltpu.make_async_remote_copy(
            src_ref=g_refs[a].at[2 * p + 1 - c], dst_ref=out_refs[a].at[p],
            send_sem=send_sems.at[4 * a + p], recv_sem=recv_sems.at[4 * a + p],
            device_id=(x, y, 1 - c), device_id_type=MESH) for a in range(n) for p in range(4)]

    def start(ins, outs, sems):
        for cp in copies(ins, outs, sems):
            cp.start()

    def finish(ins, outs, sems):
        for cp in copies(ins, outs, sems):
            cp.wait()

    def peers():
        x, y, c, _ = _place()
        return [(x, y, 1 - c)]

    return _Carry(grads, [_sds((4,) + g.shape[1:], g.dtype) for g in grads],
                  [pltpu.SemaphoreType.DMA((4 * n,)), pltpu.SemaphoreType.DMA((4 * n,))], start, finish, peers)


def _join(carries):
    carries = [c for c in carries if c is not None]
    if not carries:
        return None
    n_in = [len(c.arrays) for c in carries]
    n_out = [len(c.out_shapes) for c in carries]
    n_sem = [len(c.sems) for c in carries]

    def parts(refs, counts):
        cuts = [sum(counts[:q]) for q in range(len(counts) + 1)]
        return [refs[cuts[q]:cuts[q + 1]] for q in range(len(counts))]

    def start(ins, outs, sems):
        for c, i, o, s in zip(carries, parts(ins, n_in), parts(outs, n_out), parts(sems, n_sem)):
            c.start(i, o, s)

    def finish(ins, outs, sems):
        for c, i, o, s in zip(carries, parts(ins, n_in), parts(outs, n_out), parts(sems, n_sem)):
            c.finish(i, o, s)

    return _Carry([a for c in carries for a in c.arrays], [o for c in carries for o in c.out_shapes],
                  [s for c in carries for s in c.sems], start, finish)


def _run_carry(name, carry):
    n_in, n_out = len(carry.arrays), len(carry.out_shapes)

    def body(*refs):
        carry.start(refs[:n_in], refs[n_in:n_in + n_out], refs[n_in + n_out:])
        carry.finish(refs[:n_in], refs[n_in:n_in + n_out], refs[n_in + n_out:])

    return pl.pallas_call(body, name=name, in_specs=[ANY] * n_in, out_specs=[ANY] * n_out,
                          out_shape=carry.out_shapes, scratch_shapes=carry.sems)(*carry.arrays)


def _run_carry_async(name, carry, collective_id):
    ins = [jax.new_ref(a, memory_space=pltpu.MemorySpace.HBM) for a in carry.arrays]
    outs = [jax.empty_ref(o, memory_space=pltpu.MemorySpace.HBM) for o in carry.out_shapes]

    @pl.kernel(mesh=plsc.ScalarSubcoreMesh(axis_name="sequencer", num_cores=1), name=name,
               scratch_types=tuple(carry.sems), compiler_params=pltpu.CompilerParams(collective_id=collective_id))
    def launch(*sems):
        barrier = pltpu.get_barrier_semaphore()
        peers = carry.peers()
        for peer in peers:
            pl.semaphore_signal(barrier, inc=1, device_id=peer, device_id_type=MESH)
        pl.semaphore_wait(barrier, len(peers))
        carry.start(ins, outs, sems)
        carry.finish(ins, outs, sems)

    launch()
    return [o[...] for o in outs]


def _chip_sums(name, gs, gots, c):
    n = len(gs)

    def body(c_ref, *refs):
        for g_ref, got_ref, o_ref in zip(refs[:n], refs[n:2 * n], refs[2 * n:]):
            o_ref[...] = (g_ref[...].astype(F32) + got_ref[...].astype(F32)).astype(BF)

    mine = [pl.BlockSpec((1,) + g.shape[1:], lambda p, c_ref: (2 * p + c_ref[0], 0, 0)) for g in gs]
    slot = [pl.BlockSpec((1,) + g.shape[1:], lambda p, c_ref: (p, 0, 0)) for g in gs]
    return pl.pallas_call(
        body, name=name,
        grid_spec=pltpu.PrefetchScalarGridSpec(num_scalar_prefetch=1, grid=(4,), in_specs=[*mine, *slot],
                                               out_specs=slot),
        out_shape=[_sds((4,) + g.shape[1:], BF) for g in gs],
        compiler_params=_params(("arbitrary",)),
    )(c, *gs, *gots)


def _send_carry(sums, ks):
    n, nk = len(sums), len(ks)

    def copies(s_refs, out_refs, sems):
        send_sems, recv_sems = sems
        x, y, c, chips = _place()
        return [pltpu.make_async_remote_copy(
            src_ref=s_refs[a].at[2 * chips[k][0] + chips[k][1]], dst_ref=out_refs[a].at[q],
            send_sem=send_sems.at[nk * a + q], recv_sem=recv_sems.at[nk * a + q],
            device_id=(*chips[k], c), device_id_type=MESH) for a in range(n) for q, k in enumerate(ks)]

    def start(ins, outs, sems):
        for cp in copies(ins, outs, sems):
            cp.start()

    def finish(ins, outs, sems):
        for cp in copies(ins, outs, sems):
            cp.wait()

    def peers():
        x, y, c, chips = _place()
        return [(*chips[k], c) for k in ks]

    return _Carry(sums, [_sds((nk,) + s.shape[1:], s.dtype) for s in sums],
                  [pltpu.SemaphoreType.DMA((nk * n,)), pltpu.SemaphoreType.DMA((nk * n,))], start, finish, peers)


def _adam_math(w, g, m, v):
    m = ADAM_B1 * m + (1.0 - ADAM_B1) * g
    v = ADAM_B2 * v + (1.0 - ADAM_B2) * (g * g)
    m_hat = m / (1.0 - ADAM_B1 ** ADAM_STEP)
    v_hat = v / (1.0 - ADAM_B2 ** ADAM_STEP)
    delta = -ADAM_LR * (m_hat / (jnp.sqrt(v_hat) + ADAM_EPS) + ADAM_WD * w)
    return delta, m, v


def _adamw(name, w, g, m, v):
    rows, cols = w.shape
    tr = 256 if rows % 256 == 0 else rows

    def body(w_ref, g_ref, m_ref, v_ref, d_ref, nm_ref, nv_ref):
        d_ref[...], nm_ref[...], nv_ref[...] = _adam_math(w_ref[...], g_ref[...], m_ref[...], v_ref[...])

    t = pl.BlockSpec((tr, cols), lambda i: (i, 0))
    return pl.pallas_call(
        body, name=name, grid=(rows // tr,), in_specs=[t] * 4, out_specs=[t] * 3,
        out_shape=[_sds((rows, cols), F32)] * 3, compiler_params=_params(("arbitrary",)),
    )(w, g, m, v)


def _grad_adamw(name, g, got, got3, ids, w, m, v):
    _, rows, cols = g.shape
    n3 = len(got3)
    tr = rows // 2 if rows >= 256 else rows

    def body(ids_ref, g_ref, got_ref, *rest):
        w_ref, m_ref, v_ref, o_ref, d_ref, nm_ref, nv_ref = rest[n3:]
        tot = g_ref[0].astype(F32) + got_ref[0].astype(F32)
        for r_ref in rest[:n3]:
            for q in range(r_ref.shape[0]):
                tot = tot + r_ref[q].astype(F32)
        o_ref[...] = tot
        d_ref[...], nm_ref[...], nv_ref[...] = _adam_math(w_ref[...], tot, m_ref[...], v_ref[...])

    tile = pl.BlockSpec((tr, cols), lambda i, ids_ref: (i, 0))
    return pl.pallas_call(
        body, name=name,
        grid_spec=pltpu.PrefetchScalarGridSpec(
            num_scalar_prefetch=1, grid=(rows // tr,),
            in_specs=[pl.BlockSpec((1, tr, cols), lambda i, ids_ref: (ids_ref[0], i, 0)),
                      pl.BlockSpec((1, tr, cols), lambda i, ids_ref: (ids_ref[1], i, 0)),
                      *[pl.BlockSpec((r.shape[0], tr, cols), lambda i, ids_ref: (0, i, 0)) for r in got3],
                      tile, tile, tile],
            out_specs=[tile] * 4),
        out_shape=[_sds((rows, cols), F32)] * 4,
        compiler_params=_params(("arbitrary",)),
    )(ids, g, got, *got3, w, m, v)


SMALL_NAMES = ["g_mix_norm", "b_in", "sinks", "conv_b", "ln_g", "ln_b", "b_conv_proj", "g_ffn_norm", "g_final"]
_PACK_ROWS = 32


def _small_pack(parts):
    C = CONV_CHANNELS
    part_list = [parts["g_mix_norm"], parts["b_in"], parts["sinks"], parts["conv_b"], parts["ln_g"], parts["ln_b"],
                 parts["b_conv_proj"], parts["g_ffn_norm"], parts["g_final"], parts["loss"], parts["conv_w"]]

    def body(p_mix, p_b, p_sink, p_cb, p_lg, p_lb, p_bcp, p_ffn, p_fin, p_loss, p_cw, pack):
        pack[...] = jnp.zeros_like(pack)
        pack[0:1, :] = p_mix[...]
        pack[1:2, 0:GLU_OFF] = p_b[:, 0:GLU_OFF]
        pack[2:3, :] = p_b[:, GLU_OFF:GATE_OFF]
        pack[3:4, :] = p_b[:, GATE_OFF:GATE_OFF + D_MODEL]
        pack[4:5, :] = p_b[:, GATE_OFF + D_MODEL:]
        pack[5:6, 0:128] = p_sink[...]
        pack[6:7, 0:C] = p_cb[...]
        pack[6:7, C:2 * C] = p_lg[...]
        pack[7:8, 0:C] = p_lb[...]
        pack[8:9, :] = p_bcp[...]
        pack[9:10, :] = p_ffn[...]
        pack[10:11, :] = p_fin[...]
        pack[11:12, 0:128] = jnp.broadcast_to(p_loss[...], (1, 128))
        pack[12:28, 0:C] = p_cw[0:16, :]
        pack[12:28, C:2 * C] = p_cw[16:32, :]

    vm = pl.BlockSpec(memory_space=pltpu.VMEM)
    return pl.pallas_call(body, name="small_pack", in_specs=[vm] * len(part_list), out_specs=vm,
                          out_shape=_sds((_PACK_ROWS, D_MODEL), F32))(*part_list)


def _small_adamw(gathered, small_w, small_m, small_v):
    C = CONV_CHANNELS
    names = SMALL_NAMES
    widths = [small_w[k].shape[1] for k in names]
    n_small = len(names)

    def body(*refs):
        tot_ref = refs[0]
        w_refs = refs[1:1 + n_small]
        m_refs = refs[1 + n_small:1 + 2 * n_small]
        v_refs = refs[1 + 2 * n_small:1 + 3 * n_small]
        o = 1 + 3 * n_small
        loss_ref, cw_ref = refs[o], refs[o + 1]
        out_refs = refs[o + 2:o + 2 + 4 * n_small]
        tot = tot_ref[0:_PACK_ROWS, :]
        for d in range(1, N_DEV):
            tot = tot + tot_ref[d * _PACK_ROWS:(d + 1) * _PACK_ROWS, :]
        loss_ref[...] = tot[11:12, 0:1]
        cw_ref[0:16, :] = tot[12:28, 0:C]
        cw_ref[16:32, :] = tot[12:28, C:2 * C]
        grads = dict(
            g_mix_norm=tot[0:1, :],
            b_in=jnp.concatenate([tot[1:2, 0:GLU_OFF], tot[2:3, :], tot[3:4, :], tot[4:5, :]], axis=1),
            sinks=tot[5:6, 0:N_Q_HEADS], conv_b=tot[6:7, 0:C], ln_g=tot[6:7, C:2 * C], ln_b=tot[7:8, 0:C],
            b_conv_proj=tot[8:9, :], g_ffn_norm=tot[9:10, :], g_final=tot[10:11, :])
        for s, k in enumerate(names):
            g = grads[k]
            d, nm, nv = _adam_math(w_refs[s][...], g, m_refs[s][...], v_refs[s][...])
            out_refs[4 * s][...] = g
            out_refs[4 * s + 1][...] = d
            out_refs[4 * s + 2][...] = nm
            out_refs[4 * s + 3][...] = nv

    vm = pl.BlockSpec(memory_space=pltpu.VMEM)
    args = [gathered, *[small_w[k] for k in names], *[small_m[k] for k in names], *[small_v[k] for k in names]]
    out_shape = [_sds((1, 1), F32), _sds((CONV_PAD, C), F32)]
    for wd in widths:
        out_shape += [_sds((1, wd), F32)] * 4
    res = pl.pallas_call(
        body, name="small_adamw",
        in_specs=[vm] * len(args), out_specs=[vm] * len(out_shape), out_shape=out_shape,
        compiler_params=pltpu.CompilerParams(vmem_limit_bytes=VMEM_LIMIT_BYTES),
    )(*args)
    return res[0], res[1], {k: res[2 + 4 * s:6 + 4 * s] for s, k in enumerate(names)}


BIG = dict(w_in=True, w_attn_proj=True, w_conv_proj=True, w_out=False, w_ffn_in=True, w_ffn_down=False)
WEIGHT_NAMES = ["g_mix_norm", "w_in", "b_in", "sinks", "conv_w", "conv_b", "ln_g", "ln_b", "w_attn_proj",
                "w_conv_proj", "b_conv_proj", "w_out", "g_ffn_norm", "w_ffn_in", "w_ffn_down", "g_final"]


class _Plan:
    GROUPS = dict(down=["w_ffn_down"], ffn=["w_ffn_in"], mix=["w_out", "w_attn_proj", "w_conv_proj"], inp=["w_in"])
    ALL = (0, 1, 2)
    RIDES = dict(
        gather_mix=[("gather", ["w_attn_proj", "w_conv_proj", "w_out"])], gather_ffn=[("gather", ["w_ffn_in"])],
        gather_down=[("gather", ["w_ffn_down"])],
        ffn_in_bwd=[("swap", "down")], send_down=[("send", "down", ALL)],
        out_proj_bwd_merge=[("swap", "ffn")], send_ffn=[("send", "ffn", ALL)],
        conv_bwd=[("swap", "mix")], send_mix=[("send", "mix", ALL)],
        swap_inp=[("swap", "inp")], send_inp=[("send", "inp", ALL)])
    ASYNC = dict(gather_mix=1, gather_ffn=2, gather_down=3, send_down=4, send_ffn=5, send_mix=6, send_inp=7)

    def __init__(self, shards, c1):
        self.shards, self.c1 = shards, c1
        self.full, self.slots, self.got, self.sums, self.got3 = {}, {}, {}, {}, {}

    def weight(self, name):
        return self.full[name]

    def grad_ready(self, grads):
        for k, g in grads.items():
            self.slots[k] = g.reshape(N_DEV, g.shape[0] // N_DEV, g.shape[1])

    def _one(self, kind, what, ks=None):
        if kind == "gather":
            return _gather_carry([self.shards[k] for k in what])
        names = self.GROUPS[what]
        if kind == "swap":
            return _swap_carry([self.slots[k] for k in names])
        return _send_carry([self.sums[k] for k in names], ks)

    def carry(self, call):
        return _join([self._one(*ride) for ride in self.RIDES.get(call, [])])

    def done(self, call, outs):
        outs = list(outs)
        for kind, what, *_ in self.RIDES.get(call, []):
            names = what if kind == "gather" else self.GROUPS[what]
            mine, outs = outs[:len(names)], outs[len(names):]
            if kind == "gather":
                self.full.update(zip(names, mine))
            elif kind == "send":
                for k, r in zip(names, mine):
                    self.got3.setdefault(k, []).append(r)
            else:
                self.got.update(zip(names, mine))
                self.sums.update(zip(names, _chip_sums(f"chip_sums_{what}", [self.slots[k] for k in names], mine, self.c1)))

    def alone(self, call):
        self.done(call, _run_carry(call, self.carry(call)))

    def behind(self, group, x):
        return lax.optimization_barrier((x, tuple(self.sums[k] for k in self.GROUPS[group])))[0]

    def launch(self, call, after=None):
        carry = self._one(*self.RIDES[call][0])
        if after is not None:
            carry.arrays = list(lax.optimization_barrier((tuple(carry.arrays), after))[0])
        self.done(call, _run_carry_async(call, carry, self.ASYNC[call]))


def kernel(x, g_mix_norm, w_in, b_in, sinks, conv_w, conv_b, ln_g, ln_b, w_attn_proj, w_conv_proj, b_conv_proj, w_out, g_ffn_norm, w_ffn_in, w_ffn_down, g_final, loss_target, m_g_mix_norm, m_w_in, m_b_in, m_sinks, m_conv_w, m_conv_b, m_ln_g, m_ln_b, m_w_attn_proj, m_w_conv_proj, m_b_conv_proj, m_w_out, m_g_ffn_norm, m_w_ffn_in, m_w_ffn_down, m_g_final, v_g_mix_norm, v_w_in, v_b_in, v_sinks, v_conv_w, v_conv_b, v_ln_g, v_ln_b, v_w_attn_proj, v_w_conv_proj, v_b_conv_proj, v_w_out, v_g_ffn_norm, v_w_ffn_in, v_w_ffn_down, v_g_final):
    w = dict(g_mix_norm=g_mix_norm, w_in=w_in, b_in=b_in, sinks=sinks, conv_w=conv_w, conv_b=conv_b, ln_g=ln_g,
             ln_b=ln_b, w_attn_proj=w_attn_proj, w_conv_proj=w_conv_proj, b_conv_proj=b_conv_proj, w_out=w_out,
             g_ffn_norm=g_ffn_norm, w_ffn_in=w_ffn_in, w_ffn_down=w_ffn_down, g_final=g_final)
    m = dict(g_mix_norm=m_g_mix_norm, w_in=m_w_in, b_in=m_b_in, sinks=m_sinks, conv_w=m_conv_w, conv_b=m_conv_b,
             ln_g=m_ln_g, ln_b=m_ln_b, w_attn_proj=m_w_attn_proj, w_conv_proj=m_w_conv_proj,
             b_conv_proj=m_b_conv_proj, w_out=m_w_out, g_ffn_norm=m_g_ffn_norm, w_ffn_in=m_w_ffn_in,
             w_ffn_down=m_w_ffn_down, g_final=m_g_final)
    v = dict(g_mix_norm=v_g_mix_norm, w_in=v_w_in, b_in=v_b_in, sinks=v_sinks, conv_w=v_conv_w, conv_b=v_conv_b,
             ln_g=v_ln_g, ln_b=v_ln_b, w_attn_proj=v_w_attn_proj, w_conv_proj=v_w_conv_proj,
             b_conv_proj=v_b_conv_proj, w_out=v_w_out, g_ffn_norm=v_g_ffn_norm, w_ffn_in=v_w_ffn_in,
             w_ffn_down=v_w_ffn_down, g_final=v_g_final)
    ax, ay, ac = lax.axis_index("x"), lax.axis_index("y"), lax.axis_index("c")
    me = 4 * ax + 2 * ay + ac
    chip = 2 * ax + ay

    shards = {k: (w[k][0].T if tr else w[k][0]).astype(BF) for k, tr in BIG.items()}
    cw_shard = jnp.pad(conv_w[0].T, ((0, 0), (0, 1))).reshape(16, 128)
    wi_t, cw_full, h, r1 = _first_gather([shards["w_in"], cw_shard], x[0], g_mix_norm)
    conv_full = cw_full.reshape(CONV_CHANNELS, CONV_PAD).T

    as_row = lambda a: a.reshape(1, -1)
    small_w = {k: as_row(w[k]) for k in SMALL_NAMES}
    small_m = {k: as_row(m[k]) for k in SMALL_NAMES}
    small_v = {k: as_row(v[k]) for k in SMALL_NAMES}
    plan = _Plan(shards, ac.reshape(1).astype(jnp.int32))
    plan.launch("gather_mix", after=wi_t)
    dx, parts = _local_step(x[0], h, r1, loss_target[0], small_w, wi_t, conv_full, plan)

    ids = jnp.stack([me, chip]).astype(jnp.int32)
    grads, delta, new_m, new_v, after = {}, {}, {}, {}, dx
    packed = _small_pack(parts)
    for k in sorted(BIG, key=lambda k: k == "w_in"):
        if k == "w_in":
            packed = lax.optimization_barrier((packed, after))[0]
            small_gathered, = _run_carry_async("small_gather", _gather_carry([packed]), 8)
        flip = (lambda a: a.T) if BIG[k] else (lambda a: a)
        wk = lax.optimization_barrier((w[k][0], after))[0]
        outs = _grad_adamw(f"grad_adamw_{k}", plan.slots[k], plan.got[k], plan.got3[k], ids,
                           flip(wk), flip(m[k][0]), flip(v[k][0]))
        after = outs[0]
        grads[k], delta[k], new_m[k], new_v[k] = (flip(a)[None] for a in outs)

    loss, cw_grad, small_out = _small_adamw(small_gathered, small_w, small_m, small_v)
    for k in SMALL_NAMES:
        g, d, nm, nv = (a.reshape(w[k].shape) for a in small_out[k])
        grads[k], delta[k], new_m[k], new_v[k] = g, d, nm, nv
    cw_mine = lax.dynamic_slice(cw_grad, (0, me * 64), (CONV_WIDTH, 64))
    d, nm, nv = _adamw("adamw_conv_w", conv_w[0], cw_mine, m_conv_w[0], v_conv_w[0])
    grads["conv_w"], delta["conv_w"], new_m["conv_w"], new_v["conv_w"] = cw_mine[None], d[None], nm[None], nv[None]

    return (loss.reshape(()), dx[None], *[grads[k] for k in WEIGHT_NAMES], *[delta[k] for k in WEIGHT_NAMES],
            *[new_m[k] for k in WEIGHT_NAMES], *[new_v[k] for k in WEIGHT_NAMES])
```

```python
import functools

import jax
import jax.numpy as jnp
from jax import lax
from jax.experimental import pallas as pl
from jax.experimental.pallas import tpu as pltpu
from jax.experimental.pallas import tpu_sc as plsc

F32 = jnp.float32
BF = jnp.bfloat16

SEQ = 2048
D_MODEL = 1024
HEAD_DIM = 64
N_Q_HEADS = 8
N_KV_HEADS = 2
GROUP = N_Q_HEADS // N_KV_HEADS
BLOCK = 128
ATTN_WIDTH = 512
KV_WIDTH = 128
CONV_CHANNELS = 512
CONV_WIDTH = 31
CONV_PAD = 32
GLU_OFF = 768
GATE_OFF = 1792
IN_WIDTH = 3840
D_FF = 2816
EPS = 1e-5
NEG = -1e30
N_DEV = 8

ADAM_LR = 0.001
ADAM_B1 = 0.9
ADAM_B2 = 0.999
ADAM_EPS = 1e-08
ADAM_WD = 0.01
ADAM_STEP = 10

VMEM_LIMIT_BYTES = 56 * 1024 * 1024
MESH = pl.DeviceIdType.MESH
ANY = pl.BlockSpec(memory_space=pl.ANY)

_DIMS = {"NN": (((1,), (0,)), ((), ())), "NT": (((1,), (1,)), ((), ())), "TN": (((0,), (0,)), ((), ()))}


def _params(sem):
    return pltpu.CompilerParams(dimension_semantics=sem, vmem_limit_bytes=VMEM_LIMIT_BYTES)


class _Carry:
    def __init__(self, arrays, out_shapes, sems, start, finish, peers=None):
        self.arrays, self.out_shapes, self.sems, self.start, self.finish = arrays, out_shapes, sems, start, finish
        self.peers = peers


def _carry_io(carry):
    if carry is None:
        return [], [], []
    return list(carry.arrays), list(carry.out_shapes), list(carry.sems)


def _matmul(name, a_list, b, mode, *, m, n, tm, tn, tk=None, epilogue, extra=(), outs, b_off=(0, 0), alias=None,
            scratch=(), carry=None):
    seg_k = [a.shape[0] if mode == "TN" else a.shape[1] for a in a_list]
    whole = tk is None
    seg_nk = [1] * len(a_list) if whole else [ks // tk for ks in seg_k]
    nk = 1 if whole else sum(seg_nk)
    starts = [sum(seg_nk[:s]) for s in range(len(seg_nk))]
    k_starts = [sum(seg_k[:s]) for s in range(len(seg_k))]
    k_tot = sum(seg_k)
    n_a, n_extra, n_out = len(a_list), len(extra), len(outs)

    a_specs = []
    for st, ns, ks in zip(starts, seg_nk, seg_k):
        if mode == "TN":
            a_specs.append(pl.BlockSpec((ks if whole else tk, tm), lambda j, i, k: (k, i)))
        elif whole:
            a_specs.append(pl.BlockSpec((tm, ks), lambda j, i, k: (i, 0)))
        else:
            a_specs.append(pl.BlockSpec((tm, tk), functools.partial(
                lambda j, i, k, st, ns: (i, jnp.clip(k - st, 0, ns - 1)), st=st, ns=ns)))
    bk = k_tot if whole else tk
    if mode == "NT":
        b_spec = pl.BlockSpec((tn, bk), lambda j, i, k: (b_off[0] + j, b_off[1] + k))
    else:
        b_spec = pl.BlockSpec((bk, tn), lambda j, i, k: (b_off[0] + k, b_off[1] + j))
    n_alias = 0 if alias is None else 1
    c_in, c_out, c_sems = _carry_io(carry)
    n_acc = 0 if whole else 1
    nj, ni = n // tn, m // tm

    def body(*refs):
        pos = [n_a, 1, n_alias, n_extra, len(c_in), n_out, len(c_out), n_acc, len(scratch), len(c_sems)]
        cuts = [sum(pos[:q]) for q in range(len(pos) + 1)]
        a_refs, (b_ref,), _, ex, ci_refs, out_refs, co_refs, acc_refs, scr, cs_refs = (
            refs[cuts[q]:cuts[q + 1]] for q in range(len(pos)))
        j, i, k = pl.program_id(0), pl.program_id(1), pl.program_id(2)
        ids = (j, i)
        if carry is not None:
            @pl.when((j == 0) & (i == 0) & (k == 0))
            def _():
                carry.start(ci_refs, co_refs, cs_refs)

        def dot(a_ref, bv):
            return lax.dot_general(a_ref[...].astype(BF), bv.astype(BF), _DIMS[mode], preferred_element_type=F32)

        if whole:
            tot = None
            for a_ref, k0, ks in zip(a_refs, k_starts, seg_k):
                if n_a == 1:
                    bv = b_ref[...]
                else:
                    bv = b_ref[:, k0:k0 + ks] if mode == "NT" else b_ref[k0:k0 + ks, :]
                part = dot(a_ref, bv)
                tot = part if tot is None else tot + part
            epilogue(tot, ex, out_refs, ids, scr)
        else:
            acc, = acc_refs

            @pl.when(k == 0)
            def _():
                acc[...] = jnp.zeros_like(acc)

            for a_ref, st, ns in zip(a_refs, starts, seg_nk):
                if n_a == 1:
                    acc[...] += dot(a_ref, b_ref[...])
                else:
                    @pl.when((k >= st) & (k < st + ns))
                    def _(a_ref=a_ref):
                        acc[...] += dot(a_ref, b_ref[...])

            @pl.when(k == nk - 1)
            def _():
                epilogue(acc[...], ex, out_refs, ids, scr)

        if carry is not None:
            @pl.when((j == nj - 1) & (i == ni - 1) & (k == nk - 1))
            def _():
                carry.finish(ci_refs, co_refs, cs_refs)

    in_specs = [*a_specs, b_spec]
    args = [*a_list, b]
    io_alias = {}
    if alias is not None:
        in_specs.append(pl.BlockSpec(memory_space=pl.ANY))
        args.append(alias[0])
        io_alias = {n_a + 1: alias[1]}
    in_specs += [s for _, s in extra] + [pl.BlockSpec(memory_space=pl.ANY)] * len(c_in)
    args += [x for x, _ in extra] + c_in
    res = pl.pallas_call(
        body, name=name, grid=(nj, ni, nk), in_specs=in_specs,
        out_specs=[s for _, s in outs] + [pl.BlockSpec(memory_space=pl.ANY)] * len(c_out),
        out_shape=[o for o, _ in outs] + c_out,
        scratch_shapes=[*([] if whole else [pltpu.VMEM((tm, tn), F32)]), *scratch, *c_sems],
        input_output_aliases=io_alias,
        compiler_params=_params(("arbitrary", "arbitrary", "arbitrary")),
    )(*args)
    return res if carry is None else (res[:n_out], res[n_out:])


def _tile(tm, tn):
    return pl.BlockSpec((tm, tn), lambda j, i, k: (i, j))


def _row(tn):
    return pl.BlockSpec((1, tn), lambda j, i, k: (0, j))


def _store(dtype):
    def ep(acc, ex, outs, ids, scr):
        outs[0][...] = acc.astype(dtype)
    return ep


def _sds(shape, dtype):
    return jax.ShapeDtypeStruct(shape, dtype)


def _rms_bwd(dh, xv, r, g):
    xh = xv * r
    dxh = dh * g
    dx = r * (dxh - xh * jnp.mean(dxh * xh, axis=-1, keepdims=True))
    return dx, jnp.sum(dh * xh, axis=0, keepdims=True)


def _accumulate_rows(ref, val, first):
    @pl.when(first)
    def _():
        ref[...] = val

    @pl.when(jnp.logical_not(first))
    def _():
        ref[...] += val


def _loss_head(xv, g, target):
    r = lax.rsqrt(jnp.mean(xv * xv, axis=-1, keepdims=True) + EPS)
    err = xv * r * g - target
    dx, dg = _rms_bwd(err * (1.0 / xv.shape[-1]), xv, r, g)
    part = 0.5 * jnp.sum(jnp.mean(err * err, axis=-1, keepdims=True), axis=0, keepdims=True)
    return dx, dg, part


def _lane_half(shape, h):
    lane = lax.broadcasted_iota(jnp.int32, shape, 1)
    return (lane >= HEAD_DIM * h) & (lane < HEAD_DIM * (h + 1))


def _to_half(v, w, h):
    if w != h:
        v = pltpu.roll(v, HEAD_DIM, 1)
    return jnp.where(_lane_half(v.shape, h), v, 0.0)


def _attn_block(qkv_ref, sinks_ref, n, h):
    r0 = pl.multiple_of(n * BLOCK, BLOCK)
    p0 = pl.multiple_of(jnp.maximum(n - 1, 0) * BLOCK, BLOCK)
    rows = pl.ds(r0, BLOCK)
    prev = pl.ds(p0, BLOCK)
    k2 = jnp.concatenate([qkv_ref[prev, ATTN_WIDTH:ATTN_WIDTH + KV_WIDTH],
                          qkv_ref[rows, ATTN_WIDTH:ATTN_WIDTH + KV_WIDTH]], axis=0)
    v2 = jnp.concatenate([qkv_ref[prev, ATTN_WIDTH + KV_WIDTH:ATTN_WIDTH + 2 * KV_WIDTH],
                          qkv_ref[rows, ATTN_WIDTH + KV_WIDTH:ATTN_WIDTH + 2 * KV_WIDTH]], axis=0)
    qs = []
    for g in range(GROUP):
        hq = GROUP * h + g
        blk = qkv_ref[rows, (hq // 2) * 128:(hq // 2 + 1) * 128].astype(F32)
        qs.append(_to_half(blk, hq % 2, h))
    q4 = jnp.concatenate(qs, axis=0).astype(BF)
    s = lax.dot_general(q4, k2, _DIMS["NT"], preferred_element_type=F32) * (HEAD_DIM ** -0.5)
    shape = s.shape
    row = lax.broadcasted_iota(jnp.int32, shape, 0)
    qi = row & (BLOCK - 1)
    kj = lax.broadcasted_iota(jnp.int32, shape, 1)
    diff = qi + BLOCK - kj
    valid = (diff >= 0) & (diff < BLOCK) & ((kj >= BLOCK) | (n > 0))
    s = jnp.where(valid, s, NEG)
    row1 = lax.broadcasted_iota(jnp.int32, (shape[0], 1), 0)
    sink = jnp.zeros((shape[0], 1), F32)
    for g in range(GROUP):
        sink = jnp.where((row1 >= g * BLOCK) & (row1 < (g + 1) * BLOCK), sinks_ref[0, GROUP * h + g], sink)
    m = jnp.maximum(jnp.max(s, axis=-1, keepdims=True), sink)
    e = jnp.exp(s - m)
    es = jnp.exp(sink - m)
    inv = 1.0 / (jnp.sum(e, axis=-1, keepdims=True) + es)
    return e * inv, es * inv, q4, k2, v2, rows, prev


def _attn_fwd(proj, sinks, carry=None):
    T = proj.shape[0]
    c_in, c_out, c_sems = _carry_io(carry)

    def body(*refs):
        qkv_ref, sinks_ref = refs[:2]
        ci_refs = refs[2:2 + len(c_in)]
        o_ref = refs[2 + len(c_in)]
        co_refs = refs[3 + len(c_in):3 + len(c_in) + len(c_out)]
        cs_refs = refs[3 + len(c_in) + len(c_out):]
        if carry is not None:
            carry.start(ci_refs, co_refs, cs_refs)

        def blk(n, z):
            outs = [None] * (N_Q_HEADS // 2)
            for h in range(N_KV_HEADS):
                p, _, _, _, v2, rows, _ = _attn_block(qkv_ref, sinks_ref, n, h)
                o = lax.dot_general(p.astype(BF), v2, _DIMS["NN"], preferred_element_type=F32)
                for g in range(GROUP):
                    hq = GROUP * h + g
                    piece = jnp.where(_lane_half((BLOCK, 128), h), o[g * BLOCK:(g + 1) * BLOCK], 0.0)
                    if hq % 2 != h:
                        piece = pltpu.roll(piece, HEAD_DIM, 1)
                    outs[hq // 2] = piece if outs[hq // 2] is None else outs[hq // 2] + piece
            for pb in range(N_Q_HEADS // 2):
                o_ref[rows, pb * 128:(pb + 1) * 128] = outs[pb].astype(BF)
            return z

        lax.fori_loop(0, T // BLOCK, blk, 0)
        if carry is not None:
            carry.finish(ci_refs, co_refs, cs_refs)

    res = pl.pallas_call(
        body, name="attn_fwd", grid=(1,),
        in_specs=[pl.BlockSpec((T, GLU_OFF), lambda i: (0, 0)), pl.BlockSpec(memory_space=pltpu.SMEM),
                  *[ANY] * len(c_in)],
        out_specs=[pl.BlockSpec((T, ATTN_WIDTH), lambda i: (0, 0)), *[ANY] * len(c_out)],
        out_shape=[_sds((T, ATTN_WIDTH), BF), *c_out], scratch_shapes=c_sems,
        compiler_params=_params(("arbitrary",)),
    )(proj, sinks, *c_in)
    return res[0], res[1:]


def _attn_bwd(proj, d_o, sinks, carry=None):
    T = proj.shape[0]
    c_in, c_out, c_sems = _carry_io(carry)

    def body(*refs):
        qkv_ref, do_ref, sinks_ref = refs[:3]
        ci_refs = refs[3:3 + len(c_in)]
        dqkv_ref, dsink_ref = refs[3 + len(c_in):5 + len(c_in)]
        co_refs = refs[5 + len(c_in):5 + len(c_in) + len(c_out)]
        dk_acc, dv_acc = refs[5 + len(c_in) + len(c_out):7 + len(c_in) + len(c_out)]
        cs_refs = refs[7 + len(c_in) + len(c_out):]
        if carry is not None:
            carry.start(ci_refs, co_refs, cs_refs)
        dsink_ref[...] = jnp.zeros_like(dsink_ref)
        dk_acc[...] = jnp.zeros_like(dk_acc)
        dv_acc[...] = jnp.zeros_like(dv_acc)

        def blk(n, carry):
            dqs = [None] * (N_Q_HEADS // 2)
            for h in range(N_KV_HEADS):
                p, psink, q4, k2, v2, rows, prev = _attn_block(qkv_ref, sinks_ref, n, h)
                dos = []
                for g in range(GROUP):
                    hq = GROUP * h + g
                    dos.append(_to_half(do_ref[rows, (hq // 2) * 128:(hq // 2 + 1) * 128].astype(F32), hq % 2, h))
                do4 = jnp.concatenate(dos, axis=0).astype(BF)
                dp = lax.dot_general(do4, v2, _DIMS["NT"], preferred_element_type=F32)
                delta = jnp.sum(p * dp, axis=-1, keepdims=True)
                ds = (p * (dp - delta) * (HEAD_DIM ** -0.5)).astype(BF)
                dsk = psink * delta
                for g in range(GROUP):
                    hq = GROUP * h + g
                    tot = -jnp.sum(dsk[g * BLOCK:(g + 1) * BLOCK], axis=0, keepdims=True)
                    lane = lax.broadcasted_iota(jnp.int32, (1, 128), 1)
                    dsink_ref[...] += jnp.where(lane == hq, tot, 0.0)
                dq = lax.dot_general(ds, k2, _DIMS["NN"], preferred_element_type=F32)
                dk = lax.dot_general(ds, q4, _DIMS["TN"], preferred_element_type=F32)
                dv = lax.dot_general(p.astype(BF), do4, _DIMS["TN"], preferred_element_type=F32)
                dk_acc[prev, :] += dk[:BLOCK]
                dk_acc[rows, :] += dk[BLOCK:]
                dv_acc[prev, :] += dv[:BLOCK]
                dv_acc[rows, :] += dv[BLOCK:]
                for g in range(GROUP):
                    hq = GROUP * h + g
                    piece = jnp.where(_lane_half((BLOCK, 128), h), dq[g * BLOCK:(g + 1) * BLOCK], 0.0)
                    if hq % 2 != h:
                        piece = pltpu.roll(piece, HEAD_DIM, 1)
                    dqs[hq // 2] = piece if dqs[hq // 2] is None else dqs[hq // 2] + piece
            for pb in range(N_Q_HEADS // 2):
                dqkv_ref[rows, pb * 128:(pb + 1) * 128] = dqs[pb].astype(BF)
            return carry

        lax.fori_loop(0, T // BLOCK, blk, 0)
        dqkv_ref[:, ATTN_WIDTH:ATTN_WIDTH + KV_WIDTH] = dk_acc[...].astype(BF)
        dqkv_ref[:, ATTN_WIDTH + KV_WIDTH:] = dv_acc[...].astype(BF)
        if carry is not None:
            carry.finish(ci_refs, co_refs, cs_refs)

    res = pl.pallas_call(
        body, name="attn_bwd", grid=(1,),
        in_specs=[pl.BlockSpec((T, GLU_OFF), lambda i: (0, 0)), pl.BlockSpec((T, ATTN_WIDTH), lambda i: (0, 0)),
                  pl.BlockSpec(memory_space=pltpu.SMEM), *[ANY] * len(c_in)],
        out_specs=[pl.BlockSpec((T, GLU_OFF), lambda i: (0, 0)), pl.BlockSpec((1, 128), lambda i: (0, 0)),
                   *[ANY] * len(c_out)],
        out_shape=[_sds((T, GLU_OFF), BF), _sds((1, 128), F32), *c_out],
        scratch_shapes=[pltpu.VMEM((T, KV_WIDTH), F32), pltpu.VMEM((T, KV_WIDTH), F32), *c_sems],
        compiler_params=_params(("arbitrary",)),
    )(proj, d_o, sinks, *c_in)
    return res[:2], res[2:]


CHUNK = 256
SUB = 32
WIN = CHUNK + 32
PAD_ROWS = SEQ + 2 * CONV_PAD
_GLU_SPECS = [pl.BlockSpec((SEQ, 256), functools.partial(lambda i, c: (0, c), c=GLU_OFF // 256 + c)) for c in range(4)]


def _glu_to_pad(a0, a1, b0, b1, zpad):
    C = CONV_CHANNELS
    zpad[0:CONV_PAD, :] = jnp.zeros((CONV_PAD, C), F32)
    zpad[CONV_PAD + SEQ:, :] = jnp.zeros((CONV_PAD, C), F32)
    zpad[CONV_PAD:CONV_PAD + SEQ, 0:256] = a0[...].astype(F32) * jax.nn.sigmoid(b0[...].astype(F32))
    zpad[CONV_PAD:CONV_PAD + SEQ, 256:C] = a1[...].astype(F32) * jax.nn.sigmoid(b1[...].astype(F32))


def _tap_windows(src, base, win):
    for b in range(8):
        win[b, 0:WIN - 8, :] = src[base + b:base + b + WIN - 8, :]


def _taps(win, w_ref, init, out, flip):
    def sub(si, carry):
        r0 = pl.multiple_of(si * SUB, SUB)
        acc = jnp.broadcast_to(init, (SUB, CONV_CHANNELS))
        for k in range(CONV_WIDTH):
            wk = (CONV_WIDTH - 1 - k) if flip else k
            acc = acc + w_ref[wk:wk + 1, :] * win[k % 8, pl.ds(r0 + 8 * (k // 8), SUB), :]
        out[pl.ds(r0, SUB), :] = acc
        return carry

    lax.fori_loop(0, CHUNK // SUB, sub, 0)


def _tap_grads(win, du, dwacc):
    def sub(si, carry):
        r0 = pl.multiple_of(si * SUB, SUB)
        d = du[pl.ds(r0, SUB), :]
        for k in range(CONV_WIDTH):
            p = d * win[k % 8, pl.ds(r0 + 8 * (k // 8), SUB), :]
            dwacc[8 * k:8 * k + 8, :] += (p[0:8] + p[8:16]) + (p[16:24] + p[24:32])
        return carry

    lax.fori_loop(0, CHUNK // SUB, sub, 0)


def _ln_parts(u):
    mu = jnp.mean(u, axis=-1, keepdims=True)
    xc = u - mu
    rstd = lax.rsqrt(jnp.mean(xc * xc, axis=-1, keepdims=True) + EPS)
    return xc * rstd, rstd


def _conv_fwd(proj, conv_w, conv_b, ln_g, ln_b, carry=None):
    T, C = proj.shape[0], CONV_CHANNELS
    vec = pl.BlockSpec((1, C), lambda i: (0, 0))
    c_in, c_out, c_sems = _carry_io(carry)

    def body(*refs):
        a0, a1, b0, b1, w_ref, cb_ref, g_ref, be_ref = refs[:8]
        ci_refs = refs[8:8 + len(c_in)]
        c_ref, u_ref = refs[8 + len(c_in):10 + len(c_in)]
        co_refs = refs[10 + len(c_in):10 + len(c_in) + len(c_out)]
        zpad, win, ubuf = refs[10 + len(c_in) + len(c_out):13 + len(c_in) + len(c_out)]
        cs_refs = refs[13 + len(c_in) + len(c_out):]
        if carry is not None:
            carry.start(ci_refs, co_refs, cs_refs)
        _glu_to_pad(a0, a1, b0, b1, zpad)
        for ci in range(T // CHUNK):
            _tap_windows(zpad, ci * CHUNK + CONV_PAD - (CONV_WIDTH - 1), win)
            _taps(win, w_ref, cb_ref[...], ubuf, False)
            u = ubuf[...]
            u_ref[ci * CHUNK:(ci + 1) * CHUNK, :] = u
            xh, _ = _ln_parts(u)
            ln = xh * g_ref[...] + be_ref[...]
            c_ref[ci * CHUNK:(ci + 1) * CHUNK, :] = (ln * jax.nn.sigmoid(ln)).astype(BF)
        if carry is not None:
            carry.finish(ci_refs, co_refs, cs_refs)

    res = pl.pallas_call(
        body, name="conv_fwd", grid=(1,),
        in_specs=[*_GLU_SPECS, pl.BlockSpec((CONV_PAD, C), lambda i: (0, 0)), vec, vec, vec, *[ANY] * len(c_in)],
        out_specs=[pl.BlockSpec((T, C), lambda i: (0, 0)), pl.BlockSpec((T, C), lambda i: (0, 0)), *[ANY] * len(c_out)],
        out_shape=[_sds((T, C), BF), _sds((T, C), F32), *c_out],
        scratch_shapes=[pltpu.VMEM((PAD_ROWS, C), F32), pltpu.VMEM((8, WIN, C), F32), pltpu.VMEM((CHUNK, C), F32),
                        *c_sems],
        compiler_params=_params(("arbitrary",)),
    )(proj, proj, proj, proj, conv_w, conv_b, ln_g, ln_b, *c_in)
    return res[:2], res[2:]


def _conv_bwd(proj, u, d_c, conv_w, conv_b, ln_g, ln_b, carry=None):
    T, C = proj.shape[0], CONV_CHANNELS
    vec = pl.BlockSpec((1, C), lambda i: (0, 0))
    wspec = pl.BlockSpec((CONV_PAD, C), lambda i: (0, 0))
    c_in, c_out, c_sems = _carry_io(carry)

    def body(*refs):
        a0, a1, b0, b1, u_ref, dc_ref, w_ref, cb_ref, g_ref, be_ref = refs[:10]
        ci_refs = refs[10:10 + len(c_in)]
        o = 10 + len(c_in)
        dglu_ref, dw_ref, dcb_ref, dg_ref, dbe_ref = refs[o:o + 5]
        co_refs = refs[o + 5:o + 5 + len(c_out)]
        zpad, dupad, win, ubuf, dwacc = refs[o + 5 + len(c_out):o + 10 + len(c_out)]
        cs_refs = refs[o + 10 + len(c_out):]
        if carry is not None:
            carry.start(ci_refs, co_refs, cs_refs)
        _glu_to_pad(a0, a1, b0, b1, zpad)
        dupad[T:, :] = jnp.zeros((2 * CONV_PAD, C), F32)
        dwacc[...] = jnp.zeros_like(dwacc)
        dcb_ref[...] = jnp.zeros_like(dcb_ref)
        dg_ref[...] = jnp.zeros_like(dg_ref)
        dbe_ref[...] = jnp.zeros_like(dbe_ref)
        for ci in range(T // CHUNK):
            rows = slice(ci * CHUNK, (ci + 1) * CHUNK)
            _tap_windows(zpad, ci * CHUNK + CONV_PAD - (CONV_WIDTH - 1), win)
            xh, rstd = _ln_parts(u_ref[rows, :])
            ln = xh * g_ref[...] + be_ref[...]
            sg = jax.nn.sigmoid(ln)
            dln = dc_ref[rows, :].astype(F32) * (sg * (1.0 + ln * (1.0 - sg)))
            dg_ref[...] += jnp.sum(dln * xh, axis=0, keepdims=True)
            dbe_ref[...] += jnp.sum(dln, axis=0, keepdims=True)
            dxh = dln * g_ref[...]
            du = rstd * (dxh - jnp.mean(dxh, axis=-1, keepdims=True)
                         - xh * jnp.mean(dxh * xh, axis=-1, keepdims=True))
            dupad[rows, :] = du
            dcb_ref[...] += jnp.sum(du, axis=0, keepdims=True)
            _tap_grads(win, dupad.at[rows, :], dwacc)
        for k in range(CONV_WIDTH):
            dw_ref[k:k + 1, :] = jnp.sum(dwacc[8 * k:8 * k + 8, :], axis=0, keepdims=True)
        dw_ref[CONV_WIDTH:, :] = jnp.zeros((CONV_PAD - CONV_WIDTH, C), F32)
        for ci in range(T // CHUNK):
            rows = slice(ci * CHUNK, (ci + 1) * CHUNK)
            _tap_windows(dupad, ci * CHUNK, win)
            _taps(win, w_ref, jnp.zeros((1, C), F32), ubuf, True)
            dz = ubuf[...]
            for half, (a, b) in enumerate(((a0, b0), (a1, b1))):
                sb = jax.nn.sigmoid(b[rows, :].astype(F32))
                dzh = dz[:, half * 256:(half + 1) * 256]
                dglu_ref[rows, half * 256:(half + 1) * 256] = (dzh * sb).astype(BF)
                dglu_ref[rows, C + half * 256:C + (half + 1) * 256] = (
                    dzh * a[rows, :].astype(F32) * sb * (1.0 - sb)).astype(BF)
        if carry is not None:
            carry.finish(ci_refs, co_refs, cs_refs)

    res = pl.pallas_call(
        body, name="conv_bwd", grid=(1,),
        in_specs=[*_GLU_SPECS, pl.BlockSpec((T, C), lambda i: (0, 0)), pl.BlockSpec((T, C), lambda i: (0, 0)), wspec,
                  vec, vec, vec, *[ANY] * len(c_in)],
        out_specs=[pl.BlockSpec((T, 2 * C), lambda i: (0, 0)), wspec, vec, vec, vec, *[ANY] * len(c_out)],
        out_shape=[_sds((T, 2 * C), BF), _sds((CONV_PAD, C), F32), _sds((1, C), F32), _sds((1, C), F32),
                   _sds((1, C), F32), *c_out],
        scratch_shapes=[pltpu.VMEM((PAD_ROWS, C), F32), pltpu.VMEM((PAD_ROWS, C), F32), pltpu.VMEM((8, WIN, C), F32),
                        pltpu.VMEM((CHUNK, C), F32), pltpu.VMEM((8 * CONV_PAD, C), F32), *c_sems],
        compiler_params=_params(("arbitrary",)),
    )(proj, proj, proj, proj, u, d_c, conv_w, conv_b, ln_g, ln_b, *c_in)
    return res[:5], res[5:]


_GATE_BLK = GATE_OFF // 256


def _ffn_in_swiglu(h2, wf_t, carry=None):
    T, D = h2.shape
    tm, tn = 1024, D_FF // 2
    nj, ni = D_FF // tn, T // tm
    c_in, c_out, c_sems = _carry_io(carry)

    def body(*refs):
        a_ref, bg_ref, bu_ref = refs[:3]
        ci_refs = refs[3:3 + len(c_in)]
        act_ref, g_ref, u_ref = refs[3 + len(c_in):6 + len(c_in)]
        co_refs = refs[6 + len(c_in):6 + len(c_in) + len(c_out)]
        cs_refs = refs[6 + len(c_in) + len(c_out):]
        j, i = pl.program_id(0), pl.program_id(1)
        if carry is not None:
            @pl.when((j == 0) & (i == 0))
            def _():
                carry.start(ci_refs, co_refs, cs_refs)
        a = a_ref[...]
        for c0, c1 in ((0, 768), (768, tn)):
            g = lax.dot_general(a, bg_ref[c0:c1, :], _DIMS["NT"], preferred_element_type=F32)
            u = lax.dot_general(a, bu_ref[c0:c1, :], _DIMS["NT"], preferred_element_type=F32)
            act_ref[:, c0:c1] = (g * jax.nn.sigmoid(g) * u).astype(BF)
            g_ref[:, c0:c1] = g.astype(BF)
            u_ref[:, c0:c1] = u.astype(BF)
        if carry is not None:
            @pl.when((j == nj - 1) & (i == ni - 1))
            def _():
                carry.finish(ci_refs, co_refs, cs_refs)

    t = pl.BlockSpec((tm, tn), lambda j, i: (i, j))
    res = pl.pallas_call(
        body, name="ffn_in_swiglu", grid=(nj, ni),
        in_specs=[pl.BlockSpec((tm, D), lambda j, i: (i, 0)), pl.BlockSpec((tn, D), lambda j, i: (j, 0)),
                  pl.BlockSpec((tn, D), lambda j, i: (nj + j, 0)), *[ANY] * len(c_in)],
        out_specs=[t, t, t, *[ANY] * len(c_out)], out_shape=[*[_sds((T, D_FF), BF)] * 3, *c_out],
        scratch_shapes=c_sems,
        compiler_params=_params(("arbitrary", "arbitrary")),
    )(h2, wf_t, wf_t, *c_in)
    return res[:3], res[3:]


def _proj_merge(o, c, wap_t, wcp_t, b_cp, proj):
    T, D = o.shape[0], wap_t.shape[0]
    tm, tg = T, 256
    nj = D // tg

    def body(o_ref, c_ref, wa_ref, wc_ref, b_ref, g0_ref, g1_ref, ya_ref, yc_ref, m_ref):
        ya = lax.dot_general(o_ref[...], wa_ref[...], _DIMS["NT"], preferred_element_type=F32)
        yc = lax.dot_general(c_ref[...], wc_ref[...], _DIMS["NT"], preferred_element_type=F32) + b_ref[...]
        ya_ref[...] = ya.astype(BF)
        yc_ref[...] = yc.astype(BF)
        m_ref[...] = (jax.nn.sigmoid(g0_ref[...].astype(F32)) * ya + jax.nn.sigmoid(g1_ref[...].astype(F32)) * yc).astype(BF)

    act = pl.BlockSpec((tm, o.shape[1]), lambda j, i: (i, 0))
    wgt = pl.BlockSpec((tg, o.shape[1]), lambda j, i: (j, 0))
    t = pl.BlockSpec((tm, tg), lambda j, i: (i, j))
    return pl.pallas_call(
        body, name="proj_merge", grid=(nj, T // tm),
        in_specs=[act, act, wgt, wgt, pl.BlockSpec((1, tg), lambda j, i: (0, j)),
                  pl.BlockSpec((tm, tg), lambda j, i: (i, _GATE_BLK + j)),
                  pl.BlockSpec((tm, tg), lambda j, i: (i, _GATE_BLK + nj + j))],
        out_specs=[t, t, t], out_shape=[_sds((T, D), BF)] * 3,
        compiler_params=_params(("arbitrary", "arbitrary")),
    )(o, c, wap_t, wcp_t, b_cp, proj, proj)


def _stacked_dw(name, segs, h, tb):
    T, D = h.shape
    nblk = [seg.shape[1] // tb for seg in segs]
    starts = [sum(nblk[:q]) for q in range(len(segs))]
    n_seg = len(segs)

    def body(*refs):
        seg_refs, h_ref, o_ref, cs_ref = refs[:n_seg], refs[n_seg], refs[n_seg + 1], refs[n_seg + 2]
        i = pl.program_id(0)
        for seg_ref, st, nb in zip(seg_refs, starts, nblk):
            @pl.when((i >= st) & (i < st + nb))
            def _(seg_ref=seg_ref):
                a = seg_ref[...]
                o_ref[...] = lax.dot_general(a, h_ref[...], _DIMS["TN"], preferred_element_type=F32).astype(BF)
                cs_ref[...] = jnp.sum(a.astype(F32), axis=0, keepdims=True)

    in_specs = [pl.BlockSpec((T, tb), functools.partial(lambda i, st, nb: (0, jnp.clip(i - st, 0, nb - 1)), st=st, nb=nb))
                for st, nb in zip(starts, nblk)]
    return pl.pallas_call(
        body, name=name, grid=(sum(nblk),),
        in_specs=[*in_specs, pl.BlockSpec((T, D), lambda i: (0, 0))],
        out_specs=[pl.BlockSpec((tb, D), lambda i: (i, 0)), pl.BlockSpec((1, tb), lambda i: (0, i))],
        out_shape=[_sds((sum(nblk) * tb, D), BF), _sds((1, sum(nblk) * tb), F32)],
        compiler_params=_params(("arbitrary",)),
    )(*segs, h)


def _local_step(x, h, r1, target, small, wi_t, conv_w, plan):
    T, D = x.shape
    tm = 1024

    def carried(call, res, carry):
        if carry is None:
            return res
        outs, got = res
        plan.done(call, got)
        return outs


    def ep_add(acc, ex, outs, ids, scr):
        outs[0][...] = acc + ex[0][...]

    tn_in = IN_WIDTH // 2
    carry = plan.carry("proj_in")
    def ep_bias_bf16(acc, ex, outs, ids, scr):
        outs[0][...] = (acc + ex[0][...]).astype(BF)

    proj, = carried("proj_in", _matmul("proj_in", [h], wi_t, "NT", m=T, n=IN_WIDTH, tm=tm, tn=tn_in,
                                       epilogue=ep_bias_bf16, extra=[(small["b_in"], _row(tn_in))],
                                       outs=[(_sds((T, IN_WIDTH), BF), _tile(tm, tn_in))], carry=carry), carry)
    plan.launch("gather_ffn", after=proj)
    o, got = _attn_fwd(proj, small["sinks"], carry=plan.carry("attn_fwd"))
    plan.done("attn_fwd", got)
    (c, u_conv), got = _conv_fwd(proj, conv_w, small["conv_b"], small["ln_g"], small["ln_b"],
                                 carry=plan.carry("conv_fwd"))
    plan.done("conv_fwd", got)
    wap_t, wcp_t, w_out = plan.weight("w_attn_proj"), plan.weight("w_conv_proj"), plan.weight("w_out")
    ya, yc, merged = _proj_merge(o, c, wap_t, wcp_t, small["b_conv_proj"], proj)

    tg = 256
    gate_specs = [pl.BlockSpec((T, tg), lambda j, i, k: (i, _GATE_BLK + j)),
                  pl.BlockSpec((T, tg), lambda j, i, k: (i, _GATE_BLK + D // tg + j))]

    def ep_residual_rms(acc, ex, outs, ids, scr):
        x2v = acc + ex[0][...]
        r = lax.rsqrt(jnp.mean(x2v * x2v, axis=-1, keepdims=True) + EPS)
        outs[0][...] = x2v
        outs[1][...] = (x2v * r * ex[1][...]).astype(BF)
        outs[2][...] = r

    carry = plan.carry("out_proj")
    x2, h2, r2 = carried("out_proj", _matmul(
        "out_proj_rms", [merged], w_out, "NN", m=T, n=D, tm=512, tn=D, epilogue=ep_residual_rms,
        extra=[(x, _tile(512, D)), (small["g_ffn_norm"], _row(D))],
        outs=[(_sds((T, D), F32), _tile(512, D)), (_sds((T, D), BF), _tile(512, D)),
              (_sds((T, 1), F32), pl.BlockSpec((512, 1), lambda j, i, k: (i, 0)))], carry=carry), carry)
    plan.launch("gather_down", after=x2)
    wf_t = plan.weight("w_ffn_in")
    (act, gate, up), got = _ffn_in_swiglu(h2, wf_t, carry=plan.carry("ffn_in_swiglu"))
    plan.done("ffn_in_swiglu", got)
    w_down = plan.weight("w_ffn_down")
    def ep_residual_loss(acc, ex, outs, ids, scr):
        dx, dg, part = _loss_head(acc + ex[0][...], ex[1][...], ex[2][...])
        outs[0][...] = dx
        outs[1][...] = dx.astype(BF)
        _accumulate_rows(outs[2], dg, ids[1] == 0)
        _accumulate_rows(outs[3], part, ids[1] == 0)

    dx3, dx3_b, dg_final, loss = _matmul(
        "ffn_down_loss", [act], w_down, "NN", m=T, n=D, tm=256, tn=D, epilogue=ep_residual_loss,
        extra=[(x2, _tile(256, D)), (small["g_final"], _row(D)), (target, _tile(256, D))],
        outs=[(_sds((T, D), F32), _tile(256, D)), (_sds((T, D), BF), _tile(256, D)), (_sds((1, D), F32), _row(D)),
              (_sds((1, 1), F32), pl.BlockSpec((1, 1), lambda j, i, k: (0, 0)))])

    tn_ff = D_FF // 2

    def ep_swiglu_bwd(acc, ex, outs, ids, scr):
        g, u = ex[0][...].astype(F32), ex[1][...].astype(F32)
        sg = jax.nn.sigmoid(g)
        outs[0][...] = (acc * u * sg * (1.0 + g * (1.0 - sg))).astype(BF)
        outs[1][...] = (acc * g * sg).astype(BF)

    dgate, dup = _matmul(
        "ffn_down_bwd", [dx3_b], w_down, "NT", m=T, n=D_FF, tm=tm, tn=tn_ff, epilogue=ep_swiglu_bwd,
        extra=[(gate, _tile(tm, tn_ff)), (up, _tile(tm, tn_ff))],
        outs=[(_sds((T, D_FF), BF), _tile(tm, tn_ff)), (_sds((T, D_FF), BF), _tile(tm, tn_ff))])

    def dw(name, a, b, rows, cols, row_off=0, alias=None, total_rows=None, colsum=False):
        total_rows = rows if total_rows is None else total_rows
        tmw = min(rows, 512) if rows <= 1024 else D_FF // 2
        blk, rem = divmod(row_off, tmw)
        assert rem == 0

        def ep(acc, ex, outs, ids, scr):
            outs[0][...] = acc.astype(BF)
            if colsum:
                outs[1][...] = jnp.sum(ex[0][...].astype(F32), axis=0, keepdims=True)

        outs = [(_sds((total_rows, cols), BF), pl.BlockSpec((tmw, cols), lambda j, i, k: (blk + i, j)))]
        extra = []
        if colsum:
            extra = [(a, pl.BlockSpec((T, tmw), lambda j, i, k: (0, i)))]
            outs.append((_sds((1, rows), F32), pl.BlockSpec((1, tmw), lambda j, i, k: (0, i))))
        carry = plan.carry(name)
        res = carried(name, _matmul(name, [a], b, "TN", m=rows, n=cols, tm=tmw, tn=cols, epilogue=ep, extra=extra,
                                    outs=outs, alias=None if alias is None else (alias, 0), carry=carry), carry)
        return res if colsum else res[0]

    plan.grad_ready(dict(w_ffn_down=dw("ffn_down_dw", act, dx3_b, D_FF, D)))

    def ep_rms_bwd(acc, ex, outs, ids, scr):
        dx, dg = _rms_bwd(acc, ex[0][...], ex[1][...], ex[2][...])
        dx = ex[3][...] + dx
        outs[0][...] = dx
        outs[1][...] = dx.astype(BF)
        _accumulate_rows(outs[2], dg, ids[1] == 0)

    def rms_bwd_io(tm_, xin, r, g, dres):
        return dict(
            extra=[(xin, _tile(tm_, D)), (r, pl.BlockSpec((tm_, 1), lambda j, i, k: (i, 0))), (g, _row(D)),
                   (dres, _tile(tm_, D))],
            outs=[(_sds((T, D), F32), _tile(tm_, D)), (_sds((T, D), BF), _tile(tm_, D)), (_sds((1, D), F32), _row(D))])

    carry = plan.carry("ffn_in_bwd")
    dx2, dx2_b, dg_ffn = carried(
        "ffn_in_bwd",
        _matmul("ffn_in_bwd", [dgate, dup], wf_t, "NN", m=T, n=D, tm=256, tn=D, epilogue=ep_rms_bwd,
                carry=carry, **rms_bwd_io(256, x2, r2, small["g_ffn_norm"], dx3)), carry)
    plan.launch("send_down")
    gwf_t, _ = _stacked_dw("ffn_in_dw", [dgate, dup], h2, D_FF // 2)
    plan.grad_ready(dict(w_ffn_in=gwf_t))

    def ep_merge_bwd(acc, ex, outs, ids, scr):
        s0 = jax.nn.sigmoid(ex[2][...].astype(F32))
        s1 = jax.nn.sigmoid(ex[3][...].astype(F32))
        outs[0][...] = (acc * s0).astype(BF)
        outs[1][...] = (acc * s1).astype(BF)
        outs[2][...] = (acc * ex[0][...].astype(F32) * s0 * (1.0 - s0)).astype(BF)
        outs[3][...] = (acc * ex[1][...].astype(F32) * s1 * (1.0 - s1)).astype(BF)

    carry = plan.carry("out_proj_bwd_merge")
    dya, dyc, dg0, dg1 = carried(
        "out_proj_bwd_merge",
        _matmul("out_proj_bwd_merge", [dx2_b], w_out, "NT", m=T, n=D, tm=T, tn=tg, epilogue=ep_merge_bwd,
                extra=[(ya, _tile(T, tg)), (yc, _tile(T, tg)), (proj, gate_specs[0]), (proj, gate_specs[1])],
                outs=[(_sds((T, D), BF), _tile(T, tg))] * 4, carry=carry), carry)
    plan.launch("send_ffn")
    gw_out = dw("out_proj_dw", merged, dx2_b, D, D)
    d_o, = _matmul("attn_proj_bwd", [dya], wap_t, "NN", m=T, n=ATTN_WIDTH, tm=tm, tn=ATTN_WIDTH,
                   epilogue=_store(BF), outs=[(_sds((T, ATTN_WIDTH), BF), _tile(tm, ATTN_WIDTH))])
    d_c, = _matmul("conv_proj_bwd", [dyc], wcp_t, "NN", m=T, n=CONV_CHANNELS, tm=tm, tn=CONV_CHANNELS,
                   epilogue=_store(BF), outs=[(_sds((T, CONV_CHANNELS), BF), _tile(tm, CONV_CHANNELS))])
    gwap_t = dw("attn_proj_dw", dya, o, D, ATTN_WIDTH)
    gwcp_t, db_cp = dw("conv_proj_dw", dyc, c, D, CONV_CHANNELS, colsum=True)
    plan.grad_ready(dict(w_out=gw_out, w_attn_proj=gwap_t, w_conv_proj=gwcp_t))
    (dglu, dcw, dcb, dlng, dlnb), got = _conv_bwd(proj, u_conv, d_c, conv_w, small["conv_b"], small["ln_g"],
                                                  small["ln_b"], carry=plan.carry("conv_bwd"))
    plan.done("conv_bwd", got)
    plan.launch("send_mix")
    (dqkv, dsinks), got = _attn_bwd(proj, d_o, small["sinks"], carry=plan.carry("attn_bwd"))
    plan.done("attn_bwd", got)

    segs = [dqkv, dglu, dg0, dg1]
    gwi_t, db_in = _stacked_dw("proj_in_dw", segs, h, 256)
    plan.grad_ready(dict(w_in=gwi_t))
    plan.alone("swap_inp")
    plan.launch("send_inp")
    carry = plan.carry("proj_in_bwd")
    dx, _, dg_mix = carried(
        "proj_in_bwd",
        _matmul("proj_in_bwd", segs, wi_t, "NN", m=T, n=D, tm=256, tn=D, epilogue=ep_rms_bwd, carry=carry,
                **rms_bwd_io(256, x, r1, small["g_mix_norm"], plan.behind("inp", dx2))), carry)

    parts = dict(g_mix_norm=dg_mix, b_in=db_in, sinks=dsinks, conv_w=dcw, conv_b=dcb, ln_g=dlng, ln_b=dlnb,
                 b_conv_proj=db_cp, g_ffn_norm=dg_ffn, g_final=dg_final, loss=loss)
    return dx, parts


def _place():
    x, y, c = lax.axis_index("x"), lax.axis_index("y"), lax.axis_index("c")
    return x, y, c, [(1 - x, y), (x, 1 - y), (1 - x, 1 - y)]


def _gather_copies(x_refs, out_refs, rows_per, send_sems, recv_sems, local_sems):
    x, y, c, chips = _place()
    me, sibling = (x, y, c), (x, y, 1 - c)

    def rows(a, px, py, pc):
        return out_refs[a].at[pl.ds((4 * px + 2 * py + pc) * rows_per[a], rows_per[a])]

    def copy(a, k, block, to, src=None):
        return pltpu.make_async_remote_copy(
            src_ref=rows(a, *block) if src is None else src, dst_ref=rows(a, *block),
            send_sem=send_sems.at[7 * a + k], recv_sem=recv_sems.at[7 * a + k], device_id=to, device_id_type=MESH)

    def local(a):
        return pltpu.make_async_copy(x_refs[a], rows(a, *me), local_sems.at[a])

    def first(a):
        return [copy(a, 0, me, sibling, src=x_refs[a])] + [copy(a, 1 + j, me, (*chip, c), src=x_refs[a])
                                                          for j, chip in enumerate(chips)]

    def arrive(a, j):
        return copy(a, 1 + j, (*chips[j], c), me)

    def passed(a, j):
        return copy(a, 4 + j, (*chips[j], c), sibling)

    def from_sibling(a):
        return [copy(a, 0, sibling, me)] + [copy(a, 4 + j, (*chip, 1 - c), me) for j, chip in enumerate(chips)]

    return len(x_refs), local, first, arrive, passed, from_sibling


def _gather_start(*refs):
    n, local, first, _, _, _ = _gather_copies(*refs)
    for a in range(n):
        local(a).start()
        for cp in first(a):
            cp.start()


def _gather_finish(*refs):
    n, local, first, arrive, passed, from_sibling = _gather_copies(*refs)
    for a in range(n):
        for j in range(3):
            arrive(a, j).wait_recv()
            passed(a, j).start()
    for a in range(n):
        for cp in from_sibling(a):
            cp.wait_recv()
    for a in range(n):
        for cp in first(a) + [passed(a, j) for j in range(3)]:
            cp.wait_send()
        local(a).wait()


def _gather_peers():
    x, y, c, chips = _place()
    return [(x, y, 1 - c)] + [(*chip, c) for chip in chips]


def _gather_sems(n):
    return [pltpu.SemaphoreType.DMA((7 * n,)), pltpu.SemaphoreType.DMA((7 * n,)), pltpu.SemaphoreType.DMA((n,))]


def _gather_carry(shards):
    rows_per = [s.shape[0] for s in shards]
    return _Carry(shards, [_sds((N_DEV * s.shape[0],) + s.shape[1:], s.dtype) for s in shards],
                  _gather_sems(len(shards)),
                  lambda ins, outs, sems: _gather_start(ins, outs, rows_per, *sems),
                  lambda ins, outs, sems: _gather_finish(ins, outs, rows_per, *sems), _gather_peers)


def _first_gather(shards, x, g):
    n = len(shards)
    rows_per = [s.shape[0] for s in shards]
    T, D = x.shape

    def body(*refs):
        x_refs, (xin_ref, g_ref), out_refs, (h_ref, r_ref) = refs[:n], refs[n:n + 2], refs[n + 2:2 * n + 2], refs[2 * n + 2:2 * n + 4]
        send_sems, recv_sems, local_sems = refs[2 * n + 4:]
        x, y, c, chips = _place()
        me, sibling = (x, y, c), (x, y, 1 - c)
        near_x, near_y, far = (*chips[0], c), (*chips[1], c), (*chips[2], c)

        def rows(a, dev, part):
            h = rows_per[a] // 2
            lo, size = {"all": (0, 2 * h), "low": (0, h), "high": (h, h)}[part]
            return out_refs[a].at[pl.ds((4 * dev[0] + 2 * dev[1] + dev[2]) * rows_per[a] + lo, size)]

        def copy(a, k, block, part, to, src=None):
            return pltpu.make_async_remote_copy(
                src_ref=rows(a, block, part) if src is None else src, dst_ref=rows(a, block, part),
                send_sem=send_sems.at[9 * a + k], recv_sem=recv_sems.at[9 * a + k], device_id=to, device_id_type=MESH)

        other = lambda dev: (dev[0], dev[1], 1 - c)
        sent = []
        for a in range(n):
            pltpu.make_async_copy(x_refs[a], rows(a, me, "all"), local_sems.at[a]).start()
            sent += [copy(a, 0, me, "all", sibling, src=x_refs[a]), copy(a, 1, me, "all", near_x, src=x_refs[a]),
                     copy(a, 2, me, "all", near_y, src=x_refs[a])]
        for cp in sent:
            cp.start()
        for i in range(T // CHUNK):
            rws = slice(i * CHUNK, (i + 1) * CHUNK)
            xv = xin_ref[rws, :]
            r = lax.rsqrt(jnp.mean(xv * xv, axis=-1, keepdims=True) + EPS)
            h_ref[rws, :] = (xv * r * g_ref[...]).astype(BF)
            r_ref[rws, :] = r
        for a in range(n):
            copy(a, 1, near_x, "all", me).wait_recv()
            copy(a, 2, near_y, "all", me).wait_recv()
            passed = [copy(a, 3, near_y, "high", near_x), copy(a, 4, near_x, "low", near_y),
                      copy(a, 5, near_x, "all", sibling), copy(a, 6, near_y, "all", sibling)]
            for cp in passed:
                cp.start()
            sent += passed
        for a in range(n):
            copy(a, 3, far, "high", me).wait_recv()
            copy(a, 4, far, "low", me).wait_recv()
            passed = [copy(a, 7, far, "high", sibling), copy(a, 8, far, "low", sibling)]
            for cp in passed:
                cp.start()
            sent += passed
        for a in range(n):
            copy(a, 0, sibling, "all", me).wait_recv()
            copy(a, 5, other(near_x), "all", me).wait_recv()
            copy(a, 6, other(near_y), "all", me).wait_recv()
            copy(a, 7, other(far), "high", me).wait_recv()
            copy(a, 8, other(far), "low", me).wait_recv()
        for cp in sent:
            cp.wait_send()
        for a in range(n):
            pltpu.make_async_copy(x_refs[a], rows(a, me, "all"), local_sems.at[a]).wait()

    vm = pl.BlockSpec(memory_space=pltpu.VMEM)
    return pl.pallas_call(
        body, name="weights_first_gather", in_specs=[*[ANY] * n, vm, vm], out_specs=[*[ANY] * n, vm, vm],
        out_shape=[*[_sds((N_DEV * s.shape[0],) + s.shape[1:], s.dtype) for s in shards], _sds((T, D), BF),
                   _sds((T, 1), F32)],
        scratch_shapes=[pltpu.SemaphoreType.DMA((9 * n,)), pltpu.SemaphoreType.DMA((9 * n,)),
                        pltpu.SemaphoreType.DMA((n,))],
        compiler_params=pltpu.CompilerParams(vmem_limit_bytes=VMEM_LIMIT_BYTES),
    )(*shards, x, g)


def _swap_carry(grads):
    n = len(grads)

    def copies(g_refs, out_refs, sems):
        send_sems, recv_sems = sems
        x, y, c, _ = _place()
        return [pltpu.make_async_remote_copy(
            src_ref=g_refs[a].at[2 * p + 1 - c], dst_ref=out_refs[a].at[p],
            send_sem=send_sems.at[4 * a + p], recv_sem=recv_sems.at[4 * a + p],
            device_id=(x, y, 1 - c), device_id_type=MESH) for a in range(n) for p in range(4)]

    def start(ins, outs, sems):
        for cp in copies(ins, outs, sems):
            cp.start()

    def finish(ins, outs, sems):
        for cp in copies(ins, outs, sems):
            cp.wait()

    def peers():
        x, y, c, _ = _place()
        return [(x, y, 1 - c)]

    return _Carry(grads, [_sds((4,) + g.shape[1:], g.dtype) for g in grads],
                  [pltpu.SemaphoreType.DMA((4 * n,)), pltpu.SemaphoreType.DMA((4 * n,))], start, finish, peers)


def _join(carries):
    carries = [c for c in carries if c is not None]
    if not carries:
        return None
    n_in = [len(c.arrays) for c in carries]
    n_out = [len(c.out_shapes) for c in carries]
    n_sem = [len(c.sems) for c in carries]

    def parts(refs, counts):
        cuts = [sum(counts[:q]) for q in range(len(counts) + 1)]
        return [refs[cuts[q]:cuts[q + 1]] for q in range(len(counts))]

    def start(ins, outs, sems):
        for c, i, o, s in zip(carries, parts(ins, n_in), parts(outs, n_out), parts(sems, n_sem)):
            c.start(i, o, s)

    def finish(ins, outs, sems):
        for c, i, o, s in zip(carries, parts(ins, n_in), parts(outs, n_out), parts(sems, n_sem)):
            c.finish(i, o, s)

    return _Carry([a for c in carries for a in c.arrays], [o for c in carries for o in c.out_shapes],
                  [s for c in carries for s in c.sems], start, finish)


def _run_carry(name, carry):
    n_in, n_out = len(carry.arrays), len(carry.out_shapes)

    def body(*refs):
        carry.start(refs[:n_in], refs[n_in:n_in + n_out], refs[n_in + n_out:])
        carry.finish(refs[:n_in], refs[n_in:n_in + n_out], refs[n_in + n_out:])

    return pl.pallas_call(body, name=name, in_specs=[ANY] * n_in, out_specs=[ANY] * n_out,
                          out_shape=carry.out_shapes, scratch_shapes=carry.sems)(*carry.arrays)


def _run_carry_async(name, carry, collective_id):
    ins = [jax.new_ref(a, memory_space=pltpu.MemorySpace.HBM) for a in carry.arrays]
    outs = [jax.empty_ref(o, memory_space=pltpu.MemorySpace.HBM) for o in carry.out_shapes]

    @pl.kernel(mesh=plsc.ScalarSubcoreMesh(axis_name="sequencer", num_cores=1), name=name,
               scratch_types=tuple(carry.sems), compiler_params=pltpu.CompilerParams(collective_id=collective_id))
    def launch(*sems):
        barrier = pltpu.get_barrier_semaphore()
        peers = carry.peers()
        for peer in peers:
            pl.semaphore_signal(barrier, inc=1, device_id=peer, device_id_type=MESH)
        pl.semaphore_wait(barrier, len(peers))
        carry.start(ins, outs, sems)
        carry.finish(ins, outs, sems)

    launch()
    return [o[...] for o in outs]


def _chip_sums(name, gs, gots, c):
    n = len(gs)

    def body(c_ref, *refs):
        for g_ref, got_ref, o_ref in zip(refs[:n], refs[n:2 * n], refs[2 * n:]):
            o_ref[...] = (g_ref[...].astype(F32) + got_ref[...].astype(F32)).astype(BF)

    mine = [pl.BlockSpec((1,) + g.shape[1:], lambda p, c_ref: (2 * p + c_ref[0], 0, 0)) for g in gs]
    slot = [pl.BlockSpec((1,) + g.shape[1:], lambda p, c_ref: (p, 0, 0)) for g in gs]
    return pl.pallas_call(
        body, name=name,
        grid_spec=pltpu.PrefetchScalarGridSpec(num_scalar_prefetch=1, grid=(4,), in_specs=[*mine, *slot],
                                               out_specs=slot),
        out_shape=[_sds((4,) + g.shape[1:], BF) for g in gs],
        compiler_params=_params(("arbitrary",)),
    )(c, *gs, *gots)


def _send_carry(sums, ks):
    n, nk = len(sums), len(ks)

    def copies(s_refs, out_refs, sems):
        send_sems, recv_sems = sems
        x, y, c, chips = _place()
        return [pltpu.make_async_remote_copy(
            src_ref=s_refs[a].at[2 * chips[k][0] + chips[k][1]], dst_ref=out_refs[a].at[q],
            send_sem=send_sems.at[nk * a + q], recv_sem=recv_sems.at[nk * a + q],
            device_id=(*chips[k], c), device_id_type=MESH) for a in range(n) for q, k in enumerate(ks)]

    def start(ins, outs, sems):
        for cp in copies(ins, outs, sems):
            cp.start()

    def finish(ins, outs, sems):
        for cp in copies(ins, outs, sems):
            cp.wait()

    def peers():
        x, y, c, chips = _place()
        return [(*chips[k], c) for k in ks]

    return _Carry(sums, [_sds((nk,) + s.shape[1:], s.dtype) for s in sums],
                  [pltpu.SemaphoreType.DMA((nk * n,)), pltpu.SemaphoreType.DMA((nk * n,))], start, finish, peers)


def _adam_math(w, g, m, v):
    m = ADAM_B1 * m + (1.0 - ADAM_B1) * g
    v = ADAM_B2 * v + (1.0 - ADAM_B2) * (g * g)
    m_hat = m / (1.0 - ADAM_B1 ** ADAM_STEP)
    v_hat = v / (1.0 - ADAM_B2 ** ADAM_STEP)
    delta = -ADAM_LR * (m_hat / (jnp.sqrt(v_hat) + ADAM_EPS) + ADAM_WD * w)
    return delta, m, v


def _adamw(name, w, g, m, v):
    rows, cols = w.shape
    tr = 256 if rows % 256 == 0 else rows

    def body(w_ref, g_ref, m_ref, v_ref, d_ref, nm_ref, nv_ref):
        d_ref[...], nm_ref[...], nv_ref[...] = _adam_math(w_ref[...], g_ref[...], m_ref[...], v_ref[...])

    t = pl.BlockSpec((tr, cols), lambda i: (i, 0))
    return pl.pallas_call(
        body, name=name, grid=(rows // tr,), in_specs=[t] * 4, out_specs=[t] * 3,
        out_shape=[_sds((rows, cols), F32)] * 3, compiler_params=_params(("arbitrary",)),
    )(w, g, m, v)


def _grad_adamw(name, g, got, got3, ids, w, m, v):
    _, rows, cols = g.shape
    n3 = len(got3)
    tr = rows // 2 if rows >= 256 else rows

    def body(ids_ref, g_ref, got_ref, *rest):
        w_ref, m_ref, v_ref, o_ref, d_ref, nm_ref, nv_ref = rest[n3:]
        tot = g_ref[0].astype(F32) + got_ref[0].astype(F32)
        for r_ref in rest[:n3]:
            for q in range(r_ref.shape[0]):
                tot = tot + r_ref[q].astype(F32)
        o_ref[...] = tot
        d_ref[...], nm_ref[...], nv_ref[...] = _adam_math(w_ref[...], tot, m_ref[...], v_ref[...])

    tile = pl.BlockSpec((tr, cols), lambda i, ids_ref: (i, 0))
    return pl.pallas_call(
        body, name=name,
        grid_spec=pltpu.PrefetchScalarGridSpec(
            num_scalar_prefetch=1, grid=(rows // tr,),
            in_specs=[pl.BlockSpec((1, tr, cols), lambda i, ids_ref: (ids_ref[0], i, 0)),
                      pl.BlockSpec((1, tr, cols), lambda i, ids_ref: (ids_ref[1], i, 0)),
                      *[pl.BlockSpec((r.shape[0], tr, cols), lambda i, ids_ref: (0, i, 0)) for r in got3],
                      tile, tile, tile],
            out_specs=[tile] * 4),
        out_shape=[_sds((rows, cols), F32)] * 4,
        compiler_params=_params(("arbitrary",)),
    )(ids, g, got, *got3, w, m, v)


SMALL_NAMES = ["g_mix_norm", "b_in", "sinks", "conv_b", "ln_g", "ln_b", "b_conv_proj", "g_ffn_norm", "g_final"]
_PACK_ROWS = 32


def _small_pack(parts):
    C = CONV_CHANNELS
    part_list = [parts["g_mix_norm"], parts["b_in"], parts["sinks"], parts["conv_b"], parts["ln_g"], parts["ln_b"],
                 parts["b_conv_proj"], parts["g_ffn_norm"], parts["g_final"], parts["loss"], parts["conv_w"]]

    def body(p_mix, p_b, p_sink, p_cb, p_lg, p_lb, p_bcp, p_ffn, p_fin, p_loss, p_cw, pack):
        pack[...] = jnp.zeros_like(pack)
        pack[0:1, :] = p_mix[...]
        pack[1:2, 0:GLU_OFF] = p_b[:, 0:GLU_OFF]
        pack[2:3, :] = p_b[:, GLU_OFF:GATE_OFF]
        pack[3:4, :] = p_b[:, GATE_OFF:GATE_OFF + D_MODEL]
        pack[4:5, :] = p_b[:, GATE_OFF + D_MODEL:]
        pack[5:6, 0:128] = p_sink[...]
        pack[6:7, 0:C] = p_cb[...]
        pack[6:7, C:2 * C] = p_lg[...]
        pack[7:8, 0:C] = p_lb[...]
        pack[8:9, :] = p_bcp[...]
        pack[9:10, :] = p_ffn[...]
        pack[10:11, :] = p_fin[...]
        pack[11:12, 0:128] = jnp.broadcast_to(p_loss[...], (1, 128))
        pack[12:28, 0:C] = p_cw[0:16, :]
        pack[12:28, C:2 * C] = p_cw[16:32, :]

    vm = pl.BlockSpec(memory_space=pltpu.VMEM)
    return pl.pallas_call(body, name="small_pack", in_specs=[vm] * len(part_list), out_specs=vm,
                          out_shape=_sds((_PACK_ROWS, D_MODEL), F32))(*part_list)


def _small_adamw(gathered, small_w, small_m, small_v):
    C = CONV_CHANNELS
    names = SMALL_NAMES
    widths = [small_w[k].shape[1] for k in names]
    n_small = len(names)

    def body(*refs):
        tot_ref = refs[0]
        w_refs = refs[1:1 + n_small]
        m_refs = refs[1 + n_small:1 + 2 * n_small]
        v_refs = refs[1 + 2 * n_small:1 + 3 * n_small]
        o = 1 + 3 * n_small
        loss_ref, cw_ref = refs[o], refs[o + 1]
        out_refs = refs[o + 2:o + 2 + 4 * n_small]
        tot = tot_ref[0:_PACK_ROWS, :]
        for d in range(1, N_DEV):
            tot = tot + tot_ref[d * _PACK_ROWS:(d + 1) * _PACK_ROWS, :]
        loss_ref[...] = tot[11:12, 0:1]
        cw_ref[0:16, :] = tot[12:28, 0:C]
        cw_ref[16:32, :] = tot[12:28, C:2 * C]
        grads = dict(
            g_mix_norm=tot[0:1, :],
            b_in=jnp.concatenate([tot[1:2, 0:GLU_OFF], tot[2:3, :], tot[3:4, :], tot[4:5, :]], axis=1),
            sinks=tot[5:6, 0:N_Q_HEADS], conv_b=tot[6:7, 0:C], ln_g=tot[6:7, C:2 * C], ln_b=tot[7:8, 0:C],
            b_conv_proj=tot[8:9, :], g_ffn_norm=tot[9:10, :], g_final=tot[10:11, :])
        for s, k in enumerate(names):
            g = grads[k]
            d, nm, nv = _adam_math(w_refs[s][...], g, m_refs[s][...], v_refs[s][...])
            out_refs[4 * s][...] = g
            out_refs[4 * s + 1][...] = d
            out_refs[4 * s + 2][...] = nm
            out_refs[4 * s + 3][...] = nv

    vm = pl.BlockSpec(memory_space=pltpu.VMEM)
    args = [gathered, *[small_w[k] for k in names], *[small_m[k] for k in names], *[small_v[k] for k in names]]
    out_shape = [_sds((1, 1), F32), _sds((CONV_PAD, C), F32)]
    for wd in widths:
        out_shape += [_sds((1, wd), F32)] * 4
    res = pl.pallas_call(
        body, name="small_adamw",
        in_specs=[vm] * len(args), out_specs=[vm] * len(out_shape), out_shape=out_shape,
        compiler_params=pltpu.CompilerParams(vmem_limit_bytes=VMEM_LIMIT_BYTES),
    )(*args)
    return res[0], res[1], {k: res[2 + 4 * s:6 + 4 * s] for s, k in enumerate(names)}


BIG = dict(w_in=True, w_attn_proj=True, w_conv_proj=True, w_out=False, w_ffn_in=True, w_ffn_down=False)
WEIGHT_NAMES = ["g_mix_norm", "w_in", "b_in", "sinks", "conv_w", "conv_b", "ln_g", "ln_b", "w_attn_proj",
                "w_conv_proj", "b_conv_proj", "w_out", "g_ffn_norm", "w_ffn_in", "w_ffn_down", "g_final"]


class _Plan:
    GROUPS = dict(down=["w_ffn_down"], ffn=["w_ffn_in"], mix=["w_out", "w_attn_proj", "w_conv_proj"], inp=["w_in"])
    ALL = (0, 1, 2)
    RIDES = dict(
        gather_mix=[("gather", ["w_attn_proj", "w_conv_proj", "w_out"])], gather_ffn=[("gather", ["w_ffn_in"])],
        gather_down=[("gather", ["w_ffn_down"])],
        ffn_in_bwd=[("swap", "down")], send_down=[("send", "down", ALL)],
        out_proj_bwd_merge=[("swap", "ffn")], send_ffn=[("send", "ffn", ALL)],
        conv_bwd=[("swap", "mix")], send_mix=[("send", "mix", ALL)],
        swap_inp=[("swap", "inp")], send_inp=[("send", "inp", ALL)])
    ASYNC = dict(gather_mix=1, gather_ffn=2, gather_down=3, send_down=4, send_ffn=5, send_mix=6, send_inp=7)

    def __init__(self, shards, c1):
        self.shards, self.c1 = shards, c1
        self.full, self.slots, self.got, self.sums, self.got3 = {}, {}, {}, {}, {}

    def weight(self, name):
        return self.full[name]

    def grad_ready(self, grads):
        for k, g in grads.items():
            self.slots[k] = g.reshape(N_DEV, g.shape[0] // N_DEV, g.shape[1])

    def _one(self, kind, what, ks=None):
        if kind == "gather":
            return _gather_carry([self.shards[k] for k in what])
        names = self.GROUPS[what]
        if kind == "swap":
            return _swap_carry([self.slots[k] for k in names])
        return _send_carry([self.sums[k] for k in names], ks)

    def carry(self, call):
        return _join([self._one(*ride) for ride in self.RIDES.get(call, [])])

    def done(self, call, outs):
        outs = list(outs)
        for kind, what, *_ in self.RIDES.get(call, []):
            names = what if kind == "gather" else self.GROUPS[what]
            mine, outs = outs[:len(names)], outs[len(names):]
            if kind == "gather":
                self.full.update(zip(names, mine))
            elif kind == "send":
                for k, r in zip(names, mine):
                    self.got3.setdefault(k, []).append(r)
            else:
                self.got.update(zip(names, mine))
                self.sums.update(zip(names, _chip_sums(f"chip_sums_{what}", [self.slots[k] for k in names], mine, self.c1)))

    def alone(self, call):
        self.done(call, _run_carry(call, self.carry(call)))

    def behind(self, group, x):
        return lax.optimization_barrier((x, tuple(self.sums[k] for k in self.GROUPS[group])))[0]

    def launch(self, call, after=None):
        carry = self._one(*self.RIDES[call][0])
        if after is not None:
            carry.arrays = list(lax.optimization_barrier((tuple(carry.arrays), after))[0])
        self.done(call, _run_carry_async(call, carry, self.ASYNC[call]))


def kernel(x, g_mix_norm, w_in, b_in, sinks, conv_w, conv_b, ln_g, ln_b, w_attn_proj, w_conv_proj, b_conv_proj, w_out, g_ffn_norm, w_ffn_in, w_ffn_down, g_final, loss_target, m_g_mix_norm, m_w_in, m_b_in, m_sinks, m_conv_w, m_conv_b, m_ln_g, m_ln_b, m_w_attn_proj, m_w_conv_proj, m_b_conv_proj, m_w_out, m_g_ffn_norm, m_w_ffn_in, m_w_ffn_down, m_g_final, v_g_mix_norm, v_w_in, v_b_in, v_sinks, v_conv_w, v_conv_b, v_ln_g, v_ln_b, v_w_attn_proj, v_w_conv_proj, v_b_conv_proj, v_w_out, v_g_ffn_norm, v_w_ffn_in, v_w_ffn_down, v_g_final):
    w = dict(g_mix_norm=g_mix_norm, w_in=w_in, b_in=b_in, sinks=sinks, conv_w=conv_w, conv_b=conv_b, ln_g=ln_g,
             ln_b=ln_b, w_attn_proj=w_attn_proj, w_conv_proj=w_conv_proj, b_conv_proj=b_conv_proj, w_out=w_out,
             g_ffn_norm=g_ffn_norm, w_ffn_in=w_ffn_in, w_ffn_down=w_ffn_down, g_final=g_final)
    m = dict(g_mix_norm=m_g_mix_norm, w_in=m_w_in, b_in=m_b_in, sinks=m_sinks, conv_w=m_conv_w, conv_b=m_conv_b,
             ln_g=m_ln_g, ln_b=m_ln_b, w_attn_proj=m_w_attn_proj, w_conv_proj=m_w_conv_proj,
             b_conv_proj=m_b_conv_proj, w_out=m_w_out, g_ffn_norm=m_g_ffn_norm, w_ffn_in=m_w_ffn_in,
             w_ffn_down=m_w_ffn_down, g_final=m_g_final)
    v = dict(g_mix_norm=v_g_mix_norm, w_in=v_w_in, b_in=v_b_in, sinks=v_sinks, conv_w=v_conv_w, conv_b=v_conv_b,
             ln_g=v_ln_g, ln_b=v_ln_b, w_attn_proj=v_w_attn_proj, w_conv_proj=v_w_conv_proj,
             b_conv_proj=v_b_conv_proj, w_out=v_w_out, g_ffn_norm=v_g_ffn_norm, w_ffn_in=v_w_ffn_in,
             w_ffn_down=v_w_ffn_down, g_final=v_g_final)
    ax, ay, ac = lax.axis_index("x"), lax.axis_index("y"), lax.axis_index("c")
    me = 4 * ax + 2 * ay + ac
    chip = 2 * ax + ay

    shards = {k: (w[k][0].T if tr else w[k][0]).astype(BF) for k, tr in BIG.items()}
    cw_shard = jnp.pad(conv_w[0].T, ((0, 0), (0, 1))).reshape(16, 128)
    wi_t, cw_full, h, r1 = _first_gather([shards["w_in"], cw_shard], x[0], g_mix_norm)
    conv_full = cw_full.reshape(CONV_CHANNELS, CONV_PAD).T

    as_row = lambda a: a.reshape(1, -1)
    small_w = {k: as_row(w[k]) for k in SMALL_NAMES}
    small_m = {k: as_row(m[k]) for k in SMALL_NAMES}
    small_v = {k: as_row(v[k]) for k in SMALL_NAMES}
    plan = _Plan(shards, ac.reshape(1).astype(jnp.int32))
    plan.launch("gather_mix", after=wi_t)
    dx, parts = _local_step(x[0], h, r1, loss_target[0], small_w, wi_t, conv_full, plan)

    ids = jnp.stack([me, chip]).astype(jnp.int32)
    grads, delta, new_m, new_v, after = {}, {}, {}, {}, dx
    packed = _small_pack(parts)
    for k in sorted(BIG, key=lambda k: k == "w_in"):
        if k == "w_in":
            packed = lax.optimization_barrier((packed, after))[0]
            small_gathered, = _run_carry_async("small_gather", _gather_carry([packed]), 8)
        flip = (lambda a: a.T) if BIG[k] else (lambda a: a)
        wk = lax.optimization_barrier((w[k][0], after))[0]
        outs = _grad_adamw(f"grad_adamw_{k}", plan.slots[k], plan.got[k], plan.got3[k], ids,
                           flip(wk), flip(m[k][0]), flip(v[k][0]))
        after = outs[0]
        grads[k], delta[k], new_m[k], new_v[k] = (flip(a)[None] for a in outs)

    loss, cw_grad, small_out = _small_adamw(small_gathered, small_w, small_m, small_v)
    for k in SMALL_NAMES:
        g, d, nm, nv = (a.reshape(w[k].shape) for a in small_out[k])
        grads[k], delta[k], new_m[k], new_v[k] = g, d, nm, nv
    cw_mine = lax.dynamic_slice(cw_grad, (0, me * 64), (CONV_WIDTH, 64))
    d, nm, nv = _adamw("adamw_conv_w", conv_w[0], cw_mine, m_conv_w[0], v_conv_w[0])
    grads["conv_w"], delta["conv_w"], new_m["conv_w"], new_v["conv_w"] = cw_mine[None], d[None], nm[None], nv[None]

    return (loss.reshape(()), dx[None], *[grads[k] for k in WEIGHT_NAMES], *[delta[k] for k in WEIGHT_NAMES],
            *[new_m[k] for k in WEIGHT_NAMES], *[new_v[k] for k in WEIGHT_NAMES])
```
